```python
import math
import jax, jax.numpy as jnp
from jax import lax
import numpy as np

D_MODEL = 1024
BATCH = 8
SEQ = 2048
DEPTH = 2

N_A = DEPTH // 2
N_B = DEPTH - N_A
HG_DK = 128
HG_HEADS = D_MODEL // HG_DK
HG_DV = D_MODEL // HG_HEADS
HG_WIDTH = HG_HEADS * HG_DK
HG_VWIDTH = HG_HEADS * HG_DV
HG_CHUNK = 64
ATT_HD = 64
ATT_QH = D_MODEL // ATT_HD
ATT_KVH = 4
ATT_G = ATT_QH // ATT_KVH
WINDOW = 128
FFN_HIDDEN = -(-8 * D_MODEL // (3 * 256)) * 256
PLE_DIM = 256
DEEPNORM_ALPHA = (2.0 * DEPTH) ** 0.25
DEEPNORM_BETA = (8.0 * DEPTH) ** -0.25
LN_EPS = 1e-5
RMS_EPS = 1e-6

kernel_name = 'yoco_hgrn2_swa_sink_alibi_deepnorm'


def layer_norm(x, g, b):
    xf = x.astype(jnp.float32)
    mu = jnp.mean(xf, axis=-1, keepdims=True)
    var = jnp.mean(jnp.square(xf - mu), axis=-1, keepdims=True)
    return ((xf - mu) * lax.rsqrt(var + LN_EPS) * g.astype(jnp.float32) + b.astype(jnp.float32)).astype(x.dtype)


def rms_norm(x, g):
    xf = x.astype(jnp.float32)
    return xf * lax.rsqrt(jnp.mean(jnp.square(xf), axis=-1, keepdims=True) + RMS_EPS) * g.astype(jnp.float32)


def hgrn2_mixer(x, w_in, lb, norm_gain, w_out):
    B, S, _ = x.shape
    f32 = jnp.float32
    proj = x @ w_in
    q = proj[..., :HG_WIDTH]
    f = proj[..., HG_WIDTH:2 * HG_WIDTH].astype(f32)
    v = proj[..., 2 * HG_WIDTH:2 * HG_WIDTH + HG_VWIDTH].astype(f32)
    g = proj[..., 2 * HG_WIDTH + HG_VWIDTH:]
    lb = lb.astype(f32)
    forget = lb + (1.0 - lb) * jax.nn.sigmoid(f)
    log_f = jnp.log(forget)
    k = (1.0 - lb) * jax.nn.sigmoid(-f)
    q = jax.nn.silu(q.astype(f32)) * (HG_DK ** -0.5)
    nc = S // HG_CHUNK

    def to_chunks(t, d):
        return t.reshape(B, nc, HG_CHUNK, HG_HEADS, d).transpose(1, 0, 3, 2, 4)

    causal = jnp.tril(jnp.ones((HG_CHUNK, HG_CHUNK), dtype=bool))[:, :, None]

    def step(state, inp):
        qc, kc, gc, vc = inp
        b = jnp.cumsum(gc, axis=2)
        diff = b[:, :, :, None, :] - b[:, :, None, :, :]
        decay = jnp.exp(jnp.where(causal, diff, -jnp.inf))
        scores = jnp.einsum('bhtc,bhsc,bhtsc->bhts', qc, kc, decay)
        o = jnp.einsum('bhts,bhsv->bhtv', scores, vc) + jnp.einsum('bhtc,bhcv->bhtv', qc * jnp.exp(b), state)
        b_last = b[:, :, -1:, :]
        state = jnp.exp(b_last[:, :, 0, :, None]) * state + jnp.einsum('bhsc,bhsv->bhcv', kc * jnp.exp(b_last - b), vc)
        return state, o

    s0 = jnp.zeros((B, HG_HEADS, HG_DK, HG_DV), f32)
    _, o = lax.scan(step, s0, (to_chunks(q, HG_DK), to_chunks(k, HG_DK), to_chunks(log_f, HG_DK), to_chunks(v, HG_DV)))
    o = o.transpose(1, 0, 3, 2, 4).reshape(B, S, HG_HEADS, HG_DV)
    o = rms_norm(o, norm_gain) * jax.nn.silu(g.astype(f32)).reshape(B, S, HG_HEADS, HG_DV)
    return o.reshape(B, S, HG_VWIDTH).astype(x.dtype) @ w_out


def shared_kv(h, kv_w, kv_b):
    B, S, _ = h.shape
    kv = h @ kv_w + kv_b
    kdim = ATT_KVH * ATT_HD
    return kv[..., :kdim].reshape(B, S, ATT_KVH, ATT_HD), kv[..., kdim:].reshape(B, S, ATT_KVH, ATT_HD)


def band_blocks(t):
    B, S = t.shape[:2]
    tp = jnp.pad(t, ((0, 0), (WINDOW, 0), (0, 0), (0, 0)))
    tr = tp.reshape(B, S // WINDOW + 1, WINDOW, ATT_KVH, ATT_HD)
    return jnp.concatenate([tr[:, :-1], tr[:, 1:]], axis=2)


def alibi_slopes(n_heads):
    return jnp.exp2(-8.0 * jnp.arange(1, n_heads + 1, dtype=jnp.float32) / n_heads)


def swa_sink_mixer(x, k_blocks, v_blocks, w_q, b_q, sinks, w_out, b_out):
    B, S, _ = x.shape
    nb = S // WINDOW
    f32 = jnp.float32
    q = (x @ w_q + b_q).reshape(B, nb, WINDOW, ATT_KVH, ATT_G, ATT_HD)
    s = jnp.einsum('bnqkgd,bnskd->bnkgqs', q, k_blocks).astype(f32) * (ATT_HD ** -0.5)
    qi = jnp.arange(WINDOW)[:, None]
    si = jnp.arange(2 * WINDOW)[None, :]
    dist = qi - si + WINDOW
    blk = jnp.arange(nb)[:, None, None]
    valid = (dist >= 0) & (dist < WINDOW) & (blk * WINDOW - WINDOW + si >= 0)
    slopes = alibi_slopes(ATT_QH).reshape(ATT_KVH, ATT_G)
    s = s - slopes[:, :, None, None] * dist.astype(f32)
    s = jnp.where(valid[None, :, None, None], s, -jnp.inf)
    sink = jnp.broadcast_to(sinks.astype(f32).reshape(ATT_KVH, ATT_G)[None, None, :, :, None, None], s.shape[:-1] + (1,))
    probs = jax.nn.softmax(jnp.concatenate([s, sink], axis=-1), axis=-1)[..., :-1]
    o = jnp.einsum('bnkgqs,bnskd->bnqkgd', probs.astype(v_blocks.dtype), v_blocks).reshape(B, S, ATT_QH * ATT_HD)
    return o @ w_out + b_out


def swiglu(x, w_gate_up, w_down):
    gu = x @ w_gate_up
    return (jax.nn.silu(gu[..., :FFN_HIDDEN]) * gu[..., FFN_HIDDEN:]) @ w_down


def per_layer_embedding(x, p_i, w_up, w_gate, b_gate):
    return jax.nn.sigmoid(x @ w_gate + b_gate) * (p_i @ w_up)


def _fwd_setup_inputs(seed: int = 0) -> dict:
    key = jax.random.key(seed)
    ks = jax.random.split(key, 24)
    D = D_MODEL
    nrm = lambda k, shape, scale: jax.random.normal(k, shape, jnp.float32) * scale
    kvd = ATT_KVH * ATT_HD
    qd = ATT_QH * ATT_HD
    return {
        'x': nrm(ks[0], (BATCH, SEQ, D), 1.0),
        'p': nrm(ks[1], (DEPTH, BATCH, SEQ, PLE_DIM), 1.0),
        'a_w_in': jnp.concatenate([
            nrm(ks[2], (N_A, D, 2 * HG_WIDTH), D ** -0.5),
            nrm(ks[3], (N_A, D, HG_VWIDTH), D ** -0.5 * DEEPNORM_BETA),
            nrm(ks[4], (N_A, D, HG_VWIDTH), D ** -0.5)], axis=-1),
        'a_lower_bound': nrm(ks[5], (N_A + 1, HG_WIDTH), 0.5),
        'a_norm_gain': 1.0 + nrm(ks[6], (N_A, HG_DV), 0.05),
        'a_w_out': nrm(ks[7], (N_A, HG_VWIDTH, D), HG_VWIDTH ** -0.5 * DEEPNORM_BETA),
        'kv_w': jnp.concatenate([
            nrm(ks[8], (D, kvd), D ** -0.5),
            nrm(ks[9], (D, kvd), D ** -0.5 * DEEPNORM_BETA)], axis=-1),
        'kv_b': nrm(ks[10], (2 * kvd,), 0.02),
        'b_w_q': nrm(ks[11], (N_B, D, qd), D ** -0.5),
        'b_b_q': nrm(ks[12], (N_B, qd), 0.02),
        'b_sinks': nrm(ks[13], (N_B, ATT_QH), 1.0),
        'b_w_out': nrm(ks[14], (N_B, qd, D), qd ** -0.5 * DEEPNORM_BETA),
        'b_b_out': nrm(ks[15], (N_B, D), 0.02),
        'ffn_w_gate_up': nrm(ks[16], (DEPTH, D, 2 * FFN_HIDDEN), D ** -0.5),
        'ffn_w_down': nrm(ks[17], (DEPTH, FFN_HIDDEN, D), FFN_HIDDEN ** -0.5 * DEEPNORM_BETA),
        'ple_w_up': nrm(ks[18], (DEPTH, PLE_DIM, D), PLE_DIM ** -0.5 * DEEPNORM_BETA),
        'ple_w_gate': nrm(ks[19], (DEPTH, D, D), D ** -0.5),
        'ple_b_gate': nrm(ks[20], (DEPTH, D), 0.02),
        'ln_gain': 1.0 + nrm(ks[21], (DEPTH, 3, D), 0.05),
        'ln_bias': nrm(ks[22], (DEPTH, 3, D), 0.02),
    }


def _fwd_reference(x, p, a_w_in, a_lower_bound, a_norm_gain, a_w_out, kv_w, kv_b, b_w_q, b_b_q, b_sinks, b_w_out, b_b_out,
              ffn_w_gate_up, ffn_w_down, ple_w_up, ple_w_gate, ple_b_gate, ln_gain, ln_bias):
    lower_bounds = jnp.cumsum(jax.nn.softmax(a_lower_bound.astype(jnp.float32), axis=0), axis=0)
    k_blocks = None
    v_blocks = None
    for i in range(DEPTH):
        if i < N_A:
            h = hgrn2_mixer(x, a_w_in[i], lower_bounds[i], a_norm_gain[i], a_w_out[i])
        else:
            if i == N_A:
                k_sh, v_sh = shared_kv(x, kv_w, kv_b)
                k_blocks = band_blocks(k_sh)
                v_blocks = band_blocks(v_sh)
            j = i - N_A
            h = swa_sink_mixer(x, k_blocks, v_blocks, b_w_q[j], b_b_q[j], b_sinks[j], b_w_out[j], b_b_out[j])
        x = layer_norm(DEEPNORM_ALPHA * x + h, ln_gain[i, 0], ln_bias[i, 0])
        x = layer_norm(DEEPNORM_ALPHA * x + swiglu(x, ffn_w_gate_up[i], ffn_w_down[i]), ln_gain[i, 1], ln_bias[i, 1])
        x = layer_norm(DEEPNORM_ALPHA * x + per_layer_embedding(x, p[i], ple_w_up[i], ple_w_gate[i], ple_b_gate[i]),
                       ln_gain[i, 2], ln_bias[i, 2])
    return x


import jax as _jax
import jax.numpy as _jnp

TWIN_FORMAT = 'train_step'
FWD_PARAMS = ['x', 'p', 'a_w_in', 'a_lower_bound', 'a_norm_gain', 'a_w_out', 'kv_w', 'kv_b', 'b_w_q', 'b_b_q', 'b_sinks', 'b_w_out', 'b_b_out', 'ffn_w_gate_up', 'ffn_w_down', 'ple_w_up', 'ple_w_gate', 'ple_b_gate', 'ln_gain', 'ln_bias']
TWIN_WEIGHTS = ['a_w_in', 'a_lower_bound', 'a_norm_gain', 'a_w_out', 'kv_w', 'kv_b', 'b_w_q', 'b_b_q', 'b_sinks', 'b_w_out', 'b_b_out', 'ffn_w_gate_up', 'ffn_w_down', 'ple_w_up', 'ple_w_gate', 'ple_b_gate', 'ln_gain', 'ln_bias']
TWIN_DIFF_INPUT = 'x'
TWIN_INPUTS = ['x', 'p', 'a_w_in', 'a_lower_bound', 'a_norm_gain', 'a_w_out', 'kv_w', 'kv_b', 'b_w_q', 'b_b_q', 'b_sinks', 'b_w_out', 'b_b_out', 'ffn_w_gate_up', 'ffn_w_down', 'ple_w_up', 'ple_w_gate', 'ple_b_gate', 'ln_gain', 'ln_bias', 'loss_target', 'm_a_w_in', 'm_a_lower_bound', 'm_a_norm_gain', 'm_a_w_out', 'm_kv_w', 'm_kv_b', 'm_b_w_q', 'm_b_b_q', 'm_b_sinks', 'm_b_w_out', 'm_b_b_out', 'm_ffn_w_gate_up', 'm_ffn_w_down', 'm_ple_w_up', 'm_ple_w_gate', 'm_ple_b_gate', 'm_ln_gain', 'm_ln_bias', 'v_a_w_in', 'v_a_lower_bound', 'v_a_norm_gain', 'v_a_w_out', 'v_kv_w', 'v_kv_b', 'v_b_w_q', 'v_b_b_q', 'v_b_sinks', 'v_b_w_out', 'v_b_b_out', 'v_ffn_w_gate_up', 'v_ffn_w_down', 'v_ple_w_up', 'v_ple_w_gate', 'v_ple_b_gate', 'v_ln_gain', 'v_ln_bias']
TWIN_OUTPUTS = ['loss', 'grad_x', 'grad_a_w_in', 'grad_a_lower_bound', 'grad_a_norm_gain', 'grad_a_w_out', 'grad_kv_w', 'grad_kv_b', 'grad_b_w_q', 'grad_b_b_q', 'grad_b_sinks', 'grad_b_w_out', 'grad_b_b_out', 'grad_ffn_w_gate_up', 'grad_ffn_w_down', 'grad_ple_w_up', 'grad_ple_w_gate', 'grad_ple_b_gate', 'grad_ln_gain', 'grad_ln_bias', 'delta_a_w_in', 'delta_a_lower_bound', 'delta_a_norm_gain', 'delta_a_w_out', 'delta_kv_w', 'delta_kv_b', 'delta_b_w_q', 'delta_b_b_q', 'delta_b_sinks', 'delta_b_w_out', 'delta_b_b_out', 'delta_ffn_w_gate_up', 'delta_ffn_w_down', 'delta_ple_w_up', 'delta_ple_w_gate', 'delta_ple_b_gate', 'delta_ln_gain', 'delta_ln_bias', 'new_m_a_w_in', 'new_m_a_lower_bound', 'new_m_a_norm_gain', 'new_m_a_w_out', 'new_m_kv_w', 'new_m_kv_b', 'new_m_b_w_q', 'new_m_b_b_q', 'new_m_b_sinks', 'new_m_b_w_out', 'new_m_b_b_out', 'new_m_ffn_w_gate_up', 'new_m_ffn_w_down', 'new_m_ple_w_up', 'new_m_ple_w_gate', 'new_m_ple_b_gate', 'new_m_ln_gain', 'new_m_ln_bias', 'new_v_a_w_in', 'new_v_a_lower_bound', 'new_v_a_norm_gain', 'new_v_a_w_out', 'new_v_kv_w', 'new_v_kv_b', 'new_v_b_w_q', 'new_v_b_b_q', 'new_v_b_sinks', 'new_v_b_w_out', 'new_v_b_b_out', 'new_v_ffn_w_gate_up', 'new_v_ffn_w_down', 'new_v_ple_w_up', 'new_v_ple_w_gate', 'new_v_ple_b_gate', 'new_v_ln_gain', 'new_v_ln_bias']
TWIN_LEAF_KINDS = {'loss': 'loss', 'grad_x': 'grad_x', 'grad_a_w_in': 'grad_w', 'grad_a_lower_bound': 'grad_w', 'grad_a_norm_gain': 'grad_w', 'grad_a_w_out': 'grad_w', 'grad_kv_w': 'grad_w', 'grad_kv_b': 'grad_w', 'grad_b_w_q': 'grad_w', 'grad_b_b_q': 'grad_w', 'grad_b_sinks': 'grad_w', 'grad_b_w_out': 'grad_w', 'grad_b_b_out': 'grad_w', 'grad_ffn_w_gate_up': 'grad_w', 'grad_ffn_w_down': 'grad_w', 'grad_ple_w_up': 'grad_w', 'grad_ple_w_gate': 'grad_w', 'grad_ple_b_gate': 'grad_w', 'grad_ln_gain': 'grad_w', 'grad_ln_bias': 'grad_w', 'delta_a_w_in': 'delta_w', 'delta_a_lower_bound': 'delta_w', 'delta_a_norm_gain': 'delta_w', 'delta_a_w_out': 'delta_w', 'delta_kv_w': 'delta_w', 'delta_kv_b': 'delta_w', 'delta_b_w_q': 'delta_w', 'delta_b_b_q': 'delta_w', 'delta_b_sinks': 'delta_w', 'delta_b_w_out': 'delta_w', 'delta_b_b_out': 'delta_w', 'delta_ffn_w_gate_up': 'delta_w', 'delta_ffn_w_down': 'delta_w', 'delta_ple_w_up': 'delta_w', 'delta_ple_w_gate': 'delta_w', 'delta_ple_b_gate': 'delta_w', 'delta_ln_gain': 'delta_w', 'delta_ln_bias': 'delta_w', 'new_m_a_w_in': 'new_m', 'new_m_a_lower_bound': 'new_m', 'new_m_a_norm_gain': 'new_m', 'new_m_a_w_out': 'new_m', 'new_m_kv_w': 'new_m', 'new_m_kv_b': 'new_m', 'new_m_b_w_q': 'new_m', 'new_m_b_b_q': 'new_m', 'new_m_b_sinks': 'new_m', 'new_m_b_w_out': 'new_m', 'new_m_b_b_out': 'new_m', 'new_m_ffn_w_gate_up': 'new_m', 'new_m_ffn_w_down': 'new_m', 'new_m_ple_w_up': 'new_m', 'new_m_ple_w_gate': 'new_m', 'new_m_ple_b_gate': 'new_m', 'new_m_ln_gain': 'new_m', 'new_m_ln_bias': 'new_m', 'new_v_a_w_in': 'new_v', 'new_v_a_lower_bound': 'new_v', 'new_v_a_norm_gain': 'new_v', 'new_v_a_w_out': 'new_v', 'new_v_kv_w': 'new_v', 'new_v_kv_b': 'new_v', 'new_v_b_w_q': 'new_v', 'new_v_b_b_q': 'new_v', 'new_v_b_sinks': 'new_v', 'new_v_b_w_out': 'new_v', 'new_v_b_b_out': 'new_v', 'new_v_ffn_w_gate_up': 'new_v', 'new_v_ffn_w_down': 'new_v', 'new_v_ple_w_up': 'new_v', 'new_v_ple_w_gate': 'new_v', 'new_v_ple_b_gate': 'new_v', 'new_v_ln_gain': 'new_v', 'new_v_ln_bias': 'new_v'}


def _forward(args):
    return _fwd_reference(*[args[k] for k in FWD_PARAMS])


def _output_shape():
    out = _jax.eval_shape(lambda: _forward(_fwd_setup_inputs(0)))
    return out.shape, out.dtype

N_MICROBATCH = 1
ADAM_LR = 0.001
ADAM_B1 = 0.9
ADAM_B2 = 0.999
ADAM_EPS = 1e-08
ADAM_WD = 0.01
ADAM_STEP = 10
PER_EXAMPLE_BATCH_AXIS = {'x': 0, 'p': 1, 'loss_target': 0}
SHARED_INPUTS = []
_WEIGHT_DTYPES = {'a_w_in': _jnp.float32, 'a_lower_bound': _jnp.float32, 'a_norm_gain': _jnp.float32, 'a_w_out': _jnp.float32, 'kv_w': _jnp.float32, 'kv_b': _jnp.float32, 'b_w_q': _jnp.float32, 'b_b_q': _jnp.float32, 'b_sinks': _jnp.float32, 'b_w_out': _jnp.float32, 'b_b_out': _jnp.float32, 'ffn_w_gate_up': _jnp.float32, 'ffn_w_down': _jnp.float32, 'ple_w_up': _jnp.float32, 'ple_w_gate': _jnp.float32, 'ple_b_gate': _jnp.float32, 'ln_gain': _jnp.float32, 'ln_bias': _jnp.float32}
MOMENT_SCALE = {'a_w_in': 3.002075e-02, 'a_lower_bound': 2.357868e-03, 'a_norm_gain': 8.294808e-02, 'a_w_out': 5.298442e-02, 'kv_w': 2.017332e-02, 'kv_b': 1.030064e-01, 'b_w_q': 5.074232e-03, 'b_b_q': 4.943315e-03, 'b_sinks': 1.244502e-02, 'b_w_out': 1.340316e-02, 'b_b_out': 1.924036e-01, 'ffn_w_gate_up': 1.581740e-02, 'ffn_w_down': 5.175542e-02, 'ple_w_up': 4.771340e-02, 'ple_w_gate': 9.284208e-03, 'ple_b_gate': 1.791043e-02, 'ln_gain': 6.825247e+00, 'ln_bias': 3.823721e-01}


def _to_microbatches(a, axis):
    t = _jnp.moveaxis(a, axis, 0)
    t = t.reshape((N_MICROBATCH, t.shape[0] // N_MICROBATCH) + t.shape[1:])
    return _jnp.moveaxis(t, 1, axis + 1)


def setup_inputs(seed: int = 0) -> dict:
    inp = _fwd_setup_inputs(seed)
    key = _jax.random.fold_in(_jax.random.key(seed), 7919)
    shape, _ = _output_shape()
    out = dict(inp)
    out["loss_target"] = _jax.random.normal(_jax.random.fold_in(key, 0), shape, _jnp.float32)
    for i, name in enumerate(TWIN_WEIGHTS):
        w = inp[name].astype(_jnp.float32)
        if MOMENT_SCALE is None:
            s = _jnp.sqrt(_jnp.mean(_jnp.square(w)) + 1e-30)
        else:
            s = MOMENT_SCALE[name]
        km, kv = _jax.random.split(_jax.random.fold_in(key, i + 1))
        out[name] = w
        out["m_" + name] = s * _jax.random.normal(km, w.shape, _jnp.float32)
        out["v_" + name] = (s * s) * _jax.random.uniform(kv, w.shape, _jnp.float32, 0.5, 1.5)
    if N_MICROBATCH > 1:
        for name, axis in PER_EXAMPLE_BATCH_AXIS.items():
            out[name] = _to_microbatches(out[name], axis)
    return {'x': out['x'], 'p': out['p'], 'a_w_in': out['a_w_in'], 'a_lower_bound': out['a_lower_bound'], 'a_norm_gain': out['a_norm_gain'], 'a_w_out': out['a_w_out'], 'kv_w': out['kv_w'], 'kv_b': out['kv_b'], 'b_w_q': out['b_w_q'], 'b_b_q': out['b_b_q'], 'b_sinks': out['b_sinks'], 'b_w_out': out['b_w_out'], 'b_b_out': out['b_b_out'], 'ffn_w_gate_up': out['ffn_w_gate_up'], 'ffn_w_down': out['ffn_w_down'], 'ple_w_up': out['ple_w_up'], 'ple_w_gate': out['ple_w_gate'], 'ple_b_gate': out['ple_b_gate'], 'ln_gain': out['ln_gain'], 'ln_bias': out['ln_bias'], 'loss_target': out['loss_target'], 'm_a_w_in': out['m_a_w_in'], 'm_a_lower_bound': out['m_a_lower_bound'], 'm_a_norm_gain': out['m_a_norm_gain'], 'm_a_w_out': out['m_a_w_out'], 'm_kv_w': out['m_kv_w'], 'm_kv_b': out['m_kv_b'], 'm_b_w_q': out['m_b_w_q'], 'm_b_b_q': out['m_b_b_q'], 'm_b_sinks': out['m_b_sinks'], 'm_b_w_out': out['m_b_w_out'], 'm_b_b_out': out['m_b_b_out'], 'm_ffn_w_gate_up': out['m_ffn_w_gate_up'], 'm_ffn_w_down': out['m_ffn_w_down'], 'm_ple_w_up': out['m_ple_w_up'], 'm_ple_w_gate': out['m_ple_w_gate'], 'm_ple_b_gate': out['m_ple_b_gate'], 'm_ln_gain': out['m_ln_gain'], 'm_ln_bias': out['m_ln_bias'], 'v_a_w_in': out['v_a_w_in'], 'v_a_lower_bound': out['v_a_lower_bound'], 'v_a_norm_gain': out['v_a_norm_gain'], 'v_a_w_out': out['v_a_w_out'], 'v_kv_w': out['v_kv_w'], 'v_kv_b': out['v_kv_b'], 'v_b_w_q': out['v_b_w_q'], 'v_b_b_q': out['v_b_b_q'], 'v_b_sinks': out['v_b_sinks'], 'v_b_w_out': out['v_b_w_out'], 'v_b_b_out': out['v_b_b_out'], 'v_ffn_w_gate_up': out['v_ffn_w_gate_up'], 'v_ffn_w_down': out['v_ffn_w_down'], 'v_ple_w_up': out['v_ple_w_up'], 'v_ple_w_gate': out['v_ple_w_gate'], 'v_ple_b_gate': out['v_ple_b_gate'], 'v_ln_gain': out['v_ln_gain'], 'v_ln_bias': out['v_ln_bias']}


def _loss(weights, diff, rest, loss_target):
    with _jax.named_scope("forward"):
        args = {**rest, TWIN_DIFF_INPUT: diff, **{k: w.astype(_WEIGHT_DTYPES[k]) for k, w in weights.items()}}
        y = _forward(args)
    with _jax.named_scope("loss_head"):
        err = _jnp.square(y.astype(_jnp.float32) - loss_target)
        return 0.5 * _jnp.sum(_jnp.mean(err, axis=-1)) if err.ndim else 0.5 * err


def _adamw(w, g, m, v):
    m = ADAM_B1 * m + (1.0 - ADAM_B1) * g
    v = ADAM_B2 * v + (1.0 - ADAM_B2) * _jnp.square(g)
    m_hat = m / (1.0 - ADAM_B1 ** ADAM_STEP)
    v_hat = v / (1.0 - ADAM_B2 ** ADAM_STEP)
    delta = -ADAM_LR * (m_hat / (_jnp.sqrt(v_hat) + ADAM_EPS) + ADAM_WD * w)
    return delta, m, v


def reference(x, p, a_w_in, a_lower_bound, a_norm_gain, a_w_out, kv_w, kv_b, b_w_q, b_b_q, b_sinks, b_w_out, b_b_out, ffn_w_gate_up, ffn_w_down, ple_w_up, ple_w_gate, ple_b_gate, ln_gain, ln_bias, loss_target, m_a_w_in, m_a_lower_bound, m_a_norm_gain, m_a_w_out, m_kv_w, m_kv_b, m_b_w_q, m_b_b_q, m_b_sinks, m_b_w_out, m_b_b_out, m_ffn_w_gate_up, m_ffn_w_down, m_ple_w_up, m_ple_w_gate, m_ple_b_gate, m_ln_gain, m_ln_bias, v_a_w_in, v_a_lower_bound, v_a_norm_gain, v_a_w_out, v_kv_w, v_kv_b, v_b_w_q, v_b_b_q, v_b_sinks, v_b_w_out, v_b_b_out, v_ffn_w_gate_up, v_ffn_w_down, v_ple_w_up, v_ple_w_gate, v_ple_b_gate, v_ln_gain, v_ln_bias):
    given = dict(x=x, p=p, a_w_in=a_w_in, a_lower_bound=a_lower_bound, a_norm_gain=a_norm_gain, a_w_out=a_w_out, kv_w=kv_w, kv_b=kv_b, b_w_q=b_w_q, b_b_q=b_b_q, b_sinks=b_sinks, b_w_out=b_w_out, b_b_out=b_b_out, ffn_w_gate_up=ffn_w_gate_up, ffn_w_down=ffn_w_down, ple_w_up=ple_w_up, ple_w_gate=ple_w_gate, ple_b_gate=ple_b_gate, ln_gain=ln_gain, ln_bias=ln_bias, loss_target=loss_target, m_a_w_in=m_a_w_in, m_a_lower_bound=m_a_lower_bound, m_a_norm_gain=m_a_norm_gain, m_a_w_out=m_a_w_out, m_kv_w=m_kv_w, m_kv_b=m_kv_b, m_b_w_q=m_b_w_q, m_b_b_q=m_b_b_q, m_b_sinks=m_b_sinks, m_b_w_out=m_b_w_out, m_b_b_out=m_b_b_out, m_ffn_w_gate_up=m_ffn_w_gate_up, m_ffn_w_down=m_ffn_w_down, m_ple_w_up=m_ple_w_up, m_ple_w_gate=m_ple_w_gate, m_ple_b_gate=m_ple_b_gate, m_ln_gain=m_ln_gain, m_ln_bias=m_ln_bias, v_a_w_in=v_a_w_in, v_a_lower_bound=v_a_lower_bound, v_a_norm_gain=v_a_norm_gain, v_a_w_out=v_a_w_out, v_kv_w=v_kv_w, v_kv_b=v_kv_b, v_b_w_q=v_b_w_q, v_b_b_q=v_b_b_q, v_b_sinks=v_b_sinks, v_b_w_out=v_b_w_out, v_b_b_out=v_b_b_out, v_ffn_w_gate_up=v_ffn_w_gate_up, v_ffn_w_down=v_ffn_w_down, v_ple_w_up=v_ple_w_up, v_ple_w_gate=v_ple_w_gate, v_ple_b_gate=v_ple_b_gate, v_ln_gain=v_ln_gain, v_ln_bias=v_ln_bias)
    weights = {n: given[n] for n in TWIN_WEIGHTS}
    shared = {n: given[n] for n in SHARED_INPUTS}
    per_example = {n: given[n] for n in ['x', 'p']}
    grad_fn = _jax.value_and_grad(_loss, argnums=(0, 1))

    def one_microbatch(ex, loss_target):
        ex = dict(ex)
        diff = ex.pop(TWIN_DIFF_INPUT)
        return grad_fn(weights, diff, {**shared, **ex}, loss_target)

    if N_MICROBATCH == 1:
        loss, (grad_w, grad_x) = one_microbatch(per_example, given["loss_target"])
    else:
        def body(carry, xs):
            loss_sum, grad_sum = carry
            l_k, (gw_k, gx_k) = one_microbatch(xs[0], xs[1])
            with _jax.named_scope("update"):
                return (loss_sum + l_k, _jax.tree.map(_jnp.add, grad_sum, gw_k)), gx_k

        init = (_jnp.zeros((), _jnp.float32), _jax.tree.map(_jnp.zeros_like, weights))
        (loss, grad_w), grad_x = _jax.lax.scan(body, init, (per_example, given["loss_target"]))
    with _jax.named_scope("update"):
        delta_w, new_m, new_v = {}, {}, {}
        for n in TWIN_WEIGHTS:
            delta_w[n], new_m[n], new_v[n] = _adamw(weights[n], grad_w[n], given["m_" + n], given["v_" + n])
    return (loss, grad_x, *[grad_w[n] for n in TWIN_WEIGHTS], *[delta_w[n] for n in TWIN_WEIGHTS],
            *[new_m[n] for n in TWIN_WEIGHTS], *[new_v[n] for n in TWIN_WEIGHTS])
```

```python
import functools

import jax
import jax.numpy as jnp
from jax import lax
from jax.experimental import pallas as pl
from jax.experimental.pallas import tpu as pltpu

F32 = jnp.float32
CDT = jnp.bfloat16

N_DEV = 8
D = 1024
HG_H, HG_DK, HG_CH = 8, 128, 64
ATT_HD, ATT_QH, ATT_KVH, ATT_G, WINDOW = 64, 16, 4, 4, 128
FFN_H = 2816
FFN_B = FFN_H // 4
PLE_DIM = 256
ALPHA = (2.0 * 2) ** 0.25
LN_EPS = 1e-5
RMS_EPS = 1e-6
ADAM_LR, ADAM_B1, ADAM_B2, ADAM_EPS, ADAM_WD, ADAM_STEP = 0.001, 0.9, 0.999, 1e-08, 0.01, 10
ROW_TILES = (256, 128, 64)
VMEM_LIMIT = 48 * 1024 * 1024
NEG = -1e30

MESH = pl.DeviceIdType.MESH


def _tile(n, cands=ROW_TILES):
    for t in cands:
        if n % t == 0:
            return t
    return n


def _sds(shape, dtype):
    return jax.ShapeDtypeStruct(tuple(shape), dtype)


def _params(sem):
    return pltpu.CompilerParams(dimension_semantics=sem, vmem_limit_bytes=VMEM_LIMIT)


def _dot(a, b):
    return jnp.dot(a.astype(CDT), b.astype(CDT), preferred_element_type=F32)


def _dot_nt(a, b):
    return lax.dot_general(a.astype(CDT), b.astype(CDT), (((1,), (1,)), ((), ())), preferred_element_type=F32)


def _dot_tn(a, b):
    return lax.dot_general(a.astype(CDT), b.astype(CDT), (((0,), (0,)), ((), ())), preferred_element_type=F32)


def _sigmoid(x):
    return jax.nn.sigmoid(x)


def _ln_fwd(z, g, b):
    mu = jnp.mean(z, axis=-1, keepdims=True)
    zc = z - mu
    var = jnp.mean(zc * zc, axis=-1, keepdims=True)
    return zc * lax.rsqrt(var + LN_EPS) * g + b


def _ln_bwd(z, g, dy):
    mu = jnp.mean(z, axis=-1, keepdims=True)
    zc = z - mu
    var = jnp.mean(zc * zc, axis=-1, keepdims=True)
    rstd = lax.rsqrt(var + LN_EPS)
    xhat = zc * rstd
    dxh = dy * g
    dz = rstd * (dxh - jnp.mean(dxh, axis=-1, keepdims=True) - xhat * jnp.mean(dxh * xhat, axis=-1, keepdims=True))
    return dz, xhat


def _colsum(x):
    return jnp.sum(x, axis=0, keepdims=True)


def _acc(ref, val, first):
    @pl.when(first)
    def _():
        ref[...] = val

    @pl.when(jnp.logical_not(first))
    def _():
        ref[...] += val


def _mm_nn(a, b3, out_shape, oblock, omap, out_dtype, bias3=None, name="mm_nn"):
    M, K = a.shape
    G, _, Nb = b3.shape
    tm = _tile(M)

    def body(a_ref, b_ref, *rest):
        o_ref = rest[-1]
        acc = _dot(a_ref[...], b_ref[...])
        if bias3 is not None:
            acc = acc + rest[0][...]
        o_ref[...] = acc.astype(o_ref.dtype)

    in_specs = [pl.BlockSpec((tm, K), lambda g, i: (i, 0)), pl.BlockSpec((None, K, Nb), lambda g, i: (g, 0, 0))]
    args = [a, b3]
    if bias3 is not None:
        in_specs.append(pl.BlockSpec((None, 1, Nb), lambda g, i: (g, 0, 0)))
        args.append(bias3)
    return pl.pallas_call(
        body, grid=(G, M // tm), in_specs=in_specs, out_specs=pl.BlockSpec(oblock, omap),
        out_shape=_sds(out_shape, out_dtype), compiler_params=_params(("arbitrary", "arbitrary")), name=name)(*args)


def _mm_tn(a3, b3, G, amap, bmap, ablock, bblock, out_shape, oblock, omap, name="mm_tn"):
    S = a3.shape[1]
    tk = _tile(S, (512, 256, 128))
    Mo, No = oblock[-2], oblock[-1]

    def body(a_ref, b_ref, o_ref, acc):
        k = pl.program_id(1)
        _acc(acc, _dot_tn(a_ref[...], b_ref[...]), k == 0)

        @pl.when(k == pl.num_programs(1) - 1)
        def _():
            o_ref[...] = acc[...].astype(o_ref.dtype)

    return pl.pallas_call(
        body, grid=(G, S // tk),
        in_specs=[pl.BlockSpec(ablock(tk), amap), pl.BlockSpec(bblock(tk), bmap)],
        out_specs=pl.BlockSpec(oblock, omap), out_shape=_sds(out_shape, CDT),
        scratch_shapes=[pltpu.VMEM((Mo, No), F32)],
        compiler_params=_params(("arbitrary", "arbitrary")), name=name)(a3, b3)


def _wgrad(a3, b3, name):
    Ga, S, M = a3.shape
    Gb, _, N = b3.shape
    G = max(Ga, Gb)
    return _mm_tn(
        a3, b3, G,
        (lambda g, k: (g, k, 0)) if Ga > 1 else (lambda g, k: (0, k, 0)),
        (lambda g, k: (g, k, 0)) if Gb > 1 else (lambda g, k: (0, k, 0)),
        lambda tk: (None, tk, M), lambda tk: (None, tk, N),
        (G, M, N), (None, M, N), lambda g, k: (g, 0, 0), name=name)


def _mixout_ln(u3, w3, bias, xin, gain, beta, name):
    G, S, Kb = u3.shape
    tm = _tile(S)

    def body(u_ref, w_ref, b_ref, x_ref, g_ref, be_ref, z_ref, xo_ref, xob_ref):
        h = b_ref[...] + _dot(u_ref[0], w_ref[0])
        for g in range(1, G):
            h = h + _dot(u_ref[g], w_ref[g])
        z = ALPHA * x_ref[...] + h
        z_ref[...] = z
        y = _ln_fwd(z, g_ref[...], be_ref[...])
        xo_ref[...] = y
        xob_ref[...] = y.astype(CDT)

    row = pl.BlockSpec((tm, D), lambda i: (i, 0))
    vec = pl.BlockSpec((1, D), lambda i: (0, 0))
    return pl.pallas_call(
        body, grid=(S // tm,),
        in_specs=[pl.BlockSpec((G, tm, Kb), lambda i: (0, i, 0)), pl.BlockSpec((G, Kb, D), lambda i: (0, 0, 0)),
                  vec, row, vec, vec],
        out_specs=[row, row, row], out_shape=[_sds((S, D), F32), _sds((S, D), F32), _sds((S, D), CDT)],
        compiler_params=_params(("arbitrary",)), name=name)(u3, w3, bias, xin, gain, beta)


def _ffn_fwd(xin, xin_b, wgu, wdn, gain, beta, name):
    S = xin.shape[0]
    tm = _tile(S)

    def body(x_ref, xb_ref, wgu_ref, wdn_ref, g_ref, be_ref, gu_ref, hid_ref, z_ref, xo_ref, xob_ref, acc):
        j = pl.program_id(1)
        xb = xb_ref[...]
        gate = _dot(xb, wgu_ref[0])
        up = _dot(xb, wgu_ref[1])
        gu_ref[0] = gate
        gu_ref[1] = up
        hid = (gate * _sigmoid(gate) * up).astype(CDT)
        hid_ref[...] = hid
        _acc(acc, _dot(hid, wdn_ref[...]), j == 0)

        @pl.when(j == 3)
        def _():
            z = ALPHA * x_ref[...] + acc[...]
            z_ref[...] = z
            y = _ln_fwd(z, g_ref[...], be_ref[...])
            xo_ref[...] = y
            xob_ref[...] = y.astype(CDT)

    row = pl.BlockSpec((tm, D), lambda i, j: (i, 0))
    vec = pl.BlockSpec((1, D), lambda i, j: (0, 0))
    return pl.pallas_call(
        body, grid=(S // tm, 4),
        in_specs=[row, row, pl.BlockSpec((2, None, D, FFN_B), lambda i, j: (0, j, 0, 0)),
                  pl.BlockSpec((None, FFN_B, D), lambda i, j: (j, 0, 0)), vec, vec],
        out_specs=[pl.BlockSpec((2, None, tm, FFN_B), lambda i, j: (0, j, i, 0)),
                   pl.BlockSpec((None, tm, FFN_B), lambda i, j: (j, i, 0)), row, row, row],
        out_shape=[_sds((2, 4, S, FFN_B), F32), _sds((4, S, FFN_B), CDT), _sds((S, D), F32), _sds((S, D), F32),
                   _sds((S, D), CDT)],
        scratch_shapes=[pltpu.VMEM((tm, D), F32)],
        compiler_params=_params(("arbitrary", "arbitrary")), name=name)(xin, xin_b, wgu, wdn, gain, beta)


def _ple_fwd(xin, xin_b, p_b, wpg, bgate, wpu, gain, beta, name):
    S = xin.shape[0]
    tm = _tile(S)

    def body(x_ref, xb_ref, p_ref, wpg_ref, bg_ref, wpu_ref, g_ref, be_ref, sg_ref, up_ref, z_ref, xo_ref, xob_ref):
        sg = _sigmoid(_dot(xb_ref[...], wpg_ref[...]) + bg_ref[...])
        up = _dot(p_ref[...], wpu_ref[...])
        sg_ref[...] = sg
        up_ref[...] = up
        z = ALPHA * x_ref[...] + sg * up
        z_ref[...] = z
        y = _ln_fwd(z, g_ref[...], be_ref[...])
        xo_ref[...] = y
        xob_ref[...] = y.astype(CDT)

    row = pl.BlockSpec((tm, D), lambda i: (i, 0))
    vec = pl.BlockSpec((1, D), lambda i: (0, 0))
    return pl.pallas_call(
        body, grid=(S // tm,),
        in_specs=[row, row, pl.BlockSpec((tm, PLE_DIM), lambda i: (i, 0)), pl.BlockSpec((D, D), lambda i: (0, 0)), vec,
                  pl.BlockSpec((PLE_DIM, D), lambda i: (0, 0)), vec, vec],
        out_specs=[row] * 5,
        out_shape=[_sds((S, D), F32)] * 4 + [_sds((S, D), CDT)],
        compiler_params=_params(("arbitrary",)), name=name)(xin, xin_b, p_b, wpg, bgate, wpu, gain, beta)


def _loss_fwd_bwd(y, target):
    S = y.shape[0]
    tm = _tile(S)

    def body(y_ref, t_ref, l_ref, dy_ref):
        e = y_ref[...] - t_ref[...]
        dy_ref[...] = e * (1.0 / D)
        part = 0.5 * jnp.sum(jnp.sum(e * e, axis=-1, keepdims=True) * (1.0 / D), axis=0, keepdims=True)
        _acc(l_ref, part, pl.program_id(0) == 0)

    row = pl.BlockSpec((tm, D), lambda i: (i, 0))
    return pl.pallas_call(
        body, grid=(S // tm,), in_specs=[row, row],
        out_specs=[pl.BlockSpec((1, 1), lambda i: (0, 0)), row],
        out_shape=[_sds((1, 1), F32), _sds((S, D), F32)],
        compiler_params=_params(("arbitrary",)), name="loss")(y, target)


def _ple_bwd(dy, z, sg, up, gain, wpg, name):
    S = dy.shape[0]
    tm = _tile(S)

    def body(dy_ref, z_ref, sg_ref, up_ref, g_ref, wpg_ref, dx_ref, dgl_ref, dup_ref, dgain_ref, dbeta_ref, dbg_ref):
        first = pl.program_id(0) == 0
        dy_ = dy_ref[...]
        dz, xhat = _ln_bwd(z_ref[...], g_ref[...], dy_)
        sg_ = sg_ref[...]
        dgl = dz * up_ref[...] * sg_ * (1.0 - sg_)
        dgl_ref[...] = dgl.astype(CDT)
        dup_ref[...] = (dz * sg_).astype(CDT)
        dx_ref[...] = ALPHA * dz + _dot_nt(dgl, wpg_ref[...])
        _acc(dgain_ref, _colsum(dy_ * xhat), first)
        _acc(dbeta_ref, _colsum(dy_), first)
        _acc(dbg_ref, _colsum(dgl), first)

    row = pl.BlockSpec((tm, D), lambda i: (i, 0))
    vec = pl.BlockSpec((1, D), lambda i: (0, 0))
    return pl.pallas_call(
        body, grid=(S // tm,), in_specs=[row, row, row, row, vec, pl.BlockSpec((D, D), lambda i: (0, 0))],
        out_specs=[row, row, row, vec, vec, vec],
        out_shape=[_sds((S, D), F32), _sds((S, D), CDT), _sds((S, D), CDT)] + [_sds((1, D), F32)] * 3,
        compiler_params=_params(("arbitrary",)), name=name)(dy, z, sg, up, gain, wpg)


def _ffn_bwd(dy, z, gu, wgu, wdn, gain, name):
    S = dy.shape[0]
    tm = _tile(S)

    def body(dy_ref, z_ref, gu_ref, wgu_ref, wdn_ref, g_ref, dx_ref, dzb_ref, dgu_ref, dgain_ref, dbeta_ref,
             dz_scr, acc):
        i, j = pl.program_id(0), pl.program_id(1)

        @pl.when(j == 0)
        def _():
            dy_ = dy_ref[...]
            dz, xhat = _ln_bwd(z_ref[...], g_ref[...], dy_)
            dz_scr[...] = dz
            dzb_ref[...] = dz.astype(CDT)
            _acc(dgain_ref, _colsum(dy_ * xhat), i == 0)
            _acc(dbeta_ref, _colsum(dy_), i == 0)

        dhid = _dot_nt(dz_scr[...], wdn_ref[...])
        gate, up = gu_ref[0], gu_ref[1]
        sg = _sigmoid(gate)
        dgate = (dhid * up * (sg * (1.0 + gate * (1.0 - sg)))).astype(CDT)
        dup = (dhid * (gate * sg)).astype(CDT)
        dgu_ref[0] = dgate
        dgu_ref[1] = dup
        _acc(acc, _dot_nt(dgate, wgu_ref[0]) + _dot_nt(dup, wgu_ref[1]), j == 0)

        @pl.when(j == 3)
        def _():
            dx_ref[...] = ALPHA * dz_scr[...] + acc[...]

    row = pl.BlockSpec((tm, D), lambda i, j: (i, 0))
    vec = pl.BlockSpec((1, D), lambda i, j: (0, 0))
    return pl.pallas_call(
        body, grid=(S // tm, 4),
        in_specs=[row, row, pl.BlockSpec((2, None, tm, FFN_B), lambda i, j: (0, j, i, 0)),
                  pl.BlockSpec((2, None, D, FFN_B), lambda i, j: (0, j, 0, 0)),
                  pl.BlockSpec((None, FFN_B, D), lambda i, j: (j, 0, 0)), vec],
        out_specs=[row, row, pl.BlockSpec((2, None, tm, FFN_B), lambda i, j: (0, j, i, 0)), vec, vec],
        out_shape=[_sds((S, D), F32), _sds((S, D), CDT), _sds((2, 4, S, FFN_B), CDT), _sds((1, D), F32),
                   _sds((1, D), F32)],
        scratch_shapes=[pltpu.VMEM((tm, D), F32), pltpu.VMEM((tm, D), F32)],
        compiler_params=_params(("arbitrary", "arbitrary")), name=name)(dy, z, gu, wgu, wdn, gain)


def _mixout_bwd(dy, z, gain, w3, du_dtype, name):
    S = dy.shape[0]
    G, Kb, _ = w3.shape
    tm = _tile(S)

    def body(dy_ref, z_ref, g_ref, w_ref, dz_ref, dzb_ref, du_ref, dgain_ref, dbeta_ref, dbias_ref):
        first = pl.program_id(0) == 0
        dy_ = dy_ref[...]
        dz, xhat = _ln_bwd(z_ref[...], g_ref[...], dy_)
        dz_ref[...] = dz
        dzb = dz.astype(CDT)
        dzb_ref[...] = dzb
        for g in range(G):
            du_ref[g] = _dot_nt(dzb, w_ref[g]).astype(du_ref.dtype)
        _acc(dgain_ref, _colsum(dy_ * xhat), first)
        _acc(dbeta_ref, _colsum(dy_), first)
        _acc(dbias_ref, _colsum(dz), first)

    row = pl.BlockSpec((tm, D), lambda i: (i, 0))
    vec = pl.BlockSpec((1, D), lambda i: (0, 0))
    return pl.pallas_call(
        body, grid=(S // tm,), in_specs=[row, row, vec, pl.BlockSpec((G, Kb, D), lambda i: (0, 0, 0))],
        out_specs=[row, row, pl.BlockSpec((G, tm, Kb), lambda i: (0, i, 0)), vec, vec, vec],
        out_shape=[_sds((S, D), F32), _sds((S, D), CDT), _sds((G, S, Kb), du_dtype)] + [_sds((1, D), F32)] * 3,
        compiler_params=_params(("arbitrary",)), name=name)(dy, z, gain, w3)


def _qkv_bwd(dz, dq4, dkv4, wq4, wkv4, name):
    S = dz.shape[0]
    tm = _tile(S)
    HQ, HK = dq4.shape[0], dkv4.shape[0]

    def body(dz_ref, dq_ref, dkv_ref, wq_ref, wkv_ref, dx_ref, dkvb_ref):
        first = pl.program_id(0) == 0
        acc = ALPHA * dz_ref[...]
        for h in range(HQ):
            acc = acc + _dot_nt(dq_ref[h], wq_ref[h])
        for h in range(HK):
            acc = acc + _dot_nt(dkv_ref[h], wkv_ref[h])
        dx_ref[...] = acc
        for h in range(HK):
            _acc(dkvb_ref.at[h], _colsum(dkv_ref[h]), first)

    row = pl.BlockSpec((tm, D), lambda i: (i, 0))
    return pl.pallas_call(
        body, grid=(S // tm,),
        in_specs=[row, pl.BlockSpec((HQ, tm, ATT_HD), lambda i: (0, i, 0)), pl.BlockSpec((HK, tm, ATT_HD), lambda i: (0, i, 0)),
                  pl.BlockSpec((HQ, D, ATT_HD), lambda i: (0, 0, 0)), pl.BlockSpec((HK, D, ATT_HD), lambda i: (0, 0, 0))],
        out_specs=[row, pl.BlockSpec((HK, 1, ATT_HD), lambda i: (0, 0, 0))],
        out_shape=[_sds((S, D), F32), _sds((HK, 1, ATT_HD), F32)],
        compiler_params=_params(("arbitrary",)), name=name)(dz, dq4, dkv4, wq4, wkv4)


def _inproj_bwd(dz, dproj, wain, name):
    S = dz.shape[0]
    tm = _tile(S)
    nb = wain.shape[-1]

    def body(dz_ref, dp_ref, w_ref, dx_ref, acc):
        j = pl.program_id(1)
        _acc(acc, _dot_nt(dp_ref[...], w_ref[...]), j == 0)

        @pl.when(j == N_DEV - 1)
        def _():
            dx_ref[...] = ALPHA * dz_ref[...] + acc[...]

    row = pl.BlockSpec((tm, D), lambda i, j: (i, 0))
    return pl.pallas_call(
        body, grid=(S // tm, N_DEV),
        in_specs=[row, pl.BlockSpec((None, tm, nb), lambda i, j: (j // 2, i, j % 2)),
                  pl.BlockSpec((None, D, nb), lambda i, j: (j, 0, 0))],
        out_specs=row, out_shape=_sds((S, D), F32), scratch_shapes=[pltpu.VMEM((tm, D), F32)],
        compiler_params=_params(("arbitrary", "arbitrary")), name=name)(dz, dproj, wain)


def _hp(a, b, dims):
    ah = a.astype(CDT)
    al = (a - ah.astype(F32)).astype(CDT)
    bh = b.astype(CDT)
    bl = (b - bh.astype(F32)).astype(CDT)
    dg = lambda u, w: lax.dot_general(u, w, (dims, ((), ())), preferred_element_type=F32)
    return dg(ah, bh) + (dg(ah, bl) + dg(al, bh))


def _hdot(a, b):
    return _hp(a, b, ((1,), (0,)))


def _hdot_nt(a, b):
    return _hp(a, b, ((1,), (1,)))


def _hdot_tn(a, b):
    return _hp(a, b, ((0,), (0,)))


def _tri_dot(tri, x):
    hi = x.astype(CDT)
    r1 = x - hi.astype(F32)
    mid = r1.astype(CDT)
    lo = (r1 - mid.astype(F32)).astype(CDT)
    t = tri.astype(CDT)
    return (jnp.dot(t, hi, preferred_element_type=F32) + jnp.dot(t, mid, preferred_element_type=F32)
            + jnp.dot(t, lo, preferred_element_type=F32))


def _hg_gates(q, f, alb_ref):
    a0, a1 = alb_ref[0:1, :], alb_ref[1:2, :]
    mx = jnp.maximum(a0, a1)
    e0, e1 = jnp.exp(a0 - mx), jnp.exp(a1 - mx)
    lb = e0 / (e0 + e1)
    sig = _sigmoid(f)
    forget = lb + (1.0 - lb) * sig
    k = (1.0 - lb) * _sigmoid(-f)
    qs = q * _sigmoid(q) * (HG_DK ** -0.5)
    return qs, k, jnp.log(forget), sig, lb, forget


def _hg_intra(qs, k, b, b_scr):
    b_scr[...] = b
    bm = b_scr[pl.ds(HG_CH // 2 - 1, 1), :]
    bl = b_scr[pl.ds(HG_CH - 1, 1), :]
    eb = jnp.exp(b)
    qb = qs * eb
    e_q = jnp.exp(b - bm)
    e_k = jnp.exp(bm - b)
    e_d = jnp.exp(bl - b)
    return qb, qs * e_q, k * e_k, k * e_d, jnp.exp(bl), eb, e_q, e_k, e_d


def _hgrn_fwd(proj, alb, ngain):
    S = proj.shape[1]
    nc = S // HG_CH

    def body(pj_ref, alb_ref, ng_ref, o_ref, y_ref, st_ref, st_scr, b_scr):
        n = pl.program_id(1)

        @pl.when(n == 0)
        def _():
            st_scr[...] = jnp.zeros_like(st_scr)

        q, f, v, g = pj_ref[0], pj_ref[1], pj_ref[2], pj_ref[3]
        qs, k, logf, _, _, _ = _hg_gates(q, f, alb_ref)
        r = lax.broadcasted_iota(jnp.int32, (HG_CH, HG_CH), 0)
        c = lax.broadcasted_iota(jnp.int32, (HG_CH, HG_CH), 1)
        causal = r >= c
        b = _tri_dot(causal.astype(F32), logf)
        qb, qt, kt, kd, ebl, _, _, _, _ = _hg_intra(qs, k, b, b_scr)
        st = st_scr[...]
        st_ref[...] = st
        a = jnp.where(causal, _hdot_nt(qt, kt), 0.0)
        o = _hdot(a, v) + _hdot_nt(qb, st)
        st_scr[...] = st * ebl + _hdot_tn(v, kd)
        o_ref[...] = o
        rinv = lax.rsqrt(jnp.mean(o * o, axis=-1, keepdims=True) + RMS_EPS)
        y_ref[...] = (o * rinv * ng_ref[...] * (g * _sigmoid(g))).astype(CDT)

    blk = pl.BlockSpec((HG_CH, HG_DK), lambda h, n: (n, h))
    return pl.pallas_call(
        body, grid=(HG_H, nc),
        in_specs=[pl.BlockSpec((4, HG_CH, HG_DK), lambda h, n: (0, n, h)), pl.BlockSpec((2, HG_DK), lambda h, n: (0, h)),
                  pl.BlockSpec((1, HG_DK), lambda h, n: (0, 0))],
        out_specs=[blk, blk, pl.BlockSpec((None, None, HG_DK, HG_DK), lambda h, n: (h, n, 0, 0))],
        out_shape=[_sds((S, D), F32), _sds((S, D), CDT), _sds((HG_H, nc, HG_DK, HG_DK), F32)],
        scratch_shapes=[pltpu.VMEM((HG_DK, HG_DK), F32), pltpu.VMEM((HG_CH, HG_DK), F32)],
        compiler_params=_params(("arbitrary", "arbitrary")), name="hgrn_fwd")(proj, alb, ngain)


def _hgrn_bwd(proj, alb, ngain, o, states, dy):
    S = proj.shape[1]
    nc = S // HG_CH

    def body(pj_ref, alb_ref, ng_ref, o_ref, st_ref, dy_ref, dpj_ref, dalb_ref, dng_ref, dst_scr, b_scr):
        h, n = pl.program_id(0), pl.program_id(1)

        @pl.when(n == 0)
        def _():
            dst_scr[...] = jnp.zeros_like(dst_scr)

        q, f, v, g = pj_ref[0], pj_ref[1], pj_ref[2], pj_ref[3]
        o_ = o_ref[...]
        dy_ = dy_ref[...]
        ng = ng_ref[...]
        sg = _sigmoid(g)
        rinv = lax.rsqrt(jnp.mean(o_ * o_, axis=-1, keepdims=True) + RMS_EPS)
        nrm = o_ * rinv
        dr = dy_ * (g * sg)
        dg = dy_ * nrm * ng * (sg * (1.0 + g * (1.0 - sg)))
        dn = dr * ng
        do = rinv * (dn - nrm * jnp.mean(dn * nrm, axis=-1, keepdims=True))
        _acc(dng_ref, _colsum(dr * nrm), jnp.logical_and(h == 0, n == 0))
        qs, k, logf, sig, lb, forget = _hg_gates(q, f, alb_ref)
        r = lax.broadcasted_iota(jnp.int32, (HG_CH, HG_CH), 0)
        c = lax.broadcasted_iota(jnp.int32, (HG_CH, HG_CH), 1)
        causal = r >= c
        b = _tri_dot(causal.astype(F32), logf)
        qb, qt, kt, kd, ebl, eb, e_q, e_k, e_d = _hg_intra(qs, k, b, b_scr)
        st = st_ref[...]
        dstn = dst_scr[...]
        a = jnp.where(causal, _hdot_nt(qt, kt), 0.0)
        da = jnp.where(causal, _hdot_nt(do, v), 0.0)
        dv = _hdot_tn(a, do) + _hdot_nt(kd, dstn)
        dqb = _hdot(do, st)
        dkd = _hdot(v, dstn)
        dqt = _hdot(da, kt)
        dkt = _hdot_tn(da, qt)
        dbl = _colsum(dkd * kd) + ebl * _colsum(dstn * st)
        dst_scr[...] = dstn * ebl + _hdot_tn(do, qb)
        dqs = dqt * e_q + dqb * eb
        dk = dkt * e_k + dkd * e_d
        db = dqt * qt + dqb * qb - dkt * kt - dkd * kd
        dlogf = _tri_dot((r <= c).astype(F32), db) + dbl
        dforget = dlogf / forget
        dsig = (1.0 - lb) * (dforget - dk)
        df = dsig * sig * (1.0 - sig)
        dlb = _colsum((dforget - dk) * (1.0 - sig))
        sq = _sigmoid(q)
        dq = dqs * (HG_DK ** -0.5) * (sq * (1.0 + q * (1.0 - sq)))
        dpj_ref[0] = dq.astype(CDT)
        dpj_ref[1] = df.astype(CDT)
        dpj_ref[2] = dv.astype(CDT)
        dpj_ref[3] = dg.astype(CDT)
        da0 = dlb * lb * (1.0 - lb)
        _acc(dalb_ref.at[pl.ds(0, 1)], da0, n == 0)
        _acc(dalb_ref.at[pl.ds(1, 1)], -da0, n == 0)

    rev = lambda h, n: (nc - 1 - n, h)
    blk = pl.BlockSpec((HG_CH, HG_DK), rev)
    return pl.pallas_call(
        body, grid=(HG_H, nc),
        in_specs=[pl.BlockSpec((4, HG_CH, HG_DK), lambda h, n: (0, nc - 1 - n, h)),
                  pl.BlockSpec((2, HG_DK), lambda h, n: (0, h)), pl.BlockSpec((1, HG_DK), lambda h, n: (0, 0)), blk,
                  pl.BlockSpec((None, None, HG_DK, HG_DK), lambda h, n: (h, nc - 1 - n, 0, 0)), blk],
        out_specs=[pl.BlockSpec((4, HG_CH, HG_DK), lambda h, n: (0, nc - 1 - n, h)),
                   pl.BlockSpec((2, HG_DK), lambda h, n: (0, h)), pl.BlockSpec((1, HG_DK), lambda h, n: (0, 0))],
        out_shape=[_sds((4, S, D), CDT), _sds((2, D), F32), _sds((1, HG_DK), F32)],
        scratch_shapes=[pltpu.VMEM((HG_DK, HG_DK), F32), pltpu.VMEM((HG_CH, HG_DK), F32)],
        compiler_params=_params(("arbitrary", "arbitrary")), name="hgrn_bwd")(proj, alb, ngain, o, states, dy)


def _slope(h):
    return 2.0 ** (-8.0 * (h + 1) / ATT_QH)


def _attn_mask(n):
    qi = lax.broadcasted_iota(jnp.int32, (WINDOW, 2 * WINDOW), 0)
    si = lax.broadcasted_iota(jnp.int32, (WINDOW, 2 * WINDOW), 1)
    dist = qi - si + WINDOW
    valid = (dist >= 0) & (dist < WINDOW) & (n * WINDOW - WINDOW + si >= 0)
    return valid, dist.astype(F32)


def _attn_probs(qh, kh, sink, slope, valid, distf):
    s = _dot_nt(qh, kh) * (ATT_HD ** -0.5) - slope * distf
    s = jnp.where(valid, s, NEG)
    m = jnp.maximum(jnp.max(s, axis=-1, keepdims=True), sink)
    e = jnp.exp(s - m)
    es = jnp.exp(sink - m)
    inv = 1.0 / (jnp.sum(e, axis=-1, keepdims=True) + es)
    return e * inv, es * inv


def _attn_specs(S):
    nb = S // WINDOW
    cur = lambda H: pl.BlockSpec((H, WINDOW, ATT_HD), lambda n: (0, n, 0))
    prev = lambda H: pl.BlockSpec((H, WINDOW, ATT_HD), lambda n: (0, jnp.maximum(n - 1, 0), 0))
    return nb, cur, prev


def _attn_fwd(q4, kv4, sinks):
    S = q4.shape[1]
    nb, cur, prev = _attn_specs(S)

    def body(sink_ref, q_ref, kvc_ref, kvp_ref, o_ref):
        valid, distf = _attn_mask(pl.program_id(0))
        for h in range(ATT_QH):
            kvh = h // ATT_G
            kh = jnp.concatenate([kvp_ref[kvh], kvc_ref[kvh]], axis=0)
            vh = jnp.concatenate([kvp_ref[ATT_KVH + kvh], kvc_ref[ATT_KVH + kvh]], axis=0)
            p, _ = _attn_probs(q_ref[h], kh, sink_ref[0, h], _slope(h), valid, distf)
            o_ref[h] = _dot(p, vh).astype(CDT)

    return pl.pallas_call(
        body, grid=(nb,),
        in_specs=[pl.BlockSpec(memory_space=pltpu.SMEM), cur(ATT_QH), cur(2 * ATT_KVH), prev(2 * ATT_KVH)],
        out_specs=cur(ATT_QH), out_shape=_sds((ATT_QH, S, ATT_HD), CDT),
        compiler_params=_params(("arbitrary",)), name="attn_fwd")(sinks, q4, kv4, kv4)


def _attn_bwd(q4, kv4, sinks, do4):
    S = q4.shape[1]
    nb, cur, prev = _attn_specs(S)

    def body(sink_ref, q_ref, kvc_ref, kvp_ref, do_ref, dq_ref, dkv_ref, dbq_ref, dsink_ref):
        n = pl.program_id(0)
        first = n == 0

        @pl.when(first)
        def _():
            dkv_ref[...] = jnp.zeros_like(dkv_ref)
            dsink_ref[...] = jnp.zeros_like(dsink_ref)

        valid, distf = _attn_mask(n)
        lane = lax.broadcasted_iota(jnp.int32, (1, 128), 1)
        rows_cur = pl.ds(pl.multiple_of(n * WINDOW, WINDOW), WINDOW)
        rows_prev = pl.ds(pl.multiple_of(jnp.maximum(n - 1, 0) * WINDOW, WINDOW), WINDOW)
        for h in range(ATT_QH):
            kvh = h // ATT_G
            qh = q_ref[h]
            doh = do_ref[h]
            kh = jnp.concatenate([kvp_ref[kvh], kvc_ref[kvh]], axis=0)
            vh = jnp.concatenate([kvp_ref[ATT_KVH + kvh], kvc_ref[ATT_KVH + kvh]], axis=0)
            p, ps = _attn_probs(qh, kh, sink_ref[0, h], _slope(h), valid, distf)
            dp = _dot_nt(doh, vh)
            dd = jnp.sum(p * dp, axis=-1, keepdims=True)
            ds = p * (dp - dd)
            dsink = -jnp.sum(ps * dd, axis=0, keepdims=True)
            dsink_ref[...] += jnp.where(lane == h, dsink, 0.0)
            dqh = _dot(ds, kh) * (ATT_HD ** -0.5)
            dq_ref[h] = dqh.astype(CDT)
            _acc(dbq_ref.at[h], _colsum(dqh), first)
            dkh = _dot_tn(ds, qh) * (ATT_HD ** -0.5)
            dvh = _dot_tn(p, doh)
            dkv_ref[kvh, rows_prev, :] += dkh[:WINDOW]
            dkv_ref[kvh, rows_cur, :] += dkh[WINDOW:]
            dkv_ref[ATT_KVH + kvh, rows_prev, :] += dvh[:WINDOW]
            dkv_ref[ATT_KVH + kvh, rows_cur, :] += dvh[WINDOW:]

    return pl.pallas_call(
        body, grid=(nb,),
        in_specs=[pl.BlockSpec(memory_space=pltpu.SMEM), cur(ATT_QH), cur(2 * ATT_KVH), prev(2 * ATT_KVH), cur(ATT_QH)],
        out_specs=[cur(ATT_QH), pl.BlockSpec((2 * ATT_KVH, S, ATT_HD), lambda n: (0, 0, 0)),
                   pl.BlockSpec((ATT_QH, 1, ATT_HD), lambda n: (0, 0, 0)), pl.BlockSpec((1, 128), lambda n: (0, 0))],
        out_shape=[_sds((ATT_QH, S, ATT_HD), CDT), _sds((2 * ATT_KVH, S, ATT_HD), F32), _sds((ATT_QH, 1, ATT_HD), F32),
                   _sds((1, 128), F32)],
        compiler_params=_params(("arbitrary",)), name="attn_bwd")(sinks, q4, kv4, kv4, do4)


def _local_step(x, p, target, W, sm):
    S = x.shape[0]
    vec = lambda a: a.reshape(1, -1)
    ln_g = lambda l, k: vec(sm["ln_gain"][l, k])
    ln_b = lambda l, k: vec(sm["ln_bias"][l, k])
    xb = x.astype(CDT)
    pb = p.astype(CDT)

    proj = _mm_nn(xb, W["a_w_in"], (4, S, D), (None, _tile(S), 512), lambda g, i: (g // 2, i, g % 2), F32, name="a_in")
    o_a, y_a, states = _hgrn_fwd(proj, sm["a_lower_bound"], sm["a_norm_gain"])
    zeros = jnp.zeros((1, D), F32)
    z = [[None] * 3 for _ in range(2)]
    xs = [[None] * 3 for _ in range(2)]
    xbs = [[None] * 3 for _ in range(2)]
    z[0][0], xs[0][0], xbs[0][0] = _mixout_ln(y_a[None], W["a_w_out"][None], zeros, x, ln_g(0, 0), ln_b(0, 0), "a_out_ln")
    gu, hid, sgs, ups = [None, None], [None, None], [None, None], [None, None]

    def ffn_ple(l):
        gu[l], hid[l], z[l][1], xs[l][1], xbs[l][1] = _ffn_fwd(
            xs[l][0], xbs[l][0], W["ffn_w_gate_up"][l], W["ffn_w_down"][l], ln_g(l, 1), ln_b(l, 1), f"ffn_fwd{l}")
        sgs[l], ups[l], z[l][2], xs[l][2], xbs[l][2] = _ple_fwd(
            xs[l][1], xbs[l][1], pb[l], W["ple_w_gate"][l], vec(sm["ple_b_gate"][l]), W["ple_w_up"][l], ln_g(l, 2),
            ln_b(l, 2), f"ple_fwd{l}")

    ffn_ple(0)
    x3, x3b = xs[0][2], xbs[0][2]
    kv4 = _mm_nn(x3b, W["kv_w"], (2 * ATT_KVH, S, ATT_HD), (None, _tile(S), ATT_HD), lambda g, i: (g, i, 0), CDT,
                 bias3=sm["kv_b"].reshape(2 * ATT_KVH, 1, ATT_HD), name="kv_proj")
    q4 = _mm_nn(x3b, W["b_w_q"], (ATT_QH, S, ATT_HD), (None, _tile(S), ATT_HD), lambda g, i: (g, i, 0), CDT,
                bias3=sm["b_b_q"].reshape(ATT_QH, 1, ATT_HD), name="q_proj")
    o4 = _attn_fwd(q4, kv4, sm["b_sinks"])
    z[1][0], xs[1][0], xbs[1][0] = _mixout_ln(o4, W["b_w_out"], sm["b_b_out"], x3, ln_g(1, 0), ln_b(1, 0), "b_out_ln")
    ffn_ple(1)
    loss, dy = _loss_fwd_bwd(xs[1][2], target)

    gW, gs = {}, {}
    d_ln_g = [[None] * 3 for _ in range(2)]
    d_ln_b = [[None] * 3 for _ in range(2)]
    g_gu, g_dn, g_pu, g_pg, g_bg = [None, None], [None, None], [None, None], [None, None], [None, None]

    def ffn_ple_bwd(l, dy):
        dx2, dgl, dup, d_ln_g[l][2], d_ln_b[l][2], g_bg[l] = _ple_bwd(dy, z[l][2], sgs[l], ups[l], ln_g(l, 2),
                                                                     W["ple_w_gate"][l], f"ple_bwd{l}")
        g_pg[l] = _wgrad(xbs[l][1][None], dgl[None], f"g_ple_gate{l}")[0]
        g_pu[l] = _wgrad(pb[l][None], dup[None], f"g_ple_up{l}")[0]
        dx1, dzb, dgu, d_ln_g[l][1], d_ln_b[l][1] = _ffn_bwd(dx2, z[l][1], gu[l], W["ffn_w_gate_up"][l],
                                                           W["ffn_w_down"][l], ln_g(l, 1), f"ffn_bwd{l}")
        g_dn[l] = _wgrad(hid[l], dzb[None], f"g_ffn_down{l}")
        g_gu[l] = _wgrad(xbs[l][0][None], dgu.reshape(8, S, FFN_B), f"g_ffn_gate_up{l}")
        return dx1

    dx1 = ffn_ple_bwd(1, dy)
    dz, dzb, do4, d_ln_g[1][0], d_ln_b[1][0], gs["b_b_out"] = _mixout_bwd(dx1, z[1][0], ln_g(1, 0), W["b_w_out"], CDT,
                                                                         "b_out_bwd")
    gW["b_w_out"] = _wgrad(o4, dzb[None], "g_b_w_out")
    dq4, dkv4, dbq, dsinks = _attn_bwd(q4, kv4, sm["b_sinks"], do4)
    gs["b_b_q"] = dbq
    gs["b_sinks"] = dsinks
    gW["b_w_q"] = _wgrad(x3b[None], dq4, "g_b_w_q")
    gW["kv_w"] = _wgrad(x3b[None], dkv4.astype(CDT), "g_kv_w")
    dx3, gs["kv_b"] = _qkv_bwd(dz, dq4, dkv4, W["b_w_q"], W["kv_w"], "qkv_bwd")
    dx1 = ffn_ple_bwd(0, dx3)
    dz, dzb, dyr, d_ln_g[0][0], d_ln_b[0][0], _ = _mixout_bwd(dx1, z[0][0], ln_g(0, 0), W["a_w_out"][None], F32,
                                                              "a_out_bwd")
    gW["a_w_out"] = _wgrad(y_a[None], dzb[None], "g_a_w_out")[0]
    dproj, gs["a_lower_bound"], gs["a_norm_gain"] = _hgrn_bwd(proj, sm["a_lower_bound"], sm["a_norm_gain"], o_a, states,
                                                              dyr[0])
    tk = lambda t: (None, t, D)
    gW["a_w_in"] = _mm_tn(xb[None], dproj, N_DEV, lambda g, k: (0, k, 0), lambda g, k: (g // 2, k, g % 2),
                          tk, lambda t: (None, t, 512), (N_DEV, D, 512), (None, D, 512), lambda g, k: (g, 0, 0),
                          name="g_a_w_in")
    grad_x = _inproj_bwd(dz, dproj, W["a_w_in"], "a_in_bwd")
    gW["ffn_w_gate_up"], gW["ffn_w_down"], gW["ple_w_up"], gW["ple_w_gate"] = g_gu, g_dn, g_pu, g_pg
    gs["ple_b_gate"] = jnp.concatenate(g_bg, axis=0)
    gs["ln_gain"] = jnp.stack([jnp.concatenate(r, axis=0) for r in d_ln_g])
    gs["ln_bias"] = jnp.stack([jnp.concatenate(r, axis=0) for r in d_ln_b])
    return loss, grad_x, gW, gs


def _peer(k):
    x, y, c = lax.axis_index("x"), lax.axis_index("y"), lax.axis_index("c")
    px = 1 - x if k & 4 else x
    py = 1 - y if k & 2 else y
    pc = 1 - c if k & 1 else c
    return (px, py, pc), 4 * px + 2 * py + pc


def _my_index():
    return 4 * lax.axis_index("x") + 2 * lax.axis_index("y") + lax.axis_index("c")


def _exchange(srcs, dst_shapes, plan, name):
    n_src, n_piece = len(srcs), len(plan)

    def body(*refs):
        src_refs, dst_refs = refs[:n_src], refs[n_src:n_src + len(dst_shapes)]
        send_sems, recv_sems, local_sems = refs[n_src + len(dst_shapes):]
        me = _my_index()

        def at(ref, idx):
            return ref.at[idx] if idx else ref

        local = []
        for t, (si, sfn, di, dfn) in enumerate(plan):
            cp = pltpu.make_async_copy(at(src_refs[si], sfn(me)), at(dst_refs[di], dfn(me)), local_sems.at[t])
            cp.start()
            local.append(cp)
        sends = []
        for k in range(1, N_DEV):
            peer, pid = _peer(k)
            for t, (si, sfn, di, dfn) in enumerate(plan):
                cp = pltpu.make_async_remote_copy(
                    src_ref=at(src_refs[si], sfn(pid)), dst_ref=at(dst_refs[di], dfn(me)),
                    send_sem=send_sems.at[t * 7 + k - 1], recv_sem=recv_sems.at[t * 7 + k - 1],
                    device_id=peer, device_id_type=MESH)
                cp.start()
                sends.append(cp)
        for k in range(1, N_DEV):
            peer, pid = _peer(k)
            for t, (si, sfn, di, dfn) in enumerate(plan):
                pltpu.make_async_remote_copy(
                    src_ref=at(src_refs[si], sfn(me)), dst_ref=at(dst_refs[di], dfn(pid)),
                    send_sem=send_sems.at[t * 7 + k - 1], recv_sem=recv_sems.at[t * 7 + k - 1],
                    device_id=peer, device_id_type=MESH).wait_recv()
        for cp in sends:
            cp.wait_send()
        for cp in local:
            cp.wait()

    hbm = pl.BlockSpec(memory_space=pltpu.HBM)
    return pl.pallas_call(
        body, in_specs=[hbm] * n_src, out_specs=[hbm] * len(dst_shapes), out_shape=dst_shapes,
        scratch_shapes=[pltpu.SemaphoreType.DMA((7 * n_piece,)), pltpu.SemaphoreType.DMA((7 * n_piece,)),
                        pltpu.SemaphoreType.DMA((n_piece,))],
        name=name)(*srcs)


def _gather(shards, name):
    srcs, dsts, plan = [], [], []
    for i, (a, layers) in enumerate(shards):
        srcs.append(a)
        if layers is None:
            dsts.append(_sds((N_DEV,) + a.shape, a.dtype))
            plan.append((i, lambda j: (), i, lambda s: (s,)))
        else:
            dsts.append(_sds((layers, N_DEV) + a.shape[1:], a.dtype))
            for l in range(layers):
                plan.append((i, functools.partial(lambda j, l: (l,), l=l), i, functools.partial(lambda s, l: (l, s), l=l)))
    return _exchange(srcs, dsts, plan, name)


def _scatter(grads, name):
    srcs, dsts, plan = [], [], []
    for a, layers in grads:
        if layers is None:
            srcs.append(a)
            dsts.append(_sds(a.shape, a.dtype))
            plan.append((len(srcs) - 1, lambda j: (j,), len(dsts) - 1, lambda s: (s,)))
        else:
            dsts.append(_sds((N_DEV, layers) + a[0].shape[1:], a[0].dtype))
            for l in range(layers):
                srcs.append(a[l])
                plan.append((len(srcs) - 1, lambda j: (j,), len(dsts) - 1, functools.partial(lambda s, l: (s, l), l=l)))
    return _exchange(srcs, dsts, plan, name)


def _adamw(w, g, m, v):
    m = ADAM_B1 * m + (1.0 - ADAM_B1) * g
    v = ADAM_B2 * v + (1.0 - ADAM_B2) * (g * g)
    m_hat = m / (1.0 - ADAM_B1 ** ADAM_STEP)
    v_hat = v / (1.0 - ADAM_B2 ** ADAM_STEP)
    delta = -ADAM_LR * (m_hat / (jnp.sqrt(v_hat) + ADAM_EPS) + ADAM_WD * w)
    return delta, m, v


def _adam_big(w, parts, m, v, name):
    shape = w.shape
    C = shape[-1]
    R = w.size // C
    tr = _tile(R)

    def body(w_ref, p_ref, m_ref, v_ref, g_ref, d_ref, mo_ref, vo_ref):
        g = p_ref[0].astype(F32)
        for s in range(1, N_DEV):
            g = g + p_ref[s].astype(F32)
        g_ref[...] = g
        d_ref[...], mo_ref[...], vo_ref[...] = _adamw(w_ref[...], g, m_ref[...], v_ref[...])

    row = pl.BlockSpec((tr, C), lambda i: (i, 0))
    outs = pl.pallas_call(
        body, grid=(R // tr,), in_specs=[row, pl.BlockSpec((N_DEV, tr, C), lambda i: (0, i, 0)), row, row],
        out_specs=[row] * 4, out_shape=[_sds((R, C), F32)] * 4,
        compiler_params=_params(("arbitrary",)), name=name)(
            w.reshape(R, C), parts.reshape(N_DEV, R, C), m.reshape(R, C), v.reshape(R, C))
    return [o.reshape(shape) for o in outs]


SMALL = (("a_lower_bound", 2, True), ("ln_gain", 6, True), ("ln_bias", 6, True), ("a_norm_gain", 1, False),
         ("kv_b", 4, False), ("b_b_q", 8, False), ("b_sinks", 1, False), ("b_b_out", 8, False), ("ple_b_gate", 16, False))
SMALL_ROWS = 56
PART_ROWS = 152


def _pack_rows(a, rows):
    flat = a.reshape(-1)
    return jnp.pad(flat, (0, rows * 128 - flat.size)).reshape(rows, 128)


def _pack_small(d, prefix=""):
    blocks = [_pack_rows(d[prefix + n], r) for n, r, _ in SMALL]
    used = sum(r for _, r, _ in SMALL)
    return jnp.concatenate(blocks + [jnp.zeros((SMALL_ROWS - used, 128), F32)], axis=0)


def _unpack_small(packed, like):
    out, r0 = {}, 0
    for n, r, _ in SMALL:
        out[n] = packed[r0:r0 + r].reshape(-1)[:like[n].size].reshape(like[n].shape)
        r0 += r
    return out


def _pack_partials(gs):
    blocks = []
    for n, r, sharded in SMALL:
        g = gs[n]
        if sharded:
            blocks.append(g.reshape(r, N_DEV, 128).reshape(r * N_DEV, 128))
        else:
            blocks.append(_pack_rows(g, r))
    used = sum(b.shape[0] for b in blocks)
    return jnp.concatenate(blocks + [jnp.zeros((PART_ROWS - used, 128), F32)], axis=0)


def _adam_small(parts, w, m, v):
    def body(p_ref, w_ref, m_ref, v_ref, g_ref, d_ref, mo_ref, vo_ref, tot):
        me = _my_index()
        t = p_ref[0]
        for s in range(1, N_DEV):
            t = t + p_ref[s]
        tot[...] = t
        g_ref[...] = jnp.zeros_like(g_ref)
        src, dst = 0, 0
        for _, r, sharded in SMALL:
            if sharded:
                for i in range(r):
                    g_ref[pl.ds(dst + i, 1), :] = tot[pl.ds(src + i * N_DEV + me, 1), :]
                src += r * N_DEV
            else:
                g_ref[pl.ds(dst, r), :] = tot[pl.ds(src, r), :]
                src += r
            dst += r
        d_ref[...], mo_ref[...], vo_ref[...] = _adamw(w_ref[...], g_ref[...], m_ref[...], v_ref[...])

    full = pl.BlockSpec((SMALL_ROWS, 128), lambda: (0, 0))
    return pl.pallas_call(
        body, in_specs=[pl.BlockSpec((N_DEV, PART_ROWS, 128), lambda: (0, 0, 0)), full, full, full],
        out_specs=[full] * 4, out_shape=[_sds((SMALL_ROWS, 128), F32)] * 4,
        scratch_shapes=[pltpu.VMEM((PART_ROWS, 128), F32)], name="adam_small")(parts, w, m, v)


BIG = ("a_w_in", "a_w_out", "kv_w", "b_w_q", "b_w_out", "ffn_w_gate_up", "ffn_w_down", "ple_w_up", "ple_w_gate")
WEIGHTS = ("a_w_in", "a_lower_bound", "a_norm_gain", "a_w_out", "kv_w", "kv_b", "b_w_q", "b_b_q", "b_sinks", "b_w_out",
           "b_b_out", "ffn_w_gate_up", "ffn_w_down", "ple_w_up", "ple_w_gate", "ple_b_gate", "ln_gain", "ln_bias")


def _heads_in(w, heads):
    return w.reshape(D, heads, ATT_HD).transpose(1, 0, 2)


def _heads_out(g):
    return g.transpose(1, 0, 2).reshape(D, -1)


def kernel(x, p, a_w_in, a_lower_bound, a_norm_gain, a_w_out, kv_w, kv_b, b_w_q, b_b_q, b_sinks, b_w_out, b_b_out, ffn_w_gate_up, ffn_w_down, ple_w_up, ple_w_gate, ple_b_gate, ln_gain, ln_bias, loss_target, m_a_w_in, m_a_lower_bound, m_a_norm_gain, m_a_w_out, m_kv_w, m_kv_b, m_b_w_q, m_b_b_q, m_b_sinks, m_b_w_out, m_b_b_out, m_ffn_w_gate_up, m_ffn_w_down, m_ple_w_up, m_ple_w_gate, m_ple_b_gate, m_ln_gain, m_ln_bias, v_a_w_in, v_a_lower_bound, v_a_norm_gain, v_a_w_out, v_kv_w, v_kv_b, v_b_w_q, v_b_b_q, v_b_sinks, v_b_w_out, v_b_b_out, v_ffn_w_gate_up, v_ffn_w_down, v_ple_w_up, v_ple_w_gate, v_ple_b_gate, v_ln_gain, v_ln_bias):
    given = dict(locals())
    w = {n: given[n] for n in WEIGHTS}
    m = {n: given["m_" + n] for n in WEIGHTS}
    v = {n: given["v_" + n] for n in WEIGHTS}
    cast = lambda a: a.astype(CDT)

    g_ain, g_aout, g_kv, g_q, g_bo, g_gu, g_dn, g_pu, g_pg = _gather(
        [(cast(a_w_in[0]), None), (cast(a_w_out[0]), None), (cast(kv_w), None), (cast(b_w_q[0]), None),
         (cast(b_w_out[0]), None), (cast(ffn_w_gate_up), 2), (cast(ffn_w_down), 2), (cast(ple_w_up), 2),
         (cast(ple_w_gate), 2)], "gather_weights")
    small_sharded = jnp.concatenate([a_lower_bound, ln_gain.reshape(6, 128), ln_bias.reshape(6, 128),
                                     jnp.zeros((2, 128), F32)], axis=0)
    (g_small,) = _gather([(small_sharded, None)], "gather_small")
    full_rows = lambda r0, r: g_small[:, r0:r0 + r].transpose(1, 0, 2).reshape(r, D)
    W = {
        "a_w_in": g_ain,
        "a_w_out": g_aout.reshape(D, D),
        "kv_w": _heads_in(g_kv.reshape(D, 512), 2 * ATT_KVH),
        "b_w_q": _heads_in(g_q.reshape(D, D), ATT_QH),
        "b_w_out": g_bo.reshape(ATT_QH, ATT_HD, D),
        "ffn_w_gate_up": g_gu.reshape(2, 2, 4, D, FFN_B),
        "ffn_w_down": g_dn.reshape(2, 4, FFN_B, D),
        "ple_w_up": g_pu.transpose(0, 2, 1, 3).reshape(2, PLE_DIM, D),
        "ple_w_gate": g_pg.reshape(2, D, D),
    }
    sm = {"a_lower_bound": full_rows(0, 2), "ln_gain": full_rows(2, 6).reshape(2, 3, D),
          "ln_bias": full_rows(8, 6).reshape(2, 3, D), "a_norm_gain": a_norm_gain, "kv_b": kv_b, "b_b_q": b_b_q[0],
          "b_sinks": b_sinks, "b_b_out": b_b_out, "ple_b_gate": ple_b_gate}

    loss, grad_x, gW, gs = _local_step(x[0], p[:, 0], loss_target[0], W, sm)

    blocks = lambda a: a.reshape((N_DEV, -1) + a.shape[-1:])
    r_ain, r_aout, r_kv, r_q, r_bo, r_gu, r_dn, r_pu, r_pg = _scatter(
        [(gW["a_w_in"], None), (blocks(gW["a_w_out"]), None), (blocks(_heads_out(gW["kv_w"])), None),
         (blocks(_heads_out(gW["b_w_q"])), None), (blocks(gW["b_w_out"].reshape(D, D)), None),
         (gW["ffn_w_gate_up"], 2), ([blocks(a.reshape(FFN_H, D)) for a in gW["ffn_w_down"]], 2),
         ([a.reshape(PLE_DIM, N_DEV, 128).transpose(1, 0, 2) for a in gW["ple_w_up"]], 2),
         ([blocks(a) for a in gW["ple_w_gate"]], 2)], "scatter_grads")
    (parts_small,) = _gather([(_pack_partials(gs), None)], "gather_small_grads")

    out = {}
    for n, parts in (("a_w_in", r_ain), ("a_w_out", r_aout), ("kv_w", r_kv), ("b_w_q", r_q), ("b_w_out", r_bo),
                     ("ffn_w_gate_up", r_gu), ("ffn_w_down", r_dn), ("ple_w_up", r_pu), ("ple_w_gate", r_pg)):
        out[n] = _adam_big(w[n], parts, m[n], v[n], "adam_" + n)
    packed = _adam_small(parts_small, _pack_small(w), _pack_small(m), _pack_small(v))
    small_out = [_unpack_small(a, w) for a in packed]
    for n, _, _ in SMALL:
        out[n] = [s[n] for s in small_out]

    loss = lax.psum(loss[0, 0], ("x", "y", "c"))
    res = [loss, grad_x[None]]
    for i in range(4):
        res += [out[n][i] for n in WEIGHTS]
    return tuple(res)
```

```python
import jax
import jax.numpy as jnp
from jax import lax
from jax.experimental import pallas as pl
from jax.experimental.pallas import tpu as pltpu

F32 = jnp.float32
CDT = jnp.bfloat16

N_DEV = 8
D = 1024
HG_H, HG_DK, HG_CH = 8, 128, 64
ATT_HD, ATT_QH, ATT_KVH, ATT_G, WINDOW = 64, 16, 4, 4, 128
FFN_H = 2816
FFN_B = FFN_H // 4
PLE_DIM = 256
ALPHA = (2.0 * 2) ** 0.25
LN_EPS = 1e-5
RMS_EPS = 1e-6
ADAM_LR, ADAM_B1, ADAM_B2, ADAM_EPS, ADAM_WD, ADAM_STEP = 0.001, 0.9, 0.999, 1e-08, 0.01, 10
ROW_TILES = (256, 128, 64)
VMEM_LIMIT = 48 * 1024 * 1024
NEG = -1e30

MESH = pl.DeviceIdType.MESH


def _tile(n, cands=ROW_TILES):
    for t in cands:
        if n % t == 0:
            return t
    return n


def _sds(shape, dtype):
    return jax.ShapeDtypeStruct(tuple(shape), dtype)


def _params(sem):
    return pltpu.CompilerParams(dimension_semantics=sem, vmem_limit_bytes=VMEM_LIMIT)


def _dot(a, b):
    return jnp.dot(a.astype(CDT), b.astype(CDT), preferred_element_type=F32)


def _dot_nt(a, b):
    return lax.dot_general(a.astype(CDT), b.astype(CDT), (((1,), (1,)), ((), ())), preferred_element_type=F32)


def _dot_tn(a, b):
    return lax.dot_general(a.astype(CDT), b.astype(CDT), (((0,), (0,)), ((), ())), preferred_element_type=F32)


def _sigmoid(x):
    return jax.nn.sigmoid(x)


def _ln_fwd(z, g, b):
    mu = jnp.mean(z, axis=-1, keepdims=True)
    zc = z - mu
    var = jnp.mean(zc * zc, axis=-1, keepdims=True)
    return zc * lax.rsqrt(var + LN_EPS) * g + b


def _ln_bwd(z, g, dy):
    mu = jnp.mean(z, axis=-1, keepdims=True)
    zc = z - mu
    var = jnp.mean(zc * zc, axis=-1, keepdims=True)
    rstd = lax.rsqrt(var + LN_EPS)
    xhat = zc * rstd
    dxh = dy * g
    dz = rstd * (dxh - jnp.mean(dxh, axis=-1, keepdims=True) - xhat * jnp.mean(dxh * xhat, axis=-1, keepdims=True))
    return dz, xhat


def _colsum(x):
    return jnp.sum(x, axis=0, keepdims=True)


def _acc(ref, val, first):
    @pl.when(first)
    def _():
        ref[...] = val

    @pl.when(jnp.logical_not(first))
    def _():
        ref[...] += val


def _call(after, body, **kw):
    if after is None:
        return pl.pallas_call(body, **kw)
    kw["in_specs"] = [pl.BlockSpec(memory_space=pl.ANY)] + list(kw["in_specs"])

    def ordered_body(after_ref, *refs):
        body(*refs)

    call = pl.pallas_call(ordered_body, **kw)
    return lambda *args: call(after, *args)


def _mm_nn(a, b3, out_shape, oblock, omap, out_dtype, bias3=None, name="mm_nn"):
    M, K = a.shape
    G, _, Nb = b3.shape
    tm = _tile(M)

    def body(a_ref, b_ref, *rest):
        o_ref = rest[-1]
        acc = _dot(a_ref[...], b_ref[...])
        if bias3 is not None:
            acc = acc + rest[0][...]
        o_ref[...] = acc.astype(o_ref.dtype)

    in_specs = [pl.BlockSpec((tm, K), lambda g, i: (i, 0)), pl.BlockSpec((None, K, Nb), lambda g, i: (g, 0, 0))]
    args = [a, b3]
    if bias3 is not None:
        in_specs.append(pl.BlockSpec((None, 1, Nb), lambda g, i: (g, 0, 0)))
        args.append(bias3)
    return pl.pallas_call(
        body, grid=(G, M // tm), in_specs=in_specs, out_specs=pl.BlockSpec(oblock, omap),
        out_shape=_sds(out_shape, out_dtype), compiler_params=_params(("arbitrary", "arbitrary")), name=name)(*args)


def _mm_tn(a3, b3, G, amap, bmap, ablock, bblock, out_shape, oblock, omap, name="mm_tn"):
    S = a3.shape[1]
    tk = _tile(S, (512, 256, 128))
    Mo, No = oblock[-2], oblock[-1]

    def body(a_ref, b_ref, o_ref, acc):
        k = pl.program_id(1)
        _acc(acc, _dot_tn(a_ref[...], b_ref[...]), k == 0)

        @pl.when(k == pl.num_programs(1) - 1)
        def _():
            o_ref[...] = acc[...].astype(o_ref.dtype)

    return pl.pallas_call(
        body, grid=(G, S // tk),
        in_specs=[pl.BlockSpec(ablock(tk), amap), pl.BlockSpec(bblock(tk), bmap)],
        out_specs=pl.BlockSpec(oblock, omap), out_shape=_sds(out_shape, CDT),
        scratch_shapes=[pltpu.VMEM((Mo, No), F32)],
        compiler_params=_params(("arbitrary", "arbitrary")), name=name)(a3, b3)


def _wgrad(a3, b3, name):
    Ga, S, M = a3.shape
    Gb, _, N = b3.shape
    G = max(Ga, Gb)
    return _mm_tn(
        a3, b3, G,
        (lambda g, k: (g, k, 0)) if Ga > 1 else (lambda g, k: (0, k, 0)),
        (lambda g, k: (g, k, 0)) if Gb > 1 else (lambda g, k: (0, k, 0)),
        lambda tk: (None, tk, M), lambda tk: (None, tk, N),
        (G, M, N), (None, M, N), lambda g, k: (g, 0, 0), name=name)


def _mixout_ln(u3, w3, bias, xin, gain, beta, name):
    G, S, Kb = u3.shape
    tm = _tile(S)

    def body(u_ref, w_ref, b_ref, x_ref, g_ref, be_ref, z_ref, xo_ref, xob_ref):
        h = b_ref[...] + _dot(u_ref[0], w_ref[0])
        for g in range(1, G):
            h = h + _dot(u_ref[g], w_ref[g])
        z = ALPHA * x_ref[...] + h
        z_ref[...] = z
        y = _ln_fwd(z, g_ref[...], be_ref[...])
        xo_ref[...] = y
        xob_ref[...] = y.astype(CDT)

    row = pl.BlockSpec((tm, D), lambda i: (i, 0))
    vec = pl.BlockSpec((1, D), lambda i: (0, 0))
    return pl.pallas_call(
        body, grid=(S // tm,),
        in_specs=[pl.BlockSpec((G, tm, Kb), lambda i: (0, i, 0)), pl.BlockSpec((G, Kb, D), lambda i: (0, 0, 0)),
                  vec, row, vec, vec],
        out_specs=[row, row, row], out_shape=[_sds((S, D), F32), _sds((S, D), F32), _sds((S, D), CDT)],
        compiler_params=_params(("arbitrary",)), name=name)(u3, w3, bias, xin, gain, beta)


def _ffn_fwd(xin, xin_b, wgu, wdn, gain, beta, name):
    S = xin.shape[0]
    tm = _tile(S)

    def body(x_ref, xb_ref, wgu_ref, wdn_ref, g_ref, be_ref, gu_ref, hid_ref, z_ref, xo_ref, xob_ref, acc):
        j = pl.program_id(1)
        xb = xb_ref[...]
        gate = _dot(xb, wgu_ref[0])
        up = _dot(xb, wgu_ref[1])
        gu_ref[0] = gate
        gu_ref[1] = up
        hid = (gate * _sigmoid(gate) * up).astype(CDT)
        hid_ref[...] = hid
        _acc(acc, _dot(hid, wdn_ref[...]), j == 0)

        @pl.when(j == 3)
        def _():
            z = ALPHA * x_ref[...] + acc[...]
            z_ref[...] = z
            y = _ln_fwd(z, g_ref[...], be_ref[...])
            xo_ref[...] = y
            xob_ref[...] = y.astype(CDT)

    row = pl.BlockSpec((tm, D), lambda i, j: (i, 0))
    vec = pl.BlockSpec((1, D), lambda i, j: (0, 0))
    return pl.pallas_call(
        body, grid=(S // tm, 4),
        in_specs=[row, row, pl.BlockSpec((2, None, D, FFN_B), lambda i, j: (0, j, 0, 0)),
                  pl.BlockSpec((None, FFN_B, D), lambda i, j: (j, 0, 0)), vec, vec],
        out_specs=[pl.BlockSpec((2, None, tm, FFN_B), lambda i, j: (0, j, i, 0)),
                   pl.BlockSpec((None, tm, FFN_B), lambda i, j: (j, i, 0)), row, row, row],
        out_shape=[_sds((2, 4, S, FFN_B), F32), _sds((4, S, FFN_B), CDT), _sds((S, D), F32), _sds((S, D), F32),
                   _sds((S, D), CDT)],
        scratch_shapes=[pltpu.VMEM((tm, D), F32)],
        compiler_params=_params(("arbitrary", "arbitrary")), name=name)(xin, xin_b, wgu, wdn, gain, beta)


def _ple_fwd(xin, xin_b, p_b, wpg, bgate, wpu, gain, beta, name):
    S = xin.shape[0]
    tm = _tile(S)

    def body(x_ref, xb_ref, p_ref, wpg_ref, bg_ref, wpu_ref, g_ref, be_ref, sg_ref, up_ref, z_ref, xo_ref, xob_ref):
        sg = _sigmoid(_dot(xb_ref[...], wpg_ref[...]) + bg_ref[...])
        up = _dot(p_ref[...], wpu_ref[...])
        sg_ref[...] = sg
        up_ref[...] = up
        z = ALPHA * x_ref[...] + sg * up
        z_ref[...] = z
        y = _ln_fwd(z, g_ref[...], be_ref[...])
        xo_ref[...] = y
        xob_ref[...] = y.astype(CDT)

    row = pl.BlockSpec((tm, D), lambda i: (i, 0))
    vec = pl.BlockSpec((1, D), lambda i: (0, 0))
    return pl.pallas_call(
        body, grid=(S // tm,),
        in_specs=[row, row, pl.BlockSpec((tm, PLE_DIM), lambda i: (i, 0)), pl.BlockSpec((D, D), lambda i: (0, 0)), vec,
                  pl.BlockSpec((PLE_DIM, D), lambda i: (0, 0)), vec, vec],
        out_specs=[row] * 5,
        out_shape=[_sds((S, D), F32)] * 4 + [_sds((S, D), CDT)],
        compiler_params=_params(("arbitrary",)), name=name)(xin, xin_b, p_b, wpg, bgate, wpu, gain, beta)


def _loss_fwd_bwd(y, target):
    S = y.shape[0]
    tm = _tile(S)

    def body(y_ref, t_ref, l_ref, dy_ref):
        e = y_ref[...] - t_ref[...]
        dy_ref[...] = e * (1.0 / D)
        part = 0.5 * jnp.sum(jnp.sum(e * e, axis=-1, keepdims=True) * (1.0 / D), axis=0, keepdims=True)
        _acc(l_ref, part, pl.program_id(0) == 0)

    row = pl.BlockSpec((tm, D), lambda i: (i, 0))
    return pl.pallas_call(
        body, grid=(S // tm,), in_specs=[row, row],
        out_specs=[pl.BlockSpec((1, 1), lambda i: (0, 0)), row],
        out_shape=[_sds((1, 1), F32), _sds((S, D), F32)],
        compiler_params=_params(("arbitrary",)), name="loss")(y, target)


def _ple_bwd(dy, z, sg, up, gain, wpg, name, after=None):
    S = dy.shape[0]
    tm = _tile(S)

    def body(dy_ref, z_ref, sg_ref, up_ref, g_ref, wpg_ref, dx_ref, dgl_ref, dup_ref, dgain_ref, dbeta_ref, dbg_ref):
        first = pl.program_id(0) == 0
        dy_ = dy_ref[...]
        dz, xhat = _ln_bwd(z_ref[...], g_ref[...], dy_)
        sg_ = sg_ref[...]
        dgl = dz * up_ref[...] * sg_ * (1.0 - sg_)
        dgl_ref[...] = dgl.astype(CDT)
        dup_ref[...] = (dz * sg_).astype(CDT)
        dx_ref[...] = ALPHA * dz + _dot_nt(dgl, wpg_ref[...])
        _acc(dgain_ref, _colsum(dy_ * xhat), first)
        _acc(dbeta_ref, _colsum(dy_), first)
        _acc(dbg_ref, _colsum(dgl), first)

    row = pl.BlockSpec((tm, D), lambda i: (i, 0))
    vec = pl.BlockSpec((1, D), lambda i: (0, 0))
    return _call(
        after, body, grid=(S // tm,), in_specs=[row, row, row, row, vec, pl.BlockSpec((D, D), lambda i: (0, 0))],
        out_specs=[row, row, row, vec, vec, vec],
        out_shape=[_sds((S, D), F32), _sds((S, D), CDT), _sds((S, D), CDT)] + [_sds((1, D), F32)] * 3,
        compiler_params=_params(("arbitrary",)), name=name)(dy, z, sg, up, gain, wpg)


def _ffn_bwd(dy, z, gu, wgu, wdn, gain, name, after=None):
    S = dy.shape[0]
    tm = _tile(S)

    def body(dy_ref, z_ref, gu_ref, wgu_ref, wdn_ref, g_ref, dx_ref, dzb_ref, dgu_ref, dgain_ref, dbeta_ref,
             dz_scr, acc):
        i, j = pl.program_id(0), pl.program_id(1)

        @pl.when(j == 0)
        def _():
            dy_ = dy_ref[...]
            dz, xhat = _ln_bwd(z_ref[...], g_ref[...], dy_)
            dz_scr[...] = dz
            dzb_ref[...] = dz.astype(CDT)
            _acc(dgain_ref, _colsum(dy_ * xhat), i == 0)
            _acc(dbeta_ref, _colsum(dy_), i == 0)

        dhid = _dot_nt(dz_scr[...], wdn_ref[...])
        gate, up = gu_ref[0], gu_ref[1]
        sg = _sigmoid(gate)
        dgate = (dhid * up * (sg * (1.0 + gate * (1.0 - sg)))).astype(CDT)
        dup = (dhid * (gate * sg)).astype(CDT)
        dgu_ref[0] = dgate
        dgu_ref[1] = dup
        _acc(acc, _dot_nt(dgate, wgu_ref[0]) + _dot_nt(dup, wgu_ref[1]), j == 0)

        @pl.when(j == 3)
        def _():
            dx_ref[...] = ALPHA * dz_scr[...] + acc[...]

    row = pl.BlockSpec((tm, D), lambda i, j: (i, 0))
    vec = pl.BlockSpec((1, D), lambda i, j: (0, 0))
    return _call(
        after, body, grid=(S // tm, 4),
        in_specs=[row, row, pl.BlockSpec((2, None, tm, FFN_B), lambda i, j: (0, j, i, 0)),
                  pl.BlockSpec((2, None, D, FFN_B), lambda i, j: (0, j, 0, 0)),
                  pl.BlockSpec((None, FFN_B, D), lambda i, j: (j, 0, 0)), vec],
        out_specs=[row, row, pl.BlockSpec((2, None, tm, FFN_B), lambda i, j: (0, j, i, 0)), vec, vec],
        out_shape=[_sds((S, D), F32), _sds((S, D), CDT), _sds((2, 4, S, FFN_B), CDT), _sds((1, D), F32),
                   _sds((1, D), F32)],
        scratch_shapes=[pltpu.VMEM((tm, D), F32), pltpu.VMEM((tm, D), F32)],
        compiler_params=_params(("arbitrary", "arbitrary")), name=name)(dy, z, gu, wgu, wdn, gain)


def _mixout_bwd(dy, z, gain, w3, du_dtype, name, after=None):
    S = dy.shape[0]
    G, Kb, _ = w3.shape
    tm = _tile(S)

    def body(dy_ref, z_ref, g_ref, w_ref, dz_ref, dzb_ref, du_ref, dgain_ref, dbeta_ref, dbias_ref):
        first = pl.program_id(0) == 0
        dy_ = dy_ref[...]
        dz, xhat = _ln_bwd(z_ref[...], g_ref[...], dy_)
        dz_ref[...] = dz
        dzb = dz.astype(CDT)
        dzb_ref[...] = dzb
        for g in range(G):
            du_ref[g] = _dot_nt(dzb, w_ref[g]).astype(du_ref.dtype)
        _acc(dgain_ref, _colsum(dy_ * xhat), first)
        _acc(dbeta_ref, _colsum(dy_), first)
        _acc(dbias_ref, _colsum(dz), first)

    row = pl.BlockSpec((tm, D), lambda i: (i, 0))
    vec = pl.BlockSpec((1, D), lambda i: (0, 0))
    return _call(
        after, body, grid=(S // tm,), in_specs=[row, row, vec, pl.BlockSpec((G, Kb, D), lambda i: (0, 0, 0))],
        out_specs=[row, row, pl.BlockSpec((G, tm, Kb), lambda i: (0, i, 0)), vec, vec, vec],
        out_shape=[_sds((S, D), F32), _sds((S, D), CDT), _sds((G, S, Kb), du_dtype)] + [_sds((1, D), F32)] * 3,
        compiler_params=_params(("arbitrary",)), name=name)(dy, z, gain, w3)


def _qkv_bwd(dz, dq4, dkv4, wq4, wkv4, name, after=None):
    S = dz.shape[0]
    tm = _tile(S)
    HQ, HK = dq4.shape[0], dkv4.shape[0]

    def body(dz_ref, dq_ref, dkv_ref, wq_ref, wkv_ref, dx_ref, dkvb_ref):
        first = pl.program_id(0) == 0
        acc = ALPHA * dz_ref[...]
        for h in range(HQ):
            acc = acc + _dot_nt(dq_ref[h], wq_ref[h])
        for h in range(HK):
            acc = acc + _dot_nt(dkv_ref[h], wkv_ref[h])
        dx_ref[...] = acc
        for h in range(HK):
            _acc(dkvb_ref.at[h], _colsum(dkv_ref[h]), first)

    row = pl.BlockSpec((tm, D), lambda i: (i, 0))
    return _call(
        after, body, grid=(S // tm,),
        in_specs=[row, pl.BlockSpec((HQ, tm, ATT_HD), lambda i: (0, i, 0)), pl.BlockSpec((HK, tm, ATT_HD), lambda i: (0, i, 0)),
                  pl.BlockSpec((HQ, D, ATT_HD), lambda i: (0, 0, 0)), pl.BlockSpec((HK, D, ATT_HD), lambda i: (0, 0, 0))],
        out_specs=[row, pl.BlockSpec((HK, 1, ATT_HD), lambda i: (0, 0, 0))],
        out_shape=[_sds((S, D), F32), _sds((HK, 1, ATT_HD), F32)],
        compiler_params=_params(("arbitrary",)), name=name)(dz, dq4, dkv4, wq4, wkv4)


def _inproj_bwd(dz, dproj, wain, name, after=None):
    S = dz.shape[0]
    tm = _tile(S)
    nb = wain.shape[-1]

    def body(dz_ref, dp_ref, w_ref, dx_ref, acc):
        j = pl.program_id(1)
        _acc(acc, _dot_nt(dp_ref[...], w_ref[...]), j == 0)

        @pl.when(j == N_DEV - 1)
        def _():
            dx_ref[...] = ALPHA * dz_ref[...] + acc[...]

    row = pl.BlockSpec((tm, D), lambda i, j: (i, 0))
    return _call(
        after, body, grid=(S // tm, N_DEV),
        in_specs=[row, pl.BlockSpec((None, tm, nb), lambda i, j: (j // 2, i, j % 2)),
                  pl.BlockSpec((None, D, nb), lambda i, j: (j, 0, 0))],
        out_specs=row, out_shape=_sds((S, D), F32), scratch_shapes=[pltpu.VMEM((tm, D), F32)],
        compiler_params=_params(("arbitrary", "arbitrary")), name=name)(dz, dproj, wain)


def _hp(a, b, dims):
    ah = a.astype(CDT)
    al = (a - ah.astype(F32)).astype(CDT)
    bh = b.astype(CDT)
    bl = (b - bh.astype(F32)).astype(CDT)
    dg = lambda u, w: lax.dot_general(u, w, (dims, ((), ())), preferred_element_type=F32)
    return dg(ah, bh) + (dg(ah, bl) + dg(al, bh))


def _hdot(a, b):
    return _hp(a, b, ((1,), (0,)))


def _hdot_nt(a, b):
    return _hp(a, b, ((1,), (1,)))


def _hdot_tn(a, b):
    return _hp(a, b, ((0,), (0,)))


def _tri_dot(tri, x):
    hi = x.astype(CDT)
    r1 = x - hi.astype(F32)
    mid = r1.astype(CDT)
    lo = (r1 - mid.astype(F32)).astype(CDT)
    t = tri.astype(CDT)
    return (jnp.dot(t, hi, preferred_element_type=F32) + jnp.dot(t, mid, preferred_element_type=F32)
            + jnp.dot(t, lo, preferred_element_type=F32))


def _hg_gates(q, f, alb_ref):
    a0, a1 = alb_ref[0:1, :], alb_ref[1:2, :]
    mx = jnp.maximum(a0, a1)
    e0, e1 = jnp.exp(a0 - mx), jnp.exp(a1 - mx)
    lb = e0 / (e0 + e1)
    sig = _sigmoid(f)
    forget = lb + (1.0 - lb) * sig
    k = (1.0 - lb) * _sigmoid(-f)
    qs = q * _sigmoid(q) * (HG_DK ** -0.5)
    return qs, k, jnp.log(forget), sig, lb, forget


def _hg_intra(qs, k, b, b_scr):
    b_scr[...] = b
    bm = b_scr[pl.ds(HG_CH // 2 - 1, 1), :]
    bl = b_scr[pl.ds(HG_CH - 1, 1), :]
    eb = jnp.exp(b)
    qb = qs * eb
    e_q = jnp.exp(b - bm)
    e_k = jnp.exp(bm - b)
    e_d = jnp.exp(bl - b)
    return qb, qs * e_q, k * e_k, k * e_d, jnp.exp(bl), eb, e_q, e_k, e_d


def _hgrn_fwd(proj, alb, ngain):
    S = proj.shape[1]
    nc = S // HG_CH

    def body(pj_ref, alb_ref, ng_ref, o_ref, y_ref, st_ref, st_scr, b_scr):
        n = pl.program_id(1)

        @pl.when(n == 0)
        def _():
            st_scr[...] = jnp.zeros_like(st_scr)

        q, f, v, g = pj_ref[0], pj_ref[1], pj_ref[2], pj_ref[3]
        qs, k, logf, _, _, _ = _hg_gates(q, f, alb_ref)
        r = lax.broadcasted_iota(jnp.int32, (HG_CH, HG_CH), 0)
        c = lax.broadcasted_iota(jnp.int32, (HG_CH, HG_CH), 1)
        causal = r >= c
        b = _tri_dot(causal.astype(F32), logf)
        qb, qt, kt, kd, ebl, _, _, _, _ = _hg_intra(qs, k, b, b_scr)
        st = st_scr[...]
        st_ref[...] = st
        a = jnp.where(causal, _hdot_nt(qt, kt), 0.0)
        o = _hdot(a, v) + _hdot_nt(qb, st)
        st_scr[...] = st * ebl + _hdot_tn(v, kd)
        o_ref[...] = o
        rinv = lax.rsqrt(jnp.mean(o * o, axis=-1, keepdims=True) + RMS_EPS)
        y_ref[...] = (o * rinv * ng_ref[...] * (g * _sigmoid(g))).astype(CDT)

    blk = pl.BlockSpec((HG_CH, HG_DK), lambda h, n: (n, h))
    return pl.pallas_call(
        body, grid=(HG_H, nc),
        in_specs=[pl.BlockSpec((4, HG_CH, HG_DK), lambda h, n: (0, n, h)), pl.BlockSpec((2, HG_DK), lambda h, n: (0, h)),
                  pl.BlockSpec((1, HG_DK), lambda h, n: (0, 0))],
        out_specs=[blk, blk, pl.BlockSpec((None, None, HG_DK, HG_DK), lambda h, n: (h, n, 0, 0))],
        out_shape=[_sds((S, D), F32), _sds((S, D), CDT), _sds((HG_H, nc, HG_DK, HG_DK), F32)],
        scratch_shapes=[pltpu.VMEM((HG_DK, HG_DK), F32), pltpu.VMEM((HG_CH, HG_DK), F32)],
        compiler_params=_params(("arbitrary", "arbitrary")), name="hgrn_fwd")(proj, alb, ngain)


def _hgrn_bwd(proj, alb, ngain, o, states, dy):
    S = proj.shape[1]
    nc = S // HG_CH

    def body(pj_ref, alb_ref, ng_ref, o_ref, st_ref, dy_ref, dpj_ref, dalb_ref, dng_ref, dst_scr, b_scr):
        h, n = pl.program_id(0), pl.program_id(1)

        @pl.when(n == 0)
        def _():
            dst_scr[...] = jnp.zeros_like(dst_scr)

        q, f, v, g = pj_ref[0], pj_ref[1], pj_ref[2], pj_ref[3]
        o_ = o_ref[...]
        dy_ = dy_ref[...]
        ng = ng_ref[...]
        sg = _sigmoid(g)
        rinv = lax.rsqrt(jnp.mean(o_ * o_, axis=-1, keepdims=True) + RMS_EPS)
        nrm = o_ * rinv
        dr = dy_ * (g * sg)
        dg = dy_ * nrm * ng * (sg * (1.0 + g * (1.0 - sg)))
        dn = dr * ng
        do = rinv * (dn - nrm * jnp.mean(dn * nrm, axis=-1, keepdims=True))
        _acc(dng_ref, _colsum(dr * nrm), jnp.logical_and(h == 0, n == 0))
        qs, k, logf, sig, lb, forget = _hg_gates(q, f, alb_ref)
        r = lax.broadcasted_iota(jnp.int32, (HG_CH, HG_CH), 0)
        c = lax.broadcasted_iota(jnp.int32, (HG_CH, HG_CH), 1)
        causal = r >= c
        b = _tri_dot(causal.astype(F32), logf)
        qb, qt, kt, kd, ebl, eb, e_q, e_k, e_d = _hg_intra(qs, k, b, b_scr)
        st = st_ref[...]
        dstn = dst_scr[...]
        a = jnp.where(causal, _hdot_nt(qt, kt), 0.0)
        da = jnp.where(causal, _hdot_nt(do, v), 0.0)
        dv = _hdot_tn(a, do) + _hdot_nt(kd, dstn)
        dqb = _hdot(do, st)
        dkd = _hdot(v, dstn)
        dqt = _hdot(da, kt)
        dkt = _hdot_tn(da, qt)
        dbl = _colsum(dkd * kd) + ebl * _colsum(dstn * st)
        dst_scr[...] = dstn * ebl + _hdot_tn(do, qb)
        dqs = dqt * e_q + dqb * eb
        dk = dkt * e_k + dkd * e_d
        db = dqt * qt + dqb * qb - dkt * kt - dkd * kd
        dlogf = _tri_dot((r <= c).astype(F32), db) + dbl
        dforget = dlogf / forget
        dsig = (1.0 - lb) * (dforget - dk)
        df = dsig * sig * (1.0 - sig)
        dlb = _colsum((dforget - dk) * (1.0 - sig))
        sq = _sigmoid(q)
        dq = dqs * (HG_DK ** -0.5) * (sq * (1.0 + q * (1.0 - sq)))
        dpj_ref[0] = dq.astype(CDT)
        dpj_ref[1] = df.astype(CDT)
        dpj_ref[2] = dv.astype(CDT)
        dpj_ref[3] = dg.astype(CDT)
        da0 = dlb * lb * (1.0 - lb)
        _acc(dalb_ref.at[pl.ds(0, 1)], da0, n == 0)
        _acc(dalb_ref.at[pl.ds(1, 1)], -da0, n == 0)

    rev = lambda h, n: (nc - 1 - n, h)
    blk = pl.BlockSpec((HG_CH, HG_DK), rev)
    return pl.pallas_call(
        body, grid=(HG_H, nc),
        in_specs=[pl.BlockSpec((4, HG_CH, HG_DK), lambda h, n: (0, nc - 1 - n, h)),
                  pl.BlockSpec((2, HG_DK), lambda h, n: (0, h)), pl.BlockSpec((1, HG_DK), lambda h, n: (0, 0)), blk,
                  pl.BlockSpec((None, None, HG_DK, HG_DK), lambda h, n: (h, nc - 1 - n, 0, 0)), blk],
        out_specs=[pl.BlockSpec((4, HG_CH, HG_DK), lambda h, n: (0, nc - 1 - n, h)),
                   pl.BlockSpec((2, HG_DK), lambda h, n: (0, h)), pl.BlockSpec((1, HG_DK), lambda h, n: (0, 0))],
        out_shape=[_sds((4, S, D), CDT), _sds((2, D), F32), _sds((1, HG_DK), F32)],
        scratch_shapes=[pltpu.VMEM((HG_DK, HG_DK), F32), pltpu.VMEM((HG_CH, HG_DK), F32)],
        compiler_params=_params(("arbitrary", "arbitrary")), name="hgrn_bwd")(proj, alb, ngain, o, states, dy)


def _slope(h):
    return 2.0 ** (-8.0 * (h + 1) / ATT_QH)


def _attn_mask(n):
    qi = lax.broadcasted_iota(jnp.int32, (WINDOW, 2 * WINDOW), 0)
    si = lax.broadcasted_iota(jnp.int32, (WINDOW, 2 * WINDOW), 1)
    dist = qi - si + WINDOW
    valid = (dist >= 0) & (dist < WINDOW) & (n * WINDOW - WINDOW + si >= 0)
    return valid, dist.astype(F32)


def _attn_probs(qh, kh, sink, slope, valid, distf):
    s = _dot_nt(qh, kh) * (ATT_HD ** -0.5) - slope * distf
    s = jnp.where(valid, s, NEG)
    m = jnp.maximum(jnp.max(s, axis=-1, keepdims=True), sink)
    e = jnp.exp(s - m)
    es = jnp.exp(sink - m)
    inv = 1.0 / (jnp.sum(e, axis=-1, keepdims=True) + es)
    return e * inv, es * inv


def _attn_specs(S):
    nb = S // WINDOW
    cur = lambda H: pl.BlockSpec((H, WINDOW, ATT_HD), lambda n: (0, n, 0))
    prev = lambda H: pl.BlockSpec((H, WINDOW, ATT_HD), lambda n: (0, jnp.maximum(n - 1, 0), 0))
    return nb, cur, prev


def _attn_fwd(q4, kv4, sinks):
    S = q4.shape[1]
    nb, cur, prev = _attn_specs(S)

    def body(sink_ref, q_ref, kvc_ref, kvp_ref, o_ref):
        valid, distf = _attn_mask(pl.program_id(0))
        for h in range(ATT_QH):
            kvh = h // ATT_G
            kh = jnp.concatenate([kvp_ref[kvh], kvc_ref[kvh]], axis=0)
            vh = jnp.concatenate([kvp_ref[ATT_KVH + kvh], kvc_ref[ATT_KVH + kvh]], axis=0)
            p, _ = _attn_probs(q_ref[h], kh, sink_ref[0, h], _slope(h), valid, distf)
            o_ref[h] = _dot(p, vh).astype(CDT)

    return pl.pallas_call(
        body, grid=(nb,),
        in_specs=[pl.BlockSpec(memory_space=pltpu.SMEM), cur(ATT_QH), cur(2 * ATT_KVH), prev(2 * ATT_KVH)],
        out_specs=cur(ATT_QH), out_shape=_sds((ATT_QH, S, ATT_HD), CDT),
        compiler_params=_params(("arbitrary",)), name="attn_fwd")(sinks, q4, kv4, kv4)


def _attn_bwd(q4, kv4, sinks, do4):
    S = q4.shape[1]
    nb, cur, prev = _attn_specs(S)

    def body(sink_ref, q_ref, kvc_ref, kvp_ref, do_ref, dq_ref, dkv_ref, dbq_ref, dsink_ref):
        n = pl.program_id(0)
        first = n == 0

        @pl.when(first)
        def _():
            dkv_ref[...] = jnp.zeros_like(dkv_ref)
            dsink_ref[...] = jnp.zeros_like(dsink_ref)

        valid, distf = _attn_mask(n)
        lane = lax.broadcasted_iota(jnp.int32, (1, 128), 1)
        rows_cur = pl.ds(pl.multiple_of(n * WINDOW, WINDOW), WINDOW)
        rows_prev = pl.ds(pl.multiple_of(jnp.maximum(n - 1, 0) * WINDOW, WINDOW), WINDOW)
        for h in range(ATT_QH):
            kvh = h // ATT_G
            qh = q_ref[h]
            doh = do_ref[h]
            kh = jnp.concatenate([kvp_ref[kvh], kvc_ref[kvh]], axis=0)
            vh = jnp.concatenate([kvp_ref[ATT_KVH + kvh], kvc_ref[ATT_KVH + kvh]], axis=0)
            p, ps = _attn_probs(qh, kh, sink_ref[0, h], _slope(h), valid, distf)
            dp = _dot_nt(doh, vh)
            dd = jnp.sum(p * dp, axis=-1, keepdims=True)
            ds = p * (dp - dd)
            dsink = -jnp.sum(ps * dd, axis=0, keepdims=True)
            dsink_ref[...] += jnp.where(lane == h, dsink, 0.0)
            dqh = _dot(ds, kh) * (ATT_HD ** -0.5)
            dq_ref[h] = dqh.astype(CDT)
            _acc(dbq_ref.at[h], _colsum(dqh), first)
            dkh = _dot_tn(ds, qh) * (ATT_HD ** -0.5)
            dvh = _dot_tn(p, doh)
            dkv_ref[kvh, rows_prev, :] += dkh[:WINDOW]
            dkv_ref[kvh, rows_cur, :] += dkh[WINDOW:]
            dkv_ref[ATT_KVH + kvh, rows_prev, :] += dvh[:WINDOW]
            dkv_ref[ATT_KVH + kvh, rows_cur, :] += dvh[WINDOW:]

    return pl.pallas_call(
        body, grid=(nb,),
        in_specs=[pl.BlockSpec(memory_space=pltpu.SMEM), cur(ATT_QH), cur(2 * ATT_KVH), prev(2 * ATT_KVH), cur(ATT_QH)],
        out_specs=[cur(ATT_QH), pl.BlockSpec((2 * ATT_KVH, S, ATT_HD), lambda n: (0, 0, 0)),
                   pl.BlockSpec((ATT_QH, 1, ATT_HD), lambda n: (0, 0, 0)), pl.BlockSpec((1, 128), lambda n: (0, 0))],
        out_shape=[_sds((ATT_QH, S, ATT_HD), CDT), _sds((2 * ATT_KVH, S, ATT_HD), F32), _sds((ATT_QH, 1, ATT_HD), F32),
                   _sds((1, 128), F32)],
        compiler_params=_params(("arbitrary",)), name="attn_bwd")(sinks, q4, kv4, kv4, do4)


def _local_step(x, p, target, getw, sm, emit):
    S = x.shape[0]
    vec = lambda a: a.reshape(1, -1)
    ln_g = lambda l, k: vec(sm["ln_gain"][l, k])
    ln_b = lambda l, k: vec(sm["ln_bias"][l, k])
    xb = x.astype(CDT)
    pb = p.astype(CDT)

    proj = _mm_nn(xb, getw("a_w_in", None), (4, S, D), (None, _tile(S), 512), lambda g, i: (g // 2, i, g % 2), F32,
                  name="a_in")
    o_a, y_a, states = _hgrn_fwd(proj, sm["a_lower_bound"], sm["a_norm_gain"])
    zeros = jnp.zeros((1, D), F32)
    z = [[None] * 3 for _ in range(2)]
    xs = [[None] * 3 for _ in range(2)]
    xbs = [[None] * 3 for _ in range(2)]
    z[0][0], xs[0][0], xbs[0][0] = _mixout_ln(y_a[None], getw("a_w_out", y_a)[None], zeros, x, ln_g(0, 0), ln_b(0, 0),
                                              "a_out_ln")
    gu, hid, sgs, ups = [None, None], [None, None], [None, None], [None, None]

    def ffn_ple(l):
        wgu = getw(f"gu{l}", xbs[l][0])
        gu[l], hid[l], z[l][1], xs[l][1], xbs[l][1] = _ffn_fwd(
            xs[l][0], xbs[l][0], wgu, getw(f"dn{l}", None), ln_g(l, 1), ln_b(l, 1), f"ffn_fwd{l}")
        sgs[l], ups[l], z[l][2], xs[l][2], xbs[l][2] = _ple_fwd(
            xs[l][1], xbs[l][1], pb[l], getw(f"pg{l}", None), vec(sm["ple_b_gate"][l]), getw(f"pu{l}", None), ln_g(l, 2),
            ln_b(l, 2), f"ple_fwd{l}")

    ffn_ple(0)
    x3, x3b = xs[0][2], xbs[0][2]
    w_kv, w_q, w_bo = getw("kv_w", x3b), getw("b_w_q", None), getw("b_w_out", None)
    kv4 = _mm_nn(x3b, w_kv, (2 * ATT_KVH, S, ATT_HD), (None, _tile(S), ATT_HD), lambda g, i: (g, i, 0), CDT,
                 bias3=sm["kv_b"].reshape(2 * ATT_KVH, 1, ATT_HD), name="kv_proj")
    q4 = _mm_nn(x3b, w_q, (ATT_QH, S, ATT_HD), (None, _tile(S), ATT_HD), lambda g, i: (g, i, 0), CDT,
                bias3=sm["b_b_q"].reshape(ATT_QH, 1, ATT_HD), name="q_proj")
    o4 = _attn_fwd(q4, kv4, sm["b_sinks"])
    z[1][0], xs[1][0], xbs[1][0] = _mixout_ln(o4, w_bo, sm["b_b_out"], x3, ln_g(1, 0), ln_b(1, 0), "b_out_ln")
    ffn_ple(1)
    loss, dy = _loss_fwd_bwd(xs[1][2], target)

    gs = {}
    d_ln_g = [[None] * 3 for _ in range(2)]
    d_ln_b = [[None] * 3 for _ in range(2)]
    g_bg = [None, None]

    def ffn_ple_bwd(l, dy):
        dx2, dgl, dup, d_ln_g[l][2], d_ln_b[l][2], g_bg[l] = _ple_bwd(dy, z[l][2], sgs[l], ups[l], ln_g(l, 2),
                                                                     getw(f"pg{l}", None), f"ple_bwd{l}")
        g_pg = _wgrad(xbs[l][1][None], dgl[None], f"g_ple_gate{l}")[0]
        g_pu = _wgrad(pb[l][None], dup[None], f"g_ple_up{l}")[0]
        dx1, dzb, dgu, d_ln_g[l][1], d_ln_b[l][1] = _ffn_bwd(dx2, z[l][1], gu[l], getw(f"gu{l}", None),
                                                           getw(f"dn{l}", None), ln_g(l, 1), f"ffn_bwd{l}")
        g_dn = _wgrad(hid[l], dzb[None], f"g_ffn_down{l}")
        g_gu = _wgrad(xbs[l][0][None], dgu.reshape(8, S, FFN_B), f"g_ffn_gate_up{l}")
        return dx1, emit({f"pg{l}": g_pg, f"pu{l}": g_pu, f"dn{l}": g_dn, f"gu{l}": g_gu})

    dx1, tok = ffn_ple_bwd(1, dy)
    dz, dzb, do4, d_ln_g[1][0], d_ln_b[1][0], gs["b_b_out"] = _mixout_bwd(dx1, z[1][0], ln_g(1, 0), w_bo, CDT,
                                                                         "b_out_bwd", after=tok)
    g_bo = _wgrad(o4, dzb[None], "g_b_w_out")
    dq4, dkv4, dbq, dsinks = _attn_bwd(q4, kv4, sm["b_sinks"], do4)
    gs["b_b_q"] = dbq
    gs["b_sinks"] = dsinks
    g_q = _wgrad(x3b[None], dq4, "g_b_w_q")
    g_kv = _wgrad(x3b[None], dkv4.astype(CDT), "g_kv_w")
    tok = emit({"b_w_out": g_bo, "b_w_q": g_q, "kv_w": g_kv})
    dx3, gs["kv_b"] = _qkv_bwd(dz, dq4, dkv4, w_q, w_kv, "qkv_bwd", after=tok)
    dx1, tok = ffn_ple_bwd(0, dx3)
    w_ao = getw("a_w_out", None)
    dz, dzb, dyr, d_ln_g[0][0], d_ln_b[0][0], _ = _mixout_bwd(dx1, z[0][0], ln_g(0, 0), w_ao[None], F32, "a_out_bwd",
                                                              after=tok)
    g_ao = _wgrad(y_a[None], dzb[None], "g_a_w_out")[0]
    dproj, gs["a_lower_bound"], gs["a_norm_gain"] = _hgrn_bwd(proj, sm["a_lower_bound"], sm["a_norm_gain"], o_a, states,
                                                              dyr[0])
    tk = lambda t: (None, t, D)
    g_ain = _mm_tn(xb[None], dproj, N_DEV, lambda g, k: (0, k, 0), lambda g, k: (g // 2, k, g % 2),
                   tk, lambda t: (None, t, 512), (N_DEV, D, 512), (None, D, 512), lambda g, k: (g, 0, 0), name="g_a_w_in")
    tok = emit({"a_w_out": g_ao, "a_w_in": g_ain})
    grad_x = _inproj_bwd(dz, dproj, getw("a_w_in", None), "a_in_bwd", after=tok)
    gs["ple_b_gate"] = jnp.concatenate(g_bg, axis=0)
    gs["ln_gain"] = jnp.stack([jnp.concatenate(r, axis=0) for r in d_ln_g])
    gs["ln_bias"] = jnp.stack([jnp.concatenate(r, axis=0) for r in d_ln_b])
    return loss, grad_x, gs


def _peer(k):
    x, y, c = lax.axis_index("x"), lax.axis_index("y"), lax.axis_index("c")
    px = 1 - x if k & 4 else x
    py = 1 - y if k & 2 else y
    pc = 1 - c if k & 1 else c
    return (px, py, pc), 4 * px + 2 * py + pc


def _my_index():
    return 4 * lax.axis_index("x") + 2 * lax.axis_index("y") + lax.axis_index("c")


def _exchange(srcs, dst_shapes, plan, name):
    n_src, n_piece = len(srcs), len(plan)

    def body(*refs):
        src_refs, dst_refs = refs[:n_src], refs[n_src:n_src + len(dst_shapes)]
        send_sems, recv_sems, local_sems = refs[n_src + len(dst_shapes):]
        me = _my_index()

        def at(ref, idx):
            return ref.at[idx] if idx else ref

        local = []
        for t, (si, sfn, di, dfn) in enumerate(plan):
            cp = pltpu.make_async_copy(at(src_refs[si], sfn(me)), at(dst_refs[di], dfn(me)), local_sems.at[t])
            cp.start()
            local.append(cp)
        sends = []
        for k in range(1, N_DEV):
            peer, pid = _peer(k)
            for t, (si, sfn, di, dfn) in enumerate(plan):
                cp = pltpu.make_async_remote_copy(
                    src_ref=at(src_refs[si], sfn(pid)), dst_ref=at(dst_refs[di], dfn(me)),
                    send_sem=send_sems.at[t * 7 + k - 1], recv_sem=recv_sems.at[t * 7 + k - 1],
                    device_id=peer, device_id_type=MESH)
                cp.start()
                sends.append(cp)
        for k in range(1, N_DEV):
            peer, pid = _peer(k)
            for t, (si, sfn, di, dfn) in enumerate(plan):
                pltpu.make_async_remote_copy(
                    src_ref=at(src_refs[si], sfn(me)), dst_ref=at(dst_refs[di], dfn(pid)),
                    send_sem=send_sems.at[t * 7 + k - 1], recv_sem=recv_sems.at[t * 7 + k - 1],
                    device_id=peer, device_id_type=MESH).wait_recv()
        for cp in sends:
            cp.wait_send()
        for cp in local:
            cp.wait()

    hbm = pl.BlockSpec(memory_space=pltpu.HBM)
    return pl.pallas_call(
        body, in_specs=[hbm] * n_src, out_specs=[hbm] * len(dst_shapes), out_shape=dst_shapes,
        scratch_shapes=[pltpu.SemaphoreType.DMA((7 * n_piece,)), pltpu.SemaphoreType.DMA((7 * n_piece,)),
                        pltpu.SemaphoreType.DMA((n_piece,))],
        name=name)(*srcs)


def _gather(shards, name):
    dsts = [_sds((N_DEV,) + a.shape, a.dtype) for a in shards]
    plan = [(i, lambda j: (), i, lambda s: (s,)) for i in range(len(shards))]
    return _exchange(shards, dsts, plan, name)


_HBM = pl.BlockSpec(memory_space=pltpu.HBM)
_SEM = pl.BlockSpec(memory_space=pltpu.SEMAPHORE)
_DATAFLOW = pltpu.SideEffectType.DATAFLOW_SIDE_EFFECTING


def _piece_copy(mode, src, land, send_sems, recv_sems, t, k, sender, receiver, peer):
    return pltpu.make_async_remote_copy(
        src_ref=src if mode == "gather" else src.at[receiver], dst_ref=land.at[sender],
        send_sem=send_sems.at[t * 7 + k - 1], recv_sem=recv_sems.at[t * 7 + k - 1], device_id=peer, device_id_type=MESH)


def _xstart(groups, mode, name):
    flat = [a for g in groups for a in g]
    n, ng = len(flat), len(groups)
    land_shapes = [((N_DEV,) + a.shape) if mode == "gather" else a.shape for a in flat]
    first = [sum(len(g) for g in groups[:i]) for i in range(ng)]

    def body(*refs):
        srcs, lands, sems = refs[:n], refs[n:2 * n], refs[2 * n:2 * n + 2 * ng]
        tok_ref, local_sems = refs[-2], refs[-1]
        me = _my_index()
        local = []
        for i in range(n):
            cp = pltpu.make_async_copy(srcs[i] if mode == "gather" else srcs[i].at[me], lands[i].at[me], local_sems.at[i])
            cp.start()
            local.append(cp)
        for gi, g in enumerate(groups):
            for k in range(1, N_DEV):
                peer, pid = _peer(k)
                for t in range(len(g)):
                    i = first[gi] + t
                    _piece_copy(mode, srcs[i], lands[i], sems[2 * gi], sems[2 * gi + 1], t, k, me, pid, peer).start()
        tok_ref[...] = jnp.zeros_like(tok_ref)
        for cp in local:
            cp.wait()

    sem_shapes = []
    for g in groups:
        sem_shapes += [pltpu.SemaphoreType.DMA((7 * len(g),))] * 2
    thru = [pltpu.HBM(a.shape, a.dtype) for a in flat] + [pltpu.HBM(s, a.dtype) for s, a in zip(land_shapes, flat)]
    outs = pl.pallas_call(
        body, in_specs=[_HBM] * (2 * n),
        out_specs=[_SEM] * (2 * ng) + [_HBM] * (2 * n) + [pl.BlockSpec(memory_space=pltpu.VMEM)],
        out_shape=sem_shapes + thru + [_sds((8, 128), F32)],
        input_output_aliases={i: 2 * ng + i for i in range(2 * n)},
        scratch_shapes=[pltpu.SemaphoreType.DMA((n,))],
        compiler_params=pltpu.CompilerParams(has_side_effects=_DATAFLOW), name=name)(
            *[pltpu.with_memory_space_constraint(a, pltpu.HBM) for a in flat],
            *[pltpu.with_memory_space_constraint(lax.empty(s, a.dtype), pltpu.HBM) for s, a in zip(land_shapes, flat)])
    sems, srcs_thru, lands_thru = outs[:2 * ng], outs[2 * ng:2 * ng + n], outs[2 * ng + n:2 * ng + 2 * n]
    handles = [(sems[2 * gi], sems[2 * gi + 1], srcs_thru[first[gi]:first[gi] + len(g)],
                lands_thru[first[gi]:first[gi] + len(g)]) for gi, g in enumerate(groups)]
    return handles, outs[-1]


def _xwait(handle, mode, after, name):
    send_sems, recv_sems, srcs_thru, lands_thru = handle
    n = len(srcs_thru)

    def body(*refs):
        srcs, lands, send, recv = refs[:n], refs[n:2 * n], refs[2 * n], refs[2 * n + 1]
        me = _my_index()
        for k in range(1, N_DEV):
            peer, pid = _peer(k)
            for t in range(n):
                _piece_copy(mode, srcs[t], lands[t], send, recv, t, k, pid, me, peer).wait_recv()
        for k in range(1, N_DEV):
            peer, pid = _peer(k)
            for t in range(n):
                _piece_copy(mode, srcs[t], lands[t], send, recv, t, k, me, pid, peer).wait_send()

    extra = [] if after is None else [after]
    outs = pl.pallas_call(
        body, in_specs=[_HBM] * (2 * n) + [_SEM, _SEM] + [pl.BlockSpec(memory_space=pl.ANY)] * len(extra),
        out_specs=[_HBM] * (2 * n),
        out_shape=[pltpu.HBM(a.shape, a.dtype) for a in list(srcs_thru) + list(lands_thru)],
        input_output_aliases={i: i for i in range(2 * n)},
        compiler_params=pltpu.CompilerParams(has_side_effects=_DATAFLOW), name=name)(
            *srcs_thru, *lands_thru, send_sems, recv_sems, *extra)
    return outs[n:]


def _adamw(w, g, m, v):
    m = ADAM_B1 * m + (1.0 - ADAM_B1) * g
    v = ADAM_B2 * v + (1.0 - ADAM_B2) * (g * g)
    m_hat = m / (1.0 - ADAM_B1 ** ADAM_STEP)
    v_hat = v / (1.0 - ADAM_B2 ** ADAM_STEP)
    delta = -ADAM_LR * (m_hat / (jnp.sqrt(v_hat) + ADAM_EPS) + ADAM_WD * w)
    return delta, m, v


def _adam_big(w, parts, m, v, name):
    L, R, C = w.shape
    tr = _tile(R, (256, 128, 176, 64, 32, 16))
    nr = R // tr

    def body(w_ref, *refs):
        p_refs, (m_ref, v_ref, g_ref, d_ref, mo_ref, vo_ref) = refs[:L], refs[L:]
        for l in range(L):
            @pl.when(pl.program_id(0) == l)
            def _(p_ref=p_refs[l]):
                g = p_ref[0].astype(F32)
                for s in range(1, N_DEV):
                    g = g + p_ref[s].astype(F32)
                g_ref[...] = g
                d_ref[...], mo_ref[...], vo_ref[...] = _adamw(w_ref[...], g, m_ref[...], v_ref[...])

    row = pl.BlockSpec((None, tr, C), lambda l, i: (l, i, 0))
    park = lambda l_of: (lambda l, i: (0, jnp.where(l == l_of, i, 0 if l_of else nr - 1), 0))
    return pl.pallas_call(
        body, grid=(L, nr),
        in_specs=[row] + [pl.BlockSpec((N_DEV, tr, C), park(l)) for l in range(L)] + [row, row],
        out_specs=[row] * 4, out_shape=[_sds((L, R, C), F32)] * 4,
        compiler_params=_params(("arbitrary", "arbitrary")), name=name)(w, *parts, m, v)


SMALL = (("a_lower_bound", 2, True), ("ln_gain", 6, True), ("ln_bias", 6, True), ("a_norm_gain", 1, False),
         ("kv_b", 4, False), ("b_b_q", 8, False), ("b_sinks", 1, False), ("b_b_out", 8, False), ("ple_b_gate", 16, False))
SUBLANES = 8


def _slot(r):
    return -(-r // SUBLANES) * SUBLANES


SMALL_ROWS = sum(_slot(r) for _, r, _ in SMALL)
PART_ROWS = sum(r * N_DEV if sh else _slot(r) for _, r, sh in SMALL)


def _pack_rows(a, rows):
    flat = a.reshape(-1)
    return jnp.pad(flat, (0, rows * 128 - flat.size)).reshape(rows, 128)


def _pack_small(d):
    return jnp.concatenate([_pack_rows(d[n], _slot(r)) for n, r, _ in SMALL], axis=0)


def _unpack_small(packed, like):
    out, r0 = {}, 0
    for n, r, _ in SMALL:
        out[n] = packed[r0:r0 + r].reshape(-1)[:like[n].size].reshape(like[n].shape)
        r0 += _slot(r)
    return out


def _pack_partials(gs):
    blocks = []
    for n, r, sharded in SMALL:
        if sharded:
            blocks.append(gs[n].reshape(r * N_DEV, 128))
        else:
            blocks.append(_pack_rows(gs[n], _slot(r)))
    return jnp.concatenate(blocks, axis=0)


def _adam_small(parts, w, m, v):
    def body(p_ref, w_ref, m_ref, v_ref, g_ref, d_ref, mo_ref, vo_ref, tot):
        me = _my_index()
        t = p_ref[0]
        for s in range(1, N_DEV):
            t = t + p_ref[s]
        tot[...] = t
        g_ref[...] = jnp.zeros_like(g_ref)
        src, dst = 0, 0
        for _, r, sharded in SMALL:
            if sharded:
                for i in range(r):
                    g_ref[pl.ds(dst + i, 1), :] = tot[pl.ds(src + i * N_DEV + me, 1), :]
                src += r * N_DEV
            else:
                g_ref[pl.ds(dst, _slot(r)), :] = tot[pl.ds(src, _slot(r)), :]
                src += _slot(r)
            dst += _slot(r)
        d_ref[...], mo_ref[...], vo_ref[...] = _adamw(w_ref[...], g_ref[...], m_ref[...], v_ref[...])

    full = pl.BlockSpec((SMALL_ROWS, 128), lambda: (0, 0))
    return pl.pallas_call(
        body, in_specs=[pl.BlockSpec((N_DEV, PART_ROWS, 128), lambda: (0, 0, 0)), full, full, full],
        out_specs=[full] * 4, out_shape=[_sds((SMALL_ROWS, 128), F32)] * 4,
        scratch_shapes=[pltpu.VMEM((PART_ROWS, 128), F32)], name="adam_small")(parts, w, m, v)


WEIGHTS = ("a_w_in", "a_lower_bound", "a_norm_gain", "a_w_out", "kv_w", "kv_b", "b_w_q", "b_b_q", "b_sinks", "b_w_out",
           "b_b_out", "ffn_w_gate_up", "ffn_w_down", "ple_w_up", "ple_w_gate", "ple_b_gate", "ln_gain", "ln_bias")


def _heads_in(w, heads):
    return w.reshape(D, heads, ATT_HD).transpose(1, 0, 2)


def _heads_out(g):
    return g.transpose(1, 0, 2).reshape(D, -1)


GATHER_GROUPS = (("a_w_in",), ("a_w_out", "gu0", "dn0", "pu0", "pg0"), ("kv_w", "b_w_q", "b_w_out"),
                 ("gu1", "dn1", "pu1", "pg1"))
KERNEL_LAYOUT = {
    "a_w_in": lambda a: a,
    "a_w_out": lambda a: a.reshape(D, D),
    "kv_w": lambda a: _heads_in(a.reshape(D, 2 * ATT_KVH * ATT_HD), 2 * ATT_KVH),
    "b_w_q": lambda a: _heads_in(a.reshape(D, D), ATT_QH),
    "b_w_out": lambda a: a.reshape(ATT_QH, ATT_HD, D),
    "gu": lambda a: a.reshape(2, 4, D, FFN_B),
    "dn": lambda a: a.reshape(4, FFN_B, D),
    "pu": lambda a: a.transpose(1, 0, 2).reshape(PLE_DIM, D),
    "pg": lambda a: a.reshape(D, D),
}
_row_blocks = lambda a: a.reshape(N_DEV, -1, a.shape[-1])
OWNER_BLOCKS = {
    "a_w_in": lambda g: g,
    "a_w_out": _row_blocks,
    "kv_w": lambda g: _row_blocks(_heads_out(g)),
    "b_w_q": lambda g: _row_blocks(_heads_out(g)),
    "b_w_out": lambda g: _row_blocks(g.reshape(D, D)),
    "gu": lambda g: g,
    "dn": lambda g: _row_blocks(g.reshape(FFN_H, D)),
    "pu": lambda g: g.reshape(PLE_DIM, N_DEV, 128).transpose(1, 0, 2),
    "pg": _row_blocks,
}


def kernel(x, p, a_w_in, a_lower_bound, a_norm_gain, a_w_out, kv_w, kv_b, b_w_q, b_b_q, b_sinks, b_w_out, b_b_out, ffn_w_gate_up, ffn_w_down, ple_w_up, ple_w_gate, ple_b_gate, ln_gain, ln_bias, loss_target, m_a_w_in, m_a_lower_bound, m_a_norm_gain, m_a_w_out, m_kv_w, m_kv_b, m_b_w_q, m_b_b_q, m_b_sinks, m_b_w_out, m_b_b_out, m_ffn_w_gate_up, m_ffn_w_down, m_ple_w_up, m_ple_w_gate, m_ple_b_gate, m_ln_gain, m_ln_bias, v_a_w_in, v_a_lower_bound, v_a_norm_gain, v_a_w_out, v_kv_w, v_kv_b, v_b_w_q, v_b_b_q, v_b_sinks, v_b_w_out, v_b_b_out, v_ffn_w_gate_up, v_ffn_w_down, v_ple_w_up, v_ple_w_gate, v_ple_b_gate, v_ln_gain, v_ln_bias):
    given = dict(locals())
    w = {n: given[n] for n in WEIGHTS}
    m = {n: given["m_" + n] for n in WEIGHTS}
    v = {n: given["v_" + n] for n in WEIGHTS}
    shards = {"a_w_in": a_w_in[0], "a_w_out": a_w_out[0], "kv_w": kv_w, "b_w_q": b_w_q[0], "b_w_out": b_w_out[0]}
    for l in range(2):
        shards.update({f"gu{l}": ffn_w_gate_up[l], f"dn{l}": ffn_w_down[l], f"pu{l}": ple_w_up[l], f"pg{l}": ple_w_gate[l]})
    gather_handles, _ = _xstart([[shards[n].astype(CDT) for n in g] for g in GATHER_GROUPS], "gather", "gather_start")
    gathered = {}

    def getw(key, after):
        if key not in gathered:
            gi = [key in g for g in GATHER_GROUPS].index(True)
            lands = _xwait(gather_handles[gi], "gather", after, f"gather_wait{gi}")
            for n, a in zip(GATHER_GROUPS[gi], lands):
                gathered[n] = KERNEL_LAYOUT[n.rstrip("01")](a)
        return gathered[key]

    small_sharded = jnp.concatenate([_pack_rows(a, SUBLANES) for a in (a_lower_bound, ln_gain, ln_bias)], axis=0)
    (g_small,) = _gather([small_sharded], "gather_small")
    full_rows = lambda r0, r: g_small[:, r0:r0 + r].transpose(1, 0, 2).reshape(r, D)
    sm = {"a_lower_bound": full_rows(0, 2), "ln_gain": full_rows(SUBLANES, 6).reshape(2, 3, D),
          "ln_bias": full_rows(2 * SUBLANES, 6).reshape(2, 3, D), "a_norm_gain": a_norm_gain, "kv_b": kv_b,
          "b_b_q": b_b_q[0], "b_sinks": b_sinks, "b_b_out": b_b_out, "ple_b_gate": ple_b_gate}

    scatters = []

    def emit(grads):
        names = list(grads)
        (handle,), token = _xstart([[OWNER_BLOCKS[n.rstrip("01")](grads[n]) for n in names]], "scatter",
                                   f"scatter_start{len(scatters)}")
        scatters.append((names, handle))
        return token

    loss, grad_x, gs = _local_step(x[0], p[:, 0], loss_target[0], getw, sm, emit)

    (parts_small,) = _gather([_pack_partials(gs)], "gather_small_grads")
    packed = _adam_small(parts_small, _pack_small(w), _pack_small(m), _pack_small(v))
    small_out = [_unpack_small(a, w) for a in packed]
    out = {n: [s[n] for s in small_out] for n, _, _ in SMALL}

    parts, last = {}, grad_x
    adam_after = {1: (("kv_w", "kv_w"), ("b_w_q", "b_w_q"), ("b_w_out", "b_w_out")),
                  2: (("ffn_w_gate_up", "gu"), ("ffn_w_down", "dn"), ("ple_w_up", "pu"), ("ple_w_gate", "pg")),
                  3: (("a_w_out", "a_w_out"), ("a_w_in", "a_w_in"))}
    for i, (names, handle) in enumerate(scatters):
        parts.update(zip(names, _xwait(handle, "scatter", last, f"scatter_wait{i}")))
        for n, key in adam_after.get(i, ()):
            lrc = (1,) * (3 - w[n].ndim) + w[n].shape
            layers = [parts[key]] if key in parts else [parts[key + "0"], parts[key + "1"]]
            res = _adam_big(w[n].reshape(lrc), layers, m[n].reshape(lrc), v[n].reshape(lrc), "adam_" + n)
            out[n] = [r.reshape(w[n].shape) for r in res]
            last = res[3]

    loss = lax.psum(loss[0, 0], ("x", "y", "c"))
    res = [loss, grad_x[None]]
    for i in range(4):
        res += [out[n][i] for n in WEIGHTS]
    return tuple(res)
```

```python
import jax
import jax.numpy as jnp
from jax import lax
from jax.experimental import pallas as pl
from jax.experimental.pallas import tpu as pltpu

F32 = jnp.float32
CDT = jnp.bfloat16

N_DEV = 8
D = 1024
HG_H, HG_DK, HG_CH = 8, 128, 64
ATT_HD, ATT_QH, ATT_KVH, ATT_G, WINDOW = 64, 16, 4, 4, 128
FFN_H = 2816
FFN_B = FFN_H // 4
PLE_DIM = 256
ALPHA = (2.0 * 2) ** 0.25
LN_EPS = 1e-5
RMS_EPS = 1e-6
ADAM_LR, ADAM_B1, ADAM_B2, ADAM_EPS, ADAM_WD, ADAM_STEP = 0.001, 0.9, 0.999, 1e-08, 0.01, 10
ROW_TILES = (256, 128, 64)
VMEM_LIMIT = 48 * 1024 * 1024
NEG = -1e30

MESH = pl.DeviceIdType.MESH


def _tile(n, cands=ROW_TILES):
    for t in cands:
        if n % t == 0:
            return t
    return n


def _sds(shape, dtype):
    return jax.ShapeDtypeStruct(tuple(shape), dtype)


def _params(sem):
    return pltpu.CompilerParams(dimension_semantics=sem, vmem_limit_bytes=VMEM_LIMIT)


def _dot(a, b):
    return jnp.dot(a.astype(CDT), b.astype(CDT), preferred_element_type=F32)


def _dot_nt(a, b):
    return lax.dot_general(a.astype(CDT), b.astype(CDT), (((1,), (1,)), ((), ())), preferred_element_type=F32)


def _dot_tn(a, b):
    return lax.dot_general(a.astype(CDT), b.astype(CDT), (((0,), (0,)), ((), ())), preferred_element_type=F32)


def _sigmoid(x):
    return jax.nn.sigmoid(x)


def _ln_fwd(z, g, b):
    mu = jnp.mean(z, axis=-1, keepdims=True)
    zc = z - mu
    var = jnp.mean(zc * zc, axis=-1, keepdims=True)
    return zc * lax.rsqrt(var + LN_EPS) * g + b


def _ln_bwd(z, g, dy):
    mu = jnp.mean(z, axis=-1, keepdims=True)
    zc = z - mu
    var = jnp.mean(zc * zc, axis=-1, keepdims=True)
    rstd = lax.rsqrt(var + LN_EPS)
    xhat = zc * rstd
    dxh = dy * g
    dz = rstd * (dxh - jnp.mean(dxh, axis=-1, keepdims=True) - xhat * jnp.mean(dxh * xhat, axis=-1, keepdims=True))
    return dz, xhat


def _colsum(x):
    return jnp.sum(x, axis=0, keepdims=True)


def _acc(ref, val, first):
    @pl.when(first)
    def _():
        ref[...] = val

    @pl.when(jnp.logical_not(first))
    def _():
        ref[...] += val


def _call(after, body, **kw):
    if after is None:
        return pl.pallas_call(body, **kw)
    kw["in_specs"] = [pl.BlockSpec(memory_space=pl.ANY)] + list(kw["in_specs"])

    def ordered_body(after_ref, *refs):
        body(*refs)

    call = pl.pallas_call(ordered_body, **kw)
    return lambda *args: call(after, *args)


def _mm_nn(a, b3, out_shape, oblock, omap, out_dtype, bias3=None, name="mm_nn"):
    M, K = a.shape
    G, _, Nb = b3.shape
    tm = _tile(M)

    def body(a_ref, b_ref, *rest):
        o_ref = rest[-1]
        acc = _dot(a_ref[...], b_ref[...])
        if bias3 is not None:
            acc = acc + rest[0][...]
        o_ref[...] = acc.astype(o_ref.dtype)

    in_specs = [pl.BlockSpec((tm, K), lambda g, i: (i, 0)), pl.BlockSpec((None, K, Nb), lambda g, i: (g, 0, 0))]
    args = [a, b3]
    if bias3 is not None:
        in_specs.append(pl.BlockSpec((None, 1, Nb), lambda g, i: (g, 0, 0)))
        args.append(bias3)
    return pl.pallas_call(
        body, grid=(G, M // tm), in_specs=in_specs, out_specs=pl.BlockSpec(oblock, omap),
        out_shape=_sds(out_shape, out_dtype), compiler_params=_params(("arbitrary", "arbitrary")), name=name)(*args)


def _mm_tn(a3, b3, G, amap, bmap, ablock, bblock, out_shape, oblock, omap, name="mm_tn"):
    S = a3.shape[1]
    tk = _tile(S, (512, 256, 128))
    Mo, No = oblock[-2], oblock[-1]

    def body(a_ref, b_ref, o_ref, acc):
        k = pl.program_id(1)
        _acc(acc, _dot_tn(a_ref[...], b_ref[...]), k == 0)

        @pl.when(k == pl.num_programs(1) - 1)
        def _():
            o_ref[...] = acc[...].astype(o_ref.dtype)

    return pl.pallas_call(
        body, grid=(G, S // tk),
        in_specs=[pl.BlockSpec(ablock(tk), amap), pl.BlockSpec(bblock(tk), bmap)],
        out_specs=pl.BlockSpec(oblock, omap), out_shape=_sds(out_shape, CDT),
        scratch_shapes=[pltpu.VMEM((Mo, No), F32)],
        compiler_params=_params(("arbitrary", "arbitrary")), name=name)(a3, b3)


def _wgrad(a3, b3, name):
    Ga, S, M = a3.shape
    Gb, _, N = b3.shape
    G = max(Ga, Gb)
    return _mm_tn(
        a3, b3, G,
        (lambda g, k: (g, k, 0)) if Ga > 1 else (lambda g, k: (0, k, 0)),
        (lambda g, k: (g, k, 0)) if Gb > 1 else (lambda g, k: (0, k, 0)),
        lambda tk: (None, tk, M), lambda tk: (None, tk, N),
        (G, M, N), (None, M, N), lambda g, k: (g, 0, 0), name=name)


def _mixout_ln(u3, w3, bias, xin, gain, beta, name):
    G, S, Kb = u3.shape
    tm = _tile(S)

    def body(u_ref, w_ref, b_ref, x_ref, g_ref, be_ref, z_ref, xo_ref, xob_ref):
        h = b_ref[...] + _dot(u_ref[0], w_ref[0])
        for g in range(1, G):
            h = h + _dot(u_ref[g], w_ref[g])
        z = ALPHA * x_ref[...] + h
        z_ref[...] = z
        y = _ln_fwd(z, g_ref[...], be_ref[...])
        xo_ref[...] = y
        xob_ref[...] = y.astype(CDT)

    row = pl.BlockSpec((tm, D), lambda i: (i, 0))
    vec = pl.BlockSpec((1, D), lambda i: (0, 0))
    return pl.pallas_call(
        body, grid=(S // tm,),
        in_specs=[pl.BlockSpec((G, tm, Kb), lambda i: (0, i, 0)), pl.BlockSpec((G, Kb, D), lambda i: (0, 0, 0)),
                  vec, row, vec, vec],
        out_specs=[row, row, row], out_shape=[_sds((S, D), F32), _sds((S, D), F32), _sds((S, D), CDT)],
        compiler_params=_params(("arbitrary",)), name=name)(u3, w3, bias, xin, gain, beta)


def _ffn_fwd(xin, xin_b, wgu, wdn, gain, beta, name):
    S = xin.shape[0]
    tm = _tile(S)

    def body(x_ref, xb_ref, wgu_ref, wdn_ref, g_ref, be_ref, gu_ref, hid_ref, z_ref, xo_ref, xob_ref, acc):
        j = pl.program_id(1)
        xb = xb_ref[...]
        gate = _dot(xb, wgu_ref[0])
        up = _dot(xb, wgu_ref[1])
        gu_ref[0] = gate
        gu_ref[1] = up
        hid = (gate * _sigmoid(gate) * up).astype(CDT)
        hid_ref[...] = hid
        _acc(acc, _dot(hid, wdn_ref[...]), j == 0)

        @pl.when(j == 3)
        def _():
            z = ALPHA * x_ref[...] + acc[...]
            z_ref[...] = z
            y = _ln_fwd(z, g_ref[...], be_ref[...])
            xo_ref[...] = y
            xob_ref[...] = y.astype(CDT)

    row = pl.BlockSpec((tm, D), lambda i, j: (i, 0))
    vec = pl.BlockSpec((1, D), lambda i, j: (0, 0))
    return pl.pallas_call(
        body, grid=(S // tm, 4),
        in_specs=[row, row, pl.BlockSpec((2, None, D, FFN_B), lambda i, j: (0, j, 0, 0)),
                  pl.BlockSpec((None, FFN_B, D), lambda i, j: (j, 0, 0)), vec, vec],
        out_specs=[pl.BlockSpec((2, None, tm, FFN_B), lambda i, j: (0, j, i, 0)),
                   pl.BlockSpec((None, tm, FFN_B), lambda i, j: (j, i, 0)), row, row, row],
        out_shape=[_sds((2, 4, S, FFN_B), F32), _sds((4, S, FFN_B), CDT), _sds((S, D), F32), _sds((S, D), F32),
                   _sds((S, D), CDT)],
        scratch_shapes=[pltpu.VMEM((tm, D), F32)],
        compiler_params=_params(("arbitrary", "arbitrary")), name=name)(xin, xin_b, wgu, wdn, gain, beta)


def _ple_fwd(xin, xin_b, p_b, wpg, bgate, wpu, gain, beta, name):
    S = xin.shape[0]
    tm = _tile(S)

    def body(x_ref, xb_ref, p_ref, wpg_ref, bg_ref, wpu_ref, g_ref, be_ref, sg_ref, up_ref, z_ref, xo_ref, xob_ref):
        sg = _sigmoid(_dot(xb_ref[...], wpg_ref[...]) + bg_ref[...])
        up = _dot(p_ref[...], wpu_ref[...])
        sg_ref[...] = sg
        up_ref[...] = up
        z = ALPHA * x_ref[...] + sg * up
        z_ref[...] = z
        y = _ln_fwd(z, g_ref[...], be_ref[...])
        xo_ref[...] = y
        xob_ref[...] = y.astype(CDT)

    row = pl.BlockSpec((tm, D), lambda i: (i, 0))
    vec = pl.BlockSpec((1, D), lambda i: (0, 0))
    return pl.pallas_call(
        body, grid=(S // tm,),
        in_specs=[row, row, pl.BlockSpec((tm, PLE_DIM), lambda i: (i, 0)), pl.BlockSpec((D, D), lambda i: (0, 0)), vec,
                  pl.BlockSpec((PLE_DIM, D), lambda i: (0, 0)), vec, vec],
        out_specs=[row] * 5,
        out_shape=[_sds((S, D), F32)] * 4 + [_sds((S, D), CDT)],
        compiler_params=_params(("arbitrary",)), name=name)(xin, xin_b, p_b, wpg, bgate, wpu, gain, beta)


def _loss_fwd_bwd(y, target):
    S = y.shape[0]
    tm = _tile(S)

    def body(y_ref, t_ref, l_ref, dy_ref):
        e = y_ref[...] - t_ref[...]
        dy_ref[...] = e * (1.0 / D)
        part = 0.5 * jnp.sum(jnp.sum(e * e, axis=-1, keepdims=True) * (1.0 / D), axis=0, keepdims=True)
        _acc(l_ref, part, pl.program_id(0) == 0)

    row = pl.BlockSpec((tm, D), lambda i: (i, 0))
    return pl.pallas_call(
        body, grid=(S // tm,), in_specs=[row, row],
        out_specs=[pl.BlockSpec((1, 1), lambda i: (0, 0)), row],
        out_shape=[_sds((1, 1), F32), _sds((S, D), F32)],
        compiler_params=_params(("arbitrary",)), name="loss")(y, target)


def _ple_bwd(dy, z, sg, up, gain, wpg, name, after=None):
    S = dy.shape[0]
    tm = _tile(S)

    def body(dy_ref, z_ref, sg_ref, up_ref, g_ref, wpg_ref, dx_ref, dgl_ref, dup_ref, dgain_ref, dbeta_ref, dbg_ref):
        first = pl.program_id(0) == 0
        dy_ = dy_ref[...]
        dz, xhat = _ln_bwd(z_ref[...], g_ref[...], dy_)
        sg_ = sg_ref[...]
        dgl = dz * up_ref[...] * sg_ * (1.0 - sg_)
        dgl_ref[...] = dgl.astype(CDT)
        dup_ref[...] = (dz * sg_).astype(CDT)
        dx_ref[...] = ALPHA * dz + _dot_nt(dgl, wpg_ref[...])
        _acc(dgain_ref, _colsum(dy_ * xhat), first)
        _acc(dbeta_ref, _colsum(dy_), first)
        _acc(dbg_ref, _colsum(dgl), first)

    row = pl.BlockSpec((tm, D), lambda i: (i, 0))
    vec = pl.BlockSpec((1, D), lambda i: (0, 0))
    return _call(
        after, body, grid=(S // tm,), in_specs=[row, row, row, row, vec, pl.BlockSpec((D, D), lambda i: (0, 0))],
        out_specs=[row, row, row, vec, vec, vec],
        out_shape=[_sds((S, D), F32), _sds((S, D), CDT), _sds((S, D), CDT)] + [_sds((1, D), F32)] * 3,
        compiler_params=_params(("arbitrary",)), name=name)(dy, z, sg, up, gain, wpg)


def _ffn_bwd(dy, z, gu, wgu, wdn, gain, name, after=None):
    S = dy.shape[0]
    tm = _tile(S)

    def body(dy_ref, z_ref, gu_ref, wgu_ref, wdn_ref, g_ref, dx_ref, dzb_ref, dgu_ref, dgain_ref, dbeta_ref,
             dz_scr, acc):
        i, j = pl.program_id(0), pl.program_id(1)

        @pl.when(j == 0)
        def _():
            dy_ = dy_ref[...]
            dz, xhat = _ln_bwd(z_ref[...], g_ref[...], dy_)
            dz_scr[...] = dz
            dzb_ref[...] = dz.astype(CDT)
            _acc(dgain_ref, _colsum(dy_ * xhat), i == 0)
            _acc(dbeta_ref, _colsum(dy_), i == 0)

        dhid = _dot_nt(dz_scr[...], wdn_ref[...])
        gate, up = gu_ref[0], gu_ref[1]
        sg = _sigmoid(gate)
        dgate = (dhid * up * (sg * (1.0 + gate * (1.0 - sg)))).astype(CDT)
        dup = (dhid * (gate * sg)).astype(CDT)
        dgu_ref[0] = dgate
        dgu_ref[1] = dup
        _acc(acc, _dot_nt(dgate, wgu_ref[0]) + _dot_nt(dup, wgu_ref[1]), j == 0)

        @pl.when(j == 3)
        def _():
            dx_ref[...] = ALPHA * dz_scr[...] + acc[...]

    row = pl.BlockSpec((tm, D), lambda i, j: (i, 0))
    vec = pl.BlockSpec((1, D), lambda i, j: (0, 0))
    return _call(
        after, body, grid=(S // tm, 4),
        in_specs=[row, row, pl.BlockSpec((2, None, tm, FFN_B), lambda i, j: (0, j, i, 0)),
                  pl.BlockSpec((2, None, D, FFN_B), lambda i, j: (0, j, 0, 0)),
                  pl.BlockSpec((None, FFN_B, D), lambda i, j: (j, 0, 0)), vec],
        out_specs=[row, row, pl.BlockSpec((2, None, tm, FFN_B), lambda i, j: (0, j, i, 0)), vec, vec],
        out_shape=[_sds((S, D), F32), _sds((S, D), CDT), _sds((2, 4, S, FFN_B), CDT), _sds((1, D), F32),
                   _sds((1, D), F32)],
        scratch_shapes=[pltpu.VMEM((tm, D), F32), pltpu.VMEM((tm, D), F32)],
        compiler_params=_params(("arbitrary", "arbitrary")), name=name)(dy, z, gu, wgu, wdn, gain)


def _mixout_bwd(dy, z, gain, w3, du_dtype, name, after=None):
    S = dy.shape[0]
    G, Kb, _ = w3.shape
    tm = _tile(S)

    def body(dy_ref, z_ref, g_ref, w_ref, dz_ref, dzb_ref, du_ref, dgain_ref, dbeta_ref, dbias_ref):
        first = pl.program_id(0) == 0
        dy_ = dy_ref[...]
        dz, xhat = _ln_bwd(z_ref[...], g_ref[...], dy_)
        dz_ref[...] = dz
        dzb = dz.astype(CDT)
        dzb_ref[...] = dzb
        for g in range(G):
            du_ref[g] = _dot_nt(dzb, w_ref[g]).astype(du_ref.dtype)
        _acc(dgain_ref, _colsum(dy_ * xhat), first)
        _acc(dbeta_ref, _colsum(dy_), first)
        _acc(dbias_ref, _colsum(dz), first)

    row = pl.BlockSpec((tm, D), lambda i: (i, 0))
    vec = pl.BlockSpec((1, D), lambda i: (0, 0))
    return _call(
        after, body, grid=(S // tm,), in_specs=[row, row, vec, pl.BlockSpec((G, Kb, D), lambda i: (0, 0, 0))],
        out_specs=[row, row, pl.BlockSpec((G, tm, Kb), lambda i: (0, i, 0)), vec, vec, vec],
        out_shape=[_sds((S, D), F32), _sds((S, D), CDT), _sds((G, S, Kb), du_dtype)] + [_sds((1, D), F32)] * 3,
        compiler_params=_params(("arbitrary",)), name=name)(dy, z, gain, w3)


def _qkv_bwd(dz, dq4, dkv4, wq4, wkv4, name, after=None):
    S = dz.shape[0]
    tm = _tile(S)
    HQ, HK = dq4.shape[0], dkv4.shape[0]

    def body(dz_ref, dq_ref, dkv_ref, wq_ref, wkv_ref, dx_ref, dkvb_ref):
        first = pl.program_id(0) == 0
        acc = ALPHA * dz_ref[...]
        for h in range(HQ):
            acc = acc + _dot_nt(dq_ref[h], wq_ref[h])
        for h in range(HK):
            acc = acc + _dot_nt(dkv_ref[h], wkv_ref[h])
        dx_ref[...] = acc
        for h in range(HK):
            _acc(dkvb_ref.at[h], _colsum(dkv_ref[h]), first)

    row = pl.BlockSpec((tm, D), lambda i: (i, 0))
    return _call(
        after, body, grid=(S // tm,),
        in_specs=[row, pl.BlockSpec((HQ, tm, ATT_HD), lambda i: (0, i, 0)), pl.BlockSpec((HK, tm, ATT_HD), lambda i: (0, i, 0)),
                  pl.BlockSpec((HQ, D, ATT_HD), lambda i: (0, 0, 0)), pl.BlockSpec((HK, D, ATT_HD), lambda i: (0, 0, 0))],
        out_specs=[row, pl.BlockSpec((HK, 1, ATT_HD), lambda i: (0, 0, 0))],
        out_shape=[_sds((S, D), F32), _sds((HK, 1, ATT_HD), F32)],
        compiler_params=_params(("arbitrary",)), name=name)(dz, dq4, dkv4, wq4, wkv4)


def _inproj_bwd(dz, dproj, wain, name, after=None):
    S = dz.shape[0]
    tm = _tile(S)
    nb = wain.shape[-1]

    def body(dz_ref, dp_ref, w_ref, dx_ref, acc):
        j = pl.program_id(1)
        _acc(acc, _dot_nt(dp_ref[...], w_ref[...]), j == 0)

        @pl.when(j == N_DEV - 1)
        def _():
            dx_ref[...] = ALPHA * dz_ref[...] + acc[...]

    row = pl.BlockSpec((tm, D), lambda i, j: (i, 0))
    return _call(
        after, body, grid=(S // tm, N_DEV),
        in_specs=[row, pl.BlockSpec((None, tm, nb), lambda i, j: (j // 2, i, j % 2)),
                  pl.BlockSpec((None, D, nb), lambda i, j: (j, 0, 0))],
        out_specs=row, out_shape=_sds((S, D), F32), scratch_shapes=[pltpu.VMEM((tm, D), F32)],
        compiler_params=_params(("arbitrary", "arbitrary")), name=name)(dz, dproj, wain)


def _hp(a, b, dims):
    ah = a.astype(CDT)
    al = (a - ah.astype(F32)).astype(CDT)
    bh = b.astype(CDT)
    bl = (b - bh.astype(F32)).astype(CDT)
    dg = lambda u, w: lax.dot_general(u, w, (dims, ((), ())), preferred_element_type=F32)
    return dg(ah, bh) + (dg(ah, bl) + dg(al, bh))


def _hdot(a, b):
    return _hp(a, b, ((1,), (0,)))


def _hdot_nt(a, b):
    return _hp(a, b, ((1,), (1,)))


def _hdot_tn(a, b):
    return _hp(a, b, ((0,), (0,)))


def _tri_dot(tri, x):
    hi = x.astype(CDT)
    r1 = x - hi.astype(F32)
    mid = r1.astype(CDT)
    lo = (r1 - mid.astype(F32)).astype(CDT)
    t = tri.astype(CDT)
    return (jnp.dot(t, hi, preferred_element_type=F32) + jnp.dot(t, mid, preferred_element_type=F32)
            + jnp.dot(t, lo, preferred_element_type=F32))


def _hg_gates(q, f, alb_ref):
    a0, a1 = alb_ref[0:1, :], alb_ref[1:2, :]
    mx = jnp.maximum(a0, a1)
    e0, e1 = jnp.exp(a0 - mx), jnp.exp(a1 - mx)
    lb = e0 / (e0 + e1)
    sig = _sigmoid(f)
    forget = lb + (1.0 - lb) * sig
    k = (1.0 - lb) * _sigmoid(-f)
    qs = q * _sigmoid(q) * (HG_DK ** -0.5)
    return qs, k, jnp.log(forget), sig, lb, forget


def _hg_intra(qs, k, b, b_scr):
    b_scr[...] = b
    bm = b_scr[pl.ds(HG_CH // 2 - 1, 1), :]
    bl = b_scr[pl.ds(HG_CH - 1, 1), :]
    eb = jnp.exp(b)
    qb = qs * eb
    e_q = jnp.exp(b - bm)
    e_k = jnp.exp(bm - b)
    e_d = jnp.exp(bl - b)
    return qb, qs * e_q, k * e_k, k * e_d, jnp.exp(bl), eb, e_q, e_k, e_d


def _hgrn_fwd(proj, alb, ngain):
    S = proj.shape[1]
    nc = S // HG_CH

    def body(pj_ref, alb_ref, ng_ref, o_ref, y_ref, st_ref, st_scr, b_scr):
        n = pl.program_id(1)

        @pl.when(n == 0)
        def _():
            st_scr[...] = jnp.zeros_like(st_scr)

        q, f, v, g = pj_ref[0], pj_ref[1], pj_ref[2], pj_ref[3]
        qs, k, logf, _, _, _ = _hg_gates(q, f, alb_ref)
        r = lax.broadcasted_iota(jnp.int32, (HG_CH, HG_CH), 0)
        c = lax.broadcasted_iota(jnp.int32, (HG_CH, HG_CH), 1)
        causal = r >= c
        b = _tri_dot(causal.astype(F32), logf)
        qb, qt, kt, kd, ebl, _, _, _, _ = _hg_intra(qs, k, b, b_scr)
        st = st_scr[...]
        st_ref[...] = st
        a = jnp.where(causal, _hdot_nt(qt, kt), 0.0)
        o = _hdot(a, v) + _hdot_nt(qb, st)
        st_scr[...] = st * ebl + _hdot_tn(v, kd)
        o_ref[...] = o
        rinv = lax.rsqrt(jnp.mean(o * o, axis=-1, keepdims=True) + RMS_EPS)
        y_ref[...] = (o * rinv * ng_ref[...] * (g * _sigmoid(g))).astype(CDT)

    blk = pl.BlockSpec((HG_CH, HG_DK), lambda h, n: (n, h))
    return pl.pallas_call(
        body, grid=(HG_H, nc),
        in_specs=[pl.BlockSpec((4, HG_CH, HG_DK), lambda h, n: (0, n, h)), pl.BlockSpec((2, HG_DK), lambda h, n: (0, h)),
                  pl.BlockSpec((1, HG_DK), lambda h, n: (0, 0))],
        out_specs=[blk, blk, pl.BlockSpec((None, None, HG_DK, HG_DK), lambda h, n: (h, n, 0, 0))],
        out_shape=[_sds((S, D), F32), _sds((S, D), CDT), _sds((HG_H, nc, HG_DK, HG_DK), F32)],
        scratch_shapes=[pltpu.VMEM((HG_DK, HG_DK), F32), pltpu.VMEM((HG_CH, HG_DK), F32)],
        compiler_params=_params(("arbitrary", "arbitrary")), name="hgrn_fwd")(proj, alb, ngain)


def _hgrn_bwd(proj, alb, ngain, o, states, dy):
    S = proj.shape[1]
    nc = S // HG_CH

    def body(pj_ref, alb_ref, ng_ref, o_ref, st_ref, dy_ref, dpj_ref, dalb_ref, dng_ref, dst_scr, b_scr):
        h, n = pl.program_id(0), pl.program_id(1)

        @pl.when(n == 0)
        def _():
            dst_scr[...] = jnp.zeros_like(dst_scr)

        q, f, v, g = pj_ref[0], pj_ref[1], pj_ref[2], pj_ref[3]
        o_ = o_ref[...]
        dy_ = dy_ref[...]
        ng = ng_ref[...]
        sg = _sigmoid(g)
        rinv = lax.rsqrt(jnp.mean(o_ * o_, axis=-1, keepdims=True) + RMS_EPS)
        nrm = o_ * rinv
        dr = dy_ * (g * sg)
        dg = dy_ * nrm * ng * (sg * (1.0 + g * (1.0 - sg)))
        dn = dr * ng
        do = rinv * (dn - nrm * jnp.mean(dn * nrm, axis=-1, keepdims=True))
        _acc(dng_ref, _colsum(dr * nrm), jnp.logical_and(h == 0, n == 0))
        qs, k, logf, sig, lb, forget = _hg_gates(q, f, alb_ref)
        r = lax.broadcasted_iota(jnp.int32, (HG_CH, HG_CH), 0)
        c = lax.broadcasted_iota(jnp.int32, (HG_CH, HG_CH), 1)
        causal = r >= c
        b = _tri_dot(causal.astype(F32), logf)
        qb, qt, kt, kd, ebl, eb, e_q, e_k, e_d = _hg_intra(qs, k, b, b_scr)
        st = st_ref[...]
        dstn = dst_scr[...]
        a = jnp.where(causal, _hdot_nt(qt, kt), 0.0)
        da = jnp.where(causal, _hdot_nt(do, v), 0.0)
        dv = _hdot_tn(a, do) + _hdot_nt(kd, dstn)
        dqb = _hdot(do, st)
        dkd = _hdot(v, dstn)
        dqt = _hdot(da, kt)
        dkt = _hdot_tn(da, qt)
        dbl = _colsum(dkd * kd) + ebl * _colsum(dstn * st)
        dst_scr[...] = dstn * ebl + _hdot_tn(do, qb)
        dqs = dqt * e_q + dqb * eb
        dk = dkt * e_k + dkd * e_d
        db = dqt * qt + dqb * qb - dkt * kt - dkd * kd
        dlogf = _tri_dot((r <= c).astype(F32), db) + dbl
        dforget = dlogf / forget
        dsig = (1.0 - lb) * (dforget - dk)
        df = dsig * sig * (1.0 - sig)
        dlb = _colsum((dforget - dk) * (1.0 - sig))
        sq = _sigmoid(q)
        dq = dqs * (HG_DK ** -0.5) * (sq * (1.0 + q * (1.0 - sq)))
        dpj_ref[0] = dq.astype(CDT)
        dpj_ref[1] = df.astype(CDT)
        dpj_ref[2] = dv.astype(CDT)
        dpj_ref[3] = dg.astype(CDT)
        da0 = dlb * lb * (1.0 - lb)
        _acc(dalb_ref.at[pl.ds(0, 1)], da0, n == 0)
        _acc(dalb_ref.at[pl.ds(1, 1)], -da0, n == 0)

    rev = lambda h, n: (nc - 1 - n, h)
    blk = pl.BlockSpec((HG_CH, HG_DK), rev)
    return pl.pallas_call(
        body, grid=(HG_H, nc),
        in_specs=[pl.BlockSpec((4, HG_CH, HG_DK), lambda h, n: (0, nc - 1 - n, h)),
                  pl.BlockSpec((2, HG_DK), lambda h, n: (0, h)), pl.BlockSpec((1, HG_DK), lambda h, n: (0, 0)), blk,
                  pl.BlockSpec((None, None, HG_DK, HG_DK), lambda h, n: (h, nc - 1 - n, 0, 0)), blk],
        out_specs=[pl.BlockSpec((4, HG_CH, HG_DK), lambda h, n: (0, nc - 1 - n, h)),
                   pl.BlockSpec((2, HG_DK), lambda h, n: (0, h)), pl.BlockSpec((1, HG_DK), lambda h, n: (0, 0))],
        out_shape=[_sds((4, S, D), CDT), _sds((2, D), F32), _sds((1, HG_DK), F32)],
        scratch_shapes=[pltpu.VMEM((HG_DK, HG_DK), F32), pltpu.VMEM((HG_CH, HG_DK), F32)],
        compiler_params=_params(("arbitrary", "arbitrary")), name="hgrn_bwd")(proj, alb, ngain, o, states, dy)


def _slope(h):
    return 2.0 ** (-8.0 * (h + 1) / ATT_QH)


def _attn_mask(n):
    qi = lax.broadcasted_iota(jnp.int32, (WINDOW, 2 * WINDOW), 0)
    si = lax.broadcasted_iota(jnp.int32, (WINDOW, 2 * WINDOW), 1)
    dist = qi - si + WINDOW
    valid = (dist >= 0) & (dist < WINDOW) & (n * WINDOW - WINDOW + si >= 0)
    return valid, dist.astype(F32)


def _attn_probs(qh, kh, sink, slope, valid, distf):
    s = _dot_nt(qh, kh) * (ATT_HD ** -0.5) - slope * distf
    s = jnp.where(valid, s, NEG)
    m = jnp.maximum(jnp.max(s, axis=-1, keepdims=True), sink)
    e = jnp.exp(s - m)
    es = jnp.exp(sink - m)
    inv = 1.0 / (jnp.sum(e, axis=-1, keepdims=True) + es)
    return e * inv, es * inv


def _attn_specs(S):
    nb = S // WINDOW
    cur = lambda H: pl.BlockSpec((H, WINDOW, ATT_HD), lambda n: (0, n, 0))
    prev = lambda H: pl.BlockSpec((H, WINDOW, ATT_HD), lambda n: (0, jnp.maximum(n - 1, 0), 0))
    return nb, cur, prev


def _attn_fwd(q4, kv4, sinks):
    S = q4.shape[1]
    nb, cur, prev = _attn_specs(S)

    def body(sink_ref, q_ref, kvc_ref, kvp_ref, o_ref):
        valid, distf = _attn_mask(pl.program_id(0))
        for h in range(ATT_QH):
            kvh = h // ATT_G
            kh = jnp.concatenate([kvp_ref[kvh], kvc_ref[kvh]], axis=0)
            vh = jnp.concatenate([kvp_ref[ATT_KVH + kvh], kvc_ref[ATT_KVH + kvh]], axis=0)
            p, _ = _attn_probs(q_ref[h], kh, sink_ref[0, h], _slope(h), valid, distf)
            o_ref[h] = _dot(p, vh).astype(CDT)

    return pl.pallas_call(
        body, grid=(nb,),
        in_specs=[pl.BlockSpec(memory_space=pltpu.SMEM), cur(ATT_QH), cur(2 * ATT_KVH), prev(2 * ATT_KVH)],
        out_specs=cur(ATT_QH), out_shape=_sds((ATT_QH, S, ATT_HD), CDT),
        compiler_params=_params(("arbitrary",)), name="attn_fwd")(sinks, q4, kv4, kv4)


def _attn_bwd(q4, kv4, sinks, do4):
    S = q4.shape[1]
    nb, cur, prev = _attn_specs(S)

    def body(sink_ref, q_ref, kvc_ref, kvp_ref, do_ref, dq_ref, dkv_ref, dbq_ref, dsink_ref):
        n = pl.program_id(0)
        first = n == 0

        @pl.when(first)
        def _():
            dkv_ref[...] = jnp.zeros_like(dkv_ref)
            dsink_ref[...] = jnp.zeros_like(dsink_ref)

        valid, distf = _attn_mask(n)
        lane = lax.broadcasted_iota(jnp.int32, (1, 128), 1)
        rows_cur = pl.ds(pl.multiple_of(n * WINDOW, WINDOW), WINDOW)
        rows_prev = pl.ds(pl.multiple_of(jnp.maximum(n - 1, 0) * WINDOW, WINDOW), WINDOW)
        for h in range(ATT_QH):
            kvh = h // ATT_G
            qh = q_ref[h]
            doh = do_ref[h]
            kh = jnp.concatenate([kvp_ref[kvh], kvc_ref[kvh]], axis=0)
            vh = jnp.concatenate([kvp_ref[ATT_KVH + kvh], kvc_ref[ATT_KVH + kvh]], axis=0)
            p, ps = _attn_probs(qh, kh, sink_ref[0, h], _slope(h), valid, distf)
            dp = _dot_nt(doh, vh)
            dd = jnp.sum(p * dp, axis=-1, keepdims=True)
            ds = p * (dp - dd)
            dsink = -jnp.sum(ps * dd, axis=0, keepdims=True)
            dsink_ref[...] += jnp.where(lane == h, dsink, 0.0)
            dqh = _dot(ds, kh) * (ATT_HD ** -0.5)
            dq_ref[h] = dqh.astype(CDT)
            _acc(dbq_ref.at[h], _colsum(dqh), first)
            dkh = _dot_tn(ds, qh) * (ATT_HD ** -0.5)
            dvh = _dot_tn(p, doh)
            dkv_ref[kvh, rows_prev, :] += dkh[:WINDOW]
            dkv_ref[kvh, rows_cur, :] += dkh[WINDOW:]
            dkv_ref[ATT_KVH + kvh, rows_prev, :] += dvh[:WINDOW]
            dkv_ref[ATT_KVH + kvh, rows_cur, :] += dvh[WINDOW:]

    return pl.pallas_call(
        body, grid=(nb,),
        in_specs=[pl.BlockSpec(memory_space=pltpu.SMEM), cur(ATT_QH), cur(2 * ATT_KVH), prev(2 * ATT_KVH), cur(ATT_QH)],
        out_specs=[cur(ATT_QH), pl.BlockSpec((2 * ATT_KVH, S, ATT_HD), lambda n: (0, 0, 0)),
                   pl.BlockSpec((ATT_QH, 1, ATT_HD), lambda n: (0, 0, 0)), pl.BlockSpec((1, 128), lambda n: (0, 0))],
        out_shape=[_sds((ATT_QH, S, ATT_HD), CDT), _sds((2 * ATT_KVH, S, ATT_HD), F32), _sds((ATT_QH, 1, ATT_HD), F32),
                   _sds((1, 128), F32)],
        compiler_params=_params(("arbitrary",)), name="attn_bwd")(sinks, q4, kv4, kv4, do4)


def _local_step(x, p, target, getw, sm, emit):
    S = x.shape[0]
    vec = lambda a: a.reshape(1, -1)
    ln_g = lambda l, k: vec(sm["ln_gain"][l, k])
    ln_b = lambda l, k: vec(sm["ln_bias"][l, k])
    xb = x.astype(CDT)
    pb = p.astype(CDT)

    proj = _mm_nn(xb, getw("a_w_in", None), (4, S, D), (None, _tile(S), 512), lambda g, i: (g // 2, i, g % 2), F32,
                  name="a_in")
    o_a, y_a, states = _hgrn_fwd(proj, sm["a_lower_bound"], sm["a_norm_gain"])
    zeros = jnp.zeros((1, D), F32)
    z = [[None] * 3 for _ in range(2)]
    xs = [[None] * 3 for _ in range(2)]
    xbs = [[None] * 3 for _ in range(2)]
    z[0][0], xs[0][0], xbs[0][0] = _mixout_ln(y_a[None], getw("a_w_out", y_a)[None], zeros, x, ln_g(0, 0), ln_b(0, 0),
                                              "a_out_ln")
    gu, hid, sgs, ups = [None, None], [None, None], [None, None], [None, None]

    def ffn_ple(l):
        wgu = getw(f"gu{l}", xbs[l][0])
        gu[l], hid[l], z[l][1], xs[l][1], xbs[l][1] = _ffn_fwd(
            xs[l][0], xbs[l][0], wgu, getw(f"dn{l}", None), ln_g(l, 1), ln_b(l, 1), f"ffn_fwd{l}")
        sgs[l], ups[l], z[l][2], xs[l][2], xbs[l][2] = _ple_fwd(
            xs[l][1], xbs[l][1], pb[l], getw(f"pg{l}", None), vec(sm["ple_b_gate"][l]), getw(f"pu{l}", None), ln_g(l, 2),
            ln_b(l, 2), f"ple_fwd{l}")

    ffn_ple(0)
    x3, x3b = xs[0][2], xbs[0][2]
    w_kv, w_q, w_bo = getw("kv_w", x3b), getw("b_w_q", None), getw("b_w_out", None)
    kv4 = _mm_nn(x3b, w_kv, (2 * ATT_KVH, S, ATT_HD), (None, _tile(S), ATT_HD), lambda g, i: (g, i, 0), CDT,
                 bias3=sm["kv_b"].reshape(2 * ATT_KVH, 1, ATT_HD), name="kv_proj")
    q4 = _mm_nn(x3b, w_q, (ATT_QH, S, ATT_HD), (None, _tile(S), ATT_HD), lambda g, i: (g, i, 0), CDT,
                bias3=sm["b_b_q"].reshape(ATT_QH, 1, ATT_HD), name="q_proj")
    o4 = _attn_fwd(q4, kv4, sm["b_sinks"])
    z[1][0], xs[1][0], xbs[1][0] = _mixout_ln(o4, w_bo, sm["b_b_out"], x3, ln_g(1, 0), ln_b(1, 0), "b_out_ln")
    ffn_ple(1)
    loss, dy = _loss_fwd_bwd(xs[1][2], target)

    gs = {}
    d_ln_g = [[None] * 3 for _ in range(2)]
    d_ln_b = [[None] * 3 for _ in range(2)]
    g_bg = [None, None]

    def ffn_ple_bwd(l, dy):
        dx2, dgl, dup, d_ln_g[l][2], d_ln_b[l][2], g_bg[l] = _ple_bwd(dy, z[l][2], sgs[l], ups[l], ln_g(l, 2),
                                                                     getw(f"pg{l}", None), f"ple_bwd{l}")
        g_pg = _wgrad(xbs[l][1][None], dgl[None], f"g_ple_gate{l}")[0]
        g_pu = _wgrad(pb[l][None], dup[None], f"g_ple_up{l}")[0]
        dx1, dzb, dgu, d_ln_g[l][1], d_ln_b[l][1] = _ffn_bwd(dx2, z[l][1], gu[l], getw(f"gu{l}", None),
                                                           getw(f"dn{l}", None), ln_g(l, 1), f"ffn_bwd{l}")
        g_dn = _wgrad(hid[l], dzb[None], f"g_ffn_down{l}")
        g_gu = _wgrad(xbs[l][0][None], dgu.reshape(8, S, FFN_B), f"g_ffn_gate_up{l}")
        return dx1, emit({f"pg{l}": g_pg, f"pu{l}": g_pu, f"dn{l}": g_dn, f"gu{l}": g_gu})

    dx1, tok = ffn_ple_bwd(1, dy)
    dz, dzb, do4, d_ln_g[1][0], d_ln_b[1][0], gs["b_b_out"] = _mixout_bwd(dx1, z[1][0], ln_g(1, 0), w_bo, CDT,
                                                                         "b_out_bwd", after=tok)
    g_bo = _wgrad(o4, dzb[None], "g_b_w_out")
    dq4, dkv4, dbq, dsinks = _attn_bwd(q4, kv4, sm["b_sinks"], do4)
    gs["b_b_q"] = dbq
    gs["b_sinks"] = dsinks
    g_q = _wgrad(x3b[None], dq4, "g_b_w_q")
    g_kv = _wgrad(x3b[None], dkv4.astype(CDT), "g_kv_w")
    tok = emit({"b_w_out": g_bo, "b_w_q": g_q, "kv_w": g_kv})
    dx3, gs["kv_b"] = _qkv_bwd(dz, dq4, dkv4, w_q, w_kv, "qkv_bwd", after=tok)
    dx1, tok = ffn_ple_bwd(0, dx3)
    w_ao = getw("a_w_out", None)
    dz, dzb, dyr, d_ln_g[0][0], d_ln_b[0][0], _ = _mixout_bwd(dx1, z[0][0], ln_g(0, 0), w_ao[None], F32, "a_out_bwd",
                                                              after=tok)
    g_ao = _wgrad(y_a[None], dzb[None], "g_a_w_out")[0]
    dproj, gs["a_lower_bound"], gs["a_norm_gain"] = _hgrn_bwd(proj, sm["a_lower_bound"], sm["a_norm_gain"], o_a, states,
                                                              dyr[0])
    tk = lambda t: (None, t, D)
    g_ain = _mm_tn(xb[None], dproj, N_DEV, lambda g, k: (0, k, 0), lambda g, k: (g // 2, k, g % 2),
                   tk, lambda t: (None, t, 512), (N_DEV, D, 512), (None, D, 512), lambda g, k: (g, 0, 0), name="g_a_w_in")
    tok = emit({"a_w_out": g_ao, "a_w_in": g_ain})
    grad_x = _inproj_bwd(dz, dproj, getw("a_w_in", None), "a_in_bwd", after=tok)
    gs["ple_b_gate"] = jnp.concatenate(g_bg, axis=0)
    gs["ln_gain"] = jnp.stack([jnp.concatenate(r, axis=0) for r in d_ln_g])
    gs["ln_bias"] = jnp.stack([jnp.concatenate(r, axis=0) for r in d_ln_b])
    return loss, grad_x, gs


def _peer(k):
    x, y, c = lax.axis_index("x"), lax.axis_index("y"), lax.axis_index("c")
    px = 1 - x if k & 4 else x
    py = 1 - y if k & 2 else y
    pc = 1 - c if k & 1 else c
    return (px, py, pc), 4 * px + 2 * py + pc


def _my_index():
    return 4 * lax.axis_index("x") + 2 * lax.axis_index("y") + lax.axis_index("c")


def _exchange(srcs, dst_shapes, plan, name):
    n_src, n_piece = len(srcs), len(plan)

    def body(*refs):
        src_refs, dst_refs = refs[:n_src], refs[n_src:n_src + len(dst_shapes)]
        send_sems, recv_sems, local_sems = refs[n_src + len(dst_shapes):]
        me = _my_index()

        def at(ref, idx):
            return ref.at[idx] if idx else ref

        local = []
        for t, (si, sfn, di, dfn) in enumerate(plan):
            cp = pltpu.make_async_copy(at(src_refs[si], sfn(me)), at(dst_refs[di], dfn(me)), local_sems.at[t])
            cp.start()
            local.append(cp)
        sends = []
        for k in range(1, N_DEV):
            peer, pid = _peer(k)
            for t, (si, sfn, di, dfn) in enumerate(plan):
                cp = pltpu.make_async_remote_copy(
                    src_ref=at(src_refs[si], sfn(pid)), dst_ref=at(dst_refs[di], dfn(me)),
                    send_sem=send_sems.at[t * 7 + k - 1], recv_sem=recv_sems.at[t * 7 + k - 1],
                    device_id=peer, device_id_type=MESH)
                cp.start()
                sends.append(cp)
        for k in range(1, N_DEV):
            peer, pid = _peer(k)
            for t, (si, sfn, di, dfn) in enumerate(plan):
                pltpu.make_async_remote_copy(
                    src_ref=at(src_refs[si], sfn(me)), dst_ref=at(dst_refs[di], dfn(pid)),
                    send_sem=send_sems.at[t * 7 + k - 1], recv_sem=recv_sems.at[t * 7 + k - 1],
                    device_id=peer, device_id_type=MESH).wait_recv()
        for cp in sends:
            cp.wait_send()
        for cp in local:
            cp.wait()

    hbm = pl.BlockSpec(memory_space=pltpu.HBM)
    return pl.pallas_call(
        body, in_specs=[hbm] * n_src, out_specs=[hbm] * len(dst_shapes), out_shape=dst_shapes,
        scratch_shapes=[pltpu.SemaphoreType.DMA((7 * n_piece,)), pltpu.SemaphoreType.DMA((7 * n_piece,)),
                        pltpu.SemaphoreType.DMA((n_piece,))],
        name=name)(*srcs)


def _gather(shards, name):
    dsts = [_sds((N_DEV,) + a.shape, a.dtype) for a in shards]
    plan = [(i, lambda j: (), i, lambda s: (s,)) for i in range(len(shards))]
    return _exchange(shards, dsts, plan, name)


_HBM = pl.BlockSpec(memory_space=pltpu.HBM)
_SEM = pl.BlockSpec(memory_space=pltpu.SEMAPHORE)
_DATAFLOW = pltpu.SideEffectType.DATAFLOW_SIDE_EFFECTING


def _piece_copy(mode, src, land, send_sems, recv_sems, t, k, sender, receiver, peer):
    return pltpu.make_async_remote_copy(
        src_ref=src if mode == "gather" else src.at[receiver], dst_ref=land.at[sender],
        send_sem=send_sems.at[t * 7 + k - 1], recv_sem=recv_sems.at[t * 7 + k - 1], device_id=peer, device_id_type=MESH)


def _xstart(groups, mode, name, after=None):
    flat = [a for g in groups for a in g]
    n, ng = len(flat), len(groups)
    land_shapes = [((N_DEV,) + a.shape) if mode == "gather" else a.shape for a in flat]
    first = [sum(len(g) for g in groups[:i]) for i in range(ng)]
    extra = [] if after is None else [after]

    def body(*refs):
        srcs, lands = refs[:n], refs[n:2 * n]
        sems = refs[2 * n + len(extra):2 * n + len(extra) + 2 * ng]
        tok_ref, local_sems = refs[-2], refs[-1]
        me = _my_index()
        local = []
        for i in range(n):
            cp = pltpu.make_async_copy(srcs[i] if mode == "gather" else srcs[i].at[me], lands[i].at[me], local_sems.at[i])
            cp.start()
            local.append(cp)
        for cp in local:
            cp.wait()
        for gi, g in enumerate(groups):
            for k in range(1, N_DEV):
                peer, pid = _peer(k)
                for t in range(len(g)):
                    i = first[gi] + t
                    _piece_copy(mode, srcs[i], lands[i], sems[2 * gi], sems[2 * gi + 1], t, k, me, pid, peer).start()
        tok_ref[...] = jnp.zeros_like(tok_ref)

    sem_shapes = []
    for g in groups:
        sem_shapes += [pltpu.SemaphoreType.DMA((7 * len(g),))] * 2
    thru = [pltpu.HBM(a.shape, a.dtype) for a in flat] + [pltpu.HBM(s, a.dtype) for s, a in zip(land_shapes, flat)]
    outs = pl.pallas_call(
        body, in_specs=[_HBM] * (2 * n) + [pl.BlockSpec(memory_space=pl.ANY)] * len(extra),
        out_specs=[_SEM] * (2 * ng) + [_HBM] * (2 * n) + [pl.BlockSpec(memory_space=pltpu.VMEM)],
        out_shape=sem_shapes + thru + [_sds((8, 128), F32)],
        input_output_aliases={i: 2 * ng + i for i in range(2 * n)},
        scratch_shapes=[pltpu.SemaphoreType.DMA((n,))],
        compiler_params=pltpu.CompilerParams(has_side_effects=_DATAFLOW), name=name)(
            *[pltpu.with_memory_space_constraint(a, pltpu.HBM) for a in flat],
            *[pltpu.with_memory_space_constraint(lax.empty(s, a.dtype), pltpu.HBM) for s, a in zip(land_shapes, flat)], *extra)
    sems, srcs_thru, lands_thru = outs[:2 * ng], outs[2 * ng:2 * ng + n], outs[2 * ng + n:2 * ng + 2 * n]
    handles = [(sems[2 * gi], sems[2 * gi + 1], srcs_thru[first[gi]:first[gi] + len(g)],
                lands_thru[first[gi]:first[gi] + len(g)]) for gi, g in enumerate(groups)]
    return handles, outs[-1]


def _xwait(handle, mode, after, name):
    send_sems, recv_sems, srcs_thru, lands_thru = handle
    n = len(srcs_thru)

    def body(*refs):
        srcs, lands, send, recv = refs[:n], refs[n:2 * n], refs[2 * n], refs[2 * n + 1]
        me = _my_index()
        for k in range(1, N_DEV):
            peer, pid = _peer(k)
            for t in range(n):
                _piece_copy(mode, srcs[t], lands[t], send, recv, t, k, pid, me, peer).wait_recv()
        for k in range(1, N_DEV):
            peer, pid = _peer(k)
            for t in range(n):
                _piece_copy(mode, srcs[t], lands[t], send, recv, t, k, me, pid, peer).wait_send()

    extra = [] if after is None else [after]
    outs = pl.pallas_call(
        body, in_specs=[_HBM] * (2 * n) + [_SEM, _SEM] + [pl.BlockSpec(memory_space=pl.ANY)] * len(extra),
        out_specs=[_HBM] * (2 * n),
        out_shape=[pltpu.HBM(a.shape, a.dtype) for a in list(srcs_thru) + list(lands_thru)],
        input_output_aliases={i: i for i in range(2 * n)},
        compiler_params=pltpu.CompilerParams(has_side_effects=_DATAFLOW), name=name)(
            *srcs_thru, *lands_thru, send_sems, recv_sems, *extra)
    return outs[n:]


def _adamw(w, g, m, v):
    m = ADAM_B1 * m + (1.0 - ADAM_B1) * g
    v = ADAM_B2 * v + (1.0 - ADAM_B2) * (g * g)
    m_hat = m / (1.0 - ADAM_B1 ** ADAM_STEP)
    v_hat = v / (1.0 - ADAM_B2 ** ADAM_STEP)
    delta = -ADAM_LR * (m_hat / (jnp.sqrt(v_hat) + ADAM_EPS) + ADAM_WD * w)
    return delta, m, v


def _adam_big(w, parts, m, v, name):
    L, R, C = w.shape
    tr = _tile(R, (256, 128, 176, 64, 32, 16))
    nr = R // tr

    def body(w_ref, *refs):
        p_refs, (m_ref, v_ref, g_ref, d_ref, mo_ref, vo_ref) = refs[:L], refs[L:]
        for l in range(L):
            @pl.when(pl.program_id(0) == l)
            def _(p_ref=p_refs[l]):
                g = p_ref[0].astype(F32)
                for s in range(1, N_DEV):
                    g = g + p_ref[s].astype(F32)
                g_ref[...] = g
                d_ref[...], mo_ref[...], vo_ref[...] = _adamw(w_ref[...], g, m_ref[...], v_ref[...])

    row = pl.BlockSpec((None, tr, C), lambda l, i: (l, i, 0))
    park = lambda l_of: (lambda l, i: (0, jnp.where(l == l_of, i, 0 if l_of else nr - 1), 0))
    return pl.pallas_call(
        body, grid=(L, nr),
        in_specs=[row] + [pl.BlockSpec((N_DEV, tr, C), park(l)) for l in range(L)] + [row, row],
        out_specs=[row] * 4, out_shape=[_sds((L, R, C), F32)] * 4,
        compiler_params=_params(("arbitrary", "arbitrary")), name=name)(w, *parts, m, v)


SMALL = (("a_lower_bound", 2, True), ("ln_gain", 6, True), ("ln_bias", 6, True), ("a_norm_gain", 1, False),
         ("kv_b", 4, False), ("b_b_q", 8, False), ("b_sinks", 1, False), ("b_b_out", 8, False), ("ple_b_gate", 16, False))
SUBLANES = 8


def _slot(r):
    return -(-r // SUBLANES) * SUBLANES


SMALL_ROWS = sum(_slot(r) for _, r, _ in SMALL)
PART_ROWS = sum(r * N_DEV if sh else _slot(r) for _, r, sh in SMALL)


def _pack_rows(a, rows):
    flat = a.reshape(-1)
    return jnp.pad(flat, (0, rows * 128 - flat.size)).reshape(rows, 128)


def _pack_small(d):
    return jnp.concatenate([_pack_rows(d[n], _slot(r)) for n, r, _ in SMALL], axis=0)


def _unpack_small(packed, like):
    out, r0 = {}, 0
    for n, r, _ in SMALL:
        out[n] = packed[r0:r0 + r].reshape(-1)[:like[n].size].reshape(like[n].shape)
        r0 += _slot(r)
    return out


def _pack_partials(gs):
    blocks = []
    for n, r, sharded in SMALL:
        if sharded:
            blocks.append(gs[n].reshape(r * N_DEV, 128))
        else:
            blocks.append(_pack_rows(gs[n], _slot(r)))
    return jnp.concatenate(blocks, axis=0)


def _adam_small(parts, w, m, v):
    def body(p_ref, w_ref, m_ref, v_ref, g_ref, d_ref, mo_ref, vo_ref, tot):
        me = _my_index()
        t = p_ref[0]
        for s in range(1, N_DEV):
            t = t + p_ref[s]
        tot[...] = t
        g_ref[...] = jnp.zeros_like(g_ref)
        src, dst = 0, 0
        for _, r, sharded in SMALL:
            if sharded:
                for i in range(r):
                    g_ref[pl.ds(dst + i, 1), :] = tot[pl.ds(src + i * N_DEV + me, 1), :]
                src += r * N_DEV
            else:
                g_ref[pl.ds(dst, _slot(r)), :] = tot[pl.ds(src, _slot(r)), :]
                src += _slot(r)
            dst += _slot(r)
        d_ref[...], mo_ref[...], vo_ref[...] = _adamw(w_ref[...], g_ref[...], m_ref[...], v_ref[...])

    full = pl.BlockSpec((SMALL_ROWS, 128), lambda: (0, 0))
    return pl.pallas_call(
        body, in_specs=[pl.BlockSpec((N_DEV, PART_ROWS, 128), lambda: (0, 0, 0)), full, full, full],
        out_specs=[full] * 4, out_shape=[_sds((SMALL_ROWS, 128), F32)] * 4,
        scratch_shapes=[pltpu.VMEM((PART_ROWS, 128), F32)], name="adam_small")(parts, w, m, v)


WEIGHTS = ("a_w_in", "a_lower_bound", "a_norm_gain", "a_w_out", "kv_w", "kv_b", "b_w_q", "b_b_q", "b_sinks", "b_w_out",
           "b_b_out", "ffn_w_gate_up", "ffn_w_down", "ple_w_up", "ple_w_gate", "ple_b_gate", "ln_gain", "ln_bias")


def _heads_in(w, heads):
    return w.reshape(D, heads, ATT_HD).transpose(1, 0, 2)


def _heads_out(g):
    return g.transpose(1, 0, 2).reshape(D, -1)


GATHER_GROUPS = (("a_w_in",), ("a_w_out", "gu0", "dn0", "pu0", "pg0"), ("kv_w", "b_w_q", "b_w_out"),
                 ("gu1", "dn1", "pu1", "pg1"))
KERNEL_LAYOUT = {
    "a_w_in": lambda a: a,
    "a_w_out": lambda a: a.reshape(D, D),
    "kv_w": lambda a: _heads_in(a.reshape(D, 2 * ATT_KVH * ATT_HD), 2 * ATT_KVH),
    "b_w_q": lambda a: _heads_in(a.reshape(D, D), ATT_QH),
    "b_w_out": lambda a: a.reshape(ATT_QH, ATT_HD, D),
    "gu": lambda a: a.reshape(2, 4, D, FFN_B),
    "dn": lambda a: a.reshape(4, FFN_B, D),
    "pu": lambda a: a.transpose(1, 0, 2).reshape(PLE_DIM, D),
    "pg": lambda a: a.reshape(D, D),
}
_row_blocks = lambda a: a.reshape(N_DEV, -1, a.shape[-1])
OWNER_BLOCKS = {
    "a_w_in": lambda g: g,
    "a_w_out": _row_blocks,
    "kv_w": lambda g: _row_blocks(_heads_out(g)),
    "b_w_q": lambda g: _row_blocks(_heads_out(g)),
    "b_w_out": lambda g: _row_blocks(g.reshape(D, D)),
    "gu": lambda g: g,
    "dn": lambda g: _row_blocks(g.reshape(FFN_H, D)),
    "pu": lambda g: g.reshape(PLE_DIM, N_DEV, 128).transpose(1, 0, 2),
    "pg": _row_blocks,
}


def kernel(x, p, a_w_in, a_lower_bound, a_norm_gain, a_w_out, kv_w, kv_b, b_w_q, b_b_q, b_sinks, b_w_out, b_b_out, ffn_w_gate_up, ffn_w_down, ple_w_up, ple_w_gate, ple_b_gate, ln_gain, ln_bias, loss_target, m_a_w_in, m_a_lower_bound, m_a_norm_gain, m_a_w_out, m_kv_w, m_kv_b, m_b_w_q, m_b_b_q, m_b_sinks, m_b_w_out, m_b_b_out, m_ffn_w_gate_up, m_ffn_w_down, m_ple_w_up, m_ple_w_gate, m_ple_b_gate, m_ln_gain, m_ln_bias, v_a_w_in, v_a_lower_bound, v_a_norm_gain, v_a_w_out, v_kv_w, v_kv_b, v_b_w_q, v_b_b_q, v_b_sinks, v_b_w_out, v_b_b_out, v_ffn_w_gate_up, v_ffn_w_down, v_ple_w_up, v_ple_w_gate, v_ple_b_gate, v_ln_gain, v_ln_bias):
    given = dict(locals())
    w = {n: given[n] for n in WEIGHTS}
    m = {n: given["m_" + n] for n in WEIGHTS}
    v = {n: given["v_" + n] for n in WEIGHTS}
    small_sharded = jnp.concatenate([_pack_rows(a, SUBLANES) for a in (a_lower_bound, ln_gain, ln_bias)], axis=0)
    (g_small,) = _gather([small_sharded], "gather_small")
    shards = {"a_w_in": a_w_in[0], "a_w_out": a_w_out[0], "kv_w": kv_w, "b_w_q": b_w_q[0], "b_w_out": b_w_out[0]}
    for l in range(2):
        shards.update({f"gu{l}": ffn_w_gate_up[l], f"dn{l}": ffn_w_down[l], f"pu{l}": ple_w_up[l], f"pg{l}": ple_w_gate[l]})
    gather_handles, _ = _xstart([[shards[n].astype(CDT) for n in g] for g in GATHER_GROUPS], "gather", "gather_start",
                                after=g_small)
    gathered = {}

    def getw(key, after):
        if key not in gathered:
            gi = [key in g for g in GATHER_GROUPS].index(True)
            lands = _xwait(gather_handles[gi], "gather", after, f"gather_wait{gi}")
            for n, a in zip(GATHER_GROUPS[gi], lands):
                gathered[n] = KERNEL_LAYOUT[n.rstrip("01")](a)
        return gathered[key]

    full_rows = lambda r0, r: g_small[:, r0:r0 + r].transpose(1, 0, 2).reshape(r, D)
    sm = {"a_lower_bound": full_rows(0, 2), "ln_gain": full_rows(SUBLANES, 6).reshape(2, 3, D),
          "ln_bias": full_rows(2 * SUBLANES, 6).reshape(2, 3, D), "a_norm_gain": a_norm_gain, "kv_b": kv_b,
          "b_b_q": b_b_q[0], "b_sinks": b_sinks, "b_b_out": b_b_out, "ple_b_gate": ple_b_gate}

    scatters = []

    def emit(grads):
        names = list(grads)
        (handle,), token = _xstart([[OWNER_BLOCKS[n.rstrip("01")](grads[n]) for n in names]], "scatter",
                                   f"scatter_start{len(scatters)}")
        scatters.append((names, handle))
        return token

    loss, grad_x, gs = _local_step(x[0], p[:, 0], loss_target[0], getw, sm, emit)

    (parts_small,) = _gather([_pack_partials(gs)], "gather_small_grads")
    packed = _adam_small(parts_small, _pack_small(w), _pack_small(m), _pack_small(v))
    small_out = [_unpack_small(a, w) for a in packed]
    out = {n: [s[n] for s in small_out] for n, _, _ in SMALL}

    parts, last = {}, grad_x
    adam_after = {1: (("kv_w", "kv_w"), ("b_w_q", "b_w_q"), ("b_w_out", "b_w_out")),
                  2: (("ffn_w_gate_up", "gu"), ("ffn_w_down", "dn"), ("ple_w_up", "pu"), ("ple_w_gate", "pg")),
                  3: (("a_w_out", "a_w_out"), ("a_w_in", "a_w_in"))}
    for i, (names, handle) in enumerate(scatters):
        parts.update(zip(names, _xwait(handle, "scatter", last, f"scatter_wait{i}")))
        for n, key in adam_after.get(i, ()):
            lrc = (1,) * (3 - w[n].ndim) + w[n].shape
            layers = [parts[key]] if key in parts else [parts[key + "0"], parts[key + "1"]]
            res = _adam_big(w[n].reshape(lrc), layers, m[n].reshape(lrc), v[n].reshape(lrc), "adam_" + n)
            out[n] = [r.reshape(w[n].shape) for r in res]
            last = res[3]

    loss = lax.psum(loss[0, 0], ("x", "y", "c"))
    res = [loss, grad_x[None]]
    for i in range(4):
        res += [out[n][i] for n in WEIGHTS]
    return tuple(res)
```

```python
import jax
import jax.numpy as jnp
from jax import lax
from jax.experimental import pallas as pl
from jax.experimental.pallas import tpu as pltpu

F32 = jnp.float32
CDT = jnp.bfloat16

N_DEV = 8
D = 1024
HG_H, HG_DK, HG_CH = 8, 128, 64
HG_HPB = 4
ATT_HD, ATT_QH, ATT_KVH, ATT_G, WINDOW = 64, 16, 4, 4, 128
FFN_H = 2816
FFN_B = FFN_H // 4
PLE_DIM = 256
ALPHA = (2.0 * 2) ** 0.25
LN_EPS = 1e-5
RMS_EPS = 1e-6
ADAM_LR, ADAM_B1, ADAM_B2, ADAM_EPS, ADAM_WD, ADAM_STEP = 0.001, 0.9, 0.999, 1e-08, 0.01, 10
ROW_TILES = (256, 128, 64)
VMEM_LIMIT = 48 * 1024 * 1024
NEG = -1e30

MESH = pl.DeviceIdType.MESH


def _tile(n, cands=ROW_TILES):
    for t in cands:
        if n % t == 0:
            return t
    return n


def _sds(shape, dtype):
    return jax.ShapeDtypeStruct(tuple(shape), dtype)


def _params(sem):
    return pltpu.CompilerParams(dimension_semantics=sem, vmem_limit_bytes=VMEM_LIMIT)


def _dot(a, b):
    return jnp.dot(a.astype(CDT), b.astype(CDT), preferred_element_type=F32)


def _dot_nt(a, b):
    return lax.dot_general(a.astype(CDT), b.astype(CDT), (((1,), (1,)), ((), ())), preferred_element_type=F32)


def _dot_tn(a, b):
    return lax.dot_general(a.astype(CDT), b.astype(CDT), (((0,), (0,)), ((), ())), preferred_element_type=F32)


def _sigmoid(x):
    return jax.nn.sigmoid(x)


def _ln_fwd(z, g, b):
    mu = jnp.mean(z, axis=-1, keepdims=True)
    zc = z - mu
    var = jnp.mean(zc * zc, axis=-1, keepdims=True)
    return zc * lax.rsqrt(var + LN_EPS) * g + b


def _ln_bwd(z, g, dy):
    mu = jnp.mean(z, axis=-1, keepdims=True)
    zc = z - mu
    var = jnp.mean(zc * zc, axis=-1, keepdims=True)
    rstd = lax.rsqrt(var + LN_EPS)
    xhat = zc * rstd
    dxh = dy * g
    dz = rstd * (dxh - jnp.mean(dxh, axis=-1, keepdims=True) - xhat * jnp.mean(dxh * xhat, axis=-1, keepdims=True))
    return dz, xhat


def _colsum(x):
    return jnp.sum(x, axis=0, keepdims=True)


def _acc(ref, val, first):
    @pl.when(first)
    def _():
        ref[...] = val

    @pl.when(jnp.logical_not(first))
    def _():
        ref[...] += val


def _call(after, body, **kw):
    if after is None:
        return pl.pallas_call(body, **kw)
    kw["in_specs"] = [pl.BlockSpec(memory_space=pl.ANY)] + list(kw["in_specs"])

    def ordered_body(after_ref, *refs):
        body(*refs)

    call = pl.pallas_call(ordered_body, **kw)
    return lambda *args: call(after, *args)


def _mm_nn(a, b3, out_shape, oblock, omap, out_dtype, bias3=None, name="mm_nn"):
    M, K = a.shape
    G, _, Nb = b3.shape
    tm = _tile(M)

    def body(a_ref, b_ref, *rest):
        o_ref = rest[-1]
        acc = _dot(a_ref[...], b_ref[...])
        if bias3 is not None:
            acc = acc + rest[0][...]
        o_ref[...] = acc.astype(o_ref.dtype)

    in_specs = [pl.BlockSpec((tm, K), lambda g, i: (i, 0)), pl.BlockSpec((None, K, Nb), lambda g, i: (g, 0, 0))]
    args = [a, b3]
    if bias3 is not None:
        in_specs.append(pl.BlockSpec((None, 1, Nb), lambda g, i: (g, 0, 0)))
        args.append(bias3)
    return pl.pallas_call(
        body, grid=(G, M // tm), in_specs=in_specs, out_specs=pl.BlockSpec(oblock, omap),
        out_shape=_sds(out_shape, out_dtype), compiler_params=_params(("arbitrary", "arbitrary")), name=name)(*args)


def _mm_tn(a3, b3, G, amap, bmap, ablock, bblock, out_shape, oblock, omap, name="mm_tn"):
    S = a3.shape[1]
    tk = _tile(S, (512, 256, 128))
    Mo, No = oblock[-2], oblock[-1]

    def body(a_ref, b_ref, o_ref, acc):
        k = pl.program_id(1)
        _acc(acc, _dot_tn(a_ref[...], b_ref[...]), k == 0)

        @pl.when(k == pl.num_programs(1) - 1)
        def _():
            o_ref[...] = acc[...].astype(o_ref.dtype)

    return pl.pallas_call(
        body, grid=(G, S // tk),
        in_specs=[pl.BlockSpec(ablock(tk), amap), pl.BlockSpec(bblock(tk), bmap)],
        out_specs=pl.BlockSpec(oblock, omap), out_shape=_sds(out_shape, CDT),
        scratch_shapes=[pltpu.VMEM((Mo, No), F32)],
        compiler_params=_params(("arbitrary", "arbitrary")), name=name)(a3, b3)


def _wgrad(a3, b3, name):
    Ga, S, M = a3.shape
    Gb, _, N = b3.shape
    G = max(Ga, Gb)
    return _mm_tn(
        a3, b3, G,
        (lambda g, k: (g, k, 0)) if Ga > 1 else (lambda g, k: (0, k, 0)),
        (lambda g, k: (g, k, 0)) if Gb > 1 else (lambda g, k: (0, k, 0)),
        lambda tk: (None, tk, M), lambda tk: (None, tk, N),
        (G, M, N), (None, M, N), lambda g, k: (g, 0, 0), name=name)


def _mixout_ln(u3, w3, bias, xin, gain, beta, name):
    G, S, Kb = u3.shape
    tm = _tile(S)

    def body(u_ref, w_ref, b_ref, x_ref, g_ref, be_ref, z_ref, xo_ref, xob_ref):
        h = b_ref[...] + _dot(u_ref[0], w_ref[0])
        for g in range(1, G):
            h = h + _dot(u_ref[g], w_ref[g])
        z = ALPHA * x_ref[...] + h
        z_ref[...] = z
        y = _ln_fwd(z, g_ref[...], be_ref[...])
        xo_ref[...] = y
        xob_ref[...] = y.astype(CDT)

    row = pl.BlockSpec((tm, D), lambda i: (i, 0))
    vec = pl.BlockSpec((1, D), lambda i: (0, 0))
    return pl.pallas_call(
        body, grid=(S // tm,),
        in_specs=[pl.BlockSpec((G, tm, Kb), lambda i: (0, i, 0)), pl.BlockSpec((G, Kb, D), lambda i: (0, 0, 0)),
                  vec, row, vec, vec],
        out_specs=[row, row, row], out_shape=[_sds((S, D), F32), _sds((S, D), F32), _sds((S, D), CDT)],
        compiler_params=_params(("arbitrary",)), name=name)(u3, w3, bias, xin, gain, beta)


def _ffn_fwd(xin, xin_b, wgu, wdn, gain, beta, name):
    S = xin.shape[0]
    tm = _tile(S)

    def body(x_ref, xb_ref, wgu_ref, wdn_ref, g_ref, be_ref, gu_ref, hid_ref, z_ref, xo_ref, xob_ref, acc):
        j = pl.program_id(1)
        xb = xb_ref[...]
        gate = _dot(xb, wgu_ref[0])
        up = _dot(xb, wgu_ref[1])
        gu_ref[0] = gate
        gu_ref[1] = up
        hid = (gate * _sigmoid(gate) * up).astype(CDT)
        hid_ref[...] = hid
        _acc(acc, _dot(hid, wdn_ref[...]), j == 0)

        @pl.when(j == 3)
        def _():
            z = ALPHA * x_ref[...] + acc[...]
            z_ref[...] = z
            y = _ln_fwd(z, g_ref[...], be_ref[...])
            xo_ref[...] = y
            xob_ref[...] = y.astype(CDT)

    row = pl.BlockSpec((tm, D), lambda i, j: (i, 0))
    vec = pl.BlockSpec((1, D), lambda i, j: (0, 0))
    return pl.pallas_call(
        body, grid=(S // tm, 4),
        in_specs=[row, row, pl.BlockSpec((2, None, D, FFN_B), lambda i, j: (0, j, 0, 0)),
                  pl.BlockSpec((None, FFN_B, D), lambda i, j: (j, 0, 0)), vec, vec],
        out_specs=[pl.BlockSpec((2, None, tm, FFN_B), lambda i, j: (0, j, i, 0)),
                   pl.BlockSpec((None, tm, FFN_B), lambda i, j: (j, i, 0)), row, row, row],
        out_shape=[_sds((2, 4, S, FFN_B), F32), _sds((4, S, FFN_B), CDT), _sds((S, D), F32), _sds((S, D), F32),
                   _sds((S, D), CDT)],
        scratch_shapes=[pltpu.VMEM((tm, D), F32)],
        compiler_params=_params(("arbitrary", "arbitrary")), name=name)(xin, xin_b, wgu, wdn, gain, beta)


def _ple_fwd(xin, xin_b, p_b, wpg, bgate, wpu, gain, beta, name):
    S = xin.shape[0]
    tm = _tile(S)

    def body(x_ref, xb_ref, p_ref, wpg_ref, bg_ref, wpu_ref, g_ref, be_ref, sg_ref, up_ref, z_ref, xo_ref, xob_ref):
        sg = _sigmoid(_dot(xb_ref[...], wpg_ref[...]) + bg_ref[...])
        up = _dot(p_ref[...], wpu_ref[...])
        sg_ref[...] = sg
        up_ref[...] = up
        z = ALPHA * x_ref[...] + sg * up
        z_ref[...] = z
        y = _ln_fwd(z, g_ref[...], be_ref[...])
        xo_ref[...] = y
        xob_ref[...] = y.astype(CDT)

    row = pl.BlockSpec((tm, D), lambda i: (i, 0))
    vec = pl.BlockSpec((1, D), lambda i: (0, 0))
    return pl.pallas_call(
        body, grid=(S // tm,),
        in_specs=[row, row, pl.BlockSpec((tm, PLE_DIM), lambda i: (i, 0)), pl.BlockSpec((D, D), lambda i: (0, 0)), vec,
                  pl.BlockSpec((PLE_DIM, D), lambda i: (0, 0)), vec, vec],
        out_specs=[row] * 5,
        out_shape=[_sds((S, D), F32)] * 4 + [_sds((S, D), CDT)],
        compiler_params=_params(("arbitrary",)), name=name)(xin, xin_b, p_b, wpg, bgate, wpu, gain, beta)


def _loss_fwd_bwd(y, target):
    S = y.shape[0]
    tm = _tile(S)

    def body(y_ref, t_ref, l_ref, dy_ref):
        e = y_ref[...] - t_ref[...]
        dy_ref[...] = e * (1.0 / D)
        part = 0.5 * jnp.sum(jnp.sum(e * e, axis=-1, keepdims=True) * (1.0 / D), axis=0, keepdims=True)
        _acc(l_ref, part, pl.program_id(0) == 0)

    row = pl.BlockSpec((tm, D), lambda i: (i, 0))
    return pl.pallas_call(
        body, grid=(S // tm,), in_specs=[row, row],
        out_specs=[pl.BlockSpec((1, 1), lambda i: (0, 0)), row],
        out_shape=[_sds((1, 1), F32), _sds((S, D), F32)],
        compiler_params=_params(("arbitrary",)), name="loss")(y, target)


def _ple_bwd(dy, z, sg, up, gain, wpg, name, after=None):
    S = dy.shape[0]
    tm = _tile(S)

    def body(dy_ref, z_ref, sg_ref, up_ref, g_ref, wpg_ref, dx_ref, dgl_ref, dup_ref, dgain_ref, dbeta_ref, dbg_ref):
        first = pl.program_id(0) == 0
        dy_ = dy_ref[...]
        dz, xhat = _ln_bwd(z_ref[...], g_ref[...], dy_)
        sg_ = sg_ref[...]
        dgl = dz * up_ref[...] * sg_ * (1.0 - sg_)
        dgl_ref[...] = dgl.astype(CDT)
        dup_ref[...] = (dz * sg_).astype(CDT)
        dx_ref[...] = ALPHA * dz + _dot_nt(dgl, wpg_ref[...])
        _acc(dgain_ref, _colsum(dy_ * xhat), first)
        _acc(dbeta_ref, _colsum(dy_), first)
        _acc(dbg_ref, _colsum(dgl), first)

    row = pl.BlockSpec((tm, D), lambda i: (i, 0))
    vec = pl.BlockSpec((1, D), lambda i: (0, 0))
    return _call(
        after, body, grid=(S // tm,), in_specs=[row, row, row, row, vec, pl.BlockSpec((D, D), lambda i: (0, 0))],
        out_specs=[row, row, row, vec, vec, vec],
        out_shape=[_sds((S, D), F32), _sds((S, D), CDT), _sds((S, D), CDT)] + [_sds((1, D), F32)] * 3,
        compiler_params=_params(("arbitrary",)), name=name)(dy, z, sg, up, gain, wpg)


def _ffn_bwd(dy, z, gu, wgu, wdn, gain, name, after=None):
    S = dy.shape[0]
    tm = _tile(S)

    def body(dy_ref, z_ref, gu_ref, wgu_ref, wdn_ref, g_ref, dx_ref, dzb_ref, dgu_ref, dgain_ref, dbeta_ref,
             dz_scr, acc):
        i, j = pl.program_id(0), pl.program_id(1)

        @pl.when(j == 0)
        def _():
            dy_ = dy_ref[...]
            dz, xhat = _ln_bwd(z_ref[...], g_ref[...], dy_)
            dz_scr[...] = dz
            dzb_ref[...] = dz.astype(CDT)
            _acc(dgain_ref, _colsum(dy_ * xhat), i == 0)
            _acc(dbeta_ref, _colsum(dy_), i == 0)

        dhid = _dot_nt(dz_scr[...], wdn_ref[...])
        gate, up = gu_ref[0], gu_ref[1]
        sg = _sigmoid(gate)
        dgate = (dhid * up * (sg * (1.0 + gate * (1.0 - sg)))).astype(CDT)
        dup = (dhid * (gate * sg)).astype(CDT)
        dgu_ref[0] = dgate
        dgu_ref[1] = dup
        _acc(acc, _dot_nt(dgate, wgu_ref[0]) + _dot_nt(dup, wgu_ref[1]), j == 0)

        @pl.when(j == 3)
        def _():
            dx_ref[...] = ALPHA * dz_scr[...] + acc[...]

    row = pl.BlockSpec((tm, D), lambda i, j: (i, 0))
    vec = pl.BlockSpec((1, D), lambda i, j: (0, 0))
    return _call(
        after, body, grid=(S // tm, 4),
        in_specs=[row, row, pl.BlockSpec((2, None, tm, FFN_B), lambda i, j: (0, j, i, 0)),
                  pl.BlockSpec((2, None, D, FFN_B), lambda i, j: (0, j, 0, 0)),
                  pl.BlockSpec((None, FFN_B, D), lambda i, j: (j, 0, 0)), vec],
        out_specs=[row, row, pl.BlockSpec((2, None, tm, FFN_B), lambda i, j: (0, j, i, 0)), vec, vec],
        out_shape=[_sds((S, D), F32), _sds((S, D), CDT), _sds((2, 4, S, FFN_B), CDT), _sds((1, D), F32),
                   _sds((1, D), F32)],
        scratch_shapes=[pltpu.VMEM((tm, D), F32), pltpu.VMEM((tm, D), F32)],
        compiler_params=_params(("arbitrary", "arbitrary")), name=name)(dy, z, gu, wgu, wdn, gain)


def _mixout_bwd(dy, z, gain, w3, du_dtype, name, after=None):
    S = dy.shape[0]
    G, Kb, _ = w3.shape
    tm = _tile(S)

    def body(dy_ref, z_ref, g_ref, w_ref, dz_ref, dzb_ref, du_ref, dgain_ref, dbeta_ref, dbias_ref):
        first = pl.program_id(0) == 0
        dy_ = dy_ref[...]
        dz, xhat = _ln_bwd(z_ref[...], g_ref[...], dy_)
        dz_ref[...] = dz
        dzb = dz.astype(CDT)
        dzb_ref[...] = dzb
        for g in range(G):
            du_ref[g] = _dot_nt(dzb, w_ref[g]).astype(du_ref.dtype)
        _acc(dgain_ref, _colsum(dy_ * xhat), first)
        _acc(dbeta_ref, _colsum(dy_), first)
        _acc(dbias_ref, _colsum(dz), first)

    row = pl.BlockSpec((tm, D), lambda i: (i, 0))
    vec = pl.BlockSpec((1, D), lambda i: (0, 0))
    return _call(
        after, body, grid=(S // tm,), in_specs=[row, row, vec, pl.BlockSpec((G, Kb, D), lambda i: (0, 0, 0))],
        out_specs=[row, row, pl.BlockSpec((G, tm, Kb), lambda i: (0, i, 0)), vec, vec, vec],
        out_shape=[_sds((S, D), F32), _sds((S, D), CDT), _sds((G, S, Kb), du_dtype)] + [_sds((1, D), F32)] * 3,
        compiler_params=_params(("arbitrary",)), name=name)(dy, z, gain, w3)


def _qkv_bwd(dz, dq4, dkv4, wq4, wkv4, name, after=None):
    S = dz.shape[0]
    tm = _tile(S)
    HQ, HK = dq4.shape[0], dkv4.shape[0]

    def body(dz_ref, dq_ref, dkv_ref, wq_ref, wkv_ref, dx_ref, dkvb_ref):
        first = pl.program_id(0) == 0
        acc = ALPHA * dz_ref[...]
        for h in range(HQ):
            acc = acc + _dot_nt(dq_ref[h], wq_ref[h])
        for h in range(HK):
            acc = acc + _dot_nt(dkv_ref[h], wkv_ref[h])
        dx_ref[...] = acc
        for h in range(HK):
            _acc(dkvb_ref.at[h], _colsum(dkv_ref[h]), first)

    row = pl.BlockSpec((tm, D), lambda i: (i, 0))
    return _call(
        after, body, grid=(S // tm,),
        in_specs=[row, pl.BlockSpec((HQ, tm, ATT_HD), lambda i: (0, i, 0)), pl.BlockSpec((HK, tm, ATT_HD), lambda i: (0, i, 0)),
                  pl.BlockSpec((HQ, D, ATT_HD), lambda i: (0, 0, 0)), pl.BlockSpec((HK, D, ATT_HD), lambda i: (0, 0, 0))],
        out_specs=[row, pl.BlockSpec((HK, 1, ATT_HD), lambda i: (0, 0, 0))],
        out_shape=[_sds((S, D), F32), _sds((HK, 1, ATT_HD), F32)],
        compiler_params=_params(("arbitrary",)), name=name)(dz, dq4, dkv4, wq4, wkv4)


def _inproj_bwd(dz, dproj, wain, name, after=None):
    S = dz.shape[0]
    tm = _tile(S)
    nb = wain.shape[-1]

    def body(dz_ref, dp_ref, w_ref, dx_ref, acc):
        j = pl.program_id(1)
        _acc(acc, _dot_nt(dp_ref[...], w_ref[...]), j == 0)

        @pl.when(j == N_DEV - 1)
        def _():
            dx_ref[...] = ALPHA * dz_ref[...] + acc[...]

    row = pl.BlockSpec((tm, D), lambda i, j: (i, 0))
    return _call(
        after, body, grid=(S // tm, N_DEV),
        in_specs=[row, pl.BlockSpec((None, tm, nb), lambda i, j: (j // 2, i, j % 2)),
                  pl.BlockSpec((None, D, nb), lambda i, j: (j, 0, 0))],
        out_specs=row, out_shape=_sds((S, D), F32), scratch_shapes=[pltpu.VMEM((tm, D), F32)],
        compiler_params=_params(("arbitrary", "arbitrary")), name=name)(dz, dproj, wain)


def _hp(a, b, dims):
    ah = a.astype(CDT)
    al = (a - ah.astype(F32)).astype(CDT)
    bh = b.astype(CDT)
    bl = (b - bh.astype(F32)).astype(CDT)
    dg = lambda u, w: lax.dot_general(u, w, (dims, ((), ())), preferred_element_type=F32)
    return dg(ah, bh) + (dg(ah, bl) + dg(al, bh))


def _hdot(a, b):
    return _hp(a, b, ((1,), (0,)))


def _hdot_nt(a, b):
    return _hp(a, b, ((1,), (1,)))


def _hdot_tn(a, b):
    return _hp(a, b, ((0,), (0,)))


def _tri_dot(tri, x):
    hi = x.astype(CDT)
    r1 = x - hi.astype(F32)
    mid = r1.astype(CDT)
    lo = (r1 - mid.astype(F32)).astype(CDT)
    t = tri.astype(CDT)
    return (jnp.dot(t, hi, preferred_element_type=F32) + jnp.dot(t, mid, preferred_element_type=F32)
            + jnp.dot(t, lo, preferred_element_type=F32))


def _hg_gates(q, f, alb_ref):
    a0, a1 = alb_ref[0:1, :], alb_ref[1:2, :]
    mx = jnp.maximum(a0, a1)
    e0, e1 = jnp.exp(a0 - mx), jnp.exp(a1 - mx)
    lb = e0 / (e0 + e1)
    sig = _sigmoid(f)
    forget = lb + (1.0 - lb) * sig
    k = (1.0 - lb) * _sigmoid(-f)
    qs = q * _sigmoid(q) * (HG_DK ** -0.5)
    return qs, k, jnp.log(forget), sig, lb, forget


def _hg_intra(qs, k, b, b_scr):
    b_scr[...] = b
    bm = b_scr[pl.ds(HG_CH // 2 - 1, 1), :]
    bl = b_scr[pl.ds(HG_CH - 1, 1), :]
    eb = jnp.exp(b)
    qb = qs * eb
    e_q = jnp.exp(b - bm)
    e_k = jnp.exp(bm - b)
    e_d = jnp.exp(bl - b)
    return qb, qs * e_q, k * e_k, k * e_d, jnp.exp(bl), eb, e_q, e_k, e_d


def _hgrn_fwd(proj, alb, ngain):
    S = proj.shape[1]
    nc = S // HG_CH
    wb = HG_HPB * HG_DK

    def body(pj_ref, alb_ref, ng_ref, o_ref, y_ref, st_ref, st_scr, b_scr):
        n = pl.program_id(1)

        @pl.when(n == 0)
        def _():
            st_scr[...] = jnp.zeros_like(st_scr)

        r = lax.broadcasted_iota(jnp.int32, (HG_CH, HG_CH), 0)
        c = lax.broadcasted_iota(jnp.int32, (HG_CH, HG_CH), 1)
        causal = r >= c
        tri = causal.astype(F32)
        for j in range(HG_HPB):
            lanes = pl.ds(j * HG_DK, HG_DK)
            q, f, v, g = pj_ref[0, :, lanes], pj_ref[1, :, lanes], pj_ref[2, :, lanes], pj_ref[3, :, lanes]
            qs, k, logf, _, _, _ = _hg_gates(q, f, alb_ref.at[:, lanes])
            b = _tri_dot(tri, logf)
            qb, qt, kt, kd, ebl, _, _, _, _ = _hg_intra(qs, k, b, b_scr.at[j])
            st = st_scr[j]
            st_ref[j] = st
            a = jnp.where(causal, _hdot_nt(qt, kt), 0.0)
            o = _hdot(a, v) + _hdot_nt(qb, st)
            st_scr[j] = st * ebl + _hdot_tn(v, kd)
            o_ref[:, lanes] = o
            rinv = lax.rsqrt(jnp.mean(o * o, axis=-1, keepdims=True) + RMS_EPS)
            y_ref[:, lanes] = (o * rinv * ng_ref[...] * (g * _sigmoid(g))).astype(CDT)

    blk = pl.BlockSpec((HG_CH, wb), lambda h, n: (n, h))
    return pl.pallas_call(
        body, grid=(HG_H // HG_HPB, nc),
        in_specs=[pl.BlockSpec((4, HG_CH, wb), lambda h, n: (0, n, h)), pl.BlockSpec((2, wb), lambda h, n: (0, h)),
                  pl.BlockSpec((1, HG_DK), lambda h, n: (0, 0))],
        out_specs=[blk, blk, pl.BlockSpec((HG_HPB, None, HG_DK, HG_DK), lambda h, n: (h, n, 0, 0))],
        out_shape=[_sds((S, D), F32), _sds((S, D), CDT), _sds((HG_H, nc, HG_DK, HG_DK), F32)],
        scratch_shapes=[pltpu.VMEM((HG_HPB, HG_DK, HG_DK), F32), pltpu.VMEM((HG_HPB, HG_CH, HG_DK), F32)],
        compiler_params=_params(("arbitrary", "arbitrary")), name="hgrn_fwd")(proj, alb, ngain)


def _hgrn_bwd(proj, alb, ngain, o, states, dy):
    S = proj.shape[1]
    nc = S // HG_CH
    wb = HG_HPB * HG_DK

    def body(pj_ref, alb_ref, ng_ref, o_ref, st_ref, dy_ref, dpj_ref, dalb_ref, dng_ref, dst_scr, b_scr):
        h, n = pl.program_id(0), pl.program_id(1)

        @pl.when(n == 0)
        def _():
            dst_scr[...] = jnp.zeros_like(dst_scr)

        ng = ng_ref[...]
        r = lax.broadcasted_iota(jnp.int32, (HG_CH, HG_CH), 0)
        c = lax.broadcasted_iota(jnp.int32, (HG_CH, HG_CH), 1)
        causal = r >= c
        tri, tri_rev = causal.astype(F32), (r <= c).astype(F32)
        dng = None
        for j in range(HG_HPB):
            lanes = pl.ds(j * HG_DK, HG_DK)
            q, f, v, g = pj_ref[0, :, lanes], pj_ref[1, :, lanes], pj_ref[2, :, lanes], pj_ref[3, :, lanes]
            o_ = o_ref[:, lanes]
            dy_ = dy_ref[:, lanes]
            sg = _sigmoid(g)
            rinv = lax.rsqrt(jnp.mean(o_ * o_, axis=-1, keepdims=True) + RMS_EPS)
            nrm = o_ * rinv
            dr = dy_ * (g * sg)
            dg = dy_ * nrm * ng * (sg * (1.0 + g * (1.0 - sg)))
            dn = dr * ng
            do = rinv * (dn - nrm * jnp.mean(dn * nrm, axis=-1, keepdims=True))
            dng = _colsum(dr * nrm) if dng is None else dng + _colsum(dr * nrm)
            qs, k, logf, sig, lb, forget = _hg_gates(q, f, alb_ref.at[:, lanes])
            b = _tri_dot(tri, logf)
            qb, qt, kt, kd, ebl, eb, e_q, e_k, e_d = _hg_intra(qs, k, b, b_scr.at[j])
            st = st_ref[j]
            dstn = dst_scr[j]
            a = jnp.where(causal, _hdot_nt(qt, kt), 0.0)
            da = jnp.where(causal, _hdot_nt(do, v), 0.0)
            dv = _hdot_tn(a, do) + _hdot_nt(kd, dstn)
            dqb = _hdot(do, st)
            dkd = _hdot(v, dstn)
            dqt = _hdot(da, kt)
            dkt = _hdot_tn(da, qt)
            dbl = _colsum(dkd * kd) + ebl * _colsum(dstn * st)
            dst_scr[j] = dstn * ebl + _hdot_tn(do, qb)
            dqs = dqt * e_q + dqb * eb
            dk = dkt * e_k + dkd * e_d
            db = dqt * qt + dqb * qb - dkt * kt - dkd * kd
            dlogf = _tri_dot(tri_rev, db) + dbl
            dforget = dlogf / forget
            dsig = (1.0 - lb) * (dforget - dk)
            df = dsig * sig * (1.0 - sig)
            dlb = _colsum((dforget - dk) * (1.0 - sig))
            sq = _sigmoid(q)
            dq = dqs * (HG_DK ** -0.5) * (sq * (1.0 + q * (1.0 - sq)))
            dpj_ref[0, :, lanes] = dq.astype(CDT)
            dpj_ref[1, :, lanes] = df.astype(CDT)
            dpj_ref[2, :, lanes] = dv.astype(CDT)
            dpj_ref[3, :, lanes] = dg.astype(CDT)
            da0 = dlb * lb * (1.0 - lb)
            _acc(dalb_ref.at[pl.ds(0, 1), lanes], da0, n == 0)
            _acc(dalb_ref.at[pl.ds(1, 1), lanes], -da0, n == 0)
        _acc(dng_ref, dng, jnp.logical_and(h == 0, n == 0))

    blk = pl.BlockSpec((HG_CH, wb), lambda h, n: (nc - 1 - n, h))
    pj = pl.BlockSpec((4, HG_CH, wb), lambda h, n: (0, nc - 1 - n, h))
    alb_blk = pl.BlockSpec((2, wb), lambda h, n: (0, h))
    ng_blk = pl.BlockSpec((1, HG_DK), lambda h, n: (0, 0))
    return pl.pallas_call(
        body, grid=(HG_H // HG_HPB, nc),
        in_specs=[pj, alb_blk, ng_blk, blk,
                  pl.BlockSpec((HG_HPB, None, HG_DK, HG_DK), lambda h, n: (h, nc - 1 - n, 0, 0)), blk],
        out_specs=[pj, alb_blk, ng_blk],
        out_shape=[_sds((4, S, D), CDT), _sds((2, D), F32), _sds((1, HG_DK), F32)],
        scratch_shapes=[pltpu.VMEM((HG_HPB, HG_DK, HG_DK), F32), pltpu.VMEM((HG_HPB, HG_CH, HG_DK), F32)],
        compiler_params=_params(("arbitrary", "arbitrary")), name="hgrn_bwd")(proj, alb, ngain, o, states, dy)


def _slope(h):
    return 2.0 ** (-8.0 * (h + 1) / ATT_QH)


def _attn_mask(n):
    qi = lax.broadcasted_iota(jnp.int32, (WINDOW, 2 * WINDOW), 0)
    si = lax.broadcasted_iota(jnp.int32, (WINDOW, 2 * WINDOW), 1)
    dist = qi - si + WINDOW
    valid = (dist >= 0) & (dist < WINDOW) & (n * WINDOW - WINDOW + si >= 0)
    return valid, dist.astype(F32)


def _attn_probs(qh, kh, sink, slope, valid, distf):
    s = _dot_nt(qh, kh) * (ATT_HD ** -0.5) - slope * distf
    s = jnp.where(valid, s, NEG)
    m = jnp.maximum(jnp.max(s, axis=-1, keepdims=True), sink)
    e = jnp.exp(s - m)
    es = jnp.exp(sink - m)
    inv = 1.0 / (jnp.sum(e, axis=-1, keepdims=True) + es)
    return e * inv, es * inv


def _attn_specs(S):
    nb = S // WINDOW
    cur = lambda H: pl.BlockSpec((H, WINDOW, ATT_HD), lambda n: (0, n, 0))
    prev = lambda H: pl.BlockSpec((H, WINDOW, ATT_HD), lambda n: (0, jnp.maximum(n - 1, 0), 0))
    return nb, cur, prev


def _attn_fwd(q4, kv4, sinks):
    S = q4.shape[1]
    nb, cur, prev = _attn_specs(S)

    def body(sink_ref, q_ref, kvc_ref, kvp_ref, o_ref):
        valid, distf = _attn_mask(pl.program_id(0))
        for h in range(ATT_QH):
            kvh = h // ATT_G
            kh = jnp.concatenate([kvp_ref[kvh], kvc_ref[kvh]], axis=0)
            vh = jnp.concatenate([kvp_ref[ATT_KVH + kvh], kvc_ref[ATT_KVH + kvh]], axis=0)
            p, _ = _attn_probs(q_ref[h], kh, sink_ref[0, h], _slope(h), valid, distf)
            o_ref[h] = _dot(p, vh).astype(CDT)

    return pl.pallas_call(
        body, grid=(nb,),
        in_specs=[pl.BlockSpec(memory_space=pltpu.SMEM), cur(ATT_QH), cur(2 * ATT_KVH), prev(2 * ATT_KVH)],
        out_specs=cur(ATT_QH), out_shape=_sds((ATT_QH, S, ATT_HD), CDT),
        compiler_params=_params(("arbitrary",)), name="attn_fwd")(sinks, q4, kv4, kv4)


def _attn_bwd(q4, kv4, sinks, do4):
    S = q4.shape[1]
    nb, cur, prev = _attn_specs(S)

    def body(sink_ref, q_ref, kvc_ref, kvp_ref, do_ref, dq_ref, dkv_ref, dbq_ref, dsink_ref):
        n = pl.program_id(0)
        first = n == 0

        @pl.when(first)
        def _():
            dkv_ref[...] = jnp.zeros_like(dkv_ref)
            dsink_ref[...] = jnp.zeros_like(dsink_ref)

        valid, distf = _attn_mask(n)
        lane = lax.broadcasted_iota(jnp.int32, (1, 128), 1)
        rows_cur = pl.ds(pl.multiple_of(n * WINDOW, WINDOW), WINDOW)
        rows_prev = pl.ds(pl.multiple_of(jnp.maximum(n - 1, 0) * WINDOW, WINDOW), WINDOW)
        dsinks = jnp.zeros((1, 128), F32)
        for kvh in range(ATT_KVH):
            kh = jnp.concatenate([kvp_ref[kvh], kvc_ref[kvh]], axis=0)
            vh = jnp.concatenate([kvp_ref[ATT_KVH + kvh], kvc_ref[ATT_KVH + kvh]], axis=0)
            dk = dv = None
            for h in range(kvh * ATT_G, (kvh + 1) * ATT_G):
                qh = q_ref[h]
                doh = do_ref[h]
                p, ps = _attn_probs(qh, kh, sink_ref[0, h], _slope(h), valid, distf)
                dp = _dot_nt(doh, vh)
                dd = jnp.sum(p * dp, axis=-1, keepdims=True)
                ds = p * (dp - dd)
                dsinks = dsinks + jnp.where(lane == h, -jnp.sum(ps * dd, axis=0, keepdims=True), 0.0)
                dqh = _dot(ds, kh) * (ATT_HD ** -0.5)
                dq_ref[h] = dqh.astype(CDT)
                _acc(dbq_ref.at[h], _colsum(dqh), first)
                dkh = _dot_tn(ds, qh) * (ATT_HD ** -0.5)
                dvh = _dot_tn(p, doh)
                dk = dkh if dk is None else dk + dkh
                dv = dvh if dv is None else dv + dvh
            dkv_ref[kvh, rows_prev, :] += dk[:WINDOW]
            dkv_ref[kvh, rows_cur, :] += dk[WINDOW:]
            dkv_ref[ATT_KVH + kvh, rows_prev, :] += dv[:WINDOW]
            dkv_ref[ATT_KVH + kvh, rows_cur, :] += dv[WINDOW:]
        dsink_ref[...] += dsinks

    return pl.pallas_call(
        body, grid=(nb,),
        in_specs=[pl.BlockSpec(memory_space=pltpu.SMEM), cur(ATT_QH), cur(2 * ATT_KVH), prev(2 * ATT_KVH), cur(ATT_QH)],
        out_specs=[cur(ATT_QH), pl.BlockSpec((2 * ATT_KVH, S, ATT_HD), lambda n: (0, 0, 0)),
                   pl.BlockSpec((ATT_QH, 1, ATT_HD), lambda n: (0, 0, 0)), pl.BlockSpec((1, 128), lambda n: (0, 0))],
        out_shape=[_sds((ATT_QH, S, ATT_HD), CDT), _sds((2 * ATT_KVH, S, ATT_HD), F32), _sds((ATT_QH, 1, ATT_HD), F32),
                   _sds((1, 128), F32)],
        compiler_params=_params(("arbitrary",)), name="attn_bwd")(sinks, q4, kv4, kv4, do4)


def _local_step(x, p, target, getw, sm, emit):
    S = x.shape[0]
    vec = lambda a: a.reshape(1, -1)
    ln_g = lambda l, k: vec(sm["ln_gain"][l, k])
    ln_b = lambda l, k: vec(sm["ln_bias"][l, k])
    xb = x.astype(CDT)
    pb = p.astype(CDT)

    proj = _mm_nn(xb, getw("a_w_in", None), (4, S, D), (None, _tile(S), 512), lambda g, i: (g // 2, i, g % 2), F32,
                  name="a_in")
    o_a, y_a, states = _hgrn_fwd(proj, sm["a_lower_bound"], sm["a_norm_gain"])
    zeros = jnp.zeros((1, D), F32)
    z = [[None] * 3 for _ in range(2)]
    xs = [[None] * 3 for _ in range(2)]
    xbs = [[None] * 3 for _ in range(2)]
    z[0][0], xs[0][0], xbs[0][0] = _mixout_ln(y_a[None], getw("a_w_out", y_a)[None], zeros, x, ln_g(0, 0), ln_b(0, 0),
                                              "a_out_ln")
    gu, hid, sgs, ups = [None, None], [None, None], [None, None], [None, None]

    def ffn_ple(l):
        wgu = getw(f"gu{l}", xbs[l][0])
        gu[l], hid[l], z[l][1], xs[l][1], xbs[l][1] = _ffn_fwd(
            xs[l][0], xbs[l][0], wgu, getw(f"dn{l}", None), ln_g(l, 1), ln_b(l, 1), f"ffn_fwd{l}")
        sgs[l], ups[l], z[l][2], xs[l][2], xbs[l][2] = _ple_fwd(
            xs[l][1], xbs[l][1], pb[l], getw(f"pg{l}", None), vec(sm["ple_b_gate"][l]), getw(f"pu{l}", None), ln_g(l, 2),
            ln_b(l, 2), f"ple_fwd{l}")

    ffn_ple(0)
    x3, x3b = xs[0][2], xbs[0][2]
    w_kv, w_q, w_bo = getw("kv_w", x3b), getw("b_w_q", None), getw("b_w_out", None)
    kv4 = _mm_nn(x3b, w_kv, (2 * ATT_KVH, S, ATT_HD), (None, _tile(S), ATT_HD), lambda g, i: (g, i, 0), CDT,
                 bias3=sm["kv_b"].reshape(2 * ATT_KVH, 1, ATT_HD), name="kv_proj")
    q4 = _mm_nn(x3b, w_q, (ATT_QH, S, ATT_HD), (None, _tile(S), ATT_HD), lambda g, i: (g, i, 0), CDT,
                bias3=sm["b_b_q"].reshape(ATT_QH, 1, ATT_HD), name="q_proj")
    o4 = _attn_fwd(q4, kv4, sm["b_sinks"])
    z[1][0], xs[1][0], xbs[1][0] = _mixout_ln(o4, w_bo, sm["b_b_out"], x3, ln_g(1, 0), ln_b(1, 0), "b_out_ln")
    ffn_ple(1)
    loss, dy = _loss_fwd_bwd(xs[1][2], target)

    gs = {}
    d_ln_g = [[None] * 3 for _ in range(2)]
    d_ln_b = [[None] * 3 for _ in range(2)]
    g_bg = [None, None]

    def ffn_ple_bwd(l, dy):
        dx2, dgl, dup, d_ln_g[l][2], d_ln_b[l][2], g_bg[l] = _ple_bwd(dy, z[l][2], sgs[l], ups[l], ln_g(l, 2),
                                                                     getw(f"pg{l}", None), f"ple_bwd{l}")
        g_pg = _wgrad(xbs[l][1][None], dgl[None], f"g_ple_gate{l}")[0]
        g_pu = _wgrad(pb[l][None], dup[None], f"g_ple_up{l}")[0]
        dx1, dzb, dgu, d_ln_g[l][1], d_ln_b[l][1] = _ffn_bwd(dx2, z[l][1], gu[l], getw(f"gu{l}", None),
                                                           getw(f"dn{l}", None), ln_g(l, 1), f"ffn_bwd{l}")
        g_dn = _wgrad(hid[l], dzb[None], f"g_ffn_down{l}")
        g_gu = _wgrad(xbs[l][0][None], dgu.reshape(8, S, FFN_B), f"g_ffn_gate_up{l}")
        return dx1, emit({f"pg{l}": g_pg, f"pu{l}": g_pu, f"dn{l}": g_dn, f"gu{l}": g_gu})

    dx1, tok = ffn_ple_bwd(1, dy)
    dz, dzb, do4, d_ln_g[1][0], d_ln_b[1][0], gs["b_b_out"] = _mixout_bwd(dx1, z[1][0], ln_g(1, 0), w_bo, CDT,
                                                                         "b_out_bwd", after=tok)
    g_bo = _wgrad(o4, dzb[None], "g_b_w_out")
    dq4, dkv4, dbq, dsinks = _attn_bwd(q4, kv4, sm["b_sinks"], do4)
    gs["b_b_q"] = dbq
    gs["b_sinks"] = dsinks
    g_q = _wgrad(x3b[None], dq4, "g_b_w_q")
    g_kv = _wgrad(x3b[None], dkv4.astype(CDT), "g_kv_w")
    tok = emit({"b_w_out": g_bo, "b_w_q": g_q, "kv_w": g_kv})
    dx3, gs["kv_b"] = _qkv_bwd(dz, dq4, dkv4, w_q, w_kv, "qkv_bwd", after=tok)
    dx1, tok = ffn_ple_bwd(0, dx3)
    w_ao = getw("a_w_out", None)
    dz, dzb, dyr, d_ln_g[0][0], d_ln_b[0][0], _ = _mixout_bwd(dx1, z[0][0], ln_g(0, 0), w_ao[None], F32, "a_out_bwd",
                                                              after=tok)
    g_ao = _wgrad(y_a[None], dzb[None], "g_a_w_out")[0]
    dproj, gs["a_lower_bound"], gs["a_norm_gain"] = _hgrn_bwd(proj, sm["a_lower_bound"], sm["a_norm_gain"], o_a, states,
                                                              dyr[0])
    tk = lambda t: (None, t, D)
    g_ain = _mm_tn(xb[None], dproj, N_DEV, lambda g, k: (0, k, 0), lambda g, k: (g // 2, k, g % 2),
                   tk, lambda t: (None, t, 512), (N_DEV, D, 512), (None, D, 512), lambda g, k: (g, 0, 0), name="g_a_w_in")
    tok = emit({"a_w_out": g_ao, "a_w_in": g_ain})
    grad_x = _inproj_bwd(dz, dproj, getw("a_w_in", None), "a_in_bwd", after=tok)
    gs["ple_b_gate"] = jnp.concatenate(g_bg, axis=0)
    gs["ln_gain"] = jnp.stack([jnp.concatenate(r, axis=0) for r in d_ln_g])
    gs["ln_bias"] = jnp.stack([jnp.concatenate(r, axis=0) for r in d_ln_b])
    return loss, grad_x, gs


def _peer(k):
    x, y, c = lax.axis_index("x"), lax.axis_index("y"), lax.axis_index("c")
    px = 1 - x if k & 4 else x
    py = 1 - y if k & 2 else y
    pc = 1 - c if k & 1 else c
    return (px, py, pc), 4 * px + 2 * py + pc


def _my_index():
    return 4 * lax.axis_index("x") + 2 * lax.axis_index("y") + lax.axis_index("c")


def _exchange(srcs, dst_shapes, plan, name):
    n_src, n_piece = len(srcs), len(plan)

    def body(*refs):
        src_refs, dst_refs = refs[:n_src], refs[n_src:n_src + len(dst_shapes)]
        send_sems, recv_sems, local_sems = refs[n_src + len(dst_shapes):]
        me = _my_index()

        def at(ref, idx):
            return ref.at[idx] if idx else ref

        local = []
        for t, (si, sfn, di, dfn) in enumerate(plan):
            cp = pltpu.make_async_copy(at(src_refs[si], sfn(me)), at(dst_refs[di], dfn(me)), local_sems.at[t])
            cp.start()
            local.append(cp)
        sends = []
        for k in range(1, N_DEV):
            peer, pid = _peer(k)
            for t, (si, sfn, di, dfn) in enumerate(plan):
                cp = pltpu.make_async_remote_copy(
                    src_ref=at(src_refs[si], sfn(pid)), dst_ref=at(dst_refs[di], dfn(me)),
                    send_sem=send_sems.at[t * 7 + k - 1], recv_sem=recv_sems.at[t * 7 + k - 1],
                    device_id=peer, device_id_type=MESH)
                cp.start()
                sends.append(cp)
        for k in range(1, N_DEV):
            peer, pid = _peer(k)
            for t, (si, sfn, di, dfn) in enumerate(plan):
                pltpu.make_async_remote_copy(
                    src_ref=at(src_refs[si], sfn(me)), dst_ref=at(dst_refs[di], dfn(pid)),
                    send_sem=send_sems.at[t * 7 + k - 1], recv_sem=recv_sems.at[t * 7 + k - 1],
                    device_id=peer, device_id_type=MESH).wait_recv()
        for cp in sends:
            cp.wait_send()
        for cp in local:
            cp.wait()

    hbm = pl.BlockSpec(memory_space=pltpu.HBM)
    return pl.pallas_call(
        body, in_specs=[hbm] * n_src, out_specs=[hbm] * len(dst_shapes), out_shape=dst_shapes,
        scratch_shapes=[pltpu.SemaphoreType.DMA((7 * n_piece,)), pltpu.SemaphoreType.DMA((7 * n_piece,)),
                        pltpu.SemaphoreType.DMA((n_piece,))],
        name=name)(*srcs)


def _gather(shards, name):
    dsts = [_sds((N_DEV,) + a.shape, a.dtype) for a in shards]
    plan = [(i, lambda j: (), i, lambda s: (s,)) for i in range(len(shards))]
    return _exchange(shards, dsts, plan, name)


_HBM = pl.BlockSpec(memory_space=pltpu.HBM)
_SEM = pl.BlockSpec(memory_space=pltpu.SEMAPHORE)
_DATAFLOW = pltpu.SideEffectType.DATAFLOW_SIDE_EFFECTING


def _piece_copy(mode, src, land, send_sems, recv_sems, t, k, sender, receiver, peer):
    return pltpu.make_async_remote_copy(
        src_ref=src if mode == "gather" else src.at[receiver], dst_ref=land.at[sender],
        send_sem=send_sems.at[t * 7 + k - 1], recv_sem=recv_sems.at[t * 7 + k - 1], device_id=peer, device_id_type=MESH)


def _xstart(groups, mode, name, after=None):
    flat = [a for g in groups for a in g]
    n, ng = len(flat), len(groups)
    land_shapes = [((N_DEV,) + a.shape) if mode == "gather" else a.shape for a in flat]
    first = [sum(len(g) for g in groups[:i]) for i in range(ng)]
    extra = [] if after is None else [after]

    def body(*refs):
        srcs, lands = refs[:n], refs[n:2 * n]
        sems = refs[2 * n + len(extra):2 * n + len(extra) + 2 * ng]
        tok_ref, local_sems = refs[-2], refs[-1]
        me = _my_index()
        local = []
        for i in range(n):
            cp = pltpu.make_async_copy(srcs[i] if mode == "gather" else srcs[i].at[me], lands[i].at[me], local_sems.at[i])
            cp.start()
            local.append(cp)
        for cp in local:
            cp.wait()
        for gi, g in enumerate(groups):
            for k in range(1, N_DEV):
                peer, pid = _peer(k)
                for t in range(len(g)):
                    i = first[gi] + t
                    _piece_copy(mode, srcs[i], lands[i], sems[2 * gi], sems[2 * gi + 1], t, k, me, pid, peer).start()
        tok_ref[...] = jnp.zeros_like(tok_ref)

    sem_shapes = []
    for g in groups:
        sem_shapes += [pltpu.SemaphoreType.DMA((7 * len(g),))] * 2
    thru = [pltpu.HBM(a.shape, a.dtype) for a in flat] + [pltpu.HBM(s, a.dtype) for s, a in zip(land_shapes, flat)]
    outs = pl.pallas_call(
        body, in_specs=[_HBM] * (2 * n) + [pl.BlockSpec(memory_space=pl.ANY)] * len(extra),
        out_specs=[_SEM] * (2 * ng) + [_HBM] * (2 * n) + [pl.BlockSpec(memory_space=pltpu.VMEM)],
        out_shape=sem_shapes + thru + [_sds((8, 128), F32)],
        input_output_aliases={i: 2 * ng + i for i in range(2 * n)},
        scratch_shapes=[pltpu.SemaphoreType.DMA((n,))],
        compiler_params=pltpu.CompilerParams(has_side_effects=_DATAFLOW), name=name)(
            *[pltpu.with_memory_space_constraint(a, pltpu.HBM) for a in flat],
            *[pltpu.with_memory_space_constraint(lax.empty(s, a.dtype), pltpu.HBM) for s, a in zip(land_shapes, flat)], *extra)
    sems, srcs_thru, lands_thru = outs[:2 * ng], outs[2 * ng:2 * ng + n], outs[2 * ng + n:2 * ng + 2 * n]
    handles = [(sems[2 * gi], sems[2 * gi + 1], srcs_thru[first[gi]:first[gi] + len(g)],
                lands_thru[first[gi]:first[gi] + len(g)]) for gi, g in enumerate(groups)]
    return handles, outs[-1]


def _xwait(handle, mode, after, name):
    send_sems, recv_sems, srcs_thru, lands_thru = handle
    n = len(srcs_thru)

    def body(*refs):
        srcs, lands, send, recv = refs[:n], refs[n:2 * n], refs[2 * n], refs[2 * n + 1]
        me = _my_index()
        for k in range(1, N_DEV):
            peer, pid = _peer(k)
            for t in range(n):
                _piece_copy(mode, srcs[t], lands[t], send, recv, t, k, pid, me, peer).wait_recv()
        for k in range(1, N_DEV):
            peer, pid = _peer(k)
            for t in range(n):
                _piece_copy(mode, srcs[t], lands[t], send, recv, t, k, me, pid, peer).wait_send()

    extra = [] if after is None else [after]
    outs = pl.pallas_call(
        body, in_specs=[_HBM] * (2 * n) + [_SEM, _SEM] + [pl.BlockSpec(memory_space=pl.ANY)] * len(extra),
        out_specs=[_HBM] * (2 * n),
        out_shape=[pltpu.HBM(a.shape, a.dtype) for a in list(srcs_thru) + list(lands_thru)],
        input_output_aliases={i: i for i in range(2 * n)},
        compiler_params=pltpu.CompilerParams(has_side_effects=_DATAFLOW), name=name)(
            *srcs_thru, *lands_thru, send_sems, recv_sems, *extra)
    return outs[n:]


def _adamw(w, g, m, v):
    m = ADAM_B1 * m + (1.0 - ADAM_B1) * g
    v = ADAM_B2 * v + (1.0 - ADAM_B2) * (g * g)
    m_hat = m / (1.0 - ADAM_B1 ** ADAM_STEP)
    v_hat = v / (1.0 - ADAM_B2 ** ADAM_STEP)
    delta = -ADAM_LR * (m_hat / (jnp.sqrt(v_hat) + ADAM_EPS) + ADAM_WD * w)
    return delta, m, v


def _adam_big(w, parts, m, v, name):
    L, R, C = w.shape
    tr = _tile(R, (256, 128, 176, 64, 32, 16))
    nr = R // tr

    def body(w_ref, *refs):
        p_refs, (m_ref, v_ref, g_ref, d_ref, mo_ref, vo_ref) = refs[:L], refs[L:]
        for l in range(L):
            @pl.when(pl.program_id(0) == l)
            def _(p_ref=p_refs[l]):
                g = p_ref[0].astype(F32)
                for s in range(1, N_DEV):
                    g = g + p_ref[s].astype(F32)
                g_ref[...] = g
                d_ref[...], mo_ref[...], vo_ref[...] = _adamw(w_ref[...], g, m_ref[...], v_ref[...])

    row = pl.BlockSpec((None, tr, C), lambda l, i: (l, i, 0))
    park = lambda l_of: (lambda l, i: (0, jnp.where(l == l_of, i, 0 if l_of else nr - 1), 0))
    return pl.pallas_call(
        body, grid=(L, nr),
        in_specs=[row] + [pl.BlockSpec((N_DEV, tr, C), park(l)) for l in range(L)] + [row, row],
        out_specs=[row] * 4, out_shape=[_sds((L, R, C), F32)] * 4,
        compiler_params=_params(("arbitrary", "arbitrary")), name=name)(w, *parts, m, v)


SMALL = (("a_lower_bound", 2, True), ("ln_gain", 6, True), ("ln_bias", 6, True), ("a_norm_gain", 1, False),
         ("kv_b", 4, False), ("b_b_q", 8, False), ("b_sinks", 1, False), ("b_b_out", 8, False), ("ple_b_gate", 16, False))
SUBLANES = 8


def _slot(r):
    return -(-r // SUBLANES) * SUBLANES


SMALL_ROWS = sum(_slot(r) for _, r, _ in SMALL)
PART_ROWS = sum(r * N_DEV if sh else _slot(r) for _, r, sh in SMALL)


def _pack_rows(a, rows):
    flat = a.reshape(-1)
    return jnp.pad(flat, (0, rows * 128 - flat.size)).reshape(rows, 128)


def _pack_small(d):
    return jnp.concatenate([_pack_rows(d[n], _slot(r)) for n, r, _ in SMALL], axis=0)


def _unpack_small(packed, like):
    out, r0 = {}, 0
    for n, r, _ in SMALL:
        out[n] = packed[r0:r0 + r].reshape(-1)[:like[n].size].reshape(like[n].shape)
        r0 += _slot(r)
    return out


def _pack_partials(gs):
    blocks = []
    for n, r, sharded in SMALL:
        if sharded:
            blocks.append(gs[n].reshape(r * N_DEV, 128))
        else:
            blocks.append(_pack_rows(gs[n], _slot(r)))
    return jnp.concatenate(blocks, axis=0)


def _adam_small(parts, w, m, v):
    def body(p_ref, w_ref, m_ref, v_ref, g_ref, d_ref, mo_ref, vo_ref, tot):
        me = _my_index()
        t = p_ref[0]
        for s in range(1, N_DEV):
            t = t + p_ref[s]
        tot[...] = t
        g_ref[...] = jnp.zeros_like(g_ref)
        src, dst = 0, 0
        for _, r, sharded in SMALL:
            if sharded:
                for i in range(r):
                    g_ref[pl.ds(dst + i, 1), :] = tot[pl.ds(src + i * N_DEV + me, 1), :]
                src += r * N_DEV
            else:
                g_ref[pl.ds(dst, _slot(r)), :] = tot[pl.ds(src, _slot(r)), :]
                src += _slot(r)
            dst += _slot(r)
        d_ref[...], mo_ref[...], vo_ref[...] = _adamw(w_ref[...], g_ref[...], m_ref[...], v_ref[...])

    full = pl.BlockSpec((SMALL_ROWS, 128), lambda: (0, 0))
    return pl.pallas_call(
        body, in_specs=[pl.BlockSpec((N_DEV, PART_ROWS, 128), lambda: (0, 0, 0)), full, full, full],
        out_specs=[full] * 4, out_shape=[_sds((SMALL_ROWS, 128), F32)] * 4,
        scratch_shapes=[pltpu.VMEM((PART_ROWS, 128), F32)], name="adam_small")(parts, w, m, v)


WEIGHTS = ("a_w_in", "a_lower_bound", "a_norm_gain", "a_w_out", "kv_w", "kv_b", "b_w_q", "b_b_q", "b_sinks", "b_w_out",
           "b_b_out", "ffn_w_gate_up", "ffn_w_down", "ple_w_up", "ple_w_gate", "ple_b_gate", "ln_gain", "ln_bias")


def _heads_in(w, heads):
    return w.reshape(D, heads, ATT_HD).transpose(1, 0, 2)


def _heads_out(g):
    return g.transpose(1, 0, 2).reshape(D, -1)


GATHER_GROUPS = (("a_w_in",), ("a_w_out", "gu0", "dn0", "pu0", "pg0"), ("kv_w", "b_w_q", "b_w_out"),
                 ("gu1", "dn1", "pu1", "pg1"))
KERNEL_LAYOUT = {
    "a_w_in": lambda a: a,
    "a_w_out": lambda a: a.reshape(D, D),
    "kv_w": lambda a: _heads_in(a.reshape(D, 2 * ATT_KVH * ATT_HD), 2 * ATT_KVH),
    "b_w_q": lambda a: _heads_in(a.reshape(D, D), ATT_QH),
    "b_w_out": lambda a: a.reshape(ATT_QH, ATT_HD, D),
    "gu": lambda a: a.reshape(2, 4, D, FFN_B),
    "dn": lambda a: a.reshape(4, FFN_B, D),
    "pu": lambda a: a.transpose(1, 0, 2).reshape(PLE_DIM, D),
    "pg": lambda a: a.reshape(D, D),
}
_row_blocks = lambda a: a.reshape(N_DEV, -1, a.shape[-1])
OWNER_BLOCKS = {
    "a_w_in": lambda g: g,
    "a_w_out": _row_blocks,
    "kv_w": lambda g: _row_blocks(_heads_out(g)),
    "b_w_q": lambda g: _row_blocks(_heads_out(g)),
    "b_w_out": lambda g: _row_blocks(g.reshape(D, D)),
    "gu": lambda g: g,
    "dn": lambda g: _row_blocks(g.reshape(FFN_H, D)),
    "pu": lambda g: g.reshape(PLE_DIM, N_DEV, 128).transpose(1, 0, 2),
    "pg": _row_blocks,
}


def kernel(x, p, a_w_in, a_lower_bound, a_norm_gain, a_w_out, kv_w, kv_b, b_w_q, b_b_q, b_sinks, b_w_out, b_b_out, ffn_w_gate_up, ffn_w_down, ple_w_up, ple_w_gate, ple_b_gate, ln_gain, ln_bias, loss_target, m_a_w_in, m_a_lower_bound, m_a_norm_gain, m_a_w_out, m_kv_w, m_kv_b, m_b_w_q, m_b_b_q, m_b_sinks, m_b_w_out, m_b_b_out, m_ffn_w_gate_up, m_ffn_w_down, m_ple_w_up, m_ple_w_gate, m_ple_b_gate, m_ln_gain, m_ln_bias, v_a_w_in, v_a_lower_bound, v_a_norm_gain, v_a_w_out, v_kv_w, v_kv_b, v_b_w_q, v_b_b_q, v_b_sinks, v_b_w_out, v_b_b_out, v_ffn_w_gate_up, v_ffn_w_down, v_ple_w_up, v_ple_w_gate, v_ple_b_gate, v_ln_gain, v_ln_bias):
    given = dict(locals())
    w = {n: given[n] for n in WEIGHTS}
    m = {n: given["m_" + n] for n in WEIGHTS}
    v = {n: given["v_" + n] for n in WEIGHTS}
    small_sharded = jnp.concatenate([_pack_rows(a, SUBLANES) for a in (a_lower_bound, ln_gain, ln_bias)], axis=0)
    (g_small,) = _gather([small_sharded], "gather_small")
    shards = {"a_w_in": a_w_in[0], "a_w_out": a_w_out[0], "kv_w": kv_w, "b_w_q": b_w_q[0], "b_w_out": b_w_out[0]}
    for l in range(2):
        shards.update({f"gu{l}": ffn_w_gate_up[l], f"dn{l}": ffn_w_down[l], f"pu{l}": ple_w_up[l], f"pg{l}": ple_w_gate[l]})
    wire = [[shards[n].astype(CDT) for n in g] for g in GATHER_GROUPS]
    first_handle, token = _xstart(wire[:1], "gather", "gather_start0", after=g_small)
    rest_handles, _ = _xstart(wire[1:], "gather", "gather_start1", after=token)
    gather_handles = first_handle + rest_handles
    gathered = {}

    def getw(key, after):
        if key not in gathered:
            gi = [key in g for g in GATHER_GROUPS].index(True)
            lands = _xwait(gather_handles[gi], "gather", after, f"gather_wait{gi}")
            for n, a in zip(GATHER_GROUPS[gi], lands):
                gathered[n] = KERNEL_LAYOUT[n.rstrip("01")](a)
        return gathered[key]

    full_rows = lambda r0, r: g_small[:, r0:r0 + r].transpose(1, 0, 2).reshape(r, D)
    sm = {"a_lower_bound": full_rows(0, 2), "ln_gain": full_rows(SUBLANES, 6).reshape(2, 3, D),
          "ln_bias": full_rows(2 * SUBLANES, 6).reshape(2, 3, D), "a_norm_gain": a_norm_gain, "kv_b": kv_b,
          "b_b_q": b_b_q[0], "b_sinks": b_sinks, "b_b_out": b_b_out, "ple_b_gate": ple_b_gate}

    scatters = []

    def emit(grads):
        names = list(grads)
        (handle,), token = _xstart([[OWNER_BLOCKS[n.rstrip("01")](grads[n]) for n in names]], "scatter",
                                   f"scatter_start{len(scatters)}")
        scatters.append((names, handle))
        return token

    loss, grad_x, gs = _local_step(x[0], p[:, 0], loss_target[0], getw, sm, emit)

    (parts_small,) = _gather([_pack_partials(gs)], "gather_small_grads")
    packed = _adam_small(parts_small, _pack_small(w), _pack_small(m), _pack_small(v))
    small_out = [_unpack_small(a, w) for a in packed]
    out = {n: [s[n] for s in small_out] for n, _, _ in SMALL}

    parts, last = {}, grad_x
    adam_after = {1: (("kv_w", "kv_w"), ("b_w_q", "b_w_q"), ("b_w_out", "b_w_out")),
                  2: (("ffn_w_gate_up", "gu"), ("ffn_w_down", "dn"), ("ple_w_up", "pu"), ("ple_w_gate", "pg")),
                  3: (("a_w_out", "a_w_out"), ("a_w_in", "a_w_in"))}
    for i, (names, handle) in enumerate(scatters):
        parts.update(zip(names, _xwait(handle, "scatter", last, f"scatter_wait{i}")))
        for n, key in adam_after.get(i, ()):
            lrc = (1,) * (3 - w[n].ndim) + w[n].shape
            layers = [parts[key]] if key in parts else [parts[key + "0"], parts[key + "1"]]
            res = _adam_big(w[n].reshape(lrc), layers, m[n].reshape(lrc), v[n].reshape(lrc), "adam_" + n)
            out[n] = [r.reshape(w[n].shape) for r in res]
            last = res[3]

    loss = lax.psum(loss[0, 0], ("x", "y", "c"))
    res = [loss, grad_x[None]]
    for i in range(4):
        res += [out[n][i] for n in WEIGHTS]
    return tuple(res)
```

```python
import jax
import jax.numpy as jnp
from jax import lax
from jax.experimental import pallas as pl
from jax.experimental.pallas import tpu as pltpu
from jax.experimental.pallas import tpu_sc as plsc

F32 = jnp.float32
CDT = jnp.bfloat16

N_DEV = 8
D = 1024
HG_H, HG_DK, HG_CH = 8, 128, 64
HG_HPB = 4
ATT_HD, ATT_QH, ATT_KVH, ATT_G, WINDOW = 64, 16, 4, 4, 128
FFN_H = 2816
FFN_B = FFN_H // 4
PLE_DIM = 256
ALPHA = (2.0 * 2) ** 0.25
LN_EPS = 1e-5
RMS_EPS = 1e-6
ADAM_LR, ADAM_B1, ADAM_B2, ADAM_EPS, ADAM_WD, ADAM_STEP = 0.001, 0.9, 0.999, 1e-08, 0.01, 10
ROW_TILES = (256, 128, 64)
VMEM_LIMIT = 48 * 1024 * 1024
NEG = -1e30

MESH = pl.DeviceIdType.MESH


def _tile(n, cands=ROW_TILES):
    for t in cands:
        if n % t == 0:
            return t
    return n


def _sds(shape, dtype):
    return jax.ShapeDtypeStruct(tuple(shape), dtype)


def _params(sem):
    return pltpu.CompilerParams(dimension_semantics=sem, vmem_limit_bytes=VMEM_LIMIT)


def _dot(a, b):
    return jnp.dot(a.astype(CDT), b.astype(CDT), preferred_element_type=F32)


def _dot_nt(a, b):
    return lax.dot_general(a.astype(CDT), b.astype(CDT), (((1,), (1,)), ((), ())), preferred_element_type=F32)


def _dot_tn(a, b):
    return lax.dot_general(a.astype(CDT), b.astype(CDT), (((0,), (0,)), ((), ())), preferred_element_type=F32)


def _sigmoid(x):
    return jax.nn.sigmoid(x)


def _ln_fwd(z, g, b):
    mu = jnp.mean(z, axis=-1, keepdims=True)
    zc = z - mu
    var = jnp.mean(zc * zc, axis=-1, keepdims=True)
    return zc * lax.rsqrt(var + LN_EPS) * g + b


def _ln_bwd(z, g, dy):
    mu = jnp.mean(z, axis=-1, keepdims=True)
    zc = z - mu
    var = jnp.mean(zc * zc, axis=-1, keepdims=True)
    rstd = lax.rsqrt(var + LN_EPS)
    xhat = zc * rstd
    dxh = dy * g
    dz = rstd * (dxh - jnp.mean(dxh, axis=-1, keepdims=True) - xhat * jnp.mean(dxh * xhat, axis=-1, keepdims=True))
    return dz, xhat


def _colsum(x):
    return jnp.sum(x, axis=0, keepdims=True)


def _acc(ref, val, first):
    @pl.when(first)
    def _():
        ref[...] = val

    @pl.when(jnp.logical_not(first))
    def _():
        ref[...] += val


def _call(after, body, **kw):
    if after is None:
        return pl.pallas_call(body, **kw)
    kw["in_specs"] = [pl.BlockSpec(memory_space=pl.ANY)] + list(kw["in_specs"])

    def ordered_body(after_ref, *refs):
        body(*refs)

    call = pl.pallas_call(ordered_body, **kw)
    return lambda *args: call(after, *args)


def _mm_nn(a, b3, out_shape, oblock, omap, out_dtype, bias3=None, name="mm_nn"):
    M, K = a.shape
    G, _, Nb = b3.shape
    tm = _tile(M)

    def body(a_ref, b_ref, *rest):
        o_ref = rest[-1]
        acc = _dot(a_ref[...], b_ref[...])
        if bias3 is not None:
            acc = acc + rest[0][...]
        o_ref[...] = acc.astype(o_ref.dtype)

    in_specs = [pl.BlockSpec((tm, K), lambda g, i: (i, 0)), pl.BlockSpec((None, K, Nb), lambda g, i: (g, 0, 0))]
    args = [a, b3]
    if bias3 is not None:
        in_specs.append(pl.BlockSpec((None, 1, Nb), lambda g, i: (g, 0, 0)))
        args.append(bias3)
    return pl.pallas_call(
        body, grid=(G, M // tm), in_specs=in_specs, out_specs=pl.BlockSpec(oblock, omap),
        out_shape=_sds(out_shape, out_dtype), compiler_params=_params(("arbitrary", "arbitrary")), name=name)(*args)


def _mm_tn(a3, b3, G, amap, bmap, ablock, bblock, out_shape, oblock, omap, name="mm_tn"):
    S = a3.shape[1]
    tk = _tile(S, (512, 256, 128))
    Mo, No = oblock[-2], oblock[-1]

    def body(a_ref, b_ref, o_ref, acc):
        k = pl.program_id(1)
        _acc(acc, _dot_tn(a_ref[...], b_ref[...]), k == 0)

        @pl.when(k == pl.num_programs(1) - 1)
        def _():
            o_ref[...] = acc[...].astype(o_ref.dtype)

    return pl.pallas_call(
        body, grid=(G, S // tk),
        in_specs=[pl.BlockSpec(ablock(tk), amap), pl.BlockSpec(bblock(tk), bmap)],
        out_specs=pl.BlockSpec(oblock, omap), out_shape=_sds(out_shape, CDT),
        scratch_shapes=[pltpu.VMEM((Mo, No), F32)],
        compiler_params=_params(("arbitrary", "arbitrary")), name=name)(a3, b3)


def _wgrad(a3, b3, name):
    Ga, S, M = a3.shape
    Gb, _, N = b3.shape
    G = max(Ga, Gb)
    return _mm_tn(
        a3, b3, G,
        (lambda g, k: (g, k, 0)) if Ga > 1 else (lambda g, k: (0, k, 0)),
        (lambda g, k: (g, k, 0)) if Gb > 1 else (lambda g, k: (0, k, 0)),
        lambda tk: (None, tk, M), lambda tk: (None, tk, N),
        (G, M, N), (None, M, N), lambda g, k: (g, 0, 0), name=name)


def _mixout_ln(u3, w3, bias, xin, gain, beta, name):
    G, S, Kb = u3.shape
    tm = _tile(S)

    def body(u_ref, w_ref, b_ref, x_ref, g_ref, be_ref, z_ref, xo_ref, xob_ref):
        h = b_ref[...] + _dot(u_ref[0], w_ref[0])
        for g in range(1, G):
            h = h + _dot(u_ref[g], w_ref[g])
        z = ALPHA * x_ref[...] + h
        z_ref[...] = z
        y = _ln_fwd(z, g_ref[...], be_ref[...])
        xo_ref[...] = y
        xob_ref[...] = y.astype(CDT)

    row = pl.BlockSpec((tm, D), lambda i: (i, 0))
    vec = pl.BlockSpec((1, D), lambda i: (0, 0))
    return pl.pallas_call(
        body, grid=(S // tm,),
        in_specs=[pl.BlockSpec((G, tm, Kb), lambda i: (0, i, 0)), pl.BlockSpec((G, Kb, D), lambda i: (0, 0, 0)),
                  vec, row, vec, vec],
        out_specs=[row, row, row], out_shape=[_sds((S, D), F32), _sds((S, D), F32), _sds((S, D), CDT)],
        compiler_params=_params(("arbitrary",)), name=name)(u3, w3, bias, xin, gain, beta)


def _ffn_fwd(xin, xin_b, wgu, wdn, gain, beta, name):
    S = xin.shape[0]
    tm = _tile(S)

    def body(x_ref, xb_ref, wgu_ref, wdn_ref, g_ref, be_ref, gu_ref, hid_ref, z_ref, xo_ref, xob_ref, acc):
        j = pl.program_id(1)
        xb = xb_ref[...]
        gate = _dot(xb, wgu_ref[0])
        up = _dot(xb, wgu_ref[1])
        gu_ref[0] = gate
        gu_ref[1] = up
        hid = (gate * _sigmoid(gate) * up).astype(CDT)
        hid_ref[...] = hid
        _acc(acc, _dot(hid, wdn_ref[...]), j == 0)

        @pl.when(j == 3)
        def _():
            z = ALPHA * x_ref[...] + acc[...]
            z_ref[...] = z
            y = _ln_fwd(z, g_ref[...], be_ref[...])
            xo_ref[...] = y
            xob_ref[...] = y.astype(CDT)

    row = pl.BlockSpec((tm, D), lambda i, j: (i, 0))
    vec = pl.BlockSpec((1, D), lambda i, j: (0, 0))
    return pl.pallas_call(
        body, grid=(S // tm, 4),
        in_specs=[row, row, pl.BlockSpec((2, None, D, FFN_B), lambda i, j: (0, j, 0, 0)),
                  pl.BlockSpec((None, FFN_B, D), lambda i, j: (j, 0, 0)), vec, vec],
        out_specs=[pl.BlockSpec((2, None, tm, FFN_B), lambda i, j: (0, j, i, 0)),
                   pl.BlockSpec((None, tm, FFN_B), lambda i, j: (j, i, 0)), row, row, row],
        out_shape=[_sds((2, 4, S, FFN_B), F32), _sds((4, S, FFN_B), CDT), _sds((S, D), F32), _sds((S, D), F32),
                   _sds((S, D), CDT)],
        scratch_shapes=[pltpu.VMEM((tm, D), F32)],
        compiler_params=_params(("arbitrary", "arbitrary")), name=name)(xin, xin_b, wgu, wdn, gain, beta)


def _ple_fwd(xin, xin_b, p_b, wpg, bgate, wpu, gain, beta, name):
    S = xin.shape[0]
    tm = _tile(S)

    def body(x_ref, xb_ref, p_ref, wpg_ref, bg_ref, wpu_ref, g_ref, be_ref, sg_ref, up_ref, z_ref, xo_ref, xob_ref):
        sg = _sigmoid(_dot(xb_ref[...], wpg_ref[...]) + bg_ref[...])
        up = _dot(p_ref[...], wpu_ref[...])
        sg_ref[...] = sg
        up_ref[...] = up
        z = ALPHA * x_ref[...] + sg * up
        z_ref[...] = z
        y = _ln_fwd(z, g_ref[...], be_ref[...])
        xo_ref[...] = y
        xob_ref[...] = y.astype(CDT)

    row = pl.BlockSpec((tm, D), lambda i: (i, 0))
    vec = pl.BlockSpec((1, D), lambda i: (0, 0))
    return pl.pallas_call(
        body, grid=(S // tm,),
        in_specs=[row, row, pl.BlockSpec((tm, PLE_DIM), lambda i: (i, 0)), pl.BlockSpec((D, D), lambda i: (0, 0)), vec,
                  pl.BlockSpec((PLE_DIM, D), lambda i: (0, 0)), vec, vec],
        out_specs=[row] * 5,
        out_shape=[_sds((S, D), F32)] * 4 + [_sds((S, D), CDT)],
        compiler_params=_params(("arbitrary",)), name=name)(xin, xin_b, p_b, wpg, bgate, wpu, gain, beta)


def _loss_fwd_bwd(y, target):
    S = y.shape[0]
    tm = _tile(S)

    def body(y_ref, t_ref, l_ref, dy_ref):
        e = y_ref[...] - t_ref[...]
        dy_ref[...] = e * (1.0 / D)
        part = 0.5 * jnp.sum(jnp.sum(e * e, axis=-1, keepdims=True) * (1.0 / D), axis=0, keepdims=True)
        _acc(l_ref, part, pl.program_id(0) == 0)

    row = pl.BlockSpec((tm, D), lambda i: (i, 0))
    return pl.pallas_call(
        body, grid=(S // tm,), in_specs=[row, row],
        out_specs=[pl.BlockSpec((1, 1), lambda i: (0, 0)), row],
        out_shape=[_sds((1, 1), F32), _sds((S, D), F32)],
        compiler_params=_params(("arbitrary",)), name="loss")(y, target)


def _ple_bwd(dy, z, sg, up, gain, wpg, name, after=None):
    S = dy.shape[0]
    tm = _tile(S)

    def body(dy_ref, z_ref, sg_ref, up_ref, g_ref, wpg_ref, dx_ref, dgl_ref, dup_ref, dgain_ref, dbeta_ref, dbg_ref):
        first = pl.program_id(0) == 0
        dy_ = dy_ref[...]
        dz, xhat = _ln_bwd(z_ref[...], g_ref[...], dy_)
        sg_ = sg_ref[...]
        dgl = dz * up_ref[...] * sg_ * (1.0 - sg_)
        dgl_ref[...] = dgl.astype(CDT)
        dup_ref[...] = (dz * sg_).astype(CDT)
        dx_ref[...] = ALPHA * dz + _dot_nt(dgl, wpg_ref[...])
        _acc(dgain_ref, _colsum(dy_ * xhat), first)
        _acc(dbeta_ref, _colsum(dy_), first)
        _acc(dbg_ref, _colsum(dgl), first)

    row = pl.BlockSpec((tm, D), lambda i: (i, 0))
    vec = pl.BlockSpec((1, D), lambda i: (0, 0))
    return _call(
        after, body, grid=(S // tm,), in_specs=[row, row, row, row, vec, pl.BlockSpec((D, D), lambda i: (0, 0))],
        out_specs=[row, row, row, vec, vec, vec],
        out_shape=[_sds((S, D), F32), _sds((S, D), CDT), _sds((S, D), CDT)] + [_sds((1, D), F32)] * 3,
        compiler_params=_params(("arbitrary",)), name=name)(dy, z, sg, up, gain, wpg)


def _ffn_bwd(dy, z, gu, wgu, wdn, gain, name, after=None):
    S = dy.shape[0]
    tm = _tile(S)

    def body(dy_ref, z_ref, gu_ref, wgu_ref, wdn_ref, g_ref, dx_ref, dzb_ref, dgu_ref, dgain_ref, dbeta_ref,
             dz_scr, acc):
        i, j = pl.program_id(0), pl.program_id(1)

        @pl.when(j == 0)
        def _():
            dy_ = dy_ref[...]
            dz, xhat = _ln_bwd(z_ref[...], g_ref[...], dy_)
            dz_scr[...] = dz
            dzb_ref[...] = dz.astype(CDT)
            _acc(dgain_ref, _colsum(dy_ * xhat), i == 0)
            _acc(dbeta_ref, _colsum(dy_), i == 0)

        dhid = _dot_nt(dz_scr[...], wdn_ref[...])
        gate, up = gu_ref[0], gu_ref[1]
        sg = _sigmoid(gate)
        dgate = (dhid * up * (sg * (1.0 + gate * (1.0 - sg)))).astype(CDT)
        dup = (dhid * (gate * sg)).astype(CDT)
        dgu_ref[0] = dgate
        dgu_ref[1] = dup
        _acc(acc, _dot_nt(dgate, wgu_ref[0]) + _dot_nt(dup, wgu_ref[1]), j == 0)

        @pl.when(j == 3)
        def _():
            dx_ref[...] = ALPHA * dz_scr[...] + acc[...]

    row = pl.BlockSpec((tm, D), lambda i, j: (i, 0))
    vec = pl.BlockSpec((1, D), lambda i, j: (0, 0))
    return _call(
        after, body, grid=(S // tm, 4),
        in_specs=[row, row, pl.BlockSpec((2, None, tm, FFN_B), lambda i, j: (0, j, i, 0)),
                  pl.BlockSpec((2, None, D, FFN_B), lambda i, j: (0, j, 0, 0)),
                  pl.BlockSpec((None, FFN_B, D), lambda i, j: (j, 0, 0)), vec],
        out_specs=[row, row, pl.BlockSpec((2, None, tm, FFN_B), lambda i, j: (0, j, i, 0)), vec, vec],
        out_shape=[_sds((S, D), F32), _sds((S, D), CDT), _sds((2, 4, S, FFN_B), CDT), _sds((1, D), F32),
                   _sds((1, D), F32)],
        scratch_shapes=[pltpu.VMEM((tm, D), F32), pltpu.VMEM((tm, D), F32)],
        compiler_params=_params(("arbitrary", "arbitrary")), name=name)(dy, z, gu, wgu, wdn, gain)


def _mixout_bwd(dy, z, gain, w3, du_dtype, name, after=None):
    S = dy.shape[0]
    G, Kb, _ = w3.shape
    tm = _tile(S)

    def body(dy_ref, z_ref, g_ref, w_ref, dz_ref, dzb_ref, du_ref, dgain_ref, dbeta_ref, dbias_ref):
        first = pl.program_id(0) == 0
        dy_ = dy_ref[...]
        dz, xhat = _ln_bwd(z_ref[...], g_ref[...], dy_)
        dz_ref[...] = dz
        dzb = dz.astype(CDT)
        dzb_ref[...] = dzb
        for g in range(G):
            du_ref[g] = _dot_nt(dzb, w_ref[g]).astype(du_ref.dtype)
        _acc(dgain_ref, _colsum(dy_ * xhat), first)
        _acc(dbeta_ref, _colsum(dy_), first)
        _acc(dbias_ref, _colsum(dz), first)

    row = pl.BlockSpec((tm, D), lambda i: (i, 0))
    vec = pl.BlockSpec((1, D), lambda i: (0, 0))
    return _call(
        after, body, grid=(S // tm,), in_specs=[row, row, vec, pl.BlockSpec((G, Kb, D), lambda i: (0, 0, 0))],
        out_specs=[row, row, pl.BlockSpec((G, tm, Kb), lambda i: (0, i, 0)), vec, vec, vec],
        out_shape=[_sds((S, D), F32), _sds((S, D), CDT), _sds((G, S, Kb), du_dtype)] + [_sds((1, D), F32)] * 3,
        compiler_params=_params(("arbitrary",)), name=name)(dy, z, gain, w3)


def _qkv_bwd(dz, dq4, dkv4, wq4, wkv4, name, after=None):
    S = dz.shape[0]
    tm = _tile(S)
    HQ, HK = dq4.shape[0], dkv4.shape[0]

    def body(dz_ref, dq_ref, dkv_ref, wq_ref, wkv_ref, dx_ref, dkvb_ref):
        first = pl.program_id(0) == 0
        acc = ALPHA * dz_ref[...]
        for h in range(HQ):
            acc = acc + _dot_nt(dq_ref[h], wq_ref[h])
        for h in range(HK):
            acc = acc + _dot_nt(dkv_ref[h], wkv_ref[h])
        dx_ref[...] = acc
        for h in range(HK):
            _acc(dkvb_ref.at[h], _colsum(dkv_ref[h]), first)

    row = pl.BlockSpec((tm, D), lambda i: (i, 0))
    return _call(
        after, body, grid=(S // tm,),
        in_specs=[row, pl.BlockSpec((HQ, tm, ATT_HD), lambda i: (0, i, 0)), pl.BlockSpec((HK, tm, ATT_HD), lambda i: (0, i, 0)),
                  pl.BlockSpec((HQ, D, ATT_HD), lambda i: (0, 0, 0)), pl.BlockSpec((HK, D, ATT_HD), lambda i: (0, 0, 0))],
        out_specs=[row, pl.BlockSpec((HK, 1, ATT_HD), lambda i: (0, 0, 0))],
        out_shape=[_sds((S, D), F32), _sds((HK, 1, ATT_HD), F32)],
        compiler_params=_params(("arbitrary",)), name=name)(dz, dq4, dkv4, wq4, wkv4)


def _inproj_bwd(dz, dproj, wain, name, after=None):
    S = dz.shape[0]
    tm = _tile(S)
    nb = wain.shape[-1]

    def body(dz_ref, dp_ref, w_ref, dx_ref, acc):
        j = pl.program_id(1)
        _acc(acc, _dot_nt(dp_ref[...], w_ref[...]), j == 0)

        @pl.when(j == N_DEV - 1)
        def _():
            dx_ref[...] = ALPHA * dz_ref[...] + acc[...]

    row = pl.BlockSpec((tm, D), lambda i, j: (i, 0))
    return _call(
        after, body, grid=(S // tm, N_DEV),
        in_specs=[row, pl.BlockSpec((None, tm, nb), lambda i, j: (j // 2, i, j % 2)),
                  pl.BlockSpec((None, D, nb), lambda i, j: (j, 0, 0))],
        out_specs=row, out_shape=_sds((S, D), F32), scratch_shapes=[pltpu.VMEM((tm, D), F32)],
        compiler_params=_params(("arbitrary", "arbitrary")), name=name)(dz, dproj, wain)


def _hp(a, b, dims):
    ah = a.astype(CDT)
    al = (a - ah.astype(F32)).astype(CDT)
    bh = b.astype(CDT)
    bl = (b - bh.astype(F32)).astype(CDT)
    dg = lambda u, w: lax.dot_general(u, w, (dims, ((), ())), preferred_element_type=F32)
    return dg(ah, bh) + (dg(ah, bl) + dg(al, bh))


def _hdot(a, b):
    return _hp(a, b, ((1,), (0,)))


def _hdot_nt(a, b):
    return _hp(a, b, ((1,), (1,)))


def _hdot_tn(a, b):
    return _hp(a, b, ((0,), (0,)))


def _tri_dot(tri, x):
    hi = x.astype(CDT)
    r1 = x - hi.astype(F32)
    mid = r1.astype(CDT)
    lo = (r1 - mid.astype(F32)).astype(CDT)
    t = tri.astype(CDT)
    return (jnp.dot(t, hi, preferred_element_type=F32) + jnp.dot(t, mid, preferred_element_type=F32)
            + jnp.dot(t, lo, preferred_element_type=F32))


def _hg_gates(q, f, alb_ref):
    a0, a1 = alb_ref[0:1, :], alb_ref[1:2, :]
    mx = jnp.maximum(a0, a1)
    e0, e1 = jnp.exp(a0 - mx), jnp.exp(a1 - mx)
    lb = e0 / (e0 + e1)
    sig = _sigmoid(f)
    forget = lb + (1.0 - lb) * sig
    k = (1.0 - lb) * _sigmoid(-f)
    qs = q * _sigmoid(q) * (HG_DK ** -0.5)
    return qs, k, jnp.log(forget), sig, lb, forget


def _hg_intra(qs, k, b, b_scr):
    b_scr[...] = b
    bm = b_scr[pl.ds(HG_CH // 2 - 1, 1), :]
    bl = b_scr[pl.ds(HG_CH - 1, 1), :]
    eb = jnp.exp(b)
    qb = qs * eb
    e_q = jnp.exp(b - bm)
    e_k = jnp.exp(bm - b)
    e_d = jnp.exp(bl - b)
    return qb, qs * e_q, k * e_k, k * e_d, jnp.exp(bl), eb, e_q, e_k, e_d


def _hgrn_fwd(proj, alb, ngain):
    S = proj.shape[1]
    nc = S // HG_CH
    wb = HG_HPB * HG_DK

    def body(pj_ref, alb_ref, ng_ref, o_ref, y_ref, st_ref, st_scr, b_scr):
        n = pl.program_id(1)

        @pl.when(n == 0)
        def _():
            st_scr[...] = jnp.zeros_like(st_scr)

        r = lax.broadcasted_iota(jnp.int32, (HG_CH, HG_CH), 0)
        c = lax.broadcasted_iota(jnp.int32, (HG_CH, HG_CH), 1)
        causal = r >= c
        tri = causal.astype(F32)
        for j in range(HG_HPB):
            lanes = pl.ds(j * HG_DK, HG_DK)
            q, f, v, g = pj_ref[0, :, lanes], pj_ref[1, :, lanes], pj_ref[2, :, lanes], pj_ref[3, :, lanes]
            qs, k, logf, _, _, _ = _hg_gates(q, f, alb_ref.at[:, lanes])
            b = _tri_dot(tri, logf)
            qb, qt, kt, kd, ebl, _, _, _, _ = _hg_intra(qs, k, b, b_scr.at[j])
            st = st_scr[j]
            st_ref[j] = st
            a = jnp.where(causal, _hdot_nt(qt, kt), 0.0)
            o = _hdot(a, v) + _hdot_nt(qb, st)
            st_scr[j] = st * ebl + _hdot_tn(v, kd)
            o_ref[:, lanes] = o
            rinv = lax.rsqrt(jnp.mean(o * o, axis=-1, keepdims=True) + RMS_EPS)
            y_ref[:, lanes] = (o * rinv * ng_ref[...] * (g * _sigmoid(g))).astype(CDT)

    blk = pl.BlockSpec((HG_CH, wb), lambda h, n: (n, h))
    return pl.pallas_call(
        body, grid=(HG_H // HG_HPB, nc),
        in_specs=[pl.BlockSpec((4, HG_CH, wb), lambda h, n: (0, n, h)), pl.BlockSpec((2, wb), lambda h, n: (0, h)),
                  pl.BlockSpec((1, HG_DK), lambda h, n: (0, 0))],
        out_specs=[blk, blk, pl.BlockSpec((HG_HPB, None, HG_DK, HG_DK), lambda h, n: (h, n, 0, 0))],
        out_shape=[_sds((S, D), F32), _sds((S, D), CDT), _sds((HG_H, nc, HG_DK, HG_DK), F32)],
        scratch_shapes=[pltpu.VMEM((HG_HPB, HG_DK, HG_DK), F32), pltpu.VMEM((HG_HPB, HG_CH, HG_DK), F32)],
        compiler_params=_params(("arbitrary", "arbitrary")), name="hgrn_fwd")(proj, alb, ngain)


def _hgrn_bwd(proj, alb, ngain, o, states, dy):
    S = proj.shape[1]
    nc = S // HG_CH
    wb = HG_HPB * HG_DK

    def body(pj_ref, alb_ref, ng_ref, o_ref, st_ref, dy_ref, dpj_ref, dalb_ref, dng_ref, dst_scr, b_scr):
        h, n = pl.program_id(0), pl.program_id(1)

        @pl.when(n == 0)
        def _():
            dst_scr[...] = jnp.zeros_like(dst_scr)

        ng = ng_ref[...]
        r = lax.broadcasted_iota(jnp.int32, (HG_CH, HG_CH), 0)
        c = lax.broadcasted_iota(jnp.int32, (HG_CH, HG_CH), 1)
        causal = r >= c
        tri, tri_rev = causal.astype(F32), (r <= c).astype(F32)
        dng = None
        for j in range(HG_HPB):
            lanes = pl.ds(j * HG_DK, HG_DK)
            q, f, v, g = pj_ref[0, :, lanes], pj_ref[1, :, lanes], pj_ref[2, :, lanes], pj_ref[3, :, lanes]
            o_ = o_ref[:, lanes]
            dy_ = dy_ref[:, lanes]
            sg = _sigmoid(g)
            rinv = lax.rsqrt(jnp.mean(o_ * o_, axis=-1, keepdims=True) + RMS_EPS)
            nrm = o_ * rinv
            dr = dy_ * (g * sg)
            dg = dy_ * nrm * ng * (sg * (1.0 + g * (1.0 - sg)))
            dn = dr * ng
            do = rinv * (dn - nrm * jnp.mean(dn * nrm, axis=-1, keepdims=True))
            dng = _colsum(dr * nrm) if dng is None else dng + _colsum(dr * nrm)
            qs, k, logf, sig, lb, forget = _hg_gates(q, f, alb_ref.at[:, lanes])
            b = _tri_dot(tri, logf)
            qb, qt, kt, kd, ebl, eb, e_q, e_k, e_d = _hg_intra(qs, k, b, b_scr.at[j])
            st = st_ref[j]
            dstn = dst_scr[j]
            a = jnp.where(causal, _hdot_nt(qt, kt), 0.0)
            da = jnp.where(causal, _hdot_nt(do, v), 0.0)
            dv = _hdot_tn(a, do) + _hdot_nt(kd, dstn)
            dqb = _hdot(do, st)
            dkd = _hdot(v, dstn)
            dqt = _hdot(da, kt)
            dkt = _hdot_tn(da, qt)
            dbl = _colsum(dkd * kd) + ebl * _colsum(dstn * st)
            dst_scr[j] = dstn * ebl + _hdot_tn(do, qb)
            dqs = dqt * e_q + dqb * eb
            dk = dkt * e_k + dkd * e_d
            db = dqt * qt + dqb * qb - dkt * kt - dkd * kd
            dlogf = _tri_dot(tri_rev, db) + dbl
            dforget = dlogf / forget
            dsig = (1.0 - lb) * (dforget - dk)
            df = dsig * sig * (1.0 - sig)
            dlb = _colsum((dforget - dk) * (1.0 - sig))
            sq = _sigmoid(q)
            dq = dqs * (HG_DK ** -0.5) * (sq * (1.0 + q * (1.0 - sq)))
            dpj_ref[0, :, lanes] = dq.astype(CDT)
            dpj_ref[1, :, lanes] = df.astype(CDT)
            dpj_ref[2, :, lanes] = dv.astype(CDT)
            dpj_ref[3, :, lanes] = dg.astype(CDT)
            da0 = dlb * lb * (1.0 - lb)
            _acc(dalb_ref.at[pl.ds(0, 1), lanes], da0, n == 0)
            _acc(dalb_ref.at[pl.ds(1, 1), lanes], -da0, n == 0)
        _acc(dng_ref, dng, jnp.logical_and(h == 0, n == 0))

    blk = pl.BlockSpec((HG_CH, wb), lambda h, n: (nc - 1 - n, h))
    pj = pl.BlockSpec((4, HG_CH, wb), lambda h, n: (0, nc - 1 - n, h))
    alb_blk = pl.BlockSpec((2, wb), lambda h, n: (0, h))
    ng_blk = pl.BlockSpec((1, HG_DK), lambda h, n: (0, 0))
    return pl.pallas_call(
        body, grid=(HG_H // HG_HPB, nc),
        in_specs=[pj, alb_blk, ng_blk, blk,
                  pl.BlockSpec((HG_HPB, None, HG_DK, HG_DK), lambda h, n: (h, nc - 1 - n, 0, 0)), blk],
        out_specs=[pj, alb_blk, ng_blk],
        out_shape=[_sds((4, S, D), CDT), _sds((2, D), F32), _sds((1, HG_DK), F32)],
        scratch_shapes=[pltpu.VMEM((HG_HPB, HG_DK, HG_DK), F32), pltpu.VMEM((HG_HPB, HG_CH, HG_DK), F32)],
        compiler_params=_params(("arbitrary", "arbitrary")), name="hgrn_bwd")(proj, alb, ngain, o, states, dy)


def _slope(h):
    return 2.0 ** (-8.0 * (h + 1) / ATT_QH)


def _attn_mask(n):
    qi = lax.broadcasted_iota(jnp.int32, (WINDOW, 2 * WINDOW), 0)
    si = lax.broadcasted_iota(jnp.int32, (WINDOW, 2 * WINDOW), 1)
    dist = qi - si + WINDOW
    valid = (dist >= 0) & (dist < WINDOW) & (n * WINDOW - WINDOW + si >= 0)
    return valid, dist.astype(F32)


def _attn_probs(qh, kh, sink, slope, valid, distf):
    s = _dot_nt(qh, kh) * (ATT_HD ** -0.5) - slope * distf
    s = jnp.where(valid, s, NEG)
    m = jnp.maximum(jnp.max(s, axis=-1, keepdims=True), sink)
    e = jnp.exp(s - m)
    es = jnp.exp(sink - m)
    inv = 1.0 / (jnp.sum(e, axis=-1, keepdims=True) + es)
    return e * inv, es * inv


def _attn_specs(S):
    nb = S // WINDOW
    cur = lambda H: pl.BlockSpec((H, WINDOW, ATT_HD), lambda n: (0, n, 0))
    prev = lambda H: pl.BlockSpec((H, WINDOW, ATT_HD), lambda n: (0, jnp.maximum(n - 1, 0), 0))
    return nb, cur, prev


def _attn_fwd(q4, kv4, sinks):
    S = q4.shape[1]
    nb, cur, prev = _attn_specs(S)

    def body(sink_ref, q_ref, kvc_ref, kvp_ref, o_ref):
        valid, distf = _attn_mask(pl.program_id(0))
        for h in range(ATT_QH):
            kvh = h // ATT_G
            kh = jnp.concatenate([kvp_ref[kvh], kvc_ref[kvh]], axis=0)
            vh = jnp.concatenate([kvp_ref[ATT_KVH + kvh], kvc_ref[ATT_KVH + kvh]], axis=0)
            p, _ = _attn_probs(q_ref[h], kh, sink_ref[0, h], _slope(h), valid, distf)
            o_ref[h] = _dot(p, vh).astype(CDT)

    return pl.pallas_call(
        body, grid=(nb,),
        in_specs=[pl.BlockSpec(memory_space=pltpu.SMEM), cur(ATT_QH), cur(2 * ATT_KVH), prev(2 * ATT_KVH)],
        out_specs=cur(ATT_QH), out_shape=_sds((ATT_QH, S, ATT_HD), CDT),
        compiler_params=_params(("arbitrary",)), name="attn_fwd")(sinks, q4, kv4, kv4)


def _attn_bwd(q4, kv4, sinks, do4):
    S = q4.shape[1]
    nb, cur, prev = _attn_specs(S)

    def body(sink_ref, q_ref, kvc_ref, kvp_ref, do_ref, dq_ref, dkv_ref, dbq_ref, dsink_ref):
        n = pl.program_id(0)
        first = n == 0

        @pl.when(first)
        def _():
            dkv_ref[...] = jnp.zeros_like(dkv_ref)
            dsink_ref[...] = jnp.zeros_like(dsink_ref)

        valid, distf = _attn_mask(n)
        lane = lax.broadcasted_iota(jnp.int32, (1, 128), 1)
        rows_cur = pl.ds(pl.multiple_of(n * WINDOW, WINDOW), WINDOW)
        rows_prev = pl.ds(pl.multiple_of(jnp.maximum(n - 1, 0) * WINDOW, WINDOW), WINDOW)
        dsinks = jnp.zeros((1, 128), F32)
        for kvh in range(ATT_KVH):
            kh = jnp.concatenate([kvp_ref[kvh], kvc_ref[kvh]], axis=0)
            vh = jnp.concatenate([kvp_ref[ATT_KVH + kvh], kvc_ref[ATT_KVH + kvh]], axis=0)
            dk = dv = None
            for h in range(kvh * ATT_G, (kvh + 1) * ATT_G):
                qh = q_ref[h]
                doh = do_ref[h]
                p, ps = _attn_probs(qh, kh, sink_ref[0, h], _slope(h), valid, distf)
                dp = _dot_nt(doh, vh)
                dd = jnp.sum(p * dp, axis=-1, keepdims=True)
                ds = p * (dp - dd)
                dsinks = dsinks + jnp.where(lane == h, -jnp.sum(ps * dd, axis=0, keepdims=True), 0.0)
                dqh = _dot(ds, kh) * (ATT_HD ** -0.5)
                dq_ref[h] = dqh.astype(CDT)
                _acc(dbq_ref.at[h], _colsum(dqh), first)
                dkh = _dot_tn(ds, qh) * (ATT_HD ** -0.5)
                dvh = _dot_tn(p, doh)
                dk = dkh if dk is None else dk + dkh
                dv = dvh if dv is None else dv + dvh
            dkv_ref[kvh, rows_prev, :] += dk[:WINDOW]
            dkv_ref[kvh, rows_cur, :] += dk[WINDOW:]
            dkv_ref[ATT_KVH + kvh, rows_prev, :] += dv[:WINDOW]
            dkv_ref[ATT_KVH + kvh, rows_cur, :] += dv[WINDOW:]
        dsink_ref[...] += dsinks

    return pl.pallas_call(
        body, grid=(nb,),
        in_specs=[pl.BlockSpec(memory_space=pltpu.SMEM), cur(ATT_QH), cur(2 * ATT_KVH), prev(2 * ATT_KVH), cur(ATT_QH)],
        out_specs=[cur(ATT_QH), pl.BlockSpec((2 * ATT_KVH, S, ATT_HD), lambda n: (0, 0, 0)),
                   pl.BlockSpec((ATT_QH, 1, ATT_HD), lambda n: (0, 0, 0)), pl.BlockSpec((1, 128), lambda n: (0, 0))],
        out_shape=[_sds((ATT_QH, S, ATT_HD), CDT), _sds((2 * ATT_KVH, S, ATT_HD), F32), _sds((ATT_QH, 1, ATT_HD), F32),
                   _sds((1, 128), F32)],
        compiler_params=_params(("arbitrary",)), name="attn_bwd")(sinks, q4, kv4, kv4, do4)


def _local_step(x, p, target, getw, sm, emit):
    S = x.shape[0]
    vec = lambda a: a.reshape(1, -1)
    ln_g = lambda l, k: vec(sm["ln_gain"][l, k])
    ln_b = lambda l, k: vec(sm["ln_bias"][l, k])
    xb = x.astype(CDT)
    pb = p.astype(CDT)

    proj = _mm_nn(xb, getw("a_w_in", None), (4, S, D), (None, _tile(S), 512), lambda g, i: (g // 2, i, g % 2), F32,
                  name="a_in")
    o_a, y_a, states = _hgrn_fwd(proj, sm["a_lower_bound"], sm["a_norm_gain"])
    zeros = jnp.zeros((1, D), F32)
    z = [[None] * 3 for _ in range(2)]
    xs = [[None] * 3 for _ in range(2)]
    xbs = [[None] * 3 for _ in range(2)]
    z[0][0], xs[0][0], xbs[0][0] = _mixout_ln(y_a[None], getw("a_w_out", y_a)[None], zeros, x, ln_g(0, 0), ln_b(0, 0),
                                              "a_out_ln")
    gu, hid, sgs, ups = [None, None], [None, None], [None, None], [None, None]

    def ffn_ple(l):
        wgu = getw(f"gu{l}", xbs[l][0])
        gu[l], hid[l], z[l][1], xs[l][1], xbs[l][1] = _ffn_fwd(
            xs[l][0], xbs[l][0], wgu, getw(f"dn{l}", None), ln_g(l, 1), ln_b(l, 1), f"ffn_fwd{l}")
        sgs[l], ups[l], z[l][2], xs[l][2], xbs[l][2] = _ple_fwd(
            xs[l][1], xbs[l][1], pb[l], getw(f"pg{l}", None), vec(sm["ple_b_gate"][l]), getw(f"pu{l}", None), ln_g(l, 2),
            ln_b(l, 2), f"ple_fwd{l}")

    ffn_ple(0)
    x3, x3b = xs[0][2], xbs[0][2]
    w_kv, w_q, w_bo = getw("kv_w", x3b), getw("b_w_q", None), getw("b_w_out", None)
    kv4 = _mm_nn(x3b, w_kv, (2 * ATT_KVH, S, ATT_HD), (None, _tile(S), ATT_HD), lambda g, i: (g, i, 0), CDT,
                 bias3=sm["kv_b"].reshape(2 * ATT_KVH, 1, ATT_HD), name="kv_proj")
    q4 = _mm_nn(x3b, w_q, (ATT_QH, S, ATT_HD), (None, _tile(S), ATT_HD), lambda g, i: (g, i, 0), CDT,
                bias3=sm["b_b_q"].reshape(ATT_QH, 1, ATT_HD), name="q_proj")
    o4 = _attn_fwd(q4, kv4, sm["b_sinks"])
    z[1][0], xs[1][0], xbs[1][0] = _mixout_ln(o4, w_bo, sm["b_b_out"], x3, ln_g(1, 0), ln_b(1, 0), "b_out_ln")
    ffn_ple(1)
    loss, dy = _loss_fwd_bwd(xs[1][2], target)

    gs = {}
    d_ln_g = [[None] * 3 for _ in range(2)]
    d_ln_b = [[None] * 3 for _ in range(2)]
    g_bg = [None, None]

    def ffn_ple_bwd(l, dy):
        dx2, dgl, dup, d_ln_g[l][2], d_ln_b[l][2], g_bg[l] = _ple_bwd(dy, z[l][2], sgs[l], ups[l], ln_g(l, 2),
                                                                     getw(f"pg{l}", None), f"ple_bwd{l}")
        g_pg = _wgrad(xbs[l][1][None], dgl[None], f"g_ple_gate{l}")[0]
        g_pu = _wgrad(pb[l][None], dup[None], f"g_ple_up{l}")[0]
        dx1, dzb, dgu, d_ln_g[l][1], d_ln_b[l][1] = _ffn_bwd(dx2, z[l][1], gu[l], getw(f"gu{l}", None),
                                                           getw(f"dn{l}", None), ln_g(l, 1), f"ffn_bwd{l}")
        g_dn = _wgrad(hid[l], dzb[None], f"g_ffn_down{l}")
        g_gu = _wgrad(xbs[l][0][None], dgu.reshape(8, S, FFN_B), f"g_ffn_gate_up{l}")
        return dx1, emit({f"pg{l}": g_pg, f"pu{l}": g_pu, f"dn{l}": g_dn, f"gu{l}": g_gu})

    dx1, tok = ffn_ple_bwd(1, dy)
    dz, dzb, do4, d_ln_g[1][0], d_ln_b[1][0], gs["b_b_out"] = _mixout_bwd(dx1, z[1][0], ln_g(1, 0), w_bo, CDT,
                                                                         "b_out_bwd", after=tok)
    g_bo = _wgrad(o4, dzb[None], "g_b_w_out")
    dq4, dkv4, dbq, dsinks = _attn_bwd(q4, kv4, sm["b_sinks"], do4)
    gs["b_b_q"] = dbq
    gs["b_sinks"] = dsinks
    g_q = _wgrad(x3b[None], dq4, "g_b_w_q")
    g_kv = _wgrad(x3b[None], dkv4.astype(CDT), "g_kv_w")
    tok = emit({"b_w_out": g_bo, "b_w_q": g_q, "kv_w": g_kv})
    dx3, gs["kv_b"] = _qkv_bwd(dz, dq4, dkv4, w_q, w_kv, "qkv_bwd", after=tok)
    dx1, tok = ffn_ple_bwd(0, dx3)
    w_ao = getw("a_w_out", None)
    dz, dzb, dyr, d_ln_g[0][0], d_ln_b[0][0], _ = _mixout_bwd(dx1, z[0][0], ln_g(0, 0), w_ao[None], F32, "a_out_bwd",
                                                              after=tok)
    g_ao = _wgrad(y_a[None], dzb[None], "g_a_w_out")[0]
    dproj, gs["a_lower_bound"], gs["a_norm_gain"] = _hgrn_bwd(proj, sm["a_lower_bound"], sm["a_norm_gain"], o_a, states,
                                                              dyr[0])
    tk = lambda t: (None, t, D)
    g_ain = _mm_tn(xb[None], dproj, N_DEV, lambda g, k: (0, k, 0), lambda g, k: (g // 2, k, g % 2),
                   tk, lambda t: (None, t, 512), (N_DEV, D, 512), (None, D, 512), lambda g, k: (g, 0, 0), name="g_a_w_in")
    tok = emit({"a_w_out": g_ao, "a_w_in": g_ain})
    grad_x = _inproj_bwd(dz, dproj, getw("a_w_in", None), "a_in_bwd", after=tok)
    gs["ple_b_gate"] = jnp.concatenate(g_bg, axis=0)
    gs["ln_gain"] = jnp.stack([jnp.concatenate(r, axis=0) for r in d_ln_g])
    gs["ln_bias"] = jnp.stack([jnp.concatenate(r, axis=0) for r in d_ln_b])
    return loss, grad_x, gs


def _peer(k):
    x, y, c = lax.axis_index("x"), lax.axis_index("y"), lax.axis_index("c")
    px = 1 - x if k & 4 else x
    py = 1 - y if k & 2 else y
    pc = 1 - c if k & 1 else c
    return (px, py, pc), 4 * px + 2 * py + pc


def _my_index():
    return 4 * lax.axis_index("x") + 2 * lax.axis_index("y") + lax.axis_index("c")


def _exchange(srcs, dst_shapes, plan, name):
    n_src, n_piece = len(srcs), len(plan)

    def body(*refs):
        src_refs, dst_refs = refs[:n_src], refs[n_src:n_src + len(dst_shapes)]
        send_sems, recv_sems, local_sems = refs[n_src + len(dst_shapes):]
        me = _my_index()

        def at(ref, idx):
            return ref.at[idx] if idx else ref

        local = []
        for t, (si, sfn, di, dfn) in enumerate(plan):
            cp = pltpu.make_async_copy(at(src_refs[si], sfn(me)), at(dst_refs[di], dfn(me)), local_sems.at[t])
            cp.start()
            local.append(cp)
        sends = []
        for k in range(1, N_DEV):
            peer, pid = _peer(k)
            for t, (si, sfn, di, dfn) in enumerate(plan):
                cp = pltpu.make_async_remote_copy(
                    src_ref=at(src_refs[si], sfn(pid)), dst_ref=at(dst_refs[di], dfn(me)),
                    send_sem=send_sems.at[t * 7 + k - 1], recv_sem=recv_sems.at[t * 7 + k - 1],
                    device_id=peer, device_id_type=MESH)
                cp.start()
                sends.append(cp)
        for k in range(1, N_DEV):
            peer, pid = _peer(k)
            for t, (si, sfn, di, dfn) in enumerate(plan):
                pltpu.make_async_remote_copy(
                    src_ref=at(src_refs[si], sfn(me)), dst_ref=at(dst_refs[di], dfn(pid)),
                    send_sem=send_sems.at[t * 7 + k - 1], recv_sem=recv_sems.at[t * 7 + k - 1],
                    device_id=peer, device_id_type=MESH).wait_recv()
        for cp in sends:
            cp.wait_send()
        for cp in local:
            cp.wait()

    hbm = pl.BlockSpec(memory_space=pltpu.HBM)
    return pl.pallas_call(
        body, in_specs=[hbm] * n_src, out_specs=[hbm] * len(dst_shapes), out_shape=dst_shapes,
        scratch_shapes=[pltpu.SemaphoreType.DMA((7 * n_piece,)), pltpu.SemaphoreType.DMA((7 * n_piece,)),
                        pltpu.SemaphoreType.DMA((n_piece,))],
        name=name)(*srcs)


def _gather(shards, name):
    dsts = [_sds((N_DEV,) + a.shape, a.dtype) for a in shards]
    plan = [(i, lambda j: (), i, lambda s: (s,)) for i in range(len(shards))]
    return _exchange(shards, dsts, plan, name)


_HBM = pl.BlockSpec(memory_space=pltpu.HBM)
_SEM = pl.BlockSpec(memory_space=pltpu.SEMAPHORE)
_DATAFLOW = pltpu.SideEffectType.DATAFLOW_SIDE_EFFECTING


def _piece_copy(mode, src, land, send_sems, recv_sems, t, k, sender, receiver, peer):
    return pltpu.make_async_remote_copy(
        src_ref=src if mode == "gather" else src.at[receiver], dst_ref=land.at[sender],
        send_sem=send_sems.at[t * 7 + k - 1], recv_sem=recv_sems.at[t * 7 + k - 1], device_id=peer, device_id_type=MESH)


def _sequencer_exchange(srcs, mode, name, collective_id):
    n = len(srcs)
    land_shapes = [((N_DEV,) + a.shape) if mode == "gather" else a.shape for a in srcs]

    def body(*refs):
        src_refs, land_refs = refs[:n], refs[n:2 * n]
        send_sems, recv_sems, local_sems = refs[2 * n:]
        barrier = pltpu.get_barrier_semaphore()
        for k in range(1, N_DEV):
            pl.semaphore_signal(barrier, inc=1, device_id=_peer(k)[0], device_id_type=MESH)
        pl.semaphore_wait(barrier, N_DEV - 1)
        me = _my_index()
        local = []
        for i in range(n):
            cp = pltpu.make_async_copy(src_refs[i] if mode == "gather" else src_refs[i].at[me], land_refs[i].at[me],
                                       local_sems.at[i])
            cp.start()
            local.append(cp)
        for k in range(1, N_DEV):
            peer, pid = _peer(k)
            for t in range(n):
                _piece_copy(mode, src_refs[t], land_refs[t], send_sems, recv_sems, t, k, me, pid, peer).start()
        for k in range(1, N_DEV):
            peer, pid = _peer(k)
            for t in range(n):
                _piece_copy(mode, src_refs[t], land_refs[t], send_sems, recv_sems, t, k, pid, me, peer).wait_recv()
        for k in range(1, N_DEV):
            peer, pid = _peer(k)
            for t in range(n):
                _piece_copy(mode, src_refs[t], land_refs[t], send_sems, recv_sems, t, k, me, pid, peer).wait_send()
        for cp in local:
            cp.wait()

    return pl.kernel(
        body, out_type=[_sds(s, a.dtype) for s, a in zip(land_shapes, srcs)],
        mesh=plsc.ScalarSubcoreMesh(axis_name="sequencer", num_cores=1),
        scratch_types=[pltpu.SemaphoreType.DMA((7 * n,)), pltpu.SemaphoreType.DMA((7 * n,)), pltpu.SemaphoreType.DMA((n,))],
        compiler_params=pltpu.CompilerParams(collective_id=collective_id), name=name)(*srcs)


def _xstart(groups, mode, name, after=None):
    flat = [a for g in groups for a in g]
    n, ng = len(flat), len(groups)
    land_shapes = [((N_DEV,) + a.shape) if mode == "gather" else a.shape for a in flat]
    first = [sum(len(g) for g in groups[:i]) for i in range(ng)]
    extra = [] if after is None else [after]

    def body(*refs):
        srcs, lands = refs[:n], refs[n:2 * n]
        sems = refs[2 * n + len(extra):2 * n + len(extra) + 2 * ng]
        tok_ref, local_sems = refs[-2], refs[-1]
        me = _my_index()
        local = []
        for i in range(n):
            cp = pltpu.make_async_copy(srcs[i] if mode == "gather" else srcs[i].at[me], lands[i].at[me], local_sems.at[i])
            cp.start()
            local.append(cp)
        for cp in local:
            cp.wait()
        for gi, g in enumerate(groups):
            for k in range(1, N_DEV):
                peer, pid = _peer(k)
                for t in range(len(g)):
                    i = first[gi] + t
                    _piece_copy(mode, srcs[i], lands[i], sems[2 * gi], sems[2 * gi + 1], t, k, me, pid, peer).start()
        tok_ref[...] = jnp.zeros_like(tok_ref)

    sem_shapes = []
    for g in groups:
        sem_shapes += [pltpu.SemaphoreType.DMA((7 * len(g),))] * 2
    thru = [pltpu.HBM(a.shape, a.dtype) for a in flat] + [pltpu.HBM(s, a.dtype) for s, a in zip(land_shapes, flat)]
    outs = pl.pallas_call(
        body, in_specs=[_HBM] * (2 * n) + [pl.BlockSpec(memory_space=pl.ANY)] * len(extra),
        out_specs=[_SEM] * (2 * ng) + [_HBM] * (2 * n) + [pl.BlockSpec(memory_space=pltpu.VMEM)],
        out_shape=sem_shapes + thru + [_sds((8, 128), F32)],
        input_output_aliases={i: 2 * ng + i for i in range(2 * n)},
        scratch_shapes=[pltpu.SemaphoreType.DMA((n,))],
        compiler_params=pltpu.CompilerParams(has_side_effects=_DATAFLOW), name=name)(
            *[pltpu.with_memory_space_constraint(a, pltpu.HBM) for a in flat],
            *[pltpu.with_memory_space_constraint(lax.empty(s, a.dtype), pltpu.HBM) for s, a in zip(land_shapes, flat)], *extra)
    sems, srcs_thru, lands_thru = outs[:2 * ng], outs[2 * ng:2 * ng + n], outs[2 * ng + n:2 * ng + 2 * n]
    handles = [(sems[2 * gi], sems[2 * gi + 1], srcs_thru[first[gi]:first[gi] + len(g)],
                lands_thru[first[gi]:first[gi] + len(g)]) for gi, g in enumerate(groups)]
    return handles, outs[-1]


def _xwait(handle, mode, after, name):
    send_sems, recv_sems, srcs_thru, lands_thru = handle
    n = len(srcs_thru)

    def body(*refs):
        srcs, lands, send, recv = refs[:n], refs[n:2 * n], refs[2 * n], refs[2 * n + 1]
        me = _my_index()
        for k in range(1, N_DEV):
            peer, pid = _peer(k)
            for t in range(n):
                _piece_copy(mode, srcs[t], lands[t], send, recv, t, k, pid, me, peer).wait_recv()
        for k in range(1, N_DEV):
            peer, pid = _peer(k)
            for t in range(n):
                _piece_copy(mode, srcs[t], lands[t], send, recv, t, k, me, pid, peer).wait_send()

    extra = [] if after is None else [after]
    outs = pl.pallas_call(
        body, in_specs=[_HBM] * (2 * n) + [_SEM, _SEM] + [pl.BlockSpec(memory_space=pl.ANY)] * len(extra),
        out_specs=[_HBM] * (2 * n),
        out_shape=[pltpu.HBM(a.shape, a.dtype) for a in list(srcs_thru) + list(lands_thru)],
        input_output_aliases={i: i for i in range(2 * n)},
        compiler_params=pltpu.CompilerParams(has_side_effects=_DATAFLOW), name=name)(
            *srcs_thru, *lands_thru, send_sems, recv_sems, *extra)
    return outs[n:]


def _adamw(w, g, m, v):
    m = ADAM_B1 * m + (1.0 - ADAM_B1) * g
    v = ADAM_B2 * v + (1.0 - ADAM_B2) * (g * g)
    m_hat = m / (1.0 - ADAM_B1 ** ADAM_STEP)
    v_hat = v / (1.0 - ADAM_B2 ** ADAM_STEP)
    delta = -ADAM_LR * (m_hat / (jnp.sqrt(v_hat) + ADAM_EPS) + ADAM_WD * w)
    return delta, m, v


def _adam_big(w, parts, m, v, name):
    L, R, C = w.shape
    tr = _tile(R, (256, 128, 176, 64, 32, 16))
    nr = R // tr

    def body(w_ref, *refs):
        p_refs, (m_ref, v_ref, g_ref, d_ref, mo_ref, vo_ref) = refs[:L], refs[L:]
        for l in range(L):
            @pl.when(pl.program_id(0) == l)
            def _(p_ref=p_refs[l]):
                g = p_ref[0].astype(F32)
                for s in range(1, N_DEV):
                    g = g + p_ref[s].astype(F32)
                g_ref[...] = g
                d_ref[...], mo_ref[...], vo_ref[...] = _adamw(w_ref[...], g, m_ref[...], v_ref[...])

    row = pl.BlockSpec((None, tr, C), lambda l, i: (l, i, 0))
    park = lambda l_of: (lambda l, i: (0, jnp.where(l == l_of, i, 0 if l_of else nr - 1), 0))
    return pl.pallas_call(
        body, grid=(L, nr),
        in_specs=[row] + [pl.BlockSpec((N_DEV, tr, C), park(l)) for l in range(L)] + [row, row],
        out_specs=[row] * 4, out_shape=[_sds((L, R, C), F32)] * 4,
        compiler_params=_params(("arbitrary", "arbitrary")), name=name)(w, *parts, m, v)


SMALL = (("a_lower_bound", 2, True), ("ln_gain", 6, True), ("ln_bias", 6, True), ("a_norm_gain", 1, False),
         ("kv_b", 4, False), ("b_b_q", 8, False), ("b_sinks", 1, False), ("b_b_out", 8, False), ("ple_b_gate", 16, False))
SUBLANES = 8


def _slot(r):
    return -(-r // SUBLANES) * SUBLANES


SMALL_ROWS = sum(_slot(r) for _, r, _ in SMALL)
PART_ROWS = sum(r * N_DEV if sh else _slot(r) for _, r, sh in SMALL)


def _pack_rows(a, rows):
    flat = a.reshape(-1)
    return jnp.pad(flat, (0, rows * 128 - flat.size)).reshape(rows, 128)


def _pack_small(d):
    return jnp.concatenate([_pack_rows(d[n], _slot(r)) for n, r, _ in SMALL], axis=0)


def _unpack_small(packed, like):
    out, r0 = {}, 0
    for n, r, _ in SMALL:
        out[n] = packed[r0:r0 + r].reshape(-1)[:like[n].size].reshape(like[n].shape)
        r0 += _slot(r)
    return out


def _pack_partials(gs):
    blocks = []
    for n, r, sharded in SMALL:
        if sharded:
            blocks.append(gs[n].reshape(r * N_DEV, 128))
        else:
            blocks.append(_pack_rows(gs[n], _slot(r)))
    return jnp.concatenate(blocks, axis=0)


def _adam_small(parts, w, m, v):
    def body(p_ref, w_ref, m_ref, v_ref, g_ref, d_ref, mo_ref, vo_ref, tot):
        me = _my_index()
        t = p_ref[0]
        for s in range(1, N_DEV):
            t = t + p_ref[s]
        tot[...] = t
        g_ref[...] = jnp.zeros_like(g_ref)
        src, dst = 0, 0
        for _, r, sharded in SMALL:
            if sharded:
                for i in range(r):
                    g_ref[pl.ds(dst + i, 1), :] = tot[pl.ds(src + i * N_DEV + me, 1), :]
                src += r * N_DEV
            else:
                g_ref[pl.ds(dst, _slot(r)), :] = tot[pl.ds(src, _slot(r)), :]
                src += _slot(r)
            dst += _slot(r)
        d_ref[...], mo_ref[...], vo_ref[...] = _adamw(w_ref[...], g_ref[...], m_ref[...], v_ref[...])

    full = pl.BlockSpec((SMALL_ROWS, 128), lambda: (0, 0))
    return pl.pallas_call(
        body, in_specs=[pl.BlockSpec((N_DEV, PART_ROWS, 128), lambda: (0, 0, 0)), full, full, full],
        out_specs=[full] * 4, out_shape=[_sds((SMALL_ROWS, 128), F32)] * 4,
        scratch_shapes=[pltpu.VMEM((PART_ROWS, 128), F32)], name="adam_small")(parts, w, m, v)


WEIGHTS = ("a_w_in", "a_lower_bound", "a_norm_gain", "a_w_out", "kv_w", "kv_b", "b_w_q", "b_b_q", "b_sinks", "b_w_out",
           "b_b_out", "ffn_w_gate_up", "ffn_w_down", "ple_w_up", "ple_w_gate", "ple_b_gate", "ln_gain", "ln_bias")


def _heads_in(w, heads):
    return w.reshape(D, heads, ATT_HD).transpose(1, 0, 2)


def _heads_out(g):
    return g.transpose(1, 0, 2).reshape(D, -1)


GATHER_GROUPS = (("a_w_in",), ("a_w_out", "gu0", "dn0", "pu0", "pg0"), ("kv_w", "b_w_q", "b_w_out"),
                 ("gu1", "dn1", "pu1", "pg1"))
KERNEL_LAYOUT = {
    "a_w_in": lambda a: a,
    "a_w_out": lambda a: a.reshape(D, D),
    "kv_w": lambda a: _heads_in(a.reshape(D, 2 * ATT_KVH * ATT_HD), 2 * ATT_KVH),
    "b_w_q": lambda a: _heads_in(a.reshape(D, D), ATT_QH),
    "b_w_out": lambda a: a.reshape(ATT_QH, ATT_HD, D),
    "gu": lambda a: a.reshape(2, 4, D, FFN_B),
    "dn": lambda a: a.reshape(4, FFN_B, D),
    "pu": lambda a: a.transpose(1, 0, 2).reshape(PLE_DIM, D),
    "pg": lambda a: a.reshape(D, D),
}
_row_blocks = lambda a: a.reshape(N_DEV, -1, a.shape[-1])
OWNER_BLOCKS = {
    "a_w_in": lambda g: g,
    "a_w_out": _row_blocks,
    "kv_w": lambda g: _row_blocks(_heads_out(g)),
    "b_w_q": lambda g: _row_blocks(_heads_out(g)),
    "b_w_out": lambda g: _row_blocks(g.reshape(D, D)),
    "gu": lambda g: g,
    "dn": lambda g: _row_blocks(g.reshape(FFN_H, D)),
    "pu": lambda g: g.reshape(PLE_DIM, N_DEV, 128).transpose(1, 0, 2),
    "pg": _row_blocks,
}


def kernel(x, p, a_w_in, a_lower_bound, a_norm_gain, a_w_out, kv_w, kv_b, b_w_q, b_b_q, b_sinks, b_w_out, b_b_out, ffn_w_gate_up, ffn_w_down, ple_w_up, ple_w_gate, ple_b_gate, ln_gain, ln_bias, loss_target, m_a_w_in, m_a_lower_bound, m_a_norm_gain, m_a_w_out, m_kv_w, m_kv_b, m_b_w_q, m_b_b_q, m_b_sinks, m_b_w_out, m_b_b_out, m_ffn_w_gate_up, m_ffn_w_down, m_ple_w_up, m_ple_w_gate, m_ple_b_gate, m_ln_gain, m_ln_bias, v_a_w_in, v_a_lower_bound, v_a_norm_gain, v_a_w_out, v_kv_w, v_kv_b, v_b_w_q, v_b_b_q, v_b_sinks, v_b_w_out, v_b_b_out, v_ffn_w_gate_up, v_ffn_w_down, v_ple_w_up, v_ple_w_gate, v_ple_b_gate, v_ln_gain, v_ln_bias):
    given = dict(locals())
    w = {n: given[n] for n in WEIGHTS}
    m = {n: given["m_" + n] for n in WEIGHTS}
    v = {n: given["v_" + n] for n in WEIGHTS}
    small_sharded = jnp.concatenate([_pack_rows(a, SUBLANES) for a in (a_lower_bound, ln_gain, ln_bias)], axis=0)
    (g_small,) = _gather([small_sharded], "gather_small")
    shards = {"a_w_in": a_w_in[0], "a_w_out": a_w_out[0], "kv_w": kv_w, "b_w_q": b_w_q[0], "b_w_out": b_w_out[0]}
    for l in range(2):
        shards.update({f"gu{l}": ffn_w_gate_up[l], f"dn{l}": ffn_w_down[l], f"pu{l}": ple_w_up[l], f"pg{l}": ple_w_gate[l]})
    gathered = {}
    for gi, g in enumerate(GATHER_GROUPS):
        lands = _sequencer_exchange([shards[n].astype(CDT) for n in g], "gather", f"gather{gi}", gi)
        for n, a in zip(g, lands):
            gathered[n] = KERNEL_LAYOUT[n.rstrip("01")](a)

    def getw(key, after):
        return gathered[key]

    full_rows = lambda r0, r: g_small[:, r0:r0 + r].transpose(1, 0, 2).reshape(r, D)
    sm = {"a_lower_bound": full_rows(0, 2), "ln_gain": full_rows(SUBLANES, 6).reshape(2, 3, D),
          "ln_bias": full_rows(2 * SUBLANES, 6).reshape(2, 3, D), "a_norm_gain": a_norm_gain, "kv_b": kv_b,
          "b_b_q": b_b_q[0], "b_sinks": b_sinks, "b_b_out": b_b_out, "ple_b_gate": ple_b_gate}

    scatters = []

    def emit(grads):
        names = list(grads)
        (handle,), token = _xstart([[OWNER_BLOCKS[n.rstrip("01")](grads[n]) for n in names]], "scatter",
                                   f"scatter_start{len(scatters)}")
        scatters.append((names, handle))
        return token

    loss, grad_x, gs = _local_step(x[0], p[:, 0], loss_target[0], getw, sm, emit)

    (parts_small,) = _gather([_pack_partials(gs)], "gather_small_grads")
    packed = _adam_small(parts_small, _pack_small(w), _pack_small(m), _pack_small(v))
    small_out = [_unpack_small(a, w) for a in packed]
    out = {n: [s[n] for s in small_out] for n, _, _ in SMALL}

    parts, last = {}, grad_x
    adam_after = {1: (("kv_w", "kv_w"), ("b_w_q", "b_w_q"), ("b_w_out", "b_w_out")),
                  2: (("ffn_w_gate_up", "gu"), ("ffn_w_down", "dn"), ("ple_w_up", "pu"), ("ple_w_gate", "pg")),
                  3: (("a_w_out", "a_w_out"), ("a_w_in", "a_w_in"))}
    for i, (names, handle) in enumerate(scatters):
        parts.update(zip(names, _xwait(handle, "scatter", last, f"scatter_wait{i}")))
        for n, key in adam_after.get(i, ()):
            lrc = (1,) * (3 - w[n].ndim) + w[n].shape
            layers = [parts[key]] if key in parts else [parts[key + "0"], parts[key + "1"]]
            res = _adam_big(w[n].reshape(lrc), layers, m[n].reshape(lrc), v[n].reshape(lrc), "adam_" + n)
            out[n] = [r.reshape(w[n].shape) for r in res]
            last = res[3]

    loss = lax.psum(loss[0, 0], ("x", "y", "c"))
    res = [loss, grad_x[None]]
    for i in range(4):
        res += [out[n][i] for n in WEIGHTS]
    return tuple(res)
```

```python
import jax
import jax.numpy as jnp
from jax import lax
from jax.experimental import pallas as pl
from jax.experimental.pallas import tpu as pltpu
from jax.experimental.pallas import tpu_sc as plsc

F32 = jnp.float32
CDT = jnp.bfloat16

N_DEV = 8
D = 1024
HG_H, HG_DK, HG_CH = 8, 128, 64
HG_HPB = 4
ATT_HD, ATT_QH, ATT_KVH, ATT_G, WINDOW = 64, 16, 4, 4, 128
FFN_H = 2816
FFN_B = FFN_H // 4
PLE_DIM = 256
ALPHA = (2.0 * 2) ** 0.25
LN_EPS = 1e-5
RMS_EPS = 1e-6
ADAM_LR, ADAM_B1, ADAM_B2, ADAM_EPS, ADAM_WD, ADAM_STEP = 0.001, 0.9, 0.999, 1e-08, 0.01, 10
ROW_TILES = (256, 128, 64)
VMEM_LIMIT = 48 * 1024 * 1024
NEG = -1e30

MESH = pl.DeviceIdType.MESH


def _tile(n, cands=ROW_TILES):
    for t in cands:
        if n % t == 0:
            return t
    return n


def _sds(shape, dtype):
    return jax.ShapeDtypeStruct(tuple(shape), dtype)


def _params(sem):
    return pltpu.CompilerParams(dimension_semantics=sem, vmem_limit_bytes=VMEM_LIMIT)


def _dot(a, b):
    return jnp.dot(a.astype(CDT), b.astype(CDT), preferred_element_type=F32)


def _dot_nt(a, b):
    return lax.dot_general(a.astype(CDT), b.astype(CDT), (((1,), (1,)), ((), ())), preferred_element_type=F32)


def _dot_tn(a, b):
    return lax.dot_general(a.astype(CDT), b.astype(CDT), (((0,), (0,)), ((), ())), preferred_element_type=F32)


def _sigmoid(x):
    return jax.nn.sigmoid(x)


def _ln_fwd(z, g, b):
    mu = jnp.mean(z, axis=-1, keepdims=True)
    zc = z - mu
    var = jnp.mean(zc * zc, axis=-1, keepdims=True)
    return zc * lax.rsqrt(var + LN_EPS) * g + b


def _ln_bwd(z, g, dy):
    mu = jnp.mean(z, axis=-1, keepdims=True)
    zc = z - mu
    var = jnp.mean(zc * zc, axis=-1, keepdims=True)
    rstd = lax.rsqrt(var + LN_EPS)
    xhat = zc * rstd
    dxh = dy * g
    dz = rstd * (dxh - jnp.mean(dxh, axis=-1, keepdims=True) - xhat * jnp.mean(dxh * xhat, axis=-1, keepdims=True))
    return dz, xhat


def _colsum(x):
    return jnp.sum(x, axis=0, keepdims=True)


def _acc(ref, val, first):
    @pl.when(first)
    def _():
        ref[...] = val

    @pl.when(jnp.logical_not(first))
    def _():
        ref[...] += val


def _call(after, body, **kw):
    after = [] if after is None else list(after)
    if not after:
        return pl.pallas_call(body, **kw)
    kw["in_specs"] = [pl.BlockSpec(memory_space=pl.ANY)] * len(after) + list(kw["in_specs"])

    def ordered_body(*refs):
        body(*refs[len(after):])

    call = pl.pallas_call(ordered_body, **kw)
    return lambda *args: call(*after, *args)


def _mm_nn(a, b3, out_shape, oblock, omap, out_dtype, bias3=None, name="mm_nn"):
    M, K = a.shape
    G, _, Nb = b3.shape
    tm = _tile(M)

    def body(a_ref, b_ref, *rest):
        o_ref = rest[-1]
        acc = _dot(a_ref[...], b_ref[...])
        if bias3 is not None:
            acc = acc + rest[0][...]
        o_ref[...] = acc.astype(o_ref.dtype)

    in_specs = [pl.BlockSpec((tm, K), lambda g, i: (i, 0)), pl.BlockSpec((None, K, Nb), lambda g, i: (g, 0, 0))]
    args = [a, b3]
    if bias3 is not None:
        in_specs.append(pl.BlockSpec((None, 1, Nb), lambda g, i: (g, 0, 0)))
        args.append(bias3)
    return pl.pallas_call(
        body, grid=(G, M // tm), in_specs=in_specs, out_specs=pl.BlockSpec(oblock, omap),
        out_shape=_sds(out_shape, out_dtype), compiler_params=_params(("arbitrary", "arbitrary")), name=name)(*args)


def _mm_tn(a3, b3, G, amap, bmap, ablock, bblock, out_shape, oblock, omap, name="mm_tn"):
    S = a3.shape[1]
    tk = _tile(S, (512, 256, 128))
    Mo, No = oblock[-2], oblock[-1]

    def body(a_ref, b_ref, o_ref, acc):
        k = pl.program_id(1)
        _acc(acc, _dot_tn(a_ref[...], b_ref[...]), k == 0)

        @pl.when(k == pl.num_programs(1) - 1)
        def _():
            o_ref[...] = acc[...].astype(o_ref.dtype)

    return pl.pallas_call(
        body, grid=(G, S // tk),
        in_specs=[pl.BlockSpec(ablock(tk), amap), pl.BlockSpec(bblock(tk), bmap)],
        out_specs=pl.BlockSpec(oblock, omap), out_shape=_sds(out_shape, CDT),
        scratch_shapes=[pltpu.VMEM((Mo, No), F32)],
        compiler_params=_params(("arbitrary", "arbitrary")), name=name)(a3, b3)


def _wgrad(a3, b3, name):
    Ga, S, M = a3.shape
    Gb, _, N = b3.shape
    G = max(Ga, Gb)
    return _mm_tn(
        a3, b3, G,
        (lambda g, k: (g, k, 0)) if Ga > 1 else (lambda g, k: (0, k, 0)),
        (lambda g, k: (g, k, 0)) if Gb > 1 else (lambda g, k: (0, k, 0)),
        lambda tk: (None, tk, M), lambda tk: (None, tk, N),
        (G, M, N), (None, M, N), lambda g, k: (g, 0, 0), name=name)


def _mixout_ln(u3, w3, bias, xin, gain, beta, name):
    G, S, Kb = u3.shape
    tm = _tile(S)

    def body(u_ref, w_ref, b_ref, x_ref, g_ref, be_ref, z_ref, xo_ref, xob_ref):
        h = b_ref[...] + _dot(u_ref[0], w_ref[0])
        for g in range(1, G):
            h = h + _dot(u_ref[g], w_ref[g])
        z = ALPHA * x_ref[...] + h
        z_ref[...] = z
        y = _ln_fwd(z, g_ref[...], be_ref[...])
        xo_ref[...] = y
        xob_ref[...] = y.astype(CDT)

    row = pl.BlockSpec((tm, D), lambda i: (i, 0))
    vec = pl.BlockSpec((1, D), lambda i: (0, 0))
    return pl.pallas_call(
        body, grid=(S // tm,),
        in_specs=[pl.BlockSpec((G, tm, Kb), lambda i: (0, i, 0)), pl.BlockSpec((G, Kb, D), lambda i: (0, 0, 0)),
                  vec, row, vec, vec],
        out_specs=[row, row, row], out_shape=[_sds((S, D), F32), _sds((S, D), F32), _sds((S, D), CDT)],
        compiler_params=_params(("arbitrary",)), name=name)(u3, w3, bias, xin, gain, beta)


def _ffn_fwd(xin, xin_b, wgu, wdn, gain, beta, name):
    S = xin.shape[0]
    tm = _tile(S)

    def body(x_ref, xb_ref, wgu_ref, wdn_ref, g_ref, be_ref, gu_ref, hid_ref, z_ref, xo_ref, xob_ref, acc):
        j = pl.program_id(1)
        xb = xb_ref[...]
        gate = _dot(xb, wgu_ref[0])
        up = _dot(xb, wgu_ref[1])
        gu_ref[0] = gate
        gu_ref[1] = up
        hid = (gate * _sigmoid(gate) * up).astype(CDT)
        hid_ref[...] = hid
        _acc(acc, _dot(hid, wdn_ref[...]), j == 0)

        @pl.when(j == 3)
        def _():
            z = ALPHA * x_ref[...] + acc[...]
            z_ref[...] = z
            y = _ln_fwd(z, g_ref[...], be_ref[...])
            xo_ref[...] = y
            xob_ref[...] = y.astype(CDT)

    row = pl.BlockSpec((tm, D), lambda i, j: (i, 0))
    vec = pl.BlockSpec((1, D), lambda i, j: (0, 0))
    return pl.pallas_call(
        body, grid=(S // tm, 4),
        in_specs=[row, row, pl.BlockSpec((2, None, D, FFN_B), lambda i, j: (0, j, 0, 0)),
                  pl.BlockSpec((None, FFN_B, D), lambda i, j: (j, 0, 0)), vec, vec],
        out_specs=[pl.BlockSpec((2, None, tm, FFN_B), lambda i, j: (0, j, i, 0)),
                   pl.BlockSpec((None, tm, FFN_B), lambda i, j: (j, i, 0)), row, row, row],
        out_shape=[_sds((2, 4, S, FFN_B), F32), _sds((4, S, FFN_B), CDT), _sds((S, D), F32), _sds((S, D), F32),
                   _sds((S, D), CDT)],
        scratch_shapes=[pltpu.VMEM((tm, D), F32)],
        compiler_params=_params(("arbitrary", "arbitrary")), name=name)(xin, xin_b, wgu, wdn, gain, beta)


def _ple_fwd(xin, xin_b, p_b, wpg, bgate, wpu, gain, beta, name):
    S = xin.shape[0]
    tm = _tile(S)

    def body(x_ref, xb_ref, p_ref, wpg_ref, bg_ref, wpu_ref, g_ref, be_ref, sg_ref, up_ref, z_ref, xo_ref, xob_ref):
        sg = _sigmoid(_dot(xb_ref[...], wpg_ref[...]) + bg_ref[...])
        pb = p_ref[...]
        up = jnp.concatenate([_dot(pb, wpu_ref[j]) for j in range(N_DEV)], axis=-1)
        sg_ref[...] = sg
        up_ref[...] = up
        z = ALPHA * x_ref[...] + sg * up
        z_ref[...] = z
        y = _ln_fwd(z, g_ref[...], be_ref[...])
        xo_ref[...] = y
        xob_ref[...] = y.astype(CDT)

    row = pl.BlockSpec((tm, D), lambda i: (i, 0))
    vec = pl.BlockSpec((1, D), lambda i: (0, 0))
    return pl.pallas_call(
        body, grid=(S // tm,),
        in_specs=[row, row, pl.BlockSpec((tm, PLE_DIM), lambda i: (i, 0)), pl.BlockSpec((D, D), lambda i: (0, 0)), vec,
                  pl.BlockSpec((N_DEV, PLE_DIM, D // N_DEV), lambda i: (0, 0, 0)), vec, vec],
        out_specs=[row] * 5,
        out_shape=[_sds((S, D), F32)] * 4 + [_sds((S, D), CDT)],
        compiler_params=_params(("arbitrary",)), name=name)(xin, xin_b, p_b, wpg, bgate, wpu, gain, beta)


def _loss_fwd_bwd(y, target):
    S = y.shape[0]
    tm = _tile(S)

    def body(y_ref, t_ref, l_ref, dy_ref):
        e = y_ref[...] - t_ref[...]
        dy_ref[...] = e * (1.0 / D)
        part = 0.5 * jnp.sum(jnp.sum(e * e, axis=-1, keepdims=True) * (1.0 / D), axis=0, keepdims=True)
        _acc(l_ref, part, pl.program_id(0) == 0)

    row = pl.BlockSpec((tm, D), lambda i: (i, 0))
    return pl.pallas_call(
        body, grid=(S // tm,), in_specs=[row, row],
        out_specs=[pl.BlockSpec((1, 1), lambda i: (0, 0)), row],
        out_shape=[_sds((1, 1), F32), _sds((S, D), F32)],
        compiler_params=_params(("arbitrary",)), name="loss")(y, target)


def _ple_bwd(dy, z, sg, up, gain, wpg, name, after=None):
    S = dy.shape[0]
    tm = _tile(S)

    def body(dy_ref, z_ref, sg_ref, up_ref, g_ref, wpg_ref, dx_ref, dgl_ref, dup_ref, dgain_ref, dbeta_ref, dbg_ref):
        first = pl.program_id(0) == 0
        dy_ = dy_ref[...]
        dz, xhat = _ln_bwd(z_ref[...], g_ref[...], dy_)
        sg_ = sg_ref[...]
        dgl = dz * up_ref[...] * sg_ * (1.0 - sg_)
        dgl_ref[...] = dgl.astype(CDT)
        dup_ref[...] = (dz * sg_).astype(CDT)
        dx_ref[...] = ALPHA * dz + _dot_nt(dgl, wpg_ref[...])
        _acc(dgain_ref, _colsum(dy_ * xhat), first)
        _acc(dbeta_ref, _colsum(dy_), first)
        _acc(dbg_ref, _colsum(dgl), first)

    row = pl.BlockSpec((tm, D), lambda i: (i, 0))
    vec = pl.BlockSpec((1, D), lambda i: (0, 0))
    return _call(
        after, body, grid=(S // tm,), in_specs=[row, row, row, row, vec, pl.BlockSpec((D, D), lambda i: (0, 0))],
        out_specs=[row, row, row, vec, vec, vec],
        out_shape=[_sds((S, D), F32), _sds((S, D), CDT), _sds((S, D), CDT)] + [_sds((1, D), F32)] * 3,
        compiler_params=_params(("arbitrary",)), name=name)(dy, z, sg, up, gain, wpg)


def _ffn_bwd(dy, z, gu, wgu, wdn, gain, name, after=None):
    S = dy.shape[0]
    tm = _tile(S)

    def body(dy_ref, z_ref, gu_ref, wgu_ref, wdn_ref, g_ref, dx_ref, dzb_ref, dgu_ref, dgain_ref, dbeta_ref,
             dz_scr, acc):
        i, j = pl.program_id(0), pl.program_id(1)

        @pl.when(j == 0)
        def _():
            dy_ = dy_ref[...]
            dz, xhat = _ln_bwd(z_ref[...], g_ref[...], dy_)
            dz_scr[...] = dz
            dzb_ref[...] = dz.astype(CDT)
            _acc(dgain_ref, _colsum(dy_ * xhat), i == 0)
            _acc(dbeta_ref, _colsum(dy_), i == 0)

        dhid = _dot_nt(dz_scr[...], wdn_ref[...])
        gate, up = gu_ref[0], gu_ref[1]
        sg = _sigmoid(gate)
        dgate = (dhid * up * (sg * (1.0 + gate * (1.0 - sg)))).astype(CDT)
        dup = (dhid * (gate * sg)).astype(CDT)
        dgu_ref[0] = dgate
        dgu_ref[1] = dup
        _acc(acc, _dot_nt(dgate, wgu_ref[0]) + _dot_nt(dup, wgu_ref[1]), j == 0)

        @pl.when(j == 3)
        def _():
            dx_ref[...] = ALPHA * dz_scr[...] + acc[...]

    row = pl.BlockSpec((tm, D), lambda i, j: (i, 0))
    vec = pl.BlockSpec((1, D), lambda i, j: (0, 0))
    return _call(
        after, body, grid=(S // tm, 4),
        in_specs=[row, row, pl.BlockSpec((2, None, tm, FFN_B), lambda i, j: (0, j, i, 0)),
                  pl.BlockSpec((2, None, D, FFN_B), lambda i, j: (0, j, 0, 0)),
                  pl.BlockSpec((None, FFN_B, D), lambda i, j: (j, 0, 0)), vec],
        out_specs=[row, row, pl.BlockSpec((2, None, tm, FFN_B), lambda i, j: (0, j, i, 0)), vec, vec],
        out_shape=[_sds((S, D), F32), _sds((S, D), CDT), _sds((2, 4, S, FFN_B), CDT), _sds((1, D), F32),
                   _sds((1, D), F32)],
        scratch_shapes=[pltpu.VMEM((tm, D), F32), pltpu.VMEM((tm, D), F32)],
        compiler_params=_params(("arbitrary", "arbitrary")), name=name)(dy, z, gu, wgu, wdn, gain)


def _mixout_bwd(dy, z, gain, w3, du_dtype, name, after=None):
    S = dy.shape[0]
    G, Kb, _ = w3.shape
    tm = _tile(S)

    def body(dy_ref, z_ref, g_ref, w_ref, dz_ref, dzb_ref, du_ref, dgain_ref, dbeta_ref, dbias_ref):
        first = pl.program_id(0) == 0
        dy_ = dy_ref[...]
        dz, xhat = _ln_bwd(z_ref[...], g_ref[...], dy_)
        dz_ref[...] = dz
        dzb = dz.astype(CDT)
        dzb_ref[...] = dzb
        for g in range(G):
            du_ref[g] = _dot_nt(dzb, w_ref[g]).astype(du_ref.dtype)
        _acc(dgain_ref, _colsum(dy_ * xhat), first)
        _acc(dbeta_ref, _colsum(dy_), first)
        _acc(dbias_ref, _colsum(dz), first)

    row = pl.BlockSpec((tm, D), lambda i: (i, 0))
    vec = pl.BlockSpec((1, D), lambda i: (0, 0))
    return _call(
        after, body, grid=(S // tm,), in_specs=[row, row, vec, pl.BlockSpec((G, Kb, D), lambda i: (0, 0, 0))],
        out_specs=[row, row, pl.BlockSpec((G, tm, Kb), lambda i: (0, i, 0)), vec, vec, vec],
        out_shape=[_sds((S, D), F32), _sds((S, D), CDT), _sds((G, S, Kb), du_dtype)] + [_sds((1, D), F32)] * 3,
        compiler_params=_params(("arbitrary",)), name=name)(dy, z, gain, w3)


def _half_select(low):
    r = lax.broadcasted_iota(jnp.int32, (2 * ATT_HD, ATT_HD), 0)
    c = lax.broadcasted_iota(jnp.int32, (2 * ATT_HD, ATT_HD), 1)
    return (r == c + (0 if low else ATT_HD)).astype(CDT)


def _half_place(low):
    r = lax.broadcasted_iota(jnp.int32, (ATT_HD, 2 * ATT_HD), 0)
    c = lax.broadcasted_iota(jnp.int32, (ATT_HD, 2 * ATT_HD), 1)
    return (c == r + (0 if low else ATT_HD)).astype(CDT)


def _pair_lanes(even, odd):
    return (jnp.dot(even, _half_place(True), preferred_element_type=F32)
            + jnp.dot(odd, _half_place(False), preferred_element_type=F32)).astype(CDT)


def _proj_heads(a, w, bias, heads, name):
    S, K = a.shape
    N = heads * ATT_HD
    tm = _tile(S)

    def body(a_ref, w_ref, b_ref, o_ref):
        acc = (_dot(a_ref[...], w_ref[...]) + b_ref[...]).astype(CDT)
        sel = (_half_select(True), _half_select(False))
        for h in range(heads):
            pair = acc[:, (h // 2) * 2 * ATT_HD:(h // 2 + 1) * 2 * ATT_HD]
            o_ref[h] = jnp.dot(pair, sel[h % 2], preferred_element_type=F32).astype(CDT)

    return pl.pallas_call(
        body, grid=(S // tm,),
        in_specs=[pl.BlockSpec((tm, K), lambda i: (i, 0)), pl.BlockSpec((K, N), lambda i: (0, 0)),
                  pl.BlockSpec((1, N), lambda i: (0, 0))],
        out_specs=pl.BlockSpec((heads, tm, ATT_HD), lambda i: (0, i, 0)), out_shape=_sds((heads, S, ATT_HD), CDT),
        compiler_params=_params(("arbitrary",)), name=name)(a, w, bias)


def _qkv_bwd(dz, dq, dkv4, wq, wkv, name, after=None):
    S = dz.shape[0]
    tm = _tile(S)
    HK = dkv4.shape[0]
    NK = HK * ATT_HD

    def body(dz_ref, dq_ref, dkv_ref, wq_ref, wkv_ref, dx_ref, dkvn_ref, dkvb_ref):
        first = pl.program_id(0) == 0
        dkvn = jnp.concatenate([_pair_lanes(dkv_ref[2 * i].astype(CDT), dkv_ref[2 * i + 1].astype(CDT))
                                for i in range(HK // 2)], axis=-1)
        dkvn_ref[...] = dkvn
        dx_ref[...] = ALPHA * dz_ref[...] + _dot_nt(dq_ref[...], wq_ref[...]) + _dot_nt(dkvn, wkv_ref[...])
        for h in range(HK):
            _acc(dkvb_ref.at[h], _colsum(dkv_ref[h]), first)

    row = pl.BlockSpec((tm, D), lambda i: (i, 0))
    return _call(
        after, body, grid=(S // tm,),
        in_specs=[row, row, pl.BlockSpec((HK, tm, ATT_HD), lambda i: (0, i, 0)),
                  pl.BlockSpec((D, D), lambda i: (0, 0)), pl.BlockSpec((D, NK), lambda i: (0, 0))],
        out_specs=[row, pl.BlockSpec((tm, NK), lambda i: (i, 0)), pl.BlockSpec((HK, 1, ATT_HD), lambda i: (0, 0, 0))],
        out_shape=[_sds((S, D), F32), _sds((S, NK), CDT), _sds((HK, 1, ATT_HD), F32)],
        compiler_params=_params(("arbitrary",)), name=name)(dz, dq, dkv4, wq, wkv)


def _inproj_bwd(dz, dproj, wain, name, after=None):
    S = dz.shape[0]
    tm = _tile(S)
    nb = wain.shape[-1]

    def body(dz_ref, dp_ref, w_ref, dx_ref, acc):
        j = pl.program_id(1)
        _acc(acc, _dot_nt(dp_ref[...], w_ref[...]), j == 0)

        @pl.when(j == N_DEV - 1)
        def _():
            dx_ref[...] = ALPHA * dz_ref[...] + acc[...]

    row = pl.BlockSpec((tm, D), lambda i, j: (i, 0))
    return _call(
        after, body, grid=(S // tm, N_DEV),
        in_specs=[row, pl.BlockSpec((None, tm, nb), lambda i, j: (j // 2, i, j % 2)),
                  pl.BlockSpec((None, D, nb), lambda i, j: (j, 0, 0))],
        out_specs=row, out_shape=_sds((S, D), F32), scratch_shapes=[pltpu.VMEM((tm, D), F32)],
        compiler_params=_params(("arbitrary", "arbitrary")), name=name)(dz, dproj, wain)


def _hp(a, b, dims):
    ah = a.astype(CDT)
    al = (a - ah.astype(F32)).astype(CDT)
    bh = b.astype(CDT)
    bl = (b - bh.astype(F32)).astype(CDT)
    dg = lambda u, w: lax.dot_general(u, w, (dims, ((), ())), preferred_element_type=F32)
    return dg(ah, bh) + (dg(ah, bl) + dg(al, bh))


def _hdot(a, b):
    return _hp(a, b, ((1,), (0,)))


def _hdot_nt(a, b):
    return _hp(a, b, ((1,), (1,)))


def _hdot_tn(a, b):
    return _hp(a, b, ((0,), (0,)))


def _tri_dot(tri, x):
    hi = x.astype(CDT)
    r1 = x - hi.astype(F32)
    mid = r1.astype(CDT)
    lo = (r1 - mid.astype(F32)).astype(CDT)
    t = tri.astype(CDT)
    return (jnp.dot(t, hi, preferred_element_type=F32) + jnp.dot(t, mid, preferred_element_type=F32)
            + jnp.dot(t, lo, preferred_element_type=F32))


def _hg_gates(q, f, alb_ref):
    a0, a1 = alb_ref[0:1, :], alb_ref[1:2, :]
    mx = jnp.maximum(a0, a1)
    e0, e1 = jnp.exp(a0 - mx), jnp.exp(a1 - mx)
    lb = e0 / (e0 + e1)
    sig = _sigmoid(f)
    forget = lb + (1.0 - lb) * sig
    k = (1.0 - lb) * _sigmoid(-f)
    qs = q * _sigmoid(q) * (HG_DK ** -0.5)
    return qs, k, jnp.log(forget), sig, lb, forget


def _hg_intra(qs, k, b, b_scr):
    b_scr[...] = b
    bm = b_scr[pl.ds(HG_CH // 2 - 1, 1), :]
    bl = b_scr[pl.ds(HG_CH - 1, 1), :]
    eb = jnp.exp(b)
    qb = qs * eb
    e_q = jnp.exp(b - bm)
    e_k = jnp.exp(bm - b)
    e_d = jnp.exp(bl - b)
    return qb, qs * e_q, k * e_k, k * e_d, jnp.exp(bl), eb, e_q, e_k, e_d


def _hgrn_fwd(proj, alb, ngain):
    S = proj.shape[1]
    nc = S // HG_CH
    wb = HG_HPB * HG_DK

    def body(pj_ref, alb_ref, ng_ref, o_ref, y_ref, st_ref, st_scr, b_scr):
        n = pl.program_id(1)

        @pl.when(n == 0)
        def _():
            st_scr[...] = jnp.zeros_like(st_scr)

        r = lax.broadcasted_iota(jnp.int32, (HG_CH, HG_CH), 0)
        c = lax.broadcasted_iota(jnp.int32, (HG_CH, HG_CH), 1)
        causal = r >= c
        tri = causal.astype(F32)
        for j in range(HG_HPB):
            lanes = pl.ds(j * HG_DK, HG_DK)
            q, f, v, g = pj_ref[0, :, lanes], pj_ref[1, :, lanes], pj_ref[2, :, lanes], pj_ref[3, :, lanes]
            qs, k, logf, _, _, _ = _hg_gates(q, f, alb_ref.at[:, lanes])
            b = _tri_dot(tri, logf)
            qb, qt, kt, kd, ebl, _, _, _, _ = _hg_intra(qs, k, b, b_scr.at[j])
            st = st_scr[j]
            st_ref[j] = st
            a = jnp.where(causal, _hdot_nt(qt, kt), 0.0)
            o = _hdot(a, v) + _hdot_nt(qb, st)
            st_scr[j] = st * ebl + _hdot_tn(v, kd)
            o_ref[:, lanes] = o
            rinv = lax.rsqrt(jnp.mean(o * o, axis=-1, keepdims=True) + RMS_EPS)
            y_ref[:, lanes] = (o * rinv * ng_ref[...] * (g * _sigmoid(g))).astype(CDT)

    blk = pl.BlockSpec((HG_CH, wb), lambda h, n: (n, h))
    return pl.pallas_call(
        body, grid=(HG_H // HG_HPB, nc),
        in_specs=[pl.BlockSpec((4, HG_CH, wb), lambda h, n: (0, n, h)), pl.BlockSpec((2, wb), lambda h, n: (0, h)),
                  pl.BlockSpec((1, HG_DK), lambda h, n: (0, 0))],
        out_specs=[blk, blk, pl.BlockSpec((HG_HPB, None, HG_DK, HG_DK), lambda h, n: (h, n, 0, 0))],
        out_shape=[_sds((S, D), F32), _sds((S, D), CDT), _sds((HG_H, nc, HG_DK, HG_DK), F32)],
        scratch_shapes=[pltpu.VMEM((HG_HPB, HG_DK, HG_DK), F32), pltpu.VMEM((HG_HPB, HG_CH, HG_DK), F32)],
        compiler_params=_params(("arbitrary", "arbitrary")), name="hgrn_fwd")(proj, alb, ngain)


def _hgrn_bwd(proj, alb, ngain, o, states, dy):
    S = proj.shape[1]
    nc = S // HG_CH
    wb = HG_HPB * HG_DK

    def body(pj_ref, alb_ref, ng_ref, o_ref, st_ref, dy_ref, dpj_ref, dalb_ref, dng_ref, dst_scr, b_scr):
        h, n = pl.program_id(0), pl.program_id(1)

        @pl.when(n == 0)
        def _():
            dst_scr[...] = jnp.zeros_like(dst_scr)

        ng = ng_ref[...]
        r = lax.broadcasted_iota(jnp.int32, (HG_CH, HG_CH), 0)
        c = lax.broadcasted_iota(jnp.int32, (HG_CH, HG_CH), 1)
        causal = r >= c
        tri, tri_rev = causal.astype(F32), (r <= c).astype(F32)
        dng = None
        for j in range(HG_HPB):
            lanes = pl.ds(j * HG_DK, HG_DK)
            q, f, v, g = pj_ref[0, :, lanes], pj_ref[1, :, lanes], pj_ref[2, :, lanes], pj_ref[3, :, lanes]
            o_ = o_ref[:, lanes]
            dy_ = dy_ref[:, lanes]
            sg = _sigmoid(g)
            rinv = lax.rsqrt(jnp.mean(o_ * o_, axis=-1, keepdims=True) + RMS_EPS)
            nrm = o_ * rinv
            dr = dy_ * (g * sg)
            dg = dy_ * nrm * ng * (sg * (1.0 + g * (1.0 - sg)))
            dn = dr * ng
            do = rinv * (dn - nrm * jnp.mean(dn * nrm, axis=-1, keepdims=True))
            dng = _colsum(dr * nrm) if dng is None else dng + _colsum(dr * nrm)
            qs, k, logf, sig, lb, forget = _hg_gates(q, f, alb_ref.at[:, lanes])
            b = _tri_dot(tri, logf)
            qb, qt, kt, kd, ebl, eb, e_q, e_k, e_d = _hg_intra(qs, k, b, b_scr.at[j])
            st = st_ref[j]
            dstn = dst_scr[j]
            a = jnp.where(causal, _hdot_nt(qt, kt), 0.0)
            da = jnp.where(causal, _hdot_nt(do, v), 0.0)
            dv = _hdot_tn(a, do) + _hdot_nt(kd, dstn)
            dqb = _hdot(do, st)
            dkd = _hdot(v, dstn)
            dqt = _hdot(da, kt)
            dkt = _hdot_tn(da, qt)
            dbl = _colsum(dkd * kd) + ebl * _colsum(dstn * st)
            dst_scr[j] = dstn * ebl + _hdot_tn(do, qb)
            dqs = dqt * e_q + dqb * eb
            dk = dkt * e_k + dkd * e_d
            db = dqt * qt + dqb * qb - dkt * kt - dkd * kd
            dlogf = _tri_dot(tri_rev, db) + dbl
            dforget = dlogf / forget
            dsig = (1.0 - lb) * (dforget - dk)
            df = dsig * sig * (1.0 - sig)
            dlb = _colsum((dforget - dk) * (1.0 - sig))
            sq = _sigmoid(q)
            dq = dqs * (HG_DK ** -0.5) * (sq * (1.0 + q * (1.0 - sq)))
            dpj_ref[0, :, lanes] = dq.astype(CDT)
            dpj_ref[1, :, lanes] = df.astype(CDT)
            dpj_ref[2, :, lanes] = dv.astype(CDT)
            dpj_ref[3, :, lanes] = dg.astype(CDT)
            da0 = dlb * lb * (1.0 - lb)
            _acc(dalb_ref.at[pl.ds(0, 1), lanes], da0, n == 0)
            _acc(dalb_ref.at[pl.ds(1, 1), lanes], -da0, n == 0)
        _acc(dng_ref, dng, jnp.logical_and(h == 0, n == 0))

    blk = pl.BlockSpec((HG_CH, wb), lambda h, n: (nc - 1 - n, h))
    pj = pl.BlockSpec((4, HG_CH, wb), lambda h, n: (0, nc - 1 - n, h))
    alb_blk = pl.BlockSpec((2, wb), lambda h, n: (0, h))
    ng_blk = pl.BlockSpec((1, HG_DK), lambda h, n: (0, 0))
    return pl.pallas_call(
        body, grid=(HG_H // HG_HPB, nc),
        in_specs=[pj, alb_blk, ng_blk, blk,
                  pl.BlockSpec((HG_HPB, None, HG_DK, HG_DK), lambda h, n: (h, nc - 1 - n, 0, 0)), blk],
        out_specs=[pj, alb_blk, ng_blk],
        out_shape=[_sds((4, S, D), CDT), _sds((2, D), F32), _sds((1, HG_DK), F32)],
        scratch_shapes=[pltpu.VMEM((HG_HPB, HG_DK, HG_DK), F32), pltpu.VMEM((HG_HPB, HG_CH, HG_DK), F32)],
        compiler_params=_params(("arbitrary", "arbitrary")), name="hgrn_bwd")(proj, alb, ngain, o, states, dy)


def _slope(h):
    return 2.0 ** (-8.0 * (h + 1) / ATT_QH)


def _attn_mask(n):
    qi = lax.broadcasted_iota(jnp.int32, (WINDOW, 2 * WINDOW), 0)
    si = lax.broadcasted_iota(jnp.int32, (WINDOW, 2 * WINDOW), 1)
    dist = qi - si + WINDOW
    valid = (dist >= 0) & (dist < WINDOW) & (n * WINDOW - WINDOW + si >= 0)
    return valid, dist.astype(F32)


def _attn_probs(qh, kh, sink, slope, valid, distf):
    s = _dot_nt(qh, kh) * (ATT_HD ** -0.5) - slope * distf
    s = jnp.where(valid, s, NEG)
    m = jnp.maximum(jnp.max(s, axis=-1, keepdims=True), sink)
    e = jnp.exp(s - m)
    es = jnp.exp(sink - m)
    inv = 1.0 / (jnp.sum(e, axis=-1, keepdims=True) + es)
    return e * inv, es * inv


def _attn_specs(S):
    nb = S // WINDOW
    cur = lambda H: pl.BlockSpec((H, WINDOW, ATT_HD), lambda n: (0, n, 0))
    prev = lambda H: pl.BlockSpec((H, WINDOW, ATT_HD), lambda n: (0, jnp.maximum(n - 1, 0), 0))
    return nb, cur, prev


def _attn_fwd(q4, kv4, sinks):
    S = q4.shape[1]
    nb, cur, prev = _attn_specs(S)

    def body(sink_ref, q_ref, kvc_ref, kvp_ref, o_ref):
        valid, distf = _attn_mask(pl.program_id(0))
        for h in range(ATT_QH):
            kvh = h // ATT_G
            kh = jnp.concatenate([kvp_ref[kvh], kvc_ref[kvh]], axis=0)
            vh = jnp.concatenate([kvp_ref[ATT_KVH + kvh], kvc_ref[ATT_KVH + kvh]], axis=0)
            p, _ = _attn_probs(q_ref[h], kh, sink_ref[0, h], _slope(h), valid, distf)
            o_ref[h] = _dot(p, vh).astype(CDT)

    return pl.pallas_call(
        body, grid=(nb,),
        in_specs=[pl.BlockSpec(memory_space=pltpu.SMEM), cur(ATT_QH), cur(2 * ATT_KVH), prev(2 * ATT_KVH)],
        out_specs=cur(ATT_QH), out_shape=_sds((ATT_QH, S, ATT_HD), CDT),
        compiler_params=_params(("arbitrary",)), name="attn_fwd")(sinks, q4, kv4, kv4)


def _attn_bwd(q4, kv4, sinks, do4):
    S = q4.shape[1]
    nb, cur, prev = _attn_specs(S)

    def body(sink_ref, q_ref, kvc_ref, kvp_ref, do_ref, dq_ref, dkv_ref, dbq_ref, dsink_ref):
        n = pl.program_id(0)
        first = n == 0

        @pl.when(first)
        def _():
            dkv_ref[...] = jnp.zeros_like(dkv_ref)
            dsink_ref[...] = jnp.zeros_like(dsink_ref)

        valid, distf = _attn_mask(n)
        lane = lax.broadcasted_iota(jnp.int32, (1, 128), 1)
        rows_cur = pl.ds(pl.multiple_of(n * WINDOW, WINDOW), WINDOW)
        rows_prev = pl.ds(pl.multiple_of(jnp.maximum(n - 1, 0) * WINDOW, WINDOW), WINDOW)
        dsinks = jnp.zeros((1, 128), F32)
        for kvh in range(ATT_KVH):
            kh = jnp.concatenate([kvp_ref[kvh], kvc_ref[kvh]], axis=0)
            vh = jnp.concatenate([kvp_ref[ATT_KVH + kvh], kvc_ref[ATT_KVH + kvh]], axis=0)
            dk = dv = None
            dqs = []
            for h in range(kvh * ATT_G, (kvh + 1) * ATT_G):
                qh = q_ref[h]
                doh = do_ref[h]
                p, ps = _attn_probs(qh, kh, sink_ref[0, h], _slope(h), valid, distf)
                dp = _dot_nt(doh, vh)
                dd = jnp.sum(p * dp, axis=-1, keepdims=True)
                ds = p * (dp - dd)
                dsinks = dsinks + jnp.where(lane == h, -jnp.sum(ps * dd, axis=0, keepdims=True), 0.0)
                dqh = _dot(ds, kh) * (ATT_HD ** -0.5)
                dqs.append(dqh.astype(CDT))
                _acc(dbq_ref.at[h], _colsum(dqh), first)
                dkh = _dot_tn(ds, qh) * (ATT_HD ** -0.5)
                dvh = _dot_tn(p, doh)
                dk = dkh if dk is None else dk + dkh
                dv = dvh if dv is None else dv + dvh
            for i in range(ATT_G // 2):
                lanes = pl.ds((kvh * ATT_G + 2 * i) * ATT_HD, 2 * ATT_HD)
                dq_ref[:, lanes] = _pair_lanes(dqs[2 * i], dqs[2 * i + 1])
            dkv_ref[kvh, rows_prev, :] += dk[:WINDOW]
            dkv_ref[kvh, rows_cur, :] += dk[WINDOW:]
            dkv_ref[ATT_KVH + kvh, rows_prev, :] += dv[:WINDOW]
            dkv_ref[ATT_KVH + kvh, rows_cur, :] += dv[WINDOW:]
        dsink_ref[...] += dsinks

    return pl.pallas_call(
        body, grid=(nb,),
        in_specs=[pl.BlockSpec(memory_space=pltpu.SMEM), cur(ATT_QH), cur(2 * ATT_KVH), prev(2 * ATT_KVH), cur(ATT_QH)],
        out_specs=[pl.BlockSpec((WINDOW, D), lambda n: (n, 0)), pl.BlockSpec((2 * ATT_KVH, S, ATT_HD), lambda n: (0, 0, 0)),
                   pl.BlockSpec((ATT_QH, 1, ATT_HD), lambda n: (0, 0, 0)), pl.BlockSpec((1, 128), lambda n: (0, 0))],
        out_shape=[_sds((S, D), CDT), _sds((2 * ATT_KVH, S, ATT_HD), F32), _sds((ATT_QH, 1, ATT_HD), F32),
                   _sds((1, 128), F32)],
        compiler_params=_params(("arbitrary",)), name="attn_bwd")(sinks, q4, kv4, kv4, do4)


def _local_step(x, p, target, getw, sm, emit):
    S = x.shape[0]
    vec = lambda a: a.reshape(1, -1)
    ln_g = lambda l, k: vec(sm["ln_gain"][l, k])
    ln_b = lambda l, k: vec(sm["ln_bias"][l, k])
    xb = x.astype(CDT)
    pb = p.astype(CDT)

    proj = _mm_nn(xb, getw("a_w_in", None), (4, S, D), (None, _tile(S), 512), lambda g, i: (g // 2, i, g % 2), F32,
                  name="a_in")
    o_a, y_a, states = _hgrn_fwd(proj, sm["a_lower_bound"], sm["a_norm_gain"])
    zeros = jnp.zeros((1, D), F32)
    z = [[None] * 3 for _ in range(2)]
    xs = [[None] * 3 for _ in range(2)]
    xbs = [[None] * 3 for _ in range(2)]
    z[0][0], xs[0][0], xbs[0][0] = _mixout_ln(y_a[None], getw("a_w_out", y_a)[None], zeros, x, ln_g(0, 0), ln_b(0, 0),
                                              "a_out_ln")
    gu, hid, sgs, ups = [None, None], [None, None], [None, None], [None, None]

    def ffn_ple(l):
        wgu = getw(f"gu{l}", xbs[l][0])
        gu[l], hid[l], z[l][1], xs[l][1], xbs[l][1] = _ffn_fwd(
            xs[l][0], xbs[l][0], wgu, getw(f"dn{l}", None), ln_g(l, 1), ln_b(l, 1), f"ffn_fwd{l}")
        sgs[l], ups[l], z[l][2], xs[l][2], xbs[l][2] = _ple_fwd(
            xs[l][1], xbs[l][1], pb[l], getw(f"pg{l}", None), vec(sm["ple_b_gate"][l]), getw(f"pu{l}", None), ln_g(l, 2),
            ln_b(l, 2), f"ple_fwd{l}")

    ffn_ple(0)
    x3, x3b = xs[0][2], xbs[0][2]
    w_kv, w_q, w_bo = getw("kv_w", x3b), getw("b_w_q", None), getw("b_w_out", None)
    kv4 = _proj_heads(x3b, w_kv, vec(sm["kv_b"]), 2 * ATT_KVH, "kv_proj")
    q4 = _proj_heads(x3b, w_q, vec(sm["b_b_q"]), ATT_QH, "q_proj")
    o4 = _attn_fwd(q4, kv4, sm["b_sinks"])
    z[1][0], xs[1][0], xbs[1][0] = _mixout_ln(o4, w_bo, sm["b_b_out"], x3, ln_g(1, 0), ln_b(1, 0), "b_out_ln")
    ffn_ple(1)
    loss, dy = _loss_fwd_bwd(xs[1][2], target)

    gs = {}
    d_ln_g = [[None] * 3 for _ in range(2)]
    d_ln_b = [[None] * 3 for _ in range(2)]
    g_bg = [None, None]

    def ffn_ple_bwd(l, dy, after=None):
        dx2, dgl, dup, d_ln_g[l][2], d_ln_b[l][2], g_bg[l] = _ple_bwd(dy, z[l][2], sgs[l], ups[l], ln_g(l, 2),
                                                                     getw(f"pg{l}", None), f"ple_bwd{l}", after=after)
        g_pg = _wgrad(xbs[l][1][None], dgl[None], f"g_ple_gate{l}")[0]
        g_pu = _wgrad(pb[l][None], dup[None], f"g_ple_up{l}")[0]
        dx1, dzb, dgu, d_ln_g[l][1], d_ln_b[l][1] = _ffn_bwd(dx2, z[l][1], gu[l], getw(f"gu{l}", None),
                                                           getw(f"dn{l}", None), ln_g(l, 1), f"ffn_bwd{l}")
        g_dn = _wgrad(hid[l], dzb[None], f"g_ffn_down{l}")
        g_gu = _wgrad(xbs[l][0][None], dgu.reshape(8, S, FFN_B), f"g_ffn_gate_up{l}")
        return dx1, emit({f"pg{l}": g_pg, f"pu{l}": g_pu, f"dn{l}": g_dn, f"gu{l}": g_gu})

    dx1, tok = ffn_ple_bwd(1, dy)
    dz, dzb, do4, d_ln_g[1][0], d_ln_b[1][0], gs["b_b_out"] = _mixout_bwd(dx1, z[1][0], ln_g(1, 0), w_bo, CDT,
                                                                         "b_out_bwd", after=tok)
    g_bo = _wgrad(o4, dzb[None], "g_b_w_out")
    dq, dkv4, dbq, dsinks = _attn_bwd(q4, kv4, sm["b_sinks"], do4)
    gs["b_b_q"] = dbq
    gs["b_sinks"] = dsinks
    g_q = _wgrad(x3b[None], dq[None], "g_b_w_q")[0]
    dx3, dkv, gs["kv_b"] = _qkv_bwd(dz, dq, dkv4, w_q, w_kv, "qkv_bwd", after=tok)
    g_kv = _wgrad(x3b[None], dkv[None], "g_kv_w")[0]
    tok = emit({"b_w_out": g_bo, "b_w_q": g_q, "kv_w": g_kv})
    dx1, tok = ffn_ple_bwd(0, dx3, tok)
    w_ao = getw("a_w_out", None)
    dz, dzb, dyr, d_ln_g[0][0], d_ln_b[0][0], _ = _mixout_bwd(dx1, z[0][0], ln_g(0, 0), w_ao[None], F32, "a_out_bwd",
                                                              after=tok)
    g_ao = _wgrad(y_a[None], dzb[None], "g_a_w_out")[0]
    dproj, gs["a_lower_bound"], gs["a_norm_gain"] = _hgrn_bwd(proj, sm["a_lower_bound"], sm["a_norm_gain"], o_a, states,
                                                              dyr[0])
    tk = lambda t: (None, t, D)
    g_ain = _mm_tn(xb[None], dproj, N_DEV, lambda g, k: (0, k, 0), lambda g, k: (g // 2, k, g % 2),
                   tk, lambda t: (None, t, 512), (N_DEV, D, 512), (None, D, 512), lambda g, k: (g, 0, 0), name="g_a_w_in")
    tok = emit({"a_w_out": g_ao, "a_w_in": g_ain})
    grad_x = _inproj_bwd(dz, dproj, getw("a_w_in", None), "a_in_bwd", after=tok)
    gs["ple_b_gate"] = jnp.concatenate(g_bg, axis=0)
    gs["ln_gain"] = jnp.stack([jnp.concatenate(r, axis=0) for r in d_ln_g])
    gs["ln_bias"] = jnp.stack([jnp.concatenate(r, axis=0) for r in d_ln_b])
    return loss, grad_x, gs


def _peer(k):
    x, y, c = lax.axis_index("x"), lax.axis_index("y"), lax.axis_index("c")
    px = 1 - x if k & 4 else x
    py = 1 - y if k & 2 else y
    pc = 1 - c if k & 1 else c
    return (px, py, pc), 4 * px + 2 * py + pc


def _my_index():
    return 4 * lax.axis_index("x") + 2 * lax.axis_index("y") + lax.axis_index("c")


def _exchange(srcs, dst_shapes, plan, name):
    n_src, n_piece = len(srcs), len(plan)

    def body(*refs):
        src_refs, dst_refs = refs[:n_src], refs[n_src:n_src + len(dst_shapes)]
        send_sems, recv_sems, local_sems = refs[n_src + len(dst_shapes):]
        me = _my_index()

        def at(ref, idx):
            return ref.at[idx] if idx else ref

        local = []
        for t, (si, sfn, di, dfn) in enumerate(plan):
            cp = pltpu.make_async_copy(at(src_refs[si], sfn(me)), at(dst_refs[di], dfn(me)), local_sems.at[t])
            cp.start()
            local.append(cp)
        sends = []
        for k in range(1, N_DEV):
            peer, pid = _peer(k)
            for t, (si, sfn, di, dfn) in enumerate(plan):
                cp = pltpu.make_async_remote_copy(
                    src_ref=at(src_refs[si], sfn(pid)), dst_ref=at(dst_refs[di], dfn(me)),
                    send_sem=send_sems.at[t * 7 + k - 1], recv_sem=recv_sems.at[t * 7 + k - 1],
                    device_id=peer, device_id_type=MESH)
                cp.start()
                sends.append(cp)
        for k in range(1, N_DEV):
            peer, pid = _peer(k)
            for t, (si, sfn, di, dfn) in enumerate(plan):
                pltpu.make_async_remote_copy(
                    src_ref=at(src_refs[si], sfn(me)), dst_ref=at(dst_refs[di], dfn(pid)),
                    send_sem=send_sems.at[t * 7 + k - 1], recv_sem=recv_sems.at[t * 7 + k - 1],
                    device_id=peer, device_id_type=MESH).wait_recv()
        for cp in sends:
            cp.wait_send()
        for cp in local:
            cp.wait()

    hbm = pl.BlockSpec(memory_space=pltpu.HBM)
    return pl.pallas_call(
        body, in_specs=[hbm] * n_src, out_specs=[hbm] * len(dst_shapes), out_shape=dst_shapes,
        scratch_shapes=[pltpu.SemaphoreType.DMA((7 * n_piece,)), pltpu.SemaphoreType.DMA((7 * n_piece,)),
                        pltpu.SemaphoreType.DMA((n_piece,))],
        name=name)(*srcs)


def _gather(shards, name):
    dsts = [_sds((N_DEV,) + a.shape, a.dtype) for a in shards]
    plan = [(i, lambda j: (), i, lambda s: (s,)) for i in range(len(shards))]
    return _exchange(shards, dsts, plan, name)


_HBM = pl.BlockSpec(memory_space=pltpu.HBM)
_SEM = pl.BlockSpec(memory_space=pltpu.SEMAPHORE)
_DATAFLOW = pltpu.SideEffectType.DATAFLOW_SIDE_EFFECTING


def _piece_copy(mode, src, land, send_sems, recv_sems, t, k, sender, receiver, peer):
    return pltpu.make_async_remote_copy(
        src_ref=src if mode == "gather" else src.at[receiver], dst_ref=land.at[sender],
        send_sem=send_sems.at[t * 7 + k - 1], recv_sem=recv_sems.at[t * 7 + k - 1], device_id=peer, device_id_type=MESH)


def _sequencer_exchange(srcs, mode, name, collective_id, after=None):
    n = len(srcs)
    land_shapes = [((N_DEV,) + a.shape) if mode == "gather" else a.shape for a in srcs]
    extra = [] if after is None else [after]

    def body(*refs):
        src_refs, land_refs = refs[:n], refs[n + len(extra):2 * n + len(extra)]
        send_sems, recv_sems, local_sems = refs[2 * n + len(extra):]
        barrier = pltpu.get_barrier_semaphore()
        for k in range(1, N_DEV):
            pl.semaphore_signal(barrier, inc=1, device_id=_peer(k)[0], device_id_type=MESH)
        pl.semaphore_wait(barrier, N_DEV - 1)
        me = _my_index()
        local = []
        for i in range(n):
            cp = pltpu.make_async_copy(src_refs[i] if mode == "gather" else src_refs[i].at[me], land_refs[i].at[me],
                                       local_sems.at[i])
            cp.start()
            local.append(cp)
        for k in range(1, N_DEV):
            peer, pid = _peer(k)
            for t in range(n):
                _piece_copy(mode, src_refs[t], land_refs[t], send_sems, recv_sems, t, k, me, pid, peer).start()
        for k in range(1, N_DEV):
            peer, pid = _peer(k)
            for t in range(n):
                _piece_copy(mode, src_refs[t], land_refs[t], send_sems, recv_sems, t, k, pid, me, peer).wait_recv()
        for k in range(1, N_DEV):
            peer, pid = _peer(k)
            for t in range(n):
                _piece_copy(mode, src_refs[t], land_refs[t], send_sems, recv_sems, t, k, me, pid, peer).wait_send()
        for cp in local:
            cp.wait()

    return pl.kernel(
        body, out_type=[_sds(s, a.dtype) for s, a in zip(land_shapes, srcs)],
        mesh=plsc.ScalarSubcoreMesh(axis_name="sequencer", num_cores=1),
        scratch_types=[pltpu.SemaphoreType.DMA((7 * n,)), pltpu.SemaphoreType.DMA((7 * n,)), pltpu.SemaphoreType.DMA((n,))],
        compiler_params=pltpu.CompilerParams(collective_id=collective_id), name=name)(*srcs, *extra)


def _xstart(groups, mode, name, after=None):
    flat = [a for g in groups for a in g]
    n, ng = len(flat), len(groups)
    land_shapes = [((N_DEV,) + a.shape) if mode == "gather" else a.shape for a in flat]
    first = [sum(len(g) for g in groups[:i]) for i in range(ng)]
    extra = [] if after is None else [after]

    def body(*refs):
        srcs, lands = refs[:n], refs[n:2 * n]
        sems = refs[2 * n + len(extra):2 * n + len(extra) + 2 * ng]
        tok_ref, local_sems = refs[-2], refs[-1]
        me = _my_index()
        local = []
        for i in range(n):
            cp = pltpu.make_async_copy(srcs[i] if mode == "gather" else srcs[i].at[me], lands[i].at[me], local_sems.at[i])
            cp.start()
            local.append(cp)
        for cp in local:
            cp.wait()
        for gi, g in enumerate(groups):
            for k in range(1, N_DEV):
                peer, pid = _peer(k)
                for t in range(len(g)):
                    i = first[gi] + t
                    _piece_copy(mode, srcs[i], lands[i], sems[2 * gi], sems[2 * gi + 1], t, k, me, pid, peer).start()
        tok_ref[...] = jnp.zeros_like(tok_ref)

    sem_shapes = []
    for g in groups:
        sem_shapes += [pltpu.SemaphoreType.DMA((7 * len(g),))] * 2
    thru = [pltpu.HBM(a.shape, a.dtype) for a in flat] + [pltpu.HBM(s, a.dtype) for s, a in zip(land_shapes, flat)]
    outs = pl.pallas_call(
        body, in_specs=[_HBM] * (2 * n) + [pl.BlockSpec(memory_space=pl.ANY)] * len(extra),
        out_specs=[_SEM] * (2 * ng) + [_HBM] * (2 * n) + [pl.BlockSpec(memory_space=pltpu.VMEM)],
        out_shape=sem_shapes + thru + [_sds((8, 128), F32)],
        input_output_aliases={i: 2 * ng + i for i in range(2 * n)},
        scratch_shapes=[pltpu.SemaphoreType.DMA((n,))],
        compiler_params=pltpu.CompilerParams(has_side_effects=_DATAFLOW), name=name)(
            *[pltpu.with_memory_space_constraint(a, pltpu.HBM) for a in flat],
            *[pltpu.with_memory_space_constraint(lax.empty(s, a.dtype), pltpu.HBM) for s, a in zip(land_shapes, flat)], *extra)
    sems, srcs_thru, lands_thru = outs[:2 * ng], outs[2 * ng:2 * ng + n], outs[2 * ng + n:2 * ng + 2 * n]
    handles = [(sems[2 * gi], sems[2 * gi + 1], srcs_thru[first[gi]:first[gi] + len(g)],
                lands_thru[first[gi]:first[gi] + len(g)]) for gi, g in enumerate(groups)]
    return handles, outs[-1]


def _xwait(handle, mode, after, name):
    send_sems, recv_sems, srcs_thru, lands_thru = handle
    n = len(srcs_thru)

    def body(*refs):
        srcs, lands, send, recv = refs[:n], refs[n:2 * n], refs[2 * n], refs[2 * n + 1]
        me = _my_index()
        for k in range(1, N_DEV):
            peer, pid = _peer(k)
            for t in range(n):
                _piece_copy(mode, srcs[t], lands[t], send, recv, t, k, pid, me, peer).wait_recv()
        for k in range(1, N_DEV):
            peer, pid = _peer(k)
            for t in range(n):
                _piece_copy(mode, srcs[t], lands[t], send, recv, t, k, me, pid, peer).wait_send()

    extra = [] if after is None else [after]
    outs = pl.pallas_call(
        body, in_specs=[_HBM] * (2 * n) + [_SEM, _SEM] + [pl.BlockSpec(memory_space=pl.ANY)] * len(extra),
        out_specs=[_HBM] * (2 * n),
        out_shape=[pltpu.HBM(a.shape, a.dtype) for a in list(srcs_thru) + list(lands_thru)],
        input_output_aliases={i: i for i in range(2 * n)},
        compiler_params=pltpu.CompilerParams(has_side_effects=_DATAFLOW), name=name)(
            *srcs_thru, *lands_thru, send_sems, recv_sems, *extra)
    return outs[n:]


def _adamw(w, g, m, v):
    m = ADAM_B1 * m + (1.0 - ADAM_B1) * g
    v = ADAM_B2 * v + (1.0 - ADAM_B2) * (g * g)
    m_hat = m / (1.0 - ADAM_B1 ** ADAM_STEP)
    v_hat = v / (1.0 - ADAM_B2 ** ADAM_STEP)
    delta = -ADAM_LR * (m_hat / (jnp.sqrt(v_hat) + ADAM_EPS) + ADAM_WD * w)
    return delta, m, v


def _adam_big(w, parts, m, v, name, after=None):
    L, R, C = w.shape
    tr = _tile(R, (256, 128, 176, 64, 32, 16))
    nr = R // tr

    def body(w_ref, *refs):
        p_refs, (m_ref, v_ref, g_ref, d_ref, mo_ref, vo_ref) = refs[:L], refs[L:]
        for l in range(L):
            @pl.when(pl.program_id(0) == l)
            def _(p_ref=p_refs[l]):
                g = p_ref[0].astype(F32)
                for s in range(1, N_DEV):
                    g = g + p_ref[s].astype(F32)
                g_ref[...] = g
                d_ref[...], mo_ref[...], vo_ref[...] = _adamw(w_ref[...], g, m_ref[...], v_ref[...])

    row = pl.BlockSpec((None, tr, C), lambda l, i: (l, i, 0))
    park = lambda l_of: (lambda l, i: (0, jnp.where(l == l_of, i, 0 if l_of else nr - 1), 0))
    return _call(
        after, body, grid=(L, nr),
        in_specs=[row] + [pl.BlockSpec((N_DEV, tr, C), park(l)) for l in range(L)] + [row, row],
        out_specs=[row] * 4, out_shape=[_sds((L, R, C), F32)] * 4,
        compiler_params=_params(("arbitrary", "arbitrary")), name=name)(w, *parts, m, v)


SMALL = (("a_lower_bound", 2, True), ("ln_gain", 6, True), ("ln_bias", 6, True), ("a_norm_gain", 1, False),
         ("kv_b", 4, False), ("b_b_q", 8, False), ("b_sinks", 1, False), ("b_b_out", 8, False), ("ple_b_gate", 16, False))
SUBLANES = 8


def _slot(r):
    return -(-r // SUBLANES) * SUBLANES


SMALL_ROWS = sum(_slot(r) for _, r, _ in SMALL)
PART_ROWS = sum(r * N_DEV if sh else _slot(r) for _, r, sh in SMALL)


def _pack_rows(a, rows):
    flat = a.reshape(-1)
    return jnp.pad(flat, (0, rows * 128 - flat.size)).reshape(rows, 128)


def _pack_small(d):
    return jnp.concatenate([_pack_rows(d[n], _slot(r)) for n, r, _ in SMALL], axis=0)


def _unpack_small(packed, like):
    out, r0 = {}, 0
    for n, r, _ in SMALL:
        out[n] = packed[r0:r0 + r].reshape(-1)[:like[n].size].reshape(like[n].shape)
        r0 += _slot(r)
    return out


def _pack_partials(gs):
    blocks = []
    for n, r, sharded in SMALL:
        if sharded:
            blocks.append(gs[n].reshape(r * N_DEV, 128))
        else:
            blocks.append(_pack_rows(gs[n], _slot(r)))
    return jnp.concatenate(blocks, axis=0)


def _adam_small(parts, w, m, v, after=None):
    def body(p_ref, w_ref, m_ref, v_ref, g_ref, d_ref, mo_ref, vo_ref, tot):
        me = _my_index()
        t = p_ref[0]
        for s in range(1, N_DEV):
            t = t + p_ref[s]
        tot[...] = t
        g_ref[...] = jnp.zeros_like(g_ref)
        src, dst = 0, 0
        for _, r, sharded in SMALL:
            if sharded:
                for i in range(r):
                    g_ref[pl.ds(dst + i, 1), :] = tot[pl.ds(src + i * N_DEV + me, 1), :]
                src += r * N_DEV
            else:
                g_ref[pl.ds(dst, _slot(r)), :] = tot[pl.ds(src, _slot(r)), :]
                src += _slot(r)
            dst += _slot(r)
        d_ref[...], mo_ref[...], vo_ref[...] = _adamw(w_ref[...], g_ref[...], m_ref[...], v_ref[...])

    full = pl.BlockSpec((SMALL_ROWS, 128), lambda: (0, 0))
    return _call(
        after, body, in_specs=[pl.BlockSpec((N_DEV, PART_ROWS, 128), lambda: (0, 0, 0)), full, full, full],
        out_specs=[full] * 4, out_shape=[_sds((SMALL_ROWS, 128), F32)] * 4,
        scratch_shapes=[pltpu.VMEM((PART_ROWS, 128), F32)], name="adam_small")(parts, w, m, v)


WEIGHTS = ("a_w_in", "a_lower_bound", "a_norm_gain", "a_w_out", "kv_w", "kv_b", "b_w_q", "b_b_q", "b_sinks", "b_w_out",
           "b_b_out", "ffn_w_gate_up", "ffn_w_down", "ple_w_up", "ple_w_gate", "ple_b_gate", "ln_gain", "ln_bias")


GATHER_GROUPS = (("a_w_in",), ("a_w_out", "gu0", "dn0", "pu0", "pg0"), ("kv_w", "b_w_q", "b_w_out"),
                 ("gu1", "dn1", "pu1", "pg1"))
KERNEL_LAYOUT = {
    "a_w_in": lambda a: a,
    "a_w_out": lambda a: a.reshape(D, D),
    "kv_w": lambda a: a.reshape(D, 2 * ATT_KVH * ATT_HD),
    "b_w_q": lambda a: a.reshape(D, D),
    "b_w_out": lambda a: a.reshape(ATT_QH, ATT_HD, D),
    "gu": lambda a: a.reshape(2, 4, D, FFN_B),
    "dn": lambda a: a.reshape(4, FFN_B, D),
    "pu": lambda a: a,
    "pg": lambda a: a.reshape(D, D),
}
_row_blocks = lambda a: a.reshape(N_DEV, -1, a.shape[-1])
OWNER_BLOCKS = {
    "a_w_in": lambda g: g,
    "a_w_out": _row_blocks,
    "kv_w": _row_blocks,
    "b_w_q": _row_blocks,
    "b_w_out": lambda g: _row_blocks(g.reshape(D, D)),
    "gu": lambda g: g,
    "dn": lambda g: _row_blocks(g.reshape(FFN_H, D)),
    "pu": lambda g: g.reshape(PLE_DIM, N_DEV, 128).transpose(1, 0, 2),
    "pg": _row_blocks,
}
ADAM_AFTER = {1: (("kv_w", "kv_w"), ("b_w_q", "b_w_q"), ("b_w_out", "b_w_out")),
              2: (("ffn_w_gate_up", "gu"), ("ffn_w_down", "dn"), ("ple_w_up", "pu"), ("ple_w_gate", "pg")),
              3: (("a_w_out", "a_w_out"), ("a_w_in", "a_w_in"))}


def kernel(x, p, a_w_in, a_lower_bound, a_norm_gain, a_w_out, kv_w, kv_b, b_w_q, b_b_q, b_sinks, b_w_out, b_b_out, ffn_w_gate_up, ffn_w_down, ple_w_up, ple_w_gate, ple_b_gate, ln_gain, ln_bias, loss_target, m_a_w_in, m_a_lower_bound, m_a_norm_gain, m_a_w_out, m_kv_w, m_kv_b, m_b_w_q, m_b_b_q, m_b_sinks, m_b_w_out, m_b_b_out, m_ffn_w_gate_up, m_ffn_w_down, m_ple_w_up, m_ple_w_gate, m_ple_b_gate, m_ln_gain, m_ln_bias, v_a_w_in, v_a_lower_bound, v_a_norm_gain, v_a_w_out, v_kv_w, v_kv_b, v_b_w_q, v_b_b_q, v_b_sinks, v_b_w_out, v_b_b_out, v_ffn_w_gate_up, v_ffn_w_down, v_ple_w_up, v_ple_w_gate, v_ple_b_gate, v_ln_gain, v_ln_bias):
    given = dict(locals())
    w = {n: given[n] for n in WEIGHTS}
    m = {n: given["m_" + n] for n in WEIGHTS}
    v = {n: given["v_" + n] for n in WEIGHTS}
    small_sharded = jnp.concatenate([_pack_rows(a, SUBLANES) for a in (a_lower_bound, ln_gain, ln_bias)], axis=0)
    (g_small,) = _gather([small_sharded], "gather_small")
    shards = {"a_w_in": a_w_in[0], "a_w_out": a_w_out[0], "kv_w": kv_w, "b_w_q": b_w_q[0], "b_w_out": b_w_out[0]}
    for l in range(2):
        shards.update({f"gu{l}": ffn_w_gate_up[l], f"dn{l}": ffn_w_down[l], f"pu{l}": ple_w_up[l], f"pg{l}": ple_w_gate[l]})
    gathered = {}
    for gi, g in enumerate(GATHER_GROUPS):
        lands = _sequencer_exchange([shards[n].astype(CDT) for n in g], "gather", f"gather{gi}", gi,
                                    after=None if gi else g_small)
        for n, a in zip(g, lands):
            gathered[n] = KERNEL_LAYOUT[n.rstrip("01")](a)

    def getw(key, after):
        return gathered[key]

    full_rows = lambda r0, r: g_small[:, r0:r0 + r].transpose(1, 0, 2).reshape(r, D)
    sm = {"a_lower_bound": full_rows(0, 2), "ln_gain": full_rows(SUBLANES, 6).reshape(2, 3, D),
          "ln_bias": full_rows(2 * SUBLANES, 6).reshape(2, 3, D), "a_norm_gain": a_norm_gain, "kv_b": kv_b,
          "b_b_q": b_b_q[0], "b_sinks": b_sinks, "b_b_out": b_b_out, "ple_b_gate": ple_b_gate}

    scatters = []

    def emit(grads):
        names = list(grads)
        blocks = [OWNER_BLOCKS[n.rstrip("01")](grads[n]) for n in names]
        lands = _sequencer_exchange(blocks, "scatter", f"scatter{len(scatters)}", len(GATHER_GROUPS) + len(scatters))
        scatters.append(dict(zip(names, lands)))
        return blocks

    loss, grad_x, gs = _local_step(x[0], p[:, 0], loss_target[0], getw, sm, emit)

    (parts_small,) = _gather([_pack_partials(gs)], "gather_small_grads")
    packed = _adam_small(parts_small, _pack_small(w), _pack_small(m), _pack_small(v),
                         after=[grad_x] + list(scatters[0].values()))
    small_out = [_unpack_small(a, w) for a in packed]
    out = {n: [s[n] for s in small_out] for n, _, _ in SMALL}

    parts, last = {}, grad_x
    for i, landed in enumerate(scatters):
        parts.update(landed)
        for n, key in ADAM_AFTER.get(i, ()):
            lrc = (1,) * (3 - w[n].ndim) + w[n].shape
            layers = [parts[key]] if key in parts else [parts[key + "0"], parts[key + "1"]]
            res = _adam_big(w[n].reshape(lrc), layers, m[n].reshape(lrc), v[n].reshape(lrc), "adam_" + n, after=[last])
            out[n] = [r.reshape(w[n].shape) for r in res]
            last = res[3]

    loss = lax.psum(loss[0, 0], ("x", "y", "c"))
    res = [loss, grad_x[None]]
    for i in range(4):
        res += [out[n][i] for n in WEIGHTS]
    return tuple(res)
```

```python
import jax
import jax.numpy as jnp
from jax import lax
from jax.experimental import pallas as pl
from jax.experimental.pallas import tpu as pltpu
from jax.experimental.pallas import tpu_sc as plsc

F32 = jnp.float32
CDT = jnp.bfloat16

N_DEV = 8
D = 1024
HG_H, HG_DK, HG_CH = 8, 128, 64
HG_HPB = 4
ATT_HD, ATT_QH, ATT_KVH, ATT_G, WINDOW = 64, 16, 4, 4, 128
FFN_H = 2816
FFN_B = FFN_H // 4
PLE_DIM = 256
ALPHA = (2.0 * 2) ** 0.25
LN_EPS = 1e-5
RMS_EPS = 1e-6
ADAM_LR, ADAM_B1, ADAM_B2, ADAM_EPS, ADAM_WD, ADAM_STEP = 0.001, 0.9, 0.999, 1e-08, 0.01, 10
ROW_TILES = (512, 256, 128, 64)
VMEM_LIMIT = 48 * 1024 * 1024
NEG = -1e30

MESH = pl.DeviceIdType.MESH


def _tile(n, cands=ROW_TILES):
    for t in cands:
        if n % t == 0:
            return t
    return n


def _sds(shape, dtype):
    return jax.ShapeDtypeStruct(tuple(shape), dtype)


def _params(sem):
    return pltpu.CompilerParams(dimension_semantics=sem, vmem_limit_bytes=VMEM_LIMIT)


def _dot(a, b):
    return jnp.dot(a.astype(CDT), b.astype(CDT), preferred_element_type=F32)


def _dot_nt(a, b):
    return lax.dot_general(a.astype(CDT), b.astype(CDT), (((1,), (1,)), ((), ())), preferred_element_type=F32)


def _dot_tn(a, b):
    return lax.dot_general(a.astype(CDT), b.astype(CDT), (((0,), (0,)), ((), ())), preferred_element_type=F32)


def _sigmoid(x):
    return jax.nn.sigmoid(x)


def _ln_fwd(z, g, b):
    mu = jnp.mean(z, axis=-1, keepdims=True)
    zc = z - mu
    var = jnp.mean(zc * zc, axis=-1, keepdims=True)
    return zc * lax.rsqrt(var + LN_EPS) * g + b


def _ln_bwd(z, g, dy):
    mu = jnp.mean(z, axis=-1, keepdims=True)
    zc = z - mu
    var = jnp.mean(zc * zc, axis=-1, keepdims=True)
    rstd = lax.rsqrt(var + LN_EPS)
    xhat = zc * rstd
    dxh = dy * g
    dz = rstd * (dxh - jnp.mean(dxh, axis=-1, keepdims=True) - xhat * jnp.mean(dxh * xhat, axis=-1, keepdims=True))
    return dz, xhat


def _colsum(x):
    return jnp.sum(x, axis=0, keepdims=True)


def _acc(ref, val, first):
    @pl.when(first)
    def _():
        ref[...] = val

    @pl.when(jnp.logical_not(first))
    def _():
        ref[...] += val


def _call(after, body, **kw):
    after = [] if after is None else list(after)
    if not after:
        return pl.pallas_call(body, **kw)
    kw["in_specs"] = [pl.BlockSpec(memory_space=pl.ANY)] * len(after) + list(kw["in_specs"])

    def ordered_body(*refs):
        body(*refs[len(after):])

    call = pl.pallas_call(ordered_body, **kw)
    return lambda *args: call(*after, *args)


def _mm_nn(a, b3, out_shape, oblock, omap, out_dtype, bias3=None, name="mm_nn"):
    M, K = a.shape
    G, _, Nb = b3.shape
    tm = _tile(M)

    def body(a_ref, b_ref, *rest):
        o_ref = rest[-1]
        acc = _dot(a_ref[...], b_ref[...])
        if bias3 is not None:
            acc = acc + rest[0][...]
        o_ref[...] = acc.astype(o_ref.dtype)

    in_specs = [pl.BlockSpec((tm, K), lambda g, i: (i, 0)), pl.BlockSpec((None, K, Nb), lambda g, i: (g, 0, 0))]
    args = [a, b3]
    if bias3 is not None:
        in_specs.append(pl.BlockSpec((None, 1, Nb), lambda g, i: (g, 0, 0)))
        args.append(bias3)
    return pl.pallas_call(
        body, grid=(G, M // tm), in_specs=in_specs, out_specs=pl.BlockSpec(oblock, omap),
        out_shape=_sds(out_shape, out_dtype), compiler_params=_params(("arbitrary", "arbitrary")), name=name)(*args)


def _mm_tn(a3, b3, G, amap, bmap, ablock, bblock, out_shape, oblock, omap, name="mm_tn"):
    S = a3.shape[1]
    tk = _tile(S, (512, 256, 128))
    Mo, No = oblock[-2], oblock[-1]

    def body(a_ref, b_ref, o_ref, acc):
        k = pl.program_id(1)
        _acc(acc, _dot_tn(a_ref[...], b_ref[...]), k == 0)

        @pl.when(k == pl.num_programs(1) - 1)
        def _():
            o_ref[...] = acc[...].astype(o_ref.dtype)

    return pl.pallas_call(
        body, grid=(G, S // tk),
        in_specs=[pl.BlockSpec(ablock(tk), amap), pl.BlockSpec(bblock(tk), bmap)],
        out_specs=pl.BlockSpec(oblock, omap), out_shape=_sds(out_shape, CDT),
        scratch_shapes=[pltpu.VMEM((Mo, No), F32)],
        compiler_params=_params(("arbitrary", "arbitrary")), name=name)(a3, b3)


def _wgrad(a3, b3, name):
    Ga, S, M = a3.shape
    Gb, _, N = b3.shape
    G = max(Ga, Gb)
    return _mm_tn(
        a3, b3, G,
        (lambda g, k: (g, k, 0)) if Ga > 1 else (lambda g, k: (0, k, 0)),
        (lambda g, k: (g, k, 0)) if Gb > 1 else (lambda g, k: (0, k, 0)),
        lambda tk: (None, tk, M), lambda tk: (None, tk, N),
        (G, M, N), (None, M, N), lambda g, k: (g, 0, 0), name=name)


def _mixout_ln(u3, w3, bias, xin, gain, beta, name):
    G, S, Kb = u3.shape
    tm = _tile(S)

    def body(u_ref, w_ref, b_ref, x_ref, g_ref, be_ref, z_ref, xo_ref, xob_ref):
        h = b_ref[...] + _dot(u_ref[0], w_ref[0])
        for g in range(1, G):
            h = h + _dot(u_ref[g], w_ref[g])
        z = ALPHA * x_ref[...] + h
        z_ref[...] = z
        y = _ln_fwd(z, g_ref[...], be_ref[...])
        xo_ref[...] = y
        xob_ref[...] = y.astype(CDT)

    row = pl.BlockSpec((tm, D), lambda i: (i, 0))
    vec = pl.BlockSpec((1, D), lambda i: (0, 0))
    return pl.pallas_call(
        body, grid=(S // tm,),
        in_specs=[pl.BlockSpec((G, tm, Kb), lambda i: (0, i, 0)), pl.BlockSpec((G, Kb, D), lambda i: (0, 0, 0)),
                  vec, row, vec, vec],
        out_specs=[row, row, row], out_shape=[_sds((S, D), F32), _sds((S, D), F32), _sds((S, D), CDT)],
        compiler_params=_params(("arbitrary",)), name=name)(u3, w3, bias, xin, gain, beta)


def _ffn_fwd(xin, xin_b, wgu, wdn, gain, beta, name):
    S = xin.shape[0]
    tm = _tile(S)

    def body(x_ref, xb_ref, wgu_ref, wdn_ref, g_ref, be_ref, gu_ref, hid_ref, z_ref, xo_ref, xob_ref, acc):
        j = pl.program_id(1)
        xb = xb_ref[...]
        gate = _dot(xb, wgu_ref[0])
        up = _dot(xb, wgu_ref[1])
        gu_ref[0] = gate
        gu_ref[1] = up
        hid = (gate * _sigmoid(gate) * up).astype(CDT)
        hid_ref[...] = hid
        _acc(acc, _dot(hid, wdn_ref[...]), j == 0)

        @pl.when(j == 3)
        def _():
            z = ALPHA * x_ref[...] + acc[...]
            z_ref[...] = z
            y = _ln_fwd(z, g_ref[...], be_ref[...])
            xo_ref[...] = y
            xob_ref[...] = y.astype(CDT)

    row = pl.BlockSpec((tm, D), lambda i, j: (i, 0))
    vec = pl.BlockSpec((1, D), lambda i, j: (0, 0))
    return pl.pallas_call(
        body, grid=(S // tm, 4),
        in_specs=[row, row, pl.BlockSpec((2, None, D, FFN_B), lambda i, j: (0, j, 0, 0)),
                  pl.BlockSpec((None, FFN_B, D), lambda i, j: (j, 0, 0)), vec, vec],
        out_specs=[pl.BlockSpec((2, None, tm, FFN_B), lambda i, j: (0, j, i, 0)),
                   pl.BlockSpec((None, tm, FFN_B), lambda i, j: (j, i, 0)), row, row, row],
        out_shape=[_sds((2, 4, S, FFN_B), F32), _sds((4, S, FFN_B), CDT), _sds((S, D), F32), _sds((S, D), F32),
                   _sds((S, D), CDT)],
        scratch_shapes=[pltpu.VMEM((tm, D), F32)],
        compiler_params=_params(("arbitrary", "arbitrary")), name=name)(xin, xin_b, wgu, wdn, gain, beta)


def _ple_fwd(xin, xin_b, p_b, wpg, bgate, wpu, gain, beta, name):
    S = xin.shape[0]
    tm = _tile(S)

    def body(x_ref, xb_ref, p_ref, wpg_ref, bg_ref, wpu_ref, g_ref, be_ref, sg_ref, up_ref, z_ref, xo_ref, xob_ref):
        sg = _sigmoid(_dot(xb_ref[...], wpg_ref[...]) + bg_ref[...])
        pb = p_ref[...]
        up = jnp.concatenate([_dot(pb, wpu_ref[j]) for j in range(N_DEV)], axis=-1)
        sg_ref[...] = sg
        up_ref[...] = up
        z = ALPHA * x_ref[...] + sg * up
        z_ref[...] = z
        y = _ln_fwd(z, g_ref[...], be_ref[...])
        xo_ref[...] = y
        xob_ref[...] = y.astype(CDT)

    row = pl.BlockSpec((tm, D), lambda i: (i, 0))
    vec = pl.BlockSpec((1, D), lambda i: (0, 0))
    return pl.pallas_call(
        body, grid=(S // tm,),
        in_specs=[row, row, pl.BlockSpec((tm, PLE_DIM), lambda i: (i, 0)), pl.BlockSpec((D, D), lambda i: (0, 0)), vec,
                  pl.BlockSpec((N_DEV, PLE_DIM, D // N_DEV), lambda i: (0, 0, 0)), vec, vec],
        out_specs=[row] * 5,
        out_shape=[_sds((S, D), F32)] * 4 + [_sds((S, D), CDT)],
        compiler_params=_params(("arbitrary",)), name=name)(xin, xin_b, p_b, wpg, bgate, wpu, gain, beta)


def _loss_fwd_bwd(y, target):
    S = y.shape[0]
    tm = _tile(S)

    def body(y_ref, t_ref, l_ref, dy_ref):
        e = y_ref[...] - t_ref[...]
        dy_ref[...] = e * (1.0 / D)
        part = 0.5 * jnp.sum(jnp.sum(e * e, axis=-1, keepdims=True) * (1.0 / D), axis=0, keepdims=True)
        _acc(l_ref, part, pl.program_id(0) == 0)

    row = pl.BlockSpec((tm, D), lambda i: (i, 0))
    return pl.pallas_call(
        body, grid=(S // tm,), in_specs=[row, row],
        out_specs=[pl.BlockSpec((1, 1), lambda i: (0, 0)), row],
        out_shape=[_sds((1, 1), F32), _sds((S, D), F32)],
        compiler_params=_params(("arbitrary",)), name="loss")(y, target)


def _ple_bwd(dy, z, sg, up, gain, wpg, name, after=None):
    S = dy.shape[0]
    tm = _tile(S)

    def body(dy_ref, z_ref, sg_ref, up_ref, g_ref, wpg_ref, dx_ref, dgl_ref, dup_ref, dgain_ref, dbeta_ref, dbg_ref):
        first = pl.program_id(0) == 0
        dy_ = dy_ref[...]
        dz, xhat = _ln_bwd(z_ref[...], g_ref[...], dy_)
        sg_ = sg_ref[...]
        dgl = dz * up_ref[...] * sg_ * (1.0 - sg_)
        dgl_ref[...] = dgl.astype(CDT)
        dup_ref[...] = (dz * sg_).astype(CDT)
        dx_ref[...] = ALPHA * dz + _dot_nt(dgl, wpg_ref[...])
        _acc(dgain_ref, _colsum(dy_ * xhat), first)
        _acc(dbeta_ref, _colsum(dy_), first)
        _acc(dbg_ref, _colsum(dgl), first)

    row = pl.BlockSpec((tm, D), lambda i: (i, 0))
    vec = pl.BlockSpec((1, D), lambda i: (0, 0))
    return _call(
        after, body, grid=(S // tm,), in_specs=[row, row, row, row, vec, pl.BlockSpec((D, D), lambda i: (0, 0))],
        out_specs=[row, row, row, vec, vec, vec],
        out_shape=[_sds((S, D), F32), _sds((S, D), CDT), _sds((S, D), CDT)] + [_sds((1, D), F32)] * 3,
        compiler_params=_params(("arbitrary",)), name=name)(dy, z, sg, up, gain, wpg)


def _ffn_bwd(dy, z, gu, wgu, wdn, gain, name, after=None):
    S = dy.shape[0]
    tm = _tile(S)

    def body(dy_ref, z_ref, gu_ref, wgu_ref, wdn_ref, g_ref, dx_ref, dzb_ref, dgu_ref, dgain_ref, dbeta_ref,
             dz_scr, acc):
        i, j = pl.program_id(0), pl.program_id(1)

        @pl.when(j == 0)
        def _():
            dy_ = dy_ref[...]
            dz, xhat = _ln_bwd(z_ref[...], g_ref[...], dy_)
            dz_scr[...] = dz
            dzb_ref[...] = dz.astype(CDT)
            _acc(dgain_ref, _colsum(dy_ * xhat), i == 0)
            _acc(dbeta_ref, _colsum(dy_), i == 0)

        dhid = _dot_nt(dz_scr[...], wdn_ref[...])
        gate, up = gu_ref[0], gu_ref[1]
        sg = _sigmoid(gate)
        dgate = (dhid * up * (sg * (1.0 + gate * (1.0 - sg)))).astype(CDT)
        dup = (dhid * (gate * sg)).astype(CDT)
        dgu_ref[0] = dgate
        dgu_ref[1] = dup
        _acc(acc, _dot_nt(dgate, wgu_ref[0]) + _dot_nt(dup, wgu_ref[1]), j == 0)

        @pl.when(j == 3)
        def _():
            dx_ref[...] = ALPHA * dz_scr[...] + acc[...]

    row = pl.BlockSpec((tm, D), lambda i, j: (i, 0))
    vec = pl.BlockSpec((1, D), lambda i, j: (0, 0))
    return _call(
        after, body, grid=(S // tm, 4),
        in_specs=[row, row, pl.BlockSpec((2, None, tm, FFN_B), lambda i, j: (0, j, i, 0)),
                  pl.BlockSpec((2, None, D, FFN_B), lambda i, j: (0, j, 0, 0)),
                  pl.BlockSpec((None, FFN_B, D), lambda i, j: (j, 0, 0)), vec],
        out_specs=[row, row, pl.BlockSpec((2, None, tm, FFN_B), lambda i, j: (0, j, i, 0)), vec, vec],
        out_shape=[_sds((S, D), F32), _sds((S, D), CDT), _sds((2, 4, S, FFN_B), CDT), _sds((1, D), F32),
                   _sds((1, D), F32)],
        scratch_shapes=[pltpu.VMEM((tm, D), F32), pltpu.VMEM((tm, D), F32)],
        compiler_params=_params(("arbitrary", "arbitrary")), name=name)(dy, z, gu, wgu, wdn, gain)


def _mixout_bwd(dy, z, gain, w3, du_dtype, name, after=None):
    S = dy.shape[0]
    G, Kb, _ = w3.shape
    tm = _tile(S)

    def body(dy_ref, z_ref, g_ref, w_ref, dz_ref, dzb_ref, du_ref, dgain_ref, dbeta_ref, dbias_ref):
        first = pl.program_id(0) == 0
        dy_ = dy_ref[...]
        dz, xhat = _ln_bwd(z_ref[...], g_ref[...], dy_)
        dz_ref[...] = dz
        dzb = dz.astype(CDT)
        dzb_ref[...] = dzb
        for g in range(G):
            du_ref[g] = _dot_nt(dzb, w_ref[g]).astype(du_ref.dtype)
        _acc(dgain_ref, _colsum(dy_ * xhat), first)
        _acc(dbeta_ref, _colsum(dy_), first)
        _acc(dbias_ref, _colsum(dz), first)

    row = pl.BlockSpec((tm, D), lambda i: (i, 0))
    vec = pl.BlockSpec((1, D), lambda i: (0, 0))
    return _call(
        after, body, grid=(S // tm,), in_specs=[row, row, vec, pl.BlockSpec((G, Kb, D), lambda i: (0, 0, 0))],
        out_specs=[row, row, pl.BlockSpec((G, tm, Kb), lambda i: (0, i, 0)), vec, vec, vec],
        out_shape=[_sds((S, D), F32), _sds((S, D), CDT), _sds((G, S, Kb), du_dtype)] + [_sds((1, D), F32)] * 3,
        compiler_params=_params(("arbitrary",)), name=name)(dy, z, gain, w3)


def _half_select(low):
    r = lax.broadcasted_iota(jnp.int32, (2 * ATT_HD, ATT_HD), 0)
    c = lax.broadcasted_iota(jnp.int32, (2 * ATT_HD, ATT_HD), 1)
    return (r == c + (0 if low else ATT_HD)).astype(CDT)


def _half_place(low):
    r = lax.broadcasted_iota(jnp.int32, (ATT_HD, 2 * ATT_HD), 0)
    c = lax.broadcasted_iota(jnp.int32, (ATT_HD, 2 * ATT_HD), 1)
    return (c == r + (0 if low else ATT_HD)).astype(CDT)


def _pair_lanes(even, odd):
    return (jnp.dot(even, _half_place(True), preferred_element_type=F32)
            + jnp.dot(odd, _half_place(False), preferred_element_type=F32)).astype(CDT)


def _proj_heads(a, w, bias, heads, name):
    S, K = a.shape
    N = heads * ATT_HD
    tm = _tile(S)

    def body(a_ref, w_ref, b_ref, o_ref):
        acc = (_dot(a_ref[...], w_ref[...]) + b_ref[...]).astype(CDT)
        sel = (_half_select(True), _half_select(False))
        for h in range(heads):
            pair = acc[:, (h // 2) * 2 * ATT_HD:(h // 2 + 1) * 2 * ATT_HD]
            o_ref[h] = jnp.dot(pair, sel[h % 2], preferred_element_type=F32).astype(CDT)

    return pl.pallas_call(
        body, grid=(S // tm,),
        in_specs=[pl.BlockSpec((tm, K), lambda i: (i, 0)), pl.BlockSpec((K, N), lambda i: (0, 0)),
                  pl.BlockSpec((1, N), lambda i: (0, 0))],
        out_specs=pl.BlockSpec((heads, tm, ATT_HD), lambda i: (0, i, 0)), out_shape=_sds((heads, S, ATT_HD), CDT),
        compiler_params=_params(("arbitrary",)), name=name)(a, w, bias)


def _qkv_bwd(dz, dq, dkv4, wq, wkv, name, after=None):
    S = dz.shape[0]
    tm = _tile(S)
    HK = dkv4.shape[0]
    NK = HK * ATT_HD

    def body(dz_ref, dq_ref, dkv_ref, wq_ref, wkv_ref, dx_ref, dkvn_ref, dkvb_ref):
        first = pl.program_id(0) == 0
        dkvn = jnp.concatenate([_pair_lanes(dkv_ref[2 * i].astype(CDT), dkv_ref[2 * i + 1].astype(CDT))
                                for i in range(HK // 2)], axis=-1)
        dkvn_ref[...] = dkvn
        dx_ref[...] = ALPHA * dz_ref[...] + _dot_nt(dq_ref[...], wq_ref[...]) + _dot_nt(dkvn, wkv_ref[...])
        for h in range(HK):
            _acc(dkvb_ref.at[h], _colsum(dkv_ref[h]), first)

    row = pl.BlockSpec((tm, D), lambda i: (i, 0))
    return _call(
        after, body, grid=(S // tm,),
        in_specs=[row, row, pl.BlockSpec((HK, tm, ATT_HD), lambda i: (0, i, 0)),
                  pl.BlockSpec((D, D), lambda i: (0, 0)), pl.BlockSpec((D, NK), lambda i: (0, 0))],
        out_specs=[row, pl.BlockSpec((tm, NK), lambda i: (i, 0)), pl.BlockSpec((HK, 1, ATT_HD), lambda i: (0, 0, 0))],
        out_shape=[_sds((S, D), F32), _sds((S, NK), CDT), _sds((HK, 1, ATT_HD), F32)],
        compiler_params=_params(("arbitrary",)), name=name)(dz, dq, dkv4, wq, wkv)


def _inproj_bwd(dz, dproj, wain, name, after=None):
    S = dz.shape[0]
    tm = _tile(S)
    nb = wain.shape[-1]

    def body(dz_ref, dp_ref, w_ref, dx_ref, acc):
        j = pl.program_id(1)
        _acc(acc, _dot_nt(dp_ref[...], w_ref[...]), j == 0)

        @pl.when(j == N_DEV - 1)
        def _():
            dx_ref[...] = ALPHA * dz_ref[...] + acc[...]

    row = pl.BlockSpec((tm, D), lambda i, j: (i, 0))
    return _call(
        after, body, grid=(S // tm, N_DEV),
        in_specs=[row, pl.BlockSpec((None, tm, nb), lambda i, j: (j // 2, i, j % 2)),
                  pl.BlockSpec((None, D, nb), lambda i, j: (j, 0, 0))],
        out_specs=row, out_shape=_sds((S, D), F32), scratch_shapes=[pltpu.VMEM((tm, D), F32)],
        compiler_params=_params(("arbitrary", "arbitrary")), name=name)(dz, dproj, wain)


def _hp(a, b, dims):
    ah = a.astype(CDT)
    al = (a - ah.astype(F32)).astype(CDT)
    bh = b.astype(CDT)
    bl = (b - bh.astype(F32)).astype(CDT)
    dg = lambda u, w: lax.dot_general(u, w, (dims, ((), ())), preferred_element_type=F32)
    return dg(ah, bh) + (dg(ah, bl) + dg(al, bh))


def _hdot(a, b):
    return _hp(a, b, ((1,), (0,)))


def _hdot_nt(a, b):
    return _hp(a, b, ((1,), (1,)))


def _hdot_tn(a, b):
    return _hp(a, b, ((0,), (0,)))


def _tri_dot(tri, x):
    hi = x.astype(CDT)
    r1 = x - hi.astype(F32)
    mid = r1.astype(CDT)
    lo = (r1 - mid.astype(F32)).astype(CDT)
    t = tri.astype(CDT)
    return (jnp.dot(t, hi, preferred_element_type=F32) + jnp.dot(t, mid, preferred_element_type=F32)
            + jnp.dot(t, lo, preferred_element_type=F32))


def _hg_gates(q, f, alb_ref):
    a0, a1 = alb_ref[0:1, :], alb_ref[1:2, :]
    mx = jnp.maximum(a0, a1)
    e0, e1 = jnp.exp(a0 - mx), jnp.exp(a1 - mx)
    lb = e0 / (e0 + e1)
    sig = _sigmoid(f)
    forget = lb + (1.0 - lb) * sig
    k = (1.0 - lb) * _sigmoid(-f)
    qs = q * _sigmoid(q) * (HG_DK ** -0.5)
    return qs, k, jnp.log(forget), sig, lb, forget


def _hg_intra(qs, k, b, b_scr):
    b_scr[...] = b
    bm = b_scr[pl.ds(HG_CH // 2 - 1, 1), :]
    bl = b_scr[pl.ds(HG_CH - 1, 1), :]
    eb = jnp.exp(b)
    qb = qs * eb
    e_q = jnp.exp(b - bm)
    e_k = jnp.exp(bm - b)
    e_d = jnp.exp(bl - b)
    return qb, qs * e_q, k * e_k, k * e_d, jnp.exp(bl), eb, e_q, e_k, e_d


def _hgrn_fwd(proj, alb, ngain):
    S = proj.shape[1]
    nc = S // HG_CH
    wb = HG_HPB * HG_DK

    def body(pj_ref, alb_ref, ng_ref, o_ref, y_ref, st_ref, st_scr, b_scr):
        n = pl.program_id(1)

        @pl.when(n == 0)
        def _():
            st_scr[...] = jnp.zeros_like(st_scr)

        r = lax.broadcasted_iota(jnp.int32, (HG_CH, HG_CH), 0)
        c = lax.broadcasted_iota(jnp.int32, (HG_CH, HG_CH), 1)
        causal = r >= c
        tri = causal.astype(F32)
        for j in range(HG_HPB):
            lanes = pl.ds(j * HG_DK, HG_DK)
            q, f, v, g = pj_ref[0, :, lanes], pj_ref[1, :, lanes], pj_ref[2, :, lanes], pj_ref[3, :, lanes]
            qs, k, logf, _, _, _ = _hg_gates(q, f, alb_ref.at[:, lanes])
            b = _tri_dot(tri, logf)
            qb, qt, kt, kd, ebl, _, _, _, _ = _hg_intra(qs, k, b, b_scr.at[j])
            st = st_scr[j]
            st_ref[j] = st
            a = jnp.where(causal, _dot_nt(qt, kt), 0.0)
            o = _dot(a, v) + _dot_nt(qb, st)
            st_scr[j] = st * ebl + _dot_tn(v, kd)
            o_ref[:, lanes] = o
            rinv = lax.rsqrt(jnp.mean(o * o, axis=-1, keepdims=True) + RMS_EPS)
            y_ref[:, lanes] = (o * rinv * ng_ref[...] * (g * _sigmoid(g))).astype(CDT)

    blk = pl.BlockSpec((HG_CH, wb), lambda h, n: (n, h))
    return pl.pallas_call(
        body, grid=(HG_H // HG_HPB, nc),
        in_specs=[pl.BlockSpec((4, HG_CH, wb), lambda h, n: (0, n, h)), pl.BlockSpec((2, wb), lambda h, n: (0, h)),
                  pl.BlockSpec((1, HG_DK), lambda h, n: (0, 0))],
        out_specs=[blk, blk, pl.BlockSpec((HG_HPB, None, HG_DK, HG_DK), lambda h, n: (h, n, 0, 0))],
        out_shape=[_sds((S, D), F32), _sds((S, D), CDT), _sds((HG_H, nc, HG_DK, HG_DK), F32)],
        scratch_shapes=[pltpu.VMEM((HG_HPB, HG_DK, HG_DK), F32), pltpu.VMEM((HG_HPB, HG_CH, HG_DK), F32)],
        compiler_params=_params(("arbitrary", "arbitrary")), name="hgrn_fwd")(proj, alb, ngain)


def _hgrn_bwd(proj, alb, ngain, o, states, dy):
    S = proj.shape[1]
    nc = S // HG_CH
    wb = HG_HPB * HG_DK

    def body(pj_ref, alb_ref, ng_ref, o_ref, st_ref, dy_ref, dpj_ref, dalb_ref, dng_ref, dst_scr, b_scr):
        h, n = pl.program_id(0), pl.program_id(1)

        @pl.when(n == 0)
        def _():
            dst_scr[...] = jnp.zeros_like(dst_scr)

        ng = ng_ref[...]
        r = lax.broadcasted_iota(jnp.int32, (HG_CH, HG_CH), 0)
        c = lax.broadcasted_iota(jnp.int32, (HG_CH, HG_CH), 1)
        causal = r >= c
        tri, tri_rev = causal.astype(F32), (r <= c).astype(F32)
        dng = None
        for j in range(HG_HPB):
            lanes = pl.ds(j * HG_DK, HG_DK)
            q, f, v, g = pj_ref[0, :, lanes], pj_ref[1, :, lanes], pj_ref[2, :, lanes], pj_ref[3, :, lanes]
            o_ = o_ref[:, lanes]
            dy_ = dy_ref[:, lanes]
            sg = _sigmoid(g)
            rinv = lax.rsqrt(jnp.mean(o_ * o_, axis=-1, keepdims=True) + RMS_EPS)
            nrm = o_ * rinv
            dr = dy_ * (g * sg)
            dg = dy_ * nrm * ng * (sg * (1.0 + g * (1.0 - sg)))
            dn = dr * ng
            do = rinv * (dn - nrm * jnp.mean(dn * nrm, axis=-1, keepdims=True))
            dng = _colsum(dr * nrm) if dng is None else dng + _colsum(dr * nrm)
            qs, k, logf, sig, lb, forget = _hg_gates(q, f, alb_ref.at[:, lanes])
            b = _tri_dot(tri, logf)
            qb, qt, kt, kd, ebl, eb, e_q, e_k, e_d = _hg_intra(qs, k, b, b_scr.at[j])
            st = st_ref[j]
            dstn = dst_scr[j]
            a = jnp.where(causal, _dot_nt(qt, kt), 0.0)
            da = jnp.where(causal, _hdot_nt(do, v), 0.0)
            dv = _dot_tn(a, do) + _dot_nt(kd, dstn)
            dqb = _hdot(do, st)
            dkd = _hdot(v, dstn)
            dqt = _hdot(da, kt)
            dkt = _hdot_tn(da, qt)
            dbl = _colsum(dkd * kd) + ebl * _colsum(dstn * st)
            dst_scr[j] = dstn * ebl + _dot_tn(do, qb)
            dqs = dqt * e_q + dqb * eb
            dk = dkt * e_k + dkd * e_d
            db = dqt * qt + dqb * qb - dkt * kt - dkd * kd
            dlogf = _tri_dot(tri_rev, db) + dbl
            dforget = dlogf / forget
            dsig = (1.0 - lb) * (dforget - dk)
            df = dsig * sig * (1.0 - sig)
            dlb = _colsum((dforget - dk) * (1.0 - sig))
            sq = _sigmoid(q)
            dq = dqs * (HG_DK ** -0.5) * (sq * (1.0 + q * (1.0 - sq)))
            dpj_ref[0, :, lanes] = dq.astype(CDT)
            dpj_ref[1, :, lanes] = df.astype(CDT)
            dpj_ref[2, :, lanes] = dv.astype(CDT)
            dpj_ref[3, :, lanes] = dg.astype(CDT)
            da0 = dlb * lb * (1.0 - lb)
            _acc(dalb_ref.at[pl.ds(0, 1), lanes], da0, n == 0)
            _acc(dalb_ref.at[pl.ds(1, 1), lanes], -da0, n == 0)
        _acc(dng_ref, dng, jnp.logical_and(h == 0, n == 0))

    blk = pl.BlockSpec((HG_CH, wb), lambda h, n: (nc - 1 - n, h))
    pj = pl.BlockSpec((4, HG_CH, wb), lambda h, n: (0, nc - 1 - n, h))
    alb_blk = pl.BlockSpec((2, wb), lambda h, n: (0, h))
    ng_blk = pl.BlockSpec((1, HG_DK), lambda h, n: (0, 0))
    return pl.pallas_call(
        body, grid=(HG_H // HG_HPB, nc),
        in_specs=[pj, alb_blk, ng_blk, blk,
                  pl.BlockSpec((HG_HPB, None, HG_DK, HG_DK), lambda h, n: (h, nc - 1 - n, 0, 0)), blk],
        out_specs=[pj, alb_blk, ng_blk],
        out_shape=[_sds((4, S, D), CDT), _sds((2, D), F32), _sds((1, HG_DK), F32)],
        scratch_shapes=[pltpu.VMEM((HG_HPB, HG_DK, HG_DK), F32), pltpu.VMEM((HG_HPB, HG_CH, HG_DK), F32)],
        compiler_params=_params(("arbitrary", "arbitrary")), name="hgrn_bwd")(proj, alb, ngain, o, states, dy)


def _slope(h):
    return 2.0 ** (-8.0 * (h + 1) / ATT_QH)


def _attn_mask(n):
    qi = lax.broadcasted_iota(jnp.int32, (WINDOW, 2 * WINDOW), 0)
    si = lax.broadcasted_iota(jnp.int32, (WINDOW, 2 * WINDOW), 1)
    dist = qi - si + WINDOW
    valid = (dist >= 0) & (dist < WINDOW) & (n * WINDOW - WINDOW + si >= 0)
    return valid, dist.astype(F32)


def _attn_probs(qh, kh, sink, slope, valid, distf):
    s = _dot_nt(qh, kh) * (ATT_HD ** -0.5) - slope * distf
    s = jnp.where(valid, s, NEG)
    m = jnp.maximum(jnp.max(s, axis=-1, keepdims=True), sink)
    e = jnp.exp(s - m)
    es = jnp.exp(sink - m)
    inv = 1.0 / (jnp.sum(e, axis=-1, keepdims=True) + es)
    return e * inv, es * inv


def _attn_specs(S):
    nb = S // WINDOW
    cur = lambda H: pl.BlockSpec((H, WINDOW, ATT_HD), lambda n: (0, n, 0))
    prev = lambda H: pl.BlockSpec((H, WINDOW, ATT_HD), lambda n: (0, jnp.maximum(n - 1, 0), 0))
    return nb, cur, prev


def _attn_fwd(q4, kv4, sinks):
    S = q4.shape[1]
    nb, cur, prev = _attn_specs(S)

    def body(sink_ref, q_ref, kvc_ref, kvp_ref, o_ref):
        valid, distf = _attn_mask(pl.program_id(0))
        for h in range(ATT_QH):
            kvh = h // ATT_G
            kh = jnp.concatenate([kvp_ref[kvh], kvc_ref[kvh]], axis=0)
            vh = jnp.concatenate([kvp_ref[ATT_KVH + kvh], kvc_ref[ATT_KVH + kvh]], axis=0)
            p, _ = _attn_probs(q_ref[h], kh, sink_ref[0, h], _slope(h), valid, distf)
            o_ref[h] = _dot(p, vh).astype(CDT)

    return pl.pallas_call(
        body, grid=(nb,),
        in_specs=[pl.BlockSpec(memory_space=pltpu.SMEM), cur(ATT_QH), cur(2 * ATT_KVH), prev(2 * ATT_KVH)],
        out_specs=cur(ATT_QH), out_shape=_sds((ATT_QH, S, ATT_HD), CDT),
        compiler_params=_params(("arbitrary",)), name="attn_fwd")(sinks, q4, kv4, kv4)


def _attn_bwd(q4, kv4, sinks, do4):
    S = q4.shape[1]
    nb, cur, prev = _attn_specs(S)

    def body(sink_ref, q_ref, kvc_ref, kvp_ref, do_ref, dq_ref, dkv_ref, dbq_ref, dsink_ref):
        n = pl.program_id(0)
        first = n == 0

        @pl.when(first)
        def _():
            dkv_ref[...] = jnp.zeros_like(dkv_ref)
            dsink_ref[...] = jnp.zeros_like(dsink_ref)

        valid, distf = _attn_mask(n)
        lane = lax.broadcasted_iota(jnp.int32, (1, 128), 1)
        rows_cur = pl.ds(pl.multiple_of(n * WINDOW, WINDOW), WINDOW)
        rows_prev = pl.ds(pl.multiple_of(jnp.maximum(n - 1, 0) * WINDOW, WINDOW), WINDOW)
        dsinks = jnp.zeros((1, 128), F32)
        for kvh in range(ATT_KVH):
            kh = jnp.concatenate([kvp_ref[kvh], kvc_ref[kvh]], axis=0)
            vh = jnp.concatenate([kvp_ref[ATT_KVH + kvh], kvc_ref[ATT_KVH + kvh]], axis=0)
            dk = dv = None
            dqs = []
            for h in range(kvh * ATT_G, (kvh + 1) * ATT_G):
                qh = q_ref[h]
                doh = do_ref[h]
                p, ps = _attn_probs(qh, kh, sink_ref[0, h], _slope(h), valid, distf)
                dp = _dot_nt(doh, vh)
                dd = jnp.sum(p * dp, axis=-1, keepdims=True)
                ds = p * (dp - dd)
                dsinks = dsinks + jnp.where(lane == h, -jnp.sum(ps * dd, axis=0, keepdims=True), 0.0)
                dqh = _dot(ds, kh) * (ATT_HD ** -0.5)
                dqs.append(dqh.astype(CDT))
                _acc(dbq_ref.at[h], _colsum(dqh), first)
                dkh = _dot_tn(ds, qh) * (ATT_HD ** -0.5)
                dvh = _dot_tn(p, doh)
                dk = dkh if dk is None else dk + dkh
                dv = dvh if dv is None else dv + dvh
            for i in range(ATT_G // 2):
                lanes = pl.ds((kvh * ATT_G + 2 * i) * ATT_HD, 2 * ATT_HD)
                dq_ref[:, lanes] = _pair_lanes(dqs[2 * i], dqs[2 * i + 1])
            dkv_ref[kvh, rows_prev, :] += dk[:WINDOW]
            dkv_ref[kvh, rows_cur, :] += dk[WINDOW:]
            dkv_ref[ATT_KVH + kvh, rows_prev, :] += dv[:WINDOW]
            dkv_ref[ATT_KVH + kvh, rows_cur, :] += dv[WINDOW:]
        dsink_ref[...] += dsinks

    return pl.pallas_call(
        body, grid=(nb,),
        in_specs=[pl.BlockSpec(memory_space=pltpu.SMEM), cur(ATT_QH), cur(2 * ATT_KVH), prev(2 * ATT_KVH), cur(ATT_QH)],
        out_specs=[pl.BlockSpec((WINDOW, D), lambda n: (n, 0)), pl.BlockSpec((2 * ATT_KVH, S, ATT_HD), lambda n: (0, 0, 0)),
                   pl.BlockSpec((ATT_QH, 1, ATT_HD), lambda n: (0, 0, 0)), pl.BlockSpec((1, 128), lambda n: (0, 0))],
        out_shape=[_sds((S, D), CDT), _sds((2 * ATT_KVH, S, ATT_HD), F32), _sds((ATT_QH, 1, ATT_HD), F32),
                   _sds((1, 128), F32)],
        compiler_params=_params(("arbitrary",)), name="attn_bwd")(sinks, q4, kv4, kv4, do4)


def _local_step(x, p, target, getw, sm, emit):
    S = x.shape[0]
    vec = lambda a: a.reshape(1, -1)
    ln_g = lambda l, k: vec(sm["ln_gain"][l, k])
    ln_b = lambda l, k: vec(sm["ln_bias"][l, k])
    xb = x.astype(CDT)
    pb = p.astype(CDT)

    proj = _mm_nn(xb, getw("a_w_in", None), (4, S, D), (None, _tile(S), 512), lambda g, i: (g // 2, i, g % 2), F32,
                  name="a_in")
    o_a, y_a, states = _hgrn_fwd(proj, sm["a_lower_bound"], sm["a_norm_gain"])
    zeros = jnp.zeros((1, D), F32)
    z = [[None] * 3 for _ in range(2)]
    xs = [[None] * 3 for _ in range(2)]
    xbs = [[None] * 3 for _ in range(2)]
    z[0][0], xs[0][0], xbs[0][0] = _mixout_ln(y_a[None], getw("a_w_out", y_a)[None], zeros, x, ln_g(0, 0), ln_b(0, 0),
                                              "a_out_ln")
    gu, hid, sgs, ups = [None, None], [None, None], [None, None], [None, None]

    def ffn_ple(l):
        wgu = getw(f"gu{l}", xbs[l][0])
        gu[l], hid[l], z[l][1], xs[l][1], xbs[l][1] = _ffn_fwd(
            xs[l][0], xbs[l][0], wgu, getw(f"dn{l}", None), ln_g(l, 1), ln_b(l, 1), f"ffn_fwd{l}")
        sgs[l], ups[l], z[l][2], xs[l][2], xbs[l][2] = _ple_fwd(
            xs[l][1], xbs[l][1], pb[l], getw(f"pg{l}", None), vec(sm["ple_b_gate"][l]), getw(f"pu{l}", None), ln_g(l, 2),
            ln_b(l, 2), f"ple_fwd{l}")

    ffn_ple(0)
    x3, x3b = xs[0][2], xbs[0][2]
    w_kv, w_q, w_bo = getw("kv_w", x3b), getw("b_w_q", None), getw("b_w_out", None)
    kv4 = _proj_heads(x3b, w_kv, vec(sm["kv_b"]), 2 * ATT_KVH, "kv_proj")
    q4 = _proj_heads(x3b, w_q, vec(sm["b_b_q"]), ATT_QH, "q_proj")
    o4 = _attn_fwd(q4, kv4, sm["b_sinks"])
    z[1][0], xs[1][0], xbs[1][0] = _mixout_ln(o4, w_bo, sm["b_b_out"], x3, ln_g(1, 0), ln_b(1, 0), "b_out_ln")
    ffn_ple(1)
    loss, dy = _loss_fwd_bwd(xs[1][2], target)

    gs = {}
    d_ln_g = [[None] * 3 for _ in range(2)]
    d_ln_b = [[None] * 3 for _ in range(2)]
    g_bg = [None, None]

    def ffn_ple_bwd(l, dy, after=None):
        dx2, dgl, dup, d_ln_g[l][2], d_ln_b[l][2], g_bg[l] = _ple_bwd(dy, z[l][2], sgs[l], ups[l], ln_g(l, 2),
                                                                     getw(f"pg{l}", None), f"ple_bwd{l}", after=after)
        g_pg = _wgrad(xbs[l][1][None], dgl[None], f"g_ple_gate{l}")[0]
        g_pu = _wgrad(pb[l][None], dup[None], f"g_ple_up{l}")[0]
        dx1, dzb, dgu, d_ln_g[l][1], d_ln_b[l][1] = _ffn_bwd(dx2, z[l][1], gu[l], getw(f"gu{l}", None),
                                                           getw(f"dn{l}", None), ln_g(l, 1), f"ffn_bwd{l}")
        g_dn = _wgrad(hid[l], dzb[None], f"g_ffn_down{l}")
        g_gu = _wgrad(xbs[l][0][None], dgu.reshape(8, S, FFN_B), f"g_ffn_gate_up{l}")
        return dx1, emit({f"pg{l}": g_pg, f"pu{l}": g_pu, f"dn{l}": g_dn, f"gu{l}": g_gu})

    dx1, tok = ffn_ple_bwd(1, dy)
    dz, dzb, do4, d_ln_g[1][0], d_ln_b[1][0], gs["b_b_out"] = _mixout_bwd(dx1, z[1][0], ln_g(1, 0), w_bo, CDT,
                                                                         "b_out_bwd", after=tok)
    g_bo = _wgrad(o4, dzb[None], "g_b_w_out")
    dq, dkv4, dbq, dsinks = _attn_bwd(q4, kv4, sm["b_sinks"], do4)
    gs["b_b_q"] = dbq
    gs["b_sinks"] = dsinks
    g_q = _wgrad(x3b[None], dq[None], "g_b_w_q")[0]
    dx3, dkv, gs["kv_b"] = _qkv_bwd(dz, dq, dkv4, w_q, w_kv, "qkv_bwd", after=tok)
    g_kv = _wgrad(x3b[None], dkv[None], "g_kv_w")[0]
    tok = emit({"b_w_out": g_bo, "b_w_q": g_q, "kv_w": g_kv})
    dx1, tok = ffn_ple_bwd(0, dx3, tok)
    w_ao = getw("a_w_out", None)
    dz, dzb, dyr, d_ln_g[0][0], d_ln_b[0][0], _ = _mixout_bwd(dx1, z[0][0], ln_g(0, 0), w_ao[None], F32, "a_out_bwd",
                                                              after=tok)
    g_ao = _wgrad(y_a[None], dzb[None], "g_a_w_out")[0]
    dproj, gs["a_lower_bound"], gs["a_norm_gain"] = _hgrn_bwd(proj, sm["a_lower_bound"], sm["a_norm_gain"], o_a, states,
                                                              dyr[0])
    tk = lambda t: (None, t, D)
    g_ain = _mm_tn(xb[None], dproj, N_DEV, lambda g, k: (0, k, 0), lambda g, k: (g // 2, k, g % 2),
                   tk, lambda t: (None, t, 512), (N_DEV, D, 512), (None, D, 512), lambda g, k: (g, 0, 0), name="g_a_w_in")
    tok = emit({"a_w_out": g_ao, "a_w_in": g_ain})
    grad_x = _inproj_bwd(dz, dproj, getw("a_w_in", None), "a_in_bwd", after=tok)
    gs["ple_b_gate"] = jnp.concatenate(g_bg, axis=0)
    gs["ln_gain"] = jnp.stack([jnp.concatenate(r, axis=0) for r in d_ln_g])
    gs["ln_bias"] = jnp.stack([jnp.concatenate(r, axis=0) for r in d_ln_b])
    return loss, grad_x, gs


def _peer(k):
    x, y, c = lax.axis_index("x"), lax.axis_index("y"), lax.axis_index("c")
    px = 1 - x if k & 4 else x
    py = 1 - y if k & 2 else y
    pc = 1 - c if k & 1 else c
    return (px, py, pc), 4 * px + 2 * py + pc


def _my_index():
    return 4 * lax.axis_index("x") + 2 * lax.axis_index("y") + lax.axis_index("c")


def _exchange(srcs, dst_shapes, plan, name):
    n_src, n_piece = len(srcs), len(plan)

    def body(*refs):
        src_refs, dst_refs = refs[:n_src], refs[n_src:n_src + len(dst_shapes)]
        send_sems, recv_sems, local_sems = refs[n_src + len(dst_shapes):]
        me = _my_index()

        def at(ref, idx):
            return ref.at[idx] if idx else ref

        local = []
        for t, (si, sfn, di, dfn) in enumerate(plan):
            cp = pltpu.make_async_copy(at(src_refs[si], sfn(me)), at(dst_refs[di], dfn(me)), local_sems.at[t])
            cp.start()
            local.append(cp)
        sends = []
        for k in range(1, N_DEV):
            peer, pid = _peer(k)
            for t, (si, sfn, di, dfn) in enumerate(plan):
                cp = pltpu.make_async_remote_copy(
                    src_ref=at(src_refs[si], sfn(pid)), dst_ref=at(dst_refs[di], dfn(me)),
                    send_sem=send_sems.at[t * 7 + k - 1], recv_sem=recv_sems.at[t * 7 + k - 1],
                    device_id=peer, device_id_type=MESH)
                cp.start()
                sends.append(cp)
        for k in range(1, N_DEV):
            peer, pid = _peer(k)
            for t, (si, sfn, di, dfn) in enumerate(plan):
                pltpu.make_async_remote_copy(
                    src_ref=at(src_refs[si], sfn(me)), dst_ref=at(dst_refs[di], dfn(pid)),
                    send_sem=send_sems.at[t * 7 + k - 1], recv_sem=recv_sems.at[t * 7 + k - 1],
                    device_id=peer, device_id_type=MESH).wait_recv()
        for cp in sends:
            cp.wait_send()
        for cp in local:
            cp.wait()

    hbm = pl.BlockSpec(memory_space=pltpu.HBM)
    return pl.pallas_call(
        body, in_specs=[hbm] * n_src, out_specs=[hbm] * len(dst_shapes), out_shape=dst_shapes,
        scratch_shapes=[pltpu.SemaphoreType.DMA((7 * n_piece,)), pltpu.SemaphoreType.DMA((7 * n_piece,)),
                        pltpu.SemaphoreType.DMA((n_piece,))],
        name=name)(*srcs)


def _gather(shards, name):
    dsts = [_sds((N_DEV,) + a.shape, a.dtype) for a in shards]
    plan = [(i, lambda j: (), i, lambda s: (s,)) for i in range(len(shards))]
    return _exchange(shards, dsts, plan, name)


_HBM = pl.BlockSpec(memory_space=pltpu.HBM)
_SEM = pl.BlockSpec(memory_space=pltpu.SEMAPHORE)
_DATAFLOW = pltpu.SideEffectType.DATAFLOW_SIDE_EFFECTING


def _piece_copy(mode, src, land, send_sems, recv_sems, t, k, sender, receiver, peer):
    return pltpu.make_async_remote_copy(
        src_ref=src if mode == "gather" else src.at[receiver], dst_ref=land.at[sender],
        send_sem=send_sems.at[t * 7 + k - 1], recv_sem=recv_sems.at[t * 7 + k - 1], device_id=peer, device_id_type=MESH)


def _sequencer_exchange(srcs, mode, name, collective_id, after=None):
    n = len(srcs)
    land_shapes = [((N_DEV,) + a.shape) if mode == "gather" else a.shape for a in srcs]
    extra = [] if after is None else [after]

    def body(*refs):
        src_refs, land_refs = refs[:n], refs[n + len(extra):2 * n + len(extra)]
        send_sems, recv_sems, local_sems = refs[2 * n + len(extra):]
        barrier = pltpu.get_barrier_semaphore()
        for k in range(1, N_DEV):
            pl.semaphore_signal(barrier, inc=1, device_id=_peer(k)[0], device_id_type=MESH)
        pl.semaphore_wait(barrier, N_DEV - 1)
        me = _my_index()
        local = []
        for i in range(n):
            cp = pltpu.make_async_copy(src_refs[i] if mode == "gather" else src_refs[i].at[me], land_refs[i].at[me],
                                       local_sems.at[i])
            cp.start()
            local.append(cp)
        for k in range(1, N_DEV):
            peer, pid = _peer(k)
            for t in range(n):
                _piece_copy(mode, src_refs[t], land_refs[t], send_sems, recv_sems, t, k, me, pid, peer).start()
        for k in range(1, N_DEV):
            peer, pid = _peer(k)
            for t in range(n):
                _piece_copy(mode, src_refs[t], land_refs[t], send_sems, recv_sems, t, k, pid, me, peer).wait_recv()
        for k in range(1, N_DEV):
            peer, pid = _peer(k)
            for t in range(n):
                _piece_copy(mode, src_refs[t], land_refs[t], send_sems, recv_sems, t, k, me, pid, peer).wait_send()
        for cp in local:
            cp.wait()

    return pl.kernel(
        body, out_type=[_sds(s, a.dtype) for s, a in zip(land_shapes, srcs)],
        mesh=plsc.ScalarSubcoreMesh(axis_name="sequencer", num_cores=1),
        scratch_types=[pltpu.SemaphoreType.DMA((7 * n,)), pltpu.SemaphoreType.DMA((7 * n,)), pltpu.SemaphoreType.DMA((n,))],
        compiler_params=pltpu.CompilerParams(collective_id=collective_id), name=name)(*srcs, *extra)


def _xstart(groups, mode, name, after=None):
    flat = [a for g in groups for a in g]
    n, ng = len(flat), len(groups)
    land_shapes = [((N_DEV,) + a.shape) if mode == "gather" else a.shape for a in flat]
    first = [sum(len(g) for g in groups[:i]) for i in range(ng)]
    extra = [] if after is None else [after]

    def body(*refs):
        srcs, lands = refs[:n], refs[n:2 * n]
        sems = refs[2 * n + len(extra):2 * n + len(extra) + 2 * ng]
        tok_ref, local_sems = refs[-2], refs[-1]
        me = _my_index()
        local = []
        for i in range(n):
            cp = pltpu.make_async_copy(srcs[i] if mode == "gather" else srcs[i].at[me], lands[i].at[me], local_sems.at[i])
            cp.start()
            local.append(cp)
        for cp in local:
            cp.wait()
        for gi, g in enumerate(groups):
            for k in range(1, N_DEV):
                peer, pid = _peer(k)
                for t in range(len(g)):
                    i = first[gi] + t
                    _piece_copy(mode, srcs[i], lands[i], sems[2 * gi], sems[2 * gi + 1], t, k, me, pid, peer).start()
        tok_ref[...] = jnp.zeros_like(tok_ref)

    sem_shapes = []
    for g in groups:
        sem_shapes += [pltpu.SemaphoreType.DMA((7 * len(g),))] * 2
    thru = [pltpu.HBM(a.shape, a.dtype) for a in flat] + [pltpu.HBM(s, a.dtype) for s, a in zip(land_shapes, flat)]
    outs = pl.pallas_call(
        body, in_specs=[_HBM] * (2 * n) + [pl.BlockSpec(memory_space=pl.ANY)] * len(extra),
        out_specs=[_SEM] * (2 * ng) + [_HBM] * (2 * n) + [pl.BlockSpec(memory_space=pltpu.VMEM)],
        out_shape=sem_shapes + thru + [_sds((8, 128), F32)],
        input_output_aliases={i: 2 * ng + i for i in range(2 * n)},
        scratch_shapes=[pltpu.SemaphoreType.DMA((n,))],
        compiler_params=pltpu.CompilerParams(has_side_effects=_DATAFLOW), name=name)(
            *[pltpu.with_memory_space_constraint(a, pltpu.HBM) for a in flat],
            *[pltpu.with_memory_space_constraint(lax.empty(s, a.dtype), pltpu.HBM) for s, a in zip(land_shapes, flat)], *extra)
    sems, srcs_thru, lands_thru = outs[:2 * ng], outs[2 * ng:2 * ng + n], outs[2 * ng + n:2 * ng + 2 * n]
    handles = [(sems[2 * gi], sems[2 * gi + 1], srcs_thru[first[gi]:first[gi] + len(g)],
                lands_thru[first[gi]:first[gi] + len(g)]) for gi, g in enumerate(groups)]
    return handles, outs[-1]


def _xwait(handle, mode, after, name):
    send_sems, recv_sems, srcs_thru, lands_thru = handle
    n = len(srcs_thru)

    def body(*refs):
        srcs, lands, send, recv = refs[:n], refs[n:2 * n], refs[2 * n], refs[2 * n + 1]
        me = _my_index()
        for k in range(1, N_DEV):
            peer, pid = _peer(k)
            for t in range(n):
                _piece_copy(mode, srcs[t], lands[t], send, recv, t, k, pid, me, peer).wait_recv()
        for k in range(1, N_DEV):
            peer, pid = _peer(k)
            for t in range(n):
                _piece_copy(mode, srcs[t], lands[t], send, recv, t, k, me, pid, peer).wait_send()

    extra = [] if after is None else [after]
    outs = pl.pallas_call(
        body, in_specs=[_HBM] * (2 * n) + [_SEM, _SEM] + [pl.BlockSpec(memory_space=pl.ANY)] * len(extra),
        out_specs=[_HBM] * (2 * n),
        out_shape=[pltpu.HBM(a.shape, a.dtype) for a in list(srcs_thru) + list(lands_thru)],
        input_output_aliases={i: i for i in range(2 * n)},
        compiler_params=pltpu.CompilerParams(has_side_effects=_DATAFLOW), name=name)(
            *srcs_thru, *lands_thru, send_sems, recv_sems, *extra)
    return outs[n:]


def _adamw(w, g, m, v):
    m = ADAM_B1 * m + (1.0 - ADAM_B1) * g
    v = ADAM_B2 * v + (1.0 - ADAM_B2) * (g * g)
    m_hat = m / (1.0 - ADAM_B1 ** ADAM_STEP)
    v_hat = v / (1.0 - ADAM_B2 ** ADAM_STEP)
    delta = -ADAM_LR * (m_hat / (jnp.sqrt(v_hat) + ADAM_EPS) + ADAM_WD * w)
    return delta, m, v


def _adam_big(w, parts, m, v, name, after=None):
    L, R, C = w.shape
    tr = _tile(R, (256, 128, 176, 64, 32, 16))
    nr = R // tr

    def body(w_ref, *refs):
        p_refs, (m_ref, v_ref, g_ref, d_ref, mo_ref, vo_ref) = refs[:L], refs[L:]
        for l in range(L):
            @pl.when(pl.program_id(0) == l)
            def _(p_ref=p_refs[l]):
                g = p_ref[0].astype(F32)
                for s in range(1, N_DEV):
                    g = g + p_ref[s].astype(F32)
                g_ref[...] = g
                d_ref[...], mo_ref[...], vo_ref[...] = _adamw(w_ref[...], g, m_ref[...], v_ref[...])

    row = pl.BlockSpec((None, tr, C), lambda l, i: (l, i, 0))
    park = lambda l_of: (lambda l, i: (0, jnp.where(l == l_of, i, 0 if l_of else nr - 1), 0))
    return _call(
        after, body, grid=(L, nr),
        in_specs=[row] + [pl.BlockSpec((N_DEV, tr, C), park(l)) for l in range(L)] + [row, row],
        out_specs=[row] * 4, out_shape=[_sds((L, R, C), F32)] * 4,
        compiler_params=_params(("arbitrary", "arbitrary")), name=name)(w, *parts, m, v)


SMALL = (("a_lower_bound", 2, True), ("ln_gain", 6, True), ("ln_bias", 6, True), ("a_norm_gain", 1, False),
         ("kv_b", 4, False), ("b_b_q", 8, False), ("b_sinks", 1, False), ("b_b_out", 8, False), ("ple_b_gate", 16, False))
SUBLANES = 8


def _slot(r):
    return -(-r // SUBLANES) * SUBLANES


SMALL_ROWS = sum(_slot(r) for _, r, _ in SMALL)
PART_ROWS = sum(r * N_DEV if sh else _slot(r) for _, r, sh in SMALL)


def _pack_rows(a, rows):
    flat = a.reshape(-1)
    return jnp.pad(flat, (0, rows * 128 - flat.size)).reshape(rows, 128)


def _pack_small(d):
    return jnp.concatenate([_pack_rows(d[n], _slot(r)) for n, r, _ in SMALL], axis=0)


def _unpack_small(packed, like):
    out, r0 = {}, 0
    for n, r, _ in SMALL:
        out[n] = packed[r0:r0 + r].reshape(-1)[:like[n].size].reshape(like[n].shape)
        r0 += _slot(r)
    return out


def _pack_partials(gs):
    blocks = []
    for n, r, sharded in SMALL:
        if sharded:
            blocks.append(gs[n].reshape(r * N_DEV, 128))
        else:
            blocks.append(_pack_rows(gs[n], _slot(r)))
    return jnp.concatenate(blocks, axis=0)


def _adam_small(parts, w, m, v, after=None):
    def body(p_ref, w_ref, m_ref, v_ref, g_ref, d_ref, mo_ref, vo_ref, tot):
        me = _my_index()
        t = p_ref[0]
        for s in range(1, N_DEV):
            t = t + p_ref[s]
        tot[...] = t
        g_ref[...] = jnp.zeros_like(g_ref)
        src, dst = 0, 0
        for _, r, sharded in SMALL:
            if sharded:
                for i in range(r):
                    g_ref[pl.ds(dst + i, 1), :] = tot[pl.ds(src + i * N_DEV + me, 1), :]
                src += r * N_DEV
            else:
                g_ref[pl.ds(dst, _slot(r)), :] = tot[pl.ds(src, _slot(r)), :]
                src += _slot(r)
            dst += _slot(r)
        d_ref[...], mo_ref[...], vo_ref[...] = _adamw(w_ref[...], g_ref[...], m_ref[...], v_ref[...])

    full = pl.BlockSpec((SMALL_ROWS, 128), lambda: (0, 0))
    return _call(
        after, body, in_specs=[pl.BlockSpec((N_DEV, PART_ROWS, 128), lambda: (0, 0, 0)), full, full, full],
        out_specs=[full] * 4, out_shape=[_sds((SMALL_ROWS, 128), F32)] * 4,
        scratch_shapes=[pltpu.VMEM((PART_ROWS, 128), F32)], name="adam_small")(parts, w, m, v)


WEIGHTS = ("a_w_in", "a_lower_bound", "a_norm_gain", "a_w_out", "kv_w", "kv_b", "b_w_q", "b_b_q", "b_sinks", "b_w_out",
           "b_b_out", "ffn_w_gate_up", "ffn_w_down", "ple_w_up", "ple_w_gate", "ple_b_gate", "ln_gain", "ln_bias")


GATHER_GROUPS = (("a_w_in",), ("a_w_out", "gu0", "dn0", "pu0", "pg0"), ("kv_w", "b_w_q", "b_w_out"),
                 ("gu1", "dn1", "pu1", "pg1"))
KERNEL_LAYOUT = {
    "a_w_in": lambda a: a,
    "a_w_out": lambda a: a.reshape(D, D),
    "kv_w": lambda a: a.reshape(D, 2 * ATT_KVH * ATT_HD),
    "b_w_q": lambda a: a.reshape(D, D),
    "b_w_out": lambda a: a.reshape(ATT_QH, ATT_HD, D),
    "gu": lambda a: a.reshape(2, 4, D, FFN_B),
    "dn": lambda a: a.reshape(4, FFN_B, D),
    "pu": lambda a: a,
    "pg": lambda a: a.reshape(D, D),
}
_row_blocks = lambda a: a.reshape(N_DEV, -1, a.shape[-1])
OWNER_BLOCKS = {
    "a_w_in": lambda g: g,
    "a_w_out": _row_blocks,
    "kv_w": _row_blocks,
    "b_w_q": _row_blocks,
    "b_w_out": lambda g: _row_blocks(g.reshape(D, D)),
    "gu": lambda g: g,
    "dn": lambda g: _row_blocks(g.reshape(FFN_H, D)),
    "pu": lambda g: g.reshape(PLE_DIM, N_DEV, 128).transpose(1, 0, 2),
    "pg": _row_blocks,
}
ADAM_AFTER = {1: (("kv_w", "kv_w"), ("b_w_q", "b_w_q"), ("b_w_out", "b_w_out")),
              2: (("ffn_w_gate_up", "gu"), ("ffn_w_down", "dn"), ("ple_w_up", "pu"), ("ple_w_gate", "pg")),
              3: (("a_w_out", "a_w_out"), ("a_w_in", "a_w_in"))}


def kernel(x, p, a_w_in, a_lower_bound, a_norm_gain, a_w_out, kv_w, kv_b, b_w_q, b_b_q, b_sinks, b_w_out, b_b_out, ffn_w_gate_up, ffn_w_down, ple_w_up, ple_w_gate, ple_b_gate, ln_gain, ln_bias, loss_target, m_a_w_in, m_a_lower_bound, m_a_norm_gain, m_a_w_out, m_kv_w, m_kv_b, m_b_w_q, m_b_b_q, m_b_sinks, m_b_w_out, m_b_b_out, m_ffn_w_gate_up, m_ffn_w_down, m_ple_w_up, m_ple_w_gate, m_ple_b_gate, m_ln_gain, m_ln_bias, v_a_w_in, v_a_lower_bound, v_a_norm_gain, v_a_w_out, v_kv_w, v_kv_b, v_b_w_q, v_b_b_q, v_b_sinks, v_b_w_out, v_b_b_out, v_ffn_w_gate_up, v_ffn_w_down, v_ple_w_up, v_ple_w_gate, v_ple_b_gate, v_ln_gain, v_ln_bias):
    given = dict(locals())
    w = {n: given[n] for n in WEIGHTS}
    m = {n: given["m_" + n] for n in WEIGHTS}
    v = {n: given["v_" + n] for n in WEIGHTS}
    small_sharded = jnp.concatenate([_pack_rows(a, SUBLANES) for a in (a_lower_bound, ln_gain, ln_bias)], axis=0)
    (g_small,) = _gather([small_sharded], "gather_small")
    shards = {"a_w_in": a_w_in[0], "a_w_out": a_w_out[0], "kv_w": kv_w, "b_w_q": b_w_q[0], "b_w_out": b_w_out[0]}
    for l in range(2):
        shards.update({f"gu{l}": ffn_w_gate_up[l], f"dn{l}": ffn_w_down[l], f"pu{l}": ple_w_up[l], f"pg{l}": ple_w_gate[l]})
    gathered = {}
    for gi, g in enumerate(GATHER_GROUPS):
        lands = _sequencer_exchange([shards[n].astype(CDT) for n in g], "gather", f"gather{gi}", gi,
                                    after=None if gi else g_small)
        for n, a in zip(g, lands):
            gathered[n] = KERNEL_LAYOUT[n.rstrip("01")](a)

    def getw(key, after):
        return gathered[key]

    full_rows = lambda r0, r: g_small[:, r0:r0 + r].transpose(1, 0, 2).reshape(r, D)
    sm = {"a_lower_bound": full_rows(0, 2), "ln_gain": full_rows(SUBLANES, 6).reshape(2, 3, D),
          "ln_bias": full_rows(2 * SUBLANES, 6).reshape(2, 3, D), "a_norm_gain": a_norm_gain, "kv_b": kv_b,
          "b_b_q": b_b_q[0], "b_sinks": b_sinks, "b_b_out": b_b_out, "ple_b_gate": ple_b_gate}

    scatters = []

    def emit(grads):
        names = list(grads)
        blocks = [OWNER_BLOCKS[n.rstrip("01")](grads[n]) for n in names]
        lands = _sequencer_exchange(blocks, "scatter", f"scatter{len(scatters)}", len(GATHER_GROUPS) + len(scatters))
        scatters.append(dict(zip(names, lands)))
        return blocks

    loss, grad_x, gs = _local_step(x[0], p[:, 0], loss_target[0], getw, sm, emit)

    (parts_small,) = _gather([_pack_partials(gs)], "gather_small_grads")
    packed = _adam_small(parts_small, _pack_small(w), _pack_small(m), _pack_small(v),
                         after=[grad_x] + list(scatters[0].values()))
    small_out = [_unpack_small(a, w) for a in packed]
    out = {n: [s[n] for s in small_out] for n, _, _ in SMALL}

    parts, last = {}, grad_x
    for i, landed in enumerate(scatters):
        parts.update(landed)
        for n, key in ADAM_AFTER.get(i, ()):
            lrc = (1,) * (3 - w[n].ndim) + w[n].shape
            layers = [parts[key]] if key in parts else [parts[key + "0"], parts[key + "1"]]
            res = _adam_big(w[n].reshape(lrc), layers, m[n].reshape(lrc), v[n].reshape(lrc), "adam_" + n, after=[last])
            out[n] = [r.reshape(w[n].shape) for r in res]
            last = res[3]

    loss = lax.psum(loss[0, 0], ("x", "y", "c"))
    res = [loss, grad_x[None]]
    for i in range(4):
        res += [out[n][i] for n in WEIGHTS]
    return tuple(res)
```

```python
import jax
import jax.numpy as jnp
from jax import lax
from jax.experimental import pallas as pl
from jax.experimental.pallas import tpu as pltpu
from jax.experimental.pallas import tpu_sc as plsc

F32 = jnp.float32
CDT = jnp.bfloat16

N_DEV = 8
D = 1024
HG_H, HG_DK, HG_CH = 8, 128, 64
HG_HPB = 4
ATT_HD, ATT_QH, ATT_KVH, ATT_G, WINDOW = 64, 16, 4, 4, 128
FFN_H = 2816
FFN_B = FFN_H // 4
PLE_DIM = 256
ALPHA = (2.0 * 2) ** 0.25
LN_EPS = 1e-5
RMS_EPS = 1e-6
ADAM_LR, ADAM_B1, ADAM_B2, ADAM_EPS, ADAM_WD, ADAM_STEP = 0.001, 0.9, 0.999, 1e-08, 0.01, 10
ROW_TILES = (512, 256, 128, 64)
VMEM_LIMIT = 48 * 1024 * 1024
NEG = -1e30

MESH = pl.DeviceIdType.MESH


def _tile(n, cands=ROW_TILES):
    for t in cands:
        if n % t == 0:
            return t
    return n


def _sds(shape, dtype):
    return jax.ShapeDtypeStruct(tuple(shape), dtype)


def _params(sem):
    return pltpu.CompilerParams(dimension_semantics=sem, vmem_limit_bytes=VMEM_LIMIT)


def _dot(a, b):
    return jnp.dot(a.astype(CDT), b.astype(CDT), preferred_element_type=F32)


def _dot_nt(a, b):
    return lax.dot_general(a.astype(CDT), b.astype(CDT), (((1,), (1,)), ((), ())), preferred_element_type=F32)


def _dot_tn(a, b):
    return lax.dot_general(a.astype(CDT), b.astype(CDT), (((0,), (0,)), ((), ())), preferred_element_type=F32)


def _sigmoid(x):
    return jax.nn.sigmoid(x)


def _ln_fwd(z, g, b):
    mu = jnp.mean(z, axis=-1, keepdims=True)
    zc = z - mu
    var = jnp.mean(zc * zc, axis=-1, keepdims=True)
    return zc * lax.rsqrt(var + LN_EPS) * g + b


def _ln_bwd(z, g, dy):
    mu = jnp.mean(z, axis=-1, keepdims=True)
    zc = z - mu
    var = jnp.mean(zc * zc, axis=-1, keepdims=True)
    rstd = lax.rsqrt(var + LN_EPS)
    xhat = zc * rstd
    dxh = dy * g
    dz = rstd * (dxh - jnp.mean(dxh, axis=-1, keepdims=True) - xhat * jnp.mean(dxh * xhat, axis=-1, keepdims=True))
    return dz, xhat


def _colsum(x):
    return jnp.sum(x, axis=0, keepdims=True)


def _acc(ref, val, first):
    @pl.when(first)
    def _():
        ref[...] = val

    @pl.when(jnp.logical_not(first))
    def _():
        ref[...] += val


def _call(after, body, **kw):
    after = [] if after is None else list(after)
    if not after:
        return pl.pallas_call(body, **kw)
    kw["in_specs"] = [pl.BlockSpec(memory_space=pl.ANY)] * len(after) + list(kw["in_specs"])

    def ordered_body(*refs):
        body(*refs[len(after):])

    call = pl.pallas_call(ordered_body, **kw)
    return lambda *args: call(*after, *args)


def _mm_nn(a, b3, out_shape, oblock, omap, out_dtype, bias3=None, name="mm_nn"):
    M, K = a.shape
    G, _, Nb = b3.shape
    tm = _tile(M)

    def body(a_ref, b_ref, *rest):
        o_ref = rest[-1]
        acc = _dot(a_ref[...], b_ref[...])
        if bias3 is not None:
            acc = acc + rest[0][...]
        o_ref[...] = acc.astype(o_ref.dtype)

    in_specs = [pl.BlockSpec((tm, K), lambda g, i: (i, 0)), pl.BlockSpec((None, K, Nb), lambda g, i: (g, 0, 0))]
    args = [a, b3]
    if bias3 is not None:
        in_specs.append(pl.BlockSpec((None, 1, Nb), lambda g, i: (g, 0, 0)))
        args.append(bias3)
    return pl.pallas_call(
        body, grid=(G, M // tm), in_specs=in_specs, out_specs=pl.BlockSpec(oblock, omap),
        out_shape=_sds(out_shape, out_dtype), compiler_params=_params(("arbitrary", "arbitrary")), name=name)(*args)


def _mm_tn(a3, b3, G, amap, bmap, ablock, bblock, out_shape, oblock, omap, name="mm_tn"):
    S = a3.shape[1]
    tk = _tile(S, (512, 256, 128))
    Mo, No = oblock[-2], oblock[-1]

    def body(a_ref, b_ref, o_ref, acc):
        k = pl.program_id(1)
        _acc(acc, _dot_tn(a_ref[...], b_ref[...]), k == 0)

        @pl.when(k == pl.num_programs(1) - 1)
        def _():
            o_ref[...] = acc[...].astype(o_ref.dtype)

    return pl.pallas_call(
        body, grid=(G, S // tk),
        in_specs=[pl.BlockSpec(ablock(tk), amap), pl.BlockSpec(bblock(tk), bmap)],
        out_specs=pl.BlockSpec(oblock, omap), out_shape=_sds(out_shape, CDT),
        scratch_shapes=[pltpu.VMEM((Mo, No), F32)],
        compiler_params=_params(("arbitrary", "arbitrary")), name=name)(a3, b3)


def _wgrad(a3, b3, name):
    Ga, S, M = a3.shape
    Gb, _, N = b3.shape
    G = max(Ga, Gb)
    return _mm_tn(
        a3, b3, G,
        (lambda g, k: (g, k, 0)) if Ga > 1 else (lambda g, k: (0, k, 0)),
        (lambda g, k: (g, k, 0)) if Gb > 1 else (lambda g, k: (0, k, 0)),
        lambda tk: (None, tk, M), lambda tk: (None, tk, N),
        (G, M, N), (None, M, N), lambda g, k: (g, 0, 0), name=name)


def _mixout_ln(u3, w3, bias, xin, gain, beta, name):
    G, S, Kb = u3.shape
    tm = _tile(S)

    def body(u_ref, w_ref, b_ref, x_ref, g_ref, be_ref, z_ref, xo_ref, xob_ref):
        h = b_ref[...] + _dot(u_ref[0], w_ref[0])
        for g in range(1, G):
            h = h + _dot(u_ref[g], w_ref[g])
        z = ALPHA * x_ref[...] + h
        z_ref[...] = z
        y = _ln_fwd(z, g_ref[...], be_ref[...])
        xo_ref[...] = y
        xob_ref[...] = y.astype(CDT)

    row = pl.BlockSpec((tm, D), lambda i: (i, 0))
    vec = pl.BlockSpec((1, D), lambda i: (0, 0))
    return pl.pallas_call(
        body, grid=(S // tm,),
        in_specs=[pl.BlockSpec((G, tm, Kb), lambda i: (0, i, 0)), pl.BlockSpec((G, Kb, D), lambda i: (0, 0, 0)),
                  vec, row, vec, vec],
        out_specs=[row, row, row], out_shape=[_sds((S, D), F32), _sds((S, D), F32), _sds((S, D), CDT)],
        compiler_params=_params(("arbitrary",)), name=name)(u3, w3, bias, xin, gain, beta)


def _ffn_fwd(xin, xin_b, wgu, wdn, gain, beta, name):
    S = xin.shape[0]
    tm = _tile(S)

    def body(x_ref, xb_ref, wgu_ref, wdn_ref, g_ref, be_ref, gu_ref, hid_ref, z_ref, xo_ref, xob_ref, acc):
        j = pl.program_id(1)
        xb = xb_ref[...]
        gate = _dot(xb, wgu_ref[0])
        up = _dot(xb, wgu_ref[1])
        gu_ref[0] = gate
        gu_ref[1] = up
        hid = (gate * _sigmoid(gate) * up).astype(CDT)
        hid_ref[...] = hid
        _acc(acc, _dot(hid, wdn_ref[...]), j == 0)

        @pl.when(j == 3)
        def _():
            z = ALPHA * x_ref[...] + acc[...]
            z_ref[...] = z
            y = _ln_fwd(z, g_ref[...], be_ref[...])
            xo_ref[...] = y
            xob_ref[...] = y.astype(CDT)

    row = pl.BlockSpec((tm, D), lambda i, j: (i, 0))
    vec = pl.BlockSpec((1, D), lambda i, j: (0, 0))
    return pl.pallas_call(
        body, grid=(S // tm, 4),
        in_specs=[row, row, pl.BlockSpec((2, None, D, FFN_B), lambda i, j: (0, j, 0, 0)),
                  pl.BlockSpec((None, FFN_B, D), lambda i, j: (j, 0, 0)), vec, vec],
        out_specs=[pl.BlockSpec((2, None, tm, FFN_B), lambda i, j: (0, j, i, 0)),
                   pl.BlockSpec((None, tm, FFN_B), lambda i, j: (j, i, 0)), row, row, row],
        out_shape=[_sds((2, 4, S, FFN_B), F32), _sds((4, S, FFN_B), CDT), _sds((S, D), F32), _sds((S, D), F32),
                   _sds((S, D), CDT)],
        scratch_shapes=[pltpu.VMEM((tm, D), F32)],
        compiler_params=_params(("arbitrary", "arbitrary")), name=name)(xin, xin_b, wgu, wdn, gain, beta)


def _ple_fwd(xin, xin_b, p_b, wpg, bgate, wpu, gain, beta, name):
    S = xin.shape[0]
    tm = _tile(S)

    def body(x_ref, xb_ref, p_ref, wpg_ref, bg_ref, wpu_ref, g_ref, be_ref, sg_ref, up_ref, z_ref, xo_ref, xob_ref):
        sg = _sigmoid(_dot(xb_ref[...], wpg_ref[...]) + bg_ref[...])
        pb = p_ref[...]
        up = jnp.concatenate([_dot(pb, wpu_ref[j]) for j in range(N_DEV)], axis=-1)
        sg_ref[...] = sg
        up_ref[...] = up
        z = ALPHA * x_ref[...] + sg * up
        z_ref[...] = z
        y = _ln_fwd(z, g_ref[...], be_ref[...])
        xo_ref[...] = y
        xob_ref[...] = y.astype(CDT)

    row = pl.BlockSpec((tm, D), lambda i: (i, 0))
    vec = pl.BlockSpec((1, D), lambda i: (0, 0))
    return pl.pallas_call(
        body, grid=(S // tm,),
        in_specs=[row, row, pl.BlockSpec((tm, PLE_DIM), lambda i: (i, 0)), pl.BlockSpec((D, D), lambda i: (0, 0)), vec,
                  pl.BlockSpec((N_DEV, PLE_DIM, D // N_DEV), lambda i: (0, 0, 0)), vec, vec],
        out_specs=[row] * 5,
        out_shape=[_sds((S, D), F32)] * 4 + [_sds((S, D), CDT)],
        compiler_params=_params(("arbitrary",)), name=name)(xin, xin_b, p_b, wpg, bgate, wpu, gain, beta)


def _loss_fwd_bwd(y, target):
    S = y.shape[0]
    tm = _tile(S)

    def body(y_ref, t_ref, l_ref, dy_ref):
        e = y_ref[...] - t_ref[...]
        dy_ref[...] = e * (1.0 / D)
        part = 0.5 * jnp.sum(jnp.sum(e * e, axis=-1, keepdims=True) * (1.0 / D), axis=0, keepdims=True)
        _acc(l_ref, part, pl.program_id(0) == 0)

    row = pl.BlockSpec((tm, D), lambda i: (i, 0))
    return pl.pallas_call(
        body, grid=(S // tm,), in_specs=[row, row],
        out_specs=[pl.BlockSpec((1, 1), lambda i: (0, 0)), row],
        out_shape=[_sds((1, 1), F32), _sds((S, D), F32)],
        compiler_params=_params(("arbitrary",)), name="loss")(y, target)


def _ple_bwd(dy, z, sg, up, gain, wpg, name, after=None):
    S = dy.shape[0]
    tm = _tile(S)

    def body(dy_ref, z_ref, sg_ref, up_ref, g_ref, wpg_ref, dx_ref, dgl_ref, dup_ref, dgain_ref, dbeta_ref, dbg_ref):
        first = pl.program_id(0) == 0
        dy_ = dy_ref[...]
        dz, xhat = _ln_bwd(z_ref[...], g_ref[...], dy_)
        sg_ = sg_ref[...]
        dgl = dz * up_ref[...] * sg_ * (1.0 - sg_)
        dgl_ref[...] = dgl.astype(CDT)
        dup_ref[...] = (dz * sg_).astype(CDT)
        dx_ref[...] = ALPHA * dz + _dot_nt(dgl, wpg_ref[...])
        _acc(dgain_ref, _colsum(dy_ * xhat), first)
        _acc(dbeta_ref, _colsum(dy_), first)
        _acc(dbg_ref, _colsum(dgl), first)

    row = pl.BlockSpec((tm, D), lambda i: (i, 0))
    vec = pl.BlockSpec((1, D), lambda i: (0, 0))
    return _call(
        after, body, grid=(S // tm,), in_specs=[row, row, row, row, vec, pl.BlockSpec((D, D), lambda i: (0, 0))],
        out_specs=[row, row, row, vec, vec, vec],
        out_shape=[_sds((S, D), F32), _sds((S, D), CDT), _sds((S, D), CDT)] + [_sds((1, D), F32)] * 3,
        compiler_params=_params(("arbitrary",)), name=name)(dy, z, sg, up, gain, wpg)


def _ffn_bwd(dy, z, gu, wgu, wdn, gain, name, after=None):
    S = dy.shape[0]
    tm = _tile(S)

    def body(dy_ref, z_ref, gu_ref, wgu_ref, wdn_ref, g_ref, dx_ref, dzb_ref, dgu_ref, dgain_ref, dbeta_ref,
             dz_scr, acc):
        i, j = pl.program_id(0), pl.program_id(1)

        @pl.when(j == 0)
        def _():
            dy_ = dy_ref[...]
            dz, xhat = _ln_bwd(z_ref[...], g_ref[...], dy_)
            dz_scr[...] = dz
            dzb_ref[...] = dz.astype(CDT)
            _acc(dgain_ref, _colsum(dy_ * xhat), i == 0)
            _acc(dbeta_ref, _colsum(dy_), i == 0)

        dhid = _dot_nt(dz_scr[...], wdn_ref[...])
        gate, up = gu_ref[0], gu_ref[1]
        sg = _sigmoid(gate)
        dgate = (dhid * up * (sg * (1.0 + gate * (1.0 - sg)))).astype(CDT)
        dup = (dhid * (gate * sg)).astype(CDT)
        dgu_ref[0] = dgate
        dgu_ref[1] = dup
        _acc(acc, _dot_nt(dgate, wgu_ref[0]) + _dot_nt(dup, wgu_ref[1]), j == 0)

        @pl.when(j == 3)
        def _():
            dx_ref[...] = ALPHA * dz_scr[...] + acc[...]

    row = pl.BlockSpec((tm, D), lambda i, j: (i, 0))
    vec = pl.BlockSpec((1, D), lambda i, j: (0, 0))
    return _call(
        after, body, grid=(S // tm, 4),
        in_specs=[row, row, pl.BlockSpec((2, None, tm, FFN_B), lambda i, j: (0, j, i, 0)),
                  pl.BlockSpec((2, None, D, FFN_B), lambda i, j: (0, j, 0, 0)),
                  pl.BlockSpec((None, FFN_B, D), lambda i, j: (j, 0, 0)), vec],
        out_specs=[row, row, pl.BlockSpec((2, None, tm, FFN_B), lambda i, j: (0, j, i, 0)), vec, vec],
        out_shape=[_sds((S, D), F32), _sds((S, D), CDT), _sds((2, 4, S, FFN_B), CDT), _sds((1, D), F32),
                   _sds((1, D), F32)],
        scratch_shapes=[pltpu.VMEM((tm, D), F32), pltpu.VMEM((tm, D), F32)],
        compiler_params=_params(("arbitrary", "arbitrary")), name=name)(dy, z, gu, wgu, wdn, gain)


def _mixout_bwd(dy, z, gain, w3, du_dtype, name, after=None):
    S = dy.shape[0]
    G, Kb, _ = w3.shape
    tm = _tile(S)

    def body(dy_ref, z_ref, g_ref, w_ref, dz_ref, dzb_ref, du_ref, dgain_ref, dbeta_ref, dbias_ref):
        first = pl.program_id(0) == 0
        dy_ = dy_ref[...]
        dz, xhat = _ln_bwd(z_ref[...], g_ref[...], dy_)
        dz_ref[...] = dz
        dzb = dz.astype(CDT)
        dzb_ref[...] = dzb
        for g in range(G):
            du_ref[g] = _dot_nt(dzb, w_ref[g]).astype(du_ref.dtype)
        _acc(dgain_ref, _colsum(dy_ * xhat), first)
        _acc(dbeta_ref, _colsum(dy_), first)
        _acc(dbias_ref, _colsum(dz), first)

    row = pl.BlockSpec((tm, D), lambda i: (i, 0))
    vec = pl.BlockSpec((1, D), lambda i: (0, 0))
    return _call(
        after, body, grid=(S // tm,), in_specs=[row, row, vec, pl.BlockSpec((G, Kb, D), lambda i: (0, 0, 0))],
        out_specs=[row, row, pl.BlockSpec((G, tm, Kb), lambda i: (0, i, 0)), vec, vec, vec],
        out_shape=[_sds((S, D), F32), _sds((S, D), CDT), _sds((G, S, Kb), du_dtype)] + [_sds((1, D), F32)] * 3,
        compiler_params=_params(("arbitrary",)), name=name)(dy, z, gain, w3)


def _half_select(low):
    r = lax.broadcasted_iota(jnp.int32, (2 * ATT_HD, ATT_HD), 0)
    c = lax.broadcasted_iota(jnp.int32, (2 * ATT_HD, ATT_HD), 1)
    return (r == c + (0 if low else ATT_HD)).astype(CDT)


def _half_place(low):
    r = lax.broadcasted_iota(jnp.int32, (ATT_HD, 2 * ATT_HD), 0)
    c = lax.broadcasted_iota(jnp.int32, (ATT_HD, 2 * ATT_HD), 1)
    return (c == r + (0 if low else ATT_HD)).astype(CDT)


def _pair_lanes(even, odd):
    return (jnp.dot(even, _half_place(True), preferred_element_type=F32)
            + jnp.dot(odd, _half_place(False), preferred_element_type=F32)).astype(CDT)


def _proj_heads(a, w, bias, heads, name):
    S, K = a.shape
    N = heads * ATT_HD
    tm = _tile(S)

    def body(a_ref, w_ref, b_ref, o_ref):
        acc = (_dot(a_ref[...], w_ref[...]) + b_ref[...]).astype(CDT)
        sel = (_half_select(True), _half_select(False))
        for h in range(heads):
            pair = acc[:, (h // 2) * 2 * ATT_HD:(h // 2 + 1) * 2 * ATT_HD]
            o_ref[h] = jnp.dot(pair, sel[h % 2], preferred_element_type=F32).astype(CDT)

    return pl.pallas_call(
        body, grid=(S // tm,),
        in_specs=[pl.BlockSpec((tm, K), lambda i: (i, 0)), pl.BlockSpec((K, N), lambda i: (0, 0)),
                  pl.BlockSpec((1, N), lambda i: (0, 0))],
        out_specs=pl.BlockSpec((heads, tm, ATT_HD), lambda i: (0, i, 0)), out_shape=_sds((heads, S, ATT_HD), CDT),
        compiler_params=_params(("arbitrary",)), name=name)(a, w, bias)


def _qkv_bwd(dz, dq, dkv4, wq, wkv, name, after=None):
    S = dz.shape[0]
    tm = _tile(S)
    HK = dkv4.shape[0]
    NK = HK * ATT_HD

    def body(dz_ref, dq_ref, dkv_ref, wq_ref, wkv_ref, dx_ref, dkvn_ref, dkvb_ref):
        first = pl.program_id(0) == 0
        dkvn = jnp.concatenate([_pair_lanes(dkv_ref[2 * i].astype(CDT), dkv_ref[2 * i + 1].astype(CDT))
                                for i in range(HK // 2)], axis=-1)
        dkvn_ref[...] = dkvn
        dx_ref[...] = ALPHA * dz_ref[...] + _dot_nt(dq_ref[...], wq_ref[...]) + _dot_nt(dkvn, wkv_ref[...])
        for h in range(HK):
            _acc(dkvb_ref.at[h], _colsum(dkv_ref[h]), first)

    row = pl.BlockSpec((tm, D), lambda i: (i, 0))
    return _call(
        after, body, grid=(S // tm,),
        in_specs=[row, row, pl.BlockSpec((HK, tm, ATT_HD), lambda i: (0, i, 0)),
                  pl.BlockSpec((D, D), lambda i: (0, 0)), pl.BlockSpec((D, NK), lambda i: (0, 0))],
        out_specs=[row, pl.BlockSpec((tm, NK), lambda i: (i, 0)), pl.BlockSpec((HK, 1, ATT_HD), lambda i: (0, 0, 0))],
        out_shape=[_sds((S, D), F32), _sds((S, NK), CDT), _sds((HK, 1, ATT_HD), F32)],
        compiler_params=_params(("arbitrary",)), name=name)(dz, dq, dkv4, wq, wkv)


def _inproj_bwd(dz, dproj, wain, name, after=None):
    S = dz.shape[0]
    tm = _tile(S)
    nb = wain.shape[-1]

    def body(dz_ref, dp_ref, w_ref, dx_ref, acc):
        j = pl.program_id(1)
        _acc(acc, _dot_nt(dp_ref[...], w_ref[...]), j == 0)

        @pl.when(j == N_DEV - 1)
        def _():
            dx_ref[...] = ALPHA * dz_ref[...] + acc[...]

    row = pl.BlockSpec((tm, D), lambda i, j: (i, 0))
    return _call(
        after, body, grid=(S // tm, N_DEV),
        in_specs=[row, pl.BlockSpec((None, tm, nb), lambda i, j: (j // 2, i, j % 2)),
                  pl.BlockSpec((None, D, nb), lambda i, j: (j, 0, 0))],
        out_specs=row, out_shape=_sds((S, D), F32), scratch_shapes=[pltpu.VMEM((tm, D), F32)],
        compiler_params=_params(("arbitrary", "arbitrary")), name=name)(dz, dproj, wain)


def _hp(a, b, dims):
    ah = a.astype(CDT)
    al = (a - ah.astype(F32)).astype(CDT)
    bh = b.astype(CDT)
    bl = (b - bh.astype(F32)).astype(CDT)
    dg = lambda u, w: lax.dot_general(u, w, (dims, ((), ())), preferred_element_type=F32)
    return dg(ah, bh) + (dg(ah, bl) + dg(al, bh))


def _hdot(a, b):
    return _hp(a, b, ((1,), (0,)))


def _hdot_nt(a, b):
    return _hp(a, b, ((1,), (1,)))


def _hdot_tn(a, b):
    return _hp(a, b, ((0,), (0,)))


def _tri_dot(tri, x):
    hi = x.astype(CDT)
    r1 = x - hi.astype(F32)
    mid = r1.astype(CDT)
    lo = (r1 - mid.astype(F32)).astype(CDT)
    t = tri.astype(CDT)
    return (jnp.dot(t, hi, preferred_element_type=F32) + jnp.dot(t, mid, preferred_element_type=F32)
            + jnp.dot(t, lo, preferred_element_type=F32))


def _hg_gates(q, f, alb_ref):
    a0, a1 = alb_ref[0:1, :], alb_ref[1:2, :]
    mx = jnp.maximum(a0, a1)
    e0, e1 = jnp.exp(a0 - mx), jnp.exp(a1 - mx)
    lb = e0 / (e0 + e1)
    sig = _sigmoid(f)
    forget = lb + (1.0 - lb) * sig
    k = (1.0 - lb) * _sigmoid(-f)
    qs = q * _sigmoid(q) * (HG_DK ** -0.5)
    return qs, k, jnp.log(forget), sig, lb, forget


def _hg_intra(qs, k, b, b_scr):
    b_scr[...] = b
    bm = b_scr[pl.ds(HG_CH // 2 - 1, 1), :]
    bl = b_scr[pl.ds(HG_CH - 1, 1), :]
    eb = jnp.exp(b)
    qb = qs * eb
    e_q = jnp.exp(b - bm)
    e_k = jnp.exp(bm - b)
    e_d = jnp.exp(bl - b)
    return qb, qs * e_q, k * e_k, k * e_d, jnp.exp(bl), eb, e_q, e_k, e_d


def _hgrn_fwd(proj, alb, ngain):
    S = proj.shape[1]
    nc = S // HG_CH
    wb = HG_HPB * HG_DK

    def body(pj_ref, alb_ref, ng_ref, o_ref, y_ref, st_ref, st_scr, b_scr):
        n = pl.program_id(1)

        @pl.when(n == 0)
        def _():
            st_scr[...] = jnp.zeros_like(st_scr)

        r = lax.broadcasted_iota(jnp.int32, (HG_CH, HG_CH), 0)
        c = lax.broadcasted_iota(jnp.int32, (HG_CH, HG_CH), 1)
        causal = r >= c
        tri = causal.astype(F32)
        for j in range(HG_HPB):
            lanes = pl.ds(j * HG_DK, HG_DK)
            q, f, v, g = pj_ref[0, :, lanes], pj_ref[1, :, lanes], pj_ref[2, :, lanes], pj_ref[3, :, lanes]
            qs, k, logf, _, _, _ = _hg_gates(q, f, alb_ref.at[:, lanes])
            b = _tri_dot(tri, logf)
            qb, qt, kt, kd, ebl, _, _, _, _ = _hg_intra(qs, k, b, b_scr.at[j])
            st = st_scr[j]
            st_ref[j] = st
            a = jnp.where(causal, _dot_nt(qt, kt), 0.0)
            o = _dot(a, v) + _dot_nt(qb, st)
            st_scr[j] = st * ebl + _dot_tn(v, kd)
            o_ref[:, lanes] = o
            rinv = lax.rsqrt(jnp.mean(o * o, axis=-1, keepdims=True) + RMS_EPS)
            y_ref[:, lanes] = (o * rinv * ng_ref[...] * (g * _sigmoid(g))).astype(CDT)

    blk = pl.BlockSpec((HG_CH, wb), lambda h, n: (n, h))
    return pl.pallas_call(
        body, grid=(HG_H // HG_HPB, nc),
        in_specs=[pl.BlockSpec((4, HG_CH, wb), lambda h, n: (0, n, h)), pl.BlockSpec((2, wb), lambda h, n: (0, h)),
                  pl.BlockSpec((1, HG_DK), lambda h, n: (0, 0))],
        out_specs=[blk, blk, pl.BlockSpec((HG_HPB, None, HG_DK, HG_DK), lambda h, n: (h, n, 0, 0))],
        out_shape=[_sds((S, D), F32), _sds((S, D), CDT), _sds((HG_H, nc, HG_DK, HG_DK), F32)],
        scratch_shapes=[pltpu.VMEM((HG_HPB, HG_DK, HG_DK), F32), pltpu.VMEM((HG_HPB, HG_CH, HG_DK), F32)],
        compiler_params=_params(("arbitrary", "arbitrary")), name="hgrn_fwd")(proj, alb, ngain)


def _hgrn_bwd(proj, alb, ngain, o, states, dy):
    S = proj.shape[1]
    nc = S // HG_CH
    wb = HG_HPB * HG_DK

    def body(pj_ref, alb_ref, ng_ref, o_ref, st_ref, dy_ref, dpj_ref, dalb_ref, dng_ref, dst_scr, b_scr):
        h, n = pl.program_id(0), pl.program_id(1)

        @pl.when(n == 0)
        def _():
            dst_scr[...] = jnp.zeros_like(dst_scr)

        ng = ng_ref[...]
        r = lax.broadcasted_iota(jnp.int32, (HG_CH, HG_CH), 0)
        c = lax.broadcasted_iota(jnp.int32, (HG_CH, HG_CH), 1)
        causal = r >= c
        tri, tri_rev = causal.astype(F32), (r <= c).astype(F32)
        dng = None
        for j in range(HG_HPB):
            lanes = pl.ds(j * HG_DK, HG_DK)
            q, f, v, g = pj_ref[0, :, lanes], pj_ref[1, :, lanes], pj_ref[2, :, lanes], pj_ref[3, :, lanes]
            o_ = o_ref[:, lanes]
            dy_ = dy_ref[:, lanes]
            sg = _sigmoid(g)
            rinv = lax.rsqrt(jnp.mean(o_ * o_, axis=-1, keepdims=True) + RMS_EPS)
            nrm = o_ * rinv
            dr = dy_ * (g * sg)
            dg = dy_ * nrm * ng * (sg * (1.0 + g * (1.0 - sg)))
            dn = dr * ng
            do = rinv * (dn - nrm * jnp.mean(dn * nrm, axis=-1, keepdims=True))
            dng = _colsum(dr * nrm) if dng is None else dng + _colsum(dr * nrm)
            qs, k, logf, sig, lb, forget = _hg_gates(q, f, alb_ref.at[:, lanes])
            b = _tri_dot(tri, logf)
            qb, qt, kt, kd, ebl, eb, e_q, e_k, e_d = _hg_intra(qs, k, b, b_scr.at[j])
            st = st_ref[j]
            dstn = dst_scr[j]
            a = jnp.where(causal, _dot_nt(qt, kt), 0.0)
            da = jnp.where(causal, _hdot_nt(do, v), 0.0)
            dv = _dot_tn(a, do) + _dot_nt(kd, dstn)
            dqb = _hdot(do, st)
            dkd = _hdot(v, dstn)
            dqt = _hdot(da, kt)
            dkt = _hdot_tn(da, qt)
            dbl = _colsum(dkd * kd) + ebl * _colsum(dstn * st)
            dst_scr[j] = dstn * ebl + _dot_tn(do, qb)
            dqs = dqt * e_q + dqb * eb
            dk = dkt * e_k + dkd * e_d
            db = dqt * qt + dqb * qb - dkt * kt - dkd * kd
            dlogf = _tri_dot(tri_rev, db) + dbl
            dforget = dlogf / forget
            dsig = (1.0 - lb) * (dforget - dk)
            df = dsig * sig * (1.0 - sig)
            dlb = _colsum((dforget - dk) * (1.0 - sig))
            sq = _sigmoid(q)
            dq = dqs * (HG_DK ** -0.5) * (sq * (1.0 + q * (1.0 - sq)))
            dpj_ref[0, :, lanes] = dq.astype(CDT)
            dpj_ref[1, :, lanes] = df.astype(CDT)
            dpj_ref[2, :, lanes] = dv.astype(CDT)
            dpj_ref[3, :, lanes] = dg.astype(CDT)
            da0 = dlb * lb * (1.0 - lb)
            _acc(dalb_ref.at[pl.ds(0, 1), lanes], da0, n == 0)
            _acc(dalb_ref.at[pl.ds(1, 1), lanes], -da0, n == 0)
        _acc(dng_ref, dng, jnp.logical_and(h == 0, n == 0))

    blk = pl.BlockSpec((HG_CH, wb), lambda h, n: (nc - 1 - n, h))
    pj = pl.BlockSpec((4, HG_CH, wb), lambda h, n: (0, nc - 1 - n, h))
    alb_blk = pl.BlockSpec((2, wb), lambda h, n: (0, h))
    ng_blk = pl.BlockSpec((1, HG_DK), lambda h, n: (0, 0))
    return pl.pallas_call(
        body, grid=(HG_H // HG_HPB, nc),
        in_specs=[pj, alb_blk, ng_blk, blk,
                  pl.BlockSpec((HG_HPB, None, HG_DK, HG_DK), lambda h, n: (h, nc - 1 - n, 0, 0)), blk],
        out_specs=[pj, alb_blk, ng_blk],
        out_shape=[_sds((4, S, D), CDT), _sds((2, D), F32), _sds((1, HG_DK), F32)],
        scratch_shapes=[pltpu.VMEM((HG_HPB, HG_DK, HG_DK), F32), pltpu.VMEM((HG_HPB, HG_CH, HG_DK), F32)],
        compiler_params=_params(("arbitrary", "arbitrary")), name="hgrn_bwd")(proj, alb, ngain, o, states, dy)


def _slope(h):
    return 2.0 ** (-8.0 * (h + 1) / ATT_QH)


def _attn_mask(n):
    qi = lax.broadcasted_iota(jnp.int32, (WINDOW, 2 * WINDOW), 0)
    si = lax.broadcasted_iota(jnp.int32, (WINDOW, 2 * WINDOW), 1)
    dist = qi - si + WINDOW
    valid = (dist >= 0) & (dist < WINDOW) & (n * WINDOW - WINDOW + si >= 0)
    return valid, dist.astype(F32)


def _attn_probs(qh, kh, sink, slope, valid, distf):
    s = _dot_nt(qh, kh) * (ATT_HD ** -0.5) - slope * distf
    s = jnp.where(valid, s, NEG)
    m = jnp.maximum(jnp.max(s, axis=-1, keepdims=True), sink)
    e = jnp.exp(s - m)
    es = jnp.exp(sink - m)
    inv = 1.0 / (jnp.sum(e, axis=-1, keepdims=True) + es)
    return e * inv, es * inv


def _attn_specs(S):
    nb = S // WINDOW
    cur = lambda H: pl.BlockSpec((H, WINDOW, ATT_HD), lambda n: (0, n, 0))
    prev = lambda H: pl.BlockSpec((H, WINDOW, ATT_HD), lambda n: (0, jnp.maximum(n - 1, 0), 0))
    return nb, cur, prev


def _attn_fwd(q4, kv4, sinks):
    S = q4.shape[1]
    nb, cur, prev = _attn_specs(S)

    def body(sink_ref, q_ref, kvc_ref, kvp_ref, o_ref):
        valid, distf = _attn_mask(pl.program_id(0))
        for h in range(ATT_QH):
            kvh = h // ATT_G
            kh = jnp.concatenate([kvp_ref[kvh], kvc_ref[kvh]], axis=0)
            vh = jnp.concatenate([kvp_ref[ATT_KVH + kvh], kvc_ref[ATT_KVH + kvh]], axis=0)
            p, _ = _attn_probs(q_ref[h], kh, sink_ref[0, h], _slope(h), valid, distf)
            o_ref[h] = _dot(p, vh).astype(CDT)

    return pl.pallas_call(
        body, grid=(nb,),
        in_specs=[pl.BlockSpec(memory_space=pltpu.SMEM), cur(ATT_QH), cur(2 * ATT_KVH), prev(2 * ATT_KVH)],
        out_specs=cur(ATT_QH), out_shape=_sds((ATT_QH, S, ATT_HD), CDT),
        compiler_params=_params(("arbitrary",)), name="attn_fwd")(sinks, q4, kv4, kv4)


def _attn_bwd(q4, kv4, sinks, do4):
    S = q4.shape[1]
    nb, cur, prev = _attn_specs(S)

    def body(sink_ref, q_ref, kvc_ref, kvp_ref, do_ref, dq_ref, dkv_ref, dbq_ref, dsink_ref):
        n = pl.program_id(0)
        first = n == 0

        @pl.when(first)
        def _():
            dkv_ref[...] = jnp.zeros_like(dkv_ref)
            dsink_ref[...] = jnp.zeros_like(dsink_ref)

        valid, distf = _attn_mask(n)
        lane = lax.broadcasted_iota(jnp.int32, (1, 128), 1)
        rows_cur = pl.ds(pl.multiple_of(n * WINDOW, WINDOW), WINDOW)
        rows_prev = pl.ds(pl.multiple_of(jnp.maximum(n - 1, 0) * WINDOW, WINDOW), WINDOW)
        dsinks = jnp.zeros((1, 128), F32)
        for kvh in range(ATT_KVH):
            kh = jnp.concatenate([kvp_ref[kvh], kvc_ref[kvh]], axis=0)
            vh = jnp.concatenate([kvp_ref[ATT_KVH + kvh], kvc_ref[ATT_KVH + kvh]], axis=0)
            dk = dv = None
            dqs = []
            for h in range(kvh * ATT_G, (kvh + 1) * ATT_G):
                qh = q_ref[h]
                doh = do_ref[h]
                p, ps = _attn_probs(qh, kh, sink_ref[0, h], _slope(h), valid, distf)
                dp = _dot_nt(doh, vh)
                dd = jnp.sum(p * dp, axis=-1, keepdims=True)
                ds = p * (dp - dd)
                dsinks = dsinks + jnp.where(lane == h, -jnp.sum(ps * dd, axis=0, keepdims=True), 0.0)
                dqh = _dot(ds, kh) * (ATT_HD ** -0.5)
                dqs.append(dqh.astype(CDT))
                _acc(dbq_ref.at[h], _colsum(dqh), first)
                dkh = _dot_tn(ds, qh) * (ATT_HD ** -0.5)
                dvh = _dot_tn(p, doh)
                dk = dkh if dk is None else dk + dkh
                dv = dvh if dv is None else dv + dvh
            for i in range(ATT_G // 2):
                lanes = pl.ds((kvh * ATT_G + 2 * i) * ATT_HD, 2 * ATT_HD)
                dq_ref[:, lanes] = _pair_lanes(dqs[2 * i], dqs[2 * i + 1])
            dkv_ref[kvh, rows_prev, :] += dk[:WINDOW]
            dkv_ref[kvh, rows_cur, :] += dk[WINDOW:]
            dkv_ref[ATT_KVH + kvh, rows_prev, :] += dv[:WINDOW]
            dkv_ref[ATT_KVH + kvh, rows_cur, :] += dv[WINDOW:]
        dsink_ref[...] += dsinks

    return pl.pallas_call(
        body, grid=(nb,),
        in_specs=[pl.BlockSpec(memory_space=pltpu.SMEM), cur(ATT_QH), cur(2 * ATT_KVH), prev(2 * ATT_KVH), cur(ATT_QH)],
        out_specs=[pl.BlockSpec((WINDOW, D), lambda n: (n, 0)), pl.BlockSpec((2 * ATT_KVH, S, ATT_HD), lambda n: (0, 0, 0)),
                   pl.BlockSpec((ATT_QH, 1, ATT_HD), lambda n: (0, 0, 0)), pl.BlockSpec((1, 128), lambda n: (0, 0))],
        out_shape=[_sds((S, D), CDT), _sds((2 * ATT_KVH, S, ATT_HD), F32), _sds((ATT_QH, 1, ATT_HD), F32),
                   _sds((1, 128), F32)],
        compiler_params=_params(("arbitrary",)), name="attn_bwd")(sinks, q4, kv4, kv4, do4)


def _local_step(x, p, target, getw, sm, emit):
    S = x.shape[0]
    vec = lambda a: a.reshape(1, -1)
    ln_g = lambda l, k: vec(sm["ln_gain"][l, k])
    ln_b = lambda l, k: vec(sm["ln_bias"][l, k])
    xb = x.astype(CDT)
    pb = p.astype(CDT)

    proj = _mm_nn(xb, getw("a_w_in", None), (4, S, D), (None, _tile(S), 512), lambda g, i: (g // 2, i, g % 2), F32,
                  name="a_in")
    o_a, y_a, states = _hgrn_fwd(proj, sm["a_lower_bound"], sm["a_norm_gain"])
    zeros = jnp.zeros((1, D), F32)
    z = [[None] * 3 for _ in range(2)]
    xs = [[None] * 3 for _ in range(2)]
    xbs = [[None] * 3 for _ in range(2)]
    z[0][0], xs[0][0], xbs[0][0] = _mixout_ln(y_a[None], getw("a_w_out", y_a)[None], zeros, x, ln_g(0, 0), ln_b(0, 0),
                                              "a_out_ln")
    gu, hid, sgs, ups = [None, None], [None, None], [None, None], [None, None]

    def ffn_ple(l):
        wgu = getw(f"gu{l}", xbs[l][0])
        gu[l], hid[l], z[l][1], xs[l][1], xbs[l][1] = _ffn_fwd(
            xs[l][0], xbs[l][0], wgu, getw(f"dn{l}", None), ln_g(l, 1), ln_b(l, 1), f"ffn_fwd{l}")
        sgs[l], ups[l], z[l][2], xs[l][2], xbs[l][2] = _ple_fwd(
            xs[l][1], xbs[l][1], pb[l], getw(f"pg{l}", None), vec(sm["ple_b_gate"][l]), getw(f"pu{l}", None), ln_g(l, 2),
            ln_b(l, 2), f"ple_fwd{l}")

    ffn_ple(0)
    x3, x3b = xs[0][2], xbs[0][2]
    w_kv, w_q, w_bo = getw("kv_w", x3b), getw("b_w_q", None), getw("b_w_out", None)
    kv4 = _proj_heads(x3b, w_kv, vec(sm["kv_b"]), 2 * ATT_KVH, "kv_proj")
    q4 = _proj_heads(x3b, w_q, vec(sm["b_b_q"]), ATT_QH, "q_proj")
    o4 = _attn_fwd(q4, kv4, sm["b_sinks"])
    z[1][0], xs[1][0], xbs[1][0] = _mixout_ln(o4, w_bo, sm["b_b_out"], x3, ln_g(1, 0), ln_b(1, 0), "b_out_ln")
    ffn_ple(1)
    loss, dy = _loss_fwd_bwd(xs[1][2], target)

    gs = {}
    d_ln_g = [[None] * 3 for _ in range(2)]
    d_ln_b = [[None] * 3 for _ in range(2)]
    g_bg = [None, None]

    def ffn_ple_bwd(l, dy, after=None):
        dx2, dgl, dup, d_ln_g[l][2], d_ln_b[l][2], g_bg[l] = _ple_bwd(dy, z[l][2], sgs[l], ups[l], ln_g(l, 2),
                                                                     getw(f"pg{l}", None), f"ple_bwd{l}", after=after)
        g_pg = _wgrad(xbs[l][1][None], dgl[None], f"g_ple_gate{l}")[0]
        g_pu = _wgrad(pb[l][None], dup[None], f"g_ple_up{l}")[0]
        dx1, dzb, dgu, d_ln_g[l][1], d_ln_b[l][1] = _ffn_bwd(dx2, z[l][1], gu[l], getw(f"gu{l}", None),
                                                           getw(f"dn{l}", None), ln_g(l, 1), f"ffn_bwd{l}")
        g_dn = _wgrad(hid[l], dzb[None], f"g_ffn_down{l}")
        g_gu = _wgrad(xbs[l][0][None], dgu.reshape(8, S, FFN_B), f"g_ffn_gate_up{l}")
        return dx1, emit({f"pg{l}": g_pg, f"pu{l}": g_pu, f"dn{l}": g_dn, f"gu{l}": g_gu})

    dx1, tok = ffn_ple_bwd(1, dy)
    dz, dzb, do4, d_ln_g[1][0], d_ln_b[1][0], gs["b_b_out"] = _mixout_bwd(dx1, z[1][0], ln_g(1, 0), w_bo, CDT,
                                                                         "b_out_bwd", after=tok)
    g_bo = _wgrad(o4, dzb[None], "g_b_w_out")
    dq, dkv4, dbq, dsinks = _attn_bwd(q4, kv4, sm["b_sinks"], do4)
    gs["b_b_q"] = dbq
    gs["b_sinks"] = dsinks
    g_q = _wgrad(x3b[None], dq[None], "g_b_w_q")[0]
    dx3, dkv, gs["kv_b"] = _qkv_bwd(dz, dq, dkv4, w_q, w_kv, "qkv_bwd", after=tok)
    g_kv = _wgrad(x3b[None], dkv[None], "g_kv_w")[0]
    tok = emit({"b_w_out": g_bo, "b_w_q": g_q, "kv_w": g_kv})
    dx1, tok = ffn_ple_bwd(0, dx3, tok)
    w_ao = getw("a_w_out", None)
    dz, dzb, dyr, d_ln_g[0][0], d_ln_b[0][0], _ = _mixout_bwd(dx1, z[0][0], ln_g(0, 0), w_ao[None], F32, "a_out_bwd",
                                                              after=tok)
    g_ao = _wgrad(y_a[None], dzb[None], "g_a_w_out")[0]
    dproj, gs["a_lower_bound"], gs["a_norm_gain"] = _hgrn_bwd(proj, sm["a_lower_bound"], sm["a_norm_gain"], o_a, states,
                                                              dyr[0])
    tk = lambda t: (None, t, D)
    g_ain = _mm_tn(xb[None], dproj, N_DEV, lambda g, k: (0, k, 0), lambda g, k: (g // 2, k, g % 2),
                   tk, lambda t: (None, t, 512), (N_DEV, D, 512), (None, D, 512), lambda g, k: (g, 0, 0), name="g_a_w_in")
    tok = emit({"a_w_out": g_ao, "a_w_in": g_ain})
    grad_x = _inproj_bwd(dz, dproj, getw("a_w_in", None), "a_in_bwd", after=tok)
    gs["ple_b_gate"] = jnp.concatenate(g_bg, axis=0)
    gs["ln_gain"] = jnp.stack([jnp.concatenate(r, axis=0) for r in d_ln_g])
    gs["ln_bias"] = jnp.stack([jnp.concatenate(r, axis=0) for r in d_ln_b])
    return loss, grad_x, gs


def _peer(k):
    x, y, c = lax.axis_index("x"), lax.axis_index("y"), lax.axis_index("c")
    px = 1 - x if k & 4 else x
    py = 1 - y if k & 2 else y
    pc = 1 - c if k & 1 else c
    return (px, py, pc), 4 * px + 2 * py + pc


def _my_index():
    return 4 * lax.axis_index("x") + 2 * lax.axis_index("y") + lax.axis_index("c")


def _exchange(srcs, dst_shapes, plan, name):
    n_src, n_piece = len(srcs), len(plan)

    def body(*refs):
        src_refs, dst_refs = refs[:n_src], refs[n_src:n_src + len(dst_shapes)]
        send_sems, recv_sems, local_sems = refs[n_src + len(dst_shapes):]
        me = _my_index()

        def at(ref, idx):
            return ref.at[idx] if idx else ref

        local = []
        for t, (si, sfn, di, dfn) in enumerate(plan):
            cp = pltpu.make_async_copy(at(src_refs[si], sfn(me)), at(dst_refs[di], dfn(me)), local_sems.at[t])
            cp.start()
            local.append(cp)
        sends = []
        for k in range(1, N_DEV):
            peer, pid = _peer(k)
            for t, (si, sfn, di, dfn) in enumerate(plan):
                cp = pltpu.make_async_remote_copy(
                    src_ref=at(src_refs[si], sfn(pid)), dst_ref=at(dst_refs[di], dfn(me)),
                    send_sem=send_sems.at[t * 7 + k - 1], recv_sem=recv_sems.at[t * 7 + k - 1],
                    device_id=peer, device_id_type=MESH)
                cp.start()
                sends.append(cp)
        for k in range(1, N_DEV):
            peer, pid = _peer(k)
            for t, (si, sfn, di, dfn) in enumerate(plan):
                pltpu.make_async_remote_copy(
                    src_ref=at(src_refs[si], sfn(me)), dst_ref=at(dst_refs[di], dfn(pid)),
                    send_sem=send_sems.at[t * 7 + k - 1], recv_sem=recv_sems.at[t * 7 + k - 1],
                    device_id=peer, device_id_type=MESH).wait_recv()
        for cp in sends:
            cp.wait_send()
        for cp in local:
            cp.wait()

    hbm = pl.BlockSpec(memory_space=pltpu.HBM)
    return pl.pallas_call(
        body, in_specs=[hbm] * n_src, out_specs=[hbm] * len(dst_shapes), out_shape=dst_shapes,
        scratch_shapes=[pltpu.SemaphoreType.DMA((7 * n_piece,)), pltpu.SemaphoreType.DMA((7 * n_piece,)),
                        pltpu.SemaphoreType.DMA((n_piece,))],
        name=name)(*srcs)


def _gather(shards, name):
    dsts = [_sds((N_DEV,) + a.shape, a.dtype) for a in shards]
    plan = [(i, lambda j: (), i, lambda s: (s,)) for i in range(len(shards))]
    return _exchange(shards, dsts, plan, name)


_HBM = pl.BlockSpec(memory_space=pltpu.HBM)
_SEM = pl.BlockSpec(memory_space=pltpu.SEMAPHORE)
_DATAFLOW = pltpu.SideEffectType.DATAFLOW_SIDE_EFFECTING


def _piece_copy(mode, src, land, send_sems, recv_sems, t, k, sender, receiver, peer):
    return pltpu.make_async_remote_copy(
        src_ref=src if mode == "gather" else src.at[receiver], dst_ref=land.at[sender],
        send_sem=send_sems.at[t * 7 + k - 1], recv_sem=recv_sems.at[t * 7 + k - 1], device_id=peer, device_id_type=MESH)


def _sequencer_exchange(srcs, mode, name, collective_id, after=None):
    n = len(srcs)
    land_shapes = [((N_DEV,) + a.shape) if mode == "gather" else a.shape for a in srcs]
    extra = [] if after is None else [after]

    def body(*refs):
        src_refs, land_refs = refs[:n], refs[n + len(extra):2 * n + len(extra)]
        send_sems, recv_sems, local_sems = refs[2 * n + len(extra):]
        barrier = pltpu.get_barrier_semaphore()
        for k in range(1, N_DEV):
            pl.semaphore_signal(barrier, inc=1, device_id=_peer(k)[0], device_id_type=MESH)
        pl.semaphore_wait(barrier, N_DEV - 1)
        me = _my_index()
        local = []
        for i in range(n):
            cp = pltpu.make_async_copy(src_refs[i] if mode == "gather" else src_refs[i].at[me], land_refs[i].at[me],
                                       local_sems.at[i])
            cp.start()
            local.append(cp)
        for k in range(1, N_DEV):
            peer, pid = _peer(k)
            for t in range(n):
                _piece_copy(mode, src_refs[t], land_refs[t], send_sems, recv_sems, t, k, me, pid, peer).start()
        for k in range(1, N_DEV):
            peer, pid = _peer(k)
            for t in range(n):
                _piece_copy(mode, src_refs[t], land_refs[t], send_sems, recv_sems, t, k, pid, me, peer).wait_recv()
        for k in range(1, N_DEV):
            peer, pid = _peer(k)
            for t in range(n):
                _piece_copy(mode, src_refs[t], land_refs[t], send_sems, recv_sems, t, k, me, pid, peer).wait_send()
        for cp in local:
            cp.wait()

    return pl.kernel(
        body, out_type=[_sds(s, a.dtype) for s, a in zip(land_shapes, srcs)],
        mesh=plsc.ScalarSubcoreMesh(axis_name="sequencer", num_cores=1),
        scratch_types=[pltpu.SemaphoreType.DMA((7 * n,)), pltpu.SemaphoreType.DMA((7 * n,)), pltpu.SemaphoreType.DMA((n,))],
        compiler_params=pltpu.CompilerParams(collective_id=collective_id), name=name)(*srcs, *extra)


def _sequencer_gather(srcs, name, collective_id, after=None):
    n = len(srcs)
    extra = [] if after is None else [after]

    def body(*refs):
        src_refs, land_refs = refs[:n], refs[n + len(extra):2 * n + len(extra)]
        send_sems, recv_sems, local_sems = refs[2 * n + len(extra):]
        x, y, c = lax.axis_index("x"), lax.axis_index("y"), lax.axis_index("c")
        sibling = (x, y, 1 - c)
        chips = [(1 - x, y), (x, 1 - y), (1 - x, 1 - y)]
        index = lambda px, py, pc: 4 * px + 2 * py + pc
        barrier = pltpu.get_barrier_semaphore()
        for peer in [sibling] + [(*chip, c) for chip in chips]:
            pl.semaphore_signal(barrier, inc=1, device_id=peer, device_id_type=MESH)
        pl.semaphore_wait(barrier, 4)

        def copy(t, k, slot, to, src=None):
            return pltpu.make_async_remote_copy(
                src_ref=land_refs[t].at[slot] if src is None else src, dst_ref=land_refs[t].at[slot],
                send_sem=send_sems.at[7 * t + k], recv_sem=recv_sems.at[7 * t + k], device_id=to, device_id_type=MESH)

        me = index(x, y, c)
        local = []
        for t in range(n):
            cp = pltpu.make_async_copy(src_refs[t], land_refs[t].at[me], local_sems.at[t])
            cp.start()
            local.append(cp)
        sends = []
        for t in range(n):
            sends.append(copy(t, 0, me, sibling, src=src_refs[t]))
            sends += [copy(t, 1 + j, me, (*chip, c), src=src_refs[t]) for j, chip in enumerate(chips)]
        for cp in sends:
            cp.start()
        for j, chip in enumerate(chips):
            for t in range(n):
                copy(t, 1 + j, index(*chip, c), sibling, src=src_refs[t]).wait_recv()
                passed = copy(t, 4 + j, index(*chip, c), sibling)
                passed.start()
                sends.append(passed)
        for t in range(n):
            copy(t, 0, index(x, y, 1 - c), sibling, src=src_refs[t]).wait_recv()
        for j, chip in enumerate(chips):
            for t in range(n):
                copy(t, 4 + j, index(*chip, 1 - c), sibling, src=src_refs[t]).wait_recv()
        for cp in sends:
            cp.wait_send()
        for cp in local:
            cp.wait()

    return pl.kernel(
        body, out_type=[_sds((N_DEV,) + a.shape, a.dtype) for a in srcs],
        mesh=plsc.ScalarSubcoreMesh(axis_name="sequencer", num_cores=1),
        scratch_types=[pltpu.SemaphoreType.DMA((7 * n,)), pltpu.SemaphoreType.DMA((7 * n,)), pltpu.SemaphoreType.DMA((n,))],
        compiler_params=pltpu.CompilerParams(collective_id=collective_id), name=name)(*srcs, *extra)


def _xstart(groups, mode, name, after=None):
    flat = [a for g in groups for a in g]
    n, ng = len(flat), len(groups)
    land_shapes = [((N_DEV,) + a.shape) if mode == "gather" else a.shape for a in flat]
    first = [sum(len(g) for g in groups[:i]) for i in range(ng)]
    extra = [] if after is None else [after]

    def body(*refs):
        srcs, lands = refs[:n], refs[n:2 * n]
        sems = refs[2 * n + len(extra):2 * n + len(extra) + 2 * ng]
        tok_ref, local_sems = refs[-2], refs[-1]
        me = _my_index()
        local = []
        for i in range(n):
            cp = pltpu.make_async_copy(srcs[i] if mode == "gather" else srcs[i].at[me], lands[i].at[me], local_sems.at[i])
            cp.start()
            local.append(cp)
        for cp in local:
            cp.wait()
        for gi, g in enumerate(groups):
            for k in range(1, N_DEV):
                peer, pid = _peer(k)
                for t in range(len(g)):
                    i = first[gi] + t
                    _piece_copy(mode, srcs[i], lands[i], sems[2 * gi], sems[2 * gi + 1], t, k, me, pid, peer).start()
        tok_ref[...] = jnp.zeros_like(tok_ref)

    sem_shapes = []
    for g in groups:
        sem_shapes += [pltpu.SemaphoreType.DMA((7 * len(g),))] * 2
    thru = [pltpu.HBM(a.shape, a.dtype) for a in flat] + [pltpu.HBM(s, a.dtype) for s, a in zip(land_shapes, flat)]
    outs = pl.pallas_call(
        body, in_specs=[_HBM] * (2 * n) + [pl.BlockSpec(memory_space=pl.ANY)] * len(extra),
        out_specs=[_SEM] * (2 * ng) + [_HBM] * (2 * n) + [pl.BlockSpec(memory_space=pltpu.VMEM)],
        out_shape=sem_shapes + thru + [_sds((8, 128), F32)],
        input_output_aliases={i: 2 * ng + i for i in range(2 * n)},
        scratch_shapes=[pltpu.SemaphoreType.DMA((n,))],
        compiler_params=pltpu.CompilerParams(has_side_effects=_DATAFLOW), name=name)(
            *[pltpu.with_memory_space_constraint(a, pltpu.HBM) for a in flat],
            *[pltpu.with_memory_space_constraint(lax.empty(s, a.dtype), pltpu.HBM) for s, a in zip(land_shapes, flat)], *extra)
    sems, srcs_thru, lands_thru = outs[:2 * ng], outs[2 * ng:2 * ng + n], outs[2 * ng + n:2 * ng + 2 * n]
    handles = [(sems[2 * gi], sems[2 * gi + 1], srcs_thru[first[gi]:first[gi] + len(g)],
                lands_thru[first[gi]:first[gi] + len(g)]) for gi, g in enumerate(groups)]
    return handles, outs[-1]


def _xwait(handle, mode, after, name):
    send_sems, recv_sems, srcs_thru, lands_thru = handle
    n = len(srcs_thru)

    def body(*refs):
        srcs, lands, send, recv = refs[:n], refs[n:2 * n], refs[2 * n], refs[2 * n + 1]
        me = _my_index()
        for k in range(1, N_DEV):
            peer, pid = _peer(k)
            for t in range(n):
                _piece_copy(mode, srcs[t], lands[t], send, recv, t, k, pid, me, peer).wait_recv()
        for k in range(1, N_DEV):
            peer, pid = _peer(k)
            for t in range(n):
                _piece_copy(mode, srcs[t], lands[t], send, recv, t, k, me, pid, peer).wait_send()

    extra = [] if after is None else [after]
    outs = pl.pallas_call(
        body, in_specs=[_HBM] * (2 * n) + [_SEM, _SEM] + [pl.BlockSpec(memory_space=pl.ANY)] * len(extra),
        out_specs=[_HBM] * (2 * n),
        out_shape=[pltpu.HBM(a.shape, a.dtype) for a in list(srcs_thru) + list(lands_thru)],
        input_output_aliases={i: i for i in range(2 * n)},
        compiler_params=pltpu.CompilerParams(has_side_effects=_DATAFLOW), name=name)(
            *srcs_thru, *lands_thru, send_sems, recv_sems, *extra)
    return outs[n:]


def _adamw(w, g, m, v):
    m = ADAM_B1 * m + (1.0 - ADAM_B1) * g
    v = ADAM_B2 * v + (1.0 - ADAM_B2) * (g * g)
    m_hat = m / (1.0 - ADAM_B1 ** ADAM_STEP)
    v_hat = v / (1.0 - ADAM_B2 ** ADAM_STEP)
    delta = -ADAM_LR * (m_hat / (jnp.sqrt(v_hat) + ADAM_EPS) + ADAM_WD * w)
    return delta, m, v


def _adam_big(w, parts, m, v, name, after=None):
    L, R, C = w.shape
    tr = _tile(R, (256, 128, 176, 64, 32, 16))
    nr = R // tr

    def body(w_ref, *refs):
        p_refs, (m_ref, v_ref, g_ref, d_ref, mo_ref, vo_ref) = refs[:L], refs[L:]
        for l in range(L):
            @pl.when(pl.program_id(0) == l)
            def _(p_ref=p_refs[l]):
                g = p_ref[0].astype(F32)
                for s in range(1, N_DEV):
                    g = g + p_ref[s].astype(F32)
                g_ref[...] = g
                d_ref[...], mo_ref[...], vo_ref[...] = _adamw(w_ref[...], g, m_ref[...], v_ref[...])

    row = pl.BlockSpec((None, tr, C), lambda l, i: (l, i, 0))
    park = lambda l_of: (lambda l, i: (0, jnp.where(l == l_of, i, 0 if l_of else nr - 1), 0))
    return _call(
        after, body, grid=(L, nr),
        in_specs=[row] + [pl.BlockSpec((N_DEV, tr, C), park(l)) for l in range(L)] + [row, row],
        out_specs=[row] * 4, out_shape=[_sds((L, R, C), F32)] * 4,
        compiler_params=_params(("arbitrary", "arbitrary")), name=name)(w, *parts, m, v)


SMALL = (("a_lower_bound", 2, True), ("ln_gain", 6, True), ("ln_bias", 6, True), ("a_norm_gain", 1, False),
         ("kv_b", 4, False), ("b_b_q", 8, False), ("b_sinks", 1, False), ("b_b_out", 8, False), ("ple_b_gate", 16, False))
SUBLANES = 8


def _slot(r):
    return -(-r // SUBLANES) * SUBLANES


SMALL_ROWS = sum(_slot(r) for _, r, _ in SMALL)
PART_ROWS = sum(r * N_DEV if sh else _slot(r) for _, r, sh in SMALL)


def _pack_rows(a, rows):
    flat = a.reshape(-1)
    return jnp.pad(flat, (0, rows * 128 - flat.size)).reshape(rows, 128)


def _pack_small(d):
    return jnp.concatenate([_pack_rows(d[n], _slot(r)) for n, r, _ in SMALL], axis=0)


def _unpack_small(packed, like):
    out, r0 = {}, 0
    for n, r, _ in SMALL:
        out[n] = packed[r0:r0 + r].reshape(-1)[:like[n].size].reshape(like[n].shape)
        r0 += _slot(r)
    return out


def _pack_partials(gs):
    blocks = []
    for n, r, sharded in SMALL:
        if sharded:
            blocks.append(gs[n].reshape(r * N_DEV, 128))
        else:
            blocks.append(_pack_rows(gs[n], _slot(r)))
    return jnp.concatenate(blocks, axis=0)


def _adam_small(parts, w, m, v, after=None):
    def body(p_ref, w_ref, m_ref, v_ref, g_ref, d_ref, mo_ref, vo_ref, tot):
        me = _my_index()
        t = p_ref[0]
        for s in range(1, N_DEV):
            t = t + p_ref[s]
        tot[...] = t
        g_ref[...] = jnp.zeros_like(g_ref)
        src, dst = 0, 0
        for _, r, sharded in SMALL:
            if sharded:
                for i in range(r):
                    g_ref[pl.ds(dst + i, 1), :] = tot[pl.ds(src + i * N_DEV + me, 1), :]
                src += r * N_DEV
            else:
                g_ref[pl.ds(dst, _slot(r)), :] = tot[pl.ds(src, _slot(r)), :]
                src += _slot(r)
            dst += _slot(r)
        d_ref[...], mo_ref[...], vo_ref[...] = _adamw(w_ref[...], g_ref[...], m_ref[...], v_ref[...])

    full = pl.BlockSpec((SMALL_ROWS, 128), lambda: (0, 0))
    return _call(
        after, body, in_specs=[pl.BlockSpec((N_DEV, PART_ROWS, 128), lambda: (0, 0, 0)), full, full, full],
        out_specs=[full] * 4, out_shape=[_sds((SMALL_ROWS, 128), F32)] * 4,
        scratch_shapes=[pltpu.VMEM((PART_ROWS, 128), F32)], name="adam_small")(parts, w, m, v)


WEIGHTS = ("a_w_in", "a_lower_bound", "a_norm_gain", "a_w_out", "kv_w", "kv_b", "b_w_q", "b_b_q", "b_sinks", "b_w_out",
           "b_b_out", "ffn_w_gate_up", "ffn_w_down", "ple_w_up", "ple_w_gate", "ple_b_gate", "ln_gain", "ln_bias")


GATHER_GROUPS = (("a_w_in",), ("a_w_out", "gu0", "dn0", "pu0", "pg0"), ("kv_w", "b_w_q", "b_w_out"),
                 ("gu1", "dn1", "pu1", "pg1"))
KERNEL_LAYOUT = {
    "a_w_in": lambda a: a,
    "a_w_out": lambda a: a.reshape(D, D),
    "kv_w": lambda a: a.reshape(D, 2 * ATT_KVH * ATT_HD),
    "b_w_q": lambda a: a.reshape(D, D),
    "b_w_out": lambda a: a.reshape(ATT_QH, ATT_HD, D),
    "gu": lambda a: a.reshape(2, 4, D, FFN_B),
    "dn": lambda a: a.reshape(4, FFN_B, D),
    "pu": lambda a: a,
    "pg": lambda a: a.reshape(D, D),
}
_row_blocks = lambda a: a.reshape(N_DEV, -1, a.shape[-1])
OWNER_BLOCKS = {
    "a_w_in": lambda g: g,
    "a_w_out": _row_blocks,
    "kv_w": _row_blocks,
    "b_w_q": _row_blocks,
    "b_w_out": lambda g: _row_blocks(g.reshape(D, D)),
    "gu": lambda g: g,
    "dn": lambda g: _row_blocks(g.reshape(FFN_H, D)),
    "pu": lambda g: g.reshape(PLE_DIM, N_DEV, 128).transpose(1, 0, 2),
    "pg": _row_blocks,
}
ADAM_AFTER = {1: (("kv_w", "kv_w"), ("b_w_q", "b_w_q"), ("b_w_out", "b_w_out")),
              2: (("ffn_w_gate_up", "gu"), ("ffn_w_down", "dn"), ("ple_w_up", "pu"), ("ple_w_gate", "pg")),
              3: (("a_w_out", "a_w_out"), ("a_w_in", "a_w_in"))}


def kernel(x, p, a_w_in, a_lower_bound, a_norm_gain, a_w_out, kv_w, kv_b, b_w_q, b_b_q, b_sinks, b_w_out, b_b_out, ffn_w_gate_up, ffn_w_down, ple_w_up, ple_w_gate, ple_b_gate, ln_gain, ln_bias, loss_target, m_a_w_in, m_a_lower_bound, m_a_norm_gain, m_a_w_out, m_kv_w, m_kv_b, m_b_w_q, m_b_b_q, m_b_sinks, m_b_w_out, m_b_b_out, m_ffn_w_gate_up, m_ffn_w_down, m_ple_w_up, m_ple_w_gate, m_ple_b_gate, m_ln_gain, m_ln_bias, v_a_w_in, v_a_lower_bound, v_a_norm_gain, v_a_w_out, v_kv_w, v_kv_b, v_b_w_q, v_b_b_q, v_b_sinks, v_b_w_out, v_b_b_out, v_ffn_w_gate_up, v_ffn_w_down, v_ple_w_up, v_ple_w_gate, v_ple_b_gate, v_ln_gain, v_ln_bias):
    given = dict(locals())
    w = {n: given[n] for n in WEIGHTS}
    m = {n: given["m_" + n] for n in WEIGHTS}
    v = {n: given["v_" + n] for n in WEIGHTS}
    small_sharded = jnp.concatenate([_pack_rows(a, SUBLANES) for a in (a_lower_bound, ln_gain, ln_bias)], axis=0)
    (g_small,) = _gather([small_sharded], "gather_small")
    shards = {"a_w_in": a_w_in[0], "a_w_out": a_w_out[0], "kv_w": kv_w, "b_w_q": b_w_q[0], "b_w_out": b_w_out[0]}
    for l in range(2):
        shards.update({f"gu{l}": ffn_w_gate_up[l], f"dn{l}": ffn_w_down[l], f"pu{l}": ple_w_up[l], f"pg{l}": ple_w_gate[l]})
    gathered = {}
    for gi, g in enumerate(GATHER_GROUPS):
        lands = _sequencer_gather([shards[n].astype(CDT) for n in g], f"gather{gi}", gi, after=None if gi else g_small)
        for n, a in zip(g, lands):
            gathered[n] = KERNEL_LAYOUT[n.rstrip("01")](a)

    def getw(key, after):
        return gathered[key]

    full_rows = lambda r0, r: g_small[:, r0:r0 + r].transpose(1, 0, 2).reshape(r, D)
    sm = {"a_lower_bound": full_rows(0, 2), "ln_gain": full_rows(SUBLANES, 6).reshape(2, 3, D),
          "ln_bias": full_rows(2 * SUBLANES, 6).reshape(2, 3, D), "a_norm_gain": a_norm_gain, "kv_b": kv_b,
          "b_b_q": b_b_q[0], "b_sinks": b_sinks, "b_b_out": b_b_out, "ple_b_gate": ple_b_gate}

    scatters = []

    def emit(grads):
        names = list(grads)
        blocks = [OWNER_BLOCKS[n.rstrip("01")](grads[n]) for n in names]
        lands = _sequencer_exchange(blocks, "scatter", f"scatter{len(scatters)}", len(GATHER_GROUPS) + len(scatters))
        scatters.append(dict(zip(names, lands)))
        return blocks

    loss, grad_x, gs = _local_step(x[0], p[:, 0], loss_target[0], getw, sm, emit)

    (parts_small,) = _gather([_pack_partials(gs)], "gather_small_grads")
    packed = _adam_small(parts_small, _pack_small(w), _pack_small(m), _pack_small(v),
                         after=[grad_x] + list(scatters[0].values()))
    small_out = [_unpack_small(a, w) for a in packed]
    out = {n: [s[n] for s in small_out] for n, _, _ in SMALL}

    parts, last = {}, grad_x
    for i, landed in enumerate(scatters):
        parts.update(landed)
        for n, key in ADAM_AFTER.get(i, ()):
            lrc = (1,) * (3 - w[n].ndim) + w[n].shape
            layers = [parts[key]] if key in parts else [parts[key + "0"], parts[key + "1"]]
            res = _adam_big(w[n].reshape(lrc), layers, m[n].reshape(lrc), v[n].reshape(lrc), "adam_" + n, after=[last])
            out[n] = [r.reshape(w[n].shape) for r in res]
            last = res[3]

    loss = lax.psum(loss[0, 0], ("x", "y", "c"))
    res = [loss, grad_x[None]]
    for i in range(4):
        res += [out[n][i] for n in WEIGHTS]
    return tuple(res)
```

```python
import jax
import jax.numpy as jnp
from jax import lax
from jax.experimental import pallas as pl
from jax.experimental.pallas import tpu as pltpu
from jax.experimental.pallas import tpu_sc as plsc

F32 = jnp.float32
CDT = jnp.bfloat16

N_DEV = 8
D = 1024
HG_H, HG_DK, HG_CH = 8, 128, 64
HG_HPB = 4
ATT_HD, ATT_QH, ATT_KVH, ATT_G, WINDOW = 64, 16, 4, 4, 128
FFN_H = 2816
FFN_B = FFN_H // 4
PLE_DIM = 256
ALPHA = (2.0 * 2) ** 0.25
LN_EPS = 1e-5
RMS_EPS = 1e-6
ADAM_LR, ADAM_B1, ADAM_B2, ADAM_EPS, ADAM_WD, ADAM_STEP = 0.001, 0.9, 0.999, 1e-08, 0.01, 10
ROW_TILES = (512, 256, 128, 64)
VMEM_LIMIT = 48 * 1024 * 1024
NEG = -1e30

MESH = pl.DeviceIdType.MESH


def _tile(n, cands=ROW_TILES):
    for t in cands:
        if n % t == 0:
            return t
    return n


def _sds(shape, dtype):
    return jax.ShapeDtypeStruct(tuple(shape), dtype)


def _params(sem):
    return pltpu.CompilerParams(dimension_semantics=sem, vmem_limit_bytes=VMEM_LIMIT)


def _dot(a, b):
    return jnp.dot(a.astype(CDT), b.astype(CDT), preferred_element_type=F32)


def _dot_nt(a, b):
    return lax.dot_general(a.astype(CDT), b.astype(CDT), (((1,), (1,)), ((), ())), preferred_element_type=F32)


def _dot_tn(a, b):
    return lax.dot_general(a.astype(CDT), b.astype(CDT), (((0,), (0,)), ((), ())), preferred_element_type=F32)


def _sigmoid(x):
    return jax.nn.sigmoid(x)


def _ln_fwd(z, g, b):
    mu = jnp.mean(z, axis=-1, keepdims=True)
    zc = z - mu
    var = jnp.mean(zc * zc, axis=-1, keepdims=True)
    return zc * lax.rsqrt(var + LN_EPS) * g + b


def _ln_bwd(z, g, dy):
    mu = jnp.mean(z, axis=-1, keepdims=True)
    zc = z - mu
    var = jnp.mean(zc * zc, axis=-1, keepdims=True)
    rstd = lax.rsqrt(var + LN_EPS)
    xhat = zc * rstd
    dxh = dy * g
    dz = rstd * (dxh - jnp.mean(dxh, axis=-1, keepdims=True) - xhat * jnp.mean(dxh * xhat, axis=-1, keepdims=True))
    return dz, xhat


def _colsum(x):
    return jnp.sum(x, axis=0, keepdims=True)


def _acc(ref, val, first):
    @pl.when(first)
    def _():
        ref[...] = val

    @pl.when(jnp.logical_not(first))
    def _():
        ref[...] += val


def _call(after, body, **kw):
    after = [] if after is None else list(after)
    if not after:
        return pl.pallas_call(body, **kw)
    kw["in_specs"] = [pl.BlockSpec(memory_space=pl.ANY)] * len(after) + list(kw["in_specs"])

    def ordered_body(*refs):
        body(*refs[len(after):])

    call = pl.pallas_call(ordered_body, **kw)
    return lambda *args: call(*after, *args)


def _mm_nn(a, b3, out_shape, oblock, omap, out_dtype, bias3=None, name="mm_nn"):
    M, K = a.shape
    G, _, Nb = b3.shape
    tm = _tile(M)

    def body(a_ref, b_ref, *rest):
        o_ref = rest[-1]
        acc = _dot(a_ref[...], b_ref[...])
        if bias3 is not None:
            acc = acc + rest[0][...]
        o_ref[...] = acc.astype(o_ref.dtype)

    in_specs = [pl.BlockSpec((tm, K), lambda g, i: (i, 0)), pl.BlockSpec((None, K, Nb), lambda g, i: (g, 0, 0))]
    args = [a, b3]
    if bias3 is not None:
        in_specs.append(pl.BlockSpec((None, 1, Nb), lambda g, i: (g, 0, 0)))
        args.append(bias3)
    return pl.pallas_call(
        body, grid=(G, M // tm), in_specs=in_specs, out_specs=pl.BlockSpec(oblock, omap),
        out_shape=_sds(out_shape, out_dtype), compiler_params=_params(("arbitrary", "arbitrary")), name=name)(*args)


def _mm_tn(a3, b3, G, amap, bmap, ablock, bblock, out_shape, oblock, omap, name="mm_tn"):
    S = a3.shape[1]
    tk = _tile(S, (512, 256, 128))
    Mo, No = oblock[-2], oblock[-1]

    def body(a_ref, b_ref, o_ref, acc):
        k = pl.program_id(1)
        _acc(acc, _dot_tn(a_ref[...], b_ref[...]), k == 0)

        @pl.when(k == pl.num_programs(1) - 1)
        def _():
            o_ref[...] = acc[...].astype(o_ref.dtype)

    return pl.pallas_call(
        body, grid=(G, S // tk),
        in_specs=[pl.BlockSpec(ablock(tk), amap), pl.BlockSpec(bblock(tk), bmap)],
        out_specs=pl.BlockSpec(oblock, omap), out_shape=_sds(out_shape, CDT),
        scratch_shapes=[pltpu.VMEM((Mo, No), F32)],
        compiler_params=_params(("arbitrary", "arbitrary")), name=name)(a3, b3)


def _wgrad(a3, b3, name):
    Ga, S, M = a3.shape
    Gb, _, N = b3.shape
    G = max(Ga, Gb)
    return _mm_tn(
        a3, b3, G,
        (lambda g, k: (g, k, 0)) if Ga > 1 else (lambda g, k: (0, k, 0)),
        (lambda g, k: (g, k, 0)) if Gb > 1 else (lambda g, k: (0, k, 0)),
        lambda tk: (None, tk, M), lambda tk: (None, tk, N),
        (G, M, N), (None, M, N), lambda g, k: (g, 0, 0), name=name)


def _mixout_ln(u3, w3, bias, xin, gain, beta, name):
    G, S, Kb = u3.shape
    tm = _tile(S)

    def body(u_ref, w_ref, b_ref, x_ref, g_ref, be_ref, z_ref, xo_ref, xob_ref):
        h = b_ref[...] + _dot(u_ref[0], w_ref[0])
        for g in range(1, G):
            h = h + _dot(u_ref[g], w_ref[g])
        z = ALPHA * x_ref[...] + h
        z_ref[...] = z
        y = _ln_fwd(z, g_ref[...], be_ref[...])
        xo_ref[...] = y
        xob_ref[...] = y.astype(CDT)

    row = pl.BlockSpec((tm, D), lambda i: (i, 0))
    vec = pl.BlockSpec((1, D), lambda i: (0, 0))
    return pl.pallas_call(
        body, grid=(S // tm,),
        in_specs=[pl.BlockSpec((G, tm, Kb), lambda i: (0, i, 0)), pl.BlockSpec((G, Kb, D), lambda i: (0, 0, 0)),
                  vec, row, vec, vec],
        out_specs=[row, row, row], out_shape=[_sds((S, D), F32), _sds((S, D), F32), _sds((S, D), CDT)],
        compiler_params=_params(("arbitrary",)), name=name)(u3, w3, bias, xin, gain, beta)


def _ffn_fwd(xin, xin_b, wgu, wdn, gain, beta, name):
    S = xin.shape[0]
    tm = _tile(S)

    def body(x_ref, xb_ref, wgu_ref, wdn_ref, g_ref, be_ref, gu_ref, hid_ref, z_ref, xo_ref, xob_ref, acc):
        j = pl.program_id(1)
        xb = xb_ref[...]
        gate = _dot(xb, wgu_ref[0])
        up = _dot(xb, wgu_ref[1])
        gu_ref[0] = gate
        gu_ref[1] = up
        hid = (gate * _sigmoid(gate) * up).astype(CDT)
        hid_ref[...] = hid
        _acc(acc, _dot(hid, wdn_ref[...]), j == 0)

        @pl.when(j == 3)
        def _():
            z = ALPHA * x_ref[...] + acc[...]
            z_ref[...] = z
            y = _ln_fwd(z, g_ref[...], be_ref[...])
            xo_ref[...] = y
            xob_ref[...] = y.astype(CDT)

    row = pl.BlockSpec((tm, D), lambda i, j: (i, 0))
    vec = pl.BlockSpec((1, D), lambda i, j: (0, 0))
    return pl.pallas_call(
        body, grid=(S // tm, 4),
        in_specs=[row, row, pl.BlockSpec((2, None, D, FFN_B), lambda i, j: (0, j, 0, 0)),
                  pl.BlockSpec((None, FFN_B, D), lambda i, j: (j, 0, 0)), vec, vec],
        out_specs=[pl.BlockSpec((2, None, tm, FFN_B), lambda i, j: (0, j, i, 0)),
                   pl.BlockSpec((None, tm, FFN_B), lambda i, j: (j, i, 0)), row, row, row],
        out_shape=[_sds((2, 4, S, FFN_B), F32), _sds((4, S, FFN_B), CDT), _sds((S, D), F32), _sds((S, D), F32),
                   _sds((S, D), CDT)],
        scratch_shapes=[pltpu.VMEM((tm, D), F32)],
        compiler_params=_params(("arbitrary", "arbitrary")), name=name)(xin, xin_b, wgu, wdn, gain, beta)


def _ple_fwd(xin, xin_b, p_b, wpg, bgate, wpu, gain, beta, name):
    S = xin.shape[0]
    tm = _tile(S)

    def body(x_ref, xb_ref, p_ref, wpg_ref, bg_ref, wpu_ref, g_ref, be_ref, sg_ref, up_ref, z_ref, xo_ref, xob_ref):
        sg = _sigmoid(_dot(xb_ref[...], wpg_ref[...]) + bg_ref[...])
        pb = p_ref[...]
        up = jnp.concatenate([_dot(pb, wpu_ref[j]) for j in range(N_DEV)], axis=-1)
        sg_ref[...] = sg
        up_ref[...] = up
        z = ALPHA * x_ref[...] + sg * up
        z_ref[...] = z
        y = _ln_fwd(z, g_ref[...], be_ref[...])
        xo_ref[...] = y
        xob_ref[...] = y.astype(CDT)

    row = pl.BlockSpec((tm, D), lambda i: (i, 0))
    vec = pl.BlockSpec((1, D), lambda i: (0, 0))
    return pl.pallas_call(
        body, grid=(S // tm,),
        in_specs=[row, row, pl.BlockSpec((tm, PLE_DIM), lambda i: (i, 0)), pl.BlockSpec((D, D), lambda i: (0, 0)), vec,
                  pl.BlockSpec((N_DEV, PLE_DIM, D // N_DEV), lambda i: (0, 0, 0)), vec, vec],
        out_specs=[row] * 5,
        out_shape=[_sds((S, D), F32)] * 4 + [_sds((S, D), CDT)],
        compiler_params=_params(("arbitrary",)), name=name)(xin, xin_b, p_b, wpg, bgate, wpu, gain, beta)


def _loss_fwd_bwd(y, target):
    S = y.shape[0]
    tm = _tile(S)

    def body(y_ref, t_ref, l_ref, dy_ref):
        e = y_ref[...] - t_ref[...]
        dy_ref[...] = e * (1.0 / D)
        part = 0.5 * jnp.sum(jnp.sum(e * e, axis=-1, keepdims=True) * (1.0 / D), axis=0, keepdims=True)
        _acc(l_ref, part, pl.program_id(0) == 0)

    row = pl.BlockSpec((tm, D), lambda i: (i, 0))
    return pl.pallas_call(
        body, grid=(S // tm,), in_specs=[row, row],
        out_specs=[pl.BlockSpec((1, 1), lambda i: (0, 0)), row],
        out_shape=[_sds((1, 1), F32), _sds((S, D), F32)],
        compiler_params=_params(("arbitrary",)), name="loss")(y, target)


def _ple_bwd(dy, z, sg, up, gain, wpg, name, after=None):
    S = dy.shape[0]
    tm = _tile(S)

    def body(dy_ref, z_ref, sg_ref, up_ref, g_ref, wpg_ref, dx_ref, dgl_ref, dup_ref, dgain_ref, dbeta_ref, dbg_ref):
        first = pl.program_id(0) == 0
        dy_ = dy_ref[...]
        dz, xhat = _ln_bwd(z_ref[...], g_ref[...], dy_)
        sg_ = sg_ref[...]
        dgl = dz * up_ref[...] * sg_ * (1.0 - sg_)
        dgl_ref[...] = dgl.astype(CDT)
        dup_ref[...] = (dz * sg_).astype(CDT)
        dx_ref[...] = ALPHA * dz + _dot_nt(dgl, wpg_ref[...])
        _acc(dgain_ref, _colsum(dy_ * xhat), first)
        _acc(dbeta_ref, _colsum(dy_), first)
        _acc(dbg_ref, _colsum(dgl), first)

    row = pl.BlockSpec((tm, D), lambda i: (i, 0))
    vec = pl.BlockSpec((1, D), lambda i: (0, 0))
    return _call(
        after, body, grid=(S // tm,), in_specs=[row, row, row, row, vec, pl.BlockSpec((D, D), lambda i: (0, 0))],
        out_specs=[row, row, row, vec, vec, vec],
        out_shape=[_sds((S, D), F32), _sds((S, D), CDT), _sds((S, D), CDT)] + [_sds((1, D), F32)] * 3,
        compiler_params=_params(("arbitrary",)), name=name)(dy, z, sg, up, gain, wpg)


def _ffn_bwd(dy, z, gu, wgu, wdn, gain, name, after=None):
    S = dy.shape[0]
    tm = _tile(S)

    def body(dy_ref, z_ref, gu_ref, wgu_ref, wdn_ref, g_ref, dx_ref, dzb_ref, dgu_ref, dgain_ref, dbeta_ref,
             dz_scr, acc):
        i, j = pl.program_id(0), pl.program_id(1)

        @pl.when(j == 0)
        def _():
            dy_ = dy_ref[...]
            dz, xhat = _ln_bwd(z_ref[...], g_ref[...], dy_)
            dz_scr[...] = dz
            dzb_ref[...] = dz.astype(CDT)
            _acc(dgain_ref, _colsum(dy_ * xhat), i == 0)
            _acc(dbeta_ref, _colsum(dy_), i == 0)

        dhid = _dot_nt(dz_scr[...], wdn_ref[...])
        gate, up = gu_ref[0], gu_ref[1]
        sg = _sigmoid(gate)
        dgate = (dhid * up * (sg * (1.0 + gate * (1.0 - sg)))).astype(CDT)
        dup = (dhid * (gate * sg)).astype(CDT)
        dgu_ref[0] = dgate
        dgu_ref[1] = dup
        _acc(acc, _dot_nt(dgate, wgu_ref[0]) + _dot_nt(dup, wgu_ref[1]), j == 0)

        @pl.when(j == 3)
        def _():
            dx_ref[...] = ALPHA * dz_scr[...] + acc[...]

    row = pl.BlockSpec((tm, D), lambda i, j: (i, 0))
    vec = pl.BlockSpec((1, D), lambda i, j: (0, 0))
    return _call(
        after, body, grid=(S // tm, 4),
        in_specs=[row, row, pl.BlockSpec((2, None, tm, FFN_B), lambda i, j: (0, j, i, 0)),
                  pl.BlockSpec((2, None, D, FFN_B), lambda i, j: (0, j, 0, 0)),
                  pl.BlockSpec((None, FFN_B, D), lambda i, j: (j, 0, 0)), vec],
        out_specs=[row, row, pl.BlockSpec((2, None, tm, FFN_B), lambda i, j: (0, j, i, 0)), vec, vec],
        out_shape=[_sds((S, D), F32), _sds((S, D), CDT), _sds((2, 4, S, FFN_B), CDT), _sds((1, D), F32),
                   _sds((1, D), F32)],
        scratch_shapes=[pltpu.VMEM((tm, D), F32), pltpu.VMEM((tm, D), F32)],
        compiler_params=_params(("arbitrary", "arbitrary")), name=name)(dy, z, gu, wgu, wdn, gain)


def _mixout_bwd(dy, z, gain, w3, du_dtype, name, after=None):
    S = dy.shape[0]
    G, Kb, _ = w3.shape
    tm = _tile(S)

    def body(dy_ref, z_ref, g_ref, w_ref, dz_ref, dzb_ref, du_ref, dgain_ref, dbeta_ref, dbias_ref):
        first = pl.program_id(0) == 0
        dy_ = dy_ref[...]
        dz, xhat = _ln_bwd(z_ref[...], g_ref[...], dy_)
        dz_ref[...] = dz
        dzb = dz.astype(CDT)
        dzb_ref[...] = dzb
        for g in range(G):
            du_ref[g] = _dot_nt(dzb, w_ref[g]).astype(du_ref.dtype)
        _acc(dgain_ref, _colsum(dy_ * xhat), first)
        _acc(dbeta_ref, _colsum(dy_), first)
        _acc(dbias_ref, _colsum(dz), first)

    row = pl.BlockSpec((tm, D), lambda i: (i, 0))
    vec = pl.BlockSpec((1, D), lambda i: (0, 0))
    return _call(
        after, body, grid=(S // tm,), in_specs=[row, row, vec, pl.BlockSpec((G, Kb, D), lambda i: (0, 0, 0))],
        out_specs=[row, row, pl.BlockSpec((G, tm, Kb), lambda i: (0, i, 0)), vec, vec, vec],
        out_shape=[_sds((S, D), F32), _sds((S, D), CDT), _sds((G, S, Kb), du_dtype)] + [_sds((1, D), F32)] * 3,
        compiler_params=_params(("arbitrary",)), name=name)(dy, z, gain, w3)


def _half_select(low):
    r = lax.broadcasted_iota(jnp.int32, (2 * ATT_HD, ATT_HD), 0)
    c = lax.broadcasted_iota(jnp.int32, (2 * ATT_HD, ATT_HD), 1)
    return (r == c + (0 if low else ATT_HD)).astype(CDT)


def _half_place(low):
    r = lax.broadcasted_iota(jnp.int32, (ATT_HD, 2 * ATT_HD), 0)
    c = lax.broadcasted_iota(jnp.int32, (ATT_HD, 2 * ATT_HD), 1)
    return (c == r + (0 if low else ATT_HD)).astype(CDT)


def _pair_lanes(even, odd):
    return (jnp.dot(even, _half_place(True), preferred_element_type=F32)
            + jnp.dot(odd, _half_place(False), preferred_element_type=F32)).astype(CDT)


def _proj_heads(a, w, bias, heads, name):
    S, K = a.shape
    N = heads * ATT_HD
    tm = _tile(S)

    def body(a_ref, w_ref, b_ref, o_ref):
        acc = (_dot(a_ref[...], w_ref[...]) + b_ref[...]).astype(CDT)
        sel = (_half_select(True), _half_select(False))
        for h in range(heads):
            pair = acc[:, (h // 2) * 2 * ATT_HD:(h // 2 + 1) * 2 * ATT_HD]
            o_ref[h] = jnp.dot(pair, sel[h % 2], preferred_element_type=F32).astype(CDT)

    return pl.pallas_call(
        body, grid=(S // tm,),
        in_specs=[pl.BlockSpec((tm, K), lambda i: (i, 0)), pl.BlockSpec((K, N), lambda i: (0, 0)),
                  pl.BlockSpec((1, N), lambda i: (0, 0))],
        out_specs=pl.BlockSpec((heads, tm, ATT_HD), lambda i: (0, i, 0)), out_shape=_sds((heads, S, ATT_HD), CDT),
        compiler_params=_params(("arbitrary",)), name=name)(a, w, bias)


def _qkv_bwd(dz, dq, dkv4, wq, wkv, name, after=None):
    S = dz.shape[0]
    tm = _tile(S)
    HK = dkv4.shape[0]
    NK = HK * ATT_HD

    def body(dz_ref, dq_ref, dkv_ref, wq_ref, wkv_ref, dx_ref, dkvn_ref, dkvb_ref):
        first = pl.program_id(0) == 0
        dkvn = jnp.concatenate([_pair_lanes(dkv_ref[2 * i].astype(CDT), dkv_ref[2 * i + 1].astype(CDT))
                                for i in range(HK // 2)], axis=-1)
        dkvn_ref[...] = dkvn
        dx_ref[...] = ALPHA * dz_ref[...] + _dot_nt(dq_ref[...], wq_ref[...]) + _dot_nt(dkvn, wkv_ref[...])
        for h in range(HK):
            _acc(dkvb_ref.at[h], _colsum(dkv_ref[h]), first)

    row = pl.BlockSpec((tm, D), lambda i: (i, 0))
    return _call(
        after, body, grid=(S // tm,),
        in_specs=[row, row, pl.BlockSpec((HK, tm, ATT_HD), lambda i: (0, i, 0)),
                  pl.BlockSpec((D, D), lambda i: (0, 0)), pl.BlockSpec((D, NK), lambda i: (0, 0))],
        out_specs=[row, pl.BlockSpec((tm, NK), lambda i: (i, 0)), pl.BlockSpec((HK, 1, ATT_HD), lambda i: (0, 0, 0))],
        out_shape=[_sds((S, D), F32), _sds((S, NK), CDT), _sds((HK, 1, ATT_HD), F32)],
        compiler_params=_params(("arbitrary",)), name=name)(dz, dq, dkv4, wq, wkv)


def _inproj_bwd(dz, dproj, wain, name, after=None):
    S = dz.shape[0]
    tm = _tile(S)
    nb = wain.shape[-1]

    def body(dz_ref, dp_ref, w_ref, dx_ref, acc):
        j = pl.program_id(1)
        _acc(acc, _dot_nt(dp_ref[...], w_ref[...]), j == 0)

        @pl.when(j == N_DEV - 1)
        def _():
            dx_ref[...] = ALPHA * dz_ref[...] + acc[...]

    row = pl.BlockSpec((tm, D), lambda i, j: (i, 0))
    return _call(
        after, body, grid=(S // tm, N_DEV),
        in_specs=[row, pl.BlockSpec((None, tm, nb), lambda i, j: (j // 2, i, j % 2)),
                  pl.BlockSpec((None, D, nb), lambda i, j: (j, 0, 0))],
        out_specs=row, out_shape=_sds((S, D), F32), scratch_shapes=[pltpu.VMEM((tm, D), F32)],
        compiler_params=_params(("arbitrary", "arbitrary")), name=name)(dz, dproj, wain)


def _running_sum(x, reverse=False):
    rows = x.shape[0]
    row = lax.broadcasted_iota(jnp.int32, x.shape, 0)
    step = 1
    while step < rows:
        if reverse:
            x = x + jnp.where(row < rows - step, pltpu.roll(x, rows - step, 0), 0.0)
        else:
            x = x + jnp.where(row >= step, pltpu.roll(x, step, 0), 0.0)
        step *= 2
    return x


def _hg_gates(q, f, alb_ref):
    a0, a1 = alb_ref[0:1, :], alb_ref[1:2, :]
    mx = jnp.maximum(a0, a1)
    e0, e1 = jnp.exp(a0 - mx), jnp.exp(a1 - mx)
    lb = e0 / (e0 + e1)
    sig = _sigmoid(f)
    forget = lb + (1.0 - lb) * sig
    k = (1.0 - lb) * _sigmoid(-f)
    qs = q * _sigmoid(q) * (HG_DK ** -0.5)
    return qs, k, jnp.log(forget), sig, lb, forget


def _hg_intra(qs, k, b, b_scr):
    b_scr[...] = b
    bm = b_scr[pl.ds(HG_CH // 2 - 1, 1), :]
    bl = b_scr[pl.ds(HG_CH - 1, 1), :]
    eb = jnp.exp(b)
    qb = qs * eb
    e_q = jnp.exp(b - bm)
    e_k = jnp.exp(bm - b)
    e_d = jnp.exp(bl - b)
    return qb, qs * e_q, k * e_k, k * e_d, jnp.exp(bl), eb, e_q, e_k, e_d


def _hgrn_fwd(proj, alb, ngain):
    S = proj.shape[1]
    nc = S // HG_CH
    wb = HG_HPB * HG_DK

    def body(pj_ref, alb_ref, ng_ref, o_ref, y_ref, st_ref, st_scr, b_scr):
        n = pl.program_id(1)

        @pl.when(n == 0)
        def _():
            st_scr[...] = jnp.zeros_like(st_scr)

        r = lax.broadcasted_iota(jnp.int32, (HG_CH, HG_CH), 0)
        c = lax.broadcasted_iota(jnp.int32, (HG_CH, HG_CH), 1)
        causal = r >= c
        for j in range(HG_HPB):
            lanes = pl.ds(j * HG_DK, HG_DK)
            q, f, v, g = pj_ref[0, :, lanes], pj_ref[1, :, lanes], pj_ref[2, :, lanes], pj_ref[3, :, lanes]
            qs, k, logf, _, _, _ = _hg_gates(q, f, alb_ref.at[:, lanes])
            b = _running_sum(logf)
            qb, qt, kt, kd, ebl, _, _, _, _ = _hg_intra(qs, k, b, b_scr.at[j])
            st = st_scr[j]
            st_ref[j] = st
            a = jnp.where(causal, _dot_nt(qt, kt), 0.0)
            o = _dot(a, v) + _dot_nt(qb, st)
            st_scr[j] = st * ebl + _dot_tn(v, kd)
            o_ref[:, lanes] = o
            rinv = lax.rsqrt(jnp.mean(o * o, axis=-1, keepdims=True) + RMS_EPS)
            y_ref[:, lanes] = (o * rinv * ng_ref[...] * (g * _sigmoid(g))).astype(CDT)

    blk = pl.BlockSpec((HG_CH, wb), lambda h, n: (n, h))
    return pl.pallas_call(
        body, grid=(HG_H // HG_HPB, nc),
        in_specs=[pl.BlockSpec((4, HG_CH, wb), lambda h, n: (0, n, h)), pl.BlockSpec((2, wb), lambda h, n: (0, h)),
                  pl.BlockSpec((1, HG_DK), lambda h, n: (0, 0))],
        out_specs=[blk, blk, pl.BlockSpec((HG_HPB, None, HG_DK, HG_DK), lambda h, n: (h, n, 0, 0))],
        out_shape=[_sds((S, D), F32), _sds((S, D), CDT), _sds((HG_H, nc, HG_DK, HG_DK), F32)],
        scratch_shapes=[pltpu.VMEM((HG_HPB, HG_DK, HG_DK), F32), pltpu.VMEM((HG_HPB, HG_CH, HG_DK), F32)],
        compiler_params=_params(("arbitrary", "arbitrary")), name="hgrn_fwd")(proj, alb, ngain)


def _hgrn_bwd(proj, alb, ngain, o, states, dy):
    S = proj.shape[1]
    nc = S // HG_CH
    wb = HG_HPB * HG_DK

    def body(pj_ref, alb_ref, ng_ref, o_ref, st_ref, dy_ref, dpj_ref, dalb_ref, dng_ref, dst_scr, b_scr):
        h, n = pl.program_id(0), pl.program_id(1)

        @pl.when(n == 0)
        def _():
            dst_scr[...] = jnp.zeros_like(dst_scr)

        ng = ng_ref[...]
        r = lax.broadcasted_iota(jnp.int32, (HG_CH, HG_CH), 0)
        c = lax.broadcasted_iota(jnp.int32, (HG_CH, HG_CH), 1)
        causal = r >= c
        dng = None
        for j in range(HG_HPB):
            lanes = pl.ds(j * HG_DK, HG_DK)
            q, f, v, g = pj_ref[0, :, lanes], pj_ref[1, :, lanes], pj_ref[2, :, lanes], pj_ref[3, :, lanes]
            o_ = o_ref[:, lanes]
            dy_ = dy_ref[:, lanes]
            sg = _sigmoid(g)
            rinv = lax.rsqrt(jnp.mean(o_ * o_, axis=-1, keepdims=True) + RMS_EPS)
            nrm = o_ * rinv
            dr = dy_ * (g * sg)
            dg = dy_ * nrm * ng * (sg * (1.0 + g * (1.0 - sg)))
            dn = dr * ng
            do = rinv * (dn - nrm * jnp.mean(dn * nrm, axis=-1, keepdims=True))
            dng = _colsum(dr * nrm) if dng is None else dng + _colsum(dr * nrm)
            qs, k, logf, sig, lb, forget = _hg_gates(q, f, alb_ref.at[:, lanes])
            b = _running_sum(logf)
            qb, qt, kt, kd, ebl, eb, e_q, e_k, e_d = _hg_intra(qs, k, b, b_scr.at[j])
            st = st_ref[j]
            dstn = dst_scr[j]
            qt, kt, qb, kd = (t.astype(CDT).astype(F32) for t in (qt, kt, qb, kd))
            a = jnp.where(causal, _dot_nt(qt, kt), 0.0)
            da = jnp.where(causal, _dot_nt(do, v), 0.0)
            dv = _dot_tn(a, do) + _dot_nt(kd, dstn)
            dqb = _dot(do, st)
            dkd = _dot(v, dstn)
            dqt = _dot(da, kt)
            dkt = _dot_tn(da, qt)
            dbl = _colsum(dkd * kd) + ebl * _colsum(dstn * st)
            dst_scr[j] = dstn * ebl + _dot_tn(do, qb)
            dqs = dqt * e_q + dqb * eb
            dk = dkt * e_k + dkd * e_d
            db = dqt * qt + dqb * qb - dkt * kt - dkd * kd
            dlogf = _running_sum(db, reverse=True) + dbl
            dforget = dlogf / forget
            dsig = (1.0 - lb) * (dforget - dk)
            df = dsig * sig * (1.0 - sig)
            dlb = _colsum((dforget - dk) * (1.0 - sig))
            sq = _sigmoid(q)
            dq = dqs * (HG_DK ** -0.5) * (sq * (1.0 + q * (1.0 - sq)))
            dpj_ref[0, :, lanes] = dq.astype(CDT)
            dpj_ref[1, :, lanes] = df.astype(CDT)
            dpj_ref[2, :, lanes] = dv.astype(CDT)
            dpj_ref[3, :, lanes] = dg.astype(CDT)
            da0 = dlb * lb * (1.0 - lb)
            _acc(dalb_ref.at[pl.ds(0, 1), lanes], da0, n == 0)
            _acc(dalb_ref.at[pl.ds(1, 1), lanes], -da0, n == 0)
        _acc(dng_ref, dng, jnp.logical_and(h == 0, n == 0))

    blk = pl.BlockSpec((HG_CH, wb), lambda h, n: (nc - 1 - n, h))
    pj = pl.BlockSpec((4, HG_CH, wb), lambda h, n: (0, nc - 1 - n, h))
    alb_blk = pl.BlockSpec((2, wb), lambda h, n: (0, h))
    ng_blk = pl.BlockSpec((1, HG_DK), lambda h, n: (0, 0))
    return pl.pallas_call(
        body, grid=(HG_H // HG_HPB, nc),
        in_specs=[pj, alb_blk, ng_blk, blk,
                  pl.BlockSpec((HG_HPB, None, HG_DK, HG_DK), lambda h, n: (h, nc - 1 - n, 0, 0)), blk],
        out_specs=[pj, alb_blk, ng_blk],
        out_shape=[_sds((4, S, D), CDT), _sds((2, D), F32), _sds((1, HG_DK), F32)],
        scratch_shapes=[pltpu.VMEM((HG_HPB, HG_DK, HG_DK), F32), pltpu.VMEM((HG_HPB, HG_CH, HG_DK), F32)],
        compiler_params=_params(("arbitrary", "arbitrary")), name="hgrn_bwd")(proj, alb, ngain, o, states, dy)


def _slope(h):
    return 2.0 ** (-8.0 * (h + 1) / ATT_QH)


def _attn_mask(n):
    qi = lax.broadcasted_iota(jnp.int32, (WINDOW, 2 * WINDOW), 0)
    si = lax.broadcasted_iota(jnp.int32, (WINDOW, 2 * WINDOW), 1)
    dist = qi - si + WINDOW
    valid = (dist >= 0) & (dist < WINDOW) & (n * WINDOW - WINDOW + si >= 0)
    return valid, dist.astype(F32)


def _attn_probs(qh, kh, sink, slope, valid, distf):
    s = _dot_nt(qh, kh) * (ATT_HD ** -0.5) - slope * distf
    s = jnp.where(valid, s, NEG)
    m = jnp.maximum(jnp.max(s, axis=-1, keepdims=True), sink)
    e = jnp.exp(s - m)
    es = jnp.exp(sink - m)
    inv = 1.0 / (jnp.sum(e, axis=-1, keepdims=True) + es)
    return e * inv, es * inv


def _attn_specs(S):
    nb = S // WINDOW
    cur = lambda H: pl.BlockSpec((H, WINDOW, ATT_HD), lambda n: (0, n, 0))
    prev = lambda H: pl.BlockSpec((H, WINDOW, ATT_HD), lambda n: (0, jnp.maximum(n - 1, 0), 0))
    return nb, cur, prev


def _attn_fwd(q4, kv4, sinks):
    S = q4.shape[1]
    nb, cur, prev = _attn_specs(S)

    def body(sink_ref, q_ref, kvc_ref, kvp_ref, o_ref):
        valid, distf = _attn_mask(pl.program_id(0))
        outs = []
        for h in range(ATT_QH):
            kvh = h // ATT_G
            kh = jnp.concatenate([kvp_ref[kvh], kvc_ref[kvh]], axis=0)
            vh = jnp.concatenate([kvp_ref[ATT_KVH + kvh], kvc_ref[ATT_KVH + kvh]], axis=0)
            p, _ = _attn_probs(q_ref[h], kh, sink_ref[0, h], _slope(h), valid, distf)
            outs.append(_dot(p, vh).astype(CDT))
            if h % 2:
                o_ref[:, pl.ds((h - 1) * ATT_HD, 2 * ATT_HD)] = _pair_lanes(outs[h - 1], outs[h])

    return pl.pallas_call(
        body, grid=(nb,),
        in_specs=[pl.BlockSpec(memory_space=pltpu.SMEM), cur(ATT_QH), cur(2 * ATT_KVH), prev(2 * ATT_KVH)],
        out_specs=pl.BlockSpec((WINDOW, D), lambda n: (n, 0)), out_shape=_sds((S, D), CDT),
        compiler_params=_params(("arbitrary",)), name="attn_fwd")(sinks, q4, kv4, kv4)


def _attn_bwd(q4, kv4, sinks, do):
    S = q4.shape[1]
    nb, cur, prev = _attn_specs(S)

    def body(sink_ref, q_ref, kvc_ref, kvp_ref, do_ref, dq_ref, dkv_ref, dbq_ref, dsink_ref):
        n = pl.program_id(0)
        first = n == 0

        @pl.when(first)
        def _():
            dkv_ref[...] = jnp.zeros_like(dkv_ref)
            dsink_ref[...] = jnp.zeros_like(dsink_ref)

        valid, distf = _attn_mask(n)
        lane = lax.broadcasted_iota(jnp.int32, (1, 128), 1)
        rows_cur = pl.ds(pl.multiple_of(n * WINDOW, WINDOW), WINDOW)
        rows_prev = pl.ds(pl.multiple_of(jnp.maximum(n - 1, 0) * WINDOW, WINDOW), WINDOW)
        dsinks = jnp.zeros((1, 128), F32)
        sel = (_half_select(True), _half_select(False))
        for kvh in range(ATT_KVH):
            kh = jnp.concatenate([kvp_ref[kvh], kvc_ref[kvh]], axis=0)
            vh = jnp.concatenate([kvp_ref[ATT_KVH + kvh], kvc_ref[ATT_KVH + kvh]], axis=0)
            dk = dv = None
            dqs = []
            for h in range(kvh * ATT_G, (kvh + 1) * ATT_G):
                qh = q_ref[h]
                doh = jnp.dot(do_ref[:, pl.ds((h // 2) * 2 * ATT_HD, 2 * ATT_HD)], sel[h % 2],
                              preferred_element_type=F32).astype(CDT)
                p, ps = _attn_probs(qh, kh, sink_ref[0, h], _slope(h), valid, distf)
                dp = _dot_nt(doh, vh)
                dd = jnp.sum(p * dp, axis=-1, keepdims=True)
                ds = p * (dp - dd)
                dsinks = dsinks + jnp.where(lane == h, -jnp.sum(ps * dd, axis=0, keepdims=True), 0.0)
                dqh = _dot(ds, kh) * (ATT_HD ** -0.5)
                dqs.append(dqh.astype(CDT))
                _acc(dbq_ref.at[h], _colsum(dqh), first)
                dkh = _dot_tn(ds, qh) * (ATT_HD ** -0.5)
                dvh = _dot_tn(p, doh)
                dk = dkh if dk is None else dk + dkh
                dv = dvh if dv is None else dv + dvh
            for i in range(ATT_G // 2):
                lanes = pl.ds((kvh * ATT_G + 2 * i) * ATT_HD, 2 * ATT_HD)
                dq_ref[:, lanes] = _pair_lanes(dqs[2 * i], dqs[2 * i + 1])
            dkv_ref[kvh, rows_prev, :] += dk[:WINDOW]
            dkv_ref[kvh, rows_cur, :] += dk[WINDOW:]
            dkv_ref[ATT_KVH + kvh, rows_prev, :] += dv[:WINDOW]
            dkv_ref[ATT_KVH + kvh, rows_cur, :] += dv[WINDOW:]
        dsink_ref[...] += dsinks

    return pl.pallas_call(
        body, grid=(nb,),
        in_specs=[pl.BlockSpec(memory_space=pltpu.SMEM), cur(ATT_QH), cur(2 * ATT_KVH), prev(2 * ATT_KVH),
                  pl.BlockSpec((WINDOW, D), lambda n: (n, 0))],
        out_specs=[pl.BlockSpec((WINDOW, D), lambda n: (n, 0)), pl.BlockSpec((2 * ATT_KVH, S, ATT_HD), lambda n: (0, 0, 0)),
                   pl.BlockSpec((ATT_QH, 1, ATT_HD), lambda n: (0, 0, 0)), pl.BlockSpec((1, 128), lambda n: (0, 0))],
        out_shape=[_sds((S, D), CDT), _sds((2 * ATT_KVH, S, ATT_HD), F32), _sds((ATT_QH, 1, ATT_HD), F32),
                   _sds((1, 128), F32)],
        compiler_params=_params(("arbitrary",)), name="attn_bwd")(sinks, q4, kv4, kv4, do)


def _local_step(x, p, target, getw, sm, emit):
    S = x.shape[0]
    vec = lambda a: a.reshape(1, -1)
    ln_g = lambda l, k: vec(sm["ln_gain"][l, k])
    ln_b = lambda l, k: vec(sm["ln_bias"][l, k])
    xb = x.astype(CDT)
    pb = p.astype(CDT)

    proj = _mm_nn(xb, getw("a_w_in", None), (4, S, D), (None, _tile(S), 512), lambda g, i: (g // 2, i, g % 2), F32,
                  name="a_in")
    o_a, y_a, states = _hgrn_fwd(proj, sm["a_lower_bound"], sm["a_norm_gain"])
    zeros = jnp.zeros((1, D), F32)
    z = [[None] * 3 for _ in range(2)]
    xs = [[None] * 3 for _ in range(2)]
    xbs = [[None] * 3 for _ in range(2)]
    z[0][0], xs[0][0], xbs[0][0] = _mixout_ln(y_a[None], getw("a_w_out", y_a)[None], zeros, x, ln_g(0, 0), ln_b(0, 0),
                                              "a_out_ln")
    gu, hid, sgs, ups = [None, None], [None, None], [None, None], [None, None]

    def ffn_ple(l):
        wgu = getw(f"gu{l}", xbs[l][0])
        gu[l], hid[l], z[l][1], xs[l][1], xbs[l][1] = _ffn_fwd(
            xs[l][0], xbs[l][0], wgu, getw(f"dn{l}", None), ln_g(l, 1), ln_b(l, 1), f"ffn_fwd{l}")
        sgs[l], ups[l], z[l][2], xs[l][2], xbs[l][2] = _ple_fwd(
            xs[l][1], xbs[l][1], pb[l], getw(f"pg{l}", None), vec(sm["ple_b_gate"][l]), getw(f"pu{l}", None), ln_g(l, 2),
            ln_b(l, 2), f"ple_fwd{l}")

    ffn_ple(0)
    x3, x3b = xs[0][2], xbs[0][2]
    w_kv, w_q, w_bo = getw("kv_w", x3b), getw("b_w_q", None), getw("b_w_out", None)
    kv4 = _proj_heads(x3b, w_kv, vec(sm["kv_b"]), 2 * ATT_KVH, "kv_proj")
    q4 = _proj_heads(x3b, w_q, vec(sm["b_b_q"]), ATT_QH, "q_proj")
    o_b = _attn_fwd(q4, kv4, sm["b_sinks"])
    z[1][0], xs[1][0], xbs[1][0] = _mixout_ln(o_b[None], w_bo[None], sm["b_b_out"], x3, ln_g(1, 0), ln_b(1, 0),
                                              "b_out_ln")
    ffn_ple(1)
    loss, dy = _loss_fwd_bwd(xs[1][2], target)

    gs = {}
    d_ln_g = [[None] * 3 for _ in range(2)]
    d_ln_b = [[None] * 3 for _ in range(2)]
    g_bg = [None, None]

    def ffn_ple_bwd(l, dy, after=None):
        dx2, dgl, dup, d_ln_g[l][2], d_ln_b[l][2], g_bg[l] = _ple_bwd(dy, z[l][2], sgs[l], ups[l], ln_g(l, 2),
                                                                     getw(f"pg{l}", None), f"ple_bwd{l}", after=after)
        g_pg = _wgrad(xbs[l][1][None], dgl[None], f"g_ple_gate{l}")[0]
        g_pu = _wgrad(pb[l][None], dup[None], f"g_ple_up{l}")[0]
        dx1, dzb, dgu, d_ln_g[l][1], d_ln_b[l][1] = _ffn_bwd(dx2, z[l][1], gu[l], getw(f"gu{l}", None),
                                                           getw(f"dn{l}", None), ln_g(l, 1), f"ffn_bwd{l}")
        g_dn = _wgrad(hid[l], dzb[None], f"g_ffn_down{l}")
        g_gu = _wgrad(xbs[l][0][None], dgu.reshape(8, S, FFN_B), f"g_ffn_gate_up{l}")
        return dx1, emit({f"pg{l}": g_pg, f"pu{l}": g_pu, f"dn{l}": g_dn, f"gu{l}": g_gu})

    dx1, tok = ffn_ple_bwd(1, dy)
    dz, dzb, do, d_ln_g[1][0], d_ln_b[1][0], gs["b_b_out"] = _mixout_bwd(dx1, z[1][0], ln_g(1, 0), w_bo[None], CDT,
                                                                        "b_out_bwd", after=tok)
    g_bo = _wgrad(o_b[None], dzb[None], "g_b_w_out")[0]
    dq, dkv4, dbq, dsinks = _attn_bwd(q4, kv4, sm["b_sinks"], do[0])
    gs["b_b_q"] = dbq
    gs["b_sinks"] = dsinks
    g_q = _wgrad(x3b[None], dq[None], "g_b_w_q")[0]
    dx3, dkv, gs["kv_b"] = _qkv_bwd(dz, dq, dkv4, w_q, w_kv, "qkv_bwd", after=tok)
    g_kv = _wgrad(x3b[None], dkv[None], "g_kv_w")[0]
    tok = emit({"b_w_out": g_bo, "b_w_q": g_q, "kv_w": g_kv})
    dx1, tok = ffn_ple_bwd(0, dx3, tok)
    w_ao = getw("a_w_out", None)
    dz, dzb, dyr, d_ln_g[0][0], d_ln_b[0][0], _ = _mixout_bwd(dx1, z[0][0], ln_g(0, 0), w_ao[None], F32, "a_out_bwd",
                                                              after=tok)
    g_ao = _wgrad(y_a[None], dzb[None], "g_a_w_out")[0]
    dproj, gs["a_lower_bound"], gs["a_norm_gain"] = _hgrn_bwd(proj, sm["a_lower_bound"], sm["a_norm_gain"], o_a, states,
                                                              dyr[0])
    tk = lambda t: (None, t, D)
    g_ain = _mm_tn(xb[None], dproj, N_DEV, lambda g, k: (0, k, 0), lambda g, k: (g // 2, k, g % 2),
                   tk, lambda t: (None, t, 512), (N_DEV, D, 512), (None, D, 512), lambda g, k: (g, 0, 0), name="g_a_w_in")
    tok = emit({"a_w_out": g_ao, "a_w_in": g_ain})
    grad_x = _inproj_bwd(dz, dproj, getw("a_w_in", None), "a_in_bwd", after=tok)
    gs["ple_b_gate"] = jnp.concatenate(g_bg, axis=0)
    gs["ln_gain"] = jnp.stack([jnp.concatenate(r, axis=0) for r in d_ln_g])
    gs["ln_bias"] = jnp.stack([jnp.concatenate(r, axis=0) for r in d_ln_b])
    return loss, grad_x, gs


def _peer(k):
    x, y, c = lax.axis_index("x"), lax.axis_index("y"), lax.axis_index("c")
    px = 1 - x if k & 4 else x
    py = 1 - y if k & 2 else y
    pc = 1 - c if k & 1 else c
    return (px, py, pc), 4 * px + 2 * py + pc


def _my_index():
    return 4 * lax.axis_index("x") + 2 * lax.axis_index("y") + lax.axis_index("c")


def _exchange(srcs, dst_shapes, plan, name):
    n_src, n_piece = len(srcs), len(plan)

    def body(*refs):
        src_refs, dst_refs = refs[:n_src], refs[n_src:n_src + len(dst_shapes)]
        send_sems, recv_sems, local_sems = refs[n_src + len(dst_shapes):]
        me = _my_index()

        def at(ref, idx):
            return ref.at[idx] if idx else ref

        local = []
        for t, (si, sfn, di, dfn) in enumerate(plan):
            cp = pltpu.make_async_copy(at(src_refs[si], sfn(me)), at(dst_refs[di], dfn(me)), local_sems.at[t])
            cp.start()
            local.append(cp)
        sends = []
        for k in range(1, N_DEV):
            peer, pid = _peer(k)
            for t, (si, sfn, di, dfn) in enumerate(plan):
                cp = pltpu.make_async_remote_copy(
                    src_ref=at(src_refs[si], sfn(pid)), dst_ref=at(dst_refs[di], dfn(me)),
                    send_sem=send_sems.at[t * 7 + k - 1], recv_sem=recv_sems.at[t * 7 + k - 1],
                    device_id=peer, device_id_type=MESH)
                cp.start()
                sends.append(cp)
        for k in range(1, N_DEV):
            peer, pid = _peer(k)
            for t, (si, sfn, di, dfn) in enumerate(plan):
                pltpu.make_async_remote_copy(
                    src_ref=at(src_refs[si], sfn(me)), dst_ref=at(dst_refs[di], dfn(pid)),
                    send_sem=send_sems.at[t * 7 + k - 1], recv_sem=recv_sems.at[t * 7 + k - 1],
                    device_id=peer, device_id_type=MESH).wait_recv()
        for cp in sends:
            cp.wait_send()
        for cp in local:
            cp.wait()

    hbm = pl.BlockSpec(memory_space=pltpu.HBM)
    return pl.pallas_call(
        body, in_specs=[hbm] * n_src, out_specs=[hbm] * len(dst_shapes), out_shape=dst_shapes,
        scratch_shapes=[pltpu.SemaphoreType.DMA((7 * n_piece,)), pltpu.SemaphoreType.DMA((7 * n_piece,)),
                        pltpu.SemaphoreType.DMA((n_piece,))],
        name=name)(*srcs)


def _gather(shards, name):
    dsts = [_sds((N_DEV,) + a.shape, a.dtype) for a in shards]
    plan = [(i, lambda j: (), i, lambda s: (s,)) for i in range(len(shards))]
    return _exchange(shards, dsts, plan, name)


_HBM = pl.BlockSpec(memory_space=pltpu.HBM)
_SEM = pl.BlockSpec(memory_space=pltpu.SEMAPHORE)
_DATAFLOW = pltpu.SideEffectType.DATAFLOW_SIDE_EFFECTING


def _piece_copy(mode, src, land, send_sems, recv_sems, t, k, sender, receiver, peer):
    return pltpu.make_async_remote_copy(
        src_ref=src if mode == "gather" else src.at[receiver], dst_ref=land.at[sender],
        send_sem=send_sems.at[t * 7 + k - 1], recv_sem=recv_sems.at[t * 7 + k - 1], device_id=peer, device_id_type=MESH)


def _sequencer_exchange(srcs, mode, name, collective_id, after=None):
    n = len(srcs)
    land_shapes = [((N_DEV,) + a.shape) if mode == "gather" else a.shape for a in srcs]
    extra = [] if after is None else [after]

    def body(*refs):
        src_refs, land_refs = refs[:n], refs[n + len(extra):2 * n + len(extra)]
        send_sems, recv_sems, local_sems = refs[2 * n + len(extra):]
        barrier = pltpu.get_barrier_semaphore()
        for k in range(1, N_DEV):
            pl.semaphore_signal(barrier, inc=1, device_id=_peer(k)[0], device_id_type=MESH)
        pl.semaphore_wait(barrier, N_DEV - 1)
        me = _my_index()
        local = []
        for i in range(n):
            cp = pltpu.make_async_copy(src_refs[i] if mode == "gather" else src_refs[i].at[me], land_refs[i].at[me],
                                       local_sems.at[i])
            cp.start()
            local.append(cp)
        for k in range(1, N_DEV):
            peer, pid = _peer(k)
            for t in range(n):
                _piece_copy(mode, src_refs[t], land_refs[t], send_sems, recv_sems, t, k, me, pid, peer).start()
        for k in range(1, N_DEV):
            peer, pid = _peer(k)
            for t in range(n):
                _piece_copy(mode, src_refs[t], land_refs[t], send_sems, recv_sems, t, k, pid, me, peer).wait_recv()
        for k in range(1, N_DEV):
            peer, pid = _peer(k)
            for t in range(n):
                _piece_copy(mode, src_refs[t], land_refs[t], send_sems, recv_sems, t, k, me, pid, peer).wait_send()
        for cp in local:
            cp.wait()

    return pl.kernel(
        body, out_type=[_sds(s, a.dtype) for s, a in zip(land_shapes, srcs)],
        mesh=plsc.ScalarSubcoreMesh(axis_name="sequencer", num_cores=1),
        scratch_types=[pltpu.SemaphoreType.DMA((7 * n,)), pltpu.SemaphoreType.DMA((7 * n,)), pltpu.SemaphoreType.DMA((n,))],
        compiler_params=pltpu.CompilerParams(collective_id=collective_id), name=name)(*srcs, *extra)


def _sequencer_gather(srcs, name, collective_id, after=None):
    n = len(srcs)
    extra = [] if after is None else [after]

    def body(*refs):
        src_refs, land_refs = refs[:n], refs[n + len(extra):2 * n + len(extra)]
        send_sems, recv_sems, local_sems = refs[2 * n + len(extra):]
        x, y, c = lax.axis_index("x"), lax.axis_index("y"), lax.axis_index("c")
        sibling = (x, y, 1 - c)
        chips = [(1 - x, y), (x, 1 - y), (1 - x, 1 - y)]
        index = lambda px, py, pc: 4 * px + 2 * py + pc
        barrier = pltpu.get_barrier_semaphore()
        for peer in [sibling] + [(*chip, c) for chip in chips]:
            pl.semaphore_signal(barrier, inc=1, device_id=peer, device_id_type=MESH)
        pl.semaphore_wait(barrier, 4)

        def copy(t, k, slot, to, src=None):
            return pltpu.make_async_remote_copy(
                src_ref=land_refs[t].at[slot] if src is None else src, dst_ref=land_refs[t].at[slot],
                send_sem=send_sems.at[7 * t + k], recv_sem=recv_sems.at[7 * t + k], device_id=to, device_id_type=MESH)

        me = index(x, y, c)
        local = []
        for t in range(n):
            cp = pltpu.make_async_copy(src_refs[t], land_refs[t].at[me], local_sems.at[t])
            cp.start()
            local.append(cp)
        sends = []
        for t in range(n):
            sends.append(copy(t, 0, me, sibling, src=src_refs[t]))
            sends += [copy(t, 1 + j, me, (*chip, c), src=src_refs[t]) for j, chip in enumerate(chips)]
        for cp in sends:
            cp.start()
        for j, chip in enumerate(chips):
            for t in range(n):
                copy(t, 1 + j, index(*chip, c), sibling, src=src_refs[t]).wait_recv()
                passed = copy(t, 4 + j, index(*chip, c), sibling)
                passed.start()
                sends.append(passed)
        for t in range(n):
            copy(t, 0, index(x, y, 1 - c), sibling, src=src_refs[t]).wait_recv()
        for j, chip in enumerate(chips):
            for t in range(n):
                copy(t, 4 + j, index(*chip, 1 - c), sibling, src=src_refs[t]).wait_recv()
        for cp in sends:
            cp.wait_send()
        for cp in local:
            cp.wait()

    return pl.kernel(
        body, out_type=[_sds((N_DEV,) + a.shape, a.dtype) for a in srcs],
        mesh=plsc.ScalarSubcoreMesh(axis_name="sequencer", num_cores=1),
        scratch_types=[pltpu.SemaphoreType.DMA((7 * n,)), pltpu.SemaphoreType.DMA((7 * n,)), pltpu.SemaphoreType.DMA((n,))],
        compiler_params=pltpu.CompilerParams(collective_id=collective_id), name=name)(*srcs, *extra)


def _xstart(groups, mode, name, after=None):
    flat = [a for g in groups for a in g]
    n, ng = len(flat), len(groups)
    land_shapes = [((N_DEV,) + a.shape) if mode == "gather" else a.shape for a in flat]
    first = [sum(len(g) for g in groups[:i]) for i in range(ng)]
    extra = [] if after is None else [after]

    def body(*refs):
        srcs, lands = refs[:n], refs[n:2 * n]
        sems = refs[2 * n + len(extra):2 * n + len(extra) + 2 * ng]
        tok_ref, local_sems = refs[-2], refs[-1]
        me = _my_index()
        local = []
        for i in range(n):
            cp = pltpu.make_async_copy(srcs[i] if mode == "gather" else srcs[i].at[me], lands[i].at[me], local_sems.at[i])
            cp.start()
            local.append(cp)
        for cp in local:
            cp.wait()
        for gi, g in enumerate(groups):
            for k in range(1, N_DEV):
                peer, pid = _peer(k)
                for t in range(len(g)):
                    i = first[gi] + t
                    _piece_copy(mode, srcs[i], lands[i], sems[2 * gi], sems[2 * gi + 1], t, k, me, pid, peer).start()
        tok_ref[...] = jnp.zeros_like(tok_ref)

    sem_shapes = []
    for g in groups:
        sem_shapes += [pltpu.SemaphoreType.DMA((7 * len(g),))] * 2
    thru = [pltpu.HBM(a.shape, a.dtype) for a in flat] + [pltpu.HBM(s, a.dtype) for s, a in zip(land_shapes, flat)]
    outs = pl.pallas_call(
        body, in_specs=[_HBM] * (2 * n) + [pl.BlockSpec(memory_space=pl.ANY)] * len(extra),
        out_specs=[_SEM] * (2 * ng) + [_HBM] * (2 * n) + [pl.BlockSpec(memory_space=pltpu.VMEM)],
        out_shape=sem_shapes + thru + [_sds((8, 128), F32)],
        input_output_aliases={i: 2 * ng + i for i in range(2 * n)},
        scratch_shapes=[pltpu.SemaphoreType.DMA((n,))],
        compiler_params=pltpu.CompilerParams(has_side_effects=_DATAFLOW), name=name)(
            *[pltpu.with_memory_space_constraint(a, pltpu.HBM) for a in flat],
            *[pltpu.with_memory_space_constraint(lax.empty(s, a.dtype), pltpu.HBM) for s, a in zip(land_shapes, flat)], *extra)
    sems, srcs_thru, lands_thru = outs[:2 * ng], outs[2 * ng:2 * ng + n], outs[2 * ng + n:2 * ng + 2 * n]
    handles = [(sems[2 * gi], sems[2 * gi + 1], srcs_thru[first[gi]:first[gi] + len(g)],
                lands_thru[first[gi]:first[gi] + len(g)]) for gi, g in enumerate(groups)]
    return handles, outs[-1]


def _xwait(handle, mode, after, name):
    send_sems, recv_sems, srcs_thru, lands_thru = handle
    n = len(srcs_thru)

    def body(*refs):
        srcs, lands, send, recv = refs[:n], refs[n:2 * n], refs[2 * n], refs[2 * n + 1]
        me = _my_index()
        for k in range(1, N_DEV):
            peer, pid = _peer(k)
            for t in range(n):
                _piece_copy(mode, srcs[t], lands[t], send, recv, t, k, pid, me, peer).wait_recv()
        for k in range(1, N_DEV):
            peer, pid = _peer(k)
            for t in range(n):
                _piece_copy(mode, srcs[t], lands[t], send, recv, t, k, me, pid, peer).wait_send()

    extra = [] if after is None else [after]
    outs = pl.pallas_call(
        body, in_specs=[_HBM] * (2 * n) + [_SEM, _SEM] + [pl.BlockSpec(memory_space=pl.ANY)] * len(extra),
        out_specs=[_HBM] * (2 * n),
        out_shape=[pltpu.HBM(a.shape, a.dtype) for a in list(srcs_thru) + list(lands_thru)],
        input_output_aliases={i: i for i in range(2 * n)},
        compiler_params=pltpu.CompilerParams(has_side_effects=_DATAFLOW), name=name)(
            *srcs_thru, *lands_thru, send_sems, recv_sems, *extra)
    return outs[n:]


def _adamw(w, g, m, v):
    m = ADAM_B1 * m + (1.0 - ADAM_B1) * g
    v = ADAM_B2 * v + (1.0 - ADAM_B2) * (g * g)
    m_hat = m / (1.0 - ADAM_B1 ** ADAM_STEP)
    v_hat = v / (1.0 - ADAM_B2 ** ADAM_STEP)
    delta = -ADAM_LR * (m_hat / (jnp.sqrt(v_hat) + ADAM_EPS) + ADAM_WD * w)
    return delta, m, v


def _adam_big(w, parts, m, v, name, after=None):
    L, R, C = w.shape
    tr = _tile(R, (256, 128, 176, 64, 32, 16))
    nr = R // tr

    def body(w_ref, *refs):
        p_refs, (m_ref, v_ref, g_ref, d_ref, mo_ref, vo_ref) = refs[:L], refs[L:]
        for l in range(L):
            @pl.when(pl.program_id(0) == l)
            def _(p_ref=p_refs[l]):
                g = p_ref[0].astype(F32)
                for s in range(1, N_DEV):
                    g = g + p_ref[s].astype(F32)
                g_ref[...] = g
                d_ref[...], mo_ref[...], vo_ref[...] = _adamw(w_ref[...], g, m_ref[...], v_ref[...])

    row = pl.BlockSpec((None, tr, C), lambda l, i: (l, i, 0))
    park = lambda l_of: (lambda l, i: (0, jnp.where(l == l_of, i, 0 if l_of else nr - 1), 0))
    return _call(
        after, body, grid=(L, nr),
        in_specs=[row] + [pl.BlockSpec((N_DEV, tr, C), park(l)) for l in range(L)] + [row, row],
        out_specs=[row] * 4, out_shape=[_sds((L, R, C), F32)] * 4,
        compiler_params=_params(("arbitrary", "arbitrary")), name=name)(w, *parts, m, v)


SMALL = (("a_lower_bound", 2, True), ("ln_gain", 6, True), ("ln_bias", 6, True), ("a_norm_gain", 1, False),
         ("kv_b", 4, False), ("b_b_q", 8, False), ("b_sinks", 1, False), ("b_b_out", 8, False), ("ple_b_gate", 16, False))
SUBLANES = 8


def _slot(r):
    return -(-r // SUBLANES) * SUBLANES


SMALL_ROWS = sum(_slot(r) for _, r, _ in SMALL)
PART_ROWS = sum(r * N_DEV if sh else _slot(r) for _, r, sh in SMALL)


def _pack_rows(a, rows):
    flat = a.reshape(-1)
    return jnp.pad(flat, (0, rows * 128 - flat.size)).reshape(rows, 128)


def _pack_small(d):
    return jnp.concatenate([_pack_rows(d[n], _slot(r)) for n, r, _ in SMALL], axis=0)


def _unpack_small(packed, like):
    out, r0 = {}, 0
    for n, r, _ in SMALL:
        out[n] = packed[r0:r0 + r].reshape(-1)[:like[n].size].reshape(like[n].shape)
        r0 += _slot(r)
    return out


def _pack_partials(gs):
    blocks = []
    for n, r, sharded in SMALL:
        if sharded:
            blocks.append(gs[n].reshape(r * N_DEV, 128))
        else:
            blocks.append(_pack_rows(gs[n], _slot(r)))
    return jnp.concatenate(blocks, axis=0)


def _adam_small(parts, w, m, v, after=None):
    def body(p_ref, w_ref, m_ref, v_ref, g_ref, d_ref, mo_ref, vo_ref, tot):
        me = _my_index()
        t = p_ref[0]
        for s in range(1, N_DEV):
            t = t + p_ref[s]
        tot[...] = t
        g_ref[...] = jnp.zeros_like(g_ref)
        src, dst = 0, 0
        for _, r, sharded in SMALL:
            if sharded:
                for i in range(r):
                    g_ref[pl.ds(dst + i, 1), :] = tot[pl.ds(src + i * N_DEV + me, 1), :]
                src += r * N_DEV
            else:
                g_ref[pl.ds(dst, _slot(r)), :] = tot[pl.ds(src, _slot(r)), :]
                src += _slot(r)
            dst += _slot(r)
        d_ref[...], mo_ref[...], vo_ref[...] = _adamw(w_ref[...], g_ref[...], m_ref[...], v_ref[...])

    full = pl.BlockSpec((SMALL_ROWS, 128), lambda: (0, 0))
    return _call(
        after, body, in_specs=[pl.BlockSpec((N_DEV, PART_ROWS, 128), lambda: (0, 0, 0)), full, full, full],
        out_specs=[full] * 4, out_shape=[_sds((SMALL_ROWS, 128), F32)] * 4,
        scratch_shapes=[pltpu.VMEM((PART_ROWS, 128), F32)], name="adam_small")(parts, w, m, v)


WEIGHTS = ("a_w_in", "a_lower_bound", "a_norm_gain", "a_w_out", "kv_w", "kv_b", "b_w_q", "b_b_q", "b_sinks", "b_w_out",
           "b_b_out", "ffn_w_gate_up", "ffn_w_down", "ple_w_up", "ple_w_gate", "ple_b_gate", "ln_gain", "ln_bias")


GATHER_GROUPS = (("a_w_in",), ("a_w_out", "gu0", "dn0", "pu0", "pg0"), ("kv_w", "b_w_q", "b_w_out"),
                 ("gu1", "dn1", "pu1", "pg1"))
KERNEL_LAYOUT = {
    "a_w_in": lambda a: a,
    "a_w_out": lambda a: a.reshape(D, D),
    "kv_w": lambda a: a.reshape(D, 2 * ATT_KVH * ATT_HD),
    "b_w_q": lambda a: a.reshape(D, D),
    "b_w_out": lambda a: a.reshape(D, D),
    "gu": lambda a: a.reshape(2, 4, D, FFN_B),
    "dn": lambda a: a.reshape(4, FFN_B, D),
    "pu": lambda a: a,
    "pg": lambda a: a.reshape(D, D),
}
_row_blocks = lambda a: a.reshape(N_DEV, -1, a.shape[-1])
OWNER_BLOCKS = {
    "a_w_in": lambda g: g,
    "a_w_out": _row_blocks,
    "kv_w": _row_blocks,
    "b_w_q": _row_blocks,
    "b_w_out": _row_blocks,
    "gu": lambda g: g,
    "dn": lambda g: _row_blocks(g.reshape(FFN_H, D)),
    "pu": lambda g: g.reshape(PLE_DIM, N_DEV, 128).transpose(1, 0, 2),
    "pg": _row_blocks,
}
ADAM_AFTER = {1: (("kv_w", "kv_w"), ("b_w_q", "b_w_q"), ("b_w_out", "b_w_out")),
              2: (("ffn_w_gate_up", "gu"), ("ffn_w_down", "dn"), ("ple_w_up", "pu"), ("ple_w_gate", "pg")),
              3: (("a_w_out", "a_w_out"), ("a_w_in", "a_w_in"))}


def kernel(x, p, a_w_in, a_lower_bound, a_norm_gain, a_w_out, kv_w, kv_b, b_w_q, b_b_q, b_sinks, b_w_out, b_b_out, ffn_w_gate_up, ffn_w_down, ple_w_up, ple_w_gate, ple_b_gate, ln_gain, ln_bias, loss_target, m_a_w_in, m_a_lower_bound, m_a_norm_gain, m_a_w_out, m_kv_w, m_kv_b, m_b_w_q, m_b_b_q, m_b_sinks, m_b_w_out, m_b_b_out, m_ffn_w_gate_up, m_ffn_w_down, m_ple_w_up, m_ple_w_gate, m_ple_b_gate, m_ln_gain, m_ln_bias, v_a_w_in, v_a_lower_bound, v_a_norm_gain, v_a_w_out, v_kv_w, v_kv_b, v_b_w_q, v_b_b_q, v_b_sinks, v_b_w_out, v_b_b_out, v_ffn_w_gate_up, v_ffn_w_down, v_ple_w_up, v_ple_w_gate, v_ple_b_gate, v_ln_gain, v_ln_bias):
    given = dict(locals())
    w = {n: given[n] for n in WEIGHTS}
    m = {n: given["m_" + n] for n in WEIGHTS}
    v = {n: given["v_" + n] for n in WEIGHTS}
    small_sharded = jnp.concatenate([_pack_rows(a, SUBLANES) for a in (a_lower_bound, ln_gain, ln_bias)], axis=0)
    (g_small,) = _gather([small_sharded], "gather_small")
    shards = {"a_w_in": a_w_in[0], "a_w_out": a_w_out[0], "kv_w": kv_w, "b_w_q": b_w_q[0], "b_w_out": b_w_out[0]}
    for l in range(2):
        shards.update({f"gu{l}": ffn_w_gate_up[l], f"dn{l}": ffn_w_down[l], f"pu{l}": ple_w_up[l], f"pg{l}": ple_w_gate[l]})
    gathered = {}
    for gi, g in enumerate(GATHER_GROUPS):
        lands = _sequencer_gather([shards[n].astype(CDT) for n in g], f"gather{gi}", gi, after=None if gi else g_small)
        for n, a in zip(g, lands):
            gathered[n] = KERNEL_LAYOUT[n.rstrip("01")](a)

    def getw(key, after):
        return gathered[key]

    full_rows = lambda r0, r: g_small[:, r0:r0 + r].transpose(1, 0, 2).reshape(r, D)
    sm = {"a_lower_bound": full_rows(0, 2), "ln_gain": full_rows(SUBLANES, 6).reshape(2, 3, D),
          "ln_bias": full_rows(2 * SUBLANES, 6).reshape(2, 3, D), "a_norm_gain": a_norm_gain, "kv_b": kv_b,
          "b_b_q": b_b_q[0], "b_sinks": b_sinks, "b_b_out": b_b_out, "ple_b_gate": ple_b_gate}

    scatters = []

    def emit(grads):
        names = list(grads)
        blocks = [OWNER_BLOCKS[n.rstrip("01")](grads[n]) for n in names]
        lands = _sequencer_exchange(blocks, "scatter", f"scatter{len(scatters)}", len(GATHER_GROUPS) + len(scatters))
        scatters.append(dict(zip(names, lands)))
        return blocks

    loss, grad_x, gs = _local_step(x[0], p[:, 0], loss_target[0], getw, sm, emit)

    (parts_small,) = _gather([_pack_partials(gs)], "gather_small_grads")
    packed = _adam_small(parts_small, _pack_small(w), _pack_small(m), _pack_small(v),
                         after=[grad_x] + list(scatters[0].values()))
    small_out = [_unpack_small(a, w) for a in packed]
    out = {n: [s[n] for s in small_out] for n, _, _ in SMALL}

    parts, last = {}, grad_x
    for i, landed in enumerate(scatters):
        parts.update(landed)
        for n, key in ADAM_AFTER.get(i, ()):
            lrc = (1,) * (3 - w[n].ndim) + w[n].shape
            layers = [parts[key]] if key in parts else [parts[key + "0"], parts[key + "1"]]
            res = _adam_big(w[n].reshape(lrc), layers, m[n].reshape(lrc), v[n].reshape(lrc), "adam_" + n, after=[last])
            out[n] = [r.reshape(w[n].shape) for r in res]
            last = res[3]

    loss = lax.psum(loss[0, 0], ("x", "y", "c"))
    res = [loss, grad_x[None]]
    for i in range(4):
        res += [out[n][i] for n in WEIGHTS]
    return tuple(res)
```

```python
import jax
import jax.numpy as jnp
from jax import lax
from jax.experimental import pallas as pl
from jax.experimental.pallas import tpu as pltpu
from jax.experimental.pallas import tpu_sc as plsc

F32 = jnp.float32
CDT = jnp.bfloat16

N_DEV = 8
D = 1024
HG_H, HG_DK, HG_CH = 8, 128, 64
HG_HPB = 4
ATT_HD, ATT_QH, ATT_KVH, ATT_G, WINDOW = 64, 16, 4, 4, 128
FFN_H = 2816
FFN_B = FFN_H // 4
PLE_DIM = 256
ALPHA = (2.0 * 2) ** 0.25
LN_EPS = 1e-5
RMS_EPS = 1e-6
ADAM_LR, ADAM_B1, ADAM_B2, ADAM_EPS, ADAM_WD, ADAM_STEP = 0.001, 0.9, 0.999, 1e-08, 0.01, 10
ROW_TILES = (512, 256, 128, 64)
VMEM_LIMIT = 48 * 1024 * 1024
NEG = -1e30

MESH = pl.DeviceIdType.MESH


def _tile(n, cands=ROW_TILES):
    for t in cands:
        if n % t == 0:
            return t
    return n


def _sds(shape, dtype):
    return jax.ShapeDtypeStruct(tuple(shape), dtype)


def _params(sem):
    return pltpu.CompilerParams(dimension_semantics=sem, vmem_limit_bytes=VMEM_LIMIT)


def _dot(a, b):
    return jnp.dot(a.astype(CDT), b.astype(CDT), preferred_element_type=F32)


def _dot_nt(a, b):
    return lax.dot_general(a.astype(CDT), b.astype(CDT), (((1,), (1,)), ((), ())), preferred_element_type=F32)


def _dot_tn(a, b):
    return lax.dot_general(a.astype(CDT), b.astype(CDT), (((0,), (0,)), ((), ())), preferred_element_type=F32)


def _sigmoid(x):
    return jax.nn.sigmoid(x)


def _ln_fwd(z, g, b):
    mu = jnp.mean(z, axis=-1, keepdims=True)
    zc = z - mu
    var = jnp.mean(zc * zc, axis=-1, keepdims=True)
    return zc * lax.rsqrt(var + LN_EPS) * g + b


def _ln_bwd(z, g, dy):
    mu = jnp.mean(z, axis=-1, keepdims=True)
    zc = z - mu
    var = jnp.mean(zc * zc, axis=-1, keepdims=True)
    rstd = lax.rsqrt(var + LN_EPS)
    xhat = zc * rstd
    dxh = dy * g
    dz = rstd * (dxh - jnp.mean(dxh, axis=-1, keepdims=True) - xhat * jnp.mean(dxh * xhat, axis=-1, keepdims=True))
    return dz, xhat


def _colsum(x):
    return jnp.sum(x, axis=0, keepdims=True)


def _acc(ref, val, first):
    @pl.when(first)
    def _():
        ref[...] = val

    @pl.when(jnp.logical_not(first))
    def _():
        ref[...] += val


def _call(after, body, **kw):
    after = [] if after is None else list(after)
    specs = list(kw["in_specs"])
    kw["in_specs"] = [pl.BlockSpec(memory_space=pl.ANY)] * len(after) + specs

    def ordered_body(*refs):
        body(*refs[len(after):])

    call = pl.pallas_call(ordered_body, **kw)

    def pinned(*args):
        args = [a if s.memory_space is not None else pltpu.with_memory_space_constraint(a, pltpu.HBM)
                for a, s in zip(args, specs)]
        return call(*after, *args)

    return pinned


def _mm_nn(a, b3, out_shape, oblock, omap, out_dtype, bias3=None, name="mm_nn"):
    M, K = a.shape
    G, _, Nb = b3.shape
    tm = _tile(M)

    def body(a_ref, b_ref, *rest):
        o_ref = rest[-1]
        acc = _dot(a_ref[...], b_ref[...])
        if bias3 is not None:
            acc = acc + rest[0][...]
        o_ref[...] = acc.astype(o_ref.dtype)

    in_specs = [pl.BlockSpec((tm, K), lambda g, i: (i, 0)), pl.BlockSpec((None, K, Nb), lambda g, i: (g, 0, 0))]
    args = [a, b3]
    if bias3 is not None:
        in_specs.append(pl.BlockSpec((None, 1, Nb), lambda g, i: (g, 0, 0)))
        args.append(bias3)
    return _call(
        None, body, grid=(G, M // tm), in_specs=in_specs, out_specs=pl.BlockSpec(oblock, omap),
        out_shape=_sds(out_shape, out_dtype), compiler_params=_params(("arbitrary", "arbitrary")), name=name)(*args)


def _mm_tn(a3, b3, G, amap, bmap, ablock, bblock, out_shape, oblock, omap, name="mm_tn"):
    S = a3.shape[1]
    tk = _tile(S, (512, 256, 128))
    Mo, No = oblock[-2], oblock[-1]

    def body(a_ref, b_ref, o_ref, acc):
        k = pl.program_id(1)
        _acc(acc, _dot_tn(a_ref[...], b_ref[...]), k == 0)

        @pl.when(k == pl.num_programs(1) - 1)
        def _():
            o_ref[...] = acc[...].astype(o_ref.dtype)

    return _call(
        None, body, grid=(G, S // tk),
        in_specs=[pl.BlockSpec(ablock(tk), amap), pl.BlockSpec(bblock(tk), bmap)],
        out_specs=pl.BlockSpec(oblock, omap), out_shape=_sds(out_shape, CDT),
        scratch_shapes=[pltpu.VMEM((Mo, No), F32)],
        compiler_params=_params(("arbitrary", "arbitrary")), name=name)(a3, b3)


def _wgrad(a3, b3, name):
    Ga, S, M = a3.shape
    Gb, _, N = b3.shape
    G = max(Ga, Gb)
    return _mm_tn(
        a3, b3, G,
        (lambda g, k: (g, k, 0)) if Ga > 1 else (lambda g, k: (0, k, 0)),
        (lambda g, k: (g, k, 0)) if Gb > 1 else (lambda g, k: (0, k, 0)),
        lambda tk: (None, tk, M), lambda tk: (None, tk, N),
        (G, M, N), (None, M, N), lambda g, k: (g, 0, 0), name=name)


def _mixout_ln(u3, w3, bias, xin, gain, beta, name):
    G, S, Kb = u3.shape
    tm = _tile(S)

    def body(u_ref, w_ref, b_ref, x_ref, g_ref, be_ref, z_ref, xo_ref, xob_ref):
        h = b_ref[...] + _dot(u_ref[0], w_ref[0])
        for g in range(1, G):
            h = h + _dot(u_ref[g], w_ref[g])
        z = ALPHA * x_ref[...] + h
        z_ref[...] = z
        y = _ln_fwd(z, g_ref[...], be_ref[...])
        xo_ref[...] = y
        xob_ref[...] = y.astype(CDT)

    row = pl.BlockSpec((tm, D), lambda i: (i, 0))
    vec = pl.BlockSpec((1, D), lambda i: (0, 0))
    return _call(
        None, body, grid=(S // tm,),
        in_specs=[pl.BlockSpec((G, tm, Kb), lambda i: (0, i, 0)), pl.BlockSpec((G, Kb, D), lambda i: (0, 0, 0)),
                  vec, row, vec, vec],
        out_specs=[row, row, row], out_shape=[_sds((S, D), F32), _sds((S, D), F32), _sds((S, D), CDT)],
        compiler_params=_params(("arbitrary",)), name=name)(u3, w3, bias, xin, gain, beta)


def _ffn_fwd(xin, xin_b, wgu, wdn, gain, beta, name):
    S = xin.shape[0]
    tm = _tile(S)

    def body(x_ref, xb_ref, wgu_ref, wdn_ref, g_ref, be_ref, gu_ref, hid_ref, z_ref, xo_ref, xob_ref, acc):
        j = pl.program_id(1)
        xb = xb_ref[...]
        gate = _dot_nt(xb, wgu_ref[0])
        up = _dot_nt(xb, wgu_ref[1])
        gu_ref[0] = gate
        gu_ref[1] = up
        hid = (gate * _sigmoid(gate) * up).astype(CDT)
        hid_ref[...] = hid
        _acc(acc, _dot(hid, wdn_ref[...]), j == 0)

        @pl.when(j == 3)
        def _():
            z = ALPHA * x_ref[...] + acc[...]
            z_ref[...] = z
            y = _ln_fwd(z, g_ref[...], be_ref[...])
            xo_ref[...] = y
            xob_ref[...] = y.astype(CDT)

    row = pl.BlockSpec((tm, D), lambda i, j: (i, 0))
    vec = pl.BlockSpec((1, D), lambda i, j: (0, 0))
    return _call(
        None, body, grid=(S // tm, 4),
        in_specs=[row, row, pl.BlockSpec((2, None, FFN_B, D), lambda i, j: (0, j, 0, 0)),
                  pl.BlockSpec((None, FFN_B, D), lambda i, j: (j, 0, 0)), vec, vec],
        out_specs=[pl.BlockSpec((2, None, tm, FFN_B), lambda i, j: (0, j, i, 0)),
                   pl.BlockSpec((None, tm, FFN_B), lambda i, j: (j, i, 0)), row, row, row],
        out_shape=[_sds((2, 4, S, FFN_B), F32), _sds((4, S, FFN_B), CDT), _sds((S, D), F32), _sds((S, D), F32),
                   _sds((S, D), CDT)],
        scratch_shapes=[pltpu.VMEM((tm, D), F32)],
        compiler_params=_params(("arbitrary", "arbitrary")), name=name)(xin, xin_b, wgu, wdn, gain, beta)


def _ple_fwd(xin, xin_b, p_b, wpg, bgate, wpu, gain, beta, name):
    S = xin.shape[0]
    tm = _tile(S)

    def body(x_ref, xb_ref, p_ref, wpg_ref, bg_ref, wpu_ref, g_ref, be_ref, sg_ref, up_ref, z_ref, xo_ref, xob_ref):
        sg = _sigmoid(_dot(xb_ref[...], wpg_ref[...]) + bg_ref[...])
        pb = p_ref[...]
        up = jnp.concatenate([_dot(pb, wpu_ref[j]) for j in range(N_DEV)], axis=-1)
        sg_ref[...] = sg
        up_ref[...] = up
        z = ALPHA * x_ref[...] + sg * up
        z_ref[...] = z
        y = _ln_fwd(z, g_ref[...], be_ref[...])
        xo_ref[...] = y
        xob_ref[...] = y.astype(CDT)

    row = pl.BlockSpec((tm, D), lambda i: (i, 0))
    vec = pl.BlockSpec((1, D), lambda i: (0, 0))
    return _call(
        None, body, grid=(S // tm,),
        in_specs=[row, row, pl.BlockSpec((tm, PLE_DIM), lambda i: (i, 0)), pl.BlockSpec((D, D), lambda i: (0, 0)), vec,
                  pl.BlockSpec((N_DEV, PLE_DIM, D // N_DEV), lambda i: (0, 0, 0)), vec, vec],
        out_specs=[row] * 5,
        out_shape=[_sds((S, D), F32)] * 4 + [_sds((S, D), CDT)],
        compiler_params=_params(("arbitrary",)), name=name)(xin, xin_b, p_b, wpg, bgate, wpu, gain, beta)


def _loss_fwd_bwd(y, target):
    S = y.shape[0]
    tm = _tile(S)

    def body(y_ref, t_ref, l_ref, dy_ref):
        e = y_ref[...] - t_ref[...]
        dy_ref[...] = e * (1.0 / D)
        part = 0.5 * jnp.sum(jnp.sum(e * e, axis=-1, keepdims=True) * (1.0 / D), axis=0, keepdims=True)
        _acc(l_ref, part, pl.program_id(0) == 0)

    row = pl.BlockSpec((tm, D), lambda i: (i, 0))
    return _call(
        None, body, grid=(S // tm,), in_specs=[row, row],
        out_specs=[pl.BlockSpec((1, 1), lambda i: (0, 0)), row],
        out_shape=[_sds((1, 1), F32), _sds((S, D), F32)],
        compiler_params=_params(("arbitrary",)), name="loss")(y, target)


def _ple_bwd(dy, z, sg, up, gain, wpg, name, after=None):
    S = dy.shape[0]
    tm = _tile(S)

    def body(dy_ref, z_ref, sg_ref, up_ref, g_ref, wpg_ref, dx_ref, dgl_ref, dup_ref, dgain_ref, dbeta_ref, dbg_ref):
        first = pl.program_id(0) == 0
        dy_ = dy_ref[...]
        dz, xhat = _ln_bwd(z_ref[...], g_ref[...], dy_)
        sg_ = sg_ref[...]
        dgl = dz * up_ref[...] * sg_ * (1.0 - sg_)
        dgl_ref[...] = dgl.astype(CDT)
        dup_ref[...] = (dz * sg_).astype(CDT)
        dx_ref[...] = ALPHA * dz + _dot_nt(dgl, wpg_ref[...])
        _acc(dgain_ref, _colsum(dy_ * xhat), first)
        _acc(dbeta_ref, _colsum(dy_), first)
        _acc(dbg_ref, _colsum(dgl), first)

    row = pl.BlockSpec((tm, D), lambda i: (i, 0))
    vec = pl.BlockSpec((1, D), lambda i: (0, 0))
    return _call(
        after, body, grid=(S // tm,), in_specs=[row, row, row, row, vec, pl.BlockSpec((D, D), lambda i: (0, 0))],
        out_specs=[row, row, row, vec, vec, vec],
        out_shape=[_sds((S, D), F32), _sds((S, D), CDT), _sds((S, D), CDT)] + [_sds((1, D), F32)] * 3,
        compiler_params=_params(("arbitrary",)), name=name)(dy, z, sg, up, gain, wpg)


def _ffn_bwd(dy, z, gu, wgu, wdn, gain, name, after=None):
    S = dy.shape[0]
    tm = _tile(S)

    def body(dy_ref, z_ref, gu_ref, wgu_ref, wdn_ref, g_ref, dx_ref, dzb_ref, dgu_ref, dgain_ref, dbeta_ref,
             dz_scr, acc):
        i, j = pl.program_id(0), pl.program_id(1)

        @pl.when(j == 0)
        def _():
            dy_ = dy_ref[...]
            dz, xhat = _ln_bwd(z_ref[...], g_ref[...], dy_)
            dz_scr[...] = dz
            dzb_ref[...] = dz.astype(CDT)
            _acc(dgain_ref, _colsum(dy_ * xhat), i == 0)
            _acc(dbeta_ref, _colsum(dy_), i == 0)

        dhid = _dot_nt(dz_scr[...], wdn_ref[...])
        gate, up = gu_ref[0], gu_ref[1]
        sg = _sigmoid(gate)
        dgate = (dhid * up * (sg * (1.0 + gate * (1.0 - sg)))).astype(CDT)
        dup = (dhid * (gate * sg)).astype(CDT)
        dgu_ref[0] = dgate
        dgu_ref[1] = dup
        _acc(acc, _dot(dgate, wgu_ref[0]) + _dot(dup, wgu_ref[1]), j == 0)

        @pl.when(j == 3)
        def _():
            dx_ref[...] = ALPHA * dz_scr[...] + acc[...]

    row = pl.BlockSpec((tm, D), lambda i, j: (i, 0))
    vec = pl.BlockSpec((1, D), lambda i, j: (0, 0))
    return _call(
        after, body, grid=(S // tm, 4),
        in_specs=[row, row, pl.BlockSpec((2, None, tm, FFN_B), lambda i, j: (0, j, i, 0)),
                  pl.BlockSpec((2, None, FFN_B, D), lambda i, j: (0, j, 0, 0)),
                  pl.BlockSpec((None, FFN_B, D), lambda i, j: (j, 0, 0)), vec],
        out_specs=[row, row, pl.BlockSpec((2, None, tm, FFN_B), lambda i, j: (0, j, i, 0)), vec, vec],
        out_shape=[_sds((S, D), F32), _sds((S, D), CDT), _sds((2, 4, S, FFN_B), CDT), _sds((1, D), F32),
                   _sds((1, D), F32)],
        scratch_shapes=[pltpu.VMEM((tm, D), F32), pltpu.VMEM((tm, D), F32)],
        compiler_params=_params(("arbitrary", "arbitrary")), name=name)(dy, z, gu, wgu, wdn, gain)


def _mixout_bwd(dy, z, gain, w3, du_dtype, name, after=None):
    S = dy.shape[0]
    G, Kb, _ = w3.shape
    tm = _tile(S)

    def body(dy_ref, z_ref, g_ref, w_ref, dz_ref, dzb_ref, du_ref, dgain_ref, dbeta_ref, dbias_ref):
        first = pl.program_id(0) == 0
        dy_ = dy_ref[...]
        dz, xhat = _ln_bwd(z_ref[...], g_ref[...], dy_)
        dz_ref[...] = dz
        dzb = dz.astype(CDT)
        dzb_ref[...] = dzb
        for g in range(G):
            du_ref[g] = _dot_nt(dzb, w_ref[g]).astype(du_ref.dtype)
        _acc(dgain_ref, _colsum(dy_ * xhat), first)
        _acc(dbeta_ref, _colsum(dy_), first)
        _acc(dbias_ref, _colsum(dz), first)

    row = pl.BlockSpec((tm, D), lambda i: (i, 0))
    vec = pl.BlockSpec((1, D), lambda i: (0, 0))
    return _call(
        after, body, grid=(S // tm,), in_specs=[row, row, vec, pl.BlockSpec((G, Kb, D), lambda i: (0, 0, 0))],
        out_specs=[row, row, pl.BlockSpec((G, tm, Kb), lambda i: (0, i, 0)), vec, vec, vec],
        out_shape=[_sds((S, D), F32), _sds((S, D), CDT), _sds((G, S, Kb), du_dtype)] + [_sds((1, D), F32)] * 3,
        compiler_params=_params(("arbitrary",)), name=name)(dy, z, gain, w3)


def _half_select(low):
    r = lax.broadcasted_iota(jnp.int32, (2 * ATT_HD, ATT_HD), 0)
    c = lax.broadcasted_iota(jnp.int32, (2 * ATT_HD, ATT_HD), 1)
    return (r == c + (0 if low else ATT_HD)).astype(CDT)


def _half_place(low):
    r = lax.broadcasted_iota(jnp.int32, (ATT_HD, 2 * ATT_HD), 0)
    c = lax.broadcasted_iota(jnp.int32, (ATT_HD, 2 * ATT_HD), 1)
    return (c == r + (0 if low else ATT_HD)).astype(CDT)


def _pair_lanes(even, odd):
    return (jnp.dot(even, _half_place(True), preferred_element_type=F32)
            + jnp.dot(odd, _half_place(False), preferred_element_type=F32)).astype(CDT)


def _proj_heads(a, w, bias, heads, name):
    S, K = a.shape
    N = heads * ATT_HD
    tm = _tile(S)

    def body(a_ref, w_ref, b_ref, o_ref):
        acc = (_dot(a_ref[...], w_ref[...]) + b_ref[...]).astype(CDT)
        sel = (_half_select(True), _half_select(False))
        for h in range(heads):
            pair = acc[:, (h // 2) * 2 * ATT_HD:(h // 2 + 1) * 2 * ATT_HD]
            o_ref[h] = jnp.dot(pair, sel[h % 2], preferred_element_type=F32).astype(CDT)

    return _call(
        None, body, grid=(S // tm,),
        in_specs=[pl.BlockSpec((tm, K), lambda i: (i, 0)), pl.BlockSpec((K, N), lambda i: (0, 0)),
                  pl.BlockSpec((1, N), lambda i: (0, 0))],
        out_specs=pl.BlockSpec((heads, tm, ATT_HD), lambda i: (0, i, 0)), out_shape=_sds((heads, S, ATT_HD), CDT),
        compiler_params=_params(("arbitrary",)), name=name)(a, w, bias)


def _qkv_bwd(dz, dq, dkv4, wq, wkv, name, after=None):
    S = dz.shape[0]
    tm = _tile(S)
    HK = dkv4.shape[0]
    NK = HK * ATT_HD

    def body(dz_ref, dq_ref, dkv_ref, wq_ref, wkv_ref, dx_ref, dkvn_ref, dkvb_ref):
        first = pl.program_id(0) == 0
        dkvn = jnp.concatenate([_pair_lanes(dkv_ref[2 * i].astype(CDT), dkv_ref[2 * i + 1].astype(CDT))
                                for i in range(HK // 2)], axis=-1)
        dkvn_ref[...] = dkvn
        dx_ref[...] = ALPHA * dz_ref[...] + _dot_nt(dq_ref[...], wq_ref[...]) + _dot_nt(dkvn, wkv_ref[...])
        for h in range(HK):
            _acc(dkvb_ref.at[h], _colsum(dkv_ref[h]), first)

    row = pl.BlockSpec((tm, D), lambda i: (i, 0))
    return _call(
        after, body, grid=(S // tm,),
        in_specs=[row, row, pl.BlockSpec((HK, tm, ATT_HD), lambda i: (0, i, 0)),
                  pl.BlockSpec((D, D), lambda i: (0, 0)), pl.BlockSpec((D, NK), lambda i: (0, 0))],
        out_specs=[row, pl.BlockSpec((tm, NK), lambda i: (i, 0)), pl.BlockSpec((HK, 1, ATT_HD), lambda i: (0, 0, 0))],
        out_shape=[_sds((S, D), F32), _sds((S, NK), CDT), _sds((HK, 1, ATT_HD), F32)],
        compiler_params=_params(("arbitrary",)), name=name)(dz, dq, dkv4, wq, wkv)


def _inproj_bwd(dz, dproj, wain, name, after=None):
    S = dz.shape[0]
    tm = _tile(S)
    nb = wain.shape[-1]

    def body(dz_ref, dp_ref, w_ref, dx_ref, acc):
        j = pl.program_id(1)
        _acc(acc, _dot_nt(dp_ref[...], w_ref[...]), j == 0)

        @pl.when(j == N_DEV - 1)
        def _():
            dx_ref[...] = ALPHA * dz_ref[...] + acc[...]

    row = pl.BlockSpec((tm, D), lambda i, j: (i, 0))
    return _call(
        after, body, grid=(S // tm, N_DEV),
        in_specs=[row, pl.BlockSpec((None, tm, nb), lambda i, j: (j // 2, i, j % 2)),
                  pl.BlockSpec((None, D, nb), lambda i, j: (j, 0, 0))],
        out_specs=row, out_shape=_sds((S, D), F32), scratch_shapes=[pltpu.VMEM((tm, D), F32)],
        compiler_params=_params(("arbitrary", "arbitrary")), name=name)(dz, dproj, wain)


def _running_sum(x, reverse=False):
    rows = x.shape[0]
    row = lax.broadcasted_iota(jnp.int32, x.shape, 0)
    step = 1
    while step < rows:
        if reverse:
            x = x + jnp.where(row < rows - step, pltpu.roll(x, rows - step, 0), 0.0)
        else:
            x = x + jnp.where(row >= step, pltpu.roll(x, step, 0), 0.0)
        step *= 2
    return x


def _hg_gates(q, f, alb_ref):
    a0, a1 = alb_ref[0:1, :], alb_ref[1:2, :]
    mx = jnp.maximum(a0, a1)
    e0, e1 = jnp.exp(a0 - mx), jnp.exp(a1 - mx)
    lb = e0 / (e0 + e1)
    sig = _sigmoid(f)
    forget = lb + (1.0 - lb) * sig
    k = (1.0 - lb) * _sigmoid(-f)
    qs = q * _sigmoid(q) * (HG_DK ** -0.5)
    return qs, k, jnp.log(forget), sig, lb, forget


def _hg_intra(qs, k, b, b_scr):
    b_scr[...] = b
    bm = b_scr[pl.ds(HG_CH // 2 - 1, 1), :]
    bl = b_scr[pl.ds(HG_CH - 1, 1), :]
    eb = jnp.exp(b)
    qb = qs * eb
    e_q = jnp.exp(b - bm)
    e_k = jnp.exp(bm - b)
    e_d = jnp.exp(bl - b)
    return qb, qs * e_q, k * e_k, k * e_d, jnp.exp(bl), eb, e_q, e_k, e_d


def _hgrn_fwd(proj, alb, ngain):
    S = proj.shape[1]
    nc = S // HG_CH
    wb = HG_HPB * HG_DK

    def body(pj_ref, alb_ref, ng_ref, o_ref, y_ref, st_ref, st_scr, b_scr):
        n = pl.program_id(1)

        @pl.when(n == 0)
        def _():
            st_scr[...] = jnp.zeros_like(st_scr)

        r = lax.broadcasted_iota(jnp.int32, (HG_CH, HG_CH), 0)
        c = lax.broadcasted_iota(jnp.int32, (HG_CH, HG_CH), 1)
        causal = r >= c
        for j in range(HG_HPB):
            lanes = pl.ds(j * HG_DK, HG_DK)
            q, f, v, g = pj_ref[0, :, lanes], pj_ref[1, :, lanes], pj_ref[2, :, lanes], pj_ref[3, :, lanes]
            qs, k, logf, _, _, _ = _hg_gates(q, f, alb_ref.at[:, lanes])
            b = _running_sum(logf)
            qb, qt, kt, kd, ebl, _, _, _, _ = _hg_intra(qs, k, b, b_scr.at[j])
            st = st_scr[j]
            st_ref[j] = st
            a = jnp.where(causal, _dot_nt(qt, kt), 0.0)
            o = _dot(a, v) + _dot_nt(qb, st)
            st_scr[j] = st * ebl + _dot_tn(v, kd)
            o_ref[:, lanes] = o
            rinv = lax.rsqrt(jnp.mean(o * o, axis=-1, keepdims=True) + RMS_EPS)
            y_ref[:, lanes] = (o * rinv * ng_ref[...] * (g * _sigmoid(g))).astype(CDT)

    blk = pl.BlockSpec((HG_CH, wb), lambda h, n: (n, h))
    return _call(
        None, body, grid=(HG_H // HG_HPB, nc),
        in_specs=[pl.BlockSpec((4, HG_CH, wb), lambda h, n: (0, n, h)), pl.BlockSpec((2, wb), lambda h, n: (0, h)),
                  pl.BlockSpec((1, HG_DK), lambda h, n: (0, 0))],
        out_specs=[blk, blk, pl.BlockSpec((HG_HPB, None, HG_DK, HG_DK), lambda h, n: (h, n, 0, 0))],
        out_shape=[_sds((S, D), F32), _sds((S, D), CDT), _sds((HG_H, nc, HG_DK, HG_DK), F32)],
        scratch_shapes=[pltpu.VMEM((HG_HPB, HG_DK, HG_DK), F32), pltpu.VMEM((HG_HPB, HG_CH, HG_DK), F32)],
        compiler_params=_params(("arbitrary", "arbitrary")), name="hgrn_fwd")(proj, alb, ngain)


def _hgrn_bwd(proj, alb, ngain, o, states, dy):
    S = proj.shape[1]
    nc = S // HG_CH
    wb = HG_HPB * HG_DK

    def body(pj_ref, alb_ref, ng_ref, o_ref, st_ref, dy_ref, dpj_ref, dalb_ref, dng_ref, dst_scr, b_scr):
        h, n = pl.program_id(0), pl.program_id(1)

        @pl.when(n == 0)
        def _():
            dst_scr[...] = jnp.zeros_like(dst_scr)

        ng = ng_ref[...]
        r = lax.broadcasted_iota(jnp.int32, (HG_CH, HG_CH), 0)
        c = lax.broadcasted_iota(jnp.int32, (HG_CH, HG_CH), 1)
        causal = r >= c
        dng = None
        for j in range(HG_HPB):
            lanes = pl.ds(j * HG_DK, HG_DK)
            q, f, v, g = pj_ref[0, :, lanes], pj_ref[1, :, lanes], pj_ref[2, :, lanes], pj_ref[3, :, lanes]
            o_ = o_ref[:, lanes]
            dy_ = dy_ref[:, lanes]
            sg = _sigmoid(g)
            rinv = lax.rsqrt(jnp.mean(o_ * o_, axis=-1, keepdims=True) + RMS_EPS)
            nrm = o_ * rinv
            dr = dy_ * (g * sg)
            dg = dy_ * nrm * ng * (sg * (1.0 + g * (1.0 - sg)))
            dn = dr * ng
            do = rinv * (dn - nrm * jnp.mean(dn * nrm, axis=-1, keepdims=True))
            dng = _colsum(dr * nrm) if dng is None else dng + _colsum(dr * nrm)
            qs, k, logf, sig, lb, forget = _hg_gates(q, f, alb_ref.at[:, lanes])
            b = _running_sum(logf)
            qb, qt, kt, kd, ebl, eb, e_q, e_k, e_d = _hg_intra(qs, k, b, b_scr.at[j])
            st = st_ref[j]
            dstn = dst_scr[j]
            qt, kt, qb, kd = (t.astype(CDT).astype(F32) for t in (qt, kt, qb, kd))
            a = jnp.where(causal, _dot_nt(qt, kt), 0.0)
            da = jnp.where(causal, _dot_nt(do, v), 0.0)
            dv = _dot_tn(a, do) + _dot_nt(kd, dstn)
            dqb = _dot(do, st)
            dkd = _dot(v, dstn)
            dqt = _dot(da, kt)
            dkt = _dot_tn(da, qt)
            dbl = _colsum(dkd * kd) + ebl * _colsum(dstn * st)
            dst_scr[j] = dstn * ebl + _dot_tn(do, qb)
            dqs = dqt * e_q + dqb * eb
            dk = dkt * e_k + dkd * e_d
            db = dqt * qt + dqb * qb - dkt * kt - dkd * kd
            dlogf = _running_sum(db, reverse=True) + dbl
            dforget = dlogf / forget
            dsig = (1.0 - lb) * (dforget - dk)
            df = dsig * sig * (1.0 - sig)
            dlb = _colsum((dforget - dk) * (1.0 - sig))
            sq = _sigmoid(q)
            dq = dqs * (HG_DK ** -0.5) * (sq * (1.0 + q * (1.0 - sq)))
            dpj_ref[0, :, lanes] = dq.astype(CDT)
            dpj_ref[1, :, lanes] = df.astype(CDT)
            dpj_ref[2, :, lanes] = dv.astype(CDT)
            dpj_ref[3, :, lanes] = dg.astype(CDT)
            da0 = dlb * lb * (1.0 - lb)
            _acc(dalb_ref.at[pl.ds(0, 1), lanes], da0, n == 0)
            _acc(dalb_ref.at[pl.ds(1, 1), lanes], -da0, n == 0)
        _acc(dng_ref, dng, jnp.logical_and(h == 0, n == 0))

    blk = pl.BlockSpec((HG_CH, wb), lambda h, n: (nc - 1 - n, h))
    pj = pl.BlockSpec((4, HG_CH, wb), lambda h, n: (0, nc - 1 - n, h))
    alb_blk = pl.BlockSpec((2, wb), lambda h, n: (0, h))
    ng_blk = pl.BlockSpec((1, HG_DK), lambda h, n: (0, 0))
    return _call(
        None, body, grid=(HG_H // HG_HPB, nc),
        in_specs=[pj, alb_blk, ng_blk, blk,
                  pl.BlockSpec((HG_HPB, None, HG_DK, HG_DK), lambda h, n: (h, nc - 1 - n, 0, 0)), blk],
        out_specs=[pj, alb_blk, ng_blk],
        out_shape=[_sds((4, S, D), CDT), _sds((2, D), F32), _sds((1, HG_DK), F32)],
        scratch_shapes=[pltpu.VMEM((HG_HPB, HG_DK, HG_DK), F32), pltpu.VMEM((HG_HPB, HG_CH, HG_DK), F32)],
        compiler_params=_params(("arbitrary", "arbitrary")), name="hgrn_bwd")(proj, alb, ngain, o, states, dy)


def _slope(h):
    return 2.0 ** (-8.0 * (h + 1) / ATT_QH)


def _attn_mask(n):
    qi = lax.broadcasted_iota(jnp.int32, (WINDOW, 2 * WINDOW), 0)
    si = lax.broadcasted_iota(jnp.int32, (WINDOW, 2 * WINDOW), 1)
    dist = qi - si + WINDOW
    valid = (dist >= 0) & (dist < WINDOW) & (n * WINDOW - WINDOW + si >= 0)
    return valid, dist.astype(F32)


def _attn_probs(qh, kh, sink, slope, valid, distf):
    s = _dot_nt(qh, kh) * (ATT_HD ** -0.5) - slope * distf
    s = jnp.where(valid, s, NEG)
    m = jnp.maximum(jnp.max(s, axis=-1, keepdims=True), sink)
    e = jnp.exp(s - m)
    es = jnp.exp(sink - m)
    inv = 1.0 / (jnp.sum(e, axis=-1, keepdims=True) + es)
    return e * inv, es * inv


def _attn_specs(S):
    nb = S // WINDOW
    cur = lambda H: pl.BlockSpec((H, WINDOW, ATT_HD), lambda n: (0, n, 0))
    prev = lambda H: pl.BlockSpec((H, WINDOW, ATT_HD), lambda n: (0, jnp.maximum(n - 1, 0), 0))
    return nb, cur, prev


def _attn_fwd(q4, kv4, sinks):
    S = q4.shape[1]
    nb, cur, prev = _attn_specs(S)

    def body(sink_ref, q_ref, kvc_ref, kvp_ref, o_ref):
        valid, distf = _attn_mask(pl.program_id(0))
        outs = []
        for h in range(ATT_QH):
            kvh = h // ATT_G
            kh = jnp.concatenate([kvp_ref[kvh], kvc_ref[kvh]], axis=0)
            vh = jnp.concatenate([kvp_ref[ATT_KVH + kvh], kvc_ref[ATT_KVH + kvh]], axis=0)
            p, _ = _attn_probs(q_ref[h], kh, sink_ref[0, h], _slope(h), valid, distf)
            outs.append(_dot(p, vh).astype(CDT))
            if h % 2:
                o_ref[:, pl.ds((h - 1) * ATT_HD, 2 * ATT_HD)] = _pair_lanes(outs[h - 1], outs[h])

    return _call(
        None, body, grid=(nb,),
        in_specs=[pl.BlockSpec(memory_space=pltpu.SMEM), cur(ATT_QH), cur(2 * ATT_KVH), prev(2 * ATT_KVH)],
        out_specs=pl.BlockSpec((WINDOW, D), lambda n: (n, 0)), out_shape=_sds((S, D), CDT),
        compiler_params=_params(("arbitrary",)), name="attn_fwd")(sinks, q4, kv4, kv4)


def _attn_bwd(q4, kv4, sinks, do):
    S = q4.shape[1]
    nb, cur, prev = _attn_specs(S)

    def body(sink_ref, q_ref, kvc_ref, kvp_ref, do_ref, dq_ref, dkv_ref, dbq_ref, dsink_ref):
        n = pl.program_id(0)
        first = n == 0

        @pl.when(first)
        def _():
            dkv_ref[...] = jnp.zeros_like(dkv_ref)
            dsink_ref[...] = jnp.zeros_like(dsink_ref)

        valid, distf = _attn_mask(n)
        lane = lax.broadcasted_iota(jnp.int32, (1, 128), 1)
        rows_cur = pl.ds(pl.multiple_of(n * WINDOW, WINDOW), WINDOW)
        rows_prev = pl.ds(pl.multiple_of(jnp.maximum(n - 1, 0) * WINDOW, WINDOW), WINDOW)
        dsinks = jnp.zeros((1, 128), F32)
        sel = (_half_select(True), _half_select(False))
        for kvh in range(ATT_KVH):
            kh = jnp.concatenate([kvp_ref[kvh], kvc_ref[kvh]], axis=0)
            vh = jnp.concatenate([kvp_ref[ATT_KVH + kvh], kvc_ref[ATT_KVH + kvh]], axis=0)
            dk = dv = None
            dqs = []
            for h in range(kvh * ATT_G, (kvh + 1) * ATT_G):
                qh = q_ref[h]
                doh = jnp.dot(do_ref[:, pl.ds((h // 2) * 2 * ATT_HD, 2 * ATT_HD)], sel[h % 2],
                              preferred_element_type=F32).astype(CDT)
                p, ps = _attn_probs(qh, kh, sink_ref[0, h], _slope(h), valid, distf)
                dp = _dot_nt(doh, vh)
                dd = jnp.sum(p * dp, axis=-1, keepdims=True)
                ds = p * (dp - dd)
                dsinks = dsinks + jnp.where(lane == h, -jnp.sum(ps * dd, axis=0, keepdims=True), 0.0)
                dqh = _dot(ds, kh) * (ATT_HD ** -0.5)
                dqs.append(dqh.astype(CDT))
                _acc(dbq_ref.at[h], _colsum(dqh), first)
                dkh = _dot_tn(ds, qh) * (ATT_HD ** -0.5)
                dvh = _dot_tn(p, doh)
                dk = dkh if dk is None else dk + dkh
                dv = dvh if dv is None else dv + dvh
            for i in range(ATT_G // 2):
                lanes = pl.ds((kvh * ATT_G + 2 * i) * ATT_HD, 2 * ATT_HD)
                dq_ref[:, lanes] = _pair_lanes(dqs[2 * i], dqs[2 * i + 1])
            dkv_ref[kvh, rows_prev, :] += dk[:WINDOW]
            dkv_ref[kvh, rows_cur, :] += dk[WINDOW:]
            dkv_ref[ATT_KVH + kvh, rows_prev, :] += dv[:WINDOW]
            dkv_ref[ATT_KVH + kvh, rows_cur, :] += dv[WINDOW:]
        dsink_ref[...] += dsinks

    return _call(
        None, body, grid=(nb,),
        in_specs=[pl.BlockSpec(memory_space=pltpu.SMEM), cur(ATT_QH), cur(2 * ATT_KVH), prev(2 * ATT_KVH),
                  pl.BlockSpec((WINDOW, D), lambda n: (n, 0))],
        out_specs=[pl.BlockSpec((WINDOW, D), lambda n: (n, 0)), pl.BlockSpec((2 * ATT_KVH, S, ATT_HD), lambda n: (0, 0, 0)),
                   pl.BlockSpec((ATT_QH, 1, ATT_HD), lambda n: (0, 0, 0)), pl.BlockSpec((1, 128), lambda n: (0, 0))],
        out_shape=[_sds((S, D), CDT), _sds((2 * ATT_KVH, S, ATT_HD), F32), _sds((ATT_QH, 1, ATT_HD), F32),
                   _sds((1, 128), F32)],
        compiler_params=_params(("arbitrary",)), name="attn_bwd")(sinks, q4, kv4, kv4, do)


def _local_step(x, p, target, getw, sm, emit):
    S = x.shape[0]
    vec = lambda a: a.reshape(1, -1)
    ln_g = lambda l, k: vec(sm["ln_gain"][l, k])
    ln_b = lambda l, k: vec(sm["ln_bias"][l, k])
    xb = x.astype(CDT)
    pb = p.astype(CDT)

    proj = _mm_nn(xb, getw("a_w_in", None), (4, S, D), (None, _tile(S), 512), lambda g, i: (g // 2, i, g % 2), F32,
                  name="a_in")
    o_a, y_a, states = _hgrn_fwd(proj, sm["a_lower_bound"], sm["a_norm_gain"])
    zeros = jnp.zeros((1, D), F32)
    z = [[None] * 3 for _ in range(2)]
    xs = [[None] * 3 for _ in range(2)]
    xbs = [[None] * 3 for _ in range(2)]
    z[0][0], xs[0][0], xbs[0][0] = _mixout_ln(y_a[None], getw("a_w_out", y_a)[None], zeros, x, ln_g(0, 0), ln_b(0, 0),
                                              "a_out_ln")
    gu, hid, sgs, ups = [None, None], [None, None], [None, None], [None, None]

    def ffn_ple(l):
        wgu = getw(f"gu{l}", xbs[l][0])
        gu[l], hid[l], z[l][1], xs[l][1], xbs[l][1] = _ffn_fwd(
            xs[l][0], xbs[l][0], wgu, getw(f"dn{l}", None), ln_g(l, 1), ln_b(l, 1), f"ffn_fwd{l}")
        sgs[l], ups[l], z[l][2], xs[l][2], xbs[l][2] = _ple_fwd(
            xs[l][1], xbs[l][1], pb[l], getw(f"pg{l}", None), vec(sm["ple_b_gate"][l]), getw(f"pu{l}", None), ln_g(l, 2),
            ln_b(l, 2), f"ple_fwd{l}")

    ffn_ple(0)
    x3, x3b = xs[0][2], xbs[0][2]
    w_kv, w_q, w_bo = getw("kv_w", x3b), getw("b_w_q", None), getw("b_w_out", None)
    kv4 = _proj_heads(x3b, w_kv, vec(sm["kv_b"]), 2 * ATT_KVH, "kv_proj")
    q4 = _proj_heads(x3b, w_q, vec(sm["b_b_q"]), ATT_QH, "q_proj")
    o_b = _attn_fwd(q4, kv4, sm["b_sinks"])
    z[1][0], xs[1][0], xbs[1][0] = _mixout_ln(o_b[None], w_bo[None], sm["b_b_out"], x3, ln_g(1, 0), ln_b(1, 0),
                                              "b_out_ln")
    ffn_ple(1)
    loss, dy = _loss_fwd_bwd(xs[1][2], target)

    gs = {}
    d_ln_g = [[None] * 3 for _ in range(2)]
    d_ln_b = [[None] * 3 for _ in range(2)]
    g_bg = [None, None]

    def ffn_ple_bwd(l, dy, after=None):
        dx2, dgl, dup, d_ln_g[l][2], d_ln_b[l][2], g_bg[l] = _ple_bwd(dy, z[l][2], sgs[l], ups[l], ln_g(l, 2),
                                                                     getw(f"pg{l}", None), f"ple_bwd{l}", after=after)
        g_pg = _wgrad(xbs[l][1][None], dgl[None], f"g_ple_gate{l}")[0]
        g_pu = _wgrad(pb[l][None], dup[None], f"g_ple_up{l}")[0]
        dx1, dzb, dgu, d_ln_g[l][1], d_ln_b[l][1] = _ffn_bwd(dx2, z[l][1], gu[l], getw(f"gu{l}", None),
                                                           getw(f"dn{l}", None), ln_g(l, 1), f"ffn_bwd{l}")
        g_dn = _wgrad(hid[l], dzb[None], f"g_ffn_down{l}")
        g_gu = _wgrad(dgu.reshape(8, S, FFN_B), xbs[l][0][None], f"g_ffn_gate_up{l}")
        return dx1, emit({f"pg{l}": g_pg, f"pu{l}": g_pu, f"dn{l}": g_dn, f"gu{l}": g_gu})

    dx1, tok = ffn_ple_bwd(1, dy)
    dz, dzb, do, d_ln_g[1][0], d_ln_b[1][0], gs["b_b_out"] = _mixout_bwd(dx1, z[1][0], ln_g(1, 0), w_bo[None], CDT,
                                                                        "b_out_bwd", after=tok)
    g_bo = _wgrad(o_b[None], dzb[None], "g_b_w_out")[0]
    dq, dkv4, dbq, dsinks = _attn_bwd(q4, kv4, sm["b_sinks"], do[0])
    gs["b_b_q"] = dbq
    gs["b_sinks"] = dsinks
    g_q = _wgrad(x3b[None], dq[None], "g_b_w_q")[0]
    dx3, dkv, gs["kv_b"] = _qkv_bwd(dz, dq, dkv4, w_q, w_kv, "qkv_bwd", after=tok)
    g_kv = _wgrad(x3b[None], dkv[None], "g_kv_w")[0]
    tok = emit({"b_w_out": g_bo, "b_w_q": g_q, "kv_w": g_kv})
    dx1, tok = ffn_ple_bwd(0, dx3, tok)
    w_ao = getw("a_w_out", None)
    dz, dzb, dyr, d_ln_g[0][0], d_ln_b[0][0], _ = _mixout_bwd(dx1, z[0][0], ln_g(0, 0), w_ao[None], F32, "a_out_bwd",
                                                              after=tok)
    g_ao = _wgrad(y_a[None], dzb[None], "g_a_w_out")[0]
    dproj, gs["a_lower_bound"], gs["a_norm_gain"] = _hgrn_bwd(proj, sm["a_lower_bound"], sm["a_norm_gain"], o_a, states,
                                                              dyr[0])
    tk = lambda t: (None, t, D)
    g_ain = _mm_tn(xb[None], dproj, N_DEV, lambda g, k: (0, k, 0), lambda g, k: (g // 2, k, g % 2),
                   tk, lambda t: (None, t, 512), (N_DEV, D, 512), (None, D, 512), lambda g, k: (g, 0, 0), name="g_a_w_in")
    tok = emit({"a_w_out": g_ao, "a_w_in": g_ain})
    grad_x = _inproj_bwd(dz, dproj, getw("a_w_in", None), "a_in_bwd", after=tok)
    gs["ple_b_gate"] = jnp.concatenate(g_bg, axis=0)
    gs["ln_gain"] = jnp.stack([jnp.concatenate(r, axis=0) for r in d_ln_g])
    gs["ln_bias"] = jnp.stack([jnp.concatenate(r, axis=0) for r in d_ln_b])
    return loss, grad_x, gs


def _peer(k):
    x, y, c = lax.axis_index("x"), lax.axis_index("y"), lax.axis_index("c")
    px = 1 - x if k & 4 else x
    py = 1 - y if k & 2 else y
    pc = 1 - c if k & 1 else c
    return (px, py, pc), 4 * px + 2 * py + pc


def _my_index():
    return 4 * lax.axis_index("x") + 2 * lax.axis_index("y") + lax.axis_index("c")


def _exchange(srcs, dst_shapes, plan, name):
    n_src, n_piece = len(srcs), len(plan)

    def body(*refs):
        src_refs, dst_refs = refs[:n_src], refs[n_src:n_src + len(dst_shapes)]
        send_sems, recv_sems, local_sems = refs[n_src + len(dst_shapes):]
        me = _my_index()

        def at(ref, idx):
            return ref.at[idx] if idx else ref

        local = []
        for t, (si, sfn, di, dfn) in enumerate(plan):
            cp = pltpu.make_async_copy(at(src_refs[si], sfn(me)), at(dst_refs[di], dfn(me)), local_sems.at[t])
            cp.start()
            local.append(cp)
        sends = []
        for k in range(1, N_DEV):
            peer, pid = _peer(k)
            for t, (si, sfn, di, dfn) in enumerate(plan):
                cp = pltpu.make_async_remote_copy(
                    src_ref=at(src_refs[si], sfn(pid)), dst_ref=at(dst_refs[di], dfn(me)),
                    send_sem=send_sems.at[t * 7 + k - 1], recv_sem=recv_sems.at[t * 7 + k - 1],
                    device_id=peer, device_id_type=MESH)
                cp.start()
                sends.append(cp)
        for k in range(1, N_DEV):
            peer, pid = _peer(k)
            for t, (si, sfn, di, dfn) in enumerate(plan):
                pltpu.make_async_remote_copy(
                    src_ref=at(src_refs[si], sfn(me)), dst_ref=at(dst_refs[di], dfn(pid)),
                    send_sem=send_sems.at[t * 7 + k - 1], recv_sem=recv_sems.at[t * 7 + k - 1],
                    device_id=peer, device_id_type=MESH).wait_recv()
        for cp in sends:
            cp.wait_send()
        for cp in local:
            cp.wait()

    hbm = pl.BlockSpec(memory_space=pltpu.HBM)
    return _call(
        None, body, in_specs=[hbm] * n_src, out_specs=[hbm] * len(dst_shapes), out_shape=dst_shapes,
        scratch_shapes=[pltpu.SemaphoreType.DMA((7 * n_piece,)), pltpu.SemaphoreType.DMA((7 * n_piece,)),
                        pltpu.SemaphoreType.DMA((n_piece,))],
        name=name)(*srcs)


def _gather(shards, name):
    dsts = [_sds((N_DEV,) + a.shape, a.dtype) for a in shards]
    plan = [(i, lambda j: (), i, lambda s: (s,)) for i in range(len(shards))]
    return _exchange(shards, dsts, plan, name)


_HBM = pl.BlockSpec(memory_space=pltpu.HBM)
_SEM = pl.BlockSpec(memory_space=pltpu.SEMAPHORE)
_DATAFLOW = pltpu.SideEffectType.DATAFLOW_SIDE_EFFECTING


def _piece_copy(mode, src, land, send_sems, recv_sems, t, k, sender, receiver, peer):
    return pltpu.make_async_remote_copy(
        src_ref=src if mode == "gather" else src.at[receiver], dst_ref=land.at[sender],
        send_sem=send_sems.at[t * 7 + k - 1], recv_sem=recv_sems.at[t * 7 + k - 1], device_id=peer, device_id_type=MESH)


def _sequencer_exchange(srcs, mode, name, collective_id, after=None):
    n = len(srcs)
    land_shapes = [((N_DEV,) + a.shape) if mode == "gather" else a.shape for a in srcs]
    extra = [] if after is None else [after]

    def body(*refs):
        src_refs, land_refs = refs[:n], refs[n + len(extra):2 * n + len(extra)]
        send_sems, recv_sems, local_sems = refs[2 * n + len(extra):]
        barrier = pltpu.get_barrier_semaphore()
        for k in range(1, N_DEV):
            pl.semaphore_signal(barrier, inc=1, device_id=_peer(k)[0], device_id_type=MESH)
        pl.semaphore_wait(barrier, N_DEV - 1)
        me = _my_index()
        local = []
        for i in range(n):
            cp = pltpu.make_async_copy(src_refs[i] if mode == "gather" else src_refs[i].at[me], land_refs[i].at[me],
                                       local_sems.at[i])
            cp.start()
            local.append(cp)
        for k in range(1, N_DEV):
            peer, pid = _peer(k)
            for t in range(n):
                _piece_copy(mode, src_refs[t], land_refs[t], send_sems, recv_sems, t, k, me, pid, peer).start()
        for k in range(1, N_DEV):
            peer, pid = _peer(k)
            for t in range(n):
                _piece_copy(mode, src_refs[t], land_refs[t], send_sems, recv_sems, t, k, pid, me, peer).wait_recv()
        for k in range(1, N_DEV):
            peer, pid = _peer(k)
            for t in range(n):
                _piece_copy(mode, src_refs[t], land_refs[t], send_sems, recv_sems, t, k, me, pid, peer).wait_send()
        for cp in local:
            cp.wait()

    return pl.kernel(
        body, out_type=[_sds(s, a.dtype) for s, a in zip(land_shapes, srcs)],
        mesh=plsc.ScalarSubcoreMesh(axis_name="sequencer", num_cores=1),
        scratch_types=[pltpu.SemaphoreType.DMA((7 * n,)), pltpu.SemaphoreType.DMA((7 * n,)), pltpu.SemaphoreType.DMA((n,))],
        compiler_params=pltpu.CompilerParams(collective_id=collective_id), name=name)(*srcs, *extra)


def _sequencer_gather(srcs, name, collective_id, after=None):
    n = len(srcs)
    extra = [] if after is None else [after]

    def body(*refs):
        src_refs, land_refs = refs[:n], refs[n + len(extra):2 * n + len(extra)]
        send_sems, recv_sems, local_sems = refs[2 * n + len(extra):]
        x, y, c = lax.axis_index("x"), lax.axis_index("y"), lax.axis_index("c")
        sibling = (x, y, 1 - c)
        chips = [(1 - x, y), (x, 1 - y), (1 - x, 1 - y)]
        index = lambda px, py, pc: 4 * px + 2 * py + pc
        barrier = pltpu.get_barrier_semaphore()
        for peer in [sibling] + [(*chip, c) for chip in chips]:
            pl.semaphore_signal(barrier, inc=1, device_id=peer, device_id_type=MESH)
        pl.semaphore_wait(barrier, 4)

        def copy(t, k, slot, to, src=None):
            return pltpu.make_async_remote_copy(
                src_ref=land_refs[t].at[slot] if src is None else src, dst_ref=land_refs[t].at[slot],
                send_sem=send_sems.at[7 * t + k], recv_sem=recv_sems.at[7 * t + k], device_id=to, device_id_type=MESH)

        me = index(x, y, c)
        local = []
        for t in range(n):
            cp = pltpu.make_async_copy(src_refs[t], land_refs[t].at[me], local_sems.at[t])
            cp.start()
            local.append(cp)
        sends = []
        for t in range(n):
            sends.append(copy(t, 0, me, sibling, src=src_refs[t]))
            sends += [copy(t, 1 + j, me, (*chip, c), src=src_refs[t]) for j, chip in enumerate(chips)]
        for cp in sends:
            cp.start()
        for j, chip in enumerate(chips):
            for t in range(n):
                copy(t, 1 + j, index(*chip, c), sibling, src=src_refs[t]).wait_recv()
                passed = copy(t, 4 + j, index(*chip, c), sibling)
                passed.start()
                sends.append(passed)
        for t in range(n):
            copy(t, 0, index(x, y, 1 - c), sibling, src=src_refs[t]).wait_recv()
        for j, chip in enumerate(chips):
            for t in range(n):
                copy(t, 4 + j, index(*chip, 1 - c), sibling, src=src_refs[t]).wait_recv()
        for cp in sends:
            cp.wait_send()
        for cp in local:
            cp.wait()

    return pl.kernel(
        body, out_type=[_sds((N_DEV,) + a.shape, a.dtype) for a in srcs],
        mesh=plsc.ScalarSubcoreMesh(axis_name="sequencer", num_cores=1),
        scratch_types=[pltpu.SemaphoreType.DMA((7 * n,)), pltpu.SemaphoreType.DMA((7 * n,)), pltpu.SemaphoreType.DMA((n,))],
        compiler_params=pltpu.CompilerParams(collective_id=collective_id), name=name)(*srcs, *extra)


def _xstart(groups, mode, name, after=None):
    flat = [a for g in groups for a in g]
    n, ng = len(flat), len(groups)
    land_shapes = [((N_DEV,) + a.shape) if mode == "gather" else a.shape for a in flat]
    first = [sum(len(g) for g in groups[:i]) for i in range(ng)]
    extra = [] if after is None else [after]

    def body(*refs):
        srcs, lands = refs[:n], refs[n:2 * n]
        sems = refs[2 * n + len(extra):2 * n + len(extra) + 2 * ng]
        tok_ref, local_sems = refs[-2], refs[-1]
        me = _my_index()
        local = []
        for i in range(n):
            cp = pltpu.make_async_copy(srcs[i] if mode == "gather" else srcs[i].at[me], lands[i].at[me], local_sems.at[i])
            cp.start()
            local.append(cp)
        for cp in local:
            cp.wait()
        for gi, g in enumerate(groups):
            for k in range(1, N_DEV):
                peer, pid = _peer(k)
                for t in range(len(g)):
                    i = first[gi] + t
                    _piece_copy(mode, srcs[i], lands[i], sems[2 * gi], sems[2 * gi + 1], t, k, me, pid, peer).start()
        tok_ref[...] = jnp.zeros_like(tok_ref)

    sem_shapes = []
    for g in groups:
        sem_shapes += [pltpu.SemaphoreType.DMA((7 * len(g),))] * 2
    thru = [pltpu.HBM(a.shape, a.dtype) for a in flat] + [pltpu.HBM(s, a.dtype) for s, a in zip(land_shapes, flat)]
    outs = pl.pallas_call(
        body, in_specs=[_HBM] * (2 * n) + [pl.BlockSpec(memory_space=pl.ANY)] * len(extra),
        out_specs=[_SEM] * (2 * ng) + [_HBM] * (2 * n) + [pl.BlockSpec(memory_space=pltpu.VMEM)],
        out_shape=sem_shapes + thru + [_sds((8, 128), F32)],
        input_output_aliases={i: 2 * ng + i for i in range(2 * n)},
        scratch_shapes=[pltpu.SemaphoreType.DMA((n,))],
        compiler_params=pltpu.CompilerParams(has_side_effects=_DATAFLOW), name=name)(
            *[pltpu.with_memory_space_constraint(a, pltpu.HBM) for a in flat],
            *[pltpu.with_memory_space_constraint(lax.empty(s, a.dtype), pltpu.HBM) for s, a in zip(land_shapes, flat)], *extra)
    sems, srcs_thru, lands_thru = outs[:2 * ng], outs[2 * ng:2 * ng + n], outs[2 * ng + n:2 * ng + 2 * n]
    handles = [(sems[2 * gi], sems[2 * gi + 1], srcs_thru[first[gi]:first[gi] + len(g)],
                lands_thru[first[gi]:first[gi] + len(g)]) for gi, g in enumerate(groups)]
    return handles, outs[-1]


def _xwait(handle, mode, after, name):
    send_sems, recv_sems, srcs_thru, lands_thru = handle
    n = len(srcs_thru)

    def body(*refs):
        srcs, lands, send, recv = refs[:n], refs[n:2 * n], refs[2 * n], refs[2 * n + 1]
        me = _my_index()
        for k in range(1, N_DEV):
            peer, pid = _peer(k)
            for t in range(n):
                _piece_copy(mode, srcs[t], lands[t], send, recv, t, k, pid, me, peer).wait_recv()
        for k in range(1, N_DEV):
            peer, pid = _peer(k)
            for t in range(n):
                _piece_copy(mode, srcs[t], lands[t], send, recv, t, k, me, pid, peer).wait_send()

    extra = [] if after is None else [after]
    outs = pl.pallas_call(
        body, in_specs=[_HBM] * (2 * n) + [_SEM, _SEM] + [pl.BlockSpec(memory_space=pl.ANY)] * len(extra),
        out_specs=[_HBM] * (2 * n),
        out_shape=[pltpu.HBM(a.shape, a.dtype) for a in list(srcs_thru) + list(lands_thru)],
        input_output_aliases={i: i for i in range(2 * n)},
        compiler_params=pltpu.CompilerParams(has_side_effects=_DATAFLOW), name=name)(
            *srcs_thru, *lands_thru, send_sems, recv_sems, *extra)
    return outs[n:]


def _adamw(w, g, m, v):
    m = ADAM_B1 * m + (1.0 - ADAM_B1) * g
    v = ADAM_B2 * v + (1.0 - ADAM_B2) * (g * g)
    m_hat = m / (1.0 - ADAM_B1 ** ADAM_STEP)
    v_hat = v / (1.0 - ADAM_B2 ** ADAM_STEP)
    delta = -ADAM_LR * (m_hat / (jnp.sqrt(v_hat) + ADAM_EPS) + ADAM_WD * w)
    return delta, m, v


def _adam_big(w, parts, m, v, name, after=None):
    L, R, C = w.shape
    tr = _tile(R, (256, 128, 176, 64, 32, 16))
    nr = R // tr

    def body(w_ref, *refs):
        p_refs, (m_ref, v_ref, g_ref, d_ref, mo_ref, vo_ref) = refs[:L], refs[L:]
        for l in range(L):
            @pl.when(pl.program_id(0) == l)
            def _(p_ref=p_refs[l]):
                g = p_ref[0].astype(F32)
                for s in range(1, N_DEV):
                    g = g + p_ref[s].astype(F32)
                g_ref[...] = g
                d_ref[...], mo_ref[...], vo_ref[...] = _adamw(w_ref[...], g, m_ref[...], v_ref[...])

    row = pl.BlockSpec((None, tr, C), lambda l, i: (l, i, 0))
    park = lambda l_of: (lambda l, i: (0, jnp.where(l == l_of, i, 0 if l_of else nr - 1), 0))
    return _call(
        after, body, grid=(L, nr),
        in_specs=[row] + [pl.BlockSpec((N_DEV, tr, C), park(l)) for l in range(L)] + [row, row],
        out_specs=[row] * 4, out_shape=[_sds((L, R, C), F32)] * 4,
        compiler_params=_params(("arbitrary", "arbitrary")), name=name)(w, *parts, m, v)


SMALL = (("a_lower_bound", 2, True), ("ln_gain", 6, True), ("ln_bias", 6, True), ("a_norm_gain", 1, False),
         ("kv_b", 4, False), ("b_b_q", 8, False), ("b_sinks", 1, False), ("b_b_out", 8, False), ("ple_b_gate", 16, False))
SUBLANES = 8


def _slot(r):
    return -(-r // SUBLANES) * SUBLANES


SMALL_ROWS = sum(_slot(r) for _, r, _ in SMALL)
PART_ROWS = sum(r * N_DEV if sh else _slot(r) for _, r, sh in SMALL)


def _pack_rows(a, rows):
    flat = a.reshape(-1)
    return jnp.pad(flat, (0, rows * 128 - flat.size)).reshape(rows, 128)


def _pack_small(d):
    return jnp.concatenate([_pack_rows(d[n], _slot(r)) for n, r, _ in SMALL], axis=0)


def _unpack_small(packed, like):
    out, r0 = {}, 0
    for n, r, _ in SMALL:
        out[n] = packed[r0:r0 + r].reshape(-1)[:like[n].size].reshape(like[n].shape)
        r0 += _slot(r)
    return out


def _pack_partials(gs):
    blocks = []
    for n, r, sharded in SMALL:
        if sharded:
            blocks.append(gs[n].reshape(r * N_DEV, 128))
        else:
            blocks.append(_pack_rows(gs[n], _slot(r)))
    return jnp.concatenate(blocks, axis=0)


def _adam_small(parts, w, m, v, after=None):
    def body(p_ref, w_ref, m_ref, v_ref, g_ref, d_ref, mo_ref, vo_ref, tot):
        me = _my_index()
        t = p_ref[0]
        for s in range(1, N_DEV):
            t = t + p_ref[s]
        tot[...] = t
        g_ref[...] = jnp.zeros_like(g_ref)
        src, dst = 0, 0
        for _, r, sharded in SMALL:
            if sharded:
                for i in range(r):
                    g_ref[pl.ds(dst + i, 1), :] = tot[pl.ds(src + i * N_DEV + me, 1), :]
                src += r * N_DEV
            else:
                g_ref[pl.ds(dst, _slot(r)), :] = tot[pl.ds(src, _slot(r)), :]
                src += _slot(r)
            dst += _slot(r)
        d_ref[...], mo_ref[...], vo_ref[...] = _adamw(w_ref[...], g_ref[...], m_ref[...], v_ref[...])

    full = pl.BlockSpec((SMALL_ROWS, 128), lambda: (0, 0))
    return _call(
        after, body, in_specs=[pl.BlockSpec((N_DEV, PART_ROWS, 128), lambda: (0, 0, 0)), full, full, full],
        out_specs=[full] * 4, out_shape=[_sds((SMALL_ROWS, 128), F32)] * 4,
        scratch_shapes=[pltpu.VMEM((PART_ROWS, 128), F32)], name="adam_small")(parts, w, m, v)


WEIGHTS = ("a_w_in", "a_lower_bound", "a_norm_gain", "a_w_out", "kv_w", "kv_b", "b_w_q", "b_b_q", "b_sinks", "b_w_out",
           "b_b_out", "ffn_w_gate_up", "ffn_w_down", "ple_w_up", "ple_w_gate", "ple_b_gate", "ln_gain", "ln_bias")


GATHER_GROUPS = (("a_w_in",), ("a_w_out", "gu0", "dn0", "pu0", "pg0"), ("kv_w", "b_w_q", "b_w_out"),
                 ("gu1", "dn1", "pu1", "pg1"))
KERNEL_LAYOUT = {
    "a_w_in": lambda a: a,
    "a_w_out": lambda a: a.reshape(D, D),
    "kv_w": lambda a: a.reshape(D, 2 * ATT_KVH * ATT_HD),
    "b_w_q": lambda a: a.reshape(D, D),
    "b_w_out": lambda a: a.reshape(D, D),
    "gu": lambda a: a.reshape(2, 4, FFN_B, D),
    "dn": lambda a: a.reshape(4, FFN_B, D),
    "pu": lambda a: a,
    "pg": lambda a: a.reshape(D, D),
}
_row_blocks = lambda a: a.reshape(N_DEV, -1, a.shape[-1])
OWNER_BLOCKS = {
    "a_w_in": lambda g: g,
    "a_w_out": _row_blocks,
    "kv_w": _row_blocks,
    "b_w_q": _row_blocks,
    "b_w_out": _row_blocks,
    "gu": lambda g: g,
    "dn": lambda g: _row_blocks(g.reshape(FFN_H, D)),
    "pu": lambda g: g.reshape(PLE_DIM, N_DEV, 128).transpose(1, 0, 2),
    "pg": _row_blocks,
}
ADAM_AFTER = {1: (("kv_w", "kv_w"), ("b_w_q", "b_w_q"), ("b_w_out", "b_w_out")),
              2: (("ffn_w_gate_up", "gu"), ("ffn_w_down", "dn"), ("ple_w_up", "pu"), ("ple_w_gate", "pg")),
              3: (("a_w_out", "a_w_out"), ("a_w_in", "a_w_in"))}


def kernel(x, p, a_w_in, a_lower_bound, a_norm_gain, a_w_out, kv_w, kv_b, b_w_q, b_b_q, b_sinks, b_w_out, b_b_out, ffn_w_gate_up, ffn_w_down, ple_w_up, ple_w_gate, ple_b_gate, ln_gain, ln_bias, loss_target, m_a_w_in, m_a_lower_bound, m_a_norm_gain, m_a_w_out, m_kv_w, m_kv_b, m_b_w_q, m_b_b_q, m_b_sinks, m_b_w_out, m_b_b_out, m_ffn_w_gate_up, m_ffn_w_down, m_ple_w_up, m_ple_w_gate, m_ple_b_gate, m_ln_gain, m_ln_bias, v_a_w_in, v_a_lower_bound, v_a_norm_gain, v_a_w_out, v_kv_w, v_kv_b, v_b_w_q, v_b_b_q, v_b_sinks, v_b_w_out, v_b_b_out, v_ffn_w_gate_up, v_ffn_w_down, v_ple_w_up, v_ple_w_gate, v_ple_b_gate, v_ln_gain, v_ln_bias):
    given = dict(locals())
    w = {n: given[n] for n in WEIGHTS}
    m = {n: given["m_" + n] for n in WEIGHTS}
    v = {n: given["v_" + n] for n in WEIGHTS}
    small_sharded = jnp.concatenate([_pack_rows(a, SUBLANES) for a in (a_lower_bound, ln_gain, ln_bias)], axis=0)
    (g_small,) = _gather([small_sharded], "gather_small")
    shards = {"a_w_in": a_w_in[0], "a_w_out": a_w_out[0], "kv_w": kv_w, "b_w_q": b_w_q[0], "b_w_out": b_w_out[0]}
    for l in range(2):
        shards.update({f"gu{l}": ffn_w_gate_up[l].T, f"dn{l}": ffn_w_down[l], f"pu{l}": ple_w_up[l], f"pg{l}": ple_w_gate[l]})
    gathered = {}
    for gi, g in enumerate(GATHER_GROUPS):
        lands = _sequencer_gather([shards[n].astype(CDT) for n in g], f"gather{gi}", gi, after=None if gi else g_small)
        for n, a in zip(g, lands):
            gathered[n] = KERNEL_LAYOUT[n.rstrip("01")](a)

    def getw(key, after):
        return gathered[key]

    full_rows = lambda r0, r: g_small[:, r0:r0 + r].transpose(1, 0, 2).reshape(r, D)
    sm = {"a_lower_bound": full_rows(0, 2), "ln_gain": full_rows(SUBLANES, 6).reshape(2, 3, D),
          "ln_bias": full_rows(2 * SUBLANES, 6).reshape(2, 3, D), "a_norm_gain": a_norm_gain, "kv_b": kv_b,
          "b_b_q": b_b_q[0], "b_sinks": b_sinks, "b_b_out": b_b_out, "ple_b_gate": ple_b_gate}

    scatters = []

    def emit(grads):
        names = list(grads)
        blocks = [OWNER_BLOCKS[n.rstrip("01")](grads[n]) for n in names]
        lands = _sequencer_exchange(blocks, "scatter", f"scatter{len(scatters)}", len(GATHER_GROUPS) + len(scatters))
        scatters.append(dict(zip(names, lands)))
        return blocks

    loss, grad_x, gs = _local_step(x[0], p[:, 0], loss_target[0], getw, sm, emit)

    (parts_small,) = _gather([_pack_partials(gs)], "gather_small_grads")
    packed = _adam_small(parts_small, _pack_small(w), _pack_small(m), _pack_small(v),
                         after=[grad_x] + list(scatters[0].values()))
    small_out = [_unpack_small(a, w) for a in packed]
    out = {n: [s[n] for s in small_out] for n, _, _ in SMALL}

    parts, last = {}, grad_x
    for i, landed in enumerate(scatters):
        parts.update(landed)
        for n, key in ADAM_AFTER.get(i, ()):
            lrc = (1,) * (3 - w[n].ndim) + w[n].shape
            layers = [parts[key]] if key in parts else [parts[key + "0"], parts[key + "1"]]
            shard = (lambda a: a.reshape(lrc).swapaxes(1, 2)) if key == "gu" else (lambda a: a.reshape(lrc))
            res = _adam_big(shard(w[n]), layers, shard(m[n]), shard(v[n]), "adam_" + n, after=[last])
            out[n] = [(r.swapaxes(1, 2) if key == "gu" else r).reshape(w[n].shape) for r in res]
            last = res[3]

    loss = lax.psum(loss[0, 0], ("x", "y", "c"))
    res = [loss, grad_x[None]]
    for i in range(4):
        res += [out[n][i] for n in WEIGHTS]
    return tuple(res)
```

```python
import jax
import jax.numpy as jnp
from jax import lax
from jax.experimental import pallas as pl
from jax.experimental.pallas import tpu as pltpu
from jax.experimental.pallas import tpu_sc as plsc

F32 = jnp.float32
CDT = jnp.bfloat16

N_DEV = 8
D = 1024
HG_H, HG_DK, HG_CH = 8, 128, 64
HG_HPB = 4
ATT_HD, ATT_QH, ATT_KVH, ATT_G, WINDOW = 64, 16, 4, 4, 128
FFN_H = 2816
FFN_B = FFN_H // 4
PLE_DIM = 256
ALPHA = (2.0 * 2) ** 0.25
LN_EPS = 1e-5
RMS_EPS = 1e-6
ADAM_LR, ADAM_B1, ADAM_B2, ADAM_EPS, ADAM_WD, ADAM_STEP = 0.001, 0.9, 0.999, 1e-08, 0.01, 10
ROW_TILES = (512, 256, 128, 64)
VMEM_LIMIT = 48 * 1024 * 1024
NEG = -1e30

MESH = pl.DeviceIdType.MESH


def _tile(n, cands=ROW_TILES):
    for t in cands:
        if n % t == 0:
            return t
    return n


def _sds(shape, dtype):
    return jax.ShapeDtypeStruct(tuple(shape), dtype)


def _params(sem):
    return pltpu.CompilerParams(dimension_semantics=sem, vmem_limit_bytes=VMEM_LIMIT)


def _dot(a, b):
    return jnp.dot(a.astype(CDT), b.astype(CDT), preferred_element_type=F32)


def _dot_nt(a, b):
    return lax.dot_general(a.astype(CDT), b.astype(CDT), (((1,), (1,)), ((), ())), preferred_element_type=F32)


def _dot_tn(a, b):
    return lax.dot_general(a.astype(CDT), b.astype(CDT), (((0,), (0,)), ((), ())), preferred_element_type=F32)


def _sigmoid(x):
    return jax.nn.sigmoid(x)


def _ln_fwd(z, g, b):
    mu = jnp.mean(z, axis=-1, keepdims=True)
    zc = z - mu
    var = jnp.mean(zc * zc, axis=-1, keepdims=True)
    return zc * lax.rsqrt(var + LN_EPS) * g + b


def _ln_bwd(z, g, dy):
    mu = jnp.mean(z, axis=-1, keepdims=True)
    zc = z - mu
    var = jnp.mean(zc * zc, axis=-1, keepdims=True)
    rstd = lax.rsqrt(var + LN_EPS)
    xhat = zc * rstd
    dxh = dy * g
    dz = rstd * (dxh - jnp.mean(dxh, axis=-1, keepdims=True) - xhat * jnp.mean(dxh * xhat, axis=-1, keepdims=True))
    return dz, xhat


def _colsum(x):
    return jnp.sum(x, axis=0, keepdims=True)


def _acc(ref, val, first):
    @pl.when(first)
    def _():
        ref[...] = val

    @pl.when(jnp.logical_not(first))
    def _():
        ref[...] += val


def _zero_at(ref, first):
    @pl.when(first)
    def _():
        ref[...] = jnp.zeros_like(ref)


def _call(after, body, **kw):
    after = [] if after is None else list(after)
    specs = list(kw["in_specs"])
    kw["in_specs"] = [pl.BlockSpec(memory_space=pl.ANY)] * len(after) + specs

    def ordered_body(*refs):
        body(*refs[len(after):])

    call = pl.pallas_call(ordered_body, **kw)

    def pinned(*args):
        args = [a if s.memory_space is not None else pltpu.with_memory_space_constraint(a, pltpu.HBM)
                for a, s in zip(args, specs)]
        return call(*after, *args)

    return pinned


def _mm_nn(a, b3, out_shape, oblock, omap, out_dtype, bias3=None, name="mm_nn"):
    M, K = a.shape
    G, _, Nb = b3.shape
    tm = _tile(M)

    def body(a_ref, b_ref, *rest):
        o_ref = rest[-1]
        acc = _dot(a_ref[...], b_ref[...])
        if bias3 is not None:
            acc = acc + rest[0][...]
        o_ref[...] = acc.astype(o_ref.dtype)

    in_specs = [pl.BlockSpec((tm, K), lambda g, i: (i, 0)), pl.BlockSpec((None, K, Nb), lambda g, i: (g, 0, 0))]
    args = [a, b3]
    if bias3 is not None:
        in_specs.append(pl.BlockSpec((None, 1, Nb), lambda g, i: (g, 0, 0)))
        args.append(bias3)
    return _call(
        None, body, grid=(G, M // tm), in_specs=in_specs, out_specs=pl.BlockSpec(oblock, omap),
        out_shape=_sds(out_shape, out_dtype), compiler_params=_params(("arbitrary", "arbitrary")), name=name)(*args)


def _mm_tn(a3, b3, G, amap, bmap, ablock, bblock, out_shape, oblock, omap, name="mm_tn"):
    S = a3.shape[1]

    def body(a_ref, b_ref, o_ref):
        o_ref[...] = _dot_tn(a_ref[...], b_ref[...]).astype(o_ref.dtype)

    return _call(
        None, body, grid=(G, 1),
        in_specs=[pl.BlockSpec(ablock(S), amap), pl.BlockSpec(bblock(S), bmap)],
        out_specs=pl.BlockSpec(oblock, omap), out_shape=_sds(out_shape, CDT),
        compiler_params=_params(("arbitrary", "arbitrary")), name=name)(a3, b3)


def _wgrad(a3, b3, name):
    Ga, S, M = a3.shape
    Gb, _, N = b3.shape
    G = max(Ga, Gb)
    return _mm_tn(
        a3, b3, G,
        (lambda g, k: (g, k, 0)) if Ga > 1 else (lambda g, k: (0, k, 0)),
        (lambda g, k: (g, k, 0)) if Gb > 1 else (lambda g, k: (0, k, 0)),
        lambda tk: (None, tk, M), lambda tk: (None, tk, N),
        (G, M, N), (None, M, N), lambda g, k: (g, 0, 0), name=name)


def _mixout_ln(u3, w3, bias, xin, gain, beta, name):
    G, S, Kb = u3.shape
    tm = _tile(S)

    def body(u_ref, w_ref, b_ref, x_ref, g_ref, be_ref, z_ref, xo_ref, xob_ref):
        h = b_ref[...] + _dot(u_ref[0], w_ref[0])
        for g in range(1, G):
            h = h + _dot(u_ref[g], w_ref[g])
        z = ALPHA * x_ref[...] + h
        z_ref[...] = z
        y = _ln_fwd(z, g_ref[...], be_ref[...])
        xo_ref[...] = y
        xob_ref[...] = y.astype(CDT)

    row = pl.BlockSpec((tm, D), lambda i: (i, 0))
    vec = pl.BlockSpec((1, D), lambda i: (0, 0))
    return _call(
        None, body, grid=(S // tm,),
        in_specs=[pl.BlockSpec((G, tm, Kb), lambda i: (0, i, 0)), pl.BlockSpec((G, Kb, D), lambda i: (0, 0, 0)),
                  vec, row, vec, vec],
        out_specs=[row, row, row], out_shape=[_sds((S, D), F32), _sds((S, D), F32), _sds((S, D), CDT)],
        compiler_params=_params(("arbitrary",)), name=name)(u3, w3, bias, xin, gain, beta)


def _ffn_fwd(xin, xin_b, wgu, wdn, gain, beta, name):
    S = xin.shape[0]
    tm = _tile(S)

    def hidden(xb_ref, wgu_ref, gu_ref, hid_ref):
        xb = xb_ref[...]
        gate = _dot_nt(xb, wgu_ref[0])
        up = _dot_nt(xb, wgu_ref[1])
        gu_ref[0] = gate
        gu_ref[1] = up
        hid_ref[...] = (gate * _sigmoid(gate) * up).astype(CDT)

    gu, hid = _call(
        None, hidden, grid=(S // tm, 4),
        in_specs=[pl.BlockSpec((tm, D), lambda i, j: (i, 0)), pl.BlockSpec((2, None, FFN_B, D), lambda i, j: (0, j, 0, 0))],
        out_specs=[pl.BlockSpec((2, None, tm, FFN_B), lambda i, j: (0, j, i, 0)),
                   pl.BlockSpec((None, tm, FFN_B), lambda i, j: (j, i, 0))],
        out_shape=[_sds((2, 4, S, FFN_B), F32), _sds((4, S, FFN_B), CDT)],
        compiler_params=_params(("arbitrary", "arbitrary")), name=name + "_hidden")(xin_b, wgu)

    def down(x_ref, hid_ref, wdn_ref, g_ref, be_ref, z_ref, xo_ref, xob_ref):
        z = ALPHA * x_ref[...]
        for j in range(4):
            z = z + _dot(hid_ref[j], wdn_ref[j])
        z_ref[...] = z
        y = _ln_fwd(z, g_ref[...], be_ref[...])
        xo_ref[...] = y
        xob_ref[...] = y.astype(CDT)

    row = pl.BlockSpec((tm, D), lambda i: (i, 0))
    vec = pl.BlockSpec((1, D), lambda i: (0, 0))
    z, xo, xob = _call(
        None, down, grid=(S // tm,),
        in_specs=[row, pl.BlockSpec((4, tm, FFN_B), lambda i: (0, i, 0)), pl.BlockSpec((4, FFN_B, D), lambda i: (0, 0, 0)),
                  vec, vec],
        out_specs=[row, row, row], out_shape=[_sds((S, D), F32), _sds((S, D), F32), _sds((S, D), CDT)],
        compiler_params=_params(("arbitrary",)), name=name + "_down")(xin, hid, wdn, gain, beta)
    return gu, hid, z, xo, xob


def _ple_fwd(xin, xin_b, p_b, wpg, bgate, wpu, gain, beta, name):
    S = xin.shape[0]
    tm = _tile(S)

    def body(x_ref, xb_ref, p_ref, wpg_ref, bg_ref, wpu_ref, g_ref, be_ref, sg_ref, up_ref, z_ref, xo_ref, xob_ref):
        sg = _sigmoid(_dot(xb_ref[...], wpg_ref[...]) + bg_ref[...])
        pb = p_ref[...]
        up = jnp.concatenate([_dot(pb, wpu_ref[j]) for j in range(N_DEV)], axis=-1)
        sg_ref[...] = sg
        up_ref[...] = up
        z = ALPHA * x_ref[...] + sg * up
        z_ref[...] = z
        y = _ln_fwd(z, g_ref[...], be_ref[...])
        xo_ref[...] = y
        xob_ref[...] = y.astype(CDT)

    row = pl.BlockSpec((tm, D), lambda i: (i, 0))
    vec = pl.BlockSpec((1, D), lambda i: (0, 0))
    return _call(
        None, body, grid=(S // tm,),
        in_specs=[row, row, pl.BlockSpec((tm, PLE_DIM), lambda i: (i, 0)), pl.BlockSpec((D, D), lambda i: (0, 0)), vec,
                  pl.BlockSpec((N_DEV, PLE_DIM, D // N_DEV), lambda i: (0, 0, 0)), vec, vec],
        out_specs=[row] * 5,
        out_shape=[_sds((S, D), F32)] * 4 + [_sds((S, D), CDT)],
        compiler_params=_params(("arbitrary",)), name=name)(xin, xin_b, p_b, wpg, bgate, wpu, gain, beta)


def _loss_fwd_bwd(y, target):
    S = y.shape[0]
    tm = _tile(S)

    def body(y_ref, t_ref, l_ref, dy_ref):
        e = y_ref[...] - t_ref[...]
        dy_ref[...] = e * (1.0 / D)
        part = 0.5 * jnp.sum(jnp.sum(e * e, axis=-1, keepdims=True) * (1.0 / D), axis=0, keepdims=True)
        _acc(l_ref, part, pl.program_id(0) == 0)

    row = pl.BlockSpec((tm, D), lambda i: (i, 0))
    return _call(
        None, body, grid=(S // tm,), in_specs=[row, row],
        out_specs=[pl.BlockSpec((1, 1), lambda i: (0, 0)), row],
        out_shape=[_sds((1, 1), F32), _sds((S, D), F32)],
        compiler_params=_params(("arbitrary",)), name="loss")(y, target)


def _ple_bwd(dy, z, sg, up, gain, wpg, name, after=None):
    S = dy.shape[0]
    tm = _tile(S)

    def body(dy_ref, z_ref, sg_ref, up_ref, g_ref, wpg_ref, dx_ref, dgl_ref, dup_ref, dgain_ref, dbeta_ref, dbg_ref):
        first = pl.program_id(0) == 0
        dy_ = dy_ref[...]
        dz, xhat = _ln_bwd(z_ref[...], g_ref[...], dy_)
        sg_ = sg_ref[...]
        dgl = dz * up_ref[...] * sg_ * (1.0 - sg_)
        dgl_ref[...] = dgl.astype(CDT)
        dup_ref[...] = (dz * sg_).astype(CDT)
        dx_ref[...] = ALPHA * dz + _dot_nt(dgl, wpg_ref[...])
        _acc(dgain_ref, _colsum(dy_ * xhat), first)
        _acc(dbeta_ref, _colsum(dy_), first)
        _acc(dbg_ref, _colsum(dgl), first)

    row = pl.BlockSpec((tm, D), lambda i: (i, 0))
    vec = pl.BlockSpec((1, D), lambda i: (0, 0))
    return _call(
        after, body, grid=(S // tm,), in_specs=[row, row, row, row, vec, pl.BlockSpec((D, D), lambda i: (0, 0))],
        out_specs=[row, row, row, vec, vec, vec],
        out_shape=[_sds((S, D), F32), _sds((S, D), CDT), _sds((S, D), CDT)] + [_sds((1, D), F32)] * 3,
        compiler_params=_params(("arbitrary",)), name=name)(dy, z, sg, up, gain, wpg)


def _ffn_bwd(dy, z, gu, wgu, wdn, gain, name, after=None):
    S = dy.shape[0]
    tm = _tile(S)

    def hidden(dy_ref, z_ref, gu_ref, wdn_ref, g_ref, dz_ref, dzb_ref, dgu_ref, dgain_ref, dbeta_ref):
        i, j = pl.program_id(0), pl.program_id(1)

        @pl.when(j == 0)
        def _():
            dy_ = dy_ref[...]
            dz, xhat = _ln_bwd(z_ref[...], g_ref[...], dy_)
            dz_ref[...] = dz
            dzb_ref[...] = dz.astype(CDT)
            _acc(dgain_ref, _colsum(dy_ * xhat), i == 0)
            _acc(dbeta_ref, _colsum(dy_), i == 0)

        dhid = _dot_nt(dzb_ref[...], wdn_ref[...])
        gate, up = gu_ref[0], gu_ref[1]
        sg = _sigmoid(gate)
        dgu_ref[0] = (dhid * up * (sg * (1.0 + gate * (1.0 - sg)))).astype(CDT)
        dgu_ref[1] = (dhid * (gate * sg)).astype(CDT)

    row = pl.BlockSpec((tm, D), lambda i, j: (i, 0))
    vec = pl.BlockSpec((1, D), lambda i, j: (0, 0))
    dz, dzb, dgu, dgain, dbeta = _call(
        after, hidden, grid=(S // tm, 4),
        in_specs=[row, row, pl.BlockSpec((2, None, tm, FFN_B), lambda i, j: (0, j, i, 0)),
                  pl.BlockSpec((None, FFN_B, D), lambda i, j: (j, 0, 0)), vec],
        out_specs=[row, row, pl.BlockSpec((2, None, tm, FFN_B), lambda i, j: (0, j, i, 0)), vec, vec],
        out_shape=[_sds((S, D), F32), _sds((S, D), CDT), _sds((2, 4, S, FFN_B), CDT), _sds((1, D), F32),
                   _sds((1, D), F32)],
        compiler_params=_params(("arbitrary", "arbitrary")), name=name + "_hidden")(dy, z, gu, wdn, gain)

    def to_input(dz_ref, dgu_ref, wgu_ref, dx_ref):
        acc = ALPHA * dz_ref[...]
        for g in range(2):
            for j in range(4):
                acc = acc + _dot(dgu_ref[g, j], wgu_ref[g, j])
        dx_ref[...] = acc

    rows = pl.BlockSpec((tm, D), lambda i: (i, 0))
    dx = _call(
        None, to_input, grid=(S // tm,),
        in_specs=[rows, pl.BlockSpec((2, 4, tm, FFN_B), lambda i: (0, 0, i, 0)),
                  pl.BlockSpec((2, 4, FFN_B, D), lambda i: (0, 0, 0, 0))],
        out_specs=rows, out_shape=_sds((S, D), F32),
        compiler_params=_params(("arbitrary",)), name=name + "_input")(dz, dgu, wgu)
    return dx, dzb, dgu, dgain, dbeta


def _mixout_bwd(dy, z, gain, w3, du_dtype, name, after=None):
    S = dy.shape[0]
    G, Kb, _ = w3.shape
    tm = _tile(S)

    def body(dy_ref, z_ref, g_ref, w_ref, dz_ref, dzb_ref, du_ref, dgain_ref, dbeta_ref, dbias_ref):
        first = pl.program_id(0) == 0
        dy_ = dy_ref[...]
        dz, xhat = _ln_bwd(z_ref[...], g_ref[...], dy_)
        dz_ref[...] = dz
        dzb = dz.astype(CDT)
        dzb_ref[...] = dzb
        for g in range(G):
            du_ref[g] = _dot_nt(dzb, w_ref[g]).astype(du_ref.dtype)
        _acc(dgain_ref, _colsum(dy_ * xhat), first)
        _acc(dbeta_ref, _colsum(dy_), first)
        _acc(dbias_ref, _colsum(dz), first)

    row = pl.BlockSpec((tm, D), lambda i: (i, 0))
    vec = pl.BlockSpec((1, D), lambda i: (0, 0))
    return _call(
        after, body, grid=(S // tm,), in_specs=[row, row, vec, pl.BlockSpec((G, Kb, D), lambda i: (0, 0, 0))],
        out_specs=[row, row, pl.BlockSpec((G, tm, Kb), lambda i: (0, i, 0)), vec, vec, vec],
        out_shape=[_sds((S, D), F32), _sds((S, D), CDT), _sds((G, S, Kb), du_dtype)] + [_sds((1, D), F32)] * 3,
        compiler_params=_params(("arbitrary",)), name=name)(dy, z, gain, w3)


def _half_select(low):
    r = lax.broadcasted_iota(jnp.int32, (2 * ATT_HD, ATT_HD), 0)
    c = lax.broadcasted_iota(jnp.int32, (2 * ATT_HD, ATT_HD), 1)
    return (r == c + (0 if low else ATT_HD)).astype(CDT)


def _half_place(low):
    r = lax.broadcasted_iota(jnp.int32, (ATT_HD, 2 * ATT_HD), 0)
    c = lax.broadcasted_iota(jnp.int32, (ATT_HD, 2 * ATT_HD), 1)
    return (c == r + (0 if low else ATT_HD)).astype(CDT)


def _pair_lanes(even, odd):
    return (jnp.dot(even, _half_place(True), preferred_element_type=F32)
            + jnp.dot(odd, _half_place(False), preferred_element_type=F32)).astype(CDT)


def _proj_heads(a, w, bias, heads, name):
    S, K = a.shape
    N = heads * ATT_HD
    tm = _tile(S)

    def body(a_ref, w_ref, b_ref, o_ref):
        acc = (_dot(a_ref[...], w_ref[...]) + b_ref[...]).astype(CDT)
        sel = (_half_select(True), _half_select(False))
        for h in range(heads):
            pair = acc[:, (h // 2) * 2 * ATT_HD:(h // 2 + 1) * 2 * ATT_HD]
            o_ref[h] = jnp.dot(pair, sel[h % 2], preferred_element_type=F32).astype(CDT)

    return _call(
        None, body, grid=(S // tm,),
        in_specs=[pl.BlockSpec((tm, K), lambda i: (i, 0)), pl.BlockSpec((K, N), lambda i: (0, 0)),
                  pl.BlockSpec((1, N), lambda i: (0, 0))],
        out_specs=pl.BlockSpec((heads, tm, ATT_HD), lambda i: (0, i, 0)), out_shape=_sds((heads, S, ATT_HD), CDT),
        compiler_params=_params(("arbitrary",)), name=name)(a, w, bias)


def _qkv_bwd(dz, dq, dkv4, wq, wkv, name, after=None):
    S = dz.shape[0]
    tm = _tile(S)
    HK = dkv4.shape[0]
    NK = HK * ATT_HD

    def body(dz_ref, dq_ref, dkv_ref, wq_ref, wkv_ref, dx_ref, dkvn_ref, dkvb_ref):
        first = pl.program_id(0) == 0
        dkvn = jnp.concatenate([_pair_lanes(dkv_ref[2 * i].astype(CDT), dkv_ref[2 * i + 1].astype(CDT))
                                for i in range(HK // 2)], axis=-1)
        dkvn_ref[...] = dkvn
        dx_ref[...] = ALPHA * dz_ref[...] + _dot_nt(dq_ref[...], wq_ref[...]) + _dot_nt(dkvn, wkv_ref[...])
        for h in range(HK):
            _acc(dkvb_ref.at[h], _colsum(dkv_ref[h]), first)

    row = pl.BlockSpec((tm, D), lambda i: (i, 0))
    return _call(
        after, body, grid=(S // tm,),
        in_specs=[row, row, pl.BlockSpec((HK, tm, ATT_HD), lambda i: (0, i, 0)),
                  pl.BlockSpec((D, D), lambda i: (0, 0)), pl.BlockSpec((D, NK), lambda i: (0, 0))],
        out_specs=[row, pl.BlockSpec((tm, NK), lambda i: (i, 0)), pl.BlockSpec((HK, 1, ATT_HD), lambda i: (0, 0, 0))],
        out_shape=[_sds((S, D), F32), _sds((S, NK), CDT), _sds((HK, 1, ATT_HD), F32)],
        compiler_params=_params(("arbitrary",)), name=name)(dz, dq, dkv4, wq, wkv)


def _inproj_bwd(dz, dproj, wain, name, after=None):
    S = dz.shape[0]
    tm = _tile(S)
    nb = wain.shape[-1]

    def body(dz_ref, dp_ref, w_ref, dx_ref):
        acc = ALPHA * dz_ref[...]
        for j in range(N_DEV):
            acc = acc + _dot_nt(dp_ref[j // 2, :, pl.ds((j % 2) * nb, nb)], w_ref[j])
        dx_ref[...] = acc

    row = pl.BlockSpec((tm, D), lambda i: (i, 0))
    return _call(
        after, body, grid=(S // tm,),
        in_specs=[row, pl.BlockSpec((4, tm, D), lambda i: (0, i, 0)), pl.BlockSpec((N_DEV, D, nb), lambda i: (0, 0, 0))],
        out_specs=row, out_shape=_sds((S, D), F32),
        compiler_params=_params(("arbitrary",)), name=name)(dz, dproj, wain)


def _running_sum(x, reverse=False):
    rows = x.shape[0]
    row = lax.broadcasted_iota(jnp.int32, x.shape, 0)
    step = 1
    while step < rows:
        if reverse:
            x = x + jnp.where(row < rows - step, pltpu.roll(x, rows - step, 0), 0.0)
        else:
            x = x + jnp.where(row >= step, pltpu.roll(x, step, 0), 0.0)
        step *= 2
    return x


def _hg_gates(q, f, alb_ref):
    a0, a1 = alb_ref[0:1, :], alb_ref[1:2, :]
    mx = jnp.maximum(a0, a1)
    e0, e1 = jnp.exp(a0 - mx), jnp.exp(a1 - mx)
    lb = e0 / (e0 + e1)
    sig = _sigmoid(f)
    forget = lb + (1.0 - lb) * sig
    k = (1.0 - lb) * _sigmoid(-f)
    qs = q * _sigmoid(q) * (HG_DK ** -0.5)
    return qs, k, jnp.log(forget), sig, lb, forget


def _hg_intra(qs, k, b, b_scr):
    b_scr[...] = b
    bm = b_scr[pl.ds(HG_CH // 2 - 1, 1), :]
    bl = b_scr[pl.ds(HG_CH - 1, 1), :]
    eb = jnp.exp(b)
    qb = qs * eb
    e_q = jnp.exp(b - bm)
    e_k = jnp.exp(bm - b)
    e_d = jnp.exp(bl - b)
    return qb, qs * e_q, k * e_k, k * e_d, jnp.exp(bl), eb, e_q, e_k, e_d


def _hgrn_fwd(proj, alb, ngain):
    S = proj.shape[1]
    nc = S // HG_CH
    wb = HG_HPB * HG_DK

    def body(pj_ref, alb_ref, ng_ref, o_ref, y_ref, st_ref, st_scr, b_scr):
        n = pl.program_id(1)

        @pl.when(n == 0)
        def _():
            st_scr[...] = jnp.zeros_like(st_scr)

        r = lax.broadcasted_iota(jnp.int32, (HG_CH, HG_CH), 0)
        c = lax.broadcasted_iota(jnp.int32, (HG_CH, HG_CH), 1)
        causal = r >= c
        for j in range(HG_HPB):
            lanes = pl.ds(j * HG_DK, HG_DK)
            q, f, v, g = pj_ref[0, :, lanes], pj_ref[1, :, lanes], pj_ref[2, :, lanes], pj_ref[3, :, lanes]
            qs, k, logf, _, _, _ = _hg_gates(q, f, alb_ref.at[:, lanes])
            b = _running_sum(logf)
            qb, qt, kt, kd, ebl, _, _, _, _ = _hg_intra(qs, k, b, b_scr.at[j])
            st = st_scr[j]
            st_ref[j] = st
            a = jnp.where(causal, _dot_nt(qt, kt), 0.0)
            o = _dot(a, v) + _dot_nt(qb, st)
            st_scr[j] = st * ebl + _dot_tn(v, kd)
            o_ref[:, lanes] = o
            rinv = lax.rsqrt(jnp.mean(o * o, axis=-1, keepdims=True) + RMS_EPS)
            y_ref[:, lanes] = (o * rinv * ng_ref[...] * (g * _sigmoid(g))).astype(CDT)

    blk = pl.BlockSpec((HG_CH, wb), lambda h, n: (n, h))
    return _call(
        None, body, grid=(HG_H // HG_HPB, nc),
        in_specs=[pl.BlockSpec((4, HG_CH, wb), lambda h, n: (0, n, h)), pl.BlockSpec((2, wb), lambda h, n: (0, h)),
                  pl.BlockSpec((1, HG_DK), lambda h, n: (0, 0))],
        out_specs=[blk, blk, pl.BlockSpec((HG_HPB, None, HG_DK, HG_DK), lambda h, n: (h, n, 0, 0))],
        out_shape=[_sds((S, D), F32), _sds((S, D), CDT), _sds((HG_H, nc, HG_DK, HG_DK), F32)],
        scratch_shapes=[pltpu.VMEM((HG_HPB, HG_DK, HG_DK), F32), pltpu.VMEM((HG_HPB, HG_CH, HG_DK), F32)],
        compiler_params=_params(("arbitrary", "arbitrary")), name="hgrn_fwd")(proj, alb, ngain)


def _hgrn_bwd(proj, alb, ngain, o, states, dy):
    S = proj.shape[1]
    nc = S // HG_CH
    wb = HG_HPB * HG_DK

    def body(pj_ref, alb_ref, ng_ref, o_ref, st_ref, dy_ref, dpj_ref, dalb_ref, dng_ref, dst_scr, b_scr):
        h, n = pl.program_id(0), pl.program_id(1)

        @pl.when(n == 0)
        def _():
            dst_scr[...] = jnp.zeros_like(dst_scr)

        ng = ng_ref[...]
        r = lax.broadcasted_iota(jnp.int32, (HG_CH, HG_CH), 0)
        c = lax.broadcasted_iota(jnp.int32, (HG_CH, HG_CH), 1)
        causal = r >= c
        dng = None
        for j in range(HG_HPB):
            lanes = pl.ds(j * HG_DK, HG_DK)
            q, f, v, g = pj_ref[0, :, lanes], pj_ref[1, :, lanes], pj_ref[2, :, lanes], pj_ref[3, :, lanes]
            o_ = o_ref[:, lanes]
            dy_ = dy_ref[:, lanes]
            sg = _sigmoid(g)
            rinv = lax.rsqrt(jnp.mean(o_ * o_, axis=-1, keepdims=True) + RMS_EPS)
            nrm = o_ * rinv
            dr = dy_ * (g * sg)
            dg = dy_ * nrm * ng * (sg * (1.0 + g * (1.0 - sg)))
            dn = dr * ng
            do = rinv * (dn - nrm * jnp.mean(dn * nrm, axis=-1, keepdims=True))
            dng = _colsum(dr * nrm) if dng is None else dng + _colsum(dr * nrm)
            qs, k, logf, sig, lb, forget = _hg_gates(q, f, alb_ref.at[:, lanes])
            b = _running_sum(logf)
            qb, qt, kt, kd, ebl, eb, e_q, e_k, e_d = _hg_intra(qs, k, b, b_scr.at[j])
            st = st_ref[j]
            dstn = dst_scr[j]
            qt, kt, qb, kd = (t.astype(CDT).astype(F32) for t in (qt, kt, qb, kd))
            a = jnp.where(causal, _dot_nt(qt, kt), 0.0)
            da = jnp.where(causal, _dot_nt(do, v), 0.0)
            dv = _dot_tn(a, do) + _dot_nt(kd, dstn)
            dqb = _dot(do, st)
            dkd = _dot(v, dstn)
            dqt = _dot(da, kt)
            dkt = _dot_tn(da, qt)
            dbl = _colsum(dkd * kd) + ebl * _colsum(dstn * st)
            dst_scr[j] = dstn * ebl + _dot_tn(do, qb)
            dqs = dqt * e_q + dqb * eb
            dk = dkt * e_k + dkd * e_d
            db = dqt * qt + dqb * qb - dkt * kt - dkd * kd
            dlogf = _running_sum(db, reverse=True) + dbl
            dforget = dlogf / forget
            dsig = (1.0 - lb) * (dforget - dk)
            df = dsig * sig * (1.0 - sig)
            dlb = _colsum((dforget - dk) * (1.0 - sig))
            sq = _sigmoid(q)
            dq = dqs * (HG_DK ** -0.5) * (sq * (1.0 + q * (1.0 - sq)))
            dpj_ref[0, :, lanes] = dq.astype(CDT)
            dpj_ref[1, :, lanes] = df.astype(CDT)
            dpj_ref[2, :, lanes] = dv.astype(CDT)
            dpj_ref[3, :, lanes] = dg.astype(CDT)
            da0 = dlb * lb * (1.0 - lb)
            _acc(dalb_ref.at[pl.ds(0, 1), lanes], da0, n == 0)
            _acc(dalb_ref.at[pl.ds(1, 1), lanes], -da0, n == 0)
        _acc(dng_ref, dng, jnp.logical_and(h == 0, n == 0))

    blk = pl.BlockSpec((HG_CH, wb), lambda h, n: (nc - 1 - n, h))
    pj = pl.BlockSpec((4, HG_CH, wb), lambda h, n: (0, nc - 1 - n, h))
    alb_blk = pl.BlockSpec((2, wb), lambda h, n: (0, h))
    ng_blk = pl.BlockSpec((1, HG_DK), lambda h, n: (0, 0))
    return _call(
        None, body, grid=(HG_H // HG_HPB, nc),
        in_specs=[pj, alb_blk, ng_blk, blk,
                  pl.BlockSpec((HG_HPB, None, HG_DK, HG_DK), lambda h, n: (h, nc - 1 - n, 0, 0)), blk],
        out_specs=[pj, alb_blk, ng_blk],
        out_shape=[_sds((4, S, D), CDT), _sds((2, D), F32), _sds((1, HG_DK), F32)],
        scratch_shapes=[pltpu.VMEM((HG_HPB, HG_DK, HG_DK), F32), pltpu.VMEM((HG_HPB, HG_CH, HG_DK), F32)],
        compiler_params=_params(("arbitrary", "arbitrary")), name="hgrn_bwd")(proj, alb, ngain, o, states, dy)


def _slope(h):
    return 2.0 ** (-8.0 * (h + 1) / ATT_QH)


def _attn_mask(n):
    qi = lax.broadcasted_iota(jnp.int32, (WINDOW, 2 * WINDOW), 0)
    si = lax.broadcasted_iota(jnp.int32, (WINDOW, 2 * WINDOW), 1)
    dist = qi - si + WINDOW
    valid = (dist >= 0) & (dist < WINDOW) & (n * WINDOW - WINDOW + si >= 0)
    return valid, dist.astype(F32)


def _attn_probs(qh, kh, sink, slope, valid, distf):
    s = _dot_nt(qh, kh) * (ATT_HD ** -0.5) - slope * distf
    s = jnp.where(valid, s, NEG)
    m = jnp.maximum(jnp.max(s, axis=-1, keepdims=True), sink)
    e = jnp.exp(s - m)
    es = jnp.exp(sink - m)
    inv = 1.0 / (jnp.sum(e, axis=-1, keepdims=True) + es)
    return e * inv, es * inv


def _attn_specs(S):
    nb = S // WINDOW
    cur = lambda H: pl.BlockSpec((H, WINDOW, ATT_HD), lambda n: (0, n, 0))
    prev = lambda H: pl.BlockSpec((H, WINDOW, ATT_HD), lambda n: (0, jnp.maximum(n - 1, 0), 0))
    return nb, cur, prev


def _attn_fwd(q4, kv4, sinks):
    S = q4.shape[1]
    nb, cur, prev = _attn_specs(S)

    def body(sink_ref, q_ref, kvc_ref, kvp_ref, o_ref):
        valid, distf = _attn_mask(pl.program_id(0))
        outs = []
        for h in range(ATT_QH):
            kvh = h // ATT_G
            kh = jnp.concatenate([kvp_ref[kvh], kvc_ref[kvh]], axis=0)
            vh = jnp.concatenate([kvp_ref[ATT_KVH + kvh], kvc_ref[ATT_KVH + kvh]], axis=0)
            p, _ = _attn_probs(q_ref[h], kh, sink_ref[0, h], _slope(h), valid, distf)
            outs.append(_dot(p, vh).astype(CDT))
            if h % 2:
                o_ref[:, pl.ds((h - 1) * ATT_HD, 2 * ATT_HD)] = _pair_lanes(outs[h - 1], outs[h])

    return _call(
        None, body, grid=(nb,),
        in_specs=[pl.BlockSpec(memory_space=pltpu.SMEM), cur(ATT_QH), cur(2 * ATT_KVH), prev(2 * ATT_KVH)],
        out_specs=pl.BlockSpec((WINDOW, D), lambda n: (n, 0)), out_shape=_sds((S, D), CDT),
        compiler_params=_params(("arbitrary",)), name="attn_fwd")(sinks, q4, kv4, kv4)


def _attn_bwd(q4, kv4, sinks, do):
    S = q4.shape[1]
    nb, cur, prev = _attn_specs(S)

    def body(sink_ref, q_ref, kvc_ref, kvp_ref, do_ref, dq_ref, dkv_ref, dbq_ref, dsink_ref):
        n = pl.program_id(0)
        first = n == 0

        @pl.when(first)
        def _():
            dkv_ref[...] = jnp.zeros_like(dkv_ref)
            dsink_ref[...] = jnp.zeros_like(dsink_ref)

        valid, distf = _attn_mask(n)
        lane = lax.broadcasted_iota(jnp.int32, (1, 128), 1)
        rows_cur = pl.ds(pl.multiple_of(n * WINDOW, WINDOW), WINDOW)
        rows_prev = pl.ds(pl.multiple_of(jnp.maximum(n - 1, 0) * WINDOW, WINDOW), WINDOW)
        dsinks = jnp.zeros((1, 128), F32)
        sel = (_half_select(True), _half_select(False))
        for kvh in range(ATT_KVH):
            kh = jnp.concatenate([kvp_ref[kvh], kvc_ref[kvh]], axis=0)
            vh = jnp.concatenate([kvp_ref[ATT_KVH + kvh], kvc_ref[ATT_KVH + kvh]], axis=0)
            dk = dv = None
            dqs = []
            for h in range(kvh * ATT_G, (kvh + 1) * ATT_G):
                qh = q_ref[h]
                doh = jnp.dot(do_ref[:, pl.ds((h // 2) * 2 * ATT_HD, 2 * ATT_HD)], sel[h % 2],
                              preferred_element_type=F32).astype(CDT)
                p, ps = _attn_probs(qh, kh, sink_ref[0, h], _slope(h), valid, distf)
                dp = _dot_nt(doh, vh)
                dd = jnp.sum(p * dp, axis=-1, keepdims=True)
                ds = p * (dp - dd)
                dsinks = dsinks + jnp.where(lane == h, -jnp.sum(ps * dd, axis=0, keepdims=True), 0.0)
                dqh = _dot(ds, kh) * (ATT_HD ** -0.5)
                dqs.append(dqh.astype(CDT))
                _acc(dbq_ref.at[h], _colsum(dqh), first)
                dkh = _dot_tn(ds, qh) * (ATT_HD ** -0.5)
                dvh = _dot_tn(p, doh)
                dk = dkh if dk is None else dk + dkh
                dv = dvh if dv is None else dv + dvh
            for i in range(ATT_G // 2):
                lanes = pl.ds((kvh * ATT_G + 2 * i) * ATT_HD, 2 * ATT_HD)
                dq_ref[:, lanes] = _pair_lanes(dqs[2 * i], dqs[2 * i + 1])
            dkv_ref[kvh, rows_prev, :] += dk[:WINDOW]
            dkv_ref[kvh, rows_cur, :] += dk[WINDOW:]
            dkv_ref[ATT_KVH + kvh, rows_prev, :] += dv[:WINDOW]
            dkv_ref[ATT_KVH + kvh, rows_cur, :] += dv[WINDOW:]
        dsink_ref[...] += dsinks

    return _call(
        None, body, grid=(nb,),
        in_specs=[pl.BlockSpec(memory_space=pltpu.SMEM), cur(ATT_QH), cur(2 * ATT_KVH), prev(2 * ATT_KVH),
                  pl.BlockSpec((WINDOW, D), lambda n: (n, 0))],
        out_specs=[pl.BlockSpec((WINDOW, D), lambda n: (n, 0)), pl.BlockSpec((2 * ATT_KVH, S, ATT_HD), lambda n: (0, 0, 0)),
                   pl.BlockSpec((ATT_QH, 1, ATT_HD), lambda n: (0, 0, 0)), pl.BlockSpec((1, 128), lambda n: (0, 0))],
        out_shape=[_sds((S, D), CDT), _sds((2 * ATT_KVH, S, ATT_HD), F32), _sds((ATT_QH, 1, ATT_HD), F32),
                   _sds((1, 128), F32)],
        compiler_params=_params(("arbitrary",)), name="attn_bwd")(sinks, q4, kv4, kv4, do)


def _local_step(x, p, target, getw, sm, emit):
    S = x.shape[0]
    vec = lambda a: a.reshape(1, -1)
    ln_g = lambda l, k: vec(sm["ln_gain"][l, k])
    ln_b = lambda l, k: vec(sm["ln_bias"][l, k])
    xb = x.astype(CDT)
    pb = p.astype(CDT)

    proj = _mm_nn(xb, getw("a_w_in", None), (4, S, D), (None, _tile(S), 512), lambda g, i: (g // 2, i, g % 2), F32,
                  name="a_in")
    o_a, y_a, states = _hgrn_fwd(proj, sm["a_lower_bound"], sm["a_norm_gain"])
    zeros = jnp.zeros((1, D), F32)
    z = [[None] * 3 for _ in range(2)]
    xs = [[None] * 3 for _ in range(2)]
    xbs = [[None] * 3 for _ in range(2)]
    z[0][0], xs[0][0], xbs[0][0] = _mixout_ln(y_a[None], getw("a_w_out", y_a)[None], zeros, x, ln_g(0, 0), ln_b(0, 0),
                                              "a_out_ln")
    gu, hid, sgs, ups = [None, None], [None, None], [None, None], [None, None]

    def ffn_ple(l):
        wgu = getw(f"gu{l}", xbs[l][0])
        gu[l], hid[l], z[l][1], xs[l][1], xbs[l][1] = _ffn_fwd(
            xs[l][0], xbs[l][0], wgu, getw(f"dn{l}", None), ln_g(l, 1), ln_b(l, 1), f"ffn_fwd{l}")
        sgs[l], ups[l], z[l][2], xs[l][2], xbs[l][2] = _ple_fwd(
            xs[l][1], xbs[l][1], pb[l], getw(f"pg{l}", None), vec(sm["ple_b_gate"][l]), getw(f"pu{l}", None), ln_g(l, 2),
            ln_b(l, 2), f"ple_fwd{l}")

    ffn_ple(0)
    x3, x3b = xs[0][2], xbs[0][2]
    w_kv, w_q, w_bo = getw("kv_w", x3b), getw("b_w_q", None), getw("b_w_out", None)
    kv4 = _proj_heads(x3b, w_kv, vec(sm["kv_b"]), 2 * ATT_KVH, "kv_proj")
    q4 = _proj_heads(x3b, w_q, vec(sm["b_b_q"]), ATT_QH, "q_proj")
    o_b = _attn_fwd(q4, kv4, sm["b_sinks"])
    z[1][0], xs[1][0], xbs[1][0] = _mixout_ln(o_b[None], w_bo[None], sm["b_b_out"], x3, ln_g(1, 0), ln_b(1, 0),
                                              "b_out_ln")
    ffn_ple(1)
    loss, dy = _loss_fwd_bwd(xs[1][2], target)

    gs = {}
    d_ln_g = [[None] * 3 for _ in range(2)]
    d_ln_b = [[None] * 3 for _ in range(2)]
    g_bg = [None, None]

    def ffn_ple_bwd(l, dy, after=None):
        dx2, dgl, dup, d_ln_g[l][2], d_ln_b[l][2], g_bg[l] = _ple_bwd(dy, z[l][2], sgs[l], ups[l], ln_g(l, 2),
                                                                     getw(f"pg{l}", None), f"ple_bwd{l}", after=after)
        g_pg = _wgrad(xbs[l][1][None], dgl[None], f"g_ple_gate{l}")[0]
        g_pu = _wgrad(pb[l][None], dup[None], f"g_ple_up{l}")[0]
        dx1, dzb, dgu, d_ln_g[l][1], d_ln_b[l][1] = _ffn_bwd(dx2, z[l][1], gu[l], getw(f"gu{l}", None),
                                                           getw(f"dn{l}", None), ln_g(l, 1), f"ffn_bwd{l}")
        g_dn = _wgrad(hid[l], dzb[None], f"g_ffn_down{l}")
        g_gu = _wgrad(dgu.reshape(8, S, FFN_B), xbs[l][0][None], f"g_ffn_gate_up{l}")
        return dx1, emit({f"pg{l}": g_pg, f"pu{l}": g_pu, f"dn{l}": g_dn, f"gu{l}": g_gu})

    dx1, tok = ffn_ple_bwd(1, dy)
    dz, dzb, do, d_ln_g[1][0], d_ln_b[1][0], gs["b_b_out"] = _mixout_bwd(dx1, z[1][0], ln_g(1, 0), w_bo[None], CDT,
                                                                        "b_out_bwd", after=tok)
    g_bo = _wgrad(o_b[None], dzb[None], "g_b_w_out")[0]
    dq, dkv4, dbq, dsinks = _attn_bwd(q4, kv4, sm["b_sinks"], do[0])
    gs["b_b_q"] = dbq
    gs["b_sinks"] = dsinks
    g_q = _wgrad(x3b[None], dq[None], "g_b_w_q")[0]
    dx3, dkv, gs["kv_b"] = _qkv_bwd(dz, dq, dkv4, w_q, w_kv, "qkv_bwd", after=tok)
    g_kv = _wgrad(x3b[None], dkv[None], "g_kv_w")[0]
    tok = emit({"b_w_out": g_bo, "b_w_q": g_q, "kv_w": g_kv})
    dx1, tok = ffn_ple_bwd(0, dx3, tok)
    w_ao = getw("a_w_out", None)
    dz, dzb, dyr, d_ln_g[0][0], d_ln_b[0][0], _ = _mixout_bwd(dx1, z[0][0], ln_g(0, 0), w_ao[None], F32, "a_out_bwd",
                                                              after=tok)
    g_ao = _wgrad(y_a[None], dzb[None], "g_a_w_out")[0]
    dproj, gs["a_lower_bound"], gs["a_norm_gain"] = _hgrn_bwd(proj, sm["a_lower_bound"], sm["a_norm_gain"], o_a, states,
                                                              dyr[0])
    tk = lambda t: (None, t, D)
    g_ain = _mm_tn(xb[None], dproj, N_DEV, lambda g, k: (0, k, 0), lambda g, k: (g // 2, k, g % 2),
                   tk, lambda t: (None, t, 512), (N_DEV, D, 512), (None, D, 512), lambda g, k: (g, 0, 0), name="g_a_w_in")
    tok = emit({"a_w_out": g_ao, "a_w_in": g_ain})
    grad_x = _inproj_bwd(dz, dproj, getw("a_w_in", None), "a_in_bwd", after=tok)
    gs["ple_b_gate"] = jnp.concatenate(g_bg, axis=0)
    gs["ln_gain"] = jnp.stack([jnp.concatenate(r, axis=0) for r in d_ln_g])
    gs["ln_bias"] = jnp.stack([jnp.concatenate(r, axis=0) for r in d_ln_b])
    return loss, grad_x, gs


def _peer(k):
    x, y, c = lax.axis_index("x"), lax.axis_index("y"), lax.axis_index("c")
    px = 1 - x if k & 4 else x
    py = 1 - y if k & 2 else y
    pc = 1 - c if k & 1 else c
    return (px, py, pc), 4 * px + 2 * py + pc


def _my_index():
    return 4 * lax.axis_index("x") + 2 * lax.axis_index("y") + lax.axis_index("c")


def _exchange(srcs, dst_shapes, plan, name):
    n_src, n_piece = len(srcs), len(plan)

    def body(*refs):
        src_refs, dst_refs = refs[:n_src], refs[n_src:n_src + len(dst_shapes)]
        send_sems, recv_sems, local_sems = refs[n_src + len(dst_shapes):]
        me = _my_index()

        def at(ref, idx):
            return ref.at[idx] if idx else ref

        local = []
        for t, (si, sfn, di, dfn) in enumerate(plan):
            cp = pltpu.make_async_copy(at(src_refs[si], sfn(me)), at(dst_refs[di], dfn(me)), local_sems.at[t])
            cp.start()
            local.append(cp)
        sends = []
        for k in range(1, N_DEV):
            peer, pid = _peer(k)
            for t, (si, sfn, di, dfn) in enumerate(plan):
                cp = pltpu.make_async_remote_copy(
                    src_ref=at(src_refs[si], sfn(pid)), dst_ref=at(dst_refs[di], dfn(me)),
                    send_sem=send_sems.at[t * 7 + k - 1], recv_sem=recv_sems.at[t * 7 + k - 1],
                    device_id=peer, device_id_type=MESH)
                cp.start()
                sends.append(cp)
        for k in range(1, N_DEV):
            peer, pid = _peer(k)
            for t, (si, sfn, di, dfn) in enumerate(plan):
                pltpu.make_async_remote_copy(
                    src_ref=at(src_refs[si], sfn(me)), dst_ref=at(dst_refs[di], dfn(pid)),
                    send_sem=send_sems.at[t * 7 + k - 1], recv_sem=recv_sems.at[t * 7 + k - 1],
                    device_id=peer, device_id_type=MESH).wait_recv()
        for cp in sends:
            cp.wait_send()
        for cp in local:
            cp.wait()

    hbm = pl.BlockSpec(memory_space=pltpu.HBM)
    return _call(
        None, body, in_specs=[hbm] * n_src, out_specs=[hbm] * len(dst_shapes), out_shape=dst_shapes,
        scratch_shapes=[pltpu.SemaphoreType.DMA((7 * n_piece,)), pltpu.SemaphoreType.DMA((7 * n_piece,)),
                        pltpu.SemaphoreType.DMA((n_piece,))],
        name=name)(*srcs)


def _gather(shards, name):
    dsts = [_sds((N_DEV,) + a.shape, a.dtype) for a in shards]
    plan = [(i, lambda j: (), i, lambda s: (s,)) for i in range(len(shards))]
    return _exchange(shards, dsts, plan, name)


_HBM = pl.BlockSpec(memory_space=pltpu.HBM)
_SEM = pl.BlockSpec(memory_space=pltpu.SEMAPHORE)
_DATAFLOW = pltpu.SideEffectType.DATAFLOW_SIDE_EFFECTING


def _piece_copy(mode, src, land, send_sems, recv_sems, t, k, sender, receiver, peer):
    return pltpu.make_async_remote_copy(
        src_ref=src if mode == "gather" else src.at[receiver], dst_ref=land.at[sender],
        send_sem=send_sems.at[t * 7 + k - 1], recv_sem=recv_sems.at[t * 7 + k - 1], device_id=peer, device_id_type=MESH)


def _sequencer_exchange(srcs, mode, name, collective_id, after=None):
    n = len(srcs)
    land_shapes = [((N_DEV,) + a.shape) if mode == "gather" else a.shape for a in srcs]
    extra = [] if after is None else [after]

    def body(*refs):
        src_refs, land_refs = refs[:n], refs[n + len(extra):2 * n + len(extra)]
        send_sems, recv_sems, local_sems = refs[2 * n + len(extra):]
        barrier = pltpu.get_barrier_semaphore()
        for k in range(1, N_DEV):
            pl.semaphore_signal(barrier, inc=1, device_id=_peer(k)[0], device_id_type=MESH)
        pl.semaphore_wait(barrier, N_DEV - 1)
        me = _my_index()
        local = []
        for i in range(n):
            cp = pltpu.make_async_copy(src_refs[i] if mode == "gather" else src_refs[i].at[me], land_refs[i].at[me],
                                       local_sems.at[i])
            cp.start()
            local.append(cp)
        for k in range(1, N_DEV):
            peer, pid = _peer(k)
            for t in range(n):
                _piece_copy(mode, src_refs[t], land_refs[t], send_sems, recv_sems, t, k, me, pid, peer).start()
        for k in range(1, N_DEV):
            peer, pid = _peer(k)
            for t in range(n):
                _piece_copy(mode, src_refs[t], land_refs[t], send_sems, recv_sems, t, k, pid, me, peer).wait_recv()
        for k in range(1, N_DEV):
            peer, pid = _peer(k)
            for t in range(n):
                _piece_copy(mode, src_refs[t], land_refs[t], send_sems, recv_sems, t, k, me, pid, peer).wait_send()
        for cp in local:
            cp.wait()

    return pl.kernel(
        body, out_type=[_sds(s, a.dtype) for s, a in zip(land_shapes, srcs)],
        mesh=plsc.ScalarSubcoreMesh(axis_name="sequencer", num_cores=1),
        scratch_types=[pltpu.SemaphoreType.DMA((7 * n,)), pltpu.SemaphoreType.DMA((7 * n,)), pltpu.SemaphoreType.DMA((n,))],
        compiler_params=pltpu.CompilerParams(collective_id=collective_id), name=name)(*srcs, *extra)


def _sequencer_gather(srcs, name, collective_id, after=None):
    n = len(srcs)
    extra = [] if after is None else [after]

    def body(*refs):
        src_refs, land_refs = refs[:n], refs[n + len(extra):2 * n + len(extra)]
        send_sems, recv_sems, local_sems = refs[2 * n + len(extra):]
        x, y, c = lax.axis_index("x"), lax.axis_index("y"), lax.axis_index("c")
        sibling = (x, y, 1 - c)
        chips = [(1 - x, y), (x, 1 - y), (1 - x, 1 - y)]
        index = lambda px, py, pc: 4 * px + 2 * py + pc
        barrier = pltpu.get_barrier_semaphore()
        for peer in [sibling] + [(*chip, c) for chip in chips]:
            pl.semaphore_signal(barrier, inc=1, device_id=peer, device_id_type=MESH)
        pl.semaphore_wait(barrier, 4)

        def copy(t, k, slot, to, src=None):
            return pltpu.make_async_remote_copy(
                src_ref=land_refs[t].at[slot] if src is None else src, dst_ref=land_refs[t].at[slot],
                send_sem=send_sems.at[7 * t + k], recv_sem=recv_sems.at[7 * t + k], device_id=to, device_id_type=MESH)

        me = index(x, y, c)
        local = []
        for t in range(n):
            cp = pltpu.make_async_copy(src_refs[t], land_refs[t].at[me], local_sems.at[t])
            cp.start()
            local.append(cp)
        sends = []
        for t in range(n):
            sends.append(copy(t, 0, me, sibling, src=src_refs[t]))
            sends += [copy(t, 1 + j, me, (*chip, c), src=src_refs[t]) for j, chip in enumerate(chips)]
        for cp in sends:
            cp.start()
        for j, chip in enumerate(chips):
            for t in range(n):
                copy(t, 1 + j, index(*chip, c), sibling, src=src_refs[t]).wait_recv()
                passed = copy(t, 4 + j, index(*chip, c), sibling)
                passed.start()
                sends.append(passed)
        for t in range(n):
            copy(t, 0, index(x, y, 1 - c), sibling, src=src_refs[t]).wait_recv()
        for j, chip in enumerate(chips):
            for t in range(n):
                copy(t, 4 + j, index(*chip, 1 - c), sibling, src=src_refs[t]).wait_recv()
        for cp in sends:
            cp.wait_send()
        for cp in local:
            cp.wait()

    return pl.kernel(
        body, out_type=[_sds((N_DEV,) + a.shape, a.dtype) for a in srcs],
        mesh=plsc.ScalarSubcoreMesh(axis_name="sequencer", num_cores=1),
        scratch_types=[pltpu.SemaphoreType.DMA((7 * n,)), pltpu.SemaphoreType.DMA((7 * n,)), pltpu.SemaphoreType.DMA((n,))],
        compiler_params=pltpu.CompilerParams(collective_id=collective_id), name=name)(*srcs, *extra)


def _xstart(groups, mode, name, after=None):
    flat = [a for g in groups for a in g]
    n, ng = len(flat), len(groups)
    land_shapes = [((N_DEV,) + a.shape) if mode == "gather" else a.shape for a in flat]
    first = [sum(len(g) for g in groups[:i]) for i in range(ng)]
    extra = [] if after is None else [after]

    def body(*refs):
        srcs, lands = refs[:n], refs[n:2 * n]
        sems = refs[2 * n + len(extra):2 * n + len(extra) + 2 * ng]
        tok_ref, local_sems = refs[-2], refs[-1]
        me = _my_index()
        local = []
        for i in range(n):
            cp = pltpu.make_async_copy(srcs[i] if mode == "gather" else srcs[i].at[me], lands[i].at[me], local_sems.at[i])
            cp.start()
            local.append(cp)
        for cp in local:
            cp.wait()
        for gi, g in enumerate(groups):
            for k in range(1, N_DEV):
                peer, pid = _peer(k)
                for t in range(len(g)):
                    i = first[gi] + t
                    _piece_copy(mode, srcs[i], lands[i], sems[2 * gi], sems[2 * gi + 1], t, k, me, pid, peer).start()
        tok_ref[...] = jnp.zeros_like(tok_ref)

    sem_shapes = []
    for g in groups:
        sem_shapes += [pltpu.SemaphoreType.DMA((7 * len(g),))] * 2
    thru = [pltpu.HBM(a.shape, a.dtype) for a in flat] + [pltpu.HBM(s, a.dtype) for s, a in zip(land_shapes, flat)]
    outs = pl.pallas_call(
        body, in_specs=[_HBM] * (2 * n) + [pl.BlockSpec(memory_space=pl.ANY)] * len(extra),
        out_specs=[_SEM] * (2 * ng) + [_HBM] * (2 * n) + [pl.BlockSpec(memory_space=pltpu.VMEM)],
        out_shape=sem_shapes + thru + [_sds((8, 128), F32)],
        input_output_aliases={i: 2 * ng + i for i in range(2 * n)},
        scratch_shapes=[pltpu.SemaphoreType.DMA((n,))],
        compiler_params=pltpu.CompilerParams(has_side_effects=_DATAFLOW), name=name)(
            *[pltpu.with_memory_space_constraint(a, pltpu.HBM) for a in flat],
            *[pltpu.with_memory_space_constraint(lax.empty(s, a.dtype), pltpu.HBM) for s, a in zip(land_shapes, flat)], *extra)
    sems, srcs_thru, lands_thru = outs[:2 * ng], outs[2 * ng:2 * ng + n], outs[2 * ng + n:2 * ng + 2 * n]
    handles = [(sems[2 * gi], sems[2 * gi + 1], srcs_thru[first[gi]:first[gi] + len(g)],
                lands_thru[first[gi]:first[gi] + len(g)]) for gi, g in enumerate(groups)]
    return handles, outs[-1]


def _xwait(handle, mode, after, name):
    send_sems, recv_sems, srcs_thru, lands_thru = handle
    n = len(srcs_thru)

    def body(*refs):
        srcs, lands, send, recv = refs[:n], refs[n:2 * n], refs[2 * n], refs[2 * n + 1]
        me = _my_index()
        for k in range(1, N_DEV):
            peer, pid = _peer(k)
            for t in range(n):
                _piece_copy(mode, srcs[t], lands[t], send, recv, t, k, pid, me, peer).wait_recv()
        for k in range(1, N_DEV):
            peer, pid = _peer(k)
            for t in range(n):
                _piece_copy(mode, srcs[t], lands[t], send, recv, t, k, me, pid, peer).wait_send()

    extra = [] if after is None else [after]
    outs = pl.pallas_call(
        body, in_specs=[_HBM] * (2 * n) + [_SEM, _SEM] + [pl.BlockSpec(memory_space=pl.ANY)] * len(extra),
        out_specs=[_HBM] * (2 * n),
        out_shape=[pltpu.HBM(a.shape, a.dtype) for a in list(srcs_thru) + list(lands_thru)],
        input_output_aliases={i: i for i in range(2 * n)},
        compiler_params=pltpu.CompilerParams(has_side_effects=_DATAFLOW), name=name)(
            *srcs_thru, *lands_thru, send_sems, recv_sems, *extra)
    return outs[n:]


def _adamw(w, g, m, v):
    m = ADAM_B1 * m + (1.0 - ADAM_B1) * g
    v = ADAM_B2 * v + (1.0 - ADAM_B2) * (g * g)
    m_hat = m / (1.0 - ADAM_B1 ** ADAM_STEP)
    v_hat = v / (1.0 - ADAM_B2 ** ADAM_STEP)
    delta = -ADAM_LR * (m_hat / (jnp.sqrt(v_hat) + ADAM_EPS) + ADAM_WD * w)
    return delta, m, v


def _adam_big(w, parts, m, v, name, after=None):
    L, R, C = w.shape
    tr = _tile(R, (256, 128, 176, 64, 32, 16))
    nr = R // tr

    def body(w_ref, *refs):
        p_refs, (m_ref, v_ref, g_ref, d_ref, mo_ref, vo_ref) = refs[:L], refs[L:]
        for l in range(L):
            @pl.when(pl.program_id(0) == l)
            def _(p_ref=p_refs[l]):
                g = p_ref[0].astype(F32)
                for s in range(1, N_DEV):
                    g = g + p_ref[s].astype(F32)
                g_ref[...] = g
                d_ref[...], mo_ref[...], vo_ref[...] = _adamw(w_ref[...], g, m_ref[...], v_ref[...])

    row = pl.BlockSpec((None, tr, C), lambda l, i: (l, i, 0))
    park = lambda l_of: (lambda l, i: (0, jnp.where(l == l_of, i, 0 if l_of else nr - 1), 0))
    return _call(
        after, body, grid=(L, nr),
        in_specs=[row] + [pl.BlockSpec((N_DEV, tr, C), park(l)) for l in range(L)] + [row, row],
        out_specs=[row] * 4, out_shape=[_sds((L, R, C), F32)] * 4,
        compiler_params=_params(("arbitrary", "arbitrary")), name=name)(w, *parts, m, v)


SMALL = (("a_lower_bound", 2, True), ("ln_gain", 6, True), ("ln_bias", 6, True), ("a_norm_gain", 1, False),
         ("kv_b", 4, False), ("b_b_q", 8, False), ("b_sinks", 1, False), ("b_b_out", 8, False), ("ple_b_gate", 16, False))
SUBLANES = 8


def _slot(r):
    return -(-r // SUBLANES) * SUBLANES


SMALL_ROWS = sum(_slot(r) for _, r, _ in SMALL)
PART_ROWS = sum(r * N_DEV if sh else _slot(r) for _, r, sh in SMALL)


def _pack_rows(a, rows):
    flat = a.reshape(-1)
    return jnp.pad(flat, (0, rows * 128 - flat.size)).reshape(rows, 128)


def _pack_small(d):
    return jnp.concatenate([_pack_rows(d[n], _slot(r)) for n, r, _ in SMALL], axis=0)


def _unpack_small(packed, like):
    out, r0 = {}, 0
    for n, r, _ in SMALL:
        out[n] = packed[r0:r0 + r].reshape(-1)[:like[n].size].reshape(like[n].shape)
        r0 += _slot(r)
    return out


def _pack_partials(gs):
    blocks = []
    for n, r, sharded in SMALL:
        if sharded:
            blocks.append(gs[n].reshape(r * N_DEV, 128))
        else:
            blocks.append(_pack_rows(gs[n], _slot(r)))
    return jnp.concatenate(blocks, axis=0)


def _adam_small(parts, w, m, v, after=None):
    def body(p_ref, w_ref, m_ref, v_ref, g_ref, d_ref, mo_ref, vo_ref, tot):
        me = _my_index()
        t = p_ref[0]
        for s in range(1, N_DEV):
            t = t + p_ref[s]
        tot[...] = t
        g_ref[...] = jnp.zeros_like(g_ref)
        src, dst = 0, 0
        for _, r, sharded in SMALL:
            if sharded:
                for i in range(r):
                    g_ref[pl.ds(dst + i, 1), :] = tot[pl.ds(src + i * N_DEV + me, 1), :]
                src += r * N_DEV
            else:
                g_ref[pl.ds(dst, _slot(r)), :] = tot[pl.ds(src, _slot(r)), :]
                src += _slot(r)
            dst += _slot(r)
        d_ref[...], mo_ref[...], vo_ref[...] = _adamw(w_ref[...], g_ref[...], m_ref[...], v_ref[...])

    full = pl.BlockSpec((SMALL_ROWS, 128), lambda: (0, 0))
    return _call(
        after, body, in_specs=[pl.BlockSpec((N_DEV, PART_ROWS, 128), lambda: (0, 0, 0)), full, full, full],
        out_specs=[full] * 4, out_shape=[_sds((SMALL_ROWS, 128), F32)] * 4,
        scratch_shapes=[pltpu.VMEM((PART_ROWS, 128), F32)], name="adam_small")(parts, w, m, v)


WEIGHTS = ("a_w_in", "a_lower_bound", "a_norm_gain", "a_w_out", "kv_w", "kv_b", "b_w_q", "b_b_q", "b_sinks", "b_w_out",
           "b_b_out", "ffn_w_gate_up", "ffn_w_down", "ple_w_up", "ple_w_gate", "ple_b_gate", "ln_gain", "ln_bias")


GATHER_GROUPS = (("a_w_in",), ("a_w_out", "gu0", "dn0", "pu0", "pg0"), ("kv_w", "b_w_q", "b_w_out"),
                 ("gu1", "dn1", "pu1", "pg1"))
KERNEL_LAYOUT = {
    "a_w_in": lambda a: a,
    "a_w_out": lambda a: a.reshape(D, D),
    "kv_w": lambda a: a.reshape(D, 2 * ATT_KVH * ATT_HD),
    "b_w_q": lambda a: a.reshape(D, D),
    "b_w_out": lambda a: a.reshape(D, D),
    "gu": lambda a: a.reshape(2, 4, FFN_B, D),
    "dn": lambda a: a.reshape(4, FFN_B, D),
    "pu": lambda a: a,
    "pg": lambda a: a.reshape(D, D),
}
_row_blocks = lambda a: a.reshape(N_DEV, -1, a.shape[-1])
OWNER_BLOCKS = {
    "a_w_in": lambda g: g,
    "a_w_out": _row_blocks,
    "kv_w": _row_blocks,
    "b_w_q": _row_blocks,
    "b_w_out": _row_blocks,
    "gu": lambda g: g,
    "dn": lambda g: _row_blocks(g.reshape(FFN_H, D)),
    "pu": lambda g: g.reshape(PLE_DIM, N_DEV, 128).transpose(1, 0, 2),
    "pg": _row_blocks,
}
ADAM_AFTER = {1: (("kv_w", "kv_w"), ("b_w_q", "b_w_q"), ("b_w_out", "b_w_out")),
              2: (("ffn_w_gate_up", "gu"), ("ffn_w_down", "dn"), ("ple_w_up", "pu"), ("ple_w_gate", "pg")),
              3: (("a_w_out", "a_w_out"), ("a_w_in", "a_w_in"))}


def kernel(x, p, a_w_in, a_lower_bound, a_norm_gain, a_w_out, kv_w, kv_b, b_w_q, b_b_q, b_sinks, b_w_out, b_b_out, ffn_w_gate_up, ffn_w_down, ple_w_up, ple_w_gate, ple_b_gate, ln_gain, ln_bias, loss_target, m_a_w_in, m_a_lower_bound, m_a_norm_gain, m_a_w_out, m_kv_w, m_kv_b, m_b_w_q, m_b_b_q, m_b_sinks, m_b_w_out, m_b_b_out, m_ffn_w_gate_up, m_ffn_w_down, m_ple_w_up, m_ple_w_gate, m_ple_b_gate, m_ln_gain, m_ln_bias, v_a_w_in, v_a_lower_bound, v_a_norm_gain, v_a_w_out, v_kv_w, v_kv_b, v_b_w_q, v_b_b_q, v_b_sinks, v_b_w_out, v_b_b_out, v_ffn_w_gate_up, v_ffn_w_down, v_ple_w_up, v_ple_w_gate, v_ple_b_gate, v_ln_gain, v_ln_bias):
    given = dict(locals())
    w = {n: given[n] for n in WEIGHTS}
    m = {n: given["m_" + n] for n in WEIGHTS}
    v = {n: given["v_" + n] for n in WEIGHTS}
    small_sharded = jnp.concatenate([_pack_rows(a, SUBLANES) for a in (a_lower_bound, ln_gain, ln_bias)], axis=0)
    (g_small,) = _gather([small_sharded], "gather_small")
    shards = {"a_w_in": a_w_in[0], "a_w_out": a_w_out[0], "kv_w": kv_w, "b_w_q": b_w_q[0], "b_w_out": b_w_out[0]}
    for l in range(2):
        shards.update({f"gu{l}": ffn_w_gate_up[l].T, f"dn{l}": ffn_w_down[l], f"pu{l}": ple_w_up[l], f"pg{l}": ple_w_gate[l]})
    gathered = {}
    for gi, g in enumerate(GATHER_GROUPS):
        lands = _sequencer_gather([shards[n].astype(CDT) for n in g], f"gather{gi}", gi, after=None if gi else g_small)
        for n, a in zip(g, lands):
            gathered[n] = KERNEL_LAYOUT[n.rstrip("01")](a)

    def getw(key, after):
        return gathered[key]

    full_rows = lambda r0, r: g_small[:, r0:r0 + r].transpose(1, 0, 2).reshape(r, D)
    sm = {"a_lower_bound": full_rows(0, 2), "ln_gain": full_rows(SUBLANES, 6).reshape(2, 3, D),
          "ln_bias": full_rows(2 * SUBLANES, 6).reshape(2, 3, D), "a_norm_gain": a_norm_gain, "kv_b": kv_b,
          "b_b_q": b_b_q[0], "b_sinks": b_sinks, "b_b_out": b_b_out, "ple_b_gate": ple_b_gate}

    scatters = []

    def emit(grads):
        names = list(grads)
        blocks = [OWNER_BLOCKS[n.rstrip("01")](grads[n]) for n in names]
        lands = _sequencer_exchange(blocks, "scatter", f"scatter{len(scatters)}", len(GATHER_GROUPS) + len(scatters))
        scatters.append(dict(zip(names, lands)))
        return blocks

    loss, grad_x, gs = _local_step(x[0], p[:, 0], loss_target[0], getw, sm, emit)

    (parts_small,) = _gather([_pack_partials(gs)], "gather_small_grads")
    packed = _adam_small(parts_small, _pack_small(w), _pack_small(m), _pack_small(v),
                         after=[grad_x] + list(scatters[0].values()))
    small_out = [_unpack_small(a, w) for a in packed]
    out = {n: [s[n] for s in small_out] for n, _, _ in SMALL}

    parts, last = {}, grad_x
    for i, landed in enumerate(scatters):
        parts.update(landed)
        for n, key in ADAM_AFTER.get(i, ()):
            lrc = (1,) * (3 - w[n].ndim) + w[n].shape
            layers = [parts[key]] if key in parts else [parts[key + "0"], parts[key + "1"]]
            shard = (lambda a: a.reshape(lrc).swapaxes(1, 2)) if key == "gu" else (lambda a: a.reshape(lrc))
            res = _adam_big(shard(w[n]), layers, shard(m[n]), shard(v[n]), "adam_" + n, after=[last])
            out[n] = [(r.swapaxes(1, 2) if key == "gu" else r).reshape(w[n].shape) for r in res]
            last = res[3]

    loss = lax.psum(loss[0, 0], ("x", "y", "c"))
    res = [loss, grad_x[None]]
    for i in range(4):
        res += [out[n][i] for n in WEIGHTS]
    return tuple(res)
```

```python
import jax
import jax.numpy as jnp
from jax import lax
from jax.experimental import pallas as pl
from jax.experimental.pallas import tpu as pltpu
from jax.experimental.pallas import tpu_sc as plsc

F32 = jnp.float32
CDT = jnp.bfloat16

N_DEV = 8
D = 1024
HG_H, HG_DK, HG_CH = 8, 128, 64
HG_HPB = 4
ATT_HD, ATT_QH, ATT_KVH, ATT_G, WINDOW = 64, 16, 4, 4, 128
FFN_H = 2816
FFN_B = FFN_H // 4
PLE_DIM = 256
ALPHA = (2.0 * 2) ** 0.25
LN_EPS = 1e-5
RMS_EPS = 1e-6
ADAM_LR, ADAM_B1, ADAM_B2, ADAM_EPS, ADAM_WD, ADAM_STEP = 0.001, 0.9, 0.999, 1e-08, 0.01, 10
ROW_TILES = (512, 256, 128, 64)
VMEM_LIMIT = 48 * 1024 * 1024
NEG = -1e30

MESH = pl.DeviceIdType.MESH


def _tile(n, cands=ROW_TILES):
    for t in cands:
        if n % t == 0:
            return t
    return n


def _sds(shape, dtype):
    return jax.ShapeDtypeStruct(tuple(shape), dtype)


def _params(sem):
    return pltpu.CompilerParams(dimension_semantics=sem, vmem_limit_bytes=VMEM_LIMIT)


def _dot(a, b):
    return jnp.dot(a.astype(CDT), b.astype(CDT), preferred_element_type=F32)


def _dot_nt(a, b):
    return lax.dot_general(a.astype(CDT), b.astype(CDT), (((1,), (1,)), ((), ())), preferred_element_type=F32)


def _dot_tn(a, b):
    return lax.dot_general(a.astype(CDT), b.astype(CDT), (((0,), (0,)), ((), ())), preferred_element_type=F32)


def _sigmoid(x):
    return jax.nn.sigmoid(x)


def _ln_fwd(z, g, b):
    mu = jnp.mean(z, axis=-1, keepdims=True)
    zc = z - mu
    var = jnp.mean(zc * zc, axis=-1, keepdims=True)
    return zc * lax.rsqrt(var + LN_EPS) * g + b


def _ln_bwd(z, g, dy):
    mu = jnp.mean(z, axis=-1, keepdims=True)
    zc = z - mu
    var = jnp.mean(zc * zc, axis=-1, keepdims=True)
    rstd = lax.rsqrt(var + LN_EPS)
    xhat = zc * rstd
    dxh = dy * g
    dz = rstd * (dxh - jnp.mean(dxh, axis=-1, keepdims=True) - xhat * jnp.mean(dxh * xhat, axis=-1, keepdims=True))
    return dz, xhat


def _colsum(x):
    return jnp.sum(x, axis=0, keepdims=True)


def _acc(ref, val, first):
    @pl.when(first)
    def _():
        ref[...] = val

    @pl.when(jnp.logical_not(first))
    def _():
        ref[...] += val


def _zero_at(ref, first):
    @pl.when(first)
    def _():
        ref[...] = jnp.zeros_like(ref)


def _call(after, body, **kw):
    after = [] if after is None else list(after)
    specs = list(kw["in_specs"])
    kw["in_specs"] = [pl.BlockSpec(memory_space=pl.ANY)] * len(after) + specs

    def ordered_body(*refs):
        body(*refs[len(after):])

    call = pl.pallas_call(ordered_body, **kw)

    def pinned(*args):
        args = [a if s.memory_space is not None else pltpu.with_memory_space_constraint(a, pltpu.HBM)
                for a, s in zip(args, specs)]
        return call(*after, *args)

    return pinned


def _mm_nn(a, b3, out_shape, oblock, omap, out_dtype, bias3=None, name="mm_nn"):
    M, K = a.shape
    G, _, Nb = b3.shape
    tm = _tile(M)

    def body(a_ref, b_ref, *rest):
        o_ref = rest[-1]
        acc = _dot(a_ref[...], b_ref[...])
        if bias3 is not None:
            acc = acc + rest[0][...]
        o_ref[...] = acc.astype(o_ref.dtype)

    in_specs = [pl.BlockSpec((tm, K), lambda g, i: (i, 0)), pl.BlockSpec((None, K, Nb), lambda g, i: (g, 0, 0))]
    args = [a, b3]
    if bias3 is not None:
        in_specs.append(pl.BlockSpec((None, 1, Nb), lambda g, i: (g, 0, 0)))
        args.append(bias3)
    return _call(
        None, body, grid=(G, M // tm), in_specs=in_specs, out_specs=pl.BlockSpec(oblock, omap),
        out_shape=_sds(out_shape, out_dtype), compiler_params=_params(("arbitrary", "arbitrary")), name=name)(*args)


def _mm_tn(a3, b3, G, amap, bmap, ablock, bblock, out_shape, oblock, omap, name="mm_tn"):
    S = a3.shape[1]

    def body(a_ref, b_ref, o_ref):
        o_ref[...] = _dot_tn(a_ref[...], b_ref[...]).astype(o_ref.dtype)

    return _call(
        None, body, grid=(G, 1),
        in_specs=[pl.BlockSpec(ablock(S), amap), pl.BlockSpec(bblock(S), bmap)],
        out_specs=pl.BlockSpec(oblock, omap), out_shape=_sds(out_shape, CDT),
        compiler_params=_params(("arbitrary", "arbitrary")), name=name)(a3, b3)


def _wgrad(a3, b3, name):
    Ga, S, M = a3.shape
    Gb, _, N = b3.shape
    G = max(Ga, Gb)
    return _mm_tn(
        a3, b3, G,
        (lambda g, k: (g, k, 0)) if Ga > 1 else (lambda g, k: (0, k, 0)),
        (lambda g, k: (g, k, 0)) if Gb > 1 else (lambda g, k: (0, k, 0)),
        lambda tk: (None, tk, M), lambda tk: (None, tk, N),
        (G, M, N), (None, M, N), lambda g, k: (g, 0, 0), name=name)


def _mixout_ln(u3, w3, bias, xin, gain, beta, name):
    G, S, Kb = u3.shape
    tm = _tile(S)

    def body(u_ref, w_ref, b_ref, x_ref, g_ref, be_ref, z_ref, xo_ref, xob_ref):
        h = b_ref[...] + _dot(u_ref[0], w_ref[0])
        for g in range(1, G):
            h = h + _dot(u_ref[g], w_ref[g])
        z = ALPHA * x_ref[...] + h
        z_ref[...] = z
        y = _ln_fwd(z, g_ref[...], be_ref[...])
        xo_ref[...] = y
        xob_ref[...] = y.astype(CDT)

    row = pl.BlockSpec((tm, D), lambda i: (i, 0))
    vec = pl.BlockSpec((1, D), lambda i: (0, 0))
    return _call(
        None, body, grid=(S // tm,),
        in_specs=[pl.BlockSpec((G, tm, Kb), lambda i: (0, i, 0)), pl.BlockSpec((G, Kb, D), lambda i: (0, 0, 0)),
                  vec, row, vec, vec],
        out_specs=[row, row, row], out_shape=[_sds((S, D), F32), _sds((S, D), F32), _sds((S, D), CDT)],
        compiler_params=_params(("arbitrary",)), name=name)(u3, w3, bias, xin, gain, beta)


def _ffn_fwd(xin, xin_b, wgu, wdn, gain, beta, name):
    S = xin.shape[0]
    tm = _tile(S)

    def hidden(xb_ref, wgu_ref, gu_ref, hid_ref):
        xb = xb_ref[...]
        gate = _dot_nt(xb, wgu_ref[0])
        up = _dot_nt(xb, wgu_ref[1])
        gu_ref[0] = gate
        gu_ref[1] = up
        hid_ref[...] = (gate * _sigmoid(gate) * up).astype(CDT)

    gu, hid = _call(
        None, hidden, grid=(S // tm, 4),
        in_specs=[pl.BlockSpec((tm, D), lambda i, j: (i, 0)), pl.BlockSpec((2, None, FFN_B, D), lambda i, j: (0, j, 0, 0))],
        out_specs=[pl.BlockSpec((2, None, tm, FFN_B), lambda i, j: (0, j, i, 0)),
                   pl.BlockSpec((None, tm, FFN_B), lambda i, j: (j, i, 0))],
        out_shape=[_sds((2, 4, S, FFN_B), F32), _sds((4, S, FFN_B), CDT)],
        compiler_params=_params(("arbitrary", "arbitrary")), name=name + "_hidden")(xin_b, wgu)

    def down(x_ref, hid_ref, wdn_ref, g_ref, be_ref, z_ref, xo_ref, xob_ref):
        z = ALPHA * x_ref[...]
        for j in range(4):
            z = z + _dot(hid_ref[j], wdn_ref[j])
        z_ref[...] = z
        y = _ln_fwd(z, g_ref[...], be_ref[...])
        xo_ref[...] = y
        xob_ref[...] = y.astype(CDT)

    row = pl.BlockSpec((tm, D), lambda i: (i, 0))
    vec = pl.BlockSpec((1, D), lambda i: (0, 0))
    z, xo, xob = _call(
        None, down, grid=(S // tm,),
        in_specs=[row, pl.BlockSpec((4, tm, FFN_B), lambda i: (0, i, 0)), pl.BlockSpec((4, FFN_B, D), lambda i: (0, 0, 0)),
                  vec, vec],
        out_specs=[row, row, row], out_shape=[_sds((S, D), F32), _sds((S, D), F32), _sds((S, D), CDT)],
        compiler_params=_params(("arbitrary",)), name=name + "_down")(xin, hid, wdn, gain, beta)
    return gu, hid, z, xo, xob


def _ple_fwd(xin, xin_b, p_b, wpg, bgate, wpu, gain, beta, name):
    S = xin.shape[0]
    tm = _tile(S)

    def body(x_ref, xb_ref, p_ref, wpg_ref, bg_ref, wpu_ref, g_ref, be_ref, sg_ref, up_ref, z_ref, xo_ref, xob_ref):
        sg = _sigmoid(_dot(xb_ref[...], wpg_ref[...]) + bg_ref[...])
        pb = p_ref[...]
        up = jnp.concatenate([_dot(pb, wpu_ref[j]) for j in range(N_DEV)], axis=-1)
        sg_ref[...] = sg
        up_ref[...] = up
        z = ALPHA * x_ref[...] + sg * up
        z_ref[...] = z
        y = _ln_fwd(z, g_ref[...], be_ref[...])
        xo_ref[...] = y
        xob_ref[...] = y.astype(CDT)

    row = pl.BlockSpec((tm, D), lambda i: (i, 0))
    vec = pl.BlockSpec((1, D), lambda i: (0, 0))
    return _call(
        None, body, grid=(S // tm,),
        in_specs=[row, row, pl.BlockSpec((tm, PLE_DIM), lambda i: (i, 0)), pl.BlockSpec((D, D), lambda i: (0, 0)), vec,
                  pl.BlockSpec((N_DEV, PLE_DIM, D // N_DEV), lambda i: (0, 0, 0)), vec, vec],
        out_specs=[row] * 5,
        out_shape=[_sds((S, D), F32)] * 4 + [_sds((S, D), CDT)],
        compiler_params=_params(("arbitrary",)), name=name)(xin, xin_b, p_b, wpg, bgate, wpu, gain, beta)


def _loss_fwd_bwd(y, target):
    S = y.shape[0]
    tm = _tile(S)

    def body(y_ref, t_ref, l_ref, dy_ref):
        e = y_ref[...] - t_ref[...]
        dy_ref[...] = e * (1.0 / D)
        part = 0.5 * jnp.sum(jnp.sum(e * e, axis=-1, keepdims=True) * (1.0 / D), axis=0, keepdims=True)
        _acc(l_ref, part, pl.program_id(0) == 0)

    row = pl.BlockSpec((tm, D), lambda i: (i, 0))
    return _call(
        None, body, grid=(S // tm,), in_specs=[row, row],
        out_specs=[pl.BlockSpec((1, 1), lambda i: (0, 0)), row],
        out_shape=[_sds((1, 1), F32), _sds((S, D), F32)],
        compiler_params=_params(("arbitrary",)), name="loss")(y, target)


def _ple_bwd(dy, z, sg, up, gain, wpg, name, after=None):
    S = dy.shape[0]
    tm = _tile(S)

    def body(dy_ref, z_ref, sg_ref, up_ref, g_ref, wpg_ref, dx_ref, dgl_ref, dup_ref, dgain_ref, dbeta_ref, dbg_ref):
        first = pl.program_id(0) == 0
        dy_ = dy_ref[...]
        dz, xhat = _ln_bwd(z_ref[...], g_ref[...], dy_)
        sg_ = sg_ref[...]
        dgl = dz * up_ref[...] * sg_ * (1.0 - sg_)
        dgl_ref[...] = dgl.astype(CDT)
        dup_ref[...] = (dz * sg_).astype(CDT)
        dx_ref[...] = ALPHA * dz + _dot_nt(dgl, wpg_ref[...])
        _acc(dgain_ref, _colsum(dy_ * xhat), first)
        _acc(dbeta_ref, _colsum(dy_), first)
        _acc(dbg_ref, _colsum(dgl), first)

    row = pl.BlockSpec((tm, D), lambda i: (i, 0))
    vec = pl.BlockSpec((1, D), lambda i: (0, 0))
    return _call(
        after, body, grid=(S // tm,), in_specs=[row, row, row, row, vec, pl.BlockSpec((D, D), lambda i: (0, 0))],
        out_specs=[row, row, row, vec, vec, vec],
        out_shape=[_sds((S, D), F32), _sds((S, D), CDT), _sds((S, D), CDT)] + [_sds((1, D), F32)] * 3,
        compiler_params=_params(("arbitrary",)), name=name)(dy, z, sg, up, gain, wpg)


def _ffn_bwd(dy, z, gu, wgu, wdn, gain, name, after=None):
    S = dy.shape[0]
    tm = _tile(S)

    def hidden(dy_ref, z_ref, gu_ref, wdn_ref, g_ref, dz_ref, dzb_ref, dgu_ref, dgain_ref, dbeta_ref):
        i, j = pl.program_id(0), pl.program_id(1)

        @pl.when(j == 0)
        def _():
            dy_ = dy_ref[...]
            dz, xhat = _ln_bwd(z_ref[...], g_ref[...], dy_)
            dz_ref[...] = dz
            dzb_ref[...] = dz.astype(CDT)
            _acc(dgain_ref, _colsum(dy_ * xhat), i == 0)
            _acc(dbeta_ref, _colsum(dy_), i == 0)

        dhid = _dot_nt(dzb_ref[...], wdn_ref[...])
        gate, up = gu_ref[0], gu_ref[1]
        sg = _sigmoid(gate)
        dgu_ref[0] = (dhid * up * (sg * (1.0 + gate * (1.0 - sg)))).astype(CDT)
        dgu_ref[1] = (dhid * (gate * sg)).astype(CDT)

    row = pl.BlockSpec((tm, D), lambda i, j: (i, 0))
    vec = pl.BlockSpec((1, D), lambda i, j: (0, 0))
    dz, dzb, dgu, dgain, dbeta = _call(
        after, hidden, grid=(S // tm, 4),
        in_specs=[row, row, pl.BlockSpec((2, None, tm, FFN_B), lambda i, j: (0, j, i, 0)),
                  pl.BlockSpec((None, FFN_B, D), lambda i, j: (j, 0, 0)), vec],
        out_specs=[row, row, pl.BlockSpec((2, None, tm, FFN_B), lambda i, j: (0, j, i, 0)), vec, vec],
        out_shape=[_sds((S, D), F32), _sds((S, D), CDT), _sds((2, 4, S, FFN_B), CDT), _sds((1, D), F32),
                   _sds((1, D), F32)],
        compiler_params=_params(("arbitrary", "arbitrary")), name=name + "_hidden")(dy, z, gu, wdn, gain)

    def to_input(dz_ref, dgu_ref, wgu_ref, dx_ref):
        acc = ALPHA * dz_ref[...]
        for g in range(2):
            for j in range(4):
                acc = acc + _dot(dgu_ref[g, j], wgu_ref[g, j])
        dx_ref[...] = acc

    rows = pl.BlockSpec((tm, D), lambda i: (i, 0))
    dx = _call(
        None, to_input, grid=(S // tm,),
        in_specs=[rows, pl.BlockSpec((2, 4, tm, FFN_B), lambda i: (0, 0, i, 0)),
                  pl.BlockSpec((2, 4, FFN_B, D), lambda i: (0, 0, 0, 0))],
        out_specs=rows, out_shape=_sds((S, D), F32),
        compiler_params=_params(("arbitrary",)), name=name + "_input")(dz, dgu, wgu)
    return dx, dzb, dgu, dgain, dbeta


def _mixout_bwd(dy, z, gain, w3, du_dtype, name, after=None):
    S = dy.shape[0]
    G, Kb, _ = w3.shape
    tm = _tile(S)

    def body(dy_ref, z_ref, g_ref, w_ref, dz_ref, dzb_ref, du_ref, dgain_ref, dbeta_ref, dbias_ref):
        first = pl.program_id(0) == 0
        dy_ = dy_ref[...]
        dz, xhat = _ln_bwd(z_ref[...], g_ref[...], dy_)
        dz_ref[...] = dz
        dzb = dz.astype(CDT)
        dzb_ref[...] = dzb
        for g in range(G):
            du_ref[g] = _dot_nt(dzb, w_ref[g]).astype(du_ref.dtype)
        _acc(dgain_ref, _colsum(dy_ * xhat), first)
        _acc(dbeta_ref, _colsum(dy_), first)
        _acc(dbias_ref, _colsum(dz), first)

    row = pl.BlockSpec((tm, D), lambda i: (i, 0))
    vec = pl.BlockSpec((1, D), lambda i: (0, 0))
    return _call(
        after, body, grid=(S // tm,), in_specs=[row, row, vec, pl.BlockSpec((G, Kb, D), lambda i: (0, 0, 0))],
        out_specs=[row, row, pl.BlockSpec((G, tm, Kb), lambda i: (0, i, 0)), vec, vec, vec],
        out_shape=[_sds((S, D), F32), _sds((S, D), CDT), _sds((G, S, Kb), du_dtype)] + [_sds((1, D), F32)] * 3,
        compiler_params=_params(("arbitrary",)), name=name)(dy, z, gain, w3)


def _half_select(low):
    r = lax.broadcasted_iota(jnp.int32, (2 * ATT_HD, ATT_HD), 0)
    c = lax.broadcasted_iota(jnp.int32, (2 * ATT_HD, ATT_HD), 1)
    return (r == c + (0 if low else ATT_HD)).astype(CDT)


def _half_place(low):
    r = lax.broadcasted_iota(jnp.int32, (ATT_HD, 2 * ATT_HD), 0)
    c = lax.broadcasted_iota(jnp.int32, (ATT_HD, 2 * ATT_HD), 1)
    return (c == r + (0 if low else ATT_HD)).astype(CDT)


def _pair_lanes(even, odd):
    return (jnp.dot(even, _half_place(True), preferred_element_type=F32)
            + jnp.dot(odd, _half_place(False), preferred_element_type=F32)).astype(CDT)


def _proj_heads(a, w, bias, heads, name):
    S, K = a.shape
    N = heads * ATT_HD
    tm = _tile(S)

    def body(a_ref, w_ref, b_ref, o_ref):
        acc = (_dot(a_ref[...], w_ref[...]) + b_ref[...]).astype(CDT)
        sel = (_half_select(True), _half_select(False))
        for h in range(heads):
            pair = acc[:, (h // 2) * 2 * ATT_HD:(h // 2 + 1) * 2 * ATT_HD]
            o_ref[h] = jnp.dot(pair, sel[h % 2], preferred_element_type=F32).astype(CDT)

    return _call(
        None, body, grid=(S // tm,),
        in_specs=[pl.BlockSpec((tm, K), lambda i: (i, 0)), pl.BlockSpec((K, N), lambda i: (0, 0)),
                  pl.BlockSpec((1, N), lambda i: (0, 0))],
        out_specs=pl.BlockSpec((heads, tm, ATT_HD), lambda i: (0, i, 0)), out_shape=_sds((heads, S, ATT_HD), CDT),
        compiler_params=_params(("arbitrary",)), name=name)(a, w, bias)


def _qkv_bwd(dz, dq, dkv4, wq, wkv, name, after=None):
    S = dz.shape[0]
    tm = _tile(S)
    HK = dkv4.shape[0]
    NK = HK * ATT_HD

    def body(dz_ref, dq_ref, dkv_ref, wq_ref, wkv_ref, dx_ref, dkvn_ref, dkvb_ref):
        first = pl.program_id(0) == 0
        dkvn = jnp.concatenate([_pair_lanes(dkv_ref[2 * i].astype(CDT), dkv_ref[2 * i + 1].astype(CDT))
                                for i in range(HK // 2)], axis=-1)
        dkvn_ref[...] = dkvn
        dx_ref[...] = ALPHA * dz_ref[...] + _dot_nt(dq_ref[...], wq_ref[...]) + _dot_nt(dkvn, wkv_ref[...])
        for h in range(HK):
            _acc(dkvb_ref.at[h], _colsum(dkv_ref[h]), first)

    row = pl.BlockSpec((tm, D), lambda i: (i, 0))
    return _call(
        after, body, grid=(S // tm,),
        in_specs=[row, row, pl.BlockSpec((HK, tm, ATT_HD), lambda i: (0, i, 0)),
                  pl.BlockSpec((D, D), lambda i: (0, 0)), pl.BlockSpec((D, NK), lambda i: (0, 0))],
        out_specs=[row, pl.BlockSpec((tm, NK), lambda i: (i, 0)), pl.BlockSpec((HK, 1, ATT_HD), lambda i: (0, 0, 0))],
        out_shape=[_sds((S, D), F32), _sds((S, NK), CDT), _sds((HK, 1, ATT_HD), F32)],
        compiler_params=_params(("arbitrary",)), name=name)(dz, dq, dkv4, wq, wkv)


def _inproj_bwd(dz, dproj, wain, name, after=None):
    S = dz.shape[0]
    tm = _tile(S)
    nb = wain.shape[-1]

    def body(dz_ref, dp_ref, w_ref, dx_ref):
        acc = ALPHA * dz_ref[...]
        for j in range(N_DEV):
            acc = acc + _dot_nt(dp_ref[j // 2, :, pl.ds((j % 2) * nb, nb)], w_ref[j])
        dx_ref[...] = acc

    row = pl.BlockSpec((tm, D), lambda i: (i, 0))
    return _call(
        after, body, grid=(S // tm,),
        in_specs=[row, pl.BlockSpec((4, tm, D), lambda i: (0, i, 0)), pl.BlockSpec((N_DEV, D, nb), lambda i: (0, 0, 0))],
        out_specs=row, out_shape=_sds((S, D), F32),
        compiler_params=_params(("arbitrary",)), name=name)(dz, dproj, wain)


def _running_sum(x, reverse=False):
    rows = x.shape[0]
    row = lax.broadcasted_iota(jnp.int32, x.shape, 0)
    step = 1
    while step < rows:
        if reverse:
            x = x + jnp.where(row < rows - step, pltpu.roll(x, rows - step, 0), 0.0)
        else:
            x = x + jnp.where(row >= step, pltpu.roll(x, step, 0), 0.0)
        step *= 2
    return x


def _hg_gates(q, f, alb_ref):
    a0, a1 = alb_ref[0:1, :], alb_ref[1:2, :]
    mx = jnp.maximum(a0, a1)
    e0, e1 = jnp.exp(a0 - mx), jnp.exp(a1 - mx)
    lb = e0 / (e0 + e1)
    sig = _sigmoid(f)
    forget = lb + (1.0 - lb) * sig
    k = (1.0 - lb) * _sigmoid(-f)
    qs = q * _sigmoid(q) * (HG_DK ** -0.5)
    return qs, k, jnp.log(forget), sig, lb, forget


def _hg_intra(qs, k, b, b_scr):
    b_scr[...] = b
    bm = b_scr[pl.ds(HG_CH // 2 - 1, 1), :]
    bl = b_scr[pl.ds(HG_CH - 1, 1), :]
    eb = jnp.exp(b)
    qb = qs * eb
    e_q = jnp.exp(b - bm)
    e_k = jnp.exp(bm - b)
    e_d = jnp.exp(bl - b)
    return qb, qs * e_q, k * e_k, k * e_d, jnp.exp(bl), eb, e_q, e_k, e_d


def _hgrn_fwd(proj, alb, ngain):
    S = proj.shape[1]
    nc = S // HG_CH
    wb = HG_HPB * HG_DK

    def body(pj_ref, alb_ref, ng_ref, o_ref, y_ref, st_ref, st_scr, b_scr):
        n = pl.program_id(1)

        @pl.when(n == 0)
        def _():
            st_scr[...] = jnp.zeros_like(st_scr)

        r = lax.broadcasted_iota(jnp.int32, (HG_CH, HG_CH), 0)
        c = lax.broadcasted_iota(jnp.int32, (HG_CH, HG_CH), 1)
        causal = r >= c
        for j in range(HG_HPB):
            lanes = pl.ds(j * HG_DK, HG_DK)
            q, f, v, g = pj_ref[0, :, lanes], pj_ref[1, :, lanes], pj_ref[2, :, lanes], pj_ref[3, :, lanes]
            qs, k, logf, _, _, _ = _hg_gates(q, f, alb_ref.at[:, lanes])
            b = _running_sum(logf)
            qb, qt, kt, kd, ebl, _, _, _, _ = _hg_intra(qs, k, b, b_scr.at[j])
            st = st_scr[j]
            st_ref[j] = st
            a = jnp.where(causal, _dot_nt(qt, kt), 0.0)
            o = _dot(a, v) + _dot_nt(qb, st)
            st_scr[j] = st * ebl + _dot_tn(v, kd)
            o_ref[:, lanes] = o
            rinv = lax.rsqrt(jnp.mean(o * o, axis=-1, keepdims=True) + RMS_EPS)
            y_ref[:, lanes] = (o * rinv * ng_ref[...] * (g * _sigmoid(g))).astype(CDT)

    blk = pl.BlockSpec((HG_CH, wb), lambda h, n: (n, h))
    return _call(
        None, body, grid=(HG_H // HG_HPB, nc),
        in_specs=[pl.BlockSpec((4, HG_CH, wb), lambda h, n: (0, n, h)), pl.BlockSpec((2, wb), lambda h, n: (0, h)),
                  pl.BlockSpec((1, HG_DK), lambda h, n: (0, 0))],
        out_specs=[blk, blk, pl.BlockSpec((HG_HPB, None, HG_DK, HG_DK), lambda h, n: (h, n, 0, 0))],
        out_shape=[_sds((S, D), F32), _sds((S, D), CDT), _sds((HG_H, nc, HG_DK, HG_DK), F32)],
        scratch_shapes=[pltpu.VMEM((HG_HPB, HG_DK, HG_DK), F32), pltpu.VMEM((HG_HPB, HG_CH, HG_DK), F32)],
        compiler_params=_params(("arbitrary", "arbitrary")), name="hgrn_fwd")(proj, alb, ngain)


def _hgrn_bwd(proj, alb, ngain, o, states, dy):
    S = proj.shape[1]
    nc = S // HG_CH
    wb = HG_HPB * HG_DK

    def body(pj_ref, alb_ref, ng_ref, o_ref, st_ref, dy_ref, dpj_ref, dalb_ref, dng_ref, dst_scr, b_scr):
        h, n = pl.program_id(0), pl.program_id(1)

        @pl.when(n == 0)
        def _():
            dst_scr[...] = jnp.zeros_like(dst_scr)

        ng = ng_ref[...]
        r = lax.broadcasted_iota(jnp.int32, (HG_CH, HG_CH), 0)
        c = lax.broadcasted_iota(jnp.int32, (HG_CH, HG_CH), 1)
        causal = r >= c
        dng = None
        for j in range(HG_HPB):
            lanes = pl.ds(j * HG_DK, HG_DK)
            q, f, v, g = pj_ref[0, :, lanes], pj_ref[1, :, lanes], pj_ref[2, :, lanes], pj_ref[3, :, lanes]
            o_ = o_ref[:, lanes]
            dy_ = dy_ref[:, lanes]
            sg = _sigmoid(g)
            rinv = lax.rsqrt(jnp.mean(o_ * o_, axis=-1, keepdims=True) + RMS_EPS)
            nrm = o_ * rinv
            dr = dy_ * (g * sg)
            dg = dy_ * nrm * ng * (sg * (1.0 + g * (1.0 - sg)))
            dn = dr * ng
            do = rinv * (dn - nrm * jnp.mean(dn * nrm, axis=-1, keepdims=True))
            dng = _colsum(dr * nrm) if dng is None else dng + _colsum(dr * nrm)
            qs, k, logf, sig, lb, forget = _hg_gates(q, f, alb_ref.at[:, lanes])
            b = _running_sum(logf)
            qb, qt, kt, kd, ebl, eb, e_q, e_k, e_d = _hg_intra(qs, k, b, b_scr.at[j])
            st = st_ref[j]
            dstn = dst_scr[j]
            qt, kt, qb, kd = (t.astype(CDT).astype(F32) for t in (qt, kt, qb, kd))
            a = jnp.where(causal, _dot_nt(qt, kt), 0.0)
            da = jnp.where(causal, _dot_nt(do, v), 0.0)
            dv = _dot_tn(a, do) + _dot_nt(kd, dstn)
            dqb = _dot(do, st)
            dkd = _dot(v, dstn)
            dqt = _dot(da, kt)
            dkt = _dot_tn(da, qt)
            dbl = _colsum(dkd * kd) + ebl * _colsum(dstn * st)
            dst_scr[j] = dstn * ebl + _dot_tn(do, qb)
            dqs = dqt * e_q + dqb * eb
            dk = dkt * e_k + dkd * e_d
            db = dqt * qt + dqb * qb - dkt * kt - dkd * kd
            dlogf = _running_sum(db, reverse=True) + dbl
            dforget = dlogf / forget
            dsig = (1.0 - lb) * (dforget - dk)
            df = dsig * sig * (1.0 - sig)
            dlb = _colsum((dforget - dk) * (1.0 - sig))
            sq = _sigmoid(q)
            dq = dqs * (HG_DK ** -0.5) * (sq * (1.0 + q * (1.0 - sq)))
            dpj_ref[0, :, lanes] = dq.astype(CDT)
            dpj_ref[1, :, lanes] = df.astype(CDT)
            dpj_ref[2, :, lanes] = dv.astype(CDT)
            dpj_ref[3, :, lanes] = dg.astype(CDT)
            da0 = dlb * lb * (1.0 - lb)
            _acc(dalb_ref.at[pl.ds(0, 1), lanes], da0, n == 0)
            _acc(dalb_ref.at[pl.ds(1, 1), lanes], -da0, n == 0)
        _acc(dng_ref, dng, jnp.logical_and(h == 0, n == 0))

    blk = pl.BlockSpec((HG_CH, wb), lambda h, n: (nc - 1 - n, h))
    pj = pl.BlockSpec((4, HG_CH, wb), lambda h, n: (0, nc - 1 - n, h))
    alb_blk = pl.BlockSpec((2, wb), lambda h, n: (0, h))
    ng_blk = pl.BlockSpec((1, HG_DK), lambda h, n: (0, 0))
    return _call(
        None, body, grid=(HG_H // HG_HPB, nc),
        in_specs=[pj, alb_blk, ng_blk, blk,
                  pl.BlockSpec((HG_HPB, None, HG_DK, HG_DK), lambda h, n: (h, nc - 1 - n, 0, 0)), blk],
        out_specs=[pj, alb_blk, ng_blk],
        out_shape=[_sds((4, S, D), CDT), _sds((2, D), F32), _sds((1, HG_DK), F32)],
        scratch_shapes=[pltpu.VMEM((HG_HPB, HG_DK, HG_DK), F32), pltpu.VMEM((HG_HPB, HG_CH, HG_DK), F32)],
        compiler_params=_params(("arbitrary", "arbitrary")), name="hgrn_bwd")(proj, alb, ngain, o, states, dy)


def _slope(h):
    return 2.0 ** (-8.0 * (h + 1) / ATT_QH)


def _attn_mask(n):
    qi = lax.broadcasted_iota(jnp.int32, (WINDOW, 2 * WINDOW), 0)
    si = lax.broadcasted_iota(jnp.int32, (WINDOW, 2 * WINDOW), 1)
    dist = qi - si + WINDOW
    valid = (dist >= 0) & (dist < WINDOW) & (n * WINDOW - WINDOW + si >= 0)
    return valid, dist.astype(F32)


def _attn_probs(qh, kh, sink, slope, valid, distf):
    s = _dot_nt(qh, kh) * (ATT_HD ** -0.5) - slope * distf
    s = jnp.where(valid, s, NEG)
    m = jnp.maximum(jnp.max(s, axis=-1, keepdims=True), sink)
    e = jnp.exp(s - m)
    es = jnp.exp(sink - m)
    inv = 1.0 / (jnp.sum(e, axis=-1, keepdims=True) + es)
    return e * inv, es * inv


def _attn_specs(S):
    nb = S // WINDOW
    cur = lambda H: pl.BlockSpec((H, WINDOW, ATT_HD), lambda n: (0, n, 0))
    prev = lambda H: pl.BlockSpec((H, WINDOW, ATT_HD), lambda n: (0, jnp.maximum(n - 1, 0), 0))
    return nb, cur, prev


def _attn_fwd(q4, kv4, sinks):
    S = q4.shape[1]
    nb, cur, prev = _attn_specs(S)

    def body(sink_ref, q_ref, kvc_ref, kvp_ref, o_ref):
        valid, distf = _attn_mask(pl.program_id(0))
        outs = []
        for h in range(ATT_QH):
            kvh = h // ATT_G
            kh = jnp.concatenate([kvp_ref[kvh], kvc_ref[kvh]], axis=0)
            vh = jnp.concatenate([kvp_ref[ATT_KVH + kvh], kvc_ref[ATT_KVH + kvh]], axis=0)
            p, _ = _attn_probs(q_ref[h], kh, sink_ref[0, h], _slope(h), valid, distf)
            outs.append(_dot(p, vh).astype(CDT))
            if h % 2:
                o_ref[:, pl.ds((h - 1) * ATT_HD, 2 * ATT_HD)] = _pair_lanes(outs[h - 1], outs[h])

    return _call(
        None, body, grid=(nb,),
        in_specs=[pl.BlockSpec(memory_space=pltpu.SMEM), cur(ATT_QH), cur(2 * ATT_KVH), prev(2 * ATT_KVH)],
        out_specs=pl.BlockSpec((WINDOW, D), lambda n: (n, 0)), out_shape=_sds((S, D), CDT),
        compiler_params=_params(("arbitrary",)), name="attn_fwd")(sinks, q4, kv4, kv4)


def _attn_bwd(q4, kv4, sinks, do):
    S = q4.shape[1]
    nb, cur, prev = _attn_specs(S)

    def body(sink_ref, q_ref, kvc_ref, kvp_ref, do_ref, dq_ref, dkv_ref, dbq_ref, dsink_ref):
        n = pl.program_id(0)
        first = n == 0

        @pl.when(first)
        def _():
            dkv_ref[...] = jnp.zeros_like(dkv_ref)
            dsink_ref[...] = jnp.zeros_like(dsink_ref)

        valid, distf = _attn_mask(n)
        lane = lax.broadcasted_iota(jnp.int32, (1, 128), 1)
        rows_cur = pl.ds(pl.multiple_of(n * WINDOW, WINDOW), WINDOW)
        rows_prev = pl.ds(pl.multiple_of(jnp.maximum(n - 1, 0) * WINDOW, WINDOW), WINDOW)
        dsinks = jnp.zeros((1, 128), F32)
        sel = (_half_select(True), _half_select(False))
        for kvh in range(ATT_KVH):
            kh = jnp.concatenate([kvp_ref[kvh], kvc_ref[kvh]], axis=0)
            vh = jnp.concatenate([kvp_ref[ATT_KVH + kvh], kvc_ref[ATT_KVH + kvh]], axis=0)
            dk = dv = None
            dqs = []
            for h in range(kvh * ATT_G, (kvh + 1) * ATT_G):
                qh = q_ref[h]
                doh = jnp.dot(do_ref[:, pl.ds((h // 2) * 2 * ATT_HD, 2 * ATT_HD)], sel[h % 2],
                              preferred_element_type=F32).astype(CDT)
                p, ps = _attn_probs(qh, kh, sink_ref[0, h], _slope(h), valid, distf)
                dp = _dot_nt(doh, vh)
                dd = jnp.sum(p * dp, axis=-1, keepdims=True)
                ds = p * (dp - dd)
                dsinks = dsinks + jnp.where(lane == h, -jnp.sum(ps * dd, axis=0, keepdims=True), 0.0)
                dqh = _dot(ds, kh) * (ATT_HD ** -0.5)
                dqs.append(dqh.astype(CDT))
                _acc(dbq_ref.at[h], _colsum(dqh), first)
                dkh = _dot_tn(ds, qh) * (ATT_HD ** -0.5)
                dvh = _dot_tn(p, doh)
                dk = dkh if dk is None else dk + dkh
                dv = dvh if dv is None else dv + dvh
            for i in range(ATT_G // 2):
                lanes = pl.ds((kvh * ATT_G + 2 * i) * ATT_HD, 2 * ATT_HD)
                dq_ref[:, lanes] = _pair_lanes(dqs[2 * i], dqs[2 * i + 1])
            dkv_ref[kvh, rows_prev, :] += dk[:WINDOW]
            dkv_ref[kvh, rows_cur, :] += dk[WINDOW:]
            dkv_ref[ATT_KVH + kvh, rows_prev, :] += dv[:WINDOW]
            dkv_ref[ATT_KVH + kvh, rows_cur, :] += dv[WINDOW:]
        dsink_ref[...] += dsinks

    return _call(
        None, body, grid=(nb,),
        in_specs=[pl.BlockSpec(memory_space=pltpu.SMEM), cur(ATT_QH), cur(2 * ATT_KVH), prev(2 * ATT_KVH),
                  pl.BlockSpec((WINDOW, D), lambda n: (n, 0))],
        out_specs=[pl.BlockSpec((WINDOW, D), lambda n: (n, 0)), pl.BlockSpec((2 * ATT_KVH, S, ATT_HD), lambda n: (0, 0, 0)),
                   pl.BlockSpec((ATT_QH, 1, ATT_HD), lambda n: (0, 0, 0)), pl.BlockSpec((1, 128), lambda n: (0, 0))],
        out_shape=[_sds((S, D), CDT), _sds((2 * ATT_KVH, S, ATT_HD), F32), _sds((ATT_QH, 1, ATT_HD), F32),
                   _sds((1, 128), F32)],
        compiler_params=_params(("arbitrary",)), name="attn_bwd")(sinks, q4, kv4, kv4, do)


def _local_step(x, p, target, getw, sm, emit):
    S = x.shape[0]
    vec = lambda a: a.reshape(1, -1)
    ln_g = lambda l, k: vec(sm["ln_gain"][l, k])
    ln_b = lambda l, k: vec(sm["ln_bias"][l, k])
    xb = x.astype(CDT)
    pb = p.astype(CDT)

    proj = _mm_nn(xb, getw("a_w_in", None), (4, S, D), (None, _tile(S), 512), lambda g, i: (g // 2, i, g % 2), F32,
                  name="a_in")
    o_a, y_a, states = _hgrn_fwd(proj, sm["a_lower_bound"], sm["a_norm_gain"])
    zeros = jnp.zeros((1, D), F32)
    z = [[None] * 3 for _ in range(2)]
    xs = [[None] * 3 for _ in range(2)]
    xbs = [[None] * 3 for _ in range(2)]
    z[0][0], xs[0][0], xbs[0][0] = _mixout_ln(y_a[None], getw("a_w_out", y_a)[None], zeros, x, ln_g(0, 0), ln_b(0, 0),
                                              "a_out_ln")
    gu, hid, sgs, ups = [None, None], [None, None], [None, None], [None, None]

    def ffn_ple(l):
        wgu = getw(f"gu{l}", xbs[l][0])
        gu[l], hid[l], z[l][1], xs[l][1], xbs[l][1] = _ffn_fwd(
            xs[l][0], xbs[l][0], wgu, getw(f"dn{l}", None), ln_g(l, 1), ln_b(l, 1), f"ffn_fwd{l}")
        sgs[l], ups[l], z[l][2], xs[l][2], xbs[l][2] = _ple_fwd(
            xs[l][1], xbs[l][1], pb[l], getw(f"pg{l}", None), vec(sm["ple_b_gate"][l]), getw(f"pu{l}", None), ln_g(l, 2),
            ln_b(l, 2), f"ple_fwd{l}")

    ffn_ple(0)
    x3, x3b = xs[0][2], xbs[0][2]
    w_kv, w_q, w_bo = getw("kv_w", x3b), getw("b_w_q", None), getw("b_w_out", None)
    kv4 = _proj_heads(x3b, w_kv, vec(sm["kv_b"]), 2 * ATT_KVH, "kv_proj")
    q4 = _proj_heads(x3b, w_q, vec(sm["b_b_q"]), ATT_QH, "q_proj")
    o_b = _attn_fwd(q4, kv4, sm["b_sinks"])
    z[1][0], xs[1][0], xbs[1][0] = _mixout_ln(o_b[None], w_bo[None], sm["b_b_out"], x3, ln_g(1, 0), ln_b(1, 0),
                                              "b_out_ln")
    ffn_ple(1)
    loss, dy = _loss_fwd_bwd(xs[1][2], target)

    gs = {}
    d_ln_g = [[None] * 3 for _ in range(2)]
    d_ln_b = [[None] * 3 for _ in range(2)]
    g_bg = [None, None]

    def ffn_ple_bwd(l, dy, after=None):
        dx2, dgl, dup, d_ln_g[l][2], d_ln_b[l][2], g_bg[l] = _ple_bwd(dy, z[l][2], sgs[l], ups[l], ln_g(l, 2),
                                                                     getw(f"pg{l}", None), f"ple_bwd{l}", after=after)
        g_pg = _wgrad(xbs[l][1][None], dgl[None], f"g_ple_gate{l}")[0]
        g_pu = _wgrad(pb[l][None], dup[None], f"g_ple_up{l}")[0]
        dx1, dzb, dgu, d_ln_g[l][1], d_ln_b[l][1] = _ffn_bwd(dx2, z[l][1], gu[l], getw(f"gu{l}", None),
                                                           getw(f"dn{l}", None), ln_g(l, 1), f"ffn_bwd{l}")
        g_dn = _wgrad(hid[l], dzb[None], f"g_ffn_down{l}")
        g_gu = _wgrad(dgu.reshape(8, S, FFN_B), xbs[l][0][None], f"g_ffn_gate_up{l}")
        return dx1, emit({f"pg{l}": g_pg, f"pu{l}": g_pu, f"dn{l}": g_dn, f"gu{l}": g_gu})

    dx1, tok = ffn_ple_bwd(1, dy)
    dz, dzb, do, d_ln_g[1][0], d_ln_b[1][0], gs["b_b_out"] = _mixout_bwd(dx1, z[1][0], ln_g(1, 0), w_bo[None], CDT,
                                                                        "b_out_bwd", after=tok)
    g_bo = _wgrad(o_b[None], dzb[None], "g_b_w_out")[0]
    dq, dkv4, dbq, dsinks = _attn_bwd(q4, kv4, sm["b_sinks"], do[0])
    gs["b_b_q"] = dbq
    gs["b_sinks"] = dsinks
    g_q = _wgrad(x3b[None], dq[None], "g_b_w_q")[0]
    dx3, dkv, gs["kv_b"] = _qkv_bwd(dz, dq, dkv4, w_q, w_kv, "qkv_bwd", after=tok)
    g_kv = _wgrad(x3b[None], dkv[None], "g_kv_w")[0]
    tok = emit({"b_w_out": g_bo, "b_w_q": g_q, "kv_w": g_kv})
    dx1, tok = ffn_ple_bwd(0, dx3, tok)
    w_ao = getw("a_w_out", None)
    dz, dzb, dyr, d_ln_g[0][0], d_ln_b[0][0], _ = _mixout_bwd(dx1, z[0][0], ln_g(0, 0), w_ao[None], F32, "a_out_bwd",
                                                              after=tok)
    g_ao = _wgrad(y_a[None], dzb[None], "g_a_w_out")[0]
    dproj, gs["a_lower_bound"], gs["a_norm_gain"] = _hgrn_bwd(proj, sm["a_lower_bound"], sm["a_norm_gain"], o_a, states,
                                                              dyr[0])
    tk = lambda t: (None, t, D)
    g_ain = _mm_tn(xb[None], dproj, N_DEV, lambda g, k: (0, k, 0), lambda g, k: (g // 2, k, g % 2),
                   tk, lambda t: (None, t, 512), (N_DEV, D, 512), (None, D, 512), lambda g, k: (g, 0, 0), name="g_a_w_in")
    gs["ple_b_gate"] = jnp.concatenate(g_bg, axis=0)
    gs["ln_gain"] = jnp.stack([jnp.concatenate(r, axis=0) for r in d_ln_g])
    gs["ln_bias"] = jnp.stack([jnp.concatenate(r, axis=0) for r in d_ln_b])
    tok = emit({"a_w_out": g_ao, "a_w_in": g_ain}, small=gs)
    grad_x = _inproj_bwd(dz, dproj, getw("a_w_in", None), "a_in_bwd", after=tok)
    return loss, grad_x, gs


def _peer(k):
    x, y, c = lax.axis_index("x"), lax.axis_index("y"), lax.axis_index("c")
    px = 1 - x if k & 4 else x
    py = 1 - y if k & 2 else y
    pc = 1 - c if k & 1 else c
    return (px, py, pc), 4 * px + 2 * py + pc


def _my_index():
    return 4 * lax.axis_index("x") + 2 * lax.axis_index("y") + lax.axis_index("c")


def _exchange(srcs, dst_shapes, plan, name):
    n_src, n_piece = len(srcs), len(plan)

    def body(*refs):
        src_refs, dst_refs = refs[:n_src], refs[n_src:n_src + len(dst_shapes)]
        send_sems, recv_sems, local_sems = refs[n_src + len(dst_shapes):]
        me = _my_index()

        def at(ref, idx):
            return ref.at[idx] if idx else ref

        local = []
        for t, (si, sfn, di, dfn) in enumerate(plan):
            cp = pltpu.make_async_copy(at(src_refs[si], sfn(me)), at(dst_refs[di], dfn(me)), local_sems.at[t])
            cp.start()
            local.append(cp)
        sends = []
        for k in range(1, N_DEV):
            peer, pid = _peer(k)
            for t, (si, sfn, di, dfn) in enumerate(plan):
                cp = pltpu.make_async_remote_copy(
                    src_ref=at(src_refs[si], sfn(pid)), dst_ref=at(dst_refs[di], dfn(me)),
                    send_sem=send_sems.at[t * 7 + k - 1], recv_sem=recv_sems.at[t * 7 + k - 1],
                    device_id=peer, device_id_type=MESH)
                cp.start()
                sends.append(cp)
        for k in range(1, N_DEV):
            peer, pid = _peer(k)
            for t, (si, sfn, di, dfn) in enumerate(plan):
                pltpu.make_async_remote_copy(
                    src_ref=at(src_refs[si], sfn(me)), dst_ref=at(dst_refs[di], dfn(pid)),
                    send_sem=send_sems.at[t * 7 + k - 1], recv_sem=recv_sems.at[t * 7 + k - 1],
                    device_id=peer, device_id_type=MESH).wait_recv()
        for cp in sends:
            cp.wait_send()
        for cp in local:
            cp.wait()

    hbm = pl.BlockSpec(memory_space=pltpu.HBM)
    return _call(
        None, body, in_specs=[hbm] * n_src, out_specs=[hbm] * len(dst_shapes), out_shape=dst_shapes,
        scratch_shapes=[pltpu.SemaphoreType.DMA((7 * n_piece,)), pltpu.SemaphoreType.DMA((7 * n_piece,)),
                        pltpu.SemaphoreType.DMA((n_piece,))],
        name=name)(*srcs)


def _gather(shards, name):
    dsts = [_sds((N_DEV,) + a.shape, a.dtype) for a in shards]
    plan = [(i, lambda j: (), i, lambda s: (s,)) for i in range(len(shards))]
    return _exchange(shards, dsts, plan, name)


_HBM = pl.BlockSpec(memory_space=pltpu.HBM)
_SEM = pl.BlockSpec(memory_space=pltpu.SEMAPHORE)
_DATAFLOW = pltpu.SideEffectType.DATAFLOW_SIDE_EFFECTING


def _piece_copy(mode, src, land, send_sems, recv_sems, t, k, sender, receiver, peer):
    return pltpu.make_async_remote_copy(
        src_ref=src if mode == "gather" else src.at[receiver], dst_ref=land.at[sender],
        send_sem=send_sems.at[t * 7 + k - 1], recv_sem=recv_sems.at[t * 7 + k - 1], device_id=peer, device_id_type=MESH)


def _sequencer_exchange(srcs, modes, name, collective_id, after=None):
    n = len(srcs)
    land_shapes = [((N_DEV,) + a.shape) if mode == "gather" else a.shape for a, mode in zip(srcs, modes)]
    extra = [] if after is None else [after]

    def body(*refs):
        src_refs, land_refs = refs[:n], refs[n + len(extra):2 * n + len(extra)]
        send_sems, recv_sems, local_sems = refs[2 * n + len(extra):]
        barrier = pltpu.get_barrier_semaphore()
        for k in range(1, N_DEV):
            pl.semaphore_signal(barrier, inc=1, device_id=_peer(k)[0], device_id_type=MESH)
        pl.semaphore_wait(barrier, N_DEV - 1)
        me = _my_index()
        local = []
        for i in range(n):
            cp = pltpu.make_async_copy(src_refs[i] if modes[i] == "gather" else src_refs[i].at[me], land_refs[i].at[me],
                                       local_sems.at[i])
            cp.start()
            local.append(cp)
        for k in range(1, N_DEV):
            peer, pid = _peer(k)
            for t in range(n):
                _piece_copy(modes[t], src_refs[t], land_refs[t], send_sems, recv_sems, t, k, me, pid, peer).start()
        for k in range(1, N_DEV):
            peer, pid = _peer(k)
            for t in range(n):
                _piece_copy(modes[t], src_refs[t], land_refs[t], send_sems, recv_sems, t, k, pid, me, peer).wait_recv()
        for k in range(1, N_DEV):
            peer, pid = _peer(k)
            for t in range(n):
                _piece_copy(modes[t], src_refs[t], land_refs[t], send_sems, recv_sems, t, k, me, pid, peer).wait_send()
        for cp in local:
            cp.wait()

    return pl.kernel(
        body, out_type=[_sds(s, a.dtype) for s, a in zip(land_shapes, srcs)],
        mesh=plsc.ScalarSubcoreMesh(axis_name="sequencer", num_cores=1),
        scratch_types=[pltpu.SemaphoreType.DMA((7 * n,)), pltpu.SemaphoreType.DMA((7 * n,)), pltpu.SemaphoreType.DMA((n,))],
        compiler_params=pltpu.CompilerParams(collective_id=collective_id), name=name)(*srcs, *extra)


def _sequencer_gather(srcs, name, collective_id, after=None):
    n = len(srcs)
    extra = [] if after is None else [after]

    def body(*refs):
        src_refs, land_refs = refs[:n], refs[n + len(extra):2 * n + len(extra)]
        send_sems, recv_sems, local_sems = refs[2 * n + len(extra):]
        x, y, c = lax.axis_index("x"), lax.axis_index("y"), lax.axis_index("c")
        sibling = (x, y, 1 - c)
        chips = [(1 - x, y), (x, 1 - y), (1 - x, 1 - y)]
        index = lambda px, py, pc: 4 * px + 2 * py + pc
        barrier = pltpu.get_barrier_semaphore()
        for peer in [sibling] + [(*chip, c) for chip in chips]:
            pl.semaphore_signal(barrier, inc=1, device_id=peer, device_id_type=MESH)
        pl.semaphore_wait(barrier, 4)

        def copy(t, k, slot, to, src=None):
            return pltpu.make_async_remote_copy(
                src_ref=land_refs[t].at[slot] if src is None else src, dst_ref=land_refs[t].at[slot],
                send_sem=send_sems.at[7 * t + k], recv_sem=recv_sems.at[7 * t + k], device_id=to, device_id_type=MESH)

        me = index(x, y, c)
        local = []
        for t in range(n):
            cp = pltpu.make_async_copy(src_refs[t], land_refs[t].at[me], local_sems.at[t])
            cp.start()
            local.append(cp)
        sends = []
        for t in range(n):
            sends.append(copy(t, 0, me, sibling, src=src_refs[t]))
            sends += [copy(t, 1 + j, me, (*chip, c), src=src_refs[t]) for j, chip in enumerate(chips)]
        for cp in sends:
            cp.start()
        for j, chip in enumerate(chips):
            for t in range(n):
                copy(t, 1 + j, index(*chip, c), sibling, src=src_refs[t]).wait_recv()
                passed = copy(t, 4 + j, index(*chip, c), sibling)
                passed.start()
                sends.append(passed)
        for t in range(n):
            copy(t, 0, index(x, y, 1 - c), sibling, src=src_refs[t]).wait_recv()
        for j, chip in enumerate(chips):
            for t in range(n):
                copy(t, 4 + j, index(*chip, 1 - c), sibling, src=src_refs[t]).wait_recv()
        for cp in sends:
            cp.wait_send()
        for cp in local:
            cp.wait()

    return pl.kernel(
        body, out_type=[_sds((N_DEV,) + a.shape, a.dtype) for a in srcs],
        mesh=plsc.ScalarSubcoreMesh(axis_name="sequencer", num_cores=1),
        scratch_types=[pltpu.SemaphoreType.DMA((7 * n,)), pltpu.SemaphoreType.DMA((7 * n,)), pltpu.SemaphoreType.DMA((n,))],
        compiler_params=pltpu.CompilerParams(collective_id=collective_id), name=name)(*srcs, *extra)


def _xstart(groups, mode, name, after=None):
    flat = [a for g in groups for a in g]
    n, ng = len(flat), len(groups)
    land_shapes = [((N_DEV,) + a.shape) if mode == "gather" else a.shape for a in flat]
    first = [sum(len(g) for g in groups[:i]) for i in range(ng)]
    extra = [] if after is None else [after]

    def body(*refs):
        srcs, lands = refs[:n], refs[n:2 * n]
        sems = refs[2 * n + len(extra):2 * n + len(extra) + 2 * ng]
        tok_ref, local_sems = refs[-2], refs[-1]
        me = _my_index()
        local = []
        for i in range(n):
            cp = pltpu.make_async_copy(srcs[i] if mode == "gather" else srcs[i].at[me], lands[i].at[me], local_sems.at[i])
            cp.start()
            local.append(cp)
        for cp in local:
            cp.wait()
        for gi, g in enumerate(groups):
            for k in range(1, N_DEV):
                peer, pid = _peer(k)
                for t in range(len(g)):
                    i = first[gi] + t
                    _piece_copy(mode, srcs[i], lands[i], sems[2 * gi], sems[2 * gi + 1], t, k, me, pid, peer).start()
        tok_ref[...] = jnp.zeros_like(tok_ref)

    sem_shapes = []
    for g in groups:
        sem_shapes += [pltpu.SemaphoreType.DMA((7 * len(g),))] * 2
    thru = [pltpu.HBM(a.shape, a.dtype) for a in flat] + [pltpu.HBM(s, a.dtype) for s, a in zip(land_shapes, flat)]
    outs = pl.pallas_call(
        body, in_specs=[_HBM] * (2 * n) + [pl.BlockSpec(memory_space=pl.ANY)] * len(extra),
        out_specs=[_SEM] * (2 * ng) + [_HBM] * (2 * n) + [pl.BlockSpec(memory_space=pltpu.VMEM)],
        out_shape=sem_shapes + thru + [_sds((8, 128), F32)],
        input_output_aliases={i: 2 * ng + i for i in range(2 * n)},
        scratch_shapes=[pltpu.SemaphoreType.DMA((n,))],
        compiler_params=pltpu.CompilerParams(has_side_effects=_DATAFLOW), name=name)(
            *[pltpu.with_memory_space_constraint(a, pltpu.HBM) for a in flat],
            *[pltpu.with_memory_space_constraint(lax.empty(s, a.dtype), pltpu.HBM) for s, a in zip(land_shapes, flat)], *extra)
    sems, srcs_thru, lands_thru = outs[:2 * ng], outs[2 * ng:2 * ng + n], outs[2 * ng + n:2 * ng + 2 * n]
    handles = [(sems[2 * gi], sems[2 * gi + 1], srcs_thru[first[gi]:first[gi] + len(g)],
                lands_thru[first[gi]:first[gi] + len(g)]) for gi, g in enumerate(groups)]
    return handles, outs[-1]


def _xwait(handle, mode, after, name):
    send_sems, recv_sems, srcs_thru, lands_thru = handle
    n = len(srcs_thru)

    def body(*refs):
        srcs, lands, send, recv = refs[:n], refs[n:2 * n], refs[2 * n], refs[2 * n + 1]
        me = _my_index()
        for k in range(1, N_DEV):
            peer, pid = _peer(k)
            for t in range(n):
                _piece_copy(mode, srcs[t], lands[t], send, recv, t, k, pid, me, peer).wait_recv()
        for k in range(1, N_DEV):
            peer, pid = _peer(k)
            for t in range(n):
                _piece_copy(mode, srcs[t], lands[t], send, recv, t, k, me, pid, peer).wait_send()

    extra = [] if after is None else [after]
    outs = pl.pallas_call(
        body, in_specs=[_HBM] * (2 * n) + [_SEM, _SEM] + [pl.BlockSpec(memory_space=pl.ANY)] * len(extra),
        out_specs=[_HBM] * (2 * n),
        out_shape=[pltpu.HBM(a.shape, a.dtype) for a in list(srcs_thru) + list(lands_thru)],
        input_output_aliases={i: i for i in range(2 * n)},
        compiler_params=pltpu.CompilerParams(has_side_effects=_DATAFLOW), name=name)(
            *srcs_thru, *lands_thru, send_sems, recv_sems, *extra)
    return outs[n:]


def _adamw(w, g, m, v):
    m = ADAM_B1 * m + (1.0 - ADAM_B1) * g
    v = ADAM_B2 * v + (1.0 - ADAM_B2) * (g * g)
    m_hat = m / (1.0 - ADAM_B1 ** ADAM_STEP)
    v_hat = v / (1.0 - ADAM_B2 ** ADAM_STEP)
    delta = -ADAM_LR * (m_hat / (jnp.sqrt(v_hat) + ADAM_EPS) + ADAM_WD * w)
    return delta, m, v


def _adam_big(w, parts, m, v, name, after=None):
    L, R, C = w.shape
    tr = _tile(R, (256, 128, 176, 64, 32, 16))
    nr = R // tr

    def body(w_ref, *refs):
        p_refs, (m_ref, v_ref, g_ref, d_ref, mo_ref, vo_ref) = refs[:L], refs[L:]
        for l in range(L):
            @pl.when(pl.program_id(0) == l)
            def _(p_ref=p_refs[l]):
                g = p_ref[0].astype(F32)
                for s in range(1, N_DEV):
                    g = g + p_ref[s].astype(F32)
                g_ref[...] = g
                d_ref[...], mo_ref[...], vo_ref[...] = _adamw(w_ref[...], g, m_ref[...], v_ref[...])

    row = pl.BlockSpec((None, tr, C), lambda l, i: (l, i, 0))
    park = lambda l_of: (lambda l, i: (0, jnp.where(l == l_of, i, 0 if l_of else nr - 1), 0))
    return _call(
        after, body, grid=(L, nr),
        in_specs=[row] + [pl.BlockSpec((N_DEV, tr, C), park(l)) for l in range(L)] + [row, row],
        out_specs=[row] * 4, out_shape=[_sds((L, R, C), F32)] * 4,
        compiler_params=_params(("arbitrary", "arbitrary")), name=name)(w, *parts, m, v)


SMALL = (("a_lower_bound", (2, 128), (2, D)), ("ln_gain", (6, 128), (6, D)), ("ln_bias", (6, 128), (6, D)),
         ("a_norm_gain", (1, 128), (1, 128)), ("kv_b", (1, 512), (1, 512)), ("b_b_q", (1, D), (1, D)),
         ("b_sinks", (1, ATT_QH), (1, 128)), ("b_b_out", (1, D), (1, D)), ("ple_b_gate", (2, D), (2, D)))


def _adam_small(parts, w, m, v, after=None):
    k = len(SMALL)

    def body(*refs):
        p_refs, w_refs, m_refs, v_refs = refs[:k], refs[k:2 * k], refs[2 * k:3 * k], refs[3 * k:4 * k]
        outs = refs[4 * k:]
        me = _my_index()
        for i, (_, wshape, pshape) in enumerate(SMALL):
            cols = wshape[1]
            lanes = slice(None) if cols == pshape[1] else (
                pl.ds(0, cols) if cols < 128 else pl.ds(pl.multiple_of(me * cols, cols), cols))
            g = p_refs[i][0, :, lanes]
            for s in range(1, N_DEV):
                g = g + p_refs[i][s, :, lanes]
            g_ref, d_ref, mo_ref, vo_ref = outs[4 * i:4 * i + 4]
            g_ref[...] = g
            d_ref[...], mo_ref[...], vo_ref[...] = _adamw(w_refs[i][...], g, m_refs[i][...], v_refs[i][...])

    full = lambda shape: pl.BlockSpec(shape, lambda: (0,) * len(shape))
    names = [n for n, _, _ in SMALL]
    res = _call(
        after, body,
        in_specs=[full((N_DEV,) + ps) for _, _, ps in SMALL] + [full(ws) for _, ws, _ in SMALL] * 3,
        out_specs=[full(ws) for _, ws, _ in SMALL for _ in range(4)],
        out_shape=[_sds(ws, F32) for _, ws, _ in SMALL for _ in range(4)], name="adam_small")(
            *[parts[n] for n in names], *[a[n].reshape(ws) for a in (w, m, v) for n, ws, _ in SMALL])
    return {n: [r.reshape(w[n].shape) for r in res[4 * i:4 * i + 4]] for i, n in enumerate(names)}


WEIGHTS = ("a_w_in", "a_lower_bound", "a_norm_gain", "a_w_out", "kv_w", "kv_b", "b_w_q", "b_b_q", "b_sinks", "b_w_out",
           "b_b_out", "ffn_w_gate_up", "ffn_w_down", "ple_w_up", "ple_w_gate", "ple_b_gate", "ln_gain", "ln_bias")


GATHER_GROUPS = (("a_w_in",), ("a_w_out", "gu0", "dn0", "pu0", "pg0"), ("kv_w", "b_w_q", "b_w_out"),
                 ("gu1", "dn1", "pu1", "pg1"))
KERNEL_LAYOUT = {
    "a_w_in": lambda a: a,
    "a_w_out": lambda a: a.reshape(D, D),
    "kv_w": lambda a: a.reshape(D, 2 * ATT_KVH * ATT_HD),
    "b_w_q": lambda a: a.reshape(D, D),
    "b_w_out": lambda a: a.reshape(D, D),
    "gu": lambda a: a.reshape(2, 4, FFN_B, D),
    "dn": lambda a: a.reshape(4, FFN_B, D),
    "pu": lambda a: a,
    "pg": lambda a: a.reshape(D, D),
}
_row_blocks = lambda a: a.reshape(N_DEV, -1, a.shape[-1])
OWNER_BLOCKS = {
    "a_w_in": lambda g: g,
    "a_w_out": _row_blocks,
    "kv_w": _row_blocks,
    "b_w_q": _row_blocks,
    "b_w_out": _row_blocks,
    "gu": lambda g: g,
    "dn": lambda g: _row_blocks(g.reshape(FFN_H, D)),
    "pu": lambda g: g.reshape(PLE_DIM, N_DEV, 128).transpose(1, 0, 2),
    "pg": _row_blocks,
}
ADAM_AFTER = {1: (("kv_w", "kv_w"), ("b_w_q", "b_w_q"), ("b_w_out", "b_w_out")),
              2: (("ffn_w_gate_up", "gu"), ("ffn_w_down", "dn"), ("ple_w_up", "pu"), ("ple_w_gate", "pg")),
              3: (("a_w_out", "a_w_out"), ("a_w_in", "a_w_in"))}


def kernel(x, p, a_w_in, a_lower_bound, a_norm_gain, a_w_out, kv_w, kv_b, b_w_q, b_b_q, b_sinks, b_w_out, b_b_out, ffn_w_gate_up, ffn_w_down, ple_w_up, ple_w_gate, ple_b_gate, ln_gain, ln_bias, loss_target, m_a_w_in, m_a_lower_bound, m_a_norm_gain, m_a_w_out, m_kv_w, m_kv_b, m_b_w_q, m_b_b_q, m_b_sinks, m_b_w_out, m_b_b_out, m_ffn_w_gate_up, m_ffn_w_down, m_ple_w_up, m_ple_w_gate, m_ple_b_gate, m_ln_gain, m_ln_bias, v_a_w_in, v_a_lower_bound, v_a_norm_gain, v_a_w_out, v_kv_w, v_kv_b, v_b_w_q, v_b_b_q, v_b_sinks, v_b_w_out, v_b_b_out, v_ffn_w_gate_up, v_ffn_w_down, v_ple_w_up, v_ple_w_gate, v_ple_b_gate, v_ln_gain, v_ln_bias):
    given = dict(locals())
    w = {n: given[n] for n in WEIGHTS}
    m = {n: given["m_" + n] for n in WEIGHTS}
    v = {n: given["v_" + n] for n in WEIGHTS}
    shards = {"a_w_in": a_w_in[0], "a_w_out": a_w_out[0], "kv_w": kv_w, "b_w_q": b_w_q[0], "b_w_out": b_w_out[0]}
    for l in range(2):
        shards.update({f"gu{l}": ffn_w_gate_up[l].T, f"dn{l}": ffn_w_down[l], f"pu{l}": ple_w_up[l], f"pg{l}": ple_w_gate[l]})
    sharded_small = [a_lower_bound, ln_gain.reshape(6, 128), ln_bias.reshape(6, 128)]
    gathered = {}
    for gi, g in enumerate(GATHER_GROUPS):
        lands = _sequencer_gather([shards[n].astype(CDT) for n in g] + (sharded_small if gi == 0 else []),
                                  f"gather{gi}", gi)
        for n, a in zip(g, lands):
            gathered[n] = KERNEL_LAYOUT[n.rstrip("01")](a)
        if gi == 0:
            alb, lng, lnb = [a.transpose(1, 0, 2).reshape(a.shape[1], D) for a in lands[len(g):]]

    def getw(key, after):
        return gathered[key]

    sm = {"a_lower_bound": alb, "ln_gain": lng.reshape(2, 3, D), "ln_bias": lnb.reshape(2, 3, D),
          "a_norm_gain": a_norm_gain, "kv_b": kv_b, "b_b_q": b_b_q[0], "b_sinks": b_sinks, "b_b_out": b_b_out,
          "ple_b_gate": ple_b_gate}

    scatters, small_parts = [], {}

    def emit(grads, small=None):
        names = list(grads)
        blocks = [OWNER_BLOCKS[n.rstrip("01")](grads[n]) for n in names]
        partials = [] if small is None else [small[n].reshape(ps) for n, _, ps in SMALL]
        lands = _sequencer_exchange(blocks + partials, ["scatter"] * len(blocks) + ["gather"] * len(partials),
                                    f"scatter{len(scatters)}", len(GATHER_GROUPS) + len(scatters))
        scatters.append(dict(zip(names, lands)))
        small_parts.update(zip([n for n, _, _ in SMALL], lands[len(blocks):]))
        return blocks

    loss, grad_x, gs = _local_step(x[0], p[:, 0], loss_target[0], getw, sm, emit)

    out, parts, last = {}, {}, [grad_x] + list(scatters[0].values())
    for i, landed in enumerate(scatters):
        parts.update(landed)
        for n, key in ADAM_AFTER.get(i, ()):
            lrc = (1,) * (3 - w[n].ndim) + w[n].shape
            layers = [parts[key]] if key in parts else [parts[key + "0"], parts[key + "1"]]
            shard = (lambda a: a.reshape(lrc).swapaxes(1, 2)) if key == "gu" else (lambda a: a.reshape(lrc))
            res = _adam_big(shard(w[n]), layers, shard(m[n]), shard(v[n]), "adam_" + n, after=last)
            out[n] = [(r.swapaxes(1, 2) if key == "gu" else r).reshape(w[n].shape) for r in res]
            last = [res[3]]
    out.update(_adam_small(small_parts, w, m, v, after=last))

    loss = lax.psum(loss[0, 0], ("x", "y", "c"))
    res = [loss, grad_x[None]]
    for i in range(4):
        res += [out[n][i] for n in WEIGHTS]
    return tuple(res)
```

```python
import jax
import jax.numpy as jnp
from jax import lax
from jax.experimental import pallas as pl
from jax.experimental.pallas import tpu as pltpu
from jax.experimental.pallas import tpu_sc as plsc

F32 = jnp.float32
CDT = jnp.bfloat16

N_DEV = 8
D = 1024
HG_H, HG_DK, HG_CH = 8, 128, 64
HG_HPB = 4
HG_CPB = 8
ATT_HD, ATT_QH, ATT_KVH, ATT_G, WINDOW = 64, 16, 4, 4, 128
FFN_H = 2816
FFN_B = FFN_H // 4
PLE_DIM = 256
ALPHA = (2.0 * 2) ** 0.25
LN_EPS = 1e-5
RMS_EPS = 1e-6
ADAM_LR, ADAM_B1, ADAM_B2, ADAM_EPS, ADAM_WD, ADAM_STEP = 0.001, 0.9, 0.999, 1e-08, 0.01, 10
ROW_TILES = (512, 256, 128, 64)
VMEM_LIMIT = 48 * 1024 * 1024
NEG = -1e30

MESH = pl.DeviceIdType.MESH


def _tile(n, cands=ROW_TILES):
    for t in cands:
        if n % t == 0:
            return t
    return n


def _sds(shape, dtype):
    return jax.ShapeDtypeStruct(tuple(shape), dtype)


def _params(sem):
    return pltpu.CompilerParams(dimension_semantics=sem, vmem_limit_bytes=VMEM_LIMIT)


def _dot(a, b):
    return jnp.dot(a.astype(CDT), b.astype(CDT), preferred_element_type=F32)


def _dot_nt(a, b):
    return lax.dot_general(a.astype(CDT), b.astype(CDT), (((1,), (1,)), ((), ())), preferred_element_type=F32)


def _dot_tn(a, b):
    return lax.dot_general(a.astype(CDT), b.astype(CDT), (((0,), (0,)), ((), ())), preferred_element_type=F32)


def _sigmoid(x):
    return jax.nn.sigmoid(x)


def _ln_fwd(z, g, b):
    mu = jnp.mean(z, axis=-1, keepdims=True)
    zc = z - mu
    var = jnp.mean(zc * zc, axis=-1, keepdims=True)
    return zc * lax.rsqrt(var + LN_EPS) * g + b


def _ln_bwd(z, g, dy):
    mu = jnp.mean(z, axis=-1, keepdims=True)
    zc = z - mu
    var = jnp.mean(zc * zc, axis=-1, keepdims=True)
    rstd = lax.rsqrt(var + LN_EPS)
    xhat = zc * rstd
    dxh = dy * g
    dz = rstd * (dxh - jnp.mean(dxh, axis=-1, keepdims=True) - xhat * jnp.mean(dxh * xhat, axis=-1, keepdims=True))
    return dz, xhat


def _colsum(x):
    return jnp.sum(x, axis=0, keepdims=True)


def _acc(ref, val, first):
    @pl.when(first)
    def _():
        ref[...] = val

    @pl.when(jnp.logical_not(first))
    def _():
        ref[...] += val


def _zero_at(ref, first):
    @pl.when(first)
    def _():
        ref[...] = jnp.zeros_like(ref)


def _call(after, body, **kw):
    after = [] if after is None else list(after)
    specs = list(kw["in_specs"])
    kw["in_specs"] = [pl.BlockSpec(memory_space=pl.ANY)] * len(after) + specs

    def ordered_body(*refs):
        body(*refs[len(after):])

    call = pl.pallas_call(ordered_body, **kw)

    def pinned(*args):
        args = [a if s.memory_space is not None else pltpu.with_memory_space_constraint(a, pltpu.HBM)
                for a, s in zip(args, specs)]
        return call(*after, *args)

    return pinned


def _mm_nn(a, b3, out_shape, oblock, omap, out_dtype, bias3=None, name="mm_nn"):
    M, K = a.shape
    G, _, Nb = b3.shape
    tm = _tile(M)

    def body(a_ref, b_ref, *rest):
        o_ref = rest[-1]
        acc = _dot(a_ref[...], b_ref[...])
        if bias3 is not None:
            acc = acc + rest[0][...]
        o_ref[...] = acc.astype(o_ref.dtype)

    in_specs = [pl.BlockSpec((tm, K), lambda g, i: (i, 0)), pl.BlockSpec((None, K, Nb), lambda g, i: (g, 0, 0))]
    args = [a, b3]
    if bias3 is not None:
        in_specs.append(pl.BlockSpec((None, 1, Nb), lambda g, i: (g, 0, 0)))
        args.append(bias3)
    return _call(
        None, body, grid=(G, M // tm), in_specs=in_specs, out_specs=pl.BlockSpec(oblock, omap),
        out_shape=_sds(out_shape, out_dtype), compiler_params=_params(("arbitrary", "arbitrary")), name=name)(*args)


def _mm_tn(a3, b3, G, amap, bmap, ablock, bblock, out_shape, oblock, omap, name="mm_tn"):
    S = a3.shape[1]

    def body(a_ref, b_ref, o_ref):
        o_ref[...] = _dot_tn(a_ref[...], b_ref[...]).astype(o_ref.dtype)

    return _call(
        None, body, grid=(G, 1),
        in_specs=[pl.BlockSpec(ablock(S), amap), pl.BlockSpec(bblock(S), bmap)],
        out_specs=pl.BlockSpec(oblock, omap), out_shape=_sds(out_shape, CDT),
        compiler_params=_params(("arbitrary", "arbitrary")), name=name)(a3, b3)


def _wgrad(a3, b3, name):
    Ga, S, M = a3.shape
    Gb, _, N = b3.shape
    G = max(Ga, Gb)
    return _mm_tn(
        a3, b3, G,
        (lambda g, k: (g, k, 0)) if Ga > 1 else (lambda g, k: (0, k, 0)),
        (lambda g, k: (g, k, 0)) if Gb > 1 else (lambda g, k: (0, k, 0)),
        lambda tk: (None, tk, M), lambda tk: (None, tk, N),
        (G, M, N), (None, M, N), lambda g, k: (g, 0, 0), name=name)


def _mixout_ln(u3, w3, bias, xin, gain, beta, name):
    G, S, Kb = u3.shape
    tm = _tile(S)

    def body(u_ref, w_ref, b_ref, x_ref, g_ref, be_ref, z_ref, xo_ref, xob_ref):
        h = b_ref[...] + _dot(u_ref[0], w_ref[0])
        for g in range(1, G):
            h = h + _dot(u_ref[g], w_ref[g])
        z = ALPHA * x_ref[...] + h
        z_ref[...] = z
        y = _ln_fwd(z, g_ref[...], be_ref[...])
        xo_ref[...] = y
        xob_ref[...] = y.astype(CDT)

    row = pl.BlockSpec((tm, D), lambda i: (i, 0))
    vec = pl.BlockSpec((1, D), lambda i: (0, 0))
    return _call(
        None, body, grid=(S // tm,),
        in_specs=[pl.BlockSpec((G, tm, Kb), lambda i: (0, i, 0)), pl.BlockSpec((G, Kb, D), lambda i: (0, 0, 0)),
                  vec, row, vec, vec],
        out_specs=[row, row, row], out_shape=[_sds((S, D), F32), _sds((S, D), F32), _sds((S, D), CDT)],
        compiler_params=_params(("arbitrary",)), name=name)(u3, w3, bias, xin, gain, beta)


def _ffn_fwd(xin, xin_b, wgu, wdn, gain, beta, name):
    S = xin.shape[0]
    tm = _tile(S)

    def hidden(xb_ref, wgu_ref, gu_ref, hid_ref):
        xb = xb_ref[...]
        gate = _dot_nt(xb, wgu_ref[0])
        up = _dot_nt(xb, wgu_ref[1])
        gu_ref[0] = gate
        gu_ref[1] = up
        hid_ref[...] = (gate * _sigmoid(gate) * up).astype(CDT)

    gu, hid = _call(
        None, hidden, grid=(S // tm, 4),
        in_specs=[pl.BlockSpec((tm, D), lambda i, j: (i, 0)), pl.BlockSpec((2, None, FFN_B, D), lambda i, j: (0, j, 0, 0))],
        out_specs=[pl.BlockSpec((2, None, tm, FFN_B), lambda i, j: (0, j, i, 0)),
                   pl.BlockSpec((None, tm, FFN_B), lambda i, j: (j, i, 0))],
        out_shape=[_sds((2, 4, S, FFN_B), F32), _sds((4, S, FFN_B), CDT)],
        compiler_params=_params(("arbitrary", "arbitrary")), name=name + "_hidden")(xin_b, wgu)

    def down(x_ref, hid_ref, wdn_ref, g_ref, be_ref, z_ref, xo_ref, xob_ref):
        z = ALPHA * x_ref[...]
        for j in range(4):
            z = z + _dot(hid_ref[j], wdn_ref[j])
        z_ref[...] = z
        y = _ln_fwd(z, g_ref[...], be_ref[...])
        xo_ref[...] = y
        xob_ref[...] = y.astype(CDT)

    row = pl.BlockSpec((tm, D), lambda i: (i, 0))
    vec = pl.BlockSpec((1, D), lambda i: (0, 0))
    z, xo, xob = _call(
        None, down, grid=(S // tm,),
        in_specs=[row, pl.BlockSpec((4, tm, FFN_B), lambda i: (0, i, 0)), pl.BlockSpec((4, FFN_B, D), lambda i: (0, 0, 0)),
                  vec, vec],
        out_specs=[row, row, row], out_shape=[_sds((S, D), F32), _sds((S, D), F32), _sds((S, D), CDT)],
        compiler_params=_params(("arbitrary",)), name=name + "_down")(xin, hid, wdn, gain, beta)
    return gu, hid, z, xo, xob


def _ple_fwd(xin, xin_b, p_b, wpg, bgate, wpu, gain, beta, name):
    S = xin.shape[0]
    tm = _tile(S)

    def body(x_ref, xb_ref, p_ref, wpg_ref, bg_ref, wpu_ref, g_ref, be_ref, sg_ref, up_ref, z_ref, xo_ref, xob_ref):
        sg = _sigmoid(_dot(xb_ref[...], wpg_ref[...]) + bg_ref[...])
        pb = p_ref[...]
        up = jnp.concatenate([_dot(pb, wpu_ref[j]) for j in range(N_DEV)], axis=-1)
        sg_ref[...] = sg
        up_ref[...] = up
        z = ALPHA * x_ref[...] + sg * up
        z_ref[...] = z
        y = _ln_fwd(z, g_ref[...], be_ref[...])
        xo_ref[...] = y
        xob_ref[...] = y.astype(CDT)

    row = pl.BlockSpec((tm, D), lambda i: (i, 0))
    vec = pl.BlockSpec((1, D), lambda i: (0, 0))
    return _call(
        None, body, grid=(S // tm,),
        in_specs=[row, row, pl.BlockSpec((tm, PLE_DIM), lambda i: (i, 0)), pl.BlockSpec((D, D), lambda i: (0, 0)), vec,
                  pl.BlockSpec((N_DEV, PLE_DIM, D // N_DEV), lambda i: (0, 0, 0)), vec, vec],
        out_specs=[row] * 5,
        out_shape=[_sds((S, D), F32)] * 4 + [_sds((S, D), CDT)],
        compiler_params=_params(("arbitrary",)), name=name)(xin, xin_b, p_b, wpg, bgate, wpu, gain, beta)


def _loss_fwd_bwd(y, target):
    S = y.shape[0]
    tm = _tile(S)

    def body(y_ref, t_ref, l_ref, dy_ref):
        e = y_ref[...] - t_ref[...]
        dy_ref[...] = e * (1.0 / D)
        part = 0.5 * jnp.sum(jnp.sum(e * e, axis=-1, keepdims=True) * (1.0 / D), axis=0, keepdims=True)
        _acc(l_ref, part, pl.program_id(0) == 0)

    row = pl.BlockSpec((tm, D), lambda i: (i, 0))
    return _call(
        None, body, grid=(S // tm,), in_specs=[row, row],
        out_specs=[pl.BlockSpec((1, 1), lambda i: (0, 0)), row],
        out_shape=[_sds((1, 1), F32), _sds((S, D), F32)],
        compiler_params=_params(("arbitrary",)), name="loss")(y, target)


def _ple_bwd(dy, z, sg, up, gain, wpg, name, after=None):
    S = dy.shape[0]
    tm = _tile(S)

    def body(dy_ref, z_ref, sg_ref, up_ref, g_ref, wpg_ref, dx_ref, dgl_ref, dup_ref, dgain_ref, dbeta_ref, dbg_ref):
        first = pl.program_id(0) == 0
        dy_ = dy_ref[...]
        dz, xhat = _ln_bwd(z_ref[...], g_ref[...], dy_)
        sg_ = sg_ref[...]
        dgl = dz * up_ref[...] * sg_ * (1.0 - sg_)
        dgl_ref[...] = dgl.astype(CDT)
        dup_ref[...] = (dz * sg_).astype(CDT)
        dx_ref[...] = ALPHA * dz + _dot_nt(dgl, wpg_ref[...])
        _acc(dgain_ref, _colsum(dy_ * xhat), first)
        _acc(dbeta_ref, _colsum(dy_), first)
        _acc(dbg_ref, _colsum(dgl), first)

    row = pl.BlockSpec((tm, D), lambda i: (i, 0))
    vec = pl.BlockSpec((1, D), lambda i: (0, 0))
    return _call(
        after, body, grid=(S // tm,), in_specs=[row, row, row, row, vec, pl.BlockSpec((D, D), lambda i: (0, 0))],
        out_specs=[row, row, row, vec, vec, vec],
        out_shape=[_sds((S, D), F32), _sds((S, D), CDT), _sds((S, D), CDT)] + [_sds((1, D), F32)] * 3,
        compiler_params=_params(("arbitrary",)), name=name)(dy, z, sg, up, gain, wpg)


def _ffn_bwd(dy, z, gu, wgu, wdn, gain, name, after=None):
    S = dy.shape[0]
    tm = _tile(S)

    def hidden(dy_ref, z_ref, gu_ref, wdn_ref, g_ref, dz_ref, dzb_ref, dgu_ref, dgain_ref, dbeta_ref):
        i, j = pl.program_id(0), pl.program_id(1)

        @pl.when(j == 0)
        def _():
            dy_ = dy_ref[...]
            dz, xhat = _ln_bwd(z_ref[...], g_ref[...], dy_)
            dz_ref[...] = dz
            dzb_ref[...] = dz.astype(CDT)
            _acc(dgain_ref, _colsum(dy_ * xhat), i == 0)
            _acc(dbeta_ref, _colsum(dy_), i == 0)

        dhid = _dot_nt(dzb_ref[...], wdn_ref[...])
        gate, up = gu_ref[0], gu_ref[1]
        sg = _sigmoid(gate)
        dgu_ref[0] = (dhid * up * (sg * (1.0 + gate * (1.0 - sg)))).astype(CDT)
        dgu_ref[1] = (dhid * (gate * sg)).astype(CDT)

    row = pl.BlockSpec((tm, D), lambda i, j: (i, 0))
    vec = pl.BlockSpec((1, D), lambda i, j: (0, 0))
    dz, dzb, dgu, dgain, dbeta = _call(
        after, hidden, grid=(S // tm, 4),
        in_specs=[row, row, pl.BlockSpec((2, None, tm, FFN_B), lambda i, j: (0, j, i, 0)),
                  pl.BlockSpec((None, FFN_B, D), lambda i, j: (j, 0, 0)), vec],
        out_specs=[row, row, pl.BlockSpec((2, None, tm, FFN_B), lambda i, j: (0, j, i, 0)), vec, vec],
        out_shape=[_sds((S, D), F32), _sds((S, D), CDT), _sds((2, 4, S, FFN_B), CDT), _sds((1, D), F32),
                   _sds((1, D), F32)],
        compiler_params=_params(("arbitrary", "arbitrary")), name=name + "_hidden")(dy, z, gu, wdn, gain)

    def to_input(dz_ref, dgu_ref, wgu_ref, dx_ref):
        acc = ALPHA * dz_ref[...]
        for g in range(2):
            for j in range(4):
                acc = acc + _dot(dgu_ref[g, j], wgu_ref[g, j])
        dx_ref[...] = acc

    rows = pl.BlockSpec((tm, D), lambda i: (i, 0))
    dx = _call(
        None, to_input, grid=(S // tm,),
        in_specs=[rows, pl.BlockSpec((2, 4, tm, FFN_B), lambda i: (0, 0, i, 0)),
                  pl.BlockSpec((2, 4, FFN_B, D), lambda i: (0, 0, 0, 0))],
        out_specs=rows, out_shape=_sds((S, D), F32),
        compiler_params=_params(("arbitrary",)), name=name + "_input")(dz, dgu, wgu)
    return dx, dzb, dgu, dgain, dbeta


def _mixout_bwd(dy, z, gain, w3, du_dtype, name, after=None):
    S = dy.shape[0]
    G, Kb, _ = w3.shape
    tm = _tile(S)

    def body(dy_ref, z_ref, g_ref, w_ref, dz_ref, dzb_ref, du_ref, dgain_ref, dbeta_ref, dbias_ref):
        first = pl.program_id(0) == 0
        dy_ = dy_ref[...]
        dz, xhat = _ln_bwd(z_ref[...], g_ref[...], dy_)
        dz_ref[...] = dz
        dzb = dz.astype(CDT)
        dzb_ref[...] = dzb
        for g in range(G):
            du_ref[g] = _dot_nt(dzb, w_ref[g]).astype(du_ref.dtype)
        _acc(dgain_ref, _colsum(dy_ * xhat), first)
        _acc(dbeta_ref, _colsum(dy_), first)
        _acc(dbias_ref, _colsum(dz), first)

    row = pl.BlockSpec((tm, D), lambda i: (i, 0))
    vec = pl.BlockSpec((1, D), lambda i: (0, 0))
    return _call(
        after, body, grid=(S // tm,), in_specs=[row, row, vec, pl.BlockSpec((G, Kb, D), lambda i: (0, 0, 0))],
        out_specs=[row, row, pl.BlockSpec((G, tm, Kb), lambda i: (0, i, 0)), vec, vec, vec],
        out_shape=[_sds((S, D), F32), _sds((S, D), CDT), _sds((G, S, Kb), du_dtype)] + [_sds((1, D), F32)] * 3,
        compiler_params=_params(("arbitrary",)), name=name)(dy, z, gain, w3)


def _half_select(low):
    r = lax.broadcasted_iota(jnp.int32, (2 * ATT_HD, ATT_HD), 0)
    c = lax.broadcasted_iota(jnp.int32, (2 * ATT_HD, ATT_HD), 1)
    return (r == c + (0 if low else ATT_HD)).astype(CDT)


def _half_place(low):
    r = lax.broadcasted_iota(jnp.int32, (ATT_HD, 2 * ATT_HD), 0)
    c = lax.broadcasted_iota(jnp.int32, (ATT_HD, 2 * ATT_HD), 1)
    return (c == r + (0 if low else ATT_HD)).astype(CDT)


def _pair_lanes(even, odd):
    return (jnp.dot(even, _half_place(True), preferred_element_type=F32)
            + jnp.dot(odd, _half_place(False), preferred_element_type=F32)).astype(CDT)


def _proj_heads(a, w, bias, heads, name):
    S, K = a.shape
    N = heads * ATT_HD
    tm = _tile(S)

    def body(a_ref, w_ref, b_ref, o_ref):
        acc = (_dot(a_ref[...], w_ref[...]) + b_ref[...]).astype(CDT)
        sel = (_half_select(True), _half_select(False))
        for h in range(heads):
            pair = acc[:, (h // 2) * 2 * ATT_HD:(h // 2 + 1) * 2 * ATT_HD]
            o_ref[h] = jnp.dot(pair, sel[h % 2], preferred_element_type=F32).astype(CDT)

    return _call(
        None, body, grid=(S // tm,),
        in_specs=[pl.BlockSpec((tm, K), lambda i: (i, 0)), pl.BlockSpec((K, N), lambda i: (0, 0)),
                  pl.BlockSpec((1, N), lambda i: (0, 0))],
        out_specs=pl.BlockSpec((heads, tm, ATT_HD), lambda i: (0, i, 0)), out_shape=_sds((heads, S, ATT_HD), CDT),
        compiler_params=_params(("arbitrary",)), name=name)(a, w, bias)


def _qkv_bwd(dz, dq, dkv4, wq, wkv, name, after=None):
    S = dz.shape[0]
    tm = _tile(S)
    HK = dkv4.shape[0]
    NK = HK * ATT_HD

    def body(dz_ref, dq_ref, dkv_ref, wq_ref, wkv_ref, dx_ref, dkvn_ref, dkvb_ref):
        first = pl.program_id(0) == 0
        dkvn = jnp.concatenate([_pair_lanes(dkv_ref[2 * i].astype(CDT), dkv_ref[2 * i + 1].astype(CDT))
                                for i in range(HK // 2)], axis=-1)
        dkvn_ref[...] = dkvn
        dx_ref[...] = ALPHA * dz_ref[...] + _dot_nt(dq_ref[...], wq_ref[...]) + _dot_nt(dkvn, wkv_ref[...])
        for h in range(HK):
            _acc(dkvb_ref.at[h], _colsum(dkv_ref[h]), first)

    row = pl.BlockSpec((tm, D), lambda i: (i, 0))
    return _call(
        after, body, grid=(S // tm,),
        in_specs=[row, row, pl.BlockSpec((HK, tm, ATT_HD), lambda i: (0, i, 0)),
                  pl.BlockSpec((D, D), lambda i: (0, 0)), pl.BlockSpec((D, NK), lambda i: (0, 0))],
        out_specs=[row, pl.BlockSpec((tm, NK), lambda i: (i, 0)), pl.BlockSpec((HK, 1, ATT_HD), lambda i: (0, 0, 0))],
        out_shape=[_sds((S, D), F32), _sds((S, NK), CDT), _sds((HK, 1, ATT_HD), F32)],
        compiler_params=_params(("arbitrary",)), name=name)(dz, dq, dkv4, wq, wkv)


def _inproj_bwd(dz, dproj, wain, name, after=None):
    S = dz.shape[0]
    tm = _tile(S)
    nb = wain.shape[-1]

    def body(dz_ref, dp_ref, w_ref, dx_ref):
        acc = ALPHA * dz_ref[...]
        for j in range(N_DEV):
            acc = acc + _dot_nt(dp_ref[j // 2, :, pl.ds((j % 2) * nb, nb)], w_ref[j])
        dx_ref[...] = acc

    row = pl.BlockSpec((tm, D), lambda i: (i, 0))
    return _call(
        after, body, grid=(S // tm,),
        in_specs=[row, pl.BlockSpec((4, tm, D), lambda i: (0, i, 0)), pl.BlockSpec((N_DEV, D, nb), lambda i: (0, 0, 0))],
        out_specs=row, out_shape=_sds((S, D), F32),
        compiler_params=_params(("arbitrary",)), name=name)(dz, dproj, wain)


def _running_sum(x, reverse=False):
    rows = x.shape[0]
    row = lax.broadcasted_iota(jnp.int32, x.shape, 0)
    step = 1
    while step < rows:
        if reverse:
            x = x + jnp.where(row < rows - step, pltpu.roll(x, rows - step, 0), 0.0)
        else:
            x = x + jnp.where(row >= step, pltpu.roll(x, step, 0), 0.0)
        step *= 2
    return x


def _hg_gates(q, f, alb_ref):
    a0, a1 = alb_ref[0:1, :], alb_ref[1:2, :]
    mx = jnp.maximum(a0, a1)
    e0, e1 = jnp.exp(a0 - mx), jnp.exp(a1 - mx)
    lb = e0 / (e0 + e1)
    sig = _sigmoid(f)
    forget = lb + (1.0 - lb) * sig
    k = (1.0 - lb) * _sigmoid(-f)
    qs = q * _sigmoid(q) * (HG_DK ** -0.5)
    return qs, k, jnp.log(forget), sig, lb, forget


def _hg_intra(qs, k, b, b_scr):
    b_scr[...] = b
    bm = b_scr[pl.ds(HG_CH // 2 - 1, 1), :]
    bl = b_scr[pl.ds(HG_CH - 1, 1), :]
    eb = jnp.exp(b)
    qb = qs * eb
    e_q = jnp.exp(b - bm)
    e_k = jnp.exp(bm - b)
    e_d = jnp.exp(bl - b)
    return qb, qs * e_q, k * e_k, k * e_d, jnp.exp(bl), eb, e_q, e_k, e_d


def _hgrn_fwd(proj, alb, ngain):
    S = proj.shape[1]
    nc = S // HG_CH
    nb = nc // HG_CPB
    rb, wb = HG_CPB * HG_CH, HG_HPB * HG_DK

    def body(pj_ref, alb_ref, ng_ref, o_ref, y_ref, st_ref, st_scr, b_scr):
        n = pl.program_id(1)

        @pl.when(n == 0)
        def _():
            st_scr[...] = jnp.zeros_like(st_scr)

        r = lax.broadcasted_iota(jnp.int32, (HG_CH, HG_CH), 0)
        c = lax.broadcasted_iota(jnp.int32, (HG_CH, HG_CH), 1)
        causal = r >= c
        for ci, j in [(ci, j) for ci in range(HG_CPB) for j in range(HG_HPB)]:
            rows, lanes = pl.ds(ci * HG_CH, HG_CH), pl.ds(j * HG_DK, HG_DK)
            q, f, v, g = pj_ref[0, rows, lanes], pj_ref[1, rows, lanes], pj_ref[2, rows, lanes], pj_ref[3, rows, lanes]
            qs, k, logf, _, _, _ = _hg_gates(q, f, alb_ref.at[:, lanes])
            b = _running_sum(logf)
            qb, qt, kt, kd, ebl, _, _, _, _ = _hg_intra(qs, k, b, b_scr.at[j, ci])
            st = st_scr[j]
            st_ref[j, ci] = st
            a = jnp.where(causal, _dot_nt(qt, kt), 0.0)
            o = _dot(a, v) + _dot_nt(qb, st)
            st_scr[j] = st * ebl + _dot_tn(v, kd)
            o_ref[rows, lanes] = o
            rinv = lax.rsqrt(jnp.mean(o * o, axis=-1, keepdims=True) + RMS_EPS)
            y_ref[rows, lanes] = (o * rinv * ng_ref[...] * (g * _sigmoid(g))).astype(CDT)

    blk = pl.BlockSpec((rb, wb), lambda h, n: (n, h))
    return _call(
        None, body, grid=(HG_H // HG_HPB, nb),
        in_specs=[pl.BlockSpec((4, rb, wb), lambda h, n: (0, n, h)), pl.BlockSpec((2, wb), lambda h, n: (0, h)),
                  pl.BlockSpec((1, HG_DK), lambda h, n: (0, 0))],
        out_specs=[blk, blk, pl.BlockSpec((HG_HPB, HG_CPB, HG_DK, HG_DK), lambda h, n: (h, n, 0, 0))],
        out_shape=[_sds((S, D), F32), _sds((S, D), CDT), _sds((HG_H, nc, HG_DK, HG_DK), F32)],
        scratch_shapes=[pltpu.VMEM((HG_HPB, HG_DK, HG_DK), F32), pltpu.VMEM((HG_HPB, HG_CPB, HG_CH, HG_DK), F32)],
        compiler_params=_params(("arbitrary", "arbitrary")), name="hgrn_fwd")(proj, alb, ngain)


def _hgrn_bwd(proj, alb, ngain, o, states, dy):
    S = proj.shape[1]
    nc = S // HG_CH
    nb = nc // HG_CPB
    rb, wb = HG_CPB * HG_CH, HG_HPB * HG_DK

    def body(pj_ref, alb_ref, ng_ref, o_ref, st_ref, dy_ref, dpj_ref, dalb_ref, dng_ref, dst_scr, b_scr):
        h, n = pl.program_id(0), pl.program_id(1)

        @pl.when(n == 0)
        def _():
            dst_scr[...] = jnp.zeros_like(dst_scr)

        ng = ng_ref[...]
        r = lax.broadcasted_iota(jnp.int32, (HG_CH, HG_CH), 0)
        c = lax.broadcasted_iota(jnp.int32, (HG_CH, HG_CH), 1)
        causal = r >= c
        dng = None
        for ci, j in [(ci, j) for ci in reversed(range(HG_CPB)) for j in range(HG_HPB)]:
            rows, lanes = pl.ds(ci * HG_CH, HG_CH), pl.ds(j * HG_DK, HG_DK)
            q, f, v, g = pj_ref[0, rows, lanes], pj_ref[1, rows, lanes], pj_ref[2, rows, lanes], pj_ref[3, rows, lanes]
            o_ = o_ref[rows, lanes]
            dy_ = dy_ref[rows, lanes]
            sg = _sigmoid(g)
            rinv = lax.rsqrt(jnp.mean(o_ * o_, axis=-1, keepdims=True) + RMS_EPS)
            nrm = o_ * rinv
            dr = dy_ * (g * sg)
            dg = dy_ * nrm * ng * (sg * (1.0 + g * (1.0 - sg)))
            dn = dr * ng
            do = rinv * (dn - nrm * jnp.mean(dn * nrm, axis=-1, keepdims=True))
            dng = _colsum(dr * nrm) if dng is None else dng + _colsum(dr * nrm)
            qs, k, logf, sig, lb, forget = _hg_gates(q, f, alb_ref.at[:, lanes])
            b = _running_sum(logf)
            qb, qt, kt, kd, ebl, eb, e_q, e_k, e_d = _hg_intra(qs, k, b, b_scr.at[j, ci])
            st = st_ref[j, ci]
            dstn = dst_scr[j]
            qt, kt, qb, kd = (t.astype(CDT).astype(F32) for t in (qt, kt, qb, kd))
            a = jnp.where(causal, _dot_nt(qt, kt), 0.0)
            da = jnp.where(causal, _dot_nt(do, v), 0.0)
            dv = _dot_tn(a, do) + _dot_nt(kd, dstn)
            dqb = _dot(do, st)
            dkd = _dot(v, dstn)
            dqt = _dot(da, kt)
            dkt = _dot_tn(da, qt)
            dbl = _colsum(dkd * kd) + ebl * _colsum(dstn * st)
            dst_scr[j] = dstn * ebl + _dot_tn(do, qb)
            dqs = dqt * e_q + dqb * eb
            dk = dkt * e_k + dkd * e_d
            db = dqt * qt + dqb * qb - dkt * kt - dkd * kd
            dlogf = _running_sum(db, reverse=True) + dbl
            dforget = dlogf / forget
            dsig = (1.0 - lb) * (dforget - dk)
            df = dsig * sig * (1.0 - sig)
            dlb = _colsum((dforget - dk) * (1.0 - sig))
            sq = _sigmoid(q)
            dq = dqs * (HG_DK ** -0.5) * (sq * (1.0 + q * (1.0 - sq)))
            dpj_ref[0, rows, lanes] = dq.astype(CDT)
            dpj_ref[1, rows, lanes] = df.astype(CDT)
            dpj_ref[2, rows, lanes] = dv.astype(CDT)
            dpj_ref[3, rows, lanes] = dg.astype(CDT)
            da0 = dlb * lb * (1.0 - lb)
            first = jnp.logical_and(n == 0, ci == HG_CPB - 1)
            _acc(dalb_ref.at[pl.ds(0, 1), lanes], da0, first)
            _acc(dalb_ref.at[pl.ds(1, 1), lanes], -da0, first)
        _acc(dng_ref, dng, jnp.logical_and(h == 0, n == 0))

    blk = pl.BlockSpec((rb, wb), lambda h, n: (nb - 1 - n, h))
    pj = pl.BlockSpec((4, rb, wb), lambda h, n: (0, nb - 1 - n, h))
    alb_blk = pl.BlockSpec((2, wb), lambda h, n: (0, h))
    ng_blk = pl.BlockSpec((1, HG_DK), lambda h, n: (0, 0))
    return _call(
        None, body, grid=(HG_H // HG_HPB, nb),
        in_specs=[pj, alb_blk, ng_blk, blk,
                  pl.BlockSpec((HG_HPB, HG_CPB, HG_DK, HG_DK), lambda h, n: (h, nb - 1 - n, 0, 0)), blk],
        out_specs=[pj, alb_blk, ng_blk],
        out_shape=[_sds((4, S, D), CDT), _sds((2, D), F32), _sds((1, HG_DK), F32)],
        scratch_shapes=[pltpu.VMEM((HG_HPB, HG_DK, HG_DK), F32), pltpu.VMEM((HG_HPB, HG_CPB, HG_CH, HG_DK), F32)],
        compiler_params=_params(("arbitrary", "arbitrary")), name="hgrn_bwd")(proj, alb, ngain, o, states, dy)


def _slope(h):
    return 2.0 ** (-8.0 * (h + 1) / ATT_QH)


def _attn_mask(n):
    qi = lax.broadcasted_iota(jnp.int32, (WINDOW, 2 * WINDOW), 0)
    si = lax.broadcasted_iota(jnp.int32, (WINDOW, 2 * WINDOW), 1)
    dist = qi - si + WINDOW
    valid = (dist >= 0) & (dist < WINDOW) & (n * WINDOW - WINDOW + si >= 0)
    return valid, dist.astype(F32)


def _attn_probs(qh, kh, sink, slope, valid, distf):
    s = _dot_nt(qh, kh) * (ATT_HD ** -0.5) - slope * distf
    s = jnp.where(valid, s, NEG)
    m = jnp.maximum(jnp.max(s, axis=-1, keepdims=True), sink)
    e = jnp.exp(s - m)
    es = jnp.exp(sink - m)
    inv = 1.0 / (jnp.sum(e, axis=-1, keepdims=True) + es)
    return e * inv, es * inv


def _attn_specs(S):
    nb = S // WINDOW
    cur = lambda H: pl.BlockSpec((H, WINDOW, ATT_HD), lambda n: (0, n, 0))
    prev = lambda H: pl.BlockSpec((H, WINDOW, ATT_HD), lambda n: (0, jnp.maximum(n - 1, 0), 0))
    return nb, cur, prev


def _attn_fwd(q4, kv4, sinks):
    S = q4.shape[1]
    nb, cur, prev = _attn_specs(S)

    def body(sink_ref, q_ref, kvc_ref, kvp_ref, o_ref):
        valid, distf = _attn_mask(pl.program_id(0))
        outs = []
        for h in range(ATT_QH):
            kvh = h // ATT_G
            kh = jnp.concatenate([kvp_ref[kvh], kvc_ref[kvh]], axis=0)
            vh = jnp.concatenate([kvp_ref[ATT_KVH + kvh], kvc_ref[ATT_KVH + kvh]], axis=0)
            p, _ = _attn_probs(q_ref[h], kh, sink_ref[0, h], _slope(h), valid, distf)
            outs.append(_dot(p, vh).astype(CDT))
            if h % 2:
                o_ref[:, pl.ds((h - 1) * ATT_HD, 2 * ATT_HD)] = _pair_lanes(outs[h - 1], outs[h])

    return _call(
        None, body, grid=(nb,),
        in_specs=[pl.BlockSpec(memory_space=pltpu.SMEM), cur(ATT_QH), cur(2 * ATT_KVH), prev(2 * ATT_KVH)],
        out_specs=pl.BlockSpec((WINDOW, D), lambda n: (n, 0)), out_shape=_sds((S, D), CDT),
        compiler_params=_params(("arbitrary",)), name="attn_fwd")(sinks, q4, kv4, kv4)


def _attn_bwd(q4, kv4, sinks, do):
    S = q4.shape[1]
    nb, cur, prev = _attn_specs(S)

    def body(sink_ref, q_ref, kvc_ref, kvp_ref, do_ref, dq_ref, dkv_ref, dbq_ref, dsink_ref):
        n = pl.program_id(0)
        first = n == 0

        @pl.when(first)
        def _():
            dkv_ref[...] = jnp.zeros_like(dkv_ref)
            dsink_ref[...] = jnp.zeros_like(dsink_ref)

        valid, distf = _attn_mask(n)
        lane = lax.broadcasted_iota(jnp.int32, (1, 128), 1)
        rows_cur = pl.ds(pl.multiple_of(n * WINDOW, WINDOW), WINDOW)
        rows_prev = pl.ds(pl.multiple_of(jnp.maximum(n - 1, 0) * WINDOW, WINDOW), WINDOW)
        dsinks = jnp.zeros((1, 128), F32)
        sel = (_half_select(True), _half_select(False))
        for kvh in range(ATT_KVH):
            kh = jnp.concatenate([kvp_ref[kvh], kvc_ref[kvh]], axis=0)
            vh = jnp.concatenate([kvp_ref[ATT_KVH + kvh], kvc_ref[ATT_KVH + kvh]], axis=0)
            dk = dv = None
            dqs = []
            for h in range(kvh * ATT_G, (kvh + 1) * ATT_G):
                qh = q_ref[h]
                doh = jnp.dot(do_ref[:, pl.ds((h // 2) * 2 * ATT_HD, 2 * ATT_HD)], sel[h % 2],
                              preferred_element_type=F32).astype(CDT)
                p, ps = _attn_probs(qh, kh, sink_ref[0, h], _slope(h), valid, distf)
                dp = _dot_nt(doh, vh)
                dd = jnp.sum(p * dp, axis=-1, keepdims=True)
                ds = p * (dp - dd)
                dsinks = dsinks + jnp.where(lane == h, -jnp.sum(ps * dd, axis=0, keepdims=True), 0.0)
                dqh = _dot(ds, kh) * (ATT_HD ** -0.5)
                dqs.append(dqh.astype(CDT))
                _acc(dbq_ref.at[h], _colsum(dqh), first)
                dkh = _dot_tn(ds, qh) * (ATT_HD ** -0.5)
                dvh = _dot_tn(p, doh)
                dk = dkh if dk is None else dk + dkh
                dv = dvh if dv is None else dv + dvh
            for i in range(ATT_G // 2):
                lanes = pl.ds((kvh * ATT_G + 2 * i) * ATT_HD, 2 * ATT_HD)
                dq_ref[:, lanes] = _pair_lanes(dqs[2 * i], dqs[2 * i + 1])
            dkv_ref[kvh, rows_prev, :] += dk[:WINDOW]
            dkv_ref[kvh, rows_cur, :] += dk[WINDOW:]
            dkv_ref[ATT_KVH + kvh, rows_prev, :] += dv[:WINDOW]
            dkv_ref[ATT_KVH + kvh, rows_cur, :] += dv[WINDOW:]
        dsink_ref[...] += dsinks

    return _call(
        None, body, grid=(nb,),
        in_specs=[pl.BlockSpec(memory_space=pltpu.SMEM), cur(ATT_QH), cur(2 * ATT_KVH), prev(2 * ATT_KVH),
                  pl.BlockSpec((WINDOW, D), lambda n: (n, 0))],
        out_specs=[pl.BlockSpec((WINDOW, D), lambda n: (n, 0)), pl.BlockSpec((2 * ATT_KVH, S, ATT_HD), lambda n: (0, 0, 0)),
                   pl.BlockSpec((ATT_QH, 1, ATT_HD), lambda n: (0, 0, 0)), pl.BlockSpec((1, 128), lambda n: (0, 0))],
        out_shape=[_sds((S, D), CDT), _sds((2 * ATT_KVH, S, ATT_HD), F32), _sds((ATT_QH, 1, ATT_HD), F32),
                   _sds((1, 128), F32)],
        compiler_params=_params(("arbitrary",)), name="attn_bwd")(sinks, q4, kv4, kv4, do)


def _local_step(x, p, target, getw, sm, emit):
    S = x.shape[0]
    vec = lambda a: a.reshape(1, -1)
    ln_g = lambda l, k: vec(sm["ln_gain"][l, k])
    ln_b = lambda l, k: vec(sm["ln_bias"][l, k])
    xb = x.astype(CDT)
    pb = p.astype(CDT)

    proj = _mm_nn(xb, getw("a_w_in", None), (4, S, D), (None, _tile(S), 512), lambda g, i: (g // 2, i, g % 2), F32,
                  name="a_in")
    o_a, y_a, states = _hgrn_fwd(proj, sm["a_lower_bound"], sm["a_norm_gain"])
    zeros = jnp.zeros((1, D), F32)
    z = [[None] * 3 for _ in range(2)]
    xs = [[None] * 3 for _ in range(2)]
    xbs = [[None] * 3 for _ in range(2)]
    z[0][0], xs[0][0], xbs[0][0] = _mixout_ln(y_a[None], getw("a_w_out", y_a)[None], zeros, x, ln_g(0, 0), ln_b(0, 0),
                                              "a_out_ln")
    gu, hid, sgs, ups = [None, None], [None, None], [None, None], [None, None]

    def ffn_ple(l):
        wgu = getw(f"gu{l}", xbs[l][0])
        gu[l], hid[l], z[l][1], xs[l][1], xbs[l][1] = _ffn_fwd(
            xs[l][0], xbs[l][0], wgu, getw(f"dn{l}", None), ln_g(l, 1), ln_b(l, 1), f"ffn_fwd{l}")
        sgs[l], ups[l], z[l][2], xs[l][2], xbs[l][2] = _ple_fwd(
            xs[l][1], xbs[l][1], pb[l], getw(f"pg{l}", None), vec(sm["ple_b_gate"][l]), getw(f"pu{l}", None), ln_g(l, 2),
            ln_b(l, 2), f"ple_fwd{l}")

    ffn_ple(0)
    x3, x3b = xs[0][2], xbs[0][2]
    w_kv, w_q, w_bo = getw("kv_w", x3b), getw("b_w_q", None), getw("b_w_out", None)
    kv4 = _proj_heads(x3b, w_kv, vec(sm["kv_b"]), 2 * ATT_KVH, "kv_proj")
    q4 = _proj_heads(x3b, w_q, vec(sm["b_b_q"]), ATT_QH, "q_proj")
    o_b = _attn_fwd(q4, kv4, sm["b_sinks"])
    z[1][0], xs[1][0], xbs[1][0] = _mixout_ln(o_b[None], w_bo[None], sm["b_b_out"], x3, ln_g(1, 0), ln_b(1, 0),
                                              "b_out_ln")
    ffn_ple(1)
    loss, dy = _loss_fwd_bwd(xs[1][2], target)

    gs = {}
    d_ln_g = [[None] * 3 for _ in range(2)]
    d_ln_b = [[None] * 3 for _ in range(2)]
    g_bg = [None, None]

    def ffn_ple_bwd(l, dy, after=None):
        dx2, dgl, dup, d_ln_g[l][2], d_ln_b[l][2], g_bg[l] = _ple_bwd(dy, z[l][2], sgs[l], ups[l], ln_g(l, 2),
                                                                     getw(f"pg{l}", None), f"ple_bwd{l}", after=after)
        g_pg = _wgrad(xbs[l][1][None], dgl[None], f"g_ple_gate{l}")[0]
        g_pu = _wgrad(pb[l][None], dup[None], f"g_ple_up{l}")[0]
        dx1, dzb, dgu, d_ln_g[l][1], d_ln_b[l][1] = _ffn_bwd(dx2, z[l][1], gu[l], getw(f"gu{l}", None),
                                                           getw(f"dn{l}", None), ln_g(l, 1), f"ffn_bwd{l}")
        g_dn = _wgrad(hid[l], dzb[None], f"g_ffn_down{l}")
        g_gu = _wgrad(dgu.reshape(8, S, FFN_B), xbs[l][0][None], f"g_ffn_gate_up{l}")
        return dx1, emit({f"pg{l}": g_pg, f"pu{l}": g_pu, f"dn{l}": g_dn, f"gu{l}": g_gu})

    dx1, tok = ffn_ple_bwd(1, dy)
    dz, dzb, do, d_ln_g[1][0], d_ln_b[1][0], gs["b_b_out"] = _mixout_bwd(dx1, z[1][0], ln_g(1, 0), w_bo[None], CDT,
                                                                        "b_out_bwd", after=tok)
    g_bo = _wgrad(o_b[None], dzb[None], "g_b_w_out")[0]
    dq, dkv4, dbq, dsinks = _attn_bwd(q4, kv4, sm["b_sinks"], do[0])
    gs["b_b_q"] = dbq
    gs["b_sinks"] = dsinks
    g_q = _wgrad(x3b[None], dq[None], "g_b_w_q")[0]
    dx3, dkv, gs["kv_b"] = _qkv_bwd(dz, dq, dkv4, w_q, w_kv, "qkv_bwd", after=tok)
    g_kv = _wgrad(x3b[None], dkv[None], "g_kv_w")[0]
    tok = emit({"b_w_out": g_bo, "b_w_q": g_q, "kv_w": g_kv})
    dx1, tok = ffn_ple_bwd(0, dx3, tok)
    w_ao = getw("a_w_out", None)
    dz, dzb, dyr, d_ln_g[0][0], d_ln_b[0][0], _ = _mixout_bwd(dx1, z[0][0], ln_g(0, 0), w_ao[None], F32, "a_out_bwd",
                                                              after=tok)
    g_ao = _wgrad(y_a[None], dzb[None], "g_a_w_out")[0]
    dproj, gs["a_lower_bound"], gs["a_norm_gain"] = _hgrn_bwd(proj, sm["a_lower_bound"], sm["a_norm_gain"], o_a, states,
                                                              dyr[0])
    tk = lambda t: (None, t, D)
    g_ain = _mm_tn(xb[None], dproj, N_DEV, lambda g, k: (0, k, 0), lambda g, k: (g // 2, k, g % 2),
                   tk, lambda t: (None, t, 512), (N_DEV, D, 512), (None, D, 512), lambda g, k: (g, 0, 0), name="g_a_w_in")
    gs["ple_b_gate"] = jnp.concatenate(g_bg, axis=0)
    gs["ln_gain"] = jnp.stack([jnp.concatenate(r, axis=0) for r in d_ln_g])
    gs["ln_bias"] = jnp.stack([jnp.concatenate(r, axis=0) for r in d_ln_b])
    tok = emit({"a_w_out": g_ao, "a_w_in": g_ain}, small=gs)
    grad_x = _inproj_bwd(dz, dproj, getw("a_w_in", None), "a_in_bwd", after=tok)
    return loss, grad_x, gs


def _peer(k):
    x, y, c = lax.axis_index("x"), lax.axis_index("y"), lax.axis_index("c")
    px = 1 - x if k & 4 else x
    py = 1 - y if k & 2 else y
    pc = 1 - c if k & 1 else c
    return (px, py, pc), 4 * px + 2 * py + pc


def _my_index():
    return 4 * lax.axis_index("x") + 2 * lax.axis_index("y") + lax.axis_index("c")


def _exchange(srcs, dst_shapes, plan, name):
    n_src, n_piece = len(srcs), len(plan)

    def body(*refs):
        src_refs, dst_refs = refs[:n_src], refs[n_src:n_src + len(dst_shapes)]
        send_sems, recv_sems, local_sems = refs[n_src + len(dst_shapes):]
        me = _my_index()

        def at(ref, idx):
            return ref.at[idx] if idx else ref

        local = []
        for t, (si, sfn, di, dfn) in enumerate(plan):
            cp = pltpu.make_async_copy(at(src_refs[si], sfn(me)), at(dst_refs[di], dfn(me)), local_sems.at[t])
            cp.start()
            local.append(cp)
        sends = []
        for k in range(1, N_DEV):
            peer, pid = _peer(k)
            for t, (si, sfn, di, dfn) in enumerate(plan):
                cp = pltpu.make_async_remote_copy(
                    src_ref=at(src_refs[si], sfn(pid)), dst_ref=at(dst_refs[di], dfn(me)),
                    send_sem=send_sems.at[t * 7 + k - 1], recv_sem=recv_sems.at[t * 7 + k - 1],
                    device_id=peer, device_id_type=MESH)
                cp.start()
                sends.append(cp)
        for k in range(1, N_DEV):
            peer, pid = _peer(k)
            for t, (si, sfn, di, dfn) in enumerate(plan):
                pltpu.make_async_remote_copy(
                    src_ref=at(src_refs[si], sfn(me)), dst_ref=at(dst_refs[di], dfn(pid)),
                    send_sem=send_sems.at[t * 7 + k - 1], recv_sem=recv_sems.at[t * 7 + k - 1],
                    device_id=peer, device_id_type=MESH).wait_recv()
        for cp in sends:
            cp.wait_send()
        for cp in local:
            cp.wait()

    hbm = pl.BlockSpec(memory_space=pltpu.HBM)
    return _call(
        None, body, in_specs=[hbm] * n_src, out_specs=[hbm] * len(dst_shapes), out_shape=dst_shapes,
        scratch_shapes=[pltpu.SemaphoreType.DMA((7 * n_piece,)), pltpu.SemaphoreType.DMA((7 * n_piece,)),
                        pltpu.SemaphoreType.DMA((n_piece,))],
        name=name)(*srcs)


def _gather(shards, name):
    dsts = [_sds((N_DEV,) + a.shape, a.dtype) for a in shards]
    plan = [(i, lambda j: (), i, lambda s: (s,)) for i in range(len(shards))]
    return _exchange(shards, dsts, plan, name)


_HBM = pl.BlockSpec(memory_space=pltpu.HBM)
_SEM = pl.BlockSpec(memory_space=pltpu.SEMAPHORE)
_DATAFLOW = pltpu.SideEffectType.DATAFLOW_SIDE_EFFECTING


def _piece_copy(mode, src, land, send_sems, recv_sems, t, k, sender, receiver, peer):
    return pltpu.make_async_remote_copy(
        src_ref=src if mode == "gather" else src.at[receiver], dst_ref=land.at[sender],
        send_sem=send_sems.at[t * 7 + k - 1], recv_sem=recv_sems.at[t * 7 + k - 1], device_id=peer, device_id_type=MESH)


def _sequencer_exchange(srcs, modes, name, collective_id, after=None):
    n = len(srcs)
    land_shapes = [((N_DEV,) + a.shape) if mode == "gather" else a.shape for a, mode in zip(srcs, modes)]
    extra = [] if after is None else [after]

    def body(*refs):
        src_refs, land_refs = refs[:n], refs[n + len(extra):2 * n + len(extra)]
        send_sems, recv_sems, local_sems = refs[2 * n + len(extra):]
        barrier = pltpu.get_barrier_semaphore()
        for k in range(1, N_DEV):
            pl.semaphore_signal(barrier, inc=1, device_id=_peer(k)[0], device_id_type=MESH)
        pl.semaphore_wait(barrier, N_DEV - 1)
        me = _my_index()
        local = []
        for i in range(n):
            cp = pltpu.make_async_copy(src_refs[i] if modes[i] == "gather" else src_refs[i].at[me], land_refs[i].at[me],
                                       local_sems.at[i])
            cp.start()
            local.append(cp)
        for k in range(1, N_DEV):
            peer, pid = _peer(k)
            for t in range(n):
                _piece_copy(modes[t], src_refs[t], land_refs[t], send_sems, recv_sems, t, k, me, pid, peer).start()
        for k in range(1, N_DEV):
            peer, pid = _peer(k)
            for t in range(n):
                _piece_copy(modes[t], src_refs[t], land_refs[t], send_sems, recv_sems, t, k, pid, me, peer).wait_recv()
        for k in range(1, N_DEV):
            peer, pid = _peer(k)
            for t in range(n):
                _piece_copy(modes[t], src_refs[t], land_refs[t], send_sems, recv_sems, t, k, me, pid, peer).wait_send()
        for cp in local:
            cp.wait()

    return pl.kernel(
        body, out_type=[_sds(s, a.dtype) for s, a in zip(land_shapes, srcs)],
        mesh=plsc.ScalarSubcoreMesh(axis_name="sequencer", num_cores=1),
        scratch_types=[pltpu.SemaphoreType.DMA((7 * n,)), pltpu.SemaphoreType.DMA((7 * n,)), pltpu.SemaphoreType.DMA((n,))],
        compiler_params=pltpu.CompilerParams(collective_id=collective_id), name=name)(*srcs, *extra)


def _sequencer_gather(srcs, name, collective_id, after=None):
    n = len(srcs)
    extra = [] if after is None else [after]

    def body(*refs):
        src_refs, land_refs = refs[:n], refs[n + len(extra):2 * n + len(extra)]
        send_sems, recv_sems, local_sems = refs[2 * n + len(extra):]
        x, y, c = lax.axis_index("x"), lax.axis_index("y"), lax.axis_index("c")
        sibling = (x, y, 1 - c)
        chips = [(1 - x, y), (x, 1 - y), (1 - x, 1 - y)]
        index = lambda px, py, pc: 4 * px + 2 * py + pc
        barrier = pltpu.get_barrier_semaphore()
        for peer in [sibling] + [(*chip, c) for chip in chips]:
            pl.semaphore_signal(barrier, inc=1, device_id=peer, device_id_type=MESH)
        pl.semaphore_wait(barrier, 4)

        def copy(t, k, slot, to, src=None):
            return pltpu.make_async_remote_copy(
                src_ref=land_refs[t].at[slot] if src is None else src, dst_ref=land_refs[t].at[slot],
                send_sem=send_sems.at[7 * t + k], recv_sem=recv_sems.at[7 * t + k], device_id=to, device_id_type=MESH)

        me = index(x, y, c)
        local = []
        for t in range(n):
            cp = pltpu.make_async_copy(src_refs[t], land_refs[t].at[me], local_sems.at[t])
            cp.start()
            local.append(cp)
        sends = []
        for t in range(n):
            sends.append(copy(t, 0, me, sibling, src=src_refs[t]))
            sends += [copy(t, 1 + j, me, (*chip, c), src=src_refs[t]) for j, chip in enumerate(chips)]
        for cp in sends:
            cp.start()
        for j, chip in enumerate(chips):
            for t in range(n):
                copy(t, 1 + j, index(*chip, c), sibling, src=src_refs[t]).wait_recv()
                passed = copy(t, 4 + j, index(*chip, c), sibling)
                passed.start()
                sends.append(passed)
        for t in range(n):
            copy(t, 0, index(x, y, 1 - c), sibling, src=src_refs[t]).wait_recv()
        for j, chip in enumerate(chips):
            for t in range(n):
                copy(t, 4 + j, index(*chip, 1 - c), sibling, src=src_refs[t]).wait_recv()
        for cp in sends:
            cp.wait_send()
        for cp in local:
            cp.wait()

    return pl.kernel(
        body, out_type=[_sds((N_DEV,) + a.shape, a.dtype) for a in srcs],
        mesh=plsc.ScalarSubcoreMesh(axis_name="sequencer", num_cores=1),
        scratch_types=[pltpu.SemaphoreType.DMA((7 * n,)), pltpu.SemaphoreType.DMA((7 * n,)), pltpu.SemaphoreType.DMA((n,))],
        compiler_params=pltpu.CompilerParams(collective_id=collective_id), name=name)(*srcs, *extra)


def _xstart(groups, mode, name, after=None):
    flat = [a for g in groups for a in g]
    n, ng = len(flat), len(groups)
    land_shapes = [((N_DEV,) + a.shape) if mode == "gather" else a.shape for a in flat]
    first = [sum(len(g) for g in groups[:i]) for i in range(ng)]
    extra = [] if after is None else [after]

    def body(*refs):
        srcs, lands = refs[:n], refs[n:2 * n]
        sems = refs[2 * n + len(extra):2 * n + len(extra) + 2 * ng]
        tok_ref, local_sems = refs[-2], refs[-1]
        me = _my_index()
        local = []
        for i in range(n):
            cp = pltpu.make_async_copy(srcs[i] if mode == "gather" else srcs[i].at[me], lands[i].at[me], local_sems.at[i])
            cp.start()
            local.append(cp)
        for cp in local:
            cp.wait()
        for gi, g in enumerate(groups):
            for k in range(1, N_DEV):
                peer, pid = _peer(k)
                for t in range(len(g)):
                    i = first[gi] + t
                    _piece_copy(mode, srcs[i], lands[i], sems[2 * gi], sems[2 * gi + 1], t, k, me, pid, peer).start()
        tok_ref[...] = jnp.zeros_like(tok_ref)

    sem_shapes = []
    for g in groups:
        sem_shapes += [pltpu.SemaphoreType.DMA((7 * len(g),))] * 2
    thru = [pltpu.HBM(a.shape, a.dtype) for a in flat] + [pltpu.HBM(s, a.dtype) for s, a in zip(land_shapes, flat)]
    outs = pl.pallas_call(
        body, in_specs=[_HBM] * (2 * n) + [pl.BlockSpec(memory_space=pl.ANY)] * len(extra),
        out_specs=[_SEM] * (2 * ng) + [_HBM] * (2 * n) + [pl.BlockSpec(memory_space=pltpu.VMEM)],
        out_shape=sem_shapes + thru + [_sds((8, 128), F32)],
        input_output_aliases={i: 2 * ng + i for i in range(2 * n)},
        scratch_shapes=[pltpu.SemaphoreType.DMA((n,))],
        compiler_params=pltpu.CompilerParams(has_side_effects=_DATAFLOW), name=name)(
            *[pltpu.with_memory_space_constraint(a, pltpu.HBM) for a in flat],
            *[pltpu.with_memory_space_constraint(lax.empty(s, a.dtype), pltpu.HBM) for s, a in zip(land_shapes, flat)], *extra)
    sems, srcs_thru, lands_thru = outs[:2 * ng], outs[2 * ng:2 * ng + n], outs[2 * ng + n:2 * ng + 2 * n]
    handles = [(sems[2 * gi], sems[2 * gi + 1], srcs_thru[first[gi]:first[gi] + len(g)],
                lands_thru[first[gi]:first[gi] + len(g)]) for gi, g in enumerate(groups)]
    return handles, outs[-1]


def _xwait(handle, mode, after, name):
    send_sems, recv_sems, srcs_thru, lands_thru = handle
    n = len(srcs_thru)

    def body(*refs):
        srcs, lands, send, recv = refs[:n], refs[n:2 * n], refs[2 * n], refs[2 * n + 1]
        me = _my_index()
        for k in range(1, N_DEV):
            peer, pid = _peer(k)
            for t in range(n):
                _piece_copy(mode, srcs[t], lands[t], send, recv, t, k, pid, me, peer).wait_recv()
        for k in range(1, N_DEV):
            peer, pid = _peer(k)
            for t in range(n):
                _piece_copy(mode, srcs[t], lands[t], send, recv, t, k, me, pid, peer).wait_send()

    extra = [] if after is None else [after]
    outs = pl.pallas_call(
        body, in_specs=[_HBM] * (2 * n) + [_SEM, _SEM] + [pl.BlockSpec(memory_space=pl.ANY)] * len(extra),
        out_specs=[_HBM] * (2 * n),
        out_shape=[pltpu.HBM(a.shape, a.dtype) for a in list(srcs_thru) + list(lands_thru)],
        input_output_aliases={i: i for i in range(2 * n)},
        compiler_params=pltpu.CompilerParams(has_side_effects=_DATAFLOW), name=name)(
            *srcs_thru, *lands_thru, send_sems, recv_sems, *extra)
    return outs[n:]


def _adamw(w, g, m, v):
    m = ADAM_B1 * m + (1.0 - ADAM_B1) * g
    v = ADAM_B2 * v + (1.0 - ADAM_B2) * (g * g)
    m_hat = m / (1.0 - ADAM_B1 ** ADAM_STEP)
    v_hat = v / (1.0 - ADAM_B2 ** ADAM_STEP)
    delta = -ADAM_LR * (m_hat / (jnp.sqrt(v_hat) + ADAM_EPS) + ADAM_WD * w)
    return delta, m, v


def _adam_big(w, parts, m, v, name, after=None):
    L, R, C = w.shape
    tr = _tile(R, (256, 128, 176, 64, 32, 16))
    nr = R // tr

    def body(w_ref, *refs):
        p_refs, (m_ref, v_ref, g_ref, d_ref, mo_ref, vo_ref) = refs[:L], refs[L:]
        for l in range(L):
            @pl.when(pl.program_id(0) == l)
            def _(p_ref=p_refs[l]):
                g = p_ref[0].astype(F32)
                for s in range(1, N_DEV):
                    g = g + p_ref[s].astype(F32)
                g_ref[...] = g
                d_ref[...], mo_ref[...], vo_ref[...] = _adamw(w_ref[...], g, m_ref[...], v_ref[...])

    row = pl.BlockSpec((None, tr, C), lambda l, i: (l, i, 0))
    park = lambda l_of: (lambda l, i: (0, jnp.where(l == l_of, i, 0 if l_of else nr - 1), 0))
    return _call(
        after, body, grid=(L, nr),
        in_specs=[row] + [pl.BlockSpec((N_DEV, tr, C), park(l)) for l in range(L)] + [row, row],
        out_specs=[row] * 4, out_shape=[_sds((L, R, C), F32)] * 4,
        compiler_params=_params(("arbitrary", "arbitrary")), name=name)(w, *parts, m, v)


SMALL = (("a_lower_bound", (2, 128), (2, D)), ("ln_gain", (6, 128), (6, D)), ("ln_bias", (6, 128), (6, D)),
         ("a_norm_gain", (1, 128), (1, 128)), ("kv_b", (1, 512), (1, 512)), ("b_b_q", (1, D), (1, D)),
         ("b_sinks", (1, ATT_QH), (1, 128)), ("b_b_out", (1, D), (1, D)), ("ple_b_gate", (2, D), (2, D)))


def _adam_small(parts, w, m, v, after=None):
    k = len(SMALL)

    def body(*refs):
        p_refs, w_refs, m_refs, v_refs = refs[:k], refs[k:2 * k], refs[2 * k:3 * k], refs[3 * k:4 * k]
        outs = refs[4 * k:]
        me = _my_index()
        for i, (_, wshape, pshape) in enumerate(SMALL):
            cols = wshape[1]
            lanes = slice(None) if cols == pshape[1] else (
                pl.ds(0, cols) if cols < 128 else pl.ds(pl.multiple_of(me * cols, cols), cols))
            g = p_refs[i][0, :, lanes]
            for s in range(1, N_DEV):
                g = g + p_refs[i][s, :, lanes]
            g_ref, d_ref, mo_ref, vo_ref = outs[4 * i:4 * i + 4]
            g_ref[...] = g
            d_ref[...], mo_ref[...], vo_ref[...] = _adamw(w_refs[i][...], g, m_refs[i][...], v_refs[i][...])

    full = lambda shape: pl.BlockSpec(shape, lambda: (0,) * len(shape))
    names = [n for n, _, _ in SMALL]
    res = _call(
        after, body,
        in_specs=[full((N_DEV,) + ps) for _, _, ps in SMALL] + [full(ws) for _, ws, _ in SMALL] * 3,
        out_specs=[full(ws) for _, ws, _ in SMALL for _ in range(4)],
        out_shape=[_sds(ws, F32) for _, ws, _ in SMALL for _ in range(4)], name="adam_small")(
            *[parts[n] for n in names], *[a[n].reshape(ws) for a in (w, m, v) for n, ws, _ in SMALL])
    return {n: [r.reshape(w[n].shape) for r in res[4 * i:4 * i + 4]] for i, n in enumerate(names)}


WEIGHTS = ("a_w_in", "a_lower_bound", "a_norm_gain", "a_w_out", "kv_w", "kv_b", "b_w_q", "b_b_q", "b_sinks", "b_w_out",
           "b_b_out", "ffn_w_gate_up", "ffn_w_down", "ple_w_up", "ple_w_gate", "ple_b_gate", "ln_gain", "ln_bias")


GATHER_GROUPS = (("a_w_in",), ("a_w_out", "gu0", "dn0", "pu0", "pg0"), ("kv_w", "b_w_q", "b_w_out"),
                 ("gu1", "dn1", "pu1", "pg1"))
KERNEL_LAYOUT = {
    "a_w_in": lambda a: a,
    "a_w_out": lambda a: a.reshape(D, D),
    "kv_w": lambda a: a.reshape(D, 2 * ATT_KVH * ATT_HD),
    "b_w_q": lambda a: a.reshape(D, D),
    "b_w_out": lambda a: a.reshape(D, D),
    "gu": lambda a: a.reshape(2, 4, FFN_B, D),
    "dn": lambda a: a.reshape(4, FFN_B, D),
    "pu": lambda a: a,
    "pg": lambda a: a.reshape(D, D),
}
_row_blocks = lambda a: a.reshape(N_DEV, -1, a.shape[-1])
OWNER_BLOCKS = {
    "a_w_in": lambda g: g,
    "a_w_out": _row_blocks,
    "kv_w": _row_blocks,
    "b_w_q": _row_blocks,
    "b_w_out": _row_blocks,
    "gu": lambda g: g,
    "dn": lambda g: _row_blocks(g.reshape(FFN_H, D)),
    "pu": lambda g: g.reshape(PLE_DIM, N_DEV, 128).transpose(1, 0, 2),
    "pg": _row_blocks,
}
ADAM_AFTER = {1: (("kv_w", "kv_w"), ("b_w_q", "b_w_q"), ("b_w_out", "b_w_out")),
              2: (("ffn_w_gate_up", "gu"), ("ffn_w_down", "dn"), ("ple_w_up", "pu"), ("ple_w_gate", "pg")),
              3: (("a_w_out", "a_w_out"), ("a_w_in", "a_w_in"))}


def kernel(x, p, a_w_in, a_lower_bound, a_norm_gain, a_w_out, kv_w, kv_b, b_w_q, b_b_q, b_sinks, b_w_out, b_b_out, ffn_w_gate_up, ffn_w_down, ple_w_up, ple_w_gate, ple_b_gate, ln_gain, ln_bias, loss_target, m_a_w_in, m_a_lower_bound, m_a_norm_gain, m_a_w_out, m_kv_w, m_kv_b, m_b_w_q, m_b_b_q, m_b_sinks, m_b_w_out, m_b_b_out, m_ffn_w_gate_up, m_ffn_w_down, m_ple_w_up, m_ple_w_gate, m_ple_b_gate, m_ln_gain, m_ln_bias, v_a_w_in, v_a_lower_bound, v_a_norm_gain, v_a_w_out, v_kv_w, v_kv_b, v_b_w_q, v_b_b_q, v_b_sinks, v_b_w_out, v_b_b_out, v_ffn_w_gate_up, v_ffn_w_down, v_ple_w_up, v_ple_w_gate, v_ple_b_gate, v_ln_gain, v_ln_bias):
    given = dict(locals())
    w = {n: given[n] for n in WEIGHTS}
    m = {n: given["m_" + n] for n in WEIGHTS}
    v = {n: given["v_" + n] for n in WEIGHTS}
    shards = {"a_w_in": a_w_in[0], "a_w_out": a_w_out[0], "kv_w": kv_w, "b_w_q": b_w_q[0], "b_w_out": b_w_out[0]}
    for l in range(2):
        shards.update({f"gu{l}": ffn_w_gate_up[l].T, f"dn{l}": ffn_w_down[l], f"pu{l}": ple_w_up[l], f"pg{l}": ple_w_gate[l]})
    sharded_small = [a_lower_bound, ln_gain.reshape(6, 128), ln_bias.reshape(6, 128)]
    gathered = {}
    for gi, g in enumerate(GATHER_GROUPS):
        lands = _sequencer_gather([shards[n].astype(CDT) for n in g] + (sharded_small if gi == 0 else []),
                                  f"gather{gi}", gi)
        for n, a in zip(g, lands):
            gathered[n] = KERNEL_LAYOUT[n.rstrip("01")](a)
        if gi == 0:
            alb, lng, lnb = [a.transpose(1, 0, 2).reshape(a.shape[1], D) for a in lands[len(g):]]

    def getw(key, after):
        return gathered[key]

    sm = {"a_lower_bound": alb, "ln_gain": lng.reshape(2, 3, D), "ln_bias": lnb.reshape(2, 3, D),
          "a_norm_gain": a_norm_gain, "kv_b": kv_b, "b_b_q": b_b_q[0], "b_sinks": b_sinks, "b_b_out": b_b_out,
          "ple_b_gate": ple_b_gate}

    scatters, small_parts = [], {}

    def emit(grads, small=None):
        names = list(grads)
        blocks = [OWNER_BLOCKS[n.rstrip("01")](grads[n]) for n in names]
        partials = [] if small is None else [small[n].reshape(ps) for n, _, ps in SMALL]
        lands = _sequencer_exchange(blocks + partials, ["scatter"] * len(blocks) + ["gather"] * len(partials),
                                    f"scatter{len(scatters)}", len(GATHER_GROUPS) + len(scatters))
        scatters.append(dict(zip(names, lands)))
        small_parts.update(zip([n for n, _, _ in SMALL], lands[len(blocks):]))
        return blocks

    loss, grad_x, gs = _local_step(x[0], p[:, 0], loss_target[0], getw, sm, emit)

    out, parts, last = {}, {}, [grad_x] + list(scatters[0].values())
    for i, landed in enumerate(scatters):
        parts.update(landed)
        for n, key in ADAM_AFTER.get(i, ()):
            lrc = (1,) * (3 - w[n].ndim) + w[n].shape
            layers = [parts[key]] if key in parts else [parts[key + "0"], parts[key + "1"]]
            shard = (lambda a: a.reshape(lrc).swapaxes(1, 2)) if key == "gu" else (lambda a: a.reshape(lrc))
            res = _adam_big(shard(w[n]), layers, shard(m[n]), shard(v[n]), "adam_" + n, after=last)
            out[n] = [(r.swapaxes(1, 2) if key == "gu" else r).reshape(w[n].shape) for r in res]
            last = [res[3]]
    out.update(_adam_small(small_parts, w, m, v, after=last))

    loss = lax.psum(loss[0, 0], ("x", "y", "c"))
    res = [loss, grad_x[None]]
    for i in range(4):
        res += [out[n][i] for n in WEIGHTS]
    return tuple(res)
```

```python
import jax
import jax.numpy as jnp
from jax import lax
from jax.experimental import pallas as pl
from jax.experimental.pallas import tpu as pltpu
from jax.experimental.pallas import tpu_sc as plsc

F32 = jnp.float32
CDT = jnp.bfloat16

N_DEV = 8
D = 1024
HG_H, HG_DK, HG_CH = 8, 128, 64
HG_HPB = 4
HG_CPB = 8
ATT_HD, ATT_QH, ATT_KVH, ATT_G, WINDOW = 64, 16, 4, 4, 128
FFN_H = 2816
FFN_B = FFN_H // 4
PLE_DIM = 256
ALPHA = (2.0 * 2) ** 0.25
LN_EPS = 1e-5
RMS_EPS = 1e-6
ADAM_LR, ADAM_B1, ADAM_B2, ADAM_EPS, ADAM_WD, ADAM_STEP = 0.001, 0.9, 0.999, 1e-08, 0.01, 10
ROW_TILES = (512, 256, 128, 64)
VMEM_LIMIT = 48 * 1024 * 1024
NEG = -1e30

MESH = pl.DeviceIdType.MESH


def _tile(n, cands=ROW_TILES):
    for t in cands:
        if n % t == 0:
            return t
    return n


def _sds(shape, dtype):
    return jax.ShapeDtypeStruct(tuple(shape), dtype)


def _params(sem):
    return pltpu.CompilerParams(dimension_semantics=sem, vmem_limit_bytes=VMEM_LIMIT)


def _dot(a, b):
    return jnp.dot(a.astype(CDT), b.astype(CDT), preferred_element_type=F32)


def _dot_nt(a, b):
    return lax.dot_general(a.astype(CDT), b.astype(CDT), (((1,), (1,)), ((), ())), preferred_element_type=F32)


def _dot_tn(a, b):
    return lax.dot_general(a.astype(CDT), b.astype(CDT), (((0,), (0,)), ((), ())), preferred_element_type=F32)


def _sigmoid(x):
    return jax.nn.sigmoid(x)


def _ln_fwd(z, g, b):
    mu = jnp.mean(z, axis=-1, keepdims=True)
    zc = z - mu
    var = jnp.mean(zc * zc, axis=-1, keepdims=True)
    return zc * lax.rsqrt(var + LN_EPS) * g + b


def _ln_bwd(z, g, dy):
    mu = jnp.mean(z, axis=-1, keepdims=True)
    zc = z - mu
    var = jnp.mean(zc * zc, axis=-1, keepdims=True)
    rstd = lax.rsqrt(var + LN_EPS)
    xhat = zc * rstd
    dxh = dy * g
    dz = rstd * (dxh - jnp.mean(dxh, axis=-1, keepdims=True) - xhat * jnp.mean(dxh * xhat, axis=-1, keepdims=True))
    return dz, xhat


def _colsum(x):
    return jnp.sum(x, axis=0, keepdims=True)


def _acc(ref, val, first):
    @pl.when(first)
    def _():
        ref[...] = val

    @pl.when(jnp.logical_not(first))
    def _():
        ref[...] += val


def _zero_at(ref, first):
    @pl.when(first)
    def _():
        ref[...] = jnp.zeros_like(ref)


def _call(after, body, **kw):
    after = [] if after is None else list(after)
    specs = list(kw["in_specs"])
    kw["in_specs"] = [pl.BlockSpec(memory_space=pl.ANY)] * len(after) + specs

    def ordered_body(*refs):
        body(*refs[len(after):])

    call = pl.pallas_call(ordered_body, **kw)

    def pinned(*args):
        args = [a if s.memory_space is not None else pltpu.with_memory_space_constraint(a, pltpu.HBM)
                for a, s in zip(args, specs)]
        return call(*after, *args)

    return pinned


def _mm_nn(a, b3, out_shape, oblock, omap, out_dtype, bias3=None, name="mm_nn"):
    M, K = a.shape
    G, _, Nb = b3.shape
    tm = _tile(M)

    def body(a_ref, b_ref, *rest):
        o_ref = rest[-1]
        acc = _dot(a_ref[...], b_ref[...])
        if bias3 is not None:
            acc = acc + rest[0][...]
        o_ref[...] = acc.astype(o_ref.dtype)

    in_specs = [pl.BlockSpec((tm, K), lambda g, i: (i, 0)), pl.BlockSpec((None, K, Nb), lambda g, i: (g, 0, 0))]
    args = [a, b3]
    if bias3 is not None:
        in_specs.append(pl.BlockSpec((None, 1, Nb), lambda g, i: (g, 0, 0)))
        args.append(bias3)
    return _call(
        None, body, grid=(G, M // tm), in_specs=in_specs, out_specs=pl.BlockSpec(oblock, omap),
        out_shape=_sds(out_shape, out_dtype), compiler_params=_params(("arbitrary", "arbitrary")), name=name)(*args)


def _mm_tn(a3, b3, G, amap, bmap, ablock, bblock, out_shape, oblock, omap, name="mm_tn"):
    S = a3.shape[1]

    def body(a_ref, b_ref, o_ref):
        o_ref[...] = _dot_tn(a_ref[...], b_ref[...]).astype(o_ref.dtype)

    return _call(
        None, body, grid=(G, 1),
        in_specs=[pl.BlockSpec(ablock(S), amap), pl.BlockSpec(bblock(S), bmap)],
        out_specs=pl.BlockSpec(oblock, omap), out_shape=_sds(out_shape, CDT),
        compiler_params=_params(("arbitrary", "arbitrary")), name=name)(a3, b3)


def _wgrad(a3, b3, name):
    Ga, S, M = a3.shape
    Gb, _, N = b3.shape
    G = max(Ga, Gb)
    return _mm_tn(
        a3, b3, G,
        (lambda g, k: (g, k, 0)) if Ga > 1 else (lambda g, k: (0, k, 0)),
        (lambda g, k: (g, k, 0)) if Gb > 1 else (lambda g, k: (0, k, 0)),
        lambda tk: (None, tk, M), lambda tk: (None, tk, N),
        (G, M, N), (None, M, N), lambda g, k: (g, 0, 0), name=name)


def _mixout_ln(u3, w3, bias, xin, gain, beta, name):
    G, S, Kb = u3.shape
    tm = _tile(S)

    def body(u_ref, w_ref, b_ref, x_ref, g_ref, be_ref, z_ref, xo_ref, xob_ref):
        h = b_ref[...] + _dot(u_ref[0], w_ref[0])
        for g in range(1, G):
            h = h + _dot(u_ref[g], w_ref[g])
        z = ALPHA * x_ref[...] + h
        z_ref[...] = z
        y = _ln_fwd(z, g_ref[...], be_ref[...])
        xo_ref[...] = y
        xob_ref[...] = y.astype(CDT)

    row = pl.BlockSpec((tm, D), lambda i: (i, 0))
    vec = pl.BlockSpec((1, D), lambda i: (0, 0))
    return _call(
        None, body, grid=(S // tm,),
        in_specs=[pl.BlockSpec((G, tm, Kb), lambda i: (0, i, 0)), pl.BlockSpec((G, Kb, D), lambda i: (0, 0, 0)),
                  vec, row, vec, vec],
        out_specs=[row, row, row], out_shape=[_sds((S, D), F32), _sds((S, D), F32), _sds((S, D), CDT)],
        compiler_params=_params(("arbitrary",)), name=name)(u3, w3, bias, xin, gain, beta)


def _ffn_fwd(xin, xin_b, wgu, wdn, gain, beta, name):
    S = xin.shape[0]
    tm = _tile(S)

    def hidden(xb_ref, wgu_ref, gu_ref, hid_ref):
        xb = xb_ref[...]
        gate = _dot_nt(xb, wgu_ref[0])
        up = _dot_nt(xb, wgu_ref[1])
        gu_ref[0] = gate
        gu_ref[1] = up
        hid_ref[...] = (gate * _sigmoid(gate) * up).astype(CDT)

    gu, hid = _call(
        None, hidden, grid=(S // tm, 4),
        in_specs=[pl.BlockSpec((tm, D), lambda i, j: (i, 0)), pl.BlockSpec((2, None, FFN_B, D), lambda i, j: (0, j, 0, 0))],
        out_specs=[pl.BlockSpec((2, None, tm, FFN_B), lambda i, j: (0, j, i, 0)),
                   pl.BlockSpec((None, tm, FFN_B), lambda i, j: (j, i, 0))],
        out_shape=[_sds((2, 4, S, FFN_B), F32), _sds((4, S, FFN_B), CDT)],
        compiler_params=_params(("arbitrary", "arbitrary")), name=name + "_hidden")(xin_b, wgu)

    def down(x_ref, hid_ref, wdn_ref, g_ref, be_ref, z_ref, xo_ref, xob_ref):
        z = ALPHA * x_ref[...]
        for j in range(4):
            z = z + _dot(hid_ref[j], wdn_ref[j])
        z_ref[...] = z
        y = _ln_fwd(z, g_ref[...], be_ref[...])
        xo_ref[...] = y
        xob_ref[...] = y.astype(CDT)

    row = pl.BlockSpec((tm, D), lambda i: (i, 0))
    vec = pl.BlockSpec((1, D), lambda i: (0, 0))
    z, xo, xob = _call(
        None, down, grid=(S // tm,),
        in_specs=[row, pl.BlockSpec((4, tm, FFN_B), lambda i: (0, i, 0)), pl.BlockSpec((4, FFN_B, D), lambda i: (0, 0, 0)),
                  vec, vec],
        out_specs=[row, row, row], out_shape=[_sds((S, D), F32), _sds((S, D), F32), _sds((S, D), CDT)],
        compiler_params=_params(("arbitrary",)), name=name + "_down")(xin, hid, wdn, gain, beta)
    return gu, hid, z, xo, xob


def _ple_fwd(xin, xin_b, p_b, wpg, bgate, wpu, gain, beta, name):
    S = xin.shape[0]
    tm = _tile(S)

    def body(x_ref, xb_ref, p_ref, wpg_ref, bg_ref, wpu_ref, g_ref, be_ref, sg_ref, up_ref, z_ref, xo_ref, xob_ref):
        sg = _sigmoid(_dot(xb_ref[...], wpg_ref[...]) + bg_ref[...])
        pb = p_ref[...]
        up = jnp.concatenate([_dot(pb, wpu_ref[j]) for j in range(N_DEV)], axis=-1)
        sg_ref[...] = sg
        up_ref[...] = up
        z = ALPHA * x_ref[...] + sg * up
        z_ref[...] = z
        y = _ln_fwd(z, g_ref[...], be_ref[...])
        xo_ref[...] = y
        xob_ref[...] = y.astype(CDT)

    row = pl.BlockSpec((tm, D), lambda i: (i, 0))
    vec = pl.BlockSpec((1, D), lambda i: (0, 0))
    return _call(
        None, body, grid=(S // tm,),
        in_specs=[row, row, pl.BlockSpec((tm, PLE_DIM), lambda i: (i, 0)), pl.BlockSpec((D, D), lambda i: (0, 0)), vec,
                  pl.BlockSpec((N_DEV, PLE_DIM, D // N_DEV), lambda i: (0, 0, 0)), vec, vec],
        out_specs=[row] * 5,
        out_shape=[_sds((S, D), F32)] * 4 + [_sds((S, D), CDT)],
        compiler_params=_params(("arbitrary",)), name=name)(xin, xin_b, p_b, wpg, bgate, wpu, gain, beta)


def _loss_fwd_bwd(y, target):
    S = y.shape[0]
    tm = _tile(S)

    def body(y_ref, t_ref, l_ref, dy_ref):
        e = y_ref[...] - t_ref[...]
        dy_ref[...] = e * (1.0 / D)
        part = 0.5 * jnp.sum(jnp.sum(e * e, axis=-1, keepdims=True) * (1.0 / D), axis=0, keepdims=True)
        _acc(l_ref, part, pl.program_id(0) == 0)

    row = pl.BlockSpec((tm, D), lambda i: (i, 0))
    return _call(
        None, body, grid=(S // tm,), in_specs=[row, row],
        out_specs=[pl.BlockSpec((1, 1), lambda i: (0, 0)), row],
        out_shape=[_sds((1, 1), F32), _sds((S, D), F32)],
        compiler_params=_params(("arbitrary",)), name="loss")(y, target)


def _ple_bwd(dy, z, sg, up, gain, wpg, name, after=None):
    S = dy.shape[0]
    tm = _tile(S)

    def body(dy_ref, z_ref, sg_ref, up_ref, g_ref, wpg_ref, dx_ref, dgl_ref, dup_ref, dgain_ref, dbeta_ref, dbg_ref):
        first = pl.program_id(0) == 0
        dy_ = dy_ref[...]
        dz, xhat = _ln_bwd(z_ref[...], g_ref[...], dy_)
        sg_ = sg_ref[...]
        dgl = dz * up_ref[...] * sg_ * (1.0 - sg_)
        dgl_ref[...] = dgl.astype(CDT)
        dup_ref[...] = (dz * sg_).astype(CDT)
        dx_ref[...] = ALPHA * dz + _dot_nt(dgl, wpg_ref[...])
        _acc(dgain_ref, _colsum(dy_ * xhat), first)
        _acc(dbeta_ref, _colsum(dy_), first)
        _acc(dbg_ref, _colsum(dgl), first)

    row = pl.BlockSpec((tm, D), lambda i: (i, 0))
    vec = pl.BlockSpec((1, D), lambda i: (0, 0))
    return _call(
        after, body, grid=(S // tm,), in_specs=[row, row, row, row, vec, pl.BlockSpec((D, D), lambda i: (0, 0))],
        out_specs=[row, row, row, vec, vec, vec],
        out_shape=[_sds((S, D), F32), _sds((S, D), CDT), _sds((S, D), CDT)] + [_sds((1, D), F32)] * 3,
        compiler_params=_params(("arbitrary",)), name=name)(dy, z, sg, up, gain, wpg)


def _ffn_bwd(dy, z, gu, wgu, wdn, gain, name, after=None):
    S = dy.shape[0]
    tm = _tile(S)

    def hidden(dy_ref, z_ref, gu_ref, wdn_ref, g_ref, dz_ref, dzb_ref, dgu_ref, dgain_ref, dbeta_ref):
        i, j = pl.program_id(0), pl.program_id(1)

        @pl.when(j == 0)
        def _():
            dy_ = dy_ref[...]
            dz, xhat = _ln_bwd(z_ref[...], g_ref[...], dy_)
            dz_ref[...] = dz
            dzb_ref[...] = dz.astype(CDT)
            _acc(dgain_ref, _colsum(dy_ * xhat), i == 0)
            _acc(dbeta_ref, _colsum(dy_), i == 0)

        dhid = _dot_nt(dzb_ref[...], wdn_ref[...])
        gate, up = gu_ref[0], gu_ref[1]
        sg = _sigmoid(gate)
        dgu_ref[0] = (dhid * up * (sg * (1.0 + gate * (1.0 - sg)))).astype(CDT)
        dgu_ref[1] = (dhid * (gate * sg)).astype(CDT)

    row = pl.BlockSpec((tm, D), lambda i, j: (i, 0))
    vec = pl.BlockSpec((1, D), lambda i, j: (0, 0))
    dz, dzb, dgu, dgain, dbeta = _call(
        after, hidden, grid=(S // tm, 4),
        in_specs=[row, row, pl.BlockSpec((2, None, tm, FFN_B), lambda i, j: (0, j, i, 0)),
                  pl.BlockSpec((None, FFN_B, D), lambda i, j: (j, 0, 0)), vec],
        out_specs=[row, row, pl.BlockSpec((2, None, tm, FFN_B), lambda i, j: (0, j, i, 0)), vec, vec],
        out_shape=[_sds((S, D), F32), _sds((S, D), CDT), _sds((2, 4, S, FFN_B), CDT), _sds((1, D), F32),
                   _sds((1, D), F32)],
        compiler_params=_params(("arbitrary", "arbitrary")), name=name + "_hidden")(dy, z, gu, wdn, gain)

    def to_input(dz_ref, dgu_ref, wgu_ref, dx_ref):
        acc = ALPHA * dz_ref[...]
        for g in range(2):
            for j in range(4):
                acc = acc + _dot(dgu_ref[g, j], wgu_ref[g, j])
        dx_ref[...] = acc

    rows = pl.BlockSpec((tm, D), lambda i: (i, 0))
    dx = _call(
        None, to_input, grid=(S // tm,),
        in_specs=[rows, pl.BlockSpec((2, 4, tm, FFN_B), lambda i: (0, 0, i, 0)),
                  pl.BlockSpec((2, 4, FFN_B, D), lambda i: (0, 0, 0, 0))],
        out_specs=rows, out_shape=_sds((S, D), F32),
        compiler_params=_params(("arbitrary",)), name=name + "_input")(dz, dgu, wgu)
    return dx, dzb, dgu, dgain, dbeta


def _mixout_bwd(dy, z, gain, w3, du_dtype, name, after=None):
    S = dy.shape[0]
    G, Kb, _ = w3.shape
    tm = _tile(S)

    def body(dy_ref, z_ref, g_ref, w_ref, dz_ref, dzb_ref, du_ref, dgain_ref, dbeta_ref, dbias_ref):
        first = pl.program_id(0) == 0
        dy_ = dy_ref[...]
        dz, xhat = _ln_bwd(z_ref[...], g_ref[...], dy_)
        dz_ref[...] = dz
        dzb = dz.astype(CDT)
        dzb_ref[...] = dzb
        for g in range(G):
            du_ref[g] = _dot_nt(dzb, w_ref[g]).astype(du_ref.dtype)
        _acc(dgain_ref, _colsum(dy_ * xhat), first)
        _acc(dbeta_ref, _colsum(dy_), first)
        _acc(dbias_ref, _colsum(dz), first)

    row = pl.BlockSpec((tm, D), lambda i: (i, 0))
    vec = pl.BlockSpec((1, D), lambda i: (0, 0))
    return _call(
        after, body, grid=(S // tm,), in_specs=[row, row, vec, pl.BlockSpec((G, Kb, D), lambda i: (0, 0, 0))],
        out_specs=[row, row, pl.BlockSpec((G, tm, Kb), lambda i: (0, i, 0)), vec, vec, vec],
        out_shape=[_sds((S, D), F32), _sds((S, D), CDT), _sds((G, S, Kb), du_dtype)] + [_sds((1, D), F32)] * 3,
        compiler_params=_params(("arbitrary",)), name=name)(dy, z, gain, w3)


def _half_select(low):
    r = lax.broadcasted_iota(jnp.int32, (2 * ATT_HD, ATT_HD), 0)
    c = lax.broadcasted_iota(jnp.int32, (2 * ATT_HD, ATT_HD), 1)
    return (r == c + (0 if low else ATT_HD)).astype(CDT)


def _half_place(low):
    r = lax.broadcasted_iota(jnp.int32, (ATT_HD, 2 * ATT_HD), 0)
    c = lax.broadcasted_iota(jnp.int32, (ATT_HD, 2 * ATT_HD), 1)
    return (c == r + (0 if low else ATT_HD)).astype(CDT)


def _pair_lanes(even, odd):
    return (jnp.dot(even, _half_place(True), preferred_element_type=F32)
            + jnp.dot(odd, _half_place(False), preferred_element_type=F32)).astype(CDT)


def _proj_heads(a, w, bias, heads, name):
    S, K = a.shape
    N = heads * ATT_HD
    tm = _tile(S)

    def body(a_ref, w_ref, b_ref, o_ref):
        acc = (_dot(a_ref[...], w_ref[...]) + b_ref[...]).astype(CDT)
        sel = (_half_select(True), _half_select(False))
        for h in range(heads):
            pair = acc[:, (h // 2) * 2 * ATT_HD:(h // 2 + 1) * 2 * ATT_HD]
            o_ref[h] = jnp.dot(pair, sel[h % 2], preferred_element_type=F32).astype(CDT)

    return _call(
        None, body, grid=(S // tm,),
        in_specs=[pl.BlockSpec((tm, K), lambda i: (i, 0)), pl.BlockSpec((K, N), lambda i: (0, 0)),
                  pl.BlockSpec((1, N), lambda i: (0, 0))],
        out_specs=pl.BlockSpec((heads, tm, ATT_HD), lambda i: (0, i, 0)), out_shape=_sds((heads, S, ATT_HD), CDT),
        compiler_params=_params(("arbitrary",)), name=name)(a, w, bias)


def _qkv_bwd(dz, dq, dkv4, wq, wkv, name, after=None):
    S = dz.shape[0]
    tm = _tile(S)
    HK = dkv4.shape[0]
    NK = HK * ATT_HD

    def body(dz_ref, dq_ref, dkv_ref, wq_ref, wkv_ref, dx_ref, dkvn_ref, dkvb_ref):
        first = pl.program_id(0) == 0
        dkvn = jnp.concatenate([_pair_lanes(dkv_ref[2 * i].astype(CDT), dkv_ref[2 * i + 1].astype(CDT))
                                for i in range(HK // 2)], axis=-1)
        dkvn_ref[...] = dkvn
        dx_ref[...] = ALPHA * dz_ref[...] + _dot_nt(dq_ref[...], wq_ref[...]) + _dot_nt(dkvn, wkv_ref[...])
        for h in range(HK):
            _acc(dkvb_ref.at[h], _colsum(dkv_ref[h]), first)

    row = pl.BlockSpec((tm, D), lambda i: (i, 0))
    return _call(
        after, body, grid=(S // tm,),
        in_specs=[row, row, pl.BlockSpec((HK, tm, ATT_HD), lambda i: (0, i, 0)),
                  pl.BlockSpec((D, D), lambda i: (0, 0)), pl.BlockSpec((D, NK), lambda i: (0, 0))],
        out_specs=[row, pl.BlockSpec((tm, NK), lambda i: (i, 0)), pl.BlockSpec((HK, 1, ATT_HD), lambda i: (0, 0, 0))],
        out_shape=[_sds((S, D), F32), _sds((S, NK), CDT), _sds((HK, 1, ATT_HD), F32)],
        compiler_params=_params(("arbitrary",)), name=name)(dz, dq, dkv4, wq, wkv)


def _inproj_bwd(dz, dproj, wain, name, after=None):
    S = dz.shape[0]
    tm = _tile(S)
    nb = wain.shape[-1]

    def body(dz_ref, dp_ref, w_ref, dx_ref):
        acc = ALPHA * dz_ref[...]
        for j in range(N_DEV):
            acc = acc + _dot_nt(dp_ref[j // 2, :, pl.ds((j % 2) * nb, nb)], w_ref[j])
        dx_ref[...] = acc

    row = pl.BlockSpec((tm, D), lambda i: (i, 0))
    return _call(
        after, body, grid=(S // tm,),
        in_specs=[row, pl.BlockSpec((4, tm, D), lambda i: (0, i, 0)), pl.BlockSpec((N_DEV, D, nb), lambda i: (0, 0, 0))],
        out_specs=row, out_shape=_sds((S, D), F32),
        compiler_params=_params(("arbitrary",)), name=name)(dz, dproj, wain)


def _running_sum(x, reverse=False):
    rows = x.shape[0]
    row = lax.broadcasted_iota(jnp.int32, x.shape, 0)
    step = 1
    while step < rows:
        if reverse:
            x = x + jnp.where(row < rows - step, pltpu.roll(x, rows - step, 0), 0.0)
        else:
            x = x + jnp.where(row >= step, pltpu.roll(x, step, 0), 0.0)
        step *= 2
    return x


def _hg_gates(q, f, alb_ref):
    a0, a1 = alb_ref[0:1, :], alb_ref[1:2, :]
    mx = jnp.maximum(a0, a1)
    e0, e1 = jnp.exp(a0 - mx), jnp.exp(a1 - mx)
    lb = e0 / (e0 + e1)
    sig = _sigmoid(f)
    forget = lb + (1.0 - lb) * sig
    k = (1.0 - lb) * _sigmoid(-f)
    qs = q * _sigmoid(q) * (HG_DK ** -0.5)
    return qs, k, jnp.log(forget), sig, lb, forget


def _hg_intra(qs, k, b, b_scr):
    b_scr[...] = b
    bm = b_scr[pl.ds(HG_CH // 2 - 1, 1), :]
    bl = b_scr[pl.ds(HG_CH - 1, 1), :]
    eb = jnp.exp(b)
    qb = qs * eb
    e_q = jnp.exp(b - bm)
    e_k = jnp.exp(bm - b)
    e_d = jnp.exp(bl - b)
    return qb, qs * e_q, k * e_k, k * e_d, jnp.exp(bl), eb, e_q, e_k, e_d


def _hgrn_fwd(proj, alb, ngain):
    S = proj.shape[1]
    nc = S // HG_CH
    nb = nc // HG_CPB
    rb, wb = HG_CPB * HG_CH, HG_HPB * HG_DK

    def body(pj_ref, alb_ref, ng_ref, o_ref, y_ref, st_ref, st_scr, b_scr):
        n = pl.program_id(1)

        @pl.when(n == 0)
        def _():
            st_scr[...] = jnp.zeros_like(st_scr)

        r = lax.broadcasted_iota(jnp.int32, (HG_CH, HG_CH), 0)
        c = lax.broadcasted_iota(jnp.int32, (HG_CH, HG_CH), 1)
        causal = r >= c
        for ci, j in [(ci, j) for ci in range(HG_CPB) for j in range(HG_HPB)]:
            rows, lanes = pl.ds(ci * HG_CH, HG_CH), pl.ds(j * HG_DK, HG_DK)
            q, f, v, g = pj_ref[0, rows, lanes], pj_ref[1, rows, lanes], pj_ref[2, rows, lanes], pj_ref[3, rows, lanes]
            qs, k, logf, _, _, _ = _hg_gates(q, f, alb_ref.at[:, lanes])
            b = _running_sum(logf)
            qb, qt, kt, kd, ebl, _, _, _, _ = _hg_intra(qs, k, b, b_scr.at[j, ci])
            st = st_scr[j]
            st_ref[j, ci] = st
            a = jnp.where(causal, _dot_nt(qt, kt), 0.0)
            o = _dot(a, v) + _dot_nt(qb, st)
            st_scr[j] = st * ebl + _dot_tn(v, kd)
            o_ref[rows, lanes] = o
            rinv = lax.rsqrt(jnp.mean(o * o, axis=-1, keepdims=True) + RMS_EPS)
            y_ref[rows, lanes] = (o * rinv * ng_ref[...] * (g * _sigmoid(g))).astype(CDT)

    blk = pl.BlockSpec((rb, wb), lambda h, n: (n, h))
    return _call(
        None, body, grid=(HG_H // HG_HPB, nb),
        in_specs=[pl.BlockSpec((4, rb, wb), lambda h, n: (0, n, h)), pl.BlockSpec((2, wb), lambda h, n: (0, h)),
                  pl.BlockSpec((1, HG_DK), lambda h, n: (0, 0))],
        out_specs=[blk, blk, pl.BlockSpec((HG_HPB, HG_CPB, HG_DK, HG_DK), lambda h, n: (h, n, 0, 0))],
        out_shape=[_sds((S, D), F32), _sds((S, D), CDT), _sds((HG_H, nc, HG_DK, HG_DK), F32)],
        scratch_shapes=[pltpu.VMEM((HG_HPB, HG_DK, HG_DK), F32), pltpu.VMEM((HG_HPB, HG_CPB, HG_CH, HG_DK), F32)],
        compiler_params=_params(("arbitrary", "arbitrary")), name="hgrn_fwd")(proj, alb, ngain)


def _hgrn_bwd(proj, alb, ngain, o, states, dy, after=None):
    S = proj.shape[1]
    nc = S // HG_CH
    nb = nc // HG_CPB
    rb, wb = HG_CPB * HG_CH, HG_HPB * HG_DK

    def body(pj_ref, alb_ref, ng_ref, o_ref, st_ref, dy_ref, dpj_ref, dalb_ref, dng_ref, dst_scr, b_scr):
        h, n = pl.program_id(0), pl.program_id(1)

        @pl.when(n == 0)
        def _():
            dst_scr[...] = jnp.zeros_like(dst_scr)

        ng = ng_ref[...]
        r = lax.broadcasted_iota(jnp.int32, (HG_CH, HG_CH), 0)
        c = lax.broadcasted_iota(jnp.int32, (HG_CH, HG_CH), 1)
        causal = r >= c
        dng = None
        for ci, j in [(ci, j) for ci in reversed(range(HG_CPB)) for j in range(HG_HPB)]:
            rows, lanes = pl.ds(ci * HG_CH, HG_CH), pl.ds(j * HG_DK, HG_DK)
            q, f, v, g = pj_ref[0, rows, lanes], pj_ref[1, rows, lanes], pj_ref[2, rows, lanes], pj_ref[3, rows, lanes]
            o_ = o_ref[rows, lanes]
            dy_ = dy_ref[rows, lanes]
            sg = _sigmoid(g)
            rinv = lax.rsqrt(jnp.mean(o_ * o_, axis=-1, keepdims=True) + RMS_EPS)
            nrm = o_ * rinv
            dr = dy_ * (g * sg)
            dg = dy_ * nrm * ng * (sg * (1.0 + g * (1.0 - sg)))
            dn = dr * ng
            do = rinv * (dn - nrm * jnp.mean(dn * nrm, axis=-1, keepdims=True))
            dng = _colsum(dr * nrm) if dng is None else dng + _colsum(dr * nrm)
            qs, k, logf, sig, lb, forget = _hg_gates(q, f, alb_ref.at[:, lanes])
            b = _running_sum(logf)
            qb, qt, kt, kd, ebl, eb, e_q, e_k, e_d = _hg_intra(qs, k, b, b_scr.at[j, ci])
            st = st_ref[j, ci]
            dstn = dst_scr[j]
            qt, kt, qb, kd = (t.astype(CDT).astype(F32) for t in (qt, kt, qb, kd))
            a = jnp.where(causal, _dot_nt(qt, kt), 0.0)
            da = jnp.where(causal, _dot_nt(do, v), 0.0)
            dv = _dot_tn(a, do) + _dot_nt(kd, dstn)
            dqb = _dot(do, st)
            dkd = _dot(v, dstn)
            dqt = _dot(da, kt)
            dkt = _dot_tn(da, qt)
            dbl = _colsum(dkd * kd) + ebl * _colsum(dstn * st)
            dst_scr[j] = dstn * ebl + _dot_tn(do, qb)
            dqs = dqt * e_q + dqb * eb
            dk = dkt * e_k + dkd * e_d
            db = dqt * qt + dqb * qb - dkt * kt - dkd * kd
            dlogf = _running_sum(db, reverse=True) + dbl
            dforget = dlogf / forget
            dsig = (1.0 - lb) * (dforget - dk)
            df = dsig * sig * (1.0 - sig)
            dlb = _colsum((dforget - dk) * (1.0 - sig))
            sq = _sigmoid(q)
            dq = dqs * (HG_DK ** -0.5) * (sq * (1.0 + q * (1.0 - sq)))
            dpj_ref[0, rows, lanes] = dq.astype(CDT)
            dpj_ref[1, rows, lanes] = df.astype(CDT)
            dpj_ref[2, rows, lanes] = dv.astype(CDT)
            dpj_ref[3, rows, lanes] = dg.astype(CDT)
            da0 = dlb * lb * (1.0 - lb)
            first = jnp.logical_and(n == 0, ci == HG_CPB - 1)
            _acc(dalb_ref.at[pl.ds(0, 1), lanes], da0, first)
            _acc(dalb_ref.at[pl.ds(1, 1), lanes], -da0, first)
        _acc(dng_ref, dng, jnp.logical_and(h == 0, n == 0))

    blk = pl.BlockSpec((rb, wb), lambda h, n: (nb - 1 - n, h))
    pj = pl.BlockSpec((4, rb, wb), lambda h, n: (0, nb - 1 - n, h))
    alb_blk = pl.BlockSpec((2, wb), lambda h, n: (0, h))
    ng_blk = pl.BlockSpec((1, HG_DK), lambda h, n: (0, 0))
    return _call(
        after, body, grid=(HG_H // HG_HPB, nb),
        in_specs=[pj, alb_blk, ng_blk, blk,
                  pl.BlockSpec((HG_HPB, HG_CPB, HG_DK, HG_DK), lambda h, n: (h, nb - 1 - n, 0, 0)), blk],
        out_specs=[pj, alb_blk, ng_blk],
        out_shape=[_sds((4, S, D), CDT), _sds((2, D), F32), _sds((1, HG_DK), F32)],
        scratch_shapes=[pltpu.VMEM((HG_HPB, HG_DK, HG_DK), F32), pltpu.VMEM((HG_HPB, HG_CPB, HG_CH, HG_DK), F32)],
        compiler_params=_params(("arbitrary", "arbitrary")), name="hgrn_bwd")(proj, alb, ngain, o, states, dy)


def _slope(h):
    return 2.0 ** (-8.0 * (h + 1) / ATT_QH)


def _attn_mask(n):
    qi = lax.broadcasted_iota(jnp.int32, (WINDOW, 2 * WINDOW), 0)
    si = lax.broadcasted_iota(jnp.int32, (WINDOW, 2 * WINDOW), 1)
    dist = qi - si + WINDOW
    valid = (dist >= 0) & (dist < WINDOW) & (n * WINDOW - WINDOW + si >= 0)
    return valid, dist.astype(F32)


def _attn_probs(qh, kh, sink, slope, valid, distf):
    s = _dot_nt(qh, kh) * (ATT_HD ** -0.5) - slope * distf
    s = jnp.where(valid, s, NEG)
    m = jnp.maximum(jnp.max(s, axis=-1, keepdims=True), sink)
    e = jnp.exp(s - m)
    es = jnp.exp(sink - m)
    inv = 1.0 / (jnp.sum(e, axis=-1, keepdims=True) + es)
    return e * inv, es * inv


def _attn_specs(S):
    nb = S // WINDOW
    cur = lambda H: pl.BlockSpec((H, WINDOW, ATT_HD), lambda n: (0, n, 0))
    prev = lambda H: pl.BlockSpec((H, WINDOW, ATT_HD), lambda n: (0, jnp.maximum(n - 1, 0), 0))
    return nb, cur, prev


def _attn_fwd(q4, kv4, sinks):
    S = q4.shape[1]
    nb, cur, prev = _attn_specs(S)

    def body(sink_ref, q_ref, kvc_ref, kvp_ref, o_ref):
        valid, distf = _attn_mask(pl.program_id(0))
        outs = []
        for h in range(ATT_QH):
            kvh = h // ATT_G
            kh = jnp.concatenate([kvp_ref[kvh], kvc_ref[kvh]], axis=0)
            vh = jnp.concatenate([kvp_ref[ATT_KVH + kvh], kvc_ref[ATT_KVH + kvh]], axis=0)
            p, _ = _attn_probs(q_ref[h], kh, sink_ref[0, h], _slope(h), valid, distf)
            outs.append(_dot(p, vh).astype(CDT))
            if h % 2:
                o_ref[:, pl.ds((h - 1) * ATT_HD, 2 * ATT_HD)] = _pair_lanes(outs[h - 1], outs[h])

    return _call(
        None, body, grid=(nb,),
        in_specs=[pl.BlockSpec(memory_space=pltpu.SMEM), cur(ATT_QH), cur(2 * ATT_KVH), prev(2 * ATT_KVH)],
        out_specs=pl.BlockSpec((WINDOW, D), lambda n: (n, 0)), out_shape=_sds((S, D), CDT),
        compiler_params=_params(("arbitrary",)), name="attn_fwd")(sinks, q4, kv4, kv4)


def _attn_bwd(q4, kv4, sinks, do):
    S = q4.shape[1]
    nb, cur, prev = _attn_specs(S)

    def body(sink_ref, q_ref, kvc_ref, kvp_ref, do_ref, dq_ref, dkv_ref, dbq_ref, dsink_ref):
        n = pl.program_id(0)
        first = n == 0

        @pl.when(first)
        def _():
            dkv_ref[...] = jnp.zeros_like(dkv_ref)
            dsink_ref[...] = jnp.zeros_like(dsink_ref)

        valid, distf = _attn_mask(n)
        lane = lax.broadcasted_iota(jnp.int32, (1, 128), 1)
        rows_cur = pl.ds(pl.multiple_of(n * WINDOW, WINDOW), WINDOW)
        rows_prev = pl.ds(pl.multiple_of(jnp.maximum(n - 1, 0) * WINDOW, WINDOW), WINDOW)
        dsinks = jnp.zeros((1, 128), F32)
        sel = (_half_select(True), _half_select(False))
        for kvh in range(ATT_KVH):
            kh = jnp.concatenate([kvp_ref[kvh], kvc_ref[kvh]], axis=0)
            vh = jnp.concatenate([kvp_ref[ATT_KVH + kvh], kvc_ref[ATT_KVH + kvh]], axis=0)
            dk = dv = None
            dqs = []
            for h in range(kvh * ATT_G, (kvh + 1) * ATT_G):
                qh = q_ref[h]
                doh = jnp.dot(do_ref[:, pl.ds((h // 2) * 2 * ATT_HD, 2 * ATT_HD)], sel[h % 2],
                              preferred_element_type=F32).astype(CDT)
                p, ps = _attn_probs(qh, kh, sink_ref[0, h], _slope(h), valid, distf)
                dp = _dot_nt(doh, vh)
                dd = jnp.sum(p * dp, axis=-1, keepdims=True)
                ds = p * (dp - dd)
                dsinks = dsinks + jnp.where(lane == h, -jnp.sum(ps * dd, axis=0, keepdims=True), 0.0)
                dqh = _dot(ds, kh) * (ATT_HD ** -0.5)
                dqs.append(dqh.astype(CDT))
                _acc(dbq_ref.at[h], _colsum(dqh), first)
                dkh = _dot_tn(ds, qh) * (ATT_HD ** -0.5)
                dvh = _dot_tn(p, doh)
                dk = dkh if dk is None else dk + dkh
                dv = dvh if dv is None else dv + dvh
            for i in range(ATT_G // 2):
                lanes = pl.ds((kvh * ATT_G + 2 * i) * ATT_HD, 2 * ATT_HD)
                dq_ref[:, lanes] = _pair_lanes(dqs[2 * i], dqs[2 * i + 1])
            dkv_ref[kvh, rows_prev, :] += dk[:WINDOW]
            dkv_ref[kvh, rows_cur, :] += dk[WINDOW:]
            dkv_ref[ATT_KVH + kvh, rows_prev, :] += dv[:WINDOW]
            dkv_ref[ATT_KVH + kvh, rows_cur, :] += dv[WINDOW:]
        dsink_ref[...] += dsinks

    return _call(
        None, body, grid=(nb,),
        in_specs=[pl.BlockSpec(memory_space=pltpu.SMEM), cur(ATT_QH), cur(2 * ATT_KVH), prev(2 * ATT_KVH),
                  pl.BlockSpec((WINDOW, D), lambda n: (n, 0))],
        out_specs=[pl.BlockSpec((WINDOW, D), lambda n: (n, 0)), pl.BlockSpec((2 * ATT_KVH, S, ATT_HD), lambda n: (0, 0, 0)),
                   pl.BlockSpec((ATT_QH, 1, ATT_HD), lambda n: (0, 0, 0)), pl.BlockSpec((1, 128), lambda n: (0, 0))],
        out_shape=[_sds((S, D), CDT), _sds((2 * ATT_KVH, S, ATT_HD), F32), _sds((ATT_QH, 1, ATT_HD), F32),
                   _sds((1, 128), F32)],
        compiler_params=_params(("arbitrary",)), name="attn_bwd")(sinks, q4, kv4, kv4, do)


def _local_step(x, p, target, getw, sm, emit):
    S = x.shape[0]
    vec = lambda a: a.reshape(1, -1)
    ln_g = lambda l, k: vec(sm["ln_gain"][l, k])
    ln_b = lambda l, k: vec(sm["ln_bias"][l, k])
    xb = x.astype(CDT)
    pb = p.astype(CDT)

    proj = _mm_nn(xb, getw("a_w_in", None), (4, S, D), (None, _tile(S), 512), lambda g, i: (g // 2, i, g % 2), F32,
                  name="a_in")
    o_a, y_a, states = _hgrn_fwd(proj, sm["a_lower_bound"], sm["a_norm_gain"])
    zeros = jnp.zeros((1, D), F32)
    z = [[None] * 3 for _ in range(2)]
    xs = [[None] * 3 for _ in range(2)]
    xbs = [[None] * 3 for _ in range(2)]
    z[0][0], xs[0][0], xbs[0][0] = _mixout_ln(y_a[None], getw("a_w_out", y_a)[None], zeros, x, ln_g(0, 0), ln_b(0, 0),
                                              "a_out_ln")
    gu, hid, sgs, ups = [None, None], [None, None], [None, None], [None, None]

    def ffn_ple(l):
        wgu = getw(f"gu{l}", xbs[l][0])
        gu[l], hid[l], z[l][1], xs[l][1], xbs[l][1] = _ffn_fwd(
            xs[l][0], xbs[l][0], wgu, getw(f"dn{l}", None), ln_g(l, 1), ln_b(l, 1), f"ffn_fwd{l}")
        sgs[l], ups[l], z[l][2], xs[l][2], xbs[l][2] = _ple_fwd(
            xs[l][1], xbs[l][1], pb[l], getw(f"pg{l}", None), vec(sm["ple_b_gate"][l]), getw(f"pu{l}", None), ln_g(l, 2),
            ln_b(l, 2), f"ple_fwd{l}")

    ffn_ple(0)
    x3, x3b = xs[0][2], xbs[0][2]
    w_kv, w_q, w_bo = getw("kv_w", x3b), getw("b_w_q", None), getw("b_w_out", None)
    kv4 = _proj_heads(x3b, w_kv, vec(sm["kv_b"]), 2 * ATT_KVH, "kv_proj")
    q4 = _proj_heads(x3b, w_q, vec(sm["b_b_q"]), ATT_QH, "q_proj")
    o_b = _attn_fwd(q4, kv4, sm["b_sinks"])
    z[1][0], xs[1][0], xbs[1][0] = _mixout_ln(o_b[None], w_bo[None], sm["b_b_out"], x3, ln_g(1, 0), ln_b(1, 0),
                                              "b_out_ln")
    ffn_ple(1)
    loss, dy = _loss_fwd_bwd(xs[1][2], target)

    gs = {}
    d_ln_g = [[None] * 3 for _ in range(2)]
    d_ln_b = [[None] * 3 for _ in range(2)]
    g_bg = [None, None]

    def ffn_ple_bwd(l, dy, after=None):
        dx2, dgl, dup, d_ln_g[l][2], d_ln_b[l][2], g_bg[l] = _ple_bwd(dy, z[l][2], sgs[l], ups[l], ln_g(l, 2),
                                                                     getw(f"pg{l}", None), f"ple_bwd{l}", after=after)
        g_pg = _wgrad(xbs[l][1][None], dgl[None], f"g_ple_gate{l}")[0]
        g_pu = _wgrad(pb[l][None], dup[None], f"g_ple_up{l}")[0]
        dx1, dzb, dgu, d_ln_g[l][1], d_ln_b[l][1] = _ffn_bwd(dx2, z[l][1], gu[l], getw(f"gu{l}", None),
                                                           getw(f"dn{l}", None), ln_g(l, 1), f"ffn_bwd{l}")
        g_dn = _wgrad(hid[l], dzb[None], f"g_ffn_down{l}")
        g_gu = _wgrad(dgu.reshape(8, S, FFN_B), xbs[l][0][None], f"g_ffn_gate_up{l}")
        return dx1, emit({f"pg{l}": g_pg, f"pu{l}": g_pu, f"dn{l}": g_dn, f"gu{l}": g_gu})

    dx1, tok = ffn_ple_bwd(1, dy)
    dz, dzb, do, d_ln_g[1][0], d_ln_b[1][0], gs["b_b_out"] = _mixout_bwd(dx1, z[1][0], ln_g(1, 0), w_bo[None], CDT,
                                                                        "b_out_bwd", after=tok)
    g_bo = _wgrad(o_b[None], dzb[None], "g_b_w_out")[0]
    dq, dkv4, dbq, dsinks = _attn_bwd(q4, kv4, sm["b_sinks"], do[0])
    gs["b_b_q"] = dbq
    gs["b_sinks"] = dsinks
    g_q = _wgrad(x3b[None], dq[None], "g_b_w_q")[0]
    dx3, dkv, gs["kv_b"] = _qkv_bwd(dz, dq, dkv4, w_q, w_kv, "qkv_bwd", after=tok)
    g_kv = _wgrad(x3b[None], dkv[None], "g_kv_w")[0]
    tok = emit({"b_w_out": g_bo, "b_w_q": g_q, "kv_w": g_kv})
    dx1, tok = ffn_ple_bwd(0, dx3, tok)
    w_ao = getw("a_w_out", None)
    dz, dzb, dyr, d_ln_g[0][0], d_ln_b[0][0], _ = _mixout_bwd(dx1, z[0][0], ln_g(0, 0), w_ao[None], F32, "a_out_bwd",
                                                              after=tok)
    g_ao = _wgrad(y_a[None], dzb[None], "g_a_w_out")[0]
    tok = emit({"a_w_out": g_ao})
    dproj, gs["a_lower_bound"], gs["a_norm_gain"] = _hgrn_bwd(proj, sm["a_lower_bound"], sm["a_norm_gain"], o_a, states,
                                                              dyr[0], after=tok)
    tk = lambda t: (None, t, D)
    g_ain = _mm_tn(xb[None], dproj, N_DEV, lambda g, k: (0, k, 0), lambda g, k: (g // 2, k, g % 2),
                   tk, lambda t: (None, t, 512), (N_DEV, D, 512), (None, D, 512), lambda g, k: (g, 0, 0), name="g_a_w_in")
    gs["ple_b_gate"] = jnp.concatenate(g_bg, axis=0)
    gs["ln_gain"] = jnp.stack([jnp.concatenate(r, axis=0) for r in d_ln_g])
    gs["ln_bias"] = jnp.stack([jnp.concatenate(r, axis=0) for r in d_ln_b])
    tok = emit({"a_w_in": g_ain}, small=gs)
    grad_x = _inproj_bwd(dz, dproj, getw("a_w_in", None), "a_in_bwd", after=tok)
    return loss, grad_x, gs


def _peer(k):
    x, y, c = lax.axis_index("x"), lax.axis_index("y"), lax.axis_index("c")
    px = 1 - x if k & 4 else x
    py = 1 - y if k & 2 else y
    pc = 1 - c if k & 1 else c
    return (px, py, pc), 4 * px + 2 * py + pc


def _my_index():
    return 4 * lax.axis_index("x") + 2 * lax.axis_index("y") + lax.axis_index("c")


def _exchange(srcs, dst_shapes, plan, name):
    n_src, n_piece = len(srcs), len(plan)

    def body(*refs):
        src_refs, dst_refs = refs[:n_src], refs[n_src:n_src + len(dst_shapes)]
        send_sems, recv_sems, local_sems = refs[n_src + len(dst_shapes):]
        me = _my_index()

        def at(ref, idx):
            return ref.at[idx] if idx else ref

        local = []
        for t, (si, sfn, di, dfn) in enumerate(plan):
            cp = pltpu.make_async_copy(at(src_refs[si], sfn(me)), at(dst_refs[di], dfn(me)), local_sems.at[t])
            cp.start()
            local.append(cp)
        sends = []
        for k in range(1, N_DEV):
            peer, pid = _peer(k)
            for t, (si, sfn, di, dfn) in enumerate(plan):
                cp = pltpu.make_async_remote_copy(
                    src_ref=at(src_refs[si], sfn(pid)), dst_ref=at(dst_refs[di], dfn(me)),
                    send_sem=send_sems.at[t * 7 + k - 1], recv_sem=recv_sems.at[t * 7 + k - 1],
                    device_id=peer, device_id_type=MESH)
                cp.start()
                sends.append(cp)
        for k in range(1, N_DEV):
            peer, pid = _peer(k)
            for t, (si, sfn, di, dfn) in enumerate(plan):
                pltpu.make_async_remote_copy(
                    src_ref=at(src_refs[si], sfn(me)), dst_ref=at(dst_refs[di], dfn(pid)),
                    send_sem=send_sems.at[t * 7 + k - 1], recv_sem=recv_sems.at[t * 7 + k - 1],
                    device_id=peer, device_id_type=MESH).wait_recv()
        for cp in sends:
            cp.wait_send()
        for cp in local:
            cp.wait()

    hbm = pl.BlockSpec(memory_space=pltpu.HBM)
    return _call(
        None, body, in_specs=[hbm] * n_src, out_specs=[hbm] * len(dst_shapes), out_shape=dst_shapes,
        scratch_shapes=[pltpu.SemaphoreType.DMA((7 * n_piece,)), pltpu.SemaphoreType.DMA((7 * n_piece,)),
                        pltpu.SemaphoreType.DMA((n_piece,))],
        name=name)(*srcs)


def _gather(shards, name):
    dsts = [_sds((N_DEV,) + a.shape, a.dtype) for a in shards]
    plan = [(i, lambda j: (), i, lambda s: (s,)) for i in range(len(shards))]
    return _exchange(shards, dsts, plan, name)


_HBM = pl.BlockSpec(memory_space=pltpu.HBM)
_SEM = pl.BlockSpec(memory_space=pltpu.SEMAPHORE)
_DATAFLOW = pltpu.SideEffectType.DATAFLOW_SIDE_EFFECTING


def _piece_copy(mode, src, land, send_sems, recv_sems, t, k, sender, receiver, peer):
    return pltpu.make_async_remote_copy(
        src_ref=src if mode == "gather" else src.at[receiver], dst_ref=land.at[sender],
        send_sem=send_sems.at[t * 7 + k - 1], recv_sem=recv_sems.at[t * 7 + k - 1], device_id=peer, device_id_type=MESH)


def _sequencer_exchange(srcs, modes, name, collective_id, after=None):
    n = len(srcs)
    land_shapes = [((N_DEV,) + a.shape) if mode == "gather" else a.shape for a, mode in zip(srcs, modes)]
    extra = [] if after is None else [after]

    def body(*refs):
        src_refs, land_refs = refs[:n], refs[n + len(extra):2 * n + len(extra)]
        send_sems, recv_sems, local_sems = refs[2 * n + len(extra):]
        barrier = pltpu.get_barrier_semaphore()
        for k in range(1, N_DEV):
            pl.semaphore_signal(barrier, inc=1, device_id=_peer(k)[0], device_id_type=MESH)
        pl.semaphore_wait(barrier, N_DEV - 1)
        me = _my_index()
        local = []
        for i in range(n):
            cp = pltpu.make_async_copy(src_refs[i] if modes[i] == "gather" else src_refs[i].at[me], land_refs[i].at[me],
                                       local_sems.at[i])
            cp.start()
            local.append(cp)
        for k in range(1, N_DEV):
            peer, pid = _peer(k)
            for t in range(n):
                _piece_copy(modes[t], src_refs[t], land_refs[t], send_sems, recv_sems, t, k, me, pid, peer).start()
        for k in range(1, N_DEV):
            peer, pid = _peer(k)
            for t in range(n):
                _piece_copy(modes[t], src_refs[t], land_refs[t], send_sems, recv_sems, t, k, pid, me, peer).wait_recv()
        for k in range(1, N_DEV):
            peer, pid = _peer(k)
            for t in range(n):
                _piece_copy(modes[t], src_refs[t], land_refs[t], send_sems, recv_sems, t, k, me, pid, peer).wait_send()
        for cp in local:
            cp.wait()

    return pl.kernel(
        body, out_type=[_sds(s, a.dtype) for s, a in zip(land_shapes, srcs)],
        mesh=plsc.ScalarSubcoreMesh(axis_name="sequencer", num_cores=1),
        scratch_types=[pltpu.SemaphoreType.DMA((7 * n,)), pltpu.SemaphoreType.DMA((7 * n,)), pltpu.SemaphoreType.DMA((n,))],
        compiler_params=pltpu.CompilerParams(collective_id=collective_id), name=name)(*srcs, *extra)


def _sequencer_gather(srcs, name, collective_id, after=None):
    n = len(srcs)
    extra = [] if after is None else [after]

    def body(*refs):
        src_refs, land_refs = refs[:n], refs[n + len(extra):2 * n + len(extra)]
        send_sems, recv_sems, local_sems = refs[2 * n + len(extra):]
        x, y, c = lax.axis_index("x"), lax.axis_index("y"), lax.axis_index("c")
        sibling = (x, y, 1 - c)
        chips = [(1 - x, y), (x, 1 - y), (1 - x, 1 - y)]
        index = lambda px, py, pc: 4 * px + 2 * py + pc
        barrier = pltpu.get_barrier_semaphore()
        for peer in [sibling] + [(*chip, c) for chip in chips]:
            pl.semaphore_signal(barrier, inc=1, device_id=peer, device_id_type=MESH)
        pl.semaphore_wait(barrier, 4)

        def copy(t, k, slot, to, src=None):
            return pltpu.make_async_remote_copy(
                src_ref=land_refs[t].at[slot] if src is None else src, dst_ref=land_refs[t].at[slot],
                send_sem=send_sems.at[7 * t + k], recv_sem=recv_sems.at[7 * t + k], device_id=to, device_id_type=MESH)

        me = index(x, y, c)
        local = []
        for t in range(n):
            cp = pltpu.make_async_copy(src_refs[t], land_refs[t].at[me], local_sems.at[t])
            cp.start()
            local.append(cp)
        sends = []
        for t in range(n):
            sends.append(copy(t, 0, me, sibling, src=src_refs[t]))
            sends += [copy(t, 1 + j, me, (*chip, c), src=src_refs[t]) for j, chip in enumerate(chips)]
        for cp in sends:
            cp.start()
        for j, chip in enumerate(chips):
            for t in range(n):
                copy(t, 1 + j, index(*chip, c), sibling, src=src_refs[t]).wait_recv()
                passed = copy(t, 4 + j, index(*chip, c), sibling)
                passed.start()
                sends.append(passed)
        for t in range(n):
            copy(t, 0, index(x, y, 1 - c), sibling, src=src_refs[t]).wait_recv()
        for j, chip in enumerate(chips):
            for t in range(n):
                copy(t, 4 + j, index(*chip, 1 - c), sibling, src=src_refs[t]).wait_recv()
        for cp in sends:
            cp.wait_send()
        for cp in local:
            cp.wait()

    return pl.kernel(
        body, out_type=[_sds((N_DEV,) + a.shape, a.dtype) for a in srcs],
        mesh=plsc.ScalarSubcoreMesh(axis_name="sequencer", num_cores=1),
        scratch_types=[pltpu.SemaphoreType.DMA((7 * n,)), pltpu.SemaphoreType.DMA((7 * n,)), pltpu.SemaphoreType.DMA((n,))],
        compiler_params=pltpu.CompilerParams(collective_id=collective_id), name=name)(*srcs, *extra)


def _xstart(groups, mode, name, after=None):
    flat = [a for g in groups for a in g]
    n, ng = len(flat), len(groups)
    land_shapes = [((N_DEV,) + a.shape) if mode == "gather" else a.shape for a in flat]
    first = [sum(len(g) for g in groups[:i]) for i in range(ng)]
    extra = [] if after is None else [after]

    def body(*refs):
        srcs, lands = refs[:n], refs[n:2 * n]
        sems = refs[2 * n + len(extra):2 * n + len(extra) + 2 * ng]
        tok_ref, local_sems = refs[-2], refs[-1]
        me = _my_index()
        local = []
        for i in range(n):
            cp = pltpu.make_async_copy(srcs[i] if mode == "gather" else srcs[i].at[me], lands[i].at[me], local_sems.at[i])
            cp.start()
            local.append(cp)
        for cp in local:
            cp.wait()
        for gi, g in enumerate(groups):
            for k in range(1, N_DEV):
                peer, pid = _peer(k)
                for t in range(len(g)):
                    i = first[gi] + t
                    _piece_copy(mode, srcs[i], lands[i], sems[2 * gi], sems[2 * gi + 1], t, k, me, pid, peer).start()
        tok_ref[...] = jnp.zeros_like(tok_ref)

    sem_shapes = []
    for g in groups:
        sem_shapes += [pltpu.SemaphoreType.DMA((7 * len(g),))] * 2
    thru = [pltpu.HBM(a.shape, a.dtype) for a in flat] + [pltpu.HBM(s, a.dtype) for s, a in zip(land_shapes, flat)]
    outs = pl.pallas_call(
        body, in_specs=[_HBM] * (2 * n) + [pl.BlockSpec(memory_space=pl.ANY)] * len(extra),
        out_specs=[_SEM] * (2 * ng) + [_HBM] * (2 * n) + [pl.BlockSpec(memory_space=pltpu.VMEM)],
        out_shape=sem_shapes + thru + [_sds((8, 128), F32)],
        input_output_aliases={i: 2 * ng + i for i in range(2 * n)},
        scratch_shapes=[pltpu.SemaphoreType.DMA((n,))],
        compiler_params=pltpu.CompilerParams(has_side_effects=_DATAFLOW), name=name)(
            *[pltpu.with_memory_space_constraint(a, pltpu.HBM) for a in flat],
            *[pltpu.with_memory_space_constraint(lax.empty(s, a.dtype), pltpu.HBM) for s, a in zip(land_shapes, flat)], *extra)
    sems, srcs_thru, lands_thru = outs[:2 * ng], outs[2 * ng:2 * ng + n], outs[2 * ng + n:2 * ng + 2 * n]
    handles = [(sems[2 * gi], sems[2 * gi + 1], srcs_thru[first[gi]:first[gi] + len(g)],
                lands_thru[first[gi]:first[gi] + len(g)]) for gi, g in enumerate(groups)]
    return handles, outs[-1]


def _xwait(handle, mode, after, name):
    send_sems, recv_sems, srcs_thru, lands_thru = handle
    n = len(srcs_thru)

    def body(*refs):
        srcs, lands, send, recv = refs[:n], refs[n:2 * n], refs[2 * n], refs[2 * n + 1]
        me = _my_index()
        for k in range(1, N_DEV):
            peer, pid = _peer(k)
            for t in range(n):
                _piece_copy(mode, srcs[t], lands[t], send, recv, t, k, pid, me, peer).wait_recv()
        for k in range(1, N_DEV):
            peer, pid = _peer(k)
            for t in range(n):
                _piece_copy(mode, srcs[t], lands[t], send, recv, t, k, me, pid, peer).wait_send()

    extra = [] if after is None else [after]
    outs = pl.pallas_call(
        body, in_specs=[_HBM] * (2 * n) + [_SEM, _SEM] + [pl.BlockSpec(memory_space=pl.ANY)] * len(extra),
        out_specs=[_HBM] * (2 * n),
        out_shape=[pltpu.HBM(a.shape, a.dtype) for a in list(srcs_thru) + list(lands_thru)],
        input_output_aliases={i: i for i in range(2 * n)},
        compiler_params=pltpu.CompilerParams(has_side_effects=_DATAFLOW), name=name)(
            *srcs_thru, *lands_thru, send_sems, recv_sems, *extra)
    return outs[n:]


def _adamw(w, g, m, v):
    m = ADAM_B1 * m + (1.0 - ADAM_B1) * g
    v = ADAM_B2 * v + (1.0 - ADAM_B2) * (g * g)
    m_hat = m / (1.0 - ADAM_B1 ** ADAM_STEP)
    v_hat = v / (1.0 - ADAM_B2 ** ADAM_STEP)
    delta = -ADAM_LR * (m_hat / (jnp.sqrt(v_hat) + ADAM_EPS) + ADAM_WD * w)
    return delta, m, v


def _adam_big(w, parts, m, v, name, after=None):
    L, R, C = w.shape
    tr = _tile(R, (256, 128, 176, 64, 32, 16))
    nr = R // tr

    def body(w_ref, *refs):
        p_refs, (m_ref, v_ref, g_ref, d_ref, mo_ref, vo_ref) = refs[:L], refs[L:]
        for l in range(L):
            @pl.when(pl.program_id(0) == l)
            def _(p_ref=p_refs[l]):
                g = p_ref[0].astype(F32)
                for s in range(1, N_DEV):
                    g = g + p_ref[s].astype(F32)
                g_ref[...] = g
                d_ref[...], mo_ref[...], vo_ref[...] = _adamw(w_ref[...], g, m_ref[...], v_ref[...])

    row = pl.BlockSpec((None, tr, C), lambda l, i: (l, i, 0))
    park = lambda l_of: (lambda l, i: (0, jnp.where(l == l_of, i, 0 if l_of else nr - 1), 0))
    return _call(
        after, body, grid=(L, nr),
        in_specs=[row] + [pl.BlockSpec((N_DEV, tr, C), park(l)) for l in range(L)] + [row, row],
        out_specs=[row] * 4, out_shape=[_sds((L, R, C), F32)] * 4,
        compiler_params=_params(("arbitrary", "arbitrary")), name=name)(w, *parts, m, v)


SMALL = (("a_lower_bound", (2, 128), (2, D)), ("ln_gain", (6, 128), (6, D)), ("ln_bias", (6, 128), (6, D)),
         ("a_norm_gain", (1, 128), (1, 128)), ("kv_b", (1, 512), (1, 512)), ("b_b_q", (1, D), (1, D)),
         ("b_sinks", (1, ATT_QH), (1, 128)), ("b_b_out", (1, D), (1, D)), ("ple_b_gate", (2, D), (2, D)))


def _adam_small(parts, w, m, v, after=None):
    k = len(SMALL)

    def body(*refs):
        p_refs, w_refs, m_refs, v_refs = refs[:k], refs[k:2 * k], refs[2 * k:3 * k], refs[3 * k:4 * k]
        outs = refs[4 * k:]
        me = _my_index()
        for i, (_, wshape, pshape) in enumerate(SMALL):
            cols = wshape[1]
            lanes = slice(None) if cols == pshape[1] else (
                pl.ds(0, cols) if cols < 128 else pl.ds(pl.multiple_of(me * cols, cols), cols))
            g = p_refs[i][0, :, lanes]
            for s in range(1, N_DEV):
                g = g + p_refs[i][s, :, lanes]
            g_ref, d_ref, mo_ref, vo_ref = outs[4 * i:4 * i + 4]
            g_ref[...] = g
            d_ref[...], mo_ref[...], vo_ref[...] = _adamw(w_refs[i][...], g, m_refs[i][...], v_refs[i][...])

    full = lambda shape: pl.BlockSpec(shape, lambda: (0,) * len(shape))
    names = [n for n, _, _ in SMALL]
    res = _call(
        after, body,
        in_specs=[full((N_DEV,) + ps) for _, _, ps in SMALL] + [full(ws) for _, ws, _ in SMALL] * 3,
        out_specs=[full(ws) for _, ws, _ in SMALL for _ in range(4)],
        out_shape=[_sds(ws, F32) for _, ws, _ in SMALL for _ in range(4)], name="adam_small")(
            *[parts[n] for n in names], *[a[n].reshape(ws) for a in (w, m, v) for n, ws, _ in SMALL])
    return {n: [r.reshape(w[n].shape) for r in res[4 * i:4 * i + 4]] for i, n in enumerate(names)}


WEIGHTS = ("a_w_in", "a_lower_bound", "a_norm_gain", "a_w_out", "kv_w", "kv_b", "b_w_q", "b_b_q", "b_sinks", "b_w_out",
           "b_b_out", "ffn_w_gate_up", "ffn_w_down", "ple_w_up", "ple_w_gate", "ple_b_gate", "ln_gain", "ln_bias")


GATHER_GROUPS = (("a_w_in",), ("a_w_out", "gu0"), ("dn0", "pu0", "pg0"), ("kv_w", "b_w_q", "b_w_out"),
                 ("gu1", "dn1", "pu1", "pg1"))
KERNEL_LAYOUT = {
    "a_w_in": lambda a: a,
    "a_w_out": lambda a: a.reshape(D, D),
    "kv_w": lambda a: a.reshape(D, 2 * ATT_KVH * ATT_HD),
    "b_w_q": lambda a: a.reshape(D, D),
    "b_w_out": lambda a: a.reshape(D, D),
    "gu": lambda a: a.reshape(2, 4, FFN_B, D),
    "dn": lambda a: a.reshape(4, FFN_B, D),
    "pu": lambda a: a,
    "pg": lambda a: a.reshape(D, D),
}
_row_blocks = lambda a: a.reshape(N_DEV, -1, a.shape[-1])
OWNER_BLOCKS = {
    "a_w_in": lambda g: g,
    "a_w_out": _row_blocks,
    "kv_w": _row_blocks,
    "b_w_q": _row_blocks,
    "b_w_out": _row_blocks,
    "gu": lambda g: g,
    "dn": lambda g: _row_blocks(g.reshape(FFN_H, D)),
    "pu": lambda g: g.reshape(PLE_DIM, N_DEV, 128).transpose(1, 0, 2),
    "pg": _row_blocks,
}
ADAM_AFTER = {1: (("kv_w", "kv_w"), ("b_w_q", "b_w_q"), ("b_w_out", "b_w_out")),
              2: (("ffn_w_gate_up", "gu"), ("ffn_w_down", "dn"), ("ple_w_up", "pu"), ("ple_w_gate", "pg")),
              3: (("a_w_out", "a_w_out"),), 4: (("a_w_in", "a_w_in"),)}


def kernel(x, p, a_w_in, a_lower_bound, a_norm_gain, a_w_out, kv_w, kv_b, b_w_q, b_b_q, b_sinks, b_w_out, b_b_out, ffn_w_gate_up, ffn_w_down, ple_w_up, ple_w_gate, ple_b_gate, ln_gain, ln_bias, loss_target, m_a_w_in, m_a_lower_bound, m_a_norm_gain, m_a_w_out, m_kv_w, m_kv_b, m_b_w_q, m_b_b_q, m_b_sinks, m_b_w_out, m_b_b_out, m_ffn_w_gate_up, m_ffn_w_down, m_ple_w_up, m_ple_w_gate, m_ple_b_gate, m_ln_gain, m_ln_bias, v_a_w_in, v_a_lower_bound, v_a_norm_gain, v_a_w_out, v_kv_w, v_kv_b, v_b_w_q, v_b_b_q, v_b_sinks, v_b_w_out, v_b_b_out, v_ffn_w_gate_up, v_ffn_w_down, v_ple_w_up, v_ple_w_gate, v_ple_b_gate, v_ln_gain, v_ln_bias):
    given = dict(locals())
    w = {n: given[n] for n in WEIGHTS}
    m = {n: given["m_" + n] for n in WEIGHTS}
    v = {n: given["v_" + n] for n in WEIGHTS}
    shards = {"a_w_in": a_w_in[0], "a_w_out": a_w_out[0], "kv_w": kv_w, "b_w_q": b_w_q[0], "b_w_out": b_w_out[0]}
    for l in range(2):
        shards.update({f"gu{l}": ffn_w_gate_up[l].T, f"dn{l}": ffn_w_down[l], f"pu{l}": ple_w_up[l], f"pg{l}": ple_w_gate[l]})
    sharded_small = [a_lower_bound, ln_gain.reshape(6, 128), ln_bias.reshape(6, 128)]
    gathered = {}
    for gi, g in enumerate(GATHER_GROUPS):
        lands = _sequencer_gather([shards[n].astype(CDT) for n in g] + (sharded_small if gi == 0 else []),
                                  f"gather{gi}", gi)
        for n, a in zip(g, lands):
            gathered[n] = KERNEL_LAYOUT[n.rstrip("01")](a)
        if gi == 0:
            alb, lng, lnb = [a.transpose(1, 0, 2).reshape(a.shape[1], D) for a in lands[len(g):]]

    def getw(key, after):
        return gathered[key]

    sm = {"a_lower_bound": alb, "ln_gain": lng.reshape(2, 3, D), "ln_bias": lnb.reshape(2, 3, D),
          "a_norm_gain": a_norm_gain, "kv_b": kv_b, "b_b_q": b_b_q[0], "b_sinks": b_sinks, "b_b_out": b_b_out,
          "ple_b_gate": ple_b_gate}

    scatters, small_parts = [], {}

    def emit(grads, small=None):
        names = list(grads)
        blocks = [OWNER_BLOCKS[n.rstrip("01")](grads[n]) for n in names]
        partials = [] if small is None else [small[n].reshape(ps) for n, _, ps in SMALL]
        lands = _sequencer_exchange(blocks + partials, ["scatter"] * len(blocks) + ["gather"] * len(partials),
                                    f"scatter{len(scatters)}", len(GATHER_GROUPS) + len(scatters))
        scatters.append(dict(zip(names, lands)))
        small_parts.update(zip([n for n, _, _ in SMALL], lands[len(blocks):]))
        return blocks

    loss, grad_x, gs = _local_step(x[0], p[:, 0], loss_target[0], getw, sm, emit)

    out, parts, last = {}, {}, [grad_x] + list(scatters[0].values())
    for i, landed in enumerate(scatters):
        parts.update(landed)
        for n, key in ADAM_AFTER.get(i, ()):
            lrc = (1,) * (3 - w[n].ndim) + w[n].shape
            layers = [parts[key]] if key in parts else [parts[key + "0"], parts[key + "1"]]
            shard = (lambda a: a.reshape(lrc).swapaxes(1, 2)) if key == "gu" else (lambda a: a.reshape(lrc))
            res = _adam_big(shard(w[n]), layers, shard(m[n]), shard(v[n]), "adam_" + n, after=last)
            out[n] = [(r.swapaxes(1, 2) if key == "gu" else r).reshape(w[n].shape) for r in res]
            last = [res[3]]
    out.update(_adam_small(small_parts, w, m, v, after=last))

    loss = lax.psum(loss[0, 0], ("x", "y", "c"))
    res = [loss, grad_x[None]]
    for i in range(4):
        res += [out[n][i] for n in WEIGHTS]
    return tuple(res)
```

```python
import jax
import jax.numpy as jnp
from jax import lax
from jax.experimental import pallas as pl
from jax.experimental.pallas import tpu as pltpu
from jax.experimental.pallas import tpu_sc as plsc

F32 = jnp.float32
CDT = jnp.bfloat16

N_DEV = 8
D = 1024
HG_H, HG_DK, HG_CH = 8, 128, 64
HG_HPB = 4
HG_CPB = 8
ATT_HD, ATT_QH, ATT_KVH, ATT_G, WINDOW = 64, 16, 4, 4, 128
FFN_H = 2816
FFN_B = FFN_H // 4
PLE_DIM = 256
ALPHA = (2.0 * 2) ** 0.25
LN_EPS = 1e-5
RMS_EPS = 1e-6
ADAM_LR, ADAM_B1, ADAM_B2, ADAM_EPS, ADAM_WD, ADAM_STEP = 0.001, 0.9, 0.999, 1e-08, 0.01, 10
ROW_TILES = (512, 256, 128, 64)
VMEM_LIMIT = 48 * 1024 * 1024
NEG = -1e30

MESH = pl.DeviceIdType.MESH


def _tile(n, cands=ROW_TILES):
    for t in cands:
        if n % t == 0:
            return t
    return n


def _sds(shape, dtype):
    return jax.ShapeDtypeStruct(tuple(shape), dtype)


def _params(sem):
    return pltpu.CompilerParams(dimension_semantics=sem, vmem_limit_bytes=VMEM_LIMIT)


def _dot(a, b):
    return jnp.dot(a.astype(CDT), b.astype(CDT), preferred_element_type=F32)


def _dot_nt(a, b):
    return lax.dot_general(a.astype(CDT), b.astype(CDT), (((1,), (1,)), ((), ())), preferred_element_type=F32)


def _dot_tn(a, b):
    return lax.dot_general(a.astype(CDT), b.astype(CDT), (((0,), (0,)), ((), ())), preferred_element_type=F32)


def _sigmoid(x):
    return jax.nn.sigmoid(x)


def _ln_fwd(z, g, b):
    mu = jnp.mean(z, axis=-1, keepdims=True)
    zc = z - mu
    var = jnp.mean(zc * zc, axis=-1, keepdims=True)
    return zc * lax.rsqrt(var + LN_EPS) * g + b


def _ln_bwd(z, g, dy):
    mu = jnp.mean(z, axis=-1, keepdims=True)
    zc = z - mu
    var = jnp.mean(zc * zc, axis=-1, keepdims=True)
    rstd = lax.rsqrt(var + LN_EPS)
    xhat = zc * rstd
    dxh = dy * g
    dz = rstd * (dxh - jnp.mean(dxh, axis=-1, keepdims=True) - xhat * jnp.mean(dxh * xhat, axis=-1, keepdims=True))
    return dz, xhat


def _colsum(x):
    return jnp.sum(x, axis=0, keepdims=True)


def _acc(ref, val, first):
    @pl.when(first)
    def _():
        ref[...] = val

    @pl.when(jnp.logical_not(first))
    def _():
        ref[...] += val


def _zero_at(ref, first):
    @pl.when(first)
    def _():
        ref[...] = jnp.zeros_like(ref)


def _call(after, body, **kw):
    after = [] if after is None else list(after)
    specs = list(kw["in_specs"])
    kw["in_specs"] = [pl.BlockSpec(memory_space=pl.ANY)] * len(after) + specs

    def ordered_body(*refs):
        body(*refs[len(after):])

    call = pl.pallas_call(ordered_body, **kw)

    def pinned(*args):
        args = [a if s.memory_space is not None else pltpu.with_memory_space_constraint(a, pltpu.HBM)
                for a, s in zip(args, specs)]
        return call(*after, *args)

    return pinned


def _mm_nn(a, b3, out_shape, oblock, omap, out_dtype, bias3=None, name="mm_nn"):
    M, K = a.shape
    G, _, Nb = b3.shape
    tm = _tile(M)

    def body(a_ref, b_ref, *rest):
        o_ref = rest[-1]
        acc = _dot(a_ref[...], b_ref[...])
        if bias3 is not None:
            acc = acc + rest[0][...]
        o_ref[...] = acc.astype(o_ref.dtype)

    in_specs = [pl.BlockSpec((tm, K), lambda g, i: (i, 0)), pl.BlockSpec((None, K, Nb), lambda g, i: (g, 0, 0))]
    args = [a, b3]
    if bias3 is not None:
        in_specs.append(pl.BlockSpec((None, 1, Nb), lambda g, i: (g, 0, 0)))
        args.append(bias3)
    return _call(
        None, body, grid=(G, M // tm), in_specs=in_specs, out_specs=pl.BlockSpec(oblock, omap),
        out_shape=_sds(out_shape, out_dtype), compiler_params=_params(("arbitrary", "arbitrary")), name=name)(*args)


def _mm_tn(a3, b3, G, amap, bmap, ablock, bblock, out_shape, oblock, omap, name="mm_tn"):
    S = a3.shape[1]

    def body(a_ref, b_ref, o_ref):
        o_ref[...] = _dot_tn(a_ref[...], b_ref[...]).astype(o_ref.dtype)

    return _call(
        None, body, grid=(G, 1),
        in_specs=[pl.BlockSpec(ablock(S), amap), pl.BlockSpec(bblock(S), bmap)],
        out_specs=pl.BlockSpec(oblock, omap), out_shape=_sds(out_shape, CDT),
        compiler_params=_params(("arbitrary", "arbitrary")), name=name)(a3, b3)


def _wgrad(a3, b3, name):
    Ga, S, M = a3.shape
    Gb, _, N = b3.shape
    G = max(Ga, Gb)
    return _mm_tn(
        a3, b3, G,
        (lambda g, k: (g, k, 0)) if Ga > 1 else (lambda g, k: (0, k, 0)),
        (lambda g, k: (g, k, 0)) if Gb > 1 else (lambda g, k: (0, k, 0)),
        lambda tk: (None, tk, M), lambda tk: (None, tk, N),
        (G, M, N), (None, M, N), lambda g, k: (g, 0, 0), name=name)


def _mixout_ln(u3, w3, bias, xin, gain, beta, name):
    G, S, Kb = u3.shape
    tm = _tile(S)

    def body(u_ref, w_ref, b_ref, x_ref, g_ref, be_ref, z_ref, xo_ref, xob_ref):
        h = b_ref[...] + _dot(u_ref[0], w_ref[0])
        for g in range(1, G):
            h = h + _dot(u_ref[g], w_ref[g])
        z = ALPHA * x_ref[...] + h
        z_ref[...] = z
        y = _ln_fwd(z, g_ref[...], be_ref[...])
        xo_ref[...] = y
        xob_ref[...] = y.astype(CDT)

    row = pl.BlockSpec((tm, D), lambda i: (i, 0))
    vec = pl.BlockSpec((1, D), lambda i: (0, 0))
    return _call(
        None, body, grid=(S // tm,),
        in_specs=[pl.BlockSpec((G, tm, Kb), lambda i: (0, i, 0)), pl.BlockSpec((G, Kb, D), lambda i: (0, 0, 0)),
                  vec, row, vec, vec],
        out_specs=[row, row, row], out_shape=[_sds((S, D), F32), _sds((S, D), F32), _sds((S, D), CDT)],
        compiler_params=_params(("arbitrary",)), name=name)(u3, w3, bias, xin, gain, beta)


def _ffn_fwd(xin, xin_b, wgu, wdn, gain, beta, name):
    S = xin.shape[0]
    tm = _tile(S)

    def hidden(xb_ref, wgu_ref, gu_ref, hid_ref):
        xb = xb_ref[...]
        gate = _dot_nt(xb, wgu_ref[0])
        up = _dot_nt(xb, wgu_ref[1])
        gu_ref[0] = gate
        gu_ref[1] = up
        hid_ref[...] = (gate * _sigmoid(gate) * up).astype(CDT)

    gu, hid = _call(
        None, hidden, grid=(S // tm, 4),
        in_specs=[pl.BlockSpec((tm, D), lambda i, j: (i, 0)), pl.BlockSpec((2, None, FFN_B, D), lambda i, j: (0, j, 0, 0))],
        out_specs=[pl.BlockSpec((2, None, tm, FFN_B), lambda i, j: (0, j, i, 0)),
                   pl.BlockSpec((None, tm, FFN_B), lambda i, j: (j, i, 0))],
        out_shape=[_sds((2, 4, S, FFN_B), F32), _sds((4, S, FFN_B), CDT)],
        compiler_params=_params(("arbitrary", "arbitrary")), name=name + "_hidden")(xin_b, wgu)

    def down(x_ref, hid_ref, wdn_ref, g_ref, be_ref, z_ref, xo_ref, xob_ref):
        z = ALPHA * x_ref[...]
        for j in range(4):
            z = z + _dot(hid_ref[j], wdn_ref[j])
        z_ref[...] = z
        y = _ln_fwd(z, g_ref[...], be_ref[...])
        xo_ref[...] = y
        xob_ref[...] = y.astype(CDT)

    row = pl.BlockSpec((tm, D), lambda i: (i, 0))
    vec = pl.BlockSpec((1, D), lambda i: (0, 0))
    z, xo, xob = _call(
        None, down, grid=(S // tm,),
        in_specs=[row, pl.BlockSpec((4, tm, FFN_B), lambda i: (0, i, 0)), pl.BlockSpec((4, FFN_B, D), lambda i: (0, 0, 0)),
                  vec, vec],
        out_specs=[row, row, row], out_shape=[_sds((S, D), F32), _sds((S, D), F32), _sds((S, D), CDT)],
        compiler_params=_params(("arbitrary",)), name=name + "_down")(xin, hid, wdn, gain, beta)
    return gu, hid, z, xo, xob


def _ple_fwd(xin, xin_b, p_b, wpg, bgate, wpu, gain, beta, name):
    S = xin.shape[0]
    tm = _tile(S)

    def body(x_ref, xb_ref, p_ref, wpg_ref, bg_ref, wpu_ref, g_ref, be_ref, sg_ref, up_ref, z_ref, xo_ref, xob_ref):
        sg = _sigmoid(_dot(xb_ref[...], wpg_ref[...]) + bg_ref[...])
        pb = p_ref[...]
        up = jnp.concatenate([_dot(pb, wpu_ref[j]) for j in range(N_DEV)], axis=-1)
        sg_ref[...] = sg
        up_ref[...] = up
        z = ALPHA * x_ref[...] + sg * up
        z_ref[...] = z
        y = _ln_fwd(z, g_ref[...], be_ref[...])
        xo_ref[...] = y
        xob_ref[...] = y.astype(CDT)

    row = pl.BlockSpec((tm, D), lambda i: (i, 0))
    vec = pl.BlockSpec((1, D), lambda i: (0, 0))
    return _call(
        None, body, grid=(S // tm,),
        in_specs=[row, row, pl.BlockSpec((tm, PLE_DIM), lambda i: (i, 0)), pl.BlockSpec((D, D), lambda i: (0, 0)), vec,
                  pl.BlockSpec((N_DEV, PLE_DIM, D // N_DEV), lambda i: (0, 0, 0)), vec, vec],
        out_specs=[row] * 5,
        out_shape=[_sds((S, D), F32)] * 4 + [_sds((S, D), CDT)],
        compiler_params=_params(("arbitrary",)), name=name)(xin, xin_b, p_b, wpg, bgate, wpu, gain, beta)


def _loss_fwd_bwd(y, target):
    S = y.shape[0]
    tm = _tile(S)

    def body(y_ref, t_ref, l_ref, dy_ref):
        e = y_ref[...] - t_ref[...]
        dy_ref[...] = e * (1.0 / D)
        part = 0.5 * jnp.sum(jnp.sum(e * e, axis=-1, keepdims=True) * (1.0 / D), axis=0, keepdims=True)
        _acc(l_ref, part, pl.program_id(0) == 0)

    row = pl.BlockSpec((tm, D), lambda i: (i, 0))
    return _call(
        None, body, grid=(S // tm,), in_specs=[row, row],
        out_specs=[pl.BlockSpec((1, 1), lambda i: (0, 0)), row],
        out_shape=[_sds((1, 1), F32), _sds((S, D), F32)],
        compiler_params=_params(("arbitrary",)), name="loss")(y, target)


def _ple_bwd(dy, z, sg, up, gain, wpg, name, after=None):
    S = dy.shape[0]
    tm = _tile(S)

    def body(dy_ref, z_ref, sg_ref, up_ref, g_ref, wpg_ref, dx_ref, dgl_ref, dup_ref, dgain_ref, dbeta_ref, dbg_ref):
        first = pl.program_id(0) == 0
        dy_ = dy_ref[...]
        dz, xhat = _ln_bwd(z_ref[...], g_ref[...], dy_)
        sg_ = sg_ref[...]
        dgl = dz * up_ref[...] * sg_ * (1.0 - sg_)
        dgl_ref[...] = dgl.astype(CDT)
        dup_ref[...] = (dz * sg_).astype(CDT)
        dx_ref[...] = ALPHA * dz + _dot_nt(dgl, wpg_ref[...])
        _acc(dgain_ref, _colsum(dy_ * xhat), first)
        _acc(dbeta_ref, _colsum(dy_), first)
        _acc(dbg_ref, _colsum(dgl), first)

    row = pl.BlockSpec((tm, D), lambda i: (i, 0))
    vec = pl.BlockSpec((1, D), lambda i: (0, 0))
    return _call(
        after, body, grid=(S // tm,), in_specs=[row, row, row, row, vec, pl.BlockSpec((D, D), lambda i: (0, 0))],
        out_specs=[row, row, row, vec, vec, vec],
        out_shape=[_sds((S, D), F32), _sds((S, D), CDT), _sds((S, D), CDT)] + [_sds((1, D), F32)] * 3,
        compiler_params=_params(("arbitrary",)), name=name)(dy, z, sg, up, gain, wpg)


def _ffn_bwd(dy, z, gu, wgu, wdn, gain, name, after=None):
    S = dy.shape[0]
    tm = _tile(S)

    def hidden(dy_ref, z_ref, gu_ref, wdn_ref, g_ref, dz_ref, dzb_ref, dgu_ref, dgain_ref, dbeta_ref):
        i, j = pl.program_id(0), pl.program_id(1)

        @pl.when(j == 0)
        def _():
            dy_ = dy_ref[...]
            dz, xhat = _ln_bwd(z_ref[...], g_ref[...], dy_)
            dz_ref[...] = dz
            dzb_ref[...] = dz.astype(CDT)
            _acc(dgain_ref, _colsum(dy_ * xhat), i == 0)
            _acc(dbeta_ref, _colsum(dy_), i == 0)

        dhid = _dot_nt(dzb_ref[...], wdn_ref[...])
        gate, up = gu_ref[0], gu_ref[1]
        sg = _sigmoid(gate)
        dgu_ref[0] = (dhid * up * (sg * (1.0 + gate * (1.0 - sg)))).astype(CDT)
        dgu_ref[1] = (dhid * (gate * sg)).astype(CDT)

    row = pl.BlockSpec((tm, D), lambda i, j: (i, 0))
    vec = pl.BlockSpec((1, D), lambda i, j: (0, 0))
    dz, dzb, dgu, dgain, dbeta = _call(
        after, hidden, grid=(S // tm, 4),
        in_specs=[row, row, pl.BlockSpec((2, None, tm, FFN_B), lambda i, j: (0, j, i, 0)),
                  pl.BlockSpec((None, FFN_B, D), lambda i, j: (j, 0, 0)), vec],
        out_specs=[row, row, pl.BlockSpec((2, None, tm, FFN_B), lambda i, j: (0, j, i, 0)), vec, vec],
        out_shape=[_sds((S, D), F32), _sds((S, D), CDT), _sds((2, 4, S, FFN_B), CDT), _sds((1, D), F32),
                   _sds((1, D), F32)],
        compiler_params=_params(("arbitrary", "arbitrary")), name=name + "_hidden")(dy, z, gu, wdn, gain)

    def to_input(dz_ref, dgu_ref, wgu_ref, dx_ref):
        acc = ALPHA * dz_ref[...]
        for g in range(2):
            for j in range(4):
                acc = acc + _dot(dgu_ref[g, j], wgu_ref[g, j])
        dx_ref[...] = acc

    rows = pl.BlockSpec((tm, D), lambda i: (i, 0))
    dx = _call(
        None, to_input, grid=(S // tm,),
        in_specs=[rows, pl.BlockSpec((2, 4, tm, FFN_B), lambda i: (0, 0, i, 0)),
                  pl.BlockSpec((2, 4, FFN_B, D), lambda i: (0, 0, 0, 0))],
        out_specs=rows, out_shape=_sds((S, D), F32),
        compiler_params=_params(("arbitrary",)), name=name + "_input")(dz, dgu, wgu)
    return dx, dzb, dgu, dgain, dbeta


def _mixout_bwd(dy, z, gain, w3, du_dtype, name, after=None):
    S = dy.shape[0]
    G, Kb, _ = w3.shape
    tm = _tile(S)

    def body(dy_ref, z_ref, g_ref, w_ref, dz_ref, dzb_ref, du_ref, dgain_ref, dbeta_ref, dbias_ref):
        first = pl.program_id(0) == 0
        dy_ = dy_ref[...]
        dz, xhat = _ln_bwd(z_ref[...], g_ref[...], dy_)
        dz_ref[...] = dz
        dzb = dz.astype(CDT)
        dzb_ref[...] = dzb
        for g in range(G):
            du_ref[g] = _dot_nt(dzb, w_ref[g]).astype(du_ref.dtype)
        _acc(dgain_ref, _colsum(dy_ * xhat), first)
        _acc(dbeta_ref, _colsum(dy_), first)
        _acc(dbias_ref, _colsum(dz), first)

    row = pl.BlockSpec((tm, D), lambda i: (i, 0))
    vec = pl.BlockSpec((1, D), lambda i: (0, 0))
    return _call(
        after, body, grid=(S // tm,), in_specs=[row, row, vec, pl.BlockSpec((G, Kb, D), lambda i: (0, 0, 0))],
        out_specs=[row, row, pl.BlockSpec((G, tm, Kb), lambda i: (0, i, 0)), vec, vec, vec],
        out_shape=[_sds((S, D), F32), _sds((S, D), CDT), _sds((G, S, Kb), du_dtype)] + [_sds((1, D), F32)] * 3,
        compiler_params=_params(("arbitrary",)), name=name)(dy, z, gain, w3)


def _half_select(low):
    r = lax.broadcasted_iota(jnp.int32, (2 * ATT_HD, ATT_HD), 0)
    c = lax.broadcasted_iota(jnp.int32, (2 * ATT_HD, ATT_HD), 1)
    return (r == c + (0 if low else ATT_HD)).astype(CDT)


def _half_place(low):
    r = lax.broadcasted_iota(jnp.int32, (ATT_HD, 2 * ATT_HD), 0)
    c = lax.broadcasted_iota(jnp.int32, (ATT_HD, 2 * ATT_HD), 1)
    return (c == r + (0 if low else ATT_HD)).astype(CDT)


def _pair_lanes(even, odd):
    return (jnp.dot(even, _half_place(True), preferred_element_type=F32)
            + jnp.dot(odd, _half_place(False), preferred_element_type=F32)).astype(CDT)


def _proj_heads(a, w, bias, heads, name):
    S, K = a.shape
    N = heads * ATT_HD
    tm = _tile(S)

    def body(a_ref, w_ref, b_ref, o_ref):
        acc = (_dot(a_ref[...], w_ref[...]) + b_ref[...]).astype(CDT)
        sel = (_half_select(True), _half_select(False))
        for h in range(heads):
            pair = acc[:, (h // 2) * 2 * ATT_HD:(h // 2 + 1) * 2 * ATT_HD]
            o_ref[h] = jnp.dot(pair, sel[h % 2], preferred_element_type=F32).astype(CDT)

    return _call(
        None, body, grid=(S // tm,),
        in_specs=[pl.BlockSpec((tm, K), lambda i: (i, 0)), pl.BlockSpec((K, N), lambda i: (0, 0)),
                  pl.BlockSpec((1, N), lambda i: (0, 0))],
        out_specs=pl.BlockSpec((heads, tm, ATT_HD), lambda i: (0, i, 0)), out_shape=_sds((heads, S, ATT_HD), CDT),
        compiler_params=_params(("arbitrary",)), name=name)(a, w, bias)


def _qkv_bwd(dz, dq, dkv4, wq, wkv, name, after=None):
    S = dz.shape[0]
    tm = _tile(S)
    HK = dkv4.shape[0]
    NK = HK * ATT_HD

    def body(dz_ref, dq_ref, dkv_ref, wq_ref, wkv_ref, dx_ref, dkvn_ref, dkvb_ref):
        first = pl.program_id(0) == 0
        dkvn = jnp.concatenate([_pair_lanes(dkv_ref[2 * i].astype(CDT), dkv_ref[2 * i + 1].astype(CDT))
                                for i in range(HK // 2)], axis=-1)
        dkvn_ref[...] = dkvn
        dx_ref[...] = ALPHA * dz_ref[...] + _dot_nt(dq_ref[...], wq_ref[...]) + _dot_nt(dkvn, wkv_ref[...])
        for h in range(HK):
            _acc(dkvb_ref.at[h], _colsum(dkv_ref[h]), first)

    row = pl.BlockSpec((tm, D), lambda i: (i, 0))
    return _call(
        after, body, grid=(S // tm,),
        in_specs=[row, row, pl.BlockSpec((HK, tm, ATT_HD), lambda i: (0, i, 0)),
                  pl.BlockSpec((D, D), lambda i: (0, 0)), pl.BlockSpec((D, NK), lambda i: (0, 0))],
        out_specs=[row, pl.BlockSpec((tm, NK), lambda i: (i, 0)), pl.BlockSpec((HK, 1, ATT_HD), lambda i: (0, 0, 0))],
        out_shape=[_sds((S, D), F32), _sds((S, NK), CDT), _sds((HK, 1, ATT_HD), F32)],
        compiler_params=_params(("arbitrary",)), name=name)(dz, dq, dkv4, wq, wkv)


def _inproj_bwd(dz, dproj, wain, name, after=None):
    S = dz.shape[0]
    tm = _tile(S)
    nb = wain.shape[-1]

    def body(dz_ref, dp_ref, w_ref, dx_ref):
        acc = ALPHA * dz_ref[...]
        for j in range(N_DEV):
            acc = acc + _dot_nt(dp_ref[j // 2, :, pl.ds((j % 2) * nb, nb)], w_ref[j])
        dx_ref[...] = acc

    row = pl.BlockSpec((tm, D), lambda i: (i, 0))
    return _call(
        after, body, grid=(S // tm,),
        in_specs=[row, pl.BlockSpec((4, tm, D), lambda i: (0, i, 0)), pl.BlockSpec((N_DEV, D, nb), lambda i: (0, 0, 0))],
        out_specs=row, out_shape=_sds((S, D), F32),
        compiler_params=_params(("arbitrary",)), name=name)(dz, dproj, wain)


def _running_sum(x, reverse=False):
    rows = x.shape[0]
    row = lax.broadcasted_iota(jnp.int32, x.shape, 0)
    step = 1
    while step < rows:
        if reverse:
            x = x + jnp.where(row < rows - step, pltpu.roll(x, rows - step, 0), 0.0)
        else:
            x = x + jnp.where(row >= step, pltpu.roll(x, step, 0), 0.0)
        step *= 2
    return x


def _hg_gates(q, f, alb_ref):
    a0, a1 = alb_ref[0:1, :], alb_ref[1:2, :]
    mx = jnp.maximum(a0, a1)
    e0, e1 = jnp.exp(a0 - mx), jnp.exp(a1 - mx)
    lb = e0 / (e0 + e1)
    sig = _sigmoid(f)
    forget = lb + (1.0 - lb) * sig
    k = (1.0 - lb) * _sigmoid(-f)
    qs = q * _sigmoid(q) * (HG_DK ** -0.5)
    return qs, k, jnp.log(forget), sig, lb, forget


def _hg_intra(qs, k, b, b_scr):
    b_scr[...] = b
    bm = b_scr[pl.ds(HG_CH // 2 - 1, 1), :]
    bl = b_scr[pl.ds(HG_CH - 1, 1), :]
    eb = jnp.exp(b)
    qb = qs * eb
    e_q = jnp.exp(b - bm)
    e_k = jnp.exp(bm - b)
    e_d = jnp.exp(bl - b)
    return qb, qs * e_q, k * e_k, k * e_d, jnp.exp(bl), eb, e_q, e_k, e_d


def _hgrn_fwd(proj, alb, ngain):
    S = proj.shape[1]
    nc = S // HG_CH
    nb = nc // HG_CPB
    rb, wb = HG_CPB * HG_CH, HG_HPB * HG_DK

    def body(pj_ref, alb_ref, ng_ref, o_ref, y_ref, st_ref, st_scr, b_scr):
        n = pl.program_id(1)

        @pl.when(n == 0)
        def _():
            st_scr[...] = jnp.zeros_like(st_scr)

        r = lax.broadcasted_iota(jnp.int32, (HG_CH, HG_CH), 0)
        c = lax.broadcasted_iota(jnp.int32, (HG_CH, HG_CH), 1)
        causal = r >= c
        for ci, j in [(ci, j) for ci in range(HG_CPB) for j in range(HG_HPB)]:
            rows, lanes = pl.ds(ci * HG_CH, HG_CH), pl.ds(j * HG_DK, HG_DK)
            q, f, v, g = pj_ref[0, rows, lanes], pj_ref[1, rows, lanes], pj_ref[2, rows, lanes], pj_ref[3, rows, lanes]
            qs, k, logf, _, _, _ = _hg_gates(q, f, alb_ref.at[:, lanes])
            b = _running_sum(logf)
            qb, qt, kt, kd, ebl, _, _, _, _ = _hg_intra(qs, k, b, b_scr.at[j, ci])
            st = st_scr[j]
            st_ref[j, ci] = st
            a = jnp.where(causal, _dot_nt(qt, kt), 0.0)
            o = _dot(a, v) + _dot_nt(qb, st)
            st_scr[j] = st * ebl + _dot_tn(v, kd)
            o_ref[rows, lanes] = o
            rinv = lax.rsqrt(jnp.mean(o * o, axis=-1, keepdims=True) + RMS_EPS)
            y_ref[rows, lanes] = (o * rinv * ng_ref[...] * (g * _sigmoid(g))).astype(CDT)

    blk = pl.BlockSpec((rb, wb), lambda h, n: (n, h))
    return _call(
        None, body, grid=(HG_H // HG_HPB, nb),
        in_specs=[pl.BlockSpec((4, rb, wb), lambda h, n: (0, n, h)), pl.BlockSpec((2, wb), lambda h, n: (0, h)),
                  pl.BlockSpec((1, HG_DK), lambda h, n: (0, 0))],
        out_specs=[blk, blk, pl.BlockSpec((HG_HPB, HG_CPB, HG_DK, HG_DK), lambda h, n: (h, n, 0, 0))],
        out_shape=[_sds((S, D), F32), _sds((S, D), CDT), _sds((HG_H, nc, HG_DK, HG_DK), F32)],
        scratch_shapes=[pltpu.VMEM((HG_HPB, HG_DK, HG_DK), F32), pltpu.VMEM((HG_HPB, HG_CPB, HG_CH, HG_DK), F32)],
        compiler_params=_params(("arbitrary", "arbitrary")), name="hgrn_fwd")(proj, alb, ngain)


def _hgrn_bwd(proj, alb, ngain, o, states, dy, after=None):
    S = proj.shape[1]
    nc = S // HG_CH
    nb = nc // HG_CPB
    rb, wb = HG_CPB * HG_CH, HG_HPB * HG_DK

    def body(pj_ref, alb_ref, ng_ref, o_ref, st_ref, dy_ref, dpj_ref, dalb_ref, dng_ref, dst_scr, b_scr):
        h, n = pl.program_id(0), pl.program_id(1)

        @pl.when(n == 0)
        def _():
            dst_scr[...] = jnp.zeros_like(dst_scr)
            dalb_ref[...] = jnp.zeros_like(dalb_ref)

        _zero_at(dng_ref, jnp.logical_and(h == 0, n == 0))
        ng = ng_ref[...]
        r = lax.broadcasted_iota(jnp.int32, (HG_CH, HG_CH), 0)
        c = lax.broadcasted_iota(jnp.int32, (HG_CH, HG_CH), 1)
        causal = r >= c
        dng = None
        for ci, j in [(ci, j) for ci in reversed(range(HG_CPB)) for j in range(HG_HPB)]:
            rows, lanes = pl.ds(ci * HG_CH, HG_CH), pl.ds(j * HG_DK, HG_DK)
            q, f, v, g = pj_ref[0, rows, lanes], pj_ref[1, rows, lanes], pj_ref[2, rows, lanes], pj_ref[3, rows, lanes]
            o_ = o_ref[rows, lanes]
            dy_ = dy_ref[rows, lanes]
            sg = _sigmoid(g)
            rinv = lax.rsqrt(jnp.mean(o_ * o_, axis=-1, keepdims=True) + RMS_EPS)
            nrm = o_ * rinv
            dr = dy_ * (g * sg)
            dg = dy_ * nrm * ng * (sg * (1.0 + g * (1.0 - sg)))
            dn = dr * ng
            do = rinv * (dn - nrm * jnp.mean(dn * nrm, axis=-1, keepdims=True))
            dng = _colsum(dr * nrm) if dng is None else dng + _colsum(dr * nrm)
            qs, k, logf, sig, lb, forget = _hg_gates(q, f, alb_ref.at[:, lanes])
            b = _running_sum(logf)
            qb, qt, kt, kd, ebl, eb, e_q, e_k, e_d = _hg_intra(qs, k, b, b_scr.at[j, ci])
            st = st_ref[j, ci]
            dstn = dst_scr[j]
            qt, kt, qb, kd = (t.astype(CDT).astype(F32) for t in (qt, kt, qb, kd))
            a = jnp.where(causal, _dot_nt(qt, kt), 0.0)
            da = jnp.where(causal, _dot_nt(do, v), 0.0)
            dv = _dot_tn(a, do) + _dot_nt(kd, dstn)
            dqb = _dot(do, st)
            dkd = _dot(v, dstn)
            dqt = _dot(da, kt)
            dkt = _dot_tn(da, qt)
            dbl = _colsum(dkd * kd) + ebl * _colsum(dstn * st)
            dst_scr[j] = dstn * ebl + _dot_tn(do, qb)
            dqs = dqt * e_q + dqb * eb
            dk = dkt * e_k + dkd * e_d
            db = dqt * qt + dqb * qb - dkt * kt - dkd * kd
            dlogf = _running_sum(db, reverse=True) + dbl
            dforget = dlogf / forget
            dsig = (1.0 - lb) * (dforget - dk)
            df = dsig * sig * (1.0 - sig)
            dlb = _colsum((dforget - dk) * (1.0 - sig))
            sq = _sigmoid(q)
            dq = dqs * (HG_DK ** -0.5) * (sq * (1.0 + q * (1.0 - sq)))
            dpj_ref[0, rows, lanes] = dq.astype(CDT)
            dpj_ref[1, rows, lanes] = df.astype(CDT)
            dpj_ref[2, rows, lanes] = dv.astype(CDT)
            dpj_ref[3, rows, lanes] = dg.astype(CDT)
            da0 = dlb * lb * (1.0 - lb)
            dalb_ref[pl.ds(0, 1), lanes] += da0
            dalb_ref[pl.ds(1, 1), lanes] -= da0
        dng_ref[...] += dng

    blk = pl.BlockSpec((rb, wb), lambda h, n: (nb - 1 - n, h))
    pj = pl.BlockSpec((4, rb, wb), lambda h, n: (0, nb - 1 - n, h))
    alb_blk = pl.BlockSpec((2, wb), lambda h, n: (0, h))
    ng_blk = pl.BlockSpec((1, HG_DK), lambda h, n: (0, 0))
    return _call(
        after, body, grid=(HG_H // HG_HPB, nb),
        in_specs=[pj, alb_blk, ng_blk, blk,
                  pl.BlockSpec((HG_HPB, HG_CPB, HG_DK, HG_DK), lambda h, n: (h, nb - 1 - n, 0, 0)), blk],
        out_specs=[pj, alb_blk, ng_blk],
        out_shape=[_sds((4, S, D), CDT), _sds((2, D), F32), _sds((1, HG_DK), F32)],
        scratch_shapes=[pltpu.VMEM((HG_HPB, HG_DK, HG_DK), F32), pltpu.VMEM((HG_HPB, HG_CPB, HG_CH, HG_DK), F32)],
        compiler_params=_params(("arbitrary", "arbitrary")), name="hgrn_bwd")(proj, alb, ngain, o, states, dy)


def _slope(h):
    return 2.0 ** (-8.0 * (h + 1) / ATT_QH)


def _attn_mask(n):
    qi = lax.broadcasted_iota(jnp.int32, (WINDOW, 2 * WINDOW), 0)
    si = lax.broadcasted_iota(jnp.int32, (WINDOW, 2 * WINDOW), 1)
    dist = qi - si + WINDOW
    valid = (dist >= 0) & (dist < WINDOW) & (n * WINDOW - WINDOW + si >= 0)
    return valid, dist.astype(F32)


def _attn_probs(qh, kh, sink, slope, valid, distf):
    s = _dot_nt(qh, kh) * (ATT_HD ** -0.5) - slope * distf
    s = jnp.where(valid, s, NEG)
    m = jnp.maximum(jnp.max(s, axis=-1, keepdims=True), sink)
    e = jnp.exp(s - m)
    es = jnp.exp(sink - m)
    inv = 1.0 / (jnp.sum(e, axis=-1, keepdims=True) + es)
    return e * inv, es * inv


def _attn_specs(S):
    nb = S // WINDOW
    cur = lambda H: pl.BlockSpec((H, WINDOW, ATT_HD), lambda n: (0, n, 0))
    prev = lambda H: pl.BlockSpec((H, WINDOW, ATT_HD), lambda n: (0, jnp.maximum(n - 1, 0), 0))
    return nb, cur, prev


def _attn_fwd(q4, kv4, sinks):
    S = q4.shape[1]
    nb, cur, prev = _attn_specs(S)

    def body(sink_ref, q_ref, kvc_ref, kvp_ref, o_ref):
        valid, distf = _attn_mask(pl.program_id(0))
        outs = []
        for h in range(ATT_QH):
            kvh = h // ATT_G
            kh = jnp.concatenate([kvp_ref[kvh], kvc_ref[kvh]], axis=0)
            vh = jnp.concatenate([kvp_ref[ATT_KVH + kvh], kvc_ref[ATT_KVH + kvh]], axis=0)
            p, _ = _attn_probs(q_ref[h], kh, sink_ref[0, h], _slope(h), valid, distf)
            outs.append(_dot(p, vh).astype(CDT))
            if h % 2:
                o_ref[:, pl.ds((h - 1) * ATT_HD, 2 * ATT_HD)] = _pair_lanes(outs[h - 1], outs[h])

    return _call(
        None, body, grid=(nb,),
        in_specs=[pl.BlockSpec(memory_space=pltpu.SMEM), cur(ATT_QH), cur(2 * ATT_KVH), prev(2 * ATT_KVH)],
        out_specs=pl.BlockSpec((WINDOW, D), lambda n: (n, 0)), out_shape=_sds((S, D), CDT),
        compiler_params=_params(("arbitrary",)), name="attn_fwd")(sinks, q4, kv4, kv4)


def _attn_bwd(q4, kv4, sinks, do):
    S = q4.shape[1]
    nb, cur, prev = _attn_specs(S)

    def body(sink_ref, q_ref, kvc_ref, kvp_ref, do_ref, dq_ref, dkv_ref, dbq_ref, dsink_ref):
        n = pl.program_id(0)
        first = n == 0

        @pl.when(first)
        def _():
            dkv_ref[...] = jnp.zeros_like(dkv_ref)
            dsink_ref[...] = jnp.zeros_like(dsink_ref)
            dbq_ref[...] = jnp.zeros_like(dbq_ref)

        valid, distf = _attn_mask(n)
        lane = lax.broadcasted_iota(jnp.int32, (1, 128), 1)
        rows_cur = pl.ds(pl.multiple_of(n * WINDOW, WINDOW), WINDOW)
        rows_prev = pl.ds(pl.multiple_of(jnp.maximum(n - 1, 0) * WINDOW, WINDOW), WINDOW)
        dsinks = jnp.zeros((1, 128), F32)
        sel = (_half_select(True), _half_select(False))
        for kvh in range(ATT_KVH):
            kh = jnp.concatenate([kvp_ref[kvh], kvc_ref[kvh]], axis=0)
            vh = jnp.concatenate([kvp_ref[ATT_KVH + kvh], kvc_ref[ATT_KVH + kvh]], axis=0)
            dk = dv = None
            dqs = []
            for h in range(kvh * ATT_G, (kvh + 1) * ATT_G):
                qh = q_ref[h]
                doh = jnp.dot(do_ref[:, pl.ds((h // 2) * 2 * ATT_HD, 2 * ATT_HD)], sel[h % 2],
                              preferred_element_type=F32).astype(CDT)
                p, ps = _attn_probs(qh, kh, sink_ref[0, h], _slope(h), valid, distf)
                dp = _dot_nt(doh, vh)
                dd = jnp.sum(p * dp, axis=-1, keepdims=True)
                ds = p * (dp - dd)
                dsinks = dsinks + jnp.where(lane == h, -jnp.sum(ps * dd, axis=0, keepdims=True), 0.0)
                dqh = _dot(ds, kh) * (ATT_HD ** -0.5)
                dqs.append(dqh.astype(CDT))
                dbq_ref[h] += _colsum(dqh)
                dkh = _dot_tn(ds, qh) * (ATT_HD ** -0.5)
                dvh = _dot_tn(p, doh)
                dk = dkh if dk is None else dk + dkh
                dv = dvh if dv is None else dv + dvh
            for i in range(ATT_G // 2):
                lanes = pl.ds((kvh * ATT_G + 2 * i) * ATT_HD, 2 * ATT_HD)
                dq_ref[:, lanes] = _pair_lanes(dqs[2 * i], dqs[2 * i + 1])
            dkv_ref[kvh, rows_prev, :] += dk[:WINDOW]
            dkv_ref[kvh, rows_cur, :] += dk[WINDOW:]
            dkv_ref[ATT_KVH + kvh, rows_prev, :] += dv[:WINDOW]
            dkv_ref[ATT_KVH + kvh, rows_cur, :] += dv[WINDOW:]
        dsink_ref[...] += dsinks

    return _call(
        None, body, grid=(nb,),
        in_specs=[pl.BlockSpec(memory_space=pltpu.SMEM), cur(ATT_QH), cur(2 * ATT_KVH), prev(2 * ATT_KVH),
                  pl.BlockSpec((WINDOW, D), lambda n: (n, 0))],
        out_specs=[pl.BlockSpec((WINDOW, D), lambda n: (n, 0)), pl.BlockSpec((2 * ATT_KVH, S, ATT_HD), lambda n: (0, 0, 0)),
                   pl.BlockSpec((ATT_QH, 1, ATT_HD), lambda n: (0, 0, 0)), pl.BlockSpec((1, 128), lambda n: (0, 0))],
        out_shape=[_sds((S, D), CDT), _sds((2 * ATT_KVH, S, ATT_HD), F32), _sds((ATT_QH, 1, ATT_HD), F32),
                   _sds((1, 128), F32)],
        compiler_params=_params(("arbitrary",)), name="attn_bwd")(sinks, q4, kv4, kv4, do)


def _local_step(x, p, target, getw, sm, emit):
    S = x.shape[0]
    vec = lambda a: a.reshape(1, -1)
    ln_g = lambda l, k: vec(sm["ln_gain"][l, k])
    ln_b = lambda l, k: vec(sm["ln_bias"][l, k])
    xb = x.astype(CDT)
    pb = p.astype(CDT)

    proj = _mm_nn(xb, getw("a_w_in", None), (4, S, D), (None, _tile(S), 512), lambda g, i: (g // 2, i, g % 2), F32,
                  name="a_in")
    o_a, y_a, states = _hgrn_fwd(proj, sm["a_lower_bound"], sm["a_norm_gain"])
    zeros = jnp.zeros((1, D), F32)
    z = [[None] * 3 for _ in range(2)]
    xs = [[None] * 3 for _ in range(2)]
    xbs = [[None] * 3 for _ in range(2)]
    z[0][0], xs[0][0], xbs[0][0] = _mixout_ln(y_a[None], getw("a_w_out", y_a)[None], zeros, x, ln_g(0, 0), ln_b(0, 0),
                                              "a_out_ln")
    gu, hid, sgs, ups = [None, None], [None, None], [None, None], [None, None]

    def ffn_ple(l):
        wgu = getw(f"gu{l}", xbs[l][0])
        gu[l], hid[l], z[l][1], xs[l][1], xbs[l][1] = _ffn_fwd(
            xs[l][0], xbs[l][0], wgu, getw(f"dn{l}", None), ln_g(l, 1), ln_b(l, 1), f"ffn_fwd{l}")
        sgs[l], ups[l], z[l][2], xs[l][2], xbs[l][2] = _ple_fwd(
            xs[l][1], xbs[l][1], pb[l], getw(f"pg{l}", None), vec(sm["ple_b_gate"][l]), getw(f"pu{l}", None), ln_g(l, 2),
            ln_b(l, 2), f"ple_fwd{l}")

    ffn_ple(0)
    x3, x3b = xs[0][2], xbs[0][2]
    w_kv, w_q, w_bo = getw("kv_w", x3b), getw("b_w_q", None), getw("b_w_out", None)
    kv4 = _proj_heads(x3b, w_kv, vec(sm["kv_b"]), 2 * ATT_KVH, "kv_proj")
    q4 = _proj_heads(x3b, w_q, vec(sm["b_b_q"]), ATT_QH, "q_proj")
    o_b = _attn_fwd(q4, kv4, sm["b_sinks"])
    z[1][0], xs[1][0], xbs[1][0] = _mixout_ln(o_b[None], w_bo[None], sm["b_b_out"], x3, ln_g(1, 0), ln_b(1, 0),
                                              "b_out_ln")
    ffn_ple(1)
    loss, dy = _loss_fwd_bwd(xs[1][2], target)

    gs = {}
    d_ln_g = [[None] * 3 for _ in range(2)]
    d_ln_b = [[None] * 3 for _ in range(2)]
    g_bg = [None, None]

    def ffn_ple_bwd(l, dy, after=None):
        dx2, dgl, dup, d_ln_g[l][2], d_ln_b[l][2], g_bg[l] = _ple_bwd(dy, z[l][2], sgs[l], ups[l], ln_g(l, 2),
                                                                     getw(f"pg{l}", None), f"ple_bwd{l}", after=after)
        g_pg = _wgrad(xbs[l][1][None], dgl[None], f"g_ple_gate{l}")[0]
        g_pu = _wgrad(pb[l][None], dup[None], f"g_ple_up{l}")[0]
        dx1, dzb, dgu, d_ln_g[l][1], d_ln_b[l][1] = _ffn_bwd(dx2, z[l][1], gu[l], getw(f"gu{l}", None),
                                                           getw(f"dn{l}", None), ln_g(l, 1), f"ffn_bwd{l}")
        g_dn = _wgrad(hid[l], dzb[None], f"g_ffn_down{l}")
        g_gu = _wgrad(dgu.reshape(8, S, FFN_B), xbs[l][0][None], f"g_ffn_gate_up{l}")
        return dx1, emit({f"pg{l}": g_pg, f"pu{l}": g_pu, f"dn{l}": g_dn, f"gu{l}": g_gu})

    dx1, tok = ffn_ple_bwd(1, dy)
    dz, dzb, do, d_ln_g[1][0], d_ln_b[1][0], gs["b_b_out"] = _mixout_bwd(dx1, z[1][0], ln_g(1, 0), w_bo[None], CDT,
                                                                        "b_out_bwd", after=tok)
    g_bo = _wgrad(o_b[None], dzb[None], "g_b_w_out")[0]
    dq, dkv4, dbq, dsinks = _attn_bwd(q4, kv4, sm["b_sinks"], do[0])
    gs["b_b_q"] = dbq
    gs["b_sinks"] = dsinks
    g_q = _wgrad(x3b[None], dq[None], "g_b_w_q")[0]
    dx3, dkv, gs["kv_b"] = _qkv_bwd(dz, dq, dkv4, w_q, w_kv, "qkv_bwd", after=tok)
    g_kv = _wgrad(x3b[None], dkv[None], "g_kv_w")[0]
    tok = emit({"b_w_out": g_bo, "b_w_q": g_q, "kv_w": g_kv})
    dx1, tok = ffn_ple_bwd(0, dx3, tok)
    w_ao = getw("a_w_out", None)
    dz, dzb, dyr, d_ln_g[0][0], d_ln_b[0][0], _ = _mixout_bwd(dx1, z[0][0], ln_g(0, 0), w_ao[None], F32, "a_out_bwd",
                                                              after=tok)
    g_ao = _wgrad(y_a[None], dzb[None], "g_a_w_out")[0]
    tok = emit({"a_w_out": g_ao})
    dproj, gs["a_lower_bound"], gs["a_norm_gain"] = _hgrn_bwd(proj, sm["a_lower_bound"], sm["a_norm_gain"], o_a, states,
                                                              dyr[0], after=tok)
    tk = lambda t: (None, t, D)
    g_ain = _mm_tn(xb[None], dproj, N_DEV, lambda g, k: (0, k, 0), lambda g, k: (g // 2, k, g % 2),
                   tk, lambda t: (None, t, 512), (N_DEV, D, 512), (None, D, 512), lambda g, k: (g, 0, 0), name="g_a_w_in")
    gs["ple_b_gate"] = jnp.concatenate(g_bg, axis=0)
    gs["ln_gain"] = jnp.stack([jnp.concatenate(r, axis=0) for r in d_ln_g])
    gs["ln_bias"] = jnp.stack([jnp.concatenate(r, axis=0) for r in d_ln_b])
    tok = emit({"a_w_in": g_ain}, small=gs)
    grad_x = _inproj_bwd(dz, dproj, getw("a_w_in", None), "a_in_bwd", after=tok)
    return loss, grad_x, gs


def _peer(k):
    x, y, c = lax.axis_index("x"), lax.axis_index("y"), lax.axis_index("c")
    px = 1 - x if k & 4 else x
    py = 1 - y if k & 2 else y
    pc = 1 - c if k & 1 else c
    return (px, py, pc), 4 * px + 2 * py + pc


def _my_index():
    return 4 * lax.axis_index("x") + 2 * lax.axis_index("y") + lax.axis_index("c")


def _exchange(srcs, dst_shapes, plan, name):
    n_src, n_piece = len(srcs), len(plan)

    def body(*refs):
        src_refs, dst_refs = refs[:n_src], refs[n_src:n_src + len(dst_shapes)]
        send_sems, recv_sems, local_sems = refs[n_src + len(dst_shapes):]
        me = _my_index()

        def at(ref, idx):
            return ref.at[idx] if idx else ref

        local = []
        for t, (si, sfn, di, dfn) in enumerate(plan):
            cp = pltpu.make_async_copy(at(src_refs[si], sfn(me)), at(dst_refs[di], dfn(me)), local_sems.at[t])
            cp.start()
            local.append(cp)
        sends = []
        for k in range(1, N_DEV):
            peer, pid = _peer(k)
            for t, (si, sfn, di, dfn) in enumerate(plan):
                cp = pltpu.make_async_remote_copy(
                    src_ref=at(src_refs[si], sfn(pid)), dst_ref=at(dst_refs[di], dfn(me)),
                    send_sem=send_sems.at[t * 7 + k - 1], recv_sem=recv_sems.at[t * 7 + k - 1],
                    device_id=peer, device_id_type=MESH)
                cp.start()
                sends.append(cp)
        for k in range(1, N_DEV):
            peer, pid = _peer(k)
            for t, (si, sfn, di, dfn) in enumerate(plan):
                pltpu.make_async_remote_copy(
                    src_ref=at(src_refs[si], sfn(me)), dst_ref=at(dst_refs[di], dfn(pid)),
                    send_sem=send_sems.at[t * 7 + k - 1], recv_sem=recv_sems.at[t * 7 + k - 1],
                    device_id=peer, device_id_type=MESH).wait_recv()
        for cp in sends:
            cp.wait_send()
        for cp in local:
            cp.wait()

    hbm = pl.BlockSpec(memory_space=pltpu.HBM)
    return _call(
        None, body, in_specs=[hbm] * n_src, out_specs=[hbm] * len(dst_shapes), out_shape=dst_shapes,
        scratch_shapes=[pltpu.SemaphoreType.DMA((7 * n_piece,)), pltpu.SemaphoreType.DMA((7 * n_piece,)),
                        pltpu.SemaphoreType.DMA((n_piece,))],
        name=name)(*srcs)


def _gather(shards, name):
    dsts = [_sds((N_DEV,) + a.shape, a.dtype) for a in shards]
    plan = [(i, lambda j: (), i, lambda s: (s,)) for i in range(len(shards))]
    return _exchange(shards, dsts, plan, name)


_HBM = pl.BlockSpec(memory_space=pltpu.HBM)
_SEM = pl.BlockSpec(memory_space=pltpu.SEMAPHORE)
_DATAFLOW = pltpu.SideEffectType.DATAFLOW_SIDE_EFFECTING


def _piece_copy(mode, src, land, send_sems, recv_sems, t, k, sender, receiver, peer):
    return pltpu.make_async_remote_copy(
        src_ref=src if mode == "gather" else src.at[receiver], dst_ref=land.at[sender],
        send_sem=send_sems.at[t * 7 + k - 1], recv_sem=recv_sems.at[t * 7 + k - 1], device_id=peer, device_id_type=MESH)


def _sequencer_exchange(srcs, modes, name, collective_id, after=None):
    n = len(srcs)
    land_shapes = [((N_DEV,) + a.shape) if mode == "gather" else a.shape for a, mode in zip(srcs, modes)]
    extra = [] if after is None else [after]

    def body(*refs):
        src_refs, land_refs = refs[:n], refs[n + len(extra):2 * n + len(extra)]
        send_sems, recv_sems, local_sems = refs[2 * n + len(extra):]
        barrier = pltpu.get_barrier_semaphore()
        for k in range(1, N_DEV):
            pl.semaphore_signal(barrier, inc=1, device_id=_peer(k)[0], device_id_type=MESH)
        pl.semaphore_wait(barrier, N_DEV - 1)
        me = _my_index()
        local = []
        for i in range(n):
            cp = pltpu.make_async_copy(src_refs[i] if modes[i] == "gather" else src_refs[i].at[me], land_refs[i].at[me],
                                       local_sems.at[i])
            cp.start()
            local.append(cp)
        for k in range(1, N_DEV):
            peer, pid = _peer(k)
            for t in range(n):
                _piece_copy(modes[t], src_refs[t], land_refs[t], send_sems, recv_sems, t, k, me, pid, peer).start()
        for k in range(1, N_DEV):
            peer, pid = _peer(k)
            for t in range(n):
                _piece_copy(modes[t], src_refs[t], land_refs[t], send_sems, recv_sems, t, k, pid, me, peer).wait_recv()
        for k in range(1, N_DEV):
            peer, pid = _peer(k)
            for t in range(n):
                _piece_copy(modes[t], src_refs[t], land_refs[t], send_sems, recv_sems, t, k, me, pid, peer).wait_send()
        for cp in local:
            cp.wait()

    return pl.kernel(
        body, out_type=[_sds(s, a.dtype) for s, a in zip(land_shapes, srcs)],
        mesh=plsc.ScalarSubcoreMesh(axis_name="sequencer", num_cores=1),
        scratch_types=[pltpu.SemaphoreType.DMA((7 * n,)), pltpu.SemaphoreType.DMA((7 * n,)), pltpu.SemaphoreType.DMA((n,))],
        compiler_params=pltpu.CompilerParams(collective_id=collective_id), name=name)(*srcs, *extra)


def _sequencer_gather(srcs, name, collective_id, after=None):
    n = len(srcs)
    extra = [] if after is None else [after]

    def body(*refs):
        src_refs, land_refs = refs[:n], refs[n + len(extra):2 * n + len(extra)]
        send_sems, recv_sems, local_sems = refs[2 * n + len(extra):]
        x, y, c = lax.axis_index("x"), lax.axis_index("y"), lax.axis_index("c")
        sibling = (x, y, 1 - c)
        chips = [(1 - x, y), (x, 1 - y), (1 - x, 1 - y)]
        index = lambda px, py, pc: 4 * px + 2 * py + pc
        barrier = pltpu.get_barrier_semaphore()
        for peer in [sibling] + [(*chip, c) for chip in chips]:
            pl.semaphore_signal(barrier, inc=1, device_id=peer, device_id_type=MESH)
        pl.semaphore_wait(barrier, 4)

        def copy(t, k, slot, to, src=None):
            return pltpu.make_async_remote_copy(
                src_ref=land_refs[t].at[slot] if src is None else src, dst_ref=land_refs[t].at[slot],
                send_sem=send_sems.at[7 * t + k], recv_sem=recv_sems.at[7 * t + k], device_id=to, device_id_type=MESH)

        me = index(x, y, c)
        local = []
        for t in range(n):
            cp = pltpu.make_async_copy(src_refs[t], land_refs[t].at[me], local_sems.at[t])
            cp.start()
            local.append(cp)
        sends = []
        for t in range(n):
            sends.append(copy(t, 0, me, sibling, src=src_refs[t]))
            sends += [copy(t, 1 + j, me, (*chip, c), src=src_refs[t]) for j, chip in enumerate(chips)]
        for cp in sends:
            cp.start()
        for j, chip in enumerate(chips):
            for t in range(n):
                copy(t, 1 + j, index(*chip, c), sibling, src=src_refs[t]).wait_recv()
                passed = copy(t, 4 + j, index(*chip, c), sibling)
                passed.start()
                sends.append(passed)
        for t in range(n):
            copy(t, 0, index(x, y, 1 - c), sibling, src=src_refs[t]).wait_recv()
        for j, chip in enumerate(chips):
            for t in range(n):
                copy(t, 4 + j, index(*chip, 1 - c), sibling, src=src_refs[t]).wait_recv()
        for cp in sends:
            cp.wait_send()
        for cp in local:
            cp.wait()

    return pl.kernel(
        body, out_type=[_sds((N_DEV,) + a.shape, a.dtype) for a in srcs],
        mesh=plsc.ScalarSubcoreMesh(axis_name="sequencer", num_cores=1),
        scratch_types=[pltpu.SemaphoreType.DMA((7 * n,)), pltpu.SemaphoreType.DMA((7 * n,)), pltpu.SemaphoreType.DMA((n,))],
        compiler_params=pltpu.CompilerParams(collective_id=collective_id), name=name)(*srcs, *extra)


def _xstart(groups, mode, name, after=None):
    flat = [a for g in groups for a in g]
    n, ng = len(flat), len(groups)
    land_shapes = [((N_DEV,) + a.shape) if mode == "gather" else a.shape for a in flat]
    first = [sum(len(g) for g in groups[:i]) for i in range(ng)]
    extra = [] if after is None else [after]

    def body(*refs):
        srcs, lands = refs[:n], refs[n:2 * n]
        sems = refs[2 * n + len(extra):2 * n + len(extra) + 2 * ng]
        tok_ref, local_sems = refs[-2], refs[-1]
        me = _my_index()
        local = []
        for i in range(n):
            cp = pltpu.make_async_copy(srcs[i] if mode == "gather" else srcs[i].at[me], lands[i].at[me], local_sems.at[i])
            cp.start()
            local.append(cp)
        for cp in local:
            cp.wait()
        for gi, g in enumerate(groups):
            for k in range(1, N_DEV):
                peer, pid = _peer(k)
                for t in range(len(g)):
                    i = first[gi] + t
                    _piece_copy(mode, srcs[i], lands[i], sems[2 * gi], sems[2 * gi + 1], t, k, me, pid, peer).start()
        tok_ref[...] = jnp.zeros_like(tok_ref)

    sem_shapes = []
    for g in groups:
        sem_shapes += [pltpu.SemaphoreType.DMA((7 * len(g),))] * 2
    thru = [pltpu.HBM(a.shape, a.dtype) for a in flat] + [pltpu.HBM(s, a.dtype) for s, a in zip(land_shapes, flat)]
    outs = pl.pallas_call(
        body, in_specs=[_HBM] * (2 * n) + [pl.BlockSpec(memory_space=pl.ANY)] * len(extra),
        out_specs=[_SEM] * (2 * ng) + [_HBM] * (2 * n) + [pl.BlockSpec(memory_space=pltpu.VMEM)],
        out_shape=sem_shapes + thru + [_sds((8, 128), F32)],
        input_output_aliases={i: 2 * ng + i for i in range(2 * n)},
        scratch_shapes=[pltpu.SemaphoreType.DMA((n,))],
        compiler_params=pltpu.CompilerParams(has_side_effects=_DATAFLOW), name=name)(
            *[pltpu.with_memory_space_constraint(a, pltpu.HBM) for a in flat],
            *[pltpu.with_memory_space_constraint(lax.empty(s, a.dtype), pltpu.HBM) for s, a in zip(land_shapes, flat)], *extra)
    sems, srcs_thru, lands_thru = outs[:2 * ng], outs[2 * ng:2 * ng + n], outs[2 * ng + n:2 * ng + 2 * n]
    handles = [(sems[2 * gi], sems[2 * gi + 1], srcs_thru[first[gi]:first[gi] + len(g)],
                lands_thru[first[gi]:first[gi] + len(g)]) for gi, g in enumerate(groups)]
    return handles, outs[-1]


def _xwait(handle, mode, after, name):
    send_sems, recv_sems, srcs_thru, lands_thru = handle
    n = len(srcs_thru)

    def body(*refs):
        srcs, lands, send, recv = refs[:n], refs[n:2 * n], refs[2 * n], refs[2 * n + 1]
        me = _my_index()
        for k in range(1, N_DEV):
            peer, pid = _peer(k)
            for t in range(n):
                _piece_copy(mode, srcs[t], lands[t], send, recv, t, k, pid, me, peer).wait_recv()
        for k in range(1, N_DEV):
            peer, pid = _peer(k)
            for t in range(n):
                _piece_copy(mode, srcs[t], lands[t], send, recv, t, k, me, pid, peer).wait_send()

    extra = [] if after is None else [after]
    outs = pl.pallas_call(
        body, in_specs=[_HBM] * (2 * n) + [_SEM, _SEM] + [pl.BlockSpec(memory_space=pl.ANY)] * len(extra),
        out_specs=[_HBM] * (2 * n),
        out_shape=[pltpu.HBM(a.shape, a.dtype) for a in list(srcs_thru) + list(lands_thru)],
        input_output_aliases={i: i for i in range(2 * n)},
        compiler_params=pltpu.CompilerParams(has_side_effects=_DATAFLOW), name=name)(
            *srcs_thru, *lands_thru, send_sems, recv_sems, *extra)
    return outs[n:]


def _adamw(w, g, m, v):
    m = ADAM_B1 * m + (1.0 - ADAM_B1) * g
    v = ADAM_B2 * v + (1.0 - ADAM_B2) * (g * g)
    m_hat = m / (1.0 - ADAM_B1 ** ADAM_STEP)
    v_hat = v / (1.0 - ADAM_B2 ** ADAM_STEP)
    delta = -ADAM_LR * (m_hat / (jnp.sqrt(v_hat) + ADAM_EPS) + ADAM_WD * w)
    return delta, m, v


def _adam_big(w, parts, m, v, name, after=None):
    L, R, C = w.shape
    tr = _tile(R, (256, 128, 176, 64, 32, 16))
    nr = R // tr

    def body(w_ref, *refs):
        p_refs, (m_ref, v_ref, g_ref, d_ref, mo_ref, vo_ref) = refs[:L], refs[L:]
        for l in range(L):
            @pl.when(pl.program_id(0) == l)
            def _(p_ref=p_refs[l]):
                g = p_ref[0].astype(F32)
                for s in range(1, N_DEV):
                    g = g + p_ref[s].astype(F32)
                g_ref[...] = g
                d_ref[...], mo_ref[...], vo_ref[...] = _adamw(w_ref[...], g, m_ref[...], v_ref[...])

    row = pl.BlockSpec((None, tr, C), lambda l, i: (l, i, 0))
    park = lambda l_of: (lambda l, i: (0, jnp.where(l == l_of, i, 0 if l_of else nr - 1), 0))
    return _call(
        after, body, grid=(L, nr),
        in_specs=[row] + [pl.BlockSpec((N_DEV, tr, C), park(l)) for l in range(L)] + [row, row],
        out_specs=[row] * 4, out_shape=[_sds((L, R, C), F32)] * 4,
        compiler_params=_params(("arbitrary", "arbitrary")), name=name)(w, *parts, m, v)


SMALL = (("a_lower_bound", (2, 128), (2, D)), ("ln_gain", (6, 128), (6, D)), ("ln_bias", (6, 128), (6, D)),
         ("a_norm_gain", (1, 128), (1, 128)), ("kv_b", (1, 512), (1, 512)), ("b_b_q", (1, D), (1, D)),
         ("b_sinks", (1, ATT_QH), (1, 128)), ("b_b_out", (1, D), (1, D)), ("ple_b_gate", (2, D), (2, D)))


def _adam_small(parts, w, m, v, after=None):
    k = len(SMALL)

    def body(*refs):
        p_refs, w_refs, m_refs, v_refs = refs[:k], refs[k:2 * k], refs[2 * k:3 * k], refs[3 * k:4 * k]
        outs = refs[4 * k:]
        me = _my_index()
        for i, (_, wshape, pshape) in enumerate(SMALL):
            cols = wshape[1]
            lanes = slice(None) if cols == pshape[1] else (
                pl.ds(0, cols) if cols < 128 else pl.ds(pl.multiple_of(me * cols, cols), cols))
            g = p_refs[i][0, :, lanes]
            for s in range(1, N_DEV):
                g = g + p_refs[i][s, :, lanes]
            g_ref, d_ref, mo_ref, vo_ref = outs[4 * i:4 * i + 4]
            g_ref[...] = g
            d_ref[...], mo_ref[...], vo_ref[...] = _adamw(w_refs[i][...], g, m_refs[i][...], v_refs[i][...])

    full = lambda shape: pl.BlockSpec(shape, lambda: (0,) * len(shape))
    names = [n for n, _, _ in SMALL]
    res = _call(
        after, body,
        in_specs=[full((N_DEV,) + ps) for _, _, ps in SMALL] + [full(ws) for _, ws, _ in SMALL] * 3,
        out_specs=[full(ws) for _, ws, _ in SMALL for _ in range(4)],
        out_shape=[_sds(ws, F32) for _, ws, _ in SMALL for _ in range(4)], name="adam_small")(
            *[parts[n] for n in names], *[a[n].reshape(ws) for a in (w, m, v) for n, ws, _ in SMALL])
    return {n: [r.reshape(w[n].shape) for r in res[4 * i:4 * i + 4]] for i, n in enumerate(names)}


WEIGHTS = ("a_w_in", "a_lower_bound", "a_norm_gain", "a_w_out", "kv_w", "kv_b", "b_w_q", "b_b_q", "b_sinks", "b_w_out",
           "b_b_out", "ffn_w_gate_up", "ffn_w_down", "ple_w_up", "ple_w_gate", "ple_b_gate", "ln_gain", "ln_bias")


GATHER_GROUPS = (("a_w_in",), ("a_w_out", "gu0"), ("dn0", "pu0", "pg0"), ("kv_w", "b_w_q", "b_w_out"),
                 ("gu1", "dn1", "pu1", "pg1"))
KERNEL_LAYOUT = {
    "a_w_in": lambda a: a,
    "a_w_out": lambda a: a.reshape(D, D),
    "kv_w": lambda a: a.reshape(D, 2 * ATT_KVH * ATT_HD),
    "b_w_q": lambda a: a.reshape(D, D),
    "b_w_out": lambda a: a.reshape(D, D),
    "gu": lambda a: a.reshape(2, 4, FFN_B, D),
    "dn": lambda a: a.reshape(4, FFN_B, D),
    "pu": lambda a: a,
    "pg": lambda a: a.reshape(D, D),
}
_row_blocks = lambda a: a.reshape(N_DEV, -1, a.shape[-1])
OWNER_BLOCKS = {
    "a_w_in": lambda g: g,
    "a_w_out": _row_blocks,
    "kv_w": _row_blocks,
    "b_w_q": _row_blocks,
    "b_w_out": _row_blocks,
    "gu": lambda g: g,
    "dn": lambda g: _row_blocks(g.reshape(FFN_H, D)),
    "pu": lambda g: g.reshape(PLE_DIM, N_DEV, 128).transpose(1, 0, 2),
    "pg": _row_blocks,
}
ADAM_AFTER = {1: (("kv_w", "kv_w"), ("b_w_q", "b_w_q"), ("b_w_out", "b_w_out")),
              2: (("ffn_w_gate_up", "gu"), ("ffn_w_down", "dn"), ("ple_w_up", "pu"), ("ple_w_gate", "pg")),
              3: (("a_w_out", "a_w_out"),), 4: (("a_w_in", "a_w_in"),)}


def kernel(x, p, a_w_in, a_lower_bound, a_norm_gain, a_w_out, kv_w, kv_b, b_w_q, b_b_q, b_sinks, b_w_out, b_b_out, ffn_w_gate_up, ffn_w_down, ple_w_up, ple_w_gate, ple_b_gate, ln_gain, ln_bias, loss_target, m_a_w_in, m_a_lower_bound, m_a_norm_gain, m_a_w_out, m_kv_w, m_kv_b, m_b_w_q, m_b_b_q, m_b_sinks, m_b_w_out, m_b_b_out, m_ffn_w_gate_up, m_ffn_w_down, m_ple_w_up, m_ple_w_gate, m_ple_b_gate, m_ln_gain, m_ln_bias, v_a_w_in, v_a_lower_bound, v_a_norm_gain, v_a_w_out, v_kv_w, v_kv_b, v_b_w_q, v_b_b_q, v_b_sinks, v_b_w_out, v_b_b_out, v_ffn_w_gate_up, v_ffn_w_down, v_ple_w_up, v_ple_w_gate, v_ple_b_gate, v_ln_gain, v_ln_bias):
    given = dict(locals())
    w = {n: given[n] for n in WEIGHTS}
    m = {n: given["m_" + n] for n in WEIGHTS}
    v = {n: given["v_" + n] for n in WEIGHTS}
    shards = {"a_w_in": a_w_in[0], "a_w_out": a_w_out[0], "kv_w": kv_w, "b_w_q": b_w_q[0], "b_w_out": b_w_out[0]}
    for l in range(2):
        shards.update({f"gu{l}": ffn_w_gate_up[l].T, f"dn{l}": ffn_w_down[l], f"pu{l}": ple_w_up[l], f"pg{l}": ple_w_gate[l]})
    sharded_small = [a_lower_bound, ln_gain.reshape(6, 128), ln_bias.reshape(6, 128)]
    gathered = {}
    for gi, g in enumerate(GATHER_GROUPS):
        lands = _sequencer_gather([shards[n].astype(CDT) for n in g] + (sharded_small if gi == 0 else []),
                                  f"gather{gi}", gi)
        for n, a in zip(g, lands):
            gathered[n] = KERNEL_LAYOUT[n.rstrip("01")](a)
        if gi == 0:
            alb, lng, lnb = [a.transpose(1, 0, 2).reshape(a.shape[1], D) for a in lands[len(g):]]

    def getw(key, after):
        return gathered[key]

    sm = {"a_lower_bound": alb, "ln_gain": lng.reshape(2, 3, D), "ln_bias": lnb.reshape(2, 3, D),
          "a_norm_gain": a_norm_gain, "kv_b": kv_b, "b_b_q": b_b_q[0], "b_sinks": b_sinks, "b_b_out": b_b_out,
          "ple_b_gate": ple_b_gate}

    scatters, small_parts = [], {}

    def emit(grads, small=None):
        names = list(grads)
        blocks = [OWNER_BLOCKS[n.rstrip("01")](grads[n]) for n in names]
        partials = [] if small is None else [small[n].reshape(ps) for n, _, ps in SMALL]
        lands = _sequencer_exchange(blocks + partials, ["scatter"] * len(blocks) + ["gather"] * len(partials),
                                    f"scatter{len(scatters)}", len(GATHER_GROUPS) + len(scatters))
        scatters.append(dict(zip(names, lands)))
        small_parts.update(zip([n for n, _, _ in SMALL], lands[len(blocks):]))
        return blocks

    loss, grad_x, gs = _local_step(x[0], p[:, 0], loss_target[0], getw, sm, emit)

    out, parts, last = {}, {}, [grad_x] + list(scatters[0].values())
    for i, landed in enumerate(scatters):
        parts.update(landed)
        for n, key in ADAM_AFTER.get(i, ()):
            lrc = (1,) * (3 - w[n].ndim) + w[n].shape
            layers = [parts[key]] if key in parts else [parts[key + "0"], parts[key + "1"]]
            shard = (lambda a: a.reshape(lrc).swapaxes(1, 2)) if key == "gu" else (lambda a: a.reshape(lrc))
            res = _adam_big(shard(w[n]), layers, shard(m[n]), shard(v[n]), "adam_" + n, after=last)
            out[n] = [(r.swapaxes(1, 2) if key == "gu" else r).reshape(w[n].shape) for r in res]
            last = [res[3]]
    out.update(_adam_small(small_parts, w, m, v, after=last))

    loss = lax.psum(loss[0, 0], ("x", "y", "c"))
    res = [loss, grad_x[None]]
    for i in range(4):
        res += [out[n][i] for n in WEIGHTS]
    return tuple(res)
```

```python
import jax
import jax.numpy as jnp
from jax import lax
from jax.experimental import pallas as pl
from jax.experimental.pallas import tpu as pltpu
from jax.experimental.pallas import tpu_sc as plsc

F32 = jnp.float32
CDT = jnp.bfloat16

N_DEV = 8
D = 1024
HG_H, HG_DK, HG_CH = 8, 128, 64
HG_HPB = 4
HG_CPB = 8
ATT_HD, ATT_QH, ATT_KVH, ATT_G, WINDOW = 64, 16, 4, 4, 128
FFN_H = 2816
FFN_B = FFN_H // 4
PLE_DIM = 256
ALPHA = (2.0 * 2) ** 0.25
LN_EPS = 1e-5
RMS_EPS = 1e-6
ADAM_LR, ADAM_B1, ADAM_B2, ADAM_EPS, ADAM_WD, ADAM_STEP = 0.001, 0.9, 0.999, 1e-08, 0.01, 10
ROW_TILES = (512, 256, 128, 64)
VMEM_LIMIT = 48 * 1024 * 1024
NEG = -1e30

MESH = pl.DeviceIdType.MESH


def _tile(n, cands=ROW_TILES):
    for t in cands:
        if n % t == 0:
            return t
    return n


def _sds(shape, dtype):
    return jax.ShapeDtypeStruct(tuple(shape), dtype)


def _params(sem):
    return pltpu.CompilerParams(dimension_semantics=sem, vmem_limit_bytes=VMEM_LIMIT)


def _dot(a, b):
    return jnp.dot(a.astype(CDT), b.astype(CDT), preferred_element_type=F32)


def _dot_nt(a, b):
    return lax.dot_general(a.astype(CDT), b.astype(CDT), (((1,), (1,)), ((), ())), preferred_element_type=F32)


def _dot_tn(a, b):
    return lax.dot_general(a.astype(CDT), b.astype(CDT), (((0,), (0,)), ((), ())), preferred_element_type=F32)


def _sigmoid(x):
    return jax.nn.sigmoid(x)


def _ln_fwd(z, g, b):
    mu = jnp.mean(z, axis=-1, keepdims=True)
    zc = z - mu
    var = jnp.mean(zc * zc, axis=-1, keepdims=True)
    return zc * lax.rsqrt(var + LN_EPS) * g + b


def _ln_bwd(z, g, dy):
    mu = jnp.mean(z, axis=-1, keepdims=True)
    zc = z - mu
    var = jnp.mean(zc * zc, axis=-1, keepdims=True)
    rstd = lax.rsqrt(var + LN_EPS)
    xhat = zc * rstd
    dxh = dy * g
    dz = rstd * (dxh - jnp.mean(dxh, axis=-1, keepdims=True) - xhat * jnp.mean(dxh * xhat, axis=-1, keepdims=True))
    return dz, xhat


def _colsum(x):
    return jnp.sum(x, axis=0, keepdims=True)


def _acc(ref, val, first):
    @pl.when(first)
    def _():
        ref[...] = val

    @pl.when(jnp.logical_not(first))
    def _():
        ref[...] += val


def _zero_at(ref, first):
    @pl.when(first)
    def _():
        ref[...] = jnp.zeros_like(ref)


def _call(after, body, **kw):
    after = [] if after is None else list(after)
    specs = list(kw["in_specs"])
    kw["in_specs"] = [pl.BlockSpec(memory_space=pl.ANY)] * len(after) + specs

    def ordered_body(*refs):
        body(*refs[len(after):])

    call = pl.pallas_call(ordered_body, **kw)

    def pinned(*args):
        args = [a if s.memory_space is not None else pltpu.with_memory_space_constraint(a, pltpu.HBM)
                for a, s in zip(args, specs)]
        return call(*after, *args)

    return pinned


def _mm_nn(a, b3, out_shape, oblock, omap, out_dtype, bias3=None, name="mm_nn"):
    M, K = a.shape
    G, _, Nb = b3.shape
    tm = _tile(M)

    def body(a_ref, b_ref, *rest):
        o_ref = rest[-1]
        acc = _dot(a_ref[...], b_ref[...])
        if bias3 is not None:
            acc = acc + rest[0][...]
        o_ref[...] = acc.astype(o_ref.dtype)

    in_specs = [pl.BlockSpec((tm, K), lambda g, i: (i, 0)), pl.BlockSpec((None, K, Nb), lambda g, i: (g, 0, 0))]
    args = [a, b3]
    if bias3 is not None:
        in_specs.append(pl.BlockSpec((None, 1, Nb), lambda g, i: (g, 0, 0)))
        args.append(bias3)
    return _call(
        None, body, grid=(G, M // tm), in_specs=in_specs, out_specs=pl.BlockSpec(oblock, omap),
        out_shape=_sds(out_shape, out_dtype), compiler_params=_params(("arbitrary", "arbitrary")), name=name)(*args)


def _mm_tn(a3, b3, G, amap, bmap, ablock, bblock, out_shape, oblock, omap, name="mm_tn"):
    S = a3.shape[1]

    def body(a_ref, b_ref, o_ref):
        o_ref[...] = _dot_tn(a_ref[...], b_ref[...]).astype(o_ref.dtype)

    return _call(
        None, body, grid=(G, 1),
        in_specs=[pl.BlockSpec(ablock(S), amap), pl.BlockSpec(bblock(S), bmap)],
        out_specs=pl.BlockSpec(oblock, omap), out_shape=_sds(out_shape, CDT),
        compiler_params=_params(("arbitrary", "arbitrary")), name=name)(a3, b3)


def _wgrad(a3, b3, name):
    Ga, S, M = a3.shape
    Gb, _, N = b3.shape
    G = max(Ga, Gb)
    return _mm_tn(
        a3, b3, G,
        (lambda g, k: (g, k, 0)) if Ga > 1 else (lambda g, k: (0, k, 0)),
        (lambda g, k: (g, k, 0)) if Gb > 1 else (lambda g, k: (0, k, 0)),
        lambda tk: (None, tk, M), lambda tk: (None, tk, N),
        (G, M, N), (None, M, N), lambda g, k: (g, 0, 0), name=name)


def _mixout_ln(u3, w3, bias, xin, gain, beta, name):
    G, S, Kb = u3.shape
    tm = _tile(S)

    def body(u_ref, w_ref, b_ref, x_ref, g_ref, be_ref, z_ref, xo_ref, xob_ref):
        h = b_ref[...] + _dot(u_ref[0], w_ref[0])
        for g in range(1, G):
            h = h + _dot(u_ref[g], w_ref[g])
        z = ALPHA * x_ref[...] + h
        z_ref[...] = z
        y = _ln_fwd(z, g_ref[...], be_ref[...])
        xo_ref[...] = y
        xob_ref[...] = y.astype(CDT)

    row = pl.BlockSpec((tm, D), lambda i: (i, 0))
    vec = pl.BlockSpec((1, D), lambda i: (0, 0))
    return _call(
        None, body, grid=(S // tm,),
        in_specs=[pl.BlockSpec((G, tm, Kb), lambda i: (0, i, 0)), pl.BlockSpec((G, Kb, D), lambda i: (0, 0, 0)),
                  vec, row, vec, vec],
        out_specs=[row, row, row], out_shape=[_sds((S, D), F32), _sds((S, D), F32), _sds((S, D), CDT)],
        compiler_params=_params(("arbitrary",)), name=name)(u3, w3, bias, xin, gain, beta)


def _ffn_fwd(xin, xin_b, wgu, wdn, gain, beta, name):
    S = xin.shape[0]
    tm = _tile(S)

    def hidden(xb_ref, wgu_ref, gu_ref, hid_ref):
        xb = xb_ref[...]
        gate = _dot_nt(xb, wgu_ref[0])
        up = _dot_nt(xb, wgu_ref[1])
        gu_ref[0] = gate
        gu_ref[1] = up
        hid_ref[...] = (gate * _sigmoid(gate) * up).astype(CDT)

    gu, hid = _call(
        None, hidden, grid=(S // tm, 4),
        in_specs=[pl.BlockSpec((tm, D), lambda i, j: (i, 0)), pl.BlockSpec((2, None, FFN_B, D), lambda i, j: (0, j, 0, 0))],
        out_specs=[pl.BlockSpec((2, None, tm, FFN_B), lambda i, j: (0, j, i, 0)),
                   pl.BlockSpec((None, tm, FFN_B), lambda i, j: (j, i, 0))],
        out_shape=[_sds((2, 4, S, FFN_B), F32), _sds((4, S, FFN_B), CDT)],
        compiler_params=_params(("arbitrary", "arbitrary")), name=name + "_hidden")(xin_b, wgu)

    def down(x_ref, hid_ref, wdn_ref, g_ref, be_ref, z_ref, xo_ref, xob_ref):
        z = ALPHA * x_ref[...]
        for j in range(4):
            z = z + _dot(hid_ref[j], wdn_ref[j])
        z_ref[...] = z
        y = _ln_fwd(z, g_ref[...], be_ref[...])
        xo_ref[...] = y
        xob_ref[...] = y.astype(CDT)

    row = pl.BlockSpec((tm, D), lambda i: (i, 0))
    vec = pl.BlockSpec((1, D), lambda i: (0, 0))
    z, xo, xob = _call(
        None, down, grid=(S // tm,),
        in_specs=[row, pl.BlockSpec((4, tm, FFN_B), lambda i: (0, i, 0)), pl.BlockSpec((4, FFN_B, D), lambda i: (0, 0, 0)),
                  vec, vec],
        out_specs=[row, row, row], out_shape=[_sds((S, D), F32), _sds((S, D), F32), _sds((S, D), CDT)],
        compiler_params=_params(("arbitrary",)), name=name + "_down")(xin, hid, wdn, gain, beta)
    return gu, hid, z, xo, xob


def _ple_fwd(xin, xin_b, p_b, wpg, bgate, wpu, gain, beta, name):
    S = xin.shape[0]
    tm = _tile(S)

    def body(x_ref, xb_ref, p_ref, wpg_ref, bg_ref, wpu_ref, g_ref, be_ref, sg_ref, up_ref, z_ref, xo_ref, xob_ref):
        sg = _sigmoid(_dot(xb_ref[...], wpg_ref[...]) + bg_ref[...])
        pb = p_ref[...]
        up = jnp.concatenate([_dot(pb, wpu_ref[j]) for j in range(N_DEV)], axis=-1)
        sg_ref[...] = sg
        up_ref[...] = up
        z = ALPHA * x_ref[...] + sg * up
        z_ref[...] = z
        y = _ln_fwd(z, g_ref[...], be_ref[...])
        xo_ref[...] = y
        xob_ref[...] = y.astype(CDT)

    row = pl.BlockSpec((tm, D), lambda i: (i, 0))
    vec = pl.BlockSpec((1, D), lambda i: (0, 0))
    return _call(
        None, body, grid=(S // tm,),
        in_specs=[row, row, pl.BlockSpec((tm, PLE_DIM), lambda i: (i, 0)), pl.BlockSpec((D, D), lambda i: (0, 0)), vec,
                  pl.BlockSpec((N_DEV, PLE_DIM, D // N_DEV), lambda i: (0, 0, 0)), vec, vec],
        out_specs=[row] * 5,
        out_shape=[_sds((S, D), F32)] * 4 + [_sds((S, D), CDT)],
        compiler_params=_params(("arbitrary",)), name=name)(xin, xin_b, p_b, wpg, bgate, wpu, gain, beta)


def _loss_fwd_bwd(y, target):
    S = y.shape[0]
    tm = _tile(S)

    def body(y_ref, t_ref, l_ref, dy_ref):
        e = y_ref[...] - t_ref[...]
        dy_ref[...] = e * (1.0 / D)
        part = 0.5 * jnp.sum(jnp.sum(e * e, axis=-1, keepdims=True) * (1.0 / D), axis=0, keepdims=True)
        _acc(l_ref, jnp.broadcast_to(part, l_ref.shape), pl.program_id(0) == 0)

    row = pl.BlockSpec((tm, D), lambda i: (i, 0))
    return _call(
        None, body, grid=(S // tm,), in_specs=[row, row],
        out_specs=[pl.BlockSpec((1, 128), lambda i: (0, 0)), row],
        out_shape=[_sds((1, 128), F32), _sds((S, D), F32)],
        compiler_params=_params(("arbitrary",)), name="loss")(y, target)


def _ple_bwd(dy, z, sg, up, gain, wpg, name, after=None):
    S = dy.shape[0]
    tm = _tile(S)

    def body(dy_ref, z_ref, sg_ref, up_ref, g_ref, wpg_ref, dx_ref, dgl_ref, dup_ref, dgain_ref, dbeta_ref, dbg_ref):
        first = pl.program_id(0) == 0
        dy_ = dy_ref[...]
        dz, xhat = _ln_bwd(z_ref[...], g_ref[...], dy_)
        sg_ = sg_ref[...]
        dgl = dz * up_ref[...] * sg_ * (1.0 - sg_)
        dgl_ref[...] = dgl.astype(CDT)
        dup_ref[...] = (dz * sg_).astype(CDT)
        dx_ref[...] = ALPHA * dz + _dot_nt(dgl, wpg_ref[...])
        _acc(dgain_ref, _colsum(dy_ * xhat), first)
        _acc(dbeta_ref, _colsum(dy_), first)
        _acc(dbg_ref, _colsum(dgl), first)

    row = pl.BlockSpec((tm, D), lambda i: (i, 0))
    vec = pl.BlockSpec((1, D), lambda i: (0, 0))
    return _call(
        after, body, grid=(S // tm,), in_specs=[row, row, row, row, vec, pl.BlockSpec((D, D), lambda i: (0, 0))],
        out_specs=[row, row, row, vec, vec, vec],
        out_shape=[_sds((S, D), F32), _sds((S, D), CDT), _sds((S, D), CDT)] + [_sds((1, D), F32)] * 3,
        compiler_params=_params(("arbitrary",)), name=name)(dy, z, sg, up, gain, wpg)


def _ffn_bwd(dy, z, gu, wgu, wdn, gain, name, after=None):
    S = dy.shape[0]
    tm = _tile(S)

    def hidden(dy_ref, z_ref, gu_ref, wdn_ref, g_ref, dz_ref, dzb_ref, dgu_ref, dgain_ref, dbeta_ref):
        i, j = pl.program_id(0), pl.program_id(1)

        @pl.when(j == 0)
        def _():
            dy_ = dy_ref[...]
            dz, xhat = _ln_bwd(z_ref[...], g_ref[...], dy_)
            dz_ref[...] = dz
            dzb_ref[...] = dz.astype(CDT)
            _acc(dgain_ref, _colsum(dy_ * xhat), i == 0)
            _acc(dbeta_ref, _colsum(dy_), i == 0)

        dhid = _dot_nt(dzb_ref[...], wdn_ref[...])
        gate, up = gu_ref[0], gu_ref[1]
        sg = _sigmoid(gate)
        dgu_ref[0] = (dhid * up * (sg * (1.0 + gate * (1.0 - sg)))).astype(CDT)
        dgu_ref[1] = (dhid * (gate * sg)).astype(CDT)

    row = pl.BlockSpec((tm, D), lambda i, j: (i, 0))
    vec = pl.BlockSpec((1, D), lambda i, j: (0, 0))
    dz, dzb, dgu, dgain, dbeta = _call(
        after, hidden, grid=(S // tm, 4),
        in_specs=[row, row, pl.BlockSpec((2, None, tm, FFN_B), lambda i, j: (0, j, i, 0)),
                  pl.BlockSpec((None, FFN_B, D), lambda i, j: (j, 0, 0)), vec],
        out_specs=[row, row, pl.BlockSpec((2, None, tm, FFN_B), lambda i, j: (0, j, i, 0)), vec, vec],
        out_shape=[_sds((S, D), F32), _sds((S, D), CDT), _sds((2, 4, S, FFN_B), CDT), _sds((1, D), F32),
                   _sds((1, D), F32)],
        compiler_params=_params(("arbitrary", "arbitrary")), name=name + "_hidden")(dy, z, gu, wdn, gain)

    def to_input(dz_ref, dgu_ref, wgu_ref, dx_ref):
        acc = ALPHA * dz_ref[...]
        for g in range(2):
            for j in range(4):
                acc = acc + _dot(dgu_ref[g, j], wgu_ref[g, j])
        dx_ref[...] = acc

    rows = pl.BlockSpec((tm, D), lambda i: (i, 0))
    dx = _call(
        None, to_input, grid=(S // tm,),
        in_specs=[rows, pl.BlockSpec((2, 4, tm, FFN_B), lambda i: (0, 0, i, 0)),
                  pl.BlockSpec((2, 4, FFN_B, D), lambda i: (0, 0, 0, 0))],
        out_specs=rows, out_shape=_sds((S, D), F32),
        compiler_params=_params(("arbitrary",)), name=name + "_input")(dz, dgu, wgu)
    return dx, dzb, dgu, dgain, dbeta


def _mixout_bwd(dy, z, gain, w3, du_dtype, name, after=None):
    S = dy.shape[0]
    G, Kb, _ = w3.shape
    tm = _tile(S)

    def body(dy_ref, z_ref, g_ref, w_ref, dz_ref, dzb_ref, du_ref, dgain_ref, dbeta_ref, dbias_ref):
        first = pl.program_id(0) == 0
        dy_ = dy_ref[...]
        dz, xhat = _ln_bwd(z_ref[...], g_ref[...], dy_)
        dz_ref[...] = dz
        dzb = dz.astype(CDT)
        dzb_ref[...] = dzb
        for g in range(G):
            du_ref[g] = _dot_nt(dzb, w_ref[g]).astype(du_ref.dtype)
        _acc(dgain_ref, _colsum(dy_ * xhat), first)
        _acc(dbeta_ref, _colsum(dy_), first)
        _acc(dbias_ref, _colsum(dz), first)

    row = pl.BlockSpec((tm, D), lambda i: (i, 0))
    vec = pl.BlockSpec((1, D), lambda i: (0, 0))
    return _call(
        after, body, grid=(S // tm,), in_specs=[row, row, vec, pl.BlockSpec((G, Kb, D), lambda i: (0, 0, 0))],
        out_specs=[row, row, pl.BlockSpec((G, tm, Kb), lambda i: (0, i, 0)), vec, vec, vec],
        out_shape=[_sds((S, D), F32), _sds((S, D), CDT), _sds((G, S, Kb), du_dtype)] + [_sds((1, D), F32)] * 3,
        compiler_params=_params(("arbitrary",)), name=name)(dy, z, gain, w3)


def _half_select(low):
    r = lax.broadcasted_iota(jnp.int32, (2 * ATT_HD, ATT_HD), 0)
    c = lax.broadcasted_iota(jnp.int32, (2 * ATT_HD, ATT_HD), 1)
    return (r == c + (0 if low else ATT_HD)).astype(CDT)


def _half_place(low):
    r = lax.broadcasted_iota(jnp.int32, (ATT_HD, 2 * ATT_HD), 0)
    c = lax.broadcasted_iota(jnp.int32, (ATT_HD, 2 * ATT_HD), 1)
    return (c == r + (0 if low else ATT_HD)).astype(CDT)


def _pair_lanes(even, odd):
    return (jnp.dot(even, _half_place(True), preferred_element_type=F32)
            + jnp.dot(odd, _half_place(False), preferred_element_type=F32)).astype(CDT)


def _proj_heads(a, w, bias, heads, name):
    S, K = a.shape
    N = heads * ATT_HD
    tm = _tile(S)

    def body(a_ref, w_ref, b_ref, o_ref):
        acc = (_dot(a_ref[...], w_ref[...]) + b_ref[...]).astype(CDT)
        sel = (_half_select(True), _half_select(False))
        for h in range(heads):
            pair = acc[:, (h // 2) * 2 * ATT_HD:(h // 2 + 1) * 2 * ATT_HD]
            o_ref[h] = jnp.dot(pair, sel[h % 2], preferred_element_type=F32).astype(CDT)

    return _call(
        None, body, grid=(S // tm,),
        in_specs=[pl.BlockSpec((tm, K), lambda i: (i, 0)), pl.BlockSpec((K, N), lambda i: (0, 0)),
                  pl.BlockSpec((1, N), lambda i: (0, 0))],
        out_specs=pl.BlockSpec((heads, tm, ATT_HD), lambda i: (0, i, 0)), out_shape=_sds((heads, S, ATT_HD), CDT),
        compiler_params=_params(("arbitrary",)), name=name)(a, w, bias)


def _qkv_bwd(dz, dq, dkv4, wq, wkv, name, after=None):
    S = dz.shape[0]
    tm = _tile(S)
    HK = dkv4.shape[0]
    NK = HK * ATT_HD

    def body(dz_ref, dq_ref, dkv_ref, wq_ref, wkv_ref, dx_ref, dkvn_ref, dkvb_ref):
        first = pl.program_id(0) == 0
        dkvn = jnp.concatenate([_pair_lanes(dkv_ref[2 * i].astype(CDT), dkv_ref[2 * i + 1].astype(CDT))
                                for i in range(HK // 2)], axis=-1)
        dkvn_ref[...] = dkvn
        dx_ref[...] = ALPHA * dz_ref[...] + _dot_nt(dq_ref[...], wq_ref[...]) + _dot_nt(dkvn, wkv_ref[...])
        for h in range(HK):
            _acc(dkvb_ref.at[h], _colsum(dkv_ref[h]), first)

    row = pl.BlockSpec((tm, D), lambda i: (i, 0))
    return _call(
        after, body, grid=(S // tm,),
        in_specs=[row, row, pl.BlockSpec((HK, tm, ATT_HD), lambda i: (0, i, 0)),
                  pl.BlockSpec((D, D), lambda i: (0, 0)), pl.BlockSpec((D, NK), lambda i: (0, 0))],
        out_specs=[row, pl.BlockSpec((tm, NK), lambda i: (i, 0)), pl.BlockSpec((HK, 1, ATT_HD), lambda i: (0, 0, 0))],
        out_shape=[_sds((S, D), F32), _sds((S, NK), CDT), _sds((HK, 1, ATT_HD), F32)],
        compiler_params=_params(("arbitrary",)), name=name)(dz, dq, dkv4, wq, wkv)


def _inproj_bwd(dz, dproj, wain, name, after=None):
    S = dz.shape[0]
    tm = _tile(S)
    nb = wain.shape[-1]

    def body(dz_ref, dp_ref, w_ref, dx_ref):
        acc = ALPHA * dz_ref[...]
        for j in range(N_DEV):
            acc = acc + _dot_nt(dp_ref[j // 2, :, pl.ds((j % 2) * nb, nb)], w_ref[j])
        dx_ref[...] = acc

    row = pl.BlockSpec((tm, D), lambda i: (i, 0))
    return _call(
        after, body, grid=(S // tm,),
        in_specs=[row, pl.BlockSpec((4, tm, D), lambda i: (0, i, 0)), pl.BlockSpec((N_DEV, D, nb), lambda i: (0, 0, 0))],
        out_specs=row, out_shape=_sds((S, D), F32),
        compiler_params=_params(("arbitrary",)), name=name)(dz, dproj, wain)


def _running_sum(x, reverse=False):
    rows = x.shape[0]
    row = lax.broadcasted_iota(jnp.int32, x.shape, 0)
    step = 1
    while step < rows:
        if reverse:
            x = x + jnp.where(row < rows - step, pltpu.roll(x, rows - step, 0), 0.0)
        else:
            x = x + jnp.where(row >= step, pltpu.roll(x, step, 0), 0.0)
        step *= 2
    return x


def _hg_gates(q, f, alb_ref):
    a0, a1 = alb_ref[0:1, :], alb_ref[1:2, :]
    mx = jnp.maximum(a0, a1)
    e0, e1 = jnp.exp(a0 - mx), jnp.exp(a1 - mx)
    lb = e0 / (e0 + e1)
    sig = _sigmoid(f)
    forget = lb + (1.0 - lb) * sig
    k = (1.0 - lb) * _sigmoid(-f)
    qs = q * _sigmoid(q) * (HG_DK ** -0.5)
    return qs, k, jnp.log(forget), sig, lb, forget


def _hg_intra(qs, k, b, b_scr):
    b_scr[...] = b
    bm = b_scr[pl.ds(HG_CH // 2 - 1, 1), :]
    bl = b_scr[pl.ds(HG_CH - 1, 1), :]
    eb = jnp.exp(b)
    qb = qs * eb
    e_q = jnp.exp(b - bm)
    e_k = jnp.exp(bm - b)
    e_d = jnp.exp(bl - b)
    return qb, qs * e_q, k * e_k, k * e_d, jnp.exp(bl), eb, e_q, e_k, e_d


def _hgrn_fwd(proj, alb, ngain):
    S = proj.shape[1]
    nc = S // HG_CH
    nb = nc // HG_CPB
    rb, wb = HG_CPB * HG_CH, HG_HPB * HG_DK

    def body(pj_ref, alb_ref, ng_ref, o_ref, y_ref, st_ref, st_scr, b_scr):
        n = pl.program_id(1)

        @pl.when(n == 0)
        def _():
            st_scr[...] = jnp.zeros_like(st_scr)

        r = lax.broadcasted_iota(jnp.int32, (HG_CH, HG_CH), 0)
        c = lax.broadcasted_iota(jnp.int32, (HG_CH, HG_CH), 1)
        causal = r >= c
        for ci, j in [(ci, j) for ci in range(HG_CPB) for j in range(HG_HPB)]:
            rows, lanes = pl.ds(ci * HG_CH, HG_CH), pl.ds(j * HG_DK, HG_DK)
            q, f, v, g = pj_ref[0, rows, lanes], pj_ref[1, rows, lanes], pj_ref[2, rows, lanes], pj_ref[3, rows, lanes]
            qs, k, logf, _, _, _ = _hg_gates(q, f, alb_ref.at[:, lanes])
            b = _running_sum(logf)
            qb, qt, kt, kd, ebl, _, _, _, _ = _hg_intra(qs, k, b, b_scr.at[j, ci])
            st = st_scr[j]
            st_ref[j, ci] = st
            a = jnp.where(causal, _dot_nt(qt, kt), 0.0)
            o = _dot(a, v) + _dot_nt(qb, st)
            st_scr[j] = st * ebl + _dot_tn(v, kd)
            o_ref[rows, lanes] = o
            rinv = lax.rsqrt(jnp.mean(o * o, axis=-1, keepdims=True) + RMS_EPS)
            y_ref[rows, lanes] = (o * rinv * ng_ref[...] * (g * _sigmoid(g))).astype(CDT)

    blk = pl.BlockSpec((rb, wb), lambda h, n: (n, h))
    return _call(
        None, body, grid=(HG_H // HG_HPB, nb),
        in_specs=[pl.BlockSpec((4, rb, wb), lambda h, n: (0, n, h)), pl.BlockSpec((2, wb), lambda h, n: (0, h)),
                  pl.BlockSpec((1, HG_DK), lambda h, n: (0, 0))],
        out_specs=[blk, blk, pl.BlockSpec((HG_HPB, HG_CPB, HG_DK, HG_DK), lambda h, n: (h, n, 0, 0))],
        out_shape=[_sds((S, D), F32), _sds((S, D), CDT), _sds((HG_H, nc, HG_DK, HG_DK), F32)],
        scratch_shapes=[pltpu.VMEM((HG_HPB, HG_DK, HG_DK), F32), pltpu.VMEM((HG_HPB, HG_CPB, HG_CH, HG_DK), F32)],
        compiler_params=_params(("arbitrary", "arbitrary")), name="hgrn_fwd")(proj, alb, ngain)


def _hgrn_bwd(proj, alb, ngain, o, states, dy, after=None):
    S = proj.shape[1]
    nc = S // HG_CH
    nb = nc // HG_CPB
    rb, wb = HG_CPB * HG_CH, HG_HPB * HG_DK

    def body(pj_ref, alb_ref, ng_ref, o_ref, st_ref, dy_ref, dpj_ref, dalb_ref, dng_ref, dst_scr, b_scr):
        h, n = pl.program_id(0), pl.program_id(1)

        @pl.when(n == 0)
        def _():
            dst_scr[...] = jnp.zeros_like(dst_scr)
            dalb_ref[...] = jnp.zeros_like(dalb_ref)

        _zero_at(dng_ref, jnp.logical_and(h == 0, n == 0))
        ng = ng_ref[...]
        r = lax.broadcasted_iota(jnp.int32, (HG_CH, HG_CH), 0)
        c = lax.broadcasted_iota(jnp.int32, (HG_CH, HG_CH), 1)
        causal = r >= c
        dng = None
        for ci, j in [(ci, j) for ci in reversed(range(HG_CPB)) for j in range(HG_HPB)]:
            rows, lanes = pl.ds(ci * HG_CH, HG_CH), pl.ds(j * HG_DK, HG_DK)
            q, f, v, g = pj_ref[0, rows, lanes], pj_ref[1, rows, lanes], pj_ref[2, rows, lanes], pj_ref[3, rows, lanes]
            o_ = o_ref[rows, lanes]
            dy_ = dy_ref[rows, lanes]
            sg = _sigmoid(g)
            rinv = lax.rsqrt(jnp.mean(o_ * o_, axis=-1, keepdims=True) + RMS_EPS)
            nrm = o_ * rinv
            dr = dy_ * (g * sg)
            dg = dy_ * nrm * ng * (sg * (1.0 + g * (1.0 - sg)))
            dn = dr * ng
            do = rinv * (dn - nrm * jnp.mean(dn * nrm, axis=-1, keepdims=True))
            dng = _colsum(dr * nrm) if dng is None else dng + _colsum(dr * nrm)
            qs, k, logf, sig, lb, forget = _hg_gates(q, f, alb_ref.at[:, lanes])
            b = _running_sum(logf)
            qb, qt, kt, kd, ebl, eb, e_q, e_k, e_d = _hg_intra(qs, k, b, b_scr.at[j, ci])
            st = st_ref[j, ci]
            dstn = dst_scr[j]
            qt, kt, qb, kd = (t.astype(CDT).astype(F32) for t in (qt, kt, qb, kd))
            a = jnp.where(causal, _dot_nt(qt, kt), 0.0)
            da = jnp.where(causal, _dot_nt(do, v), 0.0)
            dv = _dot_tn(a, do) + _dot_nt(kd, dstn)
            dqb = _dot(do, st)
            dkd = _dot(v, dstn)
            dqt = _dot(da, kt)
            dkt = _dot_tn(da, qt)
            dbl = _colsum(dkd * kd) + ebl * _colsum(dstn * st)
            dst_scr[j] = dstn * ebl + _dot_tn(do, qb)
            dqs = dqt * e_q + dqb * eb
            dk = dkt * e_k + dkd * e_d
            db = dqt * qt + dqb * qb - dkt * kt - dkd * kd
            dlogf = _running_sum(db, reverse=True) + dbl
            dforget = dlogf / forget
            dsig = (1.0 - lb) * (dforget - dk)
            df = dsig * sig * (1.0 - sig)
            dlb = _colsum((dforget - dk) * (1.0 - sig))
            sq = _sigmoid(q)
            dq = dqs * (HG_DK ** -0.5) * (sq * (1.0 + q * (1.0 - sq)))
            dpj_ref[0, rows, lanes] = dq.astype(CDT)
            dpj_ref[1, rows, lanes] = df.astype(CDT)
            dpj_ref[2, rows, lanes] = dv.astype(CDT)
            dpj_ref[3, rows, lanes] = dg.astype(CDT)
            da0 = dlb * lb * (1.0 - lb)
            dalb_ref[pl.ds(0, 1), lanes] += da0
            dalb_ref[pl.ds(1, 1), lanes] -= da0
        dng_ref[...] += dng

    blk = pl.BlockSpec((rb, wb), lambda h, n: (nb - 1 - n, h))
    pj = pl.BlockSpec((4, rb, wb), lambda h, n: (0, nb - 1 - n, h))
    alb_blk = pl.BlockSpec((2, wb), lambda h, n: (0, h))
    ng_blk = pl.BlockSpec((1, HG_DK), lambda h, n: (0, 0))
    return _call(
        after, body, grid=(HG_H // HG_HPB, nb),
        in_specs=[pj, alb_blk, ng_blk, blk,
                  pl.BlockSpec((HG_HPB, HG_CPB, HG_DK, HG_DK), lambda h, n: (h, nb - 1 - n, 0, 0)), blk],
        out_specs=[pj, alb_blk, ng_blk],
        out_shape=[_sds((4, S, D), CDT), _sds((2, D), F32), _sds((1, HG_DK), F32)],
        scratch_shapes=[pltpu.VMEM((HG_HPB, HG_DK, HG_DK), F32), pltpu.VMEM((HG_HPB, HG_CPB, HG_CH, HG_DK), F32)],
        compiler_params=_params(("arbitrary", "arbitrary")), name="hgrn_bwd")(proj, alb, ngain, o, states, dy)


def _slope(h):
    return 2.0 ** (-8.0 * (h + 1) / ATT_QH)


def _attn_mask(n):
    qi = lax.broadcasted_iota(jnp.int32, (WINDOW, 2 * WINDOW), 0)
    si = lax.broadcasted_iota(jnp.int32, (WINDOW, 2 * WINDOW), 1)
    dist = qi - si + WINDOW
    valid = (dist >= 0) & (dist < WINDOW) & (n * WINDOW - WINDOW + si >= 0)
    return valid, dist.astype(F32)


def _attn_probs(qh, kh, sink, slope, valid, distf):
    s = _dot_nt(qh, kh) * (ATT_HD ** -0.5) - slope * distf
    s = jnp.where(valid, s, NEG)
    m = jnp.maximum(jnp.max(s, axis=-1, keepdims=True), sink)
    e = jnp.exp(s - m)
    es = jnp.exp(sink - m)
    inv = 1.0 / (jnp.sum(e, axis=-1, keepdims=True) + es)
    return e * inv, es * inv


def _attn_specs(S):
    nb = S // WINDOW
    cur = lambda H: pl.BlockSpec((H, WINDOW, ATT_HD), lambda n: (0, n, 0))
    prev = lambda H: pl.BlockSpec((H, WINDOW, ATT_HD), lambda n: (0, jnp.maximum(n - 1, 0), 0))
    return nb, cur, prev


def _attn_fwd(q4, kv4, sinks):
    S = q4.shape[1]
    nb, cur, prev = _attn_specs(S)

    def body(sink_ref, q_ref, kvc_ref, kvp_ref, o_ref):
        valid, distf = _attn_mask(pl.program_id(0))
        outs = []
        for h in range(ATT_QH):
            kvh = h // ATT_G
            kh = jnp.concatenate([kvp_ref[kvh], kvc_ref[kvh]], axis=0)
            vh = jnp.concatenate([kvp_ref[ATT_KVH + kvh], kvc_ref[ATT_KVH + kvh]], axis=0)
            p, _ = _attn_probs(q_ref[h], kh, sink_ref[0, h], _slope(h), valid, distf)
            outs.append(_dot(p, vh).astype(CDT))
            if h % 2:
                o_ref[:, pl.ds((h - 1) * ATT_HD, 2 * ATT_HD)] = _pair_lanes(outs[h - 1], outs[h])

    return _call(
        None, body, grid=(nb,),
        in_specs=[pl.BlockSpec(memory_space=pltpu.SMEM), cur(ATT_QH), cur(2 * ATT_KVH), prev(2 * ATT_KVH)],
        out_specs=pl.BlockSpec((WINDOW, D), lambda n: (n, 0)), out_shape=_sds((S, D), CDT),
        compiler_params=_params(("arbitrary",)), name="attn_fwd")(sinks, q4, kv4, kv4)


def _attn_bwd(q4, kv4, sinks, do):
    S = q4.shape[1]
    nb, cur, prev = _attn_specs(S)

    def body(sink_ref, q_ref, kvc_ref, kvp_ref, do_ref, dq_ref, dkv_ref, dbq_ref, dsink_ref):
        n = pl.program_id(0)
        first = n == 0

        @pl.when(first)
        def _():
            dkv_ref[...] = jnp.zeros_like(dkv_ref)
            dsink_ref[...] = jnp.zeros_like(dsink_ref)
            dbq_ref[...] = jnp.zeros_like(dbq_ref)

        valid, distf = _attn_mask(n)
        lane = lax.broadcasted_iota(jnp.int32, (1, 128), 1)
        rows_cur = pl.ds(pl.multiple_of(n * WINDOW, WINDOW), WINDOW)
        rows_prev = pl.ds(pl.multiple_of(jnp.maximum(n - 1, 0) * WINDOW, WINDOW), WINDOW)
        dsinks = jnp.zeros((1, 128), F32)
        sel = (_half_select(True), _half_select(False))
        for kvh in range(ATT_KVH):
            kh = jnp.concatenate([kvp_ref[kvh], kvc_ref[kvh]], axis=0)
            vh = jnp.concatenate([kvp_ref[ATT_KVH + kvh], kvc_ref[ATT_KVH + kvh]], axis=0)
            dk = dv = None
            dqs = []
            for h in range(kvh * ATT_G, (kvh + 1) * ATT_G):
                qh = q_ref[h]
                doh = jnp.dot(do_ref[:, pl.ds((h // 2) * 2 * ATT_HD, 2 * ATT_HD)], sel[h % 2],
                              preferred_element_type=F32).astype(CDT)
                p, ps = _attn_probs(qh, kh, sink_ref[0, h], _slope(h), valid, distf)
                dp = _dot_nt(doh, vh)
                dd = jnp.sum(p * dp, axis=-1, keepdims=True)
                ds = p * (dp - dd)
                dsinks = dsinks + jnp.where(lane == h, -jnp.sum(ps * dd, axis=0, keepdims=True), 0.0)
                dqh = _dot(ds, kh) * (ATT_HD ** -0.5)
                dqs.append(dqh.astype(CDT))
                dbq_ref[h] += _colsum(dqh)
                dkh = _dot_tn(ds, qh) * (ATT_HD ** -0.5)
                dvh = _dot_tn(p, doh)
                dk = dkh if dk is None else dk + dkh
                dv = dvh if dv is None else dv + dvh
            for i in range(ATT_G // 2):
                lanes = pl.ds((kvh * ATT_G + 2 * i) * ATT_HD, 2 * ATT_HD)
                dq_ref[:, lanes] = _pair_lanes(dqs[2 * i], dqs[2 * i + 1])
            dkv_ref[kvh, rows_prev, :] += dk[:WINDOW]
            dkv_ref[kvh, rows_cur, :] += dk[WINDOW:]
            dkv_ref[ATT_KVH + kvh, rows_prev, :] += dv[:WINDOW]
            dkv_ref[ATT_KVH + kvh, rows_cur, :] += dv[WINDOW:]
        dsink_ref[...] += dsinks

    return _call(
        None, body, grid=(nb,),
        in_specs=[pl.BlockSpec(memory_space=pltpu.SMEM), cur(ATT_QH), cur(2 * ATT_KVH), prev(2 * ATT_KVH),
                  pl.BlockSpec((WINDOW, D), lambda n: (n, 0))],
        out_specs=[pl.BlockSpec((WINDOW, D), lambda n: (n, 0)), pl.BlockSpec((2 * ATT_KVH, S, ATT_HD), lambda n: (0, 0, 0)),
                   pl.BlockSpec((ATT_QH, 1, ATT_HD), lambda n: (0, 0, 0)), pl.BlockSpec((1, 128), lambda n: (0, 0))],
        out_shape=[_sds((S, D), CDT), _sds((2 * ATT_KVH, S, ATT_HD), F32), _sds((ATT_QH, 1, ATT_HD), F32),
                   _sds((1, 128), F32)],
        compiler_params=_params(("arbitrary",)), name="attn_bwd")(sinks, q4, kv4, kv4, do)


def _local_step(x, p, target, getw, sm, emit):
    S = x.shape[0]
    vec = lambda a: a.reshape(1, -1)
    ln_g = lambda l, k: vec(sm["ln_gain"][l, k])
    ln_b = lambda l, k: vec(sm["ln_bias"][l, k])
    xb = x.astype(CDT)
    pb = p.astype(CDT)

    proj = _mm_nn(xb, getw("a_w_in", None), (4, S, D), (None, _tile(S), 512), lambda g, i: (g // 2, i, g % 2), F32,
                  name="a_in")
    o_a, y_a, states = _hgrn_fwd(proj, sm["a_lower_bound"], sm["a_norm_gain"])
    zeros = jnp.zeros((1, D), F32)
    z = [[None] * 3 for _ in range(2)]
    xs = [[None] * 3 for _ in range(2)]
    xbs = [[None] * 3 for _ in range(2)]
    z[0][0], xs[0][0], xbs[0][0] = _mixout_ln(y_a[None], getw("a_w_out", y_a)[None], zeros, x, ln_g(0, 0), ln_b(0, 0),
                                              "a_out_ln")
    gu, hid, sgs, ups = [None, None], [None, None], [None, None], [None, None]

    def ffn_ple(l):
        wgu = getw(f"gu{l}", xbs[l][0])
        gu[l], hid[l], z[l][1], xs[l][1], xbs[l][1] = _ffn_fwd(
            xs[l][0], xbs[l][0], wgu, getw(f"dn{l}", None), ln_g(l, 1), ln_b(l, 1), f"ffn_fwd{l}")
        sgs[l], ups[l], z[l][2], xs[l][2], xbs[l][2] = _ple_fwd(
            xs[l][1], xbs[l][1], pb[l], getw(f"pg{l}", None), vec(sm["ple_b_gate"][l]), getw(f"pu{l}", None), ln_g(l, 2),
            ln_b(l, 2), f"ple_fwd{l}")

    ffn_ple(0)
    x3, x3b = xs[0][2], xbs[0][2]
    w_kv, w_q, w_bo = getw("kv_w", x3b), getw("b_w_q", None), getw("b_w_out", None)
    kv4 = _proj_heads(x3b, w_kv, vec(sm["kv_b"]), 2 * ATT_KVH, "kv_proj")
    q4 = _proj_heads(x3b, w_q, vec(sm["b_b_q"]), ATT_QH, "q_proj")
    o_b = _attn_fwd(q4, kv4, sm["b_sinks"])
    z[1][0], xs[1][0], xbs[1][0] = _mixout_ln(o_b[None], w_bo[None], sm["b_b_out"], x3, ln_g(1, 0), ln_b(1, 0),
                                              "b_out_ln")
    ffn_ple(1)
    loss, dy = _loss_fwd_bwd(xs[1][2], target)

    gs = {}
    d_ln_g = [[None] * 3 for _ in range(2)]
    d_ln_b = [[None] * 3 for _ in range(2)]
    g_bg = [None, None]

    def ffn_ple_bwd(l, dy, after=None):
        dx2, dgl, dup, d_ln_g[l][2], d_ln_b[l][2], g_bg[l] = _ple_bwd(dy, z[l][2], sgs[l], ups[l], ln_g(l, 2),
                                                                     getw(f"pg{l}", None), f"ple_bwd{l}", after=after)
        g_pg = _wgrad(xbs[l][1][None], dgl[None], f"g_ple_gate{l}")[0]
        g_pu = _wgrad(pb[l][None], dup[None], f"g_ple_up{l}")[0]
        dx1, dzb, dgu, d_ln_g[l][1], d_ln_b[l][1] = _ffn_bwd(dx2, z[l][1], gu[l], getw(f"gu{l}", None),
                                                           getw(f"dn{l}", None), ln_g(l, 1), f"ffn_bwd{l}")
        g_dn = _wgrad(hid[l], dzb[None], f"g_ffn_down{l}")
        g_gu = _wgrad(dgu.reshape(8, S, FFN_B), xbs[l][0][None], f"g_ffn_gate_up{l}")
        return dx1, emit({f"pg{l}": g_pg, f"pu{l}": g_pu, f"dn{l}": g_dn, f"gu{l}": g_gu})

    dx1, tok = ffn_ple_bwd(1, dy)
    dz, dzb, do, d_ln_g[1][0], d_ln_b[1][0], gs["b_b_out"] = _mixout_bwd(dx1, z[1][0], ln_g(1, 0), w_bo[None], CDT,
                                                                        "b_out_bwd", after=tok)
    g_bo = _wgrad(o_b[None], dzb[None], "g_b_w_out")[0]
    dq, dkv4, dbq, dsinks = _attn_bwd(q4, kv4, sm["b_sinks"], do[0])
    gs["b_b_q"] = dbq
    gs["b_sinks"] = dsinks
    g_q = _wgrad(x3b[None], dq[None], "g_b_w_q")[0]
    dx3, dkv, gs["kv_b"] = _qkv_bwd(dz, dq, dkv4, w_q, w_kv, "qkv_bwd", after=tok)
    g_kv = _wgrad(x3b[None], dkv[None], "g_kv_w")[0]
    tok = emit({"b_w_out": g_bo, "b_w_q": g_q, "kv_w": g_kv})
    dx1, tok = ffn_ple_bwd(0, dx3, tok)
    w_ao = getw("a_w_out", None)
    dz, dzb, dyr, d_ln_g[0][0], d_ln_b[0][0], _ = _mixout_bwd(dx1, z[0][0], ln_g(0, 0), w_ao[None], F32, "a_out_bwd",
                                                              after=tok)
    g_ao = _wgrad(y_a[None], dzb[None], "g_a_w_out")[0]
    tok = emit({"a_w_out": g_ao})
    dproj, gs["a_lower_bound"], gs["a_norm_gain"] = _hgrn_bwd(proj, sm["a_lower_bound"], sm["a_norm_gain"], o_a, states,
                                                              dyr[0], after=tok)
    tk = lambda t: (None, t, D)
    g_ain = _mm_tn(xb[None], dproj, N_DEV, lambda g, k: (0, k, 0), lambda g, k: (g // 2, k, g % 2),
                   tk, lambda t: (None, t, 512), (N_DEV, D, 512), (None, D, 512), lambda g, k: (g, 0, 0), name="g_a_w_in")
    gs["ple_b_gate"] = jnp.concatenate(g_bg, axis=0)
    gs["ln_gain"] = jnp.stack([jnp.concatenate(r, axis=0) for r in d_ln_g])
    gs["ln_bias"] = jnp.stack([jnp.concatenate(r, axis=0) for r in d_ln_b])
    gs["loss"] = loss
    tok = emit({"a_w_in": g_ain}, small=gs)
    grad_x = _inproj_bwd(dz, dproj, getw("a_w_in", None), "a_in_bwd", after=tok)
    return loss, grad_x, gs


def _peer(k):
    x, y, c = lax.axis_index("x"), lax.axis_index("y"), lax.axis_index("c")
    px = 1 - x if k & 4 else x
    py = 1 - y if k & 2 else y
    pc = 1 - c if k & 1 else c
    return (px, py, pc), 4 * px + 2 * py + pc


def _my_index():
    return 4 * lax.axis_index("x") + 2 * lax.axis_index("y") + lax.axis_index("c")


def _exchange(srcs, dst_shapes, plan, name):
    n_src, n_piece = len(srcs), len(plan)

    def body(*refs):
        src_refs, dst_refs = refs[:n_src], refs[n_src:n_src + len(dst_shapes)]
        send_sems, recv_sems, local_sems = refs[n_src + len(dst_shapes):]
        me = _my_index()

        def at(ref, idx):
            return ref.at[idx] if idx else ref

        local = []
        for t, (si, sfn, di, dfn) in enumerate(plan):
            cp = pltpu.make_async_copy(at(src_refs[si], sfn(me)), at(dst_refs[di], dfn(me)), local_sems.at[t])
            cp.start()
            local.append(cp)
        sends = []
        for k in range(1, N_DEV):
            peer, pid = _peer(k)
            for t, (si, sfn, di, dfn) in enumerate(plan):
                cp = pltpu.make_async_remote_copy(
                    src_ref=at(src_refs[si], sfn(pid)), dst_ref=at(dst_refs[di], dfn(me)),
                    send_sem=send_sems.at[t * 7 + k - 1], recv_sem=recv_sems.at[t * 7 + k - 1],
                    device_id=peer, device_id_type=MESH)
                cp.start()
                sends.append(cp)
        for k in range(1, N_DEV):
            peer, pid = _peer(k)
            for t, (si, sfn, di, dfn) in enumerate(plan):
                pltpu.make_async_remote_copy(
                    src_ref=at(src_refs[si], sfn(me)), dst_ref=at(dst_refs[di], dfn(pid)),
                    send_sem=send_sems.at[t * 7 + k - 1], recv_sem=recv_sems.at[t * 7 + k - 1],
                    device_id=peer, device_id_type=MESH).wait_recv()
        for cp in sends:
            cp.wait_send()
        for cp in local:
            cp.wait()

    hbm = pl.BlockSpec(memory_space=pltpu.HBM)
    return _call(
        None, body, in_specs=[hbm] * n_src, out_specs=[hbm] * len(dst_shapes), out_shape=dst_shapes,
        scratch_shapes=[pltpu.SemaphoreType.DMA((7 * n_piece,)), pltpu.SemaphoreType.DMA((7 * n_piece,)),
                        pltpu.SemaphoreType.DMA((n_piece,))],
        name=name)(*srcs)


def _gather(shards, name):
    dsts = [_sds((N_DEV,) + a.shape, a.dtype) for a in shards]
    plan = [(i, lambda j: (), i, lambda s: (s,)) for i in range(len(shards))]
    return _exchange(shards, dsts, plan, name)


_HBM = pl.BlockSpec(memory_space=pltpu.HBM)
_SEM = pl.BlockSpec(memory_space=pltpu.SEMAPHORE)
_DATAFLOW = pltpu.SideEffectType.DATAFLOW_SIDE_EFFECTING


def _piece_copy(mode, src, land, send_sems, recv_sems, t, k, sender, receiver, peer):
    return pltpu.make_async_remote_copy(
        src_ref=src if mode == "gather" else src.at[receiver], dst_ref=land.at[sender],
        send_sem=send_sems.at[t * 7 + k - 1], recv_sem=recv_sems.at[t * 7 + k - 1], device_id=peer, device_id_type=MESH)


def _sequencer_exchange(srcs, modes, name, collective_id, after=None):
    n = len(srcs)
    land_shapes = [((N_DEV,) + a.shape) if mode == "gather" else a.shape for a, mode in zip(srcs, modes)]
    extra = [] if after is None else [after]

    def body(*refs):
        src_refs, land_refs = refs[:n], refs[n + len(extra):2 * n + len(extra)]
        send_sems, recv_sems, local_sems = refs[2 * n + len(extra):]
        barrier = pltpu.get_barrier_semaphore()
        for k in range(1, N_DEV):
            pl.semaphore_signal(barrier, inc=1, device_id=_peer(k)[0], device_id_type=MESH)
        pl.semaphore_wait(barrier, N_DEV - 1)
        me = _my_index()
        local = []
        for i in range(n):
            cp = pltpu.make_async_copy(src_refs[i] if modes[i] == "gather" else src_refs[i].at[me], land_refs[i].at[me],
                                       local_sems.at[i])
            cp.start()
            local.append(cp)
        for k in range(1, N_DEV):
            peer, pid = _peer(k)
            for t in range(n):
                _piece_copy(modes[t], src_refs[t], land_refs[t], send_sems, recv_sems, t, k, me, pid, peer).start()
        for k in range(1, N_DEV):
            peer, pid = _peer(k)
            for t in range(n):
                _piece_copy(modes[t], src_refs[t], land_refs[t], send_sems, recv_sems, t, k, pid, me, peer).wait_recv()
        for k in range(1, N_DEV):
            peer, pid = _peer(k)
            for t in range(n):
                _piece_copy(modes[t], src_refs[t], land_refs[t], send_sems, recv_sems, t, k, me, pid, peer).wait_send()
        for cp in local:
            cp.wait()

    return pl.kernel(
        body, out_type=[_sds(s, a.dtype) for s, a in zip(land_shapes, srcs)],
        mesh=plsc.ScalarSubcoreMesh(axis_name="sequencer", num_cores=1),
        scratch_types=[pltpu.SemaphoreType.DMA((7 * n,)), pltpu.SemaphoreType.DMA((7 * n,)), pltpu.SemaphoreType.DMA((n,))],
        compiler_params=pltpu.CompilerParams(collective_id=collective_id), name=name)(*srcs, *extra)


def _sequencer_gather(srcs, name, collective_id, after=None):
    n = len(srcs)
    extra = [] if after is None else [after]

    def body(*refs):
        src_refs, land_refs = refs[:n], refs[n + len(extra):2 * n + len(extra)]
        send_sems, recv_sems, local_sems = refs[2 * n + len(extra):]
        x, y, c = lax.axis_index("x"), lax.axis_index("y"), lax.axis_index("c")
        sibling = (x, y, 1 - c)
        chips = [(1 - x, y), (x, 1 - y), (1 - x, 1 - y)]
        index = lambda px, py, pc: 4 * px + 2 * py + pc
        barrier = pltpu.get_barrier_semaphore()
        for peer in [sibling] + [(*chip, c) for chip in chips]:
            pl.semaphore_signal(barrier, inc=1, device_id=peer, device_id_type=MESH)
        pl.semaphore_wait(barrier, 4)

        def copy(t, k, slot, to, src=None):
            return pltpu.make_async_remote_copy(
                src_ref=land_refs[t].at[slot] if src is None else src, dst_ref=land_refs[t].at[slot],
                send_sem=send_sems.at[7 * t + k], recv_sem=recv_sems.at[7 * t + k], device_id=to, device_id_type=MESH)

        me = index(x, y, c)
        local = []
        for t in range(n):
            cp = pltpu.make_async_copy(src_refs[t], land_refs[t].at[me], local_sems.at[t])
            cp.start()
            local.append(cp)
        sends = []
        for t in range(n):
            sends.append(copy(t, 0, me, sibling, src=src_refs[t]))
            sends += [copy(t, 1 + j, me, (*chip, c), src=src_refs[t]) for j, chip in enumerate(chips)]
        for cp in sends:
            cp.start()
        for j, chip in enumerate(chips):
            for t in range(n):
                copy(t, 1 + j, index(*chip, c), sibling, src=src_refs[t]).wait_recv()
                passed = copy(t, 4 + j, index(*chip, c), sibling)
                passed.start()
                sends.append(passed)
        for t in range(n):
            copy(t, 0, index(x, y, 1 - c), sibling, src=src_refs[t]).wait_recv()
        for j, chip in enumerate(chips):
            for t in range(n):
                copy(t, 4 + j, index(*chip, 1 - c), sibling, src=src_refs[t]).wait_recv()
        for cp in sends:
            cp.wait_send()
        for cp in local:
            cp.wait()

    return pl.kernel(
        body, out_type=[_sds((N_DEV,) + a.shape, a.dtype) for a in srcs],
        mesh=plsc.ScalarSubcoreMesh(axis_name="sequencer", num_cores=1),
        scratch_types=[pltpu.SemaphoreType.DMA((7 * n,)), pltpu.SemaphoreType.DMA((7 * n,)), pltpu.SemaphoreType.DMA((n,))],
        compiler_params=pltpu.CompilerParams(collective_id=collective_id), name=name)(*srcs, *extra)


def _xstart(groups, mode, name, after=None):
    flat = [a for g in groups for a in g]
    n, ng = len(flat), len(groups)
    land_shapes = [((N_DEV,) + a.shape) if mode == "gather" else a.shape for a in flat]
    first = [sum(len(g) for g in groups[:i]) for i in range(ng)]
    extra = [] if after is None else [after]

    def body(*refs):
        srcs, lands = refs[:n], refs[n:2 * n]
        sems = refs[2 * n + len(extra):2 * n + len(extra) + 2 * ng]
        tok_ref, local_sems = refs[-2], refs[-1]
        me = _my_index()
        local = []
        for i in range(n):
            cp = pltpu.make_async_copy(srcs[i] if mode == "gather" else srcs[i].at[me], lands[i].at[me], local_sems.at[i])
            cp.start()
            local.append(cp)
        for cp in local:
            cp.wait()
        for gi, g in enumerate(groups):
            for k in range(1, N_DEV):
                peer, pid = _peer(k)
                for t in range(len(g)):
                    i = first[gi] + t
                    _piece_copy(mode, srcs[i], lands[i], sems[2 * gi], sems[2 * gi + 1], t, k, me, pid, peer).start()
        tok_ref[...] = jnp.zeros_like(tok_ref)

    sem_shapes = []
    for g in groups:
        sem_shapes += [pltpu.SemaphoreType.DMA((7 * len(g),))] * 2
    thru = [pltpu.HBM(a.shape, a.dtype) for a in flat] + [pltpu.HBM(s, a.dtype) for s, a in zip(land_shapes, flat)]
    outs = pl.pallas_call(
        body, in_specs=[_HBM] * (2 * n) + [pl.BlockSpec(memory_space=pl.ANY)] * len(extra),
        out_specs=[_SEM] * (2 * ng) + [_HBM] * (2 * n) + [pl.BlockSpec(memory_space=pltpu.VMEM)],
        out_shape=sem_shapes + thru + [_sds((8, 128), F32)],
        input_output_aliases={i: 2 * ng + i for i in range(2 * n)},
        scratch_shapes=[pltpu.SemaphoreType.DMA((n,))],
        compiler_params=pltpu.CompilerParams(has_side_effects=_DATAFLOW), name=name)(
            *[pltpu.with_memory_space_constraint(a, pltpu.HBM) for a in flat],
            *[pltpu.with_memory_space_constraint(lax.empty(s, a.dtype), pltpu.HBM) for s, a in zip(land_shapes, flat)], *extra)
    sems, srcs_thru, lands_thru = outs[:2 * ng], outs[2 * ng:2 * ng + n], outs[2 * ng + n:2 * ng + 2 * n]
    handles = [(sems[2 * gi], sems[2 * gi + 1], srcs_thru[first[gi]:first[gi] + len(g)],
                lands_thru[first[gi]:first[gi] + len(g)]) for gi, g in enumerate(groups)]
    return handles, outs[-1]


def _xwait(handle, mode, after, name):
    send_sems, recv_sems, srcs_thru, lands_thru = handle
    n = len(srcs_thru)

    def body(*refs):
        srcs, lands, send, recv = refs[:n], refs[n:2 * n], refs[2 * n], refs[2 * n + 1]
        me = _my_index()
        for k in range(1, N_DEV):
            peer, pid = _peer(k)
            for t in range(n):
                _piece_copy(mode, srcs[t], lands[t], send, recv, t, k, pid, me, peer).wait_recv()
        for k in range(1, N_DEV):
            peer, pid = _peer(k)
            for t in range(n):
                _piece_copy(mode, srcs[t], lands[t], send, recv, t, k, me, pid, peer).wait_send()

    extra = [] if after is None else [after]
    outs = pl.pallas_call(
        body, in_specs=[_HBM] * (2 * n) + [_SEM, _SEM] + [pl.BlockSpec(memory_space=pl.ANY)] * len(extra),
        out_specs=[_HBM] * (2 * n),
        out_shape=[pltpu.HBM(a.shape, a.dtype) for a in list(srcs_thru) + list(lands_thru)],
        input_output_aliases={i: i for i in range(2 * n)},
        compiler_params=pltpu.CompilerParams(has_side_effects=_DATAFLOW), name=name)(
            *srcs_thru, *lands_thru, send_sems, recv_sems, *extra)
    return outs[n:]


def _adamw(w, g, m, v):
    m = ADAM_B1 * m + (1.0 - ADAM_B1) * g
    v = ADAM_B2 * v + (1.0 - ADAM_B2) * (g * g)
    m_hat = m / (1.0 - ADAM_B1 ** ADAM_STEP)
    v_hat = v / (1.0 - ADAM_B2 ** ADAM_STEP)
    delta = -ADAM_LR * (m_hat / (jnp.sqrt(v_hat) + ADAM_EPS) + ADAM_WD * w)
    return delta, m, v


def _adam_big(w, parts, m, v, name, after=None):
    L, R, C = w.shape
    tr = _tile(R, (256, 128, 176, 64, 32, 16))
    nr = R // tr

    def body(w_ref, *refs):
        p_refs, (m_ref, v_ref, g_ref, d_ref, mo_ref, vo_ref) = refs[:L], refs[L:]
        for l in range(L):
            @pl.when(pl.program_id(0) == l)
            def _(p_ref=p_refs[l]):
                g = p_ref[0].astype(F32)
                for s in range(1, N_DEV):
                    g = g + p_ref[s].astype(F32)
                g_ref[...] = g
                d_ref[...], mo_ref[...], vo_ref[...] = _adamw(w_ref[...], g, m_ref[...], v_ref[...])

    row = pl.BlockSpec((None, tr, C), lambda l, i: (l, i, 0))
    park = lambda l_of: (lambda l, i: (0, jnp.where(l == l_of, i, 0 if l_of else nr - 1), 0))
    return _call(
        after, body, grid=(L, nr),
        in_specs=[row] + [pl.BlockSpec((N_DEV, tr, C), park(l)) for l in range(L)] + [row, row],
        out_specs=[row] * 4, out_shape=[_sds((L, R, C), F32)] * 4,
        compiler_params=_params(("arbitrary", "arbitrary")), name=name)(w, *parts, m, v)


SMALL = (("a_lower_bound", (2, 128), (2, D)), ("ln_gain", (6, 128), (6, D)), ("ln_bias", (6, 128), (6, D)),
         ("a_norm_gain", (1, 128), (1, 128)), ("kv_b", (1, 512), (1, 512)), ("b_b_q", (1, D), (1, D)),
         ("b_sinks", (1, ATT_QH), (1, 128)), ("b_b_out", (1, D), (1, D)), ("ple_b_gate", (2, D), (2, D)))


def _adam_small(parts, w, m, v, losses, after=None):
    k = len(SMALL)

    def body(*refs):
        p_refs, w_refs, m_refs, v_refs = refs[:k], refs[k:2 * k], refs[2 * k:3 * k], refs[3 * k:4 * k]
        loss_ref, outs, total_ref = refs[4 * k], refs[4 * k + 1:-1], refs[-1]
        total = loss_ref[0]
        for s in range(1, N_DEV):
            total = total + loss_ref[s]
        total_ref[...] = total
        me = _my_index()
        for i, (_, wshape, pshape) in enumerate(SMALL):
            cols = wshape[1]
            lanes = slice(None) if cols == pshape[1] else (
                pl.ds(0, cols) if cols < 128 else pl.ds(pl.multiple_of(me * cols, cols), cols))
            g = p_refs[i][0, :, lanes]
            for s in range(1, N_DEV):
                g = g + p_refs[i][s, :, lanes]
            g_ref, d_ref, mo_ref, vo_ref = outs[4 * i:4 * i + 4]
            g_ref[...] = g
            d_ref[...], mo_ref[...], vo_ref[...] = _adamw(w_refs[i][...], g, m_refs[i][...], v_refs[i][...])

    full = lambda shape: pl.BlockSpec(shape, lambda: (0,) * len(shape))
    names = [n for n, _, _ in SMALL]
    res = _call(
        after, body,
        in_specs=[full((N_DEV,) + ps) for _, _, ps in SMALL] + [full(ws) for _, ws, _ in SMALL] * 3
        + [full((N_DEV, 1, 128))],
        out_specs=[full(ws) for _, ws, _ in SMALL for _ in range(4)] + [full((1, 128))],
        out_shape=[_sds(ws, F32) for _, ws, _ in SMALL for _ in range(4)] + [_sds((1, 128), F32)], name="adam_small")(
            *[parts[n] for n in names], *[a[n].reshape(ws) for a in (w, m, v) for n, ws, _ in SMALL], losses)
    return {n: [r.reshape(w[n].shape) for r in res[4 * i:4 * i + 4]] for i, n in enumerate(names)}, res[-1][0, 0]


WEIGHTS = ("a_w_in", "a_lower_bound", "a_norm_gain", "a_w_out", "kv_w", "kv_b", "b_w_q", "b_b_q", "b_sinks", "b_w_out",
           "b_b_out", "ffn_w_gate_up", "ffn_w_down", "ple_w_up", "ple_w_gate", "ple_b_gate", "ln_gain", "ln_bias")


GATHER_GROUPS = (("a_w_in",), ("a_w_out", "gu0"), ("dn0", "pu0", "pg0"), ("kv_w", "b_w_q", "b_w_out"),
                 ("gu1", "dn1", "pu1", "pg1"))
KERNEL_LAYOUT = {
    "a_w_in": lambda a: a,
    "a_w_out": lambda a: a.reshape(D, D),
    "kv_w": lambda a: a.reshape(D, 2 * ATT_KVH * ATT_HD),
    "b_w_q": lambda a: a.reshape(D, D),
    "b_w_out": lambda a: a.reshape(D, D),
    "gu": lambda a: a.reshape(2, 4, FFN_B, D),
    "dn": lambda a: a.reshape(4, FFN_B, D),
    "pu": lambda a: a,
    "pg": lambda a: a.reshape(D, D),
}
_row_blocks = lambda a: a.reshape(N_DEV, -1, a.shape[-1])
OWNER_BLOCKS = {
    "a_w_in": lambda g: g,
    "a_w_out": _row_blocks,
    "kv_w": _row_blocks,
    "b_w_q": _row_blocks,
    "b_w_out": _row_blocks,
    "gu": lambda g: g,
    "dn": lambda g: _row_blocks(g.reshape(FFN_H, D)),
    "pu": lambda g: g.reshape(PLE_DIM, N_DEV, 128).transpose(1, 0, 2),
    "pg": _row_blocks,
}
ADAM_AFTER = {1: (("kv_w", "kv_w"), ("b_w_q", "b_w_q"), ("b_w_out", "b_w_out")),
              2: (("ffn_w_gate_up", "gu"), ("ffn_w_down", "dn"), ("ple_w_up", "pu"), ("ple_w_gate", "pg")),
              3: (("a_w_out", "a_w_out"),), 4: (("a_w_in", "a_w_in"),)}


def kernel(x, p, a_w_in, a_lower_bound, a_norm_gain, a_w_out, kv_w, kv_b, b_w_q, b_b_q, b_sinks, b_w_out, b_b_out, ffn_w_gate_up, ffn_w_down, ple_w_up, ple_w_gate, ple_b_gate, ln_gain, ln_bias, loss_target, m_a_w_in, m_a_lower_bound, m_a_norm_gain, m_a_w_out, m_kv_w, m_kv_b, m_b_w_q, m_b_b_q, m_b_sinks, m_b_w_out, m_b_b_out, m_ffn_w_gate_up, m_ffn_w_down, m_ple_w_up, m_ple_w_gate, m_ple_b_gate, m_ln_gain, m_ln_bias, v_a_w_in, v_a_lower_bound, v_a_norm_gain, v_a_w_out, v_kv_w, v_kv_b, v_b_w_q, v_b_b_q, v_b_sinks, v_b_w_out, v_b_b_out, v_ffn_w_gate_up, v_ffn_w_down, v_ple_w_up, v_ple_w_gate, v_ple_b_gate, v_ln_gain, v_ln_bias):
    given = dict(locals())
    w = {n: given[n] for n in WEIGHTS}
    m = {n: given["m_" + n] for n in WEIGHTS}
    v = {n: given["v_" + n] for n in WEIGHTS}
    shards = {"a_w_in": a_w_in[0], "a_w_out": a_w_out[0], "kv_w": kv_w, "b_w_q": b_w_q[0], "b_w_out": b_w_out[0]}
    for l in range(2):
        shards.update({f"gu{l}": ffn_w_gate_up[l].T, f"dn{l}": ffn_w_down[l], f"pu{l}": ple_w_up[l], f"pg{l}": ple_w_gate[l]})
    sharded_small = [a_lower_bound, ln_gain.reshape(6, 128), ln_bias.reshape(6, 128)]
    gathered = {}
    for gi, g in enumerate(GATHER_GROUPS):
        lands = _sequencer_gather([shards[n].astype(CDT) for n in g] + (sharded_small if gi == 0 else []),
                                  f"gather{gi}", gi)
        for n, a in zip(g, lands):
            gathered[n] = KERNEL_LAYOUT[n.rstrip("01")](a)
        if gi == 0:
            alb, lng, lnb = [a.transpose(1, 0, 2).reshape(a.shape[1], D) for a in lands[len(g):]]

    def getw(key, after):
        return gathered[key]

    sm = {"a_lower_bound": alb, "ln_gain": lng.reshape(2, 3, D), "ln_bias": lnb.reshape(2, 3, D),
          "a_norm_gain": a_norm_gain, "kv_b": kv_b, "b_b_q": b_b_q[0], "b_sinks": b_sinks, "b_b_out": b_b_out,
          "ple_b_gate": ple_b_gate}

    scatters, small_parts = [], {}

    def emit(grads, small=None):
        names = list(grads)
        blocks = [OWNER_BLOCKS[n.rstrip("01")](grads[n]) for n in names]
        partials = [] if small is None else [small[n].reshape(ps) for n, _, ps in SMALL] + [small["loss"]]
        lands = _sequencer_exchange(blocks + partials, ["scatter"] * len(blocks) + ["gather"] * len(partials),
                                    f"scatter{len(scatters)}", len(GATHER_GROUPS) + len(scatters))
        scatters.append(dict(zip(names, lands)))
        small_parts.update(zip([n for n, _, _ in SMALL] + ["loss"], lands[len(blocks):]))
        return blocks

    loss, grad_x, gs = _local_step(x[0], p[:, 0], loss_target[0], getw, sm, emit)

    out, parts, last = {}, {}, [grad_x] + list(scatters[0].values())
    for i, landed in enumerate(scatters):
        parts.update(landed)
        for n, key in ADAM_AFTER.get(i, ()):
            lrc = (1,) * (3 - w[n].ndim) + w[n].shape
            layers = [parts[key]] if key in parts else [parts[key + "0"], parts[key + "1"]]
            shard = (lambda a: a.reshape(lrc).swapaxes(1, 2)) if key == "gu" else (lambda a: a.reshape(lrc))
            res = _adam_big(shard(w[n]), layers, shard(m[n]), shard(v[n]), "adam_" + n, after=last)
            out[n] = [(r.swapaxes(1, 2) if key == "gu" else r).reshape(w[n].shape) for r in res]
            last = [res[3]]
    small_out, loss = _adam_small(small_parts, w, m, v, small_parts["loss"], after=last)
    out.update(small_out)
    res = [loss, grad_x[None]]
    for i in range(4):
        res += [out[n][i] for n in WEIGHTS]
    return tuple(res)
```

```python
import jax
import jax.numpy as jnp
from jax import lax
from jax.experimental import pallas as pl
from jax.experimental.pallas import tpu as pltpu
from jax.experimental.pallas import tpu_sc as plsc

F32 = jnp.float32
CDT = jnp.bfloat16

N_DEV = 8
D = 1024
HG_H, HG_DK, HG_CH = 8, 128, 64
HG_HPB = 4
HG_CPB = 8
ATT_HD, ATT_QH, ATT_KVH, ATT_G, WINDOW = 64, 16, 4, 4, 128
FFN_H = 2816
FFN_B = FFN_H // 4
PLE_DIM = 256
ALPHA = (2.0 * 2) ** 0.25
LN_EPS = 1e-5
RMS_EPS = 1e-6
ADAM_LR, ADAM_B1, ADAM_B2, ADAM_EPS, ADAM_WD, ADAM_STEP = 0.001, 0.9, 0.999, 1e-08, 0.01, 10
ROW_TILES = (512, 256, 128, 64)
VMEM_LIMIT = 48 * 1024 * 1024
NEG = -1e30

MESH = pl.DeviceIdType.MESH


def _tile(n, cands=ROW_TILES):
    for t in cands:
        if n % t == 0:
            return t
    return n


def _sds(shape, dtype):
    return jax.ShapeDtypeStruct(tuple(shape), dtype)


def _params(sem):
    return pltpu.CompilerParams(dimension_semantics=sem, vmem_limit_bytes=VMEM_LIMIT)


def _dot(a, b):
    return jnp.dot(a.astype(CDT), b.astype(CDT), preferred_element_type=F32)


def _dot_nt(a, b):
    return lax.dot_general(a.astype(CDT), b.astype(CDT), (((1,), (1,)), ((), ())), preferred_element_type=F32)


def _dot_tn(a, b):
    return lax.dot_general(a.astype(CDT), b.astype(CDT), (((0,), (0,)), ((), ())), preferred_element_type=F32)


def _sigmoid(x):
    return jax.nn.sigmoid(x)


def _ln_fwd(z, g, b):
    mu = jnp.mean(z, axis=-1, keepdims=True)
    zc = z - mu
    var = jnp.mean(zc * zc, axis=-1, keepdims=True)
    return zc * lax.rsqrt(var + LN_EPS) * g + b


def _ln_bwd(z, g, dy):
    mu = jnp.mean(z, axis=-1, keepdims=True)
    zc = z - mu
    var = jnp.mean(zc * zc, axis=-1, keepdims=True)
    rstd = lax.rsqrt(var + LN_EPS)
    xhat = zc * rstd
    dxh = dy * g
    dz = rstd * (dxh - jnp.mean(dxh, axis=-1, keepdims=True) - xhat * jnp.mean(dxh * xhat, axis=-1, keepdims=True))
    return dz, xhat


def _colsum(x):
    return jnp.sum(x, axis=0, keepdims=True)


def _acc(ref, val, first):
    @pl.when(first)
    def _():
        ref[...] = val

    @pl.when(jnp.logical_not(first))
    def _():
        ref[...] += val


def _zero_at(ref, first):
    @pl.when(first)
    def _():
        ref[...] = jnp.zeros_like(ref)


def _call(after, body, **kw):
    after = [] if after is None else list(after)
    specs = list(kw["in_specs"])
    kw["in_specs"] = [pl.BlockSpec(memory_space=pl.ANY)] * len(after) + specs

    def ordered_body(*refs):
        body(*refs[len(after):])

    call = pl.pallas_call(ordered_body, **kw)

    def pinned(*args):
        args = [a if s.memory_space is not None else pltpu.with_memory_space_constraint(a, pltpu.HBM)
                for a, s in zip(args, specs)]
        return call(*after, *args)

    return pinned


def _mm_nn(a, b3, out_shape, oblock, omap, out_dtype, bias3=None, name="mm_nn"):
    M, K = a.shape
    G, _, Nb = b3.shape
    tm = _tile(M)

    def body(a_ref, b_ref, *rest):
        o_ref = rest[-1]
        acc = _dot(a_ref[...], b_ref[...])
        if bias3 is not None:
            acc = acc + rest[0][...]
        o_ref[...] = acc.astype(o_ref.dtype)

    in_specs = [pl.BlockSpec((tm, K), lambda g, i: (i, 0)), pl.BlockSpec((None, K, Nb), lambda g, i: (g, 0, 0))]
    args = [a, b3]
    if bias3 is not None:
        in_specs.append(pl.BlockSpec((None, 1, Nb), lambda g, i: (g, 0, 0)))
        args.append(bias3)
    return _call(
        None, body, grid=(G, M // tm), in_specs=in_specs, out_specs=pl.BlockSpec(oblock, omap),
        out_shape=_sds(out_shape, out_dtype), compiler_params=_params(("arbitrary", "arbitrary")), name=name)(*args)


def _mm_tn(a3, b3, G, amap, bmap, ablock, bblock, out_shape, oblock, omap, name="mm_tn"):
    S = a3.shape[1]

    def body(a_ref, b_ref, o_ref):
        o_ref[...] = _dot_tn(a_ref[...], b_ref[...]).astype(o_ref.dtype)

    return _call(
        None, body, grid=(G, 1),
        in_specs=[pl.BlockSpec(ablock(S), amap), pl.BlockSpec(bblock(S), bmap)],
        out_specs=pl.BlockSpec(oblock, omap), out_shape=_sds(out_shape, CDT),
        compiler_params=_params(("arbitrary", "arbitrary")), name=name)(a3, b3)


def _wgrad(a3, b3, name):
    Ga, S, M = a3.shape
    Gb, _, N = b3.shape
    G = max(Ga, Gb)
    return _mm_tn(
        a3, b3, G,
        (lambda g, k: (g, k, 0)) if Ga > 1 else (lambda g, k: (0, k, 0)),
        (lambda g, k: (g, k, 0)) if Gb > 1 else (lambda g, k: (0, k, 0)),
        lambda tk: (None, tk, M), lambda tk: (None, tk, N),
        (G, M, N), (None, M, N), lambda g, k: (g, 0, 0), name=name)


def _mixout_ln(u3, w3, bias, xin, gain, beta, name):
    G, S, Kb = u3.shape
    tm = _tile(S)

    def body(u_ref, w_ref, b_ref, x_ref, g_ref, be_ref, z_ref, xo_ref, xob_ref):
        h = b_ref[...] + _dot(u_ref[0], w_ref[0])
        for g in range(1, G):
            h = h + _dot(u_ref[g], w_ref[g])
        z = ALPHA * x_ref[...] + h
        z_ref[...] = z
        y = _ln_fwd(z, g_ref[...], be_ref[...])
        xo_ref[...] = y
        xob_ref[...] = y.astype(CDT)

    row = pl.BlockSpec((tm, D), lambda i: (i, 0))
    vec = pl.BlockSpec((1, D), lambda i: (0, 0))
    return _call(
        None, body, grid=(S // tm,),
        in_specs=[pl.BlockSpec((G, tm, Kb), lambda i: (0, i, 0)), pl.BlockSpec((G, Kb, D), lambda i: (0, 0, 0)),
                  vec, row, vec, vec],
        out_specs=[row, row, row], out_shape=[_sds((S, D), F32), _sds((S, D), F32), _sds((S, D), CDT)],
        compiler_params=_params(("arbitrary",)), name=name)(u3, w3, bias, xin, gain, beta)


def _ffn_fwd(xin, xin_b, wgu, wdn, gain, beta, name):
    S = xin.shape[0]
    tm = _tile(S)

    def hidden(xb_ref, wgu_ref, gu_ref, hid_ref):
        xb = xb_ref[...]
        gate = _dot_nt(xb, wgu_ref[0])
        up = _dot_nt(xb, wgu_ref[1])
        gu_ref[0] = gate
        gu_ref[1] = up
        hid_ref[...] = (gate * _sigmoid(gate) * up).astype(CDT)

    gu, hid = _call(
        None, hidden, grid=(S // tm, 4),
        in_specs=[pl.BlockSpec((tm, D), lambda i, j: (i, 0)), pl.BlockSpec((2, None, FFN_B, D), lambda i, j: (0, j, 0, 0))],
        out_specs=[pl.BlockSpec((2, None, tm, FFN_B), lambda i, j: (0, j, i, 0)),
                   pl.BlockSpec((None, tm, FFN_B), lambda i, j: (j, i, 0))],
        out_shape=[_sds((2, 4, S, FFN_B), F32), _sds((4, S, FFN_B), CDT)],
        compiler_params=_params(("arbitrary", "arbitrary")), name=name + "_hidden")(xin_b, wgu)

    def down(x_ref, hid_ref, wdn_ref, g_ref, be_ref, z_ref, xo_ref, xob_ref):
        z = ALPHA * x_ref[...]
        for j in range(4):
            z = z + _dot(hid_ref[j], wdn_ref[j])
        z_ref[...] = z
        y = _ln_fwd(z, g_ref[...], be_ref[...])
        xo_ref[...] = y
        xob_ref[...] = y.astype(CDT)

    row = pl.BlockSpec((tm, D), lambda i: (i, 0))
    vec = pl.BlockSpec((1, D), lambda i: (0, 0))
    z, xo, xob = _call(
        None, down, grid=(S // tm,),
        in_specs=[row, pl.BlockSpec((4, tm, FFN_B), lambda i: (0, i, 0)), pl.BlockSpec((4, FFN_B, D), lambda i: (0, 0, 0)),
                  vec, vec],
        out_specs=[row, row, row], out_shape=[_sds((S, D), F32), _sds((S, D), F32), _sds((S, D), CDT)],
        compiler_params=_params(("arbitrary",)), name=name + "_down")(xin, hid, wdn, gain, beta)
    return gu, hid, z, xo, xob


def _ple_fwd(xin, xin_b, p_b, wpg, bgate, wpu, gain, beta, name):
    S = xin.shape[0]
    tm = _tile(S)

    def body(x_ref, xb_ref, p_ref, wpg_ref, bg_ref, wpu_ref, g_ref, be_ref, sg_ref, up_ref, z_ref, xo_ref, xob_ref):
        sg = _sigmoid(_dot(xb_ref[...], wpg_ref[...]) + bg_ref[...])
        pb = p_ref[...]
        up = jnp.concatenate([_dot(pb, wpu_ref[j]) for j in range(N_DEV)], axis=-1)
        sg_ref[...] = sg
        up_ref[...] = up
        z = ALPHA * x_ref[...] + sg * up
        z_ref[...] = z
        y = _ln_fwd(z, g_ref[...], be_ref[...])
        xo_ref[...] = y
        xob_ref[...] = y.astype(CDT)

    row = pl.BlockSpec((tm, D), lambda i: (i, 0))
    vec = pl.BlockSpec((1, D), lambda i: (0, 0))
    return _call(
        None, body, grid=(S // tm,),
        in_specs=[row, row, pl.BlockSpec((tm, PLE_DIM), lambda i: (i, 0)), pl.BlockSpec((D, D), lambda i: (0, 0)), vec,
                  pl.BlockSpec((N_DEV, PLE_DIM, D // N_DEV), lambda i: (0, 0, 0)), vec, vec],
        out_specs=[row] * 5,
        out_shape=[_sds((S, D), F32)] * 4 + [_sds((S, D), CDT)],
        compiler_params=_params(("arbitrary",)), name=name)(xin, xin_b, p_b, wpg, bgate, wpu, gain, beta)


def _loss_fwd_bwd(y, target):
    S = y.shape[0]
    tm = _tile(S)

    def body(y_ref, t_ref, l_ref, dy_ref):
        e = y_ref[...] - t_ref[...]
        dy_ref[...] = e * (1.0 / D)
        part = 0.5 * jnp.sum(jnp.sum(e * e, axis=-1, keepdims=True) * (1.0 / D), axis=0, keepdims=True)
        _acc(l_ref, jnp.broadcast_to(part, l_ref.shape), pl.program_id(0) == 0)

    row = pl.BlockSpec((tm, D), lambda i: (i, 0))
    return _call(
        None, body, grid=(S // tm,), in_specs=[row, row],
        out_specs=[pl.BlockSpec((1, 128), lambda i: (0, 0)), row],
        out_shape=[_sds((1, 128), F32), _sds((S, D), F32)],
        compiler_params=_params(("arbitrary",)), name="loss")(y, target)


def _ple_bwd(dy, z, sg, up, gain, wpg, name, after=None):
    S = dy.shape[0]
    tm = _tile(S)

    def body(dy_ref, z_ref, sg_ref, up_ref, g_ref, wpg_ref, dx_ref, dgl_ref, dup_ref, dgain_ref, dbeta_ref, dbg_ref):
        first = pl.program_id(0) == 0
        dy_ = dy_ref[...]
        dz, xhat = _ln_bwd(z_ref[...], g_ref[...], dy_)
        sg_ = sg_ref[...]
        dgl = dz * up_ref[...] * sg_ * (1.0 - sg_)
        dgl_ref[...] = dgl.astype(CDT)
        dup_ref[...] = (dz * sg_).astype(CDT)
        dx_ref[...] = ALPHA * dz + _dot_nt(dgl, wpg_ref[...])
        _acc(dgain_ref, _colsum(dy_ * xhat), first)
        _acc(dbeta_ref, _colsum(dy_), first)
        _acc(dbg_ref, _colsum(dgl), first)

    row = pl.BlockSpec((tm, D), lambda i: (i, 0))
    vec = pl.BlockSpec((1, D), lambda i: (0, 0))
    return _call(
        after, body, grid=(S // tm,), in_specs=[row, row, row, row, vec, pl.BlockSpec((D, D), lambda i: (0, 0))],
        out_specs=[row, row, row, vec, vec, vec],
        out_shape=[_sds((S, D), F32), _sds((S, D), CDT), _sds((S, D), CDT)] + [_sds((1, D), F32)] * 3,
        compiler_params=_params(("arbitrary",)), name=name)(dy, z, sg, up, gain, wpg)


def _ffn_bwd(dy, z, gu, wgu, wdn, gain, name, after=None):
    S = dy.shape[0]
    tm = _tile(S)

    def hidden(dy_ref, z_ref, gu_ref, wdn_ref, g_ref, dz_ref, dzb_ref, dgu_ref, dgain_ref, dbeta_ref):
        i, j = pl.program_id(0), pl.program_id(1)

        @pl.when(j == 0)
        def _():
            dy_ = dy_ref[...]
            dz, xhat = _ln_bwd(z_ref[...], g_ref[...], dy_)
            dz_ref[...] = dz
            dzb_ref[...] = dz.astype(CDT)
            _acc(dgain_ref, _colsum(dy_ * xhat), i == 0)
            _acc(dbeta_ref, _colsum(dy_), i == 0)

        dhid = _dot_nt(dzb_ref[...], wdn_ref[...])
        gate, up = gu_ref[0], gu_ref[1]
        sg = _sigmoid(gate)
        dgu_ref[0] = (dhid * up * (sg * (1.0 + gate * (1.0 - sg)))).astype(CDT)
        dgu_ref[1] = (dhid * (gate * sg)).astype(CDT)

    row = pl.BlockSpec((tm, D), lambda i, j: (i, 0))
    vec = pl.BlockSpec((1, D), lambda i, j: (0, 0))
    dz, dzb, dgu, dgain, dbeta = _call(
        after, hidden, grid=(S // tm, 4),
        in_specs=[row, row, pl.BlockSpec((2, None, tm, FFN_B), lambda i, j: (0, j, i, 0)),
                  pl.BlockSpec((None, FFN_B, D), lambda i, j: (j, 0, 0)), vec],
        out_specs=[row, row, pl.BlockSpec((2, None, tm, FFN_B), lambda i, j: (0, j, i, 0)), vec, vec],
        out_shape=[_sds((S, D), F32), _sds((S, D), CDT), _sds((2, 4, S, FFN_B), CDT), _sds((1, D), F32),
                   _sds((1, D), F32)],
        compiler_params=_params(("arbitrary", "arbitrary")), name=name + "_hidden")(dy, z, gu, wdn, gain)

    def to_input(dz_ref, dgu_ref, wgu_ref, dx_ref):
        acc = ALPHA * dz_ref[...]
        for g in range(2):
            for j in range(4):
                acc = acc + _dot(dgu_ref[g, j], wgu_ref[g, j])
        dx_ref[...] = acc

    rows = pl.BlockSpec((tm, D), lambda i: (i, 0))
    dx = _call(
        None, to_input, grid=(S // tm,),
        in_specs=[rows, pl.BlockSpec((2, 4, tm, FFN_B), lambda i: (0, 0, i, 0)),
                  pl.BlockSpec((2, 4, FFN_B, D), lambda i: (0, 0, 0, 0))],
        out_specs=rows, out_shape=_sds((S, D), F32),
        compiler_params=_params(("arbitrary",)), name=name + "_input")(dz, dgu, wgu)
    return dx, dzb, dgu, dgain, dbeta


def _mixout_bwd(dy, z, gain, w3, du_dtype, name, after=None):
    S = dy.shape[0]
    G, Kb, _ = w3.shape
    tm = _tile(S)

    def body(dy_ref, z_ref, g_ref, w_ref, dz_ref, dzb_ref, du_ref, dgain_ref, dbeta_ref, dbias_ref):
        first = pl.program_id(0) == 0
        dy_ = dy_ref[...]
        dz, xhat = _ln_bwd(z_ref[...], g_ref[...], dy_)
        dz_ref[...] = dz
        dzb = dz.astype(CDT)
        dzb_ref[...] = dzb
        for g in range(G):
            du_ref[g] = _dot_nt(dzb, w_ref[g]).astype(du_ref.dtype)
        _acc(dgain_ref, _colsum(dy_ * xhat), first)
        _acc(dbeta_ref, _colsum(dy_), first)
        _acc(dbias_ref, _colsum(dz), first)

    row = pl.BlockSpec((tm, D), lambda i: (i, 0))
    vec = pl.BlockSpec((1, D), lambda i: (0, 0))
    return _call(
        after, body, grid=(S // tm,), in_specs=[row, row, vec, pl.BlockSpec((G, Kb, D), lambda i: (0, 0, 0))],
        out_specs=[row, row, pl.BlockSpec((G, tm, Kb), lambda i: (0, i, 0)), vec, vec, vec],
        out_shape=[_sds((S, D), F32), _sds((S, D), CDT), _sds((G, S, Kb), du_dtype)] + [_sds((1, D), F32)] * 3,
        compiler_params=_params(("arbitrary",)), name=name)(dy, z, gain, w3)


def _half_select(low):
    r = lax.broadcasted_iota(jnp.int32, (2 * ATT_HD, ATT_HD), 0)
    c = lax.broadcasted_iota(jnp.int32, (2 * ATT_HD, ATT_HD), 1)
    return (r == c + (0 if low else ATT_HD)).astype(CDT)


def _half_place(low):
    r = lax.broadcasted_iota(jnp.int32, (ATT_HD, 2 * ATT_HD), 0)
    c = lax.broadcasted_iota(jnp.int32, (ATT_HD, 2 * ATT_HD), 1)
    return (c == r + (0 if low else ATT_HD)).astype(CDT)


def _pair_lanes(even, odd):
    return (jnp.dot(even, _half_place(True), preferred_element_type=F32)
            + jnp.dot(odd, _half_place(False), preferred_element_type=F32)).astype(CDT)


def _proj_heads(a, w, bias, heads, name):
    S, K = a.shape
    N = heads * ATT_HD
    tm = _tile(S)

    def body(a_ref, w_ref, b_ref, o_ref):
        acc = (_dot(a_ref[...], w_ref[...]) + b_ref[...]).astype(CDT)
        sel = (_half_select(True), _half_select(False))
        for h in range(heads):
            pair = acc[:, (h // 2) * 2 * ATT_HD:(h // 2 + 1) * 2 * ATT_HD]
            o_ref[h] = jnp.dot(pair, sel[h % 2], preferred_element_type=F32).astype(CDT)

    return _call(
        None, body, grid=(S // tm,),
        in_specs=[pl.BlockSpec((tm, K), lambda i: (i, 0)), pl.BlockSpec((K, N), lambda i: (0, 0)),
                  pl.BlockSpec((1, N), lambda i: (0, 0))],
        out_specs=pl.BlockSpec((heads, tm, ATT_HD), lambda i: (0, i, 0)), out_shape=_sds((heads, S, ATT_HD), CDT),
        compiler_params=_params(("arbitrary",)), name=name)(a, w, bias)


def _qkv_bwd(dz, dq, dkv4, wq, wkv, name, after=None):
    S = dz.shape[0]
    tm = _tile(S)
    HK = dkv4.shape[0]
    NK = HK * ATT_HD

    def body(dz_ref, dq_ref, dkv_ref, wq_ref, wkv_ref, dx_ref, dkvn_ref, dkvb_ref):
        first = pl.program_id(0) == 0
        dkvn = jnp.concatenate([_pair_lanes(dkv_ref[2 * i].astype(CDT), dkv_ref[2 * i + 1].astype(CDT))
                                for i in range(HK // 2)], axis=-1)
        dkvn_ref[...] = dkvn
        dx_ref[...] = ALPHA * dz_ref[...] + _dot_nt(dq_ref[...], wq_ref[...]) + _dot_nt(dkvn, wkv_ref[...])
        for h in range(HK):
            _acc(dkvb_ref.at[h], _colsum(dkv_ref[h]), first)

    row = pl.BlockSpec((tm, D), lambda i: (i, 0))
    return _call(
        after, body, grid=(S // tm,),
        in_specs=[row, row, pl.BlockSpec((HK, tm, ATT_HD), lambda i: (0, i, 0)),
                  pl.BlockSpec((D, D), lambda i: (0, 0)), pl.BlockSpec((D, NK), lambda i: (0, 0))],
        out_specs=[row, pl.BlockSpec((tm, NK), lambda i: (i, 0)), pl.BlockSpec((HK, 1, ATT_HD), lambda i: (0, 0, 0))],
        out_shape=[_sds((S, D), F32), _sds((S, NK), CDT), _sds((HK, 1, ATT_HD), F32)],
        compiler_params=_params(("arbitrary",)), name=name)(dz, dq, dkv4, wq, wkv)


def _inproj_bwd(dz, dproj, wain, name, after=None):
    S = dz.shape[0]
    tm = _tile(S)
    nb = wain.shape[-1]

    def body(dz_ref, dp_ref, w_ref, dx_ref):
        acc = ALPHA * dz_ref[...]
        for j in range(N_DEV):
            acc = acc + _dot_nt(dp_ref[j // 2, :, pl.ds((j % 2) * nb, nb)], w_ref[j])
        dx_ref[...] = acc

    row = pl.BlockSpec((tm, D), lambda i: (i, 0))
    return _call(
        after, body, grid=(S // tm,),
        in_specs=[row, pl.BlockSpec((4, tm, D), lambda i: (0, i, 0)), pl.BlockSpec((N_DEV, D, nb), lambda i: (0, 0, 0))],
        out_specs=row, out_shape=_sds((S, D), F32),
        compiler_params=_params(("arbitrary",)), name=name)(dz, dproj, wain)


def _running_sum(x, reverse=False):
    rows = x.shape[0]
    row = lax.broadcasted_iota(jnp.int32, x.shape, 0)
    step = 1
    while step < rows:
        if reverse:
            x = x + jnp.where(row < rows - step, pltpu.roll(x, rows - step, 0), 0.0)
        else:
            x = x + jnp.where(row >= step, pltpu.roll(x, step, 0), 0.0)
        step *= 2
    return x


def _hg_gates(q, f, alb_ref):
    a0, a1 = alb_ref[0:1, :], alb_ref[1:2, :]
    mx = jnp.maximum(a0, a1)
    e0, e1 = jnp.exp(a0 - mx), jnp.exp(a1 - mx)
    lb = e0 / (e0 + e1)
    sig = _sigmoid(f)
    forget = lb + (1.0 - lb) * sig
    k = (1.0 - lb) * _sigmoid(-f)
    qs = q * _sigmoid(q) * (HG_DK ** -0.5)
    return qs, k, jnp.log(forget), sig, lb, forget


def _hg_intra(qs, k, b, b_scr):
    b_scr[...] = b
    bm = b_scr[pl.ds(HG_CH // 2 - 1, 1), :]
    bl = b_scr[pl.ds(HG_CH - 1, 1), :]
    eb = jnp.exp(b)
    qb = qs * eb
    e_q = jnp.exp(b - bm)
    e_k = jnp.exp(bm - b)
    e_d = jnp.exp(bl - b)
    return qb, qs * e_q, k * e_k, k * e_d, jnp.exp(bl), eb, e_q, e_k, e_d


def _hgrn_fwd(proj, alb, ngain):
    S = proj.shape[1]
    nc = S // HG_CH
    nb = nc // HG_CPB
    rb, wb = HG_CPB * HG_CH, HG_HPB * HG_DK

    def body(pj_ref, alb_ref, ng_ref, o_ref, y_ref, st_ref, st_scr, b_scr):
        n = pl.program_id(1)

        @pl.when(n == 0)
        def _():
            st_scr[...] = jnp.zeros_like(st_scr)

        r = lax.broadcasted_iota(jnp.int32, (HG_CH, HG_CH), 0)
        c = lax.broadcasted_iota(jnp.int32, (HG_CH, HG_CH), 1)
        causal = r >= c
        for ci, j in [(ci, j) for ci in range(HG_CPB) for j in range(HG_HPB)]:
            rows, lanes = pl.ds(ci * HG_CH, HG_CH), pl.ds(j * HG_DK, HG_DK)
            q, f, v, g = pj_ref[0, rows, lanes], pj_ref[1, rows, lanes], pj_ref[2, rows, lanes], pj_ref[3, rows, lanes]
            qs, k, logf, _, _, _ = _hg_gates(q, f, alb_ref.at[:, lanes])
            b = _running_sum(logf)
            qb, qt, kt, kd, ebl, _, _, _, _ = _hg_intra(qs, k, b, b_scr.at[j, ci])
            st = st_scr[j]
            st_ref[j, ci] = st
            a = jnp.where(causal, _dot_nt(qt, kt), 0.0)
            o = _dot(a, v) + _dot_nt(qb, st)
            st_scr[j] = st * ebl + _dot_tn(v, kd)
            o_ref[rows, lanes] = o
            rinv = lax.rsqrt(jnp.mean(o * o, axis=-1, keepdims=True) + RMS_EPS)
            y_ref[rows, lanes] = (o * rinv * ng_ref[...] * (g * _sigmoid(g))).astype(CDT)

    blk = pl.BlockSpec((rb, wb), lambda h, n: (n, h))
    return _call(
        None, body, grid=(HG_H // HG_HPB, nb),
        in_specs=[pl.BlockSpec((4, rb, wb), lambda h, n: (0, n, h)), pl.BlockSpec((2, wb), lambda h, n: (0, h)),
                  pl.BlockSpec((1, HG_DK), lambda h, n: (0, 0))],
        out_specs=[blk, blk, pl.BlockSpec((HG_HPB, HG_CPB, HG_DK, HG_DK), lambda h, n: (h, n, 0, 0))],
        out_shape=[_sds((S, D), F32), _sds((S, D), CDT), _sds((HG_H, nc, HG_DK, HG_DK), F32)],
        scratch_shapes=[pltpu.VMEM((HG_HPB, HG_DK, HG_DK), F32), pltpu.VMEM((HG_HPB, HG_CPB, HG_CH, HG_DK), F32)],
        compiler_params=_params(("arbitrary", "arbitrary")), name="hgrn_fwd")(proj, alb, ngain)


def _hgrn_bwd(proj, alb, ngain, o, states, dy, after=None):
    S = proj.shape[1]
    nc = S // HG_CH
    nb = nc // HG_CPB
    rb, wb = HG_CPB * HG_CH, HG_HPB * HG_DK

    def body(pj_ref, alb_ref, ng_ref, o_ref, st_ref, dy_ref, dpj_ref, dalb_ref, dng_ref, dst_scr, b_scr):
        h, n = pl.program_id(0), pl.program_id(1)

        @pl.when(n == 0)
        def _():
            dst_scr[...] = jnp.zeros_like(dst_scr)
            dalb_ref[...] = jnp.zeros_like(dalb_ref)

        _zero_at(dng_ref, jnp.logical_and(h == 0, n == 0))
        ng = ng_ref[...]
        r = lax.broadcasted_iota(jnp.int32, (HG_CH, HG_CH), 0)
        c = lax.broadcasted_iota(jnp.int32, (HG_CH, HG_CH), 1)
        causal = r >= c
        dng = None
        for ci, j in [(ci, j) for ci in reversed(range(HG_CPB)) for j in range(HG_HPB)]:
            rows, lanes = pl.ds(ci * HG_CH, HG_CH), pl.ds(j * HG_DK, HG_DK)
            q, f, v, g = pj_ref[0, rows, lanes], pj_ref[1, rows, lanes], pj_ref[2, rows, lanes], pj_ref[3, rows, lanes]
            o_ = o_ref[rows, lanes]
            dy_ = dy_ref[rows, lanes]
            sg = _sigmoid(g)
            rinv = lax.rsqrt(jnp.mean(o_ * o_, axis=-1, keepdims=True) + RMS_EPS)
            nrm = o_ * rinv
            dr = dy_ * (g * sg)
            dg = dy_ * nrm * ng * (sg * (1.0 + g * (1.0 - sg)))
            dn = dr * ng
            do = rinv * (dn - nrm * jnp.mean(dn * nrm, axis=-1, keepdims=True))
            dng = _colsum(dr * nrm) if dng is None else dng + _colsum(dr * nrm)
            qs, k, logf, sig, lb, forget = _hg_gates(q, f, alb_ref.at[:, lanes])
            b = _running_sum(logf)
            qb, qt, kt, kd, ebl, eb, e_q, e_k, e_d = _hg_intra(qs, k, b, b_scr.at[j, ci])
            st = st_ref[j, ci]
            dstn = dst_scr[j]
            qt, kt, qb, kd = (t.astype(CDT).astype(F32) for t in (qt, kt, qb, kd))
            a = jnp.where(causal, _dot_nt(qt, kt), 0.0)
            da = jnp.where(causal, _dot_nt(do, v), 0.0)
            dv = _dot_tn(a, do) + _dot_nt(kd, dstn)
            dqb = _dot(do, st)
            dkd = _dot(v, dstn)
            dqt = _dot(da, kt)
            dkt = _dot_tn(da, qt)
            dbl = _colsum(dkd * kd) + ebl * _colsum(dstn * st)
            dst_scr[j] = dstn * ebl + _dot_tn(do, qb)
            dqs = dqt * e_q + dqb * eb
            dk = dkt * e_k + dkd * e_d
            db = dqt * qt + dqb * qb - dkt * kt - dkd * kd
            dlogf = _running_sum(db, reverse=True) + dbl
            dforget = dlogf / forget
            dsig = (1.0 - lb) * (dforget - dk)
            df = dsig * sig * (1.0 - sig)
            dlb = _colsum((dforget - dk) * (1.0 - sig))
            sq = _sigmoid(q)
            dq = dqs * (HG_DK ** -0.5) * (sq * (1.0 + q * (1.0 - sq)))
            dpj_ref[0, rows, lanes] = dq.astype(CDT)
            dpj_ref[1, rows, lanes] = df.astype(CDT)
            dpj_ref[2, rows, lanes] = dv.astype(CDT)
            dpj_ref[3, rows, lanes] = dg.astype(CDT)
            da0 = dlb * lb * (1.0 - lb)
            dalb_ref[pl.ds(0, 1), lanes] += da0
            dalb_ref[pl.ds(1, 1), lanes] -= da0
        dng_ref[...] += dng

    blk = pl.BlockSpec((rb, wb), lambda h, n: (nb - 1 - n, h))
    pj = pl.BlockSpec((4, rb, wb), lambda h, n: (0, nb - 1 - n, h))
    alb_blk = pl.BlockSpec((2, wb), lambda h, n: (0, h))
    ng_blk = pl.BlockSpec((1, HG_DK), lambda h, n: (0, 0))
    return _call(
        after, body, grid=(HG_H // HG_HPB, nb),
        in_specs=[pj, alb_blk, ng_blk, blk,
                  pl.BlockSpec((HG_HPB, HG_CPB, HG_DK, HG_DK), lambda h, n: (h, nb - 1 - n, 0, 0)), blk],
        out_specs=[pj, alb_blk, ng_blk],
        out_shape=[_sds((4, S, D), CDT), _sds((2, D), F32), _sds((1, HG_DK), F32)],
        scratch_shapes=[pltpu.VMEM((HG_HPB, HG_DK, HG_DK), F32), pltpu.VMEM((HG_HPB, HG_CPB, HG_CH, HG_DK), F32)],
        compiler_params=_params(("arbitrary", "arbitrary")), name="hgrn_bwd")(proj, alb, ngain, o, states, dy)


def _slope(h):
    return 2.0 ** (-8.0 * (h + 1) / ATT_QH)


def _attn_mask(n):
    qi = lax.broadcasted_iota(jnp.int32, (WINDOW, 2 * WINDOW), 0)
    si = lax.broadcasted_iota(jnp.int32, (WINDOW, 2 * WINDOW), 1)
    dist = qi - si + WINDOW
    valid = (dist >= 0) & (dist < WINDOW) & (n * WINDOW - WINDOW + si >= 0)
    return valid, dist.astype(F32)


def _attn_probs(qh, kh, sink, slope, valid, distf):
    s = _dot_nt(qh, kh) * (ATT_HD ** -0.5) - slope * distf
    s = jnp.where(valid, s, NEG)
    m = jnp.maximum(jnp.max(s, axis=-1, keepdims=True), sink)
    e = jnp.exp(s - m)
    es = jnp.exp(sink - m)
    inv = 1.0 / (jnp.sum(e, axis=-1, keepdims=True) + es)
    return e * inv, es * inv


def _attn_specs(S):
    nb = S // WINDOW
    cur = lambda H: pl.BlockSpec((H, WINDOW, ATT_HD), lambda n: (0, n, 0))
    prev = lambda H: pl.BlockSpec((H, WINDOW, ATT_HD), lambda n: (0, jnp.maximum(n - 1, 0), 0))
    return nb, cur, prev


def _attn_fwd(q4, kv4, sinks):
    S = q4.shape[1]
    nb, cur, prev = _attn_specs(S)

    def body(sink_ref, q_ref, kvc_ref, kvp_ref, o_ref):
        valid, distf = _attn_mask(pl.program_id(0))
        outs = []
        for h in range(ATT_QH):
            kvh = h // ATT_G
            kh = jnp.concatenate([kvp_ref[kvh], kvc_ref[kvh]], axis=0)
            vh = jnp.concatenate([kvp_ref[ATT_KVH + kvh], kvc_ref[ATT_KVH + kvh]], axis=0)
            p, _ = _attn_probs(q_ref[h], kh, sink_ref[0, h], _slope(h), valid, distf)
            outs.append(_dot(p, vh).astype(CDT))
            if h % 2:
                o_ref[:, pl.ds((h - 1) * ATT_HD, 2 * ATT_HD)] = _pair_lanes(outs[h - 1], outs[h])

    return _call(
        None, body, grid=(nb,),
        in_specs=[pl.BlockSpec(memory_space=pltpu.SMEM), cur(ATT_QH), cur(2 * ATT_KVH), prev(2 * ATT_KVH)],
        out_specs=pl.BlockSpec((WINDOW, D), lambda n: (n, 0)), out_shape=_sds((S, D), CDT),
        compiler_params=_params(("arbitrary",)), name="attn_fwd")(sinks, q4, kv4, kv4)


def _attn_bwd(q4, kv4, sinks, do):
    S = q4.shape[1]
    nb, cur, prev = _attn_specs(S)

    def body(sink_ref, q_ref, kvc_ref, kvp_ref, do_ref, dq_ref, dkv_ref, dbq_ref, dsink_ref):
        n = pl.program_id(0)
        first = n == 0

        @pl.when(first)
        def _():
            dkv_ref[...] = jnp.zeros_like(dkv_ref)
            dsink_ref[...] = jnp.zeros_like(dsink_ref)
            dbq_ref[...] = jnp.zeros_like(dbq_ref)

        valid, distf = _attn_mask(n)
        lane = lax.broadcasted_iota(jnp.int32, (1, 128), 1)
        rows_cur = pl.ds(pl.multiple_of(n * WINDOW, WINDOW), WINDOW)
        rows_prev = pl.ds(pl.multiple_of(jnp.maximum(n - 1, 0) * WINDOW, WINDOW), WINDOW)
        dsinks = jnp.zeros((1, 128), F32)
        sel = (_half_select(True), _half_select(False))
        for kvh in range(ATT_KVH):
            kh = jnp.concatenate([kvp_ref[kvh], kvc_ref[kvh]], axis=0)
            vh = jnp.concatenate([kvp_ref[ATT_KVH + kvh], kvc_ref[ATT_KVH + kvh]], axis=0)
            dk = dv = None
            dqs = []
            for h in range(kvh * ATT_G, (kvh + 1) * ATT_G):
                qh = q_ref[h]
                doh = jnp.dot(do_ref[:, pl.ds((h // 2) * 2 * ATT_HD, 2 * ATT_HD)], sel[h % 2],
                              preferred_element_type=F32).astype(CDT)
                p, ps = _attn_probs(qh, kh, sink_ref[0, h], _slope(h), valid, distf)
                dp = _dot_nt(doh, vh)
                dd = jnp.sum(p * dp, axis=-1, keepdims=True)
                ds = p * (dp - dd)
                dsinks = dsinks + jnp.where(lane == h, -jnp.sum(ps * dd, axis=0, keepdims=True), 0.0)
                dqh = _dot(ds, kh) * (ATT_HD ** -0.5)
                dqs.append(dqh.astype(CDT))
                dbq_ref[h] += _colsum(dqh)
                dkh = _dot_tn(ds, qh) * (ATT_HD ** -0.5)
                dvh = _dot_tn(p, doh)
                dk = dkh if dk is None else dk + dkh
                dv = dvh if dv is None else dv + dvh
            for i in range(ATT_G // 2):
                lanes = pl.ds((kvh * ATT_G + 2 * i) * ATT_HD, 2 * ATT_HD)
                dq_ref[:, lanes] = _pair_lanes(dqs[2 * i], dqs[2 * i + 1])
            dkv_ref[kvh, rows_prev, :] += dk[:WINDOW]
            dkv_ref[kvh, rows_cur, :] += dk[WINDOW:]
            dkv_ref[ATT_KVH + kvh, rows_prev, :] += dv[:WINDOW]
            dkv_ref[ATT_KVH + kvh, rows_cur, :] += dv[WINDOW:]
        dsink_ref[...] += dsinks

    return _call(
        None, body, grid=(nb,),
        in_specs=[pl.BlockSpec(memory_space=pltpu.SMEM), cur(ATT_QH), cur(2 * ATT_KVH), prev(2 * ATT_KVH),
                  pl.BlockSpec((WINDOW, D), lambda n: (n, 0))],
        out_specs=[pl.BlockSpec((WINDOW, D), lambda n: (n, 0)), pl.BlockSpec((2 * ATT_KVH, S, ATT_HD), lambda n: (0, 0, 0)),
                   pl.BlockSpec((ATT_QH, 1, ATT_HD), lambda n: (0, 0, 0)), pl.BlockSpec((1, 128), lambda n: (0, 0))],
        out_shape=[_sds((S, D), CDT), _sds((2 * ATT_KVH, S, ATT_HD), F32), _sds((ATT_QH, 1, ATT_HD), F32),
                   _sds((1, 128), F32)],
        compiler_params=_params(("arbitrary",)), name="attn_bwd")(sinks, q4, kv4, kv4, do)


def _local_step(x, p, target, getw, sm, emit):
    S = x.shape[0]
    vec = lambda a: a.reshape(1, -1)
    ln_g = lambda l, k: vec(sm["ln_gain"][l, k])
    ln_b = lambda l, k: vec(sm["ln_bias"][l, k])
    xb = x.astype(CDT)
    pb = p.astype(CDT)

    proj = _mm_nn(xb, getw("a_w_in", None), (4, S, D), (None, _tile(S), 512), lambda g, i: (g // 2, i, g % 2), F32,
                  name="a_in")
    o_a, y_a, states = _hgrn_fwd(proj, sm["a_lower_bound"], sm["a_norm_gain"])
    zeros = jnp.zeros((1, D), F32)
    z = [[None] * 3 for _ in range(2)]
    xs = [[None] * 3 for _ in range(2)]
    xbs = [[None] * 3 for _ in range(2)]
    z[0][0], xs[0][0], xbs[0][0] = _mixout_ln(y_a[None], getw("a_w_out", y_a)[None], zeros, x, ln_g(0, 0), ln_b(0, 0),
                                              "a_out_ln")
    gu, hid, sgs, ups = [None, None], [None, None], [None, None], [None, None]

    def ffn_ple(l):
        wgu = getw(f"gu{l}", xbs[l][0])
        gu[l], hid[l], z[l][1], xs[l][1], xbs[l][1] = _ffn_fwd(
            xs[l][0], xbs[l][0], wgu, getw(f"dn{l}", None), ln_g(l, 1), ln_b(l, 1), f"ffn_fwd{l}")
        sgs[l], ups[l], z[l][2], xs[l][2], xbs[l][2] = _ple_fwd(
            xs[l][1], xbs[l][1], pb[l], getw(f"pg{l}", None), vec(sm["ple_b_gate"][l]), getw(f"pu{l}", None), ln_g(l, 2),
            ln_b(l, 2), f"ple_fwd{l}")

    ffn_ple(0)
    x3, x3b = xs[0][2], xbs[0][2]
    w_kv, w_q, w_bo = getw("kv_w", x3b), getw("b_w_q", None), getw("b_w_out", None)
    kv4 = _proj_heads(x3b, w_kv, vec(sm["kv_b"]), 2 * ATT_KVH, "kv_proj")
    q4 = _proj_heads(x3b, w_q, vec(sm["b_b_q"]), ATT_QH, "q_proj")
    o_b = _attn_fwd(q4, kv4, sm["b_sinks"])
    z[1][0], xs[1][0], xbs[1][0] = _mixout_ln(o_b[None], w_bo[None], sm["b_b_out"], x3, ln_g(1, 0), ln_b(1, 0),
                                              "b_out_ln")
    ffn_ple(1)
    loss, dy = _loss_fwd_bwd(xs[1][2], target)

    gs = {}
    d_ln_g = [[None] * 3 for _ in range(2)]
    d_ln_b = [[None] * 3 for _ in range(2)]
    g_bg = [None, None]

    def ffn_ple_bwd(l, dy, after=None):
        dx2, dgl, dup, d_ln_g[l][2], d_ln_b[l][2], g_bg[l] = _ple_bwd(dy, z[l][2], sgs[l], ups[l], ln_g(l, 2),
                                                                     getw(f"pg{l}", None), f"ple_bwd{l}", after=after)
        g_pg = _wgrad(xbs[l][1][None], dgl[None], f"g_ple_gate{l}")[0]
        g_pu = _wgrad(pb[l][None], dup[None], f"g_ple_up{l}")[0]
        dx1, dzb, dgu, d_ln_g[l][1], d_ln_b[l][1] = _ffn_bwd(dx2, z[l][1], gu[l], getw(f"gu{l}", None),
                                                           getw(f"dn{l}", None), ln_g(l, 1), f"ffn_bwd{l}")
        g_dn = _wgrad(hid[l], dzb[None], f"g_ffn_down{l}")
        g_gu = _wgrad(dgu.reshape(8, S, FFN_B), xbs[l][0][None], f"g_ffn_gate_up{l}")
        return dx1, emit({f"pg{l}": g_pg, f"pu{l}": g_pu, f"dn{l}": g_dn, f"gu{l}": g_gu})

    dx1, tok = ffn_ple_bwd(1, dy)
    dz, dzb, do, d_ln_g[1][0], d_ln_b[1][0], gs["b_b_out"] = _mixout_bwd(dx1, z[1][0], ln_g(1, 0), w_bo[None], CDT,
                                                                        "b_out_bwd", after=tok)
    g_bo = _wgrad(o_b[None], dzb[None], "g_b_w_out")[0]
    dq, dkv4, dbq, dsinks = _attn_bwd(q4, kv4, sm["b_sinks"], do[0])
    gs["b_b_q"] = dbq
    gs["b_sinks"] = dsinks
    g_q = _wgrad(x3b[None], dq[None], "g_b_w_q")[0]
    dx3, dkv, gs["kv_b"] = _qkv_bwd(dz, dq, dkv4, w_q, w_kv, "qkv_bwd", after=tok)
    g_kv = _wgrad(x3b[None], dkv[None], "g_kv_w")[0]
    tok = emit({"b_w_out": g_bo, "b_w_q": g_q, "kv_w": g_kv})
    dx1, tok = ffn_ple_bwd(0, dx3, tok)
    w_ao = getw("a_w_out", None)
    dz, dzb, dyr, d_ln_g[0][0], d_ln_b[0][0], _ = _mixout_bwd(dx1, z[0][0], ln_g(0, 0), w_ao[None], F32, "a_out_bwd",
                                                              after=tok)
    g_ao = _wgrad(y_a[None], dzb[None], "g_a_w_out")[0]
    tok = emit({"a_w_out": g_ao})
    dproj, gs["a_lower_bound"], gs["a_norm_gain"] = _hgrn_bwd(proj, sm["a_lower_bound"], sm["a_norm_gain"], o_a, states,
                                                              dyr[0], after=tok)
    tk = lambda t: (None, t, D)
    g_ain = _mm_tn(xb[None], dproj, N_DEV, lambda g, k: (0, k, 0), lambda g, k: (g // 2, k, g % 2),
                   tk, lambda t: (None, t, 512), (N_DEV, D, 512), (None, D, 512), lambda g, k: (g, 0, 0), name="g_a_w_in")
    gs["ple_b_gate"] = jnp.concatenate(g_bg, axis=0)
    gs["ln_gain"] = jnp.stack([jnp.concatenate(r, axis=0) for r in d_ln_g])
    gs["ln_bias"] = jnp.stack([jnp.concatenate(r, axis=0) for r in d_ln_b])
    gs["loss"] = loss
    tok = emit({"a_w_in": g_ain}, small=gs)
    grad_x = _inproj_bwd(dz, dproj, getw("a_w_in", None), "a_in_bwd", after=tok)
    return loss, grad_x, gs


def _peer(k):
    x, y, c = lax.axis_index("x"), lax.axis_index("y"), lax.axis_index("c")
    px = 1 - x if k & 4 else x
    py = 1 - y if k & 2 else y
    pc = 1 - c if k & 1 else c
    return (px, py, pc), 4 * px + 2 * py + pc


def _my_index():
    return 4 * lax.axis_index("x") + 2 * lax.axis_index("y") + lax.axis_index("c")


def _exchange(srcs, dst_shapes, plan, name):
    n_src, n_piece = len(srcs), len(plan)

    def body(*refs):
        src_refs, dst_refs = refs[:n_src], refs[n_src:n_src + len(dst_shapes)]
        send_sems, recv_sems, local_sems = refs[n_src + len(dst_shapes):]
        me = _my_index()

        def at(ref, idx):
            return ref.at[idx] if idx else ref

        local = []
        for t, (si, sfn, di, dfn) in enumerate(plan):
            cp = pltpu.make_async_copy(at(src_refs[si], sfn(me)), at(dst_refs[di], dfn(me)), local_sems.at[t])
            cp.start()
            local.append(cp)
        sends = []
        for k in range(1, N_DEV):
            peer, pid = _peer(k)
            for t, (si, sfn, di, dfn) in enumerate(plan):
                cp = pltpu.make_async_remote_copy(
                    src_ref=at(src_refs[si], sfn(pid)), dst_ref=at(dst_refs[di], dfn(me)),
                    send_sem=send_sems.at[t * 7 + k - 1], recv_sem=recv_sems.at[t * 7 + k - 1],
                    device_id=peer, device_id_type=MESH)
                cp.start()
                sends.append(cp)
        for k in range(1, N_DEV):
            peer, pid = _peer(k)
            for t, (si, sfn, di, dfn) in enumerate(plan):
                pltpu.make_async_remote_copy(
                    src_ref=at(src_refs[si], sfn(me)), dst_ref=at(dst_refs[di], dfn(pid)),
                    send_sem=send_sems.at[t * 7 + k - 1], recv_sem=recv_sems.at[t * 7 + k - 1],
                    device_id=peer, device_id_type=MESH).wait_recv()
        for cp in sends:
            cp.wait_send()
        for cp in local:
            cp.wait()

    hbm = pl.BlockSpec(memory_space=pltpu.HBM)
    return _call(
        None, body, in_specs=[hbm] * n_src, out_specs=[hbm] * len(dst_shapes), out_shape=dst_shapes,
        scratch_shapes=[pltpu.SemaphoreType.DMA((7 * n_piece,)), pltpu.SemaphoreType.DMA((7 * n_piece,)),
                        pltpu.SemaphoreType.DMA((n_piece,))],
        name=name)(*srcs)


def _gather(shards, name):
    dsts = [_sds((N_DEV,) + a.shape, a.dtype) for a in shards]
    plan = [(i, lambda j: (), i, lambda s: (s,)) for i in range(len(shards))]
    return _exchange(shards, dsts, plan, name)


_HBM = pl.BlockSpec(memory_space=pltpu.HBM)
_SEM = pl.BlockSpec(memory_space=pltpu.SEMAPHORE)
_DATAFLOW = pltpu.SideEffectType.DATAFLOW_SIDE_EFFECTING


def _piece_copy(mode, src, land, send_sems, recv_sems, t, k, sender, receiver, peer):
    return pltpu.make_async_remote_copy(
        src_ref=src if mode == "gather" else src.at[receiver], dst_ref=land.at[sender],
        send_sem=send_sems.at[t * 7 + k - 1], recv_sem=recv_sems.at[t * 7 + k - 1], device_id=peer, device_id_type=MESH)


def _sequencer_exchange(srcs, modes, name, collective_id, after=None):
    n = len(srcs)
    land_shapes = [((N_DEV,) + a.shape) if mode == "gather" else a.shape for a, mode in zip(srcs, modes)]
    extra = [] if after is None else [after]

    def body(*refs):
        src_refs, land_refs = refs[:n], refs[n + len(extra):2 * n + len(extra)]
        send_sems, recv_sems, local_sems = refs[2 * n + len(extra):]
        barrier = pltpu.get_barrier_semaphore()
        for k in range(1, N_DEV):
            pl.semaphore_signal(barrier, inc=1, device_id=_peer(k)[0], device_id_type=MESH)
        pl.semaphore_wait(barrier, N_DEV - 1)
        me = _my_index()
        local = []
        for i in range(n):
            cp = pltpu.make_async_copy(src_refs[i] if modes[i] == "gather" else src_refs[i].at[me], land_refs[i].at[me],
                                       local_sems.at[i])
            cp.start()
            local.append(cp)
        for k in range(1, N_DEV):
            peer, pid = _peer(k)
            for t in range(n):
                _piece_copy(modes[t], src_refs[t], land_refs[t], send_sems, recv_sems, t, k, me, pid, peer).start()
        for k in range(1, N_DEV):
            peer, pid = _peer(k)
            for t in range(n):
                _piece_copy(modes[t], src_refs[t], land_refs[t], send_sems, recv_sems, t, k, pid, me, peer).wait_recv()
        for k in range(1, N_DEV):
            peer, pid = _peer(k)
            for t in range(n):
                _piece_copy(modes[t], src_refs[t], land_refs[t], send_sems, recv_sems, t, k, me, pid, peer).wait_send()
        for cp in local:
            cp.wait()

    return pl.kernel(
        body, out_type=[_sds(s, a.dtype) for s, a in zip(land_shapes, srcs)],
        mesh=plsc.ScalarSubcoreMesh(axis_name="sequencer", num_cores=1),
        scratch_types=[pltpu.SemaphoreType.DMA((7 * n,)), pltpu.SemaphoreType.DMA((7 * n,)), pltpu.SemaphoreType.DMA((n,))],
        compiler_params=pltpu.CompilerParams(collective_id=collective_id), name=name)(*srcs, *extra)


def _sequencer_sibling_swap(blocks, name, collective_id):
    n = len(blocks)

    def body(*refs):
        srcs, mine, theirs = refs[:n], refs[n:2 * n], refs[2 * n:3 * n]
        send_sems, recv_sems, local_sems = refs[3 * n:]
        x, y, c = lax.axis_index("x"), lax.axis_index("y"), lax.axis_index("c")
        sibling = (x, y, 1 - c)
        barrier = pltpu.get_barrier_semaphore()
        pl.semaphore_signal(barrier, inc=1, device_id=sibling, device_id_type=MESH)
        pl.semaphore_wait(barrier, 1)
        copies, local = [], []
        for t in range(n):
            for q in range(4):
                cp = pltpu.make_async_copy(srcs[t].at[2 * q + c], mine[t].at[q], local_sems.at[4 * t + q])
                cp.start()
                local.append(cp)
                copies.append(pltpu.make_async_remote_copy(
                    src_ref=srcs[t].at[2 * q + (1 - c)], dst_ref=theirs[t].at[q], send_sem=send_sems.at[4 * t + q],
                    recv_sem=recv_sems.at[4 * t + q], device_id=sibling, device_id_type=MESH))
        for cp in copies:
            cp.start()
        for cp in copies:
            cp.wait_recv()
        for cp in copies:
            cp.wait_send()
        for cp in local:
            cp.wait()

    quarter = [_sds((4,) + a.shape[1:], a.dtype) for a in blocks]
    outs = pl.kernel(
        body, out_type=quarter + quarter, mesh=plsc.ScalarSubcoreMesh(axis_name="sequencer", num_cores=1),
        scratch_types=[pltpu.SemaphoreType.DMA((4 * n,)), pltpu.SemaphoreType.DMA((4 * n,)), pltpu.SemaphoreType.DMA((4 * n,))],
        compiler_params=pltpu.CompilerParams(collective_id=collective_id), name=name)(*blocks)
    return outs[:n], outs[n:]


def _pair_sums(mine, theirs, name):
    n = len(mine)

    def body(*refs):
        for t in range(n):
            refs[2 * n + t][...] = (refs[t][...].astype(F32) + refs[n + t][...].astype(F32)).astype(CDT)

    specs = []
    for a in mine:
        _, R, C = a.shape
        tr = _tile(R, (256, 128, 176, 64, 32, 16))
        specs.append((pl.BlockSpec((None, tr, C), lambda q, i, nr=R // tr: (q, jnp.minimum(i, nr - 1), 0)), R // tr))
    steps = max(s[1] for s in specs)
    blocks = [s[0] for s in specs]
    return _call(
        None, body, grid=(4, steps), in_specs=blocks + blocks, out_specs=blocks,
        out_shape=[_sds(a.shape, CDT) for a in mine],
        compiler_params=_params(("arbitrary", "arbitrary")), name=name)(*mine, *theirs)


def _sequencer_chip_scatter(pairs, name, collective_id):
    n = len(pairs)

    def body(*refs):
        srcs, lands = refs[:n], refs[n:2 * n]
        send_sems, recv_sems, local_sems = refs[2 * n:]
        x, y, c = lax.axis_index("x"), lax.axis_index("y"), lax.axis_index("c")
        chips = [(1 - x, y), (x, 1 - y), (1 - x, 1 - y)]
        chip = lambda px, py: 2 * px + py
        barrier = pltpu.get_barrier_semaphore()
        for px, py in chips:
            pl.semaphore_signal(barrier, inc=1, device_id=(px, py, c), device_id_type=MESH)
        pl.semaphore_wait(barrier, 3)
        here = chip(x, y)
        local, sends = [], []
        for t in range(n):
            cp = pltpu.make_async_copy(srcs[t].at[here], lands[t].at[here], local_sems.at[t])
            cp.start()
            local.append(cp)
            for j, (px, py) in enumerate(chips):
                sends.append(pltpu.make_async_remote_copy(
                    src_ref=srcs[t].at[chip(px, py)], dst_ref=lands[t].at[here], send_sem=send_sems.at[3 * t + j],
                    recv_sem=recv_sems.at[3 * t + j], device_id=(px, py, c), device_id_type=MESH))
        for cp in sends:
            cp.start()
        for t in range(n):
            for j, (px, py) in enumerate(chips):
                pltpu.make_async_remote_copy(
                    src_ref=srcs[t].at[here], dst_ref=lands[t].at[chip(px, py)], send_sem=send_sems.at[3 * t + j],
                    recv_sem=recv_sems.at[3 * t + j], device_id=(px, py, c), device_id_type=MESH).wait_recv()
        for cp in sends:
            cp.wait_send()
        for cp in local:
            cp.wait()

    return pl.kernel(
        body, out_type=[_sds(a.shape, a.dtype) for a in pairs],
        mesh=plsc.ScalarSubcoreMesh(axis_name="sequencer", num_cores=1),
        scratch_types=[pltpu.SemaphoreType.DMA((3 * n,)), pltpu.SemaphoreType.DMA((3 * n,)), pltpu.SemaphoreType.DMA((n,))],
        compiler_params=pltpu.CompilerParams(collective_id=collective_id), name=name)(*pairs)


def _sequencer_gather(srcs, name, collective_id, after=None):
    n = len(srcs)
    extra = [] if after is None else [after]

    def body(*refs):
        src_refs, land_refs = refs[:n], refs[n + len(extra):2 * n + len(extra)]
        send_sems, recv_sems, local_sems = refs[2 * n + len(extra):]
        x, y, c = lax.axis_index("x"), lax.axis_index("y"), lax.axis_index("c")
        sibling = (x, y, 1 - c)
        chips = [(1 - x, y), (x, 1 - y), (1 - x, 1 - y)]
        index = lambda px, py, pc: 4 * px + 2 * py + pc
        barrier = pltpu.get_barrier_semaphore()
        for peer in [sibling] + [(*chip, c) for chip in chips]:
            pl.semaphore_signal(barrier, inc=1, device_id=peer, device_id_type=MESH)
        pl.semaphore_wait(barrier, 4)

        def copy(t, k, slot, to, src=None):
            return pltpu.make_async_remote_copy(
                src_ref=land_refs[t].at[slot] if src is None else src, dst_ref=land_refs[t].at[slot],
                send_sem=send_sems.at[7 * t + k], recv_sem=recv_sems.at[7 * t + k], device_id=to, device_id_type=MESH)

        me = index(x, y, c)
        local = []
        for t in range(n):
            cp = pltpu.make_async_copy(src_refs[t], land_refs[t].at[me], local_sems.at[t])
            cp.start()
            local.append(cp)
        sends = []
        for t in range(n):
            sends.append(copy(t, 0, me, sibling, src=src_refs[t]))
            sends += [copy(t, 1 + j, me, (*chip, c), src=src_refs[t]) for j, chip in enumerate(chips)]
        for cp in sends:
            cp.start()
        for j, chip in enumerate(chips):
            for t in range(n):
                copy(t, 1 + j, index(*chip, c), sibling, src=src_refs[t]).wait_recv()
                passed = copy(t, 4 + j, index(*chip, c), sibling)
                passed.start()
                sends.append(passed)
        for t in range(n):
            copy(t, 0, index(x, y, 1 - c), sibling, src=src_refs[t]).wait_recv()
        for j, chip in enumerate(chips):
            for t in range(n):
                copy(t, 4 + j, index(*chip, 1 - c), sibling, src=src_refs[t]).wait_recv()
        for cp in sends:
            cp.wait_send()
        for cp in local:
            cp.wait()

    return pl.kernel(
        body, out_type=[_sds((N_DEV,) + a.shape, a.dtype) for a in srcs],
        mesh=plsc.ScalarSubcoreMesh(axis_name="sequencer", num_cores=1),
        scratch_types=[pltpu.SemaphoreType.DMA((7 * n,)), pltpu.SemaphoreType.DMA((7 * n,)), pltpu.SemaphoreType.DMA((n,))],
        compiler_params=pltpu.CompilerParams(collective_id=collective_id), name=name)(*srcs, *extra)


def _xstart(groups, mode, name, after=None):
    flat = [a for g in groups for a in g]
    n, ng = len(flat), len(groups)
    land_shapes = [((N_DEV,) + a.shape) if mode == "gather" else a.shape for a in flat]
    first = [sum(len(g) for g in groups[:i]) for i in range(ng)]
    extra = [] if after is None else [after]

    def body(*refs):
        srcs, lands = refs[:n], refs[n:2 * n]
        sems = refs[2 * n + len(extra):2 * n + len(extra) + 2 * ng]
        tok_ref, local_sems = refs[-2], refs[-1]
        me = _my_index()
        local = []
        for i in range(n):
            cp = pltpu.make_async_copy(srcs[i] if mode == "gather" else srcs[i].at[me], lands[i].at[me], local_sems.at[i])
            cp.start()
            local.append(cp)
        for cp in local:
            cp.wait()
        for gi, g in enumerate(groups):
            for k in range(1, N_DEV):
                peer, pid = _peer(k)
                for t in range(len(g)):
                    i = first[gi] + t
                    _piece_copy(mode, srcs[i], lands[i], sems[2 * gi], sems[2 * gi + 1], t, k, me, pid, peer).start()
        tok_ref[...] = jnp.zeros_like(tok_ref)

    sem_shapes = []
    for g in groups:
        sem_shapes += [pltpu.SemaphoreType.DMA((7 * len(g),))] * 2
    thru = [pltpu.HBM(a.shape, a.dtype) for a in flat] + [pltpu.HBM(s, a.dtype) for s, a in zip(land_shapes, flat)]
    outs = pl.pallas_call(
        body, in_specs=[_HBM] * (2 * n) + [pl.BlockSpec(memory_space=pl.ANY)] * len(extra),
        out_specs=[_SEM] * (2 * ng) + [_HBM] * (2 * n) + [pl.BlockSpec(memory_space=pltpu.VMEM)],
        out_shape=sem_shapes + thru + [_sds((8, 128), F32)],
        input_output_aliases={i: 2 * ng + i for i in range(2 * n)},
        scratch_shapes=[pltpu.SemaphoreType.DMA((n,))],
        compiler_params=pltpu.CompilerParams(has_side_effects=_DATAFLOW), name=name)(
            *[pltpu.with_memory_space_constraint(a, pltpu.HBM) for a in flat],
            *[pltpu.with_memory_space_constraint(lax.empty(s, a.dtype), pltpu.HBM) for s, a in zip(land_shapes, flat)], *extra)
    sems, srcs_thru, lands_thru = outs[:2 * ng], outs[2 * ng:2 * ng + n], outs[2 * ng + n:2 * ng + 2 * n]
    handles = [(sems[2 * gi], sems[2 * gi + 1], srcs_thru[first[gi]:first[gi] + len(g)],
                lands_thru[first[gi]:first[gi] + len(g)]) for gi, g in enumerate(groups)]
    return handles, outs[-1]


def _xwait(handle, mode, after, name):
    send_sems, recv_sems, srcs_thru, lands_thru = handle
    n = len(srcs_thru)

    def body(*refs):
        srcs, lands, send, recv = refs[:n], refs[n:2 * n], refs[2 * n], refs[2 * n + 1]
        me = _my_index()
        for k in range(1, N_DEV):
            peer, pid = _peer(k)
            for t in range(n):
                _piece_copy(mode, srcs[t], lands[t], send, recv, t, k, pid, me, peer).wait_recv()
        for k in range(1, N_DEV):
            peer, pid = _peer(k)
            for t in range(n):
                _piece_copy(mode, srcs[t], lands[t], send, recv, t, k, me, pid, peer).wait_send()

    extra = [] if after is None else [after]
    outs = pl.pallas_call(
        body, in_specs=[_HBM] * (2 * n) + [_SEM, _SEM] + [pl.BlockSpec(memory_space=pl.ANY)] * len(extra),
        out_specs=[_HBM] * (2 * n),
        out_shape=[pltpu.HBM(a.shape, a.dtype) for a in list(srcs_thru) + list(lands_thru)],
        input_output_aliases={i: i for i in range(2 * n)},
        compiler_params=pltpu.CompilerParams(has_side_effects=_DATAFLOW), name=name)(
            *srcs_thru, *lands_thru, send_sems, recv_sems, *extra)
    return outs[n:]


def _adamw(w, g, m, v):
    m = ADAM_B1 * m + (1.0 - ADAM_B1) * g
    v = ADAM_B2 * v + (1.0 - ADAM_B2) * (g * g)
    m_hat = m / (1.0 - ADAM_B1 ** ADAM_STEP)
    v_hat = v / (1.0 - ADAM_B2 ** ADAM_STEP)
    delta = -ADAM_LR * (m_hat / (jnp.sqrt(v_hat) + ADAM_EPS) + ADAM_WD * w)
    return delta, m, v


def _adam_big(w, parts, m, v, name, after=None):
    L, R, C = w.shape
    P = parts[0].shape[0]
    tr = _tile(R, (256, 128, 176, 64, 32, 16))
    nr = R // tr

    def body(w_ref, *refs):
        p_refs, (m_ref, v_ref, g_ref, d_ref, mo_ref, vo_ref) = refs[:L], refs[L:]
        for l in range(L):
            @pl.when(pl.program_id(0) == l)
            def _(p_ref=p_refs[l]):
                g = p_ref[0].astype(F32)
                for s in range(1, P):
                    g = g + p_ref[s].astype(F32)
                g_ref[...] = g
                d_ref[...], mo_ref[...], vo_ref[...] = _adamw(w_ref[...], g, m_ref[...], v_ref[...])

    row = pl.BlockSpec((None, tr, C), lambda l, i: (l, i, 0))
    park = lambda l_of: (lambda l, i: (0, jnp.where(l == l_of, i, 0 if l_of else nr - 1), 0))
    return _call(
        after, body, grid=(L, nr),
        in_specs=[row] + [pl.BlockSpec((P, tr, C), park(l)) for l in range(L)] + [row, row],
        out_specs=[row] * 4, out_shape=[_sds((L, R, C), F32)] * 4,
        compiler_params=_params(("arbitrary", "arbitrary")), name=name)(w, *parts, m, v)


SMALL = (("a_lower_bound", (2, 128), (2, D)), ("ln_gain", (6, 128), (6, D)), ("ln_bias", (6, 128), (6, D)),
         ("a_norm_gain", (1, 128), (1, 128)), ("kv_b", (1, 512), (1, 512)), ("b_b_q", (1, D), (1, D)),
         ("b_sinks", (1, ATT_QH), (1, 128)), ("b_b_out", (1, D), (1, D)), ("ple_b_gate", (2, D), (2, D)))


def _adam_small(parts, w, m, v, losses, after=None):
    k = len(SMALL)

    def body(*refs):
        p_refs, w_refs, m_refs, v_refs = refs[:k], refs[k:2 * k], refs[2 * k:3 * k], refs[3 * k:4 * k]
        loss_ref, outs, total_ref = refs[4 * k], refs[4 * k + 1:-1], refs[-1]
        total = loss_ref[0]
        for s in range(1, N_DEV):
            total = total + loss_ref[s]
        total_ref[...] = total
        me = _my_index()
        for i, (_, wshape, pshape) in enumerate(SMALL):
            cols = wshape[1]
            lanes = slice(None) if cols == pshape[1] else (
                pl.ds(0, cols) if cols < 128 else pl.ds(pl.multiple_of(me * cols, cols), cols))
            g = p_refs[i][0, :, lanes]
            for s in range(1, N_DEV):
                g = g + p_refs[i][s, :, lanes]
            g_ref, d_ref, mo_ref, vo_ref = outs[4 * i:4 * i + 4]
            g_ref[...] = g
            d_ref[...], mo_ref[...], vo_ref[...] = _adamw(w_refs[i][...], g, m_refs[i][...], v_refs[i][...])

    full = lambda shape: pl.BlockSpec(shape, lambda: (0,) * len(shape))
    names = [n for n, _, _ in SMALL]
    res = _call(
        after, body,
        in_specs=[full((N_DEV,) + ps) for _, _, ps in SMALL] + [full(ws) for _, ws, _ in SMALL] * 3
        + [full((N_DEV, 1, 128))],
        out_specs=[full(ws) for _, ws, _ in SMALL for _ in range(4)] + [full((1, 128))],
        out_shape=[_sds(ws, F32) for _, ws, _ in SMALL for _ in range(4)] + [_sds((1, 128), F32)], name="adam_small")(
            *[parts[n] for n in names], *[a[n].reshape(ws) for a in (w, m, v) for n, ws, _ in SMALL], losses)
    return {n: [r.reshape(w[n].shape) for r in res[4 * i:4 * i + 4]] for i, n in enumerate(names)}, res[-1][0, 0]


WEIGHTS = ("a_w_in", "a_lower_bound", "a_norm_gain", "a_w_out", "kv_w", "kv_b", "b_w_q", "b_b_q", "b_sinks", "b_w_out",
           "b_b_out", "ffn_w_gate_up", "ffn_w_down", "ple_w_up", "ple_w_gate", "ple_b_gate", "ln_gain", "ln_bias")


GATHER_GROUPS = (("a_w_in",), ("a_w_out", "gu0"), ("dn0", "pu0", "pg0"), ("kv_w", "b_w_q", "b_w_out"),
                 ("gu1", "dn1", "pu1", "pg1"))
KERNEL_LAYOUT = {
    "a_w_in": lambda a: a,
    "a_w_out": lambda a: a.reshape(D, D),
    "kv_w": lambda a: a.reshape(D, 2 * ATT_KVH * ATT_HD),
    "b_w_q": lambda a: a.reshape(D, D),
    "b_w_out": lambda a: a.reshape(D, D),
    "gu": lambda a: a.reshape(2, 4, FFN_B, D),
    "dn": lambda a: a.reshape(4, FFN_B, D),
    "pu": lambda a: a,
    "pg": lambda a: a.reshape(D, D),
}
_row_blocks = lambda a: a.reshape(N_DEV, -1, a.shape[-1])
OWNER_BLOCKS = {
    "a_w_in": lambda g: g,
    "a_w_out": _row_blocks,
    "kv_w": _row_blocks,
    "b_w_q": _row_blocks,
    "b_w_out": _row_blocks,
    "gu": lambda g: g,
    "dn": lambda g: _row_blocks(g.reshape(FFN_H, D)),
    "pu": lambda g: g.reshape(PLE_DIM, N_DEV, 128).transpose(1, 0, 2),
    "pg": _row_blocks,
}
ADAM_AFTER = {1: (("kv_w", "kv_w"), ("b_w_q", "b_w_q"), ("b_w_out", "b_w_out")),
              2: (("ffn_w_gate_up", "gu"), ("ffn_w_down", "dn"), ("ple_w_up", "pu"), ("ple_w_gate", "pg")),
              3: (("a_w_out", "a_w_out"),), 4: (("a_w_in", "a_w_in"),)}


def kernel(x, p, a_w_in, a_lower_bound, a_norm_gain, a_w_out, kv_w, kv_b, b_w_q, b_b_q, b_sinks, b_w_out, b_b_out, ffn_w_gate_up, ffn_w_down, ple_w_up, ple_w_gate, ple_b_gate, ln_gain, ln_bias, loss_target, m_a_w_in, m_a_lower_bound, m_a_norm_gain, m_a_w_out, m_kv_w, m_kv_b, m_b_w_q, m_b_b_q, m_b_sinks, m_b_w_out, m_b_b_out, m_ffn_w_gate_up, m_ffn_w_down, m_ple_w_up, m_ple_w_gate, m_ple_b_gate, m_ln_gain, m_ln_bias, v_a_w_in, v_a_lower_bound, v_a_norm_gain, v_a_w_out, v_kv_w, v_kv_b, v_b_w_q, v_b_b_q, v_b_sinks, v_b_w_out, v_b_b_out, v_ffn_w_gate_up, v_ffn_w_down, v_ple_w_up, v_ple_w_gate, v_ple_b_gate, v_ln_gain, v_ln_bias):
    given = dict(locals())
    w = {n: given[n] for n in WEIGHTS}
    m = {n: given["m_" + n] for n in WEIGHTS}
    v = {n: given["v_" + n] for n in WEIGHTS}
    shards = {"a_w_in": a_w_in[0], "a_w_out": a_w_out[0], "kv_w": kv_w, "b_w_q": b_w_q[0], "b_w_out": b_w_out[0]}
    for l in range(2):
        shards.update({f"gu{l}": ffn_w_gate_up[l].T, f"dn{l}": ffn_w_down[l], f"pu{l}": ple_w_up[l], f"pg{l}": ple_w_gate[l]})
    sharded_small = [a_lower_bound, ln_gain.reshape(6, 128), ln_bias.reshape(6, 128)]
    gathered = {}
    for gi, g in enumerate(GATHER_GROUPS):
        lands = _sequencer_gather([shards[n].astype(CDT) for n in g] + (sharded_small if gi == 0 else []),
                                  f"gather{gi}", gi)
        for n, a in zip(g, lands):
            gathered[n] = KERNEL_LAYOUT[n.rstrip("01")](a)
        if gi == 0:
            alb, lng, lnb = [a.transpose(1, 0, 2).reshape(a.shape[1], D) for a in lands[len(g):]]

    def getw(key, after):
        return gathered[key]

    sm = {"a_lower_bound": alb, "ln_gain": lng.reshape(2, 3, D), "ln_bias": lnb.reshape(2, 3, D),
          "a_norm_gain": a_norm_gain, "kv_b": kv_b, "b_b_q": b_b_q[0], "b_sinks": b_sinks, "b_b_out": b_b_out,
          "ple_b_gate": ple_b_gate}

    scatters, small_parts, swapped = [], {}, []
    first_id = len(GATHER_GROUPS)

    def finish(names, mine, theirs):
        k = len(scatters)
        pairs = _pair_sums(mine, theirs, f"pair_sum{k}")
        lands = _sequencer_chip_scatter(pairs, f"scatter{k}", first_id + 2 * k + 1)
        scatters.append(dict(zip(names, lands)))

    def emit(grads, small=None):
        names = list(grads)
        blocks = [OWNER_BLOCKS[n.rstrip("01")](grads[n]) for n in names]
        k = len(scatters) + len(swapped)
        mine, theirs = _sequencer_sibling_swap(blocks, f"swap{k}", first_id + 2 * k)
        if swapped:
            finish(*swapped.pop())
        swapped.append((names, mine, theirs))
        if small is not None:
            names_small = [n for n, _, _ in SMALL] + ["loss"]
            partials = [small[n].reshape(ps) for n, _, ps in SMALL] + [small["loss"]]
            small_parts.update(zip(names_small, _sequencer_gather(partials, "gather_small_grads", first_id + 2 * k + 2)))
        return blocks

    loss, grad_x, gs = _local_step(x[0], p[:, 0], loss_target[0], getw, sm, emit)
    finish(*swapped.pop())

    out, parts, last = {}, {}, [grad_x] + list(scatters[0].values())
    for i, landed in enumerate(scatters):
        parts.update(landed)
        for n, key in ADAM_AFTER.get(i, ()):
            lrc = (1,) * (3 - w[n].ndim) + w[n].shape
            layers = [parts[key]] if key in parts else [parts[key + "0"], parts[key + "1"]]
            shard = (lambda a: a.reshape(lrc).swapaxes(1, 2)) if key == "gu" else (lambda a: a.reshape(lrc))
            res = _adam_big(shard(w[n]), layers, shard(m[n]), shard(v[n]), "adam_" + n, after=last)
            out[n] = [(r.swapaxes(1, 2) if key == "gu" else r).reshape(w[n].shape) for r in res]
            last = [res[3]]
    small_out, loss = _adam_small(small_parts, w, m, v, small_parts["loss"], after=last)
    out.update(small_out)
    res = [loss, grad_x[None]]
    for i in range(4):
        res += [out[n][i] for n in WEIGHTS]
    return tuple(res)
```

```python
import jax
import jax.numpy as jnp
from jax import lax
from jax.experimental import pallas as pl
from jax.experimental.pallas import tpu as pltpu
from jax.experimental.pallas import tpu_sc as plsc

F32 = jnp.float32
CDT = jnp.bfloat16

N_DEV = 8
D = 1024
HG_H, HG_DK, HG_CH = 8, 128, 64
HG_HPB = 4
HG_CPB = 8
ATT_HD, ATT_QH, ATT_KVH, ATT_G, WINDOW = 64, 16, 4, 4, 128
FFN_H = 2816
FFN_B = FFN_H // 4
PLE_DIM = 256
ALPHA = (2.0 * 2) ** 0.25
LN_EPS = 1e-5
RMS_EPS = 1e-6
ADAM_LR, ADAM_B1, ADAM_B2, ADAM_EPS, ADAM_WD, ADAM_STEP = 0.001, 0.9, 0.999, 1e-08, 0.01, 10
ROW_TILES = (512, 256, 128, 64)
VMEM_LIMIT = 48 * 1024 * 1024
NEG = -1e30

MESH = pl.DeviceIdType.MESH


def _tile(n, cands=ROW_TILES):
    for t in cands:
        if n % t == 0:
            return t
    return n


def _sds(shape, dtype):
    return jax.ShapeDtypeStruct(tuple(shape), dtype)


def _params(sem):
    return pltpu.CompilerParams(dimension_semantics=sem, vmem_limit_bytes=VMEM_LIMIT)


def _dot(a, b):
    return jnp.dot(a.astype(CDT), b.astype(CDT), preferred_element_type=F32)


def _dot_nt(a, b):
    return lax.dot_general(a.astype(CDT), b.astype(CDT), (((1,), (1,)), ((), ())), preferred_element_type=F32)


def _dot_tn(a, b):
    return lax.dot_general(a.astype(CDT), b.astype(CDT), (((0,), (0,)), ((), ())), preferred_element_type=F32)


def _sigmoid(x):
    return jax.nn.sigmoid(x)


def _ln_fwd(z, g, b):
    mu = jnp.mean(z, axis=-1, keepdims=True)
    zc = z - mu
    var = jnp.mean(zc * zc, axis=-1, keepdims=True)
    return zc * lax.rsqrt(var + LN_EPS) * g + b


def _ln_bwd(z, g, dy):
    mu = jnp.mean(z, axis=-1, keepdims=True)
    zc = z - mu
    var = jnp.mean(zc * zc, axis=-1, keepdims=True)
    rstd = lax.rsqrt(var + LN_EPS)
    xhat = zc * rstd
    dxh = dy * g
    dz = rstd * (dxh - jnp.mean(dxh, axis=-1, keepdims=True) - xhat * jnp.mean(dxh * xhat, axis=-1, keepdims=True))
    return dz, xhat


def _colsum(x):
    return jnp.sum(x, axis=0, keepdims=True)


def _acc(ref, val, first):
    @pl.when(first)
    def _():
        ref[...] = val

    @pl.when(jnp.logical_not(first))
    def _():
        ref[...] += val


def _zero_at(ref, first):
    @pl.when(first)
    def _():
        ref[...] = jnp.zeros_like(ref)


def _call(after, body, **kw):
    after = [] if after is None else list(after)
    specs = list(kw["in_specs"])
    kw["in_specs"] = [pl.BlockSpec(memory_space=pl.ANY)] * len(after) + specs

    def ordered_body(*refs):
        body(*refs[len(after):])

    call = pl.pallas_call(ordered_body, **kw)

    def pinned(*args):
        args = [a if s.memory_space is not None else pltpu.with_memory_space_constraint(a, pltpu.HBM)
                for a, s in zip(args, specs)]
        return call(*after, *args)

    return pinned


def _mm_nn(a, b3, out_shape, oblock, omap, out_dtype, bias3=None, name="mm_nn"):
    M, K = a.shape
    G, _, Nb = b3.shape
    tm = _tile(M)

    def body(a_ref, b_ref, *rest):
        o_ref = rest[-1]
        acc = _dot(a_ref[...], b_ref[...])
        if bias3 is not None:
            acc = acc + rest[0][...]
        o_ref[...] = acc.astype(o_ref.dtype)

    in_specs = [pl.BlockSpec((tm, K), lambda g, i: (i, 0)), pl.BlockSpec((None, K, Nb), lambda g, i: (g, 0, 0))]
    args = [a, b3]
    if bias3 is not None:
        in_specs.append(pl.BlockSpec((None, 1, Nb), lambda g, i: (g, 0, 0)))
        args.append(bias3)
    return _call(
        None, body, grid=(G, M // tm), in_specs=in_specs, out_specs=pl.BlockSpec(oblock, omap),
        out_shape=_sds(out_shape, out_dtype), compiler_params=_params(("arbitrary", "arbitrary")), name=name)(*args)


def _mm_tn(a3, b3, G, amap, bmap, ablock, bblock, out_shape, oblock, omap, name="mm_tn"):
    S = a3.shape[1]

    def body(a_ref, b_ref, o_ref):
        o_ref[...] = _dot_tn(a_ref[...], b_ref[...]).astype(o_ref.dtype)

    return _call(
        None, body, grid=(G, 1),
        in_specs=[pl.BlockSpec(ablock(S), amap), pl.BlockSpec(bblock(S), bmap)],
        out_specs=pl.BlockSpec(oblock, omap), out_shape=_sds(out_shape, CDT),
        compiler_params=_params(("arbitrary", "arbitrary")), name=name)(a3, b3)


def _wgrad(a3, b3, name):
    Ga, S, M = a3.shape
    Gb, _, N = b3.shape
    G = max(Ga, Gb)
    return _mm_tn(
        a3, b3, G,
        (lambda g, k: (g, k, 0)) if Ga > 1 else (lambda g, k: (0, k, 0)),
        (lambda g, k: (g, k, 0)) if Gb > 1 else (lambda g, k: (0, k, 0)),
        lambda tk: (None, tk, M), lambda tk: (None, tk, N),
        (G, M, N), (None, M, N), lambda g, k: (g, 0, 0), name=name)


def _mixout_ln(u3, w3, bias, xin, gain, beta, name):
    G, S, Kb = u3.shape
    tm = _tile(S)

    def body(u_ref, w_ref, b_ref, x_ref, g_ref, be_ref, z_ref, xo_ref, xob_ref):
        h = b_ref[...] + _dot(u_ref[0], w_ref[0])
        for g in range(1, G):
            h = h + _dot(u_ref[g], w_ref[g])
        z = ALPHA * x_ref[...] + h
        z_ref[...] = z
        y = _ln_fwd(z, g_ref[...], be_ref[...])
        xo_ref[...] = y
        xob_ref[...] = y.astype(CDT)

    row = pl.BlockSpec((tm, D), lambda i: (i, 0))
    vec = pl.BlockSpec((1, D), lambda i: (0, 0))
    return _call(
        None, body, grid=(S // tm,),
        in_specs=[pl.BlockSpec((G, tm, Kb), lambda i: (0, i, 0)), pl.BlockSpec((G, Kb, D), lambda i: (0, 0, 0)),
                  vec, row, vec, vec],
        out_specs=[row, row, row], out_shape=[_sds((S, D), F32), _sds((S, D), F32), _sds((S, D), CDT)],
        compiler_params=_params(("arbitrary",)), name=name)(u3, w3, bias, xin, gain, beta)


def _ffn_fwd(xin, xin_b, wgu, wdn, gain, beta, name):
    S = xin.shape[0]
    tm = _tile(S)

    def hidden(xb_ref, wgu_ref, gu_ref, hid_ref):
        xb = xb_ref[...]
        gate = _dot_nt(xb, wgu_ref[0])
        up = _dot_nt(xb, wgu_ref[1])
        gu_ref[0] = gate
        gu_ref[1] = up
        hid_ref[...] = (gate * _sigmoid(gate) * up).astype(CDT)

    gu, hid = _call(
        None, hidden, grid=(S // tm, 4),
        in_specs=[pl.BlockSpec((tm, D), lambda i, j: (i, 0)), pl.BlockSpec((2, None, FFN_B, D), lambda i, j: (0, j, 0, 0))],
        out_specs=[pl.BlockSpec((2, None, tm, FFN_B), lambda i, j: (0, j, i, 0)),
                   pl.BlockSpec((None, tm, FFN_B), lambda i, j: (j, i, 0))],
        out_shape=[_sds((2, 4, S, FFN_B), F32), _sds((4, S, FFN_B), CDT)],
        compiler_params=_params(("arbitrary", "arbitrary")), name=name + "_hidden")(xin_b, wgu)

    def down(x_ref, hid_ref, wdn_ref, g_ref, be_ref, z_ref, xo_ref, xob_ref):
        z = ALPHA * x_ref[...]
        for j in range(4):
            z = z + _dot(hid_ref[j], wdn_ref[j])
        z_ref[...] = z
        y = _ln_fwd(z, g_ref[...], be_ref[...])
        xo_ref[...] = y
        xob_ref[...] = y.astype(CDT)

    row = pl.BlockSpec((tm, D), lambda i: (i, 0))
    vec = pl.BlockSpec((1, D), lambda i: (0, 0))
    z, xo, xob = _call(
        None, down, grid=(S // tm,),
        in_specs=[row, pl.BlockSpec((4, tm, FFN_B), lambda i: (0, i, 0)), pl.BlockSpec((4, FFN_B, D), lambda i: (0, 0, 0)),
                  vec, vec],
        out_specs=[row, row, row], out_shape=[_sds((S, D), F32), _sds((S, D), F32), _sds((S, D), CDT)],
        compiler_params=_params(("arbitrary",)), name=name + "_down")(xin, hid, wdn, gain, beta)
    return gu, hid, z, xo, xob


def _ple_fwd(xin, xin_b, p_b, wpg, bgate, wpu, gain, beta, name):
    S = xin.shape[0]
    tm = _tile(S)

    def body(x_ref, xb_ref, p_ref, wpg_ref, bg_ref, wpu_ref, g_ref, be_ref, sg_ref, up_ref, z_ref, xo_ref, xob_ref):
        sg = _sigmoid(_dot(xb_ref[...], wpg_ref[...]) + bg_ref[...])
        pb = p_ref[...]
        up = jnp.concatenate([_dot(pb, wpu_ref[j]) for j in range(N_DEV)], axis=-1)
        sg_ref[...] = sg
        up_ref[...] = up
        z = ALPHA * x_ref[...] + sg * up
        z_ref[...] = z
        y = _ln_fwd(z, g_ref[...], be_ref[...])
        xo_ref[...] = y
        xob_ref[...] = y.astype(CDT)

    row = pl.BlockSpec((tm, D), lambda i: (i, 0))
    vec = pl.BlockSpec((1, D), lambda i: (0, 0))
    return _call(
        None, body, grid=(S // tm,),
        in_specs=[row, row, pl.BlockSpec((tm, PLE_DIM), lambda i: (i, 0)), pl.BlockSpec((D, D), lambda i: (0, 0)), vec,
                  pl.BlockSpec((N_DEV, PLE_DIM, D // N_DEV), lambda i: (0, 0, 0)), vec, vec],
        out_specs=[row] * 5,
        out_shape=[_sds((S, D), F32)] * 4 + [_sds((S, D), CDT)],
        compiler_params=_params(("arbitrary",)), name=name)(xin, xin_b, p_b, wpg, bgate, wpu, gain, beta)


def _loss_fwd_bwd(y, target):
    S = y.shape[0]
    tm = _tile(S)

    def body(y_ref, t_ref, l_ref, dy_ref):
        e = y_ref[...] - t_ref[...]
        dy_ref[...] = e * (1.0 / D)
        part = 0.5 * jnp.sum(jnp.sum(e * e, axis=-1, keepdims=True) * (1.0 / D), axis=0, keepdims=True)
        _acc(l_ref, jnp.broadcast_to(part, l_ref.shape), pl.program_id(0) == 0)

    row = pl.BlockSpec((tm, D), lambda i: (i, 0))
    return _call(
        None, body, grid=(S // tm,), in_specs=[row, row],
        out_specs=[pl.BlockSpec((1, 128), lambda i: (0, 0)), row],
        out_shape=[_sds((1, 128), F32), _sds((S, D), F32)],
        compiler_params=_params(("arbitrary",)), name="loss")(y, target)


def _ple_bwd(dy, z, sg, up, gain, wpg, name, after=None):
    S = dy.shape[0]
    tm = _tile(S)

    def body(dy_ref, z_ref, sg_ref, up_ref, g_ref, wpg_ref, dx_ref, dgl_ref, dup_ref, dgain_ref, dbeta_ref, dbg_ref):
        first = pl.program_id(0) == 0
        dy_ = dy_ref[...]
        dz, xhat = _ln_bwd(z_ref[...], g_ref[...], dy_)
        sg_ = sg_ref[...]
        dgl = dz * up_ref[...] * sg_ * (1.0 - sg_)
        dgl_ref[...] = dgl.astype(CDT)
        dup_ref[...] = (dz * sg_).astype(CDT)
        dx_ref[...] = ALPHA * dz + _dot_nt(dgl, wpg_ref[...])
        _acc(dgain_ref, _colsum(dy_ * xhat), first)
        _acc(dbeta_ref, _colsum(dy_), first)
        _acc(dbg_ref, _colsum(dgl), first)

    row = pl.BlockSpec((tm, D), lambda i: (i, 0))
    vec = pl.BlockSpec((1, D), lambda i: (0, 0))
    return _call(
        after, body, grid=(S // tm,), in_specs=[row, row, row, row, vec, pl.BlockSpec((D, D), lambda i: (0, 0))],
        out_specs=[row, row, row, vec, vec, vec],
        out_shape=[_sds((S, D), F32), _sds((S, D), CDT), _sds((S, D), CDT)] + [_sds((1, D), F32)] * 3,
        compiler_params=_params(("arbitrary",)), name=name)(dy, z, sg, up, gain, wpg)


def _ffn_bwd(dy, z, gu, wgu, wdn, gain, name, after=None):
    S = dy.shape[0]
    tm = _tile(S)

    def hidden(dy_ref, z_ref, gu_ref, wdn_ref, g_ref, dz_ref, dzb_ref, dgu_ref, dgain_ref, dbeta_ref):
        i, j = pl.program_id(0), pl.program_id(1)

        @pl.when(j == 0)
        def _():
            dy_ = dy_ref[...]
            dz, xhat = _ln_bwd(z_ref[...], g_ref[...], dy_)
            dz_ref[...] = dz
            dzb_ref[...] = dz.astype(CDT)
            _acc(dgain_ref, _colsum(dy_ * xhat), i == 0)
            _acc(dbeta_ref, _colsum(dy_), i == 0)

        dhid = _dot_nt(dzb_ref[...], wdn_ref[...])
        gate, up = gu_ref[0], gu_ref[1]
        sg = _sigmoid(gate)
        dgu_ref[0] = (dhid * up * (sg * (1.0 + gate * (1.0 - sg)))).astype(CDT)
        dgu_ref[1] = (dhid * (gate * sg)).astype(CDT)

    row = pl.BlockSpec((tm, D), lambda i, j: (i, 0))
    vec = pl.BlockSpec((1, D), lambda i, j: (0, 0))
    dz, dzb, dgu, dgain, dbeta = _call(
        after, hidden, grid=(S // tm, 4),
        in_specs=[row, row, pl.BlockSpec((2, None, tm, FFN_B), lambda i, j: (0, j, i, 0)),
                  pl.BlockSpec((None, FFN_B, D), lambda i, j: (j, 0, 0)), vec],
        out_specs=[row, row, pl.BlockSpec((2, None, tm, FFN_B), lambda i, j: (0, j, i, 0)), vec, vec],
        out_shape=[_sds((S, D), F32), _sds((S, D), CDT), _sds((2, 4, S, FFN_B), CDT), _sds((1, D), F32),
                   _sds((1, D), F32)],
        compiler_params=_params(("arbitrary", "arbitrary")), name=name + "_hidden")(dy, z, gu, wdn, gain)

    def to_input(dz_ref, dgu_ref, wgu_ref, dx_ref):
        acc = ALPHA * dz_ref[...]
        for g in range(2):
            for j in range(4):
                acc = acc + _dot(dgu_ref[g, j], wgu_ref[g, j])
        dx_ref[...] = acc

    rows = pl.BlockSpec((tm, D), lambda i: (i, 0))
    dx = _call(
        None, to_input, grid=(S // tm,),
        in_specs=[rows, pl.BlockSpec((2, 4, tm, FFN_B), lambda i: (0, 0, i, 0)),
                  pl.BlockSpec((2, 4, FFN_B, D), lambda i: (0, 0, 0, 0))],
        out_specs=rows, out_shape=_sds((S, D), F32),
        compiler_params=_params(("arbitrary",)), name=name + "_input")(dz, dgu, wgu)
    return dx, dzb, dgu, dgain, dbeta


def _mixout_bwd(dy, z, gain, w3, du_dtype, name, after=None):
    S = dy.shape[0]
    G, Kb, _ = w3.shape
    tm = _tile(S)

    def body(dy_ref, z_ref, g_ref, w_ref, dz_ref, dzb_ref, du_ref, dgain_ref, dbeta_ref, dbias_ref):
        first = pl.program_id(0) == 0
        dy_ = dy_ref[...]
        dz, xhat = _ln_bwd(z_ref[...], g_ref[...], dy_)
        dz_ref[...] = dz
        dzb = dz.astype(CDT)
        dzb_ref[...] = dzb
        for g in range(G):
            du_ref[g] = _dot_nt(dzb, w_ref[g]).astype(du_ref.dtype)
        _acc(dgain_ref, _colsum(dy_ * xhat), first)
        _acc(dbeta_ref, _colsum(dy_), first)
        _acc(dbias_ref, _colsum(dz), first)

    row = pl.BlockSpec((tm, D), lambda i: (i, 0))
    vec = pl.BlockSpec((1, D), lambda i: (0, 0))
    return _call(
        after, body, grid=(S // tm,), in_specs=[row, row, vec, pl.BlockSpec((G, Kb, D), lambda i: (0, 0, 0))],
        out_specs=[row, row, pl.BlockSpec((G, tm, Kb), lambda i: (0, i, 0)), vec, vec, vec],
        out_shape=[_sds((S, D), F32), _sds((S, D), CDT), _sds((G, S, Kb), du_dtype)] + [_sds((1, D), F32)] * 3,
        compiler_params=_params(("arbitrary",)), name=name)(dy, z, gain, w3)


def _half_select(low):
    r = lax.broadcasted_iota(jnp.int32, (2 * ATT_HD, ATT_HD), 0)
    c = lax.broadcasted_iota(jnp.int32, (2 * ATT_HD, ATT_HD), 1)
    return (r == c + (0 if low else ATT_HD)).astype(CDT)


def _half_place(low):
    r = lax.broadcasted_iota(jnp.int32, (ATT_HD, 2 * ATT_HD), 0)
    c = lax.broadcasted_iota(jnp.int32, (ATT_HD, 2 * ATT_HD), 1)
    return (c == r + (0 if low else ATT_HD)).astype(CDT)


def _pair_lanes(even, odd):
    return (jnp.dot(even, _half_place(True), preferred_element_type=F32)
            + jnp.dot(odd, _half_place(False), preferred_element_type=F32)).astype(CDT)


def _proj_heads(a, w, bias, heads, name):
    S, K = a.shape
    N = heads * ATT_HD
    tm = _tile(S)

    def body(a_ref, w_ref, b_ref, o_ref):
        acc = (_dot(a_ref[...], w_ref[...]) + b_ref[...]).astype(CDT)
        sel = (_half_select(True), _half_select(False))
        for h in range(heads):
            pair = acc[:, (h // 2) * 2 * ATT_HD:(h // 2 + 1) * 2 * ATT_HD]
            o_ref[h] = jnp.dot(pair, sel[h % 2], preferred_element_type=F32).astype(CDT)

    return _call(
        None, body, grid=(S // tm,),
        in_specs=[pl.BlockSpec((tm, K), lambda i: (i, 0)), pl.BlockSpec((K, N), lambda i: (0, 0)),
                  pl.BlockSpec((1, N), lambda i: (0, 0))],
        out_specs=pl.BlockSpec((heads, tm, ATT_HD), lambda i: (0, i, 0)), out_shape=_sds((heads, S, ATT_HD), CDT),
        compiler_params=_params(("arbitrary",)), name=name)(a, w, bias)


def _qkv_bwd(dz, dq, dkv4, wq, wkv, name, after=None):
    S = dz.shape[0]
    tm = _tile(S)
    HK = dkv4.shape[0]
    NK = HK * ATT_HD

    def body(dz_ref, dq_ref, dkv_ref, wq_ref, wkv_ref, dx_ref, dkvn_ref, dkvb_ref):
        first = pl.program_id(0) == 0
        dkvn = jnp.concatenate([_pair_lanes(dkv_ref[2 * i].astype(CDT), dkv_ref[2 * i + 1].astype(CDT))
                                for i in range(HK // 2)], axis=-1)
        dkvn_ref[...] = dkvn
        dx_ref[...] = ALPHA * dz_ref[...] + _dot_nt(dq_ref[...], wq_ref[...]) + _dot_nt(dkvn, wkv_ref[...])
        for h in range(HK):
            _acc(dkvb_ref.at[h], _colsum(dkv_ref[h]), first)

    row = pl.BlockSpec((tm, D), lambda i: (i, 0))
    return _call(
        after, body, grid=(S // tm,),
        in_specs=[row, row, pl.BlockSpec((HK, tm, ATT_HD), lambda i: (0, i, 0)),
                  pl.BlockSpec((D, D), lambda i: (0, 0)), pl.BlockSpec((D, NK), lambda i: (0, 0))],
        out_specs=[row, pl.BlockSpec((tm, NK), lambda i: (i, 0)), pl.BlockSpec((HK, 1, ATT_HD), lambda i: (0, 0, 0))],
        out_shape=[_sds((S, D), F32), _sds((S, NK), CDT), _sds((HK, 1, ATT_HD), F32)],
        compiler_params=_params(("arbitrary",)), name=name)(dz, dq, dkv4, wq, wkv)


def _inproj_bwd(dz, dproj, wain, name, after=None):
    S = dz.shape[0]
    tm = _tile(S)
    nb = wain.shape[-1]

    def body(dz_ref, dp_ref, w_ref, dx_ref):
        acc = ALPHA * dz_ref[...]
        for j in range(N_DEV):
            acc = acc + _dot_nt(dp_ref[j // 2, :, pl.ds((j % 2) * nb, nb)], w_ref[j])
        dx_ref[...] = acc

    row = pl.BlockSpec((tm, D), lambda i: (i, 0))
    return _call(
        after, body, grid=(S // tm,),
        in_specs=[row, pl.BlockSpec((4, tm, D), lambda i: (0, i, 0)), pl.BlockSpec((N_DEV, D, nb), lambda i: (0, 0, 0))],
        out_specs=row, out_shape=_sds((S, D), F32),
        compiler_params=_params(("arbitrary",)), name=name)(dz, dproj, wain)


def _running_sum(x, reverse=False):
    rows = x.shape[0]
    row = lax.broadcasted_iota(jnp.int32, x.shape, 0)
    step = 1
    while step < rows:
        if reverse:
            x = x + jnp.where(row < rows - step, pltpu.roll(x, rows - step, 0), 0.0)
        else:
            x = x + jnp.where(row >= step, pltpu.roll(x, step, 0), 0.0)
        step *= 2
    return x


def _hg_gates(q, f, alb_ref):
    a0, a1 = alb_ref[0:1, :], alb_ref[1:2, :]
    mx = jnp.maximum(a0, a1)
    e0, e1 = jnp.exp(a0 - mx), jnp.exp(a1 - mx)
    lb = e0 / (e0 + e1)
    sig = _sigmoid(f)
    forget = lb + (1.0 - lb) * sig
    k = (1.0 - lb) * _sigmoid(-f)
    qs = q * _sigmoid(q) * (HG_DK ** -0.5)
    return qs, k, jnp.log(forget), sig, lb, forget


def _hg_intra(qs, k, b, b_scr):
    b_scr[...] = b
    bm = b_scr[pl.ds(HG_CH // 2 - 1, 1), :]
    bl = b_scr[pl.ds(HG_CH - 1, 1), :]
    eb = jnp.exp(b)
    qb = qs * eb
    e_q = jnp.exp(b - bm)
    e_k = jnp.exp(bm - b)
    e_d = jnp.exp(bl - b)
    return qb, qs * e_q, k * e_k, k * e_d, jnp.exp(bl), eb, e_q, e_k, e_d


def _hgrn_fwd(proj, alb, ngain):
    S = proj.shape[1]
    nc = S // HG_CH
    nb = nc // HG_CPB
    rb, wb = HG_CPB * HG_CH, HG_HPB * HG_DK

    def body(pj_ref, alb_ref, ng_ref, o_ref, y_ref, st_ref, st_scr, b_scr):
        n = pl.program_id(1)

        @pl.when(n == 0)
        def _():
            st_scr[...] = jnp.zeros_like(st_scr)

        r = lax.broadcasted_iota(jnp.int32, (HG_CH, HG_CH), 0)
        c = lax.broadcasted_iota(jnp.int32, (HG_CH, HG_CH), 1)
        causal = r >= c
        for ci, j in [(ci, j) for ci in range(HG_CPB) for j in range(HG_HPB)]:
            rows, lanes = pl.ds(ci * HG_CH, HG_CH), pl.ds(j * HG_DK, HG_DK)
            q, f, v, g = pj_ref[0, rows, lanes], pj_ref[1, rows, lanes], pj_ref[2, rows, lanes], pj_ref[3, rows, lanes]
            qs, k, logf, _, _, _ = _hg_gates(q, f, alb_ref.at[:, lanes])
            b = _running_sum(logf)
            qb, qt, kt, kd, ebl, _, _, _, _ = _hg_intra(qs, k, b, b_scr.at[j, ci])
            st = st_scr[j]
            st_ref[j, ci] = st
            a = jnp.where(causal, _dot_nt(qt, kt), 0.0)
            o = _dot(a, v) + _dot_nt(qb, st)
            st_scr[j] = st * ebl + _dot_tn(v, kd)
            o_ref[rows, lanes] = o
            rinv = lax.rsqrt(jnp.mean(o * o, axis=-1, keepdims=True) + RMS_EPS)
            y_ref[rows, lanes] = (o * rinv * ng_ref[...] * (g * _sigmoid(g))).astype(CDT)

    blk = pl.BlockSpec((rb, wb), lambda h, n: (n, h))
    return _call(
        None, body, grid=(HG_H // HG_HPB, nb),
        in_specs=[pl.BlockSpec((4, rb, wb), lambda h, n: (0, n, h)), pl.BlockSpec((2, wb), lambda h, n: (0, h)),
                  pl.BlockSpec((1, HG_DK), lambda h, n: (0, 0))],
        out_specs=[blk, blk, pl.BlockSpec((HG_HPB, HG_CPB, HG_DK, HG_DK), lambda h, n: (h, n, 0, 0))],
        out_shape=[_sds((S, D), F32), _sds((S, D), CDT), _sds((HG_H, nc, HG_DK, HG_DK), F32)],
        scratch_shapes=[pltpu.VMEM((HG_HPB, HG_DK, HG_DK), F32), pltpu.VMEM((HG_HPB, HG_CPB, HG_CH, HG_DK), F32)],
        compiler_params=_params(("arbitrary", "arbitrary")), name="hgrn_fwd")(proj, alb, ngain)


def _hgrn_bwd(proj, alb, ngain, o, states, dy, after=None):
    S = proj.shape[1]
    nc = S // HG_CH
    nb = nc // HG_CPB
    rb, wb = HG_CPB * HG_CH, HG_HPB * HG_DK

    def body(pj_ref, alb_ref, ng_ref, o_ref, st_ref, dy_ref, dpj_ref, dalb_ref, dng_ref, dst_scr, b_scr):
        h, n = pl.program_id(0), pl.program_id(1)

        @pl.when(n == 0)
        def _():
            dst_scr[...] = jnp.zeros_like(dst_scr)
            dalb_ref[...] = jnp.zeros_like(dalb_ref)

        _zero_at(dng_ref, jnp.logical_and(h == 0, n == 0))
        ng = ng_ref[...]
        r = lax.broadcasted_iota(jnp.int32, (HG_CH, HG_CH), 0)
        c = lax.broadcasted_iota(jnp.int32, (HG_CH, HG_CH), 1)
        causal = r >= c
        dng = None
        for ci, j in [(ci, j) for ci in reversed(range(HG_CPB)) for j in range(HG_HPB)]:
            rows, lanes = pl.ds(ci * HG_CH, HG_CH), pl.ds(j * HG_DK, HG_DK)
            q, f, v, g = pj_ref[0, rows, lanes], pj_ref[1, rows, lanes], pj_ref[2, rows, lanes], pj_ref[3, rows, lanes]
            o_ = o_ref[rows, lanes]
            dy_ = dy_ref[rows, lanes]
            sg = _sigmoid(g)
            rinv = lax.rsqrt(jnp.mean(o_ * o_, axis=-1, keepdims=True) + RMS_EPS)
            nrm = o_ * rinv
            dr = dy_ * (g * sg)
            dg = dy_ * nrm * ng * (sg * (1.0 + g * (1.0 - sg)))
            dn = dr * ng
            do = rinv * (dn - nrm * jnp.mean(dn * nrm, axis=-1, keepdims=True))
            dng = _colsum(dr * nrm) if dng is None else dng + _colsum(dr * nrm)
            qs, k, logf, sig, lb, forget = _hg_gates(q, f, alb_ref.at[:, lanes])
            b = _running_sum(logf)
            qb, qt, kt, kd, ebl, eb, e_q, e_k, e_d = _hg_intra(qs, k, b, b_scr.at[j, ci])
            st = st_ref[j, ci]
            dstn = dst_scr[j]
            qt, kt, qb, kd = (t.astype(CDT).astype(F32) for t in (qt, kt, qb, kd))
            a = jnp.where(causal, _dot_nt(qt, kt), 0.0)
            da = jnp.where(causal, _dot_nt(do, v), 0.0)
            dv = _dot_tn(a, do) + _dot_nt(kd, dstn)
            dqb = _dot(do, st)
            dkd = _dot(v, dstn)
            dqt = _dot(da, kt)
            dkt = _dot_tn(da, qt)
            dbl = _colsum(dkd * kd) + ebl * _colsum(dstn * st)
            dst_scr[j] = dstn * ebl + _dot_tn(do, qb)
            dqs = dqt * e_q + dqb * eb
            dk = dkt * e_k + dkd * e_d
            db = dqt * qt + dqb * qb - dkt * kt - dkd * kd
            dlogf = _running_sum(db, reverse=True) + dbl
            dforget = dlogf / forget
            dsig = (1.0 - lb) * (dforget - dk)
            df = dsig * sig * (1.0 - sig)
            dlb = _colsum((dforget - dk) * (1.0 - sig))
            sq = _sigmoid(q)
            dq = dqs * (HG_DK ** -0.5) * (sq * (1.0 + q * (1.0 - sq)))
            dpj_ref[0, rows, lanes] = dq.astype(CDT)
            dpj_ref[1, rows, lanes] = df.astype(CDT)
            dpj_ref[2, rows, lanes] = dv.astype(CDT)
            dpj_ref[3, rows, lanes] = dg.astype(CDT)
            da0 = dlb * lb * (1.0 - lb)
            dalb_ref[pl.ds(0, 1), lanes] += da0
            dalb_ref[pl.ds(1, 1), lanes] -= da0
        dng_ref[...] += dng

    blk = pl.BlockSpec((rb, wb), lambda h, n: (nb - 1 - n, h))
    pj = pl.BlockSpec((4, rb, wb), lambda h, n: (0, nb - 1 - n, h))
    alb_blk = pl.BlockSpec((2, wb), lambda h, n: (0, h))
    ng_blk = pl.BlockSpec((1, HG_DK), lambda h, n: (0, 0))
    return _call(
        after, body, grid=(HG_H // HG_HPB, nb),
        in_specs=[pj, alb_blk, ng_blk, blk,
                  pl.BlockSpec((HG_HPB, HG_CPB, HG_DK, HG_DK), lambda h, n: (h, nb - 1 - n, 0, 0)), blk],
        out_specs=[pj, alb_blk, ng_blk],
        out_shape=[_sds((4, S, D), CDT), _sds((2, D), F32), _sds((1, HG_DK), F32)],
        scratch_shapes=[pltpu.VMEM((HG_HPB, HG_DK, HG_DK), F32), pltpu.VMEM((HG_HPB, HG_CPB, HG_CH, HG_DK), F32)],
        compiler_params=_params(("arbitrary", "arbitrary")), name="hgrn_bwd")(proj, alb, ngain, o, states, dy)


def _slope(h):
    return 2.0 ** (-8.0 * (h + 1) / ATT_QH)


def _attn_mask(n):
    qi = lax.broadcasted_iota(jnp.int32, (WINDOW, 2 * WINDOW), 0)
    si = lax.broadcasted_iota(jnp.int32, (WINDOW, 2 * WINDOW), 1)
    dist = qi - si + WINDOW
    valid = (dist >= 0) & (dist < WINDOW) & (n * WINDOW - WINDOW + si >= 0)
    return valid, dist.astype(F32)


def _attn_probs(qh, kh, sink, slope, valid, distf):
    s = _dot_nt(qh, kh) * (ATT_HD ** -0.5) - slope * distf
    s = jnp.where(valid, s, NEG)
    m = jnp.maximum(jnp.max(s, axis=-1, keepdims=True), sink)
    e = jnp.exp(s - m)
    es = jnp.exp(sink - m)
    inv = 1.0 / (jnp.sum(e, axis=-1, keepdims=True) + es)
    return e * inv, es * inv


def _attn_specs(S):
    nb = S // WINDOW
    cur = lambda H: pl.BlockSpec((H, WINDOW, ATT_HD), lambda n: (0, n, 0))
    prev = lambda H: pl.BlockSpec((H, WINDOW, ATT_HD), lambda n: (0, jnp.maximum(n - 1, 0), 0))
    return nb, cur, prev


def _attn_fwd(q4, kv4, sinks):
    S = q4.shape[1]
    nb, cur, prev = _attn_specs(S)

    def body(sink_ref, q_ref, kvc_ref, kvp_ref, o_ref):
        valid, distf = _attn_mask(pl.program_id(0))
        outs = []
        for h in range(ATT_QH):
            kvh = h // ATT_G
            kh = jnp.concatenate([kvp_ref[kvh], kvc_ref[kvh]], axis=0)
            vh = jnp.concatenate([kvp_ref[ATT_KVH + kvh], kvc_ref[ATT_KVH + kvh]], axis=0)
            p, _ = _attn_probs(q_ref[h], kh, sink_ref[0, h], _slope(h), valid, distf)
            outs.append(_dot(p, vh).astype(CDT))
            if h % 2:
                o_ref[:, pl.ds((h - 1) * ATT_HD, 2 * ATT_HD)] = _pair_lanes(outs[h - 1], outs[h])

    return _call(
        None, body, grid=(nb,),
        in_specs=[pl.BlockSpec(memory_space=pltpu.SMEM), cur(ATT_QH), cur(2 * ATT_KVH), prev(2 * ATT_KVH)],
        out_specs=pl.BlockSpec((WINDOW, D), lambda n: (n, 0)), out_shape=_sds((S, D), CDT),
        compiler_params=_params(("arbitrary",)), name="attn_fwd")(sinks, q4, kv4, kv4)


def _attn_bwd(q4, kv4, sinks, do):
    S = q4.shape[1]
    nb, cur, prev = _attn_specs(S)

    def body(sink_ref, q_ref, kvc_ref, kvp_ref, do_ref, dq_ref, dkv_ref, dbq_ref, dsink_ref):
        n = pl.program_id(0)
        first = n == 0

        @pl.when(first)
        def _():
            dkv_ref[...] = jnp.zeros_like(dkv_ref)
            dsink_ref[...] = jnp.zeros_like(dsink_ref)
            dbq_ref[...] = jnp.zeros_like(dbq_ref)

        valid, distf = _attn_mask(n)
        lane = lax.broadcasted_iota(jnp.int32, (1, 128), 1)
        rows_cur = pl.ds(pl.multiple_of(n * WINDOW, WINDOW), WINDOW)
        rows_prev = pl.ds(pl.multiple_of(jnp.maximum(n - 1, 0) * WINDOW, WINDOW), WINDOW)
        dsinks = jnp.zeros((1, 128), F32)
        sel = (_half_select(True), _half_select(False))
        for kvh in range(ATT_KVH):
            kh = jnp.concatenate([kvp_ref[kvh], kvc_ref[kvh]], axis=0)
            vh = jnp.concatenate([kvp_ref[ATT_KVH + kvh], kvc_ref[ATT_KVH + kvh]], axis=0)
            dk = dv = None
            dqs = []
            for h in range(kvh * ATT_G, (kvh + 1) * ATT_G):
                qh = q_ref[h]
                doh = jnp.dot(do_ref[:, pl.ds((h // 2) * 2 * ATT_HD, 2 * ATT_HD)], sel[h % 2],
                              preferred_element_type=F32).astype(CDT)
                p, ps = _attn_probs(qh, kh, sink_ref[0, h], _slope(h), valid, distf)
                dp = _dot_nt(doh, vh)
                dd = jnp.sum(p * dp, axis=-1, keepdims=True)
                ds = p * (dp - dd)
                dsinks = dsinks + jnp.where(lane == h, -jnp.sum(ps * dd, axis=0, keepdims=True), 0.0)
                dqh = _dot(ds, kh) * (ATT_HD ** -0.5)
                dqs.append(dqh.astype(CDT))
                dbq_ref[h] += _colsum(dqh)
                dkh = _dot_tn(ds, qh) * (ATT_HD ** -0.5)
                dvh = _dot_tn(p, doh)
                dk = dkh if dk is None else dk + dkh
                dv = dvh if dv is None else dv + dvh
            for i in range(ATT_G // 2):
                lanes = pl.ds((kvh * ATT_G + 2 * i) * ATT_HD, 2 * ATT_HD)
                dq_ref[:, lanes] = _pair_lanes(dqs[2 * i], dqs[2 * i + 1])
            dkv_ref[kvh, rows_prev, :] += dk[:WINDOW]
            dkv_ref[kvh, rows_cur, :] += dk[WINDOW:]
            dkv_ref[ATT_KVH + kvh, rows_prev, :] += dv[:WINDOW]
            dkv_ref[ATT_KVH + kvh, rows_cur, :] += dv[WINDOW:]
        dsink_ref[...] += dsinks

    return _call(
        None, body, grid=(nb,),
        in_specs=[pl.BlockSpec(memory_space=pltpu.SMEM), cur(ATT_QH), cur(2 * ATT_KVH), prev(2 * ATT_KVH),
                  pl.BlockSpec((WINDOW, D), lambda n: (n, 0))],
        out_specs=[pl.BlockSpec((WINDOW, D), lambda n: (n, 0)), pl.BlockSpec((2 * ATT_KVH, S, ATT_HD), lambda n: (0, 0, 0)),
                   pl.BlockSpec((ATT_QH, 1, ATT_HD), lambda n: (0, 0, 0)), pl.BlockSpec((1, 128), lambda n: (0, 0))],
        out_shape=[_sds((S, D), CDT), _sds((2 * ATT_KVH, S, ATT_HD), F32), _sds((ATT_QH, 1, ATT_HD), F32),
                   _sds((1, 128), F32)],
        compiler_params=_params(("arbitrary",)), name="attn_bwd")(sinks, q4, kv4, kv4, do)


def _local_step(x, p, target, getw, sm, emit):
    S = x.shape[0]
    vec = lambda a: a.reshape(1, -1)
    ln_g = lambda l, k: vec(sm["ln_gain"][l, k])
    ln_b = lambda l, k: vec(sm["ln_bias"][l, k])
    xb = x.astype(CDT)
    pb = p.astype(CDT)

    proj = _mm_nn(xb, getw("a_w_in", None), (4, S, D), (None, _tile(S), 512), lambda g, i: (g // 2, i, g % 2), F32,
                  name="a_in")
    o_a, y_a, states = _hgrn_fwd(proj, sm["a_lower_bound"], sm["a_norm_gain"])
    zeros = jnp.zeros((1, D), F32)
    z = [[None] * 3 for _ in range(2)]
    xs = [[None] * 3 for _ in range(2)]
    xbs = [[None] * 3 for _ in range(2)]
    z[0][0], xs[0][0], xbs[0][0] = _mixout_ln(y_a[None], getw("a_w_out", y_a)[None], zeros, x, ln_g(0, 0), ln_b(0, 0),
                                              "a_out_ln")
    gu, hid, sgs, ups = [None, None], [None, None], [None, None], [None, None]

    def ffn_ple(l):
        wgu = getw(f"gu{l}", xbs[l][0])
        gu[l], hid[l], z[l][1], xs[l][1], xbs[l][1] = _ffn_fwd(
            xs[l][0], xbs[l][0], wgu, getw(f"dn{l}", None), ln_g(l, 1), ln_b(l, 1), f"ffn_fwd{l}")
        sgs[l], ups[l], z[l][2], xs[l][2], xbs[l][2] = _ple_fwd(
            xs[l][1], xbs[l][1], pb[l], getw(f"pg{l}", None), vec(sm["ple_b_gate"][l]), getw(f"pu{l}", None), ln_g(l, 2),
            ln_b(l, 2), f"ple_fwd{l}")

    ffn_ple(0)
    x3, x3b = xs[0][2], xbs[0][2]
    w_kv, w_q, w_bo = getw("kv_w", x3b), getw("b_w_q", None), getw("b_w_out", None)
    kv4 = _proj_heads(x3b, w_kv, vec(sm["kv_b"]), 2 * ATT_KVH, "kv_proj")
    q4 = _proj_heads(x3b, w_q, vec(sm["b_b_q"]), ATT_QH, "q_proj")
    o_b = _attn_fwd(q4, kv4, sm["b_sinks"])
    z[1][0], xs[1][0], xbs[1][0] = _mixout_ln(o_b[None], w_bo[None], sm["b_b_out"], x3, ln_g(1, 0), ln_b(1, 0),
                                              "b_out_ln")
    ffn_ple(1)
    loss, dy = _loss_fwd_bwd(xs[1][2], target)

    gs = {}
    d_ln_g = [[None] * 3 for _ in range(2)]
    d_ln_b = [[None] * 3 for _ in range(2)]
    g_bg = [None, None]

    def ffn_ple_bwd(l, dy, after=None):
        dx2, dgl, dup, d_ln_g[l][2], d_ln_b[l][2], g_bg[l] = _ple_bwd(dy, z[l][2], sgs[l], ups[l], ln_g(l, 2),
                                                                     getw(f"pg{l}", None), f"ple_bwd{l}", after=after)
        g_pg = _wgrad(xbs[l][1][None], dgl[None], f"g_ple_gate{l}")[0]
        g_pu = _wgrad(pb[l][None], dup[None], f"g_ple_up{l}")[0]
        dx1, dzb, dgu, d_ln_g[l][1], d_ln_b[l][1] = _ffn_bwd(dx2, z[l][1], gu[l], getw(f"gu{l}", None),
                                                           getw(f"dn{l}", None), ln_g(l, 1), f"ffn_bwd{l}")
        g_dn = _wgrad(hid[l], dzb[None], f"g_ffn_down{l}")
        g_gu = _wgrad(dgu.reshape(8, S, FFN_B), xbs[l][0][None], f"g_ffn_gate_up{l}")
        return dx1, emit({f"pg{l}": g_pg, f"pu{l}": g_pu, f"dn{l}": g_dn, f"gu{l}": g_gu})

    dx1, tok = ffn_ple_bwd(1, dy)
    dz, dzb, do, d_ln_g[1][0], d_ln_b[1][0], gs["b_b_out"] = _mixout_bwd(dx1, z[1][0], ln_g(1, 0), w_bo[None], CDT,
                                                                        "b_out_bwd", after=tok)
    g_bo = _wgrad(o_b[None], dzb[None], "g_b_w_out")[0]
    dq, dkv4, dbq, dsinks = _attn_bwd(q4, kv4, sm["b_sinks"], do[0])
    gs["b_b_q"] = dbq
    gs["b_sinks"] = dsinks
    g_q = _wgrad(x3b[None], dq[None], "g_b_w_q")[0]
    dx3, dkv, gs["kv_b"] = _qkv_bwd(dz, dq, dkv4, w_q, w_kv, "qkv_bwd", after=tok)
    g_kv = _wgrad(x3b[None], dkv[None], "g_kv_w")[0]
    tok = emit({"b_w_out": g_bo, "b_w_q": g_q, "kv_w": g_kv})
    dx1, tok = ffn_ple_bwd(0, dx3, tok)
    w_ao = getw("a_w_out", None)
    dz, dzb, dyr, d_ln_g[0][0], d_ln_b[0][0], _ = _mixout_bwd(dx1, z[0][0], ln_g(0, 0), w_ao[None], F32, "a_out_bwd",
                                                              after=tok)
    g_ao = _wgrad(y_a[None], dzb[None], "g_a_w_out")[0]
    tok = emit({"a_w_out": g_ao})
    dproj, gs["a_lower_bound"], gs["a_norm_gain"] = _hgrn_bwd(proj, sm["a_lower_bound"], sm["a_norm_gain"], o_a, states,
                                                              dyr[0], after=tok)
    tk = lambda t: (None, t, D)
    g_ain = _mm_tn(xb[None], dproj, N_DEV, lambda g, k: (0, k, 0), lambda g, k: (g // 2, k, g % 2),
                   tk, lambda t: (None, t, 512), (N_DEV, D, 512), (None, D, 512), lambda g, k: (g, 0, 0), name="g_a_w_in")
    gs["ple_b_gate"] = jnp.concatenate(g_bg, axis=0)
    gs["ln_gain"] = jnp.stack([jnp.concatenate(r, axis=0) for r in d_ln_g])
    gs["ln_bias"] = jnp.stack([jnp.concatenate(r, axis=0) for r in d_ln_b])
    gs["loss"] = loss
    tok = emit({"a_w_in": g_ain}, small=gs)
    grad_x = _inproj_bwd(dz, dproj, getw("a_w_in", None), "a_in_bwd", after=tok)
    return loss, grad_x, gs


def _peer(k):
    x, y, c = lax.axis_index("x"), lax.axis_index("y"), lax.axis_index("c")
    px = 1 - x if k & 4 else x
    py = 1 - y if k & 2 else y
    pc = 1 - c if k & 1 else c
    return (px, py, pc), 4 * px + 2 * py + pc


def _my_index():
    return 4 * lax.axis_index("x") + 2 * lax.axis_index("y") + lax.axis_index("c")


def _exchange(srcs, dst_shapes, plan, name):
    n_src, n_piece = len(srcs), len(plan)

    def body(*refs):
        src_refs, dst_refs = refs[:n_src], refs[n_src:n_src + len(dst_shapes)]
        send_sems, recv_sems, local_sems = refs[n_src + len(dst_shapes):]
        me = _my_index()

        def at(ref, idx):
            return ref.at[idx] if idx else ref

        local = []
        for t, (si, sfn, di, dfn) in enumerate(plan):
            cp = pltpu.make_async_copy(at(src_refs[si], sfn(me)), at(dst_refs[di], dfn(me)), local_sems.at[t])
            cp.start()
            local.append(cp)
        sends = []
        for k in range(1, N_DEV):
            peer, pid = _peer(k)
            for t, (si, sfn, di, dfn) in enumerate(plan):
                cp = pltpu.make_async_remote_copy(
                    src_ref=at(src_refs[si], sfn(pid)), dst_ref=at(dst_refs[di], dfn(me)),
                    send_sem=send_sems.at[t * 7 + k - 1], recv_sem=recv_sems.at[t * 7 + k - 1],
                    device_id=peer, device_id_type=MESH)
                cp.start()
                sends.append(cp)
        for k in range(1, N_DEV):
            peer, pid = _peer(k)
            for t, (si, sfn, di, dfn) in enumerate(plan):
                pltpu.make_async_remote_copy(
                    src_ref=at(src_refs[si], sfn(me)), dst_ref=at(dst_refs[di], dfn(pid)),
                    send_sem=send_sems.at[t * 7 + k - 1], recv_sem=recv_sems.at[t * 7 + k - 1],
                    device_id=peer, device_id_type=MESH).wait_recv()
        for cp in sends:
            cp.wait_send()
        for cp in local:
            cp.wait()

    hbm = pl.BlockSpec(memory_space=pltpu.HBM)
    return _call(
        None, body, in_specs=[hbm] * n_src, out_specs=[hbm] * len(dst_shapes), out_shape=dst_shapes,
        scratch_shapes=[pltpu.SemaphoreType.DMA((7 * n_piece,)), pltpu.SemaphoreType.DMA((7 * n_piece,)),
                        pltpu.SemaphoreType.DMA((n_piece,))],
        name=name)(*srcs)


def _gather(shards, name):
    dsts = [_sds((N_DEV,) + a.shape, a.dtype) for a in shards]
    plan = [(i, lambda j: (), i, lambda s: (s,)) for i in range(len(shards))]
    return _exchange(shards, dsts, plan, name)


_HBM = pl.BlockSpec(memory_space=pltpu.HBM)
_SEM = pl.BlockSpec(memory_space=pltpu.SEMAPHORE)
_DATAFLOW = pltpu.SideEffectType.DATAFLOW_SIDE_EFFECTING


def _piece_copy(mode, src, land, send_sems, recv_sems, t, k, sender, receiver, peer):
    return pltpu.make_async_remote_copy(
        src_ref=src if mode == "gather" else src.at[receiver], dst_ref=land.at[sender],
        send_sem=send_sems.at[t * 7 + k - 1], recv_sem=recv_sems.at[t * 7 + k - 1], device_id=peer, device_id_type=MESH)


def _sequencer_exchange(srcs, modes, name, collective_id, after=None):
    n = len(srcs)
    land_shapes = [((N_DEV,) + a.shape) if mode == "gather" else a.shape for a, mode in zip(srcs, modes)]
    extra = [] if after is None else [after]

    def body(*refs):
        src_refs, land_refs = refs[:n], refs[n + len(extra):2 * n + len(extra)]
        send_sems, recv_sems, local_sems = refs[2 * n + len(extra):]
        barrier = pltpu.get_barrier_semaphore()
        for k in range(1, N_DEV):
            pl.semaphore_signal(barrier, inc=1, device_id=_peer(k)[0], device_id_type=MESH)
        pl.semaphore_wait(barrier, N_DEV - 1)
        me = _my_index()
        local = []
        for i in range(n):
            cp = pltpu.make_async_copy(src_refs[i] if modes[i] == "gather" else src_refs[i].at[me], land_refs[i].at[me],
                                       local_sems.at[i])
            cp.start()
            local.append(cp)
        for k in range(1, N_DEV):
            peer, pid = _peer(k)
            for t in range(n):
                _piece_copy(modes[t], src_refs[t], land_refs[t], send_sems, recv_sems, t, k, me, pid, peer).start()
        for k in range(1, N_DEV):
            peer, pid = _peer(k)
            for t in range(n):
                _piece_copy(modes[t], src_refs[t], land_refs[t], send_sems, recv_sems, t, k, pid, me, peer).wait_recv()
        for k in range(1, N_DEV):
            peer, pid = _peer(k)
            for t in range(n):
                _piece_copy(modes[t], src_refs[t], land_refs[t], send_sems, recv_sems, t, k, me, pid, peer).wait_send()
        for cp in local:
            cp.wait()

    return pl.kernel(
        body, out_type=[_sds(s, a.dtype) for s, a in zip(land_shapes, srcs)],
        mesh=plsc.ScalarSubcoreMesh(axis_name="sequencer", num_cores=1),
        scratch_types=[pltpu.SemaphoreType.DMA((7 * n,)), pltpu.SemaphoreType.DMA((7 * n,)), pltpu.SemaphoreType.DMA((n,))],
        compiler_params=pltpu.CompilerParams(collective_id=collective_id), name=name)(*srcs, *extra)


def _sequencer_sibling_swap(blocks, name, collective_id):
    n = len(blocks)

    def body(*refs):
        srcs, theirs = refs[:n], refs[n:2 * n]
        send_sems, recv_sems = refs[2 * n:]
        x, y, c = lax.axis_index("x"), lax.axis_index("y"), lax.axis_index("c")
        sibling = (x, y, 1 - c)
        barrier = pltpu.get_barrier_semaphore()
        pl.semaphore_signal(barrier, inc=1, device_id=sibling, device_id_type=MESH)
        pl.semaphore_wait(barrier, 1)
        copies = []
        for t in range(n):
            for q in range(4):
                copies.append(pltpu.make_async_remote_copy(
                    src_ref=srcs[t].at[2 * q + (1 - c)], dst_ref=theirs[t].at[q], send_sem=send_sems.at[4 * t + q],
                    recv_sem=recv_sems.at[4 * t + q], device_id=sibling, device_id_type=MESH))
        for cp in copies:
            cp.start()
        for cp in copies:
            cp.wait_recv()
        for cp in copies:
            cp.wait_send()

    return pl.kernel(
        body, out_type=[_sds((4,) + a.shape[1:], a.dtype) for a in blocks],
        mesh=plsc.ScalarSubcoreMesh(axis_name="sequencer", num_cores=1),
        scratch_types=[pltpu.SemaphoreType.DMA((4 * n,)), pltpu.SemaphoreType.DMA((4 * n,))],
        compiler_params=pltpu.CompilerParams(collective_id=collective_id), name=name)(*blocks)


def _pair_sums(blocks, theirs, name):
    n = len(blocks)

    def body(*refs):
        c = lax.axis_index("c")
        for t in range(n):
            mine = jnp.where(c == 0, refs[3 * t][...].astype(F32), refs[3 * t + 1][...].astype(F32))
            refs[3 * n + t][...] = (mine + refs[3 * t + 2][...].astype(F32)).astype(CDT)

    in_specs, out_specs, steps = [], [], 1
    for a in blocks:
        _, R, C = a.shape
        tr = _tile(R, (256, 128, 176, 64, 32, 16))
        row = lambda q, i, nr=R // tr: jnp.minimum(i, nr - 1)
        in_specs += [pl.BlockSpec((None, None, tr, C), lambda q, i, row=row: (q, 0, row(q, i), 0)),
                     pl.BlockSpec((None, None, tr, C), lambda q, i, row=row: (q, 1, row(q, i), 0)),
                     pl.BlockSpec((None, tr, C), lambda q, i, row=row: (q, row(q, i), 0))]
        out_specs.append(pl.BlockSpec((None, tr, C), lambda q, i, row=row: (q, row(q, i), 0)))
        steps = max(steps, R // tr)
    args = []
    for a, th in zip(blocks, theirs):
        by_core = a.reshape((4, 2) + a.shape[1:])
        args += [by_core, by_core, th]
    return _call(
        None, body, grid=(4, steps), in_specs=in_specs, out_specs=out_specs,
        out_shape=[_sds(th.shape, CDT) for th in theirs],
        compiler_params=_params(("arbitrary", "arbitrary")), name=name)(*args)


def _sequencer_chip_scatter(pairs, name, collective_id):
    n = len(pairs)

    def body(*refs):
        srcs, lands = refs[:n], refs[n:2 * n]
        send_sems, recv_sems = refs[2 * n:]
        x, y, c = lax.axis_index("x"), lax.axis_index("y"), lax.axis_index("c")
        chips = [(1 - x, y), (x, 1 - y), (1 - x, 1 - y)]
        chip = lambda px, py: 2 * px + py
        barrier = pltpu.get_barrier_semaphore()
        for px, py in chips:
            pl.semaphore_signal(barrier, inc=1, device_id=(px, py, c), device_id_type=MESH)
        pl.semaphore_wait(barrier, 3)
        here = chip(x, y)
        sends = []
        for t in range(n):
            for j, (px, py) in enumerate(chips):
                sends.append(pltpu.make_async_remote_copy(
                    src_ref=srcs[t].at[chip(px, py)], dst_ref=lands[t].at[here], send_sem=send_sems.at[3 * t + j],
                    recv_sem=recv_sems.at[3 * t + j], device_id=(px, py, c), device_id_type=MESH))
        for cp in sends:
            cp.start()
        for t in range(n):
            for j, (px, py) in enumerate(chips):
                pltpu.make_async_remote_copy(
                    src_ref=srcs[t].at[here], dst_ref=lands[t].at[chip(px, py)], send_sem=send_sems.at[3 * t + j],
                    recv_sem=recv_sems.at[3 * t + j], device_id=(px, py, c), device_id_type=MESH).wait_recv()
        for cp in sends:
            cp.wait_send()

    return pl.kernel(
        body, out_type=[_sds(a.shape, a.dtype) for a in pairs],
        mesh=plsc.ScalarSubcoreMesh(axis_name="sequencer", num_cores=1),
        scratch_types=[pltpu.SemaphoreType.DMA((3 * n,)), pltpu.SemaphoreType.DMA((3 * n,))],
        compiler_params=pltpu.CompilerParams(collective_id=collective_id), name=name)(*pairs)


def _sequencer_gather(srcs, name, collective_id, after=None):
    n = len(srcs)
    extra = [] if after is None else [after]

    def body(*refs):
        src_refs, land_refs = refs[:n], refs[n + len(extra):2 * n + len(extra)]
        send_sems, recv_sems, local_sems = refs[2 * n + len(extra):]
        x, y, c = lax.axis_index("x"), lax.axis_index("y"), lax.axis_index("c")
        sibling = (x, y, 1 - c)
        chips = [(1 - x, y), (x, 1 - y), (1 - x, 1 - y)]
        index = lambda px, py, pc: 4 * px + 2 * py + pc
        barrier = pltpu.get_barrier_semaphore()
        for peer in [sibling] + [(*chip, c) for chip in chips]:
            pl.semaphore_signal(barrier, inc=1, device_id=peer, device_id_type=MESH)
        pl.semaphore_wait(barrier, 4)

        def copy(t, k, slot, to, src=None):
            return pltpu.make_async_remote_copy(
                src_ref=land_refs[t].at[slot] if src is None else src, dst_ref=land_refs[t].at[slot],
                send_sem=send_sems.at[7 * t + k], recv_sem=recv_sems.at[7 * t + k], device_id=to, device_id_type=MESH)

        me = index(x, y, c)
        local = []
        for t in range(n):
            cp = pltpu.make_async_copy(src_refs[t], land_refs[t].at[me], local_sems.at[t])
            cp.start()
            local.append(cp)
        sends = []
        for t in range(n):
            sends.append(copy(t, 0, me, sibling, src=src_refs[t]))
            sends += [copy(t, 1 + j, me, (*chip, c), src=src_refs[t]) for j, chip in enumerate(chips)]
        for cp in sends:
            cp.start()
        for j, chip in enumerate(chips):
            for t in range(n):
                copy(t, 1 + j, index(*chip, c), sibling, src=src_refs[t]).wait_recv()
                passed = copy(t, 4 + j, index(*chip, c), sibling)
                passed.start()
                sends.append(passed)
        for t in range(n):
            copy(t, 0, index(x, y, 1 - c), sibling, src=src_refs[t]).wait_recv()
        for j, chip in enumerate(chips):
            for t in range(n):
                copy(t, 4 + j, index(*chip, 1 - c), sibling, src=src_refs[t]).wait_recv()
        for cp in sends:
            cp.wait_send()
        for cp in local:
            cp.wait()

    return pl.kernel(
        body, out_type=[_sds((N_DEV,) + a.shape, a.dtype) for a in srcs],
        mesh=plsc.ScalarSubcoreMesh(axis_name="sequencer", num_cores=1),
        scratch_types=[pltpu.SemaphoreType.DMA((7 * n,)), pltpu.SemaphoreType.DMA((7 * n,)), pltpu.SemaphoreType.DMA((n,))],
        compiler_params=pltpu.CompilerParams(collective_id=collective_id), name=name)(*srcs, *extra)


def _xstart(groups, mode, name, after=None):
    flat = [a for g in groups for a in g]
    n, ng = len(flat), len(groups)
    land_shapes = [((N_DEV,) + a.shape) if mode == "gather" else a.shape for a in flat]
    first = [sum(len(g) for g in groups[:i]) for i in range(ng)]
    extra = [] if after is None else [after]

    def body(*refs):
        srcs, lands = refs[:n], refs[n:2 * n]
        sems = refs[2 * n + len(extra):2 * n + len(extra) + 2 * ng]
        tok_ref, local_sems = refs[-2], refs[-1]
        me = _my_index()
        local = []
        for i in range(n):
            cp = pltpu.make_async_copy(srcs[i] if mode == "gather" else srcs[i].at[me], lands[i].at[me], local_sems.at[i])
            cp.start()
            local.append(cp)
        for cp in local:
            cp.wait()
        for gi, g in enumerate(groups):
            for k in range(1, N_DEV):
                peer, pid = _peer(k)
                for t in range(len(g)):
                    i = first[gi] + t
                    _piece_copy(mode, srcs[i], lands[i], sems[2 * gi], sems[2 * gi + 1], t, k, me, pid, peer).start()
        tok_ref[...] = jnp.zeros_like(tok_ref)

    sem_shapes = []
    for g in groups:
        sem_shapes += [pltpu.SemaphoreType.DMA((7 * len(g),))] * 2
    thru = [pltpu.HBM(a.shape, a.dtype) for a in flat] + [pltpu.HBM(s, a.dtype) for s, a in zip(land_shapes, flat)]
    outs = pl.pallas_call(
        body, in_specs=[_HBM] * (2 * n) + [pl.BlockSpec(memory_space=pl.ANY)] * len(extra),
        out_specs=[_SEM] * (2 * ng) + [_HBM] * (2 * n) + [pl.BlockSpec(memory_space=pltpu.VMEM)],
        out_shape=sem_shapes + thru + [_sds((8, 128), F32)],
        input_output_aliases={i: 2 * ng + i for i in range(2 * n)},
        scratch_shapes=[pltpu.SemaphoreType.DMA((n,))],
        compiler_params=pltpu.CompilerParams(has_side_effects=_DATAFLOW), name=name)(
            *[pltpu.with_memory_space_constraint(a, pltpu.HBM) for a in flat],
            *[pltpu.with_memory_space_constraint(lax.empty(s, a.dtype), pltpu.HBM) for s, a in zip(land_shapes, flat)], *extra)
    sems, srcs_thru, lands_thru = outs[:2 * ng], outs[2 * ng:2 * ng + n], outs[2 * ng + n:2 * ng + 2 * n]
    handles = [(sems[2 * gi], sems[2 * gi + 1], srcs_thru[first[gi]:first[gi] + len(g)],
                lands_thru[first[gi]:first[gi] + len(g)]) for gi, g in enumerate(groups)]
    return handles, outs[-1]


def _xwait(handle, mode, after, name):
    send_sems, recv_sems, srcs_thru, lands_thru = handle
    n = len(srcs_thru)

    def body(*refs):
        srcs, lands, send, recv = refs[:n], refs[n:2 * n], refs[2 * n], refs[2 * n + 1]
        me = _my_index()
        for k in range(1, N_DEV):
            peer, pid = _peer(k)
            for t in range(n):
                _piece_copy(mode, srcs[t], lands[t], send, recv, t, k, pid, me, peer).wait_recv()
        for k in range(1, N_DEV):
            peer, pid = _peer(k)
            for t in range(n):
                _piece_copy(mode, srcs[t], lands[t], send, recv, t, k, me, pid, peer).wait_send()

    extra = [] if after is None else [after]
    outs = pl.pallas_call(
        body, in_specs=[_HBM] * (2 * n) + [_SEM, _SEM] + [pl.BlockSpec(memory_space=pl.ANY)] * len(extra),
        out_specs=[_HBM] * (2 * n),
        out_shape=[pltpu.HBM(a.shape, a.dtype) for a in list(srcs_thru) + list(lands_thru)],
        input_output_aliases={i: i for i in range(2 * n)},
        compiler_params=pltpu.CompilerParams(has_side_effects=_DATAFLOW), name=name)(
            *srcs_thru, *lands_thru, send_sems, recv_sems, *extra)
    return outs[n:]


def _adamw(w, g, m, v):
    m = ADAM_B1 * m + (1.0 - ADAM_B1) * g
    v = ADAM_B2 * v + (1.0 - ADAM_B2) * (g * g)
    m_hat = m / (1.0 - ADAM_B1 ** ADAM_STEP)
    v_hat = v / (1.0 - ADAM_B2 ** ADAM_STEP)
    delta = -ADAM_LR * (m_hat / (jnp.sqrt(v_hat) + ADAM_EPS) + ADAM_WD * w)
    return delta, m, v


def _adam_big(w, parts, own, m, v, name, after=None):
    L, R, C = w.shape
    P = parts[0].shape[0]
    tr = _tile(R, (256, 128, 176, 64, 32, 16))
    nr = R // tr

    def body(w_ref, *refs):
        p_refs, o_refs, (m_ref, v_ref, g_ref, d_ref, mo_ref, vo_ref) = refs[:L], refs[L:2 * L], refs[2 * L:]
        here = 2 * lax.axis_index("x") + lax.axis_index("y")
        for l in range(L):
            @pl.when(pl.program_id(0) == l)
            def _(p_ref=p_refs[l], o_ref=o_refs[l]):
                g = jnp.where(here == 0, o_ref[0], p_ref[0]).astype(F32)
                for s in range(1, P):
                    g = g + jnp.where(here == s, o_ref[s], p_ref[s]).astype(F32)
                g_ref[...] = g
                d_ref[...], mo_ref[...], vo_ref[...] = _adamw(w_ref[...], g, m_ref[...], v_ref[...])

    row = pl.BlockSpec((None, tr, C), lambda l, i: (l, i, 0))
    park = lambda l_of: (lambda l, i: (0, jnp.where(l == l_of, i, 0 if l_of else nr - 1), 0))
    return _call(
        after, body, grid=(L, nr),
        in_specs=[row] + [pl.BlockSpec((P, tr, C), park(l)) for l in range(L)] * 2 + [row, row],
        out_specs=[row] * 4, out_shape=[_sds((L, R, C), F32)] * 4,
        compiler_params=_params(("arbitrary", "arbitrary")), name=name)(w, *parts, *own, m, v)


SMALL = (("a_lower_bound", (2, 128), (2, D)), ("ln_gain", (6, 128), (6, D)), ("ln_bias", (6, 128), (6, D)),
         ("a_norm_gain", (1, 128), (1, 128)), ("kv_b", (1, 512), (1, 512)), ("b_b_q", (1, D), (1, D)),
         ("b_sinks", (1, ATT_QH), (1, 128)), ("b_b_out", (1, D), (1, D)), ("ple_b_gate", (2, D), (2, D)))


def _adam_small(parts, w, m, v, losses, after=None):
    k = len(SMALL)

    def body(*refs):
        p_refs, w_refs, m_refs, v_refs = refs[:k], refs[k:2 * k], refs[2 * k:3 * k], refs[3 * k:4 * k]
        loss_ref, outs, total_ref = refs[4 * k], refs[4 * k + 1:-1], refs[-1]
        total = loss_ref[0]
        for s in range(1, N_DEV):
            total = total + loss_ref[s]
        total_ref[...] = total
        me = _my_index()
        for i, (_, wshape, pshape) in enumerate(SMALL):
            cols = wshape[1]
            lanes = slice(None) if cols == pshape[1] else (
                pl.ds(0, cols) if cols < 128 else pl.ds(pl.multiple_of(me * cols, cols), cols))
            g = p_refs[i][0, :, lanes]
            for s in range(1, N_DEV):
                g = g + p_refs[i][s, :, lanes]
            g_ref, d_ref, mo_ref, vo_ref = outs[4 * i:4 * i + 4]
            g_ref[...] = g
            d_ref[...], mo_ref[...], vo_ref[...] = _adamw(w_refs[i][...], g, m_refs[i][...], v_refs[i][...])

    full = lambda shape: pl.BlockSpec(shape, lambda: (0,) * len(shape))
    names = [n for n, _, _ in SMALL]
    res = _call(
        after, body,
        in_specs=[full((N_DEV,) + ps) for _, _, ps in SMALL] + [full(ws) for _, ws, _ in SMALL] * 3
        + [full((N_DEV, 1, 128))],
        out_specs=[full(ws) for _, ws, _ in SMALL for _ in range(4)] + [full((1, 128))],
        out_shape=[_sds(ws, F32) for _, ws, _ in SMALL for _ in range(4)] + [_sds((1, 128), F32)], name="adam_small")(
            *[parts[n] for n in names], *[a[n].reshape(ws) for a in (w, m, v) for n, ws, _ in SMALL], losses)
    return {n: [r.reshape(w[n].shape) for r in res[4 * i:4 * i + 4]] for i, n in enumerate(names)}, res[-1][0, 0]


WEIGHTS = ("a_w_in", "a_lower_bound", "a_norm_gain", "a_w_out", "kv_w", "kv_b", "b_w_q", "b_b_q", "b_sinks", "b_w_out",
           "b_b_out", "ffn_w_gate_up", "ffn_w_down", "ple_w_up", "ple_w_gate", "ple_b_gate", "ln_gain", "ln_bias")


GATHER_GROUPS = (("a_w_in",), ("a_w_out", "gu0"), ("dn0", "pu0", "pg0"), ("kv_w", "b_w_q", "b_w_out"),
                 ("gu1", "dn1", "pu1", "pg1"))
KERNEL_LAYOUT = {
    "a_w_in": lambda a: a,
    "a_w_out": lambda a: a.reshape(D, D),
    "kv_w": lambda a: a.reshape(D, 2 * ATT_KVH * ATT_HD),
    "b_w_q": lambda a: a.reshape(D, D),
    "b_w_out": lambda a: a.reshape(D, D),
    "gu": lambda a: a.reshape(2, 4, FFN_B, D),
    "dn": lambda a: a.reshape(4, FFN_B, D),
    "pu": lambda a: a,
    "pg": lambda a: a.reshape(D, D),
}
_row_blocks = lambda a: a.reshape(N_DEV, -1, a.shape[-1])
OWNER_BLOCKS = {
    "a_w_in": lambda g: g,
    "a_w_out": _row_blocks,
    "kv_w": _row_blocks,
    "b_w_q": _row_blocks,
    "b_w_out": _row_blocks,
    "gu": lambda g: g,
    "dn": lambda g: _row_blocks(g.reshape(FFN_H, D)),
    "pu": lambda g: g.reshape(PLE_DIM, N_DEV, 128).transpose(1, 0, 2),
    "pg": _row_blocks,
}
ADAM_AFTER = {1: (("kv_w", "kv_w"), ("b_w_q", "b_w_q"), ("b_w_out", "b_w_out")),
              2: (("ffn_w_gate_up", "gu"), ("ffn_w_down", "dn"), ("ple_w_up", "pu"), ("ple_w_gate", "pg")),
              3: (("a_w_out", "a_w_out"),), 4: (("a_w_in", "a_w_in"),)}


def kernel(x, p, a_w_in, a_lower_bound, a_norm_gain, a_w_out, kv_w, kv_b, b_w_q, b_b_q, b_sinks, b_w_out, b_b_out, ffn_w_gate_up, ffn_w_down, ple_w_up, ple_w_gate, ple_b_gate, ln_gain, ln_bias, loss_target, m_a_w_in, m_a_lower_bound, m_a_norm_gain, m_a_w_out, m_kv_w, m_kv_b, m_b_w_q, m_b_b_q, m_b_sinks, m_b_w_out, m_b_b_out, m_ffn_w_gate_up, m_ffn_w_down, m_ple_w_up, m_ple_w_gate, m_ple_b_gate, m_ln_gain, m_ln_bias, v_a_w_in, v_a_lower_bound, v_a_norm_gain, v_a_w_out, v_kv_w, v_kv_b, v_b_w_q, v_b_b_q, v_b_sinks, v_b_w_out, v_b_b_out, v_ffn_w_gate_up, v_ffn_w_down, v_ple_w_up, v_ple_w_gate, v_ple_b_gate, v_ln_gain, v_ln_bias):
    given = dict(locals())
    w = {n: given[n] for n in WEIGHTS}
    m = {n: given["m_" + n] for n in WEIGHTS}
    v = {n: given["v_" + n] for n in WEIGHTS}
    shards = {"a_w_in": a_w_in[0], "a_w_out": a_w_out[0], "kv_w": kv_w, "b_w_q": b_w_q[0], "b_w_out": b_w_out[0]}
    for l in range(2):
        shards.update({f"gu{l}": ffn_w_gate_up[l].T, f"dn{l}": ffn_w_down[l], f"pu{l}": ple_w_up[l], f"pg{l}": ple_w_gate[l]})
    sharded_small = [a_lower_bound, ln_gain.reshape(6, 128), ln_bias.reshape(6, 128)]
    gathered = {}
    for gi, g in enumerate(GATHER_GROUPS):
        lands = _sequencer_gather([shards[n].astype(CDT) for n in g] + (sharded_small if gi == 0 else []),
                                  f"gather{gi}", gi)
        for n, a in zip(g, lands):
            gathered[n] = KERNEL_LAYOUT[n.rstrip("01")](a)
        if gi == 0:
            alb, lng, lnb = [a.transpose(1, 0, 2).reshape(a.shape[1], D) for a in lands[len(g):]]

    def getw(key, after):
        return gathered[key]

    sm = {"a_lower_bound": alb, "ln_gain": lng.reshape(2, 3, D), "ln_bias": lnb.reshape(2, 3, D),
          "a_norm_gain": a_norm_gain, "kv_b": kv_b, "b_b_q": b_b_q[0], "b_sinks": b_sinks, "b_b_out": b_b_out,
          "ple_b_gate": ple_b_gate}

    scatters, small_parts, swapped = [], {}, []
    first_id = len(GATHER_GROUPS)

    def finish(names, blocks, theirs):
        k = len(scatters)
        pairs = _pair_sums(blocks, theirs, f"pair_sum{k}")
        lands = _sequencer_chip_scatter(pairs, f"scatter{k}", first_id + 2 * k + 1)
        scatters.append(dict(zip(names, zip(lands, pairs))))

    def emit(grads, small=None):
        names = list(grads)
        blocks = [OWNER_BLOCKS[n.rstrip("01")](grads[n]) for n in names]
        k = len(scatters) + len(swapped)
        theirs = _sequencer_sibling_swap(blocks, f"swap{k}", first_id + 2 * k)
        if swapped:
            finish(*swapped.pop())
        swapped.append((names, blocks, theirs))
        if small is not None:
            names_small = [n for n, _, _ in SMALL] + ["loss"]
            partials = [small[n].reshape(ps) for n, _, ps in SMALL] + [small["loss"]]
            small_parts.update(zip(names_small, _sequencer_gather(partials, "gather_small_grads", first_id + 2 * k + 2)))
        return blocks

    loss, grad_x, gs = _local_step(x[0], p[:, 0], loss_target[0], getw, sm, emit)
    finish(*swapped.pop())

    out, parts, last = {}, {}, [grad_x] + [land for land, _ in scatters[0].values()]
    for i, landed in enumerate(scatters):
        parts.update(landed)
        for n, key in ADAM_AFTER.get(i, ()):
            lrc = (1,) * (3 - w[n].ndim) + w[n].shape
            layers = [parts[key]] if key in parts else [parts[key + "0"], parts[key + "1"]]
            shard = (lambda a: a.reshape(lrc).swapaxes(1, 2)) if key == "gu" else (lambda a: a.reshape(lrc))
            res = _adam_big(shard(w[n]), [land for land, _ in layers], [pair for _, pair in layers], shard(m[n]),
                            shard(v[n]), "adam_" + n, after=last)
            out[n] = [(r.swapaxes(1, 2) if key == "gu" else r).reshape(w[n].shape) for r in res]
            last = [res[3]]
    small_out, loss = _adam_small(small_parts, w, m, v, small_parts["loss"], after=last)
    out.update(small_out)
    res = [loss, grad_x[None]]
    for i in range(4):
        res += [out[n][i] for n in WEIGHTS]
    return tuple(res)
```

```python
import jax
import jax.numpy as jnp
from jax import lax
from jax.experimental import pallas as pl
from jax.experimental.pallas import tpu as pltpu
from jax.experimental.pallas import tpu_sc as plsc

F32 = jnp.float32
CDT = jnp.bfloat16

N_DEV = 8
D = 1024
HG_H, HG_DK, HG_CH = 8, 128, 64
HG_HPB = 4
HG_CPB = 8
ATT_HD, ATT_QH, ATT_KVH, ATT_G, WINDOW = 64, 16, 4, 4, 128
FFN_H = 2816
FFN_B = FFN_H // 4
PLE_DIM = 256
ALPHA = (2.0 * 2) ** 0.25
LN_EPS = 1e-5
RMS_EPS = 1e-6
ADAM_LR, ADAM_B1, ADAM_B2, ADAM_EPS, ADAM_WD, ADAM_STEP = 0.001, 0.9, 0.999, 1e-08, 0.01, 10
ROW_TILES = (512, 256, 128, 64)
VMEM_LIMIT = 48 * 1024 * 1024
NEG = -1e30

MESH = pl.DeviceIdType.MESH


def _tile(n, cands=ROW_TILES):
    for t in cands:
        if n % t == 0:
            return t
    return n


def _sds(shape, dtype):
    return jax.ShapeDtypeStruct(tuple(shape), dtype)


def _params(sem):
    return pltpu.CompilerParams(dimension_semantics=sem, vmem_limit_bytes=VMEM_LIMIT)


def _dot(a, b):
    return jnp.dot(a.astype(CDT), b.astype(CDT), preferred_element_type=F32)


def _dot_nt(a, b):
    return lax.dot_general(a.astype(CDT), b.astype(CDT), (((1,), (1,)), ((), ())), preferred_element_type=F32)


def _dot_tn(a, b):
    return lax.dot_general(a.astype(CDT), b.astype(CDT), (((0,), (0,)), ((), ())), preferred_element_type=F32)


def _sigmoid(x):
    return jax.nn.sigmoid(x)


def _ln_fwd(z, g, b):
    mu = jnp.mean(z, axis=-1, keepdims=True)
    zc = z - mu
    var = jnp.mean(zc * zc, axis=-1, keepdims=True)
    return zc * lax.rsqrt(var + LN_EPS) * g + b


def _ln_bwd(z, g, dy):
    mu = jnp.mean(z, axis=-1, keepdims=True)
    zc = z - mu
    var = jnp.mean(zc * zc, axis=-1, keepdims=True)
    rstd = lax.rsqrt(var + LN_EPS)
    xhat = zc * rstd
    dxh = dy * g
    dz = rstd * (dxh - jnp.mean(dxh, axis=-1, keepdims=True) - xhat * jnp.mean(dxh * xhat, axis=-1, keepdims=True))
    return dz, xhat


def _colsum(x):
    return jnp.sum(x, axis=0, keepdims=True)


def _acc(ref, val, first):
    @pl.when(first)
    def _():
        ref[...] = val

    @pl.when(jnp.logical_not(first))
    def _():
        ref[...] += val


def _zero_at(ref, first):
    @pl.when(first)
    def _():
        ref[...] = jnp.zeros_like(ref)


def _call(after, body, **kw):
    after = [] if after is None else list(after)
    specs = list(kw["in_specs"])
    kw["in_specs"] = [pl.BlockSpec(memory_space=pl.ANY)] * len(after) + specs

    def ordered_body(*refs):
        body(*refs[len(after):])

    call = pl.pallas_call(ordered_body, **kw)

    def pinned(*args):
        args = [a if s.memory_space is not None else pltpu.with_memory_space_constraint(a, pltpu.HBM)
                for a, s in zip(args, specs)]
        return call(*after, *args)

    return pinned


def _mm_nn(a, b3, out_shape, oblock, omap, out_dtype, bias3=None, name="mm_nn"):
    M, K = a.shape
    G, _, Nb = b3.shape
    tm = _tile(M)

    def body(a_ref, b_ref, *rest):
        o_ref = rest[-1]
        acc = _dot(a_ref[...], b_ref[...])
        if bias3 is not None:
            acc = acc + rest[0][...]
        o_ref[...] = acc.astype(o_ref.dtype)

    in_specs = [pl.BlockSpec((tm, K), lambda g, i: (i, 0)), pl.BlockSpec((None, K, Nb), lambda g, i: (g, 0, 0))]
    args = [a, b3]
    if bias3 is not None:
        in_specs.append(pl.BlockSpec((None, 1, Nb), lambda g, i: (g, 0, 0)))
        args.append(bias3)
    return _call(
        None, body, grid=(G, M // tm), in_specs=in_specs, out_specs=pl.BlockSpec(oblock, omap),
        out_shape=_sds(out_shape, out_dtype), compiler_params=_params(("arbitrary", "arbitrary")), name=name)(*args)


def _mm_tn(a3, b3, G, amap, bmap, ablock, bblock, out_shape, oblock, omap, name="mm_tn"):
    S = a3.shape[1]

    def body(a_ref, b_ref, o_ref):
        o_ref[...] = _dot_tn(a_ref[...], b_ref[...]).astype(o_ref.dtype)

    return _call(
        None, body, grid=(G, 1),
        in_specs=[pl.BlockSpec(ablock(S), amap), pl.BlockSpec(bblock(S), bmap)],
        out_specs=pl.BlockSpec(oblock, omap), out_shape=_sds(out_shape, CDT),
        compiler_params=_params(("arbitrary", "arbitrary")), name=name)(a3, b3)


def _wgrad(a3, b3, name):
    Ga, S, M = a3.shape
    Gb, _, N = b3.shape
    G = max(Ga, Gb)
    return _mm_tn(
        a3, b3, G,
        (lambda g, k: (g, k, 0)) if Ga > 1 else (lambda g, k: (0, k, 0)),
        (lambda g, k: (g, k, 0)) if Gb > 1 else (lambda g, k: (0, k, 0)),
        lambda tk: (None, tk, M), lambda tk: (None, tk, N),
        (G, M, N), (None, M, N), lambda g, k: (g, 0, 0), name=name)


def _mixout_ln(u3, w3, bias, xin, gain, beta, name):
    G, S, Kb = u3.shape
    tm = _tile(S)

    def body(u_ref, w_ref, b_ref, x_ref, g_ref, be_ref, z_ref, xo_ref, xob_ref):
        h = b_ref[...] + _dot(u_ref[0], w_ref[0])
        for g in range(1, G):
            h = h + _dot(u_ref[g], w_ref[g])
        z = ALPHA * x_ref[...] + h
        z_ref[...] = z
        y = _ln_fwd(z, g_ref[...], be_ref[...])
        xo_ref[...] = y
        xob_ref[...] = y.astype(CDT)

    row = pl.BlockSpec((tm, D), lambda i: (i, 0))
    vec = pl.BlockSpec((1, D), lambda i: (0, 0))
    return _call(
        None, body, grid=(S // tm,),
        in_specs=[pl.BlockSpec((G, tm, Kb), lambda i: (0, i, 0)), pl.BlockSpec((G, Kb, D), lambda i: (0, 0, 0)),
                  vec, row, vec, vec],
        out_specs=[row, row, row], out_shape=[_sds((S, D), F32), _sds((S, D), F32), _sds((S, D), CDT)],
        compiler_params=_params(("arbitrary",)), name=name)(u3, w3, bias, xin, gain, beta)


def _ffn_fwd(xin, xin_b, wgu, wdn, gain, beta, name):
    S = xin.shape[0]
    tm = _tile(S)

    def hidden(xb_ref, wgu_ref, gu_ref, hid_ref):
        xb = xb_ref[...]
        gate = _dot_nt(xb, wgu_ref[0])
        up = _dot_nt(xb, wgu_ref[1])
        gu_ref[0] = gate
        gu_ref[1] = up
        hid_ref[...] = (gate * _sigmoid(gate) * up).astype(CDT)

    gu, hid = _call(
        None, hidden, grid=(S // tm, 4),
        in_specs=[pl.BlockSpec((tm, D), lambda i, j: (i, 0)), pl.BlockSpec((2, None, FFN_B, D), lambda i, j: (0, j, 0, 0))],
        out_specs=[pl.BlockSpec((2, None, tm, FFN_B), lambda i, j: (0, j, i, 0)),
                   pl.BlockSpec((None, tm, FFN_B), lambda i, j: (j, i, 0))],
        out_shape=[_sds((2, 4, S, FFN_B), F32), _sds((4, S, FFN_B), CDT)],
        compiler_params=_params(("arbitrary", "arbitrary")), name=name + "_hidden")(xin_b, wgu)

    def down(x_ref, hid_ref, wdn_ref, g_ref, be_ref, z_ref, xo_ref, xob_ref):
        z = ALPHA * x_ref[...]
        for j in range(4):
            z = z + _dot(hid_ref[j], wdn_ref[j])
        z_ref[...] = z
        y = _ln_fwd(z, g_ref[...], be_ref[...])
        xo_ref[...] = y
        xob_ref[...] = y.astype(CDT)

    row = pl.BlockSpec((tm, D), lambda i: (i, 0))
    vec = pl.BlockSpec((1, D), lambda i: (0, 0))
    z, xo, xob = _call(
        None, down, grid=(S // tm,),
        in_specs=[row, pl.BlockSpec((4, tm, FFN_B), lambda i: (0, i, 0)), pl.BlockSpec((4, FFN_B, D), lambda i: (0, 0, 0)),
                  vec, vec],
        out_specs=[row, row, row], out_shape=[_sds((S, D), F32), _sds((S, D), F32), _sds((S, D), CDT)],
        compiler_params=_params(("arbitrary",)), name=name + "_down")(xin, hid, wdn, gain, beta)
    return gu, hid, z, xo, xob


def _ple_fwd(xin, xin_b, p_b, wpg, bgate, wpu, gain, beta, name):
    S = xin.shape[0]
    tm = _tile(S)

    def body(x_ref, xb_ref, p_ref, wpg_ref, bg_ref, wpu_ref, g_ref, be_ref, sg_ref, up_ref, z_ref, xo_ref, xob_ref):
        sg = _sigmoid(_dot(xb_ref[...], wpg_ref[...]) + bg_ref[...])
        pb = p_ref[...]
        up = jnp.concatenate([_dot(pb, wpu_ref[j]) for j in range(N_DEV)], axis=-1)
        sg_ref[...] = sg
        up_ref[...] = up
        z = ALPHA * x_ref[...] + sg * up
        z_ref[...] = z
        y = _ln_fwd(z, g_ref[...], be_ref[...])
        xo_ref[...] = y
        xob_ref[...] = y.astype(CDT)

    row = pl.BlockSpec((tm, D), lambda i: (i, 0))
    vec = pl.BlockSpec((1, D), lambda i: (0, 0))
    return _call(
        None, body, grid=(S // tm,),
        in_specs=[row, row, pl.BlockSpec((tm, PLE_DIM), lambda i: (i, 0)), pl.BlockSpec((D, D), lambda i: (0, 0)), vec,
                  pl.BlockSpec((N_DEV, PLE_DIM, D // N_DEV), lambda i: (0, 0, 0)), vec, vec],
        out_specs=[row] * 5,
        out_shape=[_sds((S, D), F32)] * 4 + [_sds((S, D), CDT)],
        compiler_params=_params(("arbitrary",)), name=name)(xin, xin_b, p_b, wpg, bgate, wpu, gain, beta)


def _loss_fwd_bwd(y, target):
    S = y.shape[0]
    tm = _tile(S)

    def body(y_ref, t_ref, l_ref, dy_ref):
        e = y_ref[...] - t_ref[...]
        dy_ref[...] = e * (1.0 / D)
        part = 0.5 * jnp.sum(jnp.sum(e * e, axis=-1, keepdims=True) * (1.0 / D), axis=0, keepdims=True)
        _acc(l_ref, jnp.broadcast_to(part, l_ref.shape), pl.program_id(0) == 0)

    row = pl.BlockSpec((tm, D), lambda i: (i, 0))
    return _call(
        None, body, grid=(S // tm,), in_specs=[row, row],
        out_specs=[pl.BlockSpec((1, 128), lambda i: (0, 0)), row],
        out_shape=[_sds((1, 128), F32), _sds((S, D), F32)],
        compiler_params=_params(("arbitrary",)), name="loss")(y, target)


def _ple_bwd(dy, z, sg, up, gain, wpg, name, after=None):
    S = dy.shape[0]
    tm = _tile(S)

    def body(dy_ref, z_ref, sg_ref, up_ref, g_ref, wpg_ref, dx_ref, dgl_ref, dup_ref, dgain_ref, dbeta_ref, dbg_ref):
        first = pl.program_id(0) == 0
        dy_ = dy_ref[...]
        dz, xhat = _ln_bwd(z_ref[...], g_ref[...], dy_)
        sg_ = sg_ref[...]
        dgl = dz * up_ref[...] * sg_ * (1.0 - sg_)
        dgl_ref[...] = dgl.astype(CDT)
        dup_ref[...] = (dz * sg_).astype(CDT)
        dx_ref[...] = ALPHA * dz + _dot_nt(dgl, wpg_ref[...])
        _acc(dgain_ref, _colsum(dy_ * xhat), first)
        _acc(dbeta_ref, _colsum(dy_), first)
        _acc(dbg_ref, _colsum(dgl), first)

    row = pl.BlockSpec((tm, D), lambda i: (i, 0))
    vec = pl.BlockSpec((1, D), lambda i: (0, 0))
    return _call(
        after, body, grid=(S // tm,), in_specs=[row, row, row, row, vec, pl.BlockSpec((D, D), lambda i: (0, 0))],
        out_specs=[row, row, row, vec, vec, vec],
        out_shape=[_sds((S, D), F32), _sds((S, D), CDT), _sds((S, D), CDT)] + [_sds((1, D), F32)] * 3,
        compiler_params=_params(("arbitrary",)), name=name)(dy, z, sg, up, gain, wpg)


def _ffn_bwd_hidden(dy, z, gu, wdn, gain, name, after=None):
    S = dy.shape[0]
    tm = _tile(S)

    def hidden(dy_ref, z_ref, gu_ref, wdn_ref, g_ref, dz_ref, dzb_ref, dgu_ref, dgain_ref, dbeta_ref):
        i, j = pl.program_id(0), pl.program_id(1)

        @pl.when(j == 0)
        def _():
            dy_ = dy_ref[...]
            dz, xhat = _ln_bwd(z_ref[...], g_ref[...], dy_)
            dz_ref[...] = dz
            dzb_ref[...] = dz.astype(CDT)
            _acc(dgain_ref, _colsum(dy_ * xhat), i == 0)
            _acc(dbeta_ref, _colsum(dy_), i == 0)

        dhid = _dot_nt(dzb_ref[...], wdn_ref[...])
        gate, up = gu_ref[0], gu_ref[1]
        sg = _sigmoid(gate)
        dgu_ref[0] = (dhid * up * (sg * (1.0 + gate * (1.0 - sg)))).astype(CDT)
        dgu_ref[1] = (dhid * (gate * sg)).astype(CDT)

    row = pl.BlockSpec((tm, D), lambda i, j: (i, 0))
    vec = pl.BlockSpec((1, D), lambda i, j: (0, 0))
    return _call(
        after, hidden, grid=(S // tm, 4),
        in_specs=[row, row, pl.BlockSpec((2, None, tm, FFN_B), lambda i, j: (0, j, i, 0)),
                  pl.BlockSpec((None, FFN_B, D), lambda i, j: (j, 0, 0)), vec],
        out_specs=[row, row, pl.BlockSpec((2, None, tm, FFN_B), lambda i, j: (0, j, i, 0)), vec, vec],
        out_shape=[_sds((S, D), F32), _sds((S, D), CDT), _sds((2, 4, S, FFN_B), CDT), _sds((1, D), F32),
                   _sds((1, D), F32)],
        compiler_params=_params(("arbitrary", "arbitrary")), name=name + "_hidden")(dy, z, gu, wdn, gain)


def _ffn_bwd_input(dz, dgu, wgu, name, after=None):
    S = dz.shape[0]
    tm = _tile(S)

    def to_input(dz_ref, dgu_ref, wgu_ref, dx_ref):
        acc = ALPHA * dz_ref[...]
        for g in range(2):
            for j in range(4):
                acc = acc + _dot(dgu_ref[g, j], wgu_ref[g, j])
        dx_ref[...] = acc

    rows = pl.BlockSpec((tm, D), lambda i: (i, 0))
    return _call(
        after, to_input, grid=(S // tm,),
        in_specs=[rows, pl.BlockSpec((2, 4, tm, FFN_B), lambda i: (0, 0, i, 0)),
                  pl.BlockSpec((2, 4, FFN_B, D), lambda i: (0, 0, 0, 0))],
        out_specs=rows, out_shape=_sds((S, D), F32),
        compiler_params=_params(("arbitrary",)), name=name + "_input")(dz, dgu, wgu)


def _mixout_bwd(dy, z, gain, w3, du_dtype, name, after=None):
    S = dy.shape[0]
    G, Kb, _ = w3.shape
    tm = _tile(S)

    def body(dy_ref, z_ref, g_ref, w_ref, dz_ref, dzb_ref, du_ref, dgain_ref, dbeta_ref, dbias_ref):
        first = pl.program_id(0) == 0
        dy_ = dy_ref[...]
        dz, xhat = _ln_bwd(z_ref[...], g_ref[...], dy_)
        dz_ref[...] = dz
        dzb = dz.astype(CDT)
        dzb_ref[...] = dzb
        for g in range(G):
            du_ref[g] = _dot_nt(dzb, w_ref[g]).astype(du_ref.dtype)
        _acc(dgain_ref, _colsum(dy_ * xhat), first)
        _acc(dbeta_ref, _colsum(dy_), first)
        _acc(dbias_ref, _colsum(dz), first)

    row = pl.BlockSpec((tm, D), lambda i: (i, 0))
    vec = pl.BlockSpec((1, D), lambda i: (0, 0))
    return _call(
        after, body, grid=(S // tm,), in_specs=[row, row, vec, pl.BlockSpec((G, Kb, D), lambda i: (0, 0, 0))],
        out_specs=[row, row, pl.BlockSpec((G, tm, Kb), lambda i: (0, i, 0)), vec, vec, vec],
        out_shape=[_sds((S, D), F32), _sds((S, D), CDT), _sds((G, S, Kb), du_dtype)] + [_sds((1, D), F32)] * 3,
        compiler_params=_params(("arbitrary",)), name=name)(dy, z, gain, w3)


def _half_select(low):
    r = lax.broadcasted_iota(jnp.int32, (2 * ATT_HD, ATT_HD), 0)
    c = lax.broadcasted_iota(jnp.int32, (2 * ATT_HD, ATT_HD), 1)
    return (r == c + (0 if low else ATT_HD)).astype(CDT)


def _half_place(low):
    r = lax.broadcasted_iota(jnp.int32, (ATT_HD, 2 * ATT_HD), 0)
    c = lax.broadcasted_iota(jnp.int32, (ATT_HD, 2 * ATT_HD), 1)
    return (c == r + (0 if low else ATT_HD)).astype(CDT)


def _pair_lanes(even, odd):
    return (jnp.dot(even, _half_place(True), preferred_element_type=F32)
            + jnp.dot(odd, _half_place(False), preferred_element_type=F32)).astype(CDT)


def _proj_heads(a, w, bias, heads, name):
    S, K = a.shape
    N = heads * ATT_HD
    tm = _tile(S)

    def body(a_ref, w_ref, b_ref, o_ref):
        acc = (_dot(a_ref[...], w_ref[...]) + b_ref[...]).astype(CDT)
        sel = (_half_select(True), _half_select(False))
        for h in range(heads):
            pair = acc[:, (h // 2) * 2 * ATT_HD:(h // 2 + 1) * 2 * ATT_HD]
            o_ref[h] = jnp.dot(pair, sel[h % 2], preferred_element_type=F32).astype(CDT)

    return _call(
        None, body, grid=(S // tm,),
        in_specs=[pl.BlockSpec((tm, K), lambda i: (i, 0)), pl.BlockSpec((K, N), lambda i: (0, 0)),
                  pl.BlockSpec((1, N), lambda i: (0, 0))],
        out_specs=pl.BlockSpec((heads, tm, ATT_HD), lambda i: (0, i, 0)), out_shape=_sds((heads, S, ATT_HD), CDT),
        compiler_params=_params(("arbitrary",)), name=name)(a, w, bias)


def _qkv_bwd(dz, dq, dkv4, wq, wkv, name, after=None):
    S = dz.shape[0]
    tm = _tile(S)
    HK = dkv4.shape[0]
    NK = HK * ATT_HD

    def body(dz_ref, dq_ref, dkv_ref, wq_ref, wkv_ref, dx_ref, dkvn_ref, dkvb_ref):
        first = pl.program_id(0) == 0
        dkvn = jnp.concatenate([_pair_lanes(dkv_ref[2 * i].astype(CDT), dkv_ref[2 * i + 1].astype(CDT))
                                for i in range(HK // 2)], axis=-1)
        dkvn_ref[...] = dkvn
        dx_ref[...] = ALPHA * dz_ref[...] + _dot_nt(dq_ref[...], wq_ref[...]) + _dot_nt(dkvn, wkv_ref[...])
        for h in range(HK):
            _acc(dkvb_ref.at[h], _colsum(dkv_ref[h]), first)

    row = pl.BlockSpec((tm, D), lambda i: (i, 0))
    return _call(
        after, body, grid=(S // tm,),
        in_specs=[row, row, pl.BlockSpec((HK, tm, ATT_HD), lambda i: (0, i, 0)),
                  pl.BlockSpec((D, D), lambda i: (0, 0)), pl.BlockSpec((D, NK), lambda i: (0, 0))],
        out_specs=[row, pl.BlockSpec((tm, NK), lambda i: (i, 0)), pl.BlockSpec((HK, 1, ATT_HD), lambda i: (0, 0, 0))],
        out_shape=[_sds((S, D), F32), _sds((S, NK), CDT), _sds((HK, 1, ATT_HD), F32)],
        compiler_params=_params(("arbitrary",)), name=name)(dz, dq, dkv4, wq, wkv)


def _inproj_bwd(dz, dproj, wain, name, after=None):
    S = dz.shape[0]
    tm = _tile(S)
    nb = wain.shape[-1]

    def body(dz_ref, dp_ref, w_ref, dx_ref):
        acc = ALPHA * dz_ref[...]
        for j in range(N_DEV):
            acc = acc + _dot_nt(dp_ref[j // 2, :, pl.ds((j % 2) * nb, nb)], w_ref[j])
        dx_ref[...] = acc

    row = pl.BlockSpec((tm, D), lambda i: (i, 0))
    return _call(
        after, body, grid=(S // tm,),
        in_specs=[row, pl.BlockSpec((4, tm, D), lambda i: (0, i, 0)), pl.BlockSpec((N_DEV, D, nb), lambda i: (0, 0, 0))],
        out_specs=row, out_shape=_sds((S, D), F32),
        compiler_params=_params(("arbitrary",)), name=name)(dz, dproj, wain)


def _running_sum(x, reverse=False):
    rows = x.shape[0]
    row = lax.broadcasted_iota(jnp.int32, x.shape, 0)
    step = 1
    while step < rows:
        if reverse:
            x = x + jnp.where(row < rows - step, pltpu.roll(x, rows - step, 0), 0.0)
        else:
            x = x + jnp.where(row >= step, pltpu.roll(x, step, 0), 0.0)
        step *= 2
    return x


def _hg_gates(q, f, alb_ref):
    a0, a1 = alb_ref[0:1, :], alb_ref[1:2, :]
    mx = jnp.maximum(a0, a1)
    e0, e1 = jnp.exp(a0 - mx), jnp.exp(a1 - mx)
    lb = e0 / (e0 + e1)
    sig = _sigmoid(f)
    forget = lb + (1.0 - lb) * sig
    k = (1.0 - lb) * _sigmoid(-f)
    qs = q * _sigmoid(q) * (HG_DK ** -0.5)
    return qs, k, jnp.log(forget), sig, lb, forget


def _hg_intra(qs, k, b, b_scr):
    b_scr[...] = b
    bm = b_scr[pl.ds(HG_CH // 2 - 1, 1), :]
    bl = b_scr[pl.ds(HG_CH - 1, 1), :]
    eb = jnp.exp(b)
    qb = qs * eb
    e_q = jnp.exp(b - bm)
    e_k = jnp.exp(bm - b)
    e_d = jnp.exp(bl - b)
    return qb, qs * e_q, k * e_k, k * e_d, jnp.exp(bl), eb, e_q, e_k, e_d


def _hgrn_fwd(proj, alb, ngain):
    S = proj.shape[1]
    nc = S // HG_CH
    nb = nc // HG_CPB
    rb, wb = HG_CPB * HG_CH, HG_HPB * HG_DK

    def body(pj_ref, alb_ref, ng_ref, o_ref, y_ref, st_ref, st_scr, b_scr):
        n = pl.program_id(1)

        @pl.when(n == 0)
        def _():
            st_scr[...] = jnp.zeros_like(st_scr)

        r = lax.broadcasted_iota(jnp.int32, (HG_CH, HG_CH), 0)
        c = lax.broadcasted_iota(jnp.int32, (HG_CH, HG_CH), 1)
        causal = r >= c
        for ci, j in [(ci, j) for ci in range(HG_CPB) for j in range(HG_HPB)]:
            rows, lanes = pl.ds(ci * HG_CH, HG_CH), pl.ds(j * HG_DK, HG_DK)
            q, f, v, g = pj_ref[0, rows, lanes], pj_ref[1, rows, lanes], pj_ref[2, rows, lanes], pj_ref[3, rows, lanes]
            qs, k, logf, _, _, _ = _hg_gates(q, f, alb_ref.at[:, lanes])
            b = _running_sum(logf)
            qb, qt, kt, kd, ebl, _, _, _, _ = _hg_intra(qs, k, b, b_scr.at[j, ci])
            st = st_scr[j]
            st_ref[j, ci] = st
            a = jnp.where(causal, _dot_nt(qt, kt), 0.0)
            o = _dot(a, v) + _dot_nt(qb, st)
            st_scr[j] = st * ebl + _dot_tn(v, kd)
            o_ref[rows, lanes] = o
            rinv = lax.rsqrt(jnp.mean(o * o, axis=-1, keepdims=True) + RMS_EPS)
            y_ref[rows, lanes] = (o * rinv * ng_ref[...] * (g * _sigmoid(g))).astype(CDT)

    blk = pl.BlockSpec((rb, wb), lambda h, n: (n, h))
    return _call(
        None, body, grid=(HG_H // HG_HPB, nb),
        in_specs=[pl.BlockSpec((4, rb, wb), lambda h, n: (0, n, h)), pl.BlockSpec((2, wb), lambda h, n: (0, h)),
                  pl.BlockSpec((1, HG_DK), lambda h, n: (0, 0))],
        out_specs=[blk, blk, pl.BlockSpec((HG_HPB, HG_CPB, HG_DK, HG_DK), lambda h, n: (h, n, 0, 0))],
        out_shape=[_sds((S, D), F32), _sds((S, D), CDT), _sds((HG_H, nc, HG_DK, HG_DK), F32)],
        scratch_shapes=[pltpu.VMEM((HG_HPB, HG_DK, HG_DK), F32), pltpu.VMEM((HG_HPB, HG_CPB, HG_CH, HG_DK), F32)],
        compiler_params=_params(("arbitrary", "arbitrary")), name="hgrn_fwd")(proj, alb, ngain)


def _hgrn_bwd(proj, alb, ngain, o, states, dy, after=None):
    S = proj.shape[1]
    nc = S // HG_CH
    nb = nc // HG_CPB
    rb, wb = HG_CPB * HG_CH, HG_HPB * HG_DK

    def body(pj_ref, alb_ref, ng_ref, o_ref, st_ref, dy_ref, dpj_ref, dalb_ref, dng_ref, dst_scr, b_scr):
        h, n = pl.program_id(0), pl.program_id(1)

        @pl.when(n == 0)
        def _():
            dst_scr[...] = jnp.zeros_like(dst_scr)
            dalb_ref[...] = jnp.zeros_like(dalb_ref)

        _zero_at(dng_ref, jnp.logical_and(h == 0, n == 0))
        ng = ng_ref[...]
        r = lax.broadcasted_iota(jnp.int32, (HG_CH, HG_CH), 0)
        c = lax.broadcasted_iota(jnp.int32, (HG_CH, HG_CH), 1)
        causal = r >= c
        dng = None
        for ci, j in [(ci, j) for ci in reversed(range(HG_CPB)) for j in range(HG_HPB)]:
            rows, lanes = pl.ds(ci * HG_CH, HG_CH), pl.ds(j * HG_DK, HG_DK)
            q, f, v, g = pj_ref[0, rows, lanes], pj_ref[1, rows, lanes], pj_ref[2, rows, lanes], pj_ref[3, rows, lanes]
            o_ = o_ref[rows, lanes]
            dy_ = dy_ref[rows, lanes]
            sg = _sigmoid(g)
            rinv = lax.rsqrt(jnp.mean(o_ * o_, axis=-1, keepdims=True) + RMS_EPS)
            nrm = o_ * rinv
            dr = dy_ * (g * sg)
            dg = dy_ * nrm * ng * (sg * (1.0 + g * (1.0 - sg)))
            dn = dr * ng
            do = rinv * (dn - nrm * jnp.mean(dn * nrm, axis=-1, keepdims=True))
            dng = _colsum(dr * nrm) if dng is None else dng + _colsum(dr * nrm)
            qs, k, logf, sig, lb, forget = _hg_gates(q, f, alb_ref.at[:, lanes])
            b = _running_sum(logf)
            qb, qt, kt, kd, ebl, eb, e_q, e_k, e_d = _hg_intra(qs, k, b, b_scr.at[j, ci])
            st = st_ref[j, ci]
            dstn = dst_scr[j]
            qt, kt, qb, kd = (t.astype(CDT).astype(F32) for t in (qt, kt, qb, kd))
            a = jnp.where(causal, _dot_nt(qt, kt), 0.0)
            da = jnp.where(causal, _dot_nt(do, v), 0.0)
            dv = _dot_tn(a, do) + _dot_nt(kd, dstn)
            dqb = _dot(do, st)
            dkd = _dot(v, dstn)
            dqt = _dot(da, kt)
            dkt = _dot_tn(da, qt)
            dbl = _colsum(dkd * kd) + ebl * _colsum(dstn * st)
            dst_scr[j] = dstn * ebl + _dot_tn(do, qb)
            dqs = dqt * e_q + dqb * eb
            dk = dkt * e_k + dkd * e_d
            db = dqt * qt + dqb * qb - dkt * kt - dkd * kd
            dlogf = _running_sum(db, reverse=True) + dbl
            dforget = dlogf / forget
            dsig = (1.0 - lb) * (dforget - dk)
            df = dsig * sig * (1.0 - sig)
            dlb = _colsum((dforget - dk) * (1.0 - sig))
            sq = _sigmoid(q)
            dq = dqs * (HG_DK ** -0.5) * (sq * (1.0 + q * (1.0 - sq)))
            dpj_ref[0, rows, lanes] = dq.astype(CDT)
            dpj_ref[1, rows, lanes] = df.astype(CDT)
            dpj_ref[2, rows, lanes] = dv.astype(CDT)
            dpj_ref[3, rows, lanes] = dg.astype(CDT)
            da0 = dlb * lb * (1.0 - lb)
            dalb_ref[pl.ds(0, 1), lanes] += da0
            dalb_ref[pl.ds(1, 1), lanes] -= da0
        dng_ref[...] += dng

    blk = pl.BlockSpec((rb, wb), lambda h, n: (nb - 1 - n, h))
    pj = pl.BlockSpec((4, rb, wb), lambda h, n: (0, nb - 1 - n, h))
    alb_blk = pl.BlockSpec((2, wb), lambda h, n: (0, h))
    ng_blk = pl.BlockSpec((1, HG_DK), lambda h, n: (0, 0))
    return _call(
        after, body, grid=(HG_H // HG_HPB, nb),
        in_specs=[pj, alb_blk, ng_blk, blk,
                  pl.BlockSpec((HG_HPB, HG_CPB, HG_DK, HG_DK), lambda h, n: (h, nb - 1 - n, 0, 0)), blk],
        out_specs=[pj, alb_blk, ng_blk],
        out_shape=[_sds((4, S, D), CDT), _sds((2, D), F32), _sds((1, HG_DK), F32)],
        scratch_shapes=[pltpu.VMEM((HG_HPB, HG_DK, HG_DK), F32), pltpu.VMEM((HG_HPB, HG_CPB, HG_CH, HG_DK), F32)],
        compiler_params=_params(("arbitrary", "arbitrary")), name="hgrn_bwd")(proj, alb, ngain, o, states, dy)


def _slope(h):
    return 2.0 ** (-8.0 * (h + 1) / ATT_QH)


def _attn_mask(n):
    qi = lax.broadcasted_iota(jnp.int32, (WINDOW, 2 * WINDOW), 0)
    si = lax.broadcasted_iota(jnp.int32, (WINDOW, 2 * WINDOW), 1)
    dist = qi - si + WINDOW
    valid = (dist >= 0) & (dist < WINDOW) & (n * WINDOW - WINDOW + si >= 0)
    return valid, dist.astype(F32)


def _attn_probs(qh, kh, sink, slope, valid, distf):
    s = _dot_nt(qh, kh) * (ATT_HD ** -0.5) - slope * distf
    s = jnp.where(valid, s, NEG)
    m = jnp.maximum(jnp.max(s, axis=-1, keepdims=True), sink)
    e = jnp.exp(s - m)
    es = jnp.exp(sink - m)
    inv = 1.0 / (jnp.sum(e, axis=-1, keepdims=True) + es)
    return e * inv, es * inv


def _attn_specs(S):
    nb = S // WINDOW
    cur = lambda H: pl.BlockSpec((H, WINDOW, ATT_HD), lambda n: (0, n, 0))
    prev = lambda H: pl.BlockSpec((H, WINDOW, ATT_HD), lambda n: (0, jnp.maximum(n - 1, 0), 0))
    return nb, cur, prev


def _attn_fwd(q4, kv4, sinks):
    S = q4.shape[1]
    nb, cur, prev = _attn_specs(S)

    def body(sink_ref, q_ref, kvc_ref, kvp_ref, o_ref):
        valid, distf = _attn_mask(pl.program_id(0))
        outs = []
        for h in range(ATT_QH):
            kvh = h // ATT_G
            kh = jnp.concatenate([kvp_ref[kvh], kvc_ref[kvh]], axis=0)
            vh = jnp.concatenate([kvp_ref[ATT_KVH + kvh], kvc_ref[ATT_KVH + kvh]], axis=0)
            p, _ = _attn_probs(q_ref[h], kh, sink_ref[0, h], _slope(h), valid, distf)
            outs.append(_dot(p, vh).astype(CDT))
            if h % 2:
                o_ref[:, pl.ds((h - 1) * ATT_HD, 2 * ATT_HD)] = _pair_lanes(outs[h - 1], outs[h])

    return _call(
        None, body, grid=(nb,),
        in_specs=[pl.BlockSpec(memory_space=pltpu.SMEM), cur(ATT_QH), cur(2 * ATT_KVH), prev(2 * ATT_KVH)],
        out_specs=pl.BlockSpec((WINDOW, D), lambda n: (n, 0)), out_shape=_sds((S, D), CDT),
        compiler_params=_params(("arbitrary",)), name="attn_fwd")(sinks, q4, kv4, kv4)


def _attn_bwd(q4, kv4, sinks, do):
    S = q4.shape[1]
    nb, cur, prev = _attn_specs(S)

    def body(sink_ref, q_ref, kvc_ref, kvp_ref, do_ref, dq_ref, dkv_ref, dbq_ref, dsink_ref):
        n = pl.program_id(0)
        first = n == 0

        @pl.when(first)
        def _():
            dkv_ref[...] = jnp.zeros_like(dkv_ref)
            dsink_ref[...] = jnp.zeros_like(dsink_ref)
            dbq_ref[...] = jnp.zeros_like(dbq_ref)

        valid, distf = _attn_mask(n)
        lane = lax.broadcasted_iota(jnp.int32, (1, 128), 1)
        rows_cur = pl.ds(pl.multiple_of(n * WINDOW, WINDOW), WINDOW)
        rows_prev = pl.ds(pl.multiple_of(jnp.maximum(n - 1, 0) * WINDOW, WINDOW), WINDOW)
        dsinks = jnp.zeros((1, 128), F32)
        sel = (_half_select(True), _half_select(False))
        for kvh in range(ATT_KVH):
            kh = jnp.concatenate([kvp_ref[kvh], kvc_ref[kvh]], axis=0)
            vh = jnp.concatenate([kvp_ref[ATT_KVH + kvh], kvc_ref[ATT_KVH + kvh]], axis=0)
            dk = dv = None
            dqs = []
            for h in range(kvh * ATT_G, (kvh + 1) * ATT_G):
                qh = q_ref[h]
                doh = jnp.dot(do_ref[:, pl.ds((h // 2) * 2 * ATT_HD, 2 * ATT_HD)], sel[h % 2],
                              preferred_element_type=F32).astype(CDT)
                p, ps = _attn_probs(qh, kh, sink_ref[0, h], _slope(h), valid, distf)
                dp = _dot_nt(doh, vh)
                dd = jnp.sum(p * dp, axis=-1, keepdims=True)
                ds = p * (dp - dd)
                dsinks = dsinks + jnp.where(lane == h, -jnp.sum(ps * dd, axis=0, keepdims=True), 0.0)
                dqh = _dot(ds, kh) * (ATT_HD ** -0.5)
                dqs.append(dqh.astype(CDT))
                dbq_ref[h] += _colsum(dqh)
                dkh = _dot_tn(ds, qh) * (ATT_HD ** -0.5)
                dvh = _dot_tn(p, doh)
                dk = dkh if dk is None else dk + dkh
                dv = dvh if dv is None else dv + dvh
            for i in range(ATT_G // 2):
                lanes = pl.ds((kvh * ATT_G + 2 * i) * ATT_HD, 2 * ATT_HD)
                dq_ref[:, lanes] = _pair_lanes(dqs[2 * i], dqs[2 * i + 1])
            dkv_ref[kvh, rows_prev, :] += dk[:WINDOW]
            dkv_ref[kvh, rows_cur, :] += dk[WINDOW:]
            dkv_ref[ATT_KVH + kvh, rows_prev, :] += dv[:WINDOW]
            dkv_ref[ATT_KVH + kvh, rows_cur, :] += dv[WINDOW:]
        dsink_ref[...] += dsinks

    return _call(
        None, body, grid=(nb,),
        in_specs=[pl.BlockSpec(memory_space=pltpu.SMEM), cur(ATT_QH), cur(2 * ATT_KVH), prev(2 * ATT_KVH),
                  pl.BlockSpec((WINDOW, D), lambda n: (n, 0))],
        out_specs=[pl.BlockSpec((WINDOW, D), lambda n: (n, 0)), pl.BlockSpec((2 * ATT_KVH, S, ATT_HD), lambda n: (0, 0, 0)),
                   pl.BlockSpec((ATT_QH, 1, ATT_HD), lambda n: (0, 0, 0)), pl.BlockSpec((1, 128), lambda n: (0, 0))],
        out_shape=[_sds((S, D), CDT), _sds((2 * ATT_KVH, S, ATT_HD), F32), _sds((ATT_QH, 1, ATT_HD), F32),
                   _sds((1, 128), F32)],
        compiler_params=_params(("arbitrary",)), name="attn_bwd")(sinks, q4, kv4, kv4, do)


def _local_step(x, p, target, getw, sm, emit):
    S = x.shape[0]
    vec = lambda a: a.reshape(1, -1)
    ln_g = lambda l, k: vec(sm["ln_gain"][l, k])
    ln_b = lambda l, k: vec(sm["ln_bias"][l, k])
    xb = x.astype(CDT)
    pb = p.astype(CDT)

    proj = _mm_nn(xb, getw("a_w_in", None), (4, S, D), (None, _tile(S), 512), lambda g, i: (g // 2, i, g % 2), F32,
                  name="a_in")
    o_a, y_a, states = _hgrn_fwd(proj, sm["a_lower_bound"], sm["a_norm_gain"])
    zeros = jnp.zeros((1, D), F32)
    z = [[None] * 3 for _ in range(2)]
    xs = [[None] * 3 for _ in range(2)]
    xbs = [[None] * 3 for _ in range(2)]
    z[0][0], xs[0][0], xbs[0][0] = _mixout_ln(y_a[None], getw("a_w_out", y_a)[None], zeros, x, ln_g(0, 0), ln_b(0, 0),
                                              "a_out_ln")
    gu, hid, sgs, ups = [None, None], [None, None], [None, None], [None, None]

    def ffn_ple(l):
        wgu = getw(f"gu{l}", xbs[l][0])
        gu[l], hid[l], z[l][1], xs[l][1], xbs[l][1] = _ffn_fwd(
            xs[l][0], xbs[l][0], wgu, getw(f"dn{l}", None), ln_g(l, 1), ln_b(l, 1), f"ffn_fwd{l}")
        sgs[l], ups[l], z[l][2], xs[l][2], xbs[l][2] = _ple_fwd(
            xs[l][1], xbs[l][1], pb[l], getw(f"pg{l}", None), vec(sm["ple_b_gate"][l]), getw(f"pu{l}", None), ln_g(l, 2),
            ln_b(l, 2), f"ple_fwd{l}")

    ffn_ple(0)
    x3, x3b = xs[0][2], xbs[0][2]
    w_kv, w_q, w_bo = getw("kv_w", x3b), getw("b_w_q", None), getw("b_w_out", None)
    kv4 = _proj_heads(x3b, w_kv, vec(sm["kv_b"]), 2 * ATT_KVH, "kv_proj")
    q4 = _proj_heads(x3b, w_q, vec(sm["b_b_q"]), ATT_QH, "q_proj")
    o_b = _attn_fwd(q4, kv4, sm["b_sinks"])
    z[1][0], xs[1][0], xbs[1][0] = _mixout_ln(o_b[None], w_bo[None], sm["b_b_out"], x3, ln_g(1, 0), ln_b(1, 0),
                                              "b_out_ln")
    ffn_ple(1)
    loss, dy = _loss_fwd_bwd(xs[1][2], target)

    gs = {}
    d_ln_g = [[None] * 3 for _ in range(2)]
    d_ln_b = [[None] * 3 for _ in range(2)]
    g_bg = [None, None]

    def ffn_ple_bwd(l, dy, after=None):
        dx2, dgl, dup, d_ln_g[l][2], d_ln_b[l][2], g_bg[l] = _ple_bwd(dy, z[l][2], sgs[l], ups[l], ln_g(l, 2),
                                                                     getw(f"pg{l}", None), f"ple_bwd{l}", after=after)
        g_pg = _wgrad(xbs[l][1][None], dgl[None], f"g_ple_gate{l}")[0]
        g_pu = _wgrad(pb[l][None], dup[None], f"g_ple_up{l}")[0]
        dz2, dzb, dgu, d_ln_g[l][1], d_ln_b[l][1] = _ffn_bwd_hidden(dx2, z[l][1], gu[l], getw(f"dn{l}", None), ln_g(l, 1),
                                                                   f"ffn_bwd{l}")
        g_dn = _wgrad(hid[l], dzb[None], f"g_ffn_down{l}")
        g_gu = _wgrad(dgu.reshape(8, S, FFN_B), xbs[l][0][None], f"g_ffn_gate_up{l}")
        tok = emit({f"gu{l}": g_gu, f"dn{l}": g_dn, f"pg{l}": g_pg, f"pu{l}": g_pu})
        dx1 = _ffn_bwd_input(dz2, dgu, getw(f"gu{l}", None), f"ffn_bwd{l}", after=tok)
        return dx1, None

    dx1, tok = ffn_ple_bwd(1, dy)
    dz, dzb, do, d_ln_g[1][0], d_ln_b[1][0], gs["b_b_out"] = _mixout_bwd(dx1, z[1][0], ln_g(1, 0), w_bo[None], CDT,
                                                                        "b_out_bwd", after=tok)
    g_bo = _wgrad(o_b[None], dzb[None], "g_b_w_out")[0]
    dq, dkv4, dbq, dsinks = _attn_bwd(q4, kv4, sm["b_sinks"], do[0])
    gs["b_b_q"] = dbq
    gs["b_sinks"] = dsinks
    g_q = _wgrad(x3b[None], dq[None], "g_b_w_q")[0]
    dx3, dkv, gs["kv_b"] = _qkv_bwd(dz, dq, dkv4, w_q, w_kv, "qkv_bwd", after=tok)
    g_kv = _wgrad(x3b[None], dkv[None], "g_kv_w")[0]
    tok = emit({"b_w_out": g_bo, "b_w_q": g_q, "kv_w": g_kv})
    dx1, tok = ffn_ple_bwd(0, dx3, tok)
    w_ao = getw("a_w_out", None)
    dz, dzb, dyr, d_ln_g[0][0], d_ln_b[0][0], _ = _mixout_bwd(dx1, z[0][0], ln_g(0, 0), w_ao[None], F32, "a_out_bwd",
                                                              after=tok)
    g_ao = _wgrad(y_a[None], dzb[None], "g_a_w_out")[0]
    tok = emit({"a_w_out": g_ao})
    dproj, gs["a_lower_bound"], gs["a_norm_gain"] = _hgrn_bwd(proj, sm["a_lower_bound"], sm["a_norm_gain"], o_a, states,
                                                              dyr[0], after=tok)
    tk = lambda t: (None, t, D)
    g_ain = _mm_tn(xb[None], dproj, N_DEV, lambda g, k: (0, k, 0), lambda g, k: (g // 2, k, g % 2),
                   tk, lambda t: (None, t, 512), (N_DEV, D, 512), (None, D, 512), lambda g, k: (g, 0, 0), name="g_a_w_in")
    gs["ple_b_gate"] = jnp.concatenate(g_bg, axis=0)
    gs["ln_gain"] = jnp.stack([jnp.concatenate(r, axis=0) for r in d_ln_g])
    gs["ln_bias"] = jnp.stack([jnp.concatenate(r, axis=0) for r in d_ln_b])
    gs["loss"] = loss
    tok = emit({"a_w_in": g_ain}, small=gs)
    grad_x = _inproj_bwd(dz, dproj, getw("a_w_in", None), "a_in_bwd", after=tok)
    return loss, grad_x, gs


def _peer(k):
    x, y, c = lax.axis_index("x"), lax.axis_index("y"), lax.axis_index("c")
    px = 1 - x if k & 4 else x
    py = 1 - y if k & 2 else y
    pc = 1 - c if k & 1 else c
    return (px, py, pc), 4 * px + 2 * py + pc


def _my_index():
    return 4 * lax.axis_index("x") + 2 * lax.axis_index("y") + lax.axis_index("c")


def _exchange(srcs, dst_shapes, plan, name):
    n_src, n_piece = len(srcs), len(plan)

    def body(*refs):
        src_refs, dst_refs = refs[:n_src], refs[n_src:n_src + len(dst_shapes)]
        send_sems, recv_sems, local_sems = refs[n_src + len(dst_shapes):]
        me = _my_index()

        def at(ref, idx):
            return ref.at[idx] if idx else ref

        local = []
        for t, (si, sfn, di, dfn) in enumerate(plan):
            cp = pltpu.make_async_copy(at(src_refs[si], sfn(me)), at(dst_refs[di], dfn(me)), local_sems.at[t])
            cp.start()
            local.append(cp)
        sends = []
        for k in range(1, N_DEV):
            peer, pid = _peer(k)
            for t, (si, sfn, di, dfn) in enumerate(plan):
                cp = pltpu.make_async_remote_copy(
                    src_ref=at(src_refs[si], sfn(pid)), dst_ref=at(dst_refs[di], dfn(me)),
                    send_sem=send_sems.at[t * 7 + k - 1], recv_sem=recv_sems.at[t * 7 + k - 1],
                    device_id=peer, device_id_type=MESH)
                cp.start()
                sends.append(cp)
        for k in range(1, N_DEV):
            peer, pid = _peer(k)
            for t, (si, sfn, di, dfn) in enumerate(plan):
                pltpu.make_async_remote_copy(
                    src_ref=at(src_refs[si], sfn(me)), dst_ref=at(dst_refs[di], dfn(pid)),
                    send_sem=send_sems.at[t * 7 + k - 1], recv_sem=recv_sems.at[t * 7 + k - 1],
                    device_id=peer, device_id_type=MESH).wait_recv()
        for cp in sends:
            cp.wait_send()
        for cp in local:
            cp.wait()

    hbm = pl.BlockSpec(memory_space=pltpu.HBM)
    return _call(
        None, body, in_specs=[hbm] * n_src, out_specs=[hbm] * len(dst_shapes), out_shape=dst_shapes,
        scratch_shapes=[pltpu.SemaphoreType.DMA((7 * n_piece,)), pltpu.SemaphoreType.DMA((7 * n_piece,)),
                        pltpu.SemaphoreType.DMA((n_piece,))],
        name=name)(*srcs)


def _gather(shards, name):
    dsts = [_sds((N_DEV,) + a.shape, a.dtype) for a in shards]
    plan = [(i, lambda j: (), i, lambda s: (s,)) for i in range(len(shards))]
    return _exchange(shards, dsts, plan, name)


_HBM = pl.BlockSpec(memory_space=pltpu.HBM)
_SEM = pl.BlockSpec(memory_space=pltpu.SEMAPHORE)
_DATAFLOW = pltpu.SideEffectType.DATAFLOW_SIDE_EFFECTING


def _piece_copy(mode, src, land, send_sems, recv_sems, t, k, sender, receiver, peer):
    return pltpu.make_async_remote_copy(
        src_ref=src if mode == "gather" else src.at[receiver], dst_ref=land.at[sender],
        send_sem=send_sems.at[t * 7 + k - 1], recv_sem=recv_sems.at[t * 7 + k - 1], device_id=peer, device_id_type=MESH)


def _sequencer_exchange(srcs, modes, name, collective_id, after=None):
    n = len(srcs)
    land_shapes = [((N_DEV,) + a.shape) if mode == "gather" else a.shape for a, mode in zip(srcs, modes)]
    extra = [] if after is None else [after]

    def body(*refs):
        src_refs, land_refs = refs[:n], refs[n + len(extra):2 * n + len(extra)]
        send_sems, recv_sems, local_sems = refs[2 * n + len(extra):]
        barrier = pltpu.get_barrier_semaphore()
        for k in range(1, N_DEV):
            pl.semaphore_signal(barrier, inc=1, device_id=_peer(k)[0], device_id_type=MESH)
        pl.semaphore_wait(barrier, N_DEV - 1)
        me = _my_index()
        local = []
        for i in range(n):
            cp = pltpu.make_async_copy(src_refs[i] if modes[i] == "gather" else src_refs[i].at[me], land_refs[i].at[me],
                                       local_sems.at[i])
            cp.start()
            local.append(cp)
        for k in range(1, N_DEV):
            peer, pid = _peer(k)
            for t in range(n):
                _piece_copy(modes[t], src_refs[t], land_refs[t], send_sems, recv_sems, t, k, me, pid, peer).start()
        for k in range(1, N_DEV):
            peer, pid = _peer(k)
            for t in range(n):
                _piece_copy(modes[t], src_refs[t], land_refs[t], send_sems, recv_sems, t, k, pid, me, peer).wait_recv()
        for k in range(1, N_DEV):
            peer, pid = _peer(k)
            for t in range(n):
                _piece_copy(modes[t], src_refs[t], land_refs[t], send_sems, recv_sems, t, k, me, pid, peer).wait_send()
        for cp in local:
            cp.wait()

    return pl.kernel(
        body, out_type=[_sds(s, a.dtype) for s, a in zip(land_shapes, srcs)],
        mesh=plsc.ScalarSubcoreMesh(axis_name="sequencer", num_cores=1),
        scratch_types=[pltpu.SemaphoreType.DMA((7 * n,)), pltpu.SemaphoreType.DMA((7 * n,)), pltpu.SemaphoreType.DMA((n,))],
        compiler_params=pltpu.CompilerParams(collective_id=collective_id), name=name)(*srcs, *extra)


def _sequencer_sibling_swap(blocks, name, collective_id):
    n = len(blocks)

    def body(*refs):
        srcs, theirs = refs[:n], refs[n:2 * n]
        send_sems, recv_sems = refs[2 * n:]
        x, y, c = lax.axis_index("x"), lax.axis_index("y"), lax.axis_index("c")
        sibling = (x, y, 1 - c)
        barrier = pltpu.get_barrier_semaphore()
        pl.semaphore_signal(barrier, inc=1, device_id=sibling, device_id_type=MESH)
        pl.semaphore_wait(barrier, 1)
        copies = []
        for t in range(n):
            for q in range(4):
                copies.append(pltpu.make_async_remote_copy(
                    src_ref=srcs[t].at[2 * q + (1 - c)], dst_ref=theirs[t].at[q], send_sem=send_sems.at[4 * t + q],
                    recv_sem=recv_sems.at[4 * t + q], device_id=sibling, device_id_type=MESH))
        for cp in copies:
            cp.start()
        for cp in copies:
            cp.wait_recv()
        for cp in copies:
            cp.wait_send()

    return pl.kernel(
        body, out_type=[_sds((4,) + a.shape[1:], a.dtype) for a in blocks],
        mesh=plsc.ScalarSubcoreMesh(axis_name="sequencer", num_cores=1),
        scratch_types=[pltpu.SemaphoreType.DMA((4 * n,)), pltpu.SemaphoreType.DMA((4 * n,))],
        compiler_params=pltpu.CompilerParams(collective_id=collective_id), name=name)(*blocks)


def _pair_sums(blocks, theirs, name):
    n = len(blocks)

    def body(*refs):
        c = lax.axis_index("c")
        for t in range(n):
            mine = jnp.where(c == 0, refs[3 * t][...].astype(F32), refs[3 * t + 1][...].astype(F32))
            refs[3 * n + t][...] = (mine + refs[3 * t + 2][...].astype(F32)).astype(CDT)

    in_specs, out_specs, steps = [], [], 1
    for a in blocks:
        _, R, C = a.shape
        tr = _tile(R, (256, 128, 176, 64, 32, 16))
        row = lambda q, i, nr=R // tr: jnp.minimum(i, nr - 1)
        in_specs += [pl.BlockSpec((None, None, tr, C), lambda q, i, row=row: (q, 0, row(q, i), 0)),
                     pl.BlockSpec((None, None, tr, C), lambda q, i, row=row: (q, 1, row(q, i), 0)),
                     pl.BlockSpec((None, tr, C), lambda q, i, row=row: (q, row(q, i), 0))]
        out_specs.append(pl.BlockSpec((None, tr, C), lambda q, i, row=row: (q, row(q, i), 0)))
        steps = max(steps, R // tr)
    args = []
    for a, th in zip(blocks, theirs):
        by_core = a.reshape((4, 2) + a.shape[1:])
        args += [by_core, by_core, th]
    return _call(
        None, body, grid=(4, steps), in_specs=in_specs, out_specs=out_specs,
        out_shape=[_sds(th.shape, CDT) for th in theirs],
        compiler_params=_params(("arbitrary", "arbitrary")), name=name)(*args)


def _sequencer_chip_scatter(pairs, name, collective_id):
    n = len(pairs)

    def body(*refs):
        srcs, lands = refs[:n], refs[n:2 * n]
        send_sems, recv_sems = refs[2 * n:]
        x, y, c = lax.axis_index("x"), lax.axis_index("y"), lax.axis_index("c")
        chips = [(1 - x, y), (x, 1 - y), (1 - x, 1 - y)]
        chip = lambda px, py: 2 * px + py
        barrier = pltpu.get_barrier_semaphore()
        for px, py in chips:
            pl.semaphore_signal(barrier, inc=1, device_id=(px, py, c), device_id_type=MESH)
        pl.semaphore_wait(barrier, 3)
        here = chip(x, y)
        sends = []
        for t in range(n):
            for j, (px, py) in enumerate(chips):
                sends.append(pltpu.make_async_remote_copy(
                    src_ref=srcs[t].at[chip(px, py)], dst_ref=lands[t].at[here], send_sem=send_sems.at[3 * t + j],
                    recv_sem=recv_sems.at[3 * t + j], device_id=(px, py, c), device_id_type=MESH))
        for cp in sends:
            cp.start()
        for t in range(n):
            for j, (px, py) in enumerate(chips):
                pltpu.make_async_remote_copy(
                    src_ref=srcs[t].at[here], dst_ref=lands[t].at[chip(px, py)], send_sem=send_sems.at[3 * t + j],
                    recv_sem=recv_sems.at[3 * t + j], device_id=(px, py, c), device_id_type=MESH).wait_recv()
        for cp in sends:
            cp.wait_send()

    return pl.kernel(
        body, out_type=[_sds(a.shape, a.dtype) for a in pairs],
        mesh=plsc.ScalarSubcoreMesh(axis_name="sequencer", num_cores=1),
        scratch_types=[pltpu.SemaphoreType.DMA((3 * n,)), pltpu.SemaphoreType.DMA((3 * n,))],
        compiler_params=pltpu.CompilerParams(collective_id=collective_id), name=name)(*pairs)


def _sequencer_gather(srcs, name, collective_id, after=None):
    n = len(srcs)
    extra = [] if after is None else [after]

    def body(*refs):
        src_refs, land_refs = refs[:n], refs[n + len(extra):2 * n + len(extra)]
        send_sems, recv_sems, local_sems = refs[2 * n + len(extra):]
        x, y, c = lax.axis_index("x"), lax.axis_index("y"), lax.axis_index("c")
        sibling = (x, y, 1 - c)
        chips = [(1 - x, y), (x, 1 - y), (1 - x, 1 - y)]
        index = lambda px, py, pc: 4 * px + 2 * py + pc
        barrier = pltpu.get_barrier_semaphore()
        for peer in [sibling] + [(*chip, c) for chip in chips]:
            pl.semaphore_signal(barrier, inc=1, device_id=peer, device_id_type=MESH)
        pl.semaphore_wait(barrier, 4)

        def copy(t, k, slot, to, src=None):
            return pltpu.make_async_remote_copy(
                src_ref=land_refs[t].at[slot] if src is None else src, dst_ref=land_refs[t].at[slot],
                send_sem=send_sems.at[7 * t + k], recv_sem=recv_sems.at[7 * t + k], device_id=to, device_id_type=MESH)

        me = index(x, y, c)
        local = []
        for t in range(n):
            cp = pltpu.make_async_copy(src_refs[t], land_refs[t].at[me], local_sems.at[t])
            cp.start()
            local.append(cp)
        sends = []
        for t in range(n):
            sends.append(copy(t, 0, me, sibling, src=src_refs[t]))
            sends += [copy(t, 1 + j, me, (*chip, c), src=src_refs[t]) for j, chip in enumerate(chips)]
        for cp in sends:
            cp.start()
        for j, chip in enumerate(chips):
            for t in range(n):
                copy(t, 1 + j, index(*chip, c), sibling, src=src_refs[t]).wait_recv()
                passed = copy(t, 4 + j, index(*chip, c), sibling)
                passed.start()
                sends.append(passed)
        for t in range(n):
            copy(t, 0, index(x, y, 1 - c), sibling, src=src_refs[t]).wait_recv()
        for j, chip in enumerate(chips):
            for t in range(n):
                copy(t, 4 + j, index(*chip, 1 - c), sibling, src=src_refs[t]).wait_recv()
        for cp in sends:
            cp.wait_send()
        for cp in local:
            cp.wait()

    return pl.kernel(
        body, out_type=[_sds((N_DEV,) + a.shape, a.dtype) for a in srcs],
        mesh=plsc.ScalarSubcoreMesh(axis_name="sequencer", num_cores=1),
        scratch_types=[pltpu.SemaphoreType.DMA((7 * n,)), pltpu.SemaphoreType.DMA((7 * n,)), pltpu.SemaphoreType.DMA((n,))],
        compiler_params=pltpu.CompilerParams(collective_id=collective_id), name=name)(*srcs, *extra)


def _xstart(groups, mode, name, after=None):
    flat = [a for g in groups for a in g]
    n, ng = len(flat), len(groups)
    land_shapes = [((N_DEV,) + a.shape) if mode == "gather" else a.shape for a in flat]
    first = [sum(len(g) for g in groups[:i]) for i in range(ng)]
    extra = [] if after is None else [after]

    def body(*refs):
        srcs, lands = refs[:n], refs[n:2 * n]
        sems = refs[2 * n + len(extra):2 * n + len(extra) + 2 * ng]
        tok_ref, local_sems = refs[-2], refs[-1]
        me = _my_index()
        local = []
        for i in range(n):
            cp = pltpu.make_async_copy(srcs[i] if mode == "gather" else srcs[i].at[me], lands[i].at[me], local_sems.at[i])
            cp.start()
            local.append(cp)
        for cp in local:
            cp.wait()
        for gi, g in enumerate(groups):
            for k in range(1, N_DEV):
                peer, pid = _peer(k)
                for t in range(len(g)):
                    i = first[gi] + t
                    _piece_copy(mode, srcs[i], lands[i], sems[2 * gi], sems[2 * gi + 1], t, k, me, pid, peer).start()
        tok_ref[...] = jnp.zeros_like(tok_ref)

    sem_shapes = []
    for g in groups:
        sem_shapes += [pltpu.SemaphoreType.DMA((7 * len(g),))] * 2
    thru = [pltpu.HBM(a.shape, a.dtype) for a in flat] + [pltpu.HBM(s, a.dtype) for s, a in zip(land_shapes, flat)]
    outs = pl.pallas_call(
        body, in_specs=[_HBM] * (2 * n) + [pl.BlockSpec(memory_space=pl.ANY)] * len(extra),
        out_specs=[_SEM] * (2 * ng) + [_HBM] * (2 * n) + [pl.BlockSpec(memory_space=pltpu.VMEM)],
        out_shape=sem_shapes + thru + [_sds((8, 128), F32)],
        input_output_aliases={i: 2 * ng + i for i in range(2 * n)},
        scratch_shapes=[pltpu.SemaphoreType.DMA((n,))],
        compiler_params=pltpu.CompilerParams(has_side_effects=_DATAFLOW), name=name)(
            *[pltpu.with_memory_space_constraint(a, pltpu.HBM) for a in flat],
            *[pltpu.with_memory_space_constraint(lax.empty(s, a.dtype), pltpu.HBM) for s, a in zip(land_shapes, flat)], *extra)
    sems, srcs_thru, lands_thru = outs[:2 * ng], outs[2 * ng:2 * ng + n], outs[2 * ng + n:2 * ng + 2 * n]
    handles = [(sems[2 * gi], sems[2 * gi + 1], srcs_thru[first[gi]:first[gi] + len(g)],
                lands_thru[first[gi]:first[gi] + len(g)]) for gi, g in enumerate(groups)]
    return handles, outs[-1]


def _xwait(handle, mode, after, name):
    send_sems, recv_sems, srcs_thru, lands_thru = handle
    n = len(srcs_thru)

    def body(*refs):
        srcs, lands, send, recv = refs[:n], refs[n:2 * n], refs[2 * n], refs[2 * n + 1]
        me = _my_index()
        for k in range(1, N_DEV):
            peer, pid = _peer(k)
            for t in range(n):
                _piece_copy(mode, srcs[t], lands[t], send, recv, t, k, pid, me, peer).wait_recv()
        for k in range(1, N_DEV):
            peer, pid = _peer(k)
            for t in range(n):
                _piece_copy(mode, srcs[t], lands[t], send, recv, t, k, me, pid, peer).wait_send()

    extra = [] if after is None else [after]
    outs = pl.pallas_call(
        body, in_specs=[_HBM] * (2 * n) + [_SEM, _SEM] + [pl.BlockSpec(memory_space=pl.ANY)] * len(extra),
        out_specs=[_HBM] * (2 * n),
        out_shape=[pltpu.HBM(a.shape, a.dtype) for a in list(srcs_thru) + list(lands_thru)],
        input_output_aliases={i: i for i in range(2 * n)},
        compiler_params=pltpu.CompilerParams(has_side_effects=_DATAFLOW), name=name)(
            *srcs_thru, *lands_thru, send_sems, recv_sems, *extra)
    return outs[n:]


def _adamw(w, g, m, v):
    m = ADAM_B1 * m + (1.0 - ADAM_B1) * g
    v = ADAM_B2 * v + (1.0 - ADAM_B2) * (g * g)
    m_hat = m / (1.0 - ADAM_B1 ** ADAM_STEP)
    v_hat = v / (1.0 - ADAM_B2 ** ADAM_STEP)
    delta = -ADAM_LR * (m_hat / (jnp.sqrt(v_hat) + ADAM_EPS) + ADAM_WD * w)
    return delta, m, v


def _adam_big(w, parts, m, v, name, after=None):
    L, R, C = w.shape
    P = parts[0].shape[0]
    tr = _tile(R, (256, 128, 176, 64, 32, 16))
    nr = R // tr

    def body(w_ref, *refs):
        p_refs, (m_ref, v_ref, g_ref, d_ref, mo_ref, vo_ref) = refs[:L], refs[L:]
        for l in range(L):
            @pl.when(pl.program_id(0) == l)
            def _(p_ref=p_refs[l]):
                g = p_ref[0].astype(F32)
                for s in range(1, P):
                    g = g + p_ref[s].astype(F32)
                g_ref[...] = g
                d_ref[...], mo_ref[...], vo_ref[...] = _adamw(w_ref[...], g, m_ref[...], v_ref[...])

    row = pl.BlockSpec((None, tr, C), lambda l, i: (l, i, 0))
    park = lambda l_of: (lambda l, i: (0, jnp.where(l == l_of, i, 0 if l_of else nr - 1), 0))
    return _call(
        after, body, grid=(L, nr),
        in_specs=[row] + [pl.BlockSpec((P, tr, C), park(l)) for l in range(L)] + [row, row],
        out_specs=[row] * 4, out_shape=[_sds((L, R, C), F32)] * 4,
        compiler_params=_params(("arbitrary", "arbitrary")), name=name)(w, *parts, m, v)


SMALL = (("a_lower_bound", (2, 128), (2, D)), ("ln_gain", (6, 128), (6, D)), ("ln_bias", (6, 128), (6, D)),
         ("a_norm_gain", (1, 128), (1, 128)), ("kv_b", (1, 512), (1, 512)), ("b_b_q", (1, D), (1, D)),
         ("b_sinks", (1, ATT_QH), (1, 128)), ("b_b_out", (1, D), (1, D)), ("ple_b_gate", (2, D), (2, D)))


def _adam_small(parts, w, m, v, losses, after=None):
    k = len(SMALL)

    def body(*refs):
        p_refs, w_refs, m_refs, v_refs = refs[:k], refs[k:2 * k], refs[2 * k:3 * k], refs[3 * k:4 * k]
        loss_ref, outs, total_ref = refs[4 * k], refs[4 * k + 1:-1], refs[-1]
        total = loss_ref[0]
        for s in range(1, N_DEV):
            total = total + loss_ref[s]
        total_ref[...] = total
        me = _my_index()
        for i, (_, wshape, pshape) in enumerate(SMALL):
            cols = wshape[1]
            lanes = slice(None) if cols == pshape[1] else (
                pl.ds(0, cols) if cols < 128 else pl.ds(pl.multiple_of(me * cols, cols), cols))
            g = p_refs[i][0, :, lanes]
            for s in range(1, N_DEV):
                g = g + p_refs[i][s, :, lanes]
            g_ref, d_ref, mo_ref, vo_ref = outs[4 * i:4 * i + 4]
            g_ref[...] = g
            d_ref[...], mo_ref[...], vo_ref[...] = _adamw(w_refs[i][...], g, m_refs[i][...], v_refs[i][...])

    full = lambda shape: pl.BlockSpec(shape, lambda: (0,) * len(shape))
    names = [n for n, _, _ in SMALL]
    res = _call(
        after, body,
        in_specs=[full((N_DEV,) + ps) for _, _, ps in SMALL] + [full(ws) for _, ws, _ in SMALL] * 3
        + [full((N_DEV, 1, 128))],
        out_specs=[full(ws) for _, ws, _ in SMALL for _ in range(4)] + [full((1, 128))],
        out_shape=[_sds(ws, F32) for _, ws, _ in SMALL for _ in range(4)] + [_sds((1, 128), F32)], name="adam_small")(
            *[parts[n] for n in names], *[a[n].reshape(ws) for a in (w, m, v) for n, ws, _ in SMALL], losses)
    return {n: [r.reshape(w[n].shape) for r in res[4 * i:4 * i + 4]] for i, n in enumerate(names)}, res[-1][0, 0]


WEIGHTS = ("a_w_in", "a_lower_bound", "a_norm_gain", "a_w_out", "kv_w", "kv_b", "b_w_q", "b_b_q", "b_sinks", "b_w_out",
           "b_b_out", "ffn_w_gate_up", "ffn_w_down", "ple_w_up", "ple_w_gate", "ple_b_gate", "ln_gain", "ln_bias")


GATHER_GROUPS = (("a_w_in",), ("a_w_out", "gu0"), ("dn0", "pu0", "pg0"), ("kv_w", "b_w_q", "b_w_out"),
                 ("gu1", "dn1", "pu1", "pg1"))
KERNEL_LAYOUT = {
    "a_w_in": lambda a: a,
    "a_w_out": lambda a: a.reshape(D, D),
    "kv_w": lambda a: a.reshape(D, 2 * ATT_KVH * ATT_HD),
    "b_w_q": lambda a: a.reshape(D, D),
    "b_w_out": lambda a: a.reshape(D, D),
    "gu": lambda a: a.reshape(2, 4, FFN_B, D),
    "dn": lambda a: a.reshape(4, FFN_B, D),
    "pu": lambda a: a,
    "pg": lambda a: a.reshape(D, D),
}
_row_blocks = lambda a: a.reshape(N_DEV, -1, a.shape[-1])
OWNER_BLOCKS = {
    "a_w_in": lambda g: g,
    "a_w_out": _row_blocks,
    "kv_w": _row_blocks,
    "b_w_q": _row_blocks,
    "b_w_out": _row_blocks,
    "gu": lambda g: g,
    "dn": lambda g: _row_blocks(g.reshape(FFN_H, D)),
    "pu": lambda g: g.reshape(PLE_DIM, N_DEV, 128).transpose(1, 0, 2),
    "pg": _row_blocks,
}
ADAM_PARTS = (("kv_w", ("kv_w",)), ("b_w_q", ("b_w_q",)), ("b_w_out", ("b_w_out",)), ("ffn_w_gate_up", ("gu0", "gu1")),
              ("ffn_w_down", ("dn0", "dn1")), ("ple_w_up", ("pu0", "pu1")), ("ple_w_gate", ("pg0", "pg1")),
              ("a_w_out", ("a_w_out",)), ("a_w_in", ("a_w_in",)))


def kernel(x, p, a_w_in, a_lower_bound, a_norm_gain, a_w_out, kv_w, kv_b, b_w_q, b_b_q, b_sinks, b_w_out, b_b_out, ffn_w_gate_up, ffn_w_down, ple_w_up, ple_w_gate, ple_b_gate, ln_gain, ln_bias, loss_target, m_a_w_in, m_a_lower_bound, m_a_norm_gain, m_a_w_out, m_kv_w, m_kv_b, m_b_w_q, m_b_b_q, m_b_sinks, m_b_w_out, m_b_b_out, m_ffn_w_gate_up, m_ffn_w_down, m_ple_w_up, m_ple_w_gate, m_ple_b_gate, m_ln_gain, m_ln_bias, v_a_w_in, v_a_lower_bound, v_a_norm_gain, v_a_w_out, v_kv_w, v_kv_b, v_b_w_q, v_b_b_q, v_b_sinks, v_b_w_out, v_b_b_out, v_ffn_w_gate_up, v_ffn_w_down, v_ple_w_up, v_ple_w_gate, v_ple_b_gate, v_ln_gain, v_ln_bias):
    given = dict(locals())
    w = {n: given[n] for n in WEIGHTS}
    m = {n: given["m_" + n] for n in WEIGHTS}
    v = {n: given["v_" + n] for n in WEIGHTS}
    shards = {"a_w_in": a_w_in[0], "a_w_out": a_w_out[0], "kv_w": kv_w, "b_w_q": b_w_q[0], "b_w_out": b_w_out[0]}
    for l in range(2):
        shards.update({f"gu{l}": ffn_w_gate_up[l].T, f"dn{l}": ffn_w_down[l], f"pu{l}": ple_w_up[l], f"pg{l}": ple_w_gate[l]})
    sharded_small = [a_lower_bound, ln_gain.reshape(6, 128), ln_bias.reshape(6, 128)]
    gathered = {}
    for gi, g in enumerate(GATHER_GROUPS):
        lands = _sequencer_gather([shards[n].astype(CDT) for n in g] + (sharded_small if gi == 0 else []),
                                  f"gather{gi}", gi)
        for n, a in zip(g, lands):
            gathered[n] = KERNEL_LAYOUT[n.rstrip("01")](a)
        if gi == 0:
            alb, lng, lnb = [a.transpose(1, 0, 2).reshape(a.shape[1], D) for a in lands[len(g):]]

    def getw(key, after):
        return gathered[key]

    sm = {"a_lower_bound": alb, "ln_gain": lng.reshape(2, 3, D), "ln_bias": lnb.reshape(2, 3, D),
          "a_norm_gain": a_norm_gain, "kv_b": kv_b, "b_b_q": b_b_q[0], "b_sinks": b_sinks, "b_b_out": b_b_out,
          "ple_b_gate": ple_b_gate}

    scatters, small_parts = [], {}

    def emit(grads, small=None):
        names = list(grads)
        blocks = [OWNER_BLOCKS[n.rstrip("01")](grads[n]) for n in names]
        partials = [] if small is None else [small[n].reshape(ps) for n, _, ps in SMALL] + [small["loss"]]
        lands = _sequencer_exchange(blocks + partials, ["scatter"] * len(blocks) + ["gather"] * len(partials),
                                    f"scatter{len(scatters)}", len(GATHER_GROUPS) + len(scatters))
        scatters.append(dict(zip(names, lands)))
        small_parts.update(zip([n for n, _, _ in SMALL] + ["loss"], lands[len(blocks):]))
        return blocks

    loss, grad_x, gs = _local_step(x[0], p[:, 0], loss_target[0], getw, sm, emit)

    out, parts, last = {}, {}, [grad_x] + list(scatters[0].values())
    for landed in scatters:
        parts.update(landed)
        for n, keys in ADAM_PARTS:
            if n in out or not all(key in parts for key in keys):
                continue
            lrc = (1,) * (3 - w[n].ndim) + w[n].shape
            shard = (lambda a: a.reshape(lrc).swapaxes(1, 2)) if n == "ffn_w_gate_up" else (lambda a: a.reshape(lrc))
            res = _adam_big(shard(w[n]), [parts[key] for key in keys], shard(m[n]), shard(v[n]), "adam_" + n, after=last)
            out[n] = [(r.swapaxes(1, 2) if n == "ffn_w_gate_up" else r).reshape(w[n].shape) for r in res]
            last = [res[3]]
    small_out, loss = _adam_small(small_parts, w, m, v, small_parts["loss"], after=last)
    out.update(small_out)
    res = [loss, grad_x[None]]
    for i in range(4):
        res += [out[n][i] for n in WEIGHTS]
    return tuple(res)
```

```python
import jax
import jax.numpy as jnp
from jax import lax
from jax.experimental import pallas as pl
from jax.experimental.pallas import tpu as pltpu
from jax.experimental.pallas import tpu_sc as plsc

F32 = jnp.float32
CDT = jnp.bfloat16

N_DEV = 8
D = 1024
HG_H, HG_DK, HG_CH = 8, 128, 64
HG_HPB = 4
HG_CPB = 8
ATT_HD, ATT_QH, ATT_KVH, ATT_G, WINDOW = 64, 16, 4, 4, 128
FFN_H = 2816
FFN_B = FFN_H // 4
PLE_DIM = 256
ALPHA = (2.0 * 2) ** 0.25
LN_EPS = 1e-5
RMS_EPS = 1e-6
ADAM_LR, ADAM_B1, ADAM_B2, ADAM_EPS, ADAM_WD, ADAM_STEP = 0.001, 0.9, 0.999, 1e-08, 0.01, 10
ROW_TILES = (512, 256, 128, 64)
VMEM_LIMIT = 48 * 1024 * 1024
NEG = -1e30

MESH = pl.DeviceIdType.MESH


def _tile(n, cands=ROW_TILES):
    for t in cands:
        if n % t == 0:
            return t
    return n


def _sds(shape, dtype):
    return jax.ShapeDtypeStruct(tuple(shape), dtype)


def _params(sem):
    return pltpu.CompilerParams(dimension_semantics=sem, vmem_limit_bytes=VMEM_LIMIT)


def _dot(a, b):
    return jnp.dot(a.astype(CDT), b.astype(CDT), preferred_element_type=F32)


def _dot_nt(a, b):
    return lax.dot_general(a.astype(CDT), b.astype(CDT), (((1,), (1,)), ((), ())), preferred_element_type=F32)


def _dot_tn(a, b):
    return lax.dot_general(a.astype(CDT), b.astype(CDT), (((0,), (0,)), ((), ())), preferred_element_type=F32)


def _sigmoid(x):
    return jax.nn.sigmoid(x)


def _ln_fwd(z, g, b):
    mu = jnp.mean(z, axis=-1, keepdims=True)
    zc = z - mu
    var = jnp.mean(zc * zc, axis=-1, keepdims=True)
    return zc * lax.rsqrt(var + LN_EPS) * g + b


def _ln_bwd(z, g, dy):
    mu = jnp.mean(z, axis=-1, keepdims=True)
    zc = z - mu
    var = jnp.mean(zc * zc, axis=-1, keepdims=True)
    rstd = lax.rsqrt(var + LN_EPS)
    xhat = zc * rstd
    dxh = dy * g
    dz = rstd * (dxh - jnp.mean(dxh, axis=-1, keepdims=True) - xhat * jnp.mean(dxh * xhat, axis=-1, keepdims=True))
    return dz, xhat


def _colsum(x):
    return jnp.sum(x, axis=0, keepdims=True)


def _acc(ref, val, first):
    @pl.when(first)
    def _():
        ref[...] = val

    @pl.when(jnp.logical_not(first))
    def _():
        ref[...] += val


def _zero_at(ref, first):
    @pl.when(first)
    def _():
        ref[...] = jnp.zeros_like(ref)


def _call(after, body, **kw):
    after = [] if after is None else list(after)
    specs = list(kw["in_specs"])
    kw["in_specs"] = [pl.BlockSpec(memory_space=pl.ANY)] * len(after) + specs

    def ordered_body(*refs):
        body(*refs[len(after):])

    call = pl.pallas_call(ordered_body, **kw)

    def pinned(*args):
        args = [a if s.memory_space is not None else pltpu.with_memory_space_constraint(a, pltpu.HBM)
                for a, s in zip(args, specs)]
        return call(*after, *args)

    return pinned


def _mm_nn(a, b3, out_shape, oblock, omap, out_dtype, bias3=None, name="mm_nn"):
    M, K = a.shape
    G, _, Nb = b3.shape
    tm = _tile(M)

    def body(a_ref, b_ref, *rest):
        o_ref = rest[-1]
        acc = _dot(a_ref[...], b_ref[...])
        if bias3 is not None:
            acc = acc + rest[0][...]
        o_ref[...] = acc.astype(o_ref.dtype)

    in_specs = [pl.BlockSpec((tm, K), lambda g, i: (i, 0)), pl.BlockSpec((None, K, Nb), lambda g, i: (g, 0, 0))]
    args = [a, b3]
    if bias3 is not None:
        in_specs.append(pl.BlockSpec((None, 1, Nb), lambda g, i: (g, 0, 0)))
        args.append(bias3)
    return _call(
        None, body, grid=(G, M // tm), in_specs=in_specs, out_specs=pl.BlockSpec(oblock, omap),
        out_shape=_sds(out_shape, out_dtype), compiler_params=_params(("arbitrary", "arbitrary")), name=name)(*args)


def _mm_tn(a3, b3, G, amap, bmap, ablock, bblock, out_shape, oblock, omap, name="mm_tn"):
    S = a3.shape[1]

    def body(a_ref, b_ref, o_ref):
        o_ref[...] = _dot_tn(a_ref[...], b_ref[...]).astype(o_ref.dtype)

    return _call(
        None, body, grid=(G, 1),
        in_specs=[pl.BlockSpec(ablock(S), amap), pl.BlockSpec(bblock(S), bmap)],
        out_specs=pl.BlockSpec(oblock, omap), out_shape=_sds(out_shape, CDT),
        compiler_params=_params(("arbitrary", "arbitrary")), name=name)(a3, b3)


def _wgrad(a3, b3, name):
    Ga, S, M = a3.shape
    Gb, _, N = b3.shape
    G = max(Ga, Gb)
    return _mm_tn(
        a3, b3, G,
        (lambda g, k: (g, k, 0)) if Ga > 1 else (lambda g, k: (0, k, 0)),
        (lambda g, k: (g, k, 0)) if Gb > 1 else (lambda g, k: (0, k, 0)),
        lambda tk: (None, tk, M), lambda tk: (None, tk, N),
        (G, M, N), (None, M, N), lambda g, k: (g, 0, 0), name=name)


def _mixout_ln(u3, w3, bias, xin, gain, beta, name):
    G, S, Kb = u3.shape
    tm = _tile(S)

    def body(u_ref, w_ref, b_ref, x_ref, g_ref, be_ref, z_ref, xo_ref, xob_ref):
        h = b_ref[...] + _dot(u_ref[0], w_ref[0])
        for g in range(1, G):
            h = h + _dot(u_ref[g], w_ref[g])
        z = ALPHA * x_ref[...] + h
        z_ref[...] = z
        y = _ln_fwd(z, g_ref[...], be_ref[...])
        xo_ref[...] = y
        xob_ref[...] = y.astype(CDT)

    row = pl.BlockSpec((tm, D), lambda i: (i, 0))
    vec = pl.BlockSpec((1, D), lambda i: (0, 0))
    return _call(
        None, body, grid=(S // tm,),
        in_specs=[pl.BlockSpec((G, tm, Kb), lambda i: (0, i, 0)), pl.BlockSpec((G, Kb, D), lambda i: (0, 0, 0)),
                  vec, row, vec, vec],
        out_specs=[row, row, row], out_shape=[_sds((S, D), F32), _sds((S, D), F32), _sds((S, D), CDT)],
        compiler_params=_params(("arbitrary",)), name=name)(u3, w3, bias, xin, gain, beta)


def _ffn_fwd(xin, xin_b, wgu, wdn, gain, beta, name):
    S = xin.shape[0]
    tm = _tile(S)

    def hidden(xb_ref, wgu_ref, gu_ref, hid_ref):
        xb = xb_ref[...]
        gate = _dot_nt(xb, wgu_ref[0])
        up = _dot_nt(xb, wgu_ref[1])
        gu_ref[0] = gate
        gu_ref[1] = up
        hid_ref[...] = (gate * _sigmoid(gate) * up).astype(CDT)

    gu, hid = _call(
        None, hidden, grid=(S // tm, 4),
        in_specs=[pl.BlockSpec((tm, D), lambda i, j: (i, 0)), pl.BlockSpec((2, None, FFN_B, D), lambda i, j: (0, j, 0, 0))],
        out_specs=[pl.BlockSpec((2, None, tm, FFN_B), lambda i, j: (0, j, i, 0)),
                   pl.BlockSpec((None, tm, FFN_B), lambda i, j: (j, i, 0))],
        out_shape=[_sds((2, 4, S, FFN_B), F32), _sds((4, S, FFN_B), CDT)],
        compiler_params=_params(("arbitrary", "arbitrary")), name=name + "_hidden")(xin_b, wgu)

    def down(x_ref, hid_ref, wdn_ref, g_ref, be_ref, z_ref, xo_ref, xob_ref):
        z = ALPHA * x_ref[...]
        for j in range(4):
            z = z + _dot(hid_ref[j], wdn_ref[j])
        z_ref[...] = z
        y = _ln_fwd(z, g_ref[...], be_ref[...])
        xo_ref[...] = y
        xob_ref[...] = y.astype(CDT)

    row = pl.BlockSpec((tm, D), lambda i: (i, 0))
    vec = pl.BlockSpec((1, D), lambda i: (0, 0))
    z, xo, xob = _call(
        None, down, grid=(S // tm,),
        in_specs=[row, pl.BlockSpec((4, tm, FFN_B), lambda i: (0, i, 0)), pl.BlockSpec((4, FFN_B, D), lambda i: (0, 0, 0)),
                  vec, vec],
        out_specs=[row, row, row], out_shape=[_sds((S, D), F32), _sds((S, D), F32), _sds((S, D), CDT)],
        compiler_params=_params(("arbitrary",)), name=name + "_down")(xin, hid, wdn, gain, beta)
    return gu, hid, z, xo, xob


def _ple_fwd(xin, xin_b, p_b, wpg, bgate, wpu, gain, beta, name):
    S = xin.shape[0]
    tm = _tile(S)

    def body(x_ref, xb_ref, p_ref, wpg_ref, bg_ref, wpu_ref, g_ref, be_ref, sg_ref, up_ref, z_ref, xo_ref, xob_ref):
        sg = _sigmoid(_dot(xb_ref[...], wpg_ref[...]) + bg_ref[...])
        pb = p_ref[...]
        up = jnp.concatenate([_dot(pb, wpu_ref[j]) for j in range(N_DEV)], axis=-1)
        sg_ref[...] = sg
        up_ref[...] = up
        z = ALPHA * x_ref[...] + sg * up
        z_ref[...] = z
        y = _ln_fwd(z, g_ref[...], be_ref[...])
        xo_ref[...] = y
        xob_ref[...] = y.astype(CDT)

    row = pl.BlockSpec((tm, D), lambda i: (i, 0))
    vec = pl.BlockSpec((1, D), lambda i: (0, 0))
    return _call(
        None, body, grid=(S // tm,),
        in_specs=[row, row, pl.BlockSpec((tm, PLE_DIM), lambda i: (i, 0)), pl.BlockSpec((D, D), lambda i: (0, 0)), vec,
                  pl.BlockSpec((N_DEV, PLE_DIM, D // N_DEV), lambda i: (0, 0, 0)), vec, vec],
        out_specs=[row] * 5,
        out_shape=[_sds((S, D), F32)] * 4 + [_sds((S, D), CDT)],
        compiler_params=_params(("arbitrary",)), name=name)(xin, xin_b, p_b, wpg, bgate, wpu, gain, beta)


def _loss_fwd_bwd(y, target):
    S = y.shape[0]
    tm = _tile(S)

    def body(y_ref, t_ref, l_ref, dy_ref):
        e = y_ref[...] - t_ref[...]
        dy_ref[...] = e * (1.0 / D)
        part = 0.5 * jnp.sum(jnp.sum(e * e, axis=-1, keepdims=True) * (1.0 / D), axis=0, keepdims=True)
        _acc(l_ref, jnp.broadcast_to(part, l_ref.shape), pl.program_id(0) == 0)

    row = pl.BlockSpec((tm, D), lambda i: (i, 0))
    return _call(
        None, body, grid=(S // tm,), in_specs=[row, row],
        out_specs=[pl.BlockSpec((1, 128), lambda i: (0, 0)), row],
        out_shape=[_sds((1, 128), F32), _sds((S, D), F32)],
        compiler_params=_params(("arbitrary",)), name="loss")(y, target)


def _ple_bwd(dy, z, sg, up, gain, wpg, name, after=None):
    S = dy.shape[0]
    tm = _tile(S)

    def body(dy_ref, z_ref, sg_ref, up_ref, g_ref, wpg_ref, dx_ref, dgl_ref, dup_ref, dgain_ref, dbeta_ref, dbg_ref):
        first = pl.program_id(0) == 0
        dy_ = dy_ref[...]
        dz, xhat = _ln_bwd(z_ref[...], g_ref[...], dy_)
        sg_ = sg_ref[...]
        dgl = dz * up_ref[...] * sg_ * (1.0 - sg_)
        dgl_ref[...] = dgl.astype(CDT)
        dup_ref[...] = (dz * sg_).astype(CDT)
        dx_ref[...] = ALPHA * dz + _dot_nt(dgl, wpg_ref[...])
        _acc(dgain_ref, _colsum(dy_ * xhat), first)
        _acc(dbeta_ref, _colsum(dy_), first)
        _acc(dbg_ref, _colsum(dgl), first)

    row = pl.BlockSpec((tm, D), lambda i: (i, 0))
    vec = pl.BlockSpec((1, D), lambda i: (0, 0))
    return _call(
        after, body, grid=(S // tm,), in_specs=[row, row, row, row, vec, pl.BlockSpec((D, D), lambda i: (0, 0))],
        out_specs=[row, row, row, vec, vec, vec],
        out_shape=[_sds((S, D), F32), _sds((S, D), CDT), _sds((S, D), CDT)] + [_sds((1, D), F32)] * 3,
        compiler_params=_params(("arbitrary",)), name=name)(dy, z, sg, up, gain, wpg)


def _ffn_bwd_hidden(dy, z, gu, wdn, gain, name, after=None):
    S = dy.shape[0]
    tm = _tile(S)

    def hidden(dy_ref, z_ref, gu_ref, wdn_ref, g_ref, dz_ref, dzb_ref, dgu_ref, dgain_ref, dbeta_ref):
        i, j = pl.program_id(0), pl.program_id(1)

        @pl.when(j == 0)
        def _():
            dy_ = dy_ref[...]
            dz, xhat = _ln_bwd(z_ref[...], g_ref[...], dy_)
            dz_ref[...] = dz
            dzb_ref[...] = dz.astype(CDT)
            _acc(dgain_ref, _colsum(dy_ * xhat), i == 0)
            _acc(dbeta_ref, _colsum(dy_), i == 0)

        dhid = _dot_nt(dzb_ref[...], wdn_ref[...])
        gate, up = gu_ref[0], gu_ref[1]
        sg = _sigmoid(gate)
        dgu_ref[0] = (dhid * up * (sg * (1.0 + gate * (1.0 - sg)))).astype(CDT)
        dgu_ref[1] = (dhid * (gate * sg)).astype(CDT)

    row = pl.BlockSpec((tm, D), lambda i, j: (i, 0))
    vec = pl.BlockSpec((1, D), lambda i, j: (0, 0))
    return _call(
        after, hidden, grid=(S // tm, 4),
        in_specs=[row, row, pl.BlockSpec((2, None, tm, FFN_B), lambda i, j: (0, j, i, 0)),
                  pl.BlockSpec((None, FFN_B, D), lambda i, j: (j, 0, 0)), vec],
        out_specs=[row, row, pl.BlockSpec((2, None, tm, FFN_B), lambda i, j: (0, j, i, 0)), vec, vec],
        out_shape=[_sds((S, D), F32), _sds((S, D), CDT), _sds((2, 4, S, FFN_B), CDT), _sds((1, D), F32),
                   _sds((1, D), F32)],
        compiler_params=_params(("arbitrary", "arbitrary")), name=name + "_hidden")(dy, z, gu, wdn, gain)


def _ffn_bwd_input(dz, dgu, wgu, name, after=None):
    S = dz.shape[0]
    tm = _tile(S)

    def to_input(dz_ref, dgu_ref, wgu_ref, dx_ref):
        acc = ALPHA * dz_ref[...]
        for g in range(2):
            for j in range(4):
                acc = acc + _dot(dgu_ref[g, j], wgu_ref[g, j])
        dx_ref[...] = acc

    rows = pl.BlockSpec((tm, D), lambda i: (i, 0))
    return _call(
        after, to_input, grid=(S // tm,),
        in_specs=[rows, pl.BlockSpec((2, 4, tm, FFN_B), lambda i: (0, 0, i, 0)),
                  pl.BlockSpec((2, 4, FFN_B, D), lambda i: (0, 0, 0, 0))],
        out_specs=rows, out_shape=_sds((S, D), F32),
        compiler_params=_params(("arbitrary",)), name=name + "_input")(dz, dgu, wgu)


def _mixout_bwd(dy, z, gain, w3, du_dtype, name, after=None):
    S = dy.shape[0]
    G, Kb, _ = w3.shape
    tm = _tile(S)

    def body(dy_ref, z_ref, g_ref, w_ref, dz_ref, dzb_ref, du_ref, dgain_ref, dbeta_ref, dbias_ref):
        first = pl.program_id(0) == 0
        dy_ = dy_ref[...]
        dz, xhat = _ln_bwd(z_ref[...], g_ref[...], dy_)
        dz_ref[...] = dz
        dzb = dz.astype(CDT)
        dzb_ref[...] = dzb
        for g in range(G):
            du_ref[g] = _dot_nt(dzb, w_ref[g]).astype(du_ref.dtype)
        _acc(dgain_ref, _colsum(dy_ * xhat), first)
        _acc(dbeta_ref, _colsum(dy_), first)
        _acc(dbias_ref, _colsum(dz), first)

    row = pl.BlockSpec((tm, D), lambda i: (i, 0))
    vec = pl.BlockSpec((1, D), lambda i: (0, 0))
    return _call(
        after, body, grid=(S // tm,), in_specs=[row, row, vec, pl.BlockSpec((G, Kb, D), lambda i: (0, 0, 0))],
        out_specs=[row, row, pl.BlockSpec((G, tm, Kb), lambda i: (0, i, 0)), vec, vec, vec],
        out_shape=[_sds((S, D), F32), _sds((S, D), CDT), _sds((G, S, Kb), du_dtype)] + [_sds((1, D), F32)] * 3,
        compiler_params=_params(("arbitrary",)), name=name)(dy, z, gain, w3)


def _half_select(low):
    r = lax.broadcasted_iota(jnp.int32, (2 * ATT_HD, ATT_HD), 0)
    c = lax.broadcasted_iota(jnp.int32, (2 * ATT_HD, ATT_HD), 1)
    return (r == c + (0 if low else ATT_HD)).astype(CDT)


def _half_place(low):
    r = lax.broadcasted_iota(jnp.int32, (ATT_HD, 2 * ATT_HD), 0)
    c = lax.broadcasted_iota(jnp.int32, (ATT_HD, 2 * ATT_HD), 1)
    return (c == r + (0 if low else ATT_HD)).astype(CDT)


def _pair_lanes(even, odd):
    return (jnp.dot(even, _half_place(True), preferred_element_type=F32)
            + jnp.dot(odd, _half_place(False), preferred_element_type=F32)).astype(CDT)


def _proj_heads(a, w, bias, heads, name):
    S, K = a.shape
    N = heads * ATT_HD
    tm = _tile(S)

    def body(a_ref, w_ref, b_ref, o_ref):
        acc = (_dot(a_ref[...], w_ref[...]) + b_ref[...]).astype(CDT)
        sel = (_half_select(True), _half_select(False))
        for h in range(heads):
            pair = acc[:, (h // 2) * 2 * ATT_HD:(h // 2 + 1) * 2 * ATT_HD]
            o_ref[h] = jnp.dot(pair, sel[h % 2], preferred_element_type=F32).astype(CDT)

    return _call(
        None, body, grid=(S // tm,),
        in_specs=[pl.BlockSpec((tm, K), lambda i: (i, 0)), pl.BlockSpec((K, N), lambda i: (0, 0)),
                  pl.BlockSpec((1, N), lambda i: (0, 0))],
        out_specs=pl.BlockSpec((heads, tm, ATT_HD), lambda i: (0, i, 0)), out_shape=_sds((heads, S, ATT_HD), CDT),
        compiler_params=_params(("arbitrary",)), name=name)(a, w, bias)


def _qkv_bwd(dz, dq, dkv4, wq, wkv, name, after=None):
    S = dz.shape[0]
    tm = _tile(S)
    HK = dkv4.shape[0]
    NK = HK * ATT_HD

    def body(dz_ref, dq_ref, dkv_ref, wq_ref, wkv_ref, dx_ref, dkvn_ref, dkvb_ref):
        first = pl.program_id(0) == 0
        dkvn = jnp.concatenate([_pair_lanes(dkv_ref[2 * i].astype(CDT), dkv_ref[2 * i + 1].astype(CDT))
                                for i in range(HK // 2)], axis=-1)
        dkvn_ref[...] = dkvn
        dx_ref[...] = ALPHA * dz_ref[...] + _dot_nt(dq_ref[...], wq_ref[...]) + _dot_nt(dkvn, wkv_ref[...])
        for h in range(HK):
            _acc(dkvb_ref.at[h], _colsum(dkv_ref[h]), first)

    row = pl.BlockSpec((tm, D), lambda i: (i, 0))
    return _call(
        after, body, grid=(S // tm,),
        in_specs=[row, row, pl.BlockSpec((HK, tm, ATT_HD), lambda i: (0, i, 0)),
                  pl.BlockSpec((D, D), lambda i: (0, 0)), pl.BlockSpec((D, NK), lambda i: (0, 0))],
        out_specs=[row, pl.BlockSpec((tm, NK), lambda i: (i, 0)), pl.BlockSpec((HK, 1, ATT_HD), lambda i: (0, 0, 0))],
        out_shape=[_sds((S, D), F32), _sds((S, NK), CDT), _sds((HK, 1, ATT_HD), F32)],
        compiler_params=_params(("arbitrary",)), name=name)(dz, dq, dkv4, wq, wkv)


def _inproj_bwd(dz, dproj, wain, name, after=None):
    S = dz.shape[0]
    tm = _tile(S)
    nb = wain.shape[-1]

    def body(dz_ref, dp_ref, w_ref, dx_ref):
        acc = ALPHA * dz_ref[...]
        for j in range(N_DEV):
            acc = acc + _dot_nt(dp_ref[j // 2, :, pl.ds((j % 2) * nb, nb)], w_ref[j])
        dx_ref[...] = acc

    row = pl.BlockSpec((tm, D), lambda i: (i, 0))
    return _call(
        after, body, grid=(S // tm,),
        in_specs=[row, pl.BlockSpec((4, tm, D), lambda i: (0, i, 0)), pl.BlockSpec((N_DEV, D, nb), lambda i: (0, 0, 0))],
        out_specs=row, out_shape=_sds((S, D), F32),
        compiler_params=_params(("arbitrary",)), name=name)(dz, dproj, wain)


def _running_sum(x, reverse=False):
    rows = x.shape[0]
    row = lax.broadcasted_iota(jnp.int32, x.shape, 0)
    step = 1
    while step < rows:
        if reverse:
            x = x + jnp.where(row < rows - step, pltpu.roll(x, rows - step, 0), 0.0)
        else:
            x = x + jnp.where(row >= step, pltpu.roll(x, step, 0), 0.0)
        step *= 2
    return x


def _hg_gates(q, f, alb_ref):
    a0, a1 = alb_ref[0:1, :], alb_ref[1:2, :]
    mx = jnp.maximum(a0, a1)
    e0, e1 = jnp.exp(a0 - mx), jnp.exp(a1 - mx)
    lb = e0 / (e0 + e1)
    sig = _sigmoid(f)
    forget = lb + (1.0 - lb) * sig
    k = (1.0 - lb) * _sigmoid(-f)
    qs = q * _sigmoid(q) * (HG_DK ** -0.5)
    return qs, k, jnp.log(forget), sig, lb, forget


def _hg_intra(qs, k, b, b_scr):
    b_scr[...] = b
    bm = b_scr[pl.ds(HG_CH // 2 - 1, 1), :]
    bl = b_scr[pl.ds(HG_CH - 1, 1), :]
    eb = jnp.exp(b)
    qb = qs * eb
    e_q = jnp.exp(b - bm)
    e_k = jnp.exp(bm - b)
    e_d = jnp.exp(bl - b)
    return qb, qs * e_q, k * e_k, k * e_d, jnp.exp(bl), eb, e_q, e_k, e_d


def _hgrn_fwd(proj, alb, ngain):
    S = proj.shape[1]
    nc = S // HG_CH
    nb = nc // HG_CPB
    rb, wb = HG_CPB * HG_CH, HG_HPB * HG_DK

    def body(pj_ref, alb_ref, ng_ref, o_ref, y_ref, st_ref, st_scr, b_scr):
        n = pl.program_id(1)

        @pl.when(n == 0)
        def _():
            st_scr[...] = jnp.zeros_like(st_scr)

        r = lax.broadcasted_iota(jnp.int32, (HG_CH, HG_CH), 0)
        c = lax.broadcasted_iota(jnp.int32, (HG_CH, HG_CH), 1)
        causal = r >= c
        for ci, j in [(ci, j) for ci in range(HG_CPB) for j in range(HG_HPB)]:
            rows, lanes = pl.ds(ci * HG_CH, HG_CH), pl.ds(j * HG_DK, HG_DK)
            q, f, v, g = pj_ref[0, rows, lanes], pj_ref[1, rows, lanes], pj_ref[2, rows, lanes], pj_ref[3, rows, lanes]
            qs, k, logf, _, _, _ = _hg_gates(q, f, alb_ref.at[:, lanes])
            b = _running_sum(logf)
            qb, qt, kt, kd, ebl, _, _, _, _ = _hg_intra(qs, k, b, b_scr.at[j, ci])
            st = st_scr[j]
            st_ref[j, ci] = st
            a = jnp.where(causal, _dot_nt(qt, kt), 0.0)
            o = _dot(a, v) + _dot_nt(qb, st)
            st_scr[j] = st * ebl + _dot_tn(v, kd)
            o_ref[rows, lanes] = o
            rinv = lax.rsqrt(jnp.mean(o * o, axis=-1, keepdims=True) + RMS_EPS)
            y_ref[rows, lanes] = (o * rinv * ng_ref[...] * (g * _sigmoid(g))).astype(CDT)

    blk = pl.BlockSpec((rb, wb), lambda h, n: (n, h))
    return _call(
        None, body, grid=(HG_H // HG_HPB, nb),
        in_specs=[pl.BlockSpec((4, rb, wb), lambda h, n: (0, n, h)), pl.BlockSpec((2, wb), lambda h, n: (0, h)),
                  pl.BlockSpec((1, HG_DK), lambda h, n: (0, 0))],
        out_specs=[blk, blk, pl.BlockSpec((HG_HPB, HG_CPB, HG_DK, HG_DK), lambda h, n: (h, n, 0, 0))],
        out_shape=[_sds((S, D), F32), _sds((S, D), CDT), _sds((HG_H, nc, HG_DK, HG_DK), F32)],
        scratch_shapes=[pltpu.VMEM((HG_HPB, HG_DK, HG_DK), F32), pltpu.VMEM((HG_HPB, HG_CPB, HG_CH, HG_DK), F32)],
        compiler_params=_params(("arbitrary", "arbitrary")), name="hgrn_fwd")(proj, alb, ngain)


def _hgrn_bwd(proj, alb, ngain, o, states, dy, after=None):
    S = proj.shape[1]
    nc = S // HG_CH
    nb = nc // HG_CPB
    rb, wb = HG_CPB * HG_CH, HG_HPB * HG_DK

    def body(pj_ref, alb_ref, ng_ref, o_ref, st_ref, dy_ref, dpj_ref, dalb_ref, dng_ref, dst_scr, b_scr):
        h, n = pl.program_id(0), pl.program_id(1)

        @pl.when(n == 0)
        def _():
            dst_scr[...] = jnp.zeros_like(dst_scr)
            dalb_ref[...] = jnp.zeros_like(dalb_ref)

        _zero_at(dng_ref, jnp.logical_and(h == 0, n == 0))
        ng = ng_ref[...]
        r = lax.broadcasted_iota(jnp.int32, (HG_CH, HG_CH), 0)
        c = lax.broadcasted_iota(jnp.int32, (HG_CH, HG_CH), 1)
        causal = r >= c
        dng = None
        for ci, j in [(ci, j) for ci in reversed(range(HG_CPB)) for j in range(HG_HPB)]:
            rows, lanes = pl.ds(ci * HG_CH, HG_CH), pl.ds(j * HG_DK, HG_DK)
            q, f, v, g = pj_ref[0, rows, lanes], pj_ref[1, rows, lanes], pj_ref[2, rows, lanes], pj_ref[3, rows, lanes]
            o_ = o_ref[rows, lanes]
            dy_ = dy_ref[rows, lanes]
            sg = _sigmoid(g)
            rinv = lax.rsqrt(jnp.mean(o_ * o_, axis=-1, keepdims=True) + RMS_EPS)
            nrm = o_ * rinv
            dr = dy_ * (g * sg)
            dg = dy_ * nrm * ng * (sg * (1.0 + g * (1.0 - sg)))
            dn = dr * ng
            do = rinv * (dn - nrm * jnp.mean(dn * nrm, axis=-1, keepdims=True))
            dng = _colsum(dr * nrm) if dng is None else dng + _colsum(dr * nrm)
            qs, k, logf, sig, lb, forget = _hg_gates(q, f, alb_ref.at[:, lanes])
            b = _running_sum(logf)
            qb, qt, kt, kd, ebl, eb, e_q, e_k, e_d = _hg_intra(qs, k, b, b_scr.at[j, ci])
            st = st_ref[j, ci]
            dstn = dst_scr[j]
            qt, kt, qb, kd = (t.astype(CDT).astype(F32) for t in (qt, kt, qb, kd))
            a = jnp.where(causal, _dot_nt(qt, kt), 0.0)
            da = jnp.where(causal, _dot_nt(do, v), 0.0)
            dv = _dot_tn(a, do) + _dot_nt(kd, dstn)
            dqb = _dot(do, st)
            dkd = _dot(v, dstn)
            dqt = _dot(da, kt)
            dkt = _dot_tn(da, qt)
            dbl = _colsum(dkd * kd) + ebl * _colsum(dstn * st)
            dst_scr[j] = dstn * ebl + _dot_tn(do, qb)
            dqs = dqt * e_q + dqb * eb
            dk = dkt * e_k + dkd * e_d
            db = dqt * qt + dqb * qb - dkt * kt - dkd * kd
            dlogf = _running_sum(db, reverse=True) + dbl
            dforget = dlogf / forget
            dsig = (1.0 - lb) * (dforget - dk)
            df = dsig * sig * (1.0 - sig)
            dlb = _colsum((dforget - dk) * (1.0 - sig))
            sq = _sigmoid(q)
            dq = dqs * (HG_DK ** -0.5) * (sq * (1.0 + q * (1.0 - sq)))
            dpj_ref[0, rows, lanes] = dq.astype(CDT)
            dpj_ref[1, rows, lanes] = df.astype(CDT)
            dpj_ref[2, rows, lanes] = dv.astype(CDT)
            dpj_ref[3, rows, lanes] = dg.astype(CDT)
            da0 = dlb * lb * (1.0 - lb)
            dalb_ref[pl.ds(0, 1), lanes] += da0
            dalb_ref[pl.ds(1, 1), lanes] -= da0
        dng_ref[...] += dng

    blk = pl.BlockSpec((rb, wb), lambda h, n: (nb - 1 - n, h))
    pj = pl.BlockSpec((4, rb, wb), lambda h, n: (0, nb - 1 - n, h))
    alb_blk = pl.BlockSpec((2, wb), lambda h, n: (0, h))
    ng_blk = pl.BlockSpec((1, HG_DK), lambda h, n: (0, 0))
    return _call(
        after, body, grid=(HG_H // HG_HPB, nb),
        in_specs=[pj, alb_blk, ng_blk, blk,
                  pl.BlockSpec((HG_HPB, HG_CPB, HG_DK, HG_DK), lambda h, n: (h, nb - 1 - n, 0, 0)), blk],
        out_specs=[pj, alb_blk, ng_blk],
        out_shape=[_sds((4, S, D), CDT), _sds((2, D), F32), _sds((1, HG_DK), F32)],
        scratch_shapes=[pltpu.VMEM((HG_HPB, HG_DK, HG_DK), F32), pltpu.VMEM((HG_HPB, HG_CPB, HG_CH, HG_DK), F32)],
        compiler_params=_params(("arbitrary", "arbitrary")), name="hgrn_bwd")(proj, alb, ngain, o, states, dy)


def _slope(h):
    return 2.0 ** (-8.0 * (h + 1) / ATT_QH)


def _attn_mask(n):
    qi = lax.broadcasted_iota(jnp.int32, (WINDOW, 2 * WINDOW), 0)
    si = lax.broadcasted_iota(jnp.int32, (WINDOW, 2 * WINDOW), 1)
    dist = qi - si + WINDOW
    valid = (dist >= 0) & (dist < WINDOW) & (n * WINDOW - WINDOW + si >= 0)
    return valid, dist.astype(F32)


def _attn_probs(qh, kh, sink, slope, valid, distf):
    s = _dot_nt(qh, kh) * (ATT_HD ** -0.5) - slope * distf
    s = jnp.where(valid, s, NEG)
    m = jnp.maximum(jnp.max(s, axis=-1, keepdims=True), sink)
    e = jnp.exp(s - m)
    es = jnp.exp(sink - m)
    inv = 1.0 / (jnp.sum(e, axis=-1, keepdims=True) + es)
    return e * inv, es * inv


def _attn_specs(S):
    nb = S // WINDOW
    cur = lambda H: pl.BlockSpec((H, WINDOW, ATT_HD), lambda n: (0, n, 0))
    prev = lambda H: pl.BlockSpec((H, WINDOW, ATT_HD), lambda n: (0, jnp.maximum(n - 1, 0), 0))
    return nb, cur, prev


def _attn_fwd(q4, kv4, sinks):
    S = q4.shape[1]
    nb, cur, prev = _attn_specs(S)

    def body(sink_ref, q_ref, kvc_ref, kvp_ref, o_ref):
        valid, distf = _attn_mask(pl.program_id(0))
        outs = []
        for h in range(ATT_QH):
            kvh = h // ATT_G
            kh = jnp.concatenate([kvp_ref[kvh], kvc_ref[kvh]], axis=0)
            vh = jnp.concatenate([kvp_ref[ATT_KVH + kvh], kvc_ref[ATT_KVH + kvh]], axis=0)
            p, _ = _attn_probs(q_ref[h], kh, sink_ref[0, h], _slope(h), valid, distf)
            outs.append(_dot(p, vh).astype(CDT))
            if h % 2:
                o_ref[:, pl.ds((h - 1) * ATT_HD, 2 * ATT_HD)] = _pair_lanes(outs[h - 1], outs[h])

    return _call(
        None, body, grid=(nb,),
        in_specs=[pl.BlockSpec(memory_space=pltpu.SMEM), cur(ATT_QH), cur(2 * ATT_KVH), prev(2 * ATT_KVH)],
        out_specs=pl.BlockSpec((WINDOW, D), lambda n: (n, 0)), out_shape=_sds((S, D), CDT),
        compiler_params=_params(("arbitrary",)), name="attn_fwd")(sinks, q4, kv4, kv4)


def _attn_bwd(q4, kv4, sinks, do):
    S = q4.shape[1]
    nb, cur, prev = _attn_specs(S)

    def body(sink_ref, q_ref, kvc_ref, kvp_ref, do_ref, dq_ref, dkv_ref, dbq_ref, dsink_ref):
        n = pl.program_id(0)
        first = n == 0

        @pl.when(first)
        def _():
            dkv_ref[...] = jnp.zeros_like(dkv_ref)
            dsink_ref[...] = jnp.zeros_like(dsink_ref)
            dbq_ref[...] = jnp.zeros_like(dbq_ref)

        valid, distf = _attn_mask(n)
        lane = lax.broadcasted_iota(jnp.int32, (1, 128), 1)
        rows_cur = pl.ds(pl.multiple_of(n * WINDOW, WINDOW), WINDOW)
        rows_prev = pl.ds(pl.multiple_of(jnp.maximum(n - 1, 0) * WINDOW, WINDOW), WINDOW)
        dsinks = jnp.zeros((1, 128), F32)
        sel = (_half_select(True), _half_select(False))
        for kvh in range(ATT_KVH):
            kh = jnp.concatenate([kvp_ref[kvh], kvc_ref[kvh]], axis=0)
            vh = jnp.concatenate([kvp_ref[ATT_KVH + kvh], kvc_ref[ATT_KVH + kvh]], axis=0)
            dk = dv = None
            dqs = []
            for h in range(kvh * ATT_G, (kvh + 1) * ATT_G):
                qh = q_ref[h]
                doh = jnp.dot(do_ref[:, pl.ds((h // 2) * 2 * ATT_HD, 2 * ATT_HD)], sel[h % 2],
                              preferred_element_type=F32).astype(CDT)
                p, ps = _attn_probs(qh, kh, sink_ref[0, h], _slope(h), valid, distf)
                dp = _dot_nt(doh, vh)
                dd = jnp.sum(p * dp, axis=-1, keepdims=True)
                ds = p * (dp - dd)
                dsinks = dsinks + jnp.where(lane == h, -jnp.sum(ps * dd, axis=0, keepdims=True), 0.0)
                dqh = _dot(ds, kh) * (ATT_HD ** -0.5)
                dqs.append(dqh.astype(CDT))
                dbq_ref[h] += _colsum(dqh)
                dkh = _dot_tn(ds, qh) * (ATT_HD ** -0.5)
                dvh = _dot_tn(p, doh)
                dk = dkh if dk is None else dk + dkh
                dv = dvh if dv is None else dv + dvh
            for i in range(ATT_G // 2):
                lanes = pl.ds((kvh * ATT_G + 2 * i) * ATT_HD, 2 * ATT_HD)
                dq_ref[:, lanes] = _pair_lanes(dqs[2 * i], dqs[2 * i + 1])
            dkv_ref[kvh, rows_prev, :] += dk[:WINDOW]
            dkv_ref[kvh, rows_cur, :] += dk[WINDOW:]
            dkv_ref[ATT_KVH + kvh, rows_prev, :] += dv[:WINDOW]
            dkv_ref[ATT_KVH + kvh, rows_cur, :] += dv[WINDOW:]
        dsink_ref[...] += dsinks

    return _call(
        None, body, grid=(nb,),
        in_specs=[pl.BlockSpec(memory_space=pltpu.SMEM), cur(ATT_QH), cur(2 * ATT_KVH), prev(2 * ATT_KVH),
                  pl.BlockSpec((WINDOW, D), lambda n: (n, 0))],
        out_specs=[pl.BlockSpec((WINDOW, D), lambda n: (n, 0)), pl.BlockSpec((2 * ATT_KVH, S, ATT_HD), lambda n: (0, 0, 0)),
                   pl.BlockSpec((ATT_QH, 1, ATT_HD), lambda n: (0, 0, 0)), pl.BlockSpec((1, 128), lambda n: (0, 0))],
        out_shape=[_sds((S, D), CDT), _sds((2 * ATT_KVH, S, ATT_HD), F32), _sds((ATT_QH, 1, ATT_HD), F32),
                   _sds((1, 128), F32)],
        compiler_params=_params(("arbitrary",)), name="attn_bwd")(sinks, q4, kv4, kv4, do)


def _local_step(x, p, target, getw, sm, emit):
    S = x.shape[0]
    vec = lambda a: a.reshape(1, -1)
    ln_g = lambda l, k: vec(sm["ln_gain"][l, k])
    ln_b = lambda l, k: vec(sm["ln_bias"][l, k])
    xb = x.astype(CDT)
    pb = p.astype(CDT)

    proj = _mm_nn(xb, getw("a_w_in", None), (4, S, D), (None, _tile(S), 512), lambda g, i: (g // 2, i, g % 2), F32,
                  name="a_in")
    o_a, y_a, states = _hgrn_fwd(proj, sm["a_lower_bound"], sm["a_norm_gain"])
    zeros = jnp.zeros((1, D), F32)
    z = [[None] * 3 for _ in range(2)]
    xs = [[None] * 3 for _ in range(2)]
    xbs = [[None] * 3 for _ in range(2)]
    z[0][0], xs[0][0], xbs[0][0] = _mixout_ln(y_a[None], getw("a_w_out", y_a)[None], zeros, x, ln_g(0, 0), ln_b(0, 0),
                                              "a_out_ln")
    gu, hid, sgs, ups = [None, None], [None, None], [None, None], [None, None]

    def ffn_ple(l):
        wgu = getw(f"gu{l}", xbs[l][0])
        gu[l], hid[l], z[l][1], xs[l][1], xbs[l][1] = _ffn_fwd(
            xs[l][0], xbs[l][0], wgu, getw(f"dn{l}", None), ln_g(l, 1), ln_b(l, 1), f"ffn_fwd{l}")
        sgs[l], ups[l], z[l][2], xs[l][2], xbs[l][2] = _ple_fwd(
            xs[l][1], xbs[l][1], pb[l], getw(f"pg{l}", None), vec(sm["ple_b_gate"][l]), getw(f"pu{l}", None), ln_g(l, 2),
            ln_b(l, 2), f"ple_fwd{l}")

    ffn_ple(0)
    x3, x3b = xs[0][2], xbs[0][2]
    w_kv, w_q, w_bo = getw("kv_w", x3b), getw("b_w_q", None), getw("b_w_out", None)
    kv4 = _proj_heads(x3b, w_kv, vec(sm["kv_b"]), 2 * ATT_KVH, "kv_proj")
    q4 = _proj_heads(x3b, w_q, vec(sm["b_b_q"]), ATT_QH, "q_proj")
    o_b = _attn_fwd(q4, kv4, sm["b_sinks"])
    z[1][0], xs[1][0], xbs[1][0] = _mixout_ln(o_b[None], w_bo[None], sm["b_b_out"], x3, ln_g(1, 0), ln_b(1, 0),
                                              "b_out_ln")
    ffn_ple(1)
    loss, dy = _loss_fwd_bwd(xs[1][2], target)

    gs = {}
    d_ln_g = [[None] * 3 for _ in range(2)]
    d_ln_b = [[None] * 3 for _ in range(2)]
    g_bg = [None, None]

    def ffn_ple_bwd(l, dy, after=None):
        dx2, dgl, dup, d_ln_g[l][2], d_ln_b[l][2], g_bg[l] = _ple_bwd(dy, z[l][2], sgs[l], ups[l], ln_g(l, 2),
                                                                     getw(f"pg{l}", None), f"ple_bwd{l}", after=after)
        g_pg = _wgrad(xbs[l][1][None], dgl[None], f"g_ple_gate{l}")[0]
        g_pu = _wgrad(pb[l][None], dup[None], f"g_ple_up{l}")[0]
        dz2, dzb, dgu, d_ln_g[l][1], d_ln_b[l][1] = _ffn_bwd_hidden(dx2, z[l][1], gu[l], getw(f"dn{l}", None), ln_g(l, 1),
                                                                   f"ffn_bwd{l}")
        g_dn = _wgrad(hid[l], dzb[None], f"g_ffn_down{l}")
        g_gu = _wgrad(dgu.reshape(8, S, FFN_B), xbs[l][0][None], f"g_ffn_gate_up{l}")
        tok = emit({f"gu{l}": g_gu, f"dn{l}": g_dn, f"pg{l}": g_pg, f"pu{l}": g_pu})
        dx1 = _ffn_bwd_input(dz2, dgu, getw(f"gu{l}", None), f"ffn_bwd{l}", after=tok)
        return dx1, None

    dx1, tok = ffn_ple_bwd(1, dy)
    dz, dzb, do, d_ln_g[1][0], d_ln_b[1][0], gs["b_b_out"] = _mixout_bwd(dx1, z[1][0], ln_g(1, 0), w_bo[None], CDT,
                                                                        "b_out_bwd", after=tok)
    g_bo = _wgrad(o_b[None], dzb[None], "g_b_w_out")[0]
    dq, dkv4, dbq, dsinks = _attn_bwd(q4, kv4, sm["b_sinks"], do[0])
    gs["b_b_q"] = dbq
    gs["b_sinks"] = dsinks
    g_q = _wgrad(x3b[None], dq[None], "g_b_w_q")[0]
    dx3, dkv, gs["kv_b"] = _qkv_bwd(dz, dq, dkv4, w_q, w_kv, "qkv_bwd", after=tok)
    g_kv = _wgrad(x3b[None], dkv[None], "g_kv_w")[0]
    tok = emit({"b_w_out": g_bo, "b_w_q": g_q, "kv_w": g_kv})
    dx1, tok = ffn_ple_bwd(0, dx3, tok)
    w_ao = getw("a_w_out", None)
    dz, dzb, dyr, d_ln_g[0][0], d_ln_b[0][0], _ = _mixout_bwd(dx1, z[0][0], ln_g(0, 0), w_ao[None], F32, "a_out_bwd",
                                                              after=tok)
    g_ao = _wgrad(y_a[None], dzb[None], "g_a_w_out")[0]
    tok = emit({"a_w_out": g_ao})
    dproj, gs["a_lower_bound"], gs["a_norm_gain"] = _hgrn_bwd(proj, sm["a_lower_bound"], sm["a_norm_gain"], o_a, states,
                                                              dyr[0], after=tok)
    tk = lambda t: (None, t, D)
    g_ain = _mm_tn(xb[None], dproj, N_DEV, lambda g, k: (0, k, 0), lambda g, k: (g // 2, k, g % 2),
                   tk, lambda t: (None, t, 512), (N_DEV, D, 512), (None, D, 512), lambda g, k: (g, 0, 0), name="g_a_w_in")
    gs["ple_b_gate"] = jnp.concatenate(g_bg, axis=0)
    gs["ln_gain"] = jnp.stack([jnp.concatenate(r, axis=0) for r in d_ln_g])
    gs["ln_bias"] = jnp.stack([jnp.concatenate(r, axis=0) for r in d_ln_b])
    gs["loss"] = loss
    tok = emit({"a_w_in": g_ain}, small=gs)
    grad_x = _inproj_bwd(dz, dproj, getw("a_w_in", None), "a_in_bwd", after=tok)
    return loss, grad_x, gs


def _peer(k):
    x, y, c = lax.axis_index("x"), lax.axis_index("y"), lax.axis_index("c")
    px = 1 - x if k & 4 else x
    py = 1 - y if k & 2 else y
    pc = 1 - c if k & 1 else c
    return (px, py, pc), 4 * px + 2 * py + pc


def _my_index():
    return 4 * lax.axis_index("x") + 2 * lax.axis_index("y") + lax.axis_index("c")


def _exchange(srcs, dst_shapes, plan, name):
    n_src, n_piece = len(srcs), len(plan)

    def body(*refs):
        src_refs, dst_refs = refs[:n_src], refs[n_src:n_src + len(dst_shapes)]
        send_sems, recv_sems, local_sems = refs[n_src + len(dst_shapes):]
        me = _my_index()

        def at(ref, idx):
            return ref.at[idx] if idx else ref

        local = []
        for t, (si, sfn, di, dfn) in enumerate(plan):
            cp = pltpu.make_async_copy(at(src_refs[si], sfn(me)), at(dst_refs[di], dfn(me)), local_sems.at[t])
            cp.start()
            local.append(cp)
        sends = []
        for k in range(1, N_DEV):
            peer, pid = _peer(k)
            for t, (si, sfn, di, dfn) in enumerate(plan):
                cp = pltpu.make_async_remote_copy(
                    src_ref=at(src_refs[si], sfn(pid)), dst_ref=at(dst_refs[di], dfn(me)),
                    send_sem=send_sems.at[t * 7 + k - 1], recv_sem=recv_sems.at[t * 7 + k - 1],
                    device_id=peer, device_id_type=MESH)
                cp.start()
                sends.append(cp)
        for k in range(1, N_DEV):
            peer, pid = _peer(k)
            for t, (si, sfn, di, dfn) in enumerate(plan):
                pltpu.make_async_remote_copy(
                    src_ref=at(src_refs[si], sfn(me)), dst_ref=at(dst_refs[di], dfn(pid)),
                    send_sem=send_sems.at[t * 7 + k - 1], recv_sem=recv_sems.at[t * 7 + k - 1],
                    device_id=peer, device_id_type=MESH).wait_recv()
        for cp in sends:
            cp.wait_send()
        for cp in local:
            cp.wait()

    hbm = pl.BlockSpec(memory_space=pltpu.HBM)
    return _call(
        None, body, in_specs=[hbm] * n_src, out_specs=[hbm] * len(dst_shapes), out_shape=dst_shapes,
        scratch_shapes=[pltpu.SemaphoreType.DMA((7 * n_piece,)), pltpu.SemaphoreType.DMA((7 * n_piece,)),
                        pltpu.SemaphoreType.DMA((n_piece,))],
        name=name)(*srcs)


def _gather(shards, name):
    dsts = [_sds((N_DEV,) + a.shape, a.dtype) for a in shards]
    plan = [(i, lambda j: (), i, lambda s: (s,)) for i in range(len(shards))]
    return _exchange(shards, dsts, plan, name)


_HBM = pl.BlockSpec(memory_space=pltpu.HBM)
_SEM = pl.BlockSpec(memory_space=pltpu.SEMAPHORE)
_DATAFLOW = pltpu.SideEffectType.DATAFLOW_SIDE_EFFECTING


def _piece_copy(mode, src, land, send_sems, recv_sems, t, k, sender, receiver, peer):
    return pltpu.make_async_remote_copy(
        src_ref=src if mode == "gather" else src.at[receiver], dst_ref=land.at[sender],
        send_sem=send_sems.at[t * 7 + k - 1], recv_sem=recv_sems.at[t * 7 + k - 1], device_id=peer, device_id_type=MESH)


def _sequencer_exchange(srcs, modes, name, collective_id, after=None):
    n = len(srcs)
    land_shapes = [((N_DEV,) + a.shape) if mode == "gather" else a.shape for a, mode in zip(srcs, modes)]
    extra = [] if after is None else [after]

    def body(*refs):
        src_refs, land_refs = refs[:n], refs[n + len(extra):2 * n + len(extra)]
        send_sems, recv_sems, local_sems = refs[2 * n + len(extra):]
        barrier = pltpu.get_barrier_semaphore()
        for k in range(1, N_DEV):
            pl.semaphore_signal(barrier, inc=1, device_id=_peer(k)[0], device_id_type=MESH)
        pl.semaphore_wait(barrier, N_DEV - 1)
        me = _my_index()
        local = []
        for i in range(n):
            cp = pltpu.make_async_copy(src_refs[i] if modes[i] == "gather" else src_refs[i].at[me], land_refs[i].at[me],
                                       local_sems.at[i])
            cp.start()
            local.append(cp)
        for k in range(1, N_DEV):
            peer, pid = _peer(k)
            for t in range(n):
                _piece_copy(modes[t], src_refs[t], land_refs[t], send_sems, recv_sems, t, k, me, pid, peer).start()
        for k in range(1, N_DEV):
            peer, pid = _peer(k)
            for t in range(n):
                _piece_copy(modes[t], src_refs[t], land_refs[t], send_sems, recv_sems, t, k, pid, me, peer).wait_recv()
        for k in range(1, N_DEV):
            peer, pid = _peer(k)
            for t in range(n):
                _piece_copy(modes[t], src_refs[t], land_refs[t], send_sems, recv_sems, t, k, me, pid, peer).wait_send()
        for cp in local:
            cp.wait()

    return pl.kernel(
        body, out_type=[_sds(s, a.dtype) for s, a in zip(land_shapes, srcs)],
        mesh=plsc.ScalarSubcoreMesh(axis_name="sequencer", num_cores=1),
        scratch_types=[pltpu.SemaphoreType.DMA((7 * n,)), pltpu.SemaphoreType.DMA((7 * n,)), pltpu.SemaphoreType.DMA((n,))],
        compiler_params=pltpu.CompilerParams(collective_id=collective_id), name=name)(*srcs, *extra)


def _sequencer_sibling_swap(blocks, name, collective_id):
    n = len(blocks)

    def body(*refs):
        srcs, theirs = refs[:n], refs[n:2 * n]
        send_sems, recv_sems = refs[2 * n:]
        x, y, c = lax.axis_index("x"), lax.axis_index("y"), lax.axis_index("c")
        sibling = (x, y, 1 - c)
        barrier = pltpu.get_barrier_semaphore()
        pl.semaphore_signal(barrier, inc=1, device_id=sibling, device_id_type=MESH)
        pl.semaphore_wait(barrier, 1)
        copies = []
        for t in range(n):
            for q in range(4):
                copies.append(pltpu.make_async_remote_copy(
                    src_ref=srcs[t].at[2 * q + (1 - c)], dst_ref=theirs[t].at[q], send_sem=send_sems.at[4 * t + q],
                    recv_sem=recv_sems.at[4 * t + q], device_id=sibling, device_id_type=MESH))
        for cp in copies:
            cp.start()
        for cp in copies:
            cp.wait_recv()
        for cp in copies:
            cp.wait_send()

    return pl.kernel(
        body, out_type=[_sds((4,) + a.shape[1:], a.dtype) for a in blocks],
        mesh=plsc.ScalarSubcoreMesh(axis_name="sequencer", num_cores=1),
        scratch_types=[pltpu.SemaphoreType.DMA((4 * n,)), pltpu.SemaphoreType.DMA((4 * n,))],
        compiler_params=pltpu.CompilerParams(collective_id=collective_id), name=name)(*blocks)


def _pair_sums(blocks, theirs, name):
    n = len(blocks)

    def body(*refs):
        c = lax.axis_index("c")
        for t in range(n):
            mine = jnp.where(c == 0, refs[3 * t][...].astype(F32), refs[3 * t + 1][...].astype(F32))
            refs[3 * n + t][...] = (mine + refs[3 * t + 2][...].astype(F32)).astype(CDT)

    in_specs, out_specs, steps = [], [], 1
    for a in blocks:
        _, R, C = a.shape
        tr = _tile(R, (256, 128, 176, 64, 32, 16))
        row = lambda q, i, nr=R // tr: jnp.minimum(i, nr - 1)
        in_specs += [pl.BlockSpec((None, None, tr, C), lambda q, i, row=row: (q, 0, row(q, i), 0)),
                     pl.BlockSpec((None, None, tr, C), lambda q, i, row=row: (q, 1, row(q, i), 0)),
                     pl.BlockSpec((None, tr, C), lambda q, i, row=row: (q, row(q, i), 0))]
        out_specs.append(pl.BlockSpec((None, tr, C), lambda q, i, row=row: (q, row(q, i), 0)))
        steps = max(steps, R // tr)
    args = []
    for a, th in zip(blocks, theirs):
        by_core = a.reshape((4, 2) + a.shape[1:])
        args += [by_core, by_core, th]
    return _call(
        None, body, grid=(4, steps), in_specs=in_specs, out_specs=out_specs,
        out_shape=[_sds(th.shape, CDT) for th in theirs],
        compiler_params=_params(("arbitrary", "arbitrary")), name=name)(*args)


def _sequencer_chip_scatter(pairs, name, collective_id):
    n = len(pairs)

    def body(*refs):
        srcs, lands = refs[:n], refs[n:2 * n]
        send_sems, recv_sems = refs[2 * n:]
        x, y, c = lax.axis_index("x"), lax.axis_index("y"), lax.axis_index("c")
        chips = [(1 - x, y), (x, 1 - y), (1 - x, 1 - y)]
        chip = lambda px, py: 2 * px + py
        barrier = pltpu.get_barrier_semaphore()
        for px, py in chips:
            pl.semaphore_signal(barrier, inc=1, device_id=(px, py, c), device_id_type=MESH)
        pl.semaphore_wait(barrier, 3)
        here = chip(x, y)
        sends = []
        for t in range(n):
            for j, (px, py) in enumerate(chips):
                sends.append(pltpu.make_async_remote_copy(
                    src_ref=srcs[t].at[chip(px, py)], dst_ref=lands[t].at[here], send_sem=send_sems.at[3 * t + j],
                    recv_sem=recv_sems.at[3 * t + j], device_id=(px, py, c), device_id_type=MESH))
        for cp in sends:
            cp.start()
        for t in range(n):
            for j, (px, py) in enumerate(chips):
                pltpu.make_async_remote_copy(
                    src_ref=srcs[t].at[here], dst_ref=lands[t].at[chip(px, py)], send_sem=send_sems.at[3 * t + j],
                    recv_sem=recv_sems.at[3 * t + j], device_id=(px, py, c), device_id_type=MESH).wait_recv()
        for cp in sends:
            cp.wait_send()

    return pl.kernel(
        body, out_type=[_sds(a.shape, a.dtype) for a in pairs],
        mesh=plsc.ScalarSubcoreMesh(axis_name="sequencer", num_cores=1),
        scratch_types=[pltpu.SemaphoreType.DMA((3 * n,)), pltpu.SemaphoreType.DMA((3 * n,))],
        compiler_params=pltpu.CompilerParams(collective_id=collective_id), name=name)(*pairs)


def _sequencer_gather(srcs, name, collective_id, after=None):
    n = len(srcs)
    extra = [] if after is None else [after]

    def body(*refs):
        src_refs, land_refs = refs[:n], refs[n + len(extra):2 * n + len(extra)]
        send_sems, recv_sems, local_sems = refs[2 * n + len(extra):]
        x, y, c = lax.axis_index("x"), lax.axis_index("y"), lax.axis_index("c")
        sibling = (x, y, 1 - c)
        chips = [(1 - x, y), (x, 1 - y), (1 - x, 1 - y)]
        index = lambda px, py, pc: 4 * px + 2 * py + pc
        barrier = pltpu.get_barrier_semaphore()
        for peer in [sibling] + [(*chip, c) for chip in chips]:
            pl.semaphore_signal(barrier, inc=1, device_id=peer, device_id_type=MESH)
        pl.semaphore_wait(barrier, 4)

        def copy(t, k, slot, to, src=None):
            return pltpu.make_async_remote_copy(
                src_ref=land_refs[t].at[slot] if src is None else src, dst_ref=land_refs[t].at[slot],
                send_sem=send_sems.at[7 * t + k], recv_sem=recv_sems.at[7 * t + k], device_id=to, device_id_type=MESH)

        me = index(x, y, c)
        local = []
        for t in range(n):
            cp = pltpu.make_async_copy(src_refs[t], land_refs[t].at[me], local_sems.at[t])
            cp.start()
            local.append(cp)
        sends = []
        for t in range(n):
            sends.append(copy(t, 0, me, sibling, src=src_refs[t]))
            sends += [copy(t, 1 + j, me, (*chip, c), src=src_refs[t]) for j, chip in enumerate(chips)]
        for cp in sends:
            cp.start()
        for j, chip in enumerate(chips):
            for t in range(n):
                copy(t, 1 + j, index(*chip, c), sibling, src=src_refs[t]).wait_recv()
                passed = copy(t, 4 + j, index(*chip, c), sibling)
                passed.start()
                sends.append(passed)
        for t in range(n):
            copy(t, 0, index(x, y, 1 - c), sibling, src=src_refs[t]).wait_recv()
        for j, chip in enumerate(chips):
            for t in range(n):
                copy(t, 4 + j, index(*chip, 1 - c), sibling, src=src_refs[t]).wait_recv()
        for cp in sends:
            cp.wait_send()
        for cp in local:
            cp.wait()

    return pl.kernel(
        body, out_type=[_sds((N_DEV,) + a.shape, a.dtype) for a in srcs],
        mesh=plsc.ScalarSubcoreMesh(axis_name="sequencer", num_cores=1),
        scratch_types=[pltpu.SemaphoreType.DMA((7 * n,)), pltpu.SemaphoreType.DMA((7 * n,)), pltpu.SemaphoreType.DMA((n,))],
        compiler_params=pltpu.CompilerParams(collective_id=collective_id), name=name)(*srcs, *extra)


def _xstart(groups, mode, name, after=None):
    flat = [a for g in groups for a in g]
    n, ng = len(flat), len(groups)
    land_shapes = [((N_DEV,) + a.shape) if mode == "gather" else a.shape for a in flat]
    first = [sum(len(g) for g in groups[:i]) for i in range(ng)]
    extra = [] if after is None else [after]

    def body(*refs):
        srcs, lands = refs[:n], refs[n:2 * n]
        sems = refs[2 * n + len(extra):2 * n + len(extra) + 2 * ng]
        tok_ref, local_sems = refs[-2], refs[-1]
        me = _my_index()
        local = []
        for i in range(n):
            cp = pltpu.make_async_copy(srcs[i] if mode == "gather" else srcs[i].at[me], lands[i].at[me], local_sems.at[i])
            cp.start()
            local.append(cp)
        for cp in local:
            cp.wait()
        for gi, g in enumerate(groups):
            for k in range(1, N_DEV):
                peer, pid = _peer(k)
                for t in range(len(g)):
                    i = first[gi] + t
                    _piece_copy(mode, srcs[i], lands[i], sems[2 * gi], sems[2 * gi + 1], t, k, me, pid, peer).start()
        tok_ref[...] = jnp.zeros_like(tok_ref)

    sem_shapes = []
    for g in groups:
        sem_shapes += [pltpu.SemaphoreType.DMA((7 * len(g),))] * 2
    thru = [pltpu.HBM(a.shape, a.dtype) for a in flat] + [pltpu.HBM(s, a.dtype) for s, a in zip(land_shapes, flat)]
    outs = pl.pallas_call(
        body, in_specs=[_HBM] * (2 * n) + [pl.BlockSpec(memory_space=pl.ANY)] * len(extra),
        out_specs=[_SEM] * (2 * ng) + [_HBM] * (2 * n) + [pl.BlockSpec(memory_space=pltpu.VMEM)],
        out_shape=sem_shapes + thru + [_sds((8, 128), F32)],
        input_output_aliases={i: 2 * ng + i for i in range(2 * n)},
        scratch_shapes=[pltpu.SemaphoreType.DMA((n,))],
        compiler_params=pltpu.CompilerParams(has_side_effects=_DATAFLOW), name=name)(
            *[pltpu.with_memory_space_constraint(a, pltpu.HBM) for a in flat],
            *[pltpu.with_memory_space_constraint(lax.empty(s, a.dtype), pltpu.HBM) for s, a in zip(land_shapes, flat)], *extra)
    sems, srcs_thru, lands_thru = outs[:2 * ng], outs[2 * ng:2 * ng + n], outs[2 * ng + n:2 * ng + 2 * n]
    handles = [(sems[2 * gi], sems[2 * gi + 1], srcs_thru[first[gi]:first[gi] + len(g)],
                lands_thru[first[gi]:first[gi] + len(g)]) for gi, g in enumerate(groups)]
    return handles, outs[-1]


def _xwait(handle, mode, after, name):
    send_sems, recv_sems, srcs_thru, lands_thru = handle
    n = len(srcs_thru)

    def body(*refs):
        srcs, lands, send, recv = refs[:n], refs[n:2 * n], refs[2 * n], refs[2 * n + 1]
        me = _my_index()
        for k in range(1, N_DEV):
            peer, pid = _peer(k)
            for t in range(n):
                _piece_copy(mode, srcs[t], lands[t], send, recv, t, k, pid, me, peer).wait_recv()
        for k in range(1, N_DEV):
            peer, pid = _peer(k)
            for t in range(n):
                _piece_copy(mode, srcs[t], lands[t], send, recv, t, k, me, pid, peer).wait_send()

    extra = [] if after is None else [after]
    outs = pl.pallas_call(
        body, in_specs=[_HBM] * (2 * n) + [_SEM, _SEM] + [pl.BlockSpec(memory_space=pl.ANY)] * len(extra),
        out_specs=[_HBM] * (2 * n),
        out_shape=[pltpu.HBM(a.shape, a.dtype) for a in list(srcs_thru) + list(lands_thru)],
        input_output_aliases={i: i for i in range(2 * n)},
        compiler_params=pltpu.CompilerParams(has_side_effects=_DATAFLOW), name=name)(
            *srcs_thru, *lands_thru, send_sems, recv_sems, *extra)
    return outs[n:]


def _adamw(w, g, m, v):
    m = ADAM_B1 * m + (1.0 - ADAM_B1) * g
    v = ADAM_B2 * v + (1.0 - ADAM_B2) * (g * g)
    m_hat = m / (1.0 - ADAM_B1 ** ADAM_STEP)
    v_hat = v / (1.0 - ADAM_B2 ** ADAM_STEP)
    delta = -ADAM_LR * (m_hat / (jnp.sqrt(v_hat) + ADAM_EPS) + ADAM_WD * w)
    return delta, m, v


def _adam_big(w, parts, m, v, name, after=None):
    L, R, C = w.shape
    P = parts[0].shape[0]
    tr = _tile(R, (256, 128, 176, 64, 32, 16))
    nr = R // tr

    def body(w_ref, *refs):
        p_refs, (m_ref, v_ref, g_ref, d_ref, mo_ref, vo_ref) = refs[:L], refs[L:]
        for l in range(L):
            @pl.when(pl.program_id(0) == l)
            def _(p_ref=p_refs[l]):
                g = p_ref[0].astype(F32)
                for s in range(1, P):
                    g = g + p_ref[s].astype(F32)
                g_ref[...] = g
                d_ref[...], mo_ref[...], vo_ref[...] = _adamw(w_ref[...], g, m_ref[...], v_ref[...])

    row = pl.BlockSpec((None, tr, C), lambda l, i: (l, i, 0))
    park = lambda l_of: (lambda l, i: (0, jnp.where(l == l_of, i, 0 if l_of else nr - 1), 0))
    return _call(
        after, body, grid=(L, nr),
        in_specs=[row] + [pl.BlockSpec((P, tr, C), park(l)) for l in range(L)] + [row, row],
        out_specs=[row] * 4, out_shape=[_sds((L, R, C), F32)] * 4,
        compiler_params=_params(("arbitrary", "arbitrary")), name=name)(w, *parts, m, v)


SMALL = (("a_lower_bound", (2, 128), (2, D)), ("ln_gain", (6, 128), (6, D)), ("ln_bias", (6, 128), (6, D)),
         ("a_norm_gain", (1, 128), (1, 128)), ("kv_b", (1, 512), (1, 512)), ("b_b_q", (1, D), (1, D)),
         ("b_sinks", (1, ATT_QH), (1, 128)), ("b_b_out", (1, D), (1, D)), ("ple_b_gate", (2, D), (2, D)))


def _adam_small(parts, w, m, v, losses, after=None):
    k = len(SMALL)

    def body(*refs):
        p_refs, w_refs, m_refs, v_refs = refs[:k], refs[k:2 * k], refs[2 * k:3 * k], refs[3 * k:4 * k]
        loss_ref, outs, total_ref = refs[4 * k], refs[4 * k + 1:-1], refs[-1]
        total = loss_ref[0]
        for s in range(1, N_DEV):
            total = total + loss_ref[s]
        total_ref[...] = total
        me = _my_index()
        for i, (_, wshape, pshape) in enumerate(SMALL):
            cols = wshape[1]
            lanes = slice(None) if cols == pshape[1] else (
                pl.ds(0, cols) if cols < 128 else pl.ds(pl.multiple_of(me * cols, cols), cols))
            g = p_refs[i][0, :, lanes]
            for s in range(1, N_DEV):
                g = g + p_refs[i][s, :, lanes]
            g_ref, d_ref, mo_ref, vo_ref = outs[4 * i:4 * i + 4]
            g_ref[...] = g
            d_ref[...], mo_ref[...], vo_ref[...] = _adamw(w_refs[i][...], g, m_refs[i][...], v_refs[i][...])

    full = lambda shape: pl.BlockSpec(shape, lambda: (0,) * len(shape))
    names = [n for n, _, _ in SMALL]
    res = _call(
        after, body,
        in_specs=[full((N_DEV,) + ps) for _, _, ps in SMALL] + [full(ws) for _, ws, _ in SMALL] * 3
        + [full((N_DEV, 1, 128))],
        out_specs=[full(ws) for _, ws, _ in SMALL for _ in range(4)] + [full((1, 128))],
        out_shape=[_sds(ws, F32) for _, ws, _ in SMALL for _ in range(4)] + [_sds((1, 128), F32)], name="adam_small")(
            *[parts[n] for n in names], *[a[n].reshape(ws) for a in (w, m, v) for n, ws, _ in SMALL], losses)
    return {n: [r.reshape(w[n].shape) for r in res[4 * i:4 * i + 4]] for i, n in enumerate(names)}, res[-1][0, 0]


WEIGHTS = ("a_w_in", "a_lower_bound", "a_norm_gain", "a_w_out", "kv_w", "kv_b", "b_w_q", "b_b_q", "b_sinks", "b_w_out",
           "b_b_out", "ffn_w_gate_up", "ffn_w_down", "ple_w_up", "ple_w_gate", "ple_b_gate", "ln_gain", "ln_bias")


GATHER_GROUPS = (("a_w_in",), ("a_w_out", "gu0"), ("dn0", "pu0", "pg0"), ("kv_w", "b_w_q", "b_w_out"),
                 ("gu1", "dn1", "pu1", "pg1"))
KERNEL_LAYOUT = {
    "a_w_in": lambda a: a,
    "a_w_out": lambda a: a.reshape(D, D),
    "kv_w": lambda a: a.reshape(D, 2 * ATT_KVH * ATT_HD),
    "b_w_q": lambda a: a.reshape(D, D),
    "b_w_out": lambda a: a.reshape(D, D),
    "gu": lambda a: a.reshape(2, 4, FFN_B, D),
    "dn": lambda a: a.reshape(4, FFN_B, D),
    "pu": lambda a: a,
    "pg": lambda a: a.reshape(D, D),
}
_row_blocks = lambda a: a.reshape(N_DEV, -1, a.shape[-1])
OWNER_BLOCKS = {
    "a_w_in": lambda g: g,
    "a_w_out": _row_blocks,
    "kv_w": _row_blocks,
    "b_w_q": _row_blocks,
    "b_w_out": _row_blocks,
    "gu": lambda g: g,
    "dn": lambda g: _row_blocks(g.reshape(FFN_H, D)),
    "pu": lambda g: g.reshape(PLE_DIM, N_DEV, 128).transpose(1, 0, 2),
    "pg": _row_blocks,
}
ADAM_PARTS = (("kv_w", ("kv_w",)), ("b_w_q", ("b_w_q",)), ("b_w_out", ("b_w_out",)), ("ffn_w_gate_up", ("gu0", "gu1")),
              ("ffn_w_down", ("dn0", "dn1")), ("ple_w_up", ("pu0", "pu1")), ("ple_w_gate", ("pg0", "pg1")),
              ("a_w_out", ("a_w_out",)), ("a_w_in", ("a_w_in",)))


def kernel(x, p, a_w_in, a_lower_bound, a_norm_gain, a_w_out, kv_w, kv_b, b_w_q, b_b_q, b_sinks, b_w_out, b_b_out, ffn_w_gate_up, ffn_w_down, ple_w_up, ple_w_gate, ple_b_gate, ln_gain, ln_bias, loss_target, m_a_w_in, m_a_lower_bound, m_a_norm_gain, m_a_w_out, m_kv_w, m_kv_b, m_b_w_q, m_b_b_q, m_b_sinks, m_b_w_out, m_b_b_out, m_ffn_w_gate_up, m_ffn_w_down, m_ple_w_up, m_ple_w_gate, m_ple_b_gate, m_ln_gain, m_ln_bias, v_a_w_in, v_a_lower_bound, v_a_norm_gain, v_a_w_out, v_kv_w, v_kv_b, v_b_w_q, v_b_b_q, v_b_sinks, v_b_w_out, v_b_b_out, v_ffn_w_gate_up, v_ffn_w_down, v_ple_w_up, v_ple_w_gate, v_ple_b_gate, v_ln_gain, v_ln_bias):
    given = dict(locals())
    w = {n: given[n] for n in WEIGHTS}
    m = {n: given["m_" + n] for n in WEIGHTS}
    v = {n: given["v_" + n] for n in WEIGHTS}
    shards = {"a_w_in": a_w_in[0], "a_w_out": a_w_out[0], "kv_w": kv_w, "b_w_q": b_w_q[0], "b_w_out": b_w_out[0]}
    for l in range(2):
        shards.update({f"gu{l}": ffn_w_gate_up[l].T, f"dn{l}": ffn_w_down[l], f"pu{l}": ple_w_up[l], f"pg{l}": ple_w_gate[l]})
    sharded_small = [a_lower_bound, ln_gain.reshape(6, 128), ln_bias.reshape(6, 128)]
    gathered = {}
    for gi, g in enumerate(GATHER_GROUPS):
        lands = _sequencer_gather([shards[n].astype(CDT) for n in g] + (sharded_small if gi == 0 else []),
                                  f"gather{gi}", gi)
        for n, a in zip(g, lands):
            gathered[n] = KERNEL_LAYOUT[n.rstrip("01")](a)
        if gi == 0:
            alb, lng, lnb = [a.transpose(1, 0, 2).reshape(a.shape[1], D) for a in lands[len(g):]]

    def getw(key, after):
        return gathered[key]

    sm = {"a_lower_bound": alb, "ln_gain": lng.reshape(2, 3, D), "ln_bias": lnb.reshape(2, 3, D),
          "a_norm_gain": a_norm_gain, "kv_b": kv_b, "b_b_q": b_b_q[0], "b_sinks": b_sinks, "b_b_out": b_b_out,
          "ple_b_gate": ple_b_gate}

    scatters, small_parts = [], {}

    def emit(grads, small=None):
        names = list(grads)
        blocks = [OWNER_BLOCKS[n.rstrip("01")](grads[n]) for n in names]
        partials = [] if small is None else [small[n].reshape(ps) for n, _, ps in SMALL] + [small["loss"]]
        lands = _sequencer_exchange(blocks + partials, ["scatter"] * len(blocks) + ["gather"] * len(partials),
                                    f"scatter{len(scatters)}", len(GATHER_GROUPS) + len(scatters))
        scatters.append(dict(zip(names, lands)))
        small_parts.update(zip([n for n, _, _ in SMALL] + ["loss"], lands[len(blocks):]))
        return blocks + (list(scatters[0].values()) if small is not None else [])

    loss, grad_x, gs = _local_step(x[0], p[:, 0], loss_target[0], getw, sm, emit)

    out, parts, last = {}, {}, [grad_x]
    for landed in scatters:
        parts.update(landed)
        for n, keys in ADAM_PARTS:
            if n in out or not all(key in parts for key in keys):
                continue
            lrc = (1,) * (3 - w[n].ndim) + w[n].shape
            shard = (lambda a: a.reshape(lrc).swapaxes(1, 2)) if n == "ffn_w_gate_up" else (lambda a: a.reshape(lrc))
            res = _adam_big(shard(w[n]), [parts[key] for key in keys], shard(m[n]), shard(v[n]), "adam_" + n, after=last)
            out[n] = [(r.swapaxes(1, 2) if n == "ffn_w_gate_up" else r).reshape(w[n].shape) for r in res]
            last = [res[3]]
    small_out, loss = _adam_small(small_parts, w, m, v, small_parts["loss"], after=last)
    out.update(small_out)
    res = [loss, grad_x[None]]
    for i in range(4):
        res += [out[n][i] for n in WEIGHTS]
    return tuple(res)
```

```python
import jax
import jax.numpy as jnp
from jax import lax
from jax.experimental import pallas as pl
from jax.experimental.pallas import tpu as pltpu
from jax.experimental.pallas import tpu_sc as plsc

F32 = jnp.float32
CDT = jnp.bfloat16

N_DEV = 8
D = 1024
HG_H, HG_DK, HG_CH = 8, 128, 64
HG_HPB = 4
HG_CPB = 8
ATT_HD, ATT_QH, ATT_KVH, ATT_G, WINDOW = 64, 16, 4, 4, 128
FFN_H = 2816
FFN_B = FFN_H // 4
PLE_DIM = 256
ALPHA = (2.0 * 2) ** 0.25
LN_EPS = 1e-5
RMS_EPS = 1e-6
ADAM_LR, ADAM_B1, ADAM_B2, ADAM_EPS, ADAM_WD, ADAM_STEP = 0.001, 0.9, 0.999, 1e-08, 0.01, 10
ROW_TILES = (512, 256, 128, 64)
VMEM_LIMIT = 48 * 1024 * 1024
NEG = -1e30

MESH = pl.DeviceIdType.MESH


def _tile(n, cands=ROW_TILES):
    for t in cands:
        if n % t == 0:
            return t
    return n


def _sds(shape, dtype):
    return jax.ShapeDtypeStruct(tuple(shape), dtype)


def _params(sem):
    return pltpu.CompilerParams(dimension_semantics=sem, vmem_limit_bytes=VMEM_LIMIT)


def _dot(a, b):
    return jnp.dot(a.astype(CDT), b.astype(CDT), preferred_element_type=F32)


def _dot_nt(a, b):
    return lax.dot_general(a.astype(CDT), b.astype(CDT), (((1,), (1,)), ((), ())), preferred_element_type=F32)


def _dot_tn(a, b):
    return lax.dot_general(a.astype(CDT), b.astype(CDT), (((0,), (0,)), ((), ())), preferred_element_type=F32)


def _sigmoid(x):
    return jax.nn.sigmoid(x)


def _ln_fwd(z, g, b):
    mu = jnp.mean(z, axis=-1, keepdims=True)
    zc = z - mu
    var = jnp.mean(zc * zc, axis=-1, keepdims=True)
    return zc * lax.rsqrt(var + LN_EPS) * g + b


def _ln_bwd(z, g, dy):
    mu = jnp.mean(z, axis=-1, keepdims=True)
    zc = z - mu
    var = jnp.mean(zc * zc, axis=-1, keepdims=True)
    rstd = lax.rsqrt(var + LN_EPS)
    xhat = zc * rstd
    dxh = dy * g
    dz = rstd * (dxh - jnp.mean(dxh, axis=-1, keepdims=True) - xhat * jnp.mean(dxh * xhat, axis=-1, keepdims=True))
    return dz, xhat


def _colsum(x):
    return jnp.sum(x, axis=0, keepdims=True)


def _acc(ref, val, first):
    @pl.when(first)
    def _():
        ref[...] = val

    @pl.when(jnp.logical_not(first))
    def _():
        ref[...] += val


def _zero_at(ref, first):
    @pl.when(first)
    def _():
        ref[...] = jnp.zeros_like(ref)


def _call(after, body, **kw):
    after = [] if after is None else list(after)
    specs = list(kw["in_specs"])
    kw["in_specs"] = [pl.BlockSpec(memory_space=pl.ANY)] * len(after) + specs

    def ordered_body(*refs):
        body(*refs[len(after):])

    call = pl.pallas_call(ordered_body, **kw)

    def pinned(*args):
        args = [a if s.memory_space is not None else pltpu.with_memory_space_constraint(a, pltpu.HBM)
                for a, s in zip(args, specs)]
        return call(*after, *args)

    return pinned


def _mm_nn(a, b3, out_shape, oblock, omap, out_dtype, bias3=None, name="mm_nn"):
    M, K = a.shape
    G, _, Nb = b3.shape
    tm = _tile(M)

    def body(a_ref, b_ref, *rest):
        o_ref = rest[-1]
        acc = _dot(a_ref[...], b_ref[...])
        if bias3 is not None:
            acc = acc + rest[0][...]
        o_ref[...] = acc.astype(o_ref.dtype)

    in_specs = [pl.BlockSpec((tm, K), lambda g, i: (i, 0)), pl.BlockSpec((None, K, Nb), lambda g, i: (g, 0, 0))]
    args = [a, b3]
    if bias3 is not None:
        in_specs.append(pl.BlockSpec((None, 1, Nb), lambda g, i: (g, 0, 0)))
        args.append(bias3)
    return _call(
        None, body, grid=(G, M // tm), in_specs=in_specs, out_specs=pl.BlockSpec(oblock, omap),
        out_shape=_sds(out_shape, out_dtype), compiler_params=_params(("arbitrary", "arbitrary")), name=name)(*args)


def _mm_tn(a3, b3, G, amap, bmap, ablock, bblock, out_shape, oblock, omap, name="mm_tn"):
    S = a3.shape[1]

    def body(a_ref, b_ref, o_ref):
        o_ref[...] = _dot_tn(a_ref[...], b_ref[...]).astype(o_ref.dtype)

    return _call(
        None, body, grid=(G, 1),
        in_specs=[pl.BlockSpec(ablock(S), amap), pl.BlockSpec(bblock(S), bmap)],
        out_specs=pl.BlockSpec(oblock, omap), out_shape=_sds(out_shape, CDT),
        compiler_params=_params(("arbitrary", "arbitrary")), name=name)(a3, b3)


def _wgrad(a3, b3, name):
    Ga, S, M = a3.shape
    Gb, _, N = b3.shape
    G = max(Ga, Gb)
    return _mm_tn(
        a3, b3, G,
        (lambda g, k: (g, k, 0)) if Ga > 1 else (lambda g, k: (0, k, 0)),
        (lambda g, k: (g, k, 0)) if Gb > 1 else (lambda g, k: (0, k, 0)),
        lambda tk: (None, tk, M), lambda tk: (None, tk, N),
        (G, M, N), (None, M, N), lambda g, k: (g, 0, 0), name=name)


def _mixout_ln(u3, w3, bias, xin, gain, beta, name):
    G, S, Kb = u3.shape
    tm = _tile(S)

    def body(u_ref, w_ref, b_ref, x_ref, g_ref, be_ref, z_ref, xo_ref, xob_ref):
        h = b_ref[...] + _dot(u_ref[0], w_ref[0])
        for g in range(1, G):
            h = h + _dot(u_ref[g], w_ref[g])
        z = ALPHA * x_ref[...] + h
        z_ref[...] = z
        y = _ln_fwd(z, g_ref[...], be_ref[...])
        xo_ref[...] = y
        xob_ref[...] = y.astype(CDT)

    row = pl.BlockSpec((tm, D), lambda i: (i, 0))
    vec = pl.BlockSpec((1, D), lambda i: (0, 0))
    return _call(
        None, body, grid=(S // tm,),
        in_specs=[pl.BlockSpec((G, tm, Kb), lambda i: (0, i, 0)), pl.BlockSpec((G, Kb, D), lambda i: (0, 0, 0)),
                  vec, row, vec, vec],
        out_specs=[row, row, row], out_shape=[_sds((S, D), F32), _sds((S, D), F32), _sds((S, D), CDT)],
        compiler_params=_params(("arbitrary",)), name=name)(u3, w3, bias, xin, gain, beta)


def _ffn_fwd(xin, xin_b, wgu, wdn, gain, beta, name):
    S = xin.shape[0]
    tm = _tile(S)

    def hidden(xb_ref, wgu_ref, gu_ref, hid_ref):
        xb = xb_ref[...]
        gate = _dot_nt(xb, wgu_ref[0])
        up = _dot_nt(xb, wgu_ref[1])
        gu_ref[0] = gate.astype(CDT)
        gu_ref[1] = up.astype(CDT)
        hid_ref[...] = (gate * _sigmoid(gate) * up).astype(CDT)

    gu, hid = _call(
        None, hidden, grid=(S // tm, 4),
        in_specs=[pl.BlockSpec((tm, D), lambda i, j: (i, 0)), pl.BlockSpec((2, None, FFN_B, D), lambda i, j: (0, j, 0, 0))],
        out_specs=[pl.BlockSpec((2, None, tm, FFN_B), lambda i, j: (0, j, i, 0)),
                   pl.BlockSpec((None, tm, FFN_B), lambda i, j: (j, i, 0))],
        out_shape=[_sds((2, 4, S, FFN_B), CDT), _sds((4, S, FFN_B), CDT)],
        compiler_params=_params(("arbitrary", "arbitrary")), name=name + "_hidden")(xin_b, wgu)

    def down(x_ref, hid_ref, wdn_ref, g_ref, be_ref, z_ref, xo_ref, xob_ref):
        z = ALPHA * x_ref[...]
        for j in range(4):
            z = z + _dot(hid_ref[j], wdn_ref[j])
        z_ref[...] = z
        y = _ln_fwd(z, g_ref[...], be_ref[...])
        xo_ref[...] = y
        xob_ref[...] = y.astype(CDT)

    row = pl.BlockSpec((tm, D), lambda i: (i, 0))
    vec = pl.BlockSpec((1, D), lambda i: (0, 0))
    z, xo, xob = _call(
        None, down, grid=(S // tm,),
        in_specs=[row, pl.BlockSpec((4, tm, FFN_B), lambda i: (0, i, 0)), pl.BlockSpec((4, FFN_B, D), lambda i: (0, 0, 0)),
                  vec, vec],
        out_specs=[row, row, row], out_shape=[_sds((S, D), F32), _sds((S, D), F32), _sds((S, D), CDT)],
        compiler_params=_params(("arbitrary",)), name=name + "_down")(xin, hid, wdn, gain, beta)
    return gu, hid, z, xo, xob


def _ple_fwd(xin, xin_b, p_b, wpg, bgate, wpu, gain, beta, name):
    S = xin.shape[0]
    tm = _tile(S)

    def body(x_ref, xb_ref, p_ref, wpg_ref, bg_ref, wpu_ref, g_ref, be_ref, sg_ref, up_ref, z_ref, xo_ref, xob_ref):
        sg = _sigmoid(_dot(xb_ref[...], wpg_ref[...]) + bg_ref[...])
        pb = p_ref[...]
        up = jnp.concatenate([_dot(pb, wpu_ref[j]) for j in range(N_DEV)], axis=-1)
        sg_ref[...] = sg.astype(CDT)
        up_ref[...] = up.astype(CDT)
        z = ALPHA * x_ref[...] + sg * up
        z_ref[...] = z
        y = _ln_fwd(z, g_ref[...], be_ref[...])
        xo_ref[...] = y
        xob_ref[...] = y.astype(CDT)

    row = pl.BlockSpec((tm, D), lambda i: (i, 0))
    vec = pl.BlockSpec((1, D), lambda i: (0, 0))
    return _call(
        None, body, grid=(S // tm,),
        in_specs=[row, row, pl.BlockSpec((tm, PLE_DIM), lambda i: (i, 0)), pl.BlockSpec((D, D), lambda i: (0, 0)), vec,
                  pl.BlockSpec((N_DEV, PLE_DIM, D // N_DEV), lambda i: (0, 0, 0)), vec, vec],
        out_specs=[row] * 5,
        out_shape=[_sds((S, D), CDT)] * 2 + [_sds((S, D), F32)] * 2 + [_sds((S, D), CDT)],
        compiler_params=_params(("arbitrary",)), name=name)(xin, xin_b, p_b, wpg, bgate, wpu, gain, beta)


def _loss_fwd_bwd(y, target):
    S = y.shape[0]
    tm = _tile(S)

    def body(y_ref, t_ref, l_ref, dy_ref):
        e = y_ref[...] - t_ref[...]
        dy_ref[...] = e * (1.0 / D)
        part = 0.5 * jnp.sum(jnp.sum(e * e, axis=-1, keepdims=True) * (1.0 / D), axis=0, keepdims=True)
        _acc(l_ref, jnp.broadcast_to(part, l_ref.shape), pl.program_id(0) == 0)

    row = pl.BlockSpec((tm, D), lambda i: (i, 0))
    return _call(
        None, body, grid=(S // tm,), in_specs=[row, row],
        out_specs=[pl.BlockSpec((1, 128), lambda i: (0, 0)), row],
        out_shape=[_sds((1, 128), F32), _sds((S, D), F32)],
        compiler_params=_params(("arbitrary",)), name="loss")(y, target)


def _ple_bwd(dy, z, sg, up, gain, wpg, name, after=None):
    S = dy.shape[0]
    tm = _tile(S)

    def body(dy_ref, z_ref, sg_ref, up_ref, g_ref, wpg_ref, dx_ref, dgl_ref, dup_ref, dgain_ref, dbeta_ref, dbg_ref):
        first = pl.program_id(0) == 0
        dy_ = dy_ref[...]
        dz, xhat = _ln_bwd(z_ref[...], g_ref[...], dy_)
        sg_ = sg_ref[...].astype(F32)
        dgl = dz * up_ref[...].astype(F32) * sg_ * (1.0 - sg_)
        dgl_ref[...] = dgl.astype(CDT)
        dup_ref[...] = (dz * sg_).astype(CDT)
        dx_ref[...] = ALPHA * dz + _dot_nt(dgl, wpg_ref[...])
        _acc(dgain_ref, _colsum(dy_ * xhat), first)
        _acc(dbeta_ref, _colsum(dy_), first)
        _acc(dbg_ref, _colsum(dgl), first)

    row = pl.BlockSpec((tm, D), lambda i: (i, 0))
    vec = pl.BlockSpec((1, D), lambda i: (0, 0))
    return _call(
        after, body, grid=(S // tm,), in_specs=[row, row, row, row, vec, pl.BlockSpec((D, D), lambda i: (0, 0))],
        out_specs=[row, row, row, vec, vec, vec],
        out_shape=[_sds((S, D), F32), _sds((S, D), CDT), _sds((S, D), CDT)] + [_sds((1, D), F32)] * 3,
        compiler_params=_params(("arbitrary",)), name=name)(dy, z, sg, up, gain, wpg)


def _ffn_bwd_hidden(dy, z, gu, wdn, gain, name, after=None):
    S = dy.shape[0]
    tm = _tile(S)

    def hidden(dy_ref, z_ref, gu_ref, wdn_ref, g_ref, dz_ref, dzb_ref, dgu_ref, dgain_ref, dbeta_ref):
        i, j = pl.program_id(0), pl.program_id(1)

        @pl.when(j == 0)
        def _():
            dy_ = dy_ref[...]
            dz, xhat = _ln_bwd(z_ref[...], g_ref[...], dy_)
            dz_ref[...] = dz
            dzb_ref[...] = dz.astype(CDT)
            _acc(dgain_ref, _colsum(dy_ * xhat), i == 0)
            _acc(dbeta_ref, _colsum(dy_), i == 0)

        dhid = _dot_nt(dzb_ref[...], wdn_ref[...])
        gate, up = gu_ref[0].astype(F32), gu_ref[1].astype(F32)
        sg = _sigmoid(gate)
        dgu_ref[0] = (dhid * up * (sg * (1.0 + gate * (1.0 - sg)))).astype(CDT)
        dgu_ref[1] = (dhid * (gate * sg)).astype(CDT)

    row = pl.BlockSpec((tm, D), lambda i, j: (i, 0))
    vec = pl.BlockSpec((1, D), lambda i, j: (0, 0))
    return _call(
        after, hidden, grid=(S // tm, 4),
        in_specs=[row, row, pl.BlockSpec((2, None, tm, FFN_B), lambda i, j: (0, j, i, 0)),
                  pl.BlockSpec((None, FFN_B, D), lambda i, j: (j, 0, 0)), vec],
        out_specs=[row, row, pl.BlockSpec((2, None, tm, FFN_B), lambda i, j: (0, j, i, 0)), vec, vec],
        out_shape=[_sds((S, D), F32), _sds((S, D), CDT), _sds((2, 4, S, FFN_B), CDT), _sds((1, D), F32),
                   _sds((1, D), F32)],
        compiler_params=_params(("arbitrary", "arbitrary")), name=name + "_hidden")(dy, z, gu, wdn, gain)


def _ffn_bwd_input(dz, dgu, wgu, name, after=None):
    S = dz.shape[0]
    tm = _tile(S)

    def to_input(dz_ref, dgu_ref, wgu_ref, dx_ref):
        acc = ALPHA * dz_ref[...]
        for g in range(2):
            for j in range(4):
                acc = acc + _dot(dgu_ref[g, j], wgu_ref[g, j])
        dx_ref[...] = acc

    rows = pl.BlockSpec((tm, D), lambda i: (i, 0))
    return _call(
        after, to_input, grid=(S // tm,),
        in_specs=[rows, pl.BlockSpec((2, 4, tm, FFN_B), lambda i: (0, 0, i, 0)),
                  pl.BlockSpec((2, 4, FFN_B, D), lambda i: (0, 0, 0, 0))],
        out_specs=rows, out_shape=_sds((S, D), F32),
        compiler_params=_params(("arbitrary",)), name=name + "_input")(dz, dgu, wgu)


def _mixout_bwd(dy, z, gain, w3, du_dtype, name, after=None):
    S = dy.shape[0]
    G, Kb, _ = w3.shape
    tm = _tile(S)

    def body(dy_ref, z_ref, g_ref, w_ref, dz_ref, dzb_ref, du_ref, dgain_ref, dbeta_ref, dbias_ref):
        first = pl.program_id(0) == 0
        dy_ = dy_ref[...]
        dz, xhat = _ln_bwd(z_ref[...], g_ref[...], dy_)
        dz_ref[...] = dz
        dzb = dz.astype(CDT)
        dzb_ref[...] = dzb
        for g in range(G):
            du_ref[g] = _dot_nt(dzb, w_ref[g]).astype(du_ref.dtype)
        _acc(dgain_ref, _colsum(dy_ * xhat), first)
        _acc(dbeta_ref, _colsum(dy_), first)
        _acc(dbias_ref, _colsum(dz), first)

    row = pl.BlockSpec((tm, D), lambda i: (i, 0))
    vec = pl.BlockSpec((1, D), lambda i: (0, 0))
    return _call(
        after, body, grid=(S // tm,), in_specs=[row, row, vec, pl.BlockSpec((G, Kb, D), lambda i: (0, 0, 0))],
        out_specs=[row, row, pl.BlockSpec((G, tm, Kb), lambda i: (0, i, 0)), vec, vec, vec],
        out_shape=[_sds((S, D), F32), _sds((S, D), CDT), _sds((G, S, Kb), du_dtype)] + [_sds((1, D), F32)] * 3,
        compiler_params=_params(("arbitrary",)), name=name)(dy, z, gain, w3)


def _half_select(low):
    r = lax.broadcasted_iota(jnp.int32, (2 * ATT_HD, ATT_HD), 0)
    c = lax.broadcasted_iota(jnp.int32, (2 * ATT_HD, ATT_HD), 1)
    return (r == c + (0 if low else ATT_HD)).astype(CDT)


def _half_place(low):
    r = lax.broadcasted_iota(jnp.int32, (ATT_HD, 2 * ATT_HD), 0)
    c = lax.broadcasted_iota(jnp.int32, (ATT_HD, 2 * ATT_HD), 1)
    return (c == r + (0 if low else ATT_HD)).astype(CDT)


def _pair_lanes(even, odd):
    return (jnp.dot(even, _half_place(True), preferred_element_type=F32)
            + jnp.dot(odd, _half_place(False), preferred_element_type=F32)).astype(CDT)


def _proj_heads(a, w, bias, heads, name):
    S, K = a.shape
    N = heads * ATT_HD
    tm = _tile(S)

    def body(a_ref, w_ref, b_ref, o_ref):
        acc = (_dot(a_ref[...], w_ref[...]) + b_ref[...]).astype(CDT)
        sel = (_half_select(True), _half_select(False))
        for h in range(heads):
            pair = acc[:, (h // 2) * 2 * ATT_HD:(h // 2 + 1) * 2 * ATT_HD]
            o_ref[h] = jnp.dot(pair, sel[h % 2], preferred_element_type=F32).astype(CDT)

    return _call(
        None, body, grid=(S // tm,),
        in_specs=[pl.BlockSpec((tm, K), lambda i: (i, 0)), pl.BlockSpec((K, N), lambda i: (0, 0)),
                  pl.BlockSpec((1, N), lambda i: (0, 0))],
        out_specs=pl.BlockSpec((heads, tm, ATT_HD), lambda i: (0, i, 0)), out_shape=_sds((heads, S, ATT_HD), CDT),
        compiler_params=_params(("arbitrary",)), name=name)(a, w, bias)


def _qkv_bwd(dz, dq, dkv4, wq, wkv, name, after=None):
    S = dz.shape[0]
    tm = _tile(S)
    HK = dkv4.shape[0]
    NK = HK * ATT_HD

    def body(dz_ref, dq_ref, dkv_ref, wq_ref, wkv_ref, dx_ref, dkvn_ref, dkvb_ref):
        first = pl.program_id(0) == 0
        dkvn = jnp.concatenate([_pair_lanes(dkv_ref[2 * i].astype(CDT), dkv_ref[2 * i + 1].astype(CDT))
                                for i in range(HK // 2)], axis=-1)
        dkvn_ref[...] = dkvn
        dx_ref[...] = ALPHA * dz_ref[...] + _dot_nt(dq_ref[...], wq_ref[...]) + _dot_nt(dkvn, wkv_ref[...])
        for h in range(HK):
            _acc(dkvb_ref.at[h], _colsum(dkv_ref[h]), first)

    row = pl.BlockSpec((tm, D), lambda i: (i, 0))
    return _call(
        after, body, grid=(S // tm,),
        in_specs=[row, row, pl.BlockSpec((HK, tm, ATT_HD), lambda i: (0, i, 0)),
                  pl.BlockSpec((D, D), lambda i: (0, 0)), pl.BlockSpec((D, NK), lambda i: (0, 0))],
        out_specs=[row, pl.BlockSpec((tm, NK), lambda i: (i, 0)), pl.BlockSpec((HK, 1, ATT_HD), lambda i: (0, 0, 0))],
        out_shape=[_sds((S, D), F32), _sds((S, NK), CDT), _sds((HK, 1, ATT_HD), F32)],
        compiler_params=_params(("arbitrary",)), name=name)(dz, dq, dkv4, wq, wkv)


def _inproj_bwd(dz, dproj, wain, name, after=None):
    S = dz.shape[0]
    tm = _tile(S)
    nb = wain.shape[-1]

    def body(dz_ref, dp_ref, w_ref, dx_ref):
        acc = ALPHA * dz_ref[...]
        for j in range(N_DEV):
            acc = acc + _dot_nt(dp_ref[j // 2, :, pl.ds((j % 2) * nb, nb)], w_ref[j])
        dx_ref[...] = acc

    row = pl.BlockSpec((tm, D), lambda i: (i, 0))
    return _call(
        after, body, grid=(S // tm,),
        in_specs=[row, pl.BlockSpec((4, tm, D), lambda i: (0, i, 0)), pl.BlockSpec((N_DEV, D, nb), lambda i: (0, 0, 0))],
        out_specs=row, out_shape=_sds((S, D), F32),
        compiler_params=_params(("arbitrary",)), name=name)(dz, dproj, wain)


def _running_sum(x, reverse=False):
    rows = x.shape[0]
    row = lax.broadcasted_iota(jnp.int32, x.shape, 0)
    step = 1
    while step < rows:
        if reverse:
            x = x + jnp.where(row < rows - step, pltpu.roll(x, rows - step, 0), 0.0)
        else:
            x = x + jnp.where(row >= step, pltpu.roll(x, step, 0), 0.0)
        step *= 2
    return x


def _hg_gates(q, f, alb_ref):
    a0, a1 = alb_ref[0:1, :], alb_ref[1:2, :]
    mx = jnp.maximum(a0, a1)
    e0, e1 = jnp.exp(a0 - mx), jnp.exp(a1 - mx)
    lb = e0 / (e0 + e1)
    sig = _sigmoid(f)
    forget = lb + (1.0 - lb) * sig
    k = (1.0 - lb) * _sigmoid(-f)
    qs = q * _sigmoid(q) * (HG_DK ** -0.5)
    return qs, k, jnp.log(forget), sig, lb, forget


def _hg_intra(qs, k, b, b_scr):
    b_scr[...] = b
    bm = b_scr[pl.ds(HG_CH // 2 - 1, 1), :]
    bl = b_scr[pl.ds(HG_CH - 1, 1), :]
    eb = jnp.exp(b)
    qb = qs * eb
    e_q = jnp.exp(b - bm)
    e_k = jnp.exp(bm - b)
    e_d = jnp.exp(bl - b)
    return qb, qs * e_q, k * e_k, k * e_d, jnp.exp(bl), eb, e_q, e_k, e_d


def _hgrn_fwd(proj, alb, ngain):
    S = proj.shape[1]
    nc = S // HG_CH
    nb = nc // HG_CPB
    rb, wb = HG_CPB * HG_CH, HG_HPB * HG_DK

    def body(pj_ref, alb_ref, ng_ref, o_ref, y_ref, st_ref, st_scr, b_scr):
        n = pl.program_id(1)

        @pl.when(n == 0)
        def _():
            st_scr[...] = jnp.zeros_like(st_scr)

        r = lax.broadcasted_iota(jnp.int32, (HG_CH, HG_CH), 0)
        c = lax.broadcasted_iota(jnp.int32, (HG_CH, HG_CH), 1)
        causal = r >= c
        for ci, j in [(ci, j) for ci in range(HG_CPB) for j in range(HG_HPB)]:
            rows, lanes = pl.ds(ci * HG_CH, HG_CH), pl.ds(j * HG_DK, HG_DK)
            q, f, v, g = pj_ref[0, rows, lanes], pj_ref[1, rows, lanes], pj_ref[2, rows, lanes], pj_ref[3, rows, lanes]
            qs, k, logf, _, _, _ = _hg_gates(q, f, alb_ref.at[:, lanes])
            b = _running_sum(logf)
            qb, qt, kt, kd, ebl, _, _, _, _ = _hg_intra(qs, k, b, b_scr.at[j, ci])
            st = st_scr[j]
            st_ref[j, ci] = st
            a = jnp.where(causal, _dot_nt(qt, kt), 0.0)
            o = _dot(a, v) + _dot_nt(qb, st)
            st_scr[j] = st * ebl + _dot_tn(v, kd)
            o_ref[rows, lanes] = o
            rinv = lax.rsqrt(jnp.mean(o * o, axis=-1, keepdims=True) + RMS_EPS)
            y_ref[rows, lanes] = (o * rinv * ng_ref[...] * (g * _sigmoid(g))).astype(CDT)

    blk = pl.BlockSpec((rb, wb), lambda h, n: (n, h))
    return _call(
        None, body, grid=(HG_H // HG_HPB, nb),
        in_specs=[pl.BlockSpec((4, rb, wb), lambda h, n: (0, n, h)), pl.BlockSpec((2, wb), lambda h, n: (0, h)),
                  pl.BlockSpec((1, HG_DK), lambda h, n: (0, 0))],
        out_specs=[blk, blk, pl.BlockSpec((HG_HPB, HG_CPB, HG_DK, HG_DK), lambda h, n: (h, n, 0, 0))],
        out_shape=[_sds((S, D), F32), _sds((S, D), CDT), _sds((HG_H, nc, HG_DK, HG_DK), F32)],
        scratch_shapes=[pltpu.VMEM((HG_HPB, HG_DK, HG_DK), F32), pltpu.VMEM((HG_HPB, HG_CPB, HG_CH, HG_DK), F32)],
        compiler_params=_params(("arbitrary", "arbitrary")), name="hgrn_fwd")(proj, alb, ngain)


def _hgrn_bwd(proj, alb, ngain, o, states, dy, after=None):
    S = proj.shape[1]
    nc = S // HG_CH
    nb = nc // HG_CPB
    rb, wb = HG_CPB * HG_CH, HG_HPB * HG_DK

    def body(pj_ref, alb_ref, ng_ref, o_ref, st_ref, dy_ref, dpj_ref, dalb_ref, dng_ref, dst_scr, b_scr):
        h, n = pl.program_id(0), pl.program_id(1)

        @pl.when(n == 0)
        def _():
            dst_scr[...] = jnp.zeros_like(dst_scr)
            dalb_ref[...] = jnp.zeros_like(dalb_ref)

        _zero_at(dng_ref, jnp.logical_and(h == 0, n == 0))
        ng = ng_ref[...]
        r = lax.broadcasted_iota(jnp.int32, (HG_CH, HG_CH), 0)
        c = lax.broadcasted_iota(jnp.int32, (HG_CH, HG_CH), 1)
        causal = r >= c
        dng = None
        for ci, j in [(ci, j) for ci in reversed(range(HG_CPB)) for j in range(HG_HPB)]:
            rows, lanes = pl.ds(ci * HG_CH, HG_CH), pl.ds(j * HG_DK, HG_DK)
            q, f, v, g = pj_ref[0, rows, lanes], pj_ref[1, rows, lanes], pj_ref[2, rows, lanes], pj_ref[3, rows, lanes]
            o_ = o_ref[rows, lanes]
            dy_ = dy_ref[rows, lanes]
            sg = _sigmoid(g)
            rinv = lax.rsqrt(jnp.mean(o_ * o_, axis=-1, keepdims=True) + RMS_EPS)
            nrm = o_ * rinv
            dr = dy_ * (g * sg)
            dg = dy_ * nrm * ng * (sg * (1.0 + g * (1.0 - sg)))
            dn = dr * ng
            do = rinv * (dn - nrm * jnp.mean(dn * nrm, axis=-1, keepdims=True))
            dng = _colsum(dr * nrm) if dng is None else dng + _colsum(dr * nrm)
            qs, k, logf, sig, lb, forget = _hg_gates(q, f, alb_ref.at[:, lanes])
            b = _running_sum(logf)
            qb, qt, kt, kd, ebl, eb, e_q, e_k, e_d = _hg_intra(qs, k, b, b_scr.at[j, ci])
            st = st_ref[j, ci]
            dstn = dst_scr[j]
            qt, kt, qb, kd = (t.astype(CDT).astype(F32) for t in (qt, kt, qb, kd))
            a = jnp.where(causal, _dot_nt(qt, kt), 0.0)
            da = jnp.where(causal, _dot_nt(do, v), 0.0)
            dv = _dot_tn(a, do) + _dot_nt(kd, dstn)
            dqb = _dot(do, st)
            dkd = _dot(v, dstn)
            dqt = _dot(da, kt)
            dkt = _dot_tn(da, qt)
            dbl = _colsum(dkd * kd) + ebl * _colsum(dstn * st)
            dst_scr[j] = dstn * ebl + _dot_tn(do, qb)
            dqs = dqt * e_q + dqb * eb
            dk = dkt * e_k + dkd * e_d
            db = dqt * qt + dqb * qb - dkt * kt - dkd * kd
            dlogf = _running_sum(db, reverse=True) + dbl
            dforget = dlogf / forget
            dsig = (1.0 - lb) * (dforget - dk)
            df = dsig * sig * (1.0 - sig)
            dlb = _colsum((dforget - dk) * (1.0 - sig))
            sq = _sigmoid(q)
            dq = dqs * (HG_DK ** -0.5) * (sq * (1.0 + q * (1.0 - sq)))
            dpj_ref[0, rows, lanes] = dq.astype(CDT)
            dpj_ref[1, rows, lanes] = df.astype(CDT)
            dpj_ref[2, rows, lanes] = dv.astype(CDT)
            dpj_ref[3, rows, lanes] = dg.astype(CDT)
            da0 = dlb * lb * (1.0 - lb)
            dalb_ref[pl.ds(0, 1), lanes] += da0
            dalb_ref[pl.ds(1, 1), lanes] -= da0
        dng_ref[...] += dng

    blk = pl.BlockSpec((rb, wb), lambda h, n: (nb - 1 - n, h))
    pj = pl.BlockSpec((4, rb, wb), lambda h, n: (0, nb - 1 - n, h))
    alb_blk = pl.BlockSpec((2, wb), lambda h, n: (0, h))
    ng_blk = pl.BlockSpec((1, HG_DK), lambda h, n: (0, 0))
    return _call(
        after, body, grid=(HG_H // HG_HPB, nb),
        in_specs=[pj, alb_blk, ng_blk, blk,
                  pl.BlockSpec((HG_HPB, HG_CPB, HG_DK, HG_DK), lambda h, n: (h, nb - 1 - n, 0, 0)), blk],
        out_specs=[pj, alb_blk, ng_blk],
        out_shape=[_sds((4, S, D), CDT), _sds((2, D), F32), _sds((1, HG_DK), F32)],
        scratch_shapes=[pltpu.VMEM((HG_HPB, HG_DK, HG_DK), F32), pltpu.VMEM((HG_HPB, HG_CPB, HG_CH, HG_DK), F32)],
        compiler_params=_params(("arbitrary", "arbitrary")), name="hgrn_bwd")(proj, alb, ngain, o, states, dy)


def _slope(h):
    return 2.0 ** (-8.0 * (h + 1) / ATT_QH)


def _attn_mask(n):
    qi = lax.broadcasted_iota(jnp.int32, (WINDOW, 2 * WINDOW), 0)
    si = lax.broadcasted_iota(jnp.int32, (WINDOW, 2 * WINDOW), 1)
    dist = qi - si + WINDOW
    valid = (dist >= 0) & (dist < WINDOW) & (n * WINDOW - WINDOW + si >= 0)
    return valid, dist.astype(F32)


def _attn_probs(qh, kh, sink, slope, valid, distf):
    s = _dot_nt(qh, kh) * (ATT_HD ** -0.5) - slope * distf
    s = jnp.where(valid, s, NEG)
    m = jnp.maximum(jnp.max(s, axis=-1, keepdims=True), sink)
    e = jnp.exp(s - m)
    es = jnp.exp(sink - m)
    inv = 1.0 / (jnp.sum(e, axis=-1, keepdims=True) + es)
    return e * inv, es * inv


def _attn_specs(S):
    nb = S // WINDOW
    cur = lambda H: pl.BlockSpec((H, WINDOW, ATT_HD), lambda n: (0, n, 0))
    prev = lambda H: pl.BlockSpec((H, WINDOW, ATT_HD), lambda n: (0, jnp.maximum(n - 1, 0), 0))
    return nb, cur, prev


def _attn_fwd(q4, kv4, sinks):
    S = q4.shape[1]
    nb, cur, prev = _attn_specs(S)

    def body(sink_ref, q_ref, kvc_ref, kvp_ref, o_ref):
        valid, distf = _attn_mask(pl.program_id(0))
        for kvh in range(ATT_KVH):
            kh = jnp.concatenate([kvp_ref[kvh], kvc_ref[kvh]], axis=0)
            vh = jnp.concatenate([kvp_ref[ATT_KVH + kvh], kvc_ref[ATT_KVH + kvh]], axis=0)
            v_low = jnp.dot(vh, _half_place(True), preferred_element_type=F32).astype(CDT)
            v_high = jnp.dot(vh, _half_place(False), preferred_element_type=F32).astype(CDT)
            for h in range(kvh * ATT_G, (kvh + 1) * ATT_G, 2):
                p_even, _ = _attn_probs(q_ref[h], kh, sink_ref[0, h], _slope(h), valid, distf)
                p_odd, _ = _attn_probs(q_ref[h + 1], kh, sink_ref[0, h + 1], _slope(h + 1), valid, distf)
                o_ref[:, pl.ds(h * ATT_HD, 2 * ATT_HD)] = (_dot(p_even, v_low) + _dot(p_odd, v_high)).astype(CDT)

    return _call(
        None, body, grid=(nb,),
        in_specs=[pl.BlockSpec(memory_space=pltpu.SMEM), cur(ATT_QH), cur(2 * ATT_KVH), prev(2 * ATT_KVH)],
        out_specs=pl.BlockSpec((WINDOW, D), lambda n: (n, 0)), out_shape=_sds((S, D), CDT),
        compiler_params=_params(("arbitrary",)), name="attn_fwd")(sinks, q4, kv4, kv4)


def _attn_bwd(q4, kv4, sinks, do):
    S = q4.shape[1]
    nb, cur, prev = _attn_specs(S)

    def body(sink_ref, q_ref, kvc_ref, kvp_ref, do_ref, dq_ref, dkv_ref, dbq_ref, dsink_ref):
        n = pl.program_id(0)
        first = n == 0

        @pl.when(first)
        def _():
            dkv_ref[...] = jnp.zeros_like(dkv_ref)
            dsink_ref[...] = jnp.zeros_like(dsink_ref)
            dbq_ref[...] = jnp.zeros_like(dbq_ref)

        valid, distf = _attn_mask(n)
        lane = lax.broadcasted_iota(jnp.int32, (1, 128), 1)
        rows_cur = pl.ds(pl.multiple_of(n * WINDOW, WINDOW), WINDOW)
        rows_prev = pl.ds(pl.multiple_of(jnp.maximum(n - 1, 0) * WINDOW, WINDOW), WINDOW)
        dsinks = jnp.zeros((1, 128), F32)
        sel = (_half_select(True), _half_select(False))
        for kvh in range(ATT_KVH):
            kh = jnp.concatenate([kvp_ref[kvh], kvc_ref[kvh]], axis=0)
            vh = jnp.concatenate([kvp_ref[ATT_KVH + kvh], kvc_ref[ATT_KVH + kvh]], axis=0)
            dk = dv = None
            dqs = []
            for h in range(kvh * ATT_G, (kvh + 1) * ATT_G):
                qh = q_ref[h]
                doh = jnp.dot(do_ref[:, pl.ds((h // 2) * 2 * ATT_HD, 2 * ATT_HD)], sel[h % 2],
                              preferred_element_type=F32).astype(CDT)
                p, ps = _attn_probs(qh, kh, sink_ref[0, h], _slope(h), valid, distf)
                dp = _dot_nt(doh, vh)
                dd = jnp.sum(p * dp, axis=-1, keepdims=True)
                ds = p * (dp - dd)
                dsinks = dsinks + jnp.where(lane == h, -jnp.sum(ps * dd, axis=0, keepdims=True), 0.0)
                dqh = _dot(ds, kh) * (ATT_HD ** -0.5)
                dqs.append(dqh.astype(CDT))
                dbq_ref[h] += _colsum(dqh)
                dkh = _dot_tn(ds, qh) * (ATT_HD ** -0.5)
                dvh = _dot_tn(p, doh)
                dk = dkh if dk is None else dk + dkh
                dv = dvh if dv is None else dv + dvh
            for i in range(ATT_G // 2):
                lanes = pl.ds((kvh * ATT_G + 2 * i) * ATT_HD, 2 * ATT_HD)
                dq_ref[:, lanes] = _pair_lanes(dqs[2 * i], dqs[2 * i + 1])
            dkv_ref[kvh, rows_prev, :] += dk[:WINDOW]
            dkv_ref[kvh, rows_cur, :] += dk[WINDOW:]
            dkv_ref[ATT_KVH + kvh, rows_prev, :] += dv[:WINDOW]
            dkv_ref[ATT_KVH + kvh, rows_cur, :] += dv[WINDOW:]
        dsink_ref[...] += dsinks

    return _call(
        None, body, grid=(nb,),
        in_specs=[pl.BlockSpec(memory_space=pltpu.SMEM), cur(ATT_QH), cur(2 * ATT_KVH), prev(2 * ATT_KVH),
                  pl.BlockSpec((WINDOW, D), lambda n: (n, 0))],
        out_specs=[pl.BlockSpec((WINDOW, D), lambda n: (n, 0)), pl.BlockSpec((2 * ATT_KVH, S, ATT_HD), lambda n: (0, 0, 0)),
                   pl.BlockSpec((ATT_QH, 1, ATT_HD), lambda n: (0, 0, 0)), pl.BlockSpec((1, 128), lambda n: (0, 0))],
        out_shape=[_sds((S, D), CDT), _sds((2 * ATT_KVH, S, ATT_HD), F32), _sds((ATT_QH, 1, ATT_HD), F32),
                   _sds((1, 128), F32)],
        compiler_params=_params(("arbitrary",)), name="attn_bwd")(sinks, q4, kv4, kv4, do)


def _local_step(x, p, target, getw, sm, emit):
    S = x.shape[0]
    vec = lambda a: a.reshape(1, -1)
    ln_g = lambda l, k: vec(sm["ln_gain"][l, k])
    ln_b = lambda l, k: vec(sm["ln_bias"][l, k])
    xb = x.astype(CDT)
    pb = p.astype(CDT)

    proj = _mm_nn(xb, getw("a_w_in", None), (4, S, D), (None, _tile(S), 512), lambda g, i: (g // 2, i, g % 2), F32,
                  name="a_in")
    o_a, y_a, states = _hgrn_fwd(proj, sm["a_lower_bound"], sm["a_norm_gain"])
    zeros = jnp.zeros((1, D), F32)
    z = [[None] * 3 for _ in range(2)]
    xs = [[None] * 3 for _ in range(2)]
    xbs = [[None] * 3 for _ in range(2)]
    z[0][0], xs[0][0], xbs[0][0] = _mixout_ln(y_a[None], getw("a_w_out", y_a)[None], zeros, x, ln_g(0, 0), ln_b(0, 0),
                                              "a_out_ln")
    gu, hid, sgs, ups = [None, None], [None, None], [None, None], [None, None]

    def ffn_ple(l):
        wgu = getw(f"gu{l}", xbs[l][0])
        gu[l], hid[l], z[l][1], xs[l][1], xbs[l][1] = _ffn_fwd(
            xs[l][0], xbs[l][0], wgu, getw(f"dn{l}", None), ln_g(l, 1), ln_b(l, 1), f"ffn_fwd{l}")
        sgs[l], ups[l], z[l][2], xs[l][2], xbs[l][2] = _ple_fwd(
            xs[l][1], xbs[l][1], pb[l], getw(f"pg{l}", None), vec(sm["ple_b_gate"][l]), getw(f"pu{l}", None), ln_g(l, 2),
            ln_b(l, 2), f"ple_fwd{l}")

    ffn_ple(0)
    x3, x3b = xs[0][2], xbs[0][2]
    w_kv, w_q, w_bo = getw("kv_w", x3b), getw("b_w_q", None), getw("b_w_out", None)
    kv4 = _proj_heads(x3b, w_kv, vec(sm["kv_b"]), 2 * ATT_KVH, "kv_proj")
    q4 = _proj_heads(x3b, w_q, vec(sm["b_b_q"]), ATT_QH, "q_proj")
    o_b = _attn_fwd(q4, kv4, sm["b_sinks"])
    z[1][0], xs[1][0], xbs[1][0] = _mixout_ln(o_b[None], w_bo[None], sm["b_b_out"], x3, ln_g(1, 0), ln_b(1, 0),
                                              "b_out_ln")
    ffn_ple(1)
    loss, dy = _loss_fwd_bwd(xs[1][2], target)

    gs = {}
    d_ln_g = [[None] * 3 for _ in range(2)]
    d_ln_b = [[None] * 3 for _ in range(2)]
    g_bg = [None, None]

    def ffn_ple_bwd(l, dy, after=None):
        dx2, dgl, dup, d_ln_g[l][2], d_ln_b[l][2], g_bg[l] = _ple_bwd(dy, z[l][2], sgs[l], ups[l], ln_g(l, 2),
                                                                     getw(f"pg{l}", None), f"ple_bwd{l}", after=after)
        g_pg = _wgrad(xbs[l][1][None], dgl[None], f"g_ple_gate{l}")[0]
        g_pu = _wgrad(pb[l][None], dup[None], f"g_ple_up{l}")[0]
        dz2, dzb, dgu, d_ln_g[l][1], d_ln_b[l][1] = _ffn_bwd_hidden(dx2, z[l][1], gu[l], getw(f"dn{l}", None), ln_g(l, 1),
                                                                   f"ffn_bwd{l}")
        g_dn = _wgrad(hid[l], dzb[None], f"g_ffn_down{l}")
        g_gu = _wgrad(dgu.reshape(8, S, FFN_B), xbs[l][0][None], f"g_ffn_gate_up{l}")
        tok = emit({f"gu{l}": g_gu, f"dn{l}": g_dn, f"pg{l}": g_pg, f"pu{l}": g_pu})
        dx1 = _ffn_bwd_input(dz2, dgu, getw(f"gu{l}", None), f"ffn_bwd{l}", after=tok)
        return dx1, None

    dx1, tok = ffn_ple_bwd(1, dy)
    dz, dzb, do, d_ln_g[1][0], d_ln_b[1][0], gs["b_b_out"] = _mixout_bwd(dx1, z[1][0], ln_g(1, 0), w_bo[None], CDT,
                                                                        "b_out_bwd", after=tok)
    g_bo = _wgrad(o_b[None], dzb[None], "g_b_w_out")[0]
    dq, dkv4, dbq, dsinks = _attn_bwd(q4, kv4, sm["b_sinks"], do[0])
    gs["b_b_q"] = dbq
    gs["b_sinks"] = dsinks
    g_q = _wgrad(x3b[None], dq[None], "g_b_w_q")[0]
    dx3, dkv, gs["kv_b"] = _qkv_bwd(dz, dq, dkv4, w_q, w_kv, "qkv_bwd", after=tok)
    g_kv = _wgrad(x3b[None], dkv[None], "g_kv_w")[0]
    tok = emit({"b_w_out": g_bo, "b_w_q": g_q, "kv_w": g_kv})
    dx1, tok = ffn_ple_bwd(0, dx3, tok)
    w_ao = getw("a_w_out", None)
    dz, dzb, dyr, d_ln_g[0][0], d_ln_b[0][0], _ = _mixout_bwd(dx1, z[0][0], ln_g(0, 0), w_ao[None], F32, "a_out_bwd",
                                                              after=tok)
    g_ao = _wgrad(y_a[None], dzb[None], "g_a_w_out")[0]
    tok = emit({"a_w_out": g_ao})
    dproj, gs["a_lower_bound"], gs["a_norm_gain"] = _hgrn_bwd(proj, sm["a_lower_bound"], sm["a_norm_gain"], o_a, states,
                                                              dyr[0], after=tok)
    tk = lambda t: (None, t, D)
    g_ain = _mm_tn(xb[None], dproj, N_DEV, lambda g, k: (0, k, 0), lambda g, k: (g // 2, k, g % 2),
                   tk, lambda t: (None, t, 512), (N_DEV, D, 512), (None, D, 512), lambda g, k: (g, 0, 0), name="g_a_w_in")
    gs["ple_b_gate"] = jnp.concatenate(g_bg, axis=0)
    gs["ln_gain"] = jnp.stack([jnp.concatenate(r, axis=0) for r in d_ln_g])
    gs["ln_bias"] = jnp.stack([jnp.concatenate(r, axis=0) for r in d_ln_b])
    gs["loss"] = loss
    tok = emit({"a_w_in": g_ain}, small=gs)
    grad_x = _inproj_bwd(dz, dproj, getw("a_w_in", None), "a_in_bwd", after=tok)
    return loss, grad_x, gs


def _peer(k):
    x, y, c = lax.axis_index("x"), lax.axis_index("y"), lax.axis_index("c")
    px = 1 - x if k & 4 else x
    py = 1 - y if k & 2 else y
    pc = 1 - c if k & 1 else c
    return (px, py, pc), 4 * px + 2 * py + pc


def _my_index():
    return 4 * lax.axis_index("x") + 2 * lax.axis_index("y") + lax.axis_index("c")


def _exchange(srcs, dst_shapes, plan, name):
    n_src, n_piece = len(srcs), len(plan)

    def body(*refs):
        src_refs, dst_refs = refs[:n_src], refs[n_src:n_src + len(dst_shapes)]
        send_sems, recv_sems, local_sems = refs[n_src + len(dst_shapes):]
        me = _my_index()

        def at(ref, idx):
            return ref.at[idx] if idx else ref

        local = []
        for t, (si, sfn, di, dfn) in enumerate(plan):
            cp = pltpu.make_async_copy(at(src_refs[si], sfn(me)), at(dst_refs[di], dfn(me)), local_sems.at[t])
            cp.start()
            local.append(cp)
        sends = []
        for k in range(1, N_DEV):
            peer, pid = _peer(k)
            for t, (si, sfn, di, dfn) in enumerate(plan):
                cp = pltpu.make_async_remote_copy(
                    src_ref=at(src_refs[si], sfn(pid)), dst_ref=at(dst_refs[di], dfn(me)),
                    send_sem=send_sems.at[t * 7 + k - 1], recv_sem=recv_sems.at[t * 7 + k - 1],
                    device_id=peer, device_id_type=MESH)
                cp.start()
                sends.append(cp)
        for k in range(1, N_DEV):
            peer, pid = _peer(k)
            for t, (si, sfn, di, dfn) in enumerate(plan):
                pltpu.make_async_remote_copy(
                    src_ref=at(src_refs[si], sfn(me)), dst_ref=at(dst_refs[di], dfn(pid)),
                    send_sem=send_sems.at[t * 7 + k - 1], recv_sem=recv_sems.at[t * 7 + k - 1],
                    device_id=peer, device_id_type=MESH).wait_recv()
        for cp in sends:
            cp.wait_send()
        for cp in local:
            cp.wait()

    hbm = pl.BlockSpec(memory_space=pltpu.HBM)
    return _call(
        None, body, in_specs=[hbm] * n_src, out_specs=[hbm] * len(dst_shapes), out_shape=dst_shapes,
        scratch_shapes=[pltpu.SemaphoreType.DMA((7 * n_piece,)), pltpu.SemaphoreType.DMA((7 * n_piece,)),
                        pltpu.SemaphoreType.DMA((n_piece,))],
        name=name)(*srcs)


def _gather(shards, name):
    dsts = [_sds((N_DEV,) + a.shape, a.dtype) for a in shards]
    plan = [(i, lambda j: (), i, lambda s: (s,)) for i in range(len(shards))]
    return _exchange(shards, dsts, plan, name)


_HBM = pl.BlockSpec(memory_space=pltpu.HBM)
_SEM = pl.BlockSpec(memory_space=pltpu.SEMAPHORE)
_DATAFLOW = pltpu.SideEffectType.DATAFLOW_SIDE_EFFECTING


def _piece_copy(mode, src, land, send_sems, recv_sems, t, k, sender, receiver, peer):
    return pltpu.make_async_remote_copy(
        src_ref=src if mode == "gather" else src.at[receiver], dst_ref=land.at[sender],
        send_sem=send_sems.at[t * 7 + k - 1], recv_sem=recv_sems.at[t * 7 + k - 1], device_id=peer, device_id_type=MESH)


def _sequencer_exchange(srcs, modes, name, collective_id, after=None):
    n = len(srcs)
    land_shapes = [((N_DEV,) + a.shape) if mode == "gather" else a.shape for a, mode in zip(srcs, modes)]
    extra = [] if after is None else [after]

    def body(*refs):
        src_refs, land_refs = refs[:n], refs[n + len(extra):2 * n + len(extra)]
        send_sems, recv_sems, local_sems = refs[2 * n + len(extra):]
        barrier = pltpu.get_barrier_semaphore()
        for k in range(1, N_DEV):
            pl.semaphore_signal(barrier, inc=1, device_id=_peer(k)[0], device_id_type=MESH)
        pl.semaphore_wait(barrier, N_DEV - 1)
        me = _my_index()
        local = []
        for i in range(n):
            cp = pltpu.make_async_copy(src_refs[i] if modes[i] == "gather" else src_refs[i].at[me], land_refs[i].at[me],
                                       local_sems.at[i])
            cp.start()
            local.append(cp)
        for k in range(1, N_DEV):
            peer, pid = _peer(k)
            for t in range(n):
                _piece_copy(modes[t], src_refs[t], land_refs[t], send_sems, recv_sems, t, k, me, pid, peer).start()
        for k in range(1, N_DEV):
            peer, pid = _peer(k)
            for t in range(n):
                _piece_copy(modes[t], src_refs[t], land_refs[t], send_sems, recv_sems, t, k, pid, me, peer).wait_recv()
        for k in range(1, N_DEV):
            peer, pid = _peer(k)
            for t in range(n):
                _piece_copy(modes[t], src_refs[t], land_refs[t], send_sems, recv_sems, t, k, me, pid, peer).wait_send()
        for cp in local:
            cp.wait()

    return pl.kernel(
        body, out_type=[_sds(s, a.dtype) for s, a in zip(land_shapes, srcs)],
        mesh=plsc.ScalarSubcoreMesh(axis_name="sequencer", num_cores=1),
        scratch_types=[pltpu.SemaphoreType.DMA((7 * n,)), pltpu.SemaphoreType.DMA((7 * n,)), pltpu.SemaphoreType.DMA((n,))],
        compiler_params=pltpu.CompilerParams(collective_id=collective_id), name=name)(*srcs, *extra)


def _sequencer_sibling_swap(blocks, name, collective_id):
    n = len(blocks)

    def body(*refs):
        srcs, theirs = refs[:n], refs[n:2 * n]
        send_sems, recv_sems = refs[2 * n:]
        x, y, c = lax.axis_index("x"), lax.axis_index("y"), lax.axis_index("c")
        sibling = (x, y, 1 - c)
        barrier = pltpu.get_barrier_semaphore()
        pl.semaphore_signal(barrier, inc=1, device_id=sibling, device_id_type=MESH)
        pl.semaphore_wait(barrier, 1)
        copies = []
        for t in range(n):
            for q in range(4):
                copies.append(pltpu.make_async_remote_copy(
                    src_ref=srcs[t].at[2 * q + (1 - c)], dst_ref=theirs[t].at[q], send_sem=send_sems.at[4 * t + q],
                    recv_sem=recv_sems.at[4 * t + q], device_id=sibling, device_id_type=MESH))
        for cp in copies:
            cp.start()
        for cp in copies:
            cp.wait_recv()
        for cp in copies:
            cp.wait_send()

    return pl.kernel(
        body, out_type=[_sds((4,) + a.shape[1:], a.dtype) for a in blocks],
        mesh=plsc.ScalarSubcoreMesh(axis_name="sequencer", num_cores=1),
        scratch_types=[pltpu.SemaphoreType.DMA((4 * n,)), pltpu.SemaphoreType.DMA((4 * n,))],
        compiler_params=pltpu.CompilerParams(collective_id=collective_id), name=name)(*blocks)


def _pair_sums(blocks, theirs, name):
    n = len(blocks)

    def body(*refs):
        c = lax.axis_index("c")
        for t in range(n):
            mine = jnp.where(c == 0, refs[3 * t][...].astype(F32), refs[3 * t + 1][...].astype(F32))
            refs[3 * n + t][...] = (mine + refs[3 * t + 2][...].astype(F32)).astype(CDT)

    in_specs, out_specs, steps = [], [], 1
    for a in blocks:
        _, R, C = a.shape
        tr = _tile(R, (256, 128, 176, 64, 32, 16))
        row = lambda q, i, nr=R // tr: jnp.minimum(i, nr - 1)
        in_specs += [pl.BlockSpec((None, None, tr, C), lambda q, i, row=row: (q, 0, row(q, i), 0)),
                     pl.BlockSpec((None, None, tr, C), lambda q, i, row=row: (q, 1, row(q, i), 0)),
                     pl.BlockSpec((None, tr, C), lambda q, i, row=row: (q, row(q, i), 0))]
        out_specs.append(pl.BlockSpec((None, tr, C), lambda q, i, row=row: (q, row(q, i), 0)))
        steps = max(steps, R // tr)
    args = []
    for a, th in zip(blocks, theirs):
        by_core = a.reshape((4, 2) + a.shape[1:])
        args += [by_core, by_core, th]
    return _call(
        None, body, grid=(4, steps), in_specs=in_specs, out_specs=out_specs,
        out_shape=[_sds(th.shape, CDT) for th in theirs],
        compiler_params=_params(("arbitrary", "arbitrary")), name=name)(*args)


def _sequencer_chip_scatter(pairs, name, collective_id):
    n = len(pairs)

    def body(*refs):
        srcs, lands = refs[:n], refs[n:2 * n]
        send_sems, recv_sems = refs[2 * n:]
        x, y, c = lax.axis_index("x"), lax.axis_index("y"), lax.axis_index("c")
        chips = [(1 - x, y), (x, 1 - y), (1 - x, 1 - y)]
        chip = lambda px, py: 2 * px + py
        barrier = pltpu.get_barrier_semaphore()
        for px, py in chips:
            pl.semaphore_signal(barrier, inc=1, device_id=(px, py, c), device_id_type=MESH)
        pl.semaphore_wait(barrier, 3)
        here = chip(x, y)
        sends = []
        for t in range(n):
            for j, (px, py) in enumerate(chips):
                sends.append(pltpu.make_async_remote_copy(
                    src_ref=srcs[t].at[chip(px, py)], dst_ref=lands[t].at[here], send_sem=send_sems.at[3 * t + j],
                    recv_sem=recv_sems.at[3 * t + j], device_id=(px, py, c), device_id_type=MESH))
        for cp in sends:
            cp.start()
        for t in range(n):
            for j, (px, py) in enumerate(chips):
                pltpu.make_async_remote_copy(
                    src_ref=srcs[t].at[here], dst_ref=lands[t].at[chip(px, py)], send_sem=send_sems.at[3 * t + j],
                    recv_sem=recv_sems.at[3 * t + j], device_id=(px, py, c), device_id_type=MESH).wait_recv()
        for cp in sends:
            cp.wait_send()

    return pl.kernel(
        body, out_type=[_sds(a.shape, a.dtype) for a in pairs],
        mesh=plsc.ScalarSubcoreMesh(axis_name="sequencer", num_cores=1),
        scratch_types=[pltpu.SemaphoreType.DMA((3 * n,)), pltpu.SemaphoreType.DMA((3 * n,))],
        compiler_params=pltpu.CompilerParams(collective_id=collective_id), name=name)(*pairs)


def _sequencer_gather(srcs, name, collective_id, after=None):
    n = len(srcs)
    extra = [] if after is None else [after]

    def body(*refs):
        src_refs, land_refs = refs[:n], refs[n + len(extra):2 * n + len(extra)]
        send_sems, recv_sems, local_sems = refs[2 * n + len(extra):]
        x, y, c = lax.axis_index("x"), lax.axis_index("y"), lax.axis_index("c")
        sibling = (x, y, 1 - c)
        chips = [(1 - x, y), (x, 1 - y), (1 - x, 1 - y)]
        index = lambda px, py, pc: 4 * px + 2 * py + pc
        barrier = pltpu.get_barrier_semaphore()
        for peer in [sibling] + [(*chip, c) for chip in chips]:
            pl.semaphore_signal(barrier, inc=1, device_id=peer, device_id_type=MESH)
        pl.semaphore_wait(barrier, 4)

        def copy(t, k, slot, to, src=None):
            return pltpu.make_async_remote_copy(
                src_ref=land_refs[t].at[slot] if src is None else src, dst_ref=land_refs[t].at[slot],
                send_sem=send_sems.at[7 * t + k], recv_sem=recv_sems.at[7 * t + k], device_id=to, device_id_type=MESH)

        me = index(x, y, c)
        local = []
        for t in range(n):
            cp = pltpu.make_async_copy(src_refs[t], land_refs[t].at[me], local_sems.at[t])
            cp.start()
            local.append(cp)
        sends = []
        for t in range(n):
            sends.append(copy(t, 0, me, sibling, src=src_refs[t]))
            sends += [copy(t, 1 + j, me, (*chip, c), src=src_refs[t]) for j, chip in enumerate(chips)]
        for cp in sends:
            cp.start()
        for j, chip in enumerate(chips):
            for t in range(n):
                copy(t, 1 + j, index(*chip, c), sibling, src=src_refs[t]).wait_recv()
                passed = copy(t, 4 + j, index(*chip, c), sibling)
                passed.start()
                sends.append(passed)
        for t in range(n):
            copy(t, 0, index(x, y, 1 - c), sibling, src=src_refs[t]).wait_recv()
        for j, chip in enumerate(chips):
            for t in range(n):
                copy(t, 4 + j, index(*chip, 1 - c), sibling, src=src_refs[t]).wait_recv()
        for cp in sends:
            cp.wait_send()
        for cp in local:
            cp.wait()

    return pl.kernel(
        body, out_type=[_sds((N_DEV,) + a.shape, a.dtype) for a in srcs],
        mesh=plsc.ScalarSubcoreMesh(axis_name="sequencer", num_cores=1),
        scratch_types=[pltpu.SemaphoreType.DMA((7 * n,)), pltpu.SemaphoreType.DMA((7 * n,)), pltpu.SemaphoreType.DMA((n,))],
        compiler_params=pltpu.CompilerParams(collective_id=collective_id), name=name)(*srcs, *extra)


def _xstart(groups, mode, name, after=None):
    flat = [a for g in groups for a in g]
    n, ng = len(flat), len(groups)
    land_shapes = [((N_DEV,) + a.shape) if mode == "gather" else a.shape for a in flat]
    first = [sum(len(g) for g in groups[:i]) for i in range(ng)]
    extra = [] if after is None else [after]

    def body(*refs):
        srcs, lands = refs[:n], refs[n:2 * n]
        sems = refs[2 * n + len(extra):2 * n + len(extra) + 2 * ng]
        tok_ref, local_sems = refs[-2], refs[-1]
        me = _my_index()
        local = []
        for i in range(n):
            cp = pltpu.make_async_copy(srcs[i] if mode == "gather" else srcs[i].at[me], lands[i].at[me], local_sems.at[i])
            cp.start()
            local.append(cp)
        for cp in local:
            cp.wait()
        for gi, g in enumerate(groups):
            for k in range(1, N_DEV):
                peer, pid = _peer(k)
                for t in range(len(g)):
                    i = first[gi] + t
                    _piece_copy(mode, srcs[i], lands[i], sems[2 * gi], sems[2 * gi + 1], t, k, me, pid, peer).start()
        tok_ref[...] = jnp.zeros_like(tok_ref)

    sem_shapes = []
    for g in groups:
        sem_shapes += [pltpu.SemaphoreType.DMA((7 * len(g),))] * 2
    thru = [pltpu.HBM(a.shape, a.dtype) for a in flat] + [pltpu.HBM(s, a.dtype) for s, a in zip(land_shapes, flat)]
    outs = pl.pallas_call(
        body, in_specs=[_HBM] * (2 * n) + [pl.BlockSpec(memory_space=pl.ANY)] * len(extra),
        out_specs=[_SEM] * (2 * ng) + [_HBM] * (2 * n) + [pl.BlockSpec(memory_space=pltpu.VMEM)],
        out_shape=sem_shapes + thru + [_sds((8, 128), F32)],
        input_output_aliases={i: 2 * ng + i for i in range(2 * n)},
        scratch_shapes=[pltpu.SemaphoreType.DMA((n,))],
        compiler_params=pltpu.CompilerParams(has_side_effects=_DATAFLOW), name=name)(
            *[pltpu.with_memory_space_constraint(a, pltpu.HBM) for a in flat],
            *[pltpu.with_memory_space_constraint(lax.empty(s, a.dtype), pltpu.HBM) for s, a in zip(land_shapes, flat)], *extra)
    sems, srcs_thru, lands_thru = outs[:2 * ng], outs[2 * ng:2 * ng + n], outs[2 * ng + n:2 * ng + 2 * n]
    handles = [(sems[2 * gi], sems[2 * gi + 1], srcs_thru[first[gi]:first[gi] + len(g)],
                lands_thru[first[gi]:first[gi] + len(g)]) for gi, g in enumerate(groups)]
    return handles, outs[-1]


def _xwait(handle, mode, after, name):
    send_sems, recv_sems, srcs_thru, lands_thru = handle
    n = len(srcs_thru)

    def body(*refs):
        srcs, lands, send, recv = refs[:n], refs[n:2 * n], refs[2 * n], refs[2 * n + 1]
        me = _my_index()
        for k in range(1, N_DEV):
            peer, pid = _peer(k)
            for t in range(n):
                _piece_copy(mode, srcs[t], lands[t], send, recv, t, k, pid, me, peer).wait_recv()
        for k in range(1, N_DEV):
            peer, pid = _peer(k)
            for t in range(n):
                _piece_copy(mode, srcs[t], lands[t], send, recv, t, k, me, pid, peer).wait_send()

    extra = [] if after is None else [after]
    outs = pl.pallas_call(
        body, in_specs=[_HBM] * (2 * n) + [_SEM, _SEM] + [pl.BlockSpec(memory_space=pl.ANY)] * len(extra),
        out_specs=[_HBM] * (2 * n),
        out_shape=[pltpu.HBM(a.shape, a.dtype) for a in list(srcs_thru) + list(lands_thru)],
        input_output_aliases={i: i for i in range(2 * n)},
        compiler_params=pltpu.CompilerParams(has_side_effects=_DATAFLOW), name=name)(
            *srcs_thru, *lands_thru, send_sems, recv_sems, *extra)
    return outs[n:]


def _adamw(w, g, m, v):
    m = ADAM_B1 * m + (1.0 - ADAM_B1) * g
    v = ADAM_B2 * v + (1.0 - ADAM_B2) * (g * g)
    m_hat = m / (1.0 - ADAM_B1 ** ADAM_STEP)
    v_hat = v / (1.0 - ADAM_B2 ** ADAM_STEP)
    delta = -ADAM_LR * (m_hat / (jnp.sqrt(v_hat) + ADAM_EPS) + ADAM_WD * w)
    return delta, m, v


def _adam_big(w, parts, m, v, name, after=None):
    L, R, C = w.shape
    P = parts[0].shape[0]
    tr = _tile(R, (256, 128, 176, 64, 32, 16))
    nr = R // tr

    def body(w_ref, *refs):
        p_refs, (m_ref, v_ref, g_ref, d_ref, mo_ref, vo_ref) = refs[:L], refs[L:]
        for l in range(L):
            @pl.when(pl.program_id(0) == l)
            def _(p_ref=p_refs[l]):
                g = p_ref[0].astype(F32)
                for s in range(1, P):
                    g = g + p_ref[s].astype(F32)
                g_ref[...] = g
                d_ref[...], mo_ref[...], vo_ref[...] = _adamw(w_ref[...], g, m_ref[...], v_ref[...])

    row = pl.BlockSpec((None, tr, C), lambda l, i: (l, i, 0))
    park = lambda l_of: (lambda l, i: (0, jnp.where(l == l_of, i, 0 if l_of else nr - 1), 0))
    return _call(
        after, body, grid=(L, nr),
        in_specs=[row] + [pl.BlockSpec((P, tr, C), park(l)) for l in range(L)] + [row, row],
        out_specs=[row] * 4, out_shape=[_sds((L, R, C), F32)] * 4,
        compiler_params=_params(("arbitrary", "arbitrary")), name=name)(w, *parts, m, v)


SMALL = (("a_lower_bound", (2, 128), (2, D)), ("ln_gain", (6, 128), (6, D)), ("ln_bias", (6, 128), (6, D)),
         ("a_norm_gain", (1, 128), (1, 128)), ("kv_b", (1, 512), (1, 512)), ("b_b_q", (1, D), (1, D)),
         ("b_sinks", (1, ATT_QH), (1, 128)), ("b_b_out", (1, D), (1, D)), ("ple_b_gate", (2, D), (2, D)))


def _adam_small(parts, w, m, v, losses, after=None):
    k = len(SMALL)

    def body(*refs):
        p_refs, w_refs, m_refs, v_refs = refs[:k], refs[k:2 * k], refs[2 * k:3 * k], refs[3 * k:4 * k]
        loss_ref, outs, total_ref = refs[4 * k], refs[4 * k + 1:-1], refs[-1]
        total = loss_ref[0]
        for s in range(1, N_DEV):
            total = total + loss_ref[s]
        total_ref[...] = total
        me = _my_index()
        for i, (_, wshape, pshape) in enumerate(SMALL):
            cols = wshape[1]
            lanes = slice(None) if cols == pshape[1] else (
                pl.ds(0, cols) if cols < 128 else pl.ds(pl.multiple_of(me * cols, cols), cols))
            g = p_refs[i][0, :, lanes]
            for s in range(1, N_DEV):
                g = g + p_refs[i][s, :, lanes]
            g_ref, d_ref, mo_ref, vo_ref = outs[4 * i:4 * i + 4]
            g_ref[...] = g
            d_ref[...], mo_ref[...], vo_ref[...] = _adamw(w_refs[i][...], g, m_refs[i][...], v_refs[i][...])

    full = lambda shape: pl.BlockSpec(shape, lambda: (0,) * len(shape))
    names = [n for n, _, _ in SMALL]
    res = _call(
        after, body,
        in_specs=[full((N_DEV,) + ps) for _, _, ps in SMALL] + [full(ws) for _, ws, _ in SMALL] * 3
        + [full((N_DEV, 1, 128))],
        out_specs=[full(ws) for _, ws, _ in SMALL for _ in range(4)] + [full((1, 128))],
        out_shape=[_sds(ws, F32) for _, ws, _ in SMALL for _ in range(4)] + [_sds((1, 128), F32)], name="adam_small")(
            *[parts[n] for n in names], *[a[n].reshape(ws) for a in (w, m, v) for n, ws, _ in SMALL], losses)
    return {n: [r.reshape(w[n].shape) for r in res[4 * i:4 * i + 4]] for i, n in enumerate(names)}, res[-1][0, 0]


WEIGHTS = ("a_w_in", "a_lower_bound", "a_norm_gain", "a_w_out", "kv_w", "kv_b", "b_w_q", "b_b_q", "b_sinks", "b_w_out",
           "b_b_out", "ffn_w_gate_up", "ffn_w_down", "ple_w_up", "ple_w_gate", "ple_b_gate", "ln_gain", "ln_bias")


GATHER_GROUPS = (("a_w_in",), ("a_w_out", "gu0"), ("dn0", "pu0", "pg0"), ("kv_w", "b_w_q", "b_w_out"),
                 ("gu1", "dn1", "pu1", "pg1"))
KERNEL_LAYOUT = {
    "a_w_in": lambda a: a,
    "a_w_out": lambda a: a.reshape(D, D),
    "kv_w": lambda a: a.reshape(D, 2 * ATT_KVH * ATT_HD),
    "b_w_q": lambda a: a.reshape(D, D),
    "b_w_out": lambda a: a.reshape(D, D),
    "gu": lambda a: a.reshape(2, 4, FFN_B, D),
    "dn": lambda a: a.reshape(4, FFN_B, D),
    "pu": lambda a: a,
    "pg": lambda a: a.reshape(D, D),
}
_row_blocks = lambda a: a.reshape(N_DEV, -1, a.shape[-1])
OWNER_BLOCKS = {
    "a_w_in": lambda g: g,
    "a_w_out": _row_blocks,
    "kv_w": _row_blocks,
    "b_w_q": _row_blocks,
    "b_w_out": _row_blocks,
    "gu": lambda g: g,
    "dn": lambda g: _row_blocks(g.reshape(FFN_H, D)),
    "pu": lambda g: g.reshape(PLE_DIM, N_DEV, 128).transpose(1, 0, 2),
    "pg": _row_blocks,
}
ADAM_PARTS = (("kv_w", ("kv_w",)), ("b_w_q", ("b_w_q",)), ("b_w_out", ("b_w_out",)), ("ffn_w_gate_up", ("gu0", "gu1")),
              ("ffn_w_down", ("dn0", "dn1")), ("ple_w_up", ("pu0", "pu1")), ("ple_w_gate", ("pg0", "pg1")),
              ("a_w_out", ("a_w_out",)), ("a_w_in", ("a_w_in",)))


def kernel(x, p, a_w_in, a_lower_bound, a_norm_gain, a_w_out, kv_w, kv_b, b_w_q, b_b_q, b_sinks, b_w_out, b_b_out, ffn_w_gate_up, ffn_w_down, ple_w_up, ple_w_gate, ple_b_gate, ln_gain, ln_bias, loss_target, m_a_w_in, m_a_lower_bound, m_a_norm_gain, m_a_w_out, m_kv_w, m_kv_b, m_b_w_q, m_b_b_q, m_b_sinks, m_b_w_out, m_b_b_out, m_ffn_w_gate_up, m_ffn_w_down, m_ple_w_up, m_ple_w_gate, m_ple_b_gate, m_ln_gain, m_ln_bias, v_a_w_in, v_a_lower_bound, v_a_norm_gain, v_a_w_out, v_kv_w, v_kv_b, v_b_w_q, v_b_b_q, v_b_sinks, v_b_w_out, v_b_b_out, v_ffn_w_gate_up, v_ffn_w_down, v_ple_w_up, v_ple_w_gate, v_ple_b_gate, v_ln_gain, v_ln_bias):
    given = dict(locals())
    w = {n: given[n] for n in WEIGHTS}
    m = {n: given["m_" + n] for n in WEIGHTS}
    v = {n: given["v_" + n] for n in WEIGHTS}
    shards = {"a_w_in": a_w_in[0], "a_w_out": a_w_out[0], "kv_w": kv_w, "b_w_q": b_w_q[0], "b_w_out": b_w_out[0]}
    for l in range(2):
        shards.update({f"gu{l}": ffn_w_gate_up[l].T, f"dn{l}": ffn_w_down[l], f"pu{l}": ple_w_up[l], f"pg{l}": ple_w_gate[l]})
    sharded_small = [a_lower_bound, ln_gain.reshape(6, 128), ln_bias.reshape(6, 128)]
    gathered = {}
    for gi, g in enumerate(GATHER_GROUPS):
        lands = _sequencer_gather([shards[n].astype(CDT) for n in g] + (sharded_small if gi == 0 else []),
                                  f"gather{gi}", gi)
        for n, a in zip(g, lands):
            gathered[n] = KERNEL_LAYOUT[n.rstrip("01")](a)
        if gi == 0:
            alb, lng, lnb = [a.transpose(1, 0, 2).reshape(a.shape[1], D) for a in lands[len(g):]]

    def getw(key, after):
        return gathered[key]

    sm = {"a_lower_bound": alb, "ln_gain": lng.reshape(2, 3, D), "ln_bias": lnb.reshape(2, 3, D),
          "a_norm_gain": a_norm_gain, "kv_b": kv_b, "b_b_q": b_b_q[0], "b_sinks": b_sinks, "b_b_out": b_b_out,
          "ple_b_gate": ple_b_gate}

    scatters, small_parts = [], {}

    def emit(grads, small=None):
        names = list(grads)
        blocks = [OWNER_BLOCKS[n.rstrip("01")](grads[n]) for n in names]
        partials = [] if small is None else [small[n].reshape(ps) for n, _, ps in SMALL] + [small["loss"]]
        lands = _sequencer_exchange(blocks + partials, ["scatter"] * len(blocks) + ["gather"] * len(partials),
                                    f"scatter{len(scatters)}", len(GATHER_GROUPS) + len(scatters))
        scatters.append(dict(zip(names, lands)))
        small_parts.update(zip([n for n, _, _ in SMALL] + ["loss"], lands[len(blocks):]))
        return blocks + (list(scatters[0].values()) if small is not None else [])

    loss, grad_x, gs = _local_step(x[0], p[:, 0], loss_target[0], getw, sm, emit)

    out, parts, last = {}, {}, [grad_x]
    for landed in scatters:
        parts.update(landed)
        for n, keys in ADAM_PARTS:
            if n in out or not all(key in parts for key in keys):
                continue
            lrc = (1,) * (3 - w[n].ndim) + w[n].shape
            shard = (lambda a: a.reshape(lrc).swapaxes(1, 2)) if n == "ffn_w_gate_up" else (lambda a: a.reshape(lrc))
            res = _adam_big(shard(w[n]), [parts[key] for key in keys], shard(m[n]), shard(v[n]), "adam_" + n, after=last)
            out[n] = [(r.swapaxes(1, 2) if n == "ffn_w_gate_up" else r).reshape(w[n].shape) for r in res]
            last = [res[3]]
    small_out, loss = _adam_small(small_parts, w, m, v, small_parts["loss"], after=last)
    out.update(small_out)
    res = [loss, grad_x[None]]
    for i in range(4):
        res += [out[n][i] for n in WEIGHTS]
    return tuple(res)
```

```python
import jax
import jax.numpy as jnp
from jax import lax
from jax.experimental import pallas as pl
from jax.experimental.pallas import tpu as pltpu
from jax.experimental.pallas import tpu_sc as plsc

F32 = jnp.float32
CDT = jnp.bfloat16

N_DEV = 8
D = 1024
HG_H, HG_DK, HG_CH = 8, 128, 64
HG_HPB = 4
HG_CPB = 8
ATT_HD, ATT_QH, ATT_KVH, ATT_G, WINDOW = 64, 16, 4, 4, 128
FFN_H = 2816
FFN_B = FFN_H // 4
PLE_DIM = 256
ALPHA = (2.0 * 2) ** 0.25
LN_EPS = 1e-5
RMS_EPS = 1e-6
ADAM_LR, ADAM_B1, ADAM_B2, ADAM_EPS, ADAM_WD, ADAM_STEP = 0.001, 0.9, 0.999, 1e-08, 0.01, 10
ROW_TILES = (512, 256, 128, 64)
VMEM_LIMIT = 48 * 1024 * 1024
NEG = -1e30

MESH = pl.DeviceIdType.MESH


def _tile(n, cands=ROW_TILES):
    for t in cands:
        if n % t == 0:
            return t
    return n


def _sds(shape, dtype):
    return jax.ShapeDtypeStruct(tuple(shape), dtype)


def _params(sem):
    return pltpu.CompilerParams(dimension_semantics=sem, vmem_limit_bytes=VMEM_LIMIT)


def _dot(a, b):
    return jnp.dot(a.astype(CDT), b.astype(CDT), preferred_element_type=F32)


def _dot_nt(a, b):
    return lax.dot_general(a.astype(CDT), b.astype(CDT), (((1,), (1,)), ((), ())), preferred_element_type=F32)


def _dot_tn(a, b):
    return lax.dot_general(a.astype(CDT), b.astype(CDT), (((0,), (0,)), ((), ())), preferred_element_type=F32)


def _sigmoid(x):
    return jax.nn.sigmoid(x)


def _ln_fwd(z, g, b):
    mu = jnp.mean(z, axis=-1, keepdims=True)
    zc = z - mu
    var = jnp.mean(zc * zc, axis=-1, keepdims=True)
    return zc * lax.rsqrt(var + LN_EPS) * g + b


def _ln_bwd(z, g, dy):
    mu = jnp.mean(z, axis=-1, keepdims=True)
    zc = z - mu
    var = jnp.mean(zc * zc, axis=-1, keepdims=True)
    rstd = lax.rsqrt(var + LN_EPS)
    xhat = zc * rstd
    dxh = dy * g
    dz = rstd * (dxh - jnp.mean(dxh, axis=-1, keepdims=True) - xhat * jnp.mean(dxh * xhat, axis=-1, keepdims=True))
    return dz, xhat


def _colsum(x):
    return jnp.sum(x, axis=0, keepdims=True)


def _acc(ref, val, first):
    @pl.when(first)
    def _():
        ref[...] = val

    @pl.when(jnp.logical_not(first))
    def _():
        ref[...] += val


def _zero_at(ref, first):
    @pl.when(first)
    def _():
        ref[...] = jnp.zeros_like(ref)


def _call(after, body, **kw):
    after = [] if after is None else list(after)
    specs = list(kw["in_specs"])
    kw["in_specs"] = [pl.BlockSpec(memory_space=pl.ANY)] * len(after) + specs

    def ordered_body(*refs):
        body(*refs[len(after):])

    call = pl.pallas_call(ordered_body, **kw)

    def pinned(*args):
        args = [a if s.memory_space is not None else pltpu.with_memory_space_constraint(a, pltpu.HBM)
                for a, s in zip(args, specs)]
        return call(*after, *args)

    return pinned


def _mm_nn(a, b3, out_shape, oblock, omap, out_dtype, bias3=None, name="mm_nn"):
    M, K = a.shape
    G, _, Nb = b3.shape
    tm = _tile(M)

    def body(a_ref, b_ref, *rest):
        o_ref = rest[-1]
        acc = _dot(a_ref[...], b_ref[...])
        if bias3 is not None:
            acc = acc + rest[0][...]
        o_ref[...] = acc.astype(o_ref.dtype)

    in_specs = [pl.BlockSpec((tm, K), lambda g, i: (i, 0)), pl.BlockSpec((None, K, Nb), lambda g, i: (g, 0, 0))]
    args = [a, b3]
    if bias3 is not None:
        in_specs.append(pl.BlockSpec((None, 1, Nb), lambda g, i: (g, 0, 0)))
        args.append(bias3)
    return _call(
        None, body, grid=(G, M // tm), in_specs=in_specs, out_specs=pl.BlockSpec(oblock, omap),
        out_shape=_sds(out_shape, out_dtype), compiler_params=_params(("arbitrary", "arbitrary")), name=name)(*args)


def _mm_tn(a3, b3, G, amap, bmap, ablock, bblock, out_shape, oblock, omap, name="mm_tn"):
    S = a3.shape[1]

    def body(a_ref, b_ref, o_ref):
        o_ref[...] = _dot_tn(a_ref[...], b_ref[...]).astype(o_ref.dtype)

    return _call(
        None, body, grid=(G, 1),
        in_specs=[pl.BlockSpec(ablock(S), amap), pl.BlockSpec(bblock(S), bmap)],
        out_specs=pl.BlockSpec(oblock, omap), out_shape=_sds(out_shape, CDT),
        compiler_params=_params(("arbitrary", "arbitrary")), name=name)(a3, b3)


def _wgrad(a3, b3, name):
    Ga, S, M = a3.shape
    Gb, _, N = b3.shape
    G = max(Ga, Gb)
    return _mm_tn(
        a3, b3, G,
        (lambda g, k: (g, k, 0)) if Ga > 1 else (lambda g, k: (0, k, 0)),
        (lambda g, k: (g, k, 0)) if Gb > 1 else (lambda g, k: (0, k, 0)),
        lambda tk: (None, tk, M), lambda tk: (None, tk, N),
        (G, M, N), (None, M, N), lambda g, k: (g, 0, 0), name=name)


def _mixout_ln(u3, w3, bias, xin, gain, beta, name):
    G, S, Kb = u3.shape
    tm = _tile(S)

    def body(u_ref, w_ref, b_ref, x_ref, g_ref, be_ref, z_ref, xo_ref, xob_ref):
        h = b_ref[...] + _dot(u_ref[0], w_ref[0])
        for g in range(1, G):
            h = h + _dot(u_ref[g], w_ref[g])
        z = ALPHA * x_ref[...] + h
        z_ref[...] = z
        y = _ln_fwd(z, g_ref[...], be_ref[...])
        xo_ref[...] = y
        xob_ref[...] = y.astype(CDT)

    row = pl.BlockSpec((tm, D), lambda i: (i, 0))
    vec = pl.BlockSpec((1, D), lambda i: (0, 0))
    return _call(
        None, body, grid=(S // tm,),
        in_specs=[pl.BlockSpec((G, tm, Kb), lambda i: (0, i, 0)), pl.BlockSpec((G, Kb, D), lambda i: (0, 0, 0)),
                  vec, row, vec, vec],
        out_specs=[row, row, row], out_shape=[_sds((S, D), F32), _sds((S, D), F32), _sds((S, D), CDT)],
        compiler_params=_params(("arbitrary",)), name=name)(u3, w3, bias, xin, gain, beta)


def _ffn_fwd(xin, xin_b, wgu, wdn, gain, beta, name):
    S = xin.shape[0]
    tm = _tile(S)

    def hidden(xb_ref, wgu_ref, gu_ref, hid_ref):
        xb = xb_ref[...]
        gate = _dot_nt(xb, wgu_ref[0])
        up = _dot_nt(xb, wgu_ref[1])
        gu_ref[0] = gate.astype(CDT)
        gu_ref[1] = up.astype(CDT)
        hid_ref[...] = (gate * _sigmoid(gate) * up).astype(CDT)

    gu, hid = _call(
        None, hidden, grid=(4, S // tm),
        in_specs=[pl.BlockSpec((tm, D), lambda j, i: (i, 0)), pl.BlockSpec((2, None, FFN_B, D), lambda j, i: (0, j, 0, 0))],
        out_specs=[pl.BlockSpec((2, None, tm, FFN_B), lambda j, i: (0, j, i, 0)),
                   pl.BlockSpec((None, tm, FFN_B), lambda j, i: (j, i, 0))],
        out_shape=[_sds((2, 4, S, FFN_B), CDT), _sds((4, S, FFN_B), CDT)],
        compiler_params=_params(("arbitrary", "arbitrary")), name=name + "_hidden")(xin_b, wgu)

    def down(x_ref, hid_ref, wdn_ref, g_ref, be_ref, z_ref, xo_ref, xob_ref):
        z = ALPHA * x_ref[...]
        for j in range(4):
            z = z + _dot(hid_ref[j], wdn_ref[j])
        z_ref[...] = z
        y = _ln_fwd(z, g_ref[...], be_ref[...])
        xo_ref[...] = y
        xob_ref[...] = y.astype(CDT)

    row = pl.BlockSpec((tm, D), lambda i: (i, 0))
    vec = pl.BlockSpec((1, D), lambda i: (0, 0))
    z, xo, xob = _call(
        None, down, grid=(S // tm,),
        in_specs=[row, pl.BlockSpec((4, tm, FFN_B), lambda i: (0, i, 0)), pl.BlockSpec((4, FFN_B, D), lambda i: (0, 0, 0)),
                  vec, vec],
        out_specs=[row, row, row], out_shape=[_sds((S, D), F32), _sds((S, D), F32), _sds((S, D), CDT)],
        compiler_params=_params(("arbitrary",)), name=name + "_down")(xin, hid, wdn, gain, beta)
    return gu, hid, z, xo, xob


def _ple_fwd(xin, xin_b, p_b, wpg, bgate, wpu, gain, beta, name):
    S = xin.shape[0]
    tm = _tile(S)

    def body(x_ref, xb_ref, p_ref, wpg_ref, bg_ref, wpu_ref, g_ref, be_ref, sg_ref, up_ref, z_ref, xo_ref, xob_ref):
        sg = _sigmoid(_dot(xb_ref[...], wpg_ref[...]) + bg_ref[...])
        pb = p_ref[...]
        up = jnp.concatenate([_dot(pb, wpu_ref[j]) for j in range(N_DEV)], axis=-1)
        sg_ref[...] = sg.astype(CDT)
        up_ref[...] = up.astype(CDT)
        z = ALPHA * x_ref[...] + sg * up
        z_ref[...] = z
        y = _ln_fwd(z, g_ref[...], be_ref[...])
        xo_ref[...] = y
        xob_ref[...] = y.astype(CDT)

    row = pl.BlockSpec((tm, D), lambda i: (i, 0))
    vec = pl.BlockSpec((1, D), lambda i: (0, 0))
    return _call(
        None, body, grid=(S // tm,),
        in_specs=[row, row, pl.BlockSpec((tm, PLE_DIM), lambda i: (i, 0)), pl.BlockSpec((D, D), lambda i: (0, 0)), vec,
                  pl.BlockSpec((N_DEV, PLE_DIM, D // N_DEV), lambda i: (0, 0, 0)), vec, vec],
        out_specs=[row] * 5,
        out_shape=[_sds((S, D), CDT)] * 2 + [_sds((S, D), F32)] * 2 + [_sds((S, D), CDT)],
        compiler_params=_params(("arbitrary",)), name=name)(xin, xin_b, p_b, wpg, bgate, wpu, gain, beta)


def _loss_fwd_bwd(y, target):
    S = y.shape[0]
    tm = _tile(S)

    def body(y_ref, t_ref, l_ref, dy_ref):
        e = y_ref[...] - t_ref[...]
        dy_ref[...] = e * (1.0 / D)
        part = 0.5 * jnp.sum(jnp.sum(e * e, axis=-1, keepdims=True) * (1.0 / D), axis=0, keepdims=True)
        _acc(l_ref, jnp.broadcast_to(part, l_ref.shape), pl.program_id(0) == 0)

    row = pl.BlockSpec((tm, D), lambda i: (i, 0))
    return _call(
        None, body, grid=(S // tm,), in_specs=[row, row],
        out_specs=[pl.BlockSpec((1, 128), lambda i: (0, 0)), row],
        out_shape=[_sds((1, 128), F32), _sds((S, D), F32)],
        compiler_params=_params(("arbitrary",)), name="loss")(y, target)


def _ple_bwd(dy, z, sg, up, gain, wpg, name, after=None):
    S = dy.shape[0]
    tm = _tile(S)

    def body(dy_ref, z_ref, sg_ref, up_ref, g_ref, wpg_ref, dx_ref, dgl_ref, dup_ref, dgain_ref, dbeta_ref, dbg_ref):
        first = pl.program_id(0) == 0
        dy_ = dy_ref[...]
        dz, xhat = _ln_bwd(z_ref[...], g_ref[...], dy_)
        sg_ = sg_ref[...].astype(F32)
        dgl = dz * up_ref[...].astype(F32) * sg_ * (1.0 - sg_)
        dgl_ref[...] = dgl.astype(CDT)
        dup_ref[...] = (dz * sg_).astype(CDT)
        dx_ref[...] = ALPHA * dz + _dot_nt(dgl, wpg_ref[...])
        _acc(dgain_ref, _colsum(dy_ * xhat), first)
        _acc(dbeta_ref, _colsum(dy_), first)
        _acc(dbg_ref, _colsum(dgl), first)

    row = pl.BlockSpec((tm, D), lambda i: (i, 0))
    vec = pl.BlockSpec((1, D), lambda i: (0, 0))
    return _call(
        after, body, grid=(S // tm,), in_specs=[row, row, row, row, vec, pl.BlockSpec((D, D), lambda i: (0, 0))],
        out_specs=[row, row, row, vec, vec, vec],
        out_shape=[_sds((S, D), F32), _sds((S, D), CDT), _sds((S, D), CDT)] + [_sds((1, D), F32)] * 3,
        compiler_params=_params(("arbitrary",)), name=name)(dy, z, sg, up, gain, wpg)


def _ffn_bwd_hidden(dy, z, gu, wdn, gain, name, after=None):
    S = dy.shape[0]
    tm = _tile(S)

    def hidden(dy_ref, z_ref, gu_ref, wdn_ref, g_ref, dz_ref, dzb_ref, dgu_ref, dgain_ref, dbeta_ref):
        i, j = pl.program_id(0), pl.program_id(1)

        @pl.when(j == 0)
        def _():
            dy_ = dy_ref[...]
            dz, xhat = _ln_bwd(z_ref[...], g_ref[...], dy_)
            dz_ref[...] = dz
            dzb_ref[...] = dz.astype(CDT)
            _acc(dgain_ref, _colsum(dy_ * xhat), i == 0)
            _acc(dbeta_ref, _colsum(dy_), i == 0)

        dhid = _dot_nt(dzb_ref[...], wdn_ref[...])
        gate, up = gu_ref[0].astype(F32), gu_ref[1].astype(F32)
        sg = _sigmoid(gate)
        dgu_ref[0] = (dhid * up * (sg * (1.0 + gate * (1.0 - sg)))).astype(CDT)
        dgu_ref[1] = (dhid * (gate * sg)).astype(CDT)

    row = pl.BlockSpec((tm, D), lambda i, j: (i, 0))
    vec = pl.BlockSpec((1, D), lambda i, j: (0, 0))
    return _call(
        after, hidden, grid=(S // tm, 4),
        in_specs=[row, row, pl.BlockSpec((2, None, tm, FFN_B), lambda i, j: (0, j, i, 0)),
                  pl.BlockSpec((None, FFN_B, D), lambda i, j: (j, 0, 0)), vec],
        out_specs=[row, row, pl.BlockSpec((2, None, tm, FFN_B), lambda i, j: (0, j, i, 0)), vec, vec],
        out_shape=[_sds((S, D), F32), _sds((S, D), CDT), _sds((2, 4, S, FFN_B), CDT), _sds((1, D), F32),
                   _sds((1, D), F32)],
        compiler_params=_params(("arbitrary", "arbitrary")), name=name + "_hidden")(dy, z, gu, wdn, gain)


def _ffn_bwd_input(dz, dgu, wgu, name, after=None):
    S = dz.shape[0]
    tm = _tile(S)

    def to_input(dz_ref, dgu_ref, wgu_ref, dx_ref):
        acc = ALPHA * dz_ref[...]
        for g in range(2):
            for j in range(4):
                acc = acc + _dot(dgu_ref[g, j], wgu_ref[g, j])
        dx_ref[...] = acc

    rows = pl.BlockSpec((tm, D), lambda i: (i, 0))
    return _call(
        after, to_input, grid=(S // tm,),
        in_specs=[rows, pl.BlockSpec((2, 4, tm, FFN_B), lambda i: (0, 0, i, 0)),
                  pl.BlockSpec((2, 4, FFN_B, D), lambda i: (0, 0, 0, 0))],
        out_specs=rows, out_shape=_sds((S, D), F32),
        compiler_params=_params(("arbitrary",)), name=name + "_input")(dz, dgu, wgu)


def _mixout_bwd(dy, z, gain, w3, du_dtype, name, after=None):
    S = dy.shape[0]
    G, Kb, _ = w3.shape
    tm = _tile(S)

    def body(dy_ref, z_ref, g_ref, w_ref, dz_ref, dzb_ref, du_ref, dgain_ref, dbeta_ref, dbias_ref):
        first = pl.program_id(0) == 0
        dy_ = dy_ref[...]
        dz, xhat = _ln_bwd(z_ref[...], g_ref[...], dy_)
        dz_ref[...] = dz
        dzb = dz.astype(CDT)
        dzb_ref[...] = dzb
        for g in range(G):
            du_ref[g] = _dot_nt(dzb, w_ref[g]).astype(du_ref.dtype)
        _acc(dgain_ref, _colsum(dy_ * xhat), first)
        _acc(dbeta_ref, _colsum(dy_), first)
        _acc(dbias_ref, _colsum(dz), first)

    row = pl.BlockSpec((tm, D), lambda i: (i, 0))
    vec = pl.BlockSpec((1, D), lambda i: (0, 0))
    return _call(
        after, body, grid=(S // tm,), in_specs=[row, row, vec, pl.BlockSpec((G, Kb, D), lambda i: (0, 0, 0))],
        out_specs=[row, row, pl.BlockSpec((G, tm, Kb), lambda i: (0, i, 0)), vec, vec, vec],
        out_shape=[_sds((S, D), F32), _sds((S, D), CDT), _sds((G, S, Kb), du_dtype)] + [_sds((1, D), F32)] * 3,
        compiler_params=_params(("arbitrary",)), name=name)(dy, z, gain, w3)


def _half_select(low):
    r = lax.broadcasted_iota(jnp.int32, (2 * ATT_HD, ATT_HD), 0)
    c = lax.broadcasted_iota(jnp.int32, (2 * ATT_HD, ATT_HD), 1)
    return (r == c + (0 if low else ATT_HD)).astype(CDT)


def _half_place(low):
    r = lax.broadcasted_iota(jnp.int32, (ATT_HD, 2 * ATT_HD), 0)
    c = lax.broadcasted_iota(jnp.int32, (ATT_HD, 2 * ATT_HD), 1)
    return (c == r + (0 if low else ATT_HD)).astype(CDT)


def _pair_lanes(even, odd):
    return (jnp.dot(even, _half_place(True), preferred_element_type=F32)
            + jnp.dot(odd, _half_place(False), preferred_element_type=F32)).astype(CDT)


def _proj_heads(a, w, bias, heads, name):
    S, K = a.shape
    N = heads * ATT_HD
    tm = _tile(S)

    def body(a_ref, w_ref, b_ref, o_ref):
        acc = (_dot(a_ref[...], w_ref[...]) + b_ref[...]).astype(CDT)
        sel = (_half_select(True), _half_select(False))
        for h in range(heads):
            pair = acc[:, (h // 2) * 2 * ATT_HD:(h // 2 + 1) * 2 * ATT_HD]
            o_ref[h] = jnp.dot(pair, sel[h % 2], preferred_element_type=F32).astype(CDT)

    return _call(
        None, body, grid=(S // tm,),
        in_specs=[pl.BlockSpec((tm, K), lambda i: (i, 0)), pl.BlockSpec((K, N), lambda i: (0, 0)),
                  pl.BlockSpec((1, N), lambda i: (0, 0))],
        out_specs=pl.BlockSpec((heads, tm, ATT_HD), lambda i: (0, i, 0)), out_shape=_sds((heads, S, ATT_HD), CDT),
        compiler_params=_params(("arbitrary",)), name=name)(a, w, bias)


def _qkv_bwd(dz, dq, dkv4, wq, wkv, name, after=None):
    S = dz.shape[0]
    tm = _tile(S)
    HK = dkv4.shape[0]
    NK = HK * ATT_HD

    def body(dz_ref, dq_ref, dkv_ref, wq_ref, wkv_ref, dx_ref, dkvn_ref, dkvb_ref):
        first = pl.program_id(0) == 0
        dkvn = jnp.concatenate([_pair_lanes(dkv_ref[2 * i].astype(CDT), dkv_ref[2 * i + 1].astype(CDT))
                                for i in range(HK // 2)], axis=-1)
        dkvn_ref[...] = dkvn
        dx_ref[...] = ALPHA * dz_ref[...] + _dot_nt(dq_ref[...], wq_ref[...]) + _dot_nt(dkvn, wkv_ref[...])
        for h in range(HK):
            _acc(dkvb_ref.at[h], _colsum(dkv_ref[h]), first)

    row = pl.BlockSpec((tm, D), lambda i: (i, 0))
    return _call(
        after, body, grid=(S // tm,),
        in_specs=[row, row, pl.BlockSpec((HK, tm, ATT_HD), lambda i: (0, i, 0)),
                  pl.BlockSpec((D, D), lambda i: (0, 0)), pl.BlockSpec((D, NK), lambda i: (0, 0))],
        out_specs=[row, pl.BlockSpec((tm, NK), lambda i: (i, 0)), pl.BlockSpec((HK, 1, ATT_HD), lambda i: (0, 0, 0))],
        out_shape=[_sds((S, D), F32), _sds((S, NK), CDT), _sds((HK, 1, ATT_HD), F32)],
        compiler_params=_params(("arbitrary",)), name=name)(dz, dq, dkv4, wq, wkv)


def _inproj_bwd(dz, dproj, wain, name, after=None):
    S = dz.shape[0]
    tm = _tile(S)
    nb = wain.shape[-1]

    def body(dz_ref, dp_ref, w_ref, dx_ref):
        acc = ALPHA * dz_ref[...]
        for j in range(N_DEV):
            acc = acc + _dot_nt(dp_ref[j // 2, :, pl.ds((j % 2) * nb, nb)], w_ref[j])
        dx_ref[...] = acc

    row = pl.BlockSpec((tm, D), lambda i: (i, 0))
    return _call(
        after, body, grid=(S // tm,),
        in_specs=[row, pl.BlockSpec((4, tm, D), lambda i: (0, i, 0)), pl.BlockSpec((N_DEV, D, nb), lambda i: (0, 0, 0))],
        out_specs=row, out_shape=_sds((S, D), F32),
        compiler_params=_params(("arbitrary",)), name=name)(dz, dproj, wain)


def _running_sum(x, reverse=False):
    rows = x.shape[0]
    row = lax.broadcasted_iota(jnp.int32, x.shape, 0)
    step = 1
    while step < rows:
        if reverse:
            x = x + jnp.where(row < rows - step, pltpu.roll(x, rows - step, 0), 0.0)
        else:
            x = x + jnp.where(row >= step, pltpu.roll(x, step, 0), 0.0)
        step *= 2
    return x


def _hg_gates(q, f, alb_ref):
    a0, a1 = alb_ref[0:1, :], alb_ref[1:2, :]
    mx = jnp.maximum(a0, a1)
    e0, e1 = jnp.exp(a0 - mx), jnp.exp(a1 - mx)
    lb = e0 / (e0 + e1)
    sig = _sigmoid(f)
    forget = lb + (1.0 - lb) * sig
    k = (1.0 - lb) * _sigmoid(-f)
    qs = q * _sigmoid(q) * (HG_DK ** -0.5)
    return qs, k, jnp.log(forget), sig, lb, forget


def _hg_intra(qs, k, b, b_scr):
    b_scr[...] = b
    bm = b_scr[pl.ds(HG_CH // 2 - 1, 1), :]
    bl = b_scr[pl.ds(HG_CH - 1, 1), :]
    eb = jnp.exp(b)
    qb = qs * eb
    e_q = jnp.exp(b - bm)
    e_k = jnp.exp(bm - b)
    e_d = jnp.exp(bl - b)
    return qb, qs * e_q, k * e_k, k * e_d, jnp.exp(bl), eb, e_q, e_k, e_d


def _hgrn_fwd(proj, alb, ngain):
    S = proj.shape[1]
    nc = S // HG_CH
    nb = nc // HG_CPB
    rb, wb = HG_CPB * HG_CH, HG_HPB * HG_DK

    def body(pj_ref, alb_ref, ng_ref, o_ref, y_ref, st_ref, st_scr, b_scr):
        n = pl.program_id(1)

        @pl.when(n == 0)
        def _():
            st_scr[...] = jnp.zeros_like(st_scr)

        r = lax.broadcasted_iota(jnp.int32, (HG_CH, HG_CH), 0)
        c = lax.broadcasted_iota(jnp.int32, (HG_CH, HG_CH), 1)
        causal = r >= c
        for ci, j in [(ci, j) for ci in range(HG_CPB) for j in range(HG_HPB)]:
            rows, lanes = pl.ds(ci * HG_CH, HG_CH), pl.ds(j * HG_DK, HG_DK)
            q, f, v, g = pj_ref[0, rows, lanes], pj_ref[1, rows, lanes], pj_ref[2, rows, lanes], pj_ref[3, rows, lanes]
            qs, k, logf, _, _, _ = _hg_gates(q, f, alb_ref.at[:, lanes])
            b = _running_sum(logf)
            qb, qt, kt, kd, ebl, _, _, _, _ = _hg_intra(qs, k, b, b_scr.at[j, ci])
            st = st_scr[j]
            st_ref[j, ci] = st
            a = jnp.where(causal, _dot_nt(qt, kt), 0.0)
            o = _dot(a, v) + _dot_nt(qb, st)
            st_scr[j] = st * ebl + _dot_tn(v, kd)
            o_ref[rows, lanes] = o
            rinv = lax.rsqrt(jnp.mean(o * o, axis=-1, keepdims=True) + RMS_EPS)
            y_ref[rows, lanes] = (o * rinv * ng_ref[...] * (g * _sigmoid(g))).astype(CDT)

    blk = pl.BlockSpec((rb, wb), lambda h, n: (n, h))
    return _call(
        None, body, grid=(HG_H // HG_HPB, nb),
        in_specs=[pl.BlockSpec((4, rb, wb), lambda h, n: (0, n, h)), pl.BlockSpec((2, wb), lambda h, n: (0, h)),
                  pl.BlockSpec((1, HG_DK), lambda h, n: (0, 0))],
        out_specs=[blk, blk, pl.BlockSpec((HG_HPB, HG_CPB, HG_DK, HG_DK), lambda h, n: (h, n, 0, 0))],
        out_shape=[_sds((S, D), F32), _sds((S, D), CDT), _sds((HG_H, nc, HG_DK, HG_DK), F32)],
        scratch_shapes=[pltpu.VMEM((HG_HPB, HG_DK, HG_DK), F32), pltpu.VMEM((HG_HPB, HG_CPB, HG_CH, HG_DK), F32)],
        compiler_params=_params(("arbitrary", "arbitrary")), name="hgrn_fwd")(proj, alb, ngain)


def _hgrn_bwd(proj, alb, ngain, o, states, dy, after=None):
    S = proj.shape[1]
    nc = S // HG_CH
    nb = nc // HG_CPB
    rb, wb = HG_CPB * HG_CH, HG_HPB * HG_DK

    def body(pj_ref, alb_ref, ng_ref, o_ref, st_ref, dy_ref, dpj_ref, dalb_ref, dng_ref, dst_scr, b_scr):
        h, n = pl.program_id(0), pl.program_id(1)

        @pl.when(n == 0)
        def _():
            dst_scr[...] = jnp.zeros_like(dst_scr)
            dalb_ref[...] = jnp.zeros_like(dalb_ref)

        _zero_at(dng_ref, jnp.logical_and(h == 0, n == 0))
        ng = ng_ref[...]
        r = lax.broadcasted_iota(jnp.int32, (HG_CH, HG_CH), 0)
        c = lax.broadcasted_iota(jnp.int32, (HG_CH, HG_CH), 1)
        causal = r >= c
        dng = None
        for ci, j in [(ci, j) for ci in reversed(range(HG_CPB)) for j in range(HG_HPB)]:
            rows, lanes = pl.ds(ci * HG_CH, HG_CH), pl.ds(j * HG_DK, HG_DK)
            q, f, v, g = pj_ref[0, rows, lanes], pj_ref[1, rows, lanes], pj_ref[2, rows, lanes], pj_ref[3, rows, lanes]
            o_ = o_ref[rows, lanes]
            dy_ = dy_ref[rows, lanes]
            sg = _sigmoid(g)
            rinv = lax.rsqrt(jnp.mean(o_ * o_, axis=-1, keepdims=True) + RMS_EPS)
            nrm = o_ * rinv
            dr = dy_ * (g * sg)
            dg = dy_ * nrm * ng * (sg * (1.0 + g * (1.0 - sg)))
            dn = dr * ng
            do = rinv * (dn - nrm * jnp.mean(dn * nrm, axis=-1, keepdims=True))
            dng = _colsum(dr * nrm) if dng is None else dng + _colsum(dr * nrm)
            qs, k, logf, sig, lb, forget = _hg_gates(q, f, alb_ref.at[:, lanes])
            b = _running_sum(logf)
            qb, qt, kt, kd, ebl, eb, e_q, e_k, e_d = _hg_intra(qs, k, b, b_scr.at[j, ci])
            st = st_ref[j, ci]
            dstn = dst_scr[j]
            qt, kt, qb, kd = (t.astype(CDT).astype(F32) for t in (qt, kt, qb, kd))
            a = jnp.where(causal, _dot_nt(qt, kt), 0.0)
            da = jnp.where(causal, _dot_nt(do, v), 0.0)
            dv = _dot_tn(a, do) + _dot_nt(kd, dstn)
            dqb = _dot(do, st)
            dkd = _dot(v, dstn)
            dqt = _dot(da, kt)
            dkt = _dot_tn(da, qt)
            dbl = _colsum(dkd * kd) + ebl * _colsum(dstn * st)
            dst_scr[j] = dstn * ebl + _dot_tn(do, qb)
            dqs = dqt * e_q + dqb * eb
            dk = dkt * e_k + dkd * e_d
            db = dqt * qt + dqb * qb - dkt * kt - dkd * kd
            dlogf = _running_sum(db, reverse=True) + dbl
            dforget = dlogf / forget
            dsig = (1.0 - lb) * (dforget - dk)
            df = dsig * sig * (1.0 - sig)
            dlb = _colsum((dforget - dk) * (1.0 - sig))
            sq = _sigmoid(q)
            dq = dqs * (HG_DK ** -0.5) * (sq * (1.0 + q * (1.0 - sq)))
            dpj_ref[0, rows, lanes] = dq.astype(CDT)
            dpj_ref[1, rows, lanes] = df.astype(CDT)
            dpj_ref[2, rows, lanes] = dv.astype(CDT)
            dpj_ref[3, rows, lanes] = dg.astype(CDT)
            da0 = dlb * lb * (1.0 - lb)
            dalb_ref[pl.ds(0, 1), lanes] += da0
            dalb_ref[pl.ds(1, 1), lanes] -= da0
        dng_ref[...] += dng

    blk = pl.BlockSpec((rb, wb), lambda h, n: (nb - 1 - n, h))
    pj = pl.BlockSpec((4, rb, wb), lambda h, n: (0, nb - 1 - n, h))
    alb_blk = pl.BlockSpec((2, wb), lambda h, n: (0, h))
    ng_blk = pl.BlockSpec((1, HG_DK), lambda h, n: (0, 0))
    return _call(
        after, body, grid=(HG_H // HG_HPB, nb),
        in_specs=[pj, alb_blk, ng_blk, blk,
                  pl.BlockSpec((HG_HPB, HG_CPB, HG_DK, HG_DK), lambda h, n: (h, nb - 1 - n, 0, 0)), blk],
        out_specs=[pj, alb_blk, ng_blk],
        out_shape=[_sds((4, S, D), CDT), _sds((2, D), F32), _sds((1, HG_DK), F32)],
        scratch_shapes=[pltpu.VMEM((HG_HPB, HG_DK, HG_DK), F32), pltpu.VMEM((HG_HPB, HG_CPB, HG_CH, HG_DK), F32)],
        compiler_params=_params(("arbitrary", "arbitrary")), name="hgrn_bwd")(proj, alb, ngain, o, states, dy)


def _slope(h):
    return 2.0 ** (-8.0 * (h + 1) / ATT_QH)


def _attn_mask(n):
    qi = lax.broadcasted_iota(jnp.int32, (WINDOW, 2 * WINDOW), 0)
    si = lax.broadcasted_iota(jnp.int32, (WINDOW, 2 * WINDOW), 1)
    dist = qi - si + WINDOW
    valid = (dist >= 0) & (dist < WINDOW) & (n * WINDOW - WINDOW + si >= 0)
    return valid, dist.astype(F32)


def _attn_probs(qh, kh, sink, slope, valid, distf):
    s = _dot_nt(qh, kh) * (ATT_HD ** -0.5) - slope * distf
    s = jnp.where(valid, s, NEG)
    m = jnp.maximum(jnp.max(s, axis=-1, keepdims=True), sink)
    e = jnp.exp(s - m)
    es = jnp.exp(sink - m)
    inv = 1.0 / (jnp.sum(e, axis=-1, keepdims=True) + es)
    return e * inv, es * inv


def _attn_specs(S):
    nb = S // WINDOW
    cur = lambda H: pl.BlockSpec((H, WINDOW, ATT_HD), lambda n: (0, n, 0))
    prev = lambda H: pl.BlockSpec((H, WINDOW, ATT_HD), lambda n: (0, jnp.maximum(n - 1, 0), 0))
    return nb, cur, prev


def _attn_fwd(q4, kv4, sinks):
    S = q4.shape[1]
    nb, cur, prev = _attn_specs(S)

    def body(sink_ref, q_ref, kvc_ref, kvp_ref, o_ref):
        valid, distf = _attn_mask(pl.program_id(0))
        for kvh in range(ATT_KVH):
            kh = jnp.concatenate([kvp_ref[kvh], kvc_ref[kvh]], axis=0)
            vh = jnp.concatenate([kvp_ref[ATT_KVH + kvh], kvc_ref[ATT_KVH + kvh]], axis=0)
            v_low = jnp.dot(vh, _half_place(True), preferred_element_type=F32).astype(CDT)
            v_high = jnp.dot(vh, _half_place(False), preferred_element_type=F32).astype(CDT)
            for h in range(kvh * ATT_G, (kvh + 1) * ATT_G, 2):
                p_even, _ = _attn_probs(q_ref[h], kh, sink_ref[0, h], _slope(h), valid, distf)
                p_odd, _ = _attn_probs(q_ref[h + 1], kh, sink_ref[0, h + 1], _slope(h + 1), valid, distf)
                o_ref[:, pl.ds(h * ATT_HD, 2 * ATT_HD)] = (_dot(p_even, v_low) + _dot(p_odd, v_high)).astype(CDT)

    return _call(
        None, body, grid=(nb,),
        in_specs=[pl.BlockSpec(memory_space=pltpu.SMEM), cur(ATT_QH), cur(2 * ATT_KVH), prev(2 * ATT_KVH)],
        out_specs=pl.BlockSpec((WINDOW, D), lambda n: (n, 0)), out_shape=_sds((S, D), CDT),
        compiler_params=_params(("arbitrary",)), name="attn_fwd")(sinks, q4, kv4, kv4)


def _attn_bwd(q4, kv4, sinks, do):
    S = q4.shape[1]
    nb, cur, prev = _attn_specs(S)

    def body(sink_ref, q_ref, kvc_ref, kvp_ref, do_ref, dq_ref, dkv_ref, dbq_ref, dsink_ref):
        n = pl.program_id(0)
        first = n == 0

        @pl.when(first)
        def _():
            dkv_ref[...] = jnp.zeros_like(dkv_ref)
            dsink_ref[...] = jnp.zeros_like(dsink_ref)
            dbq_ref[...] = jnp.zeros_like(dbq_ref)

        valid, distf = _attn_mask(n)
        lane = lax.broadcasted_iota(jnp.int32, (1, 128), 1)
        rows_cur = pl.ds(pl.multiple_of(n * WINDOW, WINDOW), WINDOW)
        rows_prev = pl.ds(pl.multiple_of(jnp.maximum(n - 1, 0) * WINDOW, WINDOW), WINDOW)
        dsinks = jnp.zeros((1, 128), F32)
        sel = (_half_select(True), _half_select(False))
        for kvh in range(ATT_KVH):
            kh = jnp.concatenate([kvp_ref[kvh], kvc_ref[kvh]], axis=0)
            vh = jnp.concatenate([kvp_ref[ATT_KVH + kvh], kvc_ref[ATT_KVH + kvh]], axis=0)
            dk = dv = None
            dqs = []
            for h in range(kvh * ATT_G, (kvh + 1) * ATT_G):
                qh = q_ref[h]
                doh = jnp.dot(do_ref[:, pl.ds((h // 2) * 2 * ATT_HD, 2 * ATT_HD)], sel[h % 2],
                              preferred_element_type=F32).astype(CDT)
                p, ps = _attn_probs(qh, kh, sink_ref[0, h], _slope(h), valid, distf)
                dp = _dot_nt(doh, vh)
                dd = jnp.sum(p * dp, axis=-1, keepdims=True)
                ds = p * (dp - dd)
                dsinks = dsinks + jnp.where(lane == h, -jnp.sum(ps * dd, axis=0, keepdims=True), 0.0)
                dqh = _dot(ds, kh) * (ATT_HD ** -0.5)
                dqs.append(dqh.astype(CDT))
                dbq_ref[h] += _colsum(dqh)
                dkh = _dot_tn(ds, qh) * (ATT_HD ** -0.5)
                dvh = _dot_tn(p, doh)
                dk = dkh if dk is None else dk + dkh
                dv = dvh if dv is None else dv + dvh
            for i in range(ATT_G // 2):
                lanes = pl.ds((kvh * ATT_G + 2 * i) * ATT_HD, 2 * ATT_HD)
                dq_ref[:, lanes] = _pair_lanes(dqs[2 * i], dqs[2 * i + 1])
            dkv_ref[kvh, rows_prev, :] += dk[:WINDOW]
            dkv_ref[kvh, rows_cur, :] += dk[WINDOW:]
            dkv_ref[ATT_KVH + kvh, rows_prev, :] += dv[:WINDOW]
            dkv_ref[ATT_KVH + kvh, rows_cur, :] += dv[WINDOW:]
        dsink_ref[...] += dsinks

    return _call(
        None, body, grid=(nb,),
        in_specs=[pl.BlockSpec(memory_space=pltpu.SMEM), cur(ATT_QH), cur(2 * ATT_KVH), prev(2 * ATT_KVH),
                  pl.BlockSpec((WINDOW, D), lambda n: (n, 0))],
        out_specs=[pl.BlockSpec((WINDOW, D), lambda n: (n, 0)), pl.BlockSpec((2 * ATT_KVH, S, ATT_HD), lambda n: (0, 0, 0)),
                   pl.BlockSpec((ATT_QH, 1, ATT_HD), lambda n: (0, 0, 0)), pl.BlockSpec((1, 128), lambda n: (0, 0))],
        out_shape=[_sds((S, D), CDT), _sds((2 * ATT_KVH, S, ATT_HD), F32), _sds((ATT_QH, 1, ATT_HD), F32),
                   _sds((1, 128), F32)],
        compiler_params=_params(("arbitrary",)), name="attn_bwd")(sinks, q4, kv4, kv4, do)


def _local_step(x, p, target, getw, sm, emit):
    S = x.shape[0]
    vec = lambda a: a.reshape(1, -1)
    ln_g = lambda l, k: vec(sm["ln_gain"][l, k])
    ln_b = lambda l, k: vec(sm["ln_bias"][l, k])
    xb = x.astype(CDT)
    pb = p.astype(CDT)

    proj = _mm_nn(xb, getw("a_w_in"), (4, S, D), (None, _tile(S), 512), lambda g, i: (g // 2, i, g % 2), F32,
                  name="a_in")
    o_a, y_a, states = _hgrn_fwd(proj, sm["a_lower_bound"], sm["a_norm_gain"])
    zeros = jnp.zeros((1, D), F32)
    z = [[None] * 3 for _ in range(2)]
    xs = [[None] * 3 for _ in range(2)]
    xbs = [[None] * 3 for _ in range(2)]
    z[0][0], xs[0][0], xbs[0][0] = _mixout_ln(y_a[None], getw("a_w_out")[None], zeros, x, ln_g(0, 0), ln_b(0, 0),
                                              "a_out_ln")
    gu, hid, sgs, ups = [None, None], [None, None], [None, None], [None, None]

    def ffn_ple(l):
        wgu = getw(f"gu{l}")
        gu[l], hid[l], z[l][1], xs[l][1], xbs[l][1] = _ffn_fwd(
            xs[l][0], xbs[l][0], wgu, getw(f"dn{l}"), ln_g(l, 1), ln_b(l, 1), f"ffn_fwd{l}")
        sgs[l], ups[l], z[l][2], xs[l][2], xbs[l][2] = _ple_fwd(
            xs[l][1], xbs[l][1], pb[l], getw(f"pg{l}"), vec(sm["ple_b_gate"][l]), getw(f"pu{l}"), ln_g(l, 2),
            ln_b(l, 2), f"ple_fwd{l}")

    ffn_ple(0)
    x3, x3b = xs[0][2], xbs[0][2]
    w_kv, w_q, w_bo = getw("kv_w"), getw("b_w_q"), getw("b_w_out")
    kv4 = _proj_heads(x3b, w_kv, vec(sm["kv_b"]), 2 * ATT_KVH, "kv_proj")
    q4 = _proj_heads(x3b, w_q, vec(sm["b_b_q"]), ATT_QH, "q_proj")
    o_b = _attn_fwd(q4, kv4, sm["b_sinks"])
    z[1][0], xs[1][0], xbs[1][0] = _mixout_ln(o_b[None], w_bo[None], sm["b_b_out"], x3, ln_g(1, 0), ln_b(1, 0),
                                              "b_out_ln")
    ffn_ple(1)
    loss, dy = _loss_fwd_bwd(xs[1][2], target)

    gs = {}
    d_ln_g = [[None] * 3 for _ in range(2)]
    d_ln_b = [[None] * 3 for _ in range(2)]
    g_bg = [None, None]

    def ffn_ple_bwd(l, dy, after=None):
        dx2, dgl, dup, d_ln_g[l][2], d_ln_b[l][2], g_bg[l] = _ple_bwd(dy, z[l][2], sgs[l], ups[l], ln_g(l, 2),
                                                                     getw(f"pg{l}"), f"ple_bwd{l}", after=after)
        g_pg = _wgrad(xbs[l][1][None], dgl[None], f"g_ple_gate{l}")[0]
        g_pu = _wgrad(pb[l][None], dup[None], f"g_ple_up{l}")[0]
        dz2, dzb, dgu, d_ln_g[l][1], d_ln_b[l][1] = _ffn_bwd_hidden(dx2, z[l][1], gu[l], getw(f"dn{l}"), ln_g(l, 1),
                                                                   f"ffn_bwd{l}")
        g_dn = _wgrad(hid[l], dzb[None], f"g_ffn_down{l}")
        g_gu = _wgrad(dgu.reshape(8, S, FFN_B), xbs[l][0][None], f"g_ffn_gate_up{l}")
        tok = emit({f"gu{l}": g_gu, f"dn{l}": g_dn, f"pg{l}": g_pg, f"pu{l}": g_pu})
        dx1 = _ffn_bwd_input(dz2, dgu, getw(f"gu{l}"), f"ffn_bwd{l}", after=tok)
        return dx1, None

    dx1, tok = ffn_ple_bwd(1, dy)
    dz, dzb, do, d_ln_g[1][0], d_ln_b[1][0], gs["b_b_out"] = _mixout_bwd(dx1, z[1][0], ln_g(1, 0), w_bo[None], CDT,
                                                                        "b_out_bwd", after=tok)
    g_bo = _wgrad(o_b[None], dzb[None], "g_b_w_out")[0]
    dq, dkv4, dbq, dsinks = _attn_bwd(q4, kv4, sm["b_sinks"], do[0])
    gs["b_b_q"] = dbq
    gs["b_sinks"] = dsinks
    g_q = _wgrad(x3b[None], dq[None], "g_b_w_q")[0]
    dx3, dkv, gs["kv_b"] = _qkv_bwd(dz, dq, dkv4, w_q, w_kv, "qkv_bwd", after=tok)
    g_kv = _wgrad(x3b[None], dkv[None], "g_kv_w")[0]
    tok = emit({"b_w_out": g_bo, "b_w_q": g_q, "kv_w": g_kv})
    dx1, tok = ffn_ple_bwd(0, dx3, tok)
    w_ao = getw("a_w_out")
    dz, dzb, dyr, d_ln_g[0][0], d_ln_b[0][0], _ = _mixout_bwd(dx1, z[0][0], ln_g(0, 0), w_ao[None], F32, "a_out_bwd",
                                                              after=tok)
    g_ao = _wgrad(y_a[None], dzb[None], "g_a_w_out")[0]
    tok = emit({"a_w_out": g_ao})
    dproj, gs["a_lower_bound"], gs["a_norm_gain"] = _hgrn_bwd(proj, sm["a_lower_bound"], sm["a_norm_gain"], o_a, states,
                                                              dyr[0], after=tok)
    tk = lambda t: (None, t, D)
    g_ain = _mm_tn(xb[None], dproj, N_DEV, lambda g, k: (0, k, 0), lambda g, k: (g // 2, k, g % 2),
                   tk, lambda t: (None, t, 512), (N_DEV, D, 512), (None, D, 512), lambda g, k: (g, 0, 0), name="g_a_w_in")
    gs["ple_b_gate"] = jnp.concatenate(g_bg, axis=0)
    gs["ln_gain"] = jnp.stack([jnp.concatenate(r, axis=0) for r in d_ln_g])
    gs["ln_bias"] = jnp.stack([jnp.concatenate(r, axis=0) for r in d_ln_b])
    gs["loss"] = loss
    tok = emit({"a_w_in": g_ain}, small=gs)
    grad_x = _inproj_bwd(dz, dproj, getw("a_w_in"), "a_in_bwd", after=tok)
    return loss, grad_x, gs


def _peer(k):
    x, y, c = lax.axis_index("x"), lax.axis_index("y"), lax.axis_index("c")
    px = 1 - x if k & 4 else x
    py = 1 - y if k & 2 else y
    pc = 1 - c if k & 1 else c
    return (px, py, pc), 4 * px + 2 * py + pc


def _my_index():
    return 4 * lax.axis_index("x") + 2 * lax.axis_index("y") + lax.axis_index("c")


def _piece_copy(mode, src, land, send_sems, recv_sems, t, k, sender, receiver, peer):
    return pltpu.make_async_remote_copy(
        src_ref=src if mode == "gather" else src.at[receiver], dst_ref=land.at[sender],
        send_sem=send_sems.at[t * 7 + k - 1], recv_sem=recv_sems.at[t * 7 + k - 1], device_id=peer, device_id_type=MESH)


def _sequencer_exchange(srcs, modes, name, collective_id, after=None):
    n = len(srcs)
    land_shapes = [((N_DEV,) + a.shape) if mode == "gather" else a.shape for a, mode in zip(srcs, modes)]
    extra = [] if after is None else [after]

    def body(*refs):
        src_refs, land_refs = refs[:n], refs[n + len(extra):2 * n + len(extra)]
        send_sems, recv_sems, local_sems = refs[2 * n + len(extra):]
        barrier = pltpu.get_barrier_semaphore()
        for k in range(1, N_DEV):
            pl.semaphore_signal(barrier, inc=1, device_id=_peer(k)[0], device_id_type=MESH)
        pl.semaphore_wait(barrier, N_DEV - 1)
        me = _my_index()
        local = []
        for i in range(n):
            cp = pltpu.make_async_copy(src_refs[i] if modes[i] == "gather" else src_refs[i].at[me], land_refs[i].at[me],
                                       local_sems.at[i])
            cp.start()
            local.append(cp)
        for k in range(1, N_DEV):
            peer, pid = _peer(k)
            for t in range(n):
                _piece_copy(modes[t], src_refs[t], land_refs[t], send_sems, recv_sems, t, k, me, pid, peer).start()
        for k in range(1, N_DEV):
            peer, pid = _peer(k)
            for t in range(n):
                _piece_copy(modes[t], src_refs[t], land_refs[t], send_sems, recv_sems, t, k, pid, me, peer).wait_recv()
        for k in range(1, N_DEV):
            peer, pid = _peer(k)
            for t in range(n):
                _piece_copy(modes[t], src_refs[t], land_refs[t], send_sems, recv_sems, t, k, me, pid, peer).wait_send()
        for cp in local:
            cp.wait()

    return pl.kernel(
        body, out_type=[_sds(s, a.dtype) for s, a in zip(land_shapes, srcs)],
        mesh=plsc.ScalarSubcoreMesh(axis_name="sequencer", num_cores=1),
        scratch_types=[pltpu.SemaphoreType.DMA((7 * n,)), pltpu.SemaphoreType.DMA((7 * n,)), pltpu.SemaphoreType.DMA((n,))],
        compiler_params=pltpu.CompilerParams(collective_id=collective_id), name=name)(*srcs, *extra)


def _sequencer_gather(srcs, name, collective_id, after=None):
    n = len(srcs)
    extra = [] if after is None else [after]

    def body(*refs):
        src_refs, land_refs = refs[:n], refs[n + len(extra):2 * n + len(extra)]
        send_sems, recv_sems, local_sems = refs[2 * n + len(extra):]
        x, y, c = lax.axis_index("x"), lax.axis_index("y"), lax.axis_index("c")
        sibling = (x, y, 1 - c)
        chips = [(1 - x, y), (x, 1 - y), (1 - x, 1 - y)]
        index = lambda px, py, pc: 4 * px + 2 * py + pc
        barrier = pltpu.get_barrier_semaphore()
        for peer in [sibling] + [(*chip, c) for chip in chips]:
            pl.semaphore_signal(barrier, inc=1, device_id=peer, device_id_type=MESH)
        pl.semaphore_wait(barrier, 4)

        def copy(t, k, slot, to, src=None):
            return pltpu.make_async_remote_copy(
                src_ref=land_refs[t].at[slot] if src is None else src, dst_ref=land_refs[t].at[slot],
                send_sem=send_sems.at[7 * t + k], recv_sem=recv_sems.at[7 * t + k], device_id=to, device_id_type=MESH)

        me = index(x, y, c)
        local = []
        for t in range(n):
            cp = pltpu.make_async_copy(src_refs[t], land_refs[t].at[me], local_sems.at[t])
            cp.start()
            local.append(cp)
        sends = []
        for t in range(n):
            sends.append(copy(t, 0, me, sibling, src=src_refs[t]))
            sends += [copy(t, 1 + j, me, (*chip, c), src=src_refs[t]) for j, chip in enumerate(chips)]
        for cp in sends:
            cp.start()
        for j, chip in enumerate(chips):
            for t in range(n):
                copy(t, 1 + j, index(*chip, c), sibling, src=src_refs[t]).wait_recv()
                passed = copy(t, 4 + j, index(*chip, c), sibling)
                passed.start()
                sends.append(passed)
        for t in range(n):
            copy(t, 0, index(x, y, 1 - c), sibling, src=src_refs[t]).wait_recv()
        for j, chip in enumerate(chips):
            for t in range(n):
                copy(t, 4 + j, index(*chip, 1 - c), sibling, src=src_refs[t]).wait_recv()
        for cp in sends:
            cp.wait_send()
        for cp in local:
            cp.wait()

    return pl.kernel(
        body, out_type=[_sds((N_DEV,) + a.shape, a.dtype) for a in srcs],
        mesh=plsc.ScalarSubcoreMesh(axis_name="sequencer", num_cores=1),
        scratch_types=[pltpu.SemaphoreType.DMA((7 * n,)), pltpu.SemaphoreType.DMA((7 * n,)), pltpu.SemaphoreType.DMA((n,))],
        compiler_params=pltpu.CompilerParams(collective_id=collective_id), name=name)(*srcs, *extra)


def _adamw(w, g, m, v):
    m = ADAM_B1 * m + (1.0 - ADAM_B1) * g
    v = ADAM_B2 * v + (1.0 - ADAM_B2) * (g * g)
    m_hat = m / (1.0 - ADAM_B1 ** ADAM_STEP)
    v_hat = v / (1.0 - ADAM_B2 ** ADAM_STEP)
    delta = -ADAM_LR * (m_hat / (jnp.sqrt(v_hat) + ADAM_EPS) + ADAM_WD * w)
    return delta, m, v


def _adam_big(w, parts, m, v, name, after=None):
    L, R, C = w.shape
    P = parts[0].shape[0]
    tr = _tile(R, (256, 128, 176, 64, 32, 16))
    nr = R // tr

    def body(w_ref, *refs):
        p_refs, (m_ref, v_ref, g_ref, d_ref, mo_ref, vo_ref) = refs[:L], refs[L:]
        for l in range(L):
            @pl.when(pl.program_id(0) == l)
            def _(p_ref=p_refs[l]):
                g = p_ref[0].astype(F32)
                for s in range(1, P):
                    g = g + p_ref[s].astype(F32)
                g_ref[...] = g
                d_ref[...], mo_ref[...], vo_ref[...] = _adamw(w_ref[...], g, m_ref[...], v_ref[...])

    row = pl.BlockSpec((None, tr, C), lambda l, i: (l, i, 0))
    park = lambda l_of: (lambda l, i: (0, jnp.where(l == l_of, i, 0 if l_of else nr - 1), 0))
    return _call(
        after, body, grid=(L, nr),
        in_specs=[row] + [pl.BlockSpec((P, tr, C), park(l)) for l in range(L)] + [row, row],
        out_specs=[row] * 4, out_shape=[_sds((L, R, C), F32)] * 4,
        compiler_params=_params(("arbitrary", "arbitrary")), name=name)(w, *parts, m, v)


SMALL = (("a_lower_bound", (2, 128), (2, D)), ("ln_gain", (6, 128), (6, D)), ("ln_bias", (6, 128), (6, D)),
         ("a_norm_gain", (1, 128), (1, 128)), ("kv_b", (1, 512), (1, 512)), ("b_b_q", (1, D), (1, D)),
         ("b_sinks", (1, ATT_QH), (1, 128)), ("b_b_out", (1, D), (1, D)), ("ple_b_gate", (2, D), (2, D)))


def _adam_small(parts, w, m, v, losses, after=None):
    k = len(SMALL)

    def body(*refs):
        p_refs, w_refs, m_refs, v_refs = refs[:k], refs[k:2 * k], refs[2 * k:3 * k], refs[3 * k:4 * k]
        loss_ref, outs, total_ref = refs[4 * k], refs[4 * k + 1:-1], refs[-1]
        total = loss_ref[0]
        for s in range(1, N_DEV):
            total = total + loss_ref[s]
        total_ref[...] = total
        me = _my_index()
        for i, (_, wshape, pshape) in enumerate(SMALL):
            cols = wshape[1]
            lanes = slice(None) if cols == pshape[1] else (
                pl.ds(0, cols) if cols < 128 else pl.ds(pl.multiple_of(me * cols, cols), cols))
            g = p_refs[i][0, :, lanes]
            for s in range(1, N_DEV):
                g = g + p_refs[i][s, :, lanes]
            g_ref, d_ref, mo_ref, vo_ref = outs[4 * i:4 * i + 4]
            g_ref[...] = g
            d_ref[...], mo_ref[...], vo_ref[...] = _adamw(w_refs[i][...], g, m_refs[i][...], v_refs[i][...])

    full = lambda shape: pl.BlockSpec(shape, lambda: (0,) * len(shape))
    names = [n for n, _, _ in SMALL]
    res = _call(
        after, body,
        in_specs=[full((N_DEV,) + ps) for _, _, ps in SMALL] + [full(ws) for _, ws, _ in SMALL] * 3
        + [full((N_DEV, 1, 128))],
        out_specs=[full(ws) for _, ws, _ in SMALL for _ in range(4)] + [full((1, 128))],
        out_shape=[_sds(ws, F32) for _, ws, _ in SMALL for _ in range(4)] + [_sds((1, 128), F32)], name="adam_small")(
            *[parts[n] for n in names], *[a[n].reshape(ws) for a in (w, m, v) for n, ws, _ in SMALL], losses)
    return {n: [r.reshape(w[n].shape) for r in res[4 * i:4 * i + 4]] for i, n in enumerate(names)}, res[-1][0, 0]


WEIGHTS = ("a_w_in", "a_lower_bound", "a_norm_gain", "a_w_out", "kv_w", "kv_b", "b_w_q", "b_b_q", "b_sinks", "b_w_out",
           "b_b_out", "ffn_w_gate_up", "ffn_w_down", "ple_w_up", "ple_w_gate", "ple_b_gate", "ln_gain", "ln_bias")


GATHER_GROUPS = (("a_w_in",), ("a_w_out", "gu0"), ("dn0", "pu0", "pg0"), ("kv_w", "b_w_q", "b_w_out"),
                 ("gu1", "dn1", "pu1", "pg1"))
KERNEL_LAYOUT = {
    "a_w_in": lambda a: a,
    "a_w_out": lambda a: a.reshape(D, D),
    "kv_w": lambda a: a.reshape(D, 2 * ATT_KVH * ATT_HD),
    "b_w_q": lambda a: a.reshape(D, D),
    "b_w_out": lambda a: a.reshape(D, D),
    "gu": lambda a: a.reshape(2, 4, FFN_B, D),
    "dn": lambda a: a.reshape(4, FFN_B, D),
    "pu": lambda a: a,
    "pg": lambda a: a.reshape(D, D),
}
_row_blocks = lambda a: a.reshape(N_DEV, -1, a.shape[-1])
OWNER_BLOCKS = {
    "a_w_in": lambda g: g,
    "a_w_out": _row_blocks,
    "kv_w": _row_blocks,
    "b_w_q": _row_blocks,
    "b_w_out": _row_blocks,
    "gu": lambda g: g,
    "dn": lambda g: _row_blocks(g.reshape(FFN_H, D)),
    "pu": lambda g: g.reshape(PLE_DIM, N_DEV, 128).transpose(1, 0, 2),
    "pg": _row_blocks,
}
ADAM_PARTS = (("kv_w", ("kv_w",)), ("b_w_q", ("b_w_q",)), ("b_w_out", ("b_w_out",)), ("ffn_w_gate_up", ("gu0", "gu1")),
              ("ffn_w_down", ("dn0", "dn1")), ("ple_w_up", ("pu0", "pu1")), ("ple_w_gate", ("pg0", "pg1")),
              ("a_w_out", ("a_w_out",)), ("a_w_in", ("a_w_in",)))


def kernel(x, p, a_w_in, a_lower_bound, a_norm_gain, a_w_out, kv_w, kv_b, b_w_q, b_b_q, b_sinks, b_w_out, b_b_out, ffn_w_gate_up, ffn_w_down, ple_w_up, ple_w_gate, ple_b_gate, ln_gain, ln_bias, loss_target, m_a_w_in, m_a_lower_bound, m_a_norm_gain, m_a_w_out, m_kv_w, m_kv_b, m_b_w_q, m_b_b_q, m_b_sinks, m_b_w_out, m_b_b_out, m_ffn_w_gate_up, m_ffn_w_down, m_ple_w_up, m_ple_w_gate, m_ple_b_gate, m_ln_gain, m_ln_bias, v_a_w_in, v_a_lower_bound, v_a_norm_gain, v_a_w_out, v_kv_w, v_kv_b, v_b_w_q, v_b_b_q, v_b_sinks, v_b_w_out, v_b_b_out, v_ffn_w_gate_up, v_ffn_w_down, v_ple_w_up, v_ple_w_gate, v_ple_b_gate, v_ln_gain, v_ln_bias):
    given = dict(locals())
    w = {n: given[n] for n in WEIGHTS}
    m = {n: given["m_" + n] for n in WEIGHTS}
    v = {n: given["v_" + n] for n in WEIGHTS}
    shards = {"a_w_in": a_w_in[0], "a_w_out": a_w_out[0], "kv_w": kv_w, "b_w_q": b_w_q[0], "b_w_out": b_w_out[0]}
    for l in range(2):
        shards.update({f"gu{l}": ffn_w_gate_up[l].T, f"dn{l}": ffn_w_down[l], f"pu{l}": ple_w_up[l], f"pg{l}": ple_w_gate[l]})
    sharded_small = [a_lower_bound, ln_gain.reshape(6, 128), ln_bias.reshape(6, 128)]
    gathered = {}
    for gi, g in enumerate(GATHER_GROUPS):
        lands = _sequencer_gather([shards[n].astype(CDT) for n in g] + (sharded_small if gi == 0 else []),
                                  f"gather{gi}", gi)
        for n, a in zip(g, lands):
            gathered[n] = KERNEL_LAYOUT[n.rstrip("01")](a)
        if gi == 0:
            alb, lng, lnb = [a.transpose(1, 0, 2).reshape(a.shape[1], D) for a in lands[len(g):]]

    getw = gathered.__getitem__

    sm = {"a_lower_bound": alb, "ln_gain": lng.reshape(2, 3, D), "ln_bias": lnb.reshape(2, 3, D),
          "a_norm_gain": a_norm_gain, "kv_b": kv_b, "b_b_q": b_b_q[0], "b_sinks": b_sinks, "b_b_out": b_b_out,
          "ple_b_gate": ple_b_gate}

    scatters, small_parts = [], {}

    def emit(grads, small=None):
        names = list(grads)
        blocks = [OWNER_BLOCKS[n.rstrip("01")](grads[n]) for n in names]
        partials = [] if small is None else [small[n].reshape(ps) for n, _, ps in SMALL] + [small["loss"]]
        lands = _sequencer_exchange(blocks + partials, ["scatter"] * len(blocks) + ["gather"] * len(partials),
                                    f"scatter{len(scatters)}", len(GATHER_GROUPS) + len(scatters))
        scatters.append(dict(zip(names, lands)))
        small_parts.update(zip([n for n, _, _ in SMALL] + ["loss"], lands[len(blocks):]))
        return blocks + (list(scatters[0].values()) if small is not None else [])

    loss, grad_x, gs = _local_step(x[0], p[:, 0], loss_target[0], getw, sm, emit)

    out, parts, last = {}, {}, [grad_x]
    for landed in scatters:
        parts.update(landed)
        for n, keys in ADAM_PARTS:
            if n in out or not all(key in parts for key in keys):
                continue
            lrc = (1,) * (3 - w[n].ndim) + w[n].shape
            shard = (lambda a: a.reshape(lrc).swapaxes(1, 2)) if n == "ffn_w_gate_up" else (lambda a: a.reshape(lrc))
            res = _adam_big(shard(w[n]), [parts[key] for key in keys], shard(m[n]), shard(v[n]), "adam_" + n, after=last)
            out[n] = [(r.swapaxes(1, 2) if n == "ffn_w_gate_up" else r).reshape(w[n].shape) for r in res]
            last = [res[3]]
    small_out, loss = _adam_small(small_parts, w, m, v, small_parts["loss"], after=last)
    out.update(small_out)
    res = [loss, grad_x[None]]
    for i in range(4):
        res += [out[n][i] for n in WEIGHTS]
    return tuple(res)
```

```python
import jax
import jax.numpy as jnp
from jax import lax
from jax.experimental import pallas as pl
from jax.experimental.pallas import tpu as pltpu
from jax.experimental.pallas import tpu_sc as plsc

F32 = jnp.float32
CDT = jnp.bfloat16

N_DEV = 8
D = 1024
HG_H, HG_DK, HG_CH = 8, 128, 64
HG_HPB = 4
HG_CPB = 8
ATT_HD, ATT_QH, ATT_KVH, ATT_G, WINDOW = 64, 16, 4, 4, 128
FFN_H = 2816
FFN_B = FFN_H // 4
PLE_DIM = 256
ALPHA = (2.0 * 2) ** 0.25
LN_EPS = 1e-5
RMS_EPS = 1e-6
ADAM_LR, ADAM_B1, ADAM_B2, ADAM_EPS, ADAM_WD, ADAM_STEP = 0.001, 0.9, 0.999, 1e-08, 0.01, 10
ROW_TILES = (512, 256, 128, 64)
VMEM_LIMIT = 48 * 1024 * 1024
NEG = -1e30

MESH = pl.DeviceIdType.MESH


def _tile(n, cands=ROW_TILES):
    for t in cands:
        if n % t == 0:
            return t
    return n


def _sds(shape, dtype):
    return jax.ShapeDtypeStruct(tuple(shape), dtype)


def _params(sem):
    return pltpu.CompilerParams(dimension_semantics=sem, vmem_limit_bytes=VMEM_LIMIT)


def _dot(a, b):
    return jnp.dot(a.astype(CDT), b.astype(CDT), preferred_element_type=F32)


def _dot_nt(a, b):
    return lax.dot_general(a.astype(CDT), b.astype(CDT), (((1,), (1,)), ((), ())), preferred_element_type=F32)


def _dot_tn(a, b):
    return lax.dot_general(a.astype(CDT), b.astype(CDT), (((0,), (0,)), ((), ())), preferred_element_type=F32)


def _sigmoid(x):
    return jax.nn.sigmoid(x)


def _ln_fwd(z, g, b):
    mu = jnp.mean(z, axis=-1, keepdims=True)
    zc = z - mu
    var = jnp.mean(zc * zc, axis=-1, keepdims=True)
    return zc * lax.rsqrt(var + LN_EPS) * g + b


def _ln_bwd(z, g, dy):
    mu = jnp.mean(z, axis=-1, keepdims=True)
    zc = z - mu
    var = jnp.mean(zc * zc, axis=-1, keepdims=True)
    rstd = lax.rsqrt(var + LN_EPS)
    xhat = zc * rstd
    dxh = dy * g
    dz = rstd * (dxh - jnp.mean(dxh, axis=-1, keepdims=True) - xhat * jnp.mean(dxh * xhat, axis=-1, keepdims=True))
    return dz, xhat


def _colsum(x):
    return jnp.sum(x, axis=0, keepdims=True)


def _acc(ref, val, first):
    @pl.when(first)
    def _():
        ref[...] = val

    @pl.when(jnp.logical_not(first))
    def _():
        ref[...] += val


def _zero_at(ref, first):
    @pl.when(first)
    def _():
        ref[...] = jnp.zeros_like(ref)


def _call(after, body, **kw):
    after = [] if after is None else list(after)
    specs = list(kw["in_specs"])
    kw["in_specs"] = [pl.BlockSpec(memory_space=pl.ANY)] * len(after) + specs

    def ordered_body(*refs):
        body(*refs[len(after):])

    call = pl.pallas_call(ordered_body, **kw)

    def pinned(*args):
        args = [a if s.memory_space is not None else pltpu.with_memory_space_constraint(a, pltpu.HBM)
                for a, s in zip(args, specs)]
        return call(*after, *args)

    return pinned


def _mm_nn(a, b3, out_shape, oblock, omap, out_dtype, bias3=None, name="mm_nn"):
    M, K = a.shape
    G, _, Nb = b3.shape
    tm = _tile(M)

    def body(a_ref, b_ref, *rest):
        o_ref = rest[-1]
        acc = _dot(a_ref[...], b_ref[...])
        if bias3 is not None:
            acc = acc + rest[0][...]
        o_ref[...] = acc.astype(o_ref.dtype)

    in_specs = [pl.BlockSpec((tm, K), lambda g, i: (i, 0)), pl.BlockSpec((None, K, Nb), lambda g, i: (g, 0, 0))]
    args = [a, b3]
    if bias3 is not None:
        in_specs.append(pl.BlockSpec((None, 1, Nb), lambda g, i: (g, 0, 0)))
        args.append(bias3)
    return _call(
        None, body, grid=(G, M // tm), in_specs=in_specs, out_specs=pl.BlockSpec(oblock, omap),
        out_shape=_sds(out_shape, out_dtype), compiler_params=_params(("arbitrary", "arbitrary")), name=name)(*args)


def _mm_tn(a3, b3, G, amap, bmap, ablock, bblock, out_shape, oblock, omap, name="mm_tn"):
    S = a3.shape[1]

    def body(a_ref, b_ref, o_ref):
        o_ref[...] = _dot_tn(a_ref[...], b_ref[...]).astype(o_ref.dtype)

    return _call(
        None, body, grid=(G, 1),
        in_specs=[pl.BlockSpec(ablock(S), amap), pl.BlockSpec(bblock(S), bmap)],
        out_specs=pl.BlockSpec(oblock, omap), out_shape=_sds(out_shape, CDT),
        compiler_params=_params(("arbitrary", "arbitrary")), name=name)(a3, b3)


def _wgrad(a3, b3, name):
    Ga, S, M = a3.shape
    Gb, _, N = b3.shape
    G = max(Ga, Gb)
    return _mm_tn(
        a3, b3, G,
        (lambda g, k: (g, k, 0)) if Ga > 1 else (lambda g, k: (0, k, 0)),
        (lambda g, k: (g, k, 0)) if Gb > 1 else (lambda g, k: (0, k, 0)),
        lambda tk: (None, tk, M), lambda tk: (None, tk, N),
        (G, M, N), (None, M, N), lambda g, k: (g, 0, 0), name=name)


def _mixout_ln(u3, w3, bias, xin, gain, beta, name):
    G, S, Kb = u3.shape
    tm = _tile(S)

    def body(u_ref, w_ref, b_ref, x_ref, g_ref, be_ref, z_ref, xo_ref, xob_ref):
        h = b_ref[...] + _dot(u_ref[0], w_ref[0])
        for g in range(1, G):
            h = h + _dot(u_ref[g], w_ref[g])
        z = ALPHA * x_ref[...] + h
        z_ref[...] = z
        y = _ln_fwd(z, g_ref[...], be_ref[...])
        xo_ref[...] = y
        xob_ref[...] = y.astype(CDT)

    row = pl.BlockSpec((tm, D), lambda i: (i, 0))
    vec = pl.BlockSpec((1, D), lambda i: (0, 0))
    return _call(
        None, body, grid=(S // tm,),
        in_specs=[pl.BlockSpec((G, tm, Kb), lambda i: (0, i, 0)), pl.BlockSpec((G, Kb, D), lambda i: (0, 0, 0)),
                  vec, row, vec, vec],
        out_specs=[row, row, row], out_shape=[_sds((S, D), F32), _sds((S, D), F32), _sds((S, D), CDT)],
        compiler_params=_params(("arbitrary",)), name=name)(u3, w3, bias, xin, gain, beta)


def _ffn_fwd(xin, xin_b, wgu, wdn, gain, beta, name):
    S = xin.shape[0]
    tm = _tile(S)

    def hidden(xb_ref, wgu_ref, gu_ref, hid_ref):
        xb = xb_ref[...]
        gate = _dot_nt(xb, wgu_ref[0])
        up = _dot_nt(xb, wgu_ref[1])
        gu_ref[0] = gate.astype(CDT)
        gu_ref[1] = up.astype(CDT)
        hid_ref[...] = (gate * _sigmoid(gate) * up).astype(CDT)

    gu, hid = _call(
        None, hidden, grid=(4, S // tm),
        in_specs=[pl.BlockSpec((tm, D), lambda j, i: (i, 0)), pl.BlockSpec((2, None, FFN_B, D), lambda j, i: (0, j, 0, 0))],
        out_specs=[pl.BlockSpec((2, None, tm, FFN_B), lambda j, i: (0, j, i, 0)),
                   pl.BlockSpec((None, tm, FFN_B), lambda j, i: (j, i, 0))],
        out_shape=[_sds((2, 4, S, FFN_B), CDT), _sds((4, S, FFN_B), CDT)],
        compiler_params=_params(("arbitrary", "arbitrary")), name=name + "_hidden")(xin_b, wgu)

    def down(x_ref, hid_ref, wdn_ref, g_ref, be_ref, z_ref, xo_ref, xob_ref):
        z = ALPHA * x_ref[...]
        for j in range(4):
            z = z + _dot(hid_ref[j], wdn_ref[j])
        z_ref[...] = z
        y = _ln_fwd(z, g_ref[...], be_ref[...])
        xo_ref[...] = y
        xob_ref[...] = y.astype(CDT)

    row = pl.BlockSpec((tm, D), lambda i: (i, 0))
    vec = pl.BlockSpec((1, D), lambda i: (0, 0))
    z, xo, xob = _call(
        None, down, grid=(S // tm,),
        in_specs=[row, pl.BlockSpec((4, tm, FFN_B), lambda i: (0, i, 0)), pl.BlockSpec((4, FFN_B, D), lambda i: (0, 0, 0)),
                  vec, vec],
        out_specs=[row, row, row], out_shape=[_sds((S, D), F32), _sds((S, D), F32), _sds((S, D), CDT)],
        compiler_params=_params(("arbitrary",)), name=name + "_down")(xin, hid, wdn, gain, beta)
    return gu, hid, z, xo, xob


def _ple_fwd(xin, xin_b, p_b, wpg, bgate, wpu, gain, beta, name):
    S = xin.shape[0]
    tm = _tile(S)

    def body(x_ref, xb_ref, p_ref, wpg_ref, bg_ref, wpu_ref, g_ref, be_ref, sg_ref, up_ref, z_ref, xo_ref, xob_ref):
        sg = _sigmoid(_dot(xb_ref[...], wpg_ref[...]) + bg_ref[...])
        pb = p_ref[...]
        up = jnp.concatenate([_dot(pb, wpu_ref[j]) for j in range(N_DEV)], axis=-1)
        sg_ref[...] = sg.astype(CDT)
        up_ref[...] = up.astype(CDT)
        z = ALPHA * x_ref[...] + sg * up
        z_ref[...] = z
        y = _ln_fwd(z, g_ref[...], be_ref[...])
        xo_ref[...] = y
        xob_ref[...] = y.astype(CDT)

    row = pl.BlockSpec((tm, D), lambda i: (i, 0))
    vec = pl.BlockSpec((1, D), lambda i: (0, 0))
    return _call(
        None, body, grid=(S // tm,),
        in_specs=[row, row, pl.BlockSpec((tm, PLE_DIM), lambda i: (i, 0)), pl.BlockSpec((D, D), lambda i: (0, 0)), vec,
                  pl.BlockSpec((N_DEV, PLE_DIM, D // N_DEV), lambda i: (0, 0, 0)), vec, vec],
        out_specs=[row] * 5,
        out_shape=[_sds((S, D), CDT)] * 2 + [_sds((S, D), F32)] * 2 + [_sds((S, D), CDT)],
        compiler_params=_params(("arbitrary",)), name=name)(xin, xin_b, p_b, wpg, bgate, wpu, gain, beta)


def _loss_fwd_bwd(y, target):
    S = y.shape[0]
    tm = _tile(S)

    def body(y_ref, t_ref, l_ref, dy_ref):
        e = y_ref[...] - t_ref[...]
        dy_ref[...] = e * (1.0 / D)
        part = 0.5 * jnp.sum(jnp.sum(e * e, axis=-1, keepdims=True) * (1.0 / D), axis=0, keepdims=True)
        _acc(l_ref, jnp.broadcast_to(part, l_ref.shape), pl.program_id(0) == 0)

    row = pl.BlockSpec((tm, D), lambda i: (i, 0))
    return _call(
        None, body, grid=(S // tm,), in_specs=[row, row],
        out_specs=[pl.BlockSpec((1, 128), lambda i: (0, 0)), row],
        out_shape=[_sds((1, 128), F32), _sds((S, D), F32)],
        compiler_params=_params(("arbitrary",)), name="loss")(y, target)


def _ple_bwd(dy, z, sg, up, gain, wpg, name, after=None):
    S = dy.shape[0]
    tm = _tile(S)

    def body(dy_ref, z_ref, sg_ref, up_ref, g_ref, wpg_ref, dx_ref, dgl_ref, dup_ref, dgain_ref, dbeta_ref, dbg_ref):
        first = pl.program_id(0) == 0
        dy_ = dy_ref[...]
        dz, xhat = _ln_bwd(z_ref[...], g_ref[...], dy_)
        sg_ = sg_ref[...].astype(F32)
        dgl = dz * up_ref[...].astype(F32) * sg_ * (1.0 - sg_)
        dgl_ref[...] = dgl.astype(CDT)
        dup_ref[...] = (dz * sg_).astype(CDT)
        dx_ref[...] = ALPHA * dz + _dot_nt(dgl, wpg_ref[...])
        _acc(dgain_ref, _colsum(dy_ * xhat), first)
        _acc(dbeta_ref, _colsum(dy_), first)
        _acc(dbg_ref, _colsum(dgl), first)

    row = pl.BlockSpec((tm, D), lambda i: (i, 0))
    vec = pl.BlockSpec((1, D), lambda i: (0, 0))
    return _call(
        after, body, grid=(S // tm,), in_specs=[row, row, row, row, vec, pl.BlockSpec((D, D), lambda i: (0, 0))],
        out_specs=[row, row, row, vec, vec, vec],
        out_shape=[_sds((S, D), F32), _sds((S, D), CDT), _sds((S, D), CDT)] + [_sds((1, D), F32)] * 3,
        compiler_params=_params(("arbitrary",)), name=name)(dy, z, sg, up, gain, wpg)


def _ffn_bwd_hidden(dy, z, gu, wdn, gain, name, after=None):
    S = dy.shape[0]
    tm = _tile(S)

    def hidden(dy_ref, z_ref, gu_ref, wdn_ref, g_ref, dz_ref, dzb_ref, dgu_ref, dgain_ref, dbeta_ref):
        i, j = pl.program_id(0), pl.program_id(1)

        @pl.when(j == 0)
        def _():
            dy_ = dy_ref[...]
            dz, xhat = _ln_bwd(z_ref[...], g_ref[...], dy_)
            dz_ref[...] = dz
            dzb_ref[...] = dz.astype(CDT)
            _acc(dgain_ref, _colsum(dy_ * xhat), i == 0)
            _acc(dbeta_ref, _colsum(dy_), i == 0)

        dhid = _dot_nt(dzb_ref[...], wdn_ref[...])
        gate, up = gu_ref[0].astype(F32), gu_ref[1].astype(F32)
        sg = _sigmoid(gate)
        dgu_ref[0] = (dhid * up * (sg * (1.0 + gate * (1.0 - sg)))).astype(CDT)
        dgu_ref[1] = (dhid * (gate * sg)).astype(CDT)

    row = pl.BlockSpec((tm, D), lambda i, j: (i, 0))
    vec = pl.BlockSpec((1, D), lambda i, j: (0, 0))
    return _call(
        after, hidden, grid=(S // tm, 4),
        in_specs=[row, row, pl.BlockSpec((2, None, tm, FFN_B), lambda i, j: (0, j, i, 0)),
                  pl.BlockSpec((None, FFN_B, D), lambda i, j: (j, 0, 0)), vec],
        out_specs=[row, row, pl.BlockSpec((2, None, tm, FFN_B), lambda i, j: (0, j, i, 0)), vec, vec],
        out_shape=[_sds((S, D), F32), _sds((S, D), CDT), _sds((2, 4, S, FFN_B), CDT), _sds((1, D), F32),
                   _sds((1, D), F32)],
        compiler_params=_params(("arbitrary", "arbitrary")), name=name + "_hidden")(dy, z, gu, wdn, gain)


def _ffn_bwd_input(dz, dgu, wgu, name, after=None):
    S = dz.shape[0]
    tm = _tile(S)

    def to_input(dz_ref, dgu_ref, wgu_ref, dx_ref):
        acc = ALPHA * dz_ref[...]
        for g in range(2):
            for j in range(4):
                acc = acc + _dot(dgu_ref[g, j], wgu_ref[g, j])
        dx_ref[...] = acc

    rows = pl.BlockSpec((tm, D), lambda i: (i, 0))
    return _call(
        after, to_input, grid=(S // tm,),
        in_specs=[rows, pl.BlockSpec((2, 4, tm, FFN_B), lambda i: (0, 0, i, 0)),
                  pl.BlockSpec((2, 4, FFN_B, D), lambda i: (0, 0, 0, 0))],
        out_specs=rows, out_shape=_sds((S, D), F32),
        compiler_params=_params(("arbitrary",)), name=name + "_input")(dz, dgu, wgu)


def _mixout_bwd(dy, z, gain, w3, du_dtype, name, after=None):
    S = dy.shape[0]
    G, Kb, _ = w3.shape
    tm = _tile(S)

    def body(dy_ref, z_ref, g_ref, w_ref, dz_ref, dzb_ref, du_ref, dgain_ref, dbeta_ref, dbias_ref):
        first = pl.program_id(0) == 0
        dy_ = dy_ref[...]
        dz, xhat = _ln_bwd(z_ref[...], g_ref[...], dy_)
        dz_ref[...] = dz
        dzb = dz.astype(CDT)
        dzb_ref[...] = dzb
        for g in range(G):
            du_ref[g] = _dot_nt(dzb, w_ref[g]).astype(du_ref.dtype)
        _acc(dgain_ref, _colsum(dy_ * xhat), first)
        _acc(dbeta_ref, _colsum(dy_), first)
        _acc(dbias_ref, _colsum(dz), first)

    row = pl.BlockSpec((tm, D), lambda i: (i, 0))
    vec = pl.BlockSpec((1, D), lambda i: (0, 0))
    return _call(
        after, body, grid=(S // tm,), in_specs=[row, row, vec, pl.BlockSpec((G, Kb, D), lambda i: (0, 0, 0))],
        out_specs=[row, row, pl.BlockSpec((G, tm, Kb), lambda i: (0, i, 0)), vec, vec, vec],
        out_shape=[_sds((S, D), F32), _sds((S, D), CDT), _sds((G, S, Kb), du_dtype)] + [_sds((1, D), F32)] * 3,
        compiler_params=_params(("arbitrary",)), name=name)(dy, z, gain, w3)


def _half_select(low):
    r = lax.broadcasted_iota(jnp.int32, (2 * ATT_HD, ATT_HD), 0)
    c = lax.broadcasted_iota(jnp.int32, (2 * ATT_HD, ATT_HD), 1)
    return (r == c + (0 if low else ATT_HD)).astype(CDT)


def _half_place(low):
    r = lax.broadcasted_iota(jnp.int32, (ATT_HD, 2 * ATT_HD), 0)
    c = lax.broadcasted_iota(jnp.int32, (ATT_HD, 2 * ATT_HD), 1)
    return (c == r + (0 if low else ATT_HD)).astype(CDT)


def _pair_lanes(even, odd):
    return (jnp.dot(even, _half_place(True), preferred_element_type=F32)
            + jnp.dot(odd, _half_place(False), preferred_element_type=F32)).astype(CDT)


def _proj_heads(a, w, bias, heads, name):
    S, K = a.shape
    N = heads * ATT_HD
    tm = _tile(S)

    def body(a_ref, w_ref, b_ref, o_ref):
        acc = (_dot(a_ref[...], w_ref[...]) + b_ref[...]).astype(CDT)
        sel = (_half_select(True), _half_select(False))
        for h in range(heads):
            pair = acc[:, (h // 2) * 2 * ATT_HD:(h // 2 + 1) * 2 * ATT_HD]
            o_ref[h] = jnp.dot(pair, sel[h % 2], preferred_element_type=F32).astype(CDT)

    return _call(
        None, body, grid=(S // tm,),
        in_specs=[pl.BlockSpec((tm, K), lambda i: (i, 0)), pl.BlockSpec((K, N), lambda i: (0, 0)),
                  pl.BlockSpec((1, N), lambda i: (0, 0))],
        out_specs=pl.BlockSpec((heads, tm, ATT_HD), lambda i: (0, i, 0)), out_shape=_sds((heads, S, ATT_HD), CDT),
        compiler_params=_params(("arbitrary",)), name=name)(a, w, bias)


def _qkv_bwd(dz, dq, dkv4, wq, wkv, name, after=None):
    S = dz.shape[0]
    tm = _tile(S)
    HK = dkv4.shape[0]
    NK = HK * ATT_HD

    def body(dz_ref, dq_ref, dkv_ref, wq_ref, wkv_ref, dx_ref, dkvn_ref, dkvb_ref):
        first = pl.program_id(0) == 0
        dkvn = jnp.concatenate([_pair_lanes(dkv_ref[2 * i].astype(CDT), dkv_ref[2 * i + 1].astype(CDT))
                                for i in range(HK // 2)], axis=-1)
        dkvn_ref[...] = dkvn
        dx_ref[...] = ALPHA * dz_ref[...] + _dot_nt(dq_ref[...], wq_ref[...]) + _dot_nt(dkvn, wkv_ref[...])
        for h in range(HK):
            _acc(dkvb_ref.at[h], _colsum(dkv_ref[h]), first)

    row = pl.BlockSpec((tm, D), lambda i: (i, 0))
    return _call(
        after, body, grid=(S // tm,),
        in_specs=[row, row, pl.BlockSpec((HK, tm, ATT_HD), lambda i: (0, i, 0)),
                  pl.BlockSpec((D, D), lambda i: (0, 0)), pl.BlockSpec((D, NK), lambda i: (0, 0))],
        out_specs=[row, pl.BlockSpec((tm, NK), lambda i: (i, 0)), pl.BlockSpec((HK, 1, ATT_HD), lambda i: (0, 0, 0))],
        out_shape=[_sds((S, D), F32), _sds((S, NK), CDT), _sds((HK, 1, ATT_HD), F32)],
        compiler_params=_params(("arbitrary",)), name=name)(dz, dq, dkv4, wq, wkv)


def _inproj_bwd(dz, dproj, wain, name, after=None):
    S = dz.shape[0]
    tm = _tile(S)
    nb = wain.shape[-1]

    def body(dz_ref, dp_ref, w_ref, dx_ref):
        acc = ALPHA * dz_ref[...]
        for j in range(N_DEV):
            acc = acc + _dot_nt(dp_ref[j // 2, :, pl.ds((j % 2) * nb, nb)], w_ref[j])
        dx_ref[...] = acc

    row = pl.BlockSpec((tm, D), lambda i: (i, 0))
    return _call(
        after, body, grid=(S // tm,),
        in_specs=[row, pl.BlockSpec((4, tm, D), lambda i: (0, i, 0)), pl.BlockSpec((N_DEV, D, nb), lambda i: (0, 0, 0))],
        out_specs=row, out_shape=_sds((S, D), F32),
        compiler_params=_params(("arbitrary",)), name=name)(dz, dproj, wain)


def _running_sum(x, reverse=False):
    rows = x.shape[0]
    row = lax.broadcasted_iota(jnp.int32, x.shape, 0)
    step = 1
    while step < rows:
        if reverse:
            x = x + jnp.where(row < rows - step, pltpu.roll(x, rows - step, 0), 0.0)
        else:
            x = x + jnp.where(row >= step, pltpu.roll(x, step, 0), 0.0)
        step *= 2
    return x


def _hg_gates(q, f, alb_ref):
    a0, a1 = alb_ref[0:1, :], alb_ref[1:2, :]
    mx = jnp.maximum(a0, a1)
    e0, e1 = jnp.exp(a0 - mx), jnp.exp(a1 - mx)
    lb = e0 / (e0 + e1)
    sig = _sigmoid(f)
    forget = lb + (1.0 - lb) * sig
    k = (1.0 - lb) * _sigmoid(-f)
    qs = q * _sigmoid(q) * (HG_DK ** -0.5)
    return qs, k, jnp.log(forget), sig, lb, forget


def _hg_intra(qs, k, b, b_scr):
    b_scr[...] = b
    bm = b_scr[pl.ds(HG_CH // 2 - 1, 1), :]
    bl = b_scr[pl.ds(HG_CH - 1, 1), :]
    eb = jnp.exp(b)
    qb = qs * eb
    e_q = jnp.exp(b - bm)
    e_k = jnp.exp(bm - b)
    e_d = jnp.exp(bl - b)
    return qb, qs * e_q, k * e_k, k * e_d, jnp.exp(bl), eb, e_q, e_k, e_d


def _hgrn_fwd(proj, alb, ngain):
    S = proj.shape[1]
    nc = S // HG_CH
    nb = nc // HG_CPB
    rb, wb = HG_CPB * HG_CH, HG_HPB * HG_DK

    def body(pj_ref, alb_ref, ng_ref, o_ref, y_ref, st_ref, st_scr, b_scr):
        n = pl.program_id(1)

        @pl.when(n == 0)
        def _():
            st_scr[...] = jnp.zeros_like(st_scr)

        r = lax.broadcasted_iota(jnp.int32, (HG_CH, HG_CH), 0)
        c = lax.broadcasted_iota(jnp.int32, (HG_CH, HG_CH), 1)
        causal = r >= c
        for ci, j in [(ci, j) for ci in range(HG_CPB) for j in range(HG_HPB)]:
            rows, lanes = pl.ds(ci * HG_CH, HG_CH), pl.ds(j * HG_DK, HG_DK)
            q, f, v, g = pj_ref[0, rows, lanes], pj_ref[1, rows, lanes], pj_ref[2, rows, lanes], pj_ref[3, rows, lanes]
            qs, k, logf, _, _, _ = _hg_gates(q, f, alb_ref.at[:, lanes])
            b = _running_sum(logf)
            qb, qt, kt, kd, ebl, _, _, _, _ = _hg_intra(qs, k, b, b_scr.at[j, ci])
            st = st_scr[j]
            st_ref[j, ci] = st
            a = jnp.where(causal, _dot_nt(qt, kt), 0.0)
            o = _dot(a, v) + _dot_nt(qb, st)
            st_scr[j] = st * ebl + _dot_tn(v, kd)
            o_ref[rows, lanes] = o
            rinv = lax.rsqrt(jnp.mean(o * o, axis=-1, keepdims=True) + RMS_EPS)
            y_ref[rows, lanes] = (o * rinv * ng_ref[...] * (g * _sigmoid(g))).astype(CDT)

    blk = pl.BlockSpec((rb, wb), lambda h, n: (n, h))
    return _call(
        None, body, grid=(HG_H // HG_HPB, nb),
        in_specs=[pl.BlockSpec((4, rb, wb), lambda h, n: (0, n, h)), pl.BlockSpec((2, wb), lambda h, n: (0, h)),
                  pl.BlockSpec((1, HG_DK), lambda h, n: (0, 0))],
        out_specs=[blk, blk, pl.BlockSpec((HG_HPB, HG_CPB, HG_DK, HG_DK), lambda h, n: (h, n, 0, 0))],
        out_shape=[_sds((S, D), F32), _sds((S, D), CDT), _sds((HG_H, nc, HG_DK, HG_DK), F32)],
        scratch_shapes=[pltpu.VMEM((HG_HPB, HG_DK, HG_DK), F32), pltpu.VMEM((HG_HPB, HG_CPB, HG_CH, HG_DK), F32)],
        compiler_params=_params(("arbitrary", "arbitrary")), name="hgrn_fwd")(proj, alb, ngain)


def _hgrn_bwd(proj, alb, ngain, o, states, dy, after=None):
    S = proj.shape[1]
    nc = S // HG_CH
    nb = nc // HG_CPB
    rb, wb = HG_CPB * HG_CH, HG_HPB * HG_DK

    def body(pj_ref, alb_ref, ng_ref, o_ref, st_ref, dy_ref, dpj_ref, dalb_ref, dng_ref, dst_scr, b_scr):
        h, n = pl.program_id(0), pl.program_id(1)

        @pl.when(n == 0)
        def _():
            dst_scr[...] = jnp.zeros_like(dst_scr)
            dalb_ref[...] = jnp.zeros_like(dalb_ref)

        _zero_at(dng_ref, jnp.logical_and(h == 0, n == 0))
        ng = ng_ref[...]
        r = lax.broadcasted_iota(jnp.int32, (HG_CH, HG_CH), 0)
        c = lax.broadcasted_iota(jnp.int32, (HG_CH, HG_CH), 1)
        causal = r >= c
        dng = None
        for ci, j in [(ci, j) for ci in reversed(range(HG_CPB)) for j in range(HG_HPB)]:
            rows, lanes = pl.ds(ci * HG_CH, HG_CH), pl.ds(j * HG_DK, HG_DK)
            q, f, v, g = pj_ref[0, rows, lanes], pj_ref[1, rows, lanes], pj_ref[2, rows, lanes], pj_ref[3, rows, lanes]
            o_ = o_ref[rows, lanes]
            dy_ = dy_ref[rows, lanes]
            sg = _sigmoid(g)
            rinv = lax.rsqrt(jnp.mean(o_ * o_, axis=-1, keepdims=True) + RMS_EPS)
            nrm = o_ * rinv
            dr = dy_ * (g * sg)
            dg = dy_ * nrm * ng * (sg * (1.0 + g * (1.0 - sg)))
            dn = dr * ng
            do = rinv * (dn - nrm * jnp.mean(dn * nrm, axis=-1, keepdims=True))
            dng = _colsum(dr * nrm) if dng is None else dng + _colsum(dr * nrm)
            qs, k, logf, sig, lb, forget = _hg_gates(q, f, alb_ref.at[:, lanes])
            b = _running_sum(logf)
            qb, qt, kt, kd, ebl, eb, e_q, e_k, e_d = _hg_intra(qs, k, b, b_scr.at[j, ci])
            st = st_ref[j, ci]
            dstn = dst_scr[j]
            qt, kt, qb, kd = (t.astype(CDT).astype(F32) for t in (qt, kt, qb, kd))
            a = jnp.where(causal, _dot_nt(qt, kt), 0.0)
            da = jnp.where(causal, _dot_nt(do, v), 0.0)
            dv = _dot_tn(a, do) + _dot_nt(kd, dstn)
            dqb = _dot(do, st)
            dkd = _dot(v, dstn)
            dqt = _dot(da, kt)
            dkt = _dot_tn(da, qt)
            dbl = _colsum(dkd * kd) + ebl * _colsum(dstn * st)
            dst_scr[j] = dstn * ebl + _dot_tn(do, qb)
            dqs = dqt * e_q + dqb * eb
            dk = dkt * e_k + dkd * e_d
            db = dqt * qt + dqb * qb - dkt * kt - dkd * kd
            dlogf = _running_sum(db, reverse=True) + dbl
            dforget = dlogf / forget
            dsig = (1.0 - lb) * (dforget - dk)
            df = dsig * sig * (1.0 - sig)
            dlb = _colsum((dforget - dk) * (1.0 - sig))
            sq = _sigmoid(q)
            dq = dqs * (HG_DK ** -0.5) * (sq * (1.0 + q * (1.0 - sq)))
            dpj_ref[0, rows, lanes] = dq.astype(CDT)
            dpj_ref[1, rows, lanes] = df.astype(CDT)
            dpj_ref[2, rows, lanes] = dv.astype(CDT)
            dpj_ref[3, rows, lanes] = dg.astype(CDT)
            da0 = dlb * lb * (1.0 - lb)
            dalb_ref[pl.ds(0, 1), lanes] += da0
            dalb_ref[pl.ds(1, 1), lanes] -= da0
        dng_ref[...] += dng

    blk = pl.BlockSpec((rb, wb), lambda h, n: (nb - 1 - n, h))
    pj = pl.BlockSpec((4, rb, wb), lambda h, n: (0, nb - 1 - n, h))
    alb_blk = pl.BlockSpec((2, wb), lambda h, n: (0, h))
    ng_blk = pl.BlockSpec((1, HG_DK), lambda h, n: (0, 0))
    return _call(
        after, body, grid=(HG_H // HG_HPB, nb),
        in_specs=[pj, alb_blk, ng_blk, blk,
                  pl.BlockSpec((HG_HPB, HG_CPB, HG_DK, HG_DK), lambda h, n: (h, nb - 1 - n, 0, 0)), blk],
        out_specs=[pj, alb_blk, ng_blk],
        out_shape=[_sds((4, S, D), CDT), _sds((2, D), F32), _sds((1, HG_DK), F32)],
        scratch_shapes=[pltpu.VMEM((HG_HPB, HG_DK, HG_DK), F32), pltpu.VMEM((HG_HPB, HG_CPB, HG_CH, HG_DK), F32)],
        compiler_params=_params(("arbitrary", "arbitrary")), name="hgrn_bwd")(proj, alb, ngain, o, states, dy)


def _slope(h):
    return 2.0 ** (-8.0 * (h + 1) / ATT_QH)


def _attn_mask(n):
    qi = lax.broadcasted_iota(jnp.int32, (WINDOW, 2 * WINDOW), 0)
    si = lax.broadcasted_iota(jnp.int32, (WINDOW, 2 * WINDOW), 1)
    dist = qi - si + WINDOW
    valid = (dist >= 0) & (dist < WINDOW) & (n * WINDOW - WINDOW + si >= 0)
    return valid, dist.astype(F32)


def _attn_probs(qh, kh, sink, slope, valid, distf):
    s = _dot_nt(qh, kh) * (ATT_HD ** -0.5) - slope * distf
    s = jnp.where(valid, s, NEG)
    m = jnp.maximum(jnp.max(s, axis=-1, keepdims=True), sink)
    e = jnp.exp(s - m)
    es = jnp.exp(sink - m)
    inv = 1.0 / (jnp.sum(e, axis=-1, keepdims=True) + es)
    return e * inv, es * inv


def _attn_specs(S):
    nb = S // WINDOW
    cur = lambda H: pl.BlockSpec((H, WINDOW, ATT_HD), lambda n: (0, n, 0))
    prev = lambda H: pl.BlockSpec((H, WINDOW, ATT_HD), lambda n: (0, jnp.maximum(n - 1, 0), 0))
    return nb, cur, prev


def _attn_fwd(q4, kv4, sinks):
    S = q4.shape[1]
    nb, cur, prev = _attn_specs(S)

    def body(sink_ref, q_ref, kvc_ref, kvp_ref, o_ref):
        valid, distf = _attn_mask(pl.program_id(0))
        for kvh in range(ATT_KVH):
            kh = jnp.concatenate([kvp_ref[kvh], kvc_ref[kvh]], axis=0)
            vh = jnp.concatenate([kvp_ref[ATT_KVH + kvh], kvc_ref[ATT_KVH + kvh]], axis=0)
            v_low = jnp.dot(vh, _half_place(True), preferred_element_type=F32).astype(CDT)
            v_high = jnp.dot(vh, _half_place(False), preferred_element_type=F32).astype(CDT)
            for h in range(kvh * ATT_G, (kvh + 1) * ATT_G, 2):
                p_even, _ = _attn_probs(q_ref[h], kh, sink_ref[0, h], _slope(h), valid, distf)
                p_odd, _ = _attn_probs(q_ref[h + 1], kh, sink_ref[0, h + 1], _slope(h + 1), valid, distf)
                o_ref[:, pl.ds(h * ATT_HD, 2 * ATT_HD)] = (_dot(p_even, v_low) + _dot(p_odd, v_high)).astype(CDT)

    return _call(
        None, body, grid=(nb,),
        in_specs=[pl.BlockSpec(memory_space=pltpu.SMEM), cur(ATT_QH), cur(2 * ATT_KVH), prev(2 * ATT_KVH)],
        out_specs=pl.BlockSpec((WINDOW, D), lambda n: (n, 0)), out_shape=_sds((S, D), CDT),
        compiler_params=_params(("arbitrary",)), name="attn_fwd")(sinks, q4, kv4, kv4)


def _attn_bwd(q4, kv4, sinks, do):
    S = q4.shape[1]
    nb, cur, prev = _attn_specs(S)

    def body(sink_ref, q_ref, kvc_ref, kvp_ref, do_ref, dq_ref, dkv_ref, dbq_ref, dsink_ref):
        n = pl.program_id(0)
        first = n == 0

        @pl.when(first)
        def _():
            dkv_ref[...] = jnp.zeros_like(dkv_ref)
            dsink_ref[...] = jnp.zeros_like(dsink_ref)
            dbq_ref[...] = jnp.zeros_like(dbq_ref)

        valid, distf = _attn_mask(n)
        lane = lax.broadcasted_iota(jnp.int32, (1, 128), 1)
        rows_cur = pl.ds(pl.multiple_of(n * WINDOW, WINDOW), WINDOW)
        rows_prev = pl.ds(pl.multiple_of(jnp.maximum(n - 1, 0) * WINDOW, WINDOW), WINDOW)
        dsinks = jnp.zeros((1, 128), F32)
        sel = (_half_select(True), _half_select(False))
        for kvh in range(ATT_KVH):
            kh = jnp.concatenate([kvp_ref[kvh], kvc_ref[kvh]], axis=0)
            vh = jnp.concatenate([kvp_ref[ATT_KVH + kvh], kvc_ref[ATT_KVH + kvh]], axis=0)
            dk = dv = None
            dqs = []
            for h in range(kvh * ATT_G, (kvh + 1) * ATT_G):
                qh = q_ref[h]
                doh = jnp.dot(do_ref[:, pl.ds((h // 2) * 2 * ATT_HD, 2 * ATT_HD)], sel[h % 2],
                              preferred_element_type=F32).astype(CDT)
                p, ps = _attn_probs(qh, kh, sink_ref[0, h], _slope(h), valid, distf)
                dp = _dot_nt(doh, vh)
                dd = jnp.sum(p * dp, axis=-1, keepdims=True)
                ds = p * (dp - dd)
                dsinks = dsinks + jnp.where(lane == h, -jnp.sum(ps * dd, axis=0, keepdims=True), 0.0)
                dqh = _dot(ds, kh) * (ATT_HD ** -0.5)
                dqs.append(dqh.astype(CDT))
                dbq_ref[h] += _colsum(dqh)
                dkh = _dot_tn(ds, qh) * (ATT_HD ** -0.5)
                dvh = _dot_tn(p, doh)
                dk = dkh if dk is None else dk + dkh
                dv = dvh if dv is None else dv + dvh
            for i in range(ATT_G // 2):
                lanes = pl.ds((kvh * ATT_G + 2 * i) * ATT_HD, 2 * ATT_HD)
                dq_ref[:, lanes] = _pair_lanes(dqs[2 * i], dqs[2 * i + 1])
            dkv_ref[kvh, rows_prev, :] += dk[:WINDOW]
            dkv_ref[kvh, rows_cur, :] += dk[WINDOW:]
            dkv_ref[ATT_KVH + kvh, rows_prev, :] += dv[:WINDOW]
            dkv_ref[ATT_KVH + kvh, rows_cur, :] += dv[WINDOW:]
        dsink_ref[...] += dsinks

    return _call(
        None, body, grid=(nb,),
        in_specs=[pl.BlockSpec(memory_space=pltpu.SMEM), cur(ATT_QH), cur(2 * ATT_KVH), prev(2 * ATT_KVH),
                  pl.BlockSpec((WINDOW, D), lambda n: (n, 0))],
        out_specs=[pl.BlockSpec((WINDOW, D), lambda n: (n, 0)), pl.BlockSpec((2 * ATT_KVH, S, ATT_HD), lambda n: (0, 0, 0)),
                   pl.BlockSpec((ATT_QH, 1, ATT_HD), lambda n: (0, 0, 0)), pl.BlockSpec((1, 128), lambda n: (0, 0))],
        out_shape=[_sds((S, D), CDT), _sds((2 * ATT_KVH, S, ATT_HD), F32), _sds((ATT_QH, 1, ATT_HD), F32),
                   _sds((1, 128), F32)],
        compiler_params=_params(("arbitrary",)), name="attn_bwd")(sinks, q4, kv4, kv4, do)


def _local_step(x, p, target, getw, sm, emit):
    S = x.shape[0]
    vec = lambda a: a.reshape(1, -1)
    ln_g = lambda l, k: vec(sm["ln_gain"][l, k])
    ln_b = lambda l, k: vec(sm["ln_bias"][l, k])
    xb = x.astype(CDT)
    pb = p.astype(CDT)

    proj = _mm_nn(xb, getw("a_w_in"), (4, S, D), (None, _tile(S), 512), lambda g, i: (g // 2, i, g % 2), F32,
                  name="a_in")
    o_a, y_a, states = _hgrn_fwd(proj, sm["a_lower_bound"], sm["a_norm_gain"])
    zeros = jnp.zeros((1, D), F32)
    z = [[None] * 3 for _ in range(2)]
    xs = [[None] * 3 for _ in range(2)]
    xbs = [[None] * 3 for _ in range(2)]
    z[0][0], xs[0][0], xbs[0][0] = _mixout_ln(y_a[None], getw("a_w_out")[None], zeros, x, ln_g(0, 0), ln_b(0, 0),
                                              "a_out_ln")
    gu, hid, sgs, ups = [None, None], [None, None], [None, None], [None, None]

    def ffn_ple(l):
        wgu = getw(f"gu{l}")
        gu[l], hid[l], z[l][1], xs[l][1], xbs[l][1] = _ffn_fwd(
            xs[l][0], xbs[l][0], wgu, getw(f"dn{l}"), ln_g(l, 1), ln_b(l, 1), f"ffn_fwd{l}")
        sgs[l], ups[l], z[l][2], xs[l][2], xbs[l][2] = _ple_fwd(
            xs[l][1], xbs[l][1], pb[l], getw(f"pg{l}"), vec(sm["ple_b_gate"][l]), getw(f"pu{l}"), ln_g(l, 2),
            ln_b(l, 2), f"ple_fwd{l}")

    ffn_ple(0)
    x3, x3b = xs[0][2], xbs[0][2]
    w_kv, w_q, w_bo = getw("kv_w"), getw("b_w_q"), getw("b_w_out")
    kv4 = _proj_heads(x3b, w_kv, vec(sm["kv_b"]), 2 * ATT_KVH, "kv_proj")
    q4 = _proj_heads(x3b, w_q, vec(sm["b_b_q"]), ATT_QH, "q_proj")
    o_b = _attn_fwd(q4, kv4, sm["b_sinks"])
    z[1][0], xs[1][0], xbs[1][0] = _mixout_ln(o_b[None], w_bo[None], sm["b_b_out"], x3, ln_g(1, 0), ln_b(1, 0),
                                              "b_out_ln")
    ffn_ple(1)
    loss, dy = _loss_fwd_bwd(xs[1][2], target)

    gs = {}
    d_ln_g = [[None] * 3 for _ in range(2)]
    d_ln_b = [[None] * 3 for _ in range(2)]
    g_bg = [None, None]

    def ffn_ple_bwd(l, dy, after=None):
        dx2, dgl, dup, d_ln_g[l][2], d_ln_b[l][2], g_bg[l] = _ple_bwd(dy, z[l][2], sgs[l], ups[l], ln_g(l, 2),
                                                                     getw(f"pg{l}"), f"ple_bwd{l}", after=after)
        g_pg = _wgrad(xbs[l][1][None], dgl[None], f"g_ple_gate{l}")[0]
        g_pu = _wgrad(pb[l][None], dup[None], f"g_ple_up{l}")[0]
        dz2, dzb, dgu, d_ln_g[l][1], d_ln_b[l][1] = _ffn_bwd_hidden(dx2, z[l][1], gu[l], getw(f"dn{l}"), ln_g(l, 1),
                                                                   f"ffn_bwd{l}")
        g_dn = _wgrad(hid[l], dzb[None], f"g_ffn_down{l}")
        g_gu = _wgrad(dgu.reshape(8, S, FFN_B), xbs[l][0][None], f"g_ffn_gate_up{l}")
        tok = emit({f"gu{l}": g_gu, f"dn{l}": g_dn, f"pg{l}": g_pg, f"pu{l}": g_pu})
        dx1 = _ffn_bwd_input(dz2, dgu, getw(f"gu{l}"), f"ffn_bwd{l}", after=tok)
        return dx1, None

    dx1, tok = ffn_ple_bwd(1, dy)
    dz, dzb, do, d_ln_g[1][0], d_ln_b[1][0], gs["b_b_out"] = _mixout_bwd(dx1, z[1][0], ln_g(1, 0), w_bo[None], CDT,
                                                                        "b_out_bwd", after=tok)
    g_bo = _wgrad(o_b[None], dzb[None], "g_b_w_out")[0]
    dq, dkv4, dbq, dsinks = _attn_bwd(q4, kv4, sm["b_sinks"], do[0])
    gs["b_b_q"] = dbq
    gs["b_sinks"] = dsinks
    g_q = _wgrad(x3b[None], dq[None], "g_b_w_q")[0]
    dx3, dkv, gs["kv_b"] = _qkv_bwd(dz, dq, dkv4, w_q, w_kv, "qkv_bwd", after=tok)
    g_kv = _wgrad(x3b[None], dkv[None], "g_kv_w")[0]
    tok = emit({"b_w_out": g_bo, "b_w_q": g_q, "kv_w": g_kv})
    dx1, tok = ffn_ple_bwd(0, dx3, tok)
    w_ao = getw("a_w_out")
    dz, dzb, dyr, d_ln_g[0][0], d_ln_b[0][0], _ = _mixout_bwd(dx1, z[0][0], ln_g(0, 0), w_ao[None], F32, "a_out_bwd",
                                                              after=tok)
    g_ao = _wgrad(y_a[None], dzb[None], "g_a_w_out")[0]
    tok = emit({"a_w_out": g_ao})
    dproj, gs["a_lower_bound"], gs["a_norm_gain"] = _hgrn_bwd(proj, sm["a_lower_bound"], sm["a_norm_gain"], o_a, states,
                                                              dyr[0], after=tok)
    tk = lambda t: (None, t, D)
    g_ain = _mm_tn(xb[None], dproj, N_DEV, lambda g, k: (0, k, 0), lambda g, k: (g // 2, k, g % 2),
                   tk, lambda t: (None, t, 512), (N_DEV, D, 512), (None, D, 512), lambda g, k: (g, 0, 0), name="g_a_w_in")
    gs["ple_b_gate"] = jnp.concatenate(g_bg, axis=0)
    gs["ln_gain"] = jnp.stack([jnp.concatenate(r, axis=0) for r in d_ln_g])
    gs["ln_bias"] = jnp.stack([jnp.concatenate(r, axis=0) for r in d_ln_b])
    gs["loss"] = loss
    tok = emit({"a_w_in": g_ain}, small=gs)
    grad_x = _inproj_bwd(dz, dproj, getw("a_w_in"), "a_in_bwd", after=tok)
    return loss, grad_x, gs


def _peer(k):
    x, y, c = lax.axis_index("x"), lax.axis_index("y"), lax.axis_index("c")
    px = 1 - x if k & 4 else x
    py = 1 - y if k & 2 else y
    pc = 1 - c if k & 1 else c
    return (px, py, pc), 4 * px + 2 * py + pc


def _my_index():
    return 4 * lax.axis_index("x") + 2 * lax.axis_index("y") + lax.axis_index("c")


def _piece_copy(mode, src, land, send_sems, recv_sems, t, k, sender, receiver, peer):
    return pltpu.make_async_remote_copy(
        src_ref=src if mode == "gather" else src.at[receiver], dst_ref=land.at[sender],
        send_sem=send_sems.at[t * 7 + k - 1], recv_sem=recv_sems.at[t * 7 + k - 1], device_id=peer, device_id_type=MESH)


def _sequencer_exchange(srcs, modes, name, collective_id, after=None):
    n = len(srcs)
    land_shapes = [((N_DEV,) + a.shape) if mode == "gather" else a.shape for a, mode in zip(srcs, modes)]
    extra = [] if after is None else [after]

    def body(*refs):
        src_refs, land_refs = refs[:n], refs[n + len(extra):2 * n + len(extra)]
        send_sems, recv_sems, local_sems = refs[2 * n + len(extra):]
        barrier = pltpu.get_barrier_semaphore()
        for k in range(1, N_DEV):
            pl.semaphore_signal(barrier, inc=1, device_id=_peer(k)[0], device_id_type=MESH)
        pl.semaphore_wait(barrier, N_DEV - 1)
        me = _my_index()
        local = []
        for i in range(n):
            cp = pltpu.make_async_copy(src_refs[i] if modes[i] == "gather" else src_refs[i].at[me], land_refs[i].at[me],
                                       local_sems.at[i])
            cp.start()
            local.append(cp)
        for k in range(1, N_DEV):
            peer, pid = _peer(k)
            for t in range(n):
                _piece_copy(modes[t], src_refs[t], land_refs[t], send_sems, recv_sems, t, k, me, pid, peer).start()
        for k in range(1, N_DEV):
            peer, pid = _peer(k)
            for t in range(n):
                _piece_copy(modes[t], src_refs[t], land_refs[t], send_sems, recv_sems, t, k, pid, me, peer).wait_recv()
        for k in range(1, N_DEV):
            peer, pid = _peer(k)
            for t in range(n):
                _piece_copy(modes[t], src_refs[t], land_refs[t], send_sems, recv_sems, t, k, me, pid, peer).wait_send()
        for cp in local:
            cp.wait()

    return pl.kernel(
        body, out_type=[_sds(s, a.dtype) for s, a in zip(land_shapes, srcs)],
        mesh=plsc.ScalarSubcoreMesh(axis_name="sequencer", num_cores=1),
        scratch_types=[pltpu.SemaphoreType.DMA((7 * n,)), pltpu.SemaphoreType.DMA((7 * n,)), pltpu.SemaphoreType.DMA((n,))],
        compiler_params=pltpu.CompilerParams(collective_id=collective_id), name=name)(*srcs, *extra)


def _sequencer_gather(srcs, name, collective_id, after=None):
    n = len(srcs)
    extra = [] if after is None else [after]

    def body(*refs):
        src_refs, land_refs = refs[:n], refs[n + len(extra):2 * n + len(extra)]
        send_sems, recv_sems, local_sems = refs[2 * n + len(extra):]
        x, y, c = lax.axis_index("x"), lax.axis_index("y"), lax.axis_index("c")
        sibling = (x, y, 1 - c)
        chips = [(1 - x, y), (x, 1 - y), (1 - x, 1 - y)]
        index = lambda px, py, pc: 4 * px + 2 * py + pc
        barrier = pltpu.get_barrier_semaphore()
        for peer in [sibling] + [(*chip, c) for chip in chips]:
            pl.semaphore_signal(barrier, inc=1, device_id=peer, device_id_type=MESH)
        pl.semaphore_wait(barrier, 4)

        def copy(t, k, slot, to, src=None):
            return pltpu.make_async_remote_copy(
                src_ref=land_refs[t].at[slot] if src is None else src, dst_ref=land_refs[t].at[slot],
                send_sem=send_sems.at[7 * t + k], recv_sem=recv_sems.at[7 * t + k], device_id=to, device_id_type=MESH)

        me = index(x, y, c)
        local = []
        for t in range(n):
            cp = pltpu.make_async_copy(src_refs[t], land_refs[t].at[me], local_sems.at[t])
            cp.start()
            local.append(cp)
        sends = []
        for t in range(n):
            sends.append(copy(t, 0, me, sibling, src=src_refs[t]))
            sends += [copy(t, 1 + j, me, (*chip, c), src=src_refs[t]) for j, chip in enumerate(chips)]
        for cp in sends:
            cp.start()
        for j, chip in enumerate(chips):
            for t in range(n):
                copy(t, 1 + j, index(*chip, c), sibling, src=src_refs[t]).wait_recv()
                passed = copy(t, 4 + j, index(*chip, c), sibling)
                passed.start()
                sends.append(passed)
        for t in range(n):
            copy(t, 0, index(x, y, 1 - c), sibling, src=src_refs[t]).wait_recv()
        for j, chip in enumerate(chips):
            for t in range(n):
                copy(t, 4 + j, index(*chip, 1 - c), sibling, src=src_refs[t]).wait_recv()
        for cp in sends:
            cp.wait_send()
        for cp in local:
            cp.wait()

    return pl.kernel(
        body, out_type=[_sds((N_DEV,) + a.shape, a.dtype) for a in srcs],
        mesh=plsc.ScalarSubcoreMesh(axis_name="sequencer", num_cores=1),
        scratch_types=[pltpu.SemaphoreType.DMA((7 * n,)), pltpu.SemaphoreType.DMA((7 * n,)), pltpu.SemaphoreType.DMA((n,))],
        compiler_params=pltpu.CompilerParams(collective_id=collective_id), name=name)(*srcs, *extra)


def _adamw(w, g, m, v):
    m = ADAM_B1 * m + (1.0 - ADAM_B1) * g
    v = ADAM_B2 * v + (1.0 - ADAM_B2) * (g * g)
    m_hat = m / (1.0 - ADAM_B1 ** ADAM_STEP)
    v_hat = v / (1.0 - ADAM_B2 ** ADAM_STEP)
    delta = -ADAM_LR * (m_hat / (jnp.sqrt(v_hat) + ADAM_EPS) + ADAM_WD * w)
    return delta, m, v


def _adam_big(w, parts, m, v, name, after=None):
    L, R, C = w.shape
    P = parts[0].shape[0]
    tr = _tile(R, (256, 128, 176, 64, 32, 16))
    nr = R // tr

    def body(w_ref, *refs):
        p_refs, (m_ref, v_ref, g_ref, d_ref, mo_ref, vo_ref) = refs[:L], refs[L:]
        for l in range(L):
            @pl.when(pl.program_id(0) == l)
            def _(p_ref=p_refs[l]):
                g = p_ref[0].astype(F32)
                for s in range(1, P):
                    g = g + p_ref[s].astype(F32)
                g_ref[...] = g
                d_ref[...], mo_ref[...], vo_ref[...] = _adamw(w_ref[...], g, m_ref[...], v_ref[...])

    row = pl.BlockSpec((None, tr, C), lambda l, i: (l, i, 0))
    park = lambda l_of: (lambda l, i: (0, jnp.where(l == l_of, i, 0 if l_of else nr - 1), 0))
    return _call(
        after, body, grid=(L, nr),
        in_specs=[row] + [pl.BlockSpec((P, tr, C), park(l)) for l in range(L)] + [row, row],
        out_specs=[row] * 4, out_shape=[_sds((L, R, C), F32)] * 4,
        compiler_params=_params(("arbitrary", "arbitrary")), name=name)(w, *parts, m, v)


SMALL = (("a_lower_bound", (2, 128), (2, D)), ("ln_gain", (6, 128), (6, D)), ("ln_bias", (6, 128), (6, D)),
         ("a_norm_gain", (1, 128), (1, 128)), ("kv_b", (1, 512), (1, 512)), ("b_b_q", (1, D), (1, D)),
         ("b_sinks", (1, ATT_QH), (1, 128)), ("b_b_out", (1, D), (1, D)), ("ple_b_gate", (2, D), (2, D)))


def _adam_small(parts, w, m, v, losses, after=None):
    k = len(SMALL)

    def body(*refs):
        p_refs, w_refs, m_refs, v_refs = refs[:k], refs[k:2 * k], refs[2 * k:3 * k], refs[3 * k:4 * k]
        loss_ref, outs, total_ref = refs[4 * k], refs[4 * k + 1:-1], refs[-1]
        total = loss_ref[0]
        for s in range(1, N_DEV):
            total = total + loss_ref[s]
        total_ref[...] = total
        me = _my_index()
        for i, (_, wshape, pshape) in enumerate(SMALL):
            cols = wshape[1]
            lanes = slice(None) if cols == pshape[1] else (
                pl.ds(0, cols) if cols < 128 else pl.ds(pl.multiple_of(me * cols, cols), cols))
            g = p_refs[i][0, :, lanes]
            for s in range(1, N_DEV):
                g = g + p_refs[i][s, :, lanes]
            g_ref, d_ref, mo_ref, vo_ref = outs[4 * i:4 * i + 4]
            g_ref[...] = g
            d_ref[...], mo_ref[...], vo_ref[...] = _adamw(w_refs[i][...], g, m_refs[i][...], v_refs[i][...])

    full = lambda shape: pl.BlockSpec(shape, lambda: (0,) * len(shape))
    names = [n for n, _, _ in SMALL]
    res = _call(
        after, body,
        in_specs=[full((N_DEV,) + ps) for _, _, ps in SMALL] + [full(ws) for _, ws, _ in SMALL] * 3
        + [full((N_DEV, 1, 128))],
        out_specs=[full(ws) for _, ws, _ in SMALL for _ in range(4)] + [full((1, 128))],
        out_shape=[_sds(ws, F32) for _, ws, _ in SMALL for _ in range(4)] + [_sds((1, 128), F32)], name="adam_small")(
            *[parts[n] for n in names], *[a[n].reshape(ws) for a in (w, m, v) for n, ws, _ in SMALL], losses)
    return {n: [r.reshape(w[n].shape) for r in res[4 * i:4 * i + 4]] for i, n in enumerate(names)}, res[-1][0, 0]


WEIGHTS = ("a_w_in", "a_lower_bound", "a_norm_gain", "a_w_out", "kv_w", "kv_b", "b_w_q", "b_b_q", "b_sinks", "b_w_out",
           "b_b_out", "ffn_w_gate_up", "ffn_w_down", "ple_w_up", "ple_w_gate", "ple_b_gate", "ln_gain", "ln_bias")


GATHER_GROUPS = (("a_w_in",), ("a_w_out", "gu0"), ("dn0", "pu0", "pg0"), ("kv_w", "b_w_q", "b_w_out"), ("gu1",),
                 ("dn1", "pu1", "pg1"))
KERNEL_LAYOUT = {
    "a_w_in": lambda a: a,
    "a_w_out": lambda a: a.reshape(D, D),
    "kv_w": lambda a: a.reshape(D, 2 * ATT_KVH * ATT_HD),
    "b_w_q": lambda a: a.reshape(D, D),
    "b_w_out": lambda a: a.reshape(D, D),
    "gu": lambda a: a.reshape(2, 4, FFN_B, D),
    "dn": lambda a: a.reshape(4, FFN_B, D),
    "pu": lambda a: a,
    "pg": lambda a: a.reshape(D, D),
}
_row_blocks = lambda a: a.reshape(N_DEV, -1, a.shape[-1])
OWNER_BLOCKS = {
    "a_w_in": lambda g: g,
    "a_w_out": _row_blocks,
    "kv_w": _row_blocks,
    "b_w_q": _row_blocks,
    "b_w_out": _row_blocks,
    "gu": lambda g: g,
    "dn": lambda g: _row_blocks(g.reshape(FFN_H, D)),
    "pu": lambda g: g.reshape(PLE_DIM, N_DEV, 128).transpose(1, 0, 2),
    "pg": _row_blocks,
}
ADAM_PARTS = (("kv_w", ("kv_w",)), ("b_w_q", ("b_w_q",)), ("b_w_out", ("b_w_out",)), ("ffn_w_gate_up", ("gu0", "gu1")),
              ("ffn_w_down", ("dn0", "dn1")), ("ple_w_up", ("pu0", "pu1")), ("ple_w_gate", ("pg0", "pg1")),
              ("a_w_out", ("a_w_out",)), ("a_w_in", ("a_w_in",)))


def kernel(x, p, a_w_in, a_lower_bound, a_norm_gain, a_w_out, kv_w, kv_b, b_w_q, b_b_q, b_sinks, b_w_out, b_b_out, ffn_w_gate_up, ffn_w_down, ple_w_up, ple_w_gate, ple_b_gate, ln_gain, ln_bias, loss_target, m_a_w_in, m_a_lower_bound, m_a_norm_gain, m_a_w_out, m_kv_w, m_kv_b, m_b_w_q, m_b_b_q, m_b_sinks, m_b_w_out, m_b_b_out, m_ffn_w_gate_up, m_ffn_w_down, m_ple_w_up, m_ple_w_gate, m_ple_b_gate, m_ln_gain, m_ln_bias, v_a_w_in, v_a_lower_bound, v_a_norm_gain, v_a_w_out, v_kv_w, v_kv_b, v_b_w_q, v_b_b_q, v_b_sinks, v_b_w_out, v_b_b_out, v_ffn_w_gate_up, v_ffn_w_down, v_ple_w_up, v_ple_w_gate, v_ple_b_gate, v_ln_gain, v_ln_bias):
    given = dict(locals())
    w = {n: given[n] for n in WEIGHTS}
    m = {n: given["m_" + n] for n in WEIGHTS}
    v = {n: given["v_" + n] for n in WEIGHTS}
    shards = {"a_w_in": a_w_in[0], "a_w_out": a_w_out[0], "kv_w": kv_w, "b_w_q": b_w_q[0], "b_w_out": b_w_out[0]}
    for l in range(2):
        shards.update({f"gu{l}": ffn_w_gate_up[l].T, f"dn{l}": ffn_w_down[l], f"pu{l}": ple_w_up[l], f"pg{l}": ple_w_gate[l]})
    sharded_small = [a_lower_bound, ln_gain.reshape(6, 128), ln_bias.reshape(6, 128)]
    gathered = {}
    for gi, g in enumerate(GATHER_GROUPS):
        lands = _sequencer_gather([shards[n].astype(CDT) for n in g] + (sharded_small if gi == 0 else []),
                                  f"gather{gi}", gi)
        for n, a in zip(g, lands):
            gathered[n] = KERNEL_LAYOUT[n.rstrip("01")](a)
        if gi == 0:
            alb, lng, lnb = [a.transpose(1, 0, 2).reshape(a.shape[1], D) for a in lands[len(g):]]

    getw = gathered.__getitem__

    sm = {"a_lower_bound": alb, "ln_gain": lng.reshape(2, 3, D), "ln_bias": lnb.reshape(2, 3, D),
          "a_norm_gain": a_norm_gain, "kv_b": kv_b, "b_b_q": b_b_q[0], "b_sinks": b_sinks, "b_b_out": b_b_out,
          "ple_b_gate": ple_b_gate}

    scatters, small_parts = [], {}

    def emit(grads, small=None):
        names = list(grads)
        blocks = [OWNER_BLOCKS[n.rstrip("01")](grads[n]) for n in names]
        partials = [] if small is None else [small[n].reshape(ps) for n, _, ps in SMALL] + [small["loss"]]
        lands = _sequencer_exchange(blocks + partials, ["scatter"] * len(blocks) + ["gather"] * len(partials),
                                    f"scatter{len(scatters)}", len(GATHER_GROUPS) + len(scatters))
        scatters.append(dict(zip(names, lands)))
        small_parts.update(zip([n for n, _, _ in SMALL] + ["loss"], lands[len(blocks):]))
        return blocks + (list(scatters[0].values()) if small is not None else [])

    loss, grad_x, gs = _local_step(x[0], p[:, 0], loss_target[0], getw, sm, emit)

    out, parts, last = {}, {}, [grad_x]
    for landed in scatters:
        parts.update(landed)
        for n, keys in ADAM_PARTS:
            if n in out or not all(key in parts for key in keys):
                continue
            lrc = (1,) * (3 - w[n].ndim) + w[n].shape
            shard = (lambda a: a.reshape(lrc).swapaxes(1, 2)) if n == "ffn_w_gate_up" else (lambda a: a.reshape(lrc))
            res = _adam_big(shard(w[n]), [parts[key] for key in keys], shard(m[n]), shard(v[n]), "adam_" + n, after=last)
            out[n] = [(r.swapaxes(1, 2) if n == "ffn_w_gate_up" else r).reshape(w[n].shape) for r in res]
            last = [res[3]]
    small_out, loss = _adam_small(small_parts, w, m, v, small_parts["loss"], after=last)
    out.update(small_out)
    res = [loss, grad_x[None]]
    for i in range(4):
        res += [out[n][i] for n in WEIGHTS]
    return tuple(res)
```

```python
import jax
import jax.numpy as jnp
from jax import lax
from jax.experimental import pallas as pl
from jax.experimental.pallas import tpu as pltpu
from jax.experimental.pallas import tpu_sc as plsc

F32 = jnp.float32
CDT = jnp.bfloat16

N_DEV = 8
D = 1024
HG_H, HG_DK, HG_CH = 8, 128, 64
HG_HPB = 4
HG_CPB = 8
ATT_HD, ATT_QH, ATT_KVH, ATT_G, WINDOW = 64, 16, 4, 4, 128
FFN_H = 2816
FFN_B = FFN_H // 4
PLE_DIM = 256
ALPHA = (2.0 * 2) ** 0.25
LN_EPS = 1e-5
RMS_EPS = 1e-6
ADAM_LR, ADAM_B1, ADAM_B2, ADAM_EPS, ADAM_WD, ADAM_STEP = 0.001, 0.9, 0.999, 1e-08, 0.01, 10
ROW_TILES = (512, 256, 128, 64)
VMEM_LIMIT = 48 * 1024 * 1024
NEG = -1e30

MESH = pl.DeviceIdType.MESH


def _tile(n, cands=ROW_TILES):
    for t in cands:
        if n % t == 0:
            return t
    return n


def _sds(shape, dtype):
    return jax.ShapeDtypeStruct(tuple(shape), dtype)


def _params(sem):
    return pltpu.CompilerParams(dimension_semantics=sem, vmem_limit_bytes=VMEM_LIMIT)


def _dot(a, b):
    return jnp.dot(a.astype(CDT), b.astype(CDT), preferred_element_type=F32)


def _dot_nt(a, b):
    return lax.dot_general(a.astype(CDT), b.astype(CDT), (((1,), (1,)), ((), ())), preferred_element_type=F32)


def _dot_tn(a, b):
    return lax.dot_general(a.astype(CDT), b.astype(CDT), (((0,), (0,)), ((), ())), preferred_element_type=F32)


def _sigmoid(x):
    return jax.nn.sigmoid(x)


def _ln_fwd(z, g, b):
    mu = jnp.mean(z, axis=-1, keepdims=True)
    zc = z - mu
    var = jnp.mean(zc * zc, axis=-1, keepdims=True)
    return zc * lax.rsqrt(var + LN_EPS) * g + b


def _ln_bwd(z, g, dy):
    mu = jnp.mean(z, axis=-1, keepdims=True)
    zc = z - mu
    var = jnp.mean(zc * zc, axis=-1, keepdims=True)
    rstd = lax.rsqrt(var + LN_EPS)
    xhat = zc * rstd
    dxh = dy * g
    dz = rstd * (dxh - jnp.mean(dxh, axis=-1, keepdims=True) - xhat * jnp.mean(dxh * xhat, axis=-1, keepdims=True))
    return dz, xhat


def _colsum(x):
    return jnp.sum(x, axis=0, keepdims=True)


def _acc(ref, val, first):
    @pl.when(first)
    def _():
        ref[...] = val

    @pl.when(jnp.logical_not(first))
    def _():
        ref[...] += val


def _zero_at(ref, first):
    @pl.when(first)
    def _():
        ref[...] = jnp.zeros_like(ref)


def _call(after, body, **kw):
    after = [] if after is None else list(after)
    specs = list(kw["in_specs"])
    kw["in_specs"] = [pl.BlockSpec(memory_space=pl.ANY)] * len(after) + specs

    def ordered_body(*refs):
        body(*refs[len(after):])

    call = pl.pallas_call(ordered_body, **kw)

    def pinned(*args):
        args = [a if s.memory_space is not None else pltpu.with_memory_space_constraint(a, pltpu.HBM)
                for a, s in zip(args, specs)]
        return call(*after, *args)

    return pinned


def _mm_nn(a, b3, out_shape, oblock, omap, out_dtype, bias3=None, name="mm_nn"):
    M, K = a.shape
    G, _, Nb = b3.shape
    tm = _tile(M)

    def body(a_ref, b_ref, *rest):
        o_ref = rest[-1]
        acc = _dot(a_ref[...], b_ref[...])
        if bias3 is not None:
            acc = acc + rest[0][...]
        o_ref[...] = acc.astype(o_ref.dtype)

    in_specs = [pl.BlockSpec((tm, K), lambda g, i: (i, 0)), pl.BlockSpec((None, K, Nb), lambda g, i: (g, 0, 0))]
    args = [a, b3]
    if bias3 is not None:
        in_specs.append(pl.BlockSpec((None, 1, Nb), lambda g, i: (g, 0, 0)))
        args.append(bias3)
    return _call(
        None, body, grid=(G, M // tm), in_specs=in_specs, out_specs=pl.BlockSpec(oblock, omap),
        out_shape=_sds(out_shape, out_dtype), compiler_params=_params(("arbitrary", "arbitrary")), name=name)(*args)


def _mm_tn(a3, b3, G, amap, bmap, ablock, bblock, out_shape, oblock, omap, name="mm_tn"):
    S = a3.shape[1]

    def body(a_ref, b_ref, o_ref):
        o_ref[...] = _dot_tn(a_ref[...], b_ref[...]).astype(o_ref.dtype)

    return _call(
        None, body, grid=(G, 1),
        in_specs=[pl.BlockSpec(ablock(S), amap), pl.BlockSpec(bblock(S), bmap)],
        out_specs=pl.BlockSpec(oblock, omap), out_shape=_sds(out_shape, CDT),
        compiler_params=_params(("arbitrary", "arbitrary")), name=name)(a3, b3)


def _wgrad(a3, b3, name):
    Ga, S, M = a3.shape
    Gb, _, N = b3.shape
    G = max(Ga, Gb)
    return _mm_tn(
        a3, b3, G,
        (lambda g, k: (g, k, 0)) if Ga > 1 else (lambda g, k: (0, k, 0)),
        (lambda g, k: (g, k, 0)) if Gb > 1 else (lambda g, k: (0, k, 0)),
        lambda tk: (None, tk, M), lambda tk: (None, tk, N),
        (G, M, N), (None, M, N), lambda g, k: (g, 0, 0), name=name)


def _mixout_ln(u3, w3, bias, xin, gain, beta, name):
    G, S, Kb = u3.shape
    tm = _tile(S)

    def body(u_ref, w_ref, b_ref, x_ref, g_ref, be_ref, z_ref, xo_ref, xob_ref):
        h = b_ref[...] + _dot(u_ref[0], w_ref[0])
        for g in range(1, G):
            h = h + _dot(u_ref[g], w_ref[g])
        z = ALPHA * x_ref[...] + h
        z_ref[...] = z
        y = _ln_fwd(z, g_ref[...], be_ref[...])
        xo_ref[...] = y
        xob_ref[...] = y.astype(CDT)

    row = pl.BlockSpec((tm, D), lambda i: (i, 0))
    vec = pl.BlockSpec((1, D), lambda i: (0, 0))
    return _call(
        None, body, grid=(S // tm,),
        in_specs=[pl.BlockSpec((G, tm, Kb), lambda i: (0, i, 0)), pl.BlockSpec((G, Kb, D), lambda i: (0, 0, 0)),
                  vec, row, vec, vec],
        out_specs=[row, row, row], out_shape=[_sds((S, D), F32), _sds((S, D), F32), _sds((S, D), CDT)],
        compiler_params=_params(("arbitrary",)), name=name)(u3, w3, bias, xin, gain, beta)


def _ffn_fwd(xin, xin_b, wgu, wdn, gain, beta, name):
    S = xin.shape[0]
    tm = _tile(S)

    def hidden(xb_ref, wgu_ref, gu_ref, hid_ref):
        xb = xb_ref[...]
        gate = _dot_nt(xb, wgu_ref[0])
        up = _dot_nt(xb, wgu_ref[1])
        gu_ref[0] = gate.astype(CDT)
        gu_ref[1] = up.astype(CDT)
        hid_ref[...] = (gate * _sigmoid(gate) * up).astype(CDT)

    gu, hid = _call(
        None, hidden, grid=(4, S // tm),
        in_specs=[pl.BlockSpec((tm, D), lambda j, i: (i, 0)), pl.BlockSpec((2, None, FFN_B, D), lambda j, i: (0, j, 0, 0))],
        out_specs=[pl.BlockSpec((2, None, tm, FFN_B), lambda j, i: (0, j, i, 0)),
                   pl.BlockSpec((None, tm, FFN_B), lambda j, i: (j, i, 0))],
        out_shape=[_sds((2, 4, S, FFN_B), CDT), _sds((4, S, FFN_B), CDT)],
        compiler_params=_params(("arbitrary", "arbitrary")), name=name + "_hidden")(xin_b, wgu)

    def down(x_ref, hid_ref, wdn_ref, g_ref, be_ref, z_ref, xo_ref, xob_ref):
        z = ALPHA * x_ref[...]
        for j in range(4):
            z = z + _dot(hid_ref[j], wdn_ref[j])
        z_ref[...] = z
        y = _ln_fwd(z, g_ref[...], be_ref[...])
        xo_ref[...] = y
        xob_ref[...] = y.astype(CDT)

    row = pl.BlockSpec((tm, D), lambda i: (i, 0))
    vec = pl.BlockSpec((1, D), lambda i: (0, 0))
    z, xo, xob = _call(
        None, down, grid=(S // tm,),
        in_specs=[row, pl.BlockSpec((4, tm, FFN_B), lambda i: (0, i, 0)), pl.BlockSpec((4, FFN_B, D), lambda i: (0, 0, 0)),
                  vec, vec],
        out_specs=[row, row, row], out_shape=[_sds((S, D), F32), _sds((S, D), F32), _sds((S, D), CDT)],
        compiler_params=_params(("arbitrary",)), name=name + "_down")(xin, hid, wdn, gain, beta)
    return gu, hid, z, xo, xob


def _ple_fwd(xin, xin_b, p_b, wpg, bgate, wpu, gain, beta, name):
    S = xin.shape[0]
    tm = _tile(S)

    def body(x_ref, xb_ref, p_ref, wpg_ref, bg_ref, wpu_ref, g_ref, be_ref, sg_ref, up_ref, z_ref, xo_ref, xob_ref):
        sg = _sigmoid(_dot(xb_ref[...], wpg_ref[...]) + bg_ref[...])
        pb = p_ref[...]
        up = jnp.concatenate([_dot(pb, wpu_ref[j]) for j in range(N_DEV)], axis=-1)
        sg_ref[...] = sg.astype(CDT)
        up_ref[...] = up.astype(CDT)
        z = ALPHA * x_ref[...] + sg * up
        z_ref[...] = z
        y = _ln_fwd(z, g_ref[...], be_ref[...])
        xo_ref[...] = y
        xob_ref[...] = y.astype(CDT)

    row = pl.BlockSpec((tm, D), lambda i: (i, 0))
    vec = pl.BlockSpec((1, D), lambda i: (0, 0))
    return _call(
        None, body, grid=(S // tm,),
        in_specs=[row, row, pl.BlockSpec((tm, PLE_DIM), lambda i: (i, 0)), pl.BlockSpec((D, D), lambda i: (0, 0)), vec,
                  pl.BlockSpec((N_DEV, PLE_DIM, D // N_DEV), lambda i: (0, 0, 0)), vec, vec],
        out_specs=[row] * 5,
        out_shape=[_sds((S, D), CDT)] * 2 + [_sds((S, D), F32)] * 2 + [_sds((S, D), CDT)],
        compiler_params=_params(("arbitrary",)), name=name)(xin, xin_b, p_b, wpg, bgate, wpu, gain, beta)


def _loss_fwd_bwd(y, target):
    S = y.shape[0]
    tm = _tile(S)

    def body(y_ref, t_ref, l_ref, dy_ref):
        e = y_ref[...] - t_ref[...]
        dy_ref[...] = e * (1.0 / D)
        part = 0.5 * jnp.sum(jnp.sum(e * e, axis=-1, keepdims=True) * (1.0 / D), axis=0, keepdims=True)
        _acc(l_ref, jnp.broadcast_to(part, l_ref.shape), pl.program_id(0) == 0)

    row = pl.BlockSpec((tm, D), lambda i: (i, 0))
    return _call(
        None, body, grid=(S // tm,), in_specs=[row, row],
        out_specs=[pl.BlockSpec((1, 128), lambda i: (0, 0)), row],
        out_shape=[_sds((1, 128), F32), _sds((S, D), F32)],
        compiler_params=_params(("arbitrary",)), name="loss")(y, target)


def _ple_bwd(dy, z, sg, up, gain, wpg, name, after=None):
    S = dy.shape[0]
    tm = _tile(S)

    def body(dy_ref, z_ref, sg_ref, up_ref, g_ref, wpg_ref, dx_ref, dgl_ref, dup_ref, dgain_ref, dbeta_ref, dbg_ref):
        first = pl.program_id(0) == 0
        dy_ = dy_ref[...]
        dz, xhat = _ln_bwd(z_ref[...], g_ref[...], dy_)
        sg_ = sg_ref[...].astype(F32)
        dgl = dz * up_ref[...].astype(F32) * sg_ * (1.0 - sg_)
        dgl_ref[...] = dgl.astype(CDT)
        dup_ref[...] = (dz * sg_).astype(CDT)
        dx_ref[...] = ALPHA * dz + _dot_nt(dgl, wpg_ref[...])
        _acc(dgain_ref, _colsum(dy_ * xhat), first)
        _acc(dbeta_ref, _colsum(dy_), first)
        _acc(dbg_ref, _colsum(dgl), first)

    row = pl.BlockSpec((tm, D), lambda i: (i, 0))
    vec = pl.BlockSpec((1, D), lambda i: (0, 0))
    return _call(
        after, body, grid=(S // tm,), in_specs=[row, row, row, row, vec, pl.BlockSpec((D, D), lambda i: (0, 0))],
        out_specs=[row, row, row, vec, vec, vec],
        out_shape=[_sds((S, D), F32), _sds((S, D), CDT), _sds((S, D), CDT)] + [_sds((1, D), F32)] * 3,
        compiler_params=_params(("arbitrary",)), name=name)(dy, z, sg, up, gain, wpg)


def _ffn_bwd_hidden(dy, z, gu, wdn, gain, name, after=None):
    S = dy.shape[0]
    tm = _tile(S)

    def hidden(dy_ref, z_ref, gu_ref, wdn_ref, g_ref, dz_ref, dzb_ref, dgu_ref, dgain_ref, dbeta_ref):
        i, j = pl.program_id(0), pl.program_id(1)

        @pl.when(j == 0)
        def _():
            dy_ = dy_ref[...]
            dz, xhat = _ln_bwd(z_ref[...], g_ref[...], dy_)
            dz_ref[...] = dz
            dzb_ref[...] = dz.astype(CDT)
            _acc(dgain_ref, _colsum(dy_ * xhat), i == 0)
            _acc(dbeta_ref, _colsum(dy_), i == 0)

        dhid = _dot_nt(dzb_ref[...], wdn_ref[...])
        gate, up = gu_ref[0].astype(F32), gu_ref[1].astype(F32)
        sg = _sigmoid(gate)
        dgu_ref[0] = (dhid * up * (sg * (1.0 + gate * (1.0 - sg)))).astype(CDT)
        dgu_ref[1] = (dhid * (gate * sg)).astype(CDT)

    row = pl.BlockSpec((tm, D), lambda i, j: (i, 0))
    vec = pl.BlockSpec((1, D), lambda i, j: (0, 0))
    return _call(
        after, hidden, grid=(S // tm, 4),
        in_specs=[row, row, pl.BlockSpec((2, None, tm, FFN_B), lambda i, j: (0, j, i, 0)),
                  pl.BlockSpec((None, FFN_B, D), lambda i, j: (j, 0, 0)), vec],
        out_specs=[row, row, pl.BlockSpec((2, None, tm, FFN_B), lambda i, j: (0, j, i, 0)), vec, vec],
        out_shape=[_sds((S, D), F32), _sds((S, D), CDT), _sds((2, 4, S, FFN_B), CDT), _sds((1, D), F32),
                   _sds((1, D), F32)],
        compiler_params=_params(("arbitrary", "arbitrary")), name=name + "_hidden")(dy, z, gu, wdn, gain)


def _ffn_bwd_input(dz, dgu, wgu, name, after=None):
    S = dz.shape[0]
    tm = _tile(S)

    def to_input(dz_ref, dgu_ref, wgu_ref, dx_ref):
        acc = ALPHA * dz_ref[...]
        for g in range(2):
            for j in range(4):
                acc = acc + _dot(dgu_ref[g, j], wgu_ref[g, j])
        dx_ref[...] = acc

    rows = pl.BlockSpec((tm, D), lambda i: (i, 0))
    return _call(
        after, to_input, grid=(S // tm,),
        in_specs=[rows, pl.BlockSpec((2, 4, tm, FFN_B), lambda i: (0, 0, i, 0)),
                  pl.BlockSpec((2, 4, FFN_B, D), lambda i: (0, 0, 0, 0))],
        out_specs=rows, out_shape=_sds((S, D), F32),
        compiler_params=_params(("arbitrary",)), name=name + "_input")(dz, dgu, wgu)


def _mixout_bwd(dy, z, gain, w3, du_dtype, name, after=None):
    S = dy.shape[0]
    G, Kb, _ = w3.shape
    tm = _tile(S)

    def body(dy_ref, z_ref, g_ref, w_ref, dz_ref, dzb_ref, du_ref, dgain_ref, dbeta_ref, dbias_ref):
        first = pl.program_id(0) == 0
        dy_ = dy_ref[...]
        dz, xhat = _ln_bwd(z_ref[...], g_ref[...], dy_)
        dz_ref[...] = dz
        dzb = dz.astype(CDT)
        dzb_ref[...] = dzb
        for g in range(G):
            du_ref[g] = _dot_nt(dzb, w_ref[g]).astype(du_ref.dtype)
        _acc(dgain_ref, _colsum(dy_ * xhat), first)
        _acc(dbeta_ref, _colsum(dy_), first)
        _acc(dbias_ref, _colsum(dz), first)

    row = pl.BlockSpec((tm, D), lambda i: (i, 0))
    vec = pl.BlockSpec((1, D), lambda i: (0, 0))
    return _call(
        after, body, grid=(S // tm,), in_specs=[row, row, vec, pl.BlockSpec((G, Kb, D), lambda i: (0, 0, 0))],
        out_specs=[row, row, pl.BlockSpec((G, tm, Kb), lambda i: (0, i, 0)), vec, vec, vec],
        out_shape=[_sds((S, D), F32), _sds((S, D), CDT), _sds((G, S, Kb), du_dtype)] + [_sds((1, D), F32)] * 3,
        compiler_params=_params(("arbitrary",)), name=name)(dy, z, gain, w3)


def _half_select(low):
    r = lax.broadcasted_iota(jnp.int32, (2 * ATT_HD, ATT_HD), 0)
    c = lax.broadcasted_iota(jnp.int32, (2 * ATT_HD, ATT_HD), 1)
    return (r == c + (0 if low else ATT_HD)).astype(CDT)


def _half_place(low):
    r = lax.broadcasted_iota(jnp.int32, (ATT_HD, 2 * ATT_HD), 0)
    c = lax.broadcasted_iota(jnp.int32, (ATT_HD, 2 * ATT_HD), 1)
    return (c == r + (0 if low else ATT_HD)).astype(CDT)


def _pair_lanes(even, odd):
    return (jnp.dot(even, _half_place(True), preferred_element_type=F32)
            + jnp.dot(odd, _half_place(False), preferred_element_type=F32)).astype(CDT)


def _proj_heads(a, w, bias, heads, name):
    S, K = a.shape
    N = heads * ATT_HD
    tm = _tile(S)

    def body(a_ref, w_ref, b_ref, o_ref):
        acc = (_dot(a_ref[...], w_ref[...]) + b_ref[...]).astype(CDT)
        sel = (_half_select(True), _half_select(False))
        for h in range(heads):
            pair = acc[:, (h // 2) * 2 * ATT_HD:(h // 2 + 1) * 2 * ATT_HD]
            o_ref[h] = jnp.dot(pair, sel[h % 2], preferred_element_type=F32).astype(CDT)

    return _call(
        None, body, grid=(S // tm,),
        in_specs=[pl.BlockSpec((tm, K), lambda i: (i, 0)), pl.BlockSpec((K, N), lambda i: (0, 0)),
                  pl.BlockSpec((1, N), lambda i: (0, 0))],
        out_specs=pl.BlockSpec((heads, tm, ATT_HD), lambda i: (0, i, 0)), out_shape=_sds((heads, S, ATT_HD), CDT),
        compiler_params=_params(("arbitrary",)), name=name)(a, w, bias)


def _qkv_bwd(dz, dq, dkv4, wq, wkv, name, after=None):
    S = dz.shape[0]
    tm = _tile(S)
    HK = dkv4.shape[0]
    NK = HK * ATT_HD

    def body(dz_ref, dq_ref, dkv_ref, wq_ref, wkv_ref, dx_ref, dkvn_ref, dkvb_ref):
        first = pl.program_id(0) == 0
        dkvn = jnp.concatenate([_pair_lanes(dkv_ref[2 * i].astype(CDT), dkv_ref[2 * i + 1].astype(CDT))
                                for i in range(HK // 2)], axis=-1)
        dkvn_ref[...] = dkvn
        dx_ref[...] = ALPHA * dz_ref[...] + _dot_nt(dq_ref[...], wq_ref[...]) + _dot_nt(dkvn, wkv_ref[...])
        for h in range(HK):
            _acc(dkvb_ref.at[h], _colsum(dkv_ref[h]), first)

    row = pl.BlockSpec((tm, D), lambda i: (i, 0))
    return _call(
        after, body, grid=(S // tm,),
        in_specs=[row, row, pl.BlockSpec((HK, tm, ATT_HD), lambda i: (0, i, 0)),
                  pl.BlockSpec((D, D), lambda i: (0, 0)), pl.BlockSpec((D, NK), lambda i: (0, 0))],
        out_specs=[row, pl.BlockSpec((tm, NK), lambda i: (i, 0)), pl.BlockSpec((HK, 1, ATT_HD), lambda i: (0, 0, 0))],
        out_shape=[_sds((S, D), F32), _sds((S, NK), CDT), _sds((HK, 1, ATT_HD), F32)],
        compiler_params=_params(("arbitrary",)), name=name)(dz, dq, dkv4, wq, wkv)


def _inproj_bwd(dz, dproj, wain, name, after=None):
    S = dz.shape[0]
    tm = _tile(S)
    nb = wain.shape[-1]

    def body(dz_ref, dp_ref, w_ref, dx_ref):
        acc = ALPHA * dz_ref[...]
        for j in range(N_DEV):
            acc = acc + _dot_nt(dp_ref[j // 2, :, pl.ds((j % 2) * nb, nb)], w_ref[j])
        dx_ref[...] = acc

    row = pl.BlockSpec((tm, D), lambda i: (i, 0))
    return _call(
        after, body, grid=(S // tm,),
        in_specs=[row, pl.BlockSpec((4, tm, D), lambda i: (0, i, 0)), pl.BlockSpec((N_DEV, D, nb), lambda i: (0, 0, 0))],
        out_specs=row, out_shape=_sds((S, D), F32),
        compiler_params=_params(("arbitrary",)), name=name)(dz, dproj, wain)


def _running_sum(x, reverse=False):
    rows = x.shape[0]
    row = lax.broadcasted_iota(jnp.int32, x.shape, 0)
    step = 1
    while step < rows:
        if reverse:
            x = x + jnp.where(row < rows - step, pltpu.roll(x, rows - step, 0), 0.0)
        else:
            x = x + jnp.where(row >= step, pltpu.roll(x, step, 0), 0.0)
        step *= 2
    return x


def _hg_gates(q, f, alb_ref):
    a0, a1 = alb_ref[0:1, :], alb_ref[1:2, :]
    mx = jnp.maximum(a0, a1)
    e0, e1 = jnp.exp(a0 - mx), jnp.exp(a1 - mx)
    lb = e0 / (e0 + e1)
    sig = _sigmoid(f)
    forget = lb + (1.0 - lb) * sig
    k = (1.0 - lb) * _sigmoid(-f)
    qs = q * _sigmoid(q) * (HG_DK ** -0.5)
    return qs, k, jnp.log(forget), sig, lb, forget


def _hg_intra(qs, k, b, b_scr):
    b_scr[...] = b
    bm = b_scr[pl.ds(HG_CH // 2 - 1, 1), :]
    bl = b_scr[pl.ds(HG_CH - 1, 1), :]
    eb = jnp.exp(b)
    qb = qs * eb
    e_q = jnp.exp(b - bm)
    e_k = jnp.exp(bm - b)
    e_d = jnp.exp(bl - b)
    return qb, qs * e_q, k * e_k, k * e_d, jnp.exp(bl), eb, e_q, e_k, e_d


def _hgrn_fwd(proj, alb, ngain):
    S = proj.shape[1]
    nc = S // HG_CH
    nb = nc // HG_CPB
    rb, wb = HG_CPB * HG_CH, HG_HPB * HG_DK

    def body(pj_ref, alb_ref, ng_ref, o_ref, y_ref, st_ref, st_scr, b_scr):
        n = pl.program_id(1)

        @pl.when(n == 0)
        def _():
            st_scr[...] = jnp.zeros_like(st_scr)

        r = lax.broadcasted_iota(jnp.int32, (HG_CH, HG_CH), 0)
        c = lax.broadcasted_iota(jnp.int32, (HG_CH, HG_CH), 1)
        causal = r >= c
        for ci, j in [(ci, j) for ci in range(HG_CPB) for j in range(HG_HPB)]:
            rows, lanes = pl.ds(ci * HG_CH, HG_CH), pl.ds(j * HG_DK, HG_DK)
            q, f, v, g = pj_ref[0, rows, lanes], pj_ref[1, rows, lanes], pj_ref[2, rows, lanes], pj_ref[3, rows, lanes]
            qs, k, logf, _, _, _ = _hg_gates(q, f, alb_ref.at[:, lanes])
            b = _running_sum(logf)
            qb, qt, kt, kd, ebl, _, _, _, _ = _hg_intra(qs, k, b, b_scr.at[j, ci])
            st = st_scr[j]
            st_ref[j, ci] = st
            a = jnp.where(causal, _dot_nt(qt, kt), 0.0)
            o = _dot(a, v) + _dot_nt(qb, st)
            st_scr[j] = st * ebl + _dot_tn(v, kd)
            o_ref[rows, lanes] = o
            rinv = lax.rsqrt(jnp.mean(o * o, axis=-1, keepdims=True) + RMS_EPS)
            y_ref[rows, lanes] = (o * rinv * ng_ref[...] * (g * _sigmoid(g))).astype(CDT)

    blk = pl.BlockSpec((rb, wb), lambda h, n: (n, h))
    return _call(
        None, body, grid=(HG_H // HG_HPB, nb),
        in_specs=[pl.BlockSpec((4, rb, wb), lambda h, n: (0, n, h)), pl.BlockSpec((2, wb), lambda h, n: (0, h)),
                  pl.BlockSpec((1, HG_DK), lambda h, n: (0, 0))],
        out_specs=[blk, blk, pl.BlockSpec((HG_HPB, HG_CPB, HG_DK, HG_DK), lambda h, n: (h, n, 0, 0))],
        out_shape=[_sds((S, D), F32), _sds((S, D), CDT), _sds((HG_H, nc, HG_DK, HG_DK), F32)],
        scratch_shapes=[pltpu.VMEM((HG_HPB, HG_DK, HG_DK), F32), pltpu.VMEM((HG_HPB, HG_CPB, HG_CH, HG_DK), F32)],
        compiler_params=_params(("arbitrary", "arbitrary")), name="hgrn_fwd")(proj, alb, ngain)


def _hgrn_bwd(proj, alb, ngain, o, states, dy, after=None):
    S = proj.shape[1]
    nc = S // HG_CH
    nb = nc // HG_CPB
    rb, wb = HG_CPB * HG_CH, HG_HPB * HG_DK

    def body(pj_ref, alb_ref, ng_ref, o_ref, st_ref, dy_ref, dpj_ref, dalb_ref, dng_ref, dst_scr, b_scr):
        h, n = pl.program_id(0), pl.program_id(1)

        @pl.when(n == 0)
        def _():
            dst_scr[...] = jnp.zeros_like(dst_scr)
            dalb_ref[...] = jnp.zeros_like(dalb_ref)

        _zero_at(dng_ref, jnp.logical_and(h == 0, n == 0))
        ng = ng_ref[...]
        r = lax.broadcasted_iota(jnp.int32, (HG_CH, HG_CH), 0)
        c = lax.broadcasted_iota(jnp.int32, (HG_CH, HG_CH), 1)
        causal = r >= c
        dng = None
        for ci, j in [(ci, j) for ci in reversed(range(HG_CPB)) for j in range(HG_HPB)]:
            rows, lanes = pl.ds(ci * HG_CH, HG_CH), pl.ds(j * HG_DK, HG_DK)
            q, f, v, g = pj_ref[0, rows, lanes], pj_ref[1, rows, lanes], pj_ref[2, rows, lanes], pj_ref[3, rows, lanes]
            o_ = o_ref[rows, lanes]
            dy_ = dy_ref[rows, lanes]
            sg = _sigmoid(g)
            rinv = lax.rsqrt(jnp.mean(o_ * o_, axis=-1, keepdims=True) + RMS_EPS)
            nrm = o_ * rinv
            dr = dy_ * (g * sg)
            dg = dy_ * nrm * ng * (sg * (1.0 + g * (1.0 - sg)))
            dn = dr * ng
            do = rinv * (dn - nrm * jnp.mean(dn * nrm, axis=-1, keepdims=True))
            dng = _colsum(dr * nrm) if dng is None else dng + _colsum(dr * nrm)
            qs, k, logf, sig, lb, forget = _hg_gates(q, f, alb_ref.at[:, lanes])
            b = _running_sum(logf)
            qb, qt, kt, kd, ebl, eb, e_q, e_k, e_d = _hg_intra(qs, k, b, b_scr.at[j, ci])
            st = st_ref[j, ci]
            dstn = dst_scr[j]
            qt, kt, qb, kd = (t.astype(CDT).astype(F32) for t in (qt, kt, qb, kd))
            a = jnp.where(causal, _dot_nt(qt, kt), 0.0)
            da = jnp.where(causal, _dot_nt(do, v), 0.0)
            dv = _dot_tn(a, do) + _dot_nt(kd, dstn)
            dqb = _dot(do, st)
            dkd = _dot(v, dstn)
            dqt = _dot(da, kt)
            dkt = _dot_tn(da, qt)
            dbl = _colsum(dkd * kd) + ebl * _colsum(dstn * st)
            dst_scr[j] = dstn * ebl + _dot_tn(do, qb)
            dqs = dqt * e_q + dqb * eb
            dk = dkt * e_k + dkd * e_d
            db = dqt * qt + dqb * qb - dkt * kt - dkd * kd
            dlogf = _running_sum(db, reverse=True) + dbl
            dforget = dlogf / forget
            dsig = (1.0 - lb) * (dforget - dk)
            df = dsig * sig * (1.0 - sig)
            dlb = _colsum((dforget - dk) * (1.0 - sig))
            sq = _sigmoid(q)
            dq = dqs * (HG_DK ** -0.5) * (sq * (1.0 + q * (1.0 - sq)))
            dpj_ref[0, rows, lanes] = dq.astype(CDT)
            dpj_ref[1, rows, lanes] = df.astype(CDT)
            dpj_ref[2, rows, lanes] = dv.astype(CDT)
            dpj_ref[3, rows, lanes] = dg.astype(CDT)
            da0 = dlb * lb * (1.0 - lb)
            dalb_ref[pl.ds(0, 1), lanes] += da0
            dalb_ref[pl.ds(1, 1), lanes] -= da0
        dng_ref[...] += dng

    blk = pl.BlockSpec((rb, wb), lambda h, n: (nb - 1 - n, h))
    pj = pl.BlockSpec((4, rb, wb), lambda h, n: (0, nb - 1 - n, h))
    alb_blk = pl.BlockSpec((2, wb), lambda h, n: (0, h))
    ng_blk = pl.BlockSpec((1, HG_DK), lambda h, n: (0, 0))
    return _call(
        after, body, grid=(HG_H // HG_HPB, nb),
        in_specs=[pj, alb_blk, ng_blk, blk,
                  pl.BlockSpec((HG_HPB, HG_CPB, HG_DK, HG_DK), lambda h, n: (h, nb - 1 - n, 0, 0)), blk],
        out_specs=[pj, alb_blk, ng_blk],
        out_shape=[_sds((4, S, D), CDT), _sds((2, D), F32), _sds((1, HG_DK), F32)],
        scratch_shapes=[pltpu.VMEM((HG_HPB, HG_DK, HG_DK), F32), pltpu.VMEM((HG_HPB, HG_CPB, HG_CH, HG_DK), F32)],
        compiler_params=_params(("arbitrary", "arbitrary")), name="hgrn_bwd")(proj, alb, ngain, o, states, dy)


def _slope(h):
    return 2.0 ** (-8.0 * (h + 1) / ATT_QH)


def _attn_mask(n):
    qi = lax.broadcasted_iota(jnp.int32, (WINDOW, 2 * WINDOW), 0)
    si = lax.broadcasted_iota(jnp.int32, (WINDOW, 2 * WINDOW), 1)
    dist = qi - si + WINDOW
    valid = (dist >= 0) & (dist < WINDOW) & (n * WINDOW - WINDOW + si >= 0)
    return valid, dist.astype(F32)


def _attn_probs(qh, kh, sink, slope, valid, distf):
    s = _dot_nt(qh, kh) * (ATT_HD ** -0.5) - slope * distf
    s = jnp.where(valid, s, NEG)
    m = jnp.maximum(jnp.max(s, axis=-1, keepdims=True), sink)
    e = jnp.exp(s - m)
    es = jnp.exp(sink - m)
    inv = 1.0 / (jnp.sum(e, axis=-1, keepdims=True) + es)
    return e * inv, es * inv


def _attn_specs(S):
    nb = S // WINDOW
    cur = lambda H: pl.BlockSpec((H, WINDOW, ATT_HD), lambda n: (0, n, 0))
    prev = lambda H: pl.BlockSpec((H, WINDOW, ATT_HD), lambda n: (0, jnp.maximum(n - 1, 0), 0))
    return nb, cur, prev


def _attn_fwd(q4, kv4, sinks):
    S = q4.shape[1]
    nb, cur, prev = _attn_specs(S)

    def body(sink_ref, q_ref, kvc_ref, kvp_ref, o_ref):
        valid, distf = _attn_mask(pl.program_id(0))
        for kvh in range(ATT_KVH):
            kh = jnp.concatenate([kvp_ref[kvh], kvc_ref[kvh]], axis=0)
            vh = jnp.concatenate([kvp_ref[ATT_KVH + kvh], kvc_ref[ATT_KVH + kvh]], axis=0)
            v_low = jnp.dot(vh, _half_place(True), preferred_element_type=F32).astype(CDT)
            v_high = jnp.dot(vh, _half_place(False), preferred_element_type=F32).astype(CDT)
            for h in range(kvh * ATT_G, (kvh + 1) * ATT_G, 2):
                p_even, _ = _attn_probs(q_ref[h], kh, sink_ref[0, h], _slope(h), valid, distf)
                p_odd, _ = _attn_probs(q_ref[h + 1], kh, sink_ref[0, h + 1], _slope(h + 1), valid, distf)
                o_ref[:, pl.ds(h * ATT_HD, 2 * ATT_HD)] = (_dot(p_even, v_low) + _dot(p_odd, v_high)).astype(CDT)

    return _call(
        None, body, grid=(nb,),
        in_specs=[pl.BlockSpec(memory_space=pltpu.SMEM), cur(ATT_QH), cur(2 * ATT_KVH), prev(2 * ATT_KVH)],
        out_specs=pl.BlockSpec((WINDOW, D), lambda n: (n, 0)), out_shape=_sds((S, D), CDT),
        compiler_params=_params(("arbitrary",)), name="attn_fwd")(sinks, q4, kv4, kv4)


def _attn_bwd(q4, kv4, sinks, do):
    S = q4.shape[1]
    nb, cur, prev = _attn_specs(S)

    def body(sink_ref, q_ref, kvc_ref, kvp_ref, do_ref, dq_ref, dkv_ref, dbq_ref, dsink_ref):
        n = pl.program_id(0)
        first = n == 0

        @pl.when(first)
        def _():
            dkv_ref[...] = jnp.zeros_like(dkv_ref)
            dsink_ref[...] = jnp.zeros_like(dsink_ref)
            dbq_ref[...] = jnp.zeros_like(dbq_ref)

        valid, distf = _attn_mask(n)
        lane = lax.broadcasted_iota(jnp.int32, (1, 128), 1)
        rows_cur = pl.ds(pl.multiple_of(n * WINDOW, WINDOW), WINDOW)
        rows_prev = pl.ds(pl.multiple_of(jnp.maximum(n - 1, 0) * WINDOW, WINDOW), WINDOW)
        dsinks = jnp.zeros((1, 128), F32)
        sel = (_half_select(True), _half_select(False))
        for kvh in range(ATT_KVH):
            kh = jnp.concatenate([kvp_ref[kvh], kvc_ref[kvh]], axis=0)
            vh = jnp.concatenate([kvp_ref[ATT_KVH + kvh], kvc_ref[ATT_KVH + kvh]], axis=0)
            dk = dv = None
            dqs = []
            for h in range(kvh * ATT_G, (kvh + 1) * ATT_G):
                qh = q_ref[h]
                doh = jnp.dot(do_ref[:, pl.ds((h // 2) * 2 * ATT_HD, 2 * ATT_HD)], sel[h % 2],
                              preferred_element_type=F32).astype(CDT)
                p, ps = _attn_probs(qh, kh, sink_ref[0, h], _slope(h), valid, distf)
                dp = _dot_nt(doh, vh)
                dd = jnp.sum(p * dp, axis=-1, keepdims=True)
                ds = p * (dp - dd)
                dsinks = dsinks + jnp.where(lane == h, -jnp.sum(ps * dd, axis=0, keepdims=True), 0.0)
                dqh = _dot(ds, kh) * (ATT_HD ** -0.5)
                dqs.append(dqh.astype(CDT))
                dbq_ref[h] += _colsum(dqh)
                dkh = _dot_tn(ds, qh) * (ATT_HD ** -0.5)
                dvh = _dot_tn(p, doh)
                dk = dkh if dk is None else dk + dkh
                dv = dvh if dv is None else dv + dvh
            for i in range(ATT_G // 2):
                lanes = pl.ds((kvh * ATT_G + 2 * i) * ATT_HD, 2 * ATT_HD)
                dq_ref[:, lanes] = _pair_lanes(dqs[2 * i], dqs[2 * i + 1])
            dkv_ref[kvh, rows_prev, :] += dk[:WINDOW]
            dkv_ref[kvh, rows_cur, :] += dk[WINDOW:]
            dkv_ref[ATT_KVH + kvh, rows_prev, :] += dv[:WINDOW]
            dkv_ref[ATT_KVH + kvh, rows_cur, :] += dv[WINDOW:]
        dsink_ref[...] += dsinks

    return _call(
        None, body, grid=(nb,),
        in_specs=[pl.BlockSpec(memory_space=pltpu.SMEM), cur(ATT_QH), cur(2 * ATT_KVH), prev(2 * ATT_KVH),
                  pl.BlockSpec((WINDOW, D), lambda n: (n, 0))],
        out_specs=[pl.BlockSpec((WINDOW, D), lambda n: (n, 0)), pl.BlockSpec((2 * ATT_KVH, S, ATT_HD), lambda n: (0, 0, 0)),
                   pl.BlockSpec((ATT_QH, 1, ATT_HD), lambda n: (0, 0, 0)), pl.BlockSpec((1, 128), lambda n: (0, 0))],
        out_shape=[_sds((S, D), CDT), _sds((2 * ATT_KVH, S, ATT_HD), F32), _sds((ATT_QH, 1, ATT_HD), F32),
                   _sds((1, 128), F32)],
        compiler_params=_params(("arbitrary",)), name="attn_bwd")(sinks, q4, kv4, kv4, do)


def _local_step(x, p, target, getw, sm, emit):
    S = x.shape[0]
    vec = lambda a: a.reshape(1, -1)
    ln_g = lambda l, k: vec(sm["ln_gain"][l, k])
    ln_b = lambda l, k: vec(sm["ln_bias"][l, k])
    xb = x.astype(CDT)
    pb = p.astype(CDT)

    proj = _mm_nn(xb, getw("a_w_in"), (4, S, D), (None, _tile(S), 512), lambda g, i: (g // 2, i, g % 2), F32,
                  name="a_in")
    o_a, y_a, states = _hgrn_fwd(proj, sm["a_lower_bound"], sm["a_norm_gain"])
    zeros = jnp.zeros((1, D), F32)
    z = [[None] * 3 for _ in range(2)]
    xs = [[None] * 3 for _ in range(2)]
    xbs = [[None] * 3 for _ in range(2)]
    z[0][0], xs[0][0], xbs[0][0] = _mixout_ln(y_a[None], getw("a_w_out")[None], zeros, x, ln_g(0, 0), ln_b(0, 0),
                                              "a_out_ln")
    gu, hid, sgs, ups = [None, None], [None, None], [None, None], [None, None]

    def ffn_ple(l):
        wgu = getw(f"gu{l}")
        gu[l], hid[l], z[l][1], xs[l][1], xbs[l][1] = _ffn_fwd(
            xs[l][0], xbs[l][0], wgu, getw(f"dn{l}"), ln_g(l, 1), ln_b(l, 1), f"ffn_fwd{l}")
        sgs[l], ups[l], z[l][2], xs[l][2], xbs[l][2] = _ple_fwd(
            xs[l][1], xbs[l][1], pb[l], getw(f"pg{l}"), vec(sm["ple_b_gate"][l]), getw(f"pu{l}"), ln_g(l, 2),
            ln_b(l, 2), f"ple_fwd{l}")

    ffn_ple(0)
    x3, x3b = xs[0][2], xbs[0][2]
    w_kv, w_q, w_bo = getw("kv_w"), getw("b_w_q"), getw("b_w_out")
    kv4 = _proj_heads(x3b, w_kv, vec(sm["kv_b"]), 2 * ATT_KVH, "kv_proj")
    q4 = _proj_heads(x3b, w_q, vec(sm["b_b_q"]), ATT_QH, "q_proj")
    o_b = _attn_fwd(q4, kv4, sm["b_sinks"])
    z[1][0], xs[1][0], xbs[1][0] = _mixout_ln(o_b[None], w_bo[None], sm["b_b_out"], x3, ln_g(1, 0), ln_b(1, 0),
                                              "b_out_ln")
    ffn_ple(1)
    loss, dy = _loss_fwd_bwd(xs[1][2], target)

    gs = {}
    d_ln_g = [[None] * 3 for _ in range(2)]
    d_ln_b = [[None] * 3 for _ in range(2)]
    g_bg = [None, None]

    def ffn_ple_bwd(l, dy, after=None):
        dx2, dgl, dup, d_ln_g[l][2], d_ln_b[l][2], g_bg[l] = _ple_bwd(dy, z[l][2], sgs[l], ups[l], ln_g(l, 2),
                                                                     getw(f"pg{l}"), f"ple_bwd{l}", after=after)
        g_pg = _wgrad(xbs[l][1][None], dgl[None], f"g_ple_gate{l}")[0]
        g_pu = _wgrad(pb[l][None], dup[None], f"g_ple_up{l}")[0]
        tok = emit({f"pg{l}": g_pg, f"pu{l}": g_pu})
        dz2, dzb, dgu, d_ln_g[l][1], d_ln_b[l][1] = _ffn_bwd_hidden(dx2, z[l][1], gu[l], getw(f"dn{l}"), ln_g(l, 1),
                                                                   f"ffn_bwd{l}", after=tok)
        tok = emit({f"dn{l}": _wgrad(hid[l], dzb[None], f"g_ffn_down{l}")})
        g_gu = _wgrad(dgu.reshape(8, S, FFN_B), xbs[l][0][None], f"g_ffn_gate_up{l}")
        tok = tok + emit({f"gu{l}": g_gu})
        dx1 = _ffn_bwd_input(dz2, dgu, getw(f"gu{l}"), f"ffn_bwd{l}", after=tok)
        return dx1, None

    dx1, tok = ffn_ple_bwd(1, dy)
    dz, dzb, do, d_ln_g[1][0], d_ln_b[1][0], gs["b_b_out"] = _mixout_bwd(dx1, z[1][0], ln_g(1, 0), w_bo[None], CDT,
                                                                        "b_out_bwd", after=tok)
    g_bo = _wgrad(o_b[None], dzb[None], "g_b_w_out")[0]
    dq, dkv4, dbq, dsinks = _attn_bwd(q4, kv4, sm["b_sinks"], do[0])
    gs["b_b_q"] = dbq
    gs["b_sinks"] = dsinks
    g_q = _wgrad(x3b[None], dq[None], "g_b_w_q")[0]
    dx3, dkv, gs["kv_b"] = _qkv_bwd(dz, dq, dkv4, w_q, w_kv, "qkv_bwd", after=tok)
    g_kv = _wgrad(x3b[None], dkv[None], "g_kv_w")[0]
    tok = emit({"b_w_out": g_bo, "b_w_q": g_q, "kv_w": g_kv})
    dx1, tok = ffn_ple_bwd(0, dx3, tok)
    w_ao = getw("a_w_out")
    dz, dzb, dyr, d_ln_g[0][0], d_ln_b[0][0], _ = _mixout_bwd(dx1, z[0][0], ln_g(0, 0), w_ao[None], F32, "a_out_bwd",
                                                              after=tok)
    g_ao = _wgrad(y_a[None], dzb[None], "g_a_w_out")[0]
    tok = emit({"a_w_out": g_ao})
    dproj, gs["a_lower_bound"], gs["a_norm_gain"] = _hgrn_bwd(proj, sm["a_lower_bound"], sm["a_norm_gain"], o_a, states,
                                                              dyr[0], after=tok)
    tk = lambda t: (None, t, D)
    g_ain = _mm_tn(xb[None], dproj, N_DEV, lambda g, k: (0, k, 0), lambda g, k: (g // 2, k, g % 2),
                   tk, lambda t: (None, t, 512), (N_DEV, D, 512), (None, D, 512), lambda g, k: (g, 0, 0), name="g_a_w_in")
    gs["ple_b_gate"] = jnp.concatenate(g_bg, axis=0)
    gs["ln_gain"] = jnp.stack([jnp.concatenate(r, axis=0) for r in d_ln_g])
    gs["ln_bias"] = jnp.stack([jnp.concatenate(r, axis=0) for r in d_ln_b])
    gs["loss"] = loss
    tok = emit({"a_w_in": g_ain}, small=gs)
    grad_x = _inproj_bwd(dz, dproj, getw("a_w_in"), "a_in_bwd", after=tok)
    return loss, grad_x, gs


def _peer(k):
    x, y, c = lax.axis_index("x"), lax.axis_index("y"), lax.axis_index("c")
    px = 1 - x if k & 4 else x
    py = 1 - y if k & 2 else y
    pc = 1 - c if k & 1 else c
    return (px, py, pc), 4 * px + 2 * py + pc


def _my_index():
    return 4 * lax.axis_index("x") + 2 * lax.axis_index("y") + lax.axis_index("c")


def _piece_copy(mode, src, land, send_sems, recv_sems, t, k, sender, receiver, peer):
    return pltpu.make_async_remote_copy(
        src_ref=src if mode == "gather" else src.at[receiver], dst_ref=land.at[sender],
        send_sem=send_sems.at[t * 7 + k - 1], recv_sem=recv_sems.at[t * 7 + k - 1], device_id=peer, device_id_type=MESH)


def _sequencer_exchange(srcs, modes, name, collective_id, after=None):
    n = len(srcs)
    land_shapes = [((N_DEV,) + a.shape) if mode == "gather" else a.shape for a, mode in zip(srcs, modes)]
    extra = [] if after is None else [after]

    def body(*refs):
        src_refs, land_refs = refs[:n], refs[n + len(extra):2 * n + len(extra)]
        send_sems, recv_sems, local_sems = refs[2 * n + len(extra):]
        barrier = pltpu.get_barrier_semaphore()
        for k in range(1, N_DEV):
            pl.semaphore_signal(barrier, inc=1, device_id=_peer(k)[0], device_id_type=MESH)
        pl.semaphore_wait(barrier, N_DEV - 1)
        me = _my_index()
        local = []
        for i in range(n):
            cp = pltpu.make_async_copy(src_refs[i] if modes[i] == "gather" else src_refs[i].at[me], land_refs[i].at[me],
                                       local_sems.at[i])
            cp.start()
            local.append(cp)
        for k in range(1, N_DEV):
            peer, pid = _peer(k)
            for t in range(n):
                _piece_copy(modes[t], src_refs[t], land_refs[t], send_sems, recv_sems, t, k, me, pid, peer).start()
        for k in range(1, N_DEV):
            peer, pid = _peer(k)
            for t in range(n):
                _piece_copy(modes[t], src_refs[t], land_refs[t], send_sems, recv_sems, t, k, pid, me, peer).wait_recv()
        for k in range(1, N_DEV):
            peer, pid = _peer(k)
            for t in range(n):
                _piece_copy(modes[t], src_refs[t], land_refs[t], send_sems, recv_sems, t, k, me, pid, peer).wait_send()
        for cp in local:
            cp.wait()

    return pl.kernel(
        body, out_type=[_sds(s, a.dtype) for s, a in zip(land_shapes, srcs)],
        mesh=plsc.ScalarSubcoreMesh(axis_name="sequencer", num_cores=1),
        scratch_types=[pltpu.SemaphoreType.DMA((7 * n,)), pltpu.SemaphoreType.DMA((7 * n,)), pltpu.SemaphoreType.DMA((n,))],
        compiler_params=pltpu.CompilerParams(collective_id=collective_id), name=name)(*srcs, *extra)


def _sequencer_gather(srcs, name, collective_id, after=None):
    n = len(srcs)
    extra = [] if after is None else [after]

    def body(*refs):
        src_refs, land_refs = refs[:n], refs[n + len(extra):2 * n + len(extra)]
        send_sems, recv_sems, local_sems = refs[2 * n + len(extra):]
        x, y, c = lax.axis_index("x"), lax.axis_index("y"), lax.axis_index("c")
        sibling = (x, y, 1 - c)
        chips = [(1 - x, y), (x, 1 - y), (1 - x, 1 - y)]
        index = lambda px, py, pc: 4 * px + 2 * py + pc
        barrier = pltpu.get_barrier_semaphore()
        for peer in [sibling] + [(*chip, c) for chip in chips]:
            pl.semaphore_signal(barrier, inc=1, device_id=peer, device_id_type=MESH)
        pl.semaphore_wait(barrier, 4)

        def copy(t, k, slot, to, src=None):
            return pltpu.make_async_remote_copy(
                src_ref=land_refs[t].at[slot] if src is None else src, dst_ref=land_refs[t].at[slot],
                send_sem=send_sems.at[7 * t + k], recv_sem=recv_sems.at[7 * t + k], device_id=to, device_id_type=MESH)

        me = index(x, y, c)
        local = []
        for t in range(n):
            cp = pltpu.make_async_copy(src_refs[t], land_refs[t].at[me], local_sems.at[t])
            cp.start()
            local.append(cp)
        sends = []
        for t in range(n):
            sends.append(copy(t, 0, me, sibling, src=src_refs[t]))
            sends += [copy(t, 1 + j, me, (*chip, c), src=src_refs[t]) for j, chip in enumerate(chips)]
        for cp in sends:
            cp.start()
        for j, chip in enumerate(chips):
            for t in range(n):
                copy(t, 1 + j, index(*chip, c), sibling, src=src_refs[t]).wait_recv()
                passed = copy(t, 4 + j, index(*chip, c), sibling)
                passed.start()
                sends.append(passed)
        for t in range(n):
            copy(t, 0, index(x, y, 1 - c), sibling, src=src_refs[t]).wait_recv()
        for j, chip in enumerate(chips):
            for t in range(n):
                copy(t, 4 + j, index(*chip, 1 - c), sibling, src=src_refs[t]).wait_recv()
        for cp in sends:
            cp.wait_send()
        for cp in local:
            cp.wait()

    return pl.kernel(
        body, out_type=[_sds((N_DEV,) + a.shape, a.dtype) for a in srcs],
        mesh=plsc.ScalarSubcoreMesh(axis_name="sequencer", num_cores=1),
        scratch_types=[pltpu.SemaphoreType.DMA((7 * n,)), pltpu.SemaphoreType.DMA((7 * n,)), pltpu.SemaphoreType.DMA((n,))],
        compiler_params=pltpu.CompilerParams(collective_id=collective_id), name=name)(*srcs, *extra)


def _adamw(w, g, m, v):
    m = ADAM_B1 * m + (1.0 - ADAM_B1) * g
    v = ADAM_B2 * v + (1.0 - ADAM_B2) * (g * g)
    m_hat = m / (1.0 - ADAM_B1 ** ADAM_STEP)
    v_hat = v / (1.0 - ADAM_B2 ** ADAM_STEP)
    delta = -ADAM_LR * (m_hat / (jnp.sqrt(v_hat) + ADAM_EPS) + ADAM_WD * w)
    return delta, m, v


def _adam_big(w, parts, m, v, name, after=None):
    L, R, C = w.shape
    P = parts[0].shape[0]
    tr = _tile(R, (256, 128, 176, 64, 32, 16))
    nr = R // tr

    def body(w_ref, *refs):
        p_refs, (m_ref, v_ref, g_ref, d_ref, mo_ref, vo_ref) = refs[:L], refs[L:]
        for l in range(L):
            @pl.when(pl.program_id(0) == l)
            def _(p_ref=p_refs[l]):
                g = p_ref[0].astype(F32)
                for s in range(1, P):
                    g = g + p_ref[s].astype(F32)
                g_ref[...] = g
                d_ref[...], mo_ref[...], vo_ref[...] = _adamw(w_ref[...], g, m_ref[...], v_ref[...])

    row = pl.BlockSpec((None, tr, C), lambda l, i: (l, i, 0))
    park = lambda l_of: (lambda l, i: (0, jnp.where(l == l_of, i, 0 if l_of else nr - 1), 0))
    return _call(
        after, body, grid=(L, nr),
        in_specs=[row] + [pl.BlockSpec((P, tr, C), park(l)) for l in range(L)] + [row, row],
        out_specs=[row] * 4, out_shape=[_sds((L, R, C), F32)] * 4,
        compiler_params=_params(("arbitrary", "arbitrary")), name=name)(w, *parts, m, v)


SMALL = (("a_lower_bound", (2, 128), (2, D)), ("ln_gain", (6, 128), (6, D)), ("ln_bias", (6, 128), (6, D)),
         ("a_norm_gain", (1, 128), (1, 128)), ("kv_b", (1, 512), (1, 512)), ("b_b_q", (1, D), (1, D)),
         ("b_sinks", (1, ATT_QH), (1, 128)), ("b_b_out", (1, D), (1, D)), ("ple_b_gate", (2, D), (2, D)))


def _adam_small(parts, w, m, v, losses, after=None):
    k = len(SMALL)

    def body(*refs):
        p_refs, w_refs, m_refs, v_refs = refs[:k], refs[k:2 * k], refs[2 * k:3 * k], refs[3 * k:4 * k]
        loss_ref, outs, total_ref = refs[4 * k], refs[4 * k + 1:-1], refs[-1]
        total = loss_ref[0]
        for s in range(1, N_DEV):
            total = total + loss_ref[s]
        total_ref[...] = total
        me = _my_index()
        for i, (_, wshape, pshape) in enumerate(SMALL):
            cols = wshape[1]
            lanes = slice(None) if cols == pshape[1] else (
                pl.ds(0, cols) if cols < 128 else pl.ds(pl.multiple_of(me * cols, cols), cols))
            g = p_refs[i][0, :, lanes]
            for s in range(1, N_DEV):
                g = g + p_refs[i][s, :, lanes]
            g_ref, d_ref, mo_ref, vo_ref = outs[4 * i:4 * i + 4]
            g_ref[...] = g
            d_ref[...], mo_ref[...], vo_ref[...] = _adamw(w_refs[i][...], g, m_refs[i][...], v_refs[i][...])

    full = lambda shape: pl.BlockSpec(shape, lambda: (0,) * len(shape))
    names = [n for n, _, _ in SMALL]
    res = _call(
        after, body,
        in_specs=[full((N_DEV,) + ps) for _, _, ps in SMALL] + [full(ws) for _, ws, _ in SMALL] * 3
        + [full((N_DEV, 1, 128))],
        out_specs=[full(ws) for _, ws, _ in SMALL for _ in range(4)] + [full((1, 128))],
        out_shape=[_sds(ws, F32) for _, ws, _ in SMALL for _ in range(4)] + [_sds((1, 128), F32)], name="adam_small")(
            *[parts[n] for n in names], *[a[n].reshape(ws) for a in (w, m, v) for n, ws, _ in SMALL], losses)
    return {n: [r.reshape(w[n].shape) for r in res[4 * i:4 * i + 4]] for i, n in enumerate(names)}, res[-1][0, 0]


WEIGHTS = ("a_w_in", "a_lower_bound", "a_norm_gain", "a_w_out", "kv_w", "kv_b", "b_w_q", "b_b_q", "b_sinks", "b_w_out",
           "b_b_out", "ffn_w_gate_up", "ffn_w_down", "ple_w_up", "ple_w_gate", "ple_b_gate", "ln_gain", "ln_bias")


GATHER_GROUPS = (("a_w_in",), ("a_w_out", "gu0"), ("dn0", "pu0", "pg0"), ("kv_w", "b_w_q", "b_w_out"), ("gu1",),
                 ("dn1", "pu1", "pg1"))
KERNEL_LAYOUT = {
    "a_w_in": lambda a: a,
    "a_w_out": lambda a: a.reshape(D, D),
    "kv_w": lambda a: a.reshape(D, 2 * ATT_KVH * ATT_HD),
    "b_w_q": lambda a: a.reshape(D, D),
    "b_w_out": lambda a: a.reshape(D, D),
    "gu": lambda a: a.reshape(2, 4, FFN_B, D),
    "dn": lambda a: a.reshape(4, FFN_B, D),
    "pu": lambda a: a,
    "pg": lambda a: a.reshape(D, D),
}
_row_blocks = lambda a: a.reshape(N_DEV, -1, a.shape[-1])
OWNER_BLOCKS = {
    "a_w_in": lambda g: g,
    "a_w_out": _row_blocks,
    "kv_w": _row_blocks,
    "b_w_q": _row_blocks,
    "b_w_out": _row_blocks,
    "gu": lambda g: g,
    "dn": lambda g: _row_blocks(g.reshape(FFN_H, D)),
    "pu": lambda g: g.reshape(PLE_DIM, N_DEV, 128).transpose(1, 0, 2),
    "pg": _row_blocks,
}
ADAM_PARTS = (("kv_w", ("kv_w",)), ("b_w_q", ("b_w_q",)), ("b_w_out", ("b_w_out",)), ("ffn_w_gate_up", ("gu0", "gu1")),
              ("ffn_w_down", ("dn0", "dn1")), ("ple_w_up", ("pu0", "pu1")), ("ple_w_gate", ("pg0", "pg1")),
              ("a_w_out", ("a_w_out",)), ("a_w_in", ("a_w_in",)))


def kernel(x, p, a_w_in, a_lower_bound, a_norm_gain, a_w_out, kv_w, kv_b, b_w_q, b_b_q, b_sinks, b_w_out, b_b_out, ffn_w_gate_up, ffn_w_down, ple_w_up, ple_w_gate, ple_b_gate, ln_gain, ln_bias, loss_target, m_a_w_in, m_a_lower_bound, m_a_norm_gain, m_a_w_out, m_kv_w, m_kv_b, m_b_w_q, m_b_b_q, m_b_sinks, m_b_w_out, m_b_b_out, m_ffn_w_gate_up, m_ffn_w_down, m_ple_w_up, m_ple_w_gate, m_ple_b_gate, m_ln_gain, m_ln_bias, v_a_w_in, v_a_lower_bound, v_a_norm_gain, v_a_w_out, v_kv_w, v_kv_b, v_b_w_q, v_b_b_q, v_b_sinks, v_b_w_out, v_b_b_out, v_ffn_w_gate_up, v_ffn_w_down, v_ple_w_up, v_ple_w_gate, v_ple_b_gate, v_ln_gain, v_ln_bias):
    given = dict(locals())
    w = {n: given[n] for n in WEIGHTS}
    m = {n: given["m_" + n] for n in WEIGHTS}
    v = {n: given["v_" + n] for n in WEIGHTS}
    shards = {"a_w_in": a_w_in[0], "a_w_out": a_w_out[0], "kv_w": kv_w, "b_w_q": b_w_q[0], "b_w_out": b_w_out[0]}
    for l in range(2):
        shards.update({f"gu{l}": ffn_w_gate_up[l].T, f"dn{l}": ffn_w_down[l], f"pu{l}": ple_w_up[l], f"pg{l}": ple_w_gate[l]})
    sharded_small = [a_lower_bound, ln_gain.reshape(6, 128), ln_bias.reshape(6, 128)]
    gathered = {}
    for gi, g in enumerate(GATHER_GROUPS):
        lands = _sequencer_gather([shards[n].astype(CDT) for n in g] + (sharded_small if gi == 0 else []),
                                  f"gather{gi}", gi)
        for n, a in zip(g, lands):
            gathered[n] = KERNEL_LAYOUT[n.rstrip("01")](a)
        if gi == 0:
            alb, lng, lnb = [a.transpose(1, 0, 2).reshape(a.shape[1], D) for a in lands[len(g):]]

    getw = gathered.__getitem__

    sm = {"a_lower_bound": alb, "ln_gain": lng.reshape(2, 3, D), "ln_bias": lnb.reshape(2, 3, D),
          "a_norm_gain": a_norm_gain, "kv_b": kv_b, "b_b_q": b_b_q[0], "b_sinks": b_sinks, "b_b_out": b_b_out,
          "ple_b_gate": ple_b_gate}

    scatters, small_parts = [], {}

    def emit(grads, small=None):
        names = list(grads)
        blocks = [OWNER_BLOCKS[n.rstrip("01")](grads[n]) for n in names]
        partials = [] if small is None else [small[n].reshape(ps) for n, _, ps in SMALL] + [small["loss"]]
        lands = _sequencer_exchange(blocks + partials, ["scatter"] * len(blocks) + ["gather"] * len(partials),
                                    f"scatter{len(scatters)}", len(GATHER_GROUPS) + len(scatters))
        scatters.append(dict(zip(names, lands)))
        small_parts.update(zip([n for n, _, _ in SMALL] + ["loss"], lands[len(blocks):]))
        return blocks + (list(scatters[-4].values()) if len(scatters) >= 4 else [])

    loss, grad_x, gs = _local_step(x[0], p[:, 0], loss_target[0], getw, sm, emit)

    out, parts, last = {}, {}, [grad_x]
    for landed in scatters:
        parts.update(landed)
        for n, keys in ADAM_PARTS:
            if n in out or not all(key in parts for key in keys):
                continue
            lrc = (1,) * (3 - w[n].ndim) + w[n].shape
            shard = (lambda a: a.reshape(lrc).swapaxes(1, 2)) if n == "ffn_w_gate_up" else (lambda a: a.reshape(lrc))
            res = _adam_big(shard(w[n]), [parts[key] for key in keys], shard(m[n]), shard(v[n]), "adam_" + n, after=last)
            out[n] = [(r.swapaxes(1, 2) if n == "ffn_w_gate_up" else r).reshape(w[n].shape) for r in res]
            last = [res[3]]
    small_out, loss = _adam_small(small_parts, w, m, v, small_parts["loss"], after=last)
    out.update(small_out)
    res = [loss, grad_x[None]]
    for i in range(4):
        res += [out[n][i] for n in WEIGHTS]
    return tuple(res)
```

```python
import jax
import jax.numpy as jnp
from jax import lax
from jax.experimental import pallas as pl
from jax.experimental.pallas import tpu as pltpu
from jax.experimental.pallas import tpu_sc as plsc

F32 = jnp.float32
CDT = jnp.bfloat16

N_DEV = 8
D = 1024
HG_H, HG_DK, HG_CH = 8, 128, 64
HG_HPB = 4
HG_CPB = 8
ATT_HD, ATT_QH, ATT_KVH, ATT_G, WINDOW = 64, 16, 4, 4, 128
ATT_BPB = 2
FFN_H = 2816
FFN_B = FFN_H // 4
PLE_DIM = 256
ALPHA = (2.0 * 2) ** 0.25
LN_EPS = 1e-5
RMS_EPS = 1e-6
ADAM_LR, ADAM_B1, ADAM_B2, ADAM_EPS, ADAM_WD, ADAM_STEP = 0.001, 0.9, 0.999, 1e-08, 0.01, 10
ROW_TILES = (512, 256, 128, 64)
VMEM_LIMIT = 48 * 1024 * 1024
NEG = -1e30

MESH = pl.DeviceIdType.MESH


def _tile(n, cands=ROW_TILES):
    for t in cands:
        if n % t == 0:
            return t
    return n


def _sds(shape, dtype):
    return jax.ShapeDtypeStruct(tuple(shape), dtype)


def _params(sem):
    return pltpu.CompilerParams(dimension_semantics=sem, vmem_limit_bytes=VMEM_LIMIT)


def _dot(a, b):
    return jnp.dot(a.astype(CDT), b.astype(CDT), preferred_element_type=F32)


def _dot_nt(a, b):
    return lax.dot_general(a.astype(CDT), b.astype(CDT), (((1,), (1,)), ((), ())), preferred_element_type=F32)


def _dot_tn(a, b):
    return lax.dot_general(a.astype(CDT), b.astype(CDT), (((0,), (0,)), ((), ())), preferred_element_type=F32)


def _sigmoid(x):
    return jax.nn.sigmoid(x)


def _ln_fwd(z, g, b):
    mu = jnp.mean(z, axis=-1, keepdims=True)
    zc = z - mu
    var = jnp.mean(zc * zc, axis=-1, keepdims=True)
    return zc * lax.rsqrt(var + LN_EPS) * g + b


def _ln_bwd(z, g, dy):
    mu = jnp.mean(z, axis=-1, keepdims=True)
    zc = z - mu
    var = jnp.mean(zc * zc, axis=-1, keepdims=True)
    rstd = lax.rsqrt(var + LN_EPS)
    xhat = zc * rstd
    dxh = dy * g
    dz = rstd * (dxh - jnp.mean(dxh, axis=-1, keepdims=True) - xhat * jnp.mean(dxh * xhat, axis=-1, keepdims=True))
    return dz, xhat


def _colsum(x):
    return jnp.sum(x, axis=0, keepdims=True)


def _acc(ref, val, first):
    @pl.when(first)
    def _():
        ref[...] = val

    @pl.when(jnp.logical_not(first))
    def _():
        ref[...] += val


def _zero_at(ref, first):
    @pl.when(first)
    def _():
        ref[...] = jnp.zeros_like(ref)


def _call(after, body, **kw):
    after = [] if after is None else list(after)
    specs = list(kw["in_specs"])
    kw["in_specs"] = [pl.BlockSpec(memory_space=pl.ANY)] * len(after) + specs

    def ordered_body(*refs):
        body(*refs[len(after):])

    call = pl.pallas_call(ordered_body, **kw)

    def pinned(*args):
        args = [a if s.memory_space is not None else pltpu.with_memory_space_constraint(a, pltpu.HBM)
                for a, s in zip(args, specs)]
        return call(*after, *args)

    return pinned


def _mm_nn(a, b3, out_shape, oblock, omap, out_dtype, bias3=None, name="mm_nn"):
    M, K = a.shape
    G, _, Nb = b3.shape
    tm = _tile(M)

    def body(a_ref, b_ref, *rest):
        o_ref = rest[-1]
        acc = _dot(a_ref[...], b_ref[...])
        if bias3 is not None:
            acc = acc + rest[0][...]
        o_ref[...] = acc.astype(o_ref.dtype)

    in_specs = [pl.BlockSpec((tm, K), lambda g, i: (i, 0)), pl.BlockSpec((None, K, Nb), lambda g, i: (g, 0, 0))]
    args = [a, b3]
    if bias3 is not None:
        in_specs.append(pl.BlockSpec((None, 1, Nb), lambda g, i: (g, 0, 0)))
        args.append(bias3)
    return _call(
        None, body, grid=(G, M // tm), in_specs=in_specs, out_specs=pl.BlockSpec(oblock, omap),
        out_shape=_sds(out_shape, out_dtype), compiler_params=_params(("arbitrary", "arbitrary")), name=name)(*args)


def _mm_tn(a3, b3, G, amap, bmap, ablock, bblock, out_shape, oblock, omap, name="mm_tn"):
    S = a3.shape[1]

    def body(a_ref, b_ref, o_ref):
        o_ref[...] = _dot_tn(a_ref[...], b_ref[...]).astype(o_ref.dtype)

    return _call(
        None, body, grid=(G, 1),
        in_specs=[pl.BlockSpec(ablock(S), amap), pl.BlockSpec(bblock(S), bmap)],
        out_specs=pl.BlockSpec(oblock, omap), out_shape=_sds(out_shape, CDT),
        compiler_params=_params(("arbitrary", "arbitrary")), name=name)(a3, b3)


def _wgrad(a3, b3, name):
    Ga, S, M = a3.shape
    Gb, _, N = b3.shape
    G = max(Ga, Gb)
    return _mm_tn(
        a3, b3, G,
        (lambda g, k: (g, k, 0)) if Ga > 1 else (lambda g, k: (0, k, 0)),
        (lambda g, k: (g, k, 0)) if Gb > 1 else (lambda g, k: (0, k, 0)),
        lambda tk: (None, tk, M), lambda tk: (None, tk, N),
        (G, M, N), (None, M, N), lambda g, k: (g, 0, 0), name=name)


def _mixout_ln(u3, w3, bias, xin, gain, beta, name):
    G, S, Kb = u3.shape
    tm = _tile(S)

    def body(u_ref, w_ref, b_ref, x_ref, g_ref, be_ref, z_ref, xo_ref, xob_ref):
        h = b_ref[...] + _dot(u_ref[0], w_ref[0])
        for g in range(1, G):
            h = h + _dot(u_ref[g], w_ref[g])
        z = ALPHA * x_ref[...] + h
        z_ref[...] = z
        y = _ln_fwd(z, g_ref[...], be_ref[...])
        xo_ref[...] = y
        xob_ref[...] = y.astype(CDT)

    row = pl.BlockSpec((tm, D), lambda i: (i, 0))
    vec = pl.BlockSpec((1, D), lambda i: (0, 0))
    return _call(
        None, body, grid=(S // tm,),
        in_specs=[pl.BlockSpec((G, tm, Kb), lambda i: (0, i, 0)), pl.BlockSpec((G, Kb, D), lambda i: (0, 0, 0)),
                  vec, row, vec, vec],
        out_specs=[row, row, row], out_shape=[_sds((S, D), F32), _sds((S, D), F32), _sds((S, D), CDT)],
        compiler_params=_params(("arbitrary",)), name=name)(u3, w3, bias, xin, gain, beta)


def _ffn_fwd(xin, xin_b, wgu, wdn, gain, beta, name):
    S = xin.shape[0]
    tm = _tile(S)

    def hidden(xb_ref, wgu_ref, gu_ref, hid_ref):
        xb = xb_ref[...]
        gate = _dot_nt(xb, wgu_ref[0])
        up = _dot_nt(xb, wgu_ref[1])
        gu_ref[0] = gate.astype(CDT)
        gu_ref[1] = up.astype(CDT)
        hid_ref[...] = (gate * _sigmoid(gate) * up).astype(CDT)

    gu, hid = _call(
        None, hidden, grid=(4, S // tm),
        in_specs=[pl.BlockSpec((tm, D), lambda j, i: (i, 0)), pl.BlockSpec((2, None, FFN_B, D), lambda j, i: (0, j, 0, 0))],
        out_specs=[pl.BlockSpec((2, None, tm, FFN_B), lambda j, i: (0, j, i, 0)),
                   pl.BlockSpec((None, tm, FFN_B), lambda j, i: (j, i, 0))],
        out_shape=[_sds((2, 4, S, FFN_B), CDT), _sds((4, S, FFN_B), CDT)],
        compiler_params=_params(("arbitrary", "arbitrary")), name=name + "_hidden")(xin_b, wgu)

    def down(x_ref, hid_ref, wdn_ref, g_ref, be_ref, z_ref, xo_ref, xob_ref):
        z = ALPHA * x_ref[...]
        for j in range(4):
            z = z + _dot(hid_ref[j], wdn_ref[j])
        z_ref[...] = z
        y = _ln_fwd(z, g_ref[...], be_ref[...])
        xo_ref[...] = y
        xob_ref[...] = y.astype(CDT)

    row = pl.BlockSpec((tm, D), lambda i: (i, 0))
    vec = pl.BlockSpec((1, D), lambda i: (0, 0))
    z, xo, xob = _call(
        None, down, grid=(S // tm,),
        in_specs=[row, pl.BlockSpec((4, tm, FFN_B), lambda i: (0, i, 0)), pl.BlockSpec((4, FFN_B, D), lambda i: (0, 0, 0)),
                  vec, vec],
        out_specs=[row, row, row], out_shape=[_sds((S, D), F32), _sds((S, D), F32), _sds((S, D), CDT)],
        compiler_params=_params(("arbitrary",)), name=name + "_down")(xin, hid, wdn, gain, beta)
    return gu, hid, z, xo, xob


def _ple_fwd(xin, xin_b, p_b, wpg, bgate, wpu, gain, beta, name):
    S = xin.shape[0]
    tm = _tile(S)

    def body(x_ref, xb_ref, p_ref, wpg_ref, bg_ref, wpu_ref, g_ref, be_ref, sg_ref, up_ref, z_ref, xo_ref, xob_ref):
        sg = _sigmoid(_dot(xb_ref[...], wpg_ref[...]) + bg_ref[...])
        pb = p_ref[...]
        up = jnp.concatenate([_dot(pb, wpu_ref[j]) for j in range(N_DEV)], axis=-1)
        sg_ref[...] = sg.astype(CDT)
        up_ref[...] = up.astype(CDT)
        z = ALPHA * x_ref[...] + sg * up
        z_ref[...] = z
        y = _ln_fwd(z, g_ref[...], be_ref[...])
        xo_ref[...] = y
        xob_ref[...] = y.astype(CDT)

    row = pl.BlockSpec((tm, D), lambda i: (i, 0))
    vec = pl.BlockSpec((1, D), lambda i: (0, 0))
    return _call(
        None, body, grid=(S // tm,),
        in_specs=[row, row, pl.BlockSpec((tm, PLE_DIM), lambda i: (i, 0)), pl.BlockSpec((D, D), lambda i: (0, 0)), vec,
                  pl.BlockSpec((N_DEV, PLE_DIM, D // N_DEV), lambda i: (0, 0, 0)), vec, vec],
        out_specs=[row] * 5,
        out_shape=[_sds((S, D), CDT)] * 2 + [_sds((S, D), F32)] * 2 + [_sds((S, D), CDT)],
        compiler_params=_params(("arbitrary",)), name=name)(xin, xin_b, p_b, wpg, bgate, wpu, gain, beta)


def _loss_fwd_bwd(y, target):
    S = y.shape[0]
    tm = _tile(S)

    def body(y_ref, t_ref, l_ref, dy_ref):
        e = y_ref[...] - t_ref[...]
        dy_ref[...] = e * (1.0 / D)
        part = 0.5 * jnp.sum(jnp.sum(e * e, axis=-1, keepdims=True) * (1.0 / D), axis=0, keepdims=True)
        _acc(l_ref, jnp.broadcast_to(part, l_ref.shape), pl.program_id(0) == 0)

    row = pl.BlockSpec((tm, D), lambda i: (i, 0))
    return _call(
        None, body, grid=(S // tm,), in_specs=[row, row],
        out_specs=[pl.BlockSpec((1, 128), lambda i: (0, 0)), row],
        out_shape=[_sds((1, 128), F32), _sds((S, D), F32)],
        compiler_params=_params(("arbitrary",)), name="loss")(y, target)


def _ple_bwd(dy, z, sg, up, gain, wpg, name, after=None):
    S = dy.shape[0]
    tm = _tile(S)

    def body(dy_ref, z_ref, sg_ref, up_ref, g_ref, wpg_ref, dx_ref, dgl_ref, dup_ref, dgain_ref, dbeta_ref, dbg_ref):
        first = pl.program_id(0) == 0
        dy_ = dy_ref[...]
        dz, xhat = _ln_bwd(z_ref[...], g_ref[...], dy_)
        sg_ = sg_ref[...].astype(F32)
        dgl = dz * up_ref[...].astype(F32) * sg_ * (1.0 - sg_)
        dgl_ref[...] = dgl.astype(CDT)
        dup_ref[...] = (dz * sg_).astype(CDT)
        dx_ref[...] = ALPHA * dz + _dot_nt(dgl, wpg_ref[...])
        _acc(dgain_ref, _colsum(dy_ * xhat), first)
        _acc(dbeta_ref, _colsum(dy_), first)
        _acc(dbg_ref, _colsum(dgl), first)

    row = pl.BlockSpec((tm, D), lambda i: (i, 0))
    vec = pl.BlockSpec((1, D), lambda i: (0, 0))
    return _call(
        after, body, grid=(S // tm,), in_specs=[row, row, row, row, vec, pl.BlockSpec((D, D), lambda i: (0, 0))],
        out_specs=[row, row, row, vec, vec, vec],
        out_shape=[_sds((S, D), F32), _sds((S, D), CDT), _sds((S, D), CDT)] + [_sds((1, D), F32)] * 3,
        compiler_params=_params(("arbitrary",)), name=name)(dy, z, sg, up, gain, wpg)


def _ffn_bwd_hidden(dy, z, gu, wdn, gain, name, after=None):
    S = dy.shape[0]
    tm = _tile(S)

    def hidden(dy_ref, z_ref, gu_ref, wdn_ref, g_ref, dz_ref, dzb_ref, dgu_ref, dgain_ref, dbeta_ref):
        i, j = pl.program_id(0), pl.program_id(1)

        @pl.when(j == 0)
        def _():
            dy_ = dy_ref[...]
            dz, xhat = _ln_bwd(z_ref[...], g_ref[...], dy_)
            dz_ref[...] = dz
            dzb_ref[...] = dz.astype(CDT)
            _acc(dgain_ref, _colsum(dy_ * xhat), i == 0)
            _acc(dbeta_ref, _colsum(dy_), i == 0)

        dhid = _dot_nt(dzb_ref[...], wdn_ref[...])
        gate, up = gu_ref[0].astype(F32), gu_ref[1].astype(F32)
        sg = _sigmoid(gate)
        dgu_ref[0] = (dhid * up * (sg * (1.0 + gate * (1.0 - sg)))).astype(CDT)
        dgu_ref[1] = (dhid * (gate * sg)).astype(CDT)

    row = pl.BlockSpec((tm, D), lambda i, j: (i, 0))
    vec = pl.BlockSpec((1, D), lambda i, j: (0, 0))
    return _call(
        after, hidden, grid=(S // tm, 4),
        in_specs=[row, row, pl.BlockSpec((2, None, tm, FFN_B), lambda i, j: (0, j, i, 0)),
                  pl.BlockSpec((None, FFN_B, D), lambda i, j: (j, 0, 0)), vec],
        out_specs=[row, row, pl.BlockSpec((2, None, tm, FFN_B), lambda i, j: (0, j, i, 0)), vec, vec],
        out_shape=[_sds((S, D), F32), _sds((S, D), CDT), _sds((2, 4, S, FFN_B), CDT), _sds((1, D), F32),
                   _sds((1, D), F32)],
        compiler_params=_params(("arbitrary", "arbitrary")), name=name + "_hidden")(dy, z, gu, wdn, gain)


def _ffn_bwd_input(dz, dgu, wgu, name, after=None):
    S = dz.shape[0]
    tm = _tile(S)

    def to_input(dz_ref, dgu_ref, wgu_ref, dx_ref):
        acc = ALPHA * dz_ref[...]
        for g in range(2):
            for j in range(4):
                acc = acc + _dot(dgu_ref[g, j], wgu_ref[g, j])
        dx_ref[...] = acc

    rows = pl.BlockSpec((tm, D), lambda i: (i, 0))
    return _call(
        after, to_input, grid=(S // tm,),
        in_specs=[rows, pl.BlockSpec((2, 4, tm, FFN_B), lambda i: (0, 0, i, 0)),
                  pl.BlockSpec((2, 4, FFN_B, D), lambda i: (0, 0, 0, 0))],
        out_specs=rows, out_shape=_sds((S, D), F32),
        compiler_params=_params(("arbitrary",)), name=name + "_input")(dz, dgu, wgu)


def _mixout_bwd(dy, z, gain, w3, du_dtype, name, after=None):
    S = dy.shape[0]
    G, Kb, _ = w3.shape
    tm = _tile(S)

    def body(dy_ref, z_ref, g_ref, w_ref, dz_ref, dzb_ref, du_ref, dgain_ref, dbeta_ref, dbias_ref):
        first = pl.program_id(0) == 0
        dy_ = dy_ref[...]
        dz, xhat = _ln_bwd(z_ref[...], g_ref[...], dy_)
        dz_ref[...] = dz
        dzb = dz.astype(CDT)
        dzb_ref[...] = dzb
        for g in range(G):
            du_ref[g] = _dot_nt(dzb, w_ref[g]).astype(du_ref.dtype)
        _acc(dgain_ref, _colsum(dy_ * xhat), first)
        _acc(dbeta_ref, _colsum(dy_), first)
        _acc(dbias_ref, _colsum(dz), first)

    row = pl.BlockSpec((tm, D), lambda i: (i, 0))
    vec = pl.BlockSpec((1, D), lambda i: (0, 0))
    return _call(
        after, body, grid=(S // tm,), in_specs=[row, row, vec, pl.BlockSpec((G, Kb, D), lambda i: (0, 0, 0))],
        out_specs=[row, row, pl.BlockSpec((G, tm, Kb), lambda i: (0, i, 0)), vec, vec, vec],
        out_shape=[_sds((S, D), F32), _sds((S, D), CDT), _sds((G, S, Kb), du_dtype)] + [_sds((1, D), F32)] * 3,
        compiler_params=_params(("arbitrary",)), name=name)(dy, z, gain, w3)


def _half_select(low):
    r = lax.broadcasted_iota(jnp.int32, (2 * ATT_HD, ATT_HD), 0)
    c = lax.broadcasted_iota(jnp.int32, (2 * ATT_HD, ATT_HD), 1)
    return (r == c + (0 if low else ATT_HD)).astype(CDT)


def _half_place(low):
    r = lax.broadcasted_iota(jnp.int32, (ATT_HD, 2 * ATT_HD), 0)
    c = lax.broadcasted_iota(jnp.int32, (ATT_HD, 2 * ATT_HD), 1)
    return (c == r + (0 if low else ATT_HD)).astype(CDT)


def _pair_lanes(even, odd):
    return (jnp.dot(even, _half_place(True), preferred_element_type=F32)
            + jnp.dot(odd, _half_place(False), preferred_element_type=F32)).astype(CDT)


def _proj_heads(a, w, bias, heads, name):
    S, K = a.shape
    N = heads * ATT_HD
    tm = _tile(S)

    def body(a_ref, w_ref, b_ref, o_ref):
        acc = (_dot(a_ref[...], w_ref[...]) + b_ref[...]).astype(CDT)
        sel = (_half_select(True), _half_select(False))
        for h in range(heads):
            pair = acc[:, (h // 2) * 2 * ATT_HD:(h // 2 + 1) * 2 * ATT_HD]
            o_ref[h] = jnp.dot(pair, sel[h % 2], preferred_element_type=F32).astype(CDT)

    return _call(
        None, body, grid=(S // tm,),
        in_specs=[pl.BlockSpec((tm, K), lambda i: (i, 0)), pl.BlockSpec((K, N), lambda i: (0, 0)),
                  pl.BlockSpec((1, N), lambda i: (0, 0))],
        out_specs=pl.BlockSpec((heads, tm, ATT_HD), lambda i: (0, i, 0)), out_shape=_sds((heads, S, ATT_HD), CDT),
        compiler_params=_params(("arbitrary",)), name=name)(a, w, bias)


def _qkv_bwd(dz, dq, dkv4, wq, wkv, name, after=None):
    S = dz.shape[0]
    tm = _tile(S)
    HK = dkv4.shape[0]
    NK = HK * ATT_HD

    def body(dz_ref, dq_ref, dkv_ref, wq_ref, wkv_ref, dx_ref, dkvn_ref, dkvb_ref):
        first = pl.program_id(0) == 0
        dkvn = jnp.concatenate([_pair_lanes(dkv_ref[2 * i].astype(CDT), dkv_ref[2 * i + 1].astype(CDT))
                                for i in range(HK // 2)], axis=-1)
        dkvn_ref[...] = dkvn
        dx_ref[...] = ALPHA * dz_ref[...] + _dot_nt(dq_ref[...], wq_ref[...]) + _dot_nt(dkvn, wkv_ref[...])
        for h in range(HK):
            _acc(dkvb_ref.at[h], _colsum(dkv_ref[h]), first)

    row = pl.BlockSpec((tm, D), lambda i: (i, 0))
    return _call(
        after, body, grid=(S // tm,),
        in_specs=[row, row, pl.BlockSpec((HK, tm, ATT_HD), lambda i: (0, i, 0)),
                  pl.BlockSpec((D, D), lambda i: (0, 0)), pl.BlockSpec((D, NK), lambda i: (0, 0))],
        out_specs=[row, pl.BlockSpec((tm, NK), lambda i: (i, 0)), pl.BlockSpec((HK, 1, ATT_HD), lambda i: (0, 0, 0))],
        out_shape=[_sds((S, D), F32), _sds((S, NK), CDT), _sds((HK, 1, ATT_HD), F32)],
        compiler_params=_params(("arbitrary",)), name=name)(dz, dq, dkv4, wq, wkv)


def _inproj_bwd(dz, dproj, wain, name, after=None):
    S = dz.shape[0]
    tm = _tile(S)
    nb = wain.shape[-1]

    def body(dz_ref, dp_ref, w_ref, dx_ref):
        acc = ALPHA * dz_ref[...]
        for j in range(N_DEV):
            acc = acc + _dot_nt(dp_ref[j // 2, :, pl.ds((j % 2) * nb, nb)], w_ref[j])
        dx_ref[...] = acc

    row = pl.BlockSpec((tm, D), lambda i: (i, 0))
    return _call(
        after, body, grid=(S // tm,),
        in_specs=[row, pl.BlockSpec((4, tm, D), lambda i: (0, i, 0)), pl.BlockSpec((N_DEV, D, nb), lambda i: (0, 0, 0))],
        out_specs=row, out_shape=_sds((S, D), F32),
        compiler_params=_params(("arbitrary",)), name=name)(dz, dproj, wain)


def _running_sum(x, reverse=False):
    rows = x.shape[0]
    row = lax.broadcasted_iota(jnp.int32, x.shape, 0)
    step = 1
    while step < rows:
        if reverse:
            x = x + jnp.where(row < rows - step, pltpu.roll(x, rows - step, 0), 0.0)
        else:
            x = x + jnp.where(row >= step, pltpu.roll(x, step, 0), 0.0)
        step *= 2
    return x


def _hg_gates(q, f, alb_ref):
    a0, a1 = alb_ref[0:1, :], alb_ref[1:2, :]
    mx = jnp.maximum(a0, a1)
    e0, e1 = jnp.exp(a0 - mx), jnp.exp(a1 - mx)
    lb = e0 / (e0 + e1)
    sig = _sigmoid(f)
    forget = lb + (1.0 - lb) * sig
    k = (1.0 - lb) * _sigmoid(-f)
    qs = q * _sigmoid(q) * (HG_DK ** -0.5)
    return qs, k, jnp.log(forget), sig, lb, forget


def _hg_intra(qs, k, b, b_scr):
    b_scr[...] = b
    bm = b_scr[pl.ds(HG_CH // 2 - 1, 1), :]
    bl = b_scr[pl.ds(HG_CH - 1, 1), :]
    eb = jnp.exp(b)
    qb = qs * eb
    e_q = jnp.exp(b - bm)
    e_k = jnp.exp(bm - b)
    e_d = jnp.exp(bl - b)
    return qb, qs * e_q, k * e_k, k * e_d, jnp.exp(bl), eb, e_q, e_k, e_d


def _hgrn_fwd(proj, alb, ngain):
    S = proj.shape[1]
    nc = S // HG_CH
    nb = nc // HG_CPB
    rb, wb = HG_CPB * HG_CH, HG_HPB * HG_DK

    def body(pj_ref, alb_ref, ng_ref, o_ref, y_ref, st_ref, st_scr, b_scr):
        n = pl.program_id(1)

        @pl.when(n == 0)
        def _():
            st_scr[...] = jnp.zeros_like(st_scr)

        r = lax.broadcasted_iota(jnp.int32, (HG_CH, HG_CH), 0)
        c = lax.broadcasted_iota(jnp.int32, (HG_CH, HG_CH), 1)
        causal = r >= c
        for ci, j in [(ci, j) for ci in range(HG_CPB) for j in range(HG_HPB)]:
            rows, lanes = pl.ds(ci * HG_CH, HG_CH), pl.ds(j * HG_DK, HG_DK)
            q, f, v, g = pj_ref[0, rows, lanes], pj_ref[1, rows, lanes], pj_ref[2, rows, lanes], pj_ref[3, rows, lanes]
            qs, k, logf, _, _, _ = _hg_gates(q, f, alb_ref.at[:, lanes])
            b = _running_sum(logf)
            qb, qt, kt, kd, ebl, _, _, _, _ = _hg_intra(qs, k, b, b_scr.at[j, ci])
            st = st_scr[j]
            st_ref[j, ci] = st
            a = jnp.where(causal, _dot_nt(qt, kt), 0.0)
            o = _dot(a, v) + _dot_nt(qb, st)
            st_scr[j] = st * ebl + _dot_tn(v, kd)
            o_ref[rows, lanes] = o
            rinv = lax.rsqrt(jnp.mean(o * o, axis=-1, keepdims=True) + RMS_EPS)
            y_ref[rows, lanes] = (o * rinv * ng_ref[...] * (g * _sigmoid(g))).astype(CDT)

    blk = pl.BlockSpec((rb, wb), lambda h, n: (n, h))
    return _call(
        None, body, grid=(HG_H // HG_HPB, nb),
        in_specs=[pl.BlockSpec((4, rb, wb), lambda h, n: (0, n, h)), pl.BlockSpec((2, wb), lambda h, n: (0, h)),
                  pl.BlockSpec((1, HG_DK), lambda h, n: (0, 0))],
        out_specs=[blk, blk, pl.BlockSpec((HG_HPB, HG_CPB, HG_DK, HG_DK), lambda h, n: (h, n, 0, 0))],
        out_shape=[_sds((S, D), F32), _sds((S, D), CDT), _sds((HG_H, nc, HG_DK, HG_DK), F32)],
        scratch_shapes=[pltpu.VMEM((HG_HPB, HG_DK, HG_DK), F32), pltpu.VMEM((HG_HPB, HG_CPB, HG_CH, HG_DK), F32)],
        compiler_params=_params(("arbitrary", "arbitrary")), name="hgrn_fwd")(proj, alb, ngain)


def _hgrn_bwd(proj, alb, ngain, o, states, dy, after=None):
    S = proj.shape[1]
    nc = S // HG_CH
    nb = nc // HG_CPB
    rb, wb = HG_CPB * HG_CH, HG_HPB * HG_DK

    def body(pj_ref, alb_ref, ng_ref, o_ref, st_ref, dy_ref, dpj_ref, dalb_ref, dng_ref, dst_scr, b_scr):
        h, n = pl.program_id(0), pl.program_id(1)

        @pl.when(n == 0)
        def _():
            dst_scr[...] = jnp.zeros_like(dst_scr)
            dalb_ref[...] = jnp.zeros_like(dalb_ref)

        _zero_at(dng_ref, jnp.logical_and(h == 0, n == 0))
        ng = ng_ref[...]
        r = lax.broadcasted_iota(jnp.int32, (HG_CH, HG_CH), 0)
        c = lax.broadcasted_iota(jnp.int32, (HG_CH, HG_CH), 1)
        causal = r >= c
        dng = None
        for ci, j in [(ci, j) for ci in reversed(range(HG_CPB)) for j in range(HG_HPB)]:
            rows, lanes = pl.ds(ci * HG_CH, HG_CH), pl.ds(j * HG_DK, HG_DK)
            q, f, v, g = pj_ref[0, rows, lanes], pj_ref[1, rows, lanes], pj_ref[2, rows, lanes], pj_ref[3, rows, lanes]
            o_ = o_ref[rows, lanes]
            dy_ = dy_ref[rows, lanes]
            sg = _sigmoid(g)
            rinv = lax.rsqrt(jnp.mean(o_ * o_, axis=-1, keepdims=True) + RMS_EPS)
            nrm = o_ * rinv
            dr = dy_ * (g * sg)
            dg = dy_ * nrm * ng * (sg * (1.0 + g * (1.0 - sg)))
            dn = dr * ng
            do = rinv * (dn - nrm * jnp.mean(dn * nrm, axis=-1, keepdims=True))
            dng = _colsum(dr * nrm) if dng is None else dng + _colsum(dr * nrm)
            qs, k, logf, sig, lb, forget = _hg_gates(q, f, alb_ref.at[:, lanes])
            b = _running_sum(logf)
            qb, qt, kt, kd, ebl, eb, e_q, e_k, e_d = _hg_intra(qs, k, b, b_scr.at[j, ci])
            st = st_ref[j, ci]
            dstn = dst_scr[j]
            qt, kt, qb, kd = (t.astype(CDT).astype(F32) for t in (qt, kt, qb, kd))
            a = jnp.where(causal, _dot_nt(qt, kt), 0.0)
            da = jnp.where(causal, _dot_nt(do, v), 0.0)
            dv = _dot_tn(a, do) + _dot_nt(kd, dstn)
            dqb = _dot(do, st)
            dkd = _dot(v, dstn)
            dqt = _dot(da, kt)
            dkt = _dot_tn(da, qt)
            dbl = _colsum(dkd * kd) + ebl * _colsum(dstn * st)
            dst_scr[j] = dstn * ebl + _dot_tn(do, qb)
            dqs = dqt * e_q + dqb * eb
            dk = dkt * e_k + dkd * e_d
            db = dqt * qt + dqb * qb - dkt * kt - dkd * kd
            dlogf = _running_sum(db, reverse=True) + dbl
            dforget = dlogf / forget
            dsig = (1.0 - lb) * (dforget - dk)
            df = dsig * sig * (1.0 - sig)
            dlb = _colsum((dforget - dk) * (1.0 - sig))
            sq = _sigmoid(q)
            dq = dqs * (HG_DK ** -0.5) * (sq * (1.0 + q * (1.0 - sq)))
            dpj_ref[0, rows, lanes] = dq.astype(CDT)
            dpj_ref[1, rows, lanes] = df.astype(CDT)
            dpj_ref[2, rows, lanes] = dv.astype(CDT)
            dpj_ref[3, rows, lanes] = dg.astype(CDT)
            da0 = dlb * lb * (1.0 - lb)
            dalb_ref[pl.ds(0, 1), lanes] += da0
            dalb_ref[pl.ds(1, 1), lanes] -= da0
        dng_ref[...] += dng

    blk = pl.BlockSpec((rb, wb), lambda h, n: (nb - 1 - n, h))
    pj = pl.BlockSpec((4, rb, wb), lambda h, n: (0, nb - 1 - n, h))
    alb_blk = pl.BlockSpec((2, wb), lambda h, n: (0, h))
    ng_blk = pl.BlockSpec((1, HG_DK), lambda h, n: (0, 0))
    return _call(
        after, body, grid=(HG_H // HG_HPB, nb),
        in_specs=[pj, alb_blk, ng_blk, blk,
                  pl.BlockSpec((HG_HPB, HG_CPB, HG_DK, HG_DK), lambda h, n: (h, nb - 1 - n, 0, 0)), blk],
        out_specs=[pj, alb_blk, ng_blk],
        out_shape=[_sds((4, S, D), CDT), _sds((2, D), F32), _sds((1, HG_DK), F32)],
        scratch_shapes=[pltpu.VMEM((HG_HPB, HG_DK, HG_DK), F32), pltpu.VMEM((HG_HPB, HG_CPB, HG_CH, HG_DK), F32)],
        compiler_params=_params(("arbitrary", "arbitrary")), name="hgrn_bwd")(proj, alb, ngain, o, states, dy)


def _slope(h):
    return 2.0 ** (-8.0 * (h + 1) / ATT_QH)


def _attn_mask(n):
    qi = lax.broadcasted_iota(jnp.int32, (WINDOW, 2 * WINDOW), 0)
    si = lax.broadcasted_iota(jnp.int32, (WINDOW, 2 * WINDOW), 1)
    dist = qi - si + WINDOW
    valid = (dist >= 0) & (dist < WINDOW) & (n * WINDOW - WINDOW + si >= 0)
    return valid, dist.astype(F32)


def _attn_probs(qh, kh, sink, slope, valid, distf):
    s = _dot_nt(qh, kh) * (ATT_HD ** -0.5) - slope * distf
    s = jnp.where(valid, s, NEG)
    m = jnp.maximum(jnp.max(s, axis=-1, keepdims=True), sink)
    e = jnp.exp(s - m)
    es = jnp.exp(sink - m)
    inv = 1.0 / (jnp.sum(e, axis=-1, keepdims=True) + es)
    return e * inv, es * inv


def _attn_specs(S):
    steps = S // (ATT_BPB * WINDOW)
    cur = lambda H: pl.BlockSpec((H, ATT_BPB * WINDOW, ATT_HD), lambda n: (0, n, 0))
    prev = lambda H: pl.BlockSpec((H, WINDOW, ATT_HD), lambda n: (0, jnp.maximum(ATT_BPB * n - 1, 0), 0))
    return steps, cur, prev


def _attn_kv(kvc_ref, kvp_ref, head, bi):
    before = kvp_ref[head] if bi == 0 else kvc_ref[head, pl.ds((bi - 1) * WINDOW, WINDOW), :]
    return jnp.concatenate([before, kvc_ref[head, pl.ds(bi * WINDOW, WINDOW), :]], axis=0)


def _attn_fwd(q4, kv4, sinks):
    S = q4.shape[1]
    nb, cur, prev = _attn_specs(S)

    def body(sink_ref, q_ref, kvc_ref, kvp_ref, o_ref):
        for bi, kvh in [(bi, kvh) for bi in range(ATT_BPB) for kvh in range(ATT_KVH)]:
            valid, distf = _attn_mask(pl.program_id(0) * ATT_BPB + bi)
            rows = pl.ds(bi * WINDOW, WINDOW)
            kh = _attn_kv(kvc_ref, kvp_ref, kvh, bi)
            vh = _attn_kv(kvc_ref, kvp_ref, ATT_KVH + kvh, bi)
            v_low = jnp.dot(vh, _half_place(True), preferred_element_type=F32).astype(CDT)
            v_high = jnp.dot(vh, _half_place(False), preferred_element_type=F32).astype(CDT)
            for h in range(kvh * ATT_G, (kvh + 1) * ATT_G, 2):
                p_even, _ = _attn_probs(q_ref[h, rows, :], kh, sink_ref[0, h], _slope(h), valid, distf)
                p_odd, _ = _attn_probs(q_ref[h + 1, rows, :], kh, sink_ref[0, h + 1], _slope(h + 1), valid, distf)
                o_ref[rows, pl.ds(h * ATT_HD, 2 * ATT_HD)] = (_dot(p_even, v_low) + _dot(p_odd, v_high)).astype(CDT)

    rows_spec = pl.BlockSpec((ATT_BPB * WINDOW, D), lambda n: (n, 0))
    return _call(
        None, body, grid=(nb,),
        in_specs=[pl.BlockSpec(memory_space=pltpu.SMEM), cur(ATT_QH), cur(2 * ATT_KVH), prev(2 * ATT_KVH)],
        out_specs=rows_spec, out_shape=_sds((S, D), CDT),
        compiler_params=_params(("arbitrary",)), name="attn_fwd")(sinks, q4, kv4, kv4)


def _attn_bwd(q4, kv4, sinks, do):
    S = q4.shape[1]
    nb, cur, prev = _attn_specs(S)

    def body(sink_ref, q_ref, kvc_ref, kvp_ref, do_ref, dq_ref, dkv_ref, dbq_ref, dsink_ref):
        n = pl.program_id(0)
        first = n == 0

        @pl.when(first)
        def _():
            dkv_ref[...] = jnp.zeros_like(dkv_ref)
            dsink_ref[...] = jnp.zeros_like(dsink_ref)
            dbq_ref[...] = jnp.zeros_like(dbq_ref)

        lane = lax.broadcasted_iota(jnp.int32, (1, 128), 1)
        dsinks = jnp.zeros((1, 128), F32)
        sel = (_half_select(True), _half_select(False))
        for bi, kvh in [(bi, kvh) for bi in range(ATT_BPB) for kvh in range(ATT_KVH)]:
            block = n * ATT_BPB + bi
            valid, distf = _attn_mask(block)
            rows = pl.ds(bi * WINDOW, WINDOW)
            rows_cur = pl.ds(pl.multiple_of(block * WINDOW, WINDOW), WINDOW)
            rows_prev = pl.ds(pl.multiple_of(jnp.maximum(block - 1, 0) * WINDOW, WINDOW), WINDOW)
            kh = _attn_kv(kvc_ref, kvp_ref, kvh, bi)
            vh = _attn_kv(kvc_ref, kvp_ref, ATT_KVH + kvh, bi)
            dk = dv = None
            dqs = []
            for h in range(kvh * ATT_G, (kvh + 1) * ATT_G):
                qh = q_ref[h, rows, :]
                doh = jnp.dot(do_ref[rows, pl.ds((h // 2) * 2 * ATT_HD, 2 * ATT_HD)], sel[h % 2],
                              preferred_element_type=F32).astype(CDT)
                p, ps = _attn_probs(qh, kh, sink_ref[0, h], _slope(h), valid, distf)
                dp = _dot_nt(doh, vh)
                dd = jnp.sum(p * dp, axis=-1, keepdims=True)
                ds = p * (dp - dd)
                dsinks = dsinks + jnp.where(lane == h, -jnp.sum(ps * dd, axis=0, keepdims=True), 0.0)
                dqh = _dot(ds, kh) * (ATT_HD ** -0.5)
                dqs.append(dqh.astype(CDT))
                dbq_ref[h] += _colsum(dqh)
                dkh = _dot_tn(ds, qh) * (ATT_HD ** -0.5)
                dvh = _dot_tn(p, doh)
                dk = dkh if dk is None else dk + dkh
                dv = dvh if dv is None else dv + dvh
            for i in range(ATT_G // 2):
                lanes = pl.ds((kvh * ATT_G + 2 * i) * ATT_HD, 2 * ATT_HD)
                dq_ref[rows, lanes] = _pair_lanes(dqs[2 * i], dqs[2 * i + 1])
            dkv_ref[kvh, rows_prev, :] += dk[:WINDOW]
            dkv_ref[kvh, rows_cur, :] += dk[WINDOW:]
            dkv_ref[ATT_KVH + kvh, rows_prev, :] += dv[:WINDOW]
            dkv_ref[ATT_KVH + kvh, rows_cur, :] += dv[WINDOW:]
        dsink_ref[...] += dsinks

    rows_spec = pl.BlockSpec((ATT_BPB * WINDOW, D), lambda n: (n, 0))
    return _call(
        None, body, grid=(nb,),
        in_specs=[pl.BlockSpec(memory_space=pltpu.SMEM), cur(ATT_QH), cur(2 * ATT_KVH), prev(2 * ATT_KVH), rows_spec],
        out_specs=[rows_spec, pl.BlockSpec((2 * ATT_KVH, S, ATT_HD), lambda n: (0, 0, 0)),
                   pl.BlockSpec((ATT_QH, 1, ATT_HD), lambda n: (0, 0, 0)), pl.BlockSpec((1, 128), lambda n: (0, 0))],
        out_shape=[_sds((S, D), CDT), _sds((2 * ATT_KVH, S, ATT_HD), F32), _sds((ATT_QH, 1, ATT_HD), F32),
                   _sds((1, 128), F32)],
        compiler_params=_params(("arbitrary",)), name="attn_bwd")(sinks, q4, kv4, kv4, do)


def _local_step(x, p, target, getw, sm, emit):
    S = x.shape[0]
    vec = lambda a: a.reshape(1, -1)
    ln_g = lambda l, k: vec(sm["ln_gain"][l, k])
    ln_b = lambda l, k: vec(sm["ln_bias"][l, k])
    xb = x.astype(CDT)
    pb = p.astype(CDT)

    proj = _mm_nn(xb, getw("a_w_in"), (4, S, D), (None, _tile(S), 512), lambda g, i: (g // 2, i, g % 2), F32,
                  name="a_in")
    o_a, y_a, states = _hgrn_fwd(proj, sm["a_lower_bound"], sm["a_norm_gain"])
    zeros = jnp.zeros((1, D), F32)
    z = [[None] * 3 for _ in range(2)]
    xs = [[None] * 3 for _ in range(2)]
    xbs = [[None] * 3 for _ in range(2)]
    z[0][0], xs[0][0], xbs[0][0] = _mixout_ln(y_a[None], getw("a_w_out")[None], zeros, x, ln_g(0, 0), ln_b(0, 0),
                                              "a_out_ln")
    gu, hid, sgs, ups = [None, None], [None, None], [None, None], [None, None]

    def ffn_ple(l):
        wgu = getw(f"gu{l}")
        gu[l], hid[l], z[l][1], xs[l][1], xbs[l][1] = _ffn_fwd(
            xs[l][0], xbs[l][0], wgu, getw(f"dn{l}"), ln_g(l, 1), ln_b(l, 1), f"ffn_fwd{l}")
        sgs[l], ups[l], z[l][2], xs[l][2], xbs[l][2] = _ple_fwd(
            xs[l][1], xbs[l][1], pb[l], getw(f"pg{l}"), vec(sm["ple_b_gate"][l]), getw(f"pu{l}"), ln_g(l, 2),
            ln_b(l, 2), f"ple_fwd{l}")

    ffn_ple(0)
    x3, x3b = xs[0][2], xbs[0][2]
    w_kv, w_q, w_bo = getw("kv_w"), getw("b_w_q"), getw("b_w_out")
    kv4 = _proj_heads(x3b, w_kv, vec(sm["kv_b"]), 2 * ATT_KVH, "kv_proj")
    q4 = _proj_heads(x3b, w_q, vec(sm["b_b_q"]), ATT_QH, "q_proj")
    o_b = _attn_fwd(q4, kv4, sm["b_sinks"])
    z[1][0], xs[1][0], xbs[1][0] = _mixout_ln(o_b[None], w_bo[None], sm["b_b_out"], x3, ln_g(1, 0), ln_b(1, 0),
                                              "b_out_ln")
    ffn_ple(1)
    loss, dy = _loss_fwd_bwd(xs[1][2], target)

    gs = {}
    d_ln_g = [[None] * 3 for _ in range(2)]
    d_ln_b = [[None] * 3 for _ in range(2)]
    g_bg = [None, None]

    def ffn_ple_bwd(l, dy, after=None):
        dx2, dgl, dup, d_ln_g[l][2], d_ln_b[l][2], g_bg[l] = _ple_bwd(dy, z[l][2], sgs[l], ups[l], ln_g(l, 2),
                                                                     getw(f"pg{l}"), f"ple_bwd{l}", after=after)
        g_pg = _wgrad(xbs[l][1][None], dgl[None], f"g_ple_gate{l}")[0]
        g_pu = _wgrad(pb[l][None], dup[None], f"g_ple_up{l}")[0]
        tok = emit({f"pg{l}": g_pg, f"pu{l}": g_pu})
        dz2, dzb, dgu, d_ln_g[l][1], d_ln_b[l][1] = _ffn_bwd_hidden(dx2, z[l][1], gu[l], getw(f"dn{l}"), ln_g(l, 1),
                                                                   f"ffn_bwd{l}", after=tok)
        tok = emit({f"dn{l}": _wgrad(hid[l], dzb[None], f"g_ffn_down{l}")})
        g_gu = _wgrad(dgu.reshape(8, S, FFN_B), xbs[l][0][None], f"g_ffn_gate_up{l}")
        tok = tok + emit({f"gu{l}": g_gu})
        dx1 = _ffn_bwd_input(dz2, dgu, getw(f"gu{l}"), f"ffn_bwd{l}", after=tok)
        return dx1, None

    dx1, tok = ffn_ple_bwd(1, dy)
    dz, dzb, do, d_ln_g[1][0], d_ln_b[1][0], gs["b_b_out"] = _mixout_bwd(dx1, z[1][0], ln_g(1, 0), w_bo[None], CDT,
                                                                        "b_out_bwd", after=tok)
    g_bo = _wgrad(o_b[None], dzb[None], "g_b_w_out")[0]
    dq, dkv4, dbq, dsinks = _attn_bwd(q4, kv4, sm["b_sinks"], do[0])
    gs["b_b_q"] = dbq
    gs["b_sinks"] = dsinks
    g_q = _wgrad(x3b[None], dq[None], "g_b_w_q")[0]
    dx3, dkv, gs["kv_b"] = _qkv_bwd(dz, dq, dkv4, w_q, w_kv, "qkv_bwd", after=tok)
    g_kv = _wgrad(x3b[None], dkv[None], "g_kv_w")[0]
    tok = emit({"b_w_out": g_bo, "b_w_q": g_q, "kv_w": g_kv})
    dx1, tok = ffn_ple_bwd(0, dx3, tok)
    w_ao = getw("a_w_out")
    dz, dzb, dyr, d_ln_g[0][0], d_ln_b[0][0], _ = _mixout_bwd(dx1, z[0][0], ln_g(0, 0), w_ao[None], F32, "a_out_bwd",
                                                              after=tok)
    g_ao = _wgrad(y_a[None], dzb[None], "g_a_w_out")[0]
    tok = emit({"a_w_out": g_ao})
    dproj, gs["a_lower_bound"], gs["a_norm_gain"] = _hgrn_bwd(proj, sm["a_lower_bound"], sm["a_norm_gain"], o_a, states,
                                                              dyr[0], after=tok)
    tk = lambda t: (None, t, D)
    g_ain = _mm_tn(xb[None], dproj, N_DEV, lambda g, k: (0, k, 0), lambda g, k: (g // 2, k, g % 2),
                   tk, lambda t: (None, t, 512), (N_DEV, D, 512), (None, D, 512), lambda g, k: (g, 0, 0), name="g_a_w_in")
    gs["ple_b_gate"] = jnp.concatenate(g_bg, axis=0)
    gs["ln_gain"] = jnp.stack([jnp.concatenate(r, axis=0) for r in d_ln_g])
    gs["ln_bias"] = jnp.stack([jnp.concatenate(r, axis=0) for r in d_ln_b])
    gs["loss"] = loss
    tok = emit({"a_w_in": g_ain}, small=gs)
    grad_x = _inproj_bwd(dz, dproj, getw("a_w_in"), "a_in_bwd", after=tok)
    return loss, grad_x, gs


def _peer(k):
    x, y, c = lax.axis_index("x"), lax.axis_index("y"), lax.axis_index("c")
    px = 1 - x if k & 4 else x
    py = 1 - y if k & 2 else y
    pc = 1 - c if k & 1 else c
    return (px, py, pc), 4 * px + 2 * py + pc


def _my_index():
    return 4 * lax.axis_index("x") + 2 * lax.axis_index("y") + lax.axis_index("c")


def _piece_copy(mode, src, land, send_sems, recv_sems, t, k, sender, receiver, peer):
    return pltpu.make_async_remote_copy(
        src_ref=src if mode == "gather" else src.at[receiver], dst_ref=land.at[sender],
        send_sem=send_sems.at[t * 7 + k - 1], recv_sem=recv_sems.at[t * 7 + k - 1], device_id=peer, device_id_type=MESH)


def _sequencer_exchange(srcs, modes, name, collective_id, after=None):
    n = len(srcs)
    land_shapes = [((N_DEV,) + a.shape) if mode == "gather" else a.shape for a, mode in zip(srcs, modes)]
    extra = [] if after is None else [after]

    def body(*refs):
        src_refs, land_refs = refs[:n], refs[n + len(extra):2 * n + len(extra)]
        send_sems, recv_sems, local_sems = refs[2 * n + len(extra):]
        barrier = pltpu.get_barrier_semaphore()
        for k in range(1, N_DEV):
            pl.semaphore_signal(barrier, inc=1, device_id=_peer(k)[0], device_id_type=MESH)
        pl.semaphore_wait(barrier, N_DEV - 1)
        me = _my_index()
        local = []
        for i in range(n):
            cp = pltpu.make_async_copy(src_refs[i] if modes[i] == "gather" else src_refs[i].at[me], land_refs[i].at[me],
                                       local_sems.at[i])
            cp.start()
            local.append(cp)
        for k in range(1, N_DEV):
            peer, pid = _peer(k)
            for t in range(n):
                _piece_copy(modes[t], src_refs[t], land_refs[t], send_sems, recv_sems, t, k, me, pid, peer).start()
        for k in range(1, N_DEV):
            peer, pid = _peer(k)
            for t in range(n):
                _piece_copy(modes[t], src_refs[t], land_refs[t], send_sems, recv_sems, t, k, pid, me, peer).wait_recv()
        for k in range(1, N_DEV):
            peer, pid = _peer(k)
            for t in range(n):
                _piece_copy(modes[t], src_refs[t], land_refs[t], send_sems, recv_sems, t, k, me, pid, peer).wait_send()
        for cp in local:
            cp.wait()

    return pl.kernel(
        body, out_type=[_sds(s, a.dtype) for s, a in zip(land_shapes, srcs)],
        mesh=plsc.ScalarSubcoreMesh(axis_name="sequencer", num_cores=1),
        scratch_types=[pltpu.SemaphoreType.DMA((7 * n,)), pltpu.SemaphoreType.DMA((7 * n,)), pltpu.SemaphoreType.DMA((n,))],
        compiler_params=pltpu.CompilerParams(collective_id=collective_id), name=name)(*srcs, *extra)


def _sequencer_gather(srcs, name, collective_id, after=None):
    n = len(srcs)
    extra = [] if after is None else [after]

    def body(*refs):
        src_refs, land_refs = refs[:n], refs[n + len(extra):2 * n + len(extra)]
        send_sems, recv_sems, local_sems = refs[2 * n + len(extra):]
        x, y, c = lax.axis_index("x"), lax.axis_index("y"), lax.axis_index("c")
        sibling = (x, y, 1 - c)
        chips = [(1 - x, y), (x, 1 - y), (1 - x, 1 - y)]
        index = lambda px, py, pc: 4 * px + 2 * py + pc
        barrier = pltpu.get_barrier_semaphore()
        for peer in [sibling] + [(*chip, c) for chip in chips]:
            pl.semaphore_signal(barrier, inc=1, device_id=peer, device_id_type=MESH)
        pl.semaphore_wait(barrier, 4)

        def copy(t, k, slot, to, src=None):
            return pltpu.make_async_remote_copy(
                src_ref=land_refs[t].at[slot] if src is None else src, dst_ref=land_refs[t].at[slot],
                send_sem=send_sems.at[7 * t + k], recv_sem=recv_sems.at[7 * t + k], device_id=to, device_id_type=MESH)

        me = index(x, y, c)
        local = []
        for t in range(n):
            cp = pltpu.make_async_copy(src_refs[t], land_refs[t].at[me], local_sems.at[t])
            cp.start()
            local.append(cp)
        sends = []
        for t in range(n):
            sends.append(copy(t, 0, me, sibling, src=src_refs[t]))
            sends += [copy(t, 1 + j, me, (*chip, c), src=src_refs[t]) for j, chip in enumerate(chips)]
        for cp in sends:
            cp.start()
        for j, chip in enumerate(chips):
            for t in range(n):
                copy(t, 1 + j, index(*chip, c), sibling, src=src_refs[t]).wait_recv()
                passed = copy(t, 4 + j, index(*chip, c), sibling)
                passed.start()
                sends.append(passed)
        for t in range(n):
            copy(t, 0, index(x, y, 1 - c), sibling, src=src_refs[t]).wait_recv()
        for j, chip in enumerate(chips):
            for t in range(n):
                copy(t, 4 + j, index(*chip, 1 - c), sibling, src=src_refs[t]).wait_recv()
        for cp in sends:
            cp.wait_send()
        for cp in local:
            cp.wait()

    return pl.kernel(
        body, out_type=[_sds((N_DEV,) + a.shape, a.dtype) for a in srcs],
        mesh=plsc.ScalarSubcoreMesh(axis_name="sequencer", num_cores=1),
        scratch_types=[pltpu.SemaphoreType.DMA((7 * n,)), pltpu.SemaphoreType.DMA((7 * n,)), pltpu.SemaphoreType.DMA((n,))],
        compiler_params=pltpu.CompilerParams(collective_id=collective_id), name=name)(*srcs, *extra)


def _adamw(w, g, m, v):
    m = ADAM_B1 * m + (1.0 - ADAM_B1) * g
    v = ADAM_B2 * v + (1.0 - ADAM_B2) * (g * g)
    m_hat = m / (1.0 - ADAM_B1 ** ADAM_STEP)
    v_hat = v / (1.0 - ADAM_B2 ** ADAM_STEP)
    delta = -ADAM_LR * (m_hat / (jnp.sqrt(v_hat) + ADAM_EPS) + ADAM_WD * w)
    return delta, m, v


def _adam_big(w, parts, m, v, name, after=None):
    L, R, C = w.shape
    P = parts[0].shape[0]
    tr = _tile(R, (256, 128, 176, 64, 32, 16))
    nr = R // tr

    def body(w_ref, *refs):
        p_refs, (m_ref, v_ref, g_ref, d_ref, mo_ref, vo_ref) = refs[:L], refs[L:]
        for l in range(L):
            @pl.when(pl.program_id(0) == l)
            def _(p_ref=p_refs[l]):
                g = p_ref[0].astype(F32)
                for s in range(1, P):
                    g = g + p_ref[s].astype(F32)
                g_ref[...] = g
                d_ref[...], mo_ref[...], vo_ref[...] = _adamw(w_ref[...], g, m_ref[...], v_ref[...])

    row = pl.BlockSpec((None, tr, C), lambda l, i: (l, i, 0))
    park = lambda l_of: (lambda l, i: (0, jnp.where(l == l_of, i, 0 if l_of else nr - 1), 0))
    return _call(
        after, body, grid=(L, nr),
        in_specs=[row] + [pl.BlockSpec((P, tr, C), park(l)) for l in range(L)] + [row, row],
        out_specs=[row] * 4, out_shape=[_sds((L, R, C), F32)] * 4,
        compiler_params=_params(("arbitrary", "arbitrary")), name=name)(w, *parts, m, v)


SMALL = (("a_lower_bound", (2, 128), (2, D)), ("ln_gain", (6, 128), (6, D)), ("ln_bias", (6, 128), (6, D)),
         ("a_norm_gain", (1, 128), (1, 128)), ("kv_b", (1, 512), (1, 512)), ("b_b_q", (1, D), (1, D)),
         ("b_sinks", (1, ATT_QH), (1, 128)), ("b_b_out", (1, D), (1, D)), ("ple_b_gate", (2, D), (2, D)))


def _adam_small(parts, w, m, v, losses, after=None):
    k = len(SMALL)

    def body(*refs):
        p_refs, w_refs, m_refs, v_refs = refs[:k], refs[k:2 * k], refs[2 * k:3 * k], refs[3 * k:4 * k]
        loss_ref, outs, total_ref = refs[4 * k], refs[4 * k + 1:-1], refs[-1]
        total = loss_ref[0]
        for s in range(1, N_DEV):
            total = total + loss_ref[s]
        total_ref[...] = total
        me = _my_index()
        for i, (_, wshape, pshape) in enumerate(SMALL):
            cols = wshape[1]
            lanes = slice(None) if cols == pshape[1] else (
                pl.ds(0, cols) if cols < 128 else pl.ds(pl.multiple_of(me * cols, cols), cols))
            g = p_refs[i][0, :, lanes]
            for s in range(1, N_DEV):
                g = g + p_refs[i][s, :, lanes]
            g_ref, d_ref, mo_ref, vo_ref = outs[4 * i:4 * i + 4]
            g_ref[...] = g
            d_ref[...], mo_ref[...], vo_ref[...] = _adamw(w_refs[i][...], g, m_refs[i][...], v_refs[i][...])

    full = lambda shape: pl.BlockSpec(shape, lambda: (0,) * len(shape))
    names = [n for n, _, _ in SMALL]
    res = _call(
        after, body,
        in_specs=[full((N_DEV,) + ps) for _, _, ps in SMALL] + [full(ws) for _, ws, _ in SMALL] * 3
        + [full((N_DEV, 1, 128))],
        out_specs=[full(ws) for _, ws, _ in SMALL for _ in range(4)] + [full((1, 128))],
        out_shape=[_sds(ws, F32) for _, ws, _ in SMALL for _ in range(4)] + [_sds((1, 128), F32)], name="adam_small")(
            *[parts[n] for n in names], *[a[n].reshape(ws) for a in (w, m, v) for n, ws, _ in SMALL], losses)
    return {n: [r.reshape(w[n].shape) for r in res[4 * i:4 * i + 4]] for i, n in enumerate(names)}, res[-1][0, 0]


WEIGHTS = ("a_w_in", "a_lower_bound", "a_norm_gain", "a_w_out", "kv_w", "kv_b", "b_w_q", "b_b_q", "b_sinks", "b_w_out",
           "b_b_out", "ffn_w_gate_up", "ffn_w_down", "ple_w_up", "ple_w_gate", "ple_b_gate", "ln_gain", "ln_bias")


GATHER_GROUPS = (("a_w_in",), ("a_w_out", "gu0"), ("dn0", "pu0", "pg0"), ("kv_w", "b_w_q", "b_w_out"), ("gu1",),
                 ("dn1", "pu1", "pg1"))
KERNEL_LAYOUT = {
    "a_w_in": lambda a: a,
    "a_w_out": lambda a: a.reshape(D, D),
    "kv_w": lambda a: a.reshape(D, 2 * ATT_KVH * ATT_HD),
    "b_w_q": lambda a: a.reshape(D, D),
    "b_w_out": lambda a: a.reshape(D, D),
    "gu": lambda a: a.reshape(2, 4, FFN_B, D),
    "dn": lambda a: a.reshape(4, FFN_B, D),
    "pu": lambda a: a,
    "pg": lambda a: a.reshape(D, D),
}
_row_blocks = lambda a: a.reshape(N_DEV, -1, a.shape[-1])
OWNER_BLOCKS = {
    "a_w_in": lambda g: g,
    "a_w_out": _row_blocks,
    "kv_w": _row_blocks,
    "b_w_q": _row_blocks,
    "b_w_out": _row_blocks,
    "gu": lambda g: g,
    "dn": lambda g: _row_blocks(g.reshape(FFN_H, D)),
    "pu": lambda g: g.reshape(PLE_DIM, N_DEV, 128).transpose(1, 0, 2),
    "pg": _row_blocks,
}
ADAM_PARTS = (("kv_w", ("kv_w",)), ("b_w_q", ("b_w_q",)), ("b_w_out", ("b_w_out",)), ("ffn_w_gate_up", ("gu0", "gu1")),
              ("ffn_w_down", ("dn0", "dn1")), ("ple_w_up", ("pu0", "pu1")), ("ple_w_gate", ("pg0", "pg1")),
              ("a_w_out", ("a_w_out",)), ("a_w_in", ("a_w_in",)))


def kernel(x, p, a_w_in, a_lower_bound, a_norm_gain, a_w_out, kv_w, kv_b, b_w_q, b_b_q, b_sinks, b_w_out, b_b_out, ffn_w_gate_up, ffn_w_down, ple_w_up, ple_w_gate, ple_b_gate, ln_gain, ln_bias, loss_target, m_a_w_in, m_a_lower_bound, m_a_norm_gain, m_a_w_out, m_kv_w, m_kv_b, m_b_w_q, m_b_b_q, m_b_sinks, m_b_w_out, m_b_b_out, m_ffn_w_gate_up, m_ffn_w_down, m_ple_w_up, m_ple_w_gate, m_ple_b_gate, m_ln_gain, m_ln_bias, v_a_w_in, v_a_lower_bound, v_a_norm_gain, v_a_w_out, v_kv_w, v_kv_b, v_b_w_q, v_b_b_q, v_b_sinks, v_b_w_out, v_b_b_out, v_ffn_w_gate_up, v_ffn_w_down, v_ple_w_up, v_ple_w_gate, v_ple_b_gate, v_ln_gain, v_ln_bias):
    given = dict(locals())
    w = {n: given[n] for n in WEIGHTS}
    m = {n: given["m_" + n] for n in WEIGHTS}
    v = {n: given["v_" + n] for n in WEIGHTS}
    shards = {"a_w_in": a_w_in[0], "a_w_out": a_w_out[0], "kv_w": kv_w, "b_w_q": b_w_q[0], "b_w_out": b_w_out[0]}
    for l in range(2):
        shards.update({f"gu{l}": ffn_w_gate_up[l].T, f"dn{l}": ffn_w_down[l], f"pu{l}": ple_w_up[l], f"pg{l}": ple_w_gate[l]})
    sharded_small = [a_lower_bound, ln_gain.reshape(6, 128), ln_bias.reshape(6, 128)]
    gathered = {}
    for gi, g in enumerate(GATHER_GROUPS):
        lands = _sequencer_gather([shards[n].astype(CDT) for n in g] + (sharded_small if gi == 0 else []),
                                  f"gather{gi}", gi)
        for n, a in zip(g, lands):
            gathered[n] = KERNEL_LAYOUT[n.rstrip("01")](a)
        if gi == 0:
            alb, lng, lnb = [a.transpose(1, 0, 2).reshape(a.shape[1], D) for a in lands[len(g):]]

    getw = gathered.__getitem__

    sm = {"a_lower_bound": alb, "ln_gain": lng.reshape(2, 3, D), "ln_bias": lnb.reshape(2, 3, D),
          "a_norm_gain": a_norm_gain, "kv_b": kv_b, "b_b_q": b_b_q[0], "b_sinks": b_sinks, "b_b_out": b_b_out,
          "ple_b_gate": ple_b_gate}

    scatters, small_parts = [], {}

    def emit(grads, small=None):
        names = list(grads)
        blocks = [OWNER_BLOCKS[n.rstrip("01")](grads[n]) for n in names]
        partials = [] if small is None else [small[n].reshape(ps) for n, _, ps in SMALL] + [small["loss"]]
        lands = _sequencer_exchange(blocks + partials, ["scatter"] * len(blocks) + ["gather"] * len(partials),
                                    f"scatter{len(scatters)}", len(GATHER_GROUPS) + len(scatters))
        scatters.append(dict(zip(names, lands)))
        small_parts.update(zip([n for n, _, _ in SMALL] + ["loss"], lands[len(blocks):]))
        return blocks + (list(scatters[-4].values()) if len(scatters) >= 4 else [])

    loss, grad_x, gs = _local_step(x[0], p[:, 0], loss_target[0], getw, sm, emit)

    out, parts, last = {}, {}, [grad_x]
    for landed in scatters:
        parts.update(landed)
        for n, keys in ADAM_PARTS:
            if n in out or not all(key in parts for key in keys):
                continue
            lrc = (1,) * (3 - w[n].ndim) + w[n].shape
            shard = (lambda a: a.reshape(lrc).swapaxes(1, 2)) if n == "ffn_w_gate_up" else (lambda a: a.reshape(lrc))
            res = _adam_big(shard(w[n]), [parts[key] for key in keys], shard(m[n]), shard(v[n]), "adam_" + n, after=last)
            out[n] = [(r.swapaxes(1, 2) if n == "ffn_w_gate_up" else r).reshape(w[n].shape) for r in res]
            last = [res[3]]
    small_out, loss = _adam_small(small_parts, w, m, v, small_parts["loss"], after=last)
    out.update(small_out)
    res = [loss, grad_x[None]]
    for i in range(4):
        res += [out[n][i] for n in WEIGHTS]
    return tuple(res)
```

```python
import jax
import jax.numpy as jnp
from jax import lax
from jax.experimental import pallas as pl
from jax.experimental.pallas import tpu as pltpu
from jax.experimental.pallas import tpu_sc as plsc

F32 = jnp.float32
CDT = jnp.bfloat16

N_DEV = 8
D = 1024
HG_H, HG_DK, HG_CH = 8, 128, 64
HG_HPB = 4
HG_CPB = 8
ATT_HD, ATT_QH, ATT_KVH, ATT_G, WINDOW = 64, 16, 4, 4, 128
ATT_BPB = 1
FFN_H = 2816
FFN_B = FFN_H // 4
PLE_DIM = 256
ALPHA = (2.0 * 2) ** 0.25
LN_EPS = 1e-5
RMS_EPS = 1e-6
ADAM_LR, ADAM_B1, ADAM_B2, ADAM_EPS, ADAM_WD, ADAM_STEP = 0.001, 0.9, 0.999, 1e-08, 0.01, 10
ROW_TILES = (512, 256, 128, 64)
VMEM_LIMIT = 48 * 1024 * 1024
NEG = -1e30

MESH = pl.DeviceIdType.MESH


def _tile(n, cands=ROW_TILES):
    for t in cands:
        if n % t == 0:
            return t
    return n


def _sds(shape, dtype):
    return jax.ShapeDtypeStruct(tuple(shape), dtype)


def _params(sem):
    return pltpu.CompilerParams(dimension_semantics=sem, vmem_limit_bytes=VMEM_LIMIT)


def _dot(a, b):
    return jnp.dot(a.astype(CDT), b.astype(CDT), preferred_element_type=F32)


def _dot_nt(a, b):
    return lax.dot_general(a.astype(CDT), b.astype(CDT), (((1,), (1,)), ((), ())), preferred_element_type=F32)


def _dot_tn(a, b):
    return lax.dot_general(a.astype(CDT), b.astype(CDT), (((0,), (0,)), ((), ())), preferred_element_type=F32)


def _sigmoid(x):
    return jax.nn.sigmoid(x)


def _ln_fwd(z, g, b):
    mu = jnp.mean(z, axis=-1, keepdims=True)
    zc = z - mu
    var = jnp.mean(zc * zc, axis=-1, keepdims=True)
    return zc * lax.rsqrt(var + LN_EPS) * g + b


def _ln_bwd(z, g, dy):
    mu = jnp.mean(z, axis=-1, keepdims=True)
    zc = z - mu
    var = jnp.mean(zc * zc, axis=-1, keepdims=True)
    rstd = lax.rsqrt(var + LN_EPS)
    xhat = zc * rstd
    dxh = dy * g
    dz = rstd * (dxh - jnp.mean(dxh, axis=-1, keepdims=True) - xhat * jnp.mean(dxh * xhat, axis=-1, keepdims=True))
    return dz, xhat


def _colsum(x):
    return jnp.sum(x, axis=0, keepdims=True)


def _acc(ref, val, first):
    @pl.when(first)
    def _():
        ref[...] = val

    @pl.when(jnp.logical_not(first))
    def _():
        ref[...] += val


def _zero_at(ref, first):
    @pl.when(first)
    def _():
        ref[...] = jnp.zeros_like(ref)


def _call(after, body, **kw):
    after = [] if after is None else list(after)
    specs = list(kw["in_specs"])
    kw["in_specs"] = [pl.BlockSpec(memory_space=pl.ANY)] * len(after) + specs

    def ordered_body(*refs):
        body(*refs[len(after):])

    call = pl.pallas_call(ordered_body, **kw)

    def pinned(*args):
        args = [a if s.memory_space is not None else pltpu.with_memory_space_constraint(a, pltpu.HBM)
                for a, s in zip(args, specs)]
        return call(*after, *args)

    return pinned


def _mm_nn(a, b3, out_shape, oblock, omap, out_dtype, bias3=None, name="mm_nn"):
    M, K = a.shape
    G, _, Nb = b3.shape
    tm = _tile(M)

    def body(a_ref, b_ref, *rest):
        o_ref = rest[-1]
        acc = _dot(a_ref[...], b_ref[...])
        if bias3 is not None:
            acc = acc + rest[0][...]
        o_ref[...] = acc.astype(o_ref.dtype)

    in_specs = [pl.BlockSpec((tm, K), lambda g, i: (i, 0)), pl.BlockSpec((None, K, Nb), lambda g, i: (g, 0, 0))]
    args = [a, b3]
    if bias3 is not None:
        in_specs.append(pl.BlockSpec((None, 1, Nb), lambda g, i: (g, 0, 0)))
        args.append(bias3)
    return _call(
        None, body, grid=(G, M // tm), in_specs=in_specs, out_specs=pl.BlockSpec(oblock, omap),
        out_shape=_sds(out_shape, out_dtype), compiler_params=_params(("arbitrary", "arbitrary")), name=name)(*args)


def _mm_tn(a3, b3, G, amap, bmap, ablock, bblock, out_shape, oblock, omap, name="mm_tn"):
    S = a3.shape[1]

    def body(a_ref, b_ref, o_ref):
        o_ref[...] = _dot_tn(a_ref[...], b_ref[...]).astype(o_ref.dtype)

    return _call(
        None, body, grid=(G, 1),
        in_specs=[pl.BlockSpec(ablock(S), amap), pl.BlockSpec(bblock(S), bmap)],
        out_specs=pl.BlockSpec(oblock, omap), out_shape=_sds(out_shape, CDT),
        compiler_params=_params(("arbitrary", "arbitrary")), name=name)(a3, b3)


def _wgrad(a3, b3, name):
    Ga, S, M = a3.shape
    Gb, _, N = b3.shape
    G = max(Ga, Gb)
    return _mm_tn(
        a3, b3, G,
        (lambda g, k: (g, k, 0)) if Ga > 1 else (lambda g, k: (0, k, 0)),
        (lambda g, k: (g, k, 0)) if Gb > 1 else (lambda g, k: (0, k, 0)),
        lambda tk: (None, tk, M), lambda tk: (None, tk, N),
        (G, M, N), (None, M, N), lambda g, k: (g, 0, 0), name=name)


def _mixout_ln(u3, w3, bias, xin, gain, beta, name):
    G, S, Kb = u3.shape
    tm = _tile(S)

    def body(u_ref, w_ref, b_ref, x_ref, g_ref, be_ref, z_ref, xo_ref, xob_ref):
        h = b_ref[...] + _dot(u_ref[0], w_ref[0])
        for g in range(1, G):
            h = h + _dot(u_ref[g], w_ref[g])
        z = ALPHA * x_ref[...] + h
        z_ref[...] = z
        y = _ln_fwd(z, g_ref[...], be_ref[...])
        xo_ref[...] = y
        xob_ref[...] = y.astype(CDT)

    row = pl.BlockSpec((tm, D), lambda i: (i, 0))
    vec = pl.BlockSpec((1, D), lambda i: (0, 0))
    return _call(
        None, body, grid=(S // tm,),
        in_specs=[pl.BlockSpec((G, tm, Kb), lambda i: (0, i, 0)), pl.BlockSpec((G, Kb, D), lambda i: (0, 0, 0)),
                  vec, row, vec, vec],
        out_specs=[row, row, row], out_shape=[_sds((S, D), F32), _sds((S, D), F32), _sds((S, D), CDT)],
        compiler_params=_params(("arbitrary",)), name=name)(u3, w3, bias, xin, gain, beta)


def _ffn_fwd(xin, xin_b, wgu, wdn, gain, beta, name):
    S = xin.shape[0]
    tm = _tile(S)

    def hidden(xb_ref, wgu_ref, gu_ref, hid_ref):
        xb = xb_ref[...]
        gate = _dot_nt(xb, wgu_ref[0])
        up = _dot_nt(xb, wgu_ref[1])
        sg = _sigmoid(gate)
        silu = gate * sg
        gu_ref[0] = (up * (sg * (1.0 + gate * (1.0 - sg)))).astype(CDT)
        gu_ref[1] = silu.astype(CDT)
        hid_ref[...] = (silu * up).astype(CDT)

    gu, hid = _call(
        None, hidden, grid=(4, S // tm),
        in_specs=[pl.BlockSpec((tm, D), lambda j, i: (i, 0)), pl.BlockSpec((2, None, FFN_B, D), lambda j, i: (0, j, 0, 0))],
        out_specs=[pl.BlockSpec((2, None, tm, FFN_B), lambda j, i: (0, j, i, 0)),
                   pl.BlockSpec((None, tm, FFN_B), lambda j, i: (j, i, 0))],
        out_shape=[_sds((2, 4, S, FFN_B), CDT), _sds((4, S, FFN_B), CDT)],
        compiler_params=_params(("arbitrary", "arbitrary")), name=name + "_hidden")(xin_b, wgu)

    def down(x_ref, hid_ref, wdn_ref, g_ref, be_ref, z_ref, xo_ref, xob_ref):
        z = ALPHA * x_ref[...]
        for j in range(4):
            z = z + _dot(hid_ref[j], wdn_ref[j])
        z_ref[...] = z
        y = _ln_fwd(z, g_ref[...], be_ref[...])
        xo_ref[...] = y
        xob_ref[...] = y.astype(CDT)

    row = pl.BlockSpec((tm, D), lambda i: (i, 0))
    vec = pl.BlockSpec((1, D), lambda i: (0, 0))
    z, xo, xob = _call(
        None, down, grid=(S // tm,),
        in_specs=[row, pl.BlockSpec((4, tm, FFN_B), lambda i: (0, i, 0)), pl.BlockSpec((4, FFN_B, D), lambda i: (0, 0, 0)),
                  vec, vec],
        out_specs=[row, row, row], out_shape=[_sds((S, D), F32), _sds((S, D), F32), _sds((S, D), CDT)],
        compiler_params=_params(("arbitrary",)), name=name + "_down")(xin, hid, wdn, gain, beta)
    return gu, hid, z, xo, xob


def _ple_fwd(xin, xin_b, p_b, wpg, bgate, wpu, gain, beta, name):
    S = xin.shape[0]
    tm = _tile(S)

    def body(x_ref, xb_ref, p_ref, wpg_ref, bg_ref, wpu_ref, g_ref, be_ref, sg_ref, up_ref, z_ref, xo_ref, xob_ref):
        sg = _sigmoid(_dot(xb_ref[...], wpg_ref[...]) + bg_ref[...])
        pb = p_ref[...]
        up = jnp.concatenate([_dot(pb, wpu_ref[j]) for j in range(N_DEV)], axis=-1)
        sg_ref[...] = sg.astype(CDT)
        up_ref[...] = (up * sg * (1.0 - sg)).astype(CDT)
        z = ALPHA * x_ref[...] + sg * up
        z_ref[...] = z
        y = _ln_fwd(z, g_ref[...], be_ref[...])
        xo_ref[...] = y
        xob_ref[...] = y.astype(CDT)

    row = pl.BlockSpec((tm, D), lambda i: (i, 0))
    vec = pl.BlockSpec((1, D), lambda i: (0, 0))
    return _call(
        None, body, grid=(S // tm,),
        in_specs=[row, row, pl.BlockSpec((tm, PLE_DIM), lambda i: (i, 0)), pl.BlockSpec((D, D), lambda i: (0, 0)), vec,
                  pl.BlockSpec((N_DEV, PLE_DIM, D // N_DEV), lambda i: (0, 0, 0)), vec, vec],
        out_specs=[row] * 5,
        out_shape=[_sds((S, D), CDT)] * 2 + [_sds((S, D), F32)] * 2 + [_sds((S, D), CDT)],
        compiler_params=_params(("arbitrary",)), name=name)(xin, xin_b, p_b, wpg, bgate, wpu, gain, beta)


def _loss_fwd_bwd(y, target):
    S = y.shape[0]
    tm = _tile(S)

    def body(y_ref, t_ref, l_ref, dy_ref):
        e = y_ref[...] - t_ref[...]
        dy_ref[...] = e * (1.0 / D)
        part = 0.5 * jnp.sum(jnp.sum(e * e, axis=-1, keepdims=True) * (1.0 / D), axis=0, keepdims=True)
        _acc(l_ref, jnp.broadcast_to(part, l_ref.shape), pl.program_id(0) == 0)

    row = pl.BlockSpec((tm, D), lambda i: (i, 0))
    return _call(
        None, body, grid=(S // tm,), in_specs=[row, row],
        out_specs=[pl.BlockSpec((1, 128), lambda i: (0, 0)), row],
        out_shape=[_sds((1, 128), F32), _sds((S, D), F32)],
        compiler_params=_params(("arbitrary",)), name="loss")(y, target)


def _ple_bwd(dy, z, sg, up, gain, wpg, name, after=None):
    S = dy.shape[0]
    tm = _tile(S)

    def body(dy_ref, z_ref, sg_ref, up_ref, g_ref, wpg_ref, dx_ref, dgl_ref, dup_ref, dgain_ref, dbeta_ref, dbg_ref):
        first = pl.program_id(0) == 0
        dy_ = dy_ref[...]
        dz, xhat = _ln_bwd(z_ref[...], g_ref[...], dy_)
        dgl = dz * up_ref[...].astype(F32)
        dgl_ref[...] = dgl.astype(CDT)
        dup_ref[...] = (dz * sg_ref[...].astype(F32)).astype(CDT)
        dx_ref[...] = ALPHA * dz + _dot_nt(dgl, wpg_ref[...])
        _acc(dgain_ref, _colsum(dy_ * xhat), first)
        _acc(dbeta_ref, _colsum(dy_), first)
        _acc(dbg_ref, _colsum(dgl), first)

    row = pl.BlockSpec((tm, D), lambda i: (i, 0))
    vec = pl.BlockSpec((1, D), lambda i: (0, 0))
    return _call(
        after, body, grid=(S // tm,), in_specs=[row, row, row, row, vec, pl.BlockSpec((D, D), lambda i: (0, 0))],
        out_specs=[row, row, row, vec, vec, vec],
        out_shape=[_sds((S, D), F32), _sds((S, D), CDT), _sds((S, D), CDT)] + [_sds((1, D), F32)] * 3,
        compiler_params=_params(("arbitrary",)), name=name)(dy, z, sg, up, gain, wpg)


def _ffn_bwd_hidden(dy, z, gu, wdn, gain, name, after=None):
    S = dy.shape[0]
    tm = _tile(S)

    def hidden(dy_ref, z_ref, gu_ref, wdn_ref, g_ref, dz_ref, dzb_ref, dgu_ref, dgain_ref, dbeta_ref):
        i, j = pl.program_id(0), pl.program_id(1)

        @pl.when(j == 0)
        def _():
            dy_ = dy_ref[...]
            dz, xhat = _ln_bwd(z_ref[...], g_ref[...], dy_)
            dz_ref[...] = dz
            dzb_ref[...] = dz.astype(CDT)
            _acc(dgain_ref, _colsum(dy_ * xhat), i == 0)
            _acc(dbeta_ref, _colsum(dy_), i == 0)

        dhid = _dot_nt(dzb_ref[...], wdn_ref[...])
        dgu_ref[0] = (dhid * gu_ref[0].astype(F32)).astype(CDT)
        dgu_ref[1] = (dhid * gu_ref[1].astype(F32)).astype(CDT)

    row = pl.BlockSpec((tm, D), lambda i, j: (i, 0))
    vec = pl.BlockSpec((1, D), lambda i, j: (0, 0))
    return _call(
        after, hidden, grid=(S // tm, 4),
        in_specs=[row, row, pl.BlockSpec((2, None, tm, FFN_B), lambda i, j: (0, j, i, 0)),
                  pl.BlockSpec((None, FFN_B, D), lambda i, j: (j, 0, 0)), vec],
        out_specs=[row, row, pl.BlockSpec((2, None, tm, FFN_B), lambda i, j: (0, j, i, 0)), vec, vec],
        out_shape=[_sds((S, D), F32), _sds((S, D), CDT), _sds((2, 4, S, FFN_B), CDT), _sds((1, D), F32),
                   _sds((1, D), F32)],
        compiler_params=_params(("arbitrary", "arbitrary")), name=name + "_hidden")(dy, z, gu, wdn, gain)


def _ffn_bwd_input(dz, dgu, wgu, name, after=None):
    S = dz.shape[0]
    tm = _tile(S)

    def to_input(dz_ref, dgu_ref, wgu_ref, dx_ref):
        acc = ALPHA * dz_ref[...]
        for g in range(2):
            for j in range(4):
                acc = acc + _dot(dgu_ref[g, j], wgu_ref[g, j])
        dx_ref[...] = acc

    rows = pl.BlockSpec((tm, D), lambda i: (i, 0))
    return _call(
        after, to_input, grid=(S // tm,),
        in_specs=[rows, pl.BlockSpec((2, 4, tm, FFN_B), lambda i: (0, 0, i, 0)),
                  pl.BlockSpec((2, 4, FFN_B, D), lambda i: (0, 0, 0, 0))],
        out_specs=rows, out_shape=_sds((S, D), F32),
        compiler_params=_params(("arbitrary",)), name=name + "_input")(dz, dgu, wgu)


def _mixout_bwd(dy, z, gain, w3, du_dtype, name, after=None):
    S = dy.shape[0]
    G, Kb, _ = w3.shape
    tm = _tile(S)

    def body(dy_ref, z_ref, g_ref, w_ref, dz_ref, dzb_ref, du_ref, dgain_ref, dbeta_ref, dbias_ref):
        first = pl.program_id(0) == 0
        dy_ = dy_ref[...]
        dz, xhat = _ln_bwd(z_ref[...], g_ref[...], dy_)
        dz_ref[...] = dz
        dzb = dz.astype(CDT)
        dzb_ref[...] = dzb
        for g in range(G):
            du_ref[g] = _dot_nt(dzb, w_ref[g]).astype(du_ref.dtype)
        _acc(dgain_ref, _colsum(dy_ * xhat), first)
        _acc(dbeta_ref, _colsum(dy_), first)
        _acc(dbias_ref, _colsum(dz), first)

    row = pl.BlockSpec((tm, D), lambda i: (i, 0))
    vec = pl.BlockSpec((1, D), lambda i: (0, 0))
    return _call(
        after, body, grid=(S // tm,), in_specs=[row, row, vec, pl.BlockSpec((G, Kb, D), lambda i: (0, 0, 0))],
        out_specs=[row, row, pl.BlockSpec((G, tm, Kb), lambda i: (0, i, 0)), vec, vec, vec],
        out_shape=[_sds((S, D), F32), _sds((S, D), CDT), _sds((G, S, Kb), du_dtype)] + [_sds((1, D), F32)] * 3,
        compiler_params=_params(("arbitrary",)), name=name)(dy, z, gain, w3)


def _half_select(low):
    r = lax.broadcasted_iota(jnp.int32, (2 * ATT_HD, ATT_HD), 0)
    c = lax.broadcasted_iota(jnp.int32, (2 * ATT_HD, ATT_HD), 1)
    return (r == c + (0 if low else ATT_HD)).astype(CDT)


def _half_place(low):
    r = lax.broadcasted_iota(jnp.int32, (ATT_HD, 2 * ATT_HD), 0)
    c = lax.broadcasted_iota(jnp.int32, (ATT_HD, 2 * ATT_HD), 1)
    return (c == r + (0 if low else ATT_HD)).astype(CDT)


def _pair_lanes(even, odd):
    return (jnp.dot(even, _half_place(True), preferred_element_type=F32)
            + jnp.dot(odd, _half_place(False), preferred_element_type=F32)).astype(CDT)


def _proj_heads(a, w, bias, heads, name):
    S, K = a.shape
    N = heads * ATT_HD
    tm = _tile(S)

    def body(a_ref, w_ref, b_ref, o_ref):
        acc = (_dot(a_ref[...], w_ref[...]) + b_ref[...]).astype(CDT)
        sel = (_half_select(True), _half_select(False))
        for h in range(heads):
            pair = acc[:, (h // 2) * 2 * ATT_HD:(h // 2 + 1) * 2 * ATT_HD]
            o_ref[h] = jnp.dot(pair, sel[h % 2], preferred_element_type=F32).astype(CDT)

    return _call(
        None, body, grid=(S // tm,),
        in_specs=[pl.BlockSpec((tm, K), lambda i: (i, 0)), pl.BlockSpec((K, N), lambda i: (0, 0)),
                  pl.BlockSpec((1, N), lambda i: (0, 0))],
        out_specs=pl.BlockSpec((heads, tm, ATT_HD), lambda i: (0, i, 0)), out_shape=_sds((heads, S, ATT_HD), CDT),
        compiler_params=_params(("arbitrary",)), name=name)(a, w, bias)


def _qkv_bwd(dz, dq, dkv4, wq, wkv, name, after=None):
    S = dz.shape[0]
    tm = _tile(S)
    HK = dkv4.shape[0]
    NK = HK * ATT_HD

    def body(dz_ref, dq_ref, dkv_ref, wq_ref, wkv_ref, dx_ref, dkvn_ref, dkvb_ref):
        first = pl.program_id(0) == 0
        dkvn = jnp.concatenate([_pair_lanes(dkv_ref[2 * i].astype(CDT), dkv_ref[2 * i + 1].astype(CDT))
                                for i in range(HK // 2)], axis=-1)
        dkvn_ref[...] = dkvn
        dx_ref[...] = ALPHA * dz_ref[...] + _dot_nt(dq_ref[...], wq_ref[...]) + _dot_nt(dkvn, wkv_ref[...])
        for h in range(HK):
            _acc(dkvb_ref.at[h], _colsum(dkv_ref[h]), first)

    row = pl.BlockSpec((tm, D), lambda i: (i, 0))
    return _call(
        after, body, grid=(S // tm,),
        in_specs=[row, row, pl.BlockSpec((HK, tm, ATT_HD), lambda i: (0, i, 0)),
                  pl.BlockSpec((D, D), lambda i: (0, 0)), pl.BlockSpec((D, NK), lambda i: (0, 0))],
        out_specs=[row, pl.BlockSpec((tm, NK), lambda i: (i, 0)), pl.BlockSpec((HK, 1, ATT_HD), lambda i: (0, 0, 0))],
        out_shape=[_sds((S, D), F32), _sds((S, NK), CDT), _sds((HK, 1, ATT_HD), F32)],
        compiler_params=_params(("arbitrary",)), name=name)(dz, dq, dkv4, wq, wkv)


def _inproj_bwd(dz, dproj, wain, name, after=None):
    S = dz.shape[0]
    tm = _tile(S)
    nb = wain.shape[-1]

    def body(dz_ref, dp_ref, w_ref, dx_ref):
        acc = ALPHA * dz_ref[...]
        for j in range(N_DEV):
            acc = acc + _dot_nt(dp_ref[j // 2, :, pl.ds((j % 2) * nb, nb)], w_ref[j])
        dx_ref[...] = acc

    row = pl.BlockSpec((tm, D), lambda i: (i, 0))
    return _call(
        after, body, grid=(S // tm,),
        in_specs=[row, pl.BlockSpec((4, tm, D), lambda i: (0, i, 0)), pl.BlockSpec((N_DEV, D, nb), lambda i: (0, 0, 0))],
        out_specs=row, out_shape=_sds((S, D), F32),
        compiler_params=_params(("arbitrary",)), name=name)(dz, dproj, wain)


def _running_sum(x, reverse=False):
    rows = x.shape[0]
    row = lax.broadcasted_iota(jnp.int32, x.shape, 0)
    step = 1
    while step < rows:
        if reverse:
            x = x + jnp.where(row < rows - step, pltpu.roll(x, rows - step, 0), 0.0)
        else:
            x = x + jnp.where(row >= step, pltpu.roll(x, step, 0), 0.0)
        step *= 2
    return x


def _hg_gates(q, f, alb_ref):
    a0, a1 = alb_ref[0:1, :], alb_ref[1:2, :]
    mx = jnp.maximum(a0, a1)
    e0, e1 = jnp.exp(a0 - mx), jnp.exp(a1 - mx)
    lb = e0 / (e0 + e1)
    sig = _sigmoid(f)
    forget = lb + (1.0 - lb) * sig
    k = (1.0 - lb) * _sigmoid(-f)
    qs = q * _sigmoid(q) * (HG_DK ** -0.5)
    return qs, k, jnp.log(forget), sig, lb, forget


def _hg_intra(qs, k, b, b_scr):
    b_scr[...] = b
    bm = b_scr[pl.ds(HG_CH // 2 - 1, 1), :]
    bl = b_scr[pl.ds(HG_CH - 1, 1), :]
    eb = jnp.exp(b)
    qb = qs * eb
    e_q = jnp.exp(b - bm)
    e_k = jnp.exp(bm - b)
    e_d = jnp.exp(bl - b)
    return qb, qs * e_q, k * e_k, k * e_d, jnp.exp(bl), eb, e_q, e_k, e_d


def _hgrn_fwd(proj, alb, ngain):
    S = proj.shape[1]
    nc = S // HG_CH
    nb = nc // HG_CPB
    rb, wb = HG_CPB * HG_CH, HG_HPB * HG_DK

    def body(pj_ref, alb_ref, ng_ref, o_ref, y_ref, st_ref, st_scr, b_scr):
        n = pl.program_id(1)

        @pl.when(n == 0)
        def _():
            st_scr[...] = jnp.zeros_like(st_scr)

        r = lax.broadcasted_iota(jnp.int32, (HG_CH, HG_CH), 0)
        c = lax.broadcasted_iota(jnp.int32, (HG_CH, HG_CH), 1)
        causal = r >= c
        for ci, j in [(ci, j) for ci in range(HG_CPB) for j in range(HG_HPB)]:
            rows, lanes = pl.ds(ci * HG_CH, HG_CH), pl.ds(j * HG_DK, HG_DK)
            q, f, v, g = pj_ref[0, rows, lanes], pj_ref[1, rows, lanes], pj_ref[2, rows, lanes], pj_ref[3, rows, lanes]
            qs, k, logf, _, _, _ = _hg_gates(q, f, alb_ref.at[:, lanes])
            b = _running_sum(logf)
            qb, qt, kt, kd, ebl, _, _, _, _ = _hg_intra(qs, k, b, b_scr.at[j, ci])
            st = st_scr[j]
            st_ref[j, ci] = st
            a = jnp.where(causal, _dot_nt(qt, kt), 0.0)
            o = _dot(a, v) + _dot_nt(qb, st)
            st_scr[j] = st * ebl + _dot_tn(v, kd)
            o_ref[rows, lanes] = o
            rinv = lax.rsqrt(jnp.mean(o * o, axis=-1, keepdims=True) + RMS_EPS)
            y_ref[rows, lanes] = (o * rinv * ng_ref[...] * (g * _sigmoid(g))).astype(CDT)

    blk = pl.BlockSpec((rb, wb), lambda h, n: (n, h))
    return _call(
        None, body, grid=(HG_H // HG_HPB, nb),
        in_specs=[pl.BlockSpec((4, rb, wb), lambda h, n: (0, n, h)), pl.BlockSpec((2, wb), lambda h, n: (0, h)),
                  pl.BlockSpec((1, HG_DK), lambda h, n: (0, 0))],
        out_specs=[blk, blk, pl.BlockSpec((HG_HPB, HG_CPB, HG_DK, HG_DK), lambda h, n: (h, n, 0, 0))],
        out_shape=[_sds((S, D), F32), _sds((S, D), CDT), _sds((HG_H, nc, HG_DK, HG_DK), F32)],
        scratch_shapes=[pltpu.VMEM((HG_HPB, HG_DK, HG_DK), F32), pltpu.VMEM((HG_HPB, HG_CPB, HG_CH, HG_DK), F32)],
        compiler_params=_params(("arbitrary", "arbitrary")), name="hgrn_fwd")(proj, alb, ngain)


def _hgrn_bwd(proj, alb, ngain, o, states, dy, after=None):
    S = proj.shape[1]
    nc = S // HG_CH
    nb = nc // HG_CPB
    rb, wb = HG_CPB * HG_CH, HG_HPB * HG_DK

    def body(pj_ref, alb_ref, ng_ref, o_ref, st_ref, dy_ref, dpj_ref, dalb_ref, dng_ref, dst_scr, b_scr):
        h, n = pl.program_id(0), pl.program_id(1)

        @pl.when(n == 0)
        def _():
            dst_scr[...] = jnp.zeros_like(dst_scr)
            dalb_ref[...] = jnp.zeros_like(dalb_ref)

        _zero_at(dng_ref, jnp.logical_and(h == 0, n == 0))
        ng = ng_ref[...]
        r = lax.broadcasted_iota(jnp.int32, (HG_CH, HG_CH), 0)
        c = lax.broadcasted_iota(jnp.int32, (HG_CH, HG_CH), 1)
        causal = r >= c
        dng = None
        for ci, j in [(ci, j) for ci in reversed(range(HG_CPB)) for j in range(HG_HPB)]:
            rows, lanes = pl.ds(ci * HG_CH, HG_CH), pl.ds(j * HG_DK, HG_DK)
            q, f, v, g = pj_ref[0, rows, lanes], pj_ref[1, rows, lanes], pj_ref[2, rows, lanes], pj_ref[3, rows, lanes]
            o_ = o_ref[rows, lanes]
            dy_ = dy_ref[rows, lanes]
            sg = _sigmoid(g)
            rinv = lax.rsqrt(jnp.mean(o_ * o_, axis=-1, keepdims=True) + RMS_EPS)
            nrm = o_ * rinv
            dr = dy_ * (g * sg)
            dg = dy_ * nrm * ng * (sg * (1.0 + g * (1.0 - sg)))
            dn = dr * ng
            do = rinv * (dn - nrm * jnp.mean(dn * nrm, axis=-1, keepdims=True))
            dng = _colsum(dr * nrm) if dng is None else dng + _colsum(dr * nrm)
            qs, k, logf, sig, lb, forget = _hg_gates(q, f, alb_ref.at[:, lanes])
            b = _running_sum(logf)
            qb, qt, kt, kd, ebl, eb, e_q, e_k, e_d = _hg_intra(qs, k, b, b_scr.at[j, ci])
            st = st_ref[j, ci]
            dstn = dst_scr[j]
            qt, kt, qb, kd = (t.astype(CDT).astype(F32) for t in (qt, kt, qb, kd))
            a = jnp.where(causal, _dot_nt(qt, kt), 0.0)
            da = jnp.where(causal, _dot_nt(do, v), 0.0)
            dv = _dot_tn(a, do) + _dot_nt(kd, dstn)
            dqb = _dot(do, st)
            dkd = _dot(v, dstn)
            dqt = _dot(da, kt)
            dkt = _dot_tn(da, qt)
            dbl = _colsum(dkd * kd) + ebl * _colsum(dstn * st)
            dst_scr[j] = dstn * ebl + _dot_tn(do, qb)
            dqs = dqt * e_q + dqb * eb
            dk = dkt * e_k + dkd * e_d
            db = dqt * qt + dqb * qb - dkt * kt - dkd * kd
            dlogf = _running_sum(db, reverse=True) + dbl
            dforget = dlogf / forget
            dsig = (1.0 - lb) * (dforget - dk)
            df = dsig * sig * (1.0 - sig)
            dlb = _colsum((dforget - dk) * (1.0 - sig))
            sq = _sigmoid(q)
            dq = dqs * (HG_DK ** -0.5) * (sq * (1.0 + q * (1.0 - sq)))
            dpj_ref[0, rows, lanes] = dq.astype(CDT)
            dpj_ref[1, rows, lanes] = df.astype(CDT)
            dpj_ref[2, rows, lanes] = dv.astype(CDT)
            dpj_ref[3, rows, lanes] = dg.astype(CDT)
            da0 = dlb * lb * (1.0 - lb)
            dalb_ref[pl.ds(0, 1), lanes] += da0
            dalb_ref[pl.ds(1, 1), lanes] -= da0
        dng_ref[...] += dng

    blk = pl.BlockSpec((rb, wb), lambda h, n: (nb - 1 - n, h))
    pj = pl.BlockSpec((4, rb, wb), lambda h, n: (0, nb - 1 - n, h))
    alb_blk = pl.BlockSpec((2, wb), lambda h, n: (0, h))
    ng_blk = pl.BlockSpec((1, HG_DK), lambda h, n: (0, 0))
    return _call(
        after, body, grid=(HG_H // HG_HPB, nb),
        in_specs=[pj, alb_blk, ng_blk, blk,
                  pl.BlockSpec((HG_HPB, HG_CPB, HG_DK, HG_DK), lambda h, n: (h, nb - 1 - n, 0, 0)), blk],
        out_specs=[pj, alb_blk, ng_blk],
        out_shape=[_sds((4, S, D), CDT), _sds((2, D), F32), _sds((1, HG_DK), F32)],
        scratch_shapes=[pltpu.VMEM((HG_HPB, HG_DK, HG_DK), F32), pltpu.VMEM((HG_HPB, HG_CPB, HG_CH, HG_DK), F32)],
        compiler_params=_params(("arbitrary", "arbitrary")), name="hgrn_bwd")(proj, alb, ngain, o, states, dy)


def _slope(h):
    return 2.0 ** (-8.0 * (h + 1) / ATT_QH)


def _attn_mask(n):
    qi = lax.broadcasted_iota(jnp.int32, (WINDOW, 2 * WINDOW), 0)
    si = lax.broadcasted_iota(jnp.int32, (WINDOW, 2 * WINDOW), 1)
    dist = qi - si + WINDOW
    valid = (dist >= 0) & (dist < WINDOW) & (n * WINDOW - WINDOW + si >= 0)
    return valid, dist.astype(F32)


def _attn_probs(qh, kh, sink, slope, valid, distf):
    s = _dot_nt(qh, kh) * (ATT_HD ** -0.5) - slope * distf
    s = jnp.where(valid, s, NEG)
    m = jnp.maximum(jnp.max(s, axis=-1, keepdims=True), sink)
    e = jnp.exp(s - m)
    es = jnp.exp(sink - m)
    inv = 1.0 / (jnp.sum(e, axis=-1, keepdims=True) + es)
    return e * inv, es * inv


def _attn_specs(S):
    steps = S // (ATT_BPB * WINDOW)
    cur = lambda H: pl.BlockSpec((H, ATT_BPB * WINDOW, ATT_HD), lambda n: (0, n, 0))
    prev = lambda H: pl.BlockSpec((H, WINDOW, ATT_HD), lambda n: (0, jnp.maximum(ATT_BPB * n - 1, 0), 0))
    return steps, cur, prev


def _attn_kv(kvc_ref, kvp_ref, head, bi):
    before = kvp_ref[head] if bi == 0 else kvc_ref[head, pl.ds((bi - 1) * WINDOW, WINDOW), :]
    return jnp.concatenate([before, kvc_ref[head, pl.ds(bi * WINDOW, WINDOW), :]], axis=0)


def _attn_fwd(q4, kv4, sinks):
    S = q4.shape[1]
    nb, cur, prev = _attn_specs(S)

    def body(sink_ref, q_ref, kvc_ref, kvp_ref, o_ref, p_ref, ps_ref):
        lane = lax.broadcasted_iota(jnp.int32, (1, 128), 1)
        for bi in range(ATT_BPB):
            valid, distf = _attn_mask(pl.program_id(0) * ATT_BPB + bi)
            rows = pl.ds(bi * WINDOW, WINDOW)
            sink_probs = jnp.zeros((WINDOW, 128), F32)
            for kvh in range(ATT_KVH):
                kh = _attn_kv(kvc_ref, kvp_ref, kvh, bi)
                vh = _attn_kv(kvc_ref, kvp_ref, ATT_KVH + kvh, bi)
                v_low = jnp.dot(vh, _half_place(True), preferred_element_type=F32).astype(CDT)
                v_high = jnp.dot(vh, _half_place(False), preferred_element_type=F32).astype(CDT)
                for h in range(kvh * ATT_G, (kvh + 1) * ATT_G, 2):
                    pair = []
                    for hh in (h, h + 1):
                        p, ps = _attn_probs(q_ref[hh, rows, :], kh, sink_ref[0, hh], _slope(hh), valid, distf)
                        p = p.astype(CDT)
                        p_ref[hh, rows, :] = p
                        sink_probs = sink_probs + jnp.where(lane == hh, ps, 0.0)
                        pair.append(p)
                    o_ref[rows, pl.ds(h * ATT_HD, 2 * ATT_HD)] = (_dot(pair[0], v_low) + _dot(pair[1], v_high)).astype(CDT)
            ps_ref[rows, :] = sink_probs

    rows_spec = pl.BlockSpec((ATT_BPB * WINDOW, D), lambda n: (n, 0))
    return _call(
        None, body, grid=(nb,),
        in_specs=[pl.BlockSpec(memory_space=pltpu.SMEM), cur(ATT_QH), cur(2 * ATT_KVH), prev(2 * ATT_KVH)],
        out_specs=[rows_spec, pl.BlockSpec((ATT_QH, ATT_BPB * WINDOW, 2 * WINDOW), lambda n: (0, n, 0)),
                   pl.BlockSpec((ATT_BPB * WINDOW, 128), lambda n: (n, 0))],
        out_shape=[_sds((S, D), CDT), _sds((ATT_QH, S, 2 * WINDOW), CDT), _sds((S, 128), F32)],
        compiler_params=_params(("arbitrary",)), name="attn_fwd")(sinks, q4, kv4, kv4)


def _attn_bwd(q4, kv4, probs, sink_probs, do):
    S = q4.shape[1]
    nb, cur, prev = _attn_specs(S)

    def body(q_ref, kvc_ref, kvp_ref, p_ref, ps_ref, do_ref, dq_ref, dkv_ref, dbq_ref, dsink_ref):
        n = pl.program_id(0)
        first = n == 0

        @pl.when(first)
        def _():
            dkv_ref[...] = jnp.zeros_like(dkv_ref)
            dsink_ref[...] = jnp.zeros_like(dsink_ref)
            dbq_ref[...] = jnp.zeros_like(dbq_ref)

        lane = lax.broadcasted_iota(jnp.int32, (1, 128), 1)
        dsinks = jnp.zeros((1, 128), F32)
        sel = (_half_select(True), _half_select(False))
        row_dots = [jnp.zeros((WINDOW, 128), F32) for _ in range(ATT_BPB)]
        for bi, kvh in [(bi, kvh) for bi in range(ATT_BPB) for kvh in range(ATT_KVH)]:
            block = n * ATT_BPB + bi
            rows = pl.ds(bi * WINDOW, WINDOW)
            rows_cur = pl.ds(pl.multiple_of(block * WINDOW, WINDOW), WINDOW)
            rows_prev = pl.ds(pl.multiple_of(jnp.maximum(block - 1, 0) * WINDOW, WINDOW), WINDOW)
            kh = _attn_kv(kvc_ref, kvp_ref, kvh, bi)
            vh = _attn_kv(kvc_ref, kvp_ref, ATT_KVH + kvh, bi)
            dk = dv = None
            dqs = []
            for h in range(kvh * ATT_G, (kvh + 1) * ATT_G):
                qh = q_ref[h, rows, :]
                doh = jnp.dot(do_ref[rows, pl.ds((h // 2) * 2 * ATT_HD, 2 * ATT_HD)], sel[h % 2],
                              preferred_element_type=F32).astype(CDT)
                p = p_ref[h, rows, :].astype(F32)
                dp = _dot_nt(doh, vh)
                dd = jnp.sum(p * dp, axis=-1, keepdims=True)
                ds = p * (dp - dd)
                row_dots[bi] = row_dots[bi] + jnp.where(lane == h, dd, 0.0)
                dqh = _dot(ds, kh) * (ATT_HD ** -0.5)
                dqs.append(dqh.astype(CDT))
                dbq_ref[h] += _colsum(dqh)
                dkh = _dot_tn(ds, qh) * (ATT_HD ** -0.5)
                dvh = _dot_tn(p, doh)
                dk = dkh if dk is None else dk + dkh
                dv = dvh if dv is None else dv + dvh
            for i in range(ATT_G // 2):
                lanes = pl.ds((kvh * ATT_G + 2 * i) * ATT_HD, 2 * ATT_HD)
                dq_ref[rows, lanes] = _pair_lanes(dqs[2 * i], dqs[2 * i + 1])
            dkv_ref[kvh, rows_prev, :] += dk[:WINDOW]
            dkv_ref[kvh, rows_cur, :] += dk[WINDOW:]
            dkv_ref[ATT_KVH + kvh, rows_prev, :] += dv[:WINDOW]
            dkv_ref[ATT_KVH + kvh, rows_cur, :] += dv[WINDOW:]
        for bi in range(ATT_BPB):
            dsinks = dsinks - _colsum(ps_ref[pl.ds(bi * WINDOW, WINDOW), :] * row_dots[bi])
        dsink_ref[...] += dsinks

    rows_spec = pl.BlockSpec((ATT_BPB * WINDOW, D), lambda n: (n, 0))
    return _call(
        None, body, grid=(nb,),
        in_specs=[cur(ATT_QH), cur(2 * ATT_KVH), prev(2 * ATT_KVH),
                  pl.BlockSpec((ATT_QH, ATT_BPB * WINDOW, 2 * WINDOW), lambda n: (0, n, 0)),
                  pl.BlockSpec((ATT_BPB * WINDOW, 128), lambda n: (n, 0)), rows_spec],
        out_specs=[rows_spec, pl.BlockSpec((2 * ATT_KVH, S, ATT_HD), lambda n: (0, 0, 0)),
                   pl.BlockSpec((ATT_QH, 1, ATT_HD), lambda n: (0, 0, 0)), pl.BlockSpec((1, 128), lambda n: (0, 0))],
        out_shape=[_sds((S, D), CDT), _sds((2 * ATT_KVH, S, ATT_HD), F32), _sds((ATT_QH, 1, ATT_HD), F32),
                   _sds((1, 128), F32)],
        compiler_params=_params(("arbitrary",)), name="attn_bwd")(q4, kv4, kv4, probs, sink_probs, do)


def _local_step(x, p, target, getw, sm, emit):
    S = x.shape[0]
    vec = lambda a: a.reshape(1, -1)
    ln_g = lambda l, k: vec(sm["ln_gain"][l, k])
    ln_b = lambda l, k: vec(sm["ln_bias"][l, k])
    xb = x.astype(CDT)
    pb = p.astype(CDT)

    proj = _mm_nn(xb, getw("a_w_in"), (4, S, D), (None, _tile(S), 512), lambda g, i: (g // 2, i, g % 2), F32,
                  name="a_in")
    o_a, y_a, states = _hgrn_fwd(proj, sm["a_lower_bound"], sm["a_norm_gain"])
    zeros = jnp.zeros((1, D), F32)
    z = [[None] * 3 for _ in range(2)]
    xs = [[None] * 3 for _ in range(2)]
    xbs = [[None] * 3 for _ in range(2)]
    z[0][0], xs[0][0], xbs[0][0] = _mixout_ln(y_a[None], getw("a_w_out")[None], zeros, x, ln_g(0, 0), ln_b(0, 0),
                                              "a_out_ln")
    gu, hid, sgs, ups = [None, None], [None, None], [None, None], [None, None]

    def ffn_ple(l):
        wgu = getw(f"gu{l}")
        gu[l], hid[l], z[l][1], xs[l][1], xbs[l][1] = _ffn_fwd(
            xs[l][0], xbs[l][0], wgu, getw(f"dn{l}"), ln_g(l, 1), ln_b(l, 1), f"ffn_fwd{l}")
        sgs[l], ups[l], z[l][2], xs[l][2], xbs[l][2] = _ple_fwd(
            xs[l][1], xbs[l][1], pb[l], getw(f"pg{l}"), vec(sm["ple_b_gate"][l]), getw(f"pu{l}"), ln_g(l, 2),
            ln_b(l, 2), f"ple_fwd{l}")

    ffn_ple(0)
    x3, x3b = xs[0][2], xbs[0][2]
    w_kv, w_q, w_bo = getw("kv_w"), getw("b_w_q"), getw("b_w_out")
    kv4 = _proj_heads(x3b, w_kv, vec(sm["kv_b"]), 2 * ATT_KVH, "kv_proj")
    q4 = _proj_heads(x3b, w_q, vec(sm["b_b_q"]), ATT_QH, "q_proj")
    o_b, probs, sink_probs = _attn_fwd(q4, kv4, sm["b_sinks"])
    z[1][0], xs[1][0], xbs[1][0] = _mixout_ln(o_b[None], w_bo[None], sm["b_b_out"], x3, ln_g(1, 0), ln_b(1, 0),
                                              "b_out_ln")
    ffn_ple(1)
    loss, dy = _loss_fwd_bwd(xs[1][2], target)

    gs = {}
    d_ln_g = [[None] * 3 for _ in range(2)]
    d_ln_b = [[None] * 3 for _ in range(2)]
    g_bg = [None, None]

    def ffn_ple_bwd(l, dy, after=None):
        dx2, dgl, dup, d_ln_g[l][2], d_ln_b[l][2], g_bg[l] = _ple_bwd(dy, z[l][2], sgs[l], ups[l], ln_g(l, 2),
                                                                     getw(f"pg{l}"), f"ple_bwd{l}", after=after)
        g_pg = _wgrad(xbs[l][1][None], dgl[None], f"g_ple_gate{l}")[0]
        g_pu = _wgrad(pb[l][None], dup[None], f"g_ple_up{l}")[0]
        tok = emit({f"pg{l}": g_pg, f"pu{l}": g_pu})
        dz2, dzb, dgu, d_ln_g[l][1], d_ln_b[l][1] = _ffn_bwd_hidden(dx2, z[l][1], gu[l], getw(f"dn{l}"), ln_g(l, 1),
                                                                   f"ffn_bwd{l}", after=tok)
        tok = emit({f"dn{l}": _wgrad(hid[l], dzb[None], f"g_ffn_down{l}")})
        g_gu = _wgrad(dgu.reshape(8, S, FFN_B), xbs[l][0][None], f"g_ffn_gate_up{l}")
        tok = tok + emit({f"gu{l}": g_gu})
        dx1 = _ffn_bwd_input(dz2, dgu, getw(f"gu{l}"), f"ffn_bwd{l}", after=tok)
        return dx1, None

    dx1, tok = ffn_ple_bwd(1, dy)
    dz, dzb, do, d_ln_g[1][0], d_ln_b[1][0], gs["b_b_out"] = _mixout_bwd(dx1, z[1][0], ln_g(1, 0), w_bo[None], CDT,
                                                                        "b_out_bwd", after=tok)
    g_bo = _wgrad(o_b[None], dzb[None], "g_b_w_out")[0]
    dq, dkv4, dbq, dsinks = _attn_bwd(q4, kv4, probs, sink_probs, do[0])
    gs["b_b_q"] = dbq
    gs["b_sinks"] = dsinks
    g_q = _wgrad(x3b[None], dq[None], "g_b_w_q")[0]
    dx3, dkv, gs["kv_b"] = _qkv_bwd(dz, dq, dkv4, w_q, w_kv, "qkv_bwd", after=tok)
    g_kv = _wgrad(x3b[None], dkv[None], "g_kv_w")[0]
    tok = emit({"b_w_out": g_bo, "b_w_q": g_q, "kv_w": g_kv})
    dx1, tok = ffn_ple_bwd(0, dx3, tok)
    w_ao = getw("a_w_out")
    dz, dzb, dyr, d_ln_g[0][0], d_ln_b[0][0], _ = _mixout_bwd(dx1, z[0][0], ln_g(0, 0), w_ao[None], F32, "a_out_bwd",
                                                              after=tok)
    g_ao = _wgrad(y_a[None], dzb[None], "g_a_w_out")[0]
    tok = emit({"a_w_out": g_ao})
    dproj, gs["a_lower_bound"], gs["a_norm_gain"] = _hgrn_bwd(proj, sm["a_lower_bound"], sm["a_norm_gain"], o_a, states,
                                                              dyr[0], after=tok)
    tk = lambda t: (None, t, D)
    g_ain = _mm_tn(xb[None], dproj, N_DEV, lambda g, k: (0, k, 0), lambda g, k: (g // 2, k, g % 2),
                   tk, lambda t: (None, t, 512), (N_DEV, D, 512), (None, D, 512), lambda g, k: (g, 0, 0), name="g_a_w_in")
    gs["ple_b_gate"] = jnp.concatenate(g_bg, axis=0)
    gs["ln_gain"] = jnp.stack([jnp.concatenate(r, axis=0) for r in d_ln_g])
    gs["ln_bias"] = jnp.stack([jnp.concatenate(r, axis=0) for r in d_ln_b])
    gs["loss"] = loss
    tok = emit({"a_w_in": g_ain}, small=gs)
    grad_x = _inproj_bwd(dz, dproj, getw("a_w_in"), "a_in_bwd", after=tok)
    return loss, grad_x, gs


def _peer(k):
    x, y, c = lax.axis_index("x"), lax.axis_index("y"), lax.axis_index("c")
    px = 1 - x if k & 4 else x
    py = 1 - y if k & 2 else y
    pc = 1 - c if k & 1 else c
    return (px, py, pc), 4 * px + 2 * py + pc


def _my_index():
    return 4 * lax.axis_index("x") + 2 * lax.axis_index("y") + lax.axis_index("c")


def _piece_copy(mode, src, land, send_sems, recv_sems, t, k, sender, receiver, peer):
    return pltpu.make_async_remote_copy(
        src_ref=src if mode == "gather" else src.at[receiver], dst_ref=land.at[sender],
        send_sem=send_sems.at[t * 7 + k - 1], recv_sem=recv_sems.at[t * 7 + k - 1], device_id=peer, device_id_type=MESH)


def _sequencer_exchange(srcs, modes, name, collective_id, after=None):
    n = len(srcs)
    land_shapes = [((N_DEV,) + a.shape) if mode == "gather" else a.shape for a, mode in zip(srcs, modes)]
    extra = [] if after is None else [after]

    def body(*refs):
        src_refs, land_refs = refs[:n], refs[n + len(extra):2 * n + len(extra)]
        send_sems, recv_sems, local_sems = refs[2 * n + len(extra):]
        barrier = pltpu.get_barrier_semaphore()
        for k in range(1, N_DEV):
            pl.semaphore_signal(barrier, inc=1, device_id=_peer(k)[0], device_id_type=MESH)
        pl.semaphore_wait(barrier, N_DEV - 1)
        me = _my_index()
        local = []
        for i in range(n):
            cp = pltpu.make_async_copy(src_refs[i] if modes[i] == "gather" else src_refs[i].at[me], land_refs[i].at[me],
                                       local_sems.at[i])
            cp.start()
            local.append(cp)
        for k in range(1, N_DEV):
            peer, pid = _peer(k)
            for t in range(n):
                _piece_copy(modes[t], src_refs[t], land_refs[t], send_sems, recv_sems, t, k, me, pid, peer).start()
        for k in range(1, N_DEV):
            peer, pid = _peer(k)
            for t in range(n):
                _piece_copy(modes[t], src_refs[t], land_refs[t], send_sems, recv_sems, t, k, pid, me, peer).wait_recv()
        for k in range(1, N_DEV):
            peer, pid = _peer(k)
            for t in range(n):
                _piece_copy(modes[t], src_refs[t], land_refs[t], send_sems, recv_sems, t, k, me, pid, peer).wait_send()
        for cp in local:
            cp.wait()

    return pl.kernel(
        body, out_type=[_sds(s, a.dtype) for s, a in zip(land_shapes, srcs)],
        mesh=plsc.ScalarSubcoreMesh(axis_name="sequencer", num_cores=1),
        scratch_types=[pltpu.SemaphoreType.DMA((7 * n,)), pltpu.SemaphoreType.DMA((7 * n,)), pltpu.SemaphoreType.DMA((n,))],
        compiler_params=pltpu.CompilerParams(collective_id=collective_id), name=name)(*srcs, *extra)


def _sequencer_gather(srcs, name, collective_id, after=None):
    n = len(srcs)
    extra = [] if after is None else [after]

    def body(*refs):
        src_refs, land_refs = refs[:n], refs[n + len(extra):2 * n + len(extra)]
        send_sems, recv_sems, local_sems = refs[2 * n + len(extra):]
        x, y, c = lax.axis_index("x"), lax.axis_index("y"), lax.axis_index("c")
        sibling = (x, y, 1 - c)
        chips = [(1 - x, y), (x, 1 - y), (1 - x, 1 - y)]
        index = lambda px, py, pc: 4 * px + 2 * py + pc
        barrier = pltpu.get_barrier_semaphore()
        for peer in [sibling] + [(*chip, c) for chip in chips]:
            pl.semaphore_signal(barrier, inc=1, device_id=peer, device_id_type=MESH)
        pl.semaphore_wait(barrier, 4)

        def copy(t, k, slot, to, src=None):
            return pltpu.make_async_remote_copy(
                src_ref=land_refs[t].at[slot] if src is None else src, dst_ref=land_refs[t].at[slot],
                send_sem=send_sems.at[7 * t + k], recv_sem=recv_sems.at[7 * t + k], device_id=to, device_id_type=MESH)

        me = index(x, y, c)
        local = []
        for t in range(n):
            cp = pltpu.make_async_copy(src_refs[t], land_refs[t].at[me], local_sems.at[t])
            cp.start()
            local.append(cp)
        sends = []
        for t in range(n):
            sends.append(copy(t, 0, me, sibling, src=src_refs[t]))
            sends += [copy(t, 1 + j, me, (*chip, c), src=src_refs[t]) for j, chip in enumerate(chips)]
        for cp in sends:
            cp.start()
        for j, chip in enumerate(chips):
            for t in range(n):
                copy(t, 1 + j, index(*chip, c), sibling, src=src_refs[t]).wait_recv()
                passed = copy(t, 4 + j, index(*chip, c), sibling)
                passed.start()
                sends.append(passed)
        for t in range(n):
            copy(t, 0, index(x, y, 1 - c), sibling, src=src_refs[t]).wait_recv()
        for j, chip in enumerate(chips):
            for t in range(n):
                copy(t, 4 + j, index(*chip, 1 - c), sibling, src=src_refs[t]).wait_recv()
        for cp in sends:
            cp.wait_send()
        for cp in local:
            cp.wait()

    return pl.kernel(
        body, out_type=[_sds((N_DEV,) + a.shape, a.dtype) for a in srcs],
        mesh=plsc.ScalarSubcoreMesh(axis_name="sequencer", num_cores=1),
        scratch_types=[pltpu.SemaphoreType.DMA((7 * n,)), pltpu.SemaphoreType.DMA((7 * n,)), pltpu.SemaphoreType.DMA((n,))],
        compiler_params=pltpu.CompilerParams(collective_id=collective_id), name=name)(*srcs, *extra)


def _adamw(w, g, m, v):
    m = ADAM_B1 * m + (1.0 - ADAM_B1) * g
    v = ADAM_B2 * v + (1.0 - ADAM_B2) * (g * g)
    m_hat = m / (1.0 - ADAM_B1 ** ADAM_STEP)
    v_hat = v / (1.0 - ADAM_B2 ** ADAM_STEP)
    delta = -ADAM_LR * (m_hat / (jnp.sqrt(v_hat) + ADAM_EPS) + ADAM_WD * w)
    return delta, m, v


def _adam_big(w, parts, m, v, name, after=None):
    L, R, C = w.shape
    P = parts[0].shape[0]
    tr = _tile(R, (256, 128, 176, 64, 32, 16))
    nr = R // tr

    def body(w_ref, *refs):
        p_refs, (m_ref, v_ref, g_ref, d_ref, mo_ref, vo_ref) = refs[:L], refs[L:]
        for l in range(L):
            @pl.when(pl.program_id(0) == l)
            def _(p_ref=p_refs[l]):
                g = p_ref[0].astype(F32)
                for s in range(1, P):
                    g = g + p_ref[s].astype(F32)
                g_ref[...] = g
                d_ref[...], mo_ref[...], vo_ref[...] = _adamw(w_ref[...], g, m_ref[...], v_ref[...])

    row = pl.BlockSpec((None, tr, C), lambda l, i: (l, i, 0))
    park = lambda l_of: (lambda l, i: (0, jnp.where(l == l_of, i, 0 if l_of else nr - 1), 0))
    return _call(
        after, body, grid=(L, nr),
        in_specs=[row] + [pl.BlockSpec((P, tr, C), park(l)) for l in range(L)] + [row, row],
        out_specs=[row] * 4, out_shape=[_sds((L, R, C), F32)] * 4,
        compiler_params=_params(("arbitrary", "arbitrary")), name=name)(w, *parts, m, v)


SMALL = (("a_lower_bound", (2, 128), (2, D)), ("ln_gain", (6, 128), (6, D)), ("ln_bias", (6, 128), (6, D)),
         ("a_norm_gain", (1, 128), (1, 128)), ("kv_b", (1, 512), (1, 512)), ("b_b_q", (1, D), (1, D)),
         ("b_sinks", (1, ATT_QH), (1, 128)), ("b_b_out", (1, D), (1, D)), ("ple_b_gate", (2, D), (2, D)))


def _adam_small(parts, w, m, v, losses, after=None):
    k = len(SMALL)

    def body(*refs):
        p_refs, w_refs, m_refs, v_refs = refs[:k], refs[k:2 * k], refs[2 * k:3 * k], refs[3 * k:4 * k]
        loss_ref, outs, total_ref = refs[4 * k], refs[4 * k + 1:-1], refs[-1]
        total = loss_ref[0]
        for s in range(1, N_DEV):
            total = total + loss_ref[s]
        total_ref[...] = total
        me = _my_index()
        for i, (_, wshape, pshape) in enumerate(SMALL):
            cols = wshape[1]
            lanes = slice(None) if cols == pshape[1] else (
                pl.ds(0, cols) if cols < 128 else pl.ds(pl.multiple_of(me * cols, cols), cols))
            g = p_refs[i][0, :, lanes]
            for s in range(1, N_DEV):
                g = g + p_refs[i][s, :, lanes]
            g_ref, d_ref, mo_ref, vo_ref = outs[4 * i:4 * i + 4]
            g_ref[...] = g
            d_ref[...], mo_ref[...], vo_ref[...] = _adamw(w_refs[i][...], g, m_refs[i][...], v_refs[i][...])

    full = lambda shape: pl.BlockSpec(shape, lambda: (0,) * len(shape))
    names = [n for n, _, _ in SMALL]
    res = _call(
        after, body,
        in_specs=[full((N_DEV,) + ps) for _, _, ps in SMALL] + [full(ws) for _, ws, _ in SMALL] * 3
        + [full((N_DEV, 1, 128))],
        out_specs=[full(ws) for _, ws, _ in SMALL for _ in range(4)] + [full((1, 128))],
        out_shape=[_sds(ws, F32) for _, ws, _ in SMALL for _ in range(4)] + [_sds((1, 128), F32)], name="adam_small")(
            *[parts[n] for n in names], *[a[n].reshape(ws) for a in (w, m, v) for n, ws, _ in SMALL], losses)
    return {n: [r.reshape(w[n].shape) for r in res[4 * i:4 * i + 4]] for i, n in enumerate(names)}, res[-1][0, 0]


WEIGHTS = ("a_w_in", "a_lower_bound", "a_norm_gain", "a_w_out", "kv_w", "kv_b", "b_w_q", "b_b_q", "b_sinks", "b_w_out",
           "b_b_out", "ffn_w_gate_up", "ffn_w_down", "ple_w_up", "ple_w_gate", "ple_b_gate", "ln_gain", "ln_bias")


GATHER_GROUPS = (("a_w_in",), ("a_w_out", "gu0"), ("dn0", "pu0", "pg0"), ("kv_w", "b_w_q", "b_w_out"), ("gu1",),
                 ("dn1", "pu1", "pg1"))
KERNEL_LAYOUT = {
    "a_w_in": lambda a: a,
    "a_w_out": lambda a: a.reshape(D, D),
    "kv_w": lambda a: a.reshape(D, 2 * ATT_KVH * ATT_HD),
    "b_w_q": lambda a: a.reshape(D, D),
    "b_w_out": lambda a: a.reshape(D, D),
    "gu": lambda a: a.reshape(2, 4, FFN_B, D),
    "dn": lambda a: a.reshape(4, FFN_B, D),
    "pu": lambda a: a,
    "pg": lambda a: a.reshape(D, D),
}
_row_blocks = lambda a: a.reshape(N_DEV, -1, a.shape[-1])
OWNER_BLOCKS = {
    "a_w_in": lambda g: g,
    "a_w_out": _row_blocks,
    "kv_w": _row_blocks,
    "b_w_q": _row_blocks,
    "b_w_out": _row_blocks,
    "gu": lambda g: g,
    "dn": lambda g: _row_blocks(g.reshape(FFN_H, D)),
    "pu": lambda g: g.reshape(PLE_DIM, N_DEV, 128).transpose(1, 0, 2),
    "pg": _row_blocks,
}
ADAM_PARTS = (("kv_w", ("kv_w",)), ("b_w_q", ("b_w_q",)), ("b_w_out", ("b_w_out",)), ("ffn_w_gate_up", ("gu0", "gu1")),
              ("ffn_w_down", ("dn0", "dn1")), ("ple_w_up", ("pu0", "pu1")), ("ple_w_gate", ("pg0", "pg1")),
              ("a_w_out", ("a_w_out",)), ("a_w_in", ("a_w_in",)))


def kernel(x, p, a_w_in, a_lower_bound, a_norm_gain, a_w_out, kv_w, kv_b, b_w_q, b_b_q, b_sinks, b_w_out, b_b_out, ffn_w_gate_up, ffn_w_down, ple_w_up, ple_w_gate, ple_b_gate, ln_gain, ln_bias, loss_target, m_a_w_in, m_a_lower_bound, m_a_norm_gain, m_a_w_out, m_kv_w, m_kv_b, m_b_w_q, m_b_b_q, m_b_sinks, m_b_w_out, m_b_b_out, m_ffn_w_gate_up, m_ffn_w_down, m_ple_w_up, m_ple_w_gate, m_ple_b_gate, m_ln_gain, m_ln_bias, v_a_w_in, v_a_lower_bound, v_a_norm_gain, v_a_w_out, v_kv_w, v_kv_b, v_b_w_q, v_b_b_q, v_b_sinks, v_b_w_out, v_b_b_out, v_ffn_w_gate_up, v_ffn_w_down, v_ple_w_up, v_ple_w_gate, v_ple_b_gate, v_ln_gain, v_ln_bias):
    given = dict(locals())
    w = {n: given[n] for n in WEIGHTS}
    m = {n: given["m_" + n] for n in WEIGHTS}
    v = {n: given["v_" + n] for n in WEIGHTS}
    shards = {"a_w_in": a_w_in[0], "a_w_out": a_w_out[0], "kv_w": kv_w, "b_w_q": b_w_q[0], "b_w_out": b_w_out[0]}
    for l in range(2):
        shards.update({f"gu{l}": ffn_w_gate_up[l].T, f"dn{l}": ffn_w_down[l], f"pu{l}": ple_w_up[l], f"pg{l}": ple_w_gate[l]})
    sharded_small = [a_lower_bound, ln_gain.reshape(6, 128), ln_bias.reshape(6, 128)]
    gathered = {}
    for gi, g in enumerate(GATHER_GROUPS):
        lands = _sequencer_gather([shards[n].astype(CDT) for n in g] + (sharded_small if gi == 0 else []),
                                  f"gather{gi}", gi)
        for n, a in zip(g, lands):
            gathered[n] = KERNEL_LAYOUT[n.rstrip("01")](a)
        if gi == 0:
            alb, lng, lnb = [a.transpose(1, 0, 2).reshape(a.shape[1], D) for a in lands[len(g):]]

    getw = gathered.__getitem__

    sm = {"a_lower_bound": alb, "ln_gain": lng.reshape(2, 3, D), "ln_bias": lnb.reshape(2, 3, D),
          "a_norm_gain": a_norm_gain, "kv_b": kv_b, "b_b_q": b_b_q[0], "b_sinks": b_sinks, "b_b_out": b_b_out,
          "ple_b_gate": ple_b_gate}

    scatters, small_parts = [], {}

    def emit(grads, small=None):
        names = list(grads)
        blocks = [OWNER_BLOCKS[n.rstrip("01")](grads[n]) for n in names]
        partials = [] if small is None else [small[n].reshape(ps) for n, _, ps in SMALL] + [small["loss"]]
        lands = _sequencer_exchange(blocks + partials, ["scatter"] * len(blocks) + ["gather"] * len(partials),
                                    f"scatter{len(scatters)}", len(GATHER_GROUPS) + len(scatters))
        scatters.append(dict(zip(names, lands)))
        small_parts.update(zip([n for n, _, _ in SMALL] + ["loss"], lands[len(blocks):]))
        return blocks + (list(scatters[-4].values()) if len(scatters) >= 4 else [])

    loss, grad_x, gs = _local_step(x[0], p[:, 0], loss_target[0], getw, sm, emit)

    out, parts, last = {}, {}, [grad_x]
    for landed in scatters:
        parts.update(landed)
        for n, keys in ADAM_PARTS:
            if n in out or not all(key in parts for key in keys):
                continue
            lrc = (1,) * (3 - w[n].ndim) + w[n].shape
            shard = (lambda a: a.reshape(lrc).swapaxes(1, 2)) if n == "ffn_w_gate_up" else (lambda a: a.reshape(lrc))
            res = _adam_big(shard(w[n]), [parts[key] for key in keys], shard(m[n]), shard(v[n]), "adam_" + n, after=last)
            out[n] = [(r.swapaxes(1, 2) if n == "ffn_w_gate_up" else r).reshape(w[n].shape) for r in res]
            last = [res[3]]
    small_out, loss = _adam_small(small_parts, w, m, v, small_parts["loss"], after=last)
    out.update(small_out)
    res = [loss, grad_x[None]]
    for i in range(4):
        res += [out[n][i] for n in WEIGHTS]
    return tuple(res)
```

```python
import jax
import jax.numpy as jnp
from jax import lax
from jax.experimental import pallas as pl
from jax.experimental.pallas import tpu as pltpu
from jax.experimental.pallas import tpu_sc as plsc

F32 = jnp.float32
CDT = jnp.bfloat16

N_DEV = 8
D = 1024
HG_H, HG_DK, HG_CH = 8, 128, 64
HG_HPB = 4
HG_CPB = 8
ATT_HD, ATT_QH, ATT_KVH, ATT_G, WINDOW = 64, 16, 4, 4, 128
ATT_BPB = 1
FFN_H = 2816
FFN_B = FFN_H // 4
PLE_DIM = 256
ALPHA = (2.0 * 2) ** 0.25
LN_EPS = 1e-5
RMS_EPS = 1e-6
ADAM_LR, ADAM_B1, ADAM_B2, ADAM_EPS, ADAM_WD, ADAM_STEP = 0.001, 0.9, 0.999, 1e-08, 0.01, 10
ROW_TILES = (512, 256, 128, 64)
VMEM_LIMIT = 48 * 1024 * 1024
NEG = -1e30

MESH = pl.DeviceIdType.MESH


def _tile(n, cands=ROW_TILES):
    for t in cands:
        if n % t == 0:
            return t
    return n


def _sds(shape, dtype):
    return jax.ShapeDtypeStruct(tuple(shape), dtype)


def _params(sem):
    return pltpu.CompilerParams(dimension_semantics=sem, vmem_limit_bytes=VMEM_LIMIT)


def _dot(a, b):
    return jnp.dot(a.astype(CDT), b.astype(CDT), preferred_element_type=F32)


def _dot_nt(a, b):
    return lax.dot_general(a.astype(CDT), b.astype(CDT), (((1,), (1,)), ((), ())), preferred_element_type=F32)


def _dot_tn(a, b):
    return lax.dot_general(a.astype(CDT), b.astype(CDT), (((0,), (0,)), ((), ())), preferred_element_type=F32)


def _sigmoid(x):
    return jax.nn.sigmoid(x)


def _ln_fwd(z, g, b):
    mu = jnp.mean(z, axis=-1, keepdims=True)
    zc = z - mu
    var = jnp.mean(zc * zc, axis=-1, keepdims=True)
    return zc * lax.rsqrt(var + LN_EPS) * g + b


def _ln_bwd(z, g, dy):
    mu = jnp.mean(z, axis=-1, keepdims=True)
    zc = z - mu
    var = jnp.mean(zc * zc, axis=-1, keepdims=True)
    rstd = lax.rsqrt(var + LN_EPS)
    xhat = zc * rstd
    dxh = dy * g
    dz = rstd * (dxh - jnp.mean(dxh, axis=-1, keepdims=True) - xhat * jnp.mean(dxh * xhat, axis=-1, keepdims=True))
    return dz, xhat


def _colsum(x):
    return jnp.sum(x, axis=0, keepdims=True)


def _acc(ref, val, first):
    @pl.when(first)
    def _():
        ref[...] = val

    @pl.when(jnp.logical_not(first))
    def _():
        ref[...] += val


def _zero_at(ref, first):
    @pl.when(first)
    def _():
        ref[...] = jnp.zeros_like(ref)


def _call(after, body, **kw):
    after = [] if after is None else list(after)
    specs = list(kw["in_specs"])
    kw["in_specs"] = [pl.BlockSpec(memory_space=pl.ANY)] * len(after) + specs

    def ordered_body(*refs):
        body(*refs[len(after):])

    call = pl.pallas_call(ordered_body, **kw)

    def pinned(*args):
        args = [a if s.memory_space is not None else pltpu.with_memory_space_constraint(a, pltpu.HBM)
                for a, s in zip(args, specs)]
        return call(*after, *args)

    return pinned


def _mm_nn(a, b3, out_shape, oblock, omap, out_dtype, bias3=None, name="mm_nn"):
    M, K = a.shape
    G, _, Nb = b3.shape
    tm = _tile(M)

    def body(a_ref, b_ref, *rest):
        o_ref = rest[-1]
        acc = _dot(a_ref[...], b_ref[...])
        if bias3 is not None:
            acc = acc + rest[0][...]
        o_ref[...] = acc.astype(o_ref.dtype)

    in_specs = [pl.BlockSpec((tm, K), lambda g, i: (i, 0)), pl.BlockSpec((None, K, Nb), lambda g, i: (g, 0, 0))]
    args = [a, b3]
    if bias3 is not None:
        in_specs.append(pl.BlockSpec((None, 1, Nb), lambda g, i: (g, 0, 0)))
        args.append(bias3)
    return _call(
        None, body, grid=(G, M // tm), in_specs=in_specs, out_specs=pl.BlockSpec(oblock, omap),
        out_shape=_sds(out_shape, out_dtype), compiler_params=_params(("arbitrary", "arbitrary")), name=name)(*args)


def _mm_tn(a3, b3, G, amap, bmap, ablock, bblock, out_shape, oblock, omap, name="mm_tn"):
    S = a3.shape[1]

    def body(a_ref, b_ref, o_ref):
        o_ref[...] = _dot_tn(a_ref[...], b_ref[...]).astype(o_ref.dtype)

    return _call(
        None, body, grid=(G, 1),
        in_specs=[pl.BlockSpec(ablock(S), amap), pl.BlockSpec(bblock(S), bmap)],
        out_specs=pl.BlockSpec(oblock, omap), out_shape=_sds(out_shape, CDT),
        compiler_params=_params(("arbitrary", "arbitrary")), name=name)(a3, b3)


def _wgrad(a3, b3, name):
    Ga, S, M = a3.shape
    Gb, _, N = b3.shape
    G = max(Ga, Gb)
    return _mm_tn(
        a3, b3, G,
        (lambda g, k: (g, k, 0)) if Ga > 1 else (lambda g, k: (0, k, 0)),
        (lambda g, k: (g, k, 0)) if Gb > 1 else (lambda g, k: (0, k, 0)),
        lambda tk: (None, tk, M), lambda tk: (None, tk, N),
        (G, M, N), (None, M, N), lambda g, k: (g, 0, 0), name=name)


def _mixout_ln(u3, w3, bias, xin, gain, beta, name):
    G, S, Kb = u3.shape
    tm = _tile(S)

    def body(u_ref, w_ref, b_ref, x_ref, g_ref, be_ref, z_ref, xo_ref, xob_ref):
        h = b_ref[...] + _dot(u_ref[0], w_ref[0])
        for g in range(1, G):
            h = h + _dot(u_ref[g], w_ref[g])
        z = ALPHA * x_ref[...] + h
        z_ref[...] = z
        y = _ln_fwd(z, g_ref[...], be_ref[...])
        xo_ref[...] = y
        xob_ref[...] = y.astype(CDT)

    row = pl.BlockSpec((tm, D), lambda i: (i, 0))
    vec = pl.BlockSpec((1, D), lambda i: (0, 0))
    return _call(
        None, body, grid=(S // tm,),
        in_specs=[pl.BlockSpec((G, tm, Kb), lambda i: (0, i, 0)), pl.BlockSpec((G, Kb, D), lambda i: (0, 0, 0)),
                  vec, row, vec, vec],
        out_specs=[row, row, row], out_shape=[_sds((S, D), F32), _sds((S, D), F32), _sds((S, D), CDT)],
        compiler_params=_params(("arbitrary",)), name=name)(u3, w3, bias, xin, gain, beta)


def _ffn_fwd(xin, xin_b, wgu, wdn, gain, beta, name):
    S = xin.shape[0]
    tm = _tile(S)

    def hidden(xb_ref, wgu_ref, gu_ref, hid_ref):
        xb = xb_ref[...]
        gate = _dot_nt(xb, wgu_ref[0])
        up = _dot_nt(xb, wgu_ref[1])
        gu_ref[0] = gate.astype(CDT)
        gu_ref[1] = up.astype(CDT)
        hid_ref[...] = (gate * _sigmoid(gate) * up).astype(CDT)

    gu, hid = _call(
        None, hidden, grid=(4, S // tm),
        in_specs=[pl.BlockSpec((tm, D), lambda j, i: (i, 0)), pl.BlockSpec((2, None, FFN_B, D), lambda j, i: (0, j, 0, 0))],
        out_specs=[pl.BlockSpec((2, None, tm, FFN_B), lambda j, i: (0, j, i, 0)),
                   pl.BlockSpec((None, tm, FFN_B), lambda j, i: (j, i, 0))],
        out_shape=[_sds((2, 4, S, FFN_B), CDT), _sds((4, S, FFN_B), CDT)],
        compiler_params=_params(("arbitrary", "arbitrary")), name=name + "_hidden")(xin_b, wgu)

    def down(x_ref, hid_ref, wdn_ref, g_ref, be_ref, z_ref, xo_ref, xob_ref):
        z = ALPHA * x_ref[...]
        for j in range(4):
            z = z + _dot(hid_ref[j], wdn_ref[j])
        z_ref[...] = z
        y = _ln_fwd(z, g_ref[...], be_ref[...])
        xo_ref[...] = y
        xob_ref[...] = y.astype(CDT)

    row = pl.BlockSpec((tm, D), lambda i: (i, 0))
    vec = pl.BlockSpec((1, D), lambda i: (0, 0))
    z, xo, xob = _call(
        None, down, grid=(S // tm,),
        in_specs=[row, pl.BlockSpec((4, tm, FFN_B), lambda i: (0, i, 0)), pl.BlockSpec((4, FFN_B, D), lambda i: (0, 0, 0)),
                  vec, vec],
        out_specs=[row, row, row], out_shape=[_sds((S, D), F32), _sds((S, D), F32), _sds((S, D), CDT)],
        compiler_params=_params(("arbitrary",)), name=name + "_down")(xin, hid, wdn, gain, beta)
    return gu, hid, z, xo, xob


def _ple_fwd(xin, xin_b, p_b, wpg, bgate, wpu, gain, beta, name):
    S = xin.shape[0]
    tm = _tile(S)

    def body(x_ref, xb_ref, p_ref, wpg_ref, bg_ref, wpu_ref, g_ref, be_ref, sg_ref, up_ref, z_ref, xo_ref, xob_ref):
        sg = _sigmoid(_dot(xb_ref[...], wpg_ref[...]) + bg_ref[...])
        pb = p_ref[...]
        up = jnp.concatenate([_dot(pb, wpu_ref[j]) for j in range(N_DEV)], axis=-1)
        sg_ref[...] = sg.astype(CDT)
        up_ref[...] = (up * sg * (1.0 - sg)).astype(CDT)
        z = ALPHA * x_ref[...] + sg * up
        z_ref[...] = z
        y = _ln_fwd(z, g_ref[...], be_ref[...])
        xo_ref[...] = y
        xob_ref[...] = y.astype(CDT)

    row = pl.BlockSpec((tm, D), lambda i: (i, 0))
    vec = pl.BlockSpec((1, D), lambda i: (0, 0))
    return _call(
        None, body, grid=(S // tm,),
        in_specs=[row, row, pl.BlockSpec((tm, PLE_DIM), lambda i: (i, 0)), pl.BlockSpec((D, D), lambda i: (0, 0)), vec,
                  pl.BlockSpec((N_DEV, PLE_DIM, D // N_DEV), lambda i: (0, 0, 0)), vec, vec],
        out_specs=[row] * 5,
        out_shape=[_sds((S, D), CDT)] * 2 + [_sds((S, D), F32)] * 2 + [_sds((S, D), CDT)],
        compiler_params=_params(("arbitrary",)), name=name)(xin, xin_b, p_b, wpg, bgate, wpu, gain, beta)


def _loss_fwd_bwd(y, target):
    S = y.shape[0]
    tm = _tile(S)

    def body(y_ref, t_ref, l_ref, dy_ref):
        e = y_ref[...] - t_ref[...]
        dy_ref[...] = e * (1.0 / D)
        part = 0.5 * jnp.sum(jnp.sum(e * e, axis=-1, keepdims=True) * (1.0 / D), axis=0, keepdims=True)
        _acc(l_ref, jnp.broadcast_to(part, l_ref.shape), pl.program_id(0) == 0)

    row = pl.BlockSpec((tm, D), lambda i: (i, 0))
    return _call(
        None, body, grid=(S // tm,), in_specs=[row, row],
        out_specs=[pl.BlockSpec((1, 128), lambda i: (0, 0)), row],
        out_shape=[_sds((1, 128), F32), _sds((S, D), F32)],
        compiler_params=_params(("arbitrary",)), name="loss")(y, target)


def _ple_bwd(dy, z, sg, up, gain, wpg, name, after=None):
    S = dy.shape[0]
    tm = _tile(S)

    def body(dy_ref, z_ref, sg_ref, up_ref, g_ref, wpg_ref, dx_ref, dgl_ref, dup_ref, dgain_ref, dbeta_ref, dbg_ref):
        first = pl.program_id(0) == 0
        dy_ = dy_ref[...]
        dz, xhat = _ln_bwd(z_ref[...], g_ref[...], dy_)
        dgl = dz * up_ref[...].astype(F32)
        dgl_ref[...] = dgl.astype(CDT)
        dup_ref[...] = (dz * sg_ref[...].astype(F32)).astype(CDT)
        dx_ref[...] = ALPHA * dz + _dot_nt(dgl, wpg_ref[...])
        _acc(dgain_ref, _colsum(dy_ * xhat), first)
        _acc(dbeta_ref, _colsum(dy_), first)
        _acc(dbg_ref, _colsum(dgl), first)

    row = pl.BlockSpec((tm, D), lambda i: (i, 0))
    vec = pl.BlockSpec((1, D), lambda i: (0, 0))
    return _call(
        after, body, grid=(S // tm,), in_specs=[row, row, row, row, vec, pl.BlockSpec((D, D), lambda i: (0, 0))],
        out_specs=[row, row, row, vec, vec, vec],
        out_shape=[_sds((S, D), F32), _sds((S, D), CDT), _sds((S, D), CDT)] + [_sds((1, D), F32)] * 3,
        compiler_params=_params(("arbitrary",)), name=name)(dy, z, sg, up, gain, wpg)


def _ffn_bwd_hidden(dy, z, gu, wdn, gain, name, after=None):
    S = dy.shape[0]
    tm = _tile(S)

    def hidden(dy_ref, z_ref, gu_ref, wdn_ref, g_ref, dz_ref, dzb_ref, dgu_ref, dgain_ref, dbeta_ref):
        i, j = pl.program_id(0), pl.program_id(1)

        @pl.when(j == 0)
        def _():
            dy_ = dy_ref[...]
            dz, xhat = _ln_bwd(z_ref[...], g_ref[...], dy_)
            dz_ref[...] = dz
            dzb_ref[...] = dz.astype(CDT)
            _acc(dgain_ref, _colsum(dy_ * xhat), i == 0)
            _acc(dbeta_ref, _colsum(dy_), i == 0)

        dhid = _dot_nt(dzb_ref[...], wdn_ref[...])
        gate, up = gu_ref[0].astype(F32), gu_ref[1].astype(F32)
        sg = _sigmoid(gate)
        dgu_ref[0] = (dhid * up * (sg * (1.0 + gate * (1.0 - sg)))).astype(CDT)
        dgu_ref[1] = (dhid * (gate * sg)).astype(CDT)

    row = pl.BlockSpec((tm, D), lambda i, j: (i, 0))
    vec = pl.BlockSpec((1, D), lambda i, j: (0, 0))
    return _call(
        after, hidden, grid=(S // tm, 4),
        in_specs=[row, row, pl.BlockSpec((2, None, tm, FFN_B), lambda i, j: (0, j, i, 0)),
                  pl.BlockSpec((None, FFN_B, D), lambda i, j: (j, 0, 0)), vec],
        out_specs=[row, row, pl.BlockSpec((2, None, tm, FFN_B), lambda i, j: (0, j, i, 0)), vec, vec],
        out_shape=[_sds((S, D), F32), _sds((S, D), CDT), _sds((2, 4, S, FFN_B), CDT), _sds((1, D), F32),
                   _sds((1, D), F32)],
        compiler_params=_params(("arbitrary", "arbitrary")), name=name + "_hidden")(dy, z, gu, wdn, gain)


def _ffn_bwd_input(dz, dgu, wgu, name, after=None):
    S = dz.shape[0]
    tm = _tile(S)

    def to_input(dz_ref, dgu_ref, wgu_ref, dx_ref):
        acc = ALPHA * dz_ref[...]
        for g in range(2):
            for j in range(4):
                acc = acc + _dot(dgu_ref[g, j], wgu_ref[g, j])
        dx_ref[...] = acc

    rows = pl.BlockSpec((tm, D), lambda i: (i, 0))
    return _call(
        after, to_input, grid=(S // tm,),
        in_specs=[rows, pl.BlockSpec((2, 4, tm, FFN_B), lambda i: (0, 0, i, 0)),
                  pl.BlockSpec((2, 4, FFN_B, D), lambda i: (0, 0, 0, 0))],
        out_specs=rows, out_shape=_sds((S, D), F32),
        compiler_params=_params(("arbitrary",)), name=name + "_input")(dz, dgu, wgu)


def _mixout_bwd(dy, z, gain, w3, du_dtype, name, after=None):
    S = dy.shape[0]
    G, Kb, _ = w3.shape
    tm = _tile(S)

    def body(dy_ref, z_ref, g_ref, w_ref, dz_ref, dzb_ref, du_ref, dgain_ref, dbeta_ref, dbias_ref):
        first = pl.program_id(0) == 0
        dy_ = dy_ref[...]
        dz, xhat = _ln_bwd(z_ref[...], g_ref[...], dy_)
        dz_ref[...] = dz
        dzb = dz.astype(CDT)
        dzb_ref[...] = dzb
        for g in range(G):
            du_ref[g] = _dot_nt(dzb, w_ref[g]).astype(du_ref.dtype)
        _acc(dgain_ref, _colsum(dy_ * xhat), first)
        _acc(dbeta_ref, _colsum(dy_), first)
        _acc(dbias_ref, _colsum(dz), first)

    row = pl.BlockSpec((tm, D), lambda i: (i, 0))
    vec = pl.BlockSpec((1, D), lambda i: (0, 0))
    return _call(
        after, body, grid=(S // tm,), in_specs=[row, row, vec, pl.BlockSpec((G, Kb, D), lambda i: (0, 0, 0))],
        out_specs=[row, row, pl.BlockSpec((G, tm, Kb), lambda i: (0, i, 0)), vec, vec, vec],
        out_shape=[_sds((S, D), F32), _sds((S, D), CDT), _sds((G, S, Kb), du_dtype)] + [_sds((1, D), F32)] * 3,
        compiler_params=_params(("arbitrary",)), name=name)(dy, z, gain, w3)


def _half_select(low):
    r = lax.broadcasted_iota(jnp.int32, (2 * ATT_HD, ATT_HD), 0)
    c = lax.broadcasted_iota(jnp.int32, (2 * ATT_HD, ATT_HD), 1)
    return (r == c + (0 if low else ATT_HD)).astype(CDT)


def _half_place(low):
    r = lax.broadcasted_iota(jnp.int32, (ATT_HD, 2 * ATT_HD), 0)
    c = lax.broadcasted_iota(jnp.int32, (ATT_HD, 2 * ATT_HD), 1)
    return (c == r + (0 if low else ATT_HD)).astype(CDT)


def _pair_lanes(even, odd):
    return (jnp.dot(even, _half_place(True), preferred_element_type=F32)
            + jnp.dot(odd, _half_place(False), preferred_element_type=F32)).astype(CDT)


def _proj_heads(a, w, bias, heads, name):
    S, K = a.shape
    N = heads * ATT_HD
    tm = _tile(S)

    def body(a_ref, w_ref, b_ref, o_ref):
        acc = (_dot(a_ref[...], w_ref[...]) + b_ref[...]).astype(CDT)
        sel = (_half_select(True), _half_select(False))
        for h in range(heads):
            pair = acc[:, (h // 2) * 2 * ATT_HD:(h // 2 + 1) * 2 * ATT_HD]
            o_ref[h] = jnp.dot(pair, sel[h % 2], preferred_element_type=F32).astype(CDT)

    return _call(
        None, body, grid=(S // tm,),
        in_specs=[pl.BlockSpec((tm, K), lambda i: (i, 0)), pl.BlockSpec((K, N), lambda i: (0, 0)),
                  pl.BlockSpec((1, N), lambda i: (0, 0))],
        out_specs=pl.BlockSpec((heads, tm, ATT_HD), lambda i: (0, i, 0)), out_shape=_sds((heads, S, ATT_HD), CDT),
        compiler_params=_params(("arbitrary",)), name=name)(a, w, bias)


def _qkv_bwd(dz, dq, dkv4, wq, wkv, name, after=None):
    S = dz.shape[0]
    tm = _tile(S)
    HK = dkv4.shape[0]
    NK = HK * ATT_HD

    def body(dz_ref, dq_ref, dkv_ref, wq_ref, wkv_ref, dx_ref, dkvn_ref, dkvb_ref):
        first = pl.program_id(0) == 0
        dkvn = jnp.concatenate([_pair_lanes(dkv_ref[2 * i].astype(CDT), dkv_ref[2 * i + 1].astype(CDT))
                                for i in range(HK // 2)], axis=-1)
        dkvn_ref[...] = dkvn
        dx_ref[...] = ALPHA * dz_ref[...] + _dot_nt(dq_ref[...], wq_ref[...]) + _dot_nt(dkvn, wkv_ref[...])
        for h in range(HK):
            _acc(dkvb_ref.at[h], _colsum(dkv_ref[h]), first)

    row = pl.BlockSpec((tm, D), lambda i: (i, 0))
    return _call(
        after, body, grid=(S // tm,),
        in_specs=[row, row, pl.BlockSpec((HK, tm, ATT_HD), lambda i: (0, i, 0)),
                  pl.BlockSpec((D, D), lambda i: (0, 0)), pl.BlockSpec((D, NK), lambda i: (0, 0))],
        out_specs=[row, pl.BlockSpec((tm, NK), lambda i: (i, 0)), pl.BlockSpec((HK, 1, ATT_HD), lambda i: (0, 0, 0))],
        out_shape=[_sds((S, D), F32), _sds((S, NK), CDT), _sds((HK, 1, ATT_HD), F32)],
        compiler_params=_params(("arbitrary",)), name=name)(dz, dq, dkv4, wq, wkv)


def _inproj_bwd(dz, dproj, wain, name, after=None):
    S = dz.shape[0]
    tm = _tile(S)
    nb = wain.shape[-1]

    def body(dz_ref, dp_ref, w_ref, dx_ref):
        acc = ALPHA * dz_ref[...]
        for j in range(N_DEV):
            acc = acc + _dot_nt(dp_ref[j // 2, :, pl.ds((j % 2) * nb, nb)], w_ref[j])
        dx_ref[...] = acc

    row = pl.BlockSpec((tm, D), lambda i: (i, 0))
    return _call(
        after, body, grid=(S // tm,),
        in_specs=[row, pl.BlockSpec((4, tm, D), lambda i: (0, i, 0)), pl.BlockSpec((N_DEV, D, nb), lambda i: (0, 0, 0))],
        out_specs=row, out_shape=_sds((S, D), F32),
        compiler_params=_params(("arbitrary",)), name=name)(dz, dproj, wain)


def _running_sum(x, reverse=False):
    rows = x.shape[0]
    row = lax.broadcasted_iota(jnp.int32, x.shape, 0)
    step = 1
    while step < rows:
        if reverse:
            x = x + jnp.where(row < rows - step, pltpu.roll(x, rows - step, 0), 0.0)
        else:
            x = x + jnp.where(row >= step, pltpu.roll(x, step, 0), 0.0)
        step *= 2
    return x


def _hg_gates(q, f, alb_ref):
    a0, a1 = alb_ref[0:1, :], alb_ref[1:2, :]
    mx = jnp.maximum(a0, a1)
    e0, e1 = jnp.exp(a0 - mx), jnp.exp(a1 - mx)
    lb = e0 / (e0 + e1)
    sig = _sigmoid(f)
    forget = lb + (1.0 - lb) * sig
    k = (1.0 - lb) * _sigmoid(-f)
    qs = q * _sigmoid(q) * (HG_DK ** -0.5)
    return qs, k, jnp.log(forget), sig, lb, forget


def _hg_intra(qs, k, b, b_scr):
    b_scr[...] = b
    bm = b_scr[pl.ds(HG_CH // 2 - 1, 1), :]
    bl = b_scr[pl.ds(HG_CH - 1, 1), :]
    eb = jnp.exp(b)
    qb = qs * eb
    e_q = jnp.exp(b - bm)
    e_k = jnp.exp(bm - b)
    e_d = jnp.exp(bl - b)
    return qb, qs * e_q, k * e_k, k * e_d, jnp.exp(bl), eb, e_q, e_k, e_d


def _hgrn_fwd(proj, alb, ngain):
    S = proj.shape[1]
    nc = S // HG_CH
    nb = nc // HG_CPB
    rb, wb = HG_CPB * HG_CH, HG_HPB * HG_DK

    def body(pj_ref, alb_ref, ng_ref, o_ref, y_ref, st_ref, st_scr, b_scr):
        n = pl.program_id(1)

        @pl.when(n == 0)
        def _():
            st_scr[...] = jnp.zeros_like(st_scr)

        r = lax.broadcasted_iota(jnp.int32, (HG_CH, HG_CH), 0)
        c = lax.broadcasted_iota(jnp.int32, (HG_CH, HG_CH), 1)
        causal = r >= c
        for ci, j in [(ci, j) for ci in range(HG_CPB) for j in range(HG_HPB)]:
            rows, lanes = pl.ds(ci * HG_CH, HG_CH), pl.ds(j * HG_DK, HG_DK)
            q, f, v, g = pj_ref[0, rows, lanes], pj_ref[1, rows, lanes], pj_ref[2, rows, lanes], pj_ref[3, rows, lanes]
            qs, k, logf, _, _, _ = _hg_gates(q, f, alb_ref.at[:, lanes])
            b = _running_sum(logf)
            qb, qt, kt, kd, ebl, _, _, _, _ = _hg_intra(qs, k, b, b_scr.at[j, ci])
            st = st_scr[j]
            st_ref[j, ci] = st
            a = jnp.where(causal, _dot_nt(qt, kt), 0.0)
            o = _dot(a, v) + _dot_nt(qb, st)
            st_scr[j] = st * ebl + _dot_tn(v, kd)
            o_ref[rows, lanes] = o
            rinv = lax.rsqrt(jnp.mean(o * o, axis=-1, keepdims=True) + RMS_EPS)
            y_ref[rows, lanes] = (o * rinv * ng_ref[...] * (g * _sigmoid(g))).astype(CDT)

    blk = pl.BlockSpec((rb, wb), lambda h, n: (n, h))
    return _call(
        None, body, grid=(HG_H // HG_HPB, nb),
        in_specs=[pl.BlockSpec((4, rb, wb), lambda h, n: (0, n, h)), pl.BlockSpec((2, wb), lambda h, n: (0, h)),
                  pl.BlockSpec((1, HG_DK), lambda h, n: (0, 0))],
        out_specs=[blk, blk, pl.BlockSpec((HG_HPB, HG_CPB, HG_DK, HG_DK), lambda h, n: (h, n, 0, 0))],
        out_shape=[_sds((S, D), F32), _sds((S, D), CDT), _sds((HG_H, nc, HG_DK, HG_DK), F32)],
        scratch_shapes=[pltpu.VMEM((HG_HPB, HG_DK, HG_DK), F32), pltpu.VMEM((HG_HPB, HG_CPB, HG_CH, HG_DK), F32)],
        compiler_params=_params(("arbitrary", "arbitrary")), name="hgrn_fwd")(proj, alb, ngain)


def _hgrn_bwd(proj, alb, ngain, o, states, dy, after=None):
    S = proj.shape[1]
    nc = S // HG_CH
    nb = nc // HG_CPB
    rb, wb = HG_CPB * HG_CH, HG_HPB * HG_DK

    def body(pj_ref, alb_ref, ng_ref, o_ref, st_ref, dy_ref, dpj_ref, dalb_ref, dng_ref, dst_scr, b_scr):
        h, n = pl.program_id(0), pl.program_id(1)

        @pl.when(n == 0)
        def _():
            dst_scr[...] = jnp.zeros_like(dst_scr)
            dalb_ref[...] = jnp.zeros_like(dalb_ref)

        _zero_at(dng_ref, jnp.logical_and(h == 0, n == 0))
        ng = ng_ref[...]
        r = lax.broadcasted_iota(jnp.int32, (HG_CH, HG_CH), 0)
        c = lax.broadcasted_iota(jnp.int32, (HG_CH, HG_CH), 1)
        causal = r >= c
        dng = None
        for ci, j in [(ci, j) for ci in reversed(range(HG_CPB)) for j in range(HG_HPB)]:
            rows, lanes = pl.ds(ci * HG_CH, HG_CH), pl.ds(j * HG_DK, HG_DK)
            q, f, v, g = pj_ref[0, rows, lanes], pj_ref[1, rows, lanes], pj_ref[2, rows, lanes], pj_ref[3, rows, lanes]
            o_ = o_ref[rows, lanes]
            dy_ = dy_ref[rows, lanes]
            sg = _sigmoid(g)
            rinv = lax.rsqrt(jnp.mean(o_ * o_, axis=-1, keepdims=True) + RMS_EPS)
            nrm = o_ * rinv
            dr = dy_ * (g * sg)
            dg = dy_ * nrm * ng * (sg * (1.0 + g * (1.0 - sg)))
            dn = dr * ng
            do = rinv * (dn - nrm * jnp.mean(dn * nrm, axis=-1, keepdims=True))
            dng = _colsum(dr * nrm) if dng is None else dng + _colsum(dr * nrm)
            qs, k, logf, sig, lb, forget = _hg_gates(q, f, alb_ref.at[:, lanes])
            b = _running_sum(logf)
            qb, qt, kt, kd, ebl, eb, e_q, e_k, e_d = _hg_intra(qs, k, b, b_scr.at[j, ci])
            st = st_ref[j, ci]
            dstn = dst_scr[j]
            qt, kt, qb, kd = (t.astype(CDT).astype(F32) for t in (qt, kt, qb, kd))
            a = jnp.where(causal, _dot_nt(qt, kt), 0.0)
            da = jnp.where(causal, _dot_nt(do, v), 0.0)
            dv = _dot_tn(a, do) + _dot_nt(kd, dstn)
            dqb = _dot(do, st)
            dkd = _dot(v, dstn)
            dqt = _dot(da, kt)
            dkt = _dot_tn(da, qt)
            dbl = _colsum(dkd * kd) + ebl * _colsum(dstn * st)
            dst_scr[j] = dstn * ebl + _dot_tn(do, qb)
            dqs = dqt * e_q + dqb * eb
            dk = dkt * e_k + dkd * e_d
            db = dqt * qt + dqb * qb - dkt * kt - dkd * kd
            dlogf = _running_sum(db, reverse=True) + dbl
            dforget = dlogf / forget
            dsig = (1.0 - lb) * (dforget - dk)
            df = dsig * sig * (1.0 - sig)
            dlb = _colsum((dforget - dk) * (1.0 - sig))
            sq = _sigmoid(q)
            dq = dqs * (HG_DK ** -0.5) * (sq * (1.0 + q * (1.0 - sq)))
            dpj_ref[0, rows, lanes] = dq.astype(CDT)
            dpj_ref[1, rows, lanes] = df.astype(CDT)
            dpj_ref[2, rows, lanes] = dv.astype(CDT)
            dpj_ref[3, rows, lanes] = dg.astype(CDT)
            da0 = dlb * lb * (1.0 - lb)
            dalb_ref[pl.ds(0, 1), lanes] += da0
            dalb_ref[pl.ds(1, 1), lanes] -= da0
        dng_ref[...] += dng

    blk = pl.BlockSpec((rb, wb), lambda h, n: (nb - 1 - n, h))
    pj = pl.BlockSpec((4, rb, wb), lambda h, n: (0, nb - 1 - n, h))
    alb_blk = pl.BlockSpec((2, wb), lambda h, n: (0, h))
    ng_blk = pl.BlockSpec((1, HG_DK), lambda h, n: (0, 0))
    return _call(
        after, body, grid=(HG_H // HG_HPB, nb),
        in_specs=[pj, alb_blk, ng_blk, blk,
                  pl.BlockSpec((HG_HPB, HG_CPB, HG_DK, HG_DK), lambda h, n: (h, nb - 1 - n, 0, 0)), blk],
        out_specs=[pj, alb_blk, ng_blk],
        out_shape=[_sds((4, S, D), CDT), _sds((2, D), F32), _sds((1, HG_DK), F32)],
        scratch_shapes=[pltpu.VMEM((HG_HPB, HG_DK, HG_DK), F32), pltpu.VMEM((HG_HPB, HG_CPB, HG_CH, HG_DK), F32)],
        compiler_params=_params(("arbitrary", "arbitrary")), name="hgrn_bwd")(proj, alb, ngain, o, states, dy)


def _slope(h):
    return 2.0 ** (-8.0 * (h + 1) / ATT_QH)


def _attn_mask(n):
    qi = lax.broadcasted_iota(jnp.int32, (WINDOW, 2 * WINDOW), 0)
    si = lax.broadcasted_iota(jnp.int32, (WINDOW, 2 * WINDOW), 1)
    dist = qi - si + WINDOW
    valid = (dist >= 0) & (dist < WINDOW) & (n * WINDOW - WINDOW + si >= 0)
    return valid, dist.astype(F32)


def _attn_probs(qh, kh, sink, slope, valid, distf):
    s = _dot_nt(qh, kh) * (ATT_HD ** -0.5) - slope * distf
    s = jnp.where(valid, s, NEG)
    m = jnp.maximum(jnp.max(s, axis=-1, keepdims=True), sink)
    e = jnp.exp(s - m)
    es = jnp.exp(sink - m)
    inv = 1.0 / (jnp.sum(e, axis=-1, keepdims=True) + es)
    return e * inv, es * inv


def _attn_specs(S):
    steps = S // (ATT_BPB * WINDOW)
    cur = lambda H: pl.BlockSpec((H, ATT_BPB * WINDOW, ATT_HD), lambda n: (0, n, 0))
    prev = lambda H: pl.BlockSpec((H, WINDOW, ATT_HD), lambda n: (0, jnp.maximum(ATT_BPB * n - 1, 0), 0))
    return steps, cur, prev


def _attn_kv(kvc_ref, kvp_ref, head, bi):
    before = kvp_ref[head] if bi == 0 else kvc_ref[head, pl.ds((bi - 1) * WINDOW, WINDOW), :]
    return jnp.concatenate([before, kvc_ref[head, pl.ds(bi * WINDOW, WINDOW), :]], axis=0)


def _attn_fwd(q4, kv4, sinks):
    S = q4.shape[1]
    nb, cur, prev = _attn_specs(S)

    def body(sink_ref, q_ref, kvc_ref, kvp_ref, o_ref, p_ref, ps_ref):
        lane = lax.broadcasted_iota(jnp.int32, (1, 128), 1)
        for bi in range(ATT_BPB):
            valid, distf = _attn_mask(pl.program_id(0) * ATT_BPB + bi)
            rows = pl.ds(bi * WINDOW, WINDOW)
            sink_probs = jnp.zeros((WINDOW, 128), F32)
            for kvh in range(ATT_KVH):
                kh = _attn_kv(kvc_ref, kvp_ref, kvh, bi)
                vh = _attn_kv(kvc_ref, kvp_ref, ATT_KVH + kvh, bi)
                v_low = jnp.dot(vh, _half_place(True), preferred_element_type=F32).astype(CDT)
                v_high = jnp.dot(vh, _half_place(False), preferred_element_type=F32).astype(CDT)
                for h in range(kvh * ATT_G, (kvh + 1) * ATT_G, 2):
                    pair = []
                    for hh in (h, h + 1):
                        p, ps = _attn_probs(q_ref[hh, rows, :], kh, sink_ref[0, hh], _slope(hh), valid, distf)
                        p = p.astype(CDT)
                        p_ref[hh, rows, :] = p
                        sink_probs = sink_probs + jnp.where(lane == hh, ps, 0.0)
                        pair.append(p)
                    o_ref[rows, pl.ds(h * ATT_HD, 2 * ATT_HD)] = (_dot(pair[0], v_low) + _dot(pair[1], v_high)).astype(CDT)
            ps_ref[rows, :] = sink_probs

    rows_spec = pl.BlockSpec((ATT_BPB * WINDOW, D), lambda n: (n, 0))
    return _call(
        None, body, grid=(nb,),
        in_specs=[pl.BlockSpec(memory_space=pltpu.SMEM), cur(ATT_QH), cur(2 * ATT_KVH), prev(2 * ATT_KVH)],
        out_specs=[rows_spec, pl.BlockSpec((ATT_QH, ATT_BPB * WINDOW, 2 * WINDOW), lambda n: (0, n, 0)),
                   pl.BlockSpec((ATT_BPB * WINDOW, 128), lambda n: (n, 0))],
        out_shape=[_sds((S, D), CDT), _sds((ATT_QH, S, 2 * WINDOW), CDT), _sds((S, 128), F32)],
        compiler_params=_params(("arbitrary",)), name="attn_fwd")(sinks, q4, kv4, kv4)


def _attn_bwd(q4, kv4, probs, sink_probs, do):
    S = q4.shape[1]
    nb, cur, prev = _attn_specs(S)

    def body(q_ref, kvc_ref, kvp_ref, p_ref, ps_ref, do_ref, dq_ref, dkv_ref, dbq_ref, dsink_ref):
        n = pl.program_id(0)
        first = n == 0

        @pl.when(first)
        def _():
            dkv_ref[...] = jnp.zeros_like(dkv_ref)
            dsink_ref[...] = jnp.zeros_like(dsink_ref)
            dbq_ref[...] = jnp.zeros_like(dbq_ref)

        lane = lax.broadcasted_iota(jnp.int32, (1, 128), 1)
        dsinks = jnp.zeros((1, 128), F32)
        sel = (_half_select(True), _half_select(False))
        row_dots = [jnp.zeros((WINDOW, 128), F32) for _ in range(ATT_BPB)]
        for bi, kvh in [(bi, kvh) for bi in range(ATT_BPB) for kvh in range(ATT_KVH)]:
            block = n * ATT_BPB + bi
            rows = pl.ds(bi * WINDOW, WINDOW)
            rows_cur = pl.ds(pl.multiple_of(block * WINDOW, WINDOW), WINDOW)
            rows_prev = pl.ds(pl.multiple_of(jnp.maximum(block - 1, 0) * WINDOW, WINDOW), WINDOW)
            kh = _attn_kv(kvc_ref, kvp_ref, kvh, bi)
            vh = _attn_kv(kvc_ref, kvp_ref, ATT_KVH + kvh, bi)
            dk = dv = None
            dqs = []
            for h in range(kvh * ATT_G, (kvh + 1) * ATT_G):
                qh = q_ref[h, rows, :]
                doh = jnp.dot(do_ref[rows, pl.ds((h // 2) * 2 * ATT_HD, 2 * ATT_HD)], sel[h % 2],
                              preferred_element_type=F32).astype(CDT)
                p = p_ref[h, rows, :].astype(F32)
                dp = _dot_nt(doh, vh)
                dd = jnp.sum(p * dp, axis=-1, keepdims=True)
                ds = p * (dp - dd)
                row_dots[bi] = row_dots[bi] + jnp.where(lane == h, dd, 0.0)
                dqh = _dot(ds, kh) * (ATT_HD ** -0.5)
                dqs.append(dqh.astype(CDT))
                dbq_ref[h] += _colsum(dqh)
                dkh = _dot_tn(ds, qh) * (ATT_HD ** -0.5)
                dvh = _dot_tn(p, doh)
                dk = dkh if dk is None else dk + dkh
                dv = dvh if dv is None else dv + dvh
            for i in range(ATT_G // 2):
                lanes = pl.ds((kvh * ATT_G + 2 * i) * ATT_HD, 2 * ATT_HD)
                dq_ref[rows, lanes] = _pair_lanes(dqs[2 * i], dqs[2 * i + 1])
            dkv_ref[kvh, rows_prev, :] += dk[:WINDOW]
            dkv_ref[kvh, rows_cur, :] += dk[WINDOW:]
            dkv_ref[ATT_KVH + kvh, rows_prev, :] += dv[:WINDOW]
            dkv_ref[ATT_KVH + kvh, rows_cur, :] += dv[WINDOW:]
        for bi in range(ATT_BPB):
            dsinks = dsinks - _colsum(ps_ref[pl.ds(bi * WINDOW, WINDOW), :] * row_dots[bi])
        dsink_ref[...] += dsinks

    rows_spec = pl.BlockSpec((ATT_BPB * WINDOW, D), lambda n: (n, 0))
    return _call(
        None, body, grid=(nb,),
        in_specs=[cur(ATT_QH), cur(2 * ATT_KVH), prev(2 * ATT_KVH),
                  pl.BlockSpec((ATT_QH, ATT_BPB * WINDOW, 2 * WINDOW), lambda n: (0, n, 0)),
                  pl.BlockSpec((ATT_BPB * WINDOW, 128), lambda n: (n, 0)), rows_spec],
        out_specs=[rows_spec, pl.BlockSpec((2 * ATT_KVH, S, ATT_HD), lambda n: (0, 0, 0)),
                   pl.BlockSpec((ATT_QH, 1, ATT_HD), lambda n: (0, 0, 0)), pl.BlockSpec((1, 128), lambda n: (0, 0))],
        out_shape=[_sds((S, D), CDT), _sds((2 * ATT_KVH, S, ATT_HD), F32), _sds((ATT_QH, 1, ATT_HD), F32),
                   _sds((1, 128), F32)],
        compiler_params=_params(("arbitrary",)), name="attn_bwd")(q4, kv4, kv4, probs, sink_probs, do)


def _local_step(x, p, target, getw, sm, emit):
    S = x.shape[0]
    vec = lambda a: a.reshape(1, -1)
    ln_g = lambda l, k: vec(sm["ln_gain"][l, k])
    ln_b = lambda l, k: vec(sm["ln_bias"][l, k])
    xb = x.astype(CDT)
    pb = p.astype(CDT)

    proj = _mm_nn(xb, getw("a_w_in"), (4, S, D), (None, _tile(S), 512), lambda g, i: (g // 2, i, g % 2), F32,
                  name="a_in")
    o_a, y_a, states = _hgrn_fwd(proj, sm["a_lower_bound"], sm["a_norm_gain"])
    zeros = jnp.zeros((1, D), F32)
    z = [[None] * 3 for _ in range(2)]
    xs = [[None] * 3 for _ in range(2)]
    xbs = [[None] * 3 for _ in range(2)]
    z[0][0], xs[0][0], xbs[0][0] = _mixout_ln(y_a[None], getw("a_w_out")[None], zeros, x, ln_g(0, 0), ln_b(0, 0),
                                              "a_out_ln")
    gu, hid, sgs, ups = [None, None], [None, None], [None, None], [None, None]

    def ffn_ple(l):
        wgu = getw(f"gu{l}")
        gu[l], hid[l], z[l][1], xs[l][1], xbs[l][1] = _ffn_fwd(
            xs[l][0], xbs[l][0], wgu, getw(f"dn{l}"), ln_g(l, 1), ln_b(l, 1), f"ffn_fwd{l}")
        sgs[l], ups[l], z[l][2], xs[l][2], xbs[l][2] = _ple_fwd(
            xs[l][1], xbs[l][1], pb[l], getw(f"pg{l}"), vec(sm["ple_b_gate"][l]), getw(f"pu{l}"), ln_g(l, 2),
            ln_b(l, 2), f"ple_fwd{l}")

    ffn_ple(0)
    x3, x3b = xs[0][2], xbs[0][2]
    w_kv, w_q, w_bo = getw("kv_w"), getw("b_w_q"), getw("b_w_out")
    kv4 = _proj_heads(x3b, w_kv, vec(sm["kv_b"]), 2 * ATT_KVH, "kv_proj")
    q4 = _proj_heads(x3b, w_q, vec(sm["b_b_q"]), ATT_QH, "q_proj")
    o_b, probs, sink_probs = _attn_fwd(q4, kv4, sm["b_sinks"])
    z[1][0], xs[1][0], xbs[1][0] = _mixout_ln(o_b[None], w_bo[None], sm["b_b_out"], x3, ln_g(1, 0), ln_b(1, 0),
                                              "b_out_ln")
    ffn_ple(1)
    loss, dy = _loss_fwd_bwd(xs[1][2], target)

    gs = {}
    d_ln_g = [[None] * 3 for _ in range(2)]
    d_ln_b = [[None] * 3 for _ in range(2)]
    g_bg = [None, None]

    def ffn_ple_bwd(l, dy, after=None):
        dx2, dgl, dup, d_ln_g[l][2], d_ln_b[l][2], g_bg[l] = _ple_bwd(dy, z[l][2], sgs[l], ups[l], ln_g(l, 2),
                                                                     getw(f"pg{l}"), f"ple_bwd{l}", after=after)
        g_pg = _wgrad(xbs[l][1][None], dgl[None], f"g_ple_gate{l}")[0]
        g_pu = _wgrad(pb[l][None], dup[None], f"g_ple_up{l}")[0]
        tok = emit({f"pg{l}": g_pg, f"pu{l}": g_pu})
        dz2, dzb, dgu, d_ln_g[l][1], d_ln_b[l][1] = _ffn_bwd_hidden(dx2, z[l][1], gu[l], getw(f"dn{l}"), ln_g(l, 1),
                                                                   f"ffn_bwd{l}", after=tok)
        tok = emit({f"dn{l}": _wgrad(hid[l], dzb[None], f"g_ffn_down{l}")})
        g_gu = _wgrad(dgu.reshape(8, S, FFN_B), xbs[l][0][None], f"g_ffn_gate_up{l}")
        tok = tok + emit({f"gu{l}": g_gu})
        dx1 = _ffn_bwd_input(dz2, dgu, getw(f"gu{l}"), f"ffn_bwd{l}", after=tok)
        return dx1, None

    dx1, tok = ffn_ple_bwd(1, dy)
    dz, dzb, do, d_ln_g[1][0], d_ln_b[1][0], gs["b_b_out"] = _mixout_bwd(dx1, z[1][0], ln_g(1, 0), w_bo[None], CDT,
                                                                        "b_out_bwd", after=tok)
    g_bo = _wgrad(o_b[None], dzb[None], "g_b_w_out")[0]
    dq, dkv4, dbq, dsinks = _attn_bwd(q4, kv4, probs, sink_probs, do[0])
    gs["b_b_q"] = dbq
    gs["b_sinks"] = dsinks
    g_q = _wgrad(x3b[None], dq[None], "g_b_w_q")[0]
    dx3, dkv, gs["kv_b"] = _qkv_bwd(dz, dq, dkv4, w_q, w_kv, "qkv_bwd", after=tok)
    g_kv = _wgrad(x3b[None], dkv[None], "g_kv_w")[0]
    tok = emit({"b_w_out": g_bo, "b_w_q": g_q, "kv_w": g_kv})
    dx1, tok = ffn_ple_bwd(0, dx3, tok)
    w_ao = getw("a_w_out")
    dz, dzb, dyr, d_ln_g[0][0], d_ln_b[0][0], _ = _mixout_bwd(dx1, z[0][0], ln_g(0, 0), w_ao[None], F32, "a_out_bwd",
                                                              after=tok)
    g_ao = _wgrad(y_a[None], dzb[None], "g_a_w_out")[0]
    tok = emit({"a_w_out": g_ao})
    dproj, gs["a_lower_bound"], gs["a_norm_gain"] = _hgrn_bwd(proj, sm["a_lower_bound"], sm["a_norm_gain"], o_a, states,
                                                              dyr[0], after=tok)
    tk = lambda t: (None, t, D)
    g_ain = _mm_tn(xb[None], dproj, N_DEV, lambda g, k: (0, k, 0), lambda g, k: (g // 2, k, g % 2),
                   tk, lambda t: (None, t, 512), (N_DEV, D, 512), (None, D, 512), lambda g, k: (g, 0, 0), name="g_a_w_in")
    gs["ple_b_gate"] = jnp.concatenate(g_bg, axis=0)
    gs["ln_gain"] = jnp.stack([jnp.concatenate(r, axis=0) for r in d_ln_g])
    gs["ln_bias"] = jnp.stack([jnp.concatenate(r, axis=0) for r in d_ln_b])
    gs["loss"] = loss
    tok = emit({"a_w_in": g_ain}, small=gs)
    grad_x = _inproj_bwd(dz, dproj, getw("a_w_in"), "a_in_bwd", after=tok)
    return loss, grad_x, gs


def _peer(k):
    x, y, c = lax.axis_index("x"), lax.axis_index("y"), lax.axis_index("c")
    px = 1 - x if k & 4 else x
    py = 1 - y if k & 2 else y
    pc = 1 - c if k & 1 else c
    return (px, py, pc), 4 * px + 2 * py + pc


def _my_index():
    return 4 * lax.axis_index("x") + 2 * lax.axis_index("y") + lax.axis_index("c")


def _piece_copy(mode, src, land, send_sems, recv_sems, t, k, sender, receiver, peer):
    return pltpu.make_async_remote_copy(
        src_ref=src if mode == "gather" else src.at[receiver], dst_ref=land.at[sender],
        send_sem=send_sems.at[t * 7 + k - 1], recv_sem=recv_sems.at[t * 7 + k - 1], device_id=peer, device_id_type=MESH)


def _sequencer_exchange(srcs, modes, name, collective_id, after=None):
    n = len(srcs)
    land_shapes = [((N_DEV,) + a.shape) if mode == "gather" else a.shape for a, mode in zip(srcs, modes)]
    extra = [] if after is None else [after]

    def body(*refs):
        src_refs, land_refs = refs[:n], refs[n + len(extra):2 * n + len(extra)]
        send_sems, recv_sems, local_sems = refs[2 * n + len(extra):]
        barrier = pltpu.get_barrier_semaphore()
        for k in range(1, N_DEV):
            pl.semaphore_signal(barrier, inc=1, device_id=_peer(k)[0], device_id_type=MESH)
        pl.semaphore_wait(barrier, N_DEV - 1)
        me = _my_index()
        local = []
        for i in range(n):
            cp = pltpu.make_async_copy(src_refs[i] if modes[i] == "gather" else src_refs[i].at[me], land_refs[i].at[me],
                                       local_sems.at[i])
            cp.start()
            local.append(cp)
        for k in range(1, N_DEV):
            peer, pid = _peer(k)
            for t in range(n):
                _piece_copy(modes[t], src_refs[t], land_refs[t], send_sems, recv_sems, t, k, me, pid, peer).start()
        for k in range(1, N_DEV):
            peer, pid = _peer(k)
            for t in range(n):
                _piece_copy(modes[t], src_refs[t], land_refs[t], send_sems, recv_sems, t, k, pid, me, peer).wait_recv()
        for k in range(1, N_DEV):
            peer, pid = _peer(k)
            for t in range(n):
                _piece_copy(modes[t], src_refs[t], land_refs[t], send_sems, recv_sems, t, k, me, pid, peer).wait_send()
        for cp in local:
            cp.wait()

    return pl.kernel(
        body, out_type=[_sds(s, a.dtype) for s, a in zip(land_shapes, srcs)],
        mesh=plsc.ScalarSubcoreMesh(axis_name="sequencer", num_cores=1),
        scratch_types=[pltpu.SemaphoreType.DMA((7 * n,)), pltpu.SemaphoreType.DMA((7 * n,)), pltpu.SemaphoreType.DMA((n,))],
        compiler_params=pltpu.CompilerParams(collective_id=collective_id), name=name)(*srcs, *extra)


def _sequencer_gather(srcs, name, collective_id, after=None):
    n = len(srcs)
    extra = [] if after is None else [after]

    def body(*refs):
        src_refs, land_refs = refs[:n], refs[n + len(extra):2 * n + len(extra)]
        send_sems, recv_sems, local_sems = refs[2 * n + len(extra):]
        x, y, c = lax.axis_index("x"), lax.axis_index("y"), lax.axis_index("c")
        sibling = (x, y, 1 - c)
        chips = [(1 - x, y), (x, 1 - y), (1 - x, 1 - y)]
        index = lambda px, py, pc: 4 * px + 2 * py + pc
        barrier = pltpu.get_barrier_semaphore()
        for peer in [sibling] + [(*chip, c) for chip in chips]:
            pl.semaphore_signal(barrier, inc=1, device_id=peer, device_id_type=MESH)
        pl.semaphore_wait(barrier, 4)

        def copy(t, k, slot, to, src=None):
            return pltpu.make_async_remote_copy(
                src_ref=land_refs[t].at[slot] if src is None else src, dst_ref=land_refs[t].at[slot],
                send_sem=send_sems.at[7 * t + k], recv_sem=recv_sems.at[7 * t + k], device_id=to, device_id_type=MESH)

        me = index(x, y, c)
        local = []
        for t in range(n):
            cp = pltpu.make_async_copy(src_refs[t], land_refs[t].at[me], local_sems.at[t])
            cp.start()
            local.append(cp)
        sends = []
        for t in range(n):
            sends.append(copy(t, 0, me, sibling, src=src_refs[t]))
            sends += [copy(t, 1 + j, me, (*chip, c), src=src_refs[t]) for j, chip in enumerate(chips)]
        for cp in sends:
            cp.start()
        for j, chip in enumerate(chips):
            for t in range(n):
                copy(t, 1 + j, index(*chip, c), sibling, src=src_refs[t]).wait_recv()
                passed = copy(t, 4 + j, index(*chip, c), sibling)
                passed.start()
                sends.append(passed)
        for t in range(n):
            copy(t, 0, index(x, y, 1 - c), sibling, src=src_refs[t]).wait_recv()
        for j, chip in enumerate(chips):
            for t in range(n):
                copy(t, 4 + j, index(*chip, 1 - c), sibling, src=src_refs[t]).wait_recv()
        for cp in sends:
            cp.wait_send()
        for cp in local:
            cp.wait()

    return pl.kernel(
        body, out_type=[_sds((N_DEV,) + a.shape, a.dtype) for a in srcs],
        mesh=plsc.ScalarSubcoreMesh(axis_name="sequencer", num_cores=1),
        scratch_types=[pltpu.SemaphoreType.DMA((7 * n,)), pltpu.SemaphoreType.DMA((7 * n,)), pltpu.SemaphoreType.DMA((n,))],
        compiler_params=pltpu.CompilerParams(collective_id=collective_id), name=name)(*srcs, *extra)


def _adamw(w, g, m, v):
    m = ADAM_B1 * m + (1.0 - ADAM_B1) * g
    v = ADAM_B2 * v + (1.0 - ADAM_B2) * (g * g)
    m_hat = m / (1.0 - ADAM_B1 ** ADAM_STEP)
    v_hat = v / (1.0 - ADAM_B2 ** ADAM_STEP)
    delta = -ADAM_LR * (m_hat / (jnp.sqrt(v_hat) + ADAM_EPS) + ADAM_WD * w)
    return delta, m, v


def _adam_big(w, parts, m, v, name, after=None):
    L, R, C = w.shape
    P = parts[0].shape[0]
    tr = _tile(R, (256, 128, 176, 64, 32, 16))
    nr = R // tr

    def body(w_ref, *refs):
        p_refs, (m_ref, v_ref, g_ref, d_ref, mo_ref, vo_ref) = refs[:L], refs[L:]
        for l in range(L):
            @pl.when(pl.program_id(0) == l)
            def _(p_ref=p_refs[l]):
                g = p_ref[0].astype(F32)
                for s in range(1, P):
                    g = g + p_ref[s].astype(F32)
                g_ref[...] = g
                d_ref[...], mo_ref[...], vo_ref[...] = _adamw(w_ref[...], g, m_ref[...], v_ref[...])

    row = pl.BlockSpec((None, tr, C), lambda l, i: (l, i, 0))
    park = lambda l_of: (lambda l, i: (0, jnp.where(l == l_of, i, 0 if l_of else nr - 1), 0))
    return _call(
        after, body, grid=(L, nr),
        in_specs=[row] + [pl.BlockSpec((P, tr, C), park(l)) for l in range(L)] + [row, row],
        out_specs=[row] * 4, out_shape=[_sds((L, R, C), F32)] * 4,
        compiler_params=_params(("arbitrary", "arbitrary")), name=name)(w, *parts, m, v)


SMALL = (("a_lower_bound", (2, 128), (2, D)), ("ln_gain", (6, 128), (6, D)), ("ln_bias", (6, 128), (6, D)),
         ("a_norm_gain", (1, 128), (1, 128)), ("kv_b", (1, 512), (1, 512)), ("b_b_q", (1, D), (1, D)),
         ("b_sinks", (1, ATT_QH), (1, 128)), ("b_b_out", (1, D), (1, D)), ("ple_b_gate", (2, D), (2, D)))


def _adam_small(parts, w, m, v, losses, after=None):
    k = len(SMALL)

    def body(*refs):
        p_refs, w_refs, m_refs, v_refs = refs[:k], refs[k:2 * k], refs[2 * k:3 * k], refs[3 * k:4 * k]
        loss_ref, outs, total_ref = refs[4 * k], refs[4 * k + 1:-1], refs[-1]
        total = loss_ref[0]
        for s in range(1, N_DEV):
            total = total + loss_ref[s]
        total_ref[...] = total
        me = _my_index()
        for i, (_, wshape, pshape) in enumerate(SMALL):
            cols = wshape[1]
            lanes = slice(None) if cols == pshape[1] else (
                pl.ds(0, cols) if cols < 128 else pl.ds(pl.multiple_of(me * cols, cols), cols))
            g = p_refs[i][0, :, lanes]
            for s in range(1, N_DEV):
                g = g + p_refs[i][s, :, lanes]
            g_ref, d_ref, mo_ref, vo_ref = outs[4 * i:4 * i + 4]
            g_ref[...] = g
            d_ref[...], mo_ref[...], vo_ref[...] = _adamw(w_refs[i][...], g, m_refs[i][...], v_refs[i][...])

    full = lambda shape: pl.BlockSpec(shape, lambda: (0,) * len(shape))
    names = [n for n, _, _ in SMALL]
    res = _call(
        after, body,
        in_specs=[full((N_DEV,) + ps) for _, _, ps in SMALL] + [full(ws) for _, ws, _ in SMALL] * 3
        + [full((N_DEV, 1, 128))],
        out_specs=[full(ws) for _, ws, _ in SMALL for _ in range(4)] + [full((1, 128))],
        out_shape=[_sds(ws, F32) for _, ws, _ in SMALL for _ in range(4)] + [_sds((1, 128), F32)], name="adam_small")(
            *[parts[n] for n in names], *[a[n].reshape(ws) for a in (w, m, v) for n, ws, _ in SMALL], losses)
    return {n: [r.reshape(w[n].shape) for r in res[4 * i:4 * i + 4]] for i, n in enumerate(names)}, res[-1][0, 0]


WEIGHTS = ("a_w_in", "a_lower_bound", "a_norm_gain", "a_w_out", "kv_w", "kv_b", "b_w_q", "b_b_q", "b_sinks", "b_w_out",
           "b_b_out", "ffn_w_gate_up", "ffn_w_down", "ple_w_up", "ple_w_gate", "ple_b_gate", "ln_gain", "ln_bias")


GATHER_GROUPS = (("a_w_in",), ("a_w_out", "gu0"), ("dn0", "pu0", "pg0"), ("kv_w", "b_w_q", "b_w_out"), ("gu1",),
                 ("dn1", "pu1", "pg1"))
KERNEL_LAYOUT = {
    "a_w_in": lambda a: a,
    "a_w_out": lambda a: a.reshape(D, D),
    "kv_w": lambda a: a.reshape(D, 2 * ATT_KVH * ATT_HD),
    "b_w_q": lambda a: a.reshape(D, D),
    "b_w_out": lambda a: a.reshape(D, D),
    "gu": lambda a: a.reshape(2, 4, FFN_B, D),
    "dn": lambda a: a.reshape(4, FFN_B, D),
    "pu": lambda a: a,
    "pg": lambda a: a.reshape(D, D),
}
_row_blocks = lambda a: a.reshape(N_DEV, -1, a.shape[-1])
OWNER_BLOCKS = {
    "a_w_in": lambda g: g,
    "a_w_out": _row_blocks,
    "kv_w": _row_blocks,
    "b_w_q": _row_blocks,
    "b_w_out": _row_blocks,
    "gu": lambda g: g,
    "dn": lambda g: _row_blocks(g.reshape(FFN_H, D)),
    "pu": lambda g: g.reshape(PLE_DIM, N_DEV, 128).transpose(1, 0, 2),
    "pg": _row_blocks,
}
ADAM_PARTS = (("kv_w", ("kv_w",)), ("b_w_q", ("b_w_q",)), ("b_w_out", ("b_w_out",)), ("ffn_w_gate_up", ("gu0", "gu1")),
              ("ffn_w_down", ("dn0", "dn1")), ("ple_w_up", ("pu0", "pu1")), ("ple_w_gate", ("pg0", "pg1")),
              ("a_w_out", ("a_w_out",)), ("a_w_in", ("a_w_in",)))


def kernel(x, p, a_w_in, a_lower_bound, a_norm_gain, a_w_out, kv_w, kv_b, b_w_q, b_b_q, b_sinks, b_w_out, b_b_out, ffn_w_gate_up, ffn_w_down, ple_w_up, ple_w_gate, ple_b_gate, ln_gain, ln_bias, loss_target, m_a_w_in, m_a_lower_bound, m_a_norm_gain, m_a_w_out, m_kv_w, m_kv_b, m_b_w_q, m_b_b_q, m_b_sinks, m_b_w_out, m_b_b_out, m_ffn_w_gate_up, m_ffn_w_down, m_ple_w_up, m_ple_w_gate, m_ple_b_gate, m_ln_gain, m_ln_bias, v_a_w_in, v_a_lower_bound, v_a_norm_gain, v_a_w_out, v_kv_w, v_kv_b, v_b_w_q, v_b_b_q, v_b_sinks, v_b_w_out, v_b_b_out, v_ffn_w_gate_up, v_ffn_w_down, v_ple_w_up, v_ple_w_gate, v_ple_b_gate, v_ln_gain, v_ln_bias):
    given = dict(locals())
    w = {n: given[n] for n in WEIGHTS}
    m = {n: given["m_" + n] for n in WEIGHTS}
    v = {n: given["v_" + n] for n in WEIGHTS}
    shards = {"a_w_in": a_w_in[0], "a_w_out": a_w_out[0], "kv_w": kv_w, "b_w_q": b_w_q[0], "b_w_out": b_w_out[0]}
    for l in range(2):
        shards.update({f"gu{l}": ffn_w_gate_up[l].T, f"dn{l}": ffn_w_down[l], f"pu{l}": ple_w_up[l], f"pg{l}": ple_w_gate[l]})
    sharded_small = [a_lower_bound, ln_gain.reshape(6, 128), ln_bias.reshape(6, 128)]
    gathered = {}
    for gi, g in enumerate(GATHER_GROUPS):
        lands = _sequencer_gather([shards[n].astype(CDT) for n in g] + (sharded_small if gi == 0 else []),
                                  f"gather{gi}", gi)
        for n, a in zip(g, lands):
            gathered[n] = KERNEL_LAYOUT[n.rstrip("01")](a)
        if gi == 0:
            alb, lng, lnb = [a.transpose(1, 0, 2).reshape(a.shape[1], D) for a in lands[len(g):]]

    getw = gathered.__getitem__

    sm = {"a_lower_bound": alb, "ln_gain": lng.reshape(2, 3, D), "ln_bias": lnb.reshape(2, 3, D),
          "a_norm_gain": a_norm_gain, "kv_b": kv_b, "b_b_q": b_b_q[0], "b_sinks": b_sinks, "b_b_out": b_b_out,
          "ple_b_gate": ple_b_gate}

    scatters, small_parts = [], {}

    def emit(grads, small=None):
        names = list(grads)
        blocks = [OWNER_BLOCKS[n.rstrip("01")](grads[n]) for n in names]
        partials = [] if small is None else [small[n].reshape(ps) for n, _, ps in SMALL] + [small["loss"]]
        lands = _sequencer_exchange(blocks + partials, ["scatter"] * len(blocks) + ["gather"] * len(partials),
                                    f"scatter{len(scatters)}", len(GATHER_GROUPS) + len(scatters))
        scatters.append(dict(zip(names, lands)))
        small_parts.update(zip([n for n, _, _ in SMALL] + ["loss"], lands[len(blocks):]))
        return blocks + (list(scatters[-4].values()) if len(scatters) >= 4 else [])

    loss, grad_x, gs = _local_step(x[0], p[:, 0], loss_target[0], getw, sm, emit)

    out, parts, last = {}, {}, [grad_x]
    for landed in scatters:
        parts.update(landed)
        for n, keys in ADAM_PARTS:
            if n in out or not all(key in parts for key in keys):
                continue
            lrc = (1,) * (3 - w[n].ndim) + w[n].shape
            shard = (lambda a: a.reshape(lrc).swapaxes(1, 2)) if n == "ffn_w_gate_up" else (lambda a: a.reshape(lrc))
            res = _adam_big(shard(w[n]), [parts[key] for key in keys], shard(m[n]), shard(v[n]), "adam_" + n, after=last)
            out[n] = [(r.swapaxes(1, 2) if n == "ffn_w_gate_up" else r).reshape(w[n].shape) for r in res]
            last = [res[3]]
    small_out, loss = _adam_small(small_parts, w, m, v, small_parts["loss"], after=last)
    out.update(small_out)
    res = [loss, grad_x[None]]
    for i in range(4):
        res += [out[n][i] for n in WEIGHTS]
    return tuple(res)
```

```python
import jax
import jax.numpy as jnp
from jax import lax
from jax.experimental import pallas as pl
from jax.experimental.pallas import tpu as pltpu
from jax.experimental.pallas import tpu_sc as plsc

F32 = jnp.float32
CDT = jnp.bfloat16

N_DEV = 8
D = 1024
HG_H, HG_DK, HG_CH = 8, 128, 64
HG_HPB = 4
HG_CPB = 8
ATT_HD, ATT_QH, ATT_KVH, ATT_G, WINDOW = 64, 16, 4, 4, 128
ATT_BPB = 1
FFN_H = 2816
FFN_B = FFN_H // 4
PLE_DIM = 256
ALPHA = (2.0 * 2) ** 0.25
LN_EPS = 1e-5
RMS_EPS = 1e-6
ADAM_LR, ADAM_B1, ADAM_B2, ADAM_EPS, ADAM_WD, ADAM_STEP = 0.001, 0.9, 0.999, 1e-08, 0.01, 10
ROW_TILES = (512, 256, 128, 64)
VMEM_LIMIT = 48 * 1024 * 1024
NEG = -1e30

MESH = pl.DeviceIdType.MESH


def _tile(n, cands=ROW_TILES):
    for t in cands:
        if n % t == 0:
            return t
    return n


def _sds(shape, dtype):
    return jax.ShapeDtypeStruct(tuple(shape), dtype)


def _params(sem):
    return pltpu.CompilerParams(dimension_semantics=sem, vmem_limit_bytes=VMEM_LIMIT)


def _dot(a, b):
    return jnp.dot(a.astype(CDT), b.astype(CDT), preferred_element_type=F32)


def _dot_nt(a, b):
    return lax.dot_general(a.astype(CDT), b.astype(CDT), (((1,), (1,)), ((), ())), preferred_element_type=F32)


def _dot_tn(a, b):
    return lax.dot_general(a.astype(CDT), b.astype(CDT), (((0,), (0,)), ((), ())), preferred_element_type=F32)


def _sigmoid(x):
    return jax.nn.sigmoid(x)


def _ln_fwd(z, g, b):
    mu = jnp.mean(z, axis=-1, keepdims=True)
    zc = z - mu
    var = jnp.mean(zc * zc, axis=-1, keepdims=True)
    return zc * lax.rsqrt(var + LN_EPS) * g + b


def _ln_bwd(z, g, dy):
    mu = jnp.mean(z, axis=-1, keepdims=True)
    zc = z - mu
    var = jnp.mean(zc * zc, axis=-1, keepdims=True)
    rstd = lax.rsqrt(var + LN_EPS)
    xhat = zc * rstd
    dxh = dy * g
    dz = rstd * (dxh - jnp.mean(dxh, axis=-1, keepdims=True) - xhat * jnp.mean(dxh * xhat, axis=-1, keepdims=True))
    return dz, xhat


def _colsum(x):
    return jnp.sum(x, axis=0, keepdims=True)


def _acc(ref, val, first):
    @pl.when(first)
    def _():
        ref[...] = val

    @pl.when(jnp.logical_not(first))
    def _():
        ref[...] += val


def _zero_at(ref, first):
    @pl.when(first)
    def _():
        ref[...] = jnp.zeros_like(ref)


def _call(after, body, **kw):
    after = [] if after is None else list(after)
    specs = list(kw["in_specs"])
    kw["in_specs"] = [pl.BlockSpec(memory_space=pl.ANY)] * len(after) + specs

    def ordered_body(*refs):
        body(*refs[len(after):])

    call = pl.pallas_call(ordered_body, **kw)
    return lambda *args: call(*after, *args)


def _mm_nn(a, b3, out_shape, oblock, omap, out_dtype, bias3=None, name="mm_nn"):
    M, K = a.shape
    G, _, Nb = b3.shape
    tm = _tile(M)

    def body(a_ref, b_ref, *rest):
        o_ref = rest[-1]
        acc = _dot(a_ref[...], b_ref[...])
        if bias3 is not None:
            acc = acc + rest[0][...]
        o_ref[...] = acc.astype(o_ref.dtype)

    in_specs = [pl.BlockSpec((tm, K), lambda g, i: (i, 0)), pl.BlockSpec((None, K, Nb), lambda g, i: (g, 0, 0))]
    args = [a, b3]
    if bias3 is not None:
        in_specs.append(pl.BlockSpec((None, 1, Nb), lambda g, i: (g, 0, 0)))
        args.append(bias3)
    return _call(
        None, body, grid=(G, M // tm), in_specs=in_specs, out_specs=pl.BlockSpec(oblock, omap),
        out_shape=_sds(out_shape, out_dtype), compiler_params=_params(("arbitrary", "arbitrary")), name=name)(*args)


def _mm_tn(a3, b3, G, amap, bmap, ablock, bblock, out_shape, oblock, omap, name="mm_tn"):
    S = a3.shape[1]

    def body(a_ref, b_ref, o_ref):
        o_ref[...] = _dot_tn(a_ref[...], b_ref[...]).astype(o_ref.dtype)

    return _call(
        None, body, grid=(G, 1),
        in_specs=[pl.BlockSpec(ablock(S), amap), pl.BlockSpec(bblock(S), bmap)],
        out_specs=pl.BlockSpec(oblock, omap), out_shape=_sds(out_shape, CDT),
        compiler_params=_params(("arbitrary", "arbitrary")), name=name)(a3, b3)


def _wgrad(a3, b3, name):
    Ga, S, M = a3.shape
    Gb, _, N = b3.shape
    G = max(Ga, Gb)
    return _mm_tn(
        a3, b3, G,
        (lambda g, k: (g, k, 0)) if Ga > 1 else (lambda g, k: (0, k, 0)),
        (lambda g, k: (g, k, 0)) if Gb > 1 else (lambda g, k: (0, k, 0)),
        lambda tk: (None, tk, M), lambda tk: (None, tk, N),
        (G, M, N), (None, M, N), lambda g, k: (g, 0, 0), name=name)


def _mixout_ln(u3, w3, bias, xin, gain, beta, name):
    G, S, Kb = u3.shape
    tm = _tile(S)

    def body(u_ref, w_ref, b_ref, x_ref, g_ref, be_ref, z_ref, xo_ref, xob_ref):
        h = b_ref[...] + _dot(u_ref[0], w_ref[0])
        for g in range(1, G):
            h = h + _dot(u_ref[g], w_ref[g])
        z = ALPHA * x_ref[...] + h
        z_ref[...] = z
        y = _ln_fwd(z, g_ref[...], be_ref[...])
        xo_ref[...] = y
        xob_ref[...] = y.astype(CDT)

    row = pl.BlockSpec((tm, D), lambda i: (i, 0))
    vec = pl.BlockSpec((1, D), lambda i: (0, 0))
    return _call(
        None, body, grid=(S // tm,),
        in_specs=[pl.BlockSpec((G, tm, Kb), lambda i: (0, i, 0)), pl.BlockSpec((G, Kb, D), lambda i: (0, 0, 0)),
                  vec, row, vec, vec],
        out_specs=[row, row, row], out_shape=[_sds((S, D), F32), _sds((S, D), F32), _sds((S, D), CDT)],
        compiler_params=_params(("arbitrary",)), name=name)(u3, w3, bias, xin, gain, beta)


def _ffn_fwd(xin, xin_b, wgu, wdn, gain, beta, name):
    S = xin.shape[0]
    tm = _tile(S)

    def hidden(xb_ref, wgu_ref, gu_ref, hid_ref):
        xb = xb_ref[...]
        gate = _dot_nt(xb, wgu_ref[0])
        up = _dot_nt(xb, wgu_ref[1])
        gu_ref[0] = gate.astype(CDT)
        gu_ref[1] = up.astype(CDT)
        hid_ref[...] = (gate * _sigmoid(gate) * up).astype(CDT)

    gu, hid = _call(
        None, hidden, grid=(4, S // tm),
        in_specs=[pl.BlockSpec((tm, D), lambda j, i: (i, 0)), pl.BlockSpec((2, None, FFN_B, D), lambda j, i: (0, j, 0, 0))],
        out_specs=[pl.BlockSpec((2, None, tm, FFN_B), lambda j, i: (0, j, i, 0)),
                   pl.BlockSpec((None, tm, FFN_B), lambda j, i: (j, i, 0))],
        out_shape=[_sds((2, 4, S, FFN_B), CDT), _sds((4, S, FFN_B), CDT)],
        compiler_params=_params(("arbitrary", "arbitrary")), name=name + "_hidden")(xin_b, wgu)

    def down(x_ref, hid_ref, wdn_ref, g_ref, be_ref, z_ref, xo_ref, xob_ref):
        z = ALPHA * x_ref[...]
        for j in range(4):
            z = z + _dot(hid_ref[j], wdn_ref[j])
        z_ref[...] = z
        y = _ln_fwd(z, g_ref[...], be_ref[...])
        xo_ref[...] = y
        xob_ref[...] = y.astype(CDT)

    row = pl.BlockSpec((tm, D), lambda i: (i, 0))
    vec = pl.BlockSpec((1, D), lambda i: (0, 0))
    z, xo, xob = _call(
        None, down, grid=(S // tm,),
        in_specs=[row, pl.BlockSpec((4, tm, FFN_B), lambda i: (0, i, 0)), pl.BlockSpec((4, FFN_B, D), lambda i: (0, 0, 0)),
                  vec, vec],
        out_specs=[row, row, row], out_shape=[_sds((S, D), F32), _sds((S, D), F32), _sds((S, D), CDT)],
        compiler_params=_params(("arbitrary",)), name=name + "_down")(xin, hid, wdn, gain, beta)
    return gu, hid, z, xo, xob


def _ple_fwd(xin, xin_b, p_b, wpg, bgate, wpu, gain, beta, name):
    S = xin.shape[0]
    tm = _tile(S)

    def body(x_ref, xb_ref, p_ref, wpg_ref, bg_ref, wpu_ref, g_ref, be_ref, sg_ref, up_ref, z_ref, xo_ref, xob_ref):
        sg = _sigmoid(_dot(xb_ref[...], wpg_ref[...]) + bg_ref[...])
        pb = p_ref[...]
        up = jnp.concatenate([_dot(pb, wpu_ref[j]) for j in range(N_DEV)], axis=-1)
        sg_ref[...] = sg.astype(CDT)
        up_ref[...] = (up * sg * (1.0 - sg)).astype(CDT)
        z = ALPHA * x_ref[...] + sg * up
        z_ref[...] = z
        y = _ln_fwd(z, g_ref[...], be_ref[...])
        xo_ref[...] = y
        xob_ref[...] = y.astype(CDT)

    row = pl.BlockSpec((tm, D), lambda i: (i, 0))
    vec = pl.BlockSpec((1, D), lambda i: (0, 0))
    return _call(
        None, body, grid=(S // tm,),
        in_specs=[row, row, pl.BlockSpec((tm, PLE_DIM), lambda i: (i, 0)), pl.BlockSpec((D, D), lambda i: (0, 0)), vec,
                  pl.BlockSpec((N_DEV, PLE_DIM, D // N_DEV), lambda i: (0, 0, 0)), vec, vec],
        out_specs=[row] * 5,
        out_shape=[_sds((S, D), CDT)] * 2 + [_sds((S, D), F32)] * 2 + [_sds((S, D), CDT)],
        compiler_params=_params(("arbitrary",)), name=name)(xin, xin_b, p_b, wpg, bgate, wpu, gain, beta)


def _loss_fwd_bwd(y, target):
    S = y.shape[0]
    tm = _tile(S)

    def body(y_ref, t_ref, l_ref, dy_ref):
        e = y_ref[...] - t_ref[...]
        dy_ref[...] = e * (1.0 / D)
        part = 0.5 * jnp.sum(jnp.sum(e * e, axis=-1, keepdims=True) * (1.0 / D), axis=0, keepdims=True)
        _acc(l_ref, jnp.broadcast_to(part, l_ref.shape), pl.program_id(0) == 0)

    row = pl.BlockSpec((tm, D), lambda i: (i, 0))
    return _call(
        None, body, grid=(S // tm,), in_specs=[row, row],
        out_specs=[pl.BlockSpec((1, 128), lambda i: (0, 0)), row],
        out_shape=[_sds((1, 128), F32), _sds((S, D), F32)],
        compiler_params=_params(("arbitrary",)), name="loss")(y, target)


def _ple_bwd(dy, z, sg, up, gain, wpg, name, after=None):
    S = dy.shape[0]
    tm = _tile(S)

    def body(dy_ref, z_ref, sg_ref, up_ref, g_ref, wpg_ref, dx_ref, dgl_ref, dup_ref, dgain_ref, dbeta_ref, dbg_ref):
        first = pl.program_id(0) == 0
        dy_ = dy_ref[...]
        dz, xhat = _ln_bwd(z_ref[...], g_ref[...], dy_)
        dgl = dz * up_ref[...].astype(F32)
        dgl_ref[...] = dgl.astype(CDT)
        dup_ref[...] = (dz * sg_ref[...].astype(F32)).astype(CDT)
        dx_ref[...] = ALPHA * dz + _dot_nt(dgl, wpg_ref[...])
        _acc(dgain_ref, _colsum(dy_ * xhat), first)
        _acc(dbeta_ref, _colsum(dy_), first)
        _acc(dbg_ref, _colsum(dgl), first)

    row = pl.BlockSpec((tm, D), lambda i: (i, 0))
    vec = pl.BlockSpec((1, D), lambda i: (0, 0))
    return _call(
        after, body, grid=(S // tm,), in_specs=[row, row, row, row, vec, pl.BlockSpec((D, D), lambda i: (0, 0))],
        out_specs=[row, row, row, vec, vec, vec],
        out_shape=[_sds((S, D), F32), _sds((S, D), CDT), _sds((S, D), CDT)] + [_sds((1, D), F32)] * 3,
        compiler_params=_params(("arbitrary",)), name=name)(dy, z, sg, up, gain, wpg)


def _ffn_bwd_hidden(dy, z, gu, wdn, gain, name, after=None):
    S = dy.shape[0]
    tm = _tile(S)

    def hidden(dy_ref, z_ref, gu_ref, wdn_ref, g_ref, dz_ref, dzb_ref, dgu_ref, dgain_ref, dbeta_ref):
        i, j = pl.program_id(0), pl.program_id(1)

        @pl.when(j == 0)
        def _():
            dy_ = dy_ref[...]
            dz, xhat = _ln_bwd(z_ref[...], g_ref[...], dy_)
            dz_ref[...] = dz
            dzb_ref[...] = dz.astype(CDT)
            _acc(dgain_ref, _colsum(dy_ * xhat), i == 0)
            _acc(dbeta_ref, _colsum(dy_), i == 0)

        dhid = _dot_nt(dzb_ref[...], wdn_ref[...])
        gate, up = gu_ref[0].astype(F32), gu_ref[1].astype(F32)
        sg = _sigmoid(gate)
        dgu_ref[0] = (dhid * up * (sg * (1.0 + gate * (1.0 - sg)))).astype(CDT)
        dgu_ref[1] = (dhid * (gate * sg)).astype(CDT)

    row = pl.BlockSpec((tm, D), lambda i, j: (i, 0))
    vec = pl.BlockSpec((1, D), lambda i, j: (0, 0))
    return _call(
        after, hidden, grid=(S // tm, 4),
        in_specs=[row, row, pl.BlockSpec((2, None, tm, FFN_B), lambda i, j: (0, j, i, 0)),
                  pl.BlockSpec((None, FFN_B, D), lambda i, j: (j, 0, 0)), vec],
        out_specs=[row, row, pl.BlockSpec((2, None, tm, FFN_B), lambda i, j: (0, j, i, 0)), vec, vec],
        out_shape=[_sds((S, D), F32), _sds((S, D), CDT), _sds((2, 4, S, FFN_B), CDT), _sds((1, D), F32),
                   _sds((1, D), F32)],
        compiler_params=_params(("arbitrary", "arbitrary")), name=name + "_hidden")(dy, z, gu, wdn, gain)


def _ffn_bwd_input(dz, dgu, wgu, name, after=None):
    S = dz.shape[0]
    tm = _tile(S)

    def to_input(dz_ref, dgu_ref, wgu_ref, dx_ref):
        acc = ALPHA * dz_ref[...]
        for g in range(2):
            for j in range(4):
                acc = acc + _dot(dgu_ref[g, j], wgu_ref[g, j])
        dx_ref[...] = acc

    rows = pl.BlockSpec((tm, D), lambda i: (i, 0))
    return _call(
        after, to_input, grid=(S // tm,),
        in_specs=[rows, pl.BlockSpec((2, 4, tm, FFN_B), lambda i: (0, 0, i, 0)),
                  pl.BlockSpec((2, 4, FFN_B, D), lambda i: (0, 0, 0, 0))],
        out_specs=rows, out_shape=_sds((S, D), F32),
        compiler_params=_params(("arbitrary",)), name=name + "_input")(dz, dgu, wgu)


def _mixout_bwd(dy, z, gain, w3, du_dtype, name, after=None):
    S = dy.shape[0]
    G, Kb, _ = w3.shape
    tm = _tile(S)

    def body(dy_ref, z_ref, g_ref, w_ref, dz_ref, dzb_ref, du_ref, dgain_ref, dbeta_ref, dbias_ref):
        first = pl.program_id(0) == 0
        dy_ = dy_ref[...]
        dz, xhat = _ln_bwd(z_ref[...], g_ref[...], dy_)
        dz_ref[...] = dz
        dzb = dz.astype(CDT)
        dzb_ref[...] = dzb
        for g in range(G):
            du_ref[g] = _dot_nt(dzb, w_ref[g]).astype(du_ref.dtype)
        _acc(dgain_ref, _colsum(dy_ * xhat), first)
        _acc(dbeta_ref, _colsum(dy_), first)
        _acc(dbias_ref, _colsum(dz), first)

    row = pl.BlockSpec((tm, D), lambda i: (i, 0))
    vec = pl.BlockSpec((1, D), lambda i: (0, 0))
    return _call(
        after, body, grid=(S // tm,), in_specs=[row, row, vec, pl.BlockSpec((G, Kb, D), lambda i: (0, 0, 0))],
        out_specs=[row, row, pl.BlockSpec((G, tm, Kb), lambda i: (0, i, 0)), vec, vec, vec],
        out_shape=[_sds((S, D), F32), _sds((S, D), CDT), _sds((G, S, Kb), du_dtype)] + [_sds((1, D), F32)] * 3,
        compiler_params=_params(("arbitrary",)), name=name)(dy, z, gain, w3)


def _half_select(low):
    r = lax.broadcasted_iota(jnp.int32, (2 * ATT_HD, ATT_HD), 0)
    c = lax.broadcasted_iota(jnp.int32, (2 * ATT_HD, ATT_HD), 1)
    return (r == c + (0 if low else ATT_HD)).astype(CDT)


def _half_place(low):
    r = lax.broadcasted_iota(jnp.int32, (ATT_HD, 2 * ATT_HD), 0)
    c = lax.broadcasted_iota(jnp.int32, (ATT_HD, 2 * ATT_HD), 1)
    return (c == r + (0 if low else ATT_HD)).astype(CDT)


def _pair_lanes(even, odd):
    return (jnp.dot(even, _half_place(True), preferred_element_type=F32)
            + jnp.dot(odd, _half_place(False), preferred_element_type=F32)).astype(CDT)


def _proj_heads(a, w, bias, heads, name):
    S, K = a.shape
    N = heads * ATT_HD
    tm = _tile(S)

    def body(a_ref, w_ref, b_ref, o_ref):
        acc = (_dot(a_ref[...], w_ref[...]) + b_ref[...]).astype(CDT)
        sel = (_half_select(True), _half_select(False))
        for h in range(heads):
            pair = acc[:, (h // 2) * 2 * ATT_HD:(h // 2 + 1) * 2 * ATT_HD]
            o_ref[h] = jnp.dot(pair, sel[h % 2], preferred_element_type=F32).astype(CDT)

    return _call(
        None, body, grid=(S // tm,),
        in_specs=[pl.BlockSpec((tm, K), lambda i: (i, 0)), pl.BlockSpec((K, N), lambda i: (0, 0)),
                  pl.BlockSpec((1, N), lambda i: (0, 0))],
        out_specs=pl.BlockSpec((heads, tm, ATT_HD), lambda i: (0, i, 0)), out_shape=_sds((heads, S, ATT_HD), CDT),
        compiler_params=_params(("arbitrary",)), name=name)(a, w, bias)


def _qkv_bwd(dz, dq, dkv4, wq, wkv, name, after=None):
    S = dz.shape[0]
    tm = _tile(S)
    HK = dkv4.shape[0]
    NK = HK * ATT_HD

    def body(dz_ref, dq_ref, dkv_ref, wq_ref, wkv_ref, dx_ref, dkvn_ref, dkvb_ref):
        first = pl.program_id(0) == 0
        dkvn = jnp.concatenate([_pair_lanes(dkv_ref[2 * i].astype(CDT), dkv_ref[2 * i + 1].astype(CDT))
                                for i in range(HK // 2)], axis=-1)
        dkvn_ref[...] = dkvn
        dx_ref[...] = ALPHA * dz_ref[...] + _dot_nt(dq_ref[...], wq_ref[...]) + _dot_nt(dkvn, wkv_ref[...])
        for h in range(HK):
            _acc(dkvb_ref.at[h], _colsum(dkv_ref[h]), first)

    row = pl.BlockSpec((tm, D), lambda i: (i, 0))
    return _call(
        after, body, grid=(S // tm,),
        in_specs=[row, row, pl.BlockSpec((HK, tm, ATT_HD), lambda i: (0, i, 0)),
                  pl.BlockSpec((D, D), lambda i: (0, 0)), pl.BlockSpec((D, NK), lambda i: (0, 0))],
        out_specs=[row, pl.BlockSpec((tm, NK), lambda i: (i, 0)), pl.BlockSpec((HK, 1, ATT_HD), lambda i: (0, 0, 0))],
        out_shape=[_sds((S, D), F32), _sds((S, NK), CDT), _sds((HK, 1, ATT_HD), F32)],
        compiler_params=_params(("arbitrary",)), name=name)(dz, dq, dkv4, wq, wkv)


def _inproj_bwd(dz, dproj, wain, name, after=None):
    S = dz.shape[0]
    tm = _tile(S)
    nb = wain.shape[-1]

    def body(dz_ref, dp_ref, w_ref, dx_ref):
        acc = ALPHA * dz_ref[...]
        for j in range(N_DEV):
            acc = acc + _dot_nt(dp_ref[j // 2, :, pl.ds((j % 2) * nb, nb)], w_ref[j])
        dx_ref[...] = acc

    row = pl.BlockSpec((tm, D), lambda i: (i, 0))
    return _call(
        after, body, grid=(S // tm,),
        in_specs=[row, pl.BlockSpec((4, tm, D), lambda i: (0, i, 0)), pl.BlockSpec((N_DEV, D, nb), lambda i: (0, 0, 0))],
        out_specs=row, out_shape=_sds((S, D), F32),
        compiler_params=_params(("arbitrary",)), name=name)(dz, dproj, wain)


def _running_sum(x, reverse=False):
    rows = x.shape[0]
    row = lax.broadcasted_iota(jnp.int32, x.shape, 0)
    step = 1
    while step < rows:
        if reverse:
            x = x + jnp.where(row < rows - step, pltpu.roll(x, rows - step, 0), 0.0)
        else:
            x = x + jnp.where(row >= step, pltpu.roll(x, step, 0), 0.0)
        step *= 2
    return x


def _hg_gates(q, f, alb_ref):
    a0, a1 = alb_ref[0:1, :], alb_ref[1:2, :]
    mx = jnp.maximum(a0, a1)
    e0, e1 = jnp.exp(a0 - mx), jnp.exp(a1 - mx)
    lb = e0 / (e0 + e1)
    sig = _sigmoid(f)
    forget = lb + (1.0 - lb) * sig
    k = (1.0 - lb) * _sigmoid(-f)
    qs = q * _sigmoid(q) * (HG_DK ** -0.5)
    return qs, k, jnp.log(forget), sig, lb, forget


def _hg_intra(qs, k, b, b_scr):
    b_scr[...] = b
    bm = b_scr[pl.ds(HG_CH // 2 - 1, 1), :]
    bl = b_scr[pl.ds(HG_CH - 1, 1), :]
    eb = jnp.exp(b)
    qb = qs * eb
    e_q = jnp.exp(b - bm)
    e_k = jnp.exp(bm - b)
    e_d = jnp.exp(bl - b)
    return qb, qs * e_q, k * e_k, k * e_d, jnp.exp(bl), eb, e_q, e_k, e_d


def _hgrn_fwd(proj, alb, ngain):
    S = proj.shape[1]
    nc = S // HG_CH
    nb = nc // HG_CPB
    rb, wb = HG_CPB * HG_CH, HG_HPB * HG_DK

    def body(pj_ref, alb_ref, ng_ref, o_ref, y_ref, st_ref, st_scr, b_scr):
        n = pl.program_id(1)

        @pl.when(n == 0)
        def _():
            st_scr[...] = jnp.zeros_like(st_scr)

        r = lax.broadcasted_iota(jnp.int32, (HG_CH, HG_CH), 0)
        c = lax.broadcasted_iota(jnp.int32, (HG_CH, HG_CH), 1)
        causal = r >= c
        for ci, j in [(ci, j) for ci in range(HG_CPB) for j in range(HG_HPB)]:
            rows, lanes = pl.ds(ci * HG_CH, HG_CH), pl.ds(j * HG_DK, HG_DK)
            q, f, v, g = pj_ref[0, rows, lanes], pj_ref[1, rows, lanes], pj_ref[2, rows, lanes], pj_ref[3, rows, lanes]
            qs, k, logf, _, _, _ = _hg_gates(q, f, alb_ref.at[:, lanes])
            b = _running_sum(logf)
            qb, qt, kt, kd, ebl, _, _, _, _ = _hg_intra(qs, k, b, b_scr.at[j, ci])
            st = st_scr[j]
            st_ref[j, ci] = st
            a = jnp.where(causal, _dot_nt(qt, kt), 0.0)
            o = _dot(a, v) + _dot_nt(qb, st)
            st_scr[j] = st * ebl + _dot_tn(v, kd)
            o_ref[rows, lanes] = o
            rinv = lax.rsqrt(jnp.mean(o * o, axis=-1, keepdims=True) + RMS_EPS)
            y_ref[rows, lanes] = (o * rinv * ng_ref[...] * (g * _sigmoid(g))).astype(CDT)

    blk = pl.BlockSpec((rb, wb), lambda h, n: (n, h))
    return _call(
        None, body, grid=(HG_H // HG_HPB, nb),
        in_specs=[pl.BlockSpec((4, rb, wb), lambda h, n: (0, n, h)), pl.BlockSpec((2, wb), lambda h, n: (0, h)),
                  pl.BlockSpec((1, HG_DK), lambda h, n: (0, 0))],
        out_specs=[blk, blk, pl.BlockSpec((HG_HPB, HG_CPB, HG_DK, HG_DK), lambda h, n: (h, n, 0, 0))],
        out_shape=[_sds((S, D), F32), _sds((S, D), CDT), _sds((HG_H, nc, HG_DK, HG_DK), F32)],
        scratch_shapes=[pltpu.VMEM((HG_HPB, HG_DK, HG_DK), F32), pltpu.VMEM((HG_HPB, HG_CPB, HG_CH, HG_DK), F32)],
        compiler_params=_params(("arbitrary", "arbitrary")), name="hgrn_fwd")(proj, alb, ngain)


def _hgrn_bwd(proj, alb, ngain, o, states, dy, after=None):
    S = proj.shape[1]
    nc = S // HG_CH
    nb = nc // HG_CPB
    rb, wb = HG_CPB * HG_CH, HG_HPB * HG_DK

    def body(pj_ref, alb_ref, ng_ref, o_ref, st_ref, dy_ref, dpj_ref, dalb_ref, dng_ref, dst_scr, b_scr):
        h, n = pl.program_id(0), pl.program_id(1)

        @pl.when(n == 0)
        def _():
            dst_scr[...] = jnp.zeros_like(dst_scr)
            dalb_ref[...] = jnp.zeros_like(dalb_ref)

        _zero_at(dng_ref, jnp.logical_and(h == 0, n == 0))
        ng = ng_ref[...]
        r = lax.broadcasted_iota(jnp.int32, (HG_CH, HG_CH), 0)
        c = lax.broadcasted_iota(jnp.int32, (HG_CH, HG_CH), 1)
        causal = r >= c
        dng = None
        for ci, j in [(ci, j) for ci in reversed(range(HG_CPB)) for j in range(HG_HPB)]:
            rows, lanes = pl.ds(ci * HG_CH, HG_CH), pl.ds(j * HG_DK, HG_DK)
            q, f, v, g = pj_ref[0, rows, lanes], pj_ref[1, rows, lanes], pj_ref[2, rows, lanes], pj_ref[3, rows, lanes]
            o_ = o_ref[rows, lanes]
            dy_ = dy_ref[rows, lanes]
            sg = _sigmoid(g)
            rinv = lax.rsqrt(jnp.mean(o_ * o_, axis=-1, keepdims=True) + RMS_EPS)
            nrm = o_ * rinv
            dr = dy_ * (g * sg)
            dg = dy_ * nrm * ng * (sg * (1.0 + g * (1.0 - sg)))
            dn = dr * ng
            do = rinv * (dn - nrm * jnp.mean(dn * nrm, axis=-1, keepdims=True))
            dng = _colsum(dr * nrm) if dng is None else dng + _colsum(dr * nrm)
            qs, k, logf, sig, lb, forget = _hg_gates(q, f, alb_ref.at[:, lanes])
            b = _running_sum(logf)
            qb, qt, kt, kd, ebl, eb, e_q, e_k, e_d = _hg_intra(qs, k, b, b_scr.at[j, ci])
            st = st_ref[j, ci]
            dstn = dst_scr[j]
            qt, kt, qb, kd = (t.astype(CDT).astype(F32) for t in (qt, kt, qb, kd))
            a = jnp.where(causal, _dot_nt(qt, kt), 0.0)
            da = jnp.where(causal, _dot_nt(do, v), 0.0)
            dv = _dot_tn(a, do) + _dot_nt(kd, dstn)
            dqb = _dot(do, st)
            dkd = _dot(v, dstn)
            dqt = _dot(da, kt)
            dkt = _dot_tn(da, qt)
            dbl = _colsum(dkd * kd) + ebl * _colsum(dstn * st)
            dst_scr[j] = dstn * ebl + _dot_tn(do, qb)
            dqs = dqt * e_q + dqb * eb
            dk = dkt * e_k + dkd * e_d
            db = dqt * qt + dqb * qb - dkt * kt - dkd * kd
            dlogf = _running_sum(db, reverse=True) + dbl
            dforget = dlogf / forget
            dsig = (1.0 - lb) * (dforget - dk)
            df = dsig * sig * (1.0 - sig)
            dlb = _colsum((dforget - dk) * (1.0 - sig))
            sq = _sigmoid(q)
            dq = dqs * (HG_DK ** -0.5) * (sq * (1.0 + q * (1.0 - sq)))
            dpj_ref[0, rows, lanes] = dq.astype(CDT)
            dpj_ref[1, rows, lanes] = df.astype(CDT)
            dpj_ref[2, rows, lanes] = dv.astype(CDT)
            dpj_ref[3, rows, lanes] = dg.astype(CDT)
            da0 = dlb * lb * (1.0 - lb)
            dalb_ref[pl.ds(0, 1), lanes] += da0
            dalb_ref[pl.ds(1, 1), lanes] -= da0
        dng_ref[...] += dng

    blk = pl.BlockSpec((rb, wb), lambda h, n: (nb - 1 - n, h))
    pj = pl.BlockSpec((4, rb, wb), lambda h, n: (0, nb - 1 - n, h))
    alb_blk = pl.BlockSpec((2, wb), lambda h, n: (0, h))
    ng_blk = pl.BlockSpec((1, HG_DK), lambda h, n: (0, 0))
    return _call(
        after, body, grid=(HG_H // HG_HPB, nb),
        in_specs=[pj, alb_blk, ng_blk, blk,
                  pl.BlockSpec((HG_HPB, HG_CPB, HG_DK, HG_DK), lambda h, n: (h, nb - 1 - n, 0, 0)), blk],
        out_specs=[pj, alb_blk, ng_blk],
        out_shape=[_sds((4, S, D), CDT), _sds((2, D), F32), _sds((1, HG_DK), F32)],
        scratch_shapes=[pltpu.VMEM((HG_HPB, HG_DK, HG_DK), F32), pltpu.VMEM((HG_HPB, HG_CPB, HG_CH, HG_DK), F32)],
        compiler_params=_params(("arbitrary", "arbitrary")), name="hgrn_bwd")(proj, alb, ngain, o, states, dy)


def _slope(h):
    return 2.0 ** (-8.0 * (h + 1) / ATT_QH)


def _attn_mask(n):
    qi = lax.broadcasted_iota(jnp.int32, (WINDOW, 2 * WINDOW), 0)
    si = lax.broadcasted_iota(jnp.int32, (WINDOW, 2 * WINDOW), 1)
    dist = qi - si + WINDOW
    valid = (dist >= 0) & (dist < WINDOW) & (n * WINDOW - WINDOW + si >= 0)
    return valid, dist.astype(F32)


def _attn_probs(qh, kh, sink, slope, valid, distf):
    s = _dot_nt(qh, kh) * (ATT_HD ** -0.5) - slope * distf
    s = jnp.where(valid, s, NEG)
    m = jnp.maximum(jnp.max(s, axis=-1, keepdims=True), sink)
    e = jnp.exp(s - m)
    es = jnp.exp(sink - m)
    inv = 1.0 / (jnp.sum(e, axis=-1, keepdims=True) + es)
    return e * inv, es * inv


def _attn_specs(S):
    steps = S // (ATT_BPB * WINDOW)
    cur = lambda H: pl.BlockSpec((H, ATT_BPB * WINDOW, ATT_HD), lambda n: (0, n, 0))
    prev = lambda H: pl.BlockSpec((H, WINDOW, ATT_HD), lambda n: (0, jnp.maximum(ATT_BPB * n - 1, 0), 0))
    return steps, cur, prev


def _attn_kv(kvc_ref, kvp_ref, head, bi):
    before = kvp_ref[head] if bi == 0 else kvc_ref[head, pl.ds((bi - 1) * WINDOW, WINDOW), :]
    return jnp.concatenate([before, kvc_ref[head, pl.ds(bi * WINDOW, WINDOW), :]], axis=0)


def _attn_fwd(q4, kv4, sinks):
    S = q4.shape[1]
    nb, cur, prev = _attn_specs(S)

    def body(sink_ref, q_ref, kvc_ref, kvp_ref, o_ref, p_ref, ps_ref):
        lane = lax.broadcasted_iota(jnp.int32, (1, 128), 1)
        for bi in range(ATT_BPB):
            valid, distf = _attn_mask(pl.program_id(0) * ATT_BPB + bi)
            rows = pl.ds(bi * WINDOW, WINDOW)
            sink_probs = jnp.zeros((WINDOW, 128), F32)
            for kvh in range(ATT_KVH):
                kh = _attn_kv(kvc_ref, kvp_ref, kvh, bi)
                vh = _attn_kv(kvc_ref, kvp_ref, ATT_KVH + kvh, bi)
                v_low = jnp.dot(vh, _half_place(True), preferred_element_type=F32).astype(CDT)
                v_high = jnp.dot(vh, _half_place(False), preferred_element_type=F32).astype(CDT)
                for h in range(kvh * ATT_G, (kvh + 1) * ATT_G, 2):
                    pair = []
                    for hh in (h, h + 1):
                        p, ps = _attn_probs(q_ref[hh, rows, :], kh, sink_ref[0, hh], _slope(hh), valid, distf)
                        p = p.astype(CDT)
                        p_ref[hh, rows, :] = p
                        sink_probs = sink_probs + jnp.where(lane == hh, ps, 0.0)
                        pair.append(p)
                    o_ref[rows, pl.ds(h * ATT_HD, 2 * ATT_HD)] = (_dot(pair[0], v_low) + _dot(pair[1], v_high)).astype(CDT)
            ps_ref[rows, :] = sink_probs

    rows_spec = pl.BlockSpec((ATT_BPB * WINDOW, D), lambda n: (n, 0))
    return _call(
        None, body, grid=(nb,),
        in_specs=[pl.BlockSpec(memory_space=pltpu.SMEM), cur(ATT_QH), cur(2 * ATT_KVH), prev(2 * ATT_KVH)],
        out_specs=[rows_spec, pl.BlockSpec((ATT_QH, ATT_BPB * WINDOW, 2 * WINDOW), lambda n: (0, n, 0)),
                   pl.BlockSpec((ATT_BPB * WINDOW, 128), lambda n: (n, 0))],
        out_shape=[_sds((S, D), CDT), _sds((ATT_QH, S, 2 * WINDOW), CDT), _sds((S, 128), F32)],
        compiler_params=_params(("arbitrary",)), name="attn_fwd")(sinks, q4, kv4, kv4)


def _attn_bwd(q4, kv4, probs, sink_probs, do):
    S = q4.shape[1]
    nb, cur, prev = _attn_specs(S)

    def body(q_ref, kvc_ref, kvp_ref, p_ref, ps_ref, do_ref, dq_ref, dkv_ref, dbq_ref, dsink_ref):
        n = pl.program_id(0)
        first = n == 0

        @pl.when(first)
        def _():
            dkv_ref[...] = jnp.zeros_like(dkv_ref)
            dsink_ref[...] = jnp.zeros_like(dsink_ref)
            dbq_ref[...] = jnp.zeros_like(dbq_ref)

        lane = lax.broadcasted_iota(jnp.int32, (1, 128), 1)
        dsinks = jnp.zeros((1, 128), F32)
        sel = (_half_select(True), _half_select(False))
        row_dots = [jnp.zeros((WINDOW, 128), F32) for _ in range(ATT_BPB)]
        for bi, kvh in [(bi, kvh) for bi in range(ATT_BPB) for kvh in range(ATT_KVH)]:
            block = n * ATT_BPB + bi
            rows = pl.ds(bi * WINDOW, WINDOW)
            rows_cur = pl.ds(pl.multiple_of(block * WINDOW, WINDOW), WINDOW)
            rows_prev = pl.ds(pl.multiple_of(jnp.maximum(block - 1, 0) * WINDOW, WINDOW), WINDOW)
            kh = _attn_kv(kvc_ref, kvp_ref, kvh, bi)
            vh = _attn_kv(kvc_ref, kvp_ref, ATT_KVH + kvh, bi)
            dk = dv = None
            dqs = []
            for h in range(kvh * ATT_G, (kvh + 1) * ATT_G):
                qh = q_ref[h, rows, :]
                doh = jnp.dot(do_ref[rows, pl.ds((h // 2) * 2 * ATT_HD, 2 * ATT_HD)], sel[h % 2],
                              preferred_element_type=F32).astype(CDT)
                p = p_ref[h, rows, :].astype(F32)
                dp = _dot_nt(doh, vh)
                dd = jnp.sum(p * dp, axis=-1, keepdims=True)
                ds = p * (dp - dd)
                row_dots[bi] = row_dots[bi] + jnp.where(lane == h, dd, 0.0)
                dqh = _dot(ds, kh) * (ATT_HD ** -0.5)
                dqs.append(dqh.astype(CDT))
                dbq_ref[h] += _colsum(dqh)
                dkh = _dot_tn(ds, qh) * (ATT_HD ** -0.5)
                dvh = _dot_tn(p, doh)
                dk = dkh if dk is None else dk + dkh
                dv = dvh if dv is None else dv + dvh
            for i in range(ATT_G // 2):
                lanes = pl.ds((kvh * ATT_G + 2 * i) * ATT_HD, 2 * ATT_HD)
                dq_ref[rows, lanes] = _pair_lanes(dqs[2 * i], dqs[2 * i + 1])
            dkv_ref[kvh, rows_prev, :] += dk[:WINDOW]
            dkv_ref[kvh, rows_cur, :] += dk[WINDOW:]
            dkv_ref[ATT_KVH + kvh, rows_prev, :] += dv[:WINDOW]
            dkv_ref[ATT_KVH + kvh, rows_cur, :] += dv[WINDOW:]
        for bi in range(ATT_BPB):
            dsinks = dsinks - _colsum(ps_ref[pl.ds(bi * WINDOW, WINDOW), :] * row_dots[bi])
        dsink_ref[...] += dsinks

    rows_spec = pl.BlockSpec((ATT_BPB * WINDOW, D), lambda n: (n, 0))
    return _call(
        None, body, grid=(nb,),
        in_specs=[cur(ATT_QH), cur(2 * ATT_KVH), prev(2 * ATT_KVH),
                  pl.BlockSpec((ATT_QH, ATT_BPB * WINDOW, 2 * WINDOW), lambda n: (0, n, 0)),
                  pl.BlockSpec((ATT_BPB * WINDOW, 128), lambda n: (n, 0)), rows_spec],
        out_specs=[rows_spec, pl.BlockSpec((2 * ATT_KVH, S, ATT_HD), lambda n: (0, 0, 0)),
                   pl.BlockSpec((ATT_QH, 1, ATT_HD), lambda n: (0, 0, 0)), pl.BlockSpec((1, 128), lambda n: (0, 0))],
        out_shape=[_sds((S, D), CDT), _sds((2 * ATT_KVH, S, ATT_HD), F32), _sds((ATT_QH, 1, ATT_HD), F32),
                   _sds((1, 128), F32)],
        compiler_params=_params(("arbitrary",)), name="attn_bwd")(q4, kv4, kv4, probs, sink_probs, do)


def _local_step(x, p, target, getw, sm, emit):
    S = x.shape[0]
    vec = lambda a: a.reshape(1, -1)
    ln_g = lambda l, k: vec(sm["ln_gain"][l, k])
    ln_b = lambda l, k: vec(sm["ln_bias"][l, k])
    xb = x.astype(CDT)
    pb = p.astype(CDT)

    proj = _mm_nn(xb, getw("a_w_in"), (4, S, D), (None, _tile(S), 512), lambda g, i: (g // 2, i, g % 2), F32,
                  name="a_in")
    o_a, y_a, states = _hgrn_fwd(proj, sm["a_lower_bound"], sm["a_norm_gain"])
    zeros = jnp.zeros((1, D), F32)
    z = [[None] * 3 for _ in range(2)]
    xs = [[None] * 3 for _ in range(2)]
    xbs = [[None] * 3 for _ in range(2)]
    z[0][0], xs[0][0], xbs[0][0] = _mixout_ln(y_a[None], getw("a_w_out")[None], zeros, x, ln_g(0, 0), ln_b(0, 0),
                                              "a_out_ln")
    gu, hid, sgs, ups = [None, None], [None, None], [None, None], [None, None]

    def ffn_ple(l):
        wgu = getw(f"gu{l}")
        gu[l], hid[l], z[l][1], xs[l][1], xbs[l][1] = _ffn_fwd(
            xs[l][0], xbs[l][0], wgu, getw(f"dn{l}"), ln_g(l, 1), ln_b(l, 1), f"ffn_fwd{l}")
        sgs[l], ups[l], z[l][2], xs[l][2], xbs[l][2] = _ple_fwd(
            xs[l][1], xbs[l][1], pb[l], getw(f"pg{l}"), vec(sm["ple_b_gate"][l]), getw(f"pu{l}"), ln_g(l, 2),
            ln_b(l, 2), f"ple_fwd{l}")

    ffn_ple(0)
    x3, x3b = xs[0][2], xbs[0][2]
    w_kv, w_q, w_bo = getw("kv_w"), getw("b_w_q"), getw("b_w_out")
    kv4 = _proj_heads(x3b, w_kv, vec(sm["kv_b"]), 2 * ATT_KVH, "kv_proj")
    q4 = _proj_heads(x3b, w_q, vec(sm["b_b_q"]), ATT_QH, "q_proj")
    o_b, probs, sink_probs = _attn_fwd(q4, kv4, sm["b_sinks"])
    z[1][0], xs[1][0], xbs[1][0] = _mixout_ln(o_b[None], w_bo[None], sm["b_b_out"], x3, ln_g(1, 0), ln_b(1, 0),
                                              "b_out_ln")
    ffn_ple(1)
    loss, dy = _loss_fwd_bwd(xs[1][2], target)

    gs = {}
    d_ln_g = [[None] * 3 for _ in range(2)]
    d_ln_b = [[None] * 3 for _ in range(2)]
    g_bg = [None, None]

    def ffn_ple_bwd(l, dy, after=None):
        dx2, dgl, dup, d_ln_g[l][2], d_ln_b[l][2], g_bg[l] = _ple_bwd(dy, z[l][2], sgs[l], ups[l], ln_g(l, 2),
                                                                     getw(f"pg{l}"), f"ple_bwd{l}", after=after)
        g_pg = _wgrad(xbs[l][1][None], dgl[None], f"g_ple_gate{l}")[0]
        g_pu = _wgrad(pb[l][None], dup[None], f"g_ple_up{l}")[0]
        tok = emit({f"pg{l}": g_pg, f"pu{l}": g_pu})
        dz2, dzb, dgu, d_ln_g[l][1], d_ln_b[l][1] = _ffn_bwd_hidden(dx2, z[l][1], gu[l], getw(f"dn{l}"), ln_g(l, 1),
                                                                   f"ffn_bwd{l}", after=tok)
        tok = emit({f"dn{l}": _wgrad(hid[l], dzb[None], f"g_ffn_down{l}")})
        g_gu = _wgrad(dgu.reshape(8, S, FFN_B), xbs[l][0][None], f"g_ffn_gate_up{l}")
        tok = tok + emit({f"gu{l}": g_gu})
        dx1 = _ffn_bwd_input(dz2, dgu, getw(f"gu{l}"), f"ffn_bwd{l}", after=tok)
        return dx1, None

    dx1, tok = ffn_ple_bwd(1, dy)
    dz, dzb, do, d_ln_g[1][0], d_ln_b[1][0], gs["b_b_out"] = _mixout_bwd(dx1, z[1][0], ln_g(1, 0), w_bo[None], CDT,
                                                                        "b_out_bwd", after=tok)
    g_bo = _wgrad(o_b[None], dzb[None], "g_b_w_out")[0]
    dq, dkv4, dbq, dsinks = _attn_bwd(q4, kv4, probs, sink_probs, do[0])
    gs["b_b_q"] = dbq
    gs["b_sinks"] = dsinks
    g_q = _wgrad(x3b[None], dq[None], "g_b_w_q")[0]
    dx3, dkv, gs["kv_b"] = _qkv_bwd(dz, dq, dkv4, w_q, w_kv, "qkv_bwd", after=tok)
    g_kv = _wgrad(x3b[None], dkv[None], "g_kv_w")[0]
    tok = emit({"b_w_out": g_bo, "b_w_q": g_q, "kv_w": g_kv})
    dx1, tok = ffn_ple_bwd(0, dx3, tok)
    w_ao = getw("a_w_out")
    dz, dzb, dyr, d_ln_g[0][0], d_ln_b[0][0], _ = _mixout_bwd(dx1, z[0][0], ln_g(0, 0), w_ao[None], F32, "a_out_bwd",
                                                              after=tok)
    g_ao = _wgrad(y_a[None], dzb[None], "g_a_w_out")[0]
    tok = emit({"a_w_out": g_ao})
    dproj, gs["a_lower_bound"], gs["a_norm_gain"] = _hgrn_bwd(proj, sm["a_lower_bound"], sm["a_norm_gain"], o_a, states,
                                                              dyr[0], after=tok)
    tk = lambda t: (None, t, D)
    g_ain = _mm_tn(xb[None], dproj, N_DEV, lambda g, k: (0, k, 0), lambda g, k: (g // 2, k, g % 2),
                   tk, lambda t: (None, t, 512), (N_DEV, D, 512), (None, D, 512), lambda g, k: (g, 0, 0), name="g_a_w_in")
    gs["ple_b_gate"] = jnp.concatenate(g_bg, axis=0)
    gs["ln_gain"] = jnp.stack([jnp.concatenate(r, axis=0) for r in d_ln_g])
    gs["ln_bias"] = jnp.stack([jnp.concatenate(r, axis=0) for r in d_ln_b])
    gs["loss"] = loss
    tok = emit({"a_w_in": g_ain}, small=gs)
    grad_x = _inproj_bwd(dz, dproj, getw("a_w_in"), "a_in_bwd", after=tok)
    return loss, grad_x, gs


def _peer(k):
    x, y, c = lax.axis_index("x"), lax.axis_index("y"), lax.axis_index("c")
    px = 1 - x if k & 4 else x
    py = 1 - y if k & 2 else y
    pc = 1 - c if k & 1 else c
    return (px, py, pc), 4 * px + 2 * py + pc


def _my_index():
    return 4 * lax.axis_index("x") + 2 * lax.axis_index("y") + lax.axis_index("c")


def _piece_copy(mode, src, land, send_sems, recv_sems, t, k, sender, receiver, peer):
    return pltpu.make_async_remote_copy(
        src_ref=src if mode == "gather" else src.at[receiver], dst_ref=land.at[sender],
        send_sem=send_sems.at[t * 7 + k - 1], recv_sem=recv_sems.at[t * 7 + k - 1], device_id=peer, device_id_type=MESH)


def _sequencer_exchange(srcs, modes, name, collective_id, after=None):
    n = len(srcs)
    land_shapes = [((N_DEV,) + a.shape) if mode == "gather" else a.shape for a, mode in zip(srcs, modes)]
    extra = [] if after is None else [after]

    def body(*refs):
        src_refs, land_refs = refs[:n], refs[n + len(extra):2 * n + len(extra)]
        send_sems, recv_sems, local_sems = refs[2 * n + len(extra):]
        barrier = pltpu.get_barrier_semaphore()
        for k in range(1, N_DEV):
            pl.semaphore_signal(barrier, inc=1, device_id=_peer(k)[0], device_id_type=MESH)
        pl.semaphore_wait(barrier, N_DEV - 1)
        me = _my_index()
        local = []
        for i in range(n):
            cp = pltpu.make_async_copy(src_refs[i] if modes[i] == "gather" else src_refs[i].at[me], land_refs[i].at[me],
                                       local_sems.at[i])
            cp.start()
            local.append(cp)
        for k in range(1, N_DEV):
            peer, pid = _peer(k)
            for t in range(n):
                _piece_copy(modes[t], src_refs[t], land_refs[t], send_sems, recv_sems, t, k, me, pid, peer).start()
        for k in range(1, N_DEV):
            peer, pid = _peer(k)
            for t in range(n):
                _piece_copy(modes[t], src_refs[t], land_refs[t], send_sems, recv_sems, t, k, pid, me, peer).wait_recv()
        for k in range(1, N_DEV):
            peer, pid = _peer(k)
            for t in range(n):
                _piece_copy(modes[t], src_refs[t], land_refs[t], send_sems, recv_sems, t, k, me, pid, peer).wait_send()
        for cp in local:
            cp.wait()

    return pl.kernel(
        body, out_type=[_sds(s, a.dtype) for s, a in zip(land_shapes, srcs)],
        mesh=plsc.ScalarSubcoreMesh(axis_name="sequencer", num_cores=1),
        scratch_types=[pltpu.SemaphoreType.DMA((7 * n,)), pltpu.SemaphoreType.DMA((7 * n,)), pltpu.SemaphoreType.DMA((n,))],
        compiler_params=pltpu.CompilerParams(collective_id=collective_id), name=name)(*srcs, *extra)


def _sequencer_gather(srcs, name, collective_id, after=None):
    n = len(srcs)
    extra = [] if after is None else [after]

    def body(*refs):
        src_refs, land_refs = refs[:n], refs[n + len(extra):2 * n + len(extra)]
        send_sems, recv_sems, local_sems = refs[2 * n + len(extra):]
        x, y, c = lax.axis_index("x"), lax.axis_index("y"), lax.axis_index("c")
        sibling = (x, y, 1 - c)
        chips = [(1 - x, y), (x, 1 - y), (1 - x, 1 - y)]
        index = lambda px, py, pc: 4 * px + 2 * py + pc
        barrier = pltpu.get_barrier_semaphore()
        for peer in [sibling] + [(*chip, c) for chip in chips]:
            pl.semaphore_signal(barrier, inc=1, device_id=peer, device_id_type=MESH)
        pl.semaphore_wait(barrier, 4)

        def copy(t, k, slot, to, src=None):
            return pltpu.make_async_remote_copy(
                src_ref=land_refs[t].at[slot] if src is None else src, dst_ref=land_refs[t].at[slot],
                send_sem=send_sems.at[7 * t + k], recv_sem=recv_sems.at[7 * t + k], device_id=to, device_id_type=MESH)

        me = index(x, y, c)
        local = []
        for t in range(n):
            cp = pltpu.make_async_copy(src_refs[t], land_refs[t].at[me], local_sems.at[t])
            cp.start()
            local.append(cp)
        sends = []
        for t in range(n):
            sends.append(copy(t, 0, me, sibling, src=src_refs[t]))
            sends += [copy(t, 1 + j, me, (*chip, c), src=src_refs[t]) for j, chip in enumerate(chips)]
        for cp in sends:
            cp.start()
        for j, chip in enumerate(chips):
            for t in range(n):
                copy(t, 1 + j, index(*chip, c), sibling, src=src_refs[t]).wait_recv()
                passed = copy(t, 4 + j, index(*chip, c), sibling)
                passed.start()
                sends.append(passed)
        for t in range(n):
            copy(t, 0, index(x, y, 1 - c), sibling, src=src_refs[t]).wait_recv()
        for j, chip in enumerate(chips):
            for t in range(n):
                copy(t, 4 + j, index(*chip, 1 - c), sibling, src=src_refs[t]).wait_recv()
        for cp in sends:
            cp.wait_send()
        for cp in local:
            cp.wait()

    return pl.kernel(
        body, out_type=[_sds((N_DEV,) + a.shape, a.dtype) for a in srcs],
        mesh=plsc.ScalarSubcoreMesh(axis_name="sequencer", num_cores=1),
        scratch_types=[pltpu.SemaphoreType.DMA((7 * n,)), pltpu.SemaphoreType.DMA((7 * n,)), pltpu.SemaphoreType.DMA((n,))],
        compiler_params=pltpu.CompilerParams(collective_id=collective_id), name=name)(*srcs, *extra)


def _adamw(w, g, m, v):
    m = ADAM_B1 * m + (1.0 - ADAM_B1) * g
    v = ADAM_B2 * v + (1.0 - ADAM_B2) * (g * g)
    m_hat = m / (1.0 - ADAM_B1 ** ADAM_STEP)
    v_hat = v / (1.0 - ADAM_B2 ** ADAM_STEP)
    delta = -ADAM_LR * (m_hat / (jnp.sqrt(v_hat) + ADAM_EPS) + ADAM_WD * w)
    return delta, m, v


def _adam_big(w, parts, m, v, name, after=None):
    L, R, C = w.shape
    P = parts[0].shape[0]
    tr = _tile(R, (256, 128, 176, 64, 32, 16))
    nr = R // tr

    def body(w_ref, *refs):
        p_refs, (m_ref, v_ref, g_ref, d_ref, mo_ref, vo_ref) = refs[:L], refs[L:]
        for l in range(L):
            @pl.when(pl.program_id(0) == l)
            def _(p_ref=p_refs[l]):
                g = p_ref[0].astype(F32)
                for s in range(1, P):
                    g = g + p_ref[s].astype(F32)
                g_ref[...] = g
                d_ref[...], mo_ref[...], vo_ref[...] = _adamw(w_ref[...], g, m_ref[...], v_ref[...])

    row = pl.BlockSpec((None, tr, C), lambda l, i: (l, i, 0))
    park = lambda l_of: (lambda l, i: (0, jnp.where(l == l_of, i, 0 if l_of else nr - 1), 0))
    return _call(
        after, body, grid=(L, nr),
        in_specs=[row] + [pl.BlockSpec((P, tr, C), park(l)) for l in range(L)] + [row, row],
        out_specs=[row] * 4, out_shape=[_sds((L, R, C), F32)] * 4,
        compiler_params=_params(("arbitrary", "arbitrary")), name=name)(w, *parts, m, v)


SMALL = (("a_lower_bound", (2, 128), (2, D)), ("ln_gain", (6, 128), (6, D)), ("ln_bias", (6, 128), (6, D)),
         ("a_norm_gain", (1, 128), (1, 128)), ("kv_b", (1, 512), (1, 512)), ("b_b_q", (1, D), (1, D)),
         ("b_sinks", (1, ATT_QH), (1, 128)), ("b_b_out", (1, D), (1, D)), ("ple_b_gate", (2, D), (2, D)))


def _adam_small(parts, w, m, v, losses, after=None):
    k = len(SMALL)

    def body(*refs):
        p_refs, w_refs, m_refs, v_refs = refs[:k], refs[k:2 * k], refs[2 * k:3 * k], refs[3 * k:4 * k]
        loss_ref, outs, total_ref = refs[4 * k], refs[4 * k + 1:-1], refs[-1]
        total = loss_ref[0]
        for s in range(1, N_DEV):
            total = total + loss_ref[s]
        total_ref[...] = total
        me = _my_index()
        for i, (_, wshape, pshape) in enumerate(SMALL):
            cols = wshape[1]
            lanes = slice(None) if cols == pshape[1] else (
                pl.ds(0, cols) if cols < 128 else pl.ds(pl.multiple_of(me * cols, cols), cols))
            g = p_refs[i][0, :, lanes]
            for s in range(1, N_DEV):
                g = g + p_refs[i][s, :, lanes]
            g_ref, d_ref, mo_ref, vo_ref = outs[4 * i:4 * i + 4]
            g_ref[...] = g
            d_ref[...], mo_ref[...], vo_ref[...] = _adamw(w_refs[i][...], g, m_refs[i][...], v_refs[i][...])

    full = lambda shape: pl.BlockSpec(shape, lambda: (0,) * len(shape))
    names = [n for n, _, _ in SMALL]
    res = _call(
        after, body,
        in_specs=[full((N_DEV,) + ps) for _, _, ps in SMALL] + [full(ws) for _, ws, _ in SMALL] * 3
        + [full((N_DEV, 1, 128))],
        out_specs=[full(ws) for _, ws, _ in SMALL for _ in range(4)] + [full((1, 128))],
        out_shape=[_sds(ws, F32) for _, ws, _ in SMALL for _ in range(4)] + [_sds((1, 128), F32)], name="adam_small")(
            *[parts[n] for n in names], *[a[n].reshape(ws) for a in (w, m, v) for n, ws, _ in SMALL], losses)
    return {n: [r.reshape(w[n].shape) for r in res[4 * i:4 * i + 4]] for i, n in enumerate(names)}, res[-1][0, 0]


WEIGHTS = ("a_w_in", "a_lower_bound", "a_norm_gain", "a_w_out", "kv_w", "kv_b", "b_w_q", "b_b_q", "b_sinks", "b_w_out",
           "b_b_out", "ffn_w_gate_up", "ffn_w_down", "ple_w_up", "ple_w_gate", "ple_b_gate", "ln_gain", "ln_bias")


GATHER_GROUPS = (("a_w_in",), ("a_w_out", "gu0"), ("dn0", "pu0", "pg0"), ("kv_w", "b_w_q", "b_w_out"), ("gu1",),
                 ("dn1", "pu1", "pg1"))
KERNEL_LAYOUT = {
    "a_w_in": lambda a: a,
    "a_w_out": lambda a: a.reshape(D, D),
    "kv_w": lambda a: a.reshape(D, 2 * ATT_KVH * ATT_HD),
    "b_w_q": lambda a: a.reshape(D, D),
    "b_w_out": lambda a: a.reshape(D, D),
    "gu": lambda a: a.reshape(2, 4, FFN_B, D),
    "dn": lambda a: a.reshape(4, FFN_B, D),
    "pu": lambda a: a,
    "pg": lambda a: a.reshape(D, D),
}
_row_blocks = lambda a: a.reshape(N_DEV, -1, a.shape[-1])
OWNER_BLOCKS = {
    "a_w_in": lambda g: g,
    "a_w_out": _row_blocks,
    "kv_w": _row_blocks,
    "b_w_q": _row_blocks,
    "b_w_out": _row_blocks,
    "gu": lambda g: g,
    "dn": lambda g: _row_blocks(g.reshape(FFN_H, D)),
    "pu": lambda g: g.reshape(PLE_DIM, N_DEV, 128).transpose(1, 0, 2),
    "pg": _row_blocks,
}
ADAM_PARTS = (("kv_w", ("kv_w",)), ("b_w_q", ("b_w_q",)), ("b_w_out", ("b_w_out",)), ("ffn_w_gate_up", ("gu0", "gu1")),
              ("ffn_w_down", ("dn0", "dn1")), ("ple_w_up", ("pu0", "pu1")), ("ple_w_gate", ("pg0", "pg1")),
              ("a_w_out", ("a_w_out",)), ("a_w_in", ("a_w_in",)))


def kernel(x, p, a_w_in, a_lower_bound, a_norm_gain, a_w_out, kv_w, kv_b, b_w_q, b_b_q, b_sinks, b_w_out, b_b_out, ffn_w_gate_up, ffn_w_down, ple_w_up, ple_w_gate, ple_b_gate, ln_gain, ln_bias, loss_target, m_a_w_in, m_a_lower_bound, m_a_norm_gain, m_a_w_out, m_kv_w, m_kv_b, m_b_w_q, m_b_b_q, m_b_sinks, m_b_w_out, m_b_b_out, m_ffn_w_gate_up, m_ffn_w_down, m_ple_w_up, m_ple_w_gate, m_ple_b_gate, m_ln_gain, m_ln_bias, v_a_w_in, v_a_lower_bound, v_a_norm_gain, v_a_w_out, v_kv_w, v_kv_b, v_b_w_q, v_b_b_q, v_b_sinks, v_b_w_out, v_b_b_out, v_ffn_w_gate_up, v_ffn_w_down, v_ple_w_up, v_ple_w_gate, v_ple_b_gate, v_ln_gain, v_ln_bias):
    given = dict(locals())
    w = {n: given[n] for n in WEIGHTS}
    m = {n: given["m_" + n] for n in WEIGHTS}
    v = {n: given["v_" + n] for n in WEIGHTS}
    shards = {"a_w_in": a_w_in[0], "a_w_out": a_w_out[0], "kv_w": kv_w, "b_w_q": b_w_q[0], "b_w_out": b_w_out[0]}
    for l in range(2):
        shards.update({f"gu{l}": ffn_w_gate_up[l].T, f"dn{l}": ffn_w_down[l], f"pu{l}": ple_w_up[l], f"pg{l}": ple_w_gate[l]})
    sharded_small = [a_lower_bound, ln_gain.reshape(6, 128), ln_bias.reshape(6, 128)]
    gathered = {}
    for gi, g in enumerate(GATHER_GROUPS):
        lands = _sequencer_gather([shards[n].astype(CDT) for n in g] + (sharded_small if gi == 0 else []),
                                  f"gather{gi}", gi)
        for n, a in zip(g, lands):
            gathered[n] = KERNEL_LAYOUT[n.rstrip("01")](a)
        if gi == 0:
            alb, lng, lnb = [a.transpose(1, 0, 2).reshape(a.shape[1], D) for a in lands[len(g):]]

    getw = gathered.__getitem__

    sm = {"a_lower_bound": alb, "ln_gain": lng.reshape(2, 3, D), "ln_bias": lnb.reshape(2, 3, D),
          "a_norm_gain": a_norm_gain, "kv_b": kv_b, "b_b_q": b_b_q[0], "b_sinks": b_sinks, "b_b_out": b_b_out,
          "ple_b_gate": ple_b_gate}

    scatters, small_parts = [], {}

    def emit(grads, small=None):
        names = list(grads)
        blocks = [OWNER_BLOCKS[n.rstrip("01")](grads[n]) for n in names]
        partials = [] if small is None else [small[n].reshape(ps) for n, _, ps in SMALL] + [small["loss"]]
        lands = _sequencer_exchange(blocks + partials, ["scatter"] * len(blocks) + ["gather"] * len(partials),
                                    f"scatter{len(scatters)}", len(GATHER_GROUPS) + len(scatters))
        scatters.append(dict(zip(names, lands)))
        small_parts.update(zip([n for n, _, _ in SMALL] + ["loss"], lands[len(blocks):]))
        return blocks + (list(scatters[-4].values()) if len(scatters) >= 4 else [])

    loss, grad_x, gs = _local_step(x[0], p[:, 0], loss_target[0], getw, sm, emit)

    out, parts, last = {}, {}, [grad_x]
    for landed in scatters:
        parts.update(landed)
        for n, keys in ADAM_PARTS:
            if n in out or not all(key in parts for key in keys):
                continue
            lrc = (1,) * (3 - w[n].ndim) + w[n].shape
            shard = (lambda a: a.reshape(lrc).swapaxes(1, 2)) if n == "ffn_w_gate_up" else (lambda a: a.reshape(lrc))
            res = _adam_big(shard(w[n]), [parts[key] for key in keys], shard(m[n]), shard(v[n]), "adam_" + n, after=last)
            out[n] = [(r.swapaxes(1, 2) if n == "ffn_w_gate_up" else r).reshape(w[n].shape) for r in res]
            last = [res[3]]
    small_out, loss = _adam_small(small_parts, w, m, v, small_parts["loss"], after=last)
    out.update(small_out)
    res = [loss, grad_x[None]]
    for i in range(4):
        res += [out[n][i] for n in WEIGHTS]
    return tuple(res)
```

```python
import jax
import jax.numpy as jnp
from jax import lax
from jax.experimental import pallas as pl
from jax.experimental.pallas import tpu as pltpu
from jax.experimental.pallas import tpu_sc as plsc

F32 = jnp.float32
CDT = jnp.bfloat16

N_DEV = 8
D = 1024
HG_H, HG_DK, HG_CH = 8, 128, 64
HG_HPB = 4
HG_CPB = 8
ATT_HD, ATT_QH, ATT_KVH, ATT_G, WINDOW = 64, 16, 4, 4, 128
ATT_BPB = 1
FFN_H = 2816
FFN_B = FFN_H // 4
PLE_DIM = 256
ALPHA = (2.0 * 2) ** 0.25
LN_EPS = 1e-5
RMS_EPS = 1e-6
ADAM_LR, ADAM_B1, ADAM_B2, ADAM_EPS, ADAM_WD, ADAM_STEP = 0.001, 0.9, 0.999, 1e-08, 0.01, 10
ROW_TILES = (512, 256, 128, 64)
VMEM_LIMIT = 48 * 1024 * 1024
NEG = -1e30

MESH = pl.DeviceIdType.MESH


def _tile(n, cands=ROW_TILES):
    for t in cands:
        if n % t == 0:
            return t
    return n


def _sds(shape, dtype):
    return jax.ShapeDtypeStruct(tuple(shape), dtype)


def _params(sem):
    return pltpu.CompilerParams(dimension_semantics=sem, vmem_limit_bytes=VMEM_LIMIT)


def _dot(a, b):
    return jnp.dot(a.astype(CDT), b.astype(CDT), preferred_element_type=F32)


def _dot_nt(a, b):
    return lax.dot_general(a.astype(CDT), b.astype(CDT), (((1,), (1,)), ((), ())), preferred_element_type=F32)


def _dot_tn(a, b):
    return lax.dot_general(a.astype(CDT), b.astype(CDT), (((0,), (0,)), ((), ())), preferred_element_type=F32)


def _sigmoid(x):
    return jax.nn.sigmoid(x)


def _ln_fwd(z, g, b):
    mu = jnp.mean(z, axis=-1, keepdims=True)
    zc = z - mu
    var = jnp.mean(zc * zc, axis=-1, keepdims=True)
    return zc * lax.rsqrt(var + LN_EPS) * g + b


def _ln_bwd(z, g, dy):
    mu = jnp.mean(z, axis=-1, keepdims=True)
    zc = z - mu
    var = jnp.mean(zc * zc, axis=-1, keepdims=True)
    rstd = lax.rsqrt(var + LN_EPS)
    xhat = zc * rstd
    dxh = dy * g
    dz = rstd * (dxh - jnp.mean(dxh, axis=-1, keepdims=True) - xhat * jnp.mean(dxh * xhat, axis=-1, keepdims=True))
    return dz, xhat


def _colsum(x):
    return jnp.sum(x, axis=0, keepdims=True)


def _acc(ref, val, first):
    @pl.when(first)
    def _():
        ref[...] = val

    @pl.when(jnp.logical_not(first))
    def _():
        ref[...] += val


def _zero_at(ref, first):
    @pl.when(first)
    def _():
        ref[...] = jnp.zeros_like(ref)


def _call(after, body, **kw):
    after = [] if after is None else list(after)
    specs = list(kw["in_specs"])
    kw["in_specs"] = [pl.BlockSpec(memory_space=pl.ANY)] * len(after) + specs

    def ordered_body(*refs):
        body(*refs[len(after):])

    call = pl.pallas_call(ordered_body, **kw)
    return lambda *args: call(*after, *args)


def _mm_tn(a3, b3, G, amap, bmap, ablock, bblock, out_shape, oblock, omap, name="mm_tn"):
    S = a3.shape[1]

    def body(a_ref, b_ref, o_ref):
        o_ref[...] = _dot_tn(a_ref[...], b_ref[...]).astype(o_ref.dtype)

    return _call(
        None, body, grid=(G, 1),
        in_specs=[pl.BlockSpec(ablock(S), amap), pl.BlockSpec(bblock(S), bmap)],
        out_specs=pl.BlockSpec(oblock, omap), out_shape=_sds(out_shape, CDT),
        compiler_params=_params(("arbitrary", "arbitrary")), name=name)(a3, b3)


def _wgrad(a3, b3, name):
    Ga, S, M = a3.shape
    Gb, _, N = b3.shape
    G = max(Ga, Gb)
    return _mm_tn(
        a3, b3, G,
        (lambda g, k: (g, k, 0)) if Ga > 1 else (lambda g, k: (0, k, 0)),
        (lambda g, k: (g, k, 0)) if Gb > 1 else (lambda g, k: (0, k, 0)),
        lambda tk: (None, tk, M), lambda tk: (None, tk, N),
        (G, M, N), (None, M, N), lambda g, k: (g, 0, 0), name=name)


def _mixout_ln(u3, w3, bias, xin, gain, beta, name):
    G, S, Kb = u3.shape
    tm = _tile(S)

    def body(u_ref, w_ref, b_ref, x_ref, g_ref, be_ref, z_ref, xo_ref, xob_ref):
        h = b_ref[...] + _dot(u_ref[0], w_ref[0])
        for g in range(1, G):
            h = h + _dot(u_ref[g], w_ref[g])
        z = ALPHA * x_ref[...] + h
        z_ref[...] = z
        y = _ln_fwd(z, g_ref[...], be_ref[...])
        xo_ref[...] = y
        xob_ref[...] = y.astype(CDT)

    row = pl.BlockSpec((tm, D), lambda i: (i, 0))
    vec = pl.BlockSpec((1, D), lambda i: (0, 0))
    return _call(
        None, body, grid=(S // tm,),
        in_specs=[pl.BlockSpec((G, tm, Kb), lambda i: (0, i, 0)), pl.BlockSpec((G, Kb, D), lambda i: (0, 0, 0)),
                  vec, row, vec, vec],
        out_specs=[row, row, row], out_shape=[_sds((S, D), F32), _sds((S, D), F32), _sds((S, D), CDT)],
        compiler_params=_params(("arbitrary",)), name=name)(u3, w3, bias, xin, gain, beta)


def _ffn_fwd(xin, xin_b, wgu, wdn, gain, beta, name):
    S = xin.shape[0]
    tm = _tile(S)

    def hidden(xb_ref, wgu_ref, gu_ref, hid_ref):
        xb = xb_ref[...]
        gate = _dot_nt(xb, wgu_ref[0])
        up = _dot_nt(xb, wgu_ref[1])
        gu_ref[0] = gate.astype(CDT)
        gu_ref[1] = up.astype(CDT)
        hid_ref[...] = (gate * _sigmoid(gate) * up).astype(CDT)

    gu, hid = _call(
        None, hidden, grid=(4, S // tm),
        in_specs=[pl.BlockSpec((tm, D), lambda j, i: (i, 0)), pl.BlockSpec((2, None, FFN_B, D), lambda j, i: (0, j, 0, 0))],
        out_specs=[pl.BlockSpec((2, None, tm, FFN_B), lambda j, i: (0, j, i, 0)),
                   pl.BlockSpec((None, tm, FFN_B), lambda j, i: (j, i, 0))],
        out_shape=[_sds((2, 4, S, FFN_B), CDT), _sds((4, S, FFN_B), CDT)],
        compiler_params=_params(("arbitrary", "arbitrary")), name=name + "_hidden")(xin_b, wgu)

    def down(x_ref, hid_ref, wdn_ref, g_ref, be_ref, z_ref, xo_ref, xob_ref):
        z = ALPHA * x_ref[...]
        for j in range(4):
            z = z + _dot(hid_ref[j], wdn_ref[j])
        z_ref[...] = z
        y = _ln_fwd(z, g_ref[...], be_ref[...])
        xo_ref[...] = y
        xob_ref[...] = y.astype(CDT)

    row = pl.BlockSpec((tm, D), lambda i: (i, 0))
    vec = pl.BlockSpec((1, D), lambda i: (0, 0))
    z, xo, xob = _call(
        None, down, grid=(S // tm,),
        in_specs=[row, pl.BlockSpec((4, tm, FFN_B), lambda i: (0, i, 0)), pl.BlockSpec((4, FFN_B, D), lambda i: (0, 0, 0)),
                  vec, vec],
        out_specs=[row, row, row], out_shape=[_sds((S, D), F32), _sds((S, D), F32), _sds((S, D), CDT)],
        compiler_params=_params(("arbitrary",)), name=name + "_down")(xin, hid, wdn, gain, beta)
    return gu, hid, z, xo, xob


def _ple_fwd(xin, xin_b, p_b, wpg, bgate, wpu, gain, beta, name):
    S = xin.shape[0]
    tm = _tile(S)

    def body(x_ref, xb_ref, p_ref, wpg_ref, bg_ref, wpu_ref, g_ref, be_ref, sg_ref, up_ref, z_ref, xo_ref, xob_ref):
        sg = _sigmoid(_dot(xb_ref[...], wpg_ref[...]) + bg_ref[...])
        pb = p_ref[...]
        up = jnp.concatenate([_dot(pb, wpu_ref[j]) for j in range(N_DEV)], axis=-1)
        sg_ref[...] = sg.astype(CDT)
        up_ref[...] = (up * sg * (1.0 - sg)).astype(CDT)
        z = ALPHA * x_ref[...] + sg * up
        z_ref[...] = z
        y = _ln_fwd(z, g_ref[...], be_ref[...])
        xo_ref[...] = y
        xob_ref[...] = y.astype(CDT)

    row = pl.BlockSpec((tm, D), lambda i: (i, 0))
    vec = pl.BlockSpec((1, D), lambda i: (0, 0))
    return _call(
        None, body, grid=(S // tm,),
        in_specs=[row, row, pl.BlockSpec((tm, PLE_DIM), lambda i: (i, 0)), pl.BlockSpec((D, D), lambda i: (0, 0)), vec,
                  pl.BlockSpec((N_DEV, PLE_DIM, D // N_DEV), lambda i: (0, 0, 0)), vec, vec],
        out_specs=[row] * 5,
        out_shape=[_sds((S, D), CDT)] * 2 + [_sds((S, D), F32)] * 2 + [_sds((S, D), CDT)],
        compiler_params=_params(("arbitrary",)), name=name)(xin, xin_b, p_b, wpg, bgate, wpu, gain, beta)


def _loss_fwd_bwd(y, target):
    S = y.shape[0]
    tm = _tile(S)

    def body(y_ref, t_ref, l_ref, dy_ref):
        e = y_ref[...] - t_ref[...]
        dy_ref[...] = e * (1.0 / D)
        part = 0.5 * jnp.sum(jnp.sum(e * e, axis=-1, keepdims=True) * (1.0 / D), axis=0, keepdims=True)
        _acc(l_ref, jnp.broadcast_to(part, l_ref.shape), pl.program_id(0) == 0)

    row = pl.BlockSpec((tm, D), lambda i: (i, 0))
    return _call(
        None, body, grid=(S // tm,), in_specs=[row, row],
        out_specs=[pl.BlockSpec((1, 128), lambda i: (0, 0)), row],
        out_shape=[_sds((1, 128), F32), _sds((S, D), F32)],
        compiler_params=_params(("arbitrary",)), name="loss")(y, target)


def _ple_bwd(dy, z, sg, up, gain, wpg, name, after=None):
    S = dy.shape[0]
    tm = _tile(S)

    def body(dy_ref, z_ref, sg_ref, up_ref, g_ref, wpg_ref, dx_ref, dgl_ref, dup_ref, dgain_ref, dbeta_ref, dbg_ref):
        first = pl.program_id(0) == 0
        dy_ = dy_ref[...]
        dz, xhat = _ln_bwd(z_ref[...], g_ref[...], dy_)
        dgl = dz * up_ref[...].astype(F32)
        dgl_ref[...] = dgl.astype(CDT)
        dup_ref[...] = (dz * sg_ref[...].astype(F32)).astype(CDT)
        dx_ref[...] = ALPHA * dz + _dot_nt(dgl, wpg_ref[...])
        _acc(dgain_ref, _colsum(dy_ * xhat), first)
        _acc(dbeta_ref, _colsum(dy_), first)
        _acc(dbg_ref, _colsum(dgl), first)

    row = pl.BlockSpec((tm, D), lambda i: (i, 0))
    vec = pl.BlockSpec((1, D), lambda i: (0, 0))
    return _call(
        after, body, grid=(S // tm,), in_specs=[row, row, row, row, vec, pl.BlockSpec((D, D), lambda i: (0, 0))],
        out_specs=[row, row, row, vec, vec, vec],
        out_shape=[_sds((S, D), F32), _sds((S, D), CDT), _sds((S, D), CDT)] + [_sds((1, D), F32)] * 3,
        compiler_params=_params(("arbitrary",)), name=name)(dy, z, sg, up, gain, wpg)


def _ffn_bwd_hidden(dy, z, gu, wdn, gain, name, after=None):
    S = dy.shape[0]
    tm = _tile(S)

    def hidden(dy_ref, z_ref, gu_ref, wdn_ref, g_ref, dz_ref, dzb_ref, dgu_ref, dgain_ref, dbeta_ref):
        i, j = pl.program_id(0), pl.program_id(1)

        @pl.when(j == 0)
        def _():
            dy_ = dy_ref[...]
            dz, xhat = _ln_bwd(z_ref[...], g_ref[...], dy_)
            dz_ref[...] = dz
            dzb_ref[...] = dz.astype(CDT)
            _acc(dgain_ref, _colsum(dy_ * xhat), i == 0)
            _acc(dbeta_ref, _colsum(dy_), i == 0)

        dhid = _dot_nt(dzb_ref[...], wdn_ref[...])
        gate, up = gu_ref[0].astype(F32), gu_ref[1].astype(F32)
        sg = _sigmoid(gate)
        dgu_ref[0] = (dhid * up * (sg * (1.0 + gate * (1.0 - sg)))).astype(CDT)
        dgu_ref[1] = (dhid * (gate * sg)).astype(CDT)

    row = pl.BlockSpec((tm, D), lambda i, j: (i, 0))
    vec = pl.BlockSpec((1, D), lambda i, j: (0, 0))
    return _call(
        after, hidden, grid=(S // tm, 4),
        in_specs=[row, row, pl.BlockSpec((2, None, tm, FFN_B), lambda i, j: (0, j, i, 0)),
                  pl.BlockSpec((None, FFN_B, D), lambda i, j: (j, 0, 0)), vec],
        out_specs=[row, row, pl.BlockSpec((2, None, tm, FFN_B), lambda i, j: (0, j, i, 0)), vec, vec],
        out_shape=[_sds((S, D), F32), _sds((S, D), CDT), _sds((2, 4, S, FFN_B), CDT), _sds((1, D), F32),
                   _sds((1, D), F32)],
        compiler_params=_params(("arbitrary", "arbitrary")), name=name + "_hidden")(dy, z, gu, wdn, gain)


def _ffn_bwd_input(dz, dgu, wgu, name, after=None):
    S = dz.shape[0]
    tm = _tile(S)

    def to_input(dz_ref, dgu_ref, wgu_ref, dx_ref):
        acc = ALPHA * dz_ref[...]
        for g in range(2):
            for j in range(4):
                acc = acc + _dot(dgu_ref[g, j], wgu_ref[g, j])
        dx_ref[...] = acc

    rows = pl.BlockSpec((tm, D), lambda i: (i, 0))
    return _call(
        after, to_input, grid=(S // tm,),
        in_specs=[rows, pl.BlockSpec((2, 4, tm, FFN_B), lambda i: (0, 0, i, 0)),
                  pl.BlockSpec((2, 4, FFN_B, D), lambda i: (0, 0, 0, 0))],
        out_specs=rows, out_shape=_sds((S, D), F32),
        compiler_params=_params(("arbitrary",)), name=name + "_input")(dz, dgu, wgu)


def _mixout_bwd(dy, z, gain, w3, du_dtype, name, after=None):
    S = dy.shape[0]
    G, Kb, _ = w3.shape
    tm = _tile(S)

    def body(dy_ref, z_ref, g_ref, w_ref, dz_ref, dzb_ref, du_ref, dgain_ref, dbeta_ref, dbias_ref):
        first = pl.program_id(0) == 0
        dy_ = dy_ref[...]
        dz, xhat = _ln_bwd(z_ref[...], g_ref[...], dy_)
        dz_ref[...] = dz
        dzb = dz.astype(CDT)
        dzb_ref[...] = dzb
        for g in range(G):
            du_ref[g] = _dot_nt(dzb, w_ref[g]).astype(du_ref.dtype)
        _acc(dgain_ref, _colsum(dy_ * xhat), first)
        _acc(dbeta_ref, _colsum(dy_), first)
        _acc(dbias_ref, _colsum(dz), first)

    row = pl.BlockSpec((tm, D), lambda i: (i, 0))
    vec = pl.BlockSpec((1, D), lambda i: (0, 0))
    return _call(
        after, body, grid=(S // tm,), in_specs=[row, row, vec, pl.BlockSpec((G, Kb, D), lambda i: (0, 0, 0))],
        out_specs=[row, row, pl.BlockSpec((G, tm, Kb), lambda i: (0, i, 0)), vec, vec, vec],
        out_shape=[_sds((S, D), F32), _sds((S, D), CDT), _sds((G, S, Kb), du_dtype)] + [_sds((1, D), F32)] * 3,
        compiler_params=_params(("arbitrary",)), name=name)(dy, z, gain, w3)


def _half_select(low):
    r = lax.broadcasted_iota(jnp.int32, (2 * ATT_HD, ATT_HD), 0)
    c = lax.broadcasted_iota(jnp.int32, (2 * ATT_HD, ATT_HD), 1)
    return (r == c + (0 if low else ATT_HD)).astype(CDT)


def _half_place(low):
    r = lax.broadcasted_iota(jnp.int32, (ATT_HD, 2 * ATT_HD), 0)
    c = lax.broadcasted_iota(jnp.int32, (ATT_HD, 2 * ATT_HD), 1)
    return (c == r + (0 if low else ATT_HD)).astype(CDT)


def _pair_lanes(even, odd):
    return (jnp.dot(even, _half_place(True), preferred_element_type=F32)
            + jnp.dot(odd, _half_place(False), preferred_element_type=F32)).astype(CDT)


def _proj_heads(a, w, bias, heads, name):
    S, K = a.shape
    N = heads * ATT_HD
    tm = _tile(S)

    def body(a_ref, w_ref, b_ref, o_ref):
        acc = (_dot(a_ref[...], w_ref[...]) + b_ref[...]).astype(CDT)
        sel = (_half_select(True), _half_select(False))
        for h in range(heads):
            pair = acc[:, (h // 2) * 2 * ATT_HD:(h // 2 + 1) * 2 * ATT_HD]
            o_ref[h] = jnp.dot(pair, sel[h % 2], preferred_element_type=F32).astype(CDT)

    return _call(
        None, body, grid=(S // tm,),
        in_specs=[pl.BlockSpec((tm, K), lambda i: (i, 0)), pl.BlockSpec((K, N), lambda i: (0, 0)),
                  pl.BlockSpec((1, N), lambda i: (0, 0))],
        out_specs=pl.BlockSpec((heads, tm, ATT_HD), lambda i: (0, i, 0)), out_shape=_sds((heads, S, ATT_HD), CDT),
        compiler_params=_params(("arbitrary",)), name=name)(a, w, bias)


def _qkv_bwd(dz, dq, dkv4, wq, wkv, name, after=None):
    S = dz.shape[0]
    tm = _tile(S)
    HK = dkv4.shape[0]
    NK = HK * ATT_HD

    def body(dz_ref, dq_ref, dkv_ref, wq_ref, wkv_ref, dx_ref, dkvn_ref, dkvb_ref):
        first = pl.program_id(0) == 0
        dkvn = jnp.concatenate([_pair_lanes(dkv_ref[2 * i].astype(CDT), dkv_ref[2 * i + 1].astype(CDT))
                                for i in range(HK // 2)], axis=-1)
        dkvn_ref[...] = dkvn
        dx_ref[...] = ALPHA * dz_ref[...] + _dot_nt(dq_ref[...], wq_ref[...]) + _dot_nt(dkvn, wkv_ref[...])
        for h in range(HK):
            _acc(dkvb_ref.at[h], _colsum(dkv_ref[h]), first)

    row = pl.BlockSpec((tm, D), lambda i: (i, 0))
    return _call(
        after, body, grid=(S // tm,),
        in_specs=[row, row, pl.BlockSpec((HK, tm, ATT_HD), lambda i: (0, i, 0)),
                  pl.BlockSpec((D, D), lambda i: (0, 0)), pl.BlockSpec((D, NK), lambda i: (0, 0))],
        out_specs=[row, pl.BlockSpec((tm, NK), lambda i: (i, 0)), pl.BlockSpec((HK, 1, ATT_HD), lambda i: (0, 0, 0))],
        out_shape=[_sds((S, D), F32), _sds((S, NK), CDT), _sds((HK, 1, ATT_HD), F32)],
        compiler_params=_params(("arbitrary",)), name=name)(dz, dq, dkv4, wq, wkv)


def _inproj_fwd(xb, wain, name):
    S = xb.shape[0]
    tm = _tile(S) // 2
    nb = wain.shape[-1]

    def body(x_ref, w_ref, o_ref):
        x = x_ref[...]
        for j in range(N_DEV):
            o_ref[j // 2, :, pl.ds((j % 2) * nb, nb)] = _dot(x, w_ref[j])

    return _call(
        None, body, grid=(S // tm,),
        in_specs=[pl.BlockSpec((tm, D), lambda i: (i, 0)), pl.BlockSpec((N_DEV, D, nb), lambda i: (0, 0, 0))],
        out_specs=pl.BlockSpec((4, tm, D), lambda i: (0, i, 0)), out_shape=_sds((4, S, D), F32),
        compiler_params=_params(("arbitrary",)), name=name)(xb, wain)


def _inproj_bwd(dz, dproj, wain, name, after=None):
    S = dz.shape[0]
    tm = _tile(S)
    nb = wain.shape[-1]

    def body(dz_ref, dp_ref, w_ref, dx_ref):
        acc = ALPHA * dz_ref[...]
        for j in range(N_DEV):
            acc = acc + _dot_nt(dp_ref[j // 2, :, pl.ds((j % 2) * nb, nb)], w_ref[j])
        dx_ref[...] = acc

    row = pl.BlockSpec((tm, D), lambda i: (i, 0))
    return _call(
        after, body, grid=(S // tm,),
        in_specs=[row, pl.BlockSpec((4, tm, D), lambda i: (0, i, 0)), pl.BlockSpec((N_DEV, D, nb), lambda i: (0, 0, 0))],
        out_specs=row, out_shape=_sds((S, D), F32),
        compiler_params=_params(("arbitrary",)), name=name)(dz, dproj, wain)


def _running_sum(x, reverse=False):
    rows = x.shape[0]
    row = lax.broadcasted_iota(jnp.int32, x.shape, 0)
    step = 1
    while step < rows:
        if reverse:
            x = x + jnp.where(row < rows - step, pltpu.roll(x, rows - step, 0), 0.0)
        else:
            x = x + jnp.where(row >= step, pltpu.roll(x, step, 0), 0.0)
        step *= 2
    return x


def _hg_gates(q, f, alb_ref):
    a0, a1 = alb_ref[0:1, :], alb_ref[1:2, :]
    mx = jnp.maximum(a0, a1)
    e0, e1 = jnp.exp(a0 - mx), jnp.exp(a1 - mx)
    lb = e0 / (e0 + e1)
    sig = _sigmoid(f)
    forget = lb + (1.0 - lb) * sig
    k = (1.0 - lb) * _sigmoid(-f)
    qs = q * _sigmoid(q) * (HG_DK ** -0.5)
    return qs, k, jnp.log(forget), sig, lb, forget


def _hg_intra(qs, k, b, b_scr):
    b_scr[...] = b
    bm = b_scr[pl.ds(HG_CH // 2 - 1, 1), :]
    bl = b_scr[pl.ds(HG_CH - 1, 1), :]
    eb = jnp.exp(b)
    qb = qs * eb
    e_q = jnp.exp(b - bm)
    e_k = jnp.exp(bm - b)
    e_d = jnp.exp(bl - b)
    return qb, qs * e_q, k * e_k, k * e_d, jnp.exp(bl), eb, e_q, e_k, e_d


def _hgrn_fwd(proj, alb, ngain):
    S = proj.shape[1]
    nc = S // HG_CH
    nb = nc // HG_CPB
    rb, wb = HG_CPB * HG_CH, HG_HPB * HG_DK

    def body(pj_ref, alb_ref, ng_ref, o_ref, y_ref, st_ref, st_scr, b_scr):
        n = pl.program_id(1)

        @pl.when(n == 0)
        def _():
            st_scr[...] = jnp.zeros_like(st_scr)

        r = lax.broadcasted_iota(jnp.int32, (HG_CH, HG_CH), 0)
        c = lax.broadcasted_iota(jnp.int32, (HG_CH, HG_CH), 1)
        causal = r >= c
        for ci, j in [(ci, j) for ci in range(HG_CPB) for j in range(HG_HPB)]:
            rows, lanes = pl.ds(ci * HG_CH, HG_CH), pl.ds(j * HG_DK, HG_DK)
            q, f, v, g = pj_ref[0, rows, lanes], pj_ref[1, rows, lanes], pj_ref[2, rows, lanes], pj_ref[3, rows, lanes]
            qs, k, logf, _, _, _ = _hg_gates(q, f, alb_ref.at[:, lanes])
            b = _running_sum(logf)
            qb, qt, kt, kd, ebl, _, _, _, _ = _hg_intra(qs, k, b, b_scr.at[j, ci])
            st = st_scr[j]
            st_ref[j, ci] = st
            a = jnp.where(causal, _dot_nt(qt, kt), 0.0)
            o = _dot(a, v) + _dot_nt(qb, st)
            st_scr[j] = st * ebl + _dot_tn(v, kd)
            o_ref[rows, lanes] = o
            rinv = lax.rsqrt(jnp.mean(o * o, axis=-1, keepdims=True) + RMS_EPS)
            y_ref[rows, lanes] = (o * rinv * ng_ref[...] * (g * _sigmoid(g))).astype(CDT)

    blk = pl.BlockSpec((rb, wb), lambda h, n: (n, h))
    return _call(
        None, body, grid=(HG_H // HG_HPB, nb),
        in_specs=[pl.BlockSpec((4, rb, wb), lambda h, n: (0, n, h)), pl.BlockSpec((2, wb), lambda h, n: (0, h)),
                  pl.BlockSpec((1, HG_DK), lambda h, n: (0, 0))],
        out_specs=[blk, blk, pl.BlockSpec((HG_HPB, HG_CPB, HG_DK, HG_DK), lambda h, n: (h, n, 0, 0))],
        out_shape=[_sds((S, D), F32), _sds((S, D), CDT), _sds((HG_H, nc, HG_DK, HG_DK), F32)],
        scratch_shapes=[pltpu.VMEM((HG_HPB, HG_DK, HG_DK), F32), pltpu.VMEM((HG_HPB, HG_CPB, HG_CH, HG_DK), F32)],
        compiler_params=_params(("arbitrary", "arbitrary")), name="hgrn_fwd")(proj, alb, ngain)


def _hgrn_bwd(proj, alb, ngain, o, states, dy, after=None):
    S = proj.shape[1]
    nc = S // HG_CH
    nb = nc // HG_CPB
    rb, wb = HG_CPB * HG_CH, HG_HPB * HG_DK

    def body(pj_ref, alb_ref, ng_ref, o_ref, st_ref, dy_ref, dpj_ref, dalb_ref, dng_ref, dst_scr, b_scr):
        h, n = pl.program_id(0), pl.program_id(1)

        @pl.when(n == 0)
        def _():
            dst_scr[...] = jnp.zeros_like(dst_scr)
            dalb_ref[...] = jnp.zeros_like(dalb_ref)

        _zero_at(dng_ref, jnp.logical_and(h == 0, n == 0))
        ng = ng_ref[...]
        r = lax.broadcasted_iota(jnp.int32, (HG_CH, HG_CH), 0)
        c = lax.broadcasted_iota(jnp.int32, (HG_CH, HG_CH), 1)
        causal = r >= c
        dng = None
        for ci, j in [(ci, j) for ci in reversed(range(HG_CPB)) for j in range(HG_HPB)]:
            rows, lanes = pl.ds(ci * HG_CH, HG_CH), pl.ds(j * HG_DK, HG_DK)
            q, f, v, g = pj_ref[0, rows, lanes], pj_ref[1, rows, lanes], pj_ref[2, rows, lanes], pj_ref[3, rows, lanes]
            o_ = o_ref[rows, lanes]
            dy_ = dy_ref[rows, lanes]
            sg = _sigmoid(g)
            rinv = lax.rsqrt(jnp.mean(o_ * o_, axis=-1, keepdims=True) + RMS_EPS)
            nrm = o_ * rinv
            dr = dy_ * (g * sg)
            dg = dy_ * nrm * ng * (sg * (1.0 + g * (1.0 - sg)))
            dn = dr * ng
            do = rinv * (dn - nrm * jnp.mean(dn * nrm, axis=-1, keepdims=True))
            dng = _colsum(dr * nrm) if dng is None else dng + _colsum(dr * nrm)
            qs, k, logf, sig, lb, forget = _hg_gates(q, f, alb_ref.at[:, lanes])
            b = _running_sum(logf)
            qb, qt, kt, kd, ebl, eb, e_q, e_k, e_d = _hg_intra(qs, k, b, b_scr.at[j, ci])
            st = st_ref[j, ci]
            dstn = dst_scr[j]
            qt, kt, qb, kd = (t.astype(CDT).astype(F32) for t in (qt, kt, qb, kd))
            a = jnp.where(causal, _dot_nt(qt, kt), 0.0)
            da = jnp.where(causal, _dot_nt(do, v), 0.0)
            dv = _dot_tn(a, do) + _dot_nt(kd, dstn)
            dqb = _dot(do, st)
            dkd = _dot(v, dstn)
            dqt = _dot(da, kt)
            dkt = _dot_tn(da, qt)
            dbl = _colsum(dkd * kd) + ebl * _colsum(dstn * st)
            dst_scr[j] = dstn * ebl + _dot_tn(do, qb)
            dqs = dqt * e_q + dqb * eb
            dk = dkt * e_k + dkd * e_d
            db = dqt * qt + dqb * qb - dkt * kt - dkd * kd
            dlogf = _running_sum(db, reverse=True) + dbl
            dforget = dlogf / forget
            dsig = (1.0 - lb) * (dforget - dk)
            df = dsig * sig * (1.0 - sig)
            dlb = _colsum((dforget - dk) * (1.0 - sig))
            sq = _sigmoid(q)
            dq = dqs * (HG_DK ** -0.5) * (sq * (1.0 + q * (1.0 - sq)))
            dpj_ref[0, rows, lanes] = dq.astype(CDT)
            dpj_ref[1, rows, lanes] = df.astype(CDT)
            dpj_ref[2, rows, lanes] = dv.astype(CDT)
            dpj_ref[3, rows, lanes] = dg.astype(CDT)
            da0 = dlb * lb * (1.0 - lb)
            dalb_ref[pl.ds(0, 1), lanes] += da0
            dalb_ref[pl.ds(1, 1), lanes] -= da0
        dng_ref[...] += dng

    blk = pl.BlockSpec((rb, wb), lambda h, n: (nb - 1 - n, h))
    pj = pl.BlockSpec((4, rb, wb), lambda h, n: (0, nb - 1 - n, h))
    alb_blk = pl.BlockSpec((2, wb), lambda h, n: (0, h))
    ng_blk = pl.BlockSpec((1, HG_DK), lambda h, n: (0, 0))
    return _call(
        after, body, grid=(HG_H // HG_HPB, nb),
        in_specs=[pj, alb_blk, ng_blk, blk,
                  pl.BlockSpec((HG_HPB, HG_CPB, HG_DK, HG_DK), lambda h, n: (h, nb - 1 - n, 0, 0)), blk],
        out_specs=[pj, alb_blk, ng_blk],
        out_shape=[_sds((4, S, D), CDT), _sds((2, D), F32), _sds((1, HG_DK), F32)],
        scratch_shapes=[pltpu.VMEM((HG_HPB, HG_DK, HG_DK), F32), pltpu.VMEM((HG_HPB, HG_CPB, HG_CH, HG_DK), F32)],
        compiler_params=_params(("arbitrary", "arbitrary")), name="hgrn_bwd")(proj, alb, ngain, o, states, dy)


def _slope(h):
    return 2.0 ** (-8.0 * (h + 1) / ATT_QH)


def _attn_mask(n):
    qi = lax.broadcasted_iota(jnp.int32, (WINDOW, 2 * WINDOW), 0)
    si = lax.broadcasted_iota(jnp.int32, (WINDOW, 2 * WINDOW), 1)
    dist = qi - si + WINDOW
    valid = (dist >= 0) & (dist < WINDOW) & (n * WINDOW - WINDOW + si >= 0)
    return valid, dist.astype(F32)


def _attn_probs(qh, kh, sink, slope, valid, distf):
    s = _dot_nt(qh, kh) * (ATT_HD ** -0.5) - slope * distf
    s = jnp.where(valid, s, NEG)
    m = jnp.maximum(jnp.max(s, axis=-1, keepdims=True), sink)
    e = jnp.exp(s - m)
    es = jnp.exp(sink - m)
    inv = 1.0 / (jnp.sum(e, axis=-1, keepdims=True) + es)
    return e * inv, es * inv


def _attn_specs(S):
    steps = S // (ATT_BPB * WINDOW)
    cur = lambda H: pl.BlockSpec((H, ATT_BPB * WINDOW, ATT_HD), lambda n: (0, n, 0))
    prev = lambda H: pl.BlockSpec((H, WINDOW, ATT_HD), lambda n: (0, jnp.maximum(ATT_BPB * n - 1, 0), 0))
    return steps, cur, prev


def _attn_kv(kvc_ref, kvp_ref, head, bi):
    before = kvp_ref[head] if bi == 0 else kvc_ref[head, pl.ds((bi - 1) * WINDOW, WINDOW), :]
    return jnp.concatenate([before, kvc_ref[head, pl.ds(bi * WINDOW, WINDOW), :]], axis=0)


def _attn_fwd(q4, kv4, sinks):
    S = q4.shape[1]
    nb, cur, prev = _attn_specs(S)

    def body(sink_ref, q_ref, kvc_ref, kvp_ref, o_ref, p_ref, ps_ref):
        lane = lax.broadcasted_iota(jnp.int32, (1, 128), 1)
        for bi in range(ATT_BPB):
            valid, distf = _attn_mask(pl.program_id(0) * ATT_BPB + bi)
            rows = pl.ds(bi * WINDOW, WINDOW)
            sink_probs = jnp.zeros((WINDOW, 128), F32)
            for kvh in range(ATT_KVH):
                kh = _attn_kv(kvc_ref, kvp_ref, kvh, bi)
                vh = _attn_kv(kvc_ref, kvp_ref, ATT_KVH + kvh, bi)
                v_low = jnp.dot(vh, _half_place(True), preferred_element_type=F32).astype(CDT)
                v_high = jnp.dot(vh, _half_place(False), preferred_element_type=F32).astype(CDT)
                for h in range(kvh * ATT_G, (kvh + 1) * ATT_G, 2):
                    pair = []
                    for hh in (h, h + 1):
                        p, ps = _attn_probs(q_ref[hh, rows, :], kh, sink_ref[0, hh], _slope(hh), valid, distf)
                        p = p.astype(CDT)
                        p_ref[hh, rows, :] = p
                        sink_probs = sink_probs + jnp.where(lane == hh, ps, 0.0)
                        pair.append(p)
                    o_ref[rows, pl.ds(h * ATT_HD, 2 * ATT_HD)] = (_dot(pair[0], v_low) + _dot(pair[1], v_high)).astype(CDT)
            ps_ref[rows, :] = sink_probs

    rows_spec = pl.BlockSpec((ATT_BPB * WINDOW, D), lambda n: (n, 0))
    return _call(
        None, body, grid=(nb,),
        in_specs=[pl.BlockSpec(memory_space=pltpu.SMEM), cur(ATT_QH), cur(2 * ATT_KVH), prev(2 * ATT_KVH)],
        out_specs=[rows_spec, pl.BlockSpec((ATT_QH, ATT_BPB * WINDOW, 2 * WINDOW), lambda n: (0, n, 0)),
                   pl.BlockSpec((ATT_BPB * WINDOW, 128), lambda n: (n, 0))],
        out_shape=[_sds((S, D), CDT), _sds((ATT_QH, S, 2 * WINDOW), CDT), _sds((S, 128), F32)],
        compiler_params=_params(("arbitrary",)), name="attn_fwd")(sinks, q4, kv4, kv4)


def _attn_bwd(q4, kv4, probs, sink_probs, do):
    S = q4.shape[1]
    nb, cur, prev = _attn_specs(S)

    def body(q_ref, kvc_ref, kvp_ref, p_ref, ps_ref, do_ref, dq_ref, dkv_ref, dbq_ref, dsink_ref):
        n = pl.program_id(0)
        first = n == 0

        @pl.when(first)
        def _():
            dkv_ref[...] = jnp.zeros_like(dkv_ref)
            dsink_ref[...] = jnp.zeros_like(dsink_ref)
            dbq_ref[...] = jnp.zeros_like(dbq_ref)

        lane = lax.broadcasted_iota(jnp.int32, (1, 128), 1)
        dsinks = jnp.zeros((1, 128), F32)
        sel = (_half_select(True), _half_select(False))
        row_dots = [jnp.zeros((WINDOW, 128), F32) for _ in range(ATT_BPB)]
        for bi, kvh in [(bi, kvh) for bi in range(ATT_BPB) for kvh in range(ATT_KVH)]:
            block = n * ATT_BPB + bi
            rows = pl.ds(bi * WINDOW, WINDOW)
            rows_cur = pl.ds(pl.multiple_of(block * WINDOW, WINDOW), WINDOW)
            rows_prev = pl.ds(pl.multiple_of(jnp.maximum(block - 1, 0) * WINDOW, WINDOW), WINDOW)
            kh = _attn_kv(kvc_ref, kvp_ref, kvh, bi)
            vh = _attn_kv(kvc_ref, kvp_ref, ATT_KVH + kvh, bi)
            dk = dv = None
            dqs = []
            for h in range(kvh * ATT_G, (kvh + 1) * ATT_G):
                qh = q_ref[h, rows, :]
                doh = jnp.dot(do_ref[rows, pl.ds((h // 2) * 2 * ATT_HD, 2 * ATT_HD)], sel[h % 2],
                              preferred_element_type=F32).astype(CDT)
                p = p_ref[h, rows, :].astype(F32)
                dp = _dot_nt(doh, vh)
                dd = jnp.sum(p * dp, axis=-1, keepdims=True)
                ds = p * (dp - dd)
                row_dots[bi] = row_dots[bi] + jnp.where(lane == h, dd, 0.0)
                dqh = _dot(ds, kh) * (ATT_HD ** -0.5)
                dqs.append(dqh.astype(CDT))
                dbq_ref[h] += _colsum(dqh)
                dkh = _dot_tn(ds, qh) * (ATT_HD ** -0.5)
                dvh = _dot_tn(p, doh)
                dk = dkh if dk is None else dk + dkh
                dv = dvh if dv is None else dv + dvh
            for i in range(ATT_G // 2):
                lanes = pl.ds((kvh * ATT_G + 2 * i) * ATT_HD, 2 * ATT_HD)
                dq_ref[rows, lanes] = _pair_lanes(dqs[2 * i], dqs[2 * i + 1])
            dkv_ref[kvh, rows_prev, :] += dk[:WINDOW]
            dkv_ref[kvh, rows_cur, :] += dk[WINDOW:]
            dkv_ref[ATT_KVH + kvh, rows_prev, :] += dv[:WINDOW]
            dkv_ref[ATT_KVH + kvh, rows_cur, :] += dv[WINDOW:]
        for bi in range(ATT_BPB):
            dsinks = dsinks - _colsum(ps_ref[pl.ds(bi * WINDOW, WINDOW), :] * row_dots[bi])
        dsink_ref[...] += dsinks

    rows_spec = pl.BlockSpec((ATT_BPB * WINDOW, D), lambda n: (n, 0))
    return _call(
        None, body, grid=(nb,),
        in_specs=[cur(ATT_QH), cur(2 * ATT_KVH), prev(2 * ATT_KVH),
                  pl.BlockSpec((ATT_QH, ATT_BPB * WINDOW, 2 * WINDOW), lambda n: (0, n, 0)),
                  pl.BlockSpec((ATT_BPB * WINDOW, 128), lambda n: (n, 0)), rows_spec],
        out_specs=[rows_spec, pl.BlockSpec((2 * ATT_KVH, S, ATT_HD), lambda n: (0, 0, 0)),
                   pl.BlockSpec((ATT_QH, 1, ATT_HD), lambda n: (0, 0, 0)), pl.BlockSpec((1, 128), lambda n: (0, 0))],
        out_shape=[_sds((S, D), CDT), _sds((2 * ATT_KVH, S, ATT_HD), F32), _sds((ATT_QH, 1, ATT_HD), F32),
                   _sds((1, 128), F32)],
        compiler_params=_params(("arbitrary",)), name="attn_bwd")(q4, kv4, kv4, probs, sink_probs, do)


def _local_step(x, p, target, getw, sm, emit):
    S = x.shape[0]
    vec = lambda a: a.reshape(1, -1)
    ln_g = lambda l, k: vec(sm["ln_gain"][l, k])
    ln_b = lambda l, k: vec(sm["ln_bias"][l, k])
    xb = x.astype(CDT)
    pb = p.astype(CDT)

    proj = _inproj_fwd(xb, getw("a_w_in"), "a_in")
    o_a, y_a, states = _hgrn_fwd(proj, sm["a_lower_bound"], sm["a_norm_gain"])
    zeros = jnp.zeros((1, D), F32)
    z = [[None] * 3 for _ in range(2)]
    xs = [[None] * 3 for _ in range(2)]
    xbs = [[None] * 3 for _ in range(2)]
    z[0][0], xs[0][0], xbs[0][0] = _mixout_ln(y_a[None], getw("a_w_out")[None], zeros, x, ln_g(0, 0), ln_b(0, 0),
                                              "a_out_ln")
    gu, hid, sgs, ups = [None, None], [None, None], [None, None], [None, None]

    def ffn_ple(l):
        wgu = getw(f"gu{l}")
        gu[l], hid[l], z[l][1], xs[l][1], xbs[l][1] = _ffn_fwd(
            xs[l][0], xbs[l][0], wgu, getw(f"dn{l}"), ln_g(l, 1), ln_b(l, 1), f"ffn_fwd{l}")
        sgs[l], ups[l], z[l][2], xs[l][2], xbs[l][2] = _ple_fwd(
            xs[l][1], xbs[l][1], pb[l], getw(f"pg{l}"), vec(sm["ple_b_gate"][l]), getw(f"pu{l}"), ln_g(l, 2),
            ln_b(l, 2), f"ple_fwd{l}")

    ffn_ple(0)
    x3, x3b = xs[0][2], xbs[0][2]
    w_kv, w_q, w_bo = getw("kv_w"), getw("b_w_q"), getw("b_w_out")
    kv4 = _proj_heads(x3b, w_kv, vec(sm["kv_b"]), 2 * ATT_KVH, "kv_proj")
    q4 = _proj_heads(x3b, w_q, vec(sm["b_b_q"]), ATT_QH, "q_proj")
    o_b, probs, sink_probs = _attn_fwd(q4, kv4, sm["b_sinks"])
    z[1][0], xs[1][0], xbs[1][0] = _mixout_ln(o_b[None], w_bo[None], sm["b_b_out"], x3, ln_g(1, 0), ln_b(1, 0),
                                              "b_out_ln")
    ffn_ple(1)
    loss, dy = _loss_fwd_bwd(xs[1][2], target)

    gs = {}
    d_ln_g = [[None] * 3 for _ in range(2)]
    d_ln_b = [[None] * 3 for _ in range(2)]
    g_bg = [None, None]

    def ffn_ple_bwd(l, dy, after=None):
        dx2, dgl, dup, d_ln_g[l][2], d_ln_b[l][2], g_bg[l] = _ple_bwd(dy, z[l][2], sgs[l], ups[l], ln_g(l, 2),
                                                                     getw(f"pg{l}"), f"ple_bwd{l}", after=after)
        g_pg = _wgrad(xbs[l][1][None], dgl[None], f"g_ple_gate{l}")[0]
        g_pu = _wgrad(pb[l][None], dup[None], f"g_ple_up{l}")[0]
        tok = emit({f"pg{l}": g_pg, f"pu{l}": g_pu})
        dz2, dzb, dgu, d_ln_g[l][1], d_ln_b[l][1] = _ffn_bwd_hidden(dx2, z[l][1], gu[l], getw(f"dn{l}"), ln_g(l, 1),
                                                                   f"ffn_bwd{l}", after=tok)
        tok = emit({f"dn{l}": _wgrad(hid[l], dzb[None], f"g_ffn_down{l}")})
        g_gu = _wgrad(dgu.reshape(8, S, FFN_B), xbs[l][0][None], f"g_ffn_gate_up{l}")
        tok = tok + emit({f"gu{l}": g_gu})
        dx1 = _ffn_bwd_input(dz2, dgu, getw(f"gu{l}"), f"ffn_bwd{l}", after=tok)
        return dx1, None

    dx1, tok = ffn_ple_bwd(1, dy)
    dz, dzb, do, d_ln_g[1][0], d_ln_b[1][0], gs["b_b_out"] = _mixout_bwd(dx1, z[1][0], ln_g(1, 0), w_bo[None], CDT,
                                                                        "b_out_bwd", after=tok)
    g_bo = _wgrad(o_b[None], dzb[None], "g_b_w_out")[0]
    dq, dkv4, dbq, dsinks = _attn_bwd(q4, kv4, probs, sink_probs, do[0])
    gs["b_b_q"] = dbq
    gs["b_sinks"] = dsinks
    g_q = _wgrad(x3b[None], dq[None], "g_b_w_q")[0]
    dx3, dkv, gs["kv_b"] = _qkv_bwd(dz, dq, dkv4, w_q, w_kv, "qkv_bwd", after=tok)
    g_kv = _wgrad(x3b[None], dkv[None], "g_kv_w")[0]
    tok = emit({"b_w_out": g_bo, "b_w_q": g_q, "kv_w": g_kv})
    dx1, tok = ffn_ple_bwd(0, dx3, tok)
    w_ao = getw("a_w_out")
    dz, dzb, dyr, d_ln_g[0][0], d_ln_b[0][0], _ = _mixout_bwd(dx1, z[0][0], ln_g(0, 0), w_ao[None], F32, "a_out_bwd",
                                                              after=tok)
    g_ao = _wgrad(y_a[None], dzb[None], "g_a_w_out")[0]
    tok = emit({"a_w_out": g_ao})
    dproj, gs["a_lower_bound"], gs["a_norm_gain"] = _hgrn_bwd(proj, sm["a_lower_bound"], sm["a_norm_gain"], o_a, states,
                                                              dyr[0], after=tok)
    tk = lambda t: (None, t, D)
    g_ain = _mm_tn(xb[None], dproj, N_DEV, lambda g, k: (0, k, 0), lambda g, k: (g // 2, k, g % 2),
                   tk, lambda t: (None, t, 512), (N_DEV, D, 512), (None, D, 512), lambda g, k: (g, 0, 0), name="g_a_w_in")
    gs["ple_b_gate"] = jnp.concatenate(g_bg, axis=0)
    gs["ln_gain"] = jnp.stack([jnp.concatenate(r, axis=0) for r in d_ln_g])
    gs["ln_bias"] = jnp.stack([jnp.concatenate(r, axis=0) for r in d_ln_b])
    gs["loss"] = loss
    tok = emit({"a_w_in": g_ain}, small=gs)
    grad_x = _inproj_bwd(dz, dproj, getw("a_w_in"), "a_in_bwd", after=tok)
    return loss, grad_x, gs


def _peer(k):
    x, y, c = lax.axis_index("x"), lax.axis_index("y"), lax.axis_index("c")
    px = 1 - x if k & 4 else x
    py = 1 - y if k & 2 else y
    pc = 1 - c if k & 1 else c
    return (px, py, pc), 4 * px + 2 * py + pc


def _my_index():
    return 4 * lax.axis_index("x") + 2 * lax.axis_index("y") + lax.axis_index("c")


def _piece_copy(mode, src, land, send_sems, recv_sems, t, k, sender, receiver, peer):
    return pltpu.make_async_remote_copy(
        src_ref=src if mode == "gather" else src.at[receiver], dst_ref=land.at[sender],
        send_sem=send_sems.at[t * 7 + k - 1], recv_sem=recv_sems.at[t * 7 + k - 1], device_id=peer, device_id_type=MESH)


def _sequencer_exchange(srcs, modes, name, collective_id, after=None):
    n = len(srcs)
    land_shapes = [((N_DEV,) + a.shape) if mode == "gather" else a.shape for a, mode in zip(srcs, modes)]
    extra = [] if after is None else [after]

    def body(*refs):
        src_refs, land_refs = refs[:n], refs[n + len(extra):2 * n + len(extra)]
        send_sems, recv_sems, local_sems = refs[2 * n + len(extra):]
        barrier = pltpu.get_barrier_semaphore()
        for k in range(1, N_DEV):
            pl.semaphore_signal(barrier, inc=1, device_id=_peer(k)[0], device_id_type=MESH)
        pl.semaphore_wait(barrier, N_DEV - 1)
        me = _my_index()
        local = []
        for i in range(n):
            cp = pltpu.make_async_copy(src_refs[i] if modes[i] == "gather" else src_refs[i].at[me], land_refs[i].at[me],
                                       local_sems.at[i])
            cp.start()
            local.append(cp)
        for k in range(1, N_DEV):
            peer, pid = _peer(k)
            for t in range(n):
                _piece_copy(modes[t], src_refs[t], land_refs[t], send_sems, recv_sems, t, k, me, pid, peer).start()
        for k in range(1, N_DEV):
            peer, pid = _peer(k)
            for t in range(n):
                _piece_copy(modes[t], src_refs[t], land_refs[t], send_sems, recv_sems, t, k, pid, me, peer).wait_recv()
        for k in range(1, N_DEV):
            peer, pid = _peer(k)
            for t in range(n):
                _piece_copy(modes[t], src_refs[t], land_refs[t], send_sems, recv_sems, t, k, me, pid, peer).wait_send()
        for cp in local:
            cp.wait()

    return pl.kernel(
        body, out_type=[_sds(s, a.dtype) for s, a in zip(land_shapes, srcs)],
        mesh=plsc.ScalarSubcoreMesh(axis_name="sequencer", num_cores=1),
        scratch_types=[pltpu.SemaphoreType.DMA((7 * n,)), pltpu.SemaphoreType.DMA((7 * n,)), pltpu.SemaphoreType.DMA((n,))],
        compiler_params=pltpu.CompilerParams(collective_id=collective_id), name=name)(*srcs, *extra)


def _sequencer_gather(srcs, name, collective_id, after=None):
    n = len(srcs)
    extra = [] if after is None else [after]

    def body(*refs):
        src_refs, land_refs = refs[:n], refs[n + len(extra):2 * n + len(extra)]
        send_sems, recv_sems, local_sems = refs[2 * n + len(extra):]
        x, y, c = lax.axis_index("x"), lax.axis_index("y"), lax.axis_index("c")
        sibling = (x, y, 1 - c)
        chips = [(1 - x, y), (x, 1 - y), (1 - x, 1 - y)]
        index = lambda px, py, pc: 4 * px + 2 * py + pc
        barrier = pltpu.get_barrier_semaphore()
        for peer in [sibling] + [(*chip, c) for chip in chips]:
            pl.semaphore_signal(barrier, inc=1, device_id=peer, device_id_type=MESH)
        pl.semaphore_wait(barrier, 4)

        def copy(t, k, slot, to, src=None):
            return pltpu.make_async_remote_copy(
                src_ref=land_refs[t].at[slot] if src is None else src, dst_ref=land_refs[t].at[slot],
                send_sem=send_sems.at[7 * t + k], recv_sem=recv_sems.at[7 * t + k], device_id=to, device_id_type=MESH)

        me = index(x, y, c)
        local = []
        for t in range(n):
            cp = pltpu.make_async_copy(src_refs[t], land_refs[t].at[me], local_sems.at[t])
            cp.start()
            local.append(cp)
        sends = []
        for t in range(n):
            sends.append(copy(t, 0, me, sibling, src=src_refs[t]))
            sends += [copy(t, 1 + j, me, (*chip, c), src=src_refs[t]) for j, chip in enumerate(chips)]
        for cp in sends:
            cp.start()
        for j, chip in enumerate(chips):
            for t in range(n):
                copy(t, 1 + j, index(*chip, c), sibling, src=src_refs[t]).wait_recv()
                passed = copy(t, 4 + j, index(*chip, c), sibling)
                passed.start()
                sends.append(passed)
        for t in range(n):
            copy(t, 0, index(x, y, 1 - c), sibling, src=src_refs[t]).wait_recv()
        for j, chip in enumerate(chips):
            for t in range(n):
                copy(t, 4 + j, index(*chip, 1 - c), sibling, src=src_refs[t]).wait_recv()
        for cp in sends:
            cp.wait_send()
        for cp in local:
            cp.wait()

    return pl.kernel(
        body, out_type=[_sds((N_DEV,) + a.shape, a.dtype) for a in srcs],
        mesh=plsc.ScalarSubcoreMesh(axis_name="sequencer", num_cores=1),
        scratch_types=[pltpu.SemaphoreType.DMA((7 * n,)), pltpu.SemaphoreType.DMA((7 * n,)), pltpu.SemaphoreType.DMA((n,))],
        compiler_params=pltpu.CompilerParams(collective_id=collective_id), name=name)(*srcs, *extra)


def _adamw(w, g, m, v):
    m = ADAM_B1 * m + (1.0 - ADAM_B1) * g
    v = ADAM_B2 * v + (1.0 - ADAM_B2) * (g * g)
    m_hat = m / (1.0 - ADAM_B1 ** ADAM_STEP)
    v_hat = v / (1.0 - ADAM_B2 ** ADAM_STEP)
    delta = -ADAM_LR * (m_hat / (jnp.sqrt(v_hat) + ADAM_EPS) + ADAM_WD * w)
    return delta, m, v


def _adam_big(w, parts, m, v, name, after=None):
    L, R, C = w.shape
    P = parts[0].shape[0]
    tr = _tile(R, (256, 128, 176, 64, 32, 16))
    nr = R // tr

    def body(w_ref, *refs):
        p_refs, (m_ref, v_ref, g_ref, d_ref, mo_ref, vo_ref) = refs[:L], refs[L:]
        for l in range(L):
            @pl.when(pl.program_id(0) == l)
            def _(p_ref=p_refs[l]):
                g = p_ref[0].astype(F32)
                for s in range(1, P):
                    g = g + p_ref[s].astype(F32)
                g_ref[...] = g
                d_ref[...], mo_ref[...], vo_ref[...] = _adamw(w_ref[...], g, m_ref[...], v_ref[...])

    row = pl.BlockSpec((None, tr, C), lambda l, i: (l, i, 0))
    park = lambda l_of: (lambda l, i: (0, jnp.where(l == l_of, i, 0 if l_of else nr - 1), 0))
    return _call(
        after, body, grid=(L, nr),
        in_specs=[row] + [pl.BlockSpec((P, tr, C), park(l)) for l in range(L)] + [row, row],
        out_specs=[row] * 4, out_shape=[_sds((L, R, C), F32)] * 4,
        compiler_params=_params(("arbitrary", "arbitrary")), name=name)(w, *parts, m, v)


SMALL = (("a_lower_bound", (2, 128), (2, D)), ("ln_gain", (3, 2, 128), (6, D)), ("ln_bias", (3, 2, 128), (6, D)),
         ("a_norm_gain", (1, 128), (1, 128)), ("kv_b", (1, 512), (1, 512)), ("b_b_q", (1, D), (1, D)),
         ("b_sinks", (1, ATT_QH), (1, 128)), ("b_b_out", (1, D), (1, D)), ("ple_b_gate", (2, D), (2, D)))


def _adam_small(parts, w, m, v, losses, after=None):
    k = len(SMALL)

    def body(*refs):
        p_refs, w_refs, m_refs, v_refs = refs[:k], refs[k:2 * k], refs[2 * k:3 * k], refs[3 * k:4 * k]
        loss_ref, outs, total_ref = refs[4 * k], refs[4 * k + 1:-1], refs[-1]
        total = loss_ref[0]
        for s in range(1, N_DEV):
            total = total + loss_ref[s]
        total_ref[...] = total
        me = _my_index()
        for i, (_, wshape, pshape) in enumerate(SMALL):
            cols = wshape[-1]
            lanes = slice(None) if cols == pshape[1] else (
                pl.ds(0, cols) if cols < 128 else pl.ds(pl.multiple_of(me * cols, cols), cols))
            at = [(slice(None), slice(None))] if len(wshape) == 2 else [
                (pl.ds(l * wshape[0] + kk, 1), (kk, pl.ds(l, 1), slice(None))) for kk in range(wshape[0]) for l in range(wshape[1])]
            for rows, own in at:
                g = p_refs[i][0, rows, lanes]
                for s in range(1, N_DEV):
                    g = g + p_refs[i][s, rows, lanes]
                g_ref, d_ref, mo_ref, vo_ref = outs[4 * i:4 * i + 4]
                g_ref[own] = g
                d_ref[own], mo_ref[own], vo_ref[own] = _adamw(w_refs[i][own], g, m_refs[i][own], v_refs[i][own])

    full = lambda shape: pl.BlockSpec(shape, lambda: (0,) * len(shape))
    names = [n for n, _, _ in SMALL]
    held = lambda a, ws: a.swapaxes(0, 1) if len(ws) == 3 else a.reshape(ws)
    back = lambda r, n: r.swapaxes(0, 1) if r.ndim == 3 else r.reshape(w[n].shape)
    res = _call(
        after, body,
        in_specs=[full((N_DEV,) + ps) for _, _, ps in SMALL] + [full(ws) for _, ws, _ in SMALL] * 3
        + [full((N_DEV, 1, 128))],
        out_specs=[full(ws) for _, ws, _ in SMALL for _ in range(4)] + [full((1, 128))],
        out_shape=[_sds(ws, F32) for _, ws, _ in SMALL for _ in range(4)] + [_sds((1, 128), F32)], name="adam_small")(
            *[parts[n] for n in names], *[held(a[n], ws) for a in (w, m, v) for n, ws, _ in SMALL], losses)
    return {n: [back(r, n) for r in res[4 * i:4 * i + 4]] for i, n in enumerate(names)}, res[-1][0, 0]


WEIGHTS = ("a_w_in", "a_lower_bound", "a_norm_gain", "a_w_out", "kv_w", "kv_b", "b_w_q", "b_b_q", "b_sinks", "b_w_out",
           "b_b_out", "ffn_w_gate_up", "ffn_w_down", "ple_w_up", "ple_w_gate", "ple_b_gate", "ln_gain", "ln_bias")


GATHER_GROUPS = (("a_w_in",), ("a_w_out", "gu0"), ("dn0", "pu0", "pg0"), ("kv_w", "b_w_q", "b_w_out"), ("gu1",),
                 ("dn1", "pu1", "pg1"))
KERNEL_LAYOUT = {
    "a_w_in": lambda a: a,
    "a_w_out": lambda a: a.reshape(D, D),
    "kv_w": lambda a: a.reshape(D, 2 * ATT_KVH * ATT_HD),
    "b_w_q": lambda a: a.reshape(D, D),
    "b_w_out": lambda a: a.reshape(D, D),
    "gu": lambda a: a.reshape(2, 4, FFN_B, D),
    "dn": lambda a: a.reshape(4, FFN_B, D),
    "pu": lambda a: a,
    "pg": lambda a: a.reshape(D, D),
}
_row_blocks = lambda a: a.reshape(N_DEV, -1, a.shape[-1])
OWNER_BLOCKS = {
    "a_w_in": lambda g: g,
    "a_w_out": _row_blocks,
    "kv_w": _row_blocks,
    "b_w_q": _row_blocks,
    "b_w_out": _row_blocks,
    "gu": lambda g: g,
    "dn": lambda g: _row_blocks(g.reshape(FFN_H, D)),
    "pu": lambda g: g.reshape(PLE_DIM, N_DEV, 128).transpose(1, 0, 2),
    "pg": _row_blocks,
}
ADAM_PARTS = (("kv_w", ("kv_w",)), ("b_w_q", ("b_w_q",)), ("b_w_out", ("b_w_out",)), ("ffn_w_gate_up", ("gu0", "gu1")),
              ("ffn_w_down", ("dn0", "dn1")), ("ple_w_up", ("pu0", "pu1")), ("ple_w_gate", ("pg0", "pg1")),
              ("a_w_out", ("a_w_out",)), ("a_w_in", ("a_w_in",)))


def kernel(x, p, a_w_in, a_lower_bound, a_norm_gain, a_w_out, kv_w, kv_b, b_w_q, b_b_q, b_sinks, b_w_out, b_b_out, ffn_w_gate_up, ffn_w_down, ple_w_up, ple_w_gate, ple_b_gate, ln_gain, ln_bias, loss_target, m_a_w_in, m_a_lower_bound, m_a_norm_gain, m_a_w_out, m_kv_w, m_kv_b, m_b_w_q, m_b_b_q, m_b_sinks, m_b_w_out, m_b_b_out, m_ffn_w_gate_up, m_ffn_w_down, m_ple_w_up, m_ple_w_gate, m_ple_b_gate, m_ln_gain, m_ln_bias, v_a_w_in, v_a_lower_bound, v_a_norm_gain, v_a_w_out, v_kv_w, v_kv_b, v_b_w_q, v_b_b_q, v_b_sinks, v_b_w_out, v_b_b_out, v_ffn_w_gate_up, v_ffn_w_down, v_ple_w_up, v_ple_w_gate, v_ple_b_gate, v_ln_gain, v_ln_bias):
    given = dict(locals())
    w = {n: given[n] for n in WEIGHTS}
    m = {n: given["m_" + n] for n in WEIGHTS}
    v = {n: given["v_" + n] for n in WEIGHTS}
    shards = {"a_w_in": a_w_in[0], "a_w_out": a_w_out[0], "kv_w": kv_w, "b_w_q": b_w_q[0], "b_w_out": b_w_out[0]}
    for l in range(2):
        shards.update({f"gu{l}": ffn_w_gate_up[l].T, f"dn{l}": ffn_w_down[l], f"pu{l}": ple_w_up[l], f"pg{l}": ple_w_gate[l]})
    sharded_small = [a_lower_bound, ln_gain.swapaxes(0, 1), ln_bias.swapaxes(0, 1)]
    gathered = {}
    for gi, g in enumerate(GATHER_GROUPS):
        lands = _sequencer_gather([shards[n].astype(CDT) for n in g] + (sharded_small if gi == 0 else []),
                                  f"gather{gi}", gi)
        for n, a in zip(g, lands):
            gathered[n] = KERNEL_LAYOUT[n.rstrip("01")](a)
        if gi == 0:
            alb = lands[len(g)].transpose(1, 0, 2).reshape(2, D)
            lng, lnb = [a.transpose(2, 1, 0, 3).reshape(2, 3, D) for a in lands[len(g) + 1:]]

    getw = gathered.__getitem__

    sm = {"a_lower_bound": alb, "ln_gain": lng, "ln_bias": lnb,
          "a_norm_gain": a_norm_gain, "kv_b": kv_b, "b_b_q": b_b_q[0], "b_sinks": b_sinks, "b_b_out": b_b_out,
          "ple_b_gate": ple_b_gate}

    scatters, small_parts = [], {}

    def emit(grads, small=None):
        names = list(grads)
        blocks = [OWNER_BLOCKS[n.rstrip("01")](grads[n]) for n in names]
        partials = [] if small is None else [small[n].reshape(ps) for n, _, ps in SMALL] + [small["loss"]]
        lands = _sequencer_exchange(blocks + partials, ["scatter"] * len(blocks) + ["gather"] * len(partials),
                                    f"scatter{len(scatters)}", len(GATHER_GROUPS) + len(scatters))
        scatters.append(dict(zip(names, lands)))
        small_parts.update(zip([n for n, _, _ in SMALL] + ["loss"], lands[len(blocks):]))
        return blocks + (list(scatters[-4].values()) if len(scatters) >= 4 else [])

    loss, grad_x, gs = _local_step(x[0], p[:, 0], loss_target[0], getw, sm, emit)

    out, parts, last = {}, {}, [grad_x]
    for landed in scatters:
        parts.update(landed)
        for n, keys in ADAM_PARTS:
            if n in out or not all(key in parts for key in keys):
                continue
            lrc = (1,) * (3 - w[n].ndim) + w[n].shape
            shard = (lambda a: a.reshape(lrc).swapaxes(1, 2)) if n == "ffn_w_gate_up" else (lambda a: a.reshape(lrc))
            res = _adam_big(shard(w[n]), [parts[key] for key in keys], shard(m[n]), shard(v[n]), "adam_" + n, after=last)
            out[n] = [(r.swapaxes(1, 2) if n == "ffn_w_gate_up" else r).reshape(w[n].shape) for r in res]
            last = [res[3]]
    small_out, loss = _adam_small(small_parts, w, m, v, small_parts["loss"], after=last)
    out.update(small_out)
    res = [loss, grad_x[None]]
    for i in range(4):
        res += [out[n][i] for n in WEIGHTS]
    return tuple(res)
```

```python
import jax
import jax.numpy as jnp
from jax import lax
from jax.experimental import pallas as pl
from jax.experimental.pallas import tpu as pltpu
from jax.experimental.pallas import tpu_sc as plsc

F32 = jnp.float32
CDT = jnp.bfloat16

N_DEV = 8
D = 1024
HG_H, HG_DK, HG_CH = 8, 128, 64
HG_HPB = 4
HG_CPB = 8
ATT_HD, ATT_QH, ATT_KVH, ATT_G, WINDOW = 64, 16, 4, 4, 128
ATT_BPB = 1
FFN_H = 2816
FFN_B = FFN_H // 4
PLE_DIM = 256
ALPHA = (2.0 * 2) ** 0.25
LN_EPS = 1e-5
RMS_EPS = 1e-6
ADAM_LR, ADAM_B1, ADAM_B2, ADAM_EPS, ADAM_WD, ADAM_STEP = 0.001, 0.9, 0.999, 1e-08, 0.01, 10
ROW_TILES = (512, 256, 128, 64)
VMEM_LIMIT = 48 * 1024 * 1024
NEG = -1e30

MESH = pl.DeviceIdType.MESH


def _tile(n, cands=ROW_TILES):
    for t in cands:
        if n % t == 0:
            return t
    return n


def _sds(shape, dtype):
    return jax.ShapeDtypeStruct(tuple(shape), dtype)


def _params(sem):
    return pltpu.CompilerParams(dimension_semantics=sem, vmem_limit_bytes=VMEM_LIMIT)


def _dot(a, b):
    return jnp.dot(a.astype(CDT), b.astype(CDT), preferred_element_type=F32)


def _dot_nt(a, b):
    return lax.dot_general(a.astype(CDT), b.astype(CDT), (((1,), (1,)), ((), ())), preferred_element_type=F32)


def _dot_tn(a, b):
    return lax.dot_general(a.astype(CDT), b.astype(CDT), (((0,), (0,)), ((), ())), preferred_element_type=F32)


def _sigmoid(x):
    return jax.nn.sigmoid(x)


def _ln_fwd(z, g, b):
    mu = jnp.mean(z, axis=-1, keepdims=True)
    zc = z - mu
    var = jnp.mean(zc * zc, axis=-1, keepdims=True)
    return zc * lax.rsqrt(var + LN_EPS) * g + b


def _ln_bwd(z, g, dy):
    mu = jnp.mean(z, axis=-1, keepdims=True)
    zc = z - mu
    var = jnp.mean(zc * zc, axis=-1, keepdims=True)
    rstd = lax.rsqrt(var + LN_EPS)
    xhat = zc * rstd
    dxh = dy * g
    dz = rstd * (dxh - jnp.mean(dxh, axis=-1, keepdims=True) - xhat * jnp.mean(dxh * xhat, axis=-1, keepdims=True))
    return dz, xhat


def _colsum(x):
    return jnp.sum(x, axis=0, keepdims=True)


def _acc(ref, val, first):
    @pl.when(first)
    def _():
        ref[...] = val

    @pl.when(jnp.logical_not(first))
    def _():
        ref[...] += val


def _zero_at(ref, first):
    @pl.when(first)
    def _():
        ref[...] = jnp.zeros_like(ref)


def _call(after, body, **kw):
    after = [] if after is None else list(after)
    specs = list(kw["in_specs"])
    kw["in_specs"] = [pl.BlockSpec(memory_space=pl.ANY)] * len(after) + specs

    def ordered_body(*refs):
        body(*refs[len(after):])

    call = pl.pallas_call(ordered_body, **kw)
    return lambda *args: call(*after, *args)


def _mm_tn(a3, b3, G, amap, bmap, ablock, bblock, out_shape, oblock, omap, name="mm_tn"):
    S = a3.shape[1]

    def body(a_ref, b_ref, o_ref):
        o_ref[...] = _dot_tn(a_ref[...], b_ref[...]).astype(o_ref.dtype)

    return _call(
        None, body, grid=(G, 1),
        in_specs=[pl.BlockSpec(ablock(S), amap), pl.BlockSpec(bblock(S), bmap)],
        out_specs=pl.BlockSpec(oblock, omap), out_shape=_sds(out_shape, CDT),
        compiler_params=_params(("arbitrary", "arbitrary")), name=name)(a3, b3)


def _wgrad(a3, b3, name):
    Ga, S, M = a3.shape
    Gb, _, N = b3.shape
    G = max(Ga, Gb)
    return _mm_tn(
        a3, b3, G,
        (lambda g, k: (g, k, 0)) if Ga > 1 else (lambda g, k: (0, k, 0)),
        (lambda g, k: (g, k, 0)) if Gb > 1 else (lambda g, k: (0, k, 0)),
        lambda tk: (None, tk, M), lambda tk: (None, tk, N),
        (G, M, N), (None, M, N), lambda g, k: (g, 0, 0), name=name)


def _mixout_ln(u3, w3, bias, xin, gain, beta, name):
    G, S, Kb = u3.shape
    tm = _tile(S)

    def body(u_ref, w_ref, b_ref, x_ref, g_ref, be_ref, z_ref, xo_ref, xob_ref):
        h = b_ref[...] + _dot(u_ref[0], w_ref[0])
        for g in range(1, G):
            h = h + _dot(u_ref[g], w_ref[g])
        z = ALPHA * x_ref[...] + h
        z_ref[...] = z
        y = _ln_fwd(z, g_ref[...], be_ref[...])
        xo_ref[...] = y
        xob_ref[...] = y.astype(CDT)

    row = pl.BlockSpec((tm, D), lambda i: (i, 0))
    vec = pl.BlockSpec((1, D), lambda i: (0, 0))
    return _call(
        None, body, grid=(S // tm,),
        in_specs=[pl.BlockSpec((G, tm, Kb), lambda i: (0, i, 0)), pl.BlockSpec((G, Kb, D), lambda i: (0, 0, 0)),
                  vec, row, vec, vec],
        out_specs=[row, row, row], out_shape=[_sds((S, D), F32), _sds((S, D), F32), _sds((S, D), CDT)],
        compiler_params=_params(("arbitrary",)), name=name)(u3, w3, bias, xin, gain, beta)


def _ffn_fwd(xin, xin_b, wgu, wdn, gain, beta, name):
    S = xin.shape[0]
    tm = _tile(S)

    def hidden(xb_ref, wgu_ref, gu_ref, hid_ref):
        xb = xb_ref[...]
        gate = _dot_nt(xb, wgu_ref[0])
        up = _dot_nt(xb, wgu_ref[1])
        gu_ref[0] = gate.astype(CDT)
        gu_ref[1] = up.astype(CDT)
        hid_ref[...] = (gate * _sigmoid(gate) * up).astype(CDT)

    gu, hid = _call(
        None, hidden, grid=(4, S // tm),
        in_specs=[pl.BlockSpec((tm, D), lambda j, i: (i, 0)), pl.BlockSpec((2, None, FFN_B, D), lambda j, i: (0, j, 0, 0))],
        out_specs=[pl.BlockSpec((2, None, tm, FFN_B), lambda j, i: (0, j, i, 0)),
                   pl.BlockSpec((None, tm, FFN_B), lambda j, i: (j, i, 0))],
        out_shape=[_sds((2, 4, S, FFN_B), CDT), _sds((4, S, FFN_B), CDT)],
        compiler_params=_params(("arbitrary", "arbitrary")), name=name + "_hidden")(xin_b, wgu)

    def down(x_ref, hid_ref, wdn_ref, g_ref, be_ref, z_ref, xo_ref, xob_ref):
        z = ALPHA * x_ref[...]
        for j in range(4):
            z = z + _dot(hid_ref[j], wdn_ref[j])
        z_ref[...] = z
        y = _ln_fwd(z, g_ref[...], be_ref[...])
        xo_ref[...] = y
        xob_ref[...] = y.astype(CDT)

    row = pl.BlockSpec((tm, D), lambda i: (i, 0))
    vec = pl.BlockSpec((1, D), lambda i: (0, 0))
    z, xo, xob = _call(
        None, down, grid=(S // tm,),
        in_specs=[row, pl.BlockSpec((4, tm, FFN_B), lambda i: (0, i, 0)), pl.BlockSpec((4, FFN_B, D), lambda i: (0, 0, 0)),
                  vec, vec],
        out_specs=[row, row, row], out_shape=[_sds((S, D), F32), _sds((S, D), F32), _sds((S, D), CDT)],
        compiler_params=_params(("arbitrary",)), name=name + "_down")(xin, hid, wdn, gain, beta)
    return gu, hid, z, xo, xob


def _ple_fwd(xin, xin_b, p_b, wpg, bgate, wpu, gain, beta, name):
    S = xin.shape[0]
    tm = _tile(S)

    def body(x_ref, xb_ref, p_ref, wpg_ref, bg_ref, wpu_ref, g_ref, be_ref, sg_ref, up_ref, z_ref, xo_ref, xob_ref):
        sg = _sigmoid(_dot(xb_ref[...], wpg_ref[...]) + bg_ref[...])
        pb = p_ref[...]
        up = jnp.concatenate([_dot(pb, wpu_ref[j]) for j in range(N_DEV)], axis=-1)
        sg_ref[...] = sg.astype(CDT)
        up_ref[...] = (up * sg * (1.0 - sg)).astype(CDT)
        z = ALPHA * x_ref[...] + sg * up
        z_ref[...] = z
        y = _ln_fwd(z, g_ref[...], be_ref[...])
        xo_ref[...] = y
        xob_ref[...] = y.astype(CDT)

    row = pl.BlockSpec((tm, D), lambda i: (i, 0))
    vec = pl.BlockSpec((1, D), lambda i: (0, 0))
    return _call(
        None, body, grid=(S // tm,),
        in_specs=[row, row, pl.BlockSpec((tm, PLE_DIM), lambda i: (i, 0)), pl.BlockSpec((D, D), lambda i: (0, 0)), vec,
                  pl.BlockSpec((N_DEV, PLE_DIM, D // N_DEV), lambda i: (0, 0, 0)), vec, vec],
        out_specs=[row] * 5,
        out_shape=[_sds((S, D), CDT)] * 2 + [_sds((S, D), F32)] * 2 + [_sds((S, D), CDT)],
        compiler_params=_params(("arbitrary",)), name=name)(xin, xin_b, p_b, wpg, bgate, wpu, gain, beta)


def _loss_fwd_bwd(y, target):
    S = y.shape[0]
    tm = _tile(S)

    def body(y_ref, t_ref, l_ref, dy_ref):
        e = y_ref[...] - t_ref[...]
        dy_ref[...] = e * (1.0 / D)
        part = 0.5 * jnp.sum(jnp.sum(e * e, axis=-1, keepdims=True) * (1.0 / D), axis=0, keepdims=True)
        _acc(l_ref, jnp.broadcast_to(part, l_ref.shape), pl.program_id(0) == 0)

    row = pl.BlockSpec((tm, D), lambda i: (i, 0))
    return _call(
        None, body, grid=(S // tm,), in_specs=[row, row],
        out_specs=[pl.BlockSpec((1, 128), lambda i: (0, 0)), row],
        out_shape=[_sds((1, 128), F32), _sds((S, D), F32)],
        compiler_params=_params(("arbitrary",)), name="loss")(y, target)


def _ple_bwd(dy, z, sg, up, gain, wpg, name, after=None):
    S = dy.shape[0]
    tm = _tile(S)

    def body(dy_ref, z_ref, sg_ref, up_ref, g_ref, wpg_ref, dx_ref, dgl_ref, dup_ref, dgain_ref, dbeta_ref, dbg_ref):
        first = pl.program_id(0) == 0
        dy_ = dy_ref[...]
        dz, xhat = _ln_bwd(z_ref[...], g_ref[...], dy_)
        dgl = dz * up_ref[...].astype(F32)
        dgl_ref[...] = dgl.astype(CDT)
        dup_ref[...] = (dz * sg_ref[...].astype(F32)).astype(CDT)
        dx_ref[...] = ALPHA * dz + _dot_nt(dgl, wpg_ref[...])
        _acc(dgain_ref, _colsum(dy_ * xhat), first)
        _acc(dbeta_ref, _colsum(dy_), first)
        _acc(dbg_ref, _colsum(dgl), first)

    row = pl.BlockSpec((tm, D), lambda i: (i, 0))
    vec = pl.BlockSpec((1, D), lambda i: (0, 0))
    return _call(
        after, body, grid=(S // tm,), in_specs=[row, row, row, row, vec, pl.BlockSpec((D, D), lambda i: (0, 0))],
        out_specs=[row, row, row, vec, vec, vec],
        out_shape=[_sds((S, D), F32), _sds((S, D), CDT), _sds((S, D), CDT)] + [_sds((1, D), F32)] * 3,
        compiler_params=_params(("arbitrary",)), name=name)(dy, z, sg, up, gain, wpg)


def _ffn_bwd_hidden(dy, z, gu, wdn, gain, name, after=None):
    S = dy.shape[0]
    tm = _tile(S)

    def hidden(dy_ref, z_ref, gu_ref, wdn_ref, g_ref, dz_ref, dzb_ref, dgu_ref, dgain_ref, dbeta_ref):
        i, j = pl.program_id(0), pl.program_id(1)

        @pl.when(j == 0)
        def _():
            dy_ = dy_ref[...]
            dz, xhat = _ln_bwd(z_ref[...], g_ref[...], dy_)
            dz_ref[...] = dz
            dzb_ref[...] = dz.astype(CDT)
            _acc(dgain_ref, _colsum(dy_ * xhat), i == 0)
            _acc(dbeta_ref, _colsum(dy_), i == 0)

        dhid = _dot_nt(dzb_ref[...], wdn_ref[...])
        gate, up = gu_ref[0].astype(F32), gu_ref[1].astype(F32)
        sg = _sigmoid(gate)
        dgu_ref[0] = (dhid * up * (sg * (1.0 + gate * (1.0 - sg)))).astype(CDT)
        dgu_ref[1] = (dhid * (gate * sg)).astype(CDT)

    row = pl.BlockSpec((tm, D), lambda i, j: (i, 0))
    vec = pl.BlockSpec((1, D), lambda i, j: (0, 0))
    return _call(
        after, hidden, grid=(S // tm, 4),
        in_specs=[row, row, pl.BlockSpec((2, None, tm, FFN_B), lambda i, j: (0, j, i, 0)),
                  pl.BlockSpec((None, FFN_B, D), lambda i, j: (j, 0, 0)), vec],
        out_specs=[row, row, pl.BlockSpec((2, None, tm, FFN_B), lambda i, j: (0, j, i, 0)), vec, vec],
        out_shape=[_sds((S, D), F32), _sds((S, D), CDT), _sds((2, 4, S, FFN_B), CDT), _sds((1, D), F32),
                   _sds((1, D), F32)],
        compiler_params=_params(("arbitrary", "arbitrary")), name=name + "_hidden")(dy, z, gu, wdn, gain)


def _ffn_bwd_input(dz, dgu, wgu, name, after=None):
    S = dz.shape[0]
    tm = _tile(S)

    def to_input(dz_ref, dgu_ref, wgu_ref, dx_ref):
        acc = ALPHA * dz_ref[...]
        for g in range(2):
            for j in range(4):
                acc = acc + _dot(dgu_ref[g, j], wgu_ref[g, j])
        dx_ref[...] = acc

    rows = pl.BlockSpec((tm, D), lambda i: (i, 0))
    return _call(
        after, to_input, grid=(S // tm,),
        in_specs=[rows, pl.BlockSpec((2, 4, tm, FFN_B), lambda i: (0, 0, i, 0)),
                  pl.BlockSpec((2, 4, FFN_B, D), lambda i: (0, 0, 0, 0))],
        out_specs=rows, out_shape=_sds((S, D), F32),
        compiler_params=_params(("arbitrary",)), name=name + "_input")(dz, dgu, wgu)


def _mixout_bwd(dy, z, gain, w3, du_dtype, name, after=None):
    S = dy.shape[0]
    G, Kb, _ = w3.shape
    tm = _tile(S)

    def body(dy_ref, z_ref, g_ref, w_ref, dz_ref, dzb_ref, du_ref, dgain_ref, dbeta_ref, dbias_ref):
        first = pl.program_id(0) == 0
        dy_ = dy_ref[...]
        dz, xhat = _ln_bwd(z_ref[...], g_ref[...], dy_)
        dz_ref[...] = dz
        dzb = dz.astype(CDT)
        dzb_ref[...] = dzb
        for g in range(G):
            du_ref[g] = _dot_nt(dzb, w_ref[g]).astype(du_ref.dtype)
        _acc(dgain_ref, _colsum(dy_ * xhat), first)
        _acc(dbeta_ref, _colsum(dy_), first)
        _acc(dbias_ref, _colsum(dz), first)

    row = pl.BlockSpec((tm, D), lambda i: (i, 0))
    vec = pl.BlockSpec((1, D), lambda i: (0, 0))
    return _call(
        after, body, grid=(S // tm,), in_specs=[row, row, vec, pl.BlockSpec((G, Kb, D), lambda i: (0, 0, 0))],
        out_specs=[row, row, pl.BlockSpec((G, tm, Kb), lambda i: (0, i, 0)), vec, vec, vec],
        out_shape=[_sds((S, D), F32), _sds((S, D), CDT), _sds((G, S, Kb), du_dtype)] + [_sds((1, D), F32)] * 3,
        compiler_params=_params(("arbitrary",)), name=name)(dy, z, gain, w3)


def _half_select(low):
    r = lax.broadcasted_iota(jnp.int32, (2 * ATT_HD, ATT_HD), 0)
    c = lax.broadcasted_iota(jnp.int32, (2 * ATT_HD, ATT_HD), 1)
    return (r == c + (0 if low else ATT_HD)).astype(CDT)


def _half_place(low):
    r = lax.broadcasted_iota(jnp.int32, (ATT_HD, 2 * ATT_HD), 0)
    c = lax.broadcasted_iota(jnp.int32, (ATT_HD, 2 * ATT_HD), 1)
    return (c == r + (0 if low else ATT_HD)).astype(CDT)


def _pair_lanes(even, odd):
    return (jnp.dot(even, _half_place(True), preferred_element_type=F32)
            + jnp.dot(odd, _half_place(False), preferred_element_type=F32)).astype(CDT)


def _proj_heads(a, w, bias, heads, name):
    S, K = a.shape
    N = heads * ATT_HD
    tm = _tile(S)

    def body(a_ref, w_ref, b_ref, o_ref):
        acc = (_dot(a_ref[...], w_ref[...]) + b_ref[...]).astype(CDT)
        sel = (_half_select(True), _half_select(False))
        for h in range(heads):
            pair = acc[:, (h // 2) * 2 * ATT_HD:(h // 2 + 1) * 2 * ATT_HD]
            o_ref[h] = jnp.dot(pair, sel[h % 2], preferred_element_type=F32).astype(CDT)

    return _call(
        None, body, grid=(S // tm,),
        in_specs=[pl.BlockSpec((tm, K), lambda i: (i, 0)), pl.BlockSpec((K, N), lambda i: (0, 0)),
                  pl.BlockSpec((1, N), lambda i: (0, 0))],
        out_specs=pl.BlockSpec((heads, tm, ATT_HD), lambda i: (0, i, 0)), out_shape=_sds((heads, S, ATT_HD), CDT),
        compiler_params=_params(("arbitrary",)), name=name)(a, w, bias)


def _qkv_bwd(dz, dq, dkv4, wq, wkv, name, after=None):
    S = dz.shape[0]
    tm = _tile(S)
    HK = dkv4.shape[0]
    NK = HK * ATT_HD

    def body(dz_ref, dq_ref, dkv_ref, wq_ref, wkv_ref, dx_ref, dkvn_ref, dkvb_ref):
        first = pl.program_id(0) == 0
        dkvn = jnp.concatenate([_pair_lanes(dkv_ref[2 * i].astype(CDT), dkv_ref[2 * i + 1].astype(CDT))
                                for i in range(HK // 2)], axis=-1)
        dkvn_ref[...] = dkvn
        dx_ref[...] = ALPHA * dz_ref[...] + _dot_nt(dq_ref[...], wq_ref[...]) + _dot_nt(dkvn, wkv_ref[...])
        for h in range(HK):
            _acc(dkvb_ref.at[h], _colsum(dkv_ref[h]), first)

    row = pl.BlockSpec((tm, D), lambda i: (i, 0))
    return _call(
        after, body, grid=(S // tm,),
        in_specs=[row, row, pl.BlockSpec((HK, tm, ATT_HD), lambda i: (0, i, 0)),
                  pl.BlockSpec((D, D), lambda i: (0, 0)), pl.BlockSpec((D, NK), lambda i: (0, 0))],
        out_specs=[row, pl.BlockSpec((tm, NK), lambda i: (i, 0)), pl.BlockSpec((HK, 1, ATT_HD), lambda i: (0, 0, 0))],
        out_shape=[_sds((S, D), F32), _sds((S, NK), CDT), _sds((HK, 1, ATT_HD), F32)],
        compiler_params=_params(("arbitrary",)), name=name)(dz, dq, dkv4, wq, wkv)


def _inproj_fwd(xb, wain, name):
    S = xb.shape[0]
    tm = _tile(S) // 2
    nb = wain.shape[-1]

    def body(x_ref, w_ref, o_ref):
        x = x_ref[...]
        for j in range(N_DEV):
            o_ref[j // 2, :, pl.ds((j % 2) * nb, nb)] = _dot(x, w_ref[j])

    return _call(
        None, body, grid=(S // tm,),
        in_specs=[pl.BlockSpec((tm, D), lambda i: (i, 0)), pl.BlockSpec((N_DEV, D, nb), lambda i: (0, 0, 0))],
        out_specs=pl.BlockSpec((4, tm, D), lambda i: (0, i, 0)), out_shape=_sds((4, S, D), F32),
        compiler_params=_params(("arbitrary",)), name=name)(xb, wain)


def _inproj_bwd(dz, dproj, wain, name, after=None):
    S = dz.shape[0]
    tm = _tile(S)
    nb = wain.shape[-1]

    def body(dz_ref, dp_ref, w_ref, dx_ref):
        acc = ALPHA * dz_ref[...]
        for j in range(N_DEV):
            acc = acc + _dot_nt(dp_ref[j // 2, :, pl.ds((j % 2) * nb, nb)], w_ref[j])
        dx_ref[...] = acc

    row = pl.BlockSpec((tm, D), lambda i: (i, 0))
    return _call(
        after, body, grid=(S // tm,),
        in_specs=[row, pl.BlockSpec((4, tm, D), lambda i: (0, i, 0)), pl.BlockSpec((N_DEV, D, nb), lambda i: (0, 0, 0))],
        out_specs=row, out_shape=_sds((S, D), F32),
        compiler_params=_params(("arbitrary",)), name=name)(dz, dproj, wain)


def _running_sum(x, reverse=False):
    rows = x.shape[0]
    row = lax.broadcasted_iota(jnp.int32, x.shape, 0)
    step = 1
    while step < rows:
        if reverse:
            x = x + jnp.where(row < rows - step, pltpu.roll(x, rows - step, 0), 0.0)
        else:
            x = x + jnp.where(row >= step, pltpu.roll(x, step, 0), 0.0)
        step *= 2
    return x


def _hg_gates(q, f, alb_ref):
    a0, a1 = alb_ref[0:1, :], alb_ref[1:2, :]
    mx = jnp.maximum(a0, a1)
    e0, e1 = jnp.exp(a0 - mx), jnp.exp(a1 - mx)
    lb = e0 / (e0 + e1)
    sig = _sigmoid(f)
    forget = lb + (1.0 - lb) * sig
    k = (1.0 - lb) * _sigmoid(-f)
    qs = q * _sigmoid(q) * (HG_DK ** -0.5)
    return qs, k, jnp.log(forget), sig, lb, forget


def _hg_intra(qs, k, b, b_scr):
    b_scr[...] = b
    bm = b_scr[pl.ds(HG_CH // 2 - 1, 1), :]
    bl = b_scr[pl.ds(HG_CH - 1, 1), :]
    eb = jnp.exp(b)
    qb = qs * eb
    e_q = jnp.exp(b - bm)
    e_k = jnp.exp(bm - b)
    e_d = jnp.exp(bl - b)
    return qb, qs * e_q, k * e_k, k * e_d, jnp.exp(bl), eb, e_q, e_k, e_d


def _hgrn_fwd(proj, alb, ngain):
    S = proj.shape[1]
    nc = S // HG_CH
    nb = nc // HG_CPB
    rb, wb = HG_CPB * HG_CH, HG_HPB * HG_DK

    def body(pj_ref, alb_ref, ng_ref, o_ref, y_ref, st_ref, st_scr, b_scr):
        n = pl.program_id(1)

        @pl.when(n == 0)
        def _():
            st_scr[...] = jnp.zeros_like(st_scr)

        r = lax.broadcasted_iota(jnp.int32, (HG_CH, HG_CH), 0)
        c = lax.broadcasted_iota(jnp.int32, (HG_CH, HG_CH), 1)
        causal = r >= c
        for ci, j in [(ci, j) for ci in range(HG_CPB) for j in range(HG_HPB)]:
            rows, lanes = pl.ds(ci * HG_CH, HG_CH), pl.ds(j * HG_DK, HG_DK)
            q, f, v, g = pj_ref[0, rows, lanes], pj_ref[1, rows, lanes], pj_ref[2, rows, lanes], pj_ref[3, rows, lanes]
            qs, k, logf, _, _, _ = _hg_gates(q, f, alb_ref.at[:, lanes])
            b = _running_sum(logf)
            qb, qt, kt, kd, ebl, _, _, _, _ = _hg_intra(qs, k, b, b_scr.at[j, ci])
            st = st_scr[j]
            st_ref[j, ci] = st
            a = jnp.where(causal, _dot_nt(qt, kt), 0.0)
            o = _dot(a, v) + _dot_nt(qb, st)
            st_scr[j] = st * ebl + _dot_tn(v, kd)
            o_ref[rows, lanes] = o
            rinv = lax.rsqrt(jnp.mean(o * o, axis=-1, keepdims=True) + RMS_EPS)
            y_ref[rows, lanes] = (o * rinv * ng_ref[...] * (g * _sigmoid(g))).astype(CDT)

    blk = pl.BlockSpec((rb, wb), lambda h, n: (n, h))
    return _call(
        None, body, grid=(HG_H // HG_HPB, nb),
        in_specs=[pl.BlockSpec((4, rb, wb), lambda h, n: (0, n, h)), pl.BlockSpec((2, wb), lambda h, n: (0, h)),
                  pl.BlockSpec((1, HG_DK), lambda h, n: (0, 0))],
        out_specs=[blk, blk, pl.BlockSpec((HG_HPB, HG_CPB, HG_DK, HG_DK), lambda h, n: (h, n, 0, 0))],
        out_shape=[_sds((S, D), F32), _sds((S, D), CDT), _sds((HG_H, nc, HG_DK, HG_DK), F32)],
        scratch_shapes=[pltpu.VMEM((HG_HPB, HG_DK, HG_DK), F32), pltpu.VMEM((HG_HPB, HG_CPB, HG_CH, HG_DK), F32)],
        compiler_params=_params(("arbitrary", "arbitrary")), name="hgrn_fwd")(proj, alb, ngain)


def _hgrn_bwd(proj, alb, ngain, o, states, dy, after=None):
    S = proj.shape[1]
    nc = S // HG_CH
    nb = nc // HG_CPB
    rb, wb = HG_CPB * HG_CH, HG_HPB * HG_DK

    def body(pj_ref, alb_ref, ng_ref, o_ref, st_ref, dy_ref, dpj_ref, dalb_ref, dng_ref, dst_scr, b_scr):
        h, n = pl.program_id(0), pl.program_id(1)

        @pl.when(n == 0)
        def _():
            dst_scr[...] = jnp.zeros_like(dst_scr)
            dalb_ref[...] = jnp.zeros_like(dalb_ref)

        _zero_at(dng_ref, jnp.logical_and(h == 0, n == 0))
        ng = ng_ref[...]
        r = lax.broadcasted_iota(jnp.int32, (HG_CH, HG_CH), 0)
        c = lax.broadcasted_iota(jnp.int32, (HG_CH, HG_CH), 1)
        causal = r >= c
        dng = None
        for ci, j in [(ci, j) for ci in reversed(range(HG_CPB)) for j in range(HG_HPB)]:
            rows, lanes = pl.ds(ci * HG_CH, HG_CH), pl.ds(j * HG_DK, HG_DK)
            q, f, v, g = pj_ref[0, rows, lanes], pj_ref[1, rows, lanes], pj_ref[2, rows, lanes], pj_ref[3, rows, lanes]
            o_ = o_ref[rows, lanes]
            dy_ = dy_ref[rows, lanes]
            sg = _sigmoid(g)
            rinv = lax.rsqrt(jnp.mean(o_ * o_, axis=-1, keepdims=True) + RMS_EPS)
            nrm = o_ * rinv
            dr = dy_ * (g * sg)
            dg = dy_ * nrm * ng * (sg * (1.0 + g * (1.0 - sg)))
            dn = dr * ng
            do = rinv * (dn - nrm * jnp.mean(dn * nrm, axis=-1, keepdims=True))
            dng = _colsum(dr * nrm) if dng is None else dng + _colsum(dr * nrm)
            qs, k, logf, sig, lb, forget = _hg_gates(q, f, alb_ref.at[:, lanes])
            b = _running_sum(logf)
            qb, qt, kt, kd, ebl, eb, e_q, e_k, e_d = _hg_intra(qs, k, b, b_scr.at[j, ci])
            st = st_ref[j, ci]
            dstn = dst_scr[j]
            qt, kt, qb, kd = (t.astype(CDT).astype(F32) for t in (qt, kt, qb, kd))
            a = jnp.where(causal, _dot_nt(qt, kt), 0.0)
            da = jnp.where(causal, _dot_nt(do, v), 0.0)
            dv = _dot_tn(a, do) + _dot_nt(kd, dstn)
            dqb = _dot(do, st)
            dkd = _dot(v, dstn)
            dqt = _dot(da, kt)
            dkt = _dot_tn(da, qt)
            dbl = _colsum(dkd * kd) + ebl * _colsum(dstn * st)
            dst_scr[j] = dstn * ebl + _dot_tn(do, qb)
            dqs = dqt * e_q + dqb * eb
            dk = dkt * e_k + dkd * e_d
            db = dqt * qt + dqb * qb - dkt * kt - dkd * kd
            dlogf = _running_sum(db, reverse=True) + dbl
            dforget = dlogf / forget
            dsig = (1.0 - lb) * (dforget - dk)
            df = dsig * sig * (1.0 - sig)
            dlb = _colsum((dforget - dk) * (1.0 - sig))
            sq = _sigmoid(q)
            dq = dqs * (HG_DK ** -0.5) * (sq * (1.0 + q * (1.0 - sq)))
            dpj_ref[0, rows, lanes] = dq.astype(CDT)
            dpj_ref[1, rows, lanes] = df.astype(CDT)
            dpj_ref[2, rows, lanes] = dv.astype(CDT)
            dpj_ref[3, rows, lanes] = dg.astype(CDT)
            da0 = dlb * lb * (1.0 - lb)
            dalb_ref[pl.ds(0, 1), lanes] += da0
            dalb_ref[pl.ds(1, 1), lanes] -= da0
        dng_ref[...] += dng

    blk = pl.BlockSpec((rb, wb), lambda h, n: (nb - 1 - n, h))
    pj = pl.BlockSpec((4, rb, wb), lambda h, n: (0, nb - 1 - n, h))
    alb_blk = pl.BlockSpec((2, wb), lambda h, n: (0, h))
    ng_blk = pl.BlockSpec((1, HG_DK), lambda h, n: (0, 0))
    return _call(
        after, body, grid=(HG_H // HG_HPB, nb),
        in_specs=[pj, alb_blk, ng_blk, blk,
                  pl.BlockSpec((HG_HPB, HG_CPB, HG_DK, HG_DK), lambda h, n: (h, nb - 1 - n, 0, 0)), blk],
        out_specs=[pj, alb_blk, ng_blk],
        out_shape=[_sds((4, S, D), CDT), _sds((2, D), F32), _sds((1, HG_DK), F32)],
        scratch_shapes=[pltpu.VMEM((HG_HPB, HG_DK, HG_DK), F32), pltpu.VMEM((HG_HPB, HG_CPB, HG_CH, HG_DK), F32)],
        compiler_params=_params(("arbitrary", "arbitrary")), name="hgrn_bwd")(proj, alb, ngain, o, states, dy)


def _slope(h):
    return 2.0 ** (-8.0 * (h + 1) / ATT_QH)


def _attn_mask(n):
    qi = lax.broadcasted_iota(jnp.int32, (WINDOW, 2 * WINDOW), 0)
    si = lax.broadcasted_iota(jnp.int32, (WINDOW, 2 * WINDOW), 1)
    dist = qi - si + WINDOW
    valid = (dist >= 0) & (dist < WINDOW) & (n * WINDOW - WINDOW + si >= 0)
    return valid, dist.astype(F32)


def _attn_probs(qh, kh, sink, slope, valid, distf):
    s = _dot_nt(qh, kh) * (ATT_HD ** -0.5) - slope * distf
    s = jnp.where(valid, s, NEG)
    m = jnp.maximum(jnp.max(s, axis=-1, keepdims=True), sink)
    e = jnp.exp(s - m)
    es = jnp.exp(sink - m)
    inv = 1.0 / (jnp.sum(e, axis=-1, keepdims=True) + es)
    return e * inv, es * inv


def _attn_specs(S):
    steps = S // (ATT_BPB * WINDOW)
    cur = lambda H: pl.BlockSpec((H, ATT_BPB * WINDOW, ATT_HD), lambda n: (0, n, 0))
    prev = lambda H: pl.BlockSpec((H, WINDOW, ATT_HD), lambda n: (0, jnp.maximum(ATT_BPB * n - 1, 0), 0))
    return steps, cur, prev


def _attn_kv(kvc_ref, kvp_ref, head, bi):
    before = kvp_ref[head] if bi == 0 else kvc_ref[head, pl.ds((bi - 1) * WINDOW, WINDOW), :]
    return jnp.concatenate([before, kvc_ref[head, pl.ds(bi * WINDOW, WINDOW), :]], axis=0)


def _attn_fwd(q4, kv4, sinks):
    S = q4.shape[1]
    nb, cur, prev = _attn_specs(S)

    def body(sink_ref, q_ref, kvc_ref, kvp_ref, o_ref, p_ref, ps_ref):
        lane = lax.broadcasted_iota(jnp.int32, (1, 128), 1)
        for bi in range(ATT_BPB):
            valid, distf = _attn_mask(pl.program_id(0) * ATT_BPB + bi)
            rows = pl.ds(bi * WINDOW, WINDOW)
            sink_probs = jnp.zeros((WINDOW, 128), F32)
            for kvh in range(ATT_KVH):
                kh = _attn_kv(kvc_ref, kvp_ref, kvh, bi)
                vh = _attn_kv(kvc_ref, kvp_ref, ATT_KVH + kvh, bi)
                v_low = jnp.dot(vh, _half_place(True), preferred_element_type=F32).astype(CDT)
                v_high = jnp.dot(vh, _half_place(False), preferred_element_type=F32).astype(CDT)
                for h in range(kvh * ATT_G, (kvh + 1) * ATT_G, 2):
                    pair = []
                    for hh in (h, h + 1):
                        p, ps = _attn_probs(q_ref[hh, rows, :], kh, sink_ref[0, hh], _slope(hh), valid, distf)
                        p = p.astype(CDT)
                        p_ref[hh, rows, :] = p
                        sink_probs = sink_probs + jnp.where(lane == hh, ps, 0.0)
                        pair.append(p)
                    o_ref[rows, pl.ds(h * ATT_HD, 2 * ATT_HD)] = (_dot(pair[0], v_low) + _dot(pair[1], v_high)).astype(CDT)
            ps_ref[rows, :] = sink_probs

    rows_spec = pl.BlockSpec((ATT_BPB * WINDOW, D), lambda n: (n, 0))
    return _call(
        None, body, grid=(nb,),
        in_specs=[pl.BlockSpec(memory_space=pltpu.SMEM), cur(ATT_QH), cur(2 * ATT_KVH), prev(2 * ATT_KVH)],
        out_specs=[rows_spec, pl.BlockSpec((ATT_QH, ATT_BPB * WINDOW, 2 * WINDOW), lambda n: (0, n, 0)),
                   pl.BlockSpec((ATT_BPB * WINDOW, 128), lambda n: (n, 0))],
        out_shape=[_sds((S, D), CDT), _sds((ATT_QH, S, 2 * WINDOW), CDT), _sds((S, 128), F32)],
        compiler_params=_params(("arbitrary",)), name="attn_fwd")(sinks, q4, kv4, kv4)


def _attn_bwd(q4, kv4, probs, sink_probs, do):
    S = q4.shape[1]
    nb, cur, prev = _attn_specs(S)

    def body(q_ref, kvc_ref, kvp_ref, p_ref, ps_ref, do_ref, dq_ref, dkv_ref, dbq_ref, dsink_ref):
        n = pl.program_id(0)
        first = n == 0

        @pl.when(first)
        def _():
            dkv_ref[...] = jnp.zeros_like(dkv_ref)
            dsink_ref[...] = jnp.zeros_like(dsink_ref)
            dbq_ref[...] = jnp.zeros_like(dbq_ref)

        lane = lax.broadcasted_iota(jnp.int32, (1, 128), 1)
        place = (_half_place(True), _half_place(False))
        row_dots = [jnp.zeros((WINDOW, 128), F32) for _ in range(ATT_BPB)]
        heads = lambda kvh: range(kvh * ATT_G, (kvh + 1) * ATT_G)
        pair_lanes = lambda h: pl.ds((h // 2) * 2 * ATT_HD, 2 * ATT_HD)

        def products_of_do(bi, kvh):
            rows = pl.ds(bi * WINDOW, WINDOW)
            kh = _attn_kv(kvc_ref, kvp_ref, kvh, bi)
            vh = _attn_kv(kvc_ref, kvp_ref, ATT_KVH + kvh, bi)
            k_pl = [jnp.dot(kh, m, preferred_element_type=F32).astype(CDT) for m in place]
            v_pl = [jnp.dot(vh, m, preferred_element_type=F32).astype(CDT) for m in place]
            dps = [_dot_nt(do_ref[rows, pair_lanes(h)], v_pl[h % 2]) for h in heads(kvh)]
            dv2 = [sum(_dot_tn(p_ref[h, rows, :], do_ref[rows, pair_lanes(h)]) for h in heads(kvh) if h % 2 == par)
                   for par in range(2)]
            return k_pl, dps, dv2

        def softmax_bwd(bi, kvh, dps):
            rows = pl.ds(bi * WINDOW, WINDOW)
            dss = []
            for h, dp in zip(heads(kvh), dps):
                p = p_ref[h, rows, :].astype(F32)
                dd = jnp.sum(p * dp, axis=-1, keepdims=True)
                dss.append((p * (dp - dd)).astype(CDT))
                row_dots[bi] = row_dots[bi] + jnp.where(lane == h, dd, 0.0)
            return dss

        def products_of_ds(bi, kvh, k_pl, dss, dv2):
            block = n * ATT_BPB + bi
            rows = pl.ds(bi * WINDOW, WINDOW)
            rows_cur = pl.ds(pl.multiple_of(block * WINDOW, WINDOW), WINDOW)
            rows_prev = pl.ds(pl.multiple_of(jnp.maximum(block - 1, 0) * WINDOW, WINDOW), WINDOW)
            for i in range(ATT_G // 2):
                h = kvh * ATT_G + 2 * i
                dq2 = (_dot(dss[2 * i], k_pl[0]) + _dot(dss[2 * i + 1], k_pl[1])) * (ATT_HD ** -0.5)
                dq_ref[rows, pair_lanes(h)] = dq2.astype(CDT)
                dbq_ref[:, pair_lanes(h)] += _colsum(dq2)
            dk = sum(_dot_tn(ds, q_ref[h, rows, :]) for h, ds in zip(heads(kvh), dss)) * (ATT_HD ** -0.5)
            dv = dv2[0][:, :ATT_HD] + pltpu.roll(dv2[1], ATT_HD, 1)[:, :ATT_HD]
            dkv_ref[kvh, rows_prev, :] += dk[:WINDOW]
            dkv_ref[kvh, rows_cur, :] += dk[WINDOW:]
            dkv_ref[ATT_KVH + kvh, rows_prev, :] += dv[:WINDOW]
            dkv_ref[ATT_KVH + kvh, rows_cur, :] += dv[WINDOW:]

        groups = [(bi, kvh) for bi in range(ATT_BPB) for kvh in range(ATT_KVH)]
        ahead = products_of_do(*groups[0])
        for gi, g in enumerate(groups):
            k_pl, dps, dv2 = ahead
            dss = softmax_bwd(*g, dps)
            if gi + 1 < len(groups):
                ahead = products_of_do(*groups[gi + 1])
            products_of_ds(*g, k_pl, dss, dv2)
        dsinks = jnp.zeros((1, 128), F32)
        for bi in range(ATT_BPB):
            dsinks = dsinks - _colsum(ps_ref[pl.ds(bi * WINDOW, WINDOW), :] * row_dots[bi])
        dsink_ref[...] += dsinks

    rows_spec = pl.BlockSpec((ATT_BPB * WINDOW, D), lambda n: (n, 0))
    return _call(
        None, body, grid=(nb,),
        in_specs=[cur(ATT_QH), cur(2 * ATT_KVH), prev(2 * ATT_KVH),
                  pl.BlockSpec((ATT_QH, ATT_BPB * WINDOW, 2 * WINDOW), lambda n: (0, n, 0)),
                  pl.BlockSpec((ATT_BPB * WINDOW, 128), lambda n: (n, 0)), rows_spec],
        out_specs=[rows_spec, pl.BlockSpec((2 * ATT_KVH, S, ATT_HD), lambda n: (0, 0, 0)),
                   pl.BlockSpec((1, D), lambda n: (0, 0)), pl.BlockSpec((1, 128), lambda n: (0, 0))],
        out_shape=[_sds((S, D), CDT), _sds((2 * ATT_KVH, S, ATT_HD), F32), _sds((1, D), F32),
                   _sds((1, 128), F32)],
        compiler_params=_params(("arbitrary",)), name="attn_bwd")(q4, kv4, kv4, probs, sink_probs, do)


def _local_step(x, p, target, getw, sm, emit):
    S = x.shape[0]
    vec = lambda a: a.reshape(1, -1)
    ln_g = lambda l, k: vec(sm["ln_gain"][l, k])
    ln_b = lambda l, k: vec(sm["ln_bias"][l, k])
    xb = x.astype(CDT)
    pb = p.astype(CDT)

    proj = _inproj_fwd(xb, getw("a_w_in"), "a_in")
    o_a, y_a, states = _hgrn_fwd(proj, sm["a_lower_bound"], sm["a_norm_gain"])
    zeros = jnp.zeros((1, D), F32)
    z = [[None] * 3 for _ in range(2)]
    xs = [[None] * 3 for _ in range(2)]
    xbs = [[None] * 3 for _ in range(2)]
    z[0][0], xs[0][0], xbs[0][0] = _mixout_ln(y_a[None], getw("a_w_out")[None], zeros, x, ln_g(0, 0), ln_b(0, 0),
                                              "a_out_ln")
    gu, hid, sgs, ups = [None, None], [None, None], [None, None], [None, None]

    def ffn_ple(l):
        wgu = getw(f"gu{l}")
        gu[l], hid[l], z[l][1], xs[l][1], xbs[l][1] = _ffn_fwd(
            xs[l][0], xbs[l][0], wgu, getw(f"dn{l}"), ln_g(l, 1), ln_b(l, 1), f"ffn_fwd{l}")
        sgs[l], ups[l], z[l][2], xs[l][2], xbs[l][2] = _ple_fwd(
            xs[l][1], xbs[l][1], pb[l], getw(f"pg{l}"), vec(sm["ple_b_gate"][l]), getw(f"pu{l}"), ln_g(l, 2),
            ln_b(l, 2), f"ple_fwd{l}")

    ffn_ple(0)
    x3, x3b = xs[0][2], xbs[0][2]
    w_kv, w_q, w_bo = getw("kv_w"), getw("b_w_q"), getw("b_w_out")
    kv4 = _proj_heads(x3b, w_kv, vec(sm["kv_b"]), 2 * ATT_KVH, "kv_proj")
    q4 = _proj_heads(x3b, w_q, vec(sm["b_b_q"]), ATT_QH, "q_proj")
    o_b, probs, sink_probs = _attn_fwd(q4, kv4, sm["b_sinks"])
    z[1][0], xs[1][0], xbs[1][0] = _mixout_ln(o_b[None], w_bo[None], sm["b_b_out"], x3, ln_g(1, 0), ln_b(1, 0),
                                              "b_out_ln")
    ffn_ple(1)
    loss, dy = _loss_fwd_bwd(xs[1][2], target)

    gs = {}
    d_ln_g = [[None] * 3 for _ in range(2)]
    d_ln_b = [[None] * 3 for _ in range(2)]
    g_bg = [None, None]

    def ffn_ple_bwd(l, dy, after=None):
        dx2, dgl, dup, d_ln_g[l][2], d_ln_b[l][2], g_bg[l] = _ple_bwd(dy, z[l][2], sgs[l], ups[l], ln_g(l, 2),
                                                                     getw(f"pg{l}"), f"ple_bwd{l}", after=after)
        g_pg = _wgrad(xbs[l][1][None], dgl[None], f"g_ple_gate{l}")[0]
        g_pu = _wgrad(pb[l][None], dup[None], f"g_ple_up{l}")[0]
        tok = emit({f"pg{l}": g_pg, f"pu{l}": g_pu})
        dz2, dzb, dgu, d_ln_g[l][1], d_ln_b[l][1] = _ffn_bwd_hidden(dx2, z[l][1], gu[l], getw(f"dn{l}"), ln_g(l, 1),
                                                                   f"ffn_bwd{l}", after=tok)
        tok = emit({f"dn{l}": _wgrad(hid[l], dzb[None], f"g_ffn_down{l}")})
        g_gu = _wgrad(dgu.reshape(8, S, FFN_B), xbs[l][0][None], f"g_ffn_gate_up{l}")
        tok = tok + emit({f"gu{l}": g_gu})
        dx1 = _ffn_bwd_input(dz2, dgu, getw(f"gu{l}"), f"ffn_bwd{l}", after=tok)
        return dx1, None

    dx1, tok = ffn_ple_bwd(1, dy)
    dz, dzb, do, d_ln_g[1][0], d_ln_b[1][0], gs["b_b_out"] = _mixout_bwd(dx1, z[1][0], ln_g(1, 0), w_bo[None], CDT,
                                                                        "b_out_bwd", after=tok)
    g_bo = _wgrad(o_b[None], dzb[None], "g_b_w_out")[0]
    dq, dkv4, dbq, dsinks = _attn_bwd(q4, kv4, probs, sink_probs, do[0])
    gs["b_b_q"] = dbq
    gs["b_sinks"] = dsinks
    g_q = _wgrad(x3b[None], dq[None], "g_b_w_q")[0]
    dx3, dkv, gs["kv_b"] = _qkv_bwd(dz, dq, dkv4, w_q, w_kv, "qkv_bwd", after=tok)
    g_kv = _wgrad(x3b[None], dkv[None], "g_kv_w")[0]
    tok = emit({"b_w_out": g_bo, "b_w_q": g_q, "kv_w": g_kv})
    dx1, tok = ffn_ple_bwd(0, dx3, tok)
    w_ao = getw("a_w_out")
    dz, dzb, dyr, d_ln_g[0][0], d_ln_b[0][0], _ = _mixout_bwd(dx1, z[0][0], ln_g(0, 0), w_ao[None], F32, "a_out_bwd",
                                                              after=tok)
    g_ao = _wgrad(y_a[None], dzb[None], "g_a_w_out")[0]
    tok = emit({"a_w_out": g_ao})
    dproj, gs["a_lower_bound"], gs["a_norm_gain"] = _hgrn_bwd(proj, sm["a_lower_bound"], sm["a_norm_gain"], o_a, states,
                                                              dyr[0], after=tok)
    tk = lambda t: (None, t, D)
    g_ain = _mm_tn(xb[None], dproj, N_DEV, lambda g, k: (0, k, 0), lambda g, k: (g // 2, k, g % 2),
                   tk, lambda t: (None, t, 512), (N_DEV, D, 512), (None, D, 512), lambda g, k: (g, 0, 0), name="g_a_w_in")
    gs["ple_b_gate"] = jnp.concatenate(g_bg, axis=0)
    gs["ln_gain"] = jnp.stack([jnp.concatenate(r, axis=0) for r in d_ln_g])
    gs["ln_bias"] = jnp.stack([jnp.concatenate(r, axis=0) for r in d_ln_b])
    gs["loss"] = loss
    tok = emit({"a_w_in": g_ain}, small=gs)
    grad_x = _inproj_bwd(dz, dproj, getw("a_w_in"), "a_in_bwd", after=tok)
    return loss, grad_x, gs


def _peer(k):
    x, y, c = lax.axis_index("x"), lax.axis_index("y"), lax.axis_index("c")
    px = 1 - x if k & 4 else x
    py = 1 - y if k & 2 else y
    pc = 1 - c if k & 1 else c
    return (px, py, pc), 4 * px + 2 * py + pc


def _my_index():
    return 4 * lax.axis_index("x") + 2 * lax.axis_index("y") + lax.axis_index("c")


def _piece_copy(mode, src, land, send_sems, recv_sems, t, k, sender, receiver, peer):
    return pltpu.make_async_remote_copy(
        src_ref=src if mode == "gather" else src.at[receiver], dst_ref=land.at[sender],
        send_sem=send_sems.at[t * 7 + k - 1], recv_sem=recv_sems.at[t * 7 + k - 1], device_id=peer, device_id_type=MESH)


def _sequencer_exchange(srcs, modes, name, collective_id, after=None):
    n = len(srcs)
    land_shapes = [((N_DEV,) + a.shape) if mode == "gather" else a.shape for a, mode in zip(srcs, modes)]
    extra = [] if after is None else [after]

    def body(*refs):
        src_refs, land_refs = refs[:n], refs[n + len(extra):2 * n + len(extra)]
        send_sems, recv_sems, local_sems = refs[2 * n + len(extra):]
        barrier = pltpu.get_barrier_semaphore()
        for k in range(1, N_DEV):
            pl.semaphore_signal(barrier, inc=1, device_id=_peer(k)[0], device_id_type=MESH)
        pl.semaphore_wait(barrier, N_DEV - 1)
        me = _my_index()
        local = []
        for i in range(n):
            cp = pltpu.make_async_copy(src_refs[i] if modes[i] == "gather" else src_refs[i].at[me], land_refs[i].at[me],
                                       local_sems.at[i])
            cp.start()
            local.append(cp)
        for k in range(1, N_DEV):
            peer, pid = _peer(k)
            for t in range(n):
                _piece_copy(modes[t], src_refs[t], land_refs[t], send_sems, recv_sems, t, k, me, pid, peer).start()
        for k in range(1, N_DEV):
            peer, pid = _peer(k)
            for t in range(n):
                _piece_copy(modes[t], src_refs[t], land_refs[t], send_sems, recv_sems, t, k, pid, me, peer).wait_recv()
        for k in range(1, N_DEV):
            peer, pid = _peer(k)
            for t in range(n):
                _piece_copy(modes[t], src_refs[t], land_refs[t], send_sems, recv_sems, t, k, me, pid, peer).wait_send()
        for cp in local:
            cp.wait()

    return pl.kernel(
        body, out_type=[_sds(s, a.dtype) for s, a in zip(land_shapes, srcs)],
        mesh=plsc.ScalarSubcoreMesh(axis_name="sequencer", num_cores=1),
        scratch_types=[pltpu.SemaphoreType.DMA((7 * n,)), pltpu.SemaphoreType.DMA((7 * n,)), pltpu.SemaphoreType.DMA((n,))],
        compiler_params=pltpu.CompilerParams(collective_id=collective_id), name=name)(*srcs, *extra)


def _sequencer_gather(srcs, name, collective_id, after=None):
    n = len(srcs)
    extra = [] if after is None else [after]

    def body(*refs):
        src_refs, land_refs = refs[:n], refs[n + len(extra):2 * n + len(extra)]
        send_sems, recv_sems, local_sems = refs[2 * n + len(extra):]
        x, y, c = lax.axis_index("x"), lax.axis_index("y"), lax.axis_index("c")
        sibling = (x, y, 1 - c)
        chips = [(1 - x, y), (x, 1 - y), (1 - x, 1 - y)]
        index = lambda px, py, pc: 4 * px + 2 * py + pc
        barrier = pltpu.get_barrier_semaphore()
        for peer in [sibling] + [(*chip, c) for chip in chips]:
            pl.semaphore_signal(barrier, inc=1, device_id=peer, device_id_type=MESH)
        pl.semaphore_wait(barrier, 4)

        def copy(t, k, slot, to, src=None):
            return pltpu.make_async_remote_copy(
                src_ref=land_refs[t].at[slot] if src is None else src, dst_ref=land_refs[t].at[slot],
                send_sem=send_sems.at[7 * t + k], recv_sem=recv_sems.at[7 * t + k], device_id=to, device_id_type=MESH)

        me = index(x, y, c)
        local = []
        for t in range(n):
            cp = pltpu.make_async_copy(src_refs[t], land_refs[t].at[me], local_sems.at[t])
            cp.start()
            local.append(cp)
        sends = []
        for t in range(n):
            sends.append(copy(t, 0, me, sibling, src=src_refs[t]))
            sends += [copy(t, 1 + j, me, (*chip, c), src=src_refs[t]) for j, chip in enumerate(chips)]
        for cp in sends:
            cp.start()
        for j, chip in enumerate(chips):
            for t in range(n):
                copy(t, 1 + j, index(*chip, c), sibling, src=src_refs[t]).wait_recv()
                passed = copy(t, 4 + j, index(*chip, c), sibling)
                passed.start()
                sends.append(passed)
        for t in range(n):
            copy(t, 0, index(x, y, 1 - c), sibling, src=src_refs[t]).wait_recv()
        for j, chip in enumerate(chips):
            for t in range(n):
                copy(t, 4 + j, index(*chip, 1 - c), sibling, src=src_refs[t]).wait_recv()
        for cp in sends:
            cp.wait_send()
        for cp in local:
            cp.wait()

    return pl.kernel(
        body, out_type=[_sds((N_DEV,) + a.shape, a.dtype) for a in srcs],
        mesh=plsc.ScalarSubcoreMesh(axis_name="sequencer", num_cores=1),
        scratch_types=[pltpu.SemaphoreType.DMA((7 * n,)), pltpu.SemaphoreType.DMA((7 * n,)), pltpu.SemaphoreType.DMA((n,))],
        compiler_params=pltpu.CompilerParams(collective_id=collective_id), name=name)(*srcs, *extra)


def _adamw(w, g, m, v):
    m = ADAM_B1 * m + (1.0 - ADAM_B1) * g
    v = ADAM_B2 * v + (1.0 - ADAM_B2) * (g * g)
    m_hat = m / (1.0 - ADAM_B1 ** ADAM_STEP)
    v_hat = v / (1.0 - ADAM_B2 ** ADAM_STEP)
    delta = -ADAM_LR * (m_hat / (jnp.sqrt(v_hat) + ADAM_EPS) + ADAM_WD * w)
    return delta, m, v


def _adam_big(w, parts, m, v, name, after=None):
    L, R, C = w.shape
    P = parts[0].shape[0]
    tr = _tile(R, (256, 128, 176, 64, 32, 16))
    nr = R // tr

    def body(w_ref, *refs):
        p_refs, (m_ref, v_ref, g_ref, d_ref, mo_ref, vo_ref) = refs[:L], refs[L:]
        for l in range(L):
            @pl.when(pl.program_id(0) == l)
            def _(p_ref=p_refs[l]):
                g = p_ref[0].astype(F32)
                for s in range(1, P):
                    g = g + p_ref[s].astype(F32)
                g_ref[...] = g
                d_ref[...], mo_ref[...], vo_ref[...] = _adamw(w_ref[...], g, m_ref[...], v_ref[...])

    row = pl.BlockSpec((None, tr, C), lambda l, i: (l, i, 0))
    park = lambda l_of: (lambda l, i: (0, jnp.where(l == l_of, i, 0 if l_of else nr - 1), 0))
    return _call(
        after, body, grid=(L, nr),
        in_specs=[row] + [pl.BlockSpec((P, tr, C), park(l)) for l in range(L)] + [row, row],
        out_specs=[row] * 4, out_shape=[_sds((L, R, C), F32)] * 4,
        compiler_params=_params(("arbitrary", "arbitrary")), name=name)(w, *parts, m, v)


SMALL = (("a_lower_bound", (2, 128), (2, D)), ("ln_gain", (3, 2, 128), (6, D)), ("ln_bias", (3, 2, 128), (6, D)),
         ("a_norm_gain", (1, 128), (1, 128)), ("kv_b", (1, 512), (1, 512)), ("b_b_q", (1, D), (1, D)),
         ("b_sinks", (1, ATT_QH), (1, 128)), ("b_b_out", (1, D), (1, D)), ("ple_b_gate", (2, D), (2, D)))


def _adam_small(parts, w, m, v, losses, after=None):
    k = len(SMALL)

    def body(*refs):
        p_refs, w_refs, m_refs, v_refs = refs[:k], refs[k:2 * k], refs[2 * k:3 * k], refs[3 * k:4 * k]
        loss_ref, outs, total_ref = refs[4 * k], refs[4 * k + 1:-1], refs[-1]
        total = loss_ref[0]
        for s in range(1, N_DEV):
            total = total + loss_ref[s]
        total_ref[...] = total
        me = _my_index()
        for i, (_, wshape, pshape) in enumerate(SMALL):
            cols = wshape[-1]
            lanes = slice(None) if cols == pshape[1] else (
                pl.ds(0, cols) if cols < 128 else pl.ds(pl.multiple_of(me * cols, cols), cols))
            at = [(slice(None), slice(None))] if len(wshape) == 2 else [
                (pl.ds(l * wshape[0] + kk, 1), (kk, pl.ds(l, 1), slice(None))) for kk in range(wshape[0]) for l in range(wshape[1])]
            for rows, own in at:
                g = p_refs[i][0, rows, lanes]
                for s in range(1, N_DEV):
                    g = g + p_refs[i][s, rows, lanes]
                g_ref, d_ref, mo_ref, vo_ref = outs[4 * i:4 * i + 4]
                g_ref[own] = g
                d_ref[own], mo_ref[own], vo_ref[own] = _adamw(w_refs[i][own], g, m_refs[i][own], v_refs[i][own])

    full = lambda shape: pl.BlockSpec(shape, lambda: (0,) * len(shape))
    names = [n for n, _, _ in SMALL]
    held = lambda a, ws: a.swapaxes(0, 1) if len(ws) == 3 else a.reshape(ws)
    back = lambda r, n: r.swapaxes(0, 1) if r.ndim == 3 else r.reshape(w[n].shape)
    res = _call(
        after, body,
        in_specs=[full((N_DEV,) + ps) for _, _, ps in SMALL] + [full(ws) for _, ws, _ in SMALL] * 3
        + [full((N_DEV, 1, 128))],
        out_specs=[full(ws) for _, ws, _ in SMALL for _ in range(4)] + [full((1, 128))],
        out_shape=[_sds(ws, F32) for _, ws, _ in SMALL for _ in range(4)] + [_sds((1, 128), F32)], name="adam_small")(
            *[parts[n] for n in names], *[held(a[n], ws) for a in (w, m, v) for n, ws, _ in SMALL], losses)
    return {n: [back(r, n) for r in res[4 * i:4 * i + 4]] for i, n in enumerate(names)}, res[-1][0, 0]


WEIGHTS = ("a_w_in", "a_lower_bound", "a_norm_gain", "a_w_out", "kv_w", "kv_b", "b_w_q", "b_b_q", "b_sinks", "b_w_out",
           "b_b_out", "ffn_w_gate_up", "ffn_w_down", "ple_w_up", "ple_w_gate", "ple_b_gate", "ln_gain", "ln_bias")


GATHER_GROUPS = (("a_w_in",), ("a_w_out", "gu0"), ("dn0", "pu0", "pg0"), ("kv_w", "b_w_q", "b_w_out"), ("gu1",),
                 ("dn1", "pu1", "pg1"))
KERNEL_LAYOUT = {
    "a_w_in": lambda a: a,
    "a_w_out": lambda a: a.reshape(D, D),
    "kv_w": lambda a: a.reshape(D, 2 * ATT_KVH * ATT_HD),
    "b_w_q": lambda a: a.reshape(D, D),
    "b_w_out": lambda a: a.reshape(D, D),
    "gu": lambda a: a.reshape(2, 4, FFN_B, D),
    "dn": lambda a: a.reshape(4, FFN_B, D),
    "pu": lambda a: a,
    "pg": lambda a: a.reshape(D, D),
}
_row_blocks = lambda a: a.reshape(N_DEV, -1, a.shape[-1])
OWNER_BLOCKS = {
    "a_w_in": lambda g: g,
    "a_w_out": _row_blocks,
    "kv_w": _row_blocks,
    "b_w_q": _row_blocks,
    "b_w_out": _row_blocks,
    "gu": lambda g: g,
    "dn": lambda g: _row_blocks(g.reshape(FFN_H, D)),
    "pu": lambda g: g.reshape(PLE_DIM, N_DEV, 128).transpose(1, 0, 2),
    "pg": _row_blocks,
}
ADAM_PARTS = (("kv_w", ("kv_w",)), ("b_w_q", ("b_w_q",)), ("b_w_out", ("b_w_out",)), ("ffn_w_gate_up", ("gu0", "gu1")),
              ("ffn_w_down", ("dn0", "dn1")), ("ple_w_up", ("pu0", "pu1")), ("ple_w_gate", ("pg0", "pg1")),
              ("a_w_out", ("a_w_out",)), ("a_w_in", ("a_w_in",)))


def kernel(x, p, a_w_in, a_lower_bound, a_norm_gain, a_w_out, kv_w, kv_b, b_w_q, b_b_q, b_sinks, b_w_out, b_b_out, ffn_w_gate_up, ffn_w_down, ple_w_up, ple_w_gate, ple_b_gate, ln_gain, ln_bias, loss_target, m_a_w_in, m_a_lower_bound, m_a_norm_gain, m_a_w_out, m_kv_w, m_kv_b, m_b_w_q, m_b_b_q, m_b_sinks, m_b_w_out, m_b_b_out, m_ffn_w_gate_up, m_ffn_w_down, m_ple_w_up, m_ple_w_gate, m_ple_b_gate, m_ln_gain, m_ln_bias, v_a_w_in, v_a_lower_bound, v_a_norm_gain, v_a_w_out, v_kv_w, v_kv_b, v_b_w_q, v_b_b_q, v_b_sinks, v_b_w_out, v_b_b_out, v_ffn_w_gate_up, v_ffn_w_down, v_ple_w_up, v_ple_w_gate, v_ple_b_gate, v_ln_gain, v_ln_bias):
    given = dict(locals())
    w = {n: given[n] for n in WEIGHTS}
    m = {n: given["m_" + n] for n in WEIGHTS}
    v = {n: given["v_" + n] for n in WEIGHTS}
    shards = {"a_w_in": a_w_in[0], "a_w_out": a_w_out[0], "kv_w": kv_w, "b_w_q": b_w_q[0], "b_w_out": b_w_out[0]}
    for l in range(2):
        shards.update({f"gu{l}": ffn_w_gate_up[l].T, f"dn{l}": ffn_w_down[l], f"pu{l}": ple_w_up[l], f"pg{l}": ple_w_gate[l]})
    sharded_small = [a_lower_bound, ln_gain.swapaxes(0, 1), ln_bias.swapaxes(0, 1)]
    gathered = {}
    for gi, g in enumerate(GATHER_GROUPS):
        lands = _sequencer_gather([shards[n].astype(CDT) for n in g] + (sharded_small if gi == 0 else []),
                                  f"gather{gi}", gi)
        for n, a in zip(g, lands):
            gathered[n] = KERNEL_LAYOUT[n.rstrip("01")](a)
        if gi == 0:
            alb = lands[len(g)].transpose(1, 0, 2).reshape(2, D)
            lng, lnb = [a.transpose(2, 1, 0, 3).reshape(2, 3, D) for a in lands[len(g) + 1:]]

    getw = gathered.__getitem__

    sm = {"a_lower_bound": alb, "ln_gain": lng, "ln_bias": lnb,
          "a_norm_gain": a_norm_gain, "kv_b": kv_b, "b_b_q": b_b_q[0], "b_sinks": b_sinks, "b_b_out": b_b_out,
          "ple_b_gate": ple_b_gate}

    scatters, small_parts = [], {}

    def emit(grads, small=None):
        names = list(grads)
        blocks = [OWNER_BLOCKS[n.rstrip("01")](grads[n]) for n in names]
        partials = [] if small is None else [small[n].reshape(ps) for n, _, ps in SMALL] + [small["loss"]]
        lands = _sequencer_exchange(blocks + partials, ["scatter"] * len(blocks) + ["gather"] * len(partials),
                                    f"scatter{len(scatters)}", len(GATHER_GROUPS) + len(scatters))
        scatters.append(dict(zip(names, lands)))
        small_parts.update(zip([n for n, _, _ in SMALL] + ["loss"], lands[len(blocks):]))
        return blocks + (list(scatters[-4].values()) if len(scatters) >= 4 else [])

    loss, grad_x, gs = _local_step(x[0], p[:, 0], loss_target[0], getw, sm, emit)

    out, parts, last = {}, {}, [grad_x]
    for landed in scatters:
        parts.update(landed)
        for n, keys in ADAM_PARTS:
            if n in out or not all(key in parts for key in keys):
                continue
            lrc = (1,) * (3 - w[n].ndim) + w[n].shape
            shard = (lambda a: a.reshape(lrc).swapaxes(1, 2)) if n == "ffn_w_gate_up" else (lambda a: a.reshape(lrc))
            res = _adam_big(shard(w[n]), [parts[key] for key in keys], shard(m[n]), shard(v[n]), "adam_" + n, after=last)
            out[n] = [(r.swapaxes(1, 2) if n == "ffn_w_gate_up" else r).reshape(w[n].shape) for r in res]
            last = [res[3]]
    small_out, loss = _adam_small(small_parts, w, m, v, small_parts["loss"], after=last)
    out.update(small_out)
    res = [loss, grad_x[None]]
    for i in range(4):
        res += [out[n][i] for n in WEIGHTS]
    return tuple(res)
```

```python
import jax
import jax.numpy as jnp
from jax import lax
from jax.experimental import pallas as pl
from jax.experimental.pallas import tpu as pltpu
from jax.experimental.pallas import tpu_sc as plsc

F32 = jnp.float32
CDT = jnp.bfloat16

N_DEV = 8
D = 1024
HG_H, HG_DK, HG_CH = 8, 128, 64
HG_HPB = 4
HG_CPB = 8
ATT_HD, ATT_QH, ATT_KVH, ATT_G, WINDOW = 64, 16, 4, 4, 128
ATT_BPB = 1
FFN_H = 2816
FFN_B = FFN_H // 4
PLE_DIM = 256
ALPHA = (2.0 * 2) ** 0.25
LN_EPS = 1e-5
RMS_EPS = 1e-6
ADAM_LR, ADAM_B1, ADAM_B2, ADAM_EPS, ADAM_WD, ADAM_STEP = 0.001, 0.9, 0.999, 1e-08, 0.01, 10
ROW_TILES = (512, 256, 128, 64)
VMEM_LIMIT = 48 * 1024 * 1024
NEG = -1e30

MESH = pl.DeviceIdType.MESH


def _tile(n, cands=ROW_TILES):
    for t in cands:
        if n % t == 0:
            return t
    return n


def _sds(shape, dtype):
    return jax.ShapeDtypeStruct(tuple(shape), dtype)


def _params(sem):
    return pltpu.CompilerParams(dimension_semantics=sem, vmem_limit_bytes=VMEM_LIMIT)


def _dot(a, b):
    return jnp.dot(a.astype(CDT), b.astype(CDT), preferred_element_type=F32)


def _dot_nt(a, b):
    return lax.dot_general(a.astype(CDT), b.astype(CDT), (((1,), (1,)), ((), ())), preferred_element_type=F32)


def _dot_tn(a, b):
    return lax.dot_general(a.astype(CDT), b.astype(CDT), (((0,), (0,)), ((), ())), preferred_element_type=F32)


def _sigmoid(x):
    return jax.nn.sigmoid(x)


def _ln_fwd(z, g, b):
    mu = jnp.mean(z, axis=-1, keepdims=True)
    zc = z - mu
    var = jnp.mean(zc * zc, axis=-1, keepdims=True)
    return zc * lax.rsqrt(var + LN_EPS) * g + b


def _ln_bwd(z, g, dy):
    mu = jnp.mean(z, axis=-1, keepdims=True)
    zc = z - mu
    var = jnp.mean(zc * zc, axis=-1, keepdims=True)
    rstd = lax.rsqrt(var + LN_EPS)
    xhat = zc * rstd
    dxh = dy * g
    dz = rstd * (dxh - jnp.mean(dxh, axis=-1, keepdims=True) - xhat * jnp.mean(dxh * xhat, axis=-1, keepdims=True))
    return dz, xhat


def _colsum(x):
    return jnp.sum(x, axis=0, keepdims=True)


def _acc(ref, val, first):
    @pl.when(first)
    def _():
        ref[...] = val

    @pl.when(jnp.logical_not(first))
    def _():
        ref[...] += val


def _zero_at(ref, first):
    @pl.when(first)
    def _():
        ref[...] = jnp.zeros_like(ref)


def _call(after, body, **kw):
    after = [] if after is None else list(after)
    specs = list(kw["in_specs"])
    kw["in_specs"] = [pl.BlockSpec(memory_space=pl.ANY)] * len(after) + specs

    def ordered_body(*refs):
        body(*refs[len(after):])

    call = pl.pallas_call(ordered_body, **kw)
    return lambda *args: call(*after, *args)


def _mm_tn(a3, b3, G, amap, bmap, ablock, bblock, out_shape, oblock, omap, name="mm_tn"):
    S = a3.shape[1]

    def body(a_ref, b_ref, o_ref):
        o_ref[...] = _dot_tn(a_ref[...], b_ref[...]).astype(o_ref.dtype)

    return _call(
        None, body, grid=(G, 1),
        in_specs=[pl.BlockSpec(ablock(S), amap), pl.BlockSpec(bblock(S), bmap)],
        out_specs=pl.BlockSpec(oblock, omap), out_shape=_sds(out_shape, CDT),
        compiler_params=_params(("arbitrary", "arbitrary")), name=name)(a3, b3)


def _wgrad(a3, b3, name):
    Ga, S, M = a3.shape
    Gb, _, N = b3.shape
    G = max(Ga, Gb)
    return _mm_tn(
        a3, b3, G,
        (lambda g, k: (g, k, 0)) if Ga > 1 else (lambda g, k: (0, k, 0)),
        (lambda g, k: (g, k, 0)) if Gb > 1 else (lambda g, k: (0, k, 0)),
        lambda tk: (None, tk, M), lambda tk: (None, tk, N),
        (G, M, N), (None, M, N), lambda g, k: (g, 0, 0), name=name)


def _mixout_ln(u3, w3, bias, xin, gain, beta, name):
    G, S, Kb = u3.shape
    tm = _tile(S)

    def body(u_ref, w_ref, b_ref, x_ref, g_ref, be_ref, z_ref, xo_ref, xob_ref):
        h = b_ref[...] + _dot(u_ref[0], w_ref[0])
        for g in range(1, G):
            h = h + _dot(u_ref[g], w_ref[g])
        z = ALPHA * x_ref[...] + h
        z_ref[...] = z
        y = _ln_fwd(z, g_ref[...], be_ref[...])
        xo_ref[...] = y
        xob_ref[...] = y.astype(CDT)

    row = pl.BlockSpec((tm, D), lambda i: (i, 0))
    vec = pl.BlockSpec((1, D), lambda i: (0, 0))
    return _call(
        None, body, grid=(S // tm,),
        in_specs=[pl.BlockSpec((G, tm, Kb), lambda i: (0, i, 0)), pl.BlockSpec((G, Kb, D), lambda i: (0, 0, 0)),
                  vec, row, vec, vec],
        out_specs=[row, row, row], out_shape=[_sds((S, D), F32), _sds((S, D), F32), _sds((S, D), CDT)],
        compiler_params=_params(("arbitrary",)), name=name)(u3, w3, bias, xin, gain, beta)


def _ffn_fwd(xin, xin_b, wgu, wdn, gain, beta, name):
    S = xin.shape[0]
    tm = _tile(S)

    def hidden(xb_ref, wgu_ref, gu_ref, hid_ref):
        xb = xb_ref[...]
        gate = _dot_nt(xb, wgu_ref[0])
        up = _dot_nt(xb, wgu_ref[1])
        gu_ref[0] = gate.astype(CDT)
        gu_ref[1] = up.astype(CDT)
        hid_ref[...] = (gate * _sigmoid(gate) * up).astype(CDT)

    gu, hid = _call(
        None, hidden, grid=(4, S // tm),
        in_specs=[pl.BlockSpec((tm, D), lambda j, i: (i, 0)), pl.BlockSpec((2, None, FFN_B, D), lambda j, i: (0, j, 0, 0))],
        out_specs=[pl.BlockSpec((2, None, tm, FFN_B), lambda j, i: (0, j, i, 0)),
                   pl.BlockSpec((None, tm, FFN_B), lambda j, i: (j, i, 0))],
        out_shape=[_sds((2, 4, S, FFN_B), CDT), _sds((4, S, FFN_B), CDT)],
        compiler_params=_params(("arbitrary", "arbitrary")), name=name + "_hidden")(xin_b, wgu)

    def down(x_ref, hid_ref, wdn_ref, g_ref, be_ref, z_ref, xo_ref, xob_ref):
        z = ALPHA * x_ref[...]
        for j in range(4):
            z = z + _dot(hid_ref[j], wdn_ref[j])
        z_ref[...] = z
        y = _ln_fwd(z, g_ref[...], be_ref[...])
        xo_ref[...] = y
        xob_ref[...] = y.astype(CDT)

    row = pl.BlockSpec((tm, D), lambda i: (i, 0))
    vec = pl.BlockSpec((1, D), lambda i: (0, 0))
    z, xo, xob = _call(
        None, down, grid=(S // tm,),
        in_specs=[row, pl.BlockSpec((4, tm, FFN_B), lambda i: (0, i, 0)), pl.BlockSpec((4, FFN_B, D), lambda i: (0, 0, 0)),
                  vec, vec],
        out_specs=[row, row, row], out_shape=[_sds((S, D), F32), _sds((S, D), F32), _sds((S, D), CDT)],
        compiler_params=_params(("arbitrary",)), name=name + "_down")(xin, hid, wdn, gain, beta)
    return gu, hid, z, xo, xob


def _ple_fwd(xin, xin_b, p_b, wpg, bgate, wpu, gain, beta, name):
    S = xin.shape[0]
    tm = _tile(S)

    def body(x_ref, xb_ref, p_ref, wpg_ref, bg_ref, wpu_ref, g_ref, be_ref, sg_ref, up_ref, z_ref, xo_ref, xob_ref):
        sg = _sigmoid(_dot(xb_ref[...], wpg_ref[...]) + bg_ref[...])
        pb = p_ref[...]
        up = jnp.concatenate([_dot(pb, wpu_ref[j]) for j in range(N_DEV)], axis=-1)
        sg_ref[...] = sg.astype(CDT)
        up_ref[...] = (up * sg * (1.0 - sg)).astype(CDT)
        z = ALPHA * x_ref[...] + sg * up
        z_ref[...] = z
        y = _ln_fwd(z, g_ref[...], be_ref[...])
        xo_ref[...] = y
        xob_ref[...] = y.astype(CDT)

    row = pl.BlockSpec((tm, D), lambda i: (i, 0))
    vec = pl.BlockSpec((1, D), lambda i: (0, 0))
    return _call(
        None, body, grid=(S // tm,),
        in_specs=[row, row, pl.BlockSpec((tm, PLE_DIM), lambda i: (i, 0)), pl.BlockSpec((D, D), lambda i: (0, 0)), vec,
                  pl.BlockSpec((N_DEV, PLE_DIM, D // N_DEV), lambda i: (0, 0, 0)), vec, vec],
        out_specs=[row] * 5,
        out_shape=[_sds((S, D), CDT)] * 2 + [_sds((S, D), F32)] * 2 + [_sds((S, D), CDT)],
        compiler_params=_params(("arbitrary",)), name=name)(xin, xin_b, p_b, wpg, bgate, wpu, gain, beta)


def _loss_fwd_bwd(y, target):
    S = y.shape[0]
    tm = _tile(S)

    def body(y_ref, t_ref, l_ref, dy_ref):
        e = y_ref[...] - t_ref[...]
        dy_ref[...] = e * (1.0 / D)
        part = 0.5 * jnp.sum(jnp.sum(e * e, axis=-1, keepdims=True) * (1.0 / D), axis=0, keepdims=True)
        _acc(l_ref, jnp.broadcast_to(part, l_ref.shape), pl.program_id(0) == 0)

    row = pl.BlockSpec((tm, D), lambda i: (i, 0))
    return _call(
        None, body, grid=(S // tm,), in_specs=[row, row],
        out_specs=[pl.BlockSpec((1, 128), lambda i: (0, 0)), row],
        out_shape=[_sds((1, 128), F32), _sds((S, D), F32)],
        compiler_params=_params(("arbitrary",)), name="loss")(y, target)


def _ple_bwd(dy, z, sg, up, gain, wpg, z_ffn, gain_ffn, name, after=None):
    S = dy.shape[0]
    tm = _tile(S)

    def body(dy_ref, z_ref, sg_ref, up_ref, g_ref, wpg_ref, zf_ref, gf_ref, dzf_ref, dzfb_ref, dgl_ref, dup_ref,
             dgain_ref, dbeta_ref, dbg_ref, dgainf_ref, dbetaf_ref):
        first = pl.program_id(0) == 0
        dy_ = dy_ref[...]
        dz, xhat = _ln_bwd(z_ref[...], g_ref[...], dy_)
        dgl = dz * up_ref[...].astype(F32)
        dgl_ref[...] = dgl.astype(CDT)
        dup_ref[...] = (dz * sg_ref[...].astype(F32)).astype(CDT)
        dx = ALPHA * dz + _dot_nt(dgl, wpg_ref[...])
        dzf, xhatf = _ln_bwd(zf_ref[...], gf_ref[...], dx)
        dzf_ref[...] = dzf
        dzfb_ref[...] = dzf.astype(CDT)
        _acc(dgain_ref, _colsum(dy_ * xhat), first)
        _acc(dbeta_ref, _colsum(dy_), first)
        _acc(dbg_ref, _colsum(dgl), first)
        _acc(dgainf_ref, _colsum(dx * xhatf), first)
        _acc(dbetaf_ref, _colsum(dx), first)

    row = pl.BlockSpec((tm, D), lambda i: (i, 0))
    vec = pl.BlockSpec((1, D), lambda i: (0, 0))
    return _call(
        after, body, grid=(S // tm,),
        in_specs=[row, row, row, row, vec, pl.BlockSpec((D, D), lambda i: (0, 0)), row, vec],
        out_specs=[row, row, row, row] + [vec] * 5,
        out_shape=[_sds((S, D), F32)] + [_sds((S, D), CDT)] * 3 + [_sds((1, D), F32)] * 5,
        compiler_params=_params(("arbitrary",)), name=name)(dy, z, sg, up, gain, wpg, z_ffn, gain_ffn)


def _ffn_bwd_hidden(dzb, gu, wdn, name, after=None):
    S = dzb.shape[0]
    tm = _tile(S)

    def hidden(dzb_ref, gu_ref, wdn_ref, dgu_ref):
        dhid = _dot_nt(dzb_ref[...], wdn_ref[...])
        gate, up = gu_ref[0].astype(F32), gu_ref[1].astype(F32)
        sg = _sigmoid(gate)
        dgu_ref[0] = (dhid * up * (sg * (1.0 + gate * (1.0 - sg)))).astype(CDT)
        dgu_ref[1] = (dhid * (gate * sg)).astype(CDT)

    blocks = pl.BlockSpec((2, None, tm, FFN_B), lambda j, i: (0, j, i, 0))
    return _call(
        after, hidden, grid=(4, S // tm),
        in_specs=[pl.BlockSpec((tm, D), lambda j, i: (i, 0)), blocks, pl.BlockSpec((None, FFN_B, D), lambda j, i: (j, 0, 0))],
        out_specs=blocks, out_shape=_sds((2, 4, S, FFN_B), CDT),
        compiler_params=_params(("arbitrary", "arbitrary")), name=name + "_hidden")(dzb, gu, wdn)


def _ffn_bwd_input(dz, dgu, wgu, name, after=None):
    S = dz.shape[0]
    tm = _tile(S)

    def to_input(dz_ref, dgu_ref, wgu_ref, dx_ref):
        acc = ALPHA * dz_ref[...]
        for g in range(2):
            for j in range(4):
                acc = acc + _dot(dgu_ref[g, j], wgu_ref[g, j])
        dx_ref[...] = acc

    rows = pl.BlockSpec((tm, D), lambda i: (i, 0))
    return _call(
        after, to_input, grid=(S // tm,),
        in_specs=[rows, pl.BlockSpec((2, 4, tm, FFN_B), lambda i: (0, 0, i, 0)),
                  pl.BlockSpec((2, 4, FFN_B, D), lambda i: (0, 0, 0, 0))],
        out_specs=rows, out_shape=_sds((S, D), F32),
        compiler_params=_params(("arbitrary",)), name=name + "_input")(dz, dgu, wgu)


def _mixout_bwd(dy, z, gain, w3, du_dtype, name, after=None):
    S = dy.shape[0]
    G, Kb, _ = w3.shape
    tm = _tile(S)

    def body(dy_ref, z_ref, g_ref, w_ref, dz_ref, dzb_ref, du_ref, dgain_ref, dbeta_ref, dbias_ref):
        first = pl.program_id(0) == 0
        dy_ = dy_ref[...]
        dz, xhat = _ln_bwd(z_ref[...], g_ref[...], dy_)
        dz_ref[...] = dz
        dzb = dz.astype(CDT)
        dzb_ref[...] = dzb
        for g in range(G):
            du_ref[g] = _dot_nt(dzb, w_ref[g]).astype(du_ref.dtype)
        _acc(dgain_ref, _colsum(dy_ * xhat), first)
        _acc(dbeta_ref, _colsum(dy_), first)
        _acc(dbias_ref, _colsum(dz), first)

    row = pl.BlockSpec((tm, D), lambda i: (i, 0))
    vec = pl.BlockSpec((1, D), lambda i: (0, 0))
    return _call(
        after, body, grid=(S // tm,), in_specs=[row, row, vec, pl.BlockSpec((G, Kb, D), lambda i: (0, 0, 0))],
        out_specs=[row, row, pl.BlockSpec((G, tm, Kb), lambda i: (0, i, 0)), vec, vec, vec],
        out_shape=[_sds((S, D), F32), _sds((S, D), CDT), _sds((G, S, Kb), du_dtype)] + [_sds((1, D), F32)] * 3,
        compiler_params=_params(("arbitrary",)), name=name)(dy, z, gain, w3)


def _half_select(low):
    r = lax.broadcasted_iota(jnp.int32, (2 * ATT_HD, ATT_HD), 0)
    c = lax.broadcasted_iota(jnp.int32, (2 * ATT_HD, ATT_HD), 1)
    return (r == c + (0 if low else ATT_HD)).astype(CDT)


def _half_place(low):
    r = lax.broadcasted_iota(jnp.int32, (ATT_HD, 2 * ATT_HD), 0)
    c = lax.broadcasted_iota(jnp.int32, (ATT_HD, 2 * ATT_HD), 1)
    return (c == r + (0 if low else ATT_HD)).astype(CDT)


def _pair_lanes(even, odd):
    return (jnp.dot(even, _half_place(True), preferred_element_type=F32)
            + jnp.dot(odd, _half_place(False), preferred_element_type=F32)).astype(CDT)


def _proj_heads(a, w, bias, heads, name):
    S, K = a.shape
    N = heads * ATT_HD
    tm = _tile(S)

    def body(a_ref, w_ref, b_ref, o_ref):
        acc = (_dot(a_ref[...], w_ref[...]) + b_ref[...]).astype(CDT)
        sel = (_half_select(True), _half_select(False))
        for h in range(heads):
            pair = acc[:, (h // 2) * 2 * ATT_HD:(h // 2 + 1) * 2 * ATT_HD]
            o_ref[h] = jnp.dot(pair, sel[h % 2], preferred_element_type=F32).astype(CDT)

    return _call(
        None, body, grid=(S // tm,),
        in_specs=[pl.BlockSpec((tm, K), lambda i: (i, 0)), pl.BlockSpec((K, N), lambda i: (0, 0)),
                  pl.BlockSpec((1, N), lambda i: (0, 0))],
        out_specs=pl.BlockSpec((heads, tm, ATT_HD), lambda i: (0, i, 0)), out_shape=_sds((heads, S, ATT_HD), CDT),
        compiler_params=_params(("arbitrary",)), name=name)(a, w, bias)


def _qkv_bwd(dz, dq, dkv4, wq, wkv, name, after=None):
    S = dz.shape[0]
    tm = _tile(S)
    HK = dkv4.shape[0]
    NK = HK * ATT_HD

    def body(dz_ref, dq_ref, dkv_ref, wq_ref, wkv_ref, dx_ref, dkvn_ref, dkvb_ref):
        first = pl.program_id(0) == 0
        dkvn = jnp.concatenate([_pair_lanes(dkv_ref[2 * i].astype(CDT), dkv_ref[2 * i + 1].astype(CDT))
                                for i in range(HK // 2)], axis=-1)
        dkvn_ref[...] = dkvn
        dx_ref[...] = ALPHA * dz_ref[...] + _dot_nt(dq_ref[...], wq_ref[...]) + _dot_nt(dkvn, wkv_ref[...])
        for h in range(HK):
            _acc(dkvb_ref.at[h], _colsum(dkv_ref[h]), first)

    row = pl.BlockSpec((tm, D), lambda i: (i, 0))
    return _call(
        after, body, grid=(S // tm,),
        in_specs=[row, row, pl.BlockSpec((HK, tm, ATT_HD), lambda i: (0, i, 0)),
                  pl.BlockSpec((D, D), lambda i: (0, 0)), pl.BlockSpec((D, NK), lambda i: (0, 0))],
        out_specs=[row, pl.BlockSpec((tm, NK), lambda i: (i, 0)), pl.BlockSpec((HK, 1, ATT_HD), lambda i: (0, 0, 0))],
        out_shape=[_sds((S, D), F32), _sds((S, NK), CDT), _sds((HK, 1, ATT_HD), F32)],
        compiler_params=_params(("arbitrary",)), name=name)(dz, dq, dkv4, wq, wkv)


def _inproj_fwd(xb, wain, name):
    S = xb.shape[0]
    tm = _tile(S) // 2
    nb = wain.shape[-1]

    def body(x_ref, w_ref, o_ref):
        x = x_ref[...]
        for j in range(N_DEV):
            o_ref[j // 2, :, pl.ds((j % 2) * nb, nb)] = _dot(x, w_ref[j])

    return _call(
        None, body, grid=(S // tm,),
        in_specs=[pl.BlockSpec((tm, D), lambda i: (i, 0)), pl.BlockSpec((N_DEV, D, nb), lambda i: (0, 0, 0))],
        out_specs=pl.BlockSpec((4, tm, D), lambda i: (0, i, 0)), out_shape=_sds((4, S, D), F32),
        compiler_params=_params(("arbitrary",)), name=name)(xb, wain)


def _inproj_bwd(dz, dproj, wain, name, after=None):
    S = dz.shape[0]
    tm = _tile(S)
    nb = wain.shape[-1]

    def body(dz_ref, dp_ref, w_ref, dx_ref):
        acc = ALPHA * dz_ref[...]
        for j in range(N_DEV):
            acc = acc + _dot_nt(dp_ref[j // 2, :, pl.ds((j % 2) * nb, nb)], w_ref[j])
        dx_ref[...] = acc

    row = pl.BlockSpec((tm, D), lambda i: (i, 0))
    return _call(
        after, body, grid=(S // tm,),
        in_specs=[row, pl.BlockSpec((4, tm, D), lambda i: (0, i, 0)), pl.BlockSpec((N_DEV, D, nb), lambda i: (0, 0, 0))],
        out_specs=row, out_shape=_sds((S, D), F32),
        compiler_params=_params(("arbitrary",)), name=name)(dz, dproj, wain)


def _running_sum(x, reverse=False):
    rows = x.shape[0]
    row = lax.broadcasted_iota(jnp.int32, x.shape, 0)
    step = 1
    while step < rows:
        if reverse:
            x = x + jnp.where(row < rows - step, pltpu.roll(x, rows - step, 0), 0.0)
        else:
            x = x + jnp.where(row >= step, pltpu.roll(x, step, 0), 0.0)
        step *= 2
    return x


def _hg_gates(q, f, alb_ref):
    a0, a1 = alb_ref[0:1, :], alb_ref[1:2, :]
    mx = jnp.maximum(a0, a1)
    e0, e1 = jnp.exp(a0 - mx), jnp.exp(a1 - mx)
    lb = e0 / (e0 + e1)
    sig = _sigmoid(f)
    forget = lb + (1.0 - lb) * sig
    k = (1.0 - lb) * _sigmoid(-f)
    qs = q * _sigmoid(q) * (HG_DK ** -0.5)
    return qs, k, jnp.log(forget), sig, lb, forget


def _hg_intra(qs, k, b, b_scr):
    b_scr[...] = b
    bm = b_scr[pl.ds(HG_CH // 2 - 1, 1), :]
    bl = b_scr[pl.ds(HG_CH - 1, 1), :]
    eb = jnp.exp(b)
    qb = qs * eb
    e_q = jnp.exp(b - bm)
    e_k = jnp.exp(bm - b)
    e_d = jnp.exp(bl - b)
    return qb, qs * e_q, k * e_k, k * e_d, jnp.exp(bl), eb, e_q, e_k, e_d


def _hgrn_fwd(proj, alb, ngain):
    S = proj.shape[1]
    nc = S // HG_CH
    nb = nc // HG_CPB
    rb, wb = HG_CPB * HG_CH, HG_HPB * HG_DK

    def body(pj_ref, alb_ref, ng_ref, o_ref, y_ref, st_ref, st_scr, b_scr):
        n = pl.program_id(1)

        @pl.when(n == 0)
        def _():
            st_scr[...] = jnp.zeros_like(st_scr)

        r = lax.broadcasted_iota(jnp.int32, (HG_CH, HG_CH), 0)
        c = lax.broadcasted_iota(jnp.int32, (HG_CH, HG_CH), 1)
        causal = r >= c
        for ci, j in [(ci, j) for ci in range(HG_CPB) for j in range(HG_HPB)]:
            rows, lanes = pl.ds(ci * HG_CH, HG_CH), pl.ds(j * HG_DK, HG_DK)
            q, f, v, g = pj_ref[0, rows, lanes], pj_ref[1, rows, lanes], pj_ref[2, rows, lanes], pj_ref[3, rows, lanes]
            qs, k, logf, _, _, _ = _hg_gates(q, f, alb_ref.at[:, lanes])
            b = _running_sum(logf)
            qb, qt, kt, kd, ebl, _, _, _, _ = _hg_intra(qs, k, b, b_scr.at[j, ci])
            st = st_scr[j]
            st_ref[j, ci] = st
            a = jnp.where(causal, _dot_nt(qt, kt), 0.0)
            o = _dot(a, v) + _dot_nt(qb, st)
            st_scr[j] = st * ebl + _dot_tn(v, kd)
            o_ref[rows, lanes] = o
            rinv = lax.rsqrt(jnp.mean(o * o, axis=-1, keepdims=True) + RMS_EPS)
            y_ref[rows, lanes] = (o * rinv * ng_ref[...] * (g * _sigmoid(g))).astype(CDT)

    blk = pl.BlockSpec((rb, wb), lambda h, n: (n, h))
    return _call(
        None, body, grid=(HG_H // HG_HPB, nb),
        in_specs=[pl.BlockSpec((4, rb, wb), lambda h, n: (0, n, h)), pl.BlockSpec((2, wb), lambda h, n: (0, h)),
                  pl.BlockSpec((1, HG_DK), lambda h, n: (0, 0))],
        out_specs=[blk, blk, pl.BlockSpec((HG_HPB, HG_CPB, HG_DK, HG_DK), lambda h, n: (h, n, 0, 0))],
        out_shape=[_sds((S, D), F32), _sds((S, D), CDT), _sds((HG_H, nc, HG_DK, HG_DK), F32)],
        scratch_shapes=[pltpu.VMEM((HG_HPB, HG_DK, HG_DK), F32), pltpu.VMEM((HG_HPB, HG_CPB, HG_CH, HG_DK), F32)],
        compiler_params=_params(("arbitrary", "arbitrary")), name="hgrn_fwd")(proj, alb, ngain)


def _hgrn_bwd(proj, alb, ngain, o, states, dy, after=None):
    S = proj.shape[1]
    nc = S // HG_CH
    nb = nc // HG_CPB
    rb, wb = HG_CPB * HG_CH, HG_HPB * HG_DK

    def body(pj_ref, alb_ref, ng_ref, o_ref, st_ref, dy_ref, dpj_ref, dalb_ref, dng_ref, dst_scr, b_scr):
        h, n = pl.program_id(0), pl.program_id(1)

        @pl.when(n == 0)
        def _():
            dst_scr[...] = jnp.zeros_like(dst_scr)
            dalb_ref[...] = jnp.zeros_like(dalb_ref)

        _zero_at(dng_ref, jnp.logical_and(h == 0, n == 0))
        ng = ng_ref[...]
        r = lax.broadcasted_iota(jnp.int32, (HG_CH, HG_CH), 0)
        c = lax.broadcasted_iota(jnp.int32, (HG_CH, HG_CH), 1)
        causal = r >= c
        dng = None
        for ci, j in [(ci, j) for ci in reversed(range(HG_CPB)) for j in range(HG_HPB)]:
            rows, lanes = pl.ds(ci * HG_CH, HG_CH), pl.ds(j * HG_DK, HG_DK)
            q, f, v, g = pj_ref[0, rows, lanes], pj_ref[1, rows, lanes], pj_ref[2, rows, lanes], pj_ref[3, rows, lanes]
            o_ = o_ref[rows, lanes]
            dy_ = dy_ref[rows, lanes]
            sg = _sigmoid(g)
            rinv = lax.rsqrt(jnp.mean(o_ * o_, axis=-1, keepdims=True) + RMS_EPS)
            nrm = o_ * rinv
            dr = dy_ * (g * sg)
            dg = dy_ * nrm * ng * (sg * (1.0 + g * (1.0 - sg)))
            dn = dr * ng
            do = rinv * (dn - nrm * jnp.mean(dn * nrm, axis=-1, keepdims=True))
            dng = _colsum(dr * nrm) if dng is None else dng + _colsum(dr * nrm)
            qs, k, logf, sig, lb, forget = _hg_gates(q, f, alb_ref.at[:, lanes])
            b = _running_sum(logf)
            qb, qt, kt, kd, ebl, eb, e_q, e_k, e_d = _hg_intra(qs, k, b, b_scr.at[j, ci])
            st = st_ref[j, ci]
            dstn = dst_scr[j]
            qt, kt, qb, kd = (t.astype(CDT).astype(F32) for t in (qt, kt, qb, kd))
            a = jnp.where(causal, _dot_nt(qt, kt), 0.0)
            da = jnp.where(causal, _dot_nt(do, v), 0.0)
            dv = _dot_tn(a, do) + _dot_nt(kd, dstn)
            dqb = _dot(do, st)
            dkd = _dot(v, dstn)
            dqt = _dot(da, kt)
            dkt = _dot_tn(da, qt)
            dbl = _colsum(dkd * kd) + ebl * _colsum(dstn * st)
            dst_scr[j] = dstn * ebl + _dot_tn(do, qb)
            dqs = dqt * e_q + dqb * eb
            dk = dkt * e_k + dkd * e_d
            db = dqt * qt + dqb * qb - dkt * kt - dkd * kd
            dlogf = _running_sum(db, reverse=True) + dbl
            dforget = dlogf / forget
            dsig = (1.0 - lb) * (dforget - dk)
            df = dsig * sig * (1.0 - sig)
            dlb = _colsum((dforget - dk) * (1.0 - sig))
            sq = _sigmoid(q)
            dq = dqs * (HG_DK ** -0.5) * (sq * (1.0 + q * (1.0 - sq)))
            dpj_ref[0, rows, lanes] = dq.astype(CDT)
            dpj_ref[1, rows, lanes] = df.astype(CDT)
            dpj_ref[2, rows, lanes] = dv.astype(CDT)
            dpj_ref[3, rows, lanes] = dg.astype(CDT)
            da0 = dlb * lb * (1.0 - lb)
            dalb_ref[pl.ds(0, 1), lanes] += da0
            dalb_ref[pl.ds(1, 1), lanes] -= da0
        dng_ref[...] += dng

    blk = pl.BlockSpec((rb, wb), lambda h, n: (nb - 1 - n, h))
    pj = pl.BlockSpec((4, rb, wb), lambda h, n: (0, nb - 1 - n, h))
    alb_blk = pl.BlockSpec((2, wb), lambda h, n: (0, h))
    ng_blk = pl.BlockSpec((1, HG_DK), lambda h, n: (0, 0))
    return _call(
        after, body, grid=(HG_H // HG_HPB, nb),
        in_specs=[pj, alb_blk, ng_blk, blk,
                  pl.BlockSpec((HG_HPB, HG_CPB, HG_DK, HG_DK), lambda h, n: (h, nb - 1 - n, 0, 0)), blk],
        out_specs=[pj, alb_blk, ng_blk],
        out_shape=[_sds((4, S, D), CDT), _sds((2, D), F32), _sds((1, HG_DK), F32)],
        scratch_shapes=[pltpu.VMEM((HG_HPB, HG_DK, HG_DK), F32), pltpu.VMEM((HG_HPB, HG_CPB, HG_CH, HG_DK), F32)],
        compiler_params=_params(("arbitrary", "arbitrary")), name="hgrn_bwd")(proj, alb, ngain, o, states, dy)


def _slope(h):
    return 2.0 ** (-8.0 * (h + 1) / ATT_QH)


def _attn_mask(n):
    qi = lax.broadcasted_iota(jnp.int32, (WINDOW, 2 * WINDOW), 0)
    si = lax.broadcasted_iota(jnp.int32, (WINDOW, 2 * WINDOW), 1)
    dist = qi - si + WINDOW
    valid = (dist >= 0) & (dist < WINDOW) & (n * WINDOW - WINDOW + si >= 0)
    return valid, dist.astype(F32)


def _attn_probs(qk, sink, slope, valid, distf):
    s = qk * (ATT_HD ** -0.5) - slope * distf
    s = jnp.where(valid, s, NEG)
    m = jnp.maximum(jnp.max(s, axis=-1, keepdims=True), sink)
    e = jnp.exp(s - m)
    es = jnp.exp(sink - m)
    inv = 1.0 / (jnp.sum(e, axis=-1, keepdims=True) + es)
    return e * inv, es * inv


def _attn_specs(S):
    steps = S // (ATT_BPB * WINDOW)
    cur = lambda H: pl.BlockSpec((H, ATT_BPB * WINDOW, ATT_HD), lambda n: (0, n, 0))
    prev = lambda H: pl.BlockSpec((H, WINDOW, ATT_HD), lambda n: (0, jnp.maximum(ATT_BPB * n - 1, 0), 0))
    return steps, cur, prev


def _attn_kv(kvc_ref, kvp_ref, head, bi):
    before = kvp_ref[head] if bi == 0 else kvc_ref[head, pl.ds((bi - 1) * WINDOW, WINDOW), :]
    return jnp.concatenate([before, kvc_ref[head, pl.ds(bi * WINDOW, WINDOW), :]], axis=0)


def _attn_fwd(q4, kv4, sinks):
    S = q4.shape[1]
    nb, cur, prev = _attn_specs(S)

    def body(sink_ref, q_ref, kvc_ref, kvp_ref, o_ref, p_ref, ps_ref):
        lane = lax.broadcasted_iota(jnp.int32, (1, 128), 1)
        place = (_half_place(True), _half_place(False))
        masks = [_attn_mask(pl.program_id(0) * ATT_BPB + bi) for bi in range(ATT_BPB)]
        sink_probs = [jnp.zeros((WINDOW, 128), F32) for _ in range(ATT_BPB)]
        heads = lambda kvh: range(kvh * ATT_G, (kvh + 1) * ATT_G)

        def scores(bi, kvh):
            rows = pl.ds(bi * WINDOW, WINDOW)
            kh = _attn_kv(kvc_ref, kvp_ref, kvh, bi)
            vh = _attn_kv(kvc_ref, kvp_ref, ATT_KVH + kvh, bi)
            v_pl = [jnp.dot(vh, m, preferred_element_type=F32).astype(CDT) for m in place]
            return v_pl, [_dot_nt(q_ref[h, rows, :], kh) for h in heads(kvh)]

        def softmax(bi, kvh, qks):
            rows = pl.ds(bi * WINDOW, WINDOW)
            probs = []
            for h, qk in zip(heads(kvh), qks):
                p, ps = _attn_probs(qk, sink_ref[0, h], _slope(h), *masks[bi])
                probs.append(p.astype(CDT))
                p_ref[h, rows, :] = probs[-1]
                sink_probs[bi] = sink_probs[bi] + jnp.where(lane == h, ps, 0.0)
            return probs

        def weighted_values(bi, kvh, v_pl, probs):
            rows = pl.ds(bi * WINDOW, WINDOW)
            for i in range(ATT_G // 2):
                lanes = pl.ds((kvh * ATT_G + 2 * i) * ATT_HD, 2 * ATT_HD)
                o_ref[rows, lanes] = (_dot(probs[2 * i], v_pl[0]) + _dot(probs[2 * i + 1], v_pl[1])).astype(CDT)

        groups = [(bi, kvh) for bi in range(ATT_BPB) for kvh in range(ATT_KVH)]
        ahead = scores(*groups[0])
        for gi, g in enumerate(groups):
            v_pl, qks = ahead
            probs = softmax(*g, qks)
            if gi + 1 < len(groups):
                ahead = scores(*groups[gi + 1])
            weighted_values(*g, v_pl, probs)
        for bi in range(ATT_BPB):
            ps_ref[pl.ds(bi * WINDOW, WINDOW), :] = sink_probs[bi]

    rows_spec = pl.BlockSpec((ATT_BPB * WINDOW, D), lambda n: (n, 0))
    return _call(
        None, body, grid=(nb,),
        in_specs=[pl.BlockSpec(memory_space=pltpu.SMEM), cur(ATT_QH), cur(2 * ATT_KVH), prev(2 * ATT_KVH)],
        out_specs=[rows_spec, pl.BlockSpec((ATT_QH, ATT_BPB * WINDOW, 2 * WINDOW), lambda n: (0, n, 0)),
                   pl.BlockSpec((ATT_BPB * WINDOW, 128), lambda n: (n, 0))],
        out_shape=[_sds((S, D), CDT), _sds((ATT_QH, S, 2 * WINDOW), CDT), _sds((S, 128), F32)],
        compiler_params=_params(("arbitrary",)), name="attn_fwd")(sinks, q4, kv4, kv4)


def _attn_bwd(q4, kv4, probs, sink_probs, do):
    S = q4.shape[1]
    nb, cur, prev = _attn_specs(S)

    def body(q_ref, kvc_ref, kvp_ref, p_ref, ps_ref, do_ref, dq_ref, dkv_ref, dbq_ref, dsink_ref):
        n = pl.program_id(0)
        first = n == 0

        @pl.when(first)
        def _():
            dkv_ref[...] = jnp.zeros_like(dkv_ref)
            dsink_ref[...] = jnp.zeros_like(dsink_ref)
            dbq_ref[...] = jnp.zeros_like(dbq_ref)

        lane = lax.broadcasted_iota(jnp.int32, (1, 128), 1)
        place = (_half_place(True), _half_place(False))
        row_dots = [jnp.zeros((WINDOW, 128), F32) for _ in range(ATT_BPB)]
        heads = lambda kvh: range(kvh * ATT_G, (kvh + 1) * ATT_G)
        pair_lanes = lambda h: pl.ds((h // 2) * 2 * ATT_HD, 2 * ATT_HD)

        def products_of_do(bi, kvh):
            rows = pl.ds(bi * WINDOW, WINDOW)
            kh = _attn_kv(kvc_ref, kvp_ref, kvh, bi)
            vh = _attn_kv(kvc_ref, kvp_ref, ATT_KVH + kvh, bi)
            k_pl = [jnp.dot(kh, m, preferred_element_type=F32).astype(CDT) for m in place]
            v_pl = [jnp.dot(vh, m, preferred_element_type=F32).astype(CDT) for m in place]
            dps = [_dot_nt(do_ref[rows, pair_lanes(h)], v_pl[h % 2]) for h in heads(kvh)]
            dv2 = [sum(_dot_tn(p_ref[h, rows, :], do_ref[rows, pair_lanes(h)]) for h in heads(kvh) if h % 2 == par)
                   for par in range(2)]
            return k_pl, dps, dv2

        def softmax_bwd(bi, kvh, dps):
            rows = pl.ds(bi * WINDOW, WINDOW)
            dss = []
            for h, dp in zip(heads(kvh), dps):
                p = p_ref[h, rows, :].astype(F32)
                dd = jnp.sum(p * dp, axis=-1, keepdims=True)
                dss.append((p * (dp - dd)).astype(CDT))
                row_dots[bi] = row_dots[bi] + jnp.where(lane == h, dd, 0.0)
            return dss

        def products_of_ds(bi, kvh, k_pl, dss, dv2):
            block = n * ATT_BPB + bi
            rows = pl.ds(bi * WINDOW, WINDOW)
            rows_cur = pl.ds(pl.multiple_of(block * WINDOW, WINDOW), WINDOW)
            rows_prev = pl.ds(pl.multiple_of(jnp.maximum(block - 1, 0) * WINDOW, WINDOW), WINDOW)
            for i in range(ATT_G // 2):
                h = kvh * ATT_G + 2 * i
                dq2 = (_dot(dss[2 * i], k_pl[0]) + _dot(dss[2 * i + 1], k_pl[1])) * (ATT_HD ** -0.5)
                dq_ref[rows, pair_lanes(h)] = dq2.astype(CDT)
                dbq_ref[:, pair_lanes(h)] += _colsum(dq2)
            dk = sum(_dot_tn(ds, q_ref[h, rows, :]) for h, ds in zip(heads(kvh), dss)) * (ATT_HD ** -0.5)
            dv = dv2[0][:, :ATT_HD] + pltpu.roll(dv2[1], ATT_HD, 1)[:, :ATT_HD]
            dkv_ref[kvh, rows_prev, :] += dk[:WINDOW]
            dkv_ref[kvh, rows_cur, :] += dk[WINDOW:]
            dkv_ref[ATT_KVH + kvh, rows_prev, :] += dv[:WINDOW]
            dkv_ref[ATT_KVH + kvh, rows_cur, :] += dv[WINDOW:]

        groups = [(bi, kvh) for bi in range(ATT_BPB) for kvh in range(ATT_KVH)]
        ahead = products_of_do(*groups[0])
        for gi, g in enumerate(groups):
            k_pl, dps, dv2 = ahead
            dss = softmax_bwd(*g, dps)
            if gi + 1 < len(groups):
                ahead = products_of_do(*groups[gi + 1])
            products_of_ds(*g, k_pl, dss, dv2)
        dsinks = jnp.zeros((1, 128), F32)
        for bi in range(ATT_BPB):
            dsinks = dsinks - _colsum(ps_ref[pl.ds(bi * WINDOW, WINDOW), :] * row_dots[bi])
        dsink_ref[...] += dsinks

    rows_spec = pl.BlockSpec((ATT_BPB * WINDOW, D), lambda n: (n, 0))
    return _call(
        None, body, grid=(nb,),
        in_specs=[cur(ATT_QH), cur(2 * ATT_KVH), prev(2 * ATT_KVH),
                  pl.BlockSpec((ATT_QH, ATT_BPB * WINDOW, 2 * WINDOW), lambda n: (0, n, 0)),
                  pl.BlockSpec((ATT_BPB * WINDOW, 128), lambda n: (n, 0)), rows_spec],
        out_specs=[rows_spec, pl.BlockSpec((2 * ATT_KVH, S, ATT_HD), lambda n: (0, 0, 0)),
                   pl.BlockSpec((1, D), lambda n: (0, 0)), pl.BlockSpec((1, 128), lambda n: (0, 0))],
        out_shape=[_sds((S, D), CDT), _sds((2 * ATT_KVH, S, ATT_HD), F32), _sds((1, D), F32),
                   _sds((1, 128), F32)],
        compiler_params=_params(("arbitrary",)), name="attn_bwd")(q4, kv4, kv4, probs, sink_probs, do)


def _local_step(x, p, target, getw, sm, emit):
    S = x.shape[0]
    vec = lambda a: a.reshape(1, -1)
    ln_g = lambda l, k: vec(sm["ln_gain"][l, k])
    ln_b = lambda l, k: vec(sm["ln_bias"][l, k])
    xb = x.astype(CDT)
    pb = p.astype(CDT)

    proj = _inproj_fwd(xb, getw("a_w_in"), "a_in")
    o_a, y_a, states = _hgrn_fwd(proj, sm["a_lower_bound"], sm["a_norm_gain"])
    zeros = jnp.zeros((1, D), F32)
    z = [[None] * 3 for _ in range(2)]
    xs = [[None] * 3 for _ in range(2)]
    xbs = [[None] * 3 for _ in range(2)]
    z[0][0], xs[0][0], xbs[0][0] = _mixout_ln(y_a[None], getw("a_w_out")[None], zeros, x, ln_g(0, 0), ln_b(0, 0),
                                              "a_out_ln")
    gu, hid, sgs, ups = [None, None], [None, None], [None, None], [None, None]

    def ffn_ple(l):
        wgu = getw(f"gu{l}")
        gu[l], hid[l], z[l][1], xs[l][1], xbs[l][1] = _ffn_fwd(
            xs[l][0], xbs[l][0], wgu, getw(f"dn{l}"), ln_g(l, 1), ln_b(l, 1), f"ffn_fwd{l}")
        sgs[l], ups[l], z[l][2], xs[l][2], xbs[l][2] = _ple_fwd(
            xs[l][1], xbs[l][1], pb[l], getw(f"pg{l}"), vec(sm["ple_b_gate"][l]), getw(f"pu{l}"), ln_g(l, 2),
            ln_b(l, 2), f"ple_fwd{l}")

    ffn_ple(0)
    x3, x3b = xs[0][2], xbs[0][2]
    w_kv, w_q, w_bo = getw("kv_w"), getw("b_w_q"), getw("b_w_out")
    kv4 = _proj_heads(x3b, w_kv, vec(sm["kv_b"]), 2 * ATT_KVH, "kv_proj")
    q4 = _proj_heads(x3b, w_q, vec(sm["b_b_q"]), ATT_QH, "q_proj")
    o_b, probs, sink_probs = _attn_fwd(q4, kv4, sm["b_sinks"])
    z[1][0], xs[1][0], xbs[1][0] = _mixout_ln(o_b[None], w_bo[None], sm["b_b_out"], x3, ln_g(1, 0), ln_b(1, 0),
                                              "b_out_ln")
    ffn_ple(1)
    loss, dy = _loss_fwd_bwd(xs[1][2], target)

    gs = {}
    d_ln_g = [[None] * 3 for _ in range(2)]
    d_ln_b = [[None] * 3 for _ in range(2)]
    g_bg = [None, None]

    def ffn_ple_bwd(l, dy, after=None):
        dz2, dzb, dgl, dup, d_ln_g[l][2], d_ln_b[l][2], g_bg[l], d_ln_g[l][1], d_ln_b[l][1] = _ple_bwd(
            dy, z[l][2], sgs[l], ups[l], ln_g(l, 2), getw(f"pg{l}"), z[l][1], ln_g(l, 1), f"ple_bwd{l}", after=after)
        g_pg = _wgrad(xbs[l][1][None], dgl[None], f"g_ple_gate{l}")[0]
        g_pu = _wgrad(pb[l][None], dup[None], f"g_ple_up{l}")[0]
        tok = emit({f"pg{l}": g_pg, f"pu{l}": g_pu})
        tok = tok + emit({f"dn{l}": _wgrad(hid[l], dzb[None], f"g_ffn_down{l}")})
        dgu = _ffn_bwd_hidden(dzb, gu[l], getw(f"dn{l}"), f"ffn_bwd{l}", after=tok)
        g_gu = _wgrad(dgu.reshape(8, S, FFN_B), xbs[l][0][None], f"g_ffn_gate_up{l}")
        tok = emit({f"gu{l}": g_gu})
        dx1 = _ffn_bwd_input(dz2, dgu, getw(f"gu{l}"), f"ffn_bwd{l}", after=tok)
        return dx1, None

    dx1, tok = ffn_ple_bwd(1, dy)
    dz, dzb, do, d_ln_g[1][0], d_ln_b[1][0], gs["b_b_out"] = _mixout_bwd(dx1, z[1][0], ln_g(1, 0), w_bo[None], CDT,
                                                                        "b_out_bwd", after=tok)
    g_bo = _wgrad(o_b[None], dzb[None], "g_b_w_out")[0]
    dq, dkv4, dbq, dsinks = _attn_bwd(q4, kv4, probs, sink_probs, do[0])
    gs["b_b_q"] = dbq
    gs["b_sinks"] = dsinks
    g_q = _wgrad(x3b[None], dq[None], "g_b_w_q")[0]
    dx3, dkv, gs["kv_b"] = _qkv_bwd(dz, dq, dkv4, w_q, w_kv, "qkv_bwd", after=tok)
    g_kv = _wgrad(x3b[None], dkv[None], "g_kv_w")[0]
    tok = emit({"b_w_out": g_bo, "b_w_q": g_q, "kv_w": g_kv})
    dx1, tok = ffn_ple_bwd(0, dx3, tok)
    w_ao = getw("a_w_out")
    dz, dzb, dyr, d_ln_g[0][0], d_ln_b[0][0], _ = _mixout_bwd(dx1, z[0][0], ln_g(0, 0), w_ao[None], F32, "a_out_bwd",
                                                              after=tok)
    g_ao = _wgrad(y_a[None], dzb[None], "g_a_w_out")[0]
    tok = emit({"a_w_out": g_ao})
    dproj, gs["a_lower_bound"], gs["a_norm_gain"] = _hgrn_bwd(proj, sm["a_lower_bound"], sm["a_norm_gain"], o_a, states,
                                                              dyr[0], after=tok)
    tk = lambda t: (None, t, D)
    g_ain = _mm_tn(xb[None], dproj, N_DEV, lambda g, k: (0, k, 0), lambda g, k: (g // 2, k, g % 2),
                   tk, lambda t: (None, t, 512), (N_DEV, D, 512), (None, D, 512), lambda g, k: (g, 0, 0), name="g_a_w_in")
    gs["ple_b_gate"] = jnp.concatenate(g_bg, axis=0)
    gs["ln_gain"] = jnp.stack([jnp.concatenate(r, axis=0) for r in d_ln_g])
    gs["ln_bias"] = jnp.stack([jnp.concatenate(r, axis=0) for r in d_ln_b])
    gs["loss"] = loss
    tok = emit({"a_w_in": g_ain}, small=gs)
    grad_x = _inproj_bwd(dz, dproj, getw("a_w_in"), "a_in_bwd", after=tok)
    return loss, grad_x, gs


def _peer(k):
    x, y, c = lax.axis_index("x"), lax.axis_index("y"), lax.axis_index("c")
    px = 1 - x if k & 4 else x
    py = 1 - y if k & 2 else y
    pc = 1 - c if k & 1 else c
    return (px, py, pc), 4 * px + 2 * py + pc


def _my_index():
    return 4 * lax.axis_index("x") + 2 * lax.axis_index("y") + lax.axis_index("c")


def _piece_copy(mode, src, land, send_sems, recv_sems, t, k, sender, receiver, peer):
    return pltpu.make_async_remote_copy(
        src_ref=src if mode == "gather" else src.at[receiver], dst_ref=land.at[sender],
        send_sem=send_sems.at[t * 7 + k - 1], recv_sem=recv_sems.at[t * 7 + k - 1], device_id=peer, device_id_type=MESH)


def _sequencer_exchange(srcs, modes, name, collective_id, after=None):
    n = len(srcs)
    land_shapes = [((N_DEV,) + a.shape) if mode == "gather" else a.shape for a, mode in zip(srcs, modes)]
    extra = [] if after is None else [after]

    def body(*refs):
        src_refs, land_refs = refs[:n], refs[n + len(extra):2 * n + len(extra)]
        send_sems, recv_sems, local_sems = refs[2 * n + len(extra):]
        barrier = pltpu.get_barrier_semaphore()
        for k in range(1, N_DEV):
            pl.semaphore_signal(barrier, inc=1, device_id=_peer(k)[0], device_id_type=MESH)
        pl.semaphore_wait(barrier, N_DEV - 1)
        me = _my_index()
        local = []
        for i in range(n):
            cp = pltpu.make_async_copy(src_refs[i] if modes[i] == "gather" else src_refs[i].at[me], land_refs[i].at[me],
                                       local_sems.at[i])
            cp.start()
            local.append(cp)
        for k in range(1, N_DEV):
            peer, pid = _peer(k)
            for t in range(n):
                _piece_copy(modes[t], src_refs[t], land_refs[t], send_sems, recv_sems, t, k, me, pid, peer).start()
        for k in range(1, N_DEV):
            peer, pid = _peer(k)
            for t in range(n):
                _piece_copy(modes[t], src_refs[t], land_refs[t], send_sems, recv_sems, t, k, pid, me, peer).wait_recv()
        for k in range(1, N_DEV):
            peer, pid = _peer(k)
            for t in range(n):
                _piece_copy(modes[t], src_refs[t], land_refs[t], send_sems, recv_sems, t, k, me, pid, peer).wait_send()
        for cp in local:
            cp.wait()

    return pl.kernel(
        body, out_type=[_sds(s, a.dtype) for s, a in zip(land_shapes, srcs)],
        mesh=plsc.ScalarSubcoreMesh(axis_name="sequencer", num_cores=1),
        scratch_types=[pltpu.SemaphoreType.DMA((7 * n,)), pltpu.SemaphoreType.DMA((7 * n,)), pltpu.SemaphoreType.DMA((n,))],
        compiler_params=pltpu.CompilerParams(collective_id=collective_id), name=name)(*srcs, *extra)


def _sequencer_gather(srcs, name, collective_id, after=None):
    n = len(srcs)
    extra = [] if after is None else [after]

    def body(*refs):
        src_refs, land_refs = refs[:n], refs[n + len(extra):2 * n + len(extra)]
        send_sems, recv_sems, local_sems = refs[2 * n + len(extra):]
        x, y, c = lax.axis_index("x"), lax.axis_index("y"), lax.axis_index("c")
        sibling = (x, y, 1 - c)
        chips = [(1 - x, y), (x, 1 - y), (1 - x, 1 - y)]
        index = lambda px, py, pc: 4 * px + 2 * py + pc
        barrier = pltpu.get_barrier_semaphore()
        for peer in [sibling] + [(*chip, c) for chip in chips]:
            pl.semaphore_signal(barrier, inc=1, device_id=peer, device_id_type=MESH)
        pl.semaphore_wait(barrier, 4)

        def copy(t, k, slot, to, src=None):
            return pltpu.make_async_remote_copy(
                src_ref=land_refs[t].at[slot] if src is None else src, dst_ref=land_refs[t].at[slot],
                send_sem=send_sems.at[7 * t + k], recv_sem=recv_sems.at[7 * t + k], device_id=to, device_id_type=MESH)

        me = index(x, y, c)
        local = []
        for t in range(n):
            cp = pltpu.make_async_copy(src_refs[t], land_refs[t].at[me], local_sems.at[t])
            cp.start()
            local.append(cp)
        sends = []
        for t in range(n):
            sends.append(copy(t, 0, me, sibling, src=src_refs[t]))
            sends += [copy(t, 1 + j, me, (*chip, c), src=src_refs[t]) for j, chip in enumerate(chips)]
        for cp in sends:
            cp.start()
        for j, chip in enumerate(chips):
            for t in range(n):
                copy(t, 1 + j, index(*chip, c), sibling, src=src_refs[t]).wait_recv()
                passed = copy(t, 4 + j, index(*chip, c), sibling)
                passed.start()
                sends.append(passed)
        for t in range(n):
            copy(t, 0, index(x, y, 1 - c), sibling, src=src_refs[t]).wait_recv()
        for j, chip in enumerate(chips):
            for t in range(n):
                copy(t, 4 + j, index(*chip, 1 - c), sibling, src=src_refs[t]).wait_recv()
        for cp in sends:
            cp.wait_send()
        for cp in local:
            cp.wait()

    return pl.kernel(
        body, out_type=[_sds((N_DEV,) + a.shape, a.dtype) for a in srcs],
        mesh=plsc.ScalarSubcoreMesh(axis_name="sequencer", num_cores=1),
        scratch_types=[pltpu.SemaphoreType.DMA((7 * n,)), pltpu.SemaphoreType.DMA((7 * n,)), pltpu.SemaphoreType.DMA((n,))],
        compiler_params=pltpu.CompilerParams(collective_id=collective_id), name=name)(*srcs, *extra)


def _adamw(w, g, m, v):
    m = ADAM_B1 * m + (1.0 - ADAM_B1) * g
    v = ADAM_B2 * v + (1.0 - ADAM_B2) * (g * g)
    m_hat = m / (1.0 - ADAM_B1 ** ADAM_STEP)
    v_hat = v / (1.0 - ADAM_B2 ** ADAM_STEP)
    delta = -ADAM_LR * (m_hat / (jnp.sqrt(v_hat) + ADAM_EPS) + ADAM_WD * w)
    return delta, m, v


def _adam_big(w, parts, m, v, name, after=None):
    L, R, C = w.shape
    P = parts[0].shape[0]
    tr = _tile(R, (256, 128, 176, 64, 32, 16))
    nr = R // tr

    def body(w_ref, *refs):
        p_refs, (m_ref, v_ref, g_ref, d_ref, mo_ref, vo_ref) = refs[:L], refs[L:]
        for l in range(L):
            @pl.when(pl.program_id(0) == l)
            def _(p_ref=p_refs[l]):
                g = p_ref[0].astype(F32)
                for s in range(1, P):
                    g = g + p_ref[s].astype(F32)
                g_ref[...] = g
                d_ref[...], mo_ref[...], vo_ref[...] = _adamw(w_ref[...], g, m_ref[...], v_ref[...])

    row = pl.BlockSpec((None, tr, C), lambda l, i: (l, i, 0))
    park = lambda l_of: (lambda l, i: (0, jnp.where(l == l_of, i, 0 if l_of else nr - 1), 0))
    return _call(
        after, body, grid=(L, nr),
        in_specs=[row] + [pl.BlockSpec((P, tr, C), park(l)) for l in range(L)] + [row, row],
        out_specs=[row] * 4, out_shape=[_sds((L, R, C), F32)] * 4,
        compiler_params=_params(("arbitrary", "arbitrary")), name=name)(w, *parts, m, v)


SMALL = (("a_lower_bound", (2, 128), (2, D)), ("ln_gain", (3, 2, 128), (6, D)), ("ln_bias", (3, 2, 128), (6, D)),
         ("a_norm_gain", (1, 128), (1, 128)), ("kv_b", (1, 512), (1, 512)), ("b_b_q", (1, D), (1, D)),
         ("b_sinks", (1, ATT_QH), (1, 128)), ("b_b_out", (1, D), (1, D)), ("ple_b_gate", (2, D), (2, D)))


def _adam_small(parts, w, m, v, losses, after=None):
    k = len(SMALL)

    def body(*refs):
        p_refs, w_refs, m_refs, v_refs = refs[:k], refs[k:2 * k], refs[2 * k:3 * k], refs[3 * k:4 * k]
        loss_ref, outs, total_ref = refs[4 * k], refs[4 * k + 1:-1], refs[-1]
        total = loss_ref[0]
        for s in range(1, N_DEV):
            total = total + loss_ref[s]
        total_ref[...] = total
        me = _my_index()
        for i, (_, wshape, pshape) in enumerate(SMALL):
            cols = wshape[-1]
            lanes = slice(None) if cols == pshape[1] else (
                pl.ds(0, cols) if cols < 128 else pl.ds(pl.multiple_of(me * cols, cols), cols))
            at = [(slice(None), slice(None))] if len(wshape) == 2 else [
                (pl.ds(l * wshape[0] + kk, 1), (kk, pl.ds(l, 1), slice(None))) for kk in range(wshape[0]) for l in range(wshape[1])]
            for rows, own in at:
                g = p_refs[i][0, rows, lanes]
                for s in range(1, N_DEV):
                    g = g + p_refs[i][s, rows, lanes]
                g_ref, d_ref, mo_ref, vo_ref = outs[4 * i:4 * i + 4]
                g_ref[own] = g
                d_ref[own], mo_ref[own], vo_ref[own] = _adamw(w_refs[i][own], g, m_refs[i][own], v_refs[i][own])

    full = lambda shape: pl.BlockSpec(shape, lambda: (0,) * len(shape))
    names = [n for n, _, _ in SMALL]
    held = lambda a, ws: a.swapaxes(0, 1) if len(ws) == 3 else a.reshape(ws)
    back = lambda r, n: r.swapaxes(0, 1) if r.ndim == 3 else r.reshape(w[n].shape)
    res = _call(
        after, body,
        in_specs=[full((N_DEV,) + ps) for _, _, ps in SMALL] + [full(ws) for _, ws, _ in SMALL] * 3
        + [full((N_DEV, 1, 128))],
        out_specs=[full(ws) for _, ws, _ in SMALL for _ in range(4)] + [full((1, 128))],
        out_shape=[_sds(ws, F32) for _, ws, _ in SMALL for _ in range(4)] + [_sds((1, 128), F32)], name="adam_small")(
            *[parts[n] for n in names], *[held(a[n], ws) for a in (w, m, v) for n, ws, _ in SMALL], losses)
    return {n: [back(r, n) for r in res[4 * i:4 * i + 4]] for i, n in enumerate(names)}, res[-1][0, 0]


WEIGHTS = ("a_w_in", "a_lower_bound", "a_norm_gain", "a_w_out", "kv_w", "kv_b", "b_w_q", "b_b_q", "b_sinks", "b_w_out",
           "b_b_out", "ffn_w_gate_up", "ffn_w_down", "ple_w_up", "ple_w_gate", "ple_b_gate", "ln_gain", "ln_bias")


GATHER_GROUPS = (("a_w_in",), ("a_w_out", "gu0"), ("dn0", "pu0", "pg0"), ("kv_w", "b_w_q", "b_w_out"), ("gu1",),
                 ("dn1", "pu1", "pg1"))
KERNEL_LAYOUT = {
    "a_w_in": lambda a: a,
    "a_w_out": lambda a: a.reshape(D, D),
    "kv_w": lambda a: a.reshape(D, 2 * ATT_KVH * ATT_HD),
    "b_w_q": lambda a: a.reshape(D, D),
    "b_w_out": lambda a: a.reshape(D, D),
    "gu": lambda a: a.reshape(2, 4, FFN_B, D),
    "dn": lambda a: a.reshape(4, FFN_B, D),
    "pu": lambda a: a,
    "pg": lambda a: a.reshape(D, D),
}
_row_blocks = lambda a: a.reshape(N_DEV, -1, a.shape[-1])
OWNER_BLOCKS = {
    "a_w_in": lambda g: g,
    "a_w_out": _row_blocks,
    "kv_w": _row_blocks,
    "b_w_q": _row_blocks,
    "b_w_out": _row_blocks,
    "gu": lambda g: g,
    "dn": lambda g: _row_blocks(g.reshape(FFN_H, D)),
    "pu": lambda g: g.reshape(PLE_DIM, N_DEV, 128).transpose(1, 0, 2),
    "pg": _row_blocks,
}
ADAM_PARTS = (("kv_w", ("kv_w",)), ("b_w_q", ("b_w_q",)), ("b_w_out", ("b_w_out",)), ("ffn_w_gate_up", ("gu0", "gu1")),
              ("ffn_w_down", ("dn0", "dn1")), ("ple_w_up", ("pu0", "pu1")), ("ple_w_gate", ("pg0", "pg1")),
              ("a_w_out", ("a_w_out",)), ("a_w_in", ("a_w_in",)))


def kernel(x, p, a_w_in, a_lower_bound, a_norm_gain, a_w_out, kv_w, kv_b, b_w_q, b_b_q, b_sinks, b_w_out, b_b_out, ffn_w_gate_up, ffn_w_down, ple_w_up, ple_w_gate, ple_b_gate, ln_gain, ln_bias, loss_target, m_a_w_in, m_a_lower_bound, m_a_norm_gain, m_a_w_out, m_kv_w, m_kv_b, m_b_w_q, m_b_b_q, m_b_sinks, m_b_w_out, m_b_b_out, m_ffn_w_gate_up, m_ffn_w_down, m_ple_w_up, m_ple_w_gate, m_ple_b_gate, m_ln_gain, m_ln_bias, v_a_w_in, v_a_lower_bound, v_a_norm_gain, v_a_w_out, v_kv_w, v_kv_b, v_b_w_q, v_b_b_q, v_b_sinks, v_b_w_out, v_b_b_out, v_ffn_w_gate_up, v_ffn_w_down, v_ple_w_up, v_ple_w_gate, v_ple_b_gate, v_ln_gain, v_ln_bias):
    given = dict(locals())
    w = {n: given[n] for n in WEIGHTS}
    m = {n: given["m_" + n] for n in WEIGHTS}
    v = {n: given["v_" + n] for n in WEIGHTS}
    shards = {"a_w_in": a_w_in[0], "a_w_out": a_w_out[0], "kv_w": kv_w, "b_w_q": b_w_q[0], "b_w_out": b_w_out[0]}
    for l in range(2):
        shards.update({f"gu{l}": ffn_w_gate_up[l].T, f"dn{l}": ffn_w_down[l], f"pu{l}": ple_w_up[l], f"pg{l}": ple_w_gate[l]})
    sharded_small = [a_lower_bound, ln_gain.swapaxes(0, 1), ln_bias.swapaxes(0, 1)]
    gathered = {}
    for gi, g in enumerate(GATHER_GROUPS):
        lands = _sequencer_gather([shards[n].astype(CDT) for n in g] + (sharded_small if gi == 0 else []),
                                  f"gather{gi}", gi)
        for n, a in zip(g, lands):
            gathered[n] = KERNEL_LAYOUT[n.rstrip("01")](a)
        if gi == 0:
            alb = lands[len(g)].transpose(1, 0, 2).reshape(2, D)
            lng, lnb = [a.transpose(2, 1, 0, 3).reshape(2, 3, D) for a in lands[len(g) + 1:]]

    getw = gathered.__getitem__

    sm = {"a_lower_bound": alb, "ln_gain": lng, "ln_bias": lnb,
          "a_norm_gain": a_norm_gain, "kv_b": kv_b, "b_b_q": b_b_q[0], "b_sinks": b_sinks, "b_b_out": b_b_out,
          "ple_b_gate": ple_b_gate}

    scatters, small_parts = [], {}

    def emit(grads, small=None):
        names = list(grads)
        blocks = [OWNER_BLOCKS[n.rstrip("01")](grads[n]) for n in names]
        partials = [] if small is None else [small[n].reshape(ps) for n, _, ps in SMALL] + [small["loss"]]
        lands = _sequencer_exchange(blocks + partials, ["scatter"] * len(blocks) + ["gather"] * len(partials),
                                    f"scatter{len(scatters)}", len(GATHER_GROUPS) + len(scatters))
        scatters.append(dict(zip(names, lands)))
        small_parts.update(zip([n for n, _, _ in SMALL] + ["loss"], lands[len(blocks):]))
        return blocks + (list(scatters[-4].values()) if len(scatters) >= 4 else [])

    loss, grad_x, gs = _local_step(x[0], p[:, 0], loss_target[0], getw, sm, emit)

    out, parts, last = {}, {}, [grad_x]
    for landed in scatters:
        parts.update(landed)
        for n, keys in ADAM_PARTS:
            if n in out or not all(key in parts for key in keys):
                continue
            lrc = (1,) * (3 - w[n].ndim) + w[n].shape
            shard = (lambda a: a.reshape(lrc).swapaxes(1, 2)) if n == "ffn_w_gate_up" else (lambda a: a.reshape(lrc))
            res = _adam_big(shard(w[n]), [parts[key] for key in keys], shard(m[n]), shard(v[n]), "adam_" + n, after=last)
            out[n] = [(r.swapaxes(1, 2) if n == "ffn_w_gate_up" else r).reshape(w[n].shape) for r in res]
            last = [res[3]]
    small_out, loss = _adam_small(small_parts, w, m, v, small_parts["loss"], after=last)
    out.update(small_out)
    res = [loss, grad_x[None]]
    for i in range(4):
        res += [out[n][i] for n in WEIGHTS]
    return tuple(res)
```

```python
import jax
import jax.numpy as jnp
from jax import lax
from jax.experimental import pallas as pl
from jax.experimental.pallas import tpu as pltpu
from jax.experimental.pallas import tpu_sc as plsc

F32 = jnp.float32
CDT = jnp.bfloat16

N_DEV = 8
D = 1024
HG_H, HG_DK, HG_CH = 8, 128, 64
HG_HPB = 4
HG_CPB = 8
ATT_HD, ATT_QH, ATT_KVH, ATT_G, WINDOW = 64, 16, 4, 4, 128
ATT_BPB = 1
FFN_H = 2816
FFN_B = FFN_H // 4
PLE_DIM = 256
ALPHA = (2.0 * 2) ** 0.25
LN_EPS = 1e-5
RMS_EPS = 1e-6
ADAM_LR, ADAM_B1, ADAM_B2, ADAM_EPS, ADAM_WD, ADAM_STEP = 0.001, 0.9, 0.999, 1e-08, 0.01, 10
ROW_TILES = (512, 256, 128, 64)
VMEM_LIMIT = 48 * 1024 * 1024
NEG = -1e30

MESH = pl.DeviceIdType.MESH


def _tile(n, cands=ROW_TILES):
    for t in cands:
        if n % t == 0:
            return t
    return n


def _sds(shape, dtype):
    return jax.ShapeDtypeStruct(tuple(shape), dtype)


def _params(sem):
    return pltpu.CompilerParams(dimension_semantics=sem, vmem_limit_bytes=VMEM_LIMIT)


def _dot(a, b):
    return jnp.dot(a.astype(CDT), b.astype(CDT), preferred_element_type=F32)


def _dot_nt(a, b):
    return lax.dot_general(a.astype(CDT), b.astype(CDT), (((1,), (1,)), ((), ())), preferred_element_type=F32)


def _dot_tn(a, b):
    return lax.dot_general(a.astype(CDT), b.astype(CDT), (((0,), (0,)), ((), ())), preferred_element_type=F32)


def _sigmoid(x):
    return jax.nn.sigmoid(x)


def _ln_fwd(z, g, b):
    mu = jnp.mean(z, axis=-1, keepdims=True)
    zc = z - mu
    var = jnp.mean(zc * zc, axis=-1, keepdims=True)
    return zc * lax.rsqrt(var + LN_EPS) * g + b


def _ln_bwd(z, g, dy):
    mu = jnp.mean(z, axis=-1, keepdims=True)
    zc = z - mu
    var = jnp.mean(zc * zc, axis=-1, keepdims=True)
    rstd = lax.rsqrt(var + LN_EPS)
    xhat = zc * rstd
    dxh = dy * g
    dz = rstd * (dxh - jnp.mean(dxh, axis=-1, keepdims=True) - xhat * jnp.mean(dxh * xhat, axis=-1, keepdims=True))
    return dz, xhat


def _colsum(x):
    return jnp.sum(x, axis=0, keepdims=True)


def _acc(ref, val, first):
    @pl.when(first)
    def _():
        ref[...] = val

    @pl.when(jnp.logical_not(first))
    def _():
        ref[...] += val


def _zero_at(ref, first):
    @pl.when(first)
    def _():
        ref[...] = jnp.zeros_like(ref)


def _call(after, body, **kw):
    after = [] if after is None else list(after)
    specs = list(kw["in_specs"])
    kw["in_specs"] = [pl.BlockSpec(memory_space=pl.ANY)] * len(after) + specs

    def ordered_body(*refs):
        body(*refs[len(after):])

    call = pl.pallas_call(ordered_body, **kw)
    return lambda *args: call(*after, *args)


def _mm_tn(a3, b3, G, amap, bmap, ablock, bblock, out_shape, oblock, omap, name="mm_tn"):
    S = a3.shape[1]

    def body(a_ref, b_ref, o_ref):
        o_ref[...] = _dot_tn(a_ref[...], b_ref[...]).astype(o_ref.dtype)

    return _call(
        None, body, grid=(G, 1),
        in_specs=[pl.BlockSpec(ablock(S), amap), pl.BlockSpec(bblock(S), bmap)],
        out_specs=pl.BlockSpec(oblock, omap), out_shape=_sds(out_shape, CDT),
        compiler_params=_params(("arbitrary", "arbitrary")), name=name)(a3, b3)


def _wgrad(a3, b3, name):
    Ga, S, M = a3.shape
    Gb, _, N = b3.shape
    G = max(Ga, Gb)
    return _mm_tn(
        a3, b3, G,
        (lambda g, k: (g, k, 0)) if Ga > 1 else (lambda g, k: (0, k, 0)),
        (lambda g, k: (g, k, 0)) if Gb > 1 else (lambda g, k: (0, k, 0)),
        lambda tk: (None, tk, M), lambda tk: (None, tk, N),
        (G, M, N), (None, M, N), lambda g, k: (g, 0, 0), name=name)


def _mixout_ln(u3, w3, bias, xin, gain, beta, name):
    G, S, Kb = u3.shape
    tm = _tile(S)

    def body(u_ref, w_ref, b_ref, x_ref, g_ref, be_ref, z_ref, xo_ref, xob_ref):
        h = b_ref[...] + _dot(u_ref[0], w_ref[0])
        for g in range(1, G):
            h = h + _dot(u_ref[g], w_ref[g])
        z = ALPHA * x_ref[...] + h
        z_ref[...] = z
        y = _ln_fwd(z, g_ref[...], be_ref[...])
        xo_ref[...] = y
        xob_ref[...] = y.astype(CDT)

    row = pl.BlockSpec((tm, D), lambda i: (i, 0))
    vec = pl.BlockSpec((1, D), lambda i: (0, 0))
    return _call(
        None, body, grid=(S // tm,),
        in_specs=[pl.BlockSpec((G, tm, Kb), lambda i: (0, i, 0)), pl.BlockSpec((G, Kb, D), lambda i: (0, 0, 0)),
                  vec, row, vec, vec],
        out_specs=[row, row, row], out_shape=[_sds((S, D), F32), _sds((S, D), F32), _sds((S, D), CDT)],
        compiler_params=_params(("arbitrary",)), name=name)(u3, w3, bias, xin, gain, beta)


def _ffn_fwd(xin, xin_b, wgu, wdn, gain, beta, name):
    S = xin.shape[0]
    tm = _tile(S)

    def hidden(xb_ref, wgu_ref, gu_ref, hid_ref):
        xb = xb_ref[...]
        gate = _dot_nt(xb, wgu_ref[0])
        up = _dot_nt(xb, wgu_ref[1])
        gu_ref[0] = gate.astype(CDT)
        gu_ref[1] = up.astype(CDT)
        hid_ref[...] = (gate * _sigmoid(gate) * up).astype(CDT)

    gu, hid = _call(
        None, hidden, grid=(4, S // tm),
        in_specs=[pl.BlockSpec((tm, D), lambda j, i: (i, 0)), pl.BlockSpec((2, None, FFN_B, D), lambda j, i: (0, j, 0, 0))],
        out_specs=[pl.BlockSpec((2, None, tm, FFN_B), lambda j, i: (0, j, i, 0)),
                   pl.BlockSpec((None, tm, FFN_B), lambda j, i: (j, i, 0))],
        out_shape=[_sds((2, 4, S, FFN_B), CDT), _sds((4, S, FFN_B), CDT)],
        compiler_params=_params(("arbitrary", "arbitrary")), name=name + "_hidden")(xin_b, wgu)

    def down(x_ref, hid_ref, wdn_ref, g_ref, be_ref, z_ref, xo_ref, xob_ref):
        z = ALPHA * x_ref[...]
        for j in range(4):
            z = z + _dot(hid_ref[j], wdn_ref[j])
        z_ref[...] = z
        y = _ln_fwd(z, g_ref[...], be_ref[...])
        xo_ref[...] = y
        xob_ref[...] = y.astype(CDT)

    row = pl.BlockSpec((tm, D), lambda i: (i, 0))
    vec = pl.BlockSpec((1, D), lambda i: (0, 0))
    z, xo, xob = _call(
        None, down, grid=(S // tm,),
        in_specs=[row, pl.BlockSpec((4, tm, FFN_B), lambda i: (0, i, 0)), pl.BlockSpec((4, FFN_B, D), lambda i: (0, 0, 0)),
                  vec, vec],
        out_specs=[row, row, row], out_shape=[_sds((S, D), F32), _sds((S, D), F32), _sds((S, D), CDT)],
        compiler_params=_params(("arbitrary",)), name=name + "_down")(xin, hid, wdn, gain, beta)
    return gu, hid, z, xo, xob


def _ple_fwd(xin, xin_b, p_b, wpg, bgate, wpu, gain, beta, name, target=None):
    S = xin.shape[0]
    tm = _tile(S)

    def body(x_ref, xb_ref, p_ref, wpg_ref, bg_ref, wpu_ref, g_ref, be_ref, *rest):
        if target is None:
            sg_ref, up_ref, z_ref, xo_ref, xob_ref = rest
        else:
            t_ref, sg_ref, up_ref, z_ref, dy_ref, l_ref = rest
        sg =_sigmoid(_dot(xb_ref[...], wpg_ref[...]) + bg_ref[...])
        pb = p_ref[...]
        up = jnp.concatenate([_dot(pb, wpu_ref[j]) for j in range(N_DEV)], axis=-1)
        sg_ref[...] = sg.astype(CDT)
        up_ref[...] = (up * sg * (1.0 - sg)).astype(CDT)
        z = ALPHA * x_ref[...] + sg * up
        z_ref[...] = z
        y = _ln_fwd(z, g_ref[...], be_ref[...])
        if target is None:
            xo_ref[...] = y
            xob_ref[...] = y.astype(CDT)
        else:
            e = y - t_ref[...]
            dy_ref[...] = e * (1.0 / D)
            part = 0.5 * jnp.sum(jnp.sum(e * e, axis=-1, keepdims=True) * (1.0 / D), axis=0, keepdims=True)
            _acc(l_ref, jnp.broadcast_to(part, l_ref.shape), pl.program_id(0) == 0)

    row = pl.BlockSpec((tm, D), lambda i: (i, 0))
    vec = pl.BlockSpec((1, D), lambda i: (0, 0))
    last = target is not None
    return _call(
        None, body, grid=(S // tm,),
        in_specs=[row, row, pl.BlockSpec((tm, PLE_DIM), lambda i: (i, 0)), pl.BlockSpec((D, D), lambda i: (0, 0)), vec,
                  pl.BlockSpec((N_DEV, PLE_DIM, D // N_DEV), lambda i: (0, 0, 0)), vec, vec] + [row] * last,
        out_specs=[row] * 4 + [pl.BlockSpec((1, 128), lambda i: (0, 0)) if last else row],
        out_shape=[_sds((S, D), CDT)] * 2 + [_sds((S, D), F32)] * 2 + [_sds((1, 128), F32) if last else _sds((S, D), CDT)],
        compiler_params=_params(("arbitrary",)), name=name)(
            xin, xin_b, p_b, wpg, bgate, wpu, gain, beta, *([target] if last else []))


def _ple_bwd(dy, z, sg, up, gain, wpg, z_ffn, gain_ffn, name, after=None):
    S = dy.shape[0]
    tm = _tile(S)

    def body(dy_ref, z_ref, sg_ref, up_ref, g_ref, wpg_ref, zf_ref, gf_ref, dzf_ref, dzfb_ref, dgl_ref, dup_ref,
             dgain_ref, dbeta_ref, dbg_ref, dgainf_ref, dbetaf_ref):
        first = pl.program_id(0) == 0
        dy_ = dy_ref[...]
        dz, xhat = _ln_bwd(z_ref[...], g_ref[...], dy_)
        dgl = dz * up_ref[...].astype(F32)
        dgl_ref[...] = dgl.astype(CDT)
        dup_ref[...] = (dz * sg_ref[...].astype(F32)).astype(CDT)
        dx = ALPHA * dz + _dot_nt(dgl, wpg_ref[...])
        dzf, xhatf = _ln_bwd(zf_ref[...], gf_ref[...], dx)
        dzf_ref[...] = dzf
        dzfb_ref[...] = dzf.astype(CDT)
        _acc(dgain_ref, _colsum(dy_ * xhat), first)
        _acc(dbeta_ref, _colsum(dy_), first)
        _acc(dbg_ref, _colsum(dgl), first)
        _acc(dgainf_ref, _colsum(dx * xhatf), first)
        _acc(dbetaf_ref, _colsum(dx), first)

    row = pl.BlockSpec((tm, D), lambda i: (i, 0))
    vec = pl.BlockSpec((1, D), lambda i: (0, 0))
    return _call(
        after, body, grid=(S // tm,),
        in_specs=[row, row, row, row, vec, pl.BlockSpec((D, D), lambda i: (0, 0)), row, vec],
        out_specs=[row, row, row, row] + [vec] * 5,
        out_shape=[_sds((S, D), F32)] + [_sds((S, D), CDT)] * 3 + [_sds((1, D), F32)] * 5,
        compiler_params=_params(("arbitrary",)), name=name)(dy, z, sg, up, gain, wpg, z_ffn, gain_ffn)


def _ffn_bwd_hidden(dzb, gu, wdn, name, after=None):
    S = dzb.shape[0]
    tm = _tile(S)

    def hidden(dzb_ref, gu_ref, wdn_ref, dgu_ref):
        dhid = _dot_nt(dzb_ref[...], wdn_ref[...])
        gate, up = gu_ref[0].astype(F32), gu_ref[1].astype(F32)
        sg = _sigmoid(gate)
        dgu_ref[0] = (dhid * up * (sg * (1.0 + gate * (1.0 - sg)))).astype(CDT)
        dgu_ref[1] = (dhid * (gate * sg)).astype(CDT)

    blocks = pl.BlockSpec((2, None, tm, FFN_B), lambda j, i: (0, j, i, 0))
    return _call(
        after, hidden, grid=(4, S // tm),
        in_specs=[pl.BlockSpec((tm, D), lambda j, i: (i, 0)), blocks, pl.BlockSpec((None, FFN_B, D), lambda j, i: (j, 0, 0))],
        out_specs=blocks, out_shape=_sds((2, 4, S, FFN_B), CDT),
        compiler_params=_params(("arbitrary", "arbitrary")), name=name + "_hidden")(dzb, gu, wdn)


def _ffn_bwd_input(dz, dgu, wgu, name, after=None):
    S = dz.shape[0]
    tm = _tile(S)

    def to_input(dz_ref, dgu_ref, wgu_ref, dx_ref):
        acc = ALPHA * dz_ref[...]
        for g in range(2):
            for j in range(4):
                acc = acc + _dot(dgu_ref[g, j], wgu_ref[g, j])
        dx_ref[...] = acc

    rows = pl.BlockSpec((tm, D), lambda i: (i, 0))
    return _call(
        after, to_input, grid=(S // tm,),
        in_specs=[rows, pl.BlockSpec((2, 4, tm, FFN_B), lambda i: (0, 0, i, 0)),
                  pl.BlockSpec((2, 4, FFN_B, D), lambda i: (0, 0, 0, 0))],
        out_specs=rows, out_shape=_sds((S, D), F32),
        compiler_params=_params(("arbitrary",)), name=name + "_input")(dz, dgu, wgu)


def _mixout_bwd(dy, z, gain, w3, du_dtype, name, after=None):
    S = dy.shape[0]
    G, Kb, _ = w3.shape
    tm = _tile(S)

    def body(dy_ref, z_ref, g_ref, w_ref, dz_ref, dzb_ref, du_ref, dgain_ref, dbeta_ref, dbias_ref):
        first = pl.program_id(0) == 0
        dy_ = dy_ref[...]
        dz, xhat = _ln_bwd(z_ref[...], g_ref[...], dy_)
        dz_ref[...] = dz
        dzb = dz.astype(CDT)
        dzb_ref[...] = dzb
        for g in range(G):
            du_ref[g] = _dot_nt(dzb, w_ref[g]).astype(du_ref.dtype)
        _acc(dgain_ref, _colsum(dy_ * xhat), first)
        _acc(dbeta_ref, _colsum(dy_), first)
        _acc(dbias_ref, _colsum(dz), first)

    row = pl.BlockSpec((tm, D), lambda i: (i, 0))
    vec = pl.BlockSpec((1, D), lambda i: (0, 0))
    return _call(
        after, body, grid=(S // tm,), in_specs=[row, row, vec, pl.BlockSpec((G, Kb, D), lambda i: (0, 0, 0))],
        out_specs=[row, row, pl.BlockSpec((G, tm, Kb), lambda i: (0, i, 0)), vec, vec, vec],
        out_shape=[_sds((S, D), F32), _sds((S, D), CDT), _sds((G, S, Kb), du_dtype)] + [_sds((1, D), F32)] * 3,
        compiler_params=_params(("arbitrary",)), name=name)(dy, z, gain, w3)


def _half_select(low):
    r = lax.broadcasted_iota(jnp.int32, (2 * ATT_HD, ATT_HD), 0)
    c = lax.broadcasted_iota(jnp.int32, (2 * ATT_HD, ATT_HD), 1)
    return (r == c + (0 if low else ATT_HD)).astype(CDT)


def _half_place(low):
    r = lax.broadcasted_iota(jnp.int32, (ATT_HD, 2 * ATT_HD), 0)
    c = lax.broadcasted_iota(jnp.int32, (ATT_HD, 2 * ATT_HD), 1)
    return (c == r + (0 if low else ATT_HD)).astype(CDT)


def _pair_lanes(even, odd):
    return (jnp.dot(even, _half_place(True), preferred_element_type=F32)
            + jnp.dot(odd, _half_place(False), preferred_element_type=F32)).astype(CDT)


def _proj_heads(a, w, bias, heads, name):
    S, K = a.shape
    N = heads * ATT_HD
    tm = _tile(S)

    def body(a_ref, w_ref, b_ref, o_ref):
        acc = (_dot(a_ref[...], w_ref[...]) + b_ref[...]).astype(CDT)
        sel = (_half_select(True), _half_select(False))
        for h in range(heads):
            pair = acc[:, (h // 2) * 2 * ATT_HD:(h // 2 + 1) * 2 * ATT_HD]
            o_ref[h] = jnp.dot(pair, sel[h % 2], preferred_element_type=F32).astype(CDT)

    return _call(
        None, body, grid=(S // tm,),
        in_specs=[pl.BlockSpec((tm, K), lambda i: (i, 0)), pl.BlockSpec((K, N), lambda i: (0, 0)),
                  pl.BlockSpec((1, N), lambda i: (0, 0))],
        out_specs=pl.BlockSpec((heads, tm, ATT_HD), lambda i: (0, i, 0)), out_shape=_sds((heads, S, ATT_HD), CDT),
        compiler_params=_params(("arbitrary",)), name=name)(a, w, bias)


def _qkv_bwd(dz, dq, dkv4, wq, wkv, name, after=None):
    S = dz.shape[0]
    tm = _tile(S)
    HK = dkv4.shape[0]
    NK = HK * ATT_HD

    def body(dz_ref, dq_ref, dkv_ref, wq_ref, wkv_ref, dx_ref, dkvn_ref, dkvb_ref):
        first = pl.program_id(0) == 0
        dkvn = jnp.concatenate([_pair_lanes(dkv_ref[2 * i].astype(CDT), dkv_ref[2 * i + 1].astype(CDT))
                                for i in range(HK // 2)], axis=-1)
        dkvn_ref[...] = dkvn
        dx_ref[...] = ALPHA * dz_ref[...] + _dot_nt(dq_ref[...], wq_ref[...]) + _dot_nt(dkvn, wkv_ref[...])
        for h in range(HK):
            _acc(dkvb_ref.at[h], _colsum(dkv_ref[h]), first)

    row = pl.BlockSpec((tm, D), lambda i: (i, 0))
    return _call(
        after, body, grid=(S // tm,),
        in_specs=[row, row, pl.BlockSpec((HK, tm, ATT_HD), lambda i: (0, i, 0)),
                  pl.BlockSpec((D, D), lambda i: (0, 0)), pl.BlockSpec((D, NK), lambda i: (0, 0))],
        out_specs=[row, pl.BlockSpec((tm, NK), lambda i: (i, 0)), pl.BlockSpec((HK, 1, ATT_HD), lambda i: (0, 0, 0))],
        out_shape=[_sds((S, D), F32), _sds((S, NK), CDT), _sds((HK, 1, ATT_HD), F32)],
        compiler_params=_params(("arbitrary",)), name=name)(dz, dq, dkv4, wq, wkv)


def _inproj_fwd(xb, wain, name):
    S = xb.shape[0]
    tm = _tile(S) // 2
    nb = wain.shape[-1]

    def body(x_ref, w_ref, o_ref):
        x = x_ref[...]
        for j in range(N_DEV):
            o_ref[j // 2, :, pl.ds((j % 2) * nb, nb)] = _dot(x, w_ref[j])

    return _call(
        None, body, grid=(S // tm,),
        in_specs=[pl.BlockSpec((tm, D), lambda i: (i, 0)), pl.BlockSpec((N_DEV, D, nb), lambda i: (0, 0, 0))],
        out_specs=pl.BlockSpec((4, tm, D), lambda i: (0, i, 0)), out_shape=_sds((4, S, D), F32),
        compiler_params=_params(("arbitrary",)), name=name)(xb, wain)


def _inproj_bwd(dz, dproj, wain, name, after=None):
    S = dz.shape[0]
    tm = _tile(S)
    nb = wain.shape[-1]

    def body(dz_ref, dp_ref, w_ref, dx_ref):
        acc = ALPHA * dz_ref[...]
        for j in range(N_DEV):
            acc = acc + _dot_nt(dp_ref[j // 2, :, pl.ds((j % 2) * nb, nb)], w_ref[j])
        dx_ref[...] = acc

    row = pl.BlockSpec((tm, D), lambda i: (i, 0))
    return _call(
        after, body, grid=(S // tm,),
        in_specs=[row, pl.BlockSpec((4, tm, D), lambda i: (0, i, 0)), pl.BlockSpec((N_DEV, D, nb), lambda i: (0, 0, 0))],
        out_specs=row, out_shape=_sds((S, D), F32),
        compiler_params=_params(("arbitrary",)), name=name)(dz, dproj, wain)


def _running_sum(x, reverse=False):
    rows = x.shape[0]
    row = lax.broadcasted_iota(jnp.int32, x.shape, 0)
    step = 1
    while step < rows:
        if reverse:
            x = x + jnp.where(row < rows - step, pltpu.roll(x, rows - step, 0), 0.0)
        else:
            x = x + jnp.where(row >= step, pltpu.roll(x, step, 0), 0.0)
        step *= 2
    return x


def _hg_gates(q, f, alb_ref):
    a0, a1 = alb_ref[0:1, :], alb_ref[1:2, :]
    mx = jnp.maximum(a0, a1)
    e0, e1 = jnp.exp(a0 - mx), jnp.exp(a1 - mx)
    lb = e0 / (e0 + e1)
    sig = _sigmoid(f)
    forget = lb + (1.0 - lb) * sig
    k = (1.0 - lb) * _sigmoid(-f)
    qs = q * _sigmoid(q) * (HG_DK ** -0.5)
    return qs, k, jnp.log(forget), sig, lb, forget


def _hg_intra(qs, k, b, b_scr):
    b_scr[...] = b
    bm = b_scr[pl.ds(HG_CH // 2 - 1, 1), :]
    bl = b_scr[pl.ds(HG_CH - 1, 1), :]
    eb = jnp.exp(b)
    qb = qs * eb
    e_q = jnp.exp(b - bm)
    e_k = jnp.exp(bm - b)
    e_d = jnp.exp(bl - b)
    return qb, qs * e_q, k * e_k, k * e_d, jnp.exp(bl), eb, e_q, e_k, e_d


def _hgrn_fwd(proj, alb, ngain):
    S = proj.shape[1]
    nc = S // HG_CH
    nb = nc // HG_CPB
    rb, wb = HG_CPB * HG_CH, HG_HPB * HG_DK

    def body(pj_ref, alb_ref, ng_ref, o_ref, y_ref, st_ref, st_scr, b_scr):
        n = pl.program_id(1)

        @pl.when(n == 0)
        def _():
            st_scr[...] = jnp.zeros_like(st_scr)

        r = lax.broadcasted_iota(jnp.int32, (HG_CH, HG_CH), 0)
        c = lax.broadcasted_iota(jnp.int32, (HG_CH, HG_CH), 1)
        causal = r >= c
        for ci, j in [(ci, j) for ci in range(HG_CPB) for j in range(HG_HPB)]:
            rows, lanes = pl.ds(ci * HG_CH, HG_CH), pl.ds(j * HG_DK, HG_DK)
            q, f, v, g = pj_ref[0, rows, lanes], pj_ref[1, rows, lanes], pj_ref[2, rows, lanes], pj_ref[3, rows, lanes]
            qs, k, logf, _, _, _ = _hg_gates(q, f, alb_ref.at[:, lanes])
            b = _running_sum(logf)
            qb, qt, kt, kd, ebl, _, _, _, _ = _hg_intra(qs, k, b, b_scr.at[j, ci])
            st = st_scr[j]
            st_ref[j, ci] = st
            a = jnp.where(causal, _dot_nt(qt, kt), 0.0)
            o = _dot(a, v) + _dot_nt(qb, st)
            st_scr[j] = st * ebl + _dot_tn(v, kd)
            o_ref[rows, lanes] = o
            rinv = lax.rsqrt(jnp.mean(o * o, axis=-1, keepdims=True) + RMS_EPS)
            y_ref[rows, lanes] = (o * rinv * ng_ref[...] * (g * _sigmoid(g))).astype(CDT)

    blk = pl.BlockSpec((rb, wb), lambda h, n: (n, h))
    return _call(
        None, body, grid=(HG_H // HG_HPB, nb),
        in_specs=[pl.BlockSpec((4, rb, wb), lambda h, n: (0, n, h)), pl.BlockSpec((2, wb), lambda h, n: (0, h)),
                  pl.BlockSpec((1, HG_DK), lambda h, n: (0, 0))],
        out_specs=[blk, blk, pl.BlockSpec((HG_HPB, HG_CPB, HG_DK, HG_DK), lambda h, n: (h, n, 0, 0))],
        out_shape=[_sds((S, D), F32), _sds((S, D), CDT), _sds((HG_H, nc, HG_DK, HG_DK), F32)],
        scratch_shapes=[pltpu.VMEM((HG_HPB, HG_DK, HG_DK), F32), pltpu.VMEM((HG_HPB, HG_CPB, HG_CH, HG_DK), F32)],
        compiler_params=_params(("arbitrary", "arbitrary")), name="hgrn_fwd")(proj, alb, ngain)


def _hgrn_bwd(proj, alb, ngain, o, states, dy, after=None):
    S = proj.shape[1]
    nc = S // HG_CH
    nb = nc // HG_CPB
    rb, wb = HG_CPB * HG_CH, HG_HPB * HG_DK

    def body(pj_ref, alb_ref, ng_ref, o_ref, st_ref, dy_ref, dpj_ref, dalb_ref, dng_ref, dst_scr, b_scr):
        h, n = pl.program_id(0), pl.program_id(1)

        @pl.when(n == 0)
        def _():
            dst_scr[...] = jnp.zeros_like(dst_scr)
            dalb_ref[...] = jnp.zeros_like(dalb_ref)

        _zero_at(dng_ref, jnp.logical_and(h == 0, n == 0))
        ng = ng_ref[...]
        r = lax.broadcasted_iota(jnp.int32, (HG_CH, HG_CH), 0)
        c = lax.broadcasted_iota(jnp.int32, (HG_CH, HG_CH), 1)
        causal = r >= c
        dng = None
        for ci, j in [(ci, j) for ci in reversed(range(HG_CPB)) for j in range(HG_HPB)]:
            rows, lanes = pl.ds(ci * HG_CH, HG_CH), pl.ds(j * HG_DK, HG_DK)
            q, f, v, g = pj_ref[0, rows, lanes], pj_ref[1, rows, lanes], pj_ref[2, rows, lanes], pj_ref[3, rows, lanes]
            o_ = o_ref[rows, lanes]
            dy_ = dy_ref[rows, lanes]
            sg = _sigmoid(g)
            rinv = lax.rsqrt(jnp.mean(o_ * o_, axis=-1, keepdims=True) + RMS_EPS)
            nrm = o_ * rinv
            dr = dy_ * (g * sg)
            dg = dy_ * nrm * ng * (sg * (1.0 + g * (1.0 - sg)))
            dn = dr * ng
            do = rinv * (dn - nrm * jnp.mean(dn * nrm, axis=-1, keepdims=True))
            dng = _colsum(dr * nrm) if dng is None else dng + _colsum(dr * nrm)
            qs, k, logf, sig, lb, forget = _hg_gates(q, f, alb_ref.at[:, lanes])
            b = _running_sum(logf)
            qb, qt, kt, kd, ebl, eb, e_q, e_k, e_d = _hg_intra(qs, k, b, b_scr.at[j, ci])
            st = st_ref[j, ci]
            dstn = dst_scr[j]
            qt, kt, qb, kd = (t.astype(CDT).astype(F32) for t in (qt, kt, qb, kd))
            a = jnp.where(causal, _dot_nt(qt, kt), 0.0)
            da = jnp.where(causal, _dot_nt(do, v), 0.0)
            dv = _dot_tn(a, do) + _dot_nt(kd, dstn)
            dqb = _dot(do, st)
            dkd = _dot(v, dstn)
            dqt = _dot(da, kt)
            dkt = _dot_tn(da, qt)
            dbl = _colsum(dkd * kd) + ebl * _colsum(dstn * st)
            dst_scr[j] = dstn * ebl + _dot_tn(do, qb)
            dqs = dqt * e_q + dqb * eb
            dk = dkt * e_k + dkd * e_d
            db = dqt * qt + dqb * qb - dkt * kt - dkd * kd
            dlogf = _running_sum(db, reverse=True) + dbl
            dforget = dlogf / forget
            dsig = (1.0 - lb) * (dforget - dk)
            df = dsig * sig * (1.0 - sig)
            dlb = _colsum((dforget - dk) * (1.0 - sig))
            sq = _sigmoid(q)
            dq = dqs * (HG_DK ** -0.5) * (sq * (1.0 + q * (1.0 - sq)))
            dpj_ref[0, rows, lanes] = dq.astype(CDT)
            dpj_ref[1, rows, lanes] = df.astype(CDT)
            dpj_ref[2, rows, lanes] = dv.astype(CDT)
            dpj_ref[3, rows, lanes] = dg.astype(CDT)
            da0 = dlb * lb * (1.0 - lb)
            dalb_ref[pl.ds(0, 1), lanes] += da0
            dalb_ref[pl.ds(1, 1), lanes] -= da0
        dng_ref[...] += dng

    blk = pl.BlockSpec((rb, wb), lambda h, n: (nb - 1 - n, h))
    pj = pl.BlockSpec((4, rb, wb), lambda h, n: (0, nb - 1 - n, h))
    alb_blk = pl.BlockSpec((2, wb), lambda h, n: (0, h))
    ng_blk = pl.BlockSpec((1, HG_DK), lambda h, n: (0, 0))
    return _call(
        after, body, grid=(HG_H // HG_HPB, nb),
        in_specs=[pj, alb_blk, ng_blk, blk,
                  pl.BlockSpec((HG_HPB, HG_CPB, HG_DK, HG_DK), lambda h, n: (h, nb - 1 - n, 0, 0)), blk],
        out_specs=[pj, alb_blk, ng_blk],
        out_shape=[_sds((4, S, D), CDT), _sds((2, D), F32), _sds((1, HG_DK), F32)],
        scratch_shapes=[pltpu.VMEM((HG_HPB, HG_DK, HG_DK), F32), pltpu.VMEM((HG_HPB, HG_CPB, HG_CH, HG_DK), F32)],
        compiler_params=_params(("arbitrary", "arbitrary")), name="hgrn_bwd")(proj, alb, ngain, o, states, dy)


def _slope(h):
    return 2.0 ** (-8.0 * (h + 1) / ATT_QH)


def _attn_mask(n):
    qi = lax.broadcasted_iota(jnp.int32, (WINDOW, 2 * WINDOW), 0)
    si = lax.broadcasted_iota(jnp.int32, (WINDOW, 2 * WINDOW), 1)
    dist = qi - si + WINDOW
    valid = (dist >= 0) & (dist < WINDOW) & (n * WINDOW - WINDOW + si >= 0)
    return valid, dist.astype(F32)


def _attn_probs(qk, sink, slope, valid, distf):
    s = qk * (ATT_HD ** -0.5) - slope * distf
    s = jnp.where(valid, s, NEG)
    m = jnp.maximum(jnp.max(s, axis=-1, keepdims=True), sink)
    e = jnp.exp(s - m)
    es = jnp.exp(sink - m)
    inv = 1.0 / (jnp.sum(e, axis=-1, keepdims=True) + es)
    return e * inv, es * inv


def _attn_specs(S):
    steps = S // (ATT_BPB * WINDOW)
    cur = lambda H: pl.BlockSpec((H, ATT_BPB * WINDOW, ATT_HD), lambda n: (0, n, 0))
    prev = lambda H: pl.BlockSpec((H, WINDOW, ATT_HD), lambda n: (0, jnp.maximum(ATT_BPB * n - 1, 0), 0))
    return steps, cur, prev


def _attn_kv(kvc_ref, kvp_ref, head, bi):
    before = kvp_ref[head] if bi == 0 else kvc_ref[head, pl.ds((bi - 1) * WINDOW, WINDOW), :]
    return jnp.concatenate([before, kvc_ref[head, pl.ds(bi * WINDOW, WINDOW), :]], axis=0)


def _attn_fwd(q4, kv4, sinks):
    S = q4.shape[1]
    nb, cur, prev = _attn_specs(S)

    def body(sink_ref, q_ref, kvc_ref, kvp_ref, o_ref, p_ref, ps_ref):
        lane = lax.broadcasted_iota(jnp.int32, (1, 128), 1)
        place = (_half_place(True), _half_place(False))
        masks = [_attn_mask(pl.program_id(0) * ATT_BPB + bi) for bi in range(ATT_BPB)]
        sink_probs = [jnp.zeros((WINDOW, 128), F32) for _ in range(ATT_BPB)]
        heads = lambda kvh: range(kvh * ATT_G, (kvh + 1) * ATT_G)

        def scores(bi, kvh):
            rows = pl.ds(bi * WINDOW, WINDOW)
            kh = _attn_kv(kvc_ref, kvp_ref, kvh, bi)
            vh = _attn_kv(kvc_ref, kvp_ref, ATT_KVH + kvh, bi)
            v_pl = [jnp.dot(vh, m, preferred_element_type=F32).astype(CDT) for m in place]
            return v_pl, [_dot_nt(q_ref[h, rows, :], kh) for h in heads(kvh)]

        def softmax(bi, kvh, qks):
            rows = pl.ds(bi * WINDOW, WINDOW)
            probs = []
            for h, qk in zip(heads(kvh), qks):
                p, ps = _attn_probs(qk, sink_ref[0, h], _slope(h), *masks[bi])
                probs.append(p.astype(CDT))
                p_ref[h, rows, :] = probs[-1]
                sink_probs[bi] = sink_probs[bi] + jnp.where(lane == h, ps, 0.0)
            return probs

        def weighted_values(bi, kvh, v_pl, probs):
            rows = pl.ds(bi * WINDOW, WINDOW)
            for i in range(ATT_G // 2):
                lanes = pl.ds((kvh * ATT_G + 2 * i) * ATT_HD, 2 * ATT_HD)
                o_ref[rows, lanes] = (_dot(probs[2 * i], v_pl[0]) + _dot(probs[2 * i + 1], v_pl[1])).astype(CDT)

        groups = [(bi, kvh) for bi in range(ATT_BPB) for kvh in range(ATT_KVH)]
        ahead = scores(*groups[0])
        for gi, g in enumerate(groups):
            v_pl, qks = ahead
            probs = softmax(*g, qks)
            if gi + 1 < len(groups):
                ahead = scores(*groups[gi + 1])
            weighted_values(*g, v_pl, probs)
        for bi in range(ATT_BPB):
            ps_ref[pl.ds(bi * WINDOW, WINDOW), :] = sink_probs[bi]

    rows_spec = pl.BlockSpec((ATT_BPB * WINDOW, D), lambda n: (n, 0))
    return _call(
        None, body, grid=(nb,),
        in_specs=[pl.BlockSpec(memory_space=pltpu.SMEM), cur(ATT_QH), cur(2 * ATT_KVH), prev(2 * ATT_KVH)],
        out_specs=[rows_spec, pl.BlockSpec((ATT_QH, ATT_BPB * WINDOW, 2 * WINDOW), lambda n: (0, n, 0)),
                   pl.BlockSpec((ATT_BPB * WINDOW, 128), lambda n: (n, 0))],
        out_shape=[_sds((S, D), CDT), _sds((ATT_QH, S, 2 * WINDOW), CDT), _sds((S, 128), F32)],
        compiler_params=_params(("arbitrary",)), name="attn_fwd")(sinks, q4, kv4, kv4)


def _attn_bwd(q4, kv4, probs, sink_probs, do):
    S = q4.shape[1]
    nb, cur, prev = _attn_specs(S)

    def body(q_ref, kvc_ref, kvp_ref, p_ref, ps_ref, do_ref, dq_ref, dkv_ref, dbq_ref, dsink_ref):
        n = pl.program_id(0)
        first = n == 0

        @pl.when(first)
        def _():
            dkv_ref[...] = jnp.zeros_like(dkv_ref)
            dsink_ref[...] = jnp.zeros_like(dsink_ref)
            dbq_ref[...] = jnp.zeros_like(dbq_ref)

        lane = lax.broadcasted_iota(jnp.int32, (1, 128), 1)
        place = (_half_place(True), _half_place(False))
        row_dots = [jnp.zeros((WINDOW, 128), F32) for _ in range(ATT_BPB)]
        heads = lambda kvh: range(kvh * ATT_G, (kvh + 1) * ATT_G)
        pair_lanes = lambda h: pl.ds((h // 2) * 2 * ATT_HD, 2 * ATT_HD)

        def products_of_do(bi, kvh):
            rows = pl.ds(bi * WINDOW, WINDOW)
            kh = _attn_kv(kvc_ref, kvp_ref, kvh, bi)
            vh = _attn_kv(kvc_ref, kvp_ref, ATT_KVH + kvh, bi)
            k_pl = [jnp.dot(kh, m, preferred_element_type=F32).astype(CDT) for m in place]
            v_pl = [jnp.dot(vh, m, preferred_element_type=F32).astype(CDT) for m in place]
            dps = [_dot_nt(do_ref[rows, pair_lanes(h)], v_pl[h % 2]) for h in heads(kvh)]
            dv2 = [sum(_dot_tn(p_ref[h, rows, :], do_ref[rows, pair_lanes(h)]) for h in heads(kvh) if h % 2 == par)
                   for par in range(2)]
            return k_pl, dps, dv2

        def softmax_bwd(bi, kvh, dps):
            rows = pl.ds(bi * WINDOW, WINDOW)
            dss = []
            for h, dp in zip(heads(kvh), dps):
                p = p_ref[h, rows, :].astype(F32)
                dd = jnp.sum(p * dp, axis=-1, keepdims=True)
                dss.append((p * (dp - dd)).astype(CDT))
                row_dots[bi] = row_dots[bi] + jnp.where(lane == h, dd, 0.0)
            return dss

        def products_of_ds(bi, kvh, k_pl, dss, dv2):
            block = n * ATT_BPB + bi
            rows = pl.ds(bi * WINDOW, WINDOW)
            rows_cur = pl.ds(pl.multiple_of(block * WINDOW, WINDOW), WINDOW)
            rows_prev = pl.ds(pl.multiple_of(jnp.maximum(block - 1, 0) * WINDOW, WINDOW), WINDOW)
            for i in range(ATT_G // 2):
                h = kvh * ATT_G + 2 * i
                dq2 = (_dot(dss[2 * i], k_pl[0]) + _dot(dss[2 * i + 1], k_pl[1])) * (ATT_HD ** -0.5)
                dq_ref[rows, pair_lanes(h)] = dq2.astype(CDT)
                dbq_ref[:, pair_lanes(h)] += _colsum(dq2)
            dk = sum(_dot_tn(ds, q_ref[h, rows, :]) for h, ds in zip(heads(kvh), dss)) * (ATT_HD ** -0.5)
            dv = dv2[0][:, :ATT_HD] + pltpu.roll(dv2[1], ATT_HD, 1)[:, :ATT_HD]
            dkv_ref[kvh, rows_prev, :] += dk[:WINDOW]
            dkv_ref[kvh, rows_cur, :] += dk[WINDOW:]
            dkv_ref[ATT_KVH + kvh, rows_prev, :] += dv[:WINDOW]
            dkv_ref[ATT_KVH + kvh, rows_cur, :] += dv[WINDOW:]

        groups = [(bi, kvh) for bi in range(ATT_BPB) for kvh in range(ATT_KVH)]
        ahead = products_of_do(*groups[0])
        for gi, g in enumerate(groups):
            k_pl, dps, dv2 = ahead
            dss = softmax_bwd(*g, dps)
            if gi + 1 < len(groups):
                ahead = products_of_do(*groups[gi + 1])
            products_of_ds(*g, k_pl, dss, dv2)
        dsinks = jnp.zeros((1, 128), F32)
        for bi in range(ATT_BPB):
            dsinks = dsinks - _colsum(ps_ref[pl.ds(bi * WINDOW, WINDOW), :] * row_dots[bi])
        dsink_ref[...] += dsinks

    rows_spec = pl.BlockSpec((ATT_BPB * WINDOW, D), lambda n: (n, 0))
    return _call(
        None, body, grid=(nb,),
        in_specs=[cur(ATT_QH), cur(2 * ATT_KVH), prev(2 * ATT_KVH),
                  pl.BlockSpec((ATT_QH, ATT_BPB * WINDOW, 2 * WINDOW), lambda n: (0, n, 0)),
                  pl.BlockSpec((ATT_BPB * WINDOW, 128), lambda n: (n, 0)), rows_spec],
        out_specs=[rows_spec, pl.BlockSpec((2 * ATT_KVH, S, ATT_HD), lambda n: (0, 0, 0)),
                   pl.BlockSpec((1, D), lambda n: (0, 0)), pl.BlockSpec((1, 128), lambda n: (0, 0))],
        out_shape=[_sds((S, D), CDT), _sds((2 * ATT_KVH, S, ATT_HD), F32), _sds((1, D), F32),
                   _sds((1, 128), F32)],
        compiler_params=_params(("arbitrary",)), name="attn_bwd")(q4, kv4, kv4, probs, sink_probs, do)


def _local_step(x, p, target, getw, sm, emit):
    S = x.shape[0]
    vec = lambda a: a.reshape(1, -1)
    ln_g = lambda l, k: vec(sm["ln_gain"][l, k])
    ln_b = lambda l, k: vec(sm["ln_bias"][l, k])
    xb = x.astype(CDT)
    pb = p.astype(CDT)

    proj = _inproj_fwd(xb, getw("a_w_in"), "a_in")
    o_a, y_a, states = _hgrn_fwd(proj, sm["a_lower_bound"], sm["a_norm_gain"])
    zeros = jnp.zeros((1, D), F32)
    z = [[None] * 3 for _ in range(2)]
    xs = [[None] * 3 for _ in range(2)]
    xbs = [[None] * 3 for _ in range(2)]
    z[0][0], xs[0][0], xbs[0][0] = _mixout_ln(y_a[None], getw("a_w_out")[None], zeros, x, ln_g(0, 0), ln_b(0, 0),
                                              "a_out_ln")
    gu, hid, sgs, ups = [None, None], [None, None], [None, None], [None, None]

    def ffn_ple(l, target=None):
        wgu = getw(f"gu{l}")
        gu[l], hid[l], z[l][1], xs[l][1], xbs[l][1] = _ffn_fwd(
            xs[l][0], xbs[l][0], wgu, getw(f"dn{l}"), ln_g(l, 1), ln_b(l, 1), f"ffn_fwd{l}")
        sgs[l], ups[l], z[l][2], *out = _ple_fwd(
            xs[l][1], xbs[l][1], pb[l], getw(f"pg{l}"), vec(sm["ple_b_gate"][l]), getw(f"pu{l}"), ln_g(l, 2),
            ln_b(l, 2), f"ple_fwd{l}", target=target)
        if target is None:
            xs[l][2], xbs[l][2] = out
        return out

    ffn_ple(0)
    x3, x3b = xs[0][2], xbs[0][2]
    w_kv, w_q, w_bo = getw("kv_w"), getw("b_w_q"), getw("b_w_out")
    kv4 = _proj_heads(x3b, w_kv, vec(sm["kv_b"]), 2 * ATT_KVH, "kv_proj")
    q4 = _proj_heads(x3b, w_q, vec(sm["b_b_q"]), ATT_QH, "q_proj")
    o_b, probs, sink_probs = _attn_fwd(q4, kv4, sm["b_sinks"])
    z[1][0], xs[1][0], xbs[1][0] = _mixout_ln(o_b[None], w_bo[None], sm["b_b_out"], x3, ln_g(1, 0), ln_b(1, 0),
                                              "b_out_ln")
    dy, loss = ffn_ple(1, target)

    gs = {}
    d_ln_g = [[None] * 3 for _ in range(2)]
    d_ln_b = [[None] * 3 for _ in range(2)]
    g_bg = [None, None]

    def ffn_ple_bwd(l, dy, after=None):
        dz2, dzb, dgl, dup, d_ln_g[l][2], d_ln_b[l][2], g_bg[l], d_ln_g[l][1], d_ln_b[l][1] = _ple_bwd(
            dy, z[l][2], sgs[l], ups[l], ln_g(l, 2), getw(f"pg{l}"), z[l][1], ln_g(l, 1), f"ple_bwd{l}", after=after)
        g_pg = _wgrad(xbs[l][1][None], dgl[None], f"g_ple_gate{l}")[0]
        g_pu = _wgrad(pb[l][None], dup[None], f"g_ple_up{l}")[0]
        tok = emit({f"pg{l}": g_pg, f"pu{l}": g_pu})
        tok = tok + emit({f"dn{l}": _wgrad(hid[l], dzb[None], f"g_ffn_down{l}")})
        dgu = _ffn_bwd_hidden(dzb, gu[l], getw(f"dn{l}"), f"ffn_bwd{l}", after=tok)
        g_gu = _wgrad(dgu.reshape(8, S, FFN_B), xbs[l][0][None], f"g_ffn_gate_up{l}")
        tok = emit({f"gu{l}": g_gu})
        dx1 = _ffn_bwd_input(dz2, dgu, getw(f"gu{l}"), f"ffn_bwd{l}", after=tok)
        return dx1, None

    dx1, tok = ffn_ple_bwd(1, dy)
    dz, dzb, do, d_ln_g[1][0], d_ln_b[1][0], gs["b_b_out"] = _mixout_bwd(dx1, z[1][0], ln_g(1, 0), w_bo[None], CDT,
                                                                        "b_out_bwd", after=tok)
    g_bo = _wgrad(o_b[None], dzb[None], "g_b_w_out")[0]
    dq, dkv4, dbq, dsinks = _attn_bwd(q4, kv4, probs, sink_probs, do[0])
    gs["b_b_q"] = dbq
    gs["b_sinks"] = dsinks
    g_q = _wgrad(x3b[None], dq[None], "g_b_w_q")[0]
    dx3, dkv, gs["kv_b"] = _qkv_bwd(dz, dq, dkv4, w_q, w_kv, "qkv_bwd", after=tok)
    g_kv = _wgrad(x3b[None], dkv[None], "g_kv_w")[0]
    tok = emit({"b_w_out": g_bo, "b_w_q": g_q, "kv_w": g_kv})
    dx1, tok = ffn_ple_bwd(0, dx3, tok)
    w_ao = getw("a_w_out")
    dz, dzb, dyr, d_ln_g[0][0], d_ln_b[0][0], _ = _mixout_bwd(dx1, z[0][0], ln_g(0, 0), w_ao[None], F32, "a_out_bwd",
                                                              after=tok)
    g_ao = _wgrad(y_a[None], dzb[None], "g_a_w_out")[0]
    tok = emit({"a_w_out": g_ao})
    dproj, gs["a_lower_bound"], gs["a_norm_gain"] = _hgrn_bwd(proj, sm["a_lower_bound"], sm["a_norm_gain"], o_a, states,
                                                              dyr[0], after=tok)
    tk = lambda t: (None, t, D)
    g_ain = _mm_tn(xb[None], dproj, N_DEV, lambda g, k: (0, k, 0), lambda g, k: (g // 2, k, g % 2),
                   tk, lambda t: (None, t, 512), (N_DEV, D, 512), (None, D, 512), lambda g, k: (g, 0, 0), name="g_a_w_in")
    gs["ple_b_gate"] = jnp.concatenate(g_bg, axis=0)
    gs["ln_gain"] = jnp.stack([jnp.concatenate(r, axis=0) for r in d_ln_g])
    gs["ln_bias"] = jnp.stack([jnp.concatenate(r, axis=0) for r in d_ln_b])
    gs["loss"] = loss
    tok = emit({"a_w_in": g_ain}, small=gs)
    grad_x = _inproj_bwd(dz, dproj, getw("a_w_in"), "a_in_bwd", after=tok)
    return loss, grad_x, gs


def _peer(k):
    x, y, c = lax.axis_index("x"), lax.axis_index("y"), lax.axis_index("c")
    px = 1 - x if k & 4 else x
    py = 1 - y if k & 2 else y
    pc = 1 - c if k & 1 else c
    return (px, py, pc), 4 * px + 2 * py + pc


def _my_index():
    return 4 * lax.axis_index("x") + 2 * lax.axis_index("y") + lax.axis_index("c")


def _piece_copy(mode, src, land, send_sems, recv_sems, t, k, sender, receiver, peer):
    return pltpu.make_async_remote_copy(
        src_ref=src if mode == "gather" else src.at[receiver], dst_ref=land.at[sender],
        send_sem=send_sems.at[t * 7 + k - 1], recv_sem=recv_sems.at[t * 7 + k - 1], device_id=peer, device_id_type=MESH)


def _sequencer_exchange(srcs, modes, name, collective_id, after=None):
    n = len(srcs)
    land_shapes = [((N_DEV,) + a.shape) if mode == "gather" else a.shape for a, mode in zip(srcs, modes)]
    extra = [] if after is None else [after]

    def body(*refs):
        src_refs, land_refs = refs[:n], refs[n + len(extra):2 * n + len(extra)]
        send_sems, recv_sems, local_sems = refs[2 * n + len(extra):]
        barrier = pltpu.get_barrier_semaphore()
        for k in range(1, N_DEV):
            pl.semaphore_signal(barrier, inc=1, device_id=_peer(k)[0], device_id_type=MESH)
        pl.semaphore_wait(barrier, N_DEV - 1)
        me = _my_index()
        local = []
        for i in range(n):
            cp = pltpu.make_async_copy(src_refs[i] if modes[i] == "gather" else src_refs[i].at[me], land_refs[i].at[me],
                                       local_sems.at[i])
            cp.start()
            local.append(cp)
        for k in range(1, N_DEV):
            peer, pid = _peer(k)
            for t in range(n):
                _piece_copy(modes[t], src_refs[t], land_refs[t], send_sems, recv_sems, t, k, me, pid, peer).start()
        for k in range(1, N_DEV):
            peer, pid = _peer(k)
            for t in range(n):
                _piece_copy(modes[t], src_refs[t], land_refs[t], send_sems, recv_sems, t, k, pid, me, peer).wait_recv()
        for k in range(1, N_DEV):
            peer, pid = _peer(k)
            for t in range(n):
                _piece_copy(modes[t], src_refs[t], land_refs[t], send_sems, recv_sems, t, k, me, pid, peer).wait_send()
        for cp in local:
            cp.wait()

    return pl.kernel(
        body, out_type=[_sds(s, a.dtype) for s, a in zip(land_shapes, srcs)],
        mesh=plsc.ScalarSubcoreMesh(axis_name="sequencer", num_cores=1),
        scratch_types=[pltpu.SemaphoreType.DMA((7 * n,)), pltpu.SemaphoreType.DMA((7 * n,)), pltpu.SemaphoreType.DMA((n,))],
        compiler_params=pltpu.CompilerParams(collective_id=collective_id), name=name)(*srcs, *extra)


def _sequencer_gather(srcs, name, collective_id, after=None):
    n = len(srcs)
    extra = [] if after is None else [after]

    def body(*refs):
        src_refs, land_refs = refs[:n], refs[n + len(extra):2 * n + len(extra)]
        send_sems, recv_sems, local_sems = refs[2 * n + len(extra):]
        x, y, c = lax.axis_index("x"), lax.axis_index("y"), lax.axis_index("c")
        sibling = (x, y, 1 - c)
        chips = [(1 - x, y), (x, 1 - y), (1 - x, 1 - y)]
        index = lambda px, py, pc: 4 * px + 2 * py + pc
        barrier = pltpu.get_barrier_semaphore()
        for peer in [sibling] + [(*chip, c) for chip in chips]:
            pl.semaphore_signal(barrier, inc=1, device_id=peer, device_id_type=MESH)
        pl.semaphore_wait(barrier, 4)

        def copy(t, k, slot, to, src=None):
            return pltpu.make_async_remote_copy(
                src_ref=land_refs[t].at[slot] if src is None else src, dst_ref=land_refs[t].at[slot],
                send_sem=send_sems.at[7 * t + k], recv_sem=recv_sems.at[7 * t + k], device_id=to, device_id_type=MESH)

        me = index(x, y, c)
        local = []
        for t in range(n):
            cp = pltpu.make_async_copy(src_refs[t], land_refs[t].at[me], local_sems.at[t])
            cp.start()
            local.append(cp)
        sends = []
        for t in range(n):
            sends.append(copy(t, 0, me, sibling, src=src_refs[t]))
            sends += [copy(t, 1 + j, me, (*chip, c), src=src_refs[t]) for j, chip in enumerate(chips)]
        for cp in sends:
            cp.start()
        for j, chip in enumerate(chips):
            for t in range(n):
                copy(t, 1 + j, index(*chip, c), sibling, src=src_refs[t]).wait_recv()
                passed = copy(t, 4 + j, index(*chip, c), sibling)
                passed.start()
                sends.append(passed)
        for t in range(n):
            copy(t, 0, index(x, y, 1 - c), sibling, src=src_refs[t]).wait_recv()
        for j, chip in enumerate(chips):
            for t in range(n):
                copy(t, 4 + j, index(*chip, 1 - c), sibling, src=src_refs[t]).wait_recv()
        for cp in sends:
            cp.wait_send()
        for cp in local:
            cp.wait()

    return pl.kernel(
        body, out_type=[_sds((N_DEV,) + a.shape, a.dtype) for a in srcs],
        mesh=plsc.ScalarSubcoreMesh(axis_name="sequencer", num_cores=1),
        scratch_types=[pltpu.SemaphoreType.DMA((7 * n,)), pltpu.SemaphoreType.DMA((7 * n,)), pltpu.SemaphoreType.DMA((n,))],
        compiler_params=pltpu.CompilerParams(collective_id=collective_id), name=name)(*srcs, *extra)


def _adamw(w, g, m, v):
    m = ADAM_B1 * m + (1.0 - ADAM_B1) * g
    v = ADAM_B2 * v + (1.0 - ADAM_B2) * (g * g)
    m_hat = m / (1.0 - ADAM_B1 ** ADAM_STEP)
    v_hat = v / (1.0 - ADAM_B2 ** ADAM_STEP)
    delta = -ADAM_LR * (m_hat / (jnp.sqrt(v_hat) + ADAM_EPS) + ADAM_WD * w)
    return delta, m, v


def _adam_big(w, parts, m, v, name, after=None):
    L, R, C = w.shape
    P = parts[0].shape[0]
    tr = _tile(R, (256, 128, 176, 64, 32, 16))
    nr = R // tr

    def body(w_ref, *refs):
        p_refs, (m_ref, v_ref, g_ref, d_ref, mo_ref, vo_ref) = refs[:L], refs[L:]
        for l in range(L):
            @pl.when(pl.program_id(0) == l)
            def _(p_ref=p_refs[l]):
                g = p_ref[0].astype(F32)
                for s in range(1, P):
                    g = g + p_ref[s].astype(F32)
                g_ref[...] = g
                d_ref[...], mo_ref[...], vo_ref[...] = _adamw(w_ref[...], g, m_ref[...], v_ref[...])

    row = pl.BlockSpec((None, tr, C), lambda l, i: (l, i, 0))
    park = lambda l_of: (lambda l, i: (0, jnp.where(l == l_of, i, 0 if l_of else nr - 1), 0))
    return _call(
        after, body, grid=(L, nr),
        in_specs=[row] + [pl.BlockSpec((P, tr, C), park(l)) for l in range(L)] + [row, row],
        out_specs=[row] * 4, out_shape=[_sds((L, R, C), F32)] * 4,
        compiler_params=_params(("arbitrary", "arbitrary")), name=name)(w, *parts, m, v)


SMALL = (("a_lower_bound", (2, 128), (2, D)), ("ln_gain", (3, 2, 128), (6, D)), ("ln_bias", (3, 2, 128), (6, D)),
         ("a_norm_gain", (1, 128), (1, 128)), ("kv_b", (1, 512), (1, 512)), ("b_b_q", (1, D), (1, D)),
         ("b_sinks", (1, ATT_QH), (1, 128)), ("b_b_out", (1, D), (1, D)), ("ple_b_gate", (2, D), (2, D)))


def _adam_small(parts, w, m, v, losses, after=None):
    k = len(SMALL)

    def body(*refs):
        p_refs, w_refs, m_refs, v_refs = refs[:k], refs[k:2 * k], refs[2 * k:3 * k], refs[3 * k:4 * k]
        loss_ref, outs, total_ref = refs[4 * k], refs[4 * k + 1:-1], refs[-1]
        total = loss_ref[0]
        for s in range(1, N_DEV):
            total = total + loss_ref[s]
        total_ref[...] = total
        me = _my_index()
        for i, (_, wshape, pshape) in enumerate(SMALL):
            cols = wshape[-1]
            lanes = slice(None) if cols == pshape[1] else (
                pl.ds(0, cols) if cols < 128 else pl.ds(pl.multiple_of(me * cols, cols), cols))
            at = [(slice(None), slice(None))] if len(wshape) == 2 else [
                (pl.ds(l * wshape[0] + kk, 1), (kk, pl.ds(l, 1), slice(None))) for kk in range(wshape[0]) for l in range(wshape[1])]
            for rows, own in at:
                g = p_refs[i][0, rows, lanes]
                for s in range(1, N_DEV):
                    g = g + p_refs[i][s, rows, lanes]
                g_ref, d_ref, mo_ref, vo_ref = outs[4 * i:4 * i + 4]
                g_ref[own] = g
                d_ref[own], mo_ref[own], vo_ref[own] = _adamw(w_refs[i][own], g, m_refs[i][own], v_refs[i][own])

    full = lambda shape: pl.BlockSpec(shape, lambda: (0,) * len(shape))
    names = [n for n, _, _ in SMALL]
    held = lambda a, ws: a.swapaxes(0, 1) if len(ws) == 3 else a.reshape(ws)
    back = lambda r, n: r.swapaxes(0, 1) if r.ndim == 3 else r.reshape(w[n].shape)
    res = _call(
        after, body,
        in_specs=[full((N_DEV,) + ps) for _, _, ps in SMALL] + [full(ws) for _, ws, _ in SMALL] * 3
        + [full((N_DEV, 1, 128))],
        out_specs=[full(ws) for _, ws, _ in SMALL for _ in range(4)] + [full((1, 128))],
        out_shape=[_sds(ws, F32) for _, ws, _ in SMALL for _ in range(4)] + [_sds((1, 128), F32)], name="adam_small")(
            *[parts[n] for n in names], *[held(a[n], ws) for a in (w, m, v) for n, ws, _ in SMALL], losses)
    return {n: [back(r, n) for r in res[4 * i:4 * i + 4]] for i, n in enumerate(names)}, res[-1][0, 0]


WEIGHTS = ("a_w_in", "a_lower_bound", "a_norm_gain", "a_w_out", "kv_w", "kv_b", "b_w_q", "b_b_q", "b_sinks", "b_w_out",
           "b_b_out", "ffn_w_gate_up", "ffn_w_down", "ple_w_up", "ple_w_gate", "ple_b_gate", "ln_gain", "ln_bias")


GATHER_GROUPS = (("a_w_in",), ("a_w_out", "gu0"), ("dn0", "pu0", "pg0"), ("kv_w", "b_w_q", "b_w_out"), ("gu1",),
                 ("dn1",), ("pu1", "pg1"))
KERNEL_LAYOUT = {
    "a_w_in": lambda a: a,
    "a_w_out": lambda a: a.reshape(D, D),
    "kv_w": lambda a: a.reshape(D, 2 * ATT_KVH * ATT_HD),
    "b_w_q": lambda a: a.reshape(D, D),
    "b_w_out": lambda a: a.reshape(D, D),
    "gu": lambda a: a.reshape(2, 4, FFN_B, D),
    "dn": lambda a: a.reshape(4, FFN_B, D),
    "pu": lambda a: a,
    "pg": lambda a: a.reshape(D, D),
}
_row_blocks = lambda a: a.reshape(N_DEV, -1, a.shape[-1])
OWNER_BLOCKS = {
    "a_w_in": lambda g: g,
    "a_w_out": _row_blocks,
    "kv_w": _row_blocks,
    "b_w_q": _row_blocks,
    "b_w_out": _row_blocks,
    "gu": lambda g: g,
    "dn": lambda g: _row_blocks(g.reshape(FFN_H, D)),
    "pu": lambda g: g.reshape(PLE_DIM, N_DEV, 128).transpose(1, 0, 2),
    "pg": _row_blocks,
}
ADAM_PARTS = (("kv_w", ("kv_w",)), ("b_w_q", ("b_w_q",)), ("b_w_out", ("b_w_out",)), ("ffn_w_gate_up", ("gu0", "gu1")),
              ("ffn_w_down", ("dn0", "dn1")), ("ple_w_up", ("pu0", "pu1")), ("ple_w_gate", ("pg0", "pg1")),
              ("a_w_out", ("a_w_out",)), ("a_w_in", ("a_w_in",)))


def kernel(x, p, a_w_in, a_lower_bound, a_norm_gain, a_w_out, kv_w, kv_b, b_w_q, b_b_q, b_sinks, b_w_out, b_b_out, ffn_w_gate_up, ffn_w_down, ple_w_up, ple_w_gate, ple_b_gate, ln_gain, ln_bias, loss_target, m_a_w_in, m_a_lower_bound, m_a_norm_gain, m_a_w_out, m_kv_w, m_kv_b, m_b_w_q, m_b_b_q, m_b_sinks, m_b_w_out, m_b_b_out, m_ffn_w_gate_up, m_ffn_w_down, m_ple_w_up, m_ple_w_gate, m_ple_b_gate, m_ln_gain, m_ln_bias, v_a_w_in, v_a_lower_bound, v_a_norm_gain, v_a_w_out, v_kv_w, v_kv_b, v_b_w_q, v_b_b_q, v_b_sinks, v_b_w_out, v_b_b_out, v_ffn_w_gate_up, v_ffn_w_down, v_ple_w_up, v_ple_w_gate, v_ple_b_gate, v_ln_gain, v_ln_bias):
    given = dict(locals())
    w = {n: given[n] for n in WEIGHTS}
    m = {n: given["m_" + n] for n in WEIGHTS}
    v = {n: given["v_" + n] for n in WEIGHTS}
    shards = {"a_w_in": a_w_in[0], "a_w_out": a_w_out[0], "kv_w": kv_w, "b_w_q": b_w_q[0], "b_w_out": b_w_out[0]}
    for l in range(2):
        shards.update({f"gu{l}": ffn_w_gate_up[l].T, f"dn{l}": ffn_w_down[l], f"pu{l}": ple_w_up[l], f"pg{l}": ple_w_gate[l]})
    sharded_small = [a_lower_bound, ln_gain.swapaxes(0, 1), ln_bias.swapaxes(0, 1)]
    gathered = {}
    for gi, g in enumerate(GATHER_GROUPS):
        lands = _sequencer_gather([shards[n].astype(CDT) for n in g] + (sharded_small if gi == 0 else []),
                                  f"gather{gi}", gi)
        for n, a in zip(g, lands):
            gathered[n] = KERNEL_LAYOUT[n.rstrip("01")](a)
        if gi == 0:
            alb = lands[len(g)].transpose(1, 0, 2).reshape(2, D)
            lng, lnb = [a.transpose(2, 1, 0, 3).reshape(2, 3, D) for a in lands[len(g) + 1:]]

    getw = gathered.__getitem__

    sm = {"a_lower_bound": alb, "ln_gain": lng, "ln_bias": lnb,
          "a_norm_gain": a_norm_gain, "kv_b": kv_b, "b_b_q": b_b_q[0], "b_sinks": b_sinks, "b_b_out": b_b_out,
          "ple_b_gate": ple_b_gate}

    scatters, small_parts = [], {}

    def emit(grads, small=None):
        names = list(grads)
        blocks = [OWNER_BLOCKS[n.rstrip("01")](grads[n]) for n in names]
        partials = [] if small is None else [small[n].reshape(ps) for n, _, ps in SMALL] + [small["loss"]]
        lands = _sequencer_exchange(blocks + partials, ["scatter"] * len(blocks) + ["gather"] * len(partials),
                                    f"scatter{len(scatters)}", len(GATHER_GROUPS) + len(scatters))
        scatters.append(dict(zip(names, lands)))
        small_parts.update(zip([n for n, _, _ in SMALL] + ["loss"], lands[len(blocks):]))
        return blocks + (list(scatters[-4].values()) if len(scatters) >= 4 else [])

    loss, grad_x, gs = _local_step(x[0], p[:, 0], loss_target[0], getw, sm, emit)

    out, parts, last = {}, {}, [grad_x]
    for landed in scatters:
        parts.update(landed)
        for n, keys in ADAM_PARTS:
            if n in out or not all(key in parts for key in keys):
                continue
            lrc = (1,) * (3 - w[n].ndim) + w[n].shape
            shard = (lambda a: a.reshape(lrc).swapaxes(1, 2)) if n == "ffn_w_gate_up" else (lambda a: a.reshape(lrc))
            res = _adam_big(shard(w[n]), [parts[key] for key in keys], shard(m[n]), shard(v[n]), "adam_" + n, after=last)
            out[n] = [(r.swapaxes(1, 2) if n == "ffn_w_gate_up" else r).reshape(w[n].shape) for r in res]
            last = [res[3]]
    small_out, loss = _adam_small(small_parts, w, m, v, small_parts["loss"], after=last)
    out.update(small_out)
    res = [loss, grad_x[None]]
    for i in range(4):
        res += [out[n][i] for n in WEIGHTS]
    return tuple(res)
```

```python
import jax
import jax.numpy as jnp
from jax import lax
from jax.experimental import pallas as pl
from jax.experimental.pallas import tpu as pltpu
from jax.experimental.pallas import tpu_sc as plsc

F32 = jnp.float32
CDT = jnp.bfloat16

N_DEV = 8
D = 1024
HG_H, HG_DK, HG_CH = 8, 128, 64
HG_HPB = 4
HG_CPB = 8
ATT_HD, ATT_QH, ATT_KVH, ATT_G, WINDOW = 64, 16, 4, 4, 128
ATT_BPB = 1
FFN_H = 2816
FFN_B = FFN_H // 4
PLE_DIM = 256
ALPHA = (2.0 * 2) ** 0.25
LN_EPS = 1e-5
RMS_EPS = 1e-6
ADAM_LR, ADAM_B1, ADAM_B2, ADAM_EPS, ADAM_WD, ADAM_STEP = 0.001, 0.9, 0.999, 1e-08, 0.01, 10
ROW_TILES = (512, 256, 128, 64)
VMEM_LIMIT = 48 * 1024 * 1024
NEG = -1e30

MESH = pl.DeviceIdType.MESH


def _tile(n, cands=ROW_TILES):
    for t in cands:
        if n % t == 0:
            return t
    return n


def _sds(shape, dtype):
    return jax.ShapeDtypeStruct(tuple(shape), dtype)


def _params(sem):
    return pltpu.CompilerParams(dimension_semantics=sem, vmem_limit_bytes=VMEM_LIMIT)


def _dot(a, b):
    return jnp.dot(a.astype(CDT), b.astype(CDT), preferred_element_type=F32)


def _dot_nt(a, b):
    return lax.dot_general(a.astype(CDT), b.astype(CDT), (((1,), (1,)), ((), ())), preferred_element_type=F32)


def _dot_tn(a, b):
    return lax.dot_general(a.astype(CDT), b.astype(CDT), (((0,), (0,)), ((), ())), preferred_element_type=F32)


def _sigmoid(x):
    return jax.nn.sigmoid(x)


def _ln_fwd(z, g, b):
    mu = jnp.mean(z, axis=-1, keepdims=True)
    zc = z - mu
    var = jnp.mean(zc * zc, axis=-1, keepdims=True)
    return zc * lax.rsqrt(var + LN_EPS) * g + b


def _ln_bwd(z, g, dy):
    mu = jnp.mean(z, axis=-1, keepdims=True)
    zc = z - mu
    var = jnp.mean(zc * zc, axis=-1, keepdims=True)
    rstd = lax.rsqrt(var + LN_EPS)
    xhat = zc * rstd
    dxh = dy * g
    dz = rstd * (dxh - jnp.mean(dxh, axis=-1, keepdims=True) - xhat * jnp.mean(dxh * xhat, axis=-1, keepdims=True))
    return dz, xhat


def _colsum(x):
    return jnp.sum(x, axis=0, keepdims=True)


def _acc(ref, val, first):
    @pl.when(first)
    def _():
        ref[...] = val

    @pl.when(jnp.logical_not(first))
    def _():
        ref[...] += val


def _zero_at(ref, first):
    @pl.when(first)
    def _():
        ref[...] = jnp.zeros_like(ref)


def _call(after, body, **kw):
    after = [] if after is None else list(after)
    specs = list(kw["in_specs"])
    kw["in_specs"] = [pl.BlockSpec(memory_space=pl.ANY)] * len(after) + specs

    def ordered_body(*refs):
        body(*refs[len(after):])

    call = pl.pallas_call(ordered_body, **kw)
    return lambda *args: call(*after, *args)


def _mm_tn(a3, b3, G, amap, bmap, ablock, bblock, out_shape, oblock, omap, name="mm_tn"):
    S = a3.shape[1]

    def body(a_ref, b_ref, o_ref):
        o_ref[...] = _dot_tn(a_ref[...], b_ref[...]).astype(o_ref.dtype)

    return _call(
        None, body, grid=(G, 1),
        in_specs=[pl.BlockSpec(ablock(S), amap), pl.BlockSpec(bblock(S), bmap)],
        out_specs=pl.BlockSpec(oblock, omap), out_shape=_sds(out_shape, CDT),
        compiler_params=_params(("arbitrary", "arbitrary")), name=name)(a3, b3)


def _wgrad(a3, b3, name):
    Ga, S, M = a3.shape
    Gb, _, N = b3.shape
    G = max(Ga, Gb)
    return _mm_tn(
        a3, b3, G,
        (lambda g, k: (g, k, 0)) if Ga > 1 else (lambda g, k: (0, k, 0)),
        (lambda g, k: (g, k, 0)) if Gb > 1 else (lambda g, k: (0, k, 0)),
        lambda tk: (None, tk, M), lambda tk: (None, tk, N),
        (G, M, N), (None, M, N), lambda g, k: (g, 0, 0), name=name)


def _mixout_ln(u3, w3, bias, xin, gain, beta, name):
    G, S, Kb = u3.shape
    tm = _tile(S)

    def body(u_ref, w_ref, b_ref, x_ref, g_ref, be_ref, z_ref, xo_ref, xob_ref):
        h = b_ref[...] + _dot(u_ref[0], w_ref[0])
        for g in range(1, G):
            h = h + _dot(u_ref[g], w_ref[g])
        z = ALPHA * x_ref[...] + h
        z_ref[...] = z
        y = _ln_fwd(z, g_ref[...], be_ref[...])
        xo_ref[...] = y
        xob_ref[...] = y.astype(CDT)

    row = pl.BlockSpec((tm, D), lambda i: (i, 0))
    vec = pl.BlockSpec((1, D), lambda i: (0, 0))
    return _call(
        None, body, grid=(S // tm,),
        in_specs=[pl.BlockSpec((G, tm, Kb), lambda i: (0, i, 0)), pl.BlockSpec((G, Kb, D), lambda i: (0, 0, 0)),
                  vec, row, vec, vec],
        out_specs=[row, row, row], out_shape=[_sds((S, D), F32), _sds((S, D), F32), _sds((S, D), CDT)],
        compiler_params=_params(("arbitrary",)), name=name)(u3, w3, bias, xin, gain, beta)


def _ffn_fwd(xin, xin_b, wgu, wdn, gain, beta, name):
    S = xin.shape[0]
    tm = _tile(S)

    def hidden(xb_ref, wgu_ref, gu_ref, hid_ref):
        xb = xb_ref[...]
        gate = _dot_nt(xb, wgu_ref[0])
        up = _dot_nt(xb, wgu_ref[1])
        gu_ref[0] = gate.astype(CDT)
        gu_ref[1] = up.astype(CDT)
        hid_ref[...] = (gate * _sigmoid(gate) * up).astype(CDT)

    gu, hid = _call(
        None, hidden, grid=(4, S // tm),
        in_specs=[pl.BlockSpec((tm, D), lambda j, i: (i, 0)), pl.BlockSpec((2, None, FFN_B, D), lambda j, i: (0, j, 0, 0))],
        out_specs=[pl.BlockSpec((2, None, tm, FFN_B), lambda j, i: (0, j, i, 0)),
                   pl.BlockSpec((None, tm, FFN_B), lambda j, i: (j, i, 0))],
        out_shape=[_sds((2, 4, S, FFN_B), CDT), _sds((4, S, FFN_B), CDT)],
        compiler_params=_params(("arbitrary", "arbitrary")), name=name + "_hidden")(xin_b, wgu)

    def down(x_ref, hid_ref, wdn_ref, g_ref, be_ref, z_ref, xo_ref, xob_ref):
        z = ALPHA * x_ref[...]
        for j in range(4):
            z = z + _dot(hid_ref[j], wdn_ref[j])
        z_ref[...] = z
        y = _ln_fwd(z, g_ref[...], be_ref[...])
        xo_ref[...] = y
        xob_ref[...] = y.astype(CDT)

    row = pl.BlockSpec((tm, D), lambda i: (i, 0))
    vec = pl.BlockSpec((1, D), lambda i: (0, 0))
    z, xo, xob = _call(
        None, down, grid=(S // tm,),
        in_specs=[row, pl.BlockSpec((4, tm, FFN_B), lambda i: (0, i, 0)), pl.BlockSpec((4, FFN_B, D), lambda i: (0, 0, 0)),
                  vec, vec],
        out_specs=[row, row, row], out_shape=[_sds((S, D), F32), _sds((S, D), F32), _sds((S, D), CDT)],
        compiler_params=_params(("arbitrary",)), name=name + "_down")(xin, hid, wdn, gain, beta)
    return gu, hid, z, xo, xob


def _ple_fwd(xin, xin_b, p_b, wpg, bgate, wpu, gain, beta, name, target=None):
    S = xin.shape[0]
    tm = _tile(S)

    def body(x_ref, xb_ref, p_ref, wpg_ref, bg_ref, wpu_ref, g_ref, be_ref, *rest):
        if target is None:
            sg_ref, up_ref, z_ref, xo_ref, xob_ref = rest
        else:
            t_ref, sg_ref, up_ref, z_ref, dy_ref, l_ref = rest
        sg =_sigmoid(_dot(xb_ref[...], wpg_ref[...]) + bg_ref[...])
        pb = p_ref[...]
        up = jnp.concatenate([_dot(pb, wpu_ref[j]) for j in range(N_DEV)], axis=-1)
        sg_ref[...] = sg.astype(CDT)
        up_ref[...] = (up * sg * (1.0 - sg)).astype(CDT)
        z = ALPHA * x_ref[...] + sg * up
        z_ref[...] = z
        y = _ln_fwd(z, g_ref[...], be_ref[...])
        if target is None:
            xo_ref[...] = y
            xob_ref[...] = y.astype(CDT)
        else:
            e = y - t_ref[...]
            dy_ref[...] = e * (1.0 / D)
            part = 0.5 * jnp.sum(jnp.sum(e * e, axis=-1, keepdims=True) * (1.0 / D), axis=0, keepdims=True)
            _acc(l_ref, jnp.broadcast_to(part, l_ref.shape), pl.program_id(0) == 0)

    row = pl.BlockSpec((tm, D), lambda i: (i, 0))
    vec = pl.BlockSpec((1, D), lambda i: (0, 0))
    last = target is not None
    return _call(
        None, body, grid=(S // tm,),
        in_specs=[row, row, pl.BlockSpec((tm, PLE_DIM), lambda i: (i, 0)), pl.BlockSpec((D, D), lambda i: (0, 0)), vec,
                  pl.BlockSpec((N_DEV, PLE_DIM, D // N_DEV), lambda i: (0, 0, 0)), vec, vec] + [row] * last,
        out_specs=[row] * 4 + [pl.BlockSpec((1, 128), lambda i: (0, 0)) if last else row],
        out_shape=[_sds((S, D), CDT)] * 2 + [_sds((S, D), F32)] * 2 + [_sds((1, 128), F32) if last else _sds((S, D), CDT)],
        compiler_params=_params(("arbitrary",)), name=name)(
            xin, xin_b, p_b, wpg, bgate, wpu, gain, beta, *([target] if last else []))


def _ple_bwd(dy, z, sg, up, gain, wpg, z_ffn, gain_ffn, name, after=None):
    S = dy.shape[0]
    tm = _tile(S)

    def body(dy_ref, z_ref, sg_ref, up_ref, g_ref, wpg_ref, zf_ref, gf_ref, dzf_ref, dzfb_ref, dgl_ref, dup_ref,
             dgain_ref, dbeta_ref, dbg_ref, dgainf_ref, dbetaf_ref):
        first = pl.program_id(0) == 0
        dy_ = dy_ref[...]
        dz, xhat = _ln_bwd(z_ref[...], g_ref[...], dy_)
        dgl = dz * up_ref[...].astype(F32)
        dgl_ref[...] = dgl.astype(CDT)
        dup_ref[...] = (dz * sg_ref[...].astype(F32)).astype(CDT)
        dx = ALPHA * dz + _dot_nt(dgl, wpg_ref[...])
        dzf, xhatf = _ln_bwd(zf_ref[...], gf_ref[...], dx)
        dzf_ref[...] = dzf
        dzfb_ref[...] = dzf.astype(CDT)
        _acc(dgain_ref, _colsum(dy_ * xhat), first)
        _acc(dbeta_ref, _colsum(dy_), first)
        _acc(dbg_ref, _colsum(dgl), first)
        _acc(dgainf_ref, _colsum(dx * xhatf), first)
        _acc(dbetaf_ref, _colsum(dx), first)

    row = pl.BlockSpec((tm, D), lambda i: (i, 0))
    vec = pl.BlockSpec((1, D), lambda i: (0, 0))
    return _call(
        after, body, grid=(S // tm,),
        in_specs=[row, row, row, row, vec, pl.BlockSpec((D, D), lambda i: (0, 0)), row, vec],
        out_specs=[row, row, row, row] + [vec] * 5,
        out_shape=[_sds((S, D), F32)] + [_sds((S, D), CDT)] * 3 + [_sds((1, D), F32)] * 5,
        compiler_params=_params(("arbitrary",)), name=name)(dy, z, sg, up, gain, wpg, z_ffn, gain_ffn)


def _ffn_bwd_hidden(dzb, gu, wdn, name, after=None):
    S = dzb.shape[0]
    tm = _tile(S)

    def hidden(dzb_ref, gu_ref, wdn_ref, dgu_ref):
        dhid = _dot_nt(dzb_ref[...], wdn_ref[...])
        gate, up = gu_ref[0].astype(F32), gu_ref[1].astype(F32)
        sg = _sigmoid(gate)
        dgu_ref[0] = (dhid * up * (sg * (1.0 + gate * (1.0 - sg)))).astype(CDT)
        dgu_ref[1] = (dhid * (gate * sg)).astype(CDT)

    blocks = pl.BlockSpec((2, None, tm, FFN_B), lambda j, i: (0, j, i, 0))
    return _call(
        after, hidden, grid=(4, S // tm),
        in_specs=[pl.BlockSpec((tm, D), lambda j, i: (i, 0)), blocks, pl.BlockSpec((None, FFN_B, D), lambda j, i: (j, 0, 0))],
        out_specs=blocks, out_shape=_sds((2, 4, S, FFN_B), CDT),
        compiler_params=_params(("arbitrary", "arbitrary")), name=name + "_hidden")(dzb, gu, wdn)


def _ffn_bwd_input(dz, dgu, wgu, name, after=None):
    S = dz.shape[0]
    tm = _tile(S)

    def to_input(dz_ref, dgu_ref, wgu_ref, dx_ref):
        acc = ALPHA * dz_ref[...]
        for g in range(2):
            for j in range(4):
                acc = acc + _dot(dgu_ref[g, j], wgu_ref[g, j])
        dx_ref[...] = acc

    rows = pl.BlockSpec((tm, D), lambda i: (i, 0))
    return _call(
        after, to_input, grid=(S // tm,),
        in_specs=[rows, pl.BlockSpec((2, 4, tm, FFN_B), lambda i: (0, 0, i, 0)),
                  pl.BlockSpec((2, 4, FFN_B, D), lambda i: (0, 0, 0, 0))],
        out_specs=rows, out_shape=_sds((S, D), F32),
        compiler_params=_params(("arbitrary",)), name=name + "_input")(dz, dgu, wgu)


def _mixout_bwd(dy, z, gain, w3, du_dtype, name, after=None):
    S = dy.shape[0]
    G, Kb, _ = w3.shape
    tm = _tile(S)

    def body(dy_ref, z_ref, g_ref, w_ref, dz_ref, dzb_ref, du_ref, dgain_ref, dbeta_ref, dbias_ref):
        first = pl.program_id(0) == 0
        dy_ = dy_ref[...]
        dz, xhat = _ln_bwd(z_ref[...], g_ref[...], dy_)
        dz_ref[...] = dz
        dzb = dz.astype(CDT)
        dzb_ref[...] = dzb
        for g in range(G):
            du_ref[g] = _dot_nt(dzb, w_ref[g]).astype(du_ref.dtype)
        _acc(dgain_ref, _colsum(dy_ * xhat), first)
        _acc(dbeta_ref, _colsum(dy_), first)
        _acc(dbias_ref, _colsum(dz), first)

    row = pl.BlockSpec((tm, D), lambda i: (i, 0))
    vec = pl.BlockSpec((1, D), lambda i: (0, 0))
    return _call(
        after, body, grid=(S // tm,), in_specs=[row, row, vec, pl.BlockSpec((G, Kb, D), lambda i: (0, 0, 0))],
        out_specs=[row, row, pl.BlockSpec((G, tm, Kb), lambda i: (0, i, 0)), vec, vec, vec],
        out_shape=[_sds((S, D), F32), _sds((S, D), CDT), _sds((G, S, Kb), du_dtype)] + [_sds((1, D), F32)] * 3,
        compiler_params=_params(("arbitrary",)), name=name)(dy, z, gain, w3)


def _half_select(low):
    r = lax.broadcasted_iota(jnp.int32, (2 * ATT_HD, ATT_HD), 0)
    c = lax.broadcasted_iota(jnp.int32, (2 * ATT_HD, ATT_HD), 1)
    return (r == c + (0 if low else ATT_HD)).astype(CDT)


def _half_place(low):
    r = lax.broadcasted_iota(jnp.int32, (ATT_HD, 2 * ATT_HD), 0)
    c = lax.broadcasted_iota(jnp.int32, (ATT_HD, 2 * ATT_HD), 1)
    return (c == r + (0 if low else ATT_HD)).astype(CDT)


def _pair_lanes(even, odd):
    return (jnp.dot(even, _half_place(True), preferred_element_type=F32)
            + jnp.dot(odd, _half_place(False), preferred_element_type=F32)).astype(CDT)


def _proj_heads(a, w, bias, heads, name):
    S, K = a.shape
    N = heads * ATT_HD
    tm = _tile(S)

    def body(a_ref, w_ref, b_ref, o_ref):
        acc = (_dot(a_ref[...], w_ref[...]) + b_ref[...]).astype(CDT)
        sel = (_half_select(True), _half_select(False))
        for h in range(heads):
            pair = acc[:, (h // 2) * 2 * ATT_HD:(h // 2 + 1) * 2 * ATT_HD]
            o_ref[h] = jnp.dot(pair, sel[h % 2], preferred_element_type=F32).astype(CDT)

    return _call(
        None, body, grid=(S // tm,),
        in_specs=[pl.BlockSpec((tm, K), lambda i: (i, 0)), pl.BlockSpec((K, N), lambda i: (0, 0)),
                  pl.BlockSpec((1, N), lambda i: (0, 0))],
        out_specs=pl.BlockSpec((heads, tm, ATT_HD), lambda i: (0, i, 0)), out_shape=_sds((heads, S, ATT_HD), CDT),
        compiler_params=_params(("arbitrary",)), name=name)(a, w, bias)


def _qkv_bwd(dz, dq, dkv4, wq, wkv, name, after=None):
    S = dz.shape[0]
    tm = _tile(S)
    HK = dkv4.shape[0]
    NK = HK * ATT_HD

    def body(dz_ref, dq_ref, dkv_ref, wq_ref, wkv_ref, dx_ref, dkvn_ref, dkvb_ref):
        first = pl.program_id(0) == 0
        dkvn = jnp.concatenate([_pair_lanes(dkv_ref[2 * i].astype(CDT), dkv_ref[2 * i + 1].astype(CDT))
                                for i in range(HK // 2)], axis=-1)
        dkvn_ref[...] = dkvn
        dx_ref[...] = ALPHA * dz_ref[...] + _dot_nt(dq_ref[...], wq_ref[...]) + _dot_nt(dkvn, wkv_ref[...])
        for h in range(HK):
            _acc(dkvb_ref.at[h], _colsum(dkv_ref[h]), first)

    row = pl.BlockSpec((tm, D), lambda i: (i, 0))
    return _call(
        after, body, grid=(S // tm,),
        in_specs=[row, row, pl.BlockSpec((HK, tm, ATT_HD), lambda i: (0, i, 0)),
                  pl.BlockSpec((D, D), lambda i: (0, 0)), pl.BlockSpec((D, NK), lambda i: (0, 0))],
        out_specs=[row, pl.BlockSpec((tm, NK), lambda i: (i, 0)), pl.BlockSpec((HK, 1, ATT_HD), lambda i: (0, 0, 0))],
        out_shape=[_sds((S, D), F32), _sds((S, NK), CDT), _sds((HK, 1, ATT_HD), F32)],
        compiler_params=_params(("arbitrary",)), name=name)(dz, dq, dkv4, wq, wkv)


def _inproj_fwd(xb, wain, name):
    S = xb.shape[0]
    tm = _tile(S) // 2
    nb = wain.shape[-1]

    def body(x_ref, w_ref, o_ref):
        x = x_ref[...]
        for j in range(N_DEV):
            o_ref[j // 2, :, pl.ds((j % 2) * nb, nb)] = _dot(x, w_ref[j])

    return _call(
        None, body, grid=(S // tm,),
        in_specs=[pl.BlockSpec((tm, D), lambda i: (i, 0)), pl.BlockSpec((N_DEV, D, nb), lambda i: (0, 0, 0))],
        out_specs=pl.BlockSpec((4, tm, D), lambda i: (0, i, 0)), out_shape=_sds((4, S, D), F32),
        compiler_params=_params(("arbitrary",)), name=name)(xb, wain)


def _inproj_bwd(dz, dproj, wain, name, after=None):
    S = dz.shape[0]
    tm = _tile(S)
    nb = wain.shape[-1]

    def body(dz_ref, dp_ref, w_ref, dx_ref):
        acc = ALPHA * dz_ref[...]
        for j in range(N_DEV):
            acc = acc + _dot_nt(dp_ref[j // 2, :, pl.ds((j % 2) * nb, nb)], w_ref[j])
        dx_ref[...] = acc

    row = pl.BlockSpec((tm, D), lambda i: (i, 0))
    return _call(
        after, body, grid=(S // tm,),
        in_specs=[row, pl.BlockSpec((4, tm, D), lambda i: (0, i, 0)), pl.BlockSpec((N_DEV, D, nb), lambda i: (0, 0, 0))],
        out_specs=row, out_shape=_sds((S, D), F32),
        compiler_params=_params(("arbitrary",)), name=name)(dz, dproj, wain)


def _running_sum(x, reverse=False):
    rows = x.shape[0]
    row = lax.broadcasted_iota(jnp.int32, x.shape, 0)
    step = 1
    while step < rows:
        if reverse:
            x = x + jnp.where(row < rows - step, pltpu.roll(x, rows - step, 0), 0.0)
        else:
            x = x + jnp.where(row >= step, pltpu.roll(x, step, 0), 0.0)
        step *= 2
    return x


def _hg_gates(q, f, alb_ref):
    a0, a1 = alb_ref[0:1, :], alb_ref[1:2, :]
    mx = jnp.maximum(a0, a1)
    e0, e1 = jnp.exp(a0 - mx), jnp.exp(a1 - mx)
    lb = e0 / (e0 + e1)
    sig = _sigmoid(f)
    forget = lb + (1.0 - lb) * sig
    k = (1.0 - lb) * _sigmoid(-f)
    qs = q * _sigmoid(q) * (HG_DK ** -0.5)
    return qs, k, jnp.log(forget), sig, lb, forget


def _hg_intra(qs, k, b, b_scr):
    b_scr[...] = b
    bm = b_scr[pl.ds(HG_CH // 2 - 1, 1), :]
    bl = b_scr[pl.ds(HG_CH - 1, 1), :]
    eb = jnp.exp(b)
    qb = qs * eb
    e_q = jnp.exp(b - bm)
    e_k = jnp.exp(bm - b)
    e_d = jnp.exp(bl - b)
    return qb, qs * e_q, k * e_k, k * e_d, jnp.exp(bl), eb, e_q, e_k, e_d


def _hgrn_fwd(proj, alb, ngain):
    S = proj.shape[1]
    nc = S // HG_CH
    nb = nc // HG_CPB
    rb, wb = HG_CPB * HG_CH, HG_HPB * HG_DK

    def body(pj_ref, alb_ref, ng_ref, o_ref, y_ref, st_ref, st_scr, b_scr):
        n = pl.program_id(1)

        @pl.when(n == 0)
        def _():
            st_scr[...] = jnp.zeros_like(st_scr)

        r = lax.broadcasted_iota(jnp.int32, (HG_CH, HG_CH), 0)
        c = lax.broadcasted_iota(jnp.int32, (HG_CH, HG_CH), 1)
        causal = r >= c
        for ci, j in [(ci, j) for ci in range(HG_CPB) for j in range(HG_HPB)]:
            rows, lanes = pl.ds(ci * HG_CH, HG_CH), pl.ds(j * HG_DK, HG_DK)
            q, f, v, g = pj_ref[0, rows, lanes], pj_ref[1, rows, lanes], pj_ref[2, rows, lanes], pj_ref[3, rows, lanes]
            qs, k, logf, _, _, _ = _hg_gates(q, f, alb_ref.at[:, lanes])
            b = _running_sum(logf)
            qb, qt, kt, kd, ebl, _, _, _, _ = _hg_intra(qs, k, b, b_scr.at[j, ci])
            st = st_scr[j]
            st_ref[j, ci] = st
            a = jnp.where(causal, _dot_nt(qt, kt), 0.0)
            o = _dot(a, v) + _dot_nt(qb, st)
            st_scr[j] = st * ebl + _dot_tn(v, kd)
            o_ref[rows, lanes] = o
            rinv = lax.rsqrt(jnp.mean(o * o, axis=-1, keepdims=True) + RMS_EPS)
            y_ref[rows, lanes] = (o * rinv * ng_ref[...] * (g * _sigmoid(g))).astype(CDT)

    blk = pl.BlockSpec((rb, wb), lambda h, n: (n, h))
    return _call(
        None, body, grid=(HG_H // HG_HPB, nb),
        in_specs=[pl.BlockSpec((4, rb, wb), lambda h, n: (0, n, h)), pl.BlockSpec((2, wb), lambda h, n: (0, h)),
                  pl.BlockSpec((1, HG_DK), lambda h, n: (0, 0))],
        out_specs=[blk, blk, pl.BlockSpec((HG_HPB, HG_CPB, HG_DK, HG_DK), lambda h, n: (h, n, 0, 0))],
        out_shape=[_sds((S, D), F32), _sds((S, D), CDT), _sds((HG_H, nc, HG_DK, HG_DK), F32)],
        scratch_shapes=[pltpu.VMEM((HG_HPB, HG_DK, HG_DK), F32), pltpu.VMEM((HG_HPB, HG_CPB, HG_CH, HG_DK), F32)],
        compiler_params=_params(("arbitrary", "arbitrary")), name="hgrn_fwd")(proj, alb, ngain)


def _hgrn_bwd(proj, alb, ngain, o, states, dy, after=None):
    S = proj.shape[1]
    nc = S // HG_CH
    nb = nc // HG_CPB
    rb, wb = HG_CPB * HG_CH, HG_HPB * HG_DK

    def body(pj_ref, alb_ref, ng_ref, o_ref, st_ref, dy_ref, dpj_ref, dalb_ref, dng_ref, dst_scr, b_scr):
        h, n = pl.program_id(0), pl.program_id(1)

        @pl.when(n == 0)
        def _():
            dst_scr[...] = jnp.zeros_like(dst_scr)
            dalb_ref[...] = jnp.zeros_like(dalb_ref)

        _zero_at(dng_ref, jnp.logical_and(h == 0, n == 0))
        ng = ng_ref[...]
        r = lax.broadcasted_iota(jnp.int32, (HG_CH, HG_CH), 0)
        c = lax.broadcasted_iota(jnp.int32, (HG_CH, HG_CH), 1)
        causal = r >= c
        dng = None
        for ci, j in [(ci, j) for ci in reversed(range(HG_CPB)) for j in range(HG_HPB)]:
            rows, lanes = pl.ds(ci * HG_CH, HG_CH), pl.ds(j * HG_DK, HG_DK)
            q, f, v, g = pj_ref[0, rows, lanes], pj_ref[1, rows, lanes], pj_ref[2, rows, lanes], pj_ref[3, rows, lanes]
            o_ = o_ref[rows, lanes]
            dy_ = dy_ref[rows, lanes]
            sg = _sigmoid(g)
            rinv = lax.rsqrt(jnp.mean(o_ * o_, axis=-1, keepdims=True) + RMS_EPS)
            nrm = o_ * rinv
            dr = dy_ * (g * sg)
            dg = dy_ * nrm * ng * (sg * (1.0 + g * (1.0 - sg)))
            dn = dr * ng
            do = rinv * (dn - nrm * jnp.mean(dn * nrm, axis=-1, keepdims=True))
            dng = _colsum(dr * nrm) if dng is None else dng + _colsum(dr * nrm)
            qs, k, logf, sig, lb, forget = _hg_gates(q, f, alb_ref.at[:, lanes])
            b = _running_sum(logf)
            qb, qt, kt, kd, ebl, eb, e_q, e_k, e_d = _hg_intra(qs, k, b, b_scr.at[j, ci])
            st = st_ref[j, ci]
            dstn = dst_scr[j]
            qt, kt, qb, kd = (t.astype(CDT).astype(F32) for t in (qt, kt, qb, kd))
            a = jnp.where(causal, _dot_nt(qt, kt), 0.0)
            da = jnp.where(causal, _dot_nt(do, v), 0.0)
            dv = _dot_tn(a, do) + _dot_nt(kd, dstn)
            dqb = _dot(do, st)
            dkd = _dot(v, dstn)
            dqt = _dot(da, kt)
            dkt = _dot_tn(da, qt)
            dbl = _colsum(dkd * kd) + ebl * _colsum(dstn * st)
            dst_scr[j] = dstn * ebl + _dot_tn(do, qb)
            dqs = dqt * e_q + dqb * eb
            dk = dkt * e_k + dkd * e_d
            db = dqt * qt + dqb * qb - dkt * kt - dkd * kd
            dlogf = _running_sum(db, reverse=True) + dbl
            dforget = dlogf / forget
            dsig = (1.0 - lb) * (dforget - dk)
            df = dsig * sig * (1.0 - sig)
            dlb = _colsum((dforget - dk) * (1.0 - sig))
            sq = _sigmoid(q)
            dq = dqs * (HG_DK ** -0.5) * (sq * (1.0 + q * (1.0 - sq)))
            dpj_ref[0, rows, lanes] = dq.astype(CDT)
            dpj_ref[1, rows, lanes] = df.astype(CDT)
            dpj_ref[2, rows, lanes] = dv.astype(CDT)
            dpj_ref[3, rows, lanes] = dg.astype(CDT)
            da0 = dlb * lb * (1.0 - lb)
            dalb_ref[pl.ds(0, 1), lanes] += da0
            dalb_ref[pl.ds(1, 1), lanes] -= da0
        dng_ref[...] += dng

    blk = pl.BlockSpec((rb, wb), lambda h, n: (nb - 1 - n, h))
    pj = pl.BlockSpec((4, rb, wb), lambda h, n: (0, nb - 1 - n, h))
    alb_blk = pl.BlockSpec((2, wb), lambda h, n: (0, h))
    ng_blk = pl.BlockSpec((1, HG_DK), lambda h, n: (0, 0))
    return _call(
        after, body, grid=(HG_H // HG_HPB, nb),
        in_specs=[pj, alb_blk, ng_blk, blk,
                  pl.BlockSpec((HG_HPB, HG_CPB, HG_DK, HG_DK), lambda h, n: (h, nb - 1 - n, 0, 0)), blk],
        out_specs=[pj, alb_blk, ng_blk],
        out_shape=[_sds((4, S, D), CDT), _sds((2, D), F32), _sds((1, HG_DK), F32)],
        scratch_shapes=[pltpu.VMEM((HG_HPB, HG_DK, HG_DK), F32), pltpu.VMEM((HG_HPB, HG_CPB, HG_CH, HG_DK), F32)],
        compiler_params=_params(("arbitrary", "arbitrary")), name="hgrn_bwd")(proj, alb, ngain, o, states, dy)


def _slope(h):
    return 2.0 ** (-8.0 * (h + 1) / ATT_QH)


def _attn_mask(n):
    qi = lax.broadcasted_iota(jnp.int32, (WINDOW, 2 * WINDOW), 0)
    si = lax.broadcasted_iota(jnp.int32, (WINDOW, 2 * WINDOW), 1)
    dist = qi - si + WINDOW
    valid = (dist >= 0) & (dist < WINDOW) & (n * WINDOW - WINDOW + si >= 0)
    return valid, dist.astype(F32)


def _attn_probs(qk, sink, slope, valid, distf):
    s = qk * (ATT_HD ** -0.5) - slope * distf
    s = jnp.where(valid, s, NEG)
    m = jnp.maximum(jnp.max(s, axis=-1, keepdims=True), sink)
    e = jnp.exp(s - m)
    es = jnp.exp(sink - m)
    inv = 1.0 / (jnp.sum(e, axis=-1, keepdims=True) + es)
    return e * inv, es * inv


def _attn_specs(S):
    steps = S // (ATT_BPB * WINDOW)
    cur = lambda H: pl.BlockSpec((H, ATT_BPB * WINDOW, ATT_HD), lambda n: (0, n, 0))
    prev = lambda H: pl.BlockSpec((H, WINDOW, ATT_HD), lambda n: (0, jnp.maximum(ATT_BPB * n - 1, 0), 0))
    return steps, cur, prev


def _attn_kv(kvc_ref, kvp_ref, head, bi):
    before = kvp_ref[head] if bi == 0 else kvc_ref[head, pl.ds((bi - 1) * WINDOW, WINDOW), :]
    return jnp.concatenate([before, kvc_ref[head, pl.ds(bi * WINDOW, WINDOW), :]], axis=0)


def _attn_fwd(q4, kv4, sinks):
    S = q4.shape[1]
    nb, cur, prev = _attn_specs(S)

    def body(sink_ref, q_ref, kvc_ref, kvp_ref, o_ref, p_ref, ps_ref):
        lane = lax.broadcasted_iota(jnp.int32, (1, 128), 1)
        place = (_half_place(True), _half_place(False))
        masks = [_attn_mask(pl.program_id(0) * ATT_BPB + bi) for bi in range(ATT_BPB)]
        sink_probs = [jnp.zeros((WINDOW, 128), F32) for _ in range(ATT_BPB)]
        heads = lambda kvh: range(kvh * ATT_G, (kvh + 1) * ATT_G)

        def scores(bi, kvh):
            rows = pl.ds(bi * WINDOW, WINDOW)
            kh = _attn_kv(kvc_ref, kvp_ref, kvh, bi)
            vh = _attn_kv(kvc_ref, kvp_ref, ATT_KVH + kvh, bi)
            v_pl = [jnp.dot(vh, m, preferred_element_type=F32).astype(CDT) for m in place]
            return v_pl, [_dot_nt(q_ref[h, rows, :], kh) for h in heads(kvh)]

        def softmax(bi, kvh, qks):
            rows = pl.ds(bi * WINDOW, WINDOW)
            probs = []
            for h, qk in zip(heads(kvh), qks):
                p, ps = _attn_probs(qk, sink_ref[0, h], _slope(h), *masks[bi])
                probs.append(p.astype(CDT))
                p_ref[h, rows, :] = probs[-1]
                sink_probs[bi] = sink_probs[bi] + jnp.where(lane == h, ps, 0.0)
            return probs

        def weighted_values(bi, kvh, v_pl, probs):
            rows = pl.ds(bi * WINDOW, WINDOW)
            for i in range(ATT_G // 2):
                lanes = pl.ds((kvh * ATT_G + 2 * i) * ATT_HD, 2 * ATT_HD)
                o_ref[rows, lanes] = (_dot(probs[2 * i], v_pl[0]) + _dot(probs[2 * i + 1], v_pl[1])).astype(CDT)

        groups = [(bi, kvh) for bi in range(ATT_BPB) for kvh in range(ATT_KVH)]
        ahead = scores(*groups[0])
        for gi, g in enumerate(groups):
            v_pl, qks = ahead
            probs = softmax(*g, qks)
            if gi + 1 < len(groups):
                ahead = scores(*groups[gi + 1])
            weighted_values(*g, v_pl, probs)
        for bi in range(ATT_BPB):
            ps_ref[pl.ds(bi * WINDOW, WINDOW), :] = sink_probs[bi]

    rows_spec = pl.BlockSpec((ATT_BPB * WINDOW, D), lambda n: (n, 0))
    return _call(
        None, body, grid=(nb,),
        in_specs=[pl.BlockSpec(memory_space=pltpu.SMEM), cur(ATT_QH), cur(2 * ATT_KVH), prev(2 * ATT_KVH)],
        out_specs=[rows_spec, pl.BlockSpec((ATT_QH, ATT_BPB * WINDOW, 2 * WINDOW), lambda n: (0, n, 0)),
                   pl.BlockSpec((ATT_BPB * WINDOW, 128), lambda n: (n, 0))],
        out_shape=[_sds((S, D), CDT), _sds((ATT_QH, S, 2 * WINDOW), CDT), _sds((S, 128), F32)],
        compiler_params=_params(("arbitrary",)), name="attn_fwd")(sinks, q4, kv4, kv4)


def _attn_bwd(q4, kv4, probs, sink_probs, do):
    S = q4.shape[1]
    nb, cur, prev = _attn_specs(S)

    def body(q_ref, kvc_ref, kvp_ref, p_ref, ps_ref, do_ref, dq_ref, dkv_ref, dbq_ref, dsink_ref):
        n = pl.program_id(0)
        first = n == 0

        @pl.when(first)
        def _():
            dkv_ref[...] = jnp.zeros_like(dkv_ref)
            dsink_ref[...] = jnp.zeros_like(dsink_ref)
            dbq_ref[...] = jnp.zeros_like(dbq_ref)

        lane = lax.broadcasted_iota(jnp.int32, (1, 128), 1)
        place = (_half_place(True), _half_place(False))
        row_dots = [jnp.zeros((WINDOW, 128), F32) for _ in range(ATT_BPB)]
        heads = lambda kvh: range(kvh * ATT_G, (kvh + 1) * ATT_G)
        pair_lanes = lambda h: pl.ds((h // 2) * 2 * ATT_HD, 2 * ATT_HD)

        def products_of_do(bi, kvh):
            rows = pl.ds(bi * WINDOW, WINDOW)
            kh = _attn_kv(kvc_ref, kvp_ref, kvh, bi)
            vh = _attn_kv(kvc_ref, kvp_ref, ATT_KVH + kvh, bi)
            k_pl = [jnp.dot(kh, m, preferred_element_type=F32).astype(CDT) for m in place]
            v_pl = [jnp.dot(vh, m, preferred_element_type=F32).astype(CDT) for m in place]
            dps = [_dot_nt(do_ref[rows, pair_lanes(h)], v_pl[h % 2]) for h in heads(kvh)]
            dv2 = [sum(_dot_tn(p_ref[h, rows, :], do_ref[rows, pair_lanes(h)]) for h in heads(kvh) if h % 2 == par)
                   for par in range(2)]
            return k_pl, dps, dv2

        def softmax_bwd(bi, kvh, dps):
            rows = pl.ds(bi * WINDOW, WINDOW)
            dss = []
            for h, dp in zip(heads(kvh), dps):
                p = p_ref[h, rows, :].astype(F32)
                dd = jnp.sum(p * dp, axis=-1, keepdims=True)
                dss.append((p * (dp - dd)).astype(CDT))
                row_dots[bi] = row_dots[bi] + jnp.where(lane == h, dd, 0.0)
            return dss

        def products_of_ds(bi, kvh, k_pl, dss, dv2):
            block = n * ATT_BPB + bi
            rows = pl.ds(bi * WINDOW, WINDOW)
            rows_cur = pl.ds(pl.multiple_of(block * WINDOW, WINDOW), WINDOW)
            rows_prev = pl.ds(pl.multiple_of(jnp.maximum(block - 1, 0) * WINDOW, WINDOW), WINDOW)
            for i in range(ATT_G // 2):
                h = kvh * ATT_G + 2 * i
                dq2 = (_dot(dss[2 * i], k_pl[0]) + _dot(dss[2 * i + 1], k_pl[1])) * (ATT_HD ** -0.5)
                dq_ref[rows, pair_lanes(h)] = dq2.astype(CDT)
                dbq_ref[:, pair_lanes(h)] += _colsum(dq2)
            dk = sum(_dot_tn(ds, q_ref[h, rows, :]) for h, ds in zip(heads(kvh), dss)) * (ATT_HD ** -0.5)
            dv = dv2[0][:, :ATT_HD] + pltpu.roll(dv2[1], ATT_HD, 1)[:, :ATT_HD]
            dkv_ref[kvh, rows_prev, :] += dk[:WINDOW]
            dkv_ref[kvh, rows_cur, :] += dk[WINDOW:]
            dkv_ref[ATT_KVH + kvh, rows_prev, :] += dv[:WINDOW]
            dkv_ref[ATT_KVH + kvh, rows_cur, :] += dv[WINDOW:]

        groups = [(bi, kvh) for bi in range(ATT_BPB) for kvh in range(ATT_KVH)]
        ahead = products_of_do(*groups[0])
        for gi, g in enumerate(groups):
            k_pl, dps, dv2 = ahead
            dss = softmax_bwd(*g, dps)
            if gi + 1 < len(groups):
                ahead = products_of_do(*groups[gi + 1])
            products_of_ds(*g, k_pl, dss, dv2)
        dsinks = jnp.zeros((1, 128), F32)
        for bi in range(ATT_BPB):
            dsinks = dsinks - _colsum(ps_ref[pl.ds(bi * WINDOW, WINDOW), :] * row_dots[bi])
        dsink_ref[...] += dsinks

    rows_spec = pl.BlockSpec((ATT_BPB * WINDOW, D), lambda n: (n, 0))
    return _call(
        None, body, grid=(nb,),
        in_specs=[cur(ATT_QH), cur(2 * ATT_KVH), prev(2 * ATT_KVH),
                  pl.BlockSpec((ATT_QH, ATT_BPB * WINDOW, 2 * WINDOW), lambda n: (0, n, 0)),
                  pl.BlockSpec((ATT_BPB * WINDOW, 128), lambda n: (n, 0)), rows_spec],
        out_specs=[rows_spec, pl.BlockSpec((2 * ATT_KVH, S, ATT_HD), lambda n: (0, 0, 0)),
                   pl.BlockSpec((1, D), lambda n: (0, 0)), pl.BlockSpec((1, 128), lambda n: (0, 0))],
        out_shape=[_sds((S, D), CDT), _sds((2 * ATT_KVH, S, ATT_HD), F32), _sds((1, D), F32),
                   _sds((1, 128), F32)],
        compiler_params=_params(("arbitrary",)), name="attn_bwd")(q4, kv4, kv4, probs, sink_probs, do)


def _local_step(x, p, target, getw, sm, emit):
    S = x.shape[0]
    vec = lambda a: a.reshape(1, -1)
    ln_g = lambda l, k: vec(sm["ln_gain"][l, k])
    ln_b = lambda l, k: vec(sm["ln_bias"][l, k])
    xb = x.astype(CDT)
    pb = p.astype(CDT)

    proj = _inproj_fwd(xb, getw("a_w_in"), "a_in")
    o_a, y_a, states = _hgrn_fwd(proj, sm["a_lower_bound"], sm["a_norm_gain"])
    zeros = jnp.zeros((1, D), F32)
    z = [[None] * 3 for _ in range(2)]
    xs = [[None] * 3 for _ in range(2)]
    xbs = [[None] * 3 for _ in range(2)]
    z[0][0], xs[0][0], xbs[0][0] = _mixout_ln(y_a[None], getw("a_w_out")[None], zeros, x, ln_g(0, 0), ln_b(0, 0),
                                              "a_out_ln")
    gu, hid, sgs, ups = [None, None], [None, None], [None, None], [None, None]

    def ffn_ple(l, target=None):
        wgu = getw(f"gu{l}")
        gu[l], hid[l], z[l][1], xs[l][1], xbs[l][1] = _ffn_fwd(
            xs[l][0], xbs[l][0], wgu, getw(f"dn{l}"), ln_g(l, 1), ln_b(l, 1), f"ffn_fwd{l}")
        sgs[l], ups[l], z[l][2], *out = _ple_fwd(
            xs[l][1], xbs[l][1], pb[l], getw(f"pg{l}"), vec(sm["ple_b_gate"][l]), getw(f"pu{l}"), ln_g(l, 2),
            ln_b(l, 2), f"ple_fwd{l}", target=target)
        if target is None:
            xs[l][2], xbs[l][2] = out
        return out

    ffn_ple(0)
    x3, x3b = xs[0][2], xbs[0][2]
    w_kv, w_q, w_bo = getw("kv_w"), getw("b_w_q"), getw("b_w_out")
    kv4 = _proj_heads(x3b, w_kv, vec(sm["kv_b"]), 2 * ATT_KVH, "kv_proj")
    q4 = _proj_heads(x3b, w_q, vec(sm["b_b_q"]), ATT_QH, "q_proj")
    o_b, probs, sink_probs = _attn_fwd(q4, kv4, sm["b_sinks"])
    z[1][0], xs[1][0], xbs[1][0] = _mixout_ln(o_b[None], w_bo[None], sm["b_b_out"], x3, ln_g(1, 0), ln_b(1, 0),
                                              "b_out_ln")
    dy, loss = ffn_ple(1, target)

    gs = {}
    d_ln_g = [[None] * 3 for _ in range(2)]
    d_ln_b = [[None] * 3 for _ in range(2)]
    g_bg = [None, None]

    def ffn_ple_bwd(l, dy, after=None):
        dz2, dzb, dgl, dup, d_ln_g[l][2], d_ln_b[l][2], g_bg[l], d_ln_g[l][1], d_ln_b[l][1] = _ple_bwd(
            dy, z[l][2], sgs[l], ups[l], ln_g(l, 2), getw(f"pg{l}"), z[l][1], ln_g(l, 1), f"ple_bwd{l}", after=after)
        g_pg = _wgrad(xbs[l][1][None], dgl[None], f"g_ple_gate{l}")[0]
        g_pu = _wgrad(pb[l][None], dup[None], f"g_ple_up{l}")[0]
        tok = emit({f"pg{l}": g_pg, f"pu{l}": g_pu})
        tok = tok + emit({f"dn{l}": _wgrad(hid[l], dzb[None], f"g_ffn_down{l}")})
        dgu = _ffn_bwd_hidden(dzb, gu[l], getw(f"dn{l}"), f"ffn_bwd{l}", after=tok)
        g_gu = _wgrad(dgu.reshape(8, S, FFN_B), xbs[l][0][None], f"g_ffn_gate_up{l}")
        tok = emit({f"gu{l}": g_gu})
        dx1 = _ffn_bwd_input(dz2, dgu, getw(f"gu{l}"), f"ffn_bwd{l}", after=tok)
        return dx1, None

    dx1, tok = ffn_ple_bwd(1, dy)
    dz, dzb, do, d_ln_g[1][0], d_ln_b[1][0], gs["b_b_out"] = _mixout_bwd(dx1, z[1][0], ln_g(1, 0), w_bo[None], CDT,
                                                                        "b_out_bwd", after=tok)
    g_bo = _wgrad(o_b[None], dzb[None], "g_b_w_out")[0]
    dq, dkv4, dbq, dsinks = _attn_bwd(q4, kv4, probs, sink_probs, do[0])
    gs["b_b_q"] = dbq
    gs["b_sinks"] = dsinks
    g_q = _wgrad(x3b[None], dq[None], "g_b_w_q")[0]
    dx3, dkv, gs["kv_b"] = _qkv_bwd(dz, dq, dkv4, w_q, w_kv, "qkv_bwd", after=tok)
    g_kv = _wgrad(x3b[None], dkv[None], "g_kv_w")[0]
    tok = emit({"b_w_out": g_bo, "b_w_q": g_q, "kv_w": g_kv})
    dx1, tok = ffn_ple_bwd(0, dx3, tok)
    w_ao = getw("a_w_out")
    dz, dzb, dyr, d_ln_g[0][0], d_ln_b[0][0], _ = _mixout_bwd(dx1, z[0][0], ln_g(0, 0), w_ao[None], F32, "a_out_bwd",
                                                              after=tok)
    g_ao = _wgrad(y_a[None], dzb[None], "g_a_w_out")[0]
    tok = emit({"a_w_out": g_ao})
    dproj, gs["a_lower_bound"], gs["a_norm_gain"] = _hgrn_bwd(proj, sm["a_lower_bound"], sm["a_norm_gain"], o_a, states,
                                                              dyr[0], after=tok)
    tk = lambda t: (None, t, D)
    g_ain = _mm_tn(xb[None], dproj, N_DEV, lambda g, k: (0, k, 0), lambda g, k: (g // 2, k, g % 2),
                   tk, lambda t: (None, t, 512), (N_DEV, D, 512), (None, D, 512), lambda g, k: (g, 0, 0), name="g_a_w_in")
    gs["ple_b_gate"] = jnp.concatenate(g_bg, axis=0)
    gs["ln_gain"] = jnp.stack([jnp.concatenate(r, axis=0) for r in d_ln_g])
    gs["ln_bias"] = jnp.stack([jnp.concatenate(r, axis=0) for r in d_ln_b])
    gs["loss"] = loss
    tok = emit({"a_w_in": g_ain}, small=gs)
    grad_x = _inproj_bwd(dz, dproj, getw("a_w_in"), "a_in_bwd", after=tok)
    return loss, grad_x, gs


def _peer(k):
    x, y, c = lax.axis_index("x"), lax.axis_index("y"), lax.axis_index("c")
    px = 1 - x if k & 4 else x
    py = 1 - y if k & 2 else y
    pc = 1 - c if k & 1 else c
    return (px, py, pc), 4 * px + 2 * py + pc


def _my_index():
    return 4 * lax.axis_index("x") + 2 * lax.axis_index("y") + lax.axis_index("c")


def _piece_copy(mode, src, land, send_sems, recv_sems, t, k, sender, receiver, peer):
    return pltpu.make_async_remote_copy(
        src_ref=src if mode == "gather" else src.at[receiver], dst_ref=land.at[sender],
        send_sem=send_sems.at[t * 7 + k - 1], recv_sem=recv_sems.at[t * 7 + k - 1], device_id=peer, device_id_type=MESH)


def _sequencer_exchange(srcs, modes, name, collective_id, after=None):
    n = len(srcs)
    land_shapes = [((N_DEV,) + a.shape) if mode == "gather" else a.shape for a, mode in zip(srcs, modes)]
    extra = [] if after is None else [after]

    def body(*refs):
        src_refs, land_refs = refs[:n], refs[n + len(extra):2 * n + len(extra)]
        send_sems, recv_sems, local_sems = refs[2 * n + len(extra):]
        barrier = pltpu.get_barrier_semaphore()
        for k in range(1, N_DEV):
            pl.semaphore_signal(barrier, inc=1, device_id=_peer(k)[0], device_id_type=MESH)
        pl.semaphore_wait(barrier, N_DEV - 1)
        me = _my_index()
        local = []
        for i in range(n):
            cp = pltpu.make_async_copy(src_refs[i] if modes[i] == "gather" else src_refs[i].at[me], land_refs[i].at[me],
                                       local_sems.at[i])
            cp.start()
            local.append(cp)
        for k in range(1, N_DEV):
            peer, pid = _peer(k)
            for t in range(n):
                _piece_copy(modes[t], src_refs[t], land_refs[t], send_sems, recv_sems, t, k, me, pid, peer).start()
        for k in range(1, N_DEV):
            peer, pid = _peer(k)
            for t in range(n):
                _piece_copy(modes[t], src_refs[t], land_refs[t], send_sems, recv_sems, t, k, pid, me, peer).wait_recv()
        for k in range(1, N_DEV):
            peer, pid = _peer(k)
            for t in range(n):
                _piece_copy(modes[t], src_refs[t], land_refs[t], send_sems, recv_sems, t, k, me, pid, peer).wait_send()
        for cp in local:
            cp.wait()

    return pl.kernel(
        body, out_type=[_sds(s, a.dtype) for s, a in zip(land_shapes, srcs)],
        mesh=plsc.ScalarSubcoreMesh(axis_name="sequencer", num_cores=1),
        scratch_types=[pltpu.SemaphoreType.DMA((7 * n,)), pltpu.SemaphoreType.DMA((7 * n,)), pltpu.SemaphoreType.DMA((n,))],
        compiler_params=pltpu.CompilerParams(collective_id=collective_id), name=name)(*srcs, *extra)


def _sequencer_gather(srcs, name, collective_id, after=None):
    n = len(srcs)
    extra = [] if after is None else [after]

    def body(*refs):
        src_refs, land_refs = refs[:n], refs[n + len(extra):2 * n + len(extra)]
        send_sems, recv_sems, local_sems = refs[2 * n + len(extra):]
        x, y, c = lax.axis_index("x"), lax.axis_index("y"), lax.axis_index("c")
        sibling = (x, y, 1 - c)
        chips = [(1 - x, y), (x, 1 - y), (1 - x, 1 - y)]
        index = lambda px, py, pc: 4 * px + 2 * py + pc
        barrier = pltpu.get_barrier_semaphore()
        for peer in [sibling] + [(*chip, c) for chip in chips]:
            pl.semaphore_signal(barrier, inc=1, device_id=peer, device_id_type=MESH)
        pl.semaphore_wait(barrier, 4)

        def copy(t, k, slot, to, src=None):
            return pltpu.make_async_remote_copy(
                src_ref=land_refs[t].at[slot] if src is None else src, dst_ref=land_refs[t].at[slot],
                send_sem=send_sems.at[7 * t + k], recv_sem=recv_sems.at[7 * t + k], device_id=to, device_id_type=MESH)

        me = index(x, y, c)
        local = []
        for t in range(n):
            cp = pltpu.make_async_copy(src_refs[t], land_refs[t].at[me], local_sems.at[t])
            cp.start()
            local.append(cp)
        sends = []
        for t in range(n):
            sends.append(copy(t, 0, me, sibling, src=src_refs[t]))
            sends += [copy(t, 1 + j, me, (*chip, c), src=src_refs[t]) for j, chip in enumerate(chips)]
        for cp in sends:
            cp.start()
        for j, chip in enumerate(chips):
            for t in range(n):
                copy(t, 1 + j, index(*chip, c), sibling, src=src_refs[t]).wait_recv()
                passed = copy(t, 4 + j, index(*chip, c), sibling)
                passed.start()
                sends.append(passed)
        for t in range(n):
            copy(t, 0, index(x, y, 1 - c), sibling, src=src_refs[t]).wait_recv()
        for j, chip in enumerate(chips):
            for t in range(n):
                copy(t, 4 + j, index(*chip, 1 - c), sibling, src=src_refs[t]).wait_recv()
        for cp in sends:
            cp.wait_send()
        for cp in local:
            cp.wait()

    return pl.kernel(
        body, out_type=[_sds((N_DEV,) + a.shape, a.dtype) for a in srcs],
        mesh=plsc.ScalarSubcoreMesh(axis_name="sequencer", num_cores=1),
        scratch_types=[pltpu.SemaphoreType.DMA((7 * n,)), pltpu.SemaphoreType.DMA((7 * n,)), pltpu.SemaphoreType.DMA((n,))],
        compiler_params=pltpu.CompilerParams(collective_id=collective_id), name=name)(*srcs, *extra)


def _adamw(w, g, m, v):
    m = ADAM_B1 * m + (1.0 - ADAM_B1) * g
    v = ADAM_B2 * v + (1.0 - ADAM_B2) * (g * g)
    m_hat = m / (1.0 - ADAM_B1 ** ADAM_STEP)
    v_hat = v / (1.0 - ADAM_B2 ** ADAM_STEP)
    delta = -ADAM_LR * (m_hat / (jnp.sqrt(v_hat) + ADAM_EPS) + ADAM_WD * w)
    return delta, m, v


def _adam_big(w, parts, m, v, name, after=None):
    L, R, C = w.shape
    P = parts[0].shape[0]
    tr = _tile(R, (256, 128, 176, 64, 32, 16))
    nr = R // tr

    def body(w_ref, *refs):
        p_refs, (m_ref, v_ref, g_ref, d_ref, mo_ref, vo_ref) = refs[:L], refs[L:]
        for l in range(L):
            @pl.when(pl.program_id(0) == l)
            def _(p_ref=p_refs[l]):
                g = p_ref[0].astype(F32)
                for s in range(1, P):
                    g = g + p_ref[s].astype(F32)
                g_ref[...] = g
                d_ref[...], mo_ref[...], vo_ref[...] = _adamw(w_ref[...], g, m_ref[...], v_ref[...])

    row = pl.BlockSpec((None, tr, C), lambda l, i: (l, i, 0))
    park = lambda l_of: (lambda l, i: (0, jnp.where(l == l_of, i, 0 if l_of else nr - 1), 0))
    return _call(
        after, body, grid=(L, nr),
        in_specs=[row] + [pl.BlockSpec((P, tr, C), park(l)) for l in range(L)] + [row, row],
        out_specs=[row] * 4, out_shape=[_sds((L, R, C), F32)] * 4,
        compiler_params=_params(("arbitrary", "arbitrary")), name=name)(w, *parts, m, v)


SMALL = (("a_lower_bound", (2, 128), (2, D)), ("ln_gain", (3, 2, 128), (6, D)), ("ln_bias", (3, 2, 128), (6, D)),
         ("a_norm_gain", (1, 128), (1, 128)), ("kv_b", (1, 512), (1, 512)), ("b_b_q", (1, D), (1, D)),
         ("b_sinks", (1, ATT_QH), (1, 128)), ("b_b_out", (1, D), (1, D)), ("ple_b_gate", (2, D), (2, D)))


def _adam_small(parts, w, m, v, losses, after=None):
    k = len(SMALL)

    def body(*refs):
        p_refs, w_refs, m_refs, v_refs = refs[:k], refs[k:2 * k], refs[2 * k:3 * k], refs[3 * k:4 * k]
        loss_ref, outs, total_ref = refs[4 * k], refs[4 * k + 1:-1], refs[-1]
        total = loss_ref[0]
        for s in range(1, N_DEV):
            total = total + loss_ref[s]
        total_ref[...] = total
        me = _my_index()
        for i, (_, wshape, pshape) in enumerate(SMALL):
            cols = wshape[-1]
            lanes = slice(None) if cols == pshape[1] else (
                pl.ds(0, cols) if cols < 128 else pl.ds(pl.multiple_of(me * cols, cols), cols))
            at = [(slice(None), slice(None))] if len(wshape) == 2 else [
                (pl.ds(l * wshape[0] + kk, 1), (kk, pl.ds(l, 1), slice(None))) for kk in range(wshape[0]) for l in range(wshape[1])]
            for rows, own in at:
                g = p_refs[i][0, rows, lanes]
                for s in range(1, N_DEV):
                    g = g + p_refs[i][s, rows, lanes]
                g_ref, d_ref, mo_ref, vo_ref = outs[4 * i:4 * i + 4]
                g_ref[own] = g
                d_ref[own], mo_ref[own], vo_ref[own] = _adamw(w_refs[i][own], g, m_refs[i][own], v_refs[i][own])

    full = lambda shape: pl.BlockSpec(shape, lambda: (0,) * len(shape))
    names = [n for n, _, _ in SMALL]
    held = lambda a, ws: a.swapaxes(0, 1) if len(ws) == 3 else a.reshape(ws)
    back = lambda r, n: r.swapaxes(0, 1) if r.ndim == 3 else r.reshape(w[n].shape)
    res = _call(
        after, body,
        in_specs=[full((N_DEV,) + ps) for _, _, ps in SMALL] + [full(ws) for _, ws, _ in SMALL] * 3
        + [full((N_DEV, 1, 128))],
        out_specs=[full(ws) for _, ws, _ in SMALL for _ in range(4)] + [full((1, 128))],
        out_shape=[_sds(ws, F32) for _, ws, _ in SMALL for _ in range(4)] + [_sds((1, 128), F32)], name="adam_small")(
            *[parts[n] for n in names], *[held(a[n], ws) for a in (w, m, v) for n, ws, _ in SMALL], losses)
    return {n: [back(r, n) for r in res[4 * i:4 * i + 4]] for i, n in enumerate(names)}, res[-1][0, 0]


WEIGHTS = ("a_w_in", "a_lower_bound", "a_norm_gain", "a_w_out", "kv_w", "kv_b", "b_w_q", "b_b_q", "b_sinks", "b_w_out",
           "b_b_out", "ffn_w_gate_up", "ffn_w_down", "ple_w_up", "ple_w_gate", "ple_b_gate", "ln_gain", "ln_bias")


GATHER_GROUPS = (("a_w_in",), ("a_w_out",), ("gu0",), ("dn0", "pu0", "pg0"), ("kv_w", "b_w_q", "b_w_out"), ("gu1",),
                 ("dn1",), ("pu1", "pg1"))
KERNEL_LAYOUT = {
    "a_w_in": lambda a: a,
    "a_w_out": lambda a: a.reshape(D, D),
    "kv_w": lambda a: a.reshape(D, 2 * ATT_KVH * ATT_HD),
    "b_w_q": lambda a: a.reshape(D, D),
    "b_w_out": lambda a: a.reshape(D, D),
    "gu": lambda a: a.reshape(2, 4, FFN_B, D),
    "dn": lambda a: a.reshape(4, FFN_B, D),
    "pu": lambda a: a,
    "pg": lambda a: a.reshape(D, D),
}
_row_blocks = lambda a: a.reshape(N_DEV, -1, a.shape[-1])
OWNER_BLOCKS = {
    "a_w_in": lambda g: g,
    "a_w_out": _row_blocks,
    "kv_w": _row_blocks,
    "b_w_q": _row_blocks,
    "b_w_out": _row_blocks,
    "gu": lambda g: g,
    "dn": lambda g: _row_blocks(g.reshape(FFN_H, D)),
    "pu": lambda g: g.reshape(PLE_DIM, N_DEV, 128).transpose(1, 0, 2),
    "pg": _row_blocks,
}
ADAM_PARTS = (("kv_w", ("kv_w",)), ("b_w_q", ("b_w_q",)), ("b_w_out", ("b_w_out",)), ("ffn_w_gate_up", ("gu0", "gu1")),
              ("ffn_w_down", ("dn0", "dn1")), ("ple_w_up", ("pu0", "pu1")), ("ple_w_gate", ("pg0", "pg1")),
              ("a_w_out", ("a_w_out",)), ("a_w_in", ("a_w_in",)))


def kernel(x, p, a_w_in, a_lower_bound, a_norm_gain, a_w_out, kv_w, kv_b, b_w_q, b_b_q, b_sinks, b_w_out, b_b_out, ffn_w_gate_up, ffn_w_down, ple_w_up, ple_w_gate, ple_b_gate, ln_gain, ln_bias, loss_target, m_a_w_in, m_a_lower_bound, m_a_norm_gain, m_a_w_out, m_kv_w, m_kv_b, m_b_w_q, m_b_b_q, m_b_sinks, m_b_w_out, m_b_b_out, m_ffn_w_gate_up, m_ffn_w_down, m_ple_w_up, m_ple_w_gate, m_ple_b_gate, m_ln_gain, m_ln_bias, v_a_w_in, v_a_lower_bound, v_a_norm_gain, v_a_w_out, v_kv_w, v_kv_b, v_b_w_q, v_b_b_q, v_b_sinks, v_b_w_out, v_b_b_out, v_ffn_w_gate_up, v_ffn_w_down, v_ple_w_up, v_ple_w_gate, v_ple_b_gate, v_ln_gain, v_ln_bias):
    given = dict(locals())
    w = {n: given[n] for n in WEIGHTS}
    m = {n: given["m_" + n] for n in WEIGHTS}
    v = {n: given["v_" + n] for n in WEIGHTS}
    shards = {"a_w_in": a_w_in[0], "a_w_out": a_w_out[0], "kv_w": kv_w, "b_w_q": b_w_q[0], "b_w_out": b_w_out[0]}
    for l in range(2):
        shards.update({f"gu{l}": ffn_w_gate_up[l].T, f"dn{l}": ffn_w_down[l], f"pu{l}": ple_w_up[l], f"pg{l}": ple_w_gate[l]})
    sharded_small = [a_lower_bound, ln_gain.swapaxes(0, 1), ln_bias.swapaxes(0, 1)]
    gathered = {}
    for gi, g in enumerate(GATHER_GROUPS):
        lands = _sequencer_gather([shards[n].astype(CDT) for n in g] + (sharded_small if gi == 0 else []),
                                  f"gather{gi}", gi)
        for n, a in zip(g, lands):
            gathered[n] = KERNEL_LAYOUT[n.rstrip("01")](a)
        if gi == 0:
            alb = lands[len(g)].transpose(1, 0, 2).reshape(2, D)
            lng, lnb = [a.transpose(2, 1, 0, 3).reshape(2, 3, D) for a in lands[len(g) + 1:]]

    getw = gathered.__getitem__

    sm = {"a_lower_bound": alb, "ln_gain": lng, "ln_bias": lnb,
          "a_norm_gain": a_norm_gain, "kv_b": kv_b, "b_b_q": b_b_q[0], "b_sinks": b_sinks, "b_b_out": b_b_out,
          "ple_b_gate": ple_b_gate}

    scatters, small_parts = [], {}

    def emit(grads, small=None):
        names = list(grads)
        blocks = [OWNER_BLOCKS[n.rstrip("01")](grads[n]) for n in names]
        partials = [] if small is None else [small[n].reshape(ps) for n, _, ps in SMALL] + [small["loss"]]
        lands = _sequencer_exchange(blocks + partials, ["scatter"] * len(blocks) + ["gather"] * len(partials),
                                    f"scatter{len(scatters)}", len(GATHER_GROUPS) + len(scatters))
        scatters.append(dict(zip(names, lands)))
        small_parts.update(zip([n for n, _, _ in SMALL] + ["loss"], lands[len(blocks):]))
        return blocks + (list(scatters[-4].values()) if len(scatters) >= 4 else [])

    loss, grad_x, gs = _local_step(x[0], p[:, 0], loss_target[0], getw, sm, emit)

    out, parts, last = {}, {}, [grad_x]
    for landed in scatters:
        parts.update(landed)
        for n, keys in ADAM_PARTS:
            if n in out or not all(key in parts for key in keys):
                continue
            lrc = (1,) * (3 - w[n].ndim) + w[n].shape
            shard = (lambda a: a.reshape(lrc).swapaxes(1, 2)) if n == "ffn_w_gate_up" else (lambda a: a.reshape(lrc))
            res = _adam_big(shard(w[n]), [parts[key] for key in keys], shard(m[n]), shard(v[n]), "adam_" + n, after=last)
            out[n] = [(r.swapaxes(1, 2) if n == "ffn_w_gate_up" else r).reshape(w[n].shape) for r in res]
            last = [res[3]]
    small_out, loss = _adam_small(small_parts, w, m, v, small_parts["loss"], after=last)
    out.update(small_out)
    res = [loss, grad_x[None]]
    for i in range(4):
        res += [out[n][i] for n in WEIGHTS]
    return tuple(res)
```

```python
import jax
import jax.numpy as jnp
from jax import lax
from jax.experimental import pallas as pl
from jax.experimental.pallas import tpu as pltpu
from jax.experimental.pallas import tpu_sc as plsc

F32 = jnp.float32
CDT = jnp.bfloat16

N_DEV = 8
D = 1024
HG_H, HG_DK, HG_CH = 8, 128, 64
HG_HPB = 4
HG_CPB = 8
ATT_HD, ATT_QH, ATT_KVH, ATT_G, WINDOW = 64, 16, 4, 4, 128
ATT_BPB = 1
FFN_H = 2816
FFN_B = FFN_H // 4
PLE_DIM = 256
ALPHA = (2.0 * 2) ** 0.25
LN_EPS = 1e-5
RMS_EPS = 1e-6
ADAM_LR, ADAM_B1, ADAM_B2, ADAM_EPS, ADAM_WD, ADAM_STEP = 0.001, 0.9, 0.999, 1e-08, 0.01, 10
ROW_TILES = (512, 256, 128, 64)
VMEM_LIMIT = 48 * 1024 * 1024
NEG = -1e30

MESH = pl.DeviceIdType.MESH


def _tile(n, cands=ROW_TILES):
    for t in cands:
        if n % t == 0:
            return t
    return n


def _sds(shape, dtype):
    return jax.ShapeDtypeStruct(tuple(shape), dtype)


def _params(sem):
    return pltpu.CompilerParams(dimension_semantics=sem, vmem_limit_bytes=VMEM_LIMIT)


def _dot(a, b):
    return jnp.dot(a.astype(CDT), b.astype(CDT), preferred_element_type=F32)


def _dot_nt(a, b):
    return lax.dot_general(a.astype(CDT), b.astype(CDT), (((1,), (1,)), ((), ())), preferred_element_type=F32)


def _dot_tn(a, b):
    return lax.dot_general(a.astype(CDT), b.astype(CDT), (((0,), (0,)), ((), ())), preferred_element_type=F32)


def _sigmoid(x):
    return jax.nn.sigmoid(x)


def _ln_fwd(z, g, b):
    mu = jnp.mean(z, axis=-1, keepdims=True)
    zc = z - mu
    var = jnp.mean(zc * zc, axis=-1, keepdims=True)
    return zc * lax.rsqrt(var + LN_EPS) * g + b


def _ln_bwd(z, g, dy):
    mu = jnp.mean(z, axis=-1, keepdims=True)
    zc = z - mu
    var = jnp.mean(zc * zc, axis=-1, keepdims=True)
    rstd = lax.rsqrt(var + LN_EPS)
    xhat = zc * rstd
    dxh = dy * g
    dz = rstd * (dxh - jnp.mean(dxh, axis=-1, keepdims=True) - xhat * jnp.mean(dxh * xhat, axis=-1, keepdims=True))
    return dz, xhat


def _colsum(x):
    return jnp.sum(x, axis=0, keepdims=True)


def _acc(ref, val, first):
    @pl.when(first)
    def _():
        ref[...] = val

    @pl.when(jnp.logical_not(first))
    def _():
        ref[...] += val


def _zero_at(ref, first):
    @pl.when(first)
    def _():
        ref[...] = jnp.zeros_like(ref)


def _call(after, body, **kw):
    after = [] if after is None else list(after)
    specs = list(kw["in_specs"])
    kw["in_specs"] = [pl.BlockSpec(memory_space=pl.ANY)] * len(after) + specs

    def ordered_body(*refs):
        body(*refs[len(after):])

    call = pl.pallas_call(ordered_body, **kw)
    return lambda *args: call(*after, *args)


def _mm_tn(a3, b3, G, amap, bmap, ablock, bblock, out_shape, oblock, omap, name="mm_tn"):
    S = a3.shape[1]

    def body(a_ref, b_ref, o_ref):
        o_ref[...] = _dot_tn(a_ref[...], b_ref[...]).astype(o_ref.dtype)

    return _call(
        None, body, grid=(G, 1),
        in_specs=[pl.BlockSpec(ablock(S), amap), pl.BlockSpec(bblock(S), bmap)],
        out_specs=pl.BlockSpec(oblock, omap), out_shape=_sds(out_shape, CDT),
        compiler_params=_params(("arbitrary", "arbitrary")), name=name)(a3, b3)


def _wgrad(a3, b3, name):
    Ga, S, M = a3.shape
    Gb, _, N = b3.shape
    G = max(Ga, Gb)
    return _mm_tn(
        a3, b3, G,
        (lambda g, k: (g, k, 0)) if Ga > 1 else (lambda g, k: (0, k, 0)),
        (lambda g, k: (g, k, 0)) if Gb > 1 else (lambda g, k: (0, k, 0)),
        lambda tk: (None, tk, M), lambda tk: (None, tk, N),
        (G, M, N), (None, M, N), lambda g, k: (g, 0, 0), name=name)


def _mixout_ln(u3, w3, bias, xin, gain, beta, name):
    G, S, Kb = u3.shape
    tm = _tile(S)

    def body(u_ref, w_ref, b_ref, x_ref, g_ref, be_ref, z_ref, xo_ref, xob_ref):
        h = b_ref[...] + _dot(u_ref[0], w_ref[0])
        for g in range(1, G):
            h = h + _dot(u_ref[g], w_ref[g])
        z = ALPHA * x_ref[...] + h
        z_ref[...] = z
        y = _ln_fwd(z, g_ref[...], be_ref[...])
        xo_ref[...] = y
        xob_ref[...] = y.astype(CDT)

    row = pl.BlockSpec((tm, D), lambda i: (i, 0))
    vec = pl.BlockSpec((1, D), lambda i: (0, 0))
    return _call(
        None, body, grid=(S // tm,),
        in_specs=[pl.BlockSpec((G, tm, Kb), lambda i: (0, i, 0)), pl.BlockSpec((G, Kb, D), lambda i: (0, 0, 0)),
                  vec, row, vec, vec],
        out_specs=[row, row, row], out_shape=[_sds((S, D), F32), _sds((S, D), F32), _sds((S, D), CDT)],
        compiler_params=_params(("arbitrary",)), name=name)(u3, w3, bias, xin, gain, beta)


def _ffn_fwd(xin, xin_b, wgu, wdn, gain, beta, name):
    S = xin.shape[0]
    tm = _tile(S)

    def hidden(xb_ref, wgu_ref, gu_ref, hid_ref):
        xb = xb_ref[...]
        gate = _dot_nt(xb, wgu_ref[0])
        up = _dot_nt(xb, wgu_ref[1])
        gu_ref[0] = gate.astype(CDT)
        gu_ref[1] = up.astype(CDT)
        hid_ref[...] = (gate * _sigmoid(gate) * up).astype(CDT)

    gu, hid = _call(
        None, hidden, grid=(4, S // tm),
        in_specs=[pl.BlockSpec((tm, D), lambda j, i: (i, 0)), pl.BlockSpec((2, None, FFN_B, D), lambda j, i: (0, j, 0, 0))],
        out_specs=[pl.BlockSpec((2, None, tm, FFN_B), lambda j, i: (0, j, i, 0)),
                   pl.BlockSpec((None, tm, FFN_B), lambda j, i: (j, i, 0))],
        out_shape=[_sds((2, 4, S, FFN_B), CDT), _sds((4, S, FFN_B), CDT)],
        compiler_params=_params(("arbitrary", "arbitrary")), name=name + "_hidden")(xin_b, wgu)

    def down(x_ref, hid_ref, wdn_ref, g_ref, be_ref, z_ref, xo_ref, xob_ref):
        z = ALPHA * x_ref[...]
        for j in range(4):
            z = z + _dot(hid_ref[j], wdn_ref[j])
        z_ref[...] = z
        y = _ln_fwd(z, g_ref[...], be_ref[...])
        xo_ref[...] = y
        xob_ref[...] = y.astype(CDT)

    row = pl.BlockSpec((tm, D), lambda i: (i, 0))
    vec = pl.BlockSpec((1, D), lambda i: (0, 0))
    z, xo, xob = _call(
        None, down, grid=(S // tm,),
        in_specs=[row, pl.BlockSpec((4, tm, FFN_B), lambda i: (0, i, 0)), pl.BlockSpec((4, FFN_B, D), lambda i: (0, 0, 0)),
                  vec, vec],
        out_specs=[row, row, row], out_shape=[_sds((S, D), F32), _sds((S, D), F32), _sds((S, D), CDT)],
        compiler_params=_params(("arbitrary",)), name=name + "_down")(xin, hid, wdn, gain, beta)
    return gu, hid, z, xo, xob


def _ple_fwd(xin, xin_b, p_b, wpg, bgate, wpu, gain, beta, name, target=None):
    S = xin.shape[0]
    tm = _tile(S)

    def body(x_ref, xb_ref, p_ref, wpg_ref, bg_ref, wpu_ref, g_ref, be_ref, *rest):
        if target is None:
            sg_ref, up_ref, z_ref, xo_ref, xob_ref = rest
        else:
            t_ref, sg_ref, up_ref, z_ref, dy_ref, l_ref = rest
        sg =_sigmoid(_dot(xb_ref[...], wpg_ref[...]) + bg_ref[...])
        pb = p_ref[...]
        up = jnp.concatenate([_dot(pb, wpu_ref[j]) for j in range(N_DEV)], axis=-1)
        sg_ref[...] = sg.astype(CDT)
        up_ref[...] = (up * sg * (1.0 - sg)).astype(CDT)
        z = ALPHA * x_ref[...] + sg * up
        z_ref[...] = z
        y = _ln_fwd(z, g_ref[...], be_ref[...])
        if target is None:
            xo_ref[...] = y
            xob_ref[...] = y.astype(CDT)
        else:
            e = y - t_ref[...]
            dy_ref[...] = e * (1.0 / D)
            part = 0.5 * jnp.sum(jnp.sum(e * e, axis=-1, keepdims=True) * (1.0 / D), axis=0, keepdims=True)
            _acc(l_ref, jnp.broadcast_to(part, l_ref.shape), pl.program_id(0) == 0)

    row = pl.BlockSpec((tm, D), lambda i: (i, 0))
    vec = pl.BlockSpec((1, D), lambda i: (0, 0))
    last = target is not None
    return _call(
        None, body, grid=(S // tm,),
        in_specs=[row, row, pl.BlockSpec((tm, PLE_DIM), lambda i: (i, 0)), pl.BlockSpec((D, D), lambda i: (0, 0)), vec,
                  pl.BlockSpec((N_DEV, PLE_DIM, D // N_DEV), lambda i: (0, 0, 0)), vec, vec] + [row] * last,
        out_specs=[row] * 4 + [pl.BlockSpec((1, 128), lambda i: (0, 0)) if last else row],
        out_shape=[_sds((S, D), CDT)] * 2 + [_sds((S, D), F32)] * 2 + [_sds((1, 128), F32) if last else _sds((S, D), CDT)],
        compiler_params=_params(("arbitrary",)), name=name)(
            xin, xin_b, p_b, wpg, bgate, wpu, gain, beta, *([target] if last else []))


def _ple_bwd(dy, z, sg, up, gain, wpg, z_ffn, gain_ffn, name, after=None):
    S = dy.shape[0]
    tm = _tile(S)

    def body(dy_ref, z_ref, sg_ref, up_ref, g_ref, wpg_ref, zf_ref, gf_ref, dzf_ref, dzfb_ref, dgl_ref, dup_ref,
             dgain_ref, dbeta_ref, dbg_ref, dgainf_ref, dbetaf_ref):
        first = pl.program_id(0) == 0
        dy_ = dy_ref[...]
        dz, xhat = _ln_bwd(z_ref[...], g_ref[...], dy_)
        dgl = dz * up_ref[...].astype(F32)
        dgl_ref[...] = dgl.astype(CDT)
        dup_ref[...] = (dz * sg_ref[...].astype(F32)).astype(CDT)
        dx = ALPHA * dz + _dot_nt(dgl, wpg_ref[...])
        dzf, xhatf = _ln_bwd(zf_ref[...], gf_ref[...], dx)
        dzf_ref[...] = dzf
        dzfb_ref[...] = dzf.astype(CDT)
        _acc(dgain_ref, _colsum(dy_ * xhat), first)
        _acc(dbeta_ref, _colsum(dy_), first)
        _acc(dbg_ref, _colsum(dgl), first)
        _acc(dgainf_ref, _colsum(dx * xhatf), first)
        _acc(dbetaf_ref, _colsum(dx), first)

    row = pl.BlockSpec((tm, D), lambda i: (i, 0))
    vec = pl.BlockSpec((1, D), lambda i: (0, 0))
    return _call(
        after, body, grid=(S // tm,),
        in_specs=[row, row, row, row, vec, pl.BlockSpec((D, D), lambda i: (0, 0)), row, vec],
        out_specs=[row, row, row, row] + [vec] * 5,
        out_shape=[_sds((S, D), F32)] + [_sds((S, D), CDT)] * 3 + [_sds((1, D), F32)] * 5,
        compiler_params=_params(("arbitrary",)), name=name)(dy, z, sg, up, gain, wpg, z_ffn, gain_ffn)


def _ffn_bwd_hidden(dzb, gu, wdn, name, after=None):
    S = dzb.shape[0]
    tm = _tile(S)

    def hidden(dzb_ref, gu_ref, wdn_ref, dgu_ref):
        dhid = _dot_nt(dzb_ref[...], wdn_ref[...])
        gate, up = gu_ref[0].astype(F32), gu_ref[1].astype(F32)
        sg = _sigmoid(gate)
        dgu_ref[0] = (dhid * up * (sg * (1.0 + gate * (1.0 - sg)))).astype(CDT)
        dgu_ref[1] = (dhid * (gate * sg)).astype(CDT)

    blocks = pl.BlockSpec((2, None, tm, FFN_B), lambda j, i: (0, j, i, 0))
    return _call(
        after, hidden, grid=(4, S // tm),
        in_specs=[pl.BlockSpec((tm, D), lambda j, i: (i, 0)), blocks, pl.BlockSpec((None, FFN_B, D), lambda j, i: (j, 0, 0))],
        out_specs=blocks, out_shape=_sds((2, 4, S, FFN_B), CDT),
        compiler_params=_params(("arbitrary", "arbitrary")), name=name + "_hidden")(dzb, gu, wdn)


def _ffn_bwd_input(dz, dgu, wgu, name, after=None):
    S = dz.shape[0]
    tm = _tile(S)

    def to_input(dz_ref, dgu_ref, wgu_ref, dx_ref):
        acc = ALPHA * dz_ref[...]
        for g in range(2):
            for j in range(4):
                acc = acc + _dot(dgu_ref[g, j], wgu_ref[g, j])
        dx_ref[...] = acc

    rows = pl.BlockSpec((tm, D), lambda i: (i, 0))
    return _call(
        after, to_input, grid=(S // tm,),
        in_specs=[rows, pl.BlockSpec((2, 4, tm, FFN_B), lambda i: (0, 0, i, 0)),
                  pl.BlockSpec((2, 4, FFN_B, D), lambda i: (0, 0, 0, 0))],
        out_specs=rows, out_shape=_sds((S, D), F32),
        compiler_params=_params(("arbitrary",)), name=name + "_input")(dz, dgu, wgu)


def _mixout_bwd(dy, z, gain, w3, du_dtype, name, after=None):
    S = dy.shape[0]
    G, Kb, _ = w3.shape
    tm = _tile(S)

    def body(dy_ref, z_ref, g_ref, w_ref, dz_ref, dzb_ref, du_ref, dgain_ref, dbeta_ref, dbias_ref):
        first = pl.program_id(0) == 0
        dy_ = dy_ref[...]
        dz, xhat = _ln_bwd(z_ref[...], g_ref[...], dy_)
        dz_ref[...] = dz
        dzb = dz.astype(CDT)
        dzb_ref[...] = dzb
        for g in range(G):
            du_ref[g] = _dot_nt(dzb, w_ref[g]).astype(du_ref.dtype)
        _acc(dgain_ref, _colsum(dy_ * xhat), first)
        _acc(dbeta_ref, _colsum(dy_), first)
        _acc(dbias_ref, _colsum(dz), first)

    row = pl.BlockSpec((tm, D), lambda i: (i, 0))
    vec = pl.BlockSpec((1, D), lambda i: (0, 0))
    return _call(
        after, body, grid=(S // tm,), in_specs=[row, row, vec, pl.BlockSpec((G, Kb, D), lambda i: (0, 0, 0))],
        out_specs=[row, row, pl.BlockSpec((G, tm, Kb), lambda i: (0, i, 0)), vec, vec, vec],
        out_shape=[_sds((S, D), F32), _sds((S, D), CDT), _sds((G, S, Kb), du_dtype)] + [_sds((1, D), F32)] * 3,
        compiler_params=_params(("arbitrary",)), name=name)(dy, z, gain, w3)


def _half_select(low):
    r = lax.broadcasted_iota(jnp.int32, (2 * ATT_HD, ATT_HD), 0)
    c = lax.broadcasted_iota(jnp.int32, (2 * ATT_HD, ATT_HD), 1)
    return (r == c + (0 if low else ATT_HD)).astype(CDT)


def _half_place(low):
    r = lax.broadcasted_iota(jnp.int32, (ATT_HD, 2 * ATT_HD), 0)
    c = lax.broadcasted_iota(jnp.int32, (ATT_HD, 2 * ATT_HD), 1)
    return (c == r + (0 if low else ATT_HD)).astype(CDT)


def _pair_lanes(even, odd):
    return (jnp.dot(even, _half_place(True), preferred_element_type=F32)
            + jnp.dot(odd, _half_place(False), preferred_element_type=F32)).astype(CDT)


def _proj_heads(a, w, bias, heads, name):
    S, K = a.shape
    N = heads * ATT_HD
    tm = _tile(S)

    def body(a_ref, w_ref, b_ref, o_ref):
        acc = (_dot(a_ref[...], w_ref[...]) + b_ref[...]).astype(CDT)
        sel = (_half_select(True), _half_select(False))
        for h in range(heads):
            pair = acc[:, (h // 2) * 2 * ATT_HD:(h // 2 + 1) * 2 * ATT_HD]
            o_ref[h] = jnp.dot(pair, sel[h % 2], preferred_element_type=F32).astype(CDT)

    return _call(
        None, body, grid=(S // tm,),
        in_specs=[pl.BlockSpec((tm, K), lambda i: (i, 0)), pl.BlockSpec((K, N), lambda i: (0, 0)),
                  pl.BlockSpec((1, N), lambda i: (0, 0))],
        out_specs=pl.BlockSpec((heads, tm, ATT_HD), lambda i: (0, i, 0)), out_shape=_sds((heads, S, ATT_HD), CDT),
        compiler_params=_params(("arbitrary",)), name=name)(a, w, bias)


def _qkv_bwd(dz, dq, dkv4, wq, wkv, name, after=None):
    S = dz.shape[0]
    tm = _tile(S)
    HK = dkv4.shape[0]
    NK = HK * ATT_HD

    def body(dz_ref, dq_ref, dkv_ref, wq_ref, wkv_ref, dx_ref, dkvn_ref, dkvb_ref):
        first = pl.program_id(0) == 0
        dkvn = jnp.concatenate([_pair_lanes(dkv_ref[2 * i].astype(CDT), dkv_ref[2 * i + 1].astype(CDT))
                                for i in range(HK // 2)], axis=-1)
        dkvn_ref[...] = dkvn
        dx_ref[...] = ALPHA * dz_ref[...] + _dot_nt(dq_ref[...], wq_ref[...]) + _dot_nt(dkvn, wkv_ref[...])
        for h in range(HK):
            _acc(dkvb_ref.at[h], _colsum(dkv_ref[h]), first)

    row = pl.BlockSpec((tm, D), lambda i: (i, 0))
    return _call(
        after, body, grid=(S // tm,),
        in_specs=[row, row, pl.BlockSpec((HK, tm, ATT_HD), lambda i: (0, i, 0)),
                  pl.BlockSpec((D, D), lambda i: (0, 0)), pl.BlockSpec((D, NK), lambda i: (0, 0))],
        out_specs=[row, pl.BlockSpec((tm, NK), lambda i: (i, 0)), pl.BlockSpec((HK, 1, ATT_HD), lambda i: (0, 0, 0))],
        out_shape=[_sds((S, D), F32), _sds((S, NK), CDT), _sds((HK, 1, ATT_HD), F32)],
        compiler_params=_params(("arbitrary",)), name=name)(dz, dq, dkv4, wq, wkv)


def _inproj_fwd(xb, wain, name):
    S = xb.shape[0]
    tm = _tile(S) // 2
    nb = wain.shape[-1]

    def body(x_ref, w_ref, o_ref):
        x = x_ref[...]
        for j in range(N_DEV):
            o_ref[j // 2, :, pl.ds((j % 2) * nb, nb)] = _dot(x, w_ref[j])

    return _call(
        None, body, grid=(S // tm,),
        in_specs=[pl.BlockSpec((tm, D), lambda i: (i, 0)), pl.BlockSpec((N_DEV, D, nb), lambda i: (0, 0, 0))],
        out_specs=pl.BlockSpec((4, tm, D), lambda i: (0, i, 0)), out_shape=_sds((4, S, D), F32),
        compiler_params=_params(("arbitrary",)), name=name)(xb, wain)


def _inproj_bwd(dz, dproj, wain, name, after=None):
    S = dz.shape[0]
    tm = _tile(S)
    nb = wain.shape[-1]

    def body(dz_ref, dp_ref, w_ref, dx_ref):
        acc = ALPHA * dz_ref[...]
        for j in range(N_DEV):
            acc = acc + _dot_nt(dp_ref[j // 2, :, pl.ds((j % 2) * nb, nb)], w_ref[j])
        dx_ref[...] = acc

    row = pl.BlockSpec((tm, D), lambda i: (i, 0))
    return _call(
        after, body, grid=(S // tm,),
        in_specs=[row, pl.BlockSpec((4, tm, D), lambda i: (0, i, 0)), pl.BlockSpec((N_DEV, D, nb), lambda i: (0, 0, 0))],
        out_specs=row, out_shape=_sds((S, D), F32),
        compiler_params=_params(("arbitrary",)), name=name)(dz, dproj, wain)


def _running_sum(x, reverse=False):
    rows = x.shape[0]
    row = lax.broadcasted_iota(jnp.int32, x.shape, 0)
    step = 1
    while step < rows:
        if reverse:
            x = x + jnp.where(row < rows - step, pltpu.roll(x, rows - step, 0), 0.0)
        else:
            x = x + jnp.where(row >= step, pltpu.roll(x, step, 0), 0.0)
        step *= 2
    return x


def _hg_gates(q, f, alb_ref):
    a0, a1 = alb_ref[0:1, :], alb_ref[1:2, :]
    mx = jnp.maximum(a0, a1)
    e0, e1 = jnp.exp(a0 - mx), jnp.exp(a1 - mx)
    lb = e0 / (e0 + e1)
    sig = _sigmoid(f)
    forget = lb + (1.0 - lb) * sig
    k = (1.0 - lb) * _sigmoid(-f)
    qs = q * _sigmoid(q) * (HG_DK ** -0.5)
    return qs, k, jnp.log(forget), sig, lb, forget


def _hg_intra(qs, k, b, b_scr):
    b_scr[...] = b
    bm = b_scr[pl.ds(HG_CH // 2 - 1, 1), :]
    bl = b_scr[pl.ds(HG_CH - 1, 1), :]
    eb = jnp.exp(b)
    qb = qs * eb
    e_q = jnp.exp(b - bm)
    e_k = jnp.exp(bm - b)
    e_d = jnp.exp(bl - b)
    return qb, qs * e_q, k * e_k, k * e_d, jnp.exp(bl), eb, e_q, e_k, e_d


def _hgrn_fwd(proj, alb, ngain):
    S = proj.shape[1]
    nc = S // HG_CH
    nb = nc // HG_CPB
    rb, wb = HG_CPB * HG_CH, HG_HPB * HG_DK

    def body(pj_ref, alb_ref, ng_ref, o_ref, y_ref, st_ref, st_scr, b_scr):
        n = pl.program_id(1)

        @pl.when(n == 0)
        def _():
            st_scr[...] = jnp.zeros_like(st_scr)

        r = lax.broadcasted_iota(jnp.int32, (HG_CH, HG_CH), 0)
        c = lax.broadcasted_iota(jnp.int32, (HG_CH, HG_CH), 1)
        causal = r >= c
        for ci, j in [(ci, j) for ci in range(HG_CPB) for j in range(HG_HPB)]:
            rows, lanes = pl.ds(ci * HG_CH, HG_CH), pl.ds(j * HG_DK, HG_DK)
            q, f, v, g = pj_ref[0, rows, lanes], pj_ref[1, rows, lanes], pj_ref[2, rows, lanes], pj_ref[3, rows, lanes]
            qs, k, logf, _, _, _ = _hg_gates(q, f, alb_ref.at[:, lanes])
            b = _running_sum(logf)
            qb, qt, kt, kd, ebl, _, _, _, _ = _hg_intra(qs, k, b, b_scr.at[j, ci])
            st = st_scr[j]
            st_ref[j, ci] = st
            a = jnp.where(causal, _dot_nt(qt, kt), 0.0)
            o = _dot(a, v) + _dot_nt(qb, st)
            st_scr[j] = st * ebl + _dot_tn(v, kd)
            o_ref[rows, lanes] = o
            rinv = lax.rsqrt(jnp.mean(o * o, axis=-1, keepdims=True) + RMS_EPS)
            y_ref[rows, lanes] = (o * rinv * ng_ref[...] * (g * _sigmoid(g))).astype(CDT)

    blk = pl.BlockSpec((rb, wb), lambda h, n: (n, h))
    return _call(
        None, body, grid=(HG_H // HG_HPB, nb),
        in_specs=[pl.BlockSpec((4, rb, wb), lambda h, n: (0, n, h)), pl.BlockSpec((2, wb), lambda h, n: (0, h)),
                  pl.BlockSpec((1, HG_DK), lambda h, n: (0, 0))],
        out_specs=[blk, blk, pl.BlockSpec((HG_HPB, HG_CPB, HG_DK, HG_DK), lambda h, n: (h, n, 0, 0))],
        out_shape=[_sds((S, D), F32), _sds((S, D), CDT), _sds((HG_H, nc, HG_DK, HG_DK), F32)],
        scratch_shapes=[pltpu.VMEM((HG_HPB, HG_DK, HG_DK), F32), pltpu.VMEM((HG_HPB, HG_CPB, HG_CH, HG_DK), F32)],
        compiler_params=_params(("arbitrary", "arbitrary")), name="hgrn_fwd")(proj, alb, ngain)


def _hgrn_bwd(proj, alb, ngain, o, states, dy, after=None):
    S = proj.shape[1]
    nc = S // HG_CH
    nb = nc // HG_CPB
    rb, wb = HG_CPB * HG_CH, HG_HPB * HG_DK

    def body(pj_ref, alb_ref, ng_ref, o_ref, st_ref, dy_ref, dpj_ref, dalb_ref, dng_ref, dst_scr, b_scr):
        h, n = pl.program_id(0), pl.program_id(1)

        @pl.when(n == 0)
        def _():
            dst_scr[...] = jnp.zeros_like(dst_scr)
            dalb_ref[...] = jnp.zeros_like(dalb_ref)

        _zero_at(dng_ref, jnp.logical_and(h == 0, n == 0))
        ng = ng_ref[...]
        r = lax.broadcasted_iota(jnp.int32, (HG_CH, HG_CH), 0)
        c = lax.broadcasted_iota(jnp.int32, (HG_CH, HG_CH), 1)
        causal = r >= c
        dng = None
        for ci, j in [(ci, j) for ci in reversed(range(HG_CPB)) for j in range(HG_HPB)]:
            rows, lanes = pl.ds(ci * HG_CH, HG_CH), pl.ds(j * HG_DK, HG_DK)
            q, f, v, g = pj_ref[0, rows, lanes], pj_ref[1, rows, lanes], pj_ref[2, rows, lanes], pj_ref[3, rows, lanes]
            o_ = o_ref[rows, lanes]
            dy_ = dy_ref[rows, lanes]
            sg = _sigmoid(g)
            rinv = lax.rsqrt(jnp.mean(o_ * o_, axis=-1, keepdims=True) + RMS_EPS)
            nrm = o_ * rinv
            dr = dy_ * (g * sg)
            dg = dy_ * nrm * ng * (sg * (1.0 + g * (1.0 - sg)))
            dn = dr * ng
            do = rinv * (dn - nrm * jnp.mean(dn * nrm, axis=-1, keepdims=True))
            dng = _colsum(dr * nrm) if dng is None else dng + _colsum(dr * nrm)
            qs, k, logf, sig, lb, forget = _hg_gates(q, f, alb_ref.at[:, lanes])
            b = _running_sum(logf)
            qb, qt, kt, kd, ebl, eb, e_q, e_k, e_d = _hg_intra(qs, k, b, b_scr.at[j, ci])
            st = st_ref[j, ci]
            dstn = dst_scr[j]
            qt, kt, qb, kd = (t.astype(CDT).astype(F32) for t in (qt, kt, qb, kd))
            a = jnp.where(causal, _dot_nt(qt, kt), 0.0)
            da = jnp.where(causal, _dot_nt(do, v), 0.0)
            dv = _dot_tn(a, do) + _dot_nt(kd, dstn)
            dqb = _dot(do, st)
            dkd = _dot(v, dstn)
            dqt = _dot(da, kt)
            dkt = _dot_tn(da, qt)
            dbl = _colsum(dkd * kd) + ebl * _colsum(dstn * st)
            dst_scr[j] = dstn * ebl + _dot_tn(do, qb)
            dqs = dqt * e_q + dqb * eb
            dk = dkt * e_k + dkd * e_d
            db = dqt * qt + dqb * qb - dkt * kt - dkd * kd
            dlogf = _running_sum(db, reverse=True) + dbl
            dforget = dlogf / forget
            dsig = (1.0 - lb) * (dforget - dk)
            df = dsig * sig * (1.0 - sig)
            dlb = _colsum((dforget - dk) * (1.0 - sig))
            sq = _sigmoid(q)
            dq = dqs * (HG_DK ** -0.5) * (sq * (1.0 + q * (1.0 - sq)))
            dpj_ref[0, rows, lanes] = dq.astype(CDT)
            dpj_ref[1, rows, lanes] = df.astype(CDT)
            dpj_ref[2, rows, lanes] = dv.astype(CDT)
            dpj_ref[3, rows, lanes] = dg.astype(CDT)
            da0 = dlb * lb * (1.0 - lb)
            dalb_ref[pl.ds(0, 1), lanes] += da0
            dalb_ref[pl.ds(1, 1), lanes] -= da0
        dng_ref[...] += dng

    blk = pl.BlockSpec((rb, wb), lambda h, n: (nb - 1 - n, h))
    pj = pl.BlockSpec((4, rb, wb), lambda h, n: (0, nb - 1 - n, h))
    alb_blk = pl.BlockSpec((2, wb), lambda h, n: (0, h))
    ng_blk = pl.BlockSpec((1, HG_DK), lambda h, n: (0, 0))
    return _call(
        after, body, grid=(HG_H // HG_HPB, nb),
        in_specs=[pj, alb_blk, ng_blk, blk,
                  pl.BlockSpec((HG_HPB, HG_CPB, HG_DK, HG_DK), lambda h, n: (h, nb - 1 - n, 0, 0)), blk],
        out_specs=[pj, alb_blk, ng_blk],
        out_shape=[_sds((4, S, D), CDT), _sds((2, D), F32), _sds((1, HG_DK), F32)],
        scratch_shapes=[pltpu.VMEM((HG_HPB, HG_DK, HG_DK), F32), pltpu.VMEM((HG_HPB, HG_CPB, HG_CH, HG_DK), F32)],
        compiler_params=_params(("arbitrary", "arbitrary")), name="hgrn_bwd")(proj, alb, ngain, o, states, dy)


def _slope(h):
    return 2.0 ** (-8.0 * (h + 1) / ATT_QH)


def _attn_mask(n):
    qi = lax.broadcasted_iota(jnp.int32, (WINDOW, 2 * WINDOW), 0)
    si = lax.broadcasted_iota(jnp.int32, (WINDOW, 2 * WINDOW), 1)
    dist = qi - si + WINDOW
    valid = (dist >= 0) & (dist < WINDOW) & (n * WINDOW - WINDOW + si >= 0)
    return valid, dist.astype(F32)


def _attn_probs(qk, sink, slope, valid, distf):
    s = qk * (ATT_HD ** -0.5) - slope * distf
    s = jnp.where(valid, s, NEG)
    m = jnp.maximum(jnp.max(s, axis=-1, keepdims=True), sink)
    e = jnp.exp(s - m)
    es = jnp.exp(sink - m)
    inv = 1.0 / (jnp.sum(e, axis=-1, keepdims=True) + es)
    return e * inv, es * inv


def _attn_specs(S):
    steps = S // (ATT_BPB * WINDOW)
    cur = lambda H: pl.BlockSpec((H, ATT_BPB * WINDOW, ATT_HD), lambda n: (0, n, 0))
    prev = lambda H: pl.BlockSpec((H, WINDOW, ATT_HD), lambda n: (0, jnp.maximum(ATT_BPB * n - 1, 0), 0))
    return steps, cur, prev


def _attn_kv(kvc_ref, kvp_ref, head, bi):
    before = kvp_ref[head] if bi == 0 else kvc_ref[head, pl.ds((bi - 1) * WINDOW, WINDOW), :]
    return jnp.concatenate([before, kvc_ref[head, pl.ds(bi * WINDOW, WINDOW), :]], axis=0)


def _attn_fwd(q4, kv4, sinks):
    S = q4.shape[1]
    nb, cur, prev = _attn_specs(S)

    def body(sink_ref, q_ref, kvc_ref, kvp_ref, o_ref, p_ref, ps_ref):
        lane = lax.broadcasted_iota(jnp.int32, (1, 128), 1)
        place = (_half_place(True), _half_place(False))
        masks = [_attn_mask(pl.program_id(0) * ATT_BPB + bi) for bi in range(ATT_BPB)]
        sink_probs = [jnp.zeros((WINDOW, 128), F32) for _ in range(ATT_BPB)]
        heads = lambda kvh: range(kvh * ATT_G, (kvh + 1) * ATT_G)

        def scores(bi, kvh):
            rows = pl.ds(bi * WINDOW, WINDOW)
            kh = _attn_kv(kvc_ref, kvp_ref, kvh, bi)
            vh = _attn_kv(kvc_ref, kvp_ref, ATT_KVH + kvh, bi)
            v_pl = [jnp.dot(vh, m, preferred_element_type=F32).astype(CDT) for m in place]
            return v_pl, [_dot_nt(q_ref[h, rows, :], kh) for h in heads(kvh)]

        def softmax(bi, kvh, qks):
            rows = pl.ds(bi * WINDOW, WINDOW)
            probs = []
            for h, qk in zip(heads(kvh), qks):
                p, ps = _attn_probs(qk, sink_ref[0, h], _slope(h), *masks[bi])
                probs.append(p.astype(CDT))
                p_ref[h, rows, :] = probs[-1]
                sink_probs[bi] = sink_probs[bi] + jnp.where(lane == h, ps, 0.0)
            return probs

        def weighted_values(bi, kvh, v_pl, probs):
            rows = pl.ds(bi * WINDOW, WINDOW)
            for i in range(ATT_G // 2):
                lanes = pl.ds((kvh * ATT_G + 2 * i) * ATT_HD, 2 * ATT_HD)
                o_ref[rows, lanes] = (_dot(probs[2 * i], v_pl[0]) + _dot(probs[2 * i + 1], v_pl[1])).astype(CDT)

        groups = [(bi, kvh) for bi in range(ATT_BPB) for kvh in range(ATT_KVH)]
        ahead = scores(*groups[0])
        for gi, g in enumerate(groups):
            v_pl, qks = ahead
            probs = softmax(*g, qks)
            if gi + 1 < len(groups):
                ahead = scores(*groups[gi + 1])
            weighted_values(*g, v_pl, probs)
        for bi in range(ATT_BPB):
            ps_ref[pl.ds(bi * WINDOW, WINDOW), :] = sink_probs[bi]

    rows_spec = pl.BlockSpec((ATT_BPB * WINDOW, D), lambda n: (n, 0))
    return _call(
        None, body, grid=(nb,),
        in_specs=[pl.BlockSpec(memory_space=pltpu.SMEM), cur(ATT_QH), cur(2 * ATT_KVH), prev(2 * ATT_KVH)],
        out_specs=[rows_spec, pl.BlockSpec((ATT_QH, ATT_BPB * WINDOW, 2 * WINDOW), lambda n: (0, n, 0)),
                   pl.BlockSpec((ATT_BPB * WINDOW, 128), lambda n: (n, 0))],
        out_shape=[_sds((S, D), CDT), _sds((ATT_QH, S, 2 * WINDOW), CDT), _sds((S, 128), F32)],
        compiler_params=_params(("arbitrary",)), name="attn_fwd")(sinks, q4, kv4, kv4)


def _attn_bwd(q4, kv4, probs, sink_probs, do):
    S = q4.shape[1]
    nb, cur, prev = _attn_specs(S)

    def body(q_ref, kvc_ref, kvp_ref, p_ref, ps_ref, do_ref, dq_ref, dkv_ref, dbq_ref, dsink_ref):
        n = pl.program_id(0)
        first = n == 0

        @pl.when(first)
        def _():
            dkv_ref[...] = jnp.zeros_like(dkv_ref)
            dsink_ref[...] = jnp.zeros_like(dsink_ref)
            dbq_ref[...] = jnp.zeros_like(dbq_ref)

        lane = lax.broadcasted_iota(jnp.int32, (1, 128), 1)
        place = (_half_place(True), _half_place(False))
        row_dots = [jnp.zeros((WINDOW, 128), F32) for _ in range(ATT_BPB)]
        heads = lambda kvh: range(kvh * ATT_G, (kvh + 1) * ATT_G)
        pair_lanes = lambda h: pl.ds((h // 2) * 2 * ATT_HD, 2 * ATT_HD)

        def products_of_do(bi, kvh):
            rows = pl.ds(bi * WINDOW, WINDOW)
            kh = _attn_kv(kvc_ref, kvp_ref, kvh, bi)
            vh = _attn_kv(kvc_ref, kvp_ref, ATT_KVH + kvh, bi)
            k_pl = [jnp.dot(kh, m, preferred_element_type=F32).astype(CDT) for m in place]
            v_pl = [jnp.dot(vh, m, preferred_element_type=F32).astype(CDT) for m in place]
            dps = [_dot_nt(do_ref[rows, pair_lanes(h)], v_pl[h % 2]) for h in heads(kvh)]
            dv2 = [sum(_dot_tn(p_ref[h, rows, :], do_ref[rows, pair_lanes(h)]) for h in heads(kvh) if h % 2 == par)
                   for par in range(2)]
            return k_pl, dps, dv2

        def softmax_bwd(bi, kvh, dps):
            rows = pl.ds(bi * WINDOW, WINDOW)
            dss = []
            for h, dp in zip(heads(kvh), dps):
                p = p_ref[h, rows, :].astype(F32)
                dd = jnp.sum(p * dp, axis=-1, keepdims=True)
                dss.append((p * (dp - dd)).astype(CDT))
                row_dots[bi] = row_dots[bi] + jnp.where(lane == h, dd, 0.0)
            return dss

        def products_of_ds(bi, kvh, k_pl, dss, dv2):
            block = n * ATT_BPB + bi
            rows = pl.ds(bi * WINDOW, WINDOW)
            rows_cur = pl.ds(pl.multiple_of(block * WINDOW, WINDOW), WINDOW)
            rows_prev = pl.ds(pl.multiple_of(jnp.maximum(block - 1, 0) * WINDOW, WINDOW), WINDOW)
            for i in range(ATT_G // 2):
                h = kvh * ATT_G + 2 * i
                dq2 = (_dot(dss[2 * i], k_pl[0]) + _dot(dss[2 * i + 1], k_pl[1])) * (ATT_HD ** -0.5)
                dq_ref[rows, pair_lanes(h)] = dq2.astype(CDT)
                dbq_ref[:, pair_lanes(h)] += _colsum(dq2)
            dk = sum(_dot_tn(ds, q_ref[h, rows, :]) for h, ds in zip(heads(kvh), dss)) * (ATT_HD ** -0.5)
            dv = dv2[0][:, :ATT_HD] + pltpu.roll(dv2[1], ATT_HD, 1)[:, :ATT_HD]
            dkv_ref[kvh, rows_prev, :] += dk[:WINDOW]
            dkv_ref[kvh, rows_cur, :] += dk[WINDOW:]
            dkv_ref[ATT_KVH + kvh, rows_prev, :] += dv[:WINDOW]
            dkv_ref[ATT_KVH + kvh, rows_cur, :] += dv[WINDOW:]

        groups = [(bi, kvh) for bi in range(ATT_BPB) for kvh in range(ATT_KVH)]
        ahead = products_of_do(*groups[0])
        for gi, g in enumerate(groups):
            k_pl, dps, dv2 = ahead
            dss = softmax_bwd(*g, dps)
            if gi + 1 < len(groups):
                ahead = products_of_do(*groups[gi + 1])
            products_of_ds(*g, k_pl, dss, dv2)
        dsinks = jnp.zeros((1, 128), F32)
        for bi in range(ATT_BPB):
            dsinks = dsinks - _colsum(ps_ref[pl.ds(bi * WINDOW, WINDOW), :] * row_dots[bi])
        dsink_ref[...] += dsinks

    rows_spec = pl.BlockSpec((ATT_BPB * WINDOW, D), lambda n: (n, 0))
    return _call(
        None, body, grid=(nb,),
        in_specs=[cur(ATT_QH), cur(2 * ATT_KVH), prev(2 * ATT_KVH),
                  pl.BlockSpec((ATT_QH, ATT_BPB * WINDOW, 2 * WINDOW), lambda n: (0, n, 0)),
                  pl.BlockSpec((ATT_BPB * WINDOW, 128), lambda n: (n, 0)), rows_spec],
        out_specs=[rows_spec, pl.BlockSpec((2 * ATT_KVH, S, ATT_HD), lambda n: (0, 0, 0)),
                   pl.BlockSpec((1, D), lambda n: (0, 0)), pl.BlockSpec((1, 128), lambda n: (0, 0))],
        out_shape=[_sds((S, D), CDT), _sds((2 * ATT_KVH, S, ATT_HD), F32), _sds((1, D), F32),
                   _sds((1, 128), F32)],
        compiler_params=_params(("arbitrary",)), name="attn_bwd")(q4, kv4, kv4, probs, sink_probs, do)


def _local_step(x, p, target, getw, sm, emit):
    S = x.shape[0]
    vec = lambda a: a.reshape(1, -1)
    ln_g = lambda l, k: vec(sm["ln_gain"][l, k])
    ln_b = lambda l, k: vec(sm["ln_bias"][l, k])
    xb = x.astype(CDT)
    pb = p.astype(CDT)

    proj = _inproj_fwd(xb, getw("a_w_in"), "a_in")
    o_a, y_a, states = _hgrn_fwd(proj, sm["a_lower_bound"], sm["a_norm_gain"])
    zeros = jnp.zeros((1, D), F32)
    z = [[None] * 3 for _ in range(2)]
    xs = [[None] * 3 for _ in range(2)]
    xbs = [[None] * 3 for _ in range(2)]
    z[0][0], xs[0][0], xbs[0][0] = _mixout_ln(y_a[None], getw("a_w_out")[None], zeros, x, ln_g(0, 0), ln_b(0, 0),
                                              "a_out_ln")
    gu, hid, sgs, ups = [None, None], [None, None], [None, None], [None, None]

    def ffn_ple(l, target=None):
        wgu = getw(f"gu{l}")
        gu[l], hid[l], z[l][1], xs[l][1], xbs[l][1] = _ffn_fwd(
            xs[l][0], xbs[l][0], wgu, getw(f"dn{l}"), ln_g(l, 1), ln_b(l, 1), f"ffn_fwd{l}")
        sgs[l], ups[l], z[l][2], *out = _ple_fwd(
            xs[l][1], xbs[l][1], pb[l], getw(f"pg{l}"), vec(sm["ple_b_gate"][l]), getw(f"pu{l}"), ln_g(l, 2),
            ln_b(l, 2), f"ple_fwd{l}", target=target)
        if target is None:
            xs[l][2], xbs[l][2] = out
        return out

    ffn_ple(0)
    x3, x3b = xs[0][2], xbs[0][2]
    w_kv, w_q, w_bo = getw("kv_w"), getw("b_w_q"), getw("b_w_out")
    kv4 = _proj_heads(x3b, w_kv, vec(sm["kv_b"]), 2 * ATT_KVH, "kv_proj")
    q4 = _proj_heads(x3b, w_q, vec(sm["b_b_q"]), ATT_QH, "q_proj")
    o_b, probs, sink_probs = _attn_fwd(q4, kv4, sm["b_sinks"])
    z[1][0], xs[1][0], xbs[1][0] = _mixout_ln(o_b[None], w_bo[None], sm["b_b_out"], x3, ln_g(1, 0), ln_b(1, 0),
                                              "b_out_ln")
    dy, loss = ffn_ple(1, target)

    gs = {}
    d_ln_g = [[None] * 3 for _ in range(2)]
    d_ln_b = [[None] * 3 for _ in range(2)]
    g_bg = [None, None]

    def ffn_ple_bwd(l, dy, after=None):
        dz2, dzb, dgl, dup, d_ln_g[l][2], d_ln_b[l][2], g_bg[l], d_ln_g[l][1], d_ln_b[l][1] = _ple_bwd(
            dy, z[l][2], sgs[l], ups[l], ln_g(l, 2), getw(f"pg{l}"), z[l][1], ln_g(l, 1), f"ple_bwd{l}", after=after)
        g_pg = _wgrad(xbs[l][1][None], dgl[None], f"g_ple_gate{l}")[0]
        g_pu = _wgrad(pb[l][None], dup[None], f"g_ple_up{l}")[0]
        tok = emit({f"pg{l}": g_pg, f"pu{l}": g_pu})
        tok = tok + emit({f"dn{l}": _wgrad(hid[l], dzb[None], f"g_ffn_down{l}")})
        dgu = _ffn_bwd_hidden(dzb, gu[l], getw(f"dn{l}"), f"ffn_bwd{l}", after=tok)
        g_gu = _wgrad(dgu.reshape(8, S, FFN_B), xbs[l][0][None], f"g_ffn_gate_up{l}")
        tok = emit({f"gu{l}": g_gu})
        dx1 = _ffn_bwd_input(dz2, dgu, getw(f"gu{l}"), f"ffn_bwd{l}", after=tok)
        return dx1, None

    dx1, tok = ffn_ple_bwd(1, dy)
    dz, dzb, do, d_ln_g[1][0], d_ln_b[1][0], gs["b_b_out"] = _mixout_bwd(dx1, z[1][0], ln_g(1, 0), w_bo[None], CDT,
                                                                        "b_out_bwd", after=tok)
    g_bo = _wgrad(o_b[None], dzb[None], "g_b_w_out")[0]
    dq, dkv4, dbq, dsinks = _attn_bwd(q4, kv4, probs, sink_probs, do[0])
    gs["b_b_q"] = dbq
    gs["b_sinks"] = dsinks
    g_q = _wgrad(x3b[None], dq[None], "g_b_w_q")[0]
    dx3, dkv, gs["kv_b"] = _qkv_bwd(dz, dq, dkv4, w_q, w_kv, "qkv_bwd", after=tok)
    g_kv = _wgrad(x3b[None], dkv[None], "g_kv_w")[0]
    tok = emit({"b_w_out": g_bo, "b_w_q": g_q, "kv_w": g_kv})
    dx1, tok = ffn_ple_bwd(0, dx3, tok)
    w_ao = getw("a_w_out")
    dz, dzb, dyr, d_ln_g[0][0], d_ln_b[0][0], _ = _mixout_bwd(dx1, z[0][0], ln_g(0, 0), w_ao[None], F32, "a_out_bwd",
                                                              after=tok)
    g_ao = _wgrad(y_a[None], dzb[None], "g_a_w_out")[0]
    tok = emit({"a_w_out": g_ao})
    dproj, gs["a_lower_bound"], gs["a_norm_gain"] = _hgrn_bwd(proj, sm["a_lower_bound"], sm["a_norm_gain"], o_a, states,
                                                              dyr[0], after=tok)
    tk = lambda t: (None, t, D)
    g_ain = _mm_tn(xb[None], dproj, N_DEV, lambda g, k: (0, k, 0), lambda g, k: (g // 2, k, g % 2),
                   tk, lambda t: (None, t, 512), (N_DEV, D, 512), (None, D, 512), lambda g, k: (g, 0, 0), name="g_a_w_in")
    gs["ple_b_gate"] = jnp.concatenate(g_bg, axis=0)
    gs["ln_gain"] = jnp.stack([jnp.concatenate(r, axis=0) for r in d_ln_g])
    gs["ln_bias"] = jnp.stack([jnp.concatenate(r, axis=0) for r in d_ln_b])
    gs["loss"] = loss
    tok = emit({"a_w_in": g_ain}, small=gs)
    grad_x = _inproj_bwd(dz, dproj, getw("a_w_in"), "a_in_bwd", after=tok)
    return loss, grad_x, gs


def _peer(k):
    x, y, c = lax.axis_index("x"), lax.axis_index("y"), lax.axis_index("c")
    px = 1 - x if k & 4 else x
    py = 1 - y if k & 2 else y
    pc = 1 - c if k & 1 else c
    return (px, py, pc), 4 * px + 2 * py + pc


def _my_index():
    return 4 * lax.axis_index("x") + 2 * lax.axis_index("y") + lax.axis_index("c")


def _piece_copy(mode, src, land, send_sems, recv_sems, t, k, sender, receiver, peer):
    return pltpu.make_async_remote_copy(
        src_ref=src if mode == "gather" else src.at[receiver], dst_ref=land.at[sender],
        send_sem=send_sems.at[t * 7 + k - 1], recv_sem=recv_sems.at[t * 7 + k - 1], device_id=peer, device_id_type=MESH)


def _sequencer_exchange(srcs, modes, name, collective_id, after=None):
    n = len(srcs)
    land_shapes = [((N_DEV,) + a.shape) if mode == "gather" else a.shape for a, mode in zip(srcs, modes)]
    extra = [] if after is None else [after]

    def body(*refs):
        src_refs, land_refs = refs[:n], refs[n + len(extra):2 * n + len(extra)]
        send_sems, recv_sems, local_sems = refs[2 * n + len(extra):]
        barrier = pltpu.get_barrier_semaphore()
        for k in range(1, N_DEV):
            pl.semaphore_signal(barrier, inc=1, device_id=_peer(k)[0], device_id_type=MESH)
        pl.semaphore_wait(barrier, N_DEV - 1)
        me = _my_index()
        local = []
        for i in range(n):
            cp = pltpu.make_async_copy(src_refs[i] if modes[i] == "gather" else src_refs[i].at[me], land_refs[i].at[me],
                                       local_sems.at[i])
            cp.start()
            local.append(cp)
        for k in range(1, N_DEV):
            peer, pid = _peer(k)
            for t in range(n):
                _piece_copy(modes[t], src_refs[t], land_refs[t], send_sems, recv_sems, t, k, me, pid, peer).start()
        for k in range(1, N_DEV):
            peer, pid = _peer(k)
            for t in range(n):
                _piece_copy(modes[t], src_refs[t], land_refs[t], send_sems, recv_sems, t, k, pid, me, peer).wait_recv()
        for k in range(1, N_DEV):
            peer, pid = _peer(k)
            for t in range(n):
                _piece_copy(modes[t], src_refs[t], land_refs[t], send_sems, recv_sems, t, k, me, pid, peer).wait_send()
        for cp in local:
            cp.wait()

    return pl.kernel(
        body, out_type=[_sds(s, a.dtype) for s, a in zip(land_shapes, srcs)],
        mesh=plsc.ScalarSubcoreMesh(axis_name="sequencer", num_cores=1),
        scratch_types=[pltpu.SemaphoreType.DMA((7 * n,)), pltpu.SemaphoreType.DMA((7 * n,)), pltpu.SemaphoreType.DMA((n,))],
        compiler_params=pltpu.CompilerParams(collective_id=collective_id), name=name)(*srcs, *extra)


def _sequencer_gather(srcs, name, collective_id, after=None):
    n = len(srcs)
    extra = [] if after is None else [after]

    def body(*refs):
        src_refs, land_refs = refs[:n], refs[n + len(extra):2 * n + len(extra)]
        send_sems, recv_sems, local_sems = refs[2 * n + len(extra):]
        x, y, c = lax.axis_index("x"), lax.axis_index("y"), lax.axis_index("c")
        sibling = (x, y, 1 - c)
        chips = [(1 - x, y), (x, 1 - y), (1 - x, 1 - y)]
        index = lambda px, py, pc: 4 * px + 2 * py + pc
        barrier = pltpu.get_barrier_semaphore()
        for peer in [sibling] + [(*chip, c) for chip in chips]:
            pl.semaphore_signal(barrier, inc=1, device_id=peer, device_id_type=MESH)
        pl.semaphore_wait(barrier, 4)

        def copy(t, k, slot, to, src=None):
            return pltpu.make_async_remote_copy(
                src_ref=land_refs[t].at[slot] if src is None else src, dst_ref=land_refs[t].at[slot],
                send_sem=send_sems.at[7 * t + k], recv_sem=recv_sems.at[7 * t + k], device_id=to, device_id_type=MESH)

        me = index(x, y, c)
        local = []
        for t in range(n):
            cp = pltpu.make_async_copy(src_refs[t], land_refs[t].at[me], local_sems.at[t])
            cp.start()
            local.append(cp)
        sends = []
        for t in range(n):
            sends.append(copy(t, 0, me, sibling, src=src_refs[t]))
            sends += [copy(t, 1 + j, me, (*chip, c), src=src_refs[t]) for j, chip in enumerate(chips)]
        for cp in sends:
            cp.start()
        for j, chip in enumerate(chips):
            for t in range(n):
                copy(t, 1 + j, index(*chip, c), sibling, src=src_refs[t]).wait_recv()
                passed = copy(t, 4 + j, index(*chip, c), sibling)
                passed.start()
                sends.append(passed)
        for t in range(n):
            copy(t, 0, index(x, y, 1 - c), sibling, src=src_refs[t]).wait_recv()
        for j, chip in enumerate(chips):
            for t in range(n):
                copy(t, 4 + j, index(*chip, 1 - c), sibling, src=src_refs[t]).wait_recv()
        for cp in sends:
            cp.wait_send()
        for cp in local:
            cp.wait()

    return pl.kernel(
        body, out_type=[_sds((N_DEV,) + a.shape, a.dtype) for a in srcs],
        mesh=plsc.ScalarSubcoreMesh(axis_name="sequencer", num_cores=1),
        scratch_types=[pltpu.SemaphoreType.DMA((7 * n,)), pltpu.SemaphoreType.DMA((7 * n,)), pltpu.SemaphoreType.DMA((n,))],
        compiler_params=pltpu.CompilerParams(collective_id=collective_id), name=name)(*srcs, *extra)


def _adamw(w, g, m, v):
    m = ADAM_B1 * m + (1.0 - ADAM_B1) * g
    v = ADAM_B2 * v + (1.0 - ADAM_B2) * (g * g)
    m_hat = m / (1.0 - ADAM_B1 ** ADAM_STEP)
    v_hat = v / (1.0 - ADAM_B2 ** ADAM_STEP)
    delta = -ADAM_LR * (m_hat / (jnp.sqrt(v_hat) + ADAM_EPS) + ADAM_WD * w)
    return delta, m, v


def _adam_big(w, parts, m, v, name, after=None):
    L, R, C = w.shape
    P = parts[0].shape[0]
    tr = _tile(R, (256, 128, 176, 64, 32, 16))
    nr = R // tr

    def body(w_ref, *refs):
        p_refs, (m_ref, v_ref, g_ref, d_ref, mo_ref, vo_ref) = refs[:L], refs[L:]
        for l in range(L):
            @pl.when(pl.program_id(0) == l)
            def _(p_ref=p_refs[l]):
                g = p_ref[0].astype(F32)
                for s in range(1, P):
                    g = g + p_ref[s].astype(F32)
                g_ref[...] = g
                d_ref[...], mo_ref[...], vo_ref[...] = _adamw(w_ref[...], g, m_ref[...], v_ref[...])

    row = pl.BlockSpec((None, tr, C), lambda l, i: (l, i, 0))
    park = lambda l_of: (lambda l, i: (0, jnp.where(l == l_of, i, 0 if l_of else nr - 1), 0))
    return _call(
        after, body, grid=(L, nr),
        in_specs=[row] + [pl.BlockSpec((P, tr, C), park(l)) for l in range(L)] + [row, row],
        out_specs=[row] * 4, out_shape=[_sds((L, R, C), F32)] * 4,
        compiler_params=_params(("arbitrary", "arbitrary")), name=name)(w, *parts, m, v)


SMALL = (("a_lower_bound", (2, 128), (2, D)), ("ln_gain", (3, 2, 128), (6, D)), ("ln_bias", (3, 2, 128), (6, D)),
         ("a_norm_gain", (1, 128), (1, 128)), ("kv_b", (1, 512), (1, 512)), ("b_b_q", (1, D), (1, D)),
         ("b_sinks", (1, ATT_QH), (1, 128)), ("b_b_out", (1, D), (1, D)), ("ple_b_gate", (2, D), (2, D)))


def _adam_small(parts, w, m, v, losses, after=None):
    k = len(SMALL)

    def body(*refs):
        p_refs, w_refs, m_refs, v_refs = refs[:k], refs[k:2 * k], refs[2 * k:3 * k], refs[3 * k:4 * k]
        loss_ref, outs, total_ref = refs[4 * k], refs[4 * k + 1:-1], refs[-1]
        total = loss_ref[0]
        for s in range(1, N_DEV):
            total = total + loss_ref[s]
        total_ref[...] = total
        me = _my_index()
        for i, (_, wshape, pshape) in enumerate(SMALL):
            cols = wshape[-1]
            lanes = slice(None) if cols == pshape[1] else (
                pl.ds(0, cols) if cols < 128 else pl.ds(pl.multiple_of(me * cols, cols), cols))
            at = [(slice(None), slice(None))] if len(wshape) == 2 else [
                (pl.ds(l * wshape[0] + kk, 1), (kk, pl.ds(l, 1), slice(None))) for kk in range(wshape[0]) for l in range(wshape[1])]
            for rows, own in at:
                g = p_refs[i][0, rows, lanes]
                for s in range(1, N_DEV):
                    g = g + p_refs[i][s, rows, lanes]
                g_ref, d_ref, mo_ref, vo_ref = outs[4 * i:4 * i + 4]
                g_ref[own] = g
                d_ref[own], mo_ref[own], vo_ref[own] = _adamw(w_refs[i][own], g, m_refs[i][own], v_refs[i][own])

    full = lambda shape: pl.BlockSpec(shape, lambda: (0,) * len(shape))
    names = [n for n, _, _ in SMALL]
    held = lambda a, ws: a.swapaxes(0, 1) if len(ws) == 3 else a.reshape(ws)
    back = lambda r, n: r.swapaxes(0, 1) if r.ndim == 3 else r.reshape(w[n].shape)
    res = _call(
        after, body,
        in_specs=[full((N_DEV,) + ps) for _, _, ps in SMALL] + [full(ws) for _, ws, _ in SMALL] * 3
        + [full((N_DEV, 1, 128))],
        out_specs=[full(ws) for _, ws, _ in SMALL for _ in range(4)] + [full((1, 128))],
        out_shape=[_sds(ws, F32) for _, ws, _ in SMALL for _ in range(4)] + [_sds((1, 128), F32)], name="adam_small")(
            *[parts[n] for n in names], *[held(a[n], ws) for a in (w, m, v) for n, ws, _ in SMALL], losses)
    return {n: [back(r, n) for r in res[4 * i:4 * i + 4]] for i, n in enumerate(names)}, res[-1][0, 0]


WEIGHTS = ("a_w_in", "a_lower_bound", "a_norm_gain", "a_w_out", "kv_w", "kv_b", "b_w_q", "b_b_q", "b_sinks", "b_w_out",
           "b_b_out", "ffn_w_gate_up", "ffn_w_down", "ple_w_up", "ple_w_gate", "ple_b_gate", "ln_gain", "ln_bias")


GATHER_GROUPS = (("a_w_in",), ("a_w_out", "gu0"), ("dn0", "pu0", "pg0"), ("kv_w", "b_w_q", "b_w_out"), ("gu1",),
                 ("dn1", "pu1", "pg1"))
KERNEL_LAYOUT = {
    "a_w_in": lambda a: a,
    "a_w_out": lambda a: a.reshape(D, D),
    "kv_w": lambda a: a.reshape(D, 2 * ATT_KVH * ATT_HD),
    "b_w_q": lambda a: a.reshape(D, D),
    "b_w_out": lambda a: a.reshape(D, D),
    "gu": lambda a: a.reshape(2, 4, FFN_B, D),
    "dn": lambda a: a.reshape(4, FFN_B, D),
    "pu": lambda a: a,
    "pg": lambda a: a.reshape(D, D),
}
_row_blocks = lambda a: a.reshape(N_DEV, -1, a.shape[-1])
OWNER_BLOCKS = {
    "a_w_in": lambda g: g,
    "a_w_out": _row_blocks,
    "kv_w": _row_blocks,
    "b_w_q": _row_blocks,
    "b_w_out": _row_blocks,
    "gu": lambda g: g,
    "dn": lambda g: _row_blocks(g.reshape(FFN_H, D)),
    "pu": lambda g: g.reshape(PLE_DIM, N_DEV, 128).transpose(1, 0, 2),
    "pg": _row_blocks,
}
ADAM_PARTS = (("kv_w", ("kv_w",)), ("b_w_q", ("b_w_q",)), ("b_w_out", ("b_w_out",)), ("ffn_w_gate_up", ("gu0", "gu1")),
              ("ffn_w_down", ("dn0", "dn1")), ("ple_w_up", ("pu0", "pu1")), ("ple_w_gate", ("pg0", "pg1")),
              ("a_w_out", ("a_w_out",)), ("a_w_in", ("a_w_in",)))


def kernel(x, p, a_w_in, a_lower_bound, a_norm_gain, a_w_out, kv_w, kv_b, b_w_q, b_b_q, b_sinks, b_w_out, b_b_out, ffn_w_gate_up, ffn_w_down, ple_w_up, ple_w_gate, ple_b_gate, ln_gain, ln_bias, loss_target, m_a_w_in, m_a_lower_bound, m_a_norm_gain, m_a_w_out, m_kv_w, m_kv_b, m_b_w_q, m_b_b_q, m_b_sinks, m_b_w_out, m_b_b_out, m_ffn_w_gate_up, m_ffn_w_down, m_ple_w_up, m_ple_w_gate, m_ple_b_gate, m_ln_gain, m_ln_bias, v_a_w_in, v_a_lower_bound, v_a_norm_gain, v_a_w_out, v_kv_w, v_kv_b, v_b_w_q, v_b_b_q, v_b_sinks, v_b_w_out, v_b_b_out, v_ffn_w_gate_up, v_ffn_w_down, v_ple_w_up, v_ple_w_gate, v_ple_b_gate, v_ln_gain, v_ln_bias):
    given = dict(locals())
    w = {n: given[n] for n in WEIGHTS}
    m = {n: given["m_" + n] for n in WEIGHTS}
    v = {n: given["v_" + n] for n in WEIGHTS}
    shards = {"a_w_in": a_w_in[0], "a_w_out": a_w_out[0], "kv_w": kv_w, "b_w_q": b_w_q[0], "b_w_out": b_w_out[0]}
    for l in range(2):
        shards.update({f"gu{l}": ffn_w_gate_up[l].T, f"dn{l}": ffn_w_down[l], f"pu{l}": ple_w_up[l], f"pg{l}": ple_w_gate[l]})
    sharded_small = [a_lower_bound, ln_gain.swapaxes(0, 1), ln_bias.swapaxes(0, 1)]
    gathered = {}
    for gi, g in enumerate(GATHER_GROUPS):
        lands = _sequencer_gather([shards[n].astype(CDT) for n in g] + (sharded_small if gi == 0 else []),
                                  f"gather{gi}", gi)
        for n, a in zip(g, lands):
            gathered[n] = KERNEL_LAYOUT[n.rstrip("01")](a)
        if gi == 0:
            alb = lands[len(g)].transpose(1, 0, 2).reshape(2, D)
            lng, lnb = [a.transpose(2, 1, 0, 3).reshape(2, 3, D) for a in lands[len(g) + 1:]]

    getw = gathered.__getitem__

    sm = {"a_lower_bound": alb, "ln_gain": lng, "ln_bias": lnb,
          "a_norm_gain": a_norm_gain, "kv_b": kv_b, "b_b_q": b_b_q[0], "b_sinks": b_sinks, "b_b_out": b_b_out,
          "ple_b_gate": ple_b_gate}

    scatters, small_parts = [], {}

    def emit(grads, small=None):
        names = list(grads)
        blocks = [OWNER_BLOCKS[n.rstrip("01")](grads[n]) for n in names]
        partials = [] if small is None else [small[n].reshape(ps) for n, _, ps in SMALL] + [small["loss"]]
        lands = _sequencer_exchange(blocks + partials, ["scatter"] * len(blocks) + ["gather"] * len(partials),
                                    f"scatter{len(scatters)}", len(GATHER_GROUPS) + len(scatters))
        scatters.append(dict(zip(names, lands)))
        small_parts.update(zip([n for n, _, _ in SMALL] + ["loss"], lands[len(blocks):]))
        return blocks + (list(scatters[-4].values()) if len(scatters) >= 4 else [])

    loss, grad_x, gs = _local_step(x[0], p[:, 0], loss_target[0], getw, sm, emit)

    out, parts, last = {}, {}, [grad_x]
    for landed in scatters:
        parts.update(landed)
        for n, keys in ADAM_PARTS:
            if n in out or not all(key in parts for key in keys):
                continue
            lrc = (1,) * (3 - w[n].ndim) + w[n].shape
            shard = (lambda a: a.reshape(lrc).swapaxes(1, 2)) if n == "ffn_w_gate_up" else (lambda a: a.reshape(lrc))
            res = _adam_big(shard(w[n]), [parts[key] for key in keys], shard(m[n]), shard(v[n]), "adam_" + n, after=last)
            out[n] = [(r.swapaxes(1, 2) if n == "ffn_w_gate_up" else r).reshape(w[n].shape) for r in res]
            last = [res[3]]
    small_out, loss = _adam_small(small_parts, w, m, v, small_parts["loss"], after=last)
    out.update(small_out)
    res = [loss, grad_x[None]]
    for i in range(4):
        res += [out[n][i] for n in WEIGHTS]
    return tuple(res)
```

```python
import jax
import jax.numpy as jnp
from jax import lax
from jax.experimental import pallas as pl
from jax.experimental.pallas import tpu as pltpu
from jax.experimental.pallas import tpu_sc as plsc

F32 = jnp.float32
CDT = jnp.bfloat16

N_DEV = 8
D = 1024
HG_H, HG_DK, HG_CH = 8, 128, 64
HG_HPB = 4
HG_CPB = 8
ATT_HD, ATT_QH, ATT_KVH, ATT_G, WINDOW = 64, 16, 4, 4, 128
ATT_BPB = 1
FFN_H = 2816
FFN_B = FFN_H // 4
PLE_DIM = 256
ALPHA = (2.0 * 2) ** 0.25
LN_EPS = 1e-5
RMS_EPS = 1e-6
ADAM_LR, ADAM_B1, ADAM_B2, ADAM_EPS, ADAM_WD, ADAM_STEP = 0.001, 0.9, 0.999, 1e-08, 0.01, 10
ROW_TILES = (512, 256, 128, 64)
VMEM_LIMIT = 60 * 1024 * 1024
NEG = -1e30

MESH = pl.DeviceIdType.MESH


def _tile(n, cands=ROW_TILES):
    for t in cands:
        if n % t == 0:
            return t
    return n


def _sds(shape, dtype):
    return jax.ShapeDtypeStruct(tuple(shape), dtype)


def _params(sem):
    return pltpu.CompilerParams(dimension_semantics=sem, vmem_limit_bytes=VMEM_LIMIT)


def _dot(a, b):
    return jnp.dot(a.astype(CDT), b.astype(CDT), preferred_element_type=F32)


def _dot_nt(a, b):
    return lax.dot_general(a.astype(CDT), b.astype(CDT), (((1,), (1,)), ((), ())), preferred_element_type=F32)


def _dot_tn(a, b):
    return lax.dot_general(a.astype(CDT), b.astype(CDT), (((0,), (0,)), ((), ())), preferred_element_type=F32)


def _sigmoid(x):
    return jax.nn.sigmoid(x)


def _ln_fwd(z, g, b):
    mu = jnp.mean(z, axis=-1, keepdims=True)
    zc = z - mu
    var = jnp.mean(zc * zc, axis=-1, keepdims=True)
    return zc * lax.rsqrt(var + LN_EPS) * g + b


def _ln_bwd(z, g, dy):
    mu = jnp.mean(z, axis=-1, keepdims=True)
    zc = z - mu
    var = jnp.mean(zc * zc, axis=-1, keepdims=True)
    rstd = lax.rsqrt(var + LN_EPS)
    xhat = zc * rstd
    dxh = dy * g
    dz = rstd * (dxh - jnp.mean(dxh, axis=-1, keepdims=True) - xhat * jnp.mean(dxh * xhat, axis=-1, keepdims=True))
    return dz, xhat


def _colsum(x):
    return jnp.sum(x, axis=0, keepdims=True)


def _acc(ref, val, first):
    @pl.when(first)
    def _():
        ref[...] = val

    @pl.when(jnp.logical_not(first))
    def _():
        ref[...] += val


def _zero_at(ref, first):
    @pl.when(first)
    def _():
        ref[...] = jnp.zeros_like(ref)


def _call(after, body, **kw):
    after = [] if after is None else list(after)
    specs = list(kw["in_specs"])
    kw["in_specs"] = [pl.BlockSpec(memory_space=pl.ANY)] * len(after) + specs

    def ordered_body(*refs):
        body(*refs[len(after):])

    call = pl.pallas_call(ordered_body, **kw)
    return lambda *args: call(*after, *args)


def _mm_tn(a3, b3, G, amap, bmap, ablock, bblock, out_shape, oblock, omap, name="mm_tn"):
    S = a3.shape[1]

    def body(a_ref, b_ref, o_ref):
        o_ref[...] = _dot_tn(a_ref[...], b_ref[...]).astype(o_ref.dtype)

    return _call(
        None, body, grid=(G, 1),
        in_specs=[pl.BlockSpec(ablock(S), amap), pl.BlockSpec(bblock(S), bmap)],
        out_specs=pl.BlockSpec(oblock, omap), out_shape=_sds(out_shape, CDT),
        compiler_params=_params(("arbitrary", "arbitrary")), name=name)(a3, b3)


def _wgrad(a3, b3, name):
    Ga, S, M = a3.shape
    Gb, _, N = b3.shape
    G = max(Ga, Gb)
    return _mm_tn(
        a3, b3, G,
        (lambda g, k: (g, k, 0)) if Ga > 1 else (lambda g, k: (0, k, 0)),
        (lambda g, k: (g, k, 0)) if Gb > 1 else (lambda g, k: (0, k, 0)),
        lambda tk: (None, tk, M), lambda tk: (None, tk, N),
        (G, M, N), (None, M, N), lambda g, k: (g, 0, 0), name=name)


def _mixout_ln(u3, w3, bias, xin, gain, beta, name):
    G, S, Kb = u3.shape
    tm = _tile(S)

    def body(u_ref, w_ref, b_ref, x_ref, g_ref, be_ref, z_ref, xo_ref, xob_ref):
        h = b_ref[...] + _dot(u_ref[0], w_ref[0])
        for g in range(1, G):
            h = h + _dot(u_ref[g], w_ref[g])
        z = ALPHA * x_ref[...] + h
        z_ref[...] = z
        y = _ln_fwd(z, g_ref[...], be_ref[...])
        xo_ref[...] = y
        xob_ref[...] = y.astype(CDT)

    row = pl.BlockSpec((tm, D), lambda i: (i, 0))
    vec = pl.BlockSpec((1, D), lambda i: (0, 0))
    return _call(
        None, body, grid=(S // tm,),
        in_specs=[pl.BlockSpec((G, tm, Kb), lambda i: (0, i, 0)), pl.BlockSpec((G, Kb, D), lambda i: (0, 0, 0)),
                  vec, row, vec, vec],
        out_specs=[row, row, row], out_shape=[_sds((S, D), F32), _sds((S, D), F32), _sds((S, D), CDT)],
        compiler_params=_params(("arbitrary",)), name=name)(u3, w3, bias, xin, gain, beta)


def _ffn_fwd(xin, xin_b, wgu, wdn, gain, beta, name):
    S = xin.shape[0]
    tm = _tile(S)

    def hidden(xb_ref, wgu_ref, gu_ref, hid_ref):
        xb = xb_ref[...]
        gate = _dot_nt(xb, wgu_ref[0])
        up = _dot_nt(xb, wgu_ref[1])
        gu_ref[0] = gate.astype(CDT)
        gu_ref[1] = up.astype(CDT)
        hid_ref[...] = (gate * _sigmoid(gate) * up).astype(CDT)

    gu, hid = _call(
        None, hidden, grid=(4, S // tm),
        in_specs=[pl.BlockSpec((tm, D), lambda j, i: (i, 0)), pl.BlockSpec((2, None, FFN_B, D), lambda j, i: (0, j, 0, 0))],
        out_specs=[pl.BlockSpec((2, None, tm, FFN_B), lambda j, i: (0, j, i, 0)),
                   pl.BlockSpec((None, tm, FFN_B), lambda j, i: (j, i, 0))],
        out_shape=[_sds((2, 4, S, FFN_B), CDT), _sds((4, S, FFN_B), CDT)],
        compiler_params=_params(("arbitrary", "arbitrary")), name=name + "_hidden")(xin_b, wgu)

    def down(x_ref, hid_ref, wdn_ref, g_ref, be_ref, z_ref, xo_ref, xob_ref):
        z = ALPHA * x_ref[...]
        for j in range(4):
            z = z + _dot(hid_ref[j], wdn_ref[j])
        z_ref[...] = z
        y = _ln_fwd(z, g_ref[...], be_ref[...])
        xo_ref[...] = y
        xob_ref[...] = y.astype(CDT)

    row = pl.BlockSpec((tm, D), lambda i: (i, 0))
    vec = pl.BlockSpec((1, D), lambda i: (0, 0))
    z, xo, xob = _call(
        None, down, grid=(S // tm,),
        in_specs=[row, pl.BlockSpec((4, tm, FFN_B), lambda i: (0, i, 0)), pl.BlockSpec((4, FFN_B, D), lambda i: (0, 0, 0)),
                  vec, vec],
        out_specs=[row, row, row], out_shape=[_sds((S, D), F32), _sds((S, D), F32), _sds((S, D), CDT)],
        compiler_params=_params(("arbitrary",)), name=name + "_down")(xin, hid, wdn, gain, beta)
    return gu, hid, z, xo, xob


def _ple_fwd(xin, xin_b, p_b, wpg, bgate, wpu, gain, beta, name, target=None):
    S = xin.shape[0]
    tm = _tile(S)

    def body(x_ref, xb_ref, p_ref, wpg_ref, bg_ref, wpu_ref, g_ref, be_ref, *rest):
        if target is None:
            sg_ref, up_ref, z_ref, xo_ref, xob_ref = rest
        else:
            t_ref, sg_ref, up_ref, z_ref, dy_ref, l_ref = rest
        sg =_sigmoid(_dot(xb_ref[...], wpg_ref[...]) + bg_ref[...])
        pb = p_ref[...]
        up = jnp.concatenate([_dot(pb, wpu_ref[j]) for j in range(N_DEV)], axis=-1)
        sg_ref[...] = sg.astype(CDT)
        up_ref[...] = (up * sg * (1.0 - sg)).astype(CDT)
        z = ALPHA * x_ref[...] + sg * up
        z_ref[...] = z
        y = _ln_fwd(z, g_ref[...], be_ref[...])
        if target is None:
            xo_ref[...] = y
            xob_ref[...] = y.astype(CDT)
        else:
            e = y - t_ref[...]
            dy_ref[...] = e * (1.0 / D)
            part = 0.5 * jnp.sum(jnp.sum(e * e, axis=-1, keepdims=True) * (1.0 / D), axis=0, keepdims=True)
            _acc(l_ref, jnp.broadcast_to(part, l_ref.shape), pl.program_id(0) == 0)

    row = pl.BlockSpec((tm, D), lambda i: (i, 0))
    vec = pl.BlockSpec((1, D), lambda i: (0, 0))
    last = target is not None
    return _call(
        None, body, grid=(S // tm,),
        in_specs=[row, row, pl.BlockSpec((tm, PLE_DIM), lambda i: (i, 0)), pl.BlockSpec((D, D), lambda i: (0, 0)), vec,
                  pl.BlockSpec((N_DEV, PLE_DIM, D // N_DEV), lambda i: (0, 0, 0)), vec, vec] + [row] * last,
        out_specs=[row] * 4 + [pl.BlockSpec((1, 128), lambda i: (0, 0)) if last else row],
        out_shape=[_sds((S, D), CDT)] * 2 + [_sds((S, D), F32)] * 2 + [_sds((1, 128), F32) if last else _sds((S, D), CDT)],
        compiler_params=_params(("arbitrary",)), name=name)(
            xin, xin_b, p_b, wpg, bgate, wpu, gain, beta, *([target] if last else []))


def _ple_bwd(dy, z, sg, up, gain, wpg, z_ffn, gain_ffn, name, after=None):
    S = dy.shape[0]
    tm = _tile(S)

    def body(dy_ref, z_ref, sg_ref, up_ref, g_ref, wpg_ref, zf_ref, gf_ref, dzf_ref, dzfb_ref, dgl_ref, dup_ref,
             dgain_ref, dbeta_ref, dbg_ref, dgainf_ref, dbetaf_ref):
        first = pl.program_id(0) == 0
        dy_ = dy_ref[...]
        dz, xhat = _ln_bwd(z_ref[...], g_ref[...], dy_)
        dgl = dz * up_ref[...].astype(F32)
        dgl_ref[...] = dgl.astype(CDT)
        dup_ref[...] = (dz * sg_ref[...].astype(F32)).astype(CDT)
        dx = ALPHA * dz + _dot_nt(dgl, wpg_ref[...])
        dzf, xhatf = _ln_bwd(zf_ref[...], gf_ref[...], dx)
        dzf_ref[...] = dzf
        dzfb_ref[...] = dzf.astype(CDT)
        _acc(dgain_ref, _colsum(dy_ * xhat), first)
        _acc(dbeta_ref, _colsum(dy_), first)
        _acc(dbg_ref, _colsum(dgl), first)
        _acc(dgainf_ref, _colsum(dx * xhatf), first)
        _acc(dbetaf_ref, _colsum(dx), first)

    row = pl.BlockSpec((tm, D), lambda i: (i, 0))
    vec = pl.BlockSpec((1, D), lambda i: (0, 0))
    return _call(
        after, body, grid=(S // tm,),
        in_specs=[row, row, row, row, vec, pl.BlockSpec((D, D), lambda i: (0, 0)), row, vec],
        out_specs=[row, row, row, row] + [vec] * 5,
        out_shape=[_sds((S, D), F32)] + [_sds((S, D), CDT)] * 3 + [_sds((1, D), F32)] * 5,
        compiler_params=_params(("arbitrary",)), name=name)(dy, z, sg, up, gain, wpg, z_ffn, gain_ffn)


def _ffn_bwd_hidden(dzb, gu, wdn, name, after=None):
    S = dzb.shape[0]
    tm = _tile(S)

    def hidden(dzb_ref, gu_ref, wdn_ref, dgu_ref):
        dhid = _dot_nt(dzb_ref[...], wdn_ref[...])
        gate, up = gu_ref[0].astype(F32), gu_ref[1].astype(F32)
        sg = _sigmoid(gate)
        dgu_ref[0] = (dhid * up * (sg * (1.0 + gate * (1.0 - sg)))).astype(CDT)
        dgu_ref[1] = (dhid * (gate * sg)).astype(CDT)

    blocks = pl.BlockSpec((2, None, tm, FFN_B), lambda j, i: (0, j, i, 0))
    return _call(
        after, hidden, grid=(4, S // tm),
        in_specs=[pl.BlockSpec((tm, D), lambda j, i: (i, 0)), blocks, pl.BlockSpec((None, FFN_B, D), lambda j, i: (j, 0, 0))],
        out_specs=blocks, out_shape=_sds((2, 4, S, FFN_B), CDT),
        compiler_params=_params(("arbitrary", "arbitrary")), name=name + "_hidden")(dzb, gu, wdn)


def _ffn_bwd_input(dz, dgu, wgu, name, after=None):
    S = dz.shape[0]
    tm = _tile(S)

    def to_input(dz_ref, dgu_ref, wgu_ref, dx_ref):
        acc = ALPHA * dz_ref[...]
        for g in range(2):
            for j in range(4):
                acc = acc + _dot(dgu_ref[g, j], wgu_ref[g, j])
        dx_ref[...] = acc

    rows = pl.BlockSpec((tm, D), lambda i: (i, 0))
    return _call(
        after, to_input, grid=(S // tm,),
        in_specs=[rows, pl.BlockSpec((2, 4, tm, FFN_B), lambda i: (0, 0, i, 0)),
                  pl.BlockSpec((2, 4, FFN_B, D), lambda i: (0, 0, 0, 0))],
        out_specs=rows, out_shape=_sds((S, D), F32),
        compiler_params=_params(("arbitrary",)), name=name + "_input")(dz, dgu, wgu)


def _mixout_bwd(dy, z, gain, w3, du_dtype, name, after=None):
    S = dy.shape[0]
    G, Kb, _ = w3.shape
    tm = _tile(S)

    def body(dy_ref, z_ref, g_ref, w_ref, dz_ref, dzb_ref, du_ref, dgain_ref, dbeta_ref, dbias_ref):
        first = pl.program_id(0) == 0
        dy_ = dy_ref[...]
        dz, xhat = _ln_bwd(z_ref[...], g_ref[...], dy_)
        dz_ref[...] = dz
        dzb = dz.astype(CDT)
        dzb_ref[...] = dzb
        for g in range(G):
            du_ref[g] = _dot_nt(dzb, w_ref[g]).astype(du_ref.dtype)
        _acc(dgain_ref, _colsum(dy_ * xhat), first)
        _acc(dbeta_ref, _colsum(dy_), first)
        _acc(dbias_ref, _colsum(dz), first)

    row = pl.BlockSpec((tm, D), lambda i: (i, 0))
    vec = pl.BlockSpec((1, D), lambda i: (0, 0))
    return _call(
        after, body, grid=(S // tm,), in_specs=[row, row, vec, pl.BlockSpec((G, Kb, D), lambda i: (0, 0, 0))],
        out_specs=[row, row, pl.BlockSpec((G, tm, Kb), lambda i: (0, i, 0)), vec, vec, vec],
        out_shape=[_sds((S, D), F32), _sds((S, D), CDT), _sds((G, S, Kb), du_dtype)] + [_sds((1, D), F32)] * 3,
        compiler_params=_params(("arbitrary",)), name=name)(dy, z, gain, w3)


def _half_select(low):
    r = lax.broadcasted_iota(jnp.int32, (2 * ATT_HD, ATT_HD), 0)
    c = lax.broadcasted_iota(jnp.int32, (2 * ATT_HD, ATT_HD), 1)
    return (r == c + (0 if low else ATT_HD)).astype(CDT)


def _half_place(low):
    r = lax.broadcasted_iota(jnp.int32, (ATT_HD, 2 * ATT_HD), 0)
    c = lax.broadcasted_iota(jnp.int32, (ATT_HD, 2 * ATT_HD), 1)
    return (c == r + (0 if low else ATT_HD)).astype(CDT)


def _pair_lanes(even, odd):
    return (jnp.dot(even, _half_place(True), preferred_element_type=F32)
            + jnp.dot(odd, _half_place(False), preferred_element_type=F32)).astype(CDT)


def _proj_heads(a, w, bias, heads, name):
    S, K = a.shape
    N = heads * ATT_HD
    tm = _tile(S)

    def body(a_ref, w_ref, b_ref, o_ref):
        acc = (_dot(a_ref[...], w_ref[...]) + b_ref[...]).astype(CDT)
        sel = (_half_select(True), _half_select(False))
        for h in range(heads):
            pair = acc[:, (h // 2) * 2 * ATT_HD:(h // 2 + 1) * 2 * ATT_HD]
            o_ref[h] = jnp.dot(pair, sel[h % 2], preferred_element_type=F32).astype(CDT)

    return _call(
        None, body, grid=(S // tm,),
        in_specs=[pl.BlockSpec((tm, K), lambda i: (i, 0)), pl.BlockSpec((K, N), lambda i: (0, 0)),
                  pl.BlockSpec((1, N), lambda i: (0, 0))],
        out_specs=pl.BlockSpec((heads, tm, ATT_HD), lambda i: (0, i, 0)), out_shape=_sds((heads, S, ATT_HD), CDT),
        compiler_params=_params(("arbitrary",)), name=name)(a, w, bias)


def _qkv_bwd(dz, dq, dkv4, wq, wkv, name, after=None):
    S = dz.shape[0]
    tm = _tile(S)
    HK = dkv4.shape[0]
    NK = HK * ATT_HD

    def body(dz_ref, dq_ref, dkv_ref, wq_ref, wkv_ref, dx_ref, dkvn_ref, dkvb_ref):
        first = pl.program_id(0) == 0
        dkvn = jnp.concatenate([_pair_lanes(dkv_ref[2 * i].astype(CDT), dkv_ref[2 * i + 1].astype(CDT))
                                for i in range(HK // 2)], axis=-1)
        dkvn_ref[...] = dkvn
        dx_ref[...] = ALPHA * dz_ref[...] + _dot_nt(dq_ref[...], wq_ref[...]) + _dot_nt(dkvn, wkv_ref[...])
        for h in range(HK):
            _acc(dkvb_ref.at[h], _colsum(dkv_ref[h]), first)

    row = pl.BlockSpec((tm, D), lambda i: (i, 0))
    return _call(
        after, body, grid=(S // tm,),
        in_specs=[row, row, pl.BlockSpec((HK, tm, ATT_HD), lambda i: (0, i, 0)),
                  pl.BlockSpec((D, D), lambda i: (0, 0)), pl.BlockSpec((D, NK), lambda i: (0, 0))],
        out_specs=[row, pl.BlockSpec((tm, NK), lambda i: (i, 0)), pl.BlockSpec((HK, 1, ATT_HD), lambda i: (0, 0, 0))],
        out_shape=[_sds((S, D), F32), _sds((S, NK), CDT), _sds((HK, 1, ATT_HD), F32)],
        compiler_params=_params(("arbitrary",)), name=name)(dz, dq, dkv4, wq, wkv)


def _inproj_fwd(xb, wain, name):
    S = xb.shape[0]
    tm = _tile(S) // 2
    nb = wain.shape[-1]

    def body(x_ref, w_ref, o_ref):
        x = x_ref[...]
        for j in range(N_DEV):
            o_ref[j // 2, :, pl.ds((j % 2) * nb, nb)] = _dot(x, w_ref[j])

    return _call(
        None, body, grid=(S // tm,),
        in_specs=[pl.BlockSpec((tm, D), lambda i: (i, 0)), pl.BlockSpec((N_DEV, D, nb), lambda i: (0, 0, 0))],
        out_specs=pl.BlockSpec((4, tm, D), lambda i: (0, i, 0)), out_shape=_sds((4, S, D), F32),
        compiler_params=_params(("arbitrary",)), name=name)(xb, wain)


def _inproj_bwd(dz, dproj, wain, name, after=None):
    S = dz.shape[0]
    tm = _tile(S)
    nb = wain.shape[-1]

    def body(dz_ref, dp_ref, w_ref, dx_ref):
        acc = ALPHA * dz_ref[...]
        for j in range(N_DEV):
            acc = acc + _dot_nt(dp_ref[j // 2, :, pl.ds((j % 2) * nb, nb)], w_ref[j])
        dx_ref[...] = acc

    row = pl.BlockSpec((tm, D), lambda i: (i, 0))
    return _call(
        after, body, grid=(S // tm,),
        in_specs=[row, pl.BlockSpec((4, tm, D), lambda i: (0, i, 0)), pl.BlockSpec((N_DEV, D, nb), lambda i: (0, 0, 0))],
        out_specs=row, out_shape=_sds((S, D), F32),
        compiler_params=_params(("arbitrary",)), name=name)(dz, dproj, wain)


def _running_sum(x, reverse=False):
    rows = x.shape[0]
    row = lax.broadcasted_iota(jnp.int32, x.shape, 0)
    step = 1
    while step < rows:
        if reverse:
            x = x + jnp.where(row < rows - step, pltpu.roll(x, rows - step, 0), 0.0)
        else:
            x = x + jnp.where(row >= step, pltpu.roll(x, step, 0), 0.0)
        step *= 2
    return x


def _hg_gates(q, f, alb_ref):
    a0, a1 = alb_ref[0:1, :], alb_ref[1:2, :]
    mx = jnp.maximum(a0, a1)
    e0, e1 = jnp.exp(a0 - mx), jnp.exp(a1 - mx)
    lb = e0 / (e0 + e1)
    sig = _sigmoid(f)
    forget = lb + (1.0 - lb) * sig
    k = (1.0 - lb) * _sigmoid(-f)
    qs = q * _sigmoid(q) * (HG_DK ** -0.5)
    return qs, k, jnp.log(forget), sig, lb, forget


def _hg_intra(qs, k, b, b_scr):
    b_scr[...] = b
    bm = b_scr[pl.ds(HG_CH // 2 - 1, 1), :]
    bl = b_scr[pl.ds(HG_CH - 1, 1), :]
    eb = jnp.exp(b)
    qb = qs * eb
    e_q = jnp.exp(b - bm)
    e_k = jnp.exp(bm - b)
    e_d = jnp.exp(bl - b)
    return qb, qs * e_q, k * e_k, k * e_d, jnp.exp(bl), eb, e_q, e_k, e_d


def _hgrn_fwd(proj, alb, ngain):
    S = proj.shape[1]
    nc = S // HG_CH
    nb = nc // HG_CPB
    rb, wb = HG_CPB * HG_CH, HG_HPB * HG_DK

    def body(pj_ref, alb_ref, ng_ref, o_ref, y_ref, st_ref, st_scr, b_scr):
        n = pl.program_id(1)

        @pl.when(n == 0)
        def _():
            st_scr[...] = jnp.zeros_like(st_scr)

        r = lax.broadcasted_iota(jnp.int32, (HG_CH, HG_CH), 0)
        c = lax.broadcasted_iota(jnp.int32, (HG_CH, HG_CH), 1)
        causal = r >= c
        for ci, j in [(ci, j) for ci in range(HG_CPB) for j in range(HG_HPB)]:
            rows, lanes = pl.ds(ci * HG_CH, HG_CH), pl.ds(j * HG_DK, HG_DK)
            q, f, v, g = pj_ref[0, rows, lanes], pj_ref[1, rows, lanes], pj_ref[2, rows, lanes], pj_ref[3, rows, lanes]
            qs, k, logf, _, _, _ = _hg_gates(q, f, alb_ref.at[:, lanes])
            b = _running_sum(logf)
            qb, qt, kt, kd, ebl, _, _, _, _ = _hg_intra(qs, k, b, b_scr.at[j, ci])
            st = st_scr[j]
            st_ref[j, ci] = st
            a = jnp.where(causal, _dot_nt(qt, kt), 0.0)
            o = _dot(a, v) + _dot_nt(qb, st)
            st_scr[j] = st * ebl + _dot_tn(v, kd)
            o_ref[rows, lanes] = o
            rinv = lax.rsqrt(jnp.mean(o * o, axis=-1, keepdims=True) + RMS_EPS)
            y_ref[rows, lanes] = (o * rinv * ng_ref[...] * (g * _sigmoid(g))).astype(CDT)

    blk = pl.BlockSpec((rb, wb), lambda h, n: (n, h))
    return _call(
        None, body, grid=(HG_H // HG_HPB, nb),
        in_specs=[pl.BlockSpec((4, rb, wb), lambda h, n: (0, n, h)), pl.BlockSpec((2, wb), lambda h, n: (0, h)),
                  pl.BlockSpec((1, HG_DK), lambda h, n: (0, 0))],
        out_specs=[blk, blk, pl.BlockSpec((HG_HPB, HG_CPB, HG_DK, HG_DK), lambda h, n: (h, n, 0, 0))],
        out_shape=[_sds((S, D), F32), _sds((S, D), CDT), _sds((HG_H, nc, HG_DK, HG_DK), F32)],
        scratch_shapes=[pltpu.VMEM((HG_HPB, HG_DK, HG_DK), F32), pltpu.VMEM((HG_HPB, HG_CPB, HG_CH, HG_DK), F32)],
        compiler_params=_params(("arbitrary", "arbitrary")), name="hgrn_fwd")(proj, alb, ngain)


def _hgrn_bwd(proj, alb, ngain, o, states, dy, after=None):
    S = proj.shape[1]
    nc = S // HG_CH
    nb = nc // HG_CPB
    rb, wb = HG_CPB * HG_CH, HG_HPB * HG_DK

    def body(pj_ref, alb_ref, ng_ref, o_ref, st_ref, dy_ref, dpj_ref, dalb_ref, dng_ref, dst_scr, b_scr):
        h, n = pl.program_id(0), pl.program_id(1)

        @pl.when(n == 0)
        def _():
            dst_scr[...] = jnp.zeros_like(dst_scr)
            dalb_ref[...] = jnp.zeros_like(dalb_ref)

        _zero_at(dng_ref, jnp.logical_and(h == 0, n == 0))
        ng = ng_ref[...]
        r = lax.broadcasted_iota(jnp.int32, (HG_CH, HG_CH), 0)
        c = lax.broadcasted_iota(jnp.int32, (HG_CH, HG_CH), 1)
        causal = r >= c
        dng = None
        for ci, j in [(ci, j) for ci in reversed(range(HG_CPB)) for j in range(HG_HPB)]:
            rows, lanes = pl.ds(ci * HG_CH, HG_CH), pl.ds(j * HG_DK, HG_DK)
            q, f, v, g = pj_ref[0, rows, lanes], pj_ref[1, rows, lanes], pj_ref[2, rows, lanes], pj_ref[3, rows, lanes]
            o_ = o_ref[rows, lanes]
            dy_ = dy_ref[rows, lanes]
            sg = _sigmoid(g)
            rinv = lax.rsqrt(jnp.mean(o_ * o_, axis=-1, keepdims=True) + RMS_EPS)
            nrm = o_ * rinv
            dr = dy_ * (g * sg)
            dg = dy_ * nrm * ng * (sg * (1.0 + g * (1.0 - sg)))
            dn = dr * ng
            do = rinv * (dn - nrm * jnp.mean(dn * nrm, axis=-1, keepdims=True))
            dng = _colsum(dr * nrm) if dng is None else dng + _colsum(dr * nrm)
            qs, k, logf, sig, lb, forget = _hg_gates(q, f, alb_ref.at[:, lanes])
            b = _running_sum(logf)
            qb, qt, kt, kd, ebl, eb, e_q, e_k, e_d = _hg_intra(qs, k, b, b_scr.at[j, ci])
            st = st_ref[j, ci]
            dstn = dst_scr[j]
            qt, kt, qb, kd = (t.astype(CDT).astype(F32) for t in (qt, kt, qb, kd))
            a = jnp.where(causal, _dot_nt(qt, kt), 0.0)
            da = jnp.where(causal, _dot_nt(do, v), 0.0)
            dv = _dot_tn(a, do) + _dot_nt(kd, dstn)
            dqb = _dot(do, st)
            dkd = _dot(v, dstn)
            dqt = _dot(da, kt)
            dkt = _dot_tn(da, qt)
            dbl = _colsum(dkd * kd) + ebl * _colsum(dstn * st)
            dst_scr[j] = dstn * ebl + _dot_tn(do, qb)
            dqs = dqt * e_q + dqb * eb
            dk = dkt * e_k + dkd * e_d
            db = dqt * qt + dqb * qb - dkt * kt - dkd * kd
            dlogf = _running_sum(db, reverse=True) + dbl
            dforget = dlogf / forget
            dsig = (1.0 - lb) * (dforget - dk)
            df = dsig * sig * (1.0 - sig)
            dlb = _colsum((dforget - dk) * (1.0 - sig))
            sq = _sigmoid(q)
            dq = dqs * (HG_DK ** -0.5) * (sq * (1.0 + q * (1.0 - sq)))
            dpj_ref[0, rows, lanes] = dq.astype(CDT)
            dpj_ref[1, rows, lanes] = df.astype(CDT)
            dpj_ref[2, rows, lanes] = dv.astype(CDT)
            dpj_ref[3, rows, lanes] = dg.astype(CDT)
            da0 = dlb * lb * (1.0 - lb)
            dalb_ref[pl.ds(0, 1), lanes] += da0
            dalb_ref[pl.ds(1, 1), lanes] -= da0
        dng_ref[...] += dng

    blk = pl.BlockSpec((rb, wb), lambda h, n: (nb - 1 - n, h))
    pj = pl.BlockSpec((4, rb, wb), lambda h, n: (0, nb - 1 - n, h))
    alb_blk = pl.BlockSpec((2, wb), lambda h, n: (0, h))
    ng_blk = pl.BlockSpec((1, HG_DK), lambda h, n: (0, 0))
    return _call(
        after, body, grid=(HG_H // HG_HPB, nb),
        in_specs=[pj, alb_blk, ng_blk, blk,
                  pl.BlockSpec((HG_HPB, HG_CPB, HG_DK, HG_DK), lambda h, n: (h, nb - 1 - n, 0, 0)), blk],
        out_specs=[pj, alb_blk, ng_blk],
        out_shape=[_sds((4, S, D), CDT), _sds((2, D), F32), _sds((1, HG_DK), F32)],
        scratch_shapes=[pltpu.VMEM((HG_HPB, HG_DK, HG_DK), F32), pltpu.VMEM((HG_HPB, HG_CPB, HG_CH, HG_DK), F32)],
        compiler_params=_params(("arbitrary", "arbitrary")), name="hgrn_bwd")(proj, alb, ngain, o, states, dy)


def _slope(h):
    return 2.0 ** (-8.0 * (h + 1) / ATT_QH)


def _attn_mask(n):
    qi = lax.broadcasted_iota(jnp.int32, (WINDOW, 2 * WINDOW), 0)
    si = lax.broadcasted_iota(jnp.int32, (WINDOW, 2 * WINDOW), 1)
    dist = qi - si + WINDOW
    valid = (dist >= 0) & (dist < WINDOW) & (n * WINDOW - WINDOW + si >= 0)
    return valid, dist.astype(F32)


def _attn_probs(qk, sink, slope, valid, distf):
    s = qk * (ATT_HD ** -0.5) - slope * distf
    s = jnp.where(valid, s, NEG)
    m = jnp.maximum(jnp.max(s, axis=-1, keepdims=True), sink)
    e = jnp.exp(s - m)
    es = jnp.exp(sink - m)
    inv = 1.0 / (jnp.sum(e, axis=-1, keepdims=True) + es)
    return e * inv, es * inv


def _attn_specs(S):
    steps = S // (ATT_BPB * WINDOW)
    cur = lambda H: pl.BlockSpec((H, ATT_BPB * WINDOW, ATT_HD), lambda n: (0, n, 0))
    prev = lambda H: pl.BlockSpec((H, WINDOW, ATT_HD), lambda n: (0, jnp.maximum(ATT_BPB * n - 1, 0), 0))
    return steps, cur, prev


def _attn_kv(kvc_ref, kvp_ref, head, bi):
    before = kvp_ref[head] if bi == 0 else kvc_ref[head, pl.ds((bi - 1) * WINDOW, WINDOW), :]
    return jnp.concatenate([before, kvc_ref[head, pl.ds(bi * WINDOW, WINDOW), :]], axis=0)


def _attn_fwd(q4, kv4, sinks):
    S = q4.shape[1]
    nb, cur, prev = _attn_specs(S)

    def body(sink_ref, q_ref, kvc_ref, kvp_ref, o_ref, p_ref, ps_ref):
        lane = lax.broadcasted_iota(jnp.int32, (1, 128), 1)
        place = (_half_place(True), _half_place(False))
        masks = [_attn_mask(pl.program_id(0) * ATT_BPB + bi) for bi in range(ATT_BPB)]
        sink_probs = [jnp.zeros((WINDOW, 128), F32) for _ in range(ATT_BPB)]
        heads = lambda kvh: range(kvh * ATT_G, (kvh + 1) * ATT_G)

        def scores(bi, kvh):
            rows = pl.ds(bi * WINDOW, WINDOW)
            kh = _attn_kv(kvc_ref, kvp_ref, kvh, bi)
            vh = _attn_kv(kvc_ref, kvp_ref, ATT_KVH + kvh, bi)
            v_pl = [jnp.dot(vh, m, preferred_element_type=F32).astype(CDT) for m in place]
            return v_pl, [_dot_nt(q_ref[h, rows, :], kh) for h in heads(kvh)]

        def softmax(bi, kvh, qks):
            rows = pl.ds(bi * WINDOW, WINDOW)
            probs = []
            for h, qk in zip(heads(kvh), qks):
                p, ps = _attn_probs(qk, sink_ref[0, h], _slope(h), *masks[bi])
                probs.append(p.astype(CDT))
                p_ref[h, rows, :] = probs[-1]
                sink_probs[bi] = sink_probs[bi] + jnp.where(lane == h, ps, 0.0)
            return probs

        def weighted_values(bi, kvh, v_pl, probs):
            rows = pl.ds(bi * WINDOW, WINDOW)
            for i in range(ATT_G // 2):
                lanes = pl.ds((kvh * ATT_G + 2 * i) * ATT_HD, 2 * ATT_HD)
                o_ref[rows, lanes] = (_dot(probs[2 * i], v_pl[0]) + _dot(probs[2 * i + 1], v_pl[1])).astype(CDT)

        groups = [(bi, kvh) for bi in range(ATT_BPB) for kvh in range(ATT_KVH)]
        ahead = scores(*groups[0])
        for gi, g in enumerate(groups):
            v_pl, qks = ahead
            probs = softmax(*g, qks)
            if gi + 1 < len(groups):
                ahead = scores(*groups[gi + 1])
            weighted_values(*g, v_pl, probs)
        for bi in range(ATT_BPB):
            ps_ref[pl.ds(bi * WINDOW, WINDOW), :] = sink_probs[bi]

    rows_spec = pl.BlockSpec((ATT_BPB * WINDOW, D), lambda n: (n, 0))
    return _call(
        None, body, grid=(nb,),
        in_specs=[pl.BlockSpec(memory_space=pltpu.SMEM), cur(ATT_QH), cur(2 * ATT_KVH), prev(2 * ATT_KVH)],
        out_specs=[rows_spec, pl.BlockSpec((ATT_QH, ATT_BPB * WINDOW, 2 * WINDOW), lambda n: (0, n, 0)),
                   pl.BlockSpec((ATT_BPB * WINDOW, 128), lambda n: (n, 0))],
        out_shape=[_sds((S, D), CDT), _sds((ATT_QH, S, 2 * WINDOW), CDT), _sds((S, 128), F32)],
        compiler_params=_params(("arbitrary",)), name="attn_fwd")(sinks, q4, kv4, kv4)


def _attn_bwd(q4, kv4, probs, sink_probs, do):
    S = q4.shape[1]
    nb, cur, prev = _attn_specs(S)

    def body(q_ref, kvc_ref, kvp_ref, p_ref, ps_ref, do_ref, dq_ref, dkv_ref, dbq_ref, dsink_ref):
        n = pl.program_id(0)
        first = n == 0

        @pl.when(first)
        def _():
            dkv_ref[...] = jnp.zeros_like(dkv_ref)
            dsink_ref[...] = jnp.zeros_like(dsink_ref)
            dbq_ref[...] = jnp.zeros_like(dbq_ref)

        lane = lax.broadcasted_iota(jnp.int32, (1, 128), 1)
        place = (_half_place(True), _half_place(False))
        row_dots = [jnp.zeros((WINDOW, 128), F32) for _ in range(ATT_BPB)]
        heads = lambda kvh: range(kvh * ATT_G, (kvh + 1) * ATT_G)
        pair_lanes = lambda h: pl.ds((h // 2) * 2 * ATT_HD, 2 * ATT_HD)

        def products_of_do(bi, kvh):
            rows = pl.ds(bi * WINDOW, WINDOW)
            kh = _attn_kv(kvc_ref, kvp_ref, kvh, bi)
            vh = _attn_kv(kvc_ref, kvp_ref, ATT_KVH + kvh, bi)
            k_pl = [jnp.dot(kh, m, preferred_element_type=F32).astype(CDT) for m in place]
            v_pl = [jnp.dot(vh, m, preferred_element_type=F32).astype(CDT) for m in place]
            dps = [_dot_nt(do_ref[rows, pair_lanes(h)], v_pl[h % 2]) for h in heads(kvh)]
            dv2 = [sum(_dot_tn(p_ref[h, rows, :], do_ref[rows, pair_lanes(h)]) for h in heads(kvh) if h % 2 == par)
                   for par in range(2)]
            return k_pl, dps, dv2

        def softmax_bwd(bi, kvh, dps):
            rows = pl.ds(bi * WINDOW, WINDOW)
            dss = []
            for h, dp in zip(heads(kvh), dps):
                p = p_ref[h, rows, :].astype(F32)
                dd = jnp.sum(p * dp, axis=-1, keepdims=True)
                dss.append((p * (dp - dd)).astype(CDT))
                row_dots[bi] = row_dots[bi] + jnp.where(lane == h, dd, 0.0)
            return dss

        def products_of_ds(bi, kvh, k_pl, dss, dv2):
            block = n * ATT_BPB + bi
            rows = pl.ds(bi * WINDOW, WINDOW)
            rows_cur = pl.ds(pl.multiple_of(block * WINDOW, WINDOW), WINDOW)
            rows_prev = pl.ds(pl.multiple_of(jnp.maximum(block - 1, 0) * WINDOW, WINDOW), WINDOW)
            for i in range(ATT_G // 2):
                h = kvh * ATT_G + 2 * i
                dq2 = (_dot(dss[2 * i], k_pl[0]) + _dot(dss[2 * i + 1], k_pl[1])) * (ATT_HD ** -0.5)
                dq_ref[rows, pair_lanes(h)] = dq2.astype(CDT)
                dbq_ref[:, pair_lanes(h)] += _colsum(dq2)
            dk = sum(_dot_tn(ds, q_ref[h, rows, :]) for h, ds in zip(heads(kvh), dss)) * (ATT_HD ** -0.5)
            dv = dv2[0][:, :ATT_HD] + pltpu.roll(dv2[1], ATT_HD, 1)[:, :ATT_HD]
            dkv_ref[kvh, rows_prev, :] += dk[:WINDOW]
            dkv_ref[kvh, rows_cur, :] += dk[WINDOW:]
            dkv_ref[ATT_KVH + kvh, rows_prev, :] += dv[:WINDOW]
            dkv_ref[ATT_KVH + kvh, rows_cur, :] += dv[WINDOW:]

        groups = [(bi, kvh) for bi in range(ATT_BPB) for kvh in range(ATT_KVH)]
        ahead = products_of_do(*groups[0])
        for gi, g in enumerate(groups):
            k_pl, dps, dv2 = ahead
            dss = softmax_bwd(*g, dps)
            if gi + 1 < len(groups):
                ahead = products_of_do(*groups[gi + 1])
            products_of_ds(*g, k_pl, dss, dv2)
        dsinks = jnp.zeros((1, 128), F32)
        for bi in range(ATT_BPB):
            dsinks = dsinks - _colsum(ps_ref[pl.ds(bi * WINDOW, WINDOW), :] * row_dots[bi])
        dsink_ref[...] += dsinks

    rows_spec = pl.BlockSpec((ATT_BPB * WINDOW, D), lambda n: (n, 0))
    return _call(
        None, body, grid=(nb,),
        in_specs=[cur(ATT_QH), cur(2 * ATT_KVH), prev(2 * ATT_KVH),
                  pl.BlockSpec((ATT_QH, ATT_BPB * WINDOW, 2 * WINDOW), lambda n: (0, n, 0)),
                  pl.BlockSpec((ATT_BPB * WINDOW, 128), lambda n: (n, 0)), rows_spec],
        out_specs=[rows_spec, pl.BlockSpec((2 * ATT_KVH, S, ATT_HD), lambda n: (0, 0, 0)),
                   pl.BlockSpec((1, D), lambda n: (0, 0)), pl.BlockSpec((1, 128), lambda n: (0, 0))],
        out_shape=[_sds((S, D), CDT), _sds((2 * ATT_KVH, S, ATT_HD), F32), _sds((1, D), F32),
                   _sds((1, 128), F32)],
        compiler_params=_params(("arbitrary",)), name="attn_bwd")(q4, kv4, kv4, probs, sink_probs, do)


def _local_step(x, p, target, getw, sm, emit):
    S = x.shape[0]
    vec = lambda a: a.reshape(1, -1)
    ln_g = lambda l, k: vec(sm["ln_gain"][l, k])
    ln_b = lambda l, k: vec(sm["ln_bias"][l, k])
    xb = x.astype(CDT)
    pb = p.astype(CDT)

    proj = _inproj_fwd(xb, getw("a_w_in"), "a_in")
    o_a, y_a, states = _hgrn_fwd(proj, sm["a_lower_bound"], sm["a_norm_gain"])
    zeros = jnp.zeros((1, D), F32)
    z = [[None] * 3 for _ in range(2)]
    xs = [[None] * 3 for _ in range(2)]
    xbs = [[None] * 3 for _ in range(2)]
    z[0][0], xs[0][0], xbs[0][0] = _mixout_ln(y_a[None], getw("a_w_out")[None], zeros, x, ln_g(0, 0), ln_b(0, 0),
                                              "a_out_ln")
    gu, hid, sgs, ups = [None, None], [None, None], [None, None], [None, None]

    def ffn_ple(l, target=None):
        wgu = getw(f"gu{l}")
        gu[l], hid[l], z[l][1], xs[l][1], xbs[l][1] = _ffn_fwd(
            xs[l][0], xbs[l][0], wgu, getw(f"dn{l}"), ln_g(l, 1), ln_b(l, 1), f"ffn_fwd{l}")
        sgs[l], ups[l], z[l][2], *out = _ple_fwd(
            xs[l][1], xbs[l][1], pb[l], getw(f"pg{l}"), vec(sm["ple_b_gate"][l]), getw(f"pu{l}"), ln_g(l, 2),
            ln_b(l, 2), f"ple_fwd{l}", target=target)
        if target is None:
            xs[l][2], xbs[l][2] = out
        return out

    ffn_ple(0)
    x3, x3b = xs[0][2], xbs[0][2]
    w_kv, w_q, w_bo = getw("kv_w"), getw("b_w_q"), getw("b_w_out")
    kv4 = _proj_heads(x3b, w_kv, vec(sm["kv_b"]), 2 * ATT_KVH, "kv_proj")
    q4 = _proj_heads(x3b, w_q, vec(sm["b_b_q"]), ATT_QH, "q_proj")
    o_b, probs, sink_probs = _attn_fwd(q4, kv4, sm["b_sinks"])
    z[1][0], xs[1][0], xbs[1][0] = _mixout_ln(o_b[None], w_bo[None], sm["b_b_out"], x3, ln_g(1, 0), ln_b(1, 0),
                                              "b_out_ln")
    dy, loss = ffn_ple(1, target)

    gs = {}
    d_ln_g = [[None] * 3 for _ in range(2)]
    d_ln_b = [[None] * 3 for _ in range(2)]
    g_bg = [None, None]

    def ffn_ple_bwd(l, dy, after=None):
        dz2, dzb, dgl, dup, d_ln_g[l][2], d_ln_b[l][2], g_bg[l], d_ln_g[l][1], d_ln_b[l][1] = _ple_bwd(
            dy, z[l][2], sgs[l], ups[l], ln_g(l, 2), getw(f"pg{l}"), z[l][1], ln_g(l, 1), f"ple_bwd{l}", after=after)
        g_pg = _wgrad(xbs[l][1][None], dgl[None], f"g_ple_gate{l}")[0]
        g_pu = _wgrad(pb[l][None], dup[None], f"g_ple_up{l}")[0]
        tok = emit({f"pg{l}": g_pg, f"pu{l}": g_pu})
        tok = tok + emit({f"dn{l}": _wgrad(hid[l], dzb[None], f"g_ffn_down{l}")})
        dgu = _ffn_bwd_hidden(dzb, gu[l], getw(f"dn{l}"), f"ffn_bwd{l}", after=tok)
        g_gu = _wgrad(dgu.reshape(8, S, FFN_B), xbs[l][0][None], f"g_ffn_gate_up{l}")
        tok = emit({f"gu{l}": g_gu})
        dx1 = _ffn_bwd_input(dz2, dgu, getw(f"gu{l}"), f"ffn_bwd{l}", after=tok)
        return dx1, None

    dx1, tok = ffn_ple_bwd(1, dy)
    dz, dzb, do, d_ln_g[1][0], d_ln_b[1][0], gs["b_b_out"] = _mixout_bwd(dx1, z[1][0], ln_g(1, 0), w_bo[None], CDT,
                                                                        "b_out_bwd", after=tok)
    g_bo = _wgrad(o_b[None], dzb[None], "g_b_w_out")[0]
    dq, dkv4, dbq, dsinks = _attn_bwd(q4, kv4, probs, sink_probs, do[0])
    gs["b_b_q"] = dbq
    gs["b_sinks"] = dsinks
    g_q = _wgrad(x3b[None], dq[None], "g_b_w_q")[0]
    dx3, dkv, gs["kv_b"] = _qkv_bwd(dz, dq, dkv4, w_q, w_kv, "qkv_bwd", after=tok)
    g_kv = _wgrad(x3b[None], dkv[None], "g_kv_w")[0]
    tok = emit({"b_w_out": g_bo, "b_w_q": g_q, "kv_w": g_kv})
    dx1, tok = ffn_ple_bwd(0, dx3, tok)
    w_ao = getw("a_w_out")
    dz, dzb, dyr, d_ln_g[0][0], d_ln_b[0][0], _ = _mixout_bwd(dx1, z[0][0], ln_g(0, 0), w_ao[None], F32, "a_out_bwd",
                                                              after=tok)
    g_ao = _wgrad(y_a[None], dzb[None], "g_a_w_out")[0]
    tok = emit({"a_w_out": g_ao})
    dproj, gs["a_lower_bound"], gs["a_norm_gain"] = _hgrn_bwd(proj, sm["a_lower_bound"], sm["a_norm_gain"], o_a, states,
                                                              dyr[0], after=tok)
    tk = lambda t: (None, t, D)
    g_ain = _mm_tn(xb[None], dproj, N_DEV, lambda g, k: (0, k, 0), lambda g, k: (g // 2, k, g % 2),
                   tk, lambda t: (None, t, 512), (N_DEV, D, 512), (None, D, 512), lambda g, k: (g, 0, 0), name="g_a_w_in")
    gs["ple_b_gate"] = jnp.concatenate(g_bg, axis=0)
    gs["ln_gain"] = jnp.stack([jnp.concatenate(r, axis=0) for r in d_ln_g])
    gs["ln_bias"] = jnp.stack([jnp.concatenate(r, axis=0) for r in d_ln_b])
    gs["loss"] = loss
    tok = emit({"a_w_in": g_ain}, small=gs)
    grad_x = _inproj_bwd(dz, dproj, getw("a_w_in"), "a_in_bwd", after=tok)
    return loss, grad_x, gs


def _peer(k):
    x, y, c = lax.axis_index("x"), lax.axis_index("y"), lax.axis_index("c")
    px = 1 - x if k & 4 else x
    py = 1 - y if k & 2 else y
    pc = 1 - c if k & 1 else c
    return (px, py, pc), 4 * px + 2 * py + pc


def _my_index():
    return 4 * lax.axis_index("x") + 2 * lax.axis_index("y") + lax.axis_index("c")


def _piece_copy(mode, src, land, send_sems, recv_sems, t, k, sender, receiver, peer):
    return pltpu.make_async_remote_copy(
        src_ref=src if mode == "gather" else src.at[receiver], dst_ref=land.at[sender],
        send_sem=send_sems.at[t * 7 + k - 1], recv_sem=recv_sems.at[t * 7 + k - 1], device_id=peer, device_id_type=MESH)


def _sequencer_exchange(srcs, modes, name, collective_id, after=None):
    n = len(srcs)
    land_shapes = [((N_DEV,) + a.shape) if mode == "gather" else a.shape for a, mode in zip(srcs, modes)]
    extra = [] if after is None else [after]

    def body(*refs):
        src_refs, land_refs = refs[:n], refs[n + len(extra):2 * n + len(extra)]
        send_sems, recv_sems, local_sems = refs[2 * n + len(extra):]
        barrier = pltpu.get_barrier_semaphore()
        for k in range(1, N_DEV):
            pl.semaphore_signal(barrier, inc=1, device_id=_peer(k)[0], device_id_type=MESH)
        pl.semaphore_wait(barrier, N_DEV - 1)
        me = _my_index()
        local = []
        for i in range(n):
            cp = pltpu.make_async_copy(src_refs[i] if modes[i] == "gather" else src_refs[i].at[me], land_refs[i].at[me],
                                       local_sems.at[i])
            cp.start()
            local.append(cp)
        for k in range(1, N_DEV):
            peer, pid = _peer(k)
            for t in range(n):
                _piece_copy(modes[t], src_refs[t], land_refs[t], send_sems, recv_sems, t, k, me, pid, peer).start()
        for k in range(1, N_DEV):
            peer, pid = _peer(k)
            for t in range(n):
                _piece_copy(modes[t], src_refs[t], land_refs[t], send_sems, recv_sems, t, k, pid, me, peer).wait_recv()
        for k in range(1, N_DEV):
            peer, pid = _peer(k)
            for t in range(n):
                _piece_copy(modes[t], src_refs[t], land_refs[t], send_sems, recv_sems, t, k, me, pid, peer).wait_send()
        for cp in local:
            cp.wait()

    return pl.kernel(
        body, out_type=[_sds(s, a.dtype) for s, a in zip(land_shapes, srcs)],
        mesh=plsc.ScalarSubcoreMesh(axis_name="sequencer", num_cores=1),
        scratch_types=[pltpu.SemaphoreType.DMA((7 * n,)), pltpu.SemaphoreType.DMA((7 * n,)), pltpu.SemaphoreType.DMA((n,))],
        compiler_params=pltpu.CompilerParams(collective_id=collective_id), name=name)(*srcs, *extra)


def _sequencer_gather(srcs, name, collective_id, after=None):
    n = len(srcs)
    extra = [] if after is None else [after]

    def body(*refs):
        src_refs, land_refs = refs[:n], refs[n + len(extra):2 * n + len(extra)]
        send_sems, recv_sems, local_sems = refs[2 * n + len(extra):]
        x, y, c = lax.axis_index("x"), lax.axis_index("y"), lax.axis_index("c")
        sibling = (x, y, 1 - c)
        chips = [(1 - x, y), (x, 1 - y), (1 - x, 1 - y)]
        index = lambda px, py, pc: 4 * px + 2 * py + pc
        barrier = pltpu.get_barrier_semaphore()
        for peer in [sibling] + [(*chip, c) for chip in chips]:
            pl.semaphore_signal(barrier, inc=1, device_id=peer, device_id_type=MESH)
        pl.semaphore_wait(barrier, 4)

        def copy(t, k, slot, to, src=None):
            return pltpu.make_async_remote_copy(
                src_ref=land_refs[t].at[slot] if src is None else src, dst_ref=land_refs[t].at[slot],
                send_sem=send_sems.at[7 * t + k], recv_sem=recv_sems.at[7 * t + k], device_id=to, device_id_type=MESH)

        me = index(x, y, c)
        local = []
        for t in range(n):
            cp = pltpu.make_async_copy(src_refs[t], land_refs[t].at[me], local_sems.at[t])
            cp.start()
            local.append(cp)
        sends = []
        for t in range(n):
            sends.append(copy(t, 0, me, sibling, src=src_refs[t]))
            sends += [copy(t, 1 + j, me, (*chip, c), src=src_refs[t]) for j, chip in enumerate(chips)]
        for cp in sends:
            cp.start()
        for j, chip in enumerate(chips):
            for t in range(n):
                copy(t, 1 + j, index(*chip, c), sibling, src=src_refs[t]).wait_recv()
                passed = copy(t, 4 + j, index(*chip, c), sibling)
                passed.start()
                sends.append(passed)
        for t in range(n):
            copy(t, 0, index(x, y, 1 - c), sibling, src=src_refs[t]).wait_recv()
        for j, chip in enumerate(chips):
            for t in range(n):
                copy(t, 4 + j, index(*chip, 1 - c), sibling, src=src_refs[t]).wait_recv()
        for cp in sends:
            cp.wait_send()
        for cp in local:
            cp.wait()

    return pl.kernel(
        body, out_type=[_sds((N_DEV,) + a.shape, a.dtype) for a in srcs],
        mesh=plsc.ScalarSubcoreMesh(axis_name="sequencer", num_cores=1),
        scratch_types=[pltpu.SemaphoreType.DMA((7 * n,)), pltpu.SemaphoreType.DMA((7 * n,)), pltpu.SemaphoreType.DMA((n,))],
        compiler_params=pltpu.CompilerParams(collective_id=collective_id), name=name)(*srcs, *extra)


def _adamw(w, g, m, v):
    m = ADAM_B1 * m + (1.0 - ADAM_B1) * g
    v = ADAM_B2 * v + (1.0 - ADAM_B2) * (g * g)
    m_hat = m / (1.0 - ADAM_B1 ** ADAM_STEP)
    v_hat = v / (1.0 - ADAM_B2 ** ADAM_STEP)
    delta = -ADAM_LR * (m_hat / (jnp.sqrt(v_hat) + ADAM_EPS) + ADAM_WD * w)
    return delta, m, v


def _adam_big(w, parts, m, v, name, after=None):
    L, R, C = w.shape
    P = parts[0].shape[0]
    tr = _tile(R, (256, 128, 176, 64, 32, 16))
    nr = R // tr

    def body(w_ref, *refs):
        p_refs, (m_ref, v_ref, g_ref, d_ref, mo_ref, vo_ref) = refs[:L], refs[L:]
        for l in range(L):
            @pl.when(pl.program_id(0) == l)
            def _(p_ref=p_refs[l]):
                g = p_ref[0].astype(F32)
                for s in range(1, P):
                    g = g + p_ref[s].astype(F32)
                g_ref[...] = g
                d_ref[...], mo_ref[...], vo_ref[...] = _adamw(w_ref[...], g, m_ref[...], v_ref[...])

    row = pl.BlockSpec((None, tr, C), lambda l, i: (l, i, 0))
    park = lambda l_of: (lambda l, i: (0, jnp.where(l == l_of, i, 0 if l_of else nr - 1), 0))
    return _call(
        after, body, grid=(L, nr),
        in_specs=[row] + [pl.BlockSpec((P, tr, C), park(l)) for l in range(L)] + [row, row],
        out_specs=[row] * 4, out_shape=[_sds((L, R, C), F32)] * 4,
        compiler_params=_params(("arbitrary", "arbitrary")), name=name)(w, *parts, m, v)


SMALL = (("a_lower_bound", (2, 128), (2, D)), ("ln_gain", (3, 2, 128), (6, D)), ("ln_bias", (3, 2, 128), (6, D)),
         ("a_norm_gain", (1, 128), (1, 128)), ("kv_b", (1, 512), (1, 512)), ("b_b_q", (1, D), (1, D)),
         ("b_sinks", (1, ATT_QH), (1, 128)), ("b_b_out", (1, D), (1, D)), ("ple_b_gate", (2, D), (2, D)))


def _adam_small(parts, w, m, v, losses, after=None):
    k = len(SMALL)

    def body(*refs):
        p_refs, w_refs, m_refs, v_refs = refs[:k], refs[k:2 * k], refs[2 * k:3 * k], refs[3 * k:4 * k]
        loss_ref, outs, total_ref = refs[4 * k], refs[4 * k + 1:-1], refs[-1]
        total = loss_ref[0]
        for s in range(1, N_DEV):
            total = total + loss_ref[s]
        total_ref[...] = total
        me = _my_index()
        for i, (_, wshape, pshape) in enumerate(SMALL):
            cols = wshape[-1]
            lanes = slice(None) if cols == pshape[1] else (
                pl.ds(0, cols) if cols < 128 else pl.ds(pl.multiple_of(me * cols, cols), cols))
            at = [(slice(None), slice(None))] if len(wshape) == 2 else [
                (pl.ds(l * wshape[0] + kk, 1), (kk, pl.ds(l, 1), slice(None))) for kk in range(wshape[0]) for l in range(wshape[1])]
            for rows, own in at:
                g = p_refs[i][0, rows, lanes]
                for s in range(1, N_DEV):
                    g = g + p_refs[i][s, rows, lanes]
                g_ref, d_ref, mo_ref, vo_ref = outs[4 * i:4 * i + 4]
                g_ref[own] = g
                d_ref[own], mo_ref[own], vo_ref[own] = _adamw(w_refs[i][own], g, m_refs[i][own], v_refs[i][own])

    full = lambda shape: pl.BlockSpec(shape, lambda: (0,) * len(shape))
    names = [n for n, _, _ in SMALL]
    held = lambda a, ws: a.swapaxes(0, 1) if len(ws) == 3 else a.reshape(ws)
    back = lambda r, n: r.swapaxes(0, 1) if r.ndim == 3 else r.reshape(w[n].shape)
    res = _call(
        after, body,
        in_specs=[full((N_DEV,) + ps) for _, _, ps in SMALL] + [full(ws) for _, ws, _ in SMALL] * 3
        + [full((N_DEV, 1, 128))],
        out_specs=[full(ws) for _, ws, _ in SMALL for _ in range(4)] + [full((1, 128))],
        out_shape=[_sds(ws, F32) for _, ws, _ in SMALL for _ in range(4)] + [_sds((1, 128), F32)], name="adam_small")(
            *[parts[n] for n in names], *[held(a[n], ws) for a in (w, m, v) for n, ws, _ in SMALL], losses)
    return {n: [back(r, n) for r in res[4 * i:4 * i + 4]] for i, n in enumerate(names)}, res[-1][0, 0]


WEIGHTS = ("a_w_in", "a_lower_bound", "a_norm_gain", "a_w_out", "kv_w", "kv_b", "b_w_q", "b_b_q", "b_sinks", "b_w_out",
           "b_b_out", "ffn_w_gate_up", "ffn_w_down", "ple_w_up", "ple_w_gate", "ple_b_gate", "ln_gain", "ln_bias")


GATHER_GROUPS = (("a_w_in",), ("a_w_out", "gu0"), ("dn0", "pu0", "pg0"), ("kv_w", "b_w_q", "b_w_out"), ("gu1",),
                 ("dn1", "pu1", "pg1"))
KERNEL_LAYOUT = {
    "a_w_in": lambda a: a,
    "a_w_out": lambda a: a.reshape(D, D),
    "kv_w": lambda a: a.reshape(D, 2 * ATT_KVH * ATT_HD),
    "b_w_q": lambda a: a.reshape(D, D),
    "b_w_out": lambda a: a.reshape(D, D),
    "gu": lambda a: a.reshape(2, 4, FFN_B, D),
    "dn": lambda a: a.reshape(4, FFN_B, D),
    "pu": lambda a: a,
    "pg": lambda a: a.reshape(D, D),
}
_row_blocks = lambda a: a.reshape(N_DEV, -1, a.shape[-1])
OWNER_BLOCKS = {
    "a_w_in": lambda g: g,
    "a_w_out": _row_blocks,
    "kv_w": _row_blocks,
    "b_w_q": _row_blocks,
    "b_w_out": _row_blocks,
    "gu": lambda g: g,
    "dn": lambda g: _row_blocks(g.reshape(FFN_H, D)),
    "pu": lambda g: g.reshape(PLE_DIM, N_DEV, 128).transpose(1, 0, 2),
    "pg": _row_blocks,
}
ADAM_PARTS = (("kv_w", ("kv_w",)), ("b_w_q", ("b_w_q",)), ("b_w_out", ("b_w_out",)), ("ffn_w_gate_up", ("gu0", "gu1")),
              ("ffn_w_down", ("dn0", "dn1")), ("ple_w_up", ("pu0", "pu1")), ("ple_w_gate", ("pg0", "pg1")),
              ("a_w_out", ("a_w_out",)), ("a_w_in", ("a_w_in",)))


def kernel(x, p, a_w_in, a_lower_bound, a_norm_gain, a_w_out, kv_w, kv_b, b_w_q, b_b_q, b_sinks, b_w_out, b_b_out, ffn_w_gate_up, ffn_w_down, ple_w_up, ple_w_gate, ple_b_gate, ln_gain, ln_bias, loss_target, m_a_w_in, m_a_lower_bound, m_a_norm_gain, m_a_w_out, m_kv_w, m_kv_b, m_b_w_q, m_b_b_q, m_b_sinks, m_b_w_out, m_b_b_out, m_ffn_w_gate_up, m_ffn_w_down, m_ple_w_up, m_ple_w_gate, m_ple_b_gate, m_ln_gain, m_ln_bias, v_a_w_in, v_a_lower_bound, v_a_norm_gain, v_a_w_out, v_kv_w, v_kv_b, v_b_w_q, v_b_b_q, v_b_sinks, v_b_w_out, v_b_b_out, v_ffn_w_gate_up, v_ffn_w_down, v_ple_w_up, v_ple_w_gate, v_ple_b_gate, v_ln_gain, v_ln_bias):
    given = dict(locals())
    w = {n: given[n] for n in WEIGHTS}
    m = {n: given["m_" + n] for n in WEIGHTS}
    v = {n: given["v_" + n] for n in WEIGHTS}
    shards = {"a_w_in": a_w_in[0], "a_w_out": a_w_out[0], "kv_w": kv_w, "b_w_q": b_w_q[0], "b_w_out": b_w_out[0]}
    for l in range(2):
        shards.update({f"gu{l}": ffn_w_gate_up[l].T, f"dn{l}": ffn_w_down[l], f"pu{l}": ple_w_up[l], f"pg{l}": ple_w_gate[l]})
    sharded_small = [a_lower_bound, ln_gain.swapaxes(0, 1), ln_bias.swapaxes(0, 1)]
    gathered = {}
    for gi, g in enumerate(GATHER_GROUPS):
        lands = _sequencer_gather([shards[n].astype(CDT) for n in g] + (sharded_small if gi == 0 else []),
                                  f"gather{gi}", gi)
        for n, a in zip(g, lands):
            gathered[n] = KERNEL_LAYOUT[n.rstrip("01")](a)
        if gi == 0:
            alb = lands[len(g)].transpose(1, 0, 2).reshape(2, D)
            lng, lnb = [a.transpose(2, 1, 0, 3).reshape(2, 3, D) for a in lands[len(g) + 1:]]

    getw = gathered.__getitem__

    sm = {"a_lower_bound": alb, "ln_gain": lng, "ln_bias": lnb,
          "a_norm_gain": a_norm_gain, "kv_b": kv_b, "b_b_q": b_b_q[0], "b_sinks": b_sinks, "b_b_out": b_b_out,
          "ple_b_gate": ple_b_gate}

    scatters, small_parts = [], {}

    def emit(grads, small=None):
        names = list(grads)
        blocks = [OWNER_BLOCKS[n.rstrip("01")](grads[n]) for n in names]
        partials = [] if small is None else [small[n].reshape(ps) for n, _, ps in SMALL] + [small["loss"]]
        lands = _sequencer_exchange(blocks + partials, ["scatter"] * len(blocks) + ["gather"] * len(partials),
                                    f"scatter{len(scatters)}", len(GATHER_GROUPS) + len(scatters))
        scatters.append(dict(zip(names, lands)))
        small_parts.update(zip([n for n, _, _ in SMALL] + ["loss"], lands[len(blocks):]))
        return blocks + (list(scatters[-4].values()) if len(scatters) >= 4 else [])

    loss, grad_x, gs = _local_step(x[0], p[:, 0], loss_target[0], getw, sm, emit)

    out, parts, last = {}, {}, [grad_x]
    for landed in scatters:
        parts.update(landed)
        for n, keys in ADAM_PARTS:
            if n in out or not all(key in parts for key in keys):
                continue
            lrc = (1,) * (3 - w[n].ndim) + w[n].shape
            shard = (lambda a: a.reshape(lrc).swapaxes(1, 2)) if n == "ffn_w_gate_up" else (lambda a: a.reshape(lrc))
            res = _adam_big(shard(w[n]), [parts[key] for key in keys], shard(m[n]), shard(v[n]), "adam_" + n, after=last)
            out[n] = [(r.swapaxes(1, 2) if n == "ffn_w_gate_up" else r).reshape(w[n].shape) for r in res]
            last = [res[3]]
    small_out, loss = _adam_small(small_parts, w, m, v, small_parts["loss"], after=last)
    out.update(small_out)
    res = [loss, grad_x[None]]
    for i in range(4):
        res += [out[n][i] for n in WEIGHTS]
    return tuple(res)
```

```python
import jax
import jax.numpy as jnp
from jax import lax
from jax.experimental import pallas as pl
from jax.experimental.pallas import tpu as pltpu
from jax.experimental.pallas import tpu_sc as plsc

F32 = jnp.float32
CDT = jnp.bfloat16

N_DEV = 8
D = 1024
HG_H, HG_DK, HG_CH = 8, 128, 64
HG_HPB = 4
HG_CPB = 8
ATT_HD, ATT_QH, ATT_KVH, ATT_G, WINDOW = 64, 16, 4, 4, 128
ATT_BPB = 1
FFN_H = 2816
FFN_B = FFN_H // 4
PLE_DIM = 256
ALPHA = (2.0 * 2) ** 0.25
LN_EPS = 1e-5
RMS_EPS = 1e-6
ADAM_LR, ADAM_B1, ADAM_B2, ADAM_EPS, ADAM_WD, ADAM_STEP = 0.001, 0.9, 0.999, 1e-08, 0.01, 10
ROW_TILES = (512, 256, 128, 64)
VMEM_LIMIT = 48 * 1024 * 1024
NEG = -1e30

MESH = pl.DeviceIdType.MESH


def _tile(n, cands=ROW_TILES):
    for t in cands:
        if n % t == 0:
            return t
    return n


def _sds(shape, dtype):
    return jax.ShapeDtypeStruct(tuple(shape), dtype)


def _params(sem):
    return pltpu.CompilerParams(dimension_semantics=sem, vmem_limit_bytes=VMEM_LIMIT)


def _dot(a, b):
    return jnp.dot(a.astype(CDT), b.astype(CDT), preferred_element_type=F32)


def _dot_nt(a, b):
    return lax.dot_general(a.astype(CDT), b.astype(CDT), (((1,), (1,)), ((), ())), preferred_element_type=F32)


def _dot_tn(a, b):
    return lax.dot_general(a.astype(CDT), b.astype(CDT), (((0,), (0,)), ((), ())), preferred_element_type=F32)


def _sigmoid(x):
    return jax.nn.sigmoid(x)


def _ln_fwd(z, g, b):
    mu = jnp.mean(z, axis=-1, keepdims=True)
    zc = z - mu
    var = jnp.mean(zc * zc, axis=-1, keepdims=True)
    return zc * lax.rsqrt(var + LN_EPS) * g + b


def _ln_bwd(z, g, dy):
    mu = jnp.mean(z, axis=-1, keepdims=True)
    zc = z - mu
    var = jnp.mean(zc * zc, axis=-1, keepdims=True)
    rstd = lax.rsqrt(var + LN_EPS)
    xhat = zc * rstd
    dxh = dy * g
    dz = rstd * (dxh - jnp.mean(dxh, axis=-1, keepdims=True) - xhat * jnp.mean(dxh * xhat, axis=-1, keepdims=True))
    return dz, xhat


def _colsum(x):
    return jnp.sum(x, axis=0, keepdims=True)


def _acc(ref, val, first):
    @pl.when(first)
    def _():
        ref[...] = val

    @pl.when(jnp.logical_not(first))
    def _():
        ref[...] += val


def _zero_at(ref, first):
    @pl.when(first)
    def _():
        ref[...] = jnp.zeros_like(ref)


def _call(after, body, **kw):
    after = [] if after is None else list(after)
    specs = list(kw["in_specs"])
    kw["in_specs"] = [pl.BlockSpec(memory_space=pl.ANY)] * len(after) + specs

    def ordered_body(*refs):
        body(*refs[len(after):])

    call = pl.pallas_call(ordered_body, **kw)
    return lambda *args: call(*after, *args)


def _mm_tn(a3, b3, G, amap, bmap, ablock, bblock, out_shape, oblock, omap, name="mm_tn"):
    S = a3.shape[1]

    def body(a_ref, b_ref, o_ref):
        o_ref[...] = _dot_tn(a_ref[...], b_ref[...]).astype(o_ref.dtype)

    return _call(
        None, body, grid=(G, 1),
        in_specs=[pl.BlockSpec(ablock(S), amap), pl.BlockSpec(bblock(S), bmap)],
        out_specs=pl.BlockSpec(oblock, omap), out_shape=_sds(out_shape, CDT),
        compiler_params=_params(("arbitrary", "arbitrary")), name=name)(a3, b3)


def _wgrad(a3, b3, name):
    Ga, S, M = a3.shape
    Gb, _, N = b3.shape
    G = max(Ga, Gb)
    return _mm_tn(
        a3, b3, G,
        (lambda g, k: (g, k, 0)) if Ga > 1 else (lambda g, k: (0, k, 0)),
        (lambda g, k: (g, k, 0)) if Gb > 1 else (lambda g, k: (0, k, 0)),
        lambda tk: (None, tk, M), lambda tk: (None, tk, N),
        (G, M, N), (None, M, N), lambda g, k: (g, 0, 0), name=name)


def _mixout_ln(u3, w3, bias, xin, gain, beta, name):
    G, S, Kb = u3.shape
    tm = _tile(S)

    def body(u_ref, w_ref, b_ref, x_ref, g_ref, be_ref, z_ref, xo_ref, xob_ref):
        h = b_ref[...] + _dot(u_ref[0], w_ref[0])
        for g in range(1, G):
            h = h + _dot(u_ref[g], w_ref[g])
        z = ALPHA * x_ref[...] + h
        z_ref[...] = z
        y = _ln_fwd(z, g_ref[...], be_ref[...])
        xo_ref[...] = y
        xob_ref[...] = y.astype(CDT)

    row = pl.BlockSpec((tm, D), lambda i: (i, 0))
    vec = pl.BlockSpec((1, D), lambda i: (0, 0))
    return _call(
        None, body, grid=(S // tm,),
        in_specs=[pl.BlockSpec((G, tm, Kb), lambda i: (0, i, 0)), pl.BlockSpec((G, Kb, D), lambda i: (0, 0, 0)),
                  vec, row, vec, vec],
        out_specs=[row, row, row], out_shape=[_sds((S, D), F32), _sds((S, D), F32), _sds((S, D), CDT)],
        compiler_params=_params(("arbitrary",)), name=name)(u3, w3, bias, xin, gain, beta)


def _ffn_fwd(xin, xin_b, wgu, wdn, gain, beta, name):
    S = xin.shape[0]
    tm = _tile(S)

    def hidden(xb_ref, wgu_ref, gu_ref, hid_ref):
        xb = xb_ref[...]
        gate = _dot_nt(xb, wgu_ref[0])
        up = _dot_nt(xb, wgu_ref[1])
        gu_ref[0] = gate.astype(CDT)
        gu_ref[1] = up.astype(CDT)
        hid_ref[...] = (gate * _sigmoid(gate) * up).astype(CDT)

    gu, hid = _call(
        None, hidden, grid=(4, S // tm),
        in_specs=[pl.BlockSpec((tm, D), lambda j, i: (i, 0)), pl.BlockSpec((2, None, FFN_B, D), lambda j, i: (0, j, 0, 0))],
        out_specs=[pl.BlockSpec((2, None, tm, FFN_B), lambda j, i: (0, j, i, 0)),
                   pl.BlockSpec((None, tm, FFN_B), lambda j, i: (j, i, 0))],
        out_shape=[_sds((2, 4, S, FFN_B), CDT), _sds((4, S, FFN_B), CDT)],
        compiler_params=_params(("arbitrary", "arbitrary")), name=name + "_hidden")(xin_b, wgu)

    def down(x_ref, hid_ref, wdn_ref, g_ref, be_ref, z_ref, xo_ref, xob_ref):
        z = ALPHA * x_ref[...]
        for j in range(4):
            z = z + _dot(hid_ref[j], wdn_ref[j])
        z_ref[...] = z
        y = _ln_fwd(z, g_ref[...], be_ref[...])
        xo_ref[...] = y
        xob_ref[...] = y.astype(CDT)

    row = pl.BlockSpec((tm, D), lambda i: (i, 0))
    vec = pl.BlockSpec((1, D), lambda i: (0, 0))
    z, xo, xob = _call(
        None, down, grid=(S // tm,),
        in_specs=[row, pl.BlockSpec((4, tm, FFN_B), lambda i: (0, i, 0)), pl.BlockSpec((4, FFN_B, D), lambda i: (0, 0, 0)),
                  vec, vec],
        out_specs=[row, row, row], out_shape=[_sds((S, D), F32), _sds((S, D), F32), _sds((S, D), CDT)],
        compiler_params=_params(("arbitrary",)), name=name + "_down")(xin, hid, wdn, gain, beta)
    return gu, hid, z, xo, xob


def _ple_fwd(xin, xin_b, p_b, wpg, bgate, wpu, gain, beta, name, target=None):
    S = xin.shape[0]
    tm = _tile(S)

    def body(x_ref, xb_ref, p_ref, wpg_ref, bg_ref, wpu_ref, g_ref, be_ref, *rest):
        if target is None:
            sg_ref, up_ref, z_ref, xo_ref, xob_ref = rest
        else:
            t_ref, sg_ref, up_ref, z_ref, dy_ref, l_ref = rest
        sg =_sigmoid(_dot(xb_ref[...], wpg_ref[...]) + bg_ref[...])
        pb = p_ref[...]
        up = jnp.concatenate([_dot(pb, wpu_ref[j]) for j in range(N_DEV)], axis=-1)
        sg_ref[...] = sg.astype(CDT)
        up_ref[...] = (up * sg * (1.0 - sg)).astype(CDT)
        z = ALPHA * x_ref[...] + sg * up
        z_ref[...] = z
        y = _ln_fwd(z, g_ref[...], be_ref[...])
        if target is None:
            xo_ref[...] = y
            xob_ref[...] = y.astype(CDT)
        else:
            e = y - t_ref[...]
            dy_ref[...] = e * (1.0 / D)
            part = 0.5 * jnp.sum(jnp.sum(e * e, axis=-1, keepdims=True) * (1.0 / D), axis=0, keepdims=True)
            _acc(l_ref, jnp.broadcast_to(part, l_ref.shape), pl.program_id(0) == 0)

    row = pl.BlockSpec((tm, D), lambda i: (i, 0))
    vec = pl.BlockSpec((1, D), lambda i: (0, 0))
    last = target is not None
    return _call(
        None, body, grid=(S // tm,),
        in_specs=[row, row, pl.BlockSpec((tm, PLE_DIM), lambda i: (i, 0)), pl.BlockSpec((D, D), lambda i: (0, 0)), vec,
                  pl.BlockSpec((N_DEV, PLE_DIM, D // N_DEV), lambda i: (0, 0, 0)), vec, vec] + [row] * last,
        out_specs=[row] * 4 + [pl.BlockSpec((1, 128), lambda i: (0, 0)) if last else row],
        out_shape=[_sds((S, D), CDT)] * 2 + [_sds((S, D), F32)] * 2 + [_sds((1, 128), F32) if last else _sds((S, D), CDT)],
        compiler_params=_params(("arbitrary",)), name=name)(
            xin, xin_b, p_b, wpg, bgate, wpu, gain, beta, *([target] if last else []))


def _ple_bwd(dy, z, sg, up, gain, wpg, z_ffn, gain_ffn, name, after=None):
    S = dy.shape[0]
    tm = _tile(S)

    def body(dy_ref, z_ref, sg_ref, up_ref, g_ref, wpg_ref, zf_ref, gf_ref, dzf_ref, dzfb_ref, dgl_ref, dup_ref,
             dgain_ref, dbeta_ref, dbg_ref, dgainf_ref, dbetaf_ref):
        first = pl.program_id(0) == 0
        dy_ = dy_ref[...]
        dz, xhat = _ln_bwd(z_ref[...], g_ref[...], dy_)
        dgl = dz * up_ref[...].astype(F32)
        dgl_ref[...] = dgl.astype(CDT)
        dup_ref[...] = (dz * sg_ref[...].astype(F32)).astype(CDT)
        dx = ALPHA * dz + _dot_nt(dgl, wpg_ref[...])
        dzf, xhatf = _ln_bwd(zf_ref[...], gf_ref[...], dx)
        dzf_ref[...] = dzf
        dzfb_ref[...] = dzf.astype(CDT)
        _acc(dgain_ref, _colsum(dy_ * xhat), first)
        _acc(dbeta_ref, _colsum(dy_), first)
        _acc(dbg_ref, _colsum(dgl), first)
        _acc(dgainf_ref, _colsum(dx * xhatf), first)
        _acc(dbetaf_ref, _colsum(dx), first)

    row = pl.BlockSpec((tm, D), lambda i: (i, 0))
    vec = pl.BlockSpec((1, D), lambda i: (0, 0))
    return _call(
        after, body, grid=(S // tm,),
        in_specs=[row, row, row, row, vec, pl.BlockSpec((D, D), lambda i: (0, 0)), row, vec],
        out_specs=[row, row, row, row] + [vec] * 5,
        out_shape=[_sds((S, D), F32)] + [_sds((S, D), CDT)] * 3 + [_sds((1, D), F32)] * 5,
        compiler_params=_params(("arbitrary",)), name=name)(dy, z, sg, up, gain, wpg, z_ffn, gain_ffn)


def _ffn_bwd_hidden(dzb, gu, wdn, name, after=None):
    S = dzb.shape[0]
    tm = _tile(S)

    def hidden(dzb_ref, gu_ref, wdn_ref, dgu_ref):
        dhid = _dot_nt(dzb_ref[...], wdn_ref[...])
        gate, up = gu_ref[0].astype(F32), gu_ref[1].astype(F32)
        sg = _sigmoid(gate)
        dgu_ref[0] = (dhid * up * (sg * (1.0 + gate * (1.0 - sg)))).astype(CDT)
        dgu_ref[1] = (dhid * (gate * sg)).astype(CDT)

    blocks = pl.BlockSpec((2, None, tm, FFN_B), lambda j, i: (0, j, i, 0))
    return _call(
        after, hidden, grid=(4, S // tm),
        in_specs=[pl.BlockSpec((tm, D), lambda j, i: (i, 0)), blocks, pl.BlockSpec((None, FFN_B, D), lambda j, i: (j, 0, 0))],
        out_specs=blocks, out_shape=_sds((2, 4, S, FFN_B), CDT),
        compiler_params=_params(("arbitrary", "arbitrary")), name=name + "_hidden")(dzb, gu, wdn)


def _ffn_bwd_input(dz, dgu, wgu, name, after=None):
    S = dz.shape[0]
    tm = _tile(S)

    def to_input(dz_ref, dgu_ref, wgu_ref, dx_ref):
        acc = ALPHA * dz_ref[...]
        for g in range(2):
            for j in range(4):
                acc = acc + _dot(dgu_ref[g, j], wgu_ref[g, j])
        dx_ref[...] = acc

    rows = pl.BlockSpec((tm, D), lambda i: (i, 0))
    return _call(
        after, to_input, grid=(S // tm,),
        in_specs=[rows, pl.BlockSpec((2, 4, tm, FFN_B), lambda i: (0, 0, i, 0)),
                  pl.BlockSpec((2, 4, FFN_B, D), lambda i: (0, 0, 0, 0))],
        out_specs=rows, out_shape=_sds((S, D), F32),
        compiler_params=_params(("arbitrary",)), name=name + "_input")(dz, dgu, wgu)


def _mixout_bwd(dy, z, gain, w3, du_dtype, name, after=None):
    S = dy.shape[0]
    G, Kb, _ = w3.shape
    tm = _tile(S)

    def body(dy_ref, z_ref, g_ref, w_ref, dz_ref, dzb_ref, du_ref, dgain_ref, dbeta_ref, dbias_ref):
        first = pl.program_id(0) == 0
        dy_ = dy_ref[...]
        dz, xhat = _ln_bwd(z_ref[...], g_ref[...], dy_)
        dz_ref[...] = dz
        dzb = dz.astype(CDT)
        dzb_ref[...] = dzb
        for g in range(G):
            du_ref[g] = _dot_nt(dzb, w_ref[g]).astype(du_ref.dtype)
        _acc(dgain_ref, _colsum(dy_ * xhat), first)
        _acc(dbeta_ref, _colsum(dy_), first)
        _acc(dbias_ref, _colsum(dz), first)

    row = pl.BlockSpec((tm, D), lambda i: (i, 0))
    vec = pl.BlockSpec((1, D), lambda i: (0, 0))
    return _call(
        after, body, grid=(S // tm,), in_specs=[row, row, vec, pl.BlockSpec((G, Kb, D), lambda i: (0, 0, 0))],
        out_specs=[row, row, pl.BlockSpec((G, tm, Kb), lambda i: (0, i, 0)), vec, vec, vec],
        out_shape=[_sds((S, D), F32), _sds((S, D), CDT), _sds((G, S, Kb), du_dtype)] + [_sds((1, D), F32)] * 3,
        compiler_params=_params(("arbitrary",)), name=name)(dy, z, gain, w3)


def _half_select(low):
    r = lax.broadcasted_iota(jnp.int32, (2 * ATT_HD, ATT_HD), 0)
    c = lax.broadcasted_iota(jnp.int32, (2 * ATT_HD, ATT_HD), 1)
    return (r == c + (0 if low else ATT_HD)).astype(CDT)


def _half_place(low):
    r = lax.broadcasted_iota(jnp.int32, (ATT_HD, 2 * ATT_HD), 0)
    c = lax.broadcasted_iota(jnp.int32, (ATT_HD, 2 * ATT_HD), 1)
    return (c == r + (0 if low else ATT_HD)).astype(CDT)


def _pair_lanes(even, odd):
    return (jnp.dot(even, _half_place(True), preferred_element_type=F32)
            + jnp.dot(odd, _half_place(False), preferred_element_type=F32)).astype(CDT)


def _proj_heads(a, w, bias, heads, name):
    S, K = a.shape
    N = heads * ATT_HD
    tm = _tile(S)

    def body(a_ref, w_ref, b_ref, o_ref):
        acc = (_dot(a_ref[...], w_ref[...]) + b_ref[...]).astype(CDT)
        sel = (_half_select(True), _half_select(False))
        for h in range(heads):
            pair = acc[:, (h // 2) * 2 * ATT_HD:(h // 2 + 1) * 2 * ATT_HD]
            o_ref[h] = jnp.dot(pair, sel[h % 2], preferred_element_type=F32).astype(CDT)

    return _call(
        None, body, grid=(S // tm,),
        in_specs=[pl.BlockSpec((tm, K), lambda i: (i, 0)), pl.BlockSpec((K, N), lambda i: (0, 0)),
                  pl.BlockSpec((1, N), lambda i: (0, 0))],
        out_specs=pl.BlockSpec((heads, tm, ATT_HD), lambda i: (0, i, 0)), out_shape=_sds((heads, S, ATT_HD), CDT),
        compiler_params=_params(("arbitrary",)), name=name)(a, w, bias)


def _qkv_bwd(dz, dq, dkv4, wq, wkv, name, after=None):
    S = dz.shape[0]
    tm = _tile(S)
    HK = dkv4.shape[0]
    NK = HK * ATT_HD

    def body(dz_ref, dq_ref, dkv_ref, wq_ref, wkv_ref, dx_ref, dkvn_ref, dkvb_ref):
        first = pl.program_id(0) == 0
        dkvn = jnp.concatenate([_pair_lanes(dkv_ref[2 * i].astype(CDT), dkv_ref[2 * i + 1].astype(CDT))
                                for i in range(HK // 2)], axis=-1)
        dkvn_ref[...] = dkvn
        dx_ref[...] = ALPHA * dz_ref[...] + _dot_nt(dq_ref[...], wq_ref[...]) + _dot_nt(dkvn, wkv_ref[...])
        for h in range(HK):
            _acc(dkvb_ref.at[h], _colsum(dkv_ref[h]), first)

    row = pl.BlockSpec((tm, D), lambda i: (i, 0))
    return _call(
        after, body, grid=(S // tm,),
        in_specs=[row, row, pl.BlockSpec((HK, tm, ATT_HD), lambda i: (0, i, 0)),
                  pl.BlockSpec((D, D), lambda i: (0, 0)), pl.BlockSpec((D, NK), lambda i: (0, 0))],
        out_specs=[row, pl.BlockSpec((tm, NK), lambda i: (i, 0)), pl.BlockSpec((HK, 1, ATT_HD), lambda i: (0, 0, 0))],
        out_shape=[_sds((S, D), F32), _sds((S, NK), CDT), _sds((HK, 1, ATT_HD), F32)],
        compiler_params=_params(("arbitrary",)), name=name)(dz, dq, dkv4, wq, wkv)


def _inproj_fwd(xb, wain, name):
    S = xb.shape[0]
    tm = _tile(S) // 2
    nb = wain.shape[-1]

    def body(x_ref, w_ref, o_ref):
        x = x_ref[...]
        for j in range(N_DEV):
            o_ref[j // 2, :, pl.ds((j % 2) * nb, nb)] = _dot(x, w_ref[j])

    return _call(
        None, body, grid=(S // tm,),
        in_specs=[pl.BlockSpec((tm, D), lambda i: (i, 0)), pl.BlockSpec((N_DEV, D, nb), lambda i: (0, 0, 0))],
        out_specs=pl.BlockSpec((4, tm, D), lambda i: (0, i, 0)), out_shape=_sds((4, S, D), F32),
        compiler_params=_params(("arbitrary",)), name=name)(xb, wain)


def _inproj_bwd(dz, dproj, wain, name, after=None):
    S = dz.shape[0]
    tm = _tile(S)
    nb = wain.shape[-1]

    def body(dz_ref, dp_ref, w_ref, dx_ref):
        acc = ALPHA * dz_ref[...]
        for j in range(N_DEV):
            acc = acc + _dot_nt(dp_ref[j // 2, :, pl.ds((j % 2) * nb, nb)], w_ref[j])
        dx_ref[...] = acc

    row = pl.BlockSpec((tm, D), lambda i: (i, 0))
    return _call(
        after, body, grid=(S // tm,),
        in_specs=[row, pl.BlockSpec((4, tm, D), lambda i: (0, i, 0)), pl.BlockSpec((N_DEV, D, nb), lambda i: (0, 0, 0))],
        out_specs=row, out_shape=_sds((S, D), F32),
        compiler_params=_params(("arbitrary",)), name=name)(dz, dproj, wain)


def _running_sum(x, reverse=False):
    rows = x.shape[0]
    row = lax.broadcasted_iota(jnp.int32, x.shape, 0)
    step = 1
    while step < rows:
        if reverse:
            x = x + jnp.where(row < rows - step, pltpu.roll(x, rows - step, 0), 0.0)
        else:
            x = x + jnp.where(row >= step, pltpu.roll(x, step, 0), 0.0)
        step *= 2
    return x


def _hg_gates(q, f, alb_ref):
    a0, a1 = alb_ref[0:1, :], alb_ref[1:2, :]
    mx = jnp.maximum(a0, a1)
    e0, e1 = jnp.exp(a0 - mx), jnp.exp(a1 - mx)
    lb = e0 / (e0 + e1)
    sig = _sigmoid(f)
    forget = lb + (1.0 - lb) * sig
    k = (1.0 - lb) * _sigmoid(-f)
    qs = q * _sigmoid(q) * (HG_DK ** -0.5)
    return qs, k, jnp.log(forget), sig, lb, forget


def _hg_intra(qs, k, b, b_scr):
    b_scr[...] = b
    bm = b_scr[pl.ds(HG_CH // 2 - 1, 1), :]
    bl = b_scr[pl.ds(HG_CH - 1, 1), :]
    eb = jnp.exp(b)
    qb = qs * eb
    e_q = jnp.exp(b - bm)
    e_k = jnp.exp(bm - b)
    e_d = jnp.exp(bl - b)
    return qb, qs * e_q, k * e_k, k * e_d, jnp.exp(bl), eb, e_q, e_k, e_d


def _hgrn_fwd(proj, alb, ngain):
    S = proj.shape[1]
    nc = S // HG_CH
    nb = nc // HG_CPB
    rb, wb = HG_CPB * HG_CH, HG_HPB * HG_DK

    def body(pj_ref, alb_ref, ng_ref, o_ref, y_ref, st_ref, st_scr, b_scr):
        n = pl.program_id(1)

        @pl.when(n == 0)
        def _():
            st_scr[...] = jnp.zeros_like(st_scr)

        r = lax.broadcasted_iota(jnp.int32, (HG_CH, HG_CH), 0)
        c = lax.broadcasted_iota(jnp.int32, (HG_CH, HG_CH), 1)
        causal = r >= c
        for ci, j in [(ci, j) for ci in range(HG_CPB) for j in range(HG_HPB)]:
            rows, lanes = pl.ds(ci * HG_CH, HG_CH), pl.ds(j * HG_DK, HG_DK)
            q, f, v, g = pj_ref[0, rows, lanes], pj_ref[1, rows, lanes], pj_ref[2, rows, lanes], pj_ref[3, rows, lanes]
            qs, k, logf, _, _, _ = _hg_gates(q, f, alb_ref.at[:, lanes])
            b = _running_sum(logf)
            qb, qt, kt, kd, ebl, _, _, _, _ = _hg_intra(qs, k, b, b_scr.at[j, ci])
            st = st_scr[j]
            st_ref[j, ci] = st
            a = jnp.where(causal, _dot_nt(qt, kt), 0.0)
            o = _dot(a, v) + _dot_nt(qb, st)
            st_scr[j] = st * ebl + _dot_tn(v, kd)
            o_ref[rows, lanes] = o
            rinv = lax.rsqrt(jnp.mean(o * o, axis=-1, keepdims=True) + RMS_EPS)
            y_ref[rows, lanes] = (o * rinv * ng_ref[...] * (g * _sigmoid(g))).astype(CDT)

    blk = pl.BlockSpec((rb, wb), lambda h, n: (n, h))
    return _call(
        None, body, grid=(HG_H // HG_HPB, nb),
        in_specs=[pl.BlockSpec((4, rb, wb), lambda h, n: (0, n, h)), pl.BlockSpec((2, wb), lambda h, n: (0, h)),
                  pl.BlockSpec((1, HG_DK), lambda h, n: (0, 0))],
        out_specs=[blk, blk, pl.BlockSpec((HG_HPB, HG_CPB, HG_DK, HG_DK), lambda h, n: (h, n, 0, 0))],
        out_shape=[_sds((S, D), F32), _sds((S, D), CDT), _sds((HG_H, nc, HG_DK, HG_DK), F32)],
        scratch_shapes=[pltpu.VMEM((HG_HPB, HG_DK, HG_DK), F32), pltpu.VMEM((HG_HPB, HG_CPB, HG_CH, HG_DK), F32)],
        compiler_params=_params(("arbitrary", "arbitrary")), name="hgrn_fwd")(proj, alb, ngain)


def _hgrn_bwd(proj, alb, ngain, o, states, dy, after=None):
    S = proj.shape[1]
    nc = S // HG_CH
    nb = nc // HG_CPB
    rb, wb = HG_CPB * HG_CH, HG_HPB * HG_DK

    def body(pj_ref, alb_ref, ng_ref, o_ref, st_ref, dy_ref, dpj_ref, dalb_ref, dng_ref, dst_scr, b_scr):
        h, n = pl.program_id(0), pl.program_id(1)

        @pl.when(n == 0)
        def _():
            dst_scr[...] = jnp.zeros_like(dst_scr)
            dalb_ref[...] = jnp.zeros_like(dalb_ref)

        _zero_at(dng_ref, jnp.logical_and(h == 0, n == 0))
        ng = ng_ref[...]
        r = lax.broadcasted_iota(jnp.int32, (HG_CH, HG_CH), 0)
        c = lax.broadcasted_iota(jnp.int32, (HG_CH, HG_CH), 1)
        causal = r >= c
        dng = None
        for ci, j in [(ci, j) for ci in reversed(range(HG_CPB)) for j in range(HG_HPB)]:
            rows, lanes = pl.ds(ci * HG_CH, HG_CH), pl.ds(j * HG_DK, HG_DK)
            q, f, v, g = pj_ref[0, rows, lanes], pj_ref[1, rows, lanes], pj_ref[2, rows, lanes], pj_ref[3, rows, lanes]
            o_ = o_ref[rows, lanes]
            dy_ = dy_ref[rows, lanes]
            sg = _sigmoid(g)
            rinv = lax.rsqrt(jnp.mean(o_ * o_, axis=-1, keepdims=True) + RMS_EPS)
            nrm = o_ * rinv
            dr = dy_ * (g * sg)
            dg = dy_ * nrm * ng * (sg * (1.0 + g * (1.0 - sg)))
            dn = dr * ng
            do = rinv * (dn - nrm * jnp.mean(dn * nrm, axis=-1, keepdims=True))
            dng = _colsum(dr * nrm) if dng is None else dng + _colsum(dr * nrm)
            qs, k, logf, sig, lb, forget = _hg_gates(q, f, alb_ref.at[:, lanes])
            b = _running_sum(logf)
            qb, qt, kt, kd, ebl, eb, e_q, e_k, e_d = _hg_intra(qs, k, b, b_scr.at[j, ci])
            st = st_ref[j, ci]
            dstn = dst_scr[j]
            qt, kt, qb, kd = (t.astype(CDT).astype(F32) for t in (qt, kt, qb, kd))
            a = jnp.where(causal, _dot_nt(qt, kt), 0.0)
            da = jnp.where(causal, _dot_nt(do, v), 0.0)
            dv = _dot_tn(a, do) + _dot_nt(kd, dstn)
            dqb = _dot(do, st)
            dkd = _dot(v, dstn)
            dqt = _dot(da, kt)
            dkt = _dot_tn(da, qt)
            dbl = _colsum(dkd * kd) + ebl * _colsum(dstn * st)
            dst_scr[j] = dstn * ebl + _dot_tn(do, qb)
            dqs = dqt * e_q + dqb * eb
            dk = dkt * e_k + dkd * e_d
            db = dqt * qt + dqb * qb - dkt * kt - dkd * kd
            dlogf = _running_sum(db, reverse=True) + dbl
            dforget = dlogf / forget
            dsig = (1.0 - lb) * (dforget - dk)
            df = dsig * sig * (1.0 - sig)
            dlb = _colsum((dforget - dk) * (1.0 - sig))
            sq = _sigmoid(q)
            dq = dqs * (HG_DK ** -0.5) * (sq * (1.0 + q * (1.0 - sq)))
            dpj_ref[0, rows, lanes] = dq.astype(CDT)
            dpj_ref[1, rows, lanes] = df.astype(CDT)
            dpj_ref[2, rows, lanes] = dv.astype(CDT)
            dpj_ref[3, rows, lanes] = dg.astype(CDT)
            da0 = dlb * lb * (1.0 - lb)
            dalb_ref[pl.ds(0, 1), lanes] += da0
            dalb_ref[pl.ds(1, 1), lanes] -= da0
        dng_ref[...] += dng

    blk = pl.BlockSpec((rb, wb), lambda h, n: (nb - 1 - n, h))
    pj = pl.BlockSpec((4, rb, wb), lambda h, n: (0, nb - 1 - n, h))
    alb_blk = pl.BlockSpec((2, wb), lambda h, n: (0, h))
    ng_blk = pl.BlockSpec((1, HG_DK), lambda h, n: (0, 0))
    return _call(
        after, body, grid=(HG_H // HG_HPB, nb),
        in_specs=[pj, alb_blk, ng_blk, blk,
                  pl.BlockSpec((HG_HPB, HG_CPB, HG_DK, HG_DK), lambda h, n: (h, nb - 1 - n, 0, 0)), blk],
        out_specs=[pj, alb_blk, ng_blk],
        out_shape=[_sds((4, S, D), CDT), _sds((2, D), F32), _sds((1, HG_DK), F32)],
        scratch_shapes=[pltpu.VMEM((HG_HPB, HG_DK, HG_DK), F32), pltpu.VMEM((HG_HPB, HG_CPB, HG_CH, HG_DK), F32)],
        compiler_params=_params(("arbitrary", "arbitrary")), name="hgrn_bwd")(proj, alb, ngain, o, states, dy)


def _slope(h):
    return 2.0 ** (-8.0 * (h + 1) / ATT_QH)


def _attn_mask(n):
    qi = lax.broadcasted_iota(jnp.int32, (WINDOW, 2 * WINDOW), 0)
    si = lax.broadcasted_iota(jnp.int32, (WINDOW, 2 * WINDOW), 1)
    dist = qi - si + WINDOW
    valid = (dist >= 0) & (dist < WINDOW) & (n * WINDOW - WINDOW + si >= 0)
    return valid, dist.astype(F32)


def _attn_probs(qk, sink, slope, valid, distf):
    s = qk * (ATT_HD ** -0.5) - slope * distf
    s = jnp.where(valid, s, NEG)
    m = jnp.maximum(jnp.max(s, axis=-1, keepdims=True), sink)
    e = jnp.exp(s - m)
    es = jnp.exp(sink - m)
    inv = 1.0 / (jnp.sum(e, axis=-1, keepdims=True) + es)
    return e * inv, es * inv


def _attn_specs(S):
    steps = S // (ATT_BPB * WINDOW)
    cur = lambda H: pl.BlockSpec((H, ATT_BPB * WINDOW, ATT_HD), lambda n: (0, n, 0))
    prev = lambda H: pl.BlockSpec((H, WINDOW, ATT_HD), lambda n: (0, jnp.maximum(ATT_BPB * n - 1, 0), 0))
    return steps, cur, prev


def _attn_kv(kvc_ref, kvp_ref, head, bi):
    before = kvp_ref[head] if bi == 0 else kvc_ref[head, pl.ds((bi - 1) * WINDOW, WINDOW), :]
    return jnp.concatenate([before, kvc_ref[head, pl.ds(bi * WINDOW, WINDOW), :]], axis=0)


def _attn_fwd(q4, kv4, sinks):
    S = q4.shape[1]
    nb, cur, prev = _attn_specs(S)

    def body(sink_ref, q_ref, kvc_ref, kvp_ref, o_ref, p_ref, ps_ref):
        lane = lax.broadcasted_iota(jnp.int32, (1, 128), 1)
        place = (_half_place(True), _half_place(False))
        masks = [_attn_mask(pl.program_id(0) * ATT_BPB + bi) for bi in range(ATT_BPB)]
        sink_probs = [jnp.zeros((WINDOW, 128), F32) for _ in range(ATT_BPB)]
        heads = lambda kvh: range(kvh * ATT_G, (kvh + 1) * ATT_G)

        def scores(bi, kvh):
            rows = pl.ds(bi * WINDOW, WINDOW)
            kh = _attn_kv(kvc_ref, kvp_ref, kvh, bi)
            vh = _attn_kv(kvc_ref, kvp_ref, ATT_KVH + kvh, bi)
            v_pl = [jnp.dot(vh, m, preferred_element_type=F32).astype(CDT) for m in place]
            return v_pl, [_dot_nt(q_ref[h, rows, :], kh) for h in heads(kvh)]

        def softmax(bi, kvh, qks):
            rows = pl.ds(bi * WINDOW, WINDOW)
            probs = []
            for h, qk in zip(heads(kvh), qks):
                p, ps = _attn_probs(qk, sink_ref[0, h], _slope(h), *masks[bi])
                probs.append(p.astype(CDT))
                p_ref[h, rows, :] = probs[-1]
                sink_probs[bi] = sink_probs[bi] + jnp.where(lane == h, ps, 0.0)
            return probs

        def weighted_values(bi, kvh, v_pl, probs):
            rows = pl.ds(bi * WINDOW, WINDOW)
            for i in range(ATT_G // 2):
                lanes = pl.ds((kvh * ATT_G + 2 * i) * ATT_HD, 2 * ATT_HD)
                o_ref[rows, lanes] = (_dot(probs[2 * i], v_pl[0]) + _dot(probs[2 * i + 1], v_pl[1])).astype(CDT)

        groups = [(bi, kvh) for bi in range(ATT_BPB) for kvh in range(ATT_KVH)]
        ahead = scores(*groups[0])
        for gi, g in enumerate(groups):
            v_pl, qks = ahead
            probs = softmax(*g, qks)
            if gi + 1 < len(groups):
                ahead = scores(*groups[gi + 1])
            weighted_values(*g, v_pl, probs)
        for bi in range(ATT_BPB):
            ps_ref[pl.ds(bi * WINDOW, WINDOW), :] = sink_probs[bi]

    rows_spec = pl.BlockSpec((ATT_BPB * WINDOW, D), lambda n: (n, 0))
    return _call(
        None, body, grid=(nb,),
        in_specs=[pl.BlockSpec(memory_space=pltpu.SMEM), cur(ATT_QH), cur(2 * ATT_KVH), prev(2 * ATT_KVH)],
        out_specs=[rows_spec, pl.BlockSpec((ATT_QH, ATT_BPB * WINDOW, 2 * WINDOW), lambda n: (0, n, 0)),
                   pl.BlockSpec((ATT_BPB * WINDOW, 128), lambda n: (n, 0))],
        out_shape=[_sds((S, D), CDT), _sds((ATT_QH, S, 2 * WINDOW), CDT), _sds((S, 128), F32)],
        compiler_params=_params(("arbitrary",)), name="attn_fwd")(sinks, q4, kv4, kv4)


def _attn_bwd(q4, kv4, probs, sink_probs, do):
    S = q4.shape[1]
    nb, cur, prev = _attn_specs(S)

    def body(q_ref, kvc_ref, kvp_ref, p_ref, ps_ref, do_ref, dq_ref, dkv_ref, dbq_ref, dsink_ref):
        n = pl.program_id(0)
        first = n == 0

        @pl.when(first)
        def _():
            dkv_ref[...] = jnp.zeros_like(dkv_ref)
            dsink_ref[...] = jnp.zeros_like(dsink_ref)
            dbq_ref[...] = jnp.zeros_like(dbq_ref)

        lane = lax.broadcasted_iota(jnp.int32, (1, 128), 1)
        place = (_half_place(True), _half_place(False))
        row_dots = [jnp.zeros((WINDOW, 128), F32) for _ in range(ATT_BPB)]
        heads = lambda kvh: range(kvh * ATT_G, (kvh + 1) * ATT_G)
        pair_lanes = lambda h: pl.ds((h // 2) * 2 * ATT_HD, 2 * ATT_HD)

        def products_of_do(bi, kvh):
            rows = pl.ds(bi * WINDOW, WINDOW)
            kh = _attn_kv(kvc_ref, kvp_ref, kvh, bi)
            vh = _attn_kv(kvc_ref, kvp_ref, ATT_KVH + kvh, bi)
            k_pl = [jnp.dot(kh, m, preferred_element_type=F32).astype(CDT) for m in place]
            v_pl = [jnp.dot(vh, m, preferred_element_type=F32).astype(CDT) for m in place]
            dps = [_dot_nt(do_ref[rows, pair_lanes(h)], v_pl[h % 2]) for h in heads(kvh)]
            dv2 = [sum(_dot_tn(p_ref[h, rows, :], do_ref[rows, pair_lanes(h)]) for h in heads(kvh) if h % 2 == par)
                   for par in range(2)]
            return k_pl, dps, dv2

        def softmax_bwd(bi, kvh, dps):
            rows = pl.ds(bi * WINDOW, WINDOW)
            dss = []
            for h, dp in zip(heads(kvh), dps):
                p = p_ref[h, rows, :].astype(F32)
                dd = jnp.sum(p * dp, axis=-1, keepdims=True)
                dss.append((p * (dp - dd)).astype(CDT))
                row_dots[bi] = row_dots[bi] + jnp.where(lane == h, dd, 0.0)
            return dss

        def products_of_ds(bi, kvh, k_pl, dss, dv2):
            block = n * ATT_BPB + bi
            rows = pl.ds(bi * WINDOW, WINDOW)
            rows_cur = pl.ds(pl.multiple_of(block * WINDOW, WINDOW), WINDOW)
            rows_prev = pl.ds(pl.multiple_of(jnp.maximum(block - 1, 0) * WINDOW, WINDOW), WINDOW)
            for i in range(ATT_G // 2):
                h = kvh * ATT_G + 2 * i
                dq2 = (_dot(dss[2 * i], k_pl[0]) + _dot(dss[2 * i + 1], k_pl[1])) * (ATT_HD ** -0.5)
                dq_ref[rows, pair_lanes(h)] = dq2.astype(CDT)
                dbq_ref[:, pair_lanes(h)] += _colsum(dq2)
            dk = sum(_dot_tn(ds, q_ref[h, rows, :]) for h, ds in zip(heads(kvh), dss)) * (ATT_HD ** -0.5)
            dv = dv2[0][:, :ATT_HD] + pltpu.roll(dv2[1], ATT_HD, 1)[:, :ATT_HD]
            dkv_ref[kvh, rows_prev, :] += dk[:WINDOW]
            dkv_ref[kvh, rows_cur, :] += dk[WINDOW:]
            dkv_ref[ATT_KVH + kvh, rows_prev, :] += dv[:WINDOW]
            dkv_ref[ATT_KVH + kvh, rows_cur, :] += dv[WINDOW:]

        groups = [(bi, kvh) for bi in range(ATT_BPB) for kvh in range(ATT_KVH)]
        ahead = products_of_do(*groups[0])
        for gi, g in enumerate(groups):
            k_pl, dps, dv2 = ahead
            dss = softmax_bwd(*g, dps)
            if gi + 1 < len(groups):
                ahead = products_of_do(*groups[gi + 1])
            products_of_ds(*g, k_pl, dss, dv2)
        dsinks = jnp.zeros((1, 128), F32)
        for bi in range(ATT_BPB):
            dsinks = dsinks - _colsum(ps_ref[pl.ds(bi * WINDOW, WINDOW), :] * row_dots[bi])
        dsink_ref[...] += dsinks

    rows_spec = pl.BlockSpec((ATT_BPB * WINDOW, D), lambda n: (n, 0))
    return _call(
        None, body, grid=(nb,),
        in_specs=[cur(ATT_QH), cur(2 * ATT_KVH), prev(2 * ATT_KVH),
                  pl.BlockSpec((ATT_QH, ATT_BPB * WINDOW, 2 * WINDOW), lambda n: (0, n, 0)),
                  pl.BlockSpec((ATT_BPB * WINDOW, 128), lambda n: (n, 0)), rows_spec],
        out_specs=[rows_spec, pl.BlockSpec((2 * ATT_KVH, S, ATT_HD), lambda n: (0, 0, 0)),
                   pl.BlockSpec((1, D), lambda n: (0, 0)), pl.BlockSpec((1, 128), lambda n: (0, 0))],
        out_shape=[_sds((S, D), CDT), _sds((2 * ATT_KVH, S, ATT_HD), F32), _sds((1, D), F32),
                   _sds((1, 128), F32)],
        compiler_params=_params(("arbitrary",)), name="attn_bwd")(q4, kv4, kv4, probs, sink_probs, do)


def _local_step(x, p, target, getw, sm, emit):
    S = x.shape[0]
    vec = lambda a: a.reshape(1, -1)
    ln_g = lambda l, k: vec(sm["ln_gain"][l, k])
    ln_b = lambda l, k: vec(sm["ln_bias"][l, k])
    xb = x.astype(CDT)
    pb = p.astype(CDT)

    proj = _inproj_fwd(xb, getw("a_w_in"), "a_in")
    o_a, y_a, states = _hgrn_fwd(proj, sm["a_lower_bound"], sm["a_norm_gain"])
    zeros = jnp.zeros((1, D), F32)
    z = [[None] * 3 for _ in range(2)]
    xs = [[None] * 3 for _ in range(2)]
    xbs = [[None] * 3 for _ in range(2)]
    z[0][0], xs[0][0], xbs[0][0] = _mixout_ln(y_a[None], getw("a_w_out")[None], zeros, x, ln_g(0, 0), ln_b(0, 0),
                                              "a_out_ln")
    gu, hid, sgs, ups = [None, None], [None, None], [None, None], [None, None]

    def ffn_ple(l, target=None):
        wgu = getw(f"gu{l}")
        gu[l], hid[l], z[l][1], xs[l][1], xbs[l][1] = _ffn_fwd(
            xs[l][0], xbs[l][0], wgu, getw(f"dn{l}"), ln_g(l, 1), ln_b(l, 1), f"ffn_fwd{l}")
        sgs[l], ups[l], z[l][2], *out = _ple_fwd(
            xs[l][1], xbs[l][1], pb[l], getw(f"pg{l}"), vec(sm["ple_b_gate"][l]), getw(f"pu{l}"), ln_g(l, 2),
            ln_b(l, 2), f"ple_fwd{l}", target=target)
        if target is None:
            xs[l][2], xbs[l][2] = out
        return out

    ffn_ple(0)
    x3, x3b = xs[0][2], xbs[0][2]
    w_kv, w_q, w_bo = getw("kv_w"), getw("b_w_q"), getw("b_w_out")
    kv4 = _proj_heads(x3b, w_kv, vec(sm["kv_b"]), 2 * ATT_KVH, "kv_proj")
    q4 = _proj_heads(x3b, w_q, vec(sm["b_b_q"]), ATT_QH, "q_proj")
    o_b, probs, sink_probs = _attn_fwd(q4, kv4, sm["b_sinks"])
    z[1][0], xs[1][0], xbs[1][0] = _mixout_ln(o_b[None], w_bo[None], sm["b_b_out"], x3, ln_g(1, 0), ln_b(1, 0),
                                              "b_out_ln")
    dy, loss = ffn_ple(1, target)

    gs = {}
    d_ln_g = [[None] * 3 for _ in range(2)]
    d_ln_b = [[None] * 3 for _ in range(2)]
    g_bg = [None, None]

    def ffn_ple_bwd(l, dy, after=None):
        dz2, dzb, dgl, dup, d_ln_g[l][2], d_ln_b[l][2], g_bg[l], d_ln_g[l][1], d_ln_b[l][1] = _ple_bwd(
            dy, z[l][2], sgs[l], ups[l], ln_g(l, 2), getw(f"pg{l}"), z[l][1], ln_g(l, 1), f"ple_bwd{l}", after=after)
        g_pg = _wgrad(xbs[l][1][None], dgl[None], f"g_ple_gate{l}")[0]
        g_pu = _wgrad(pb[l][None], dup[None], f"g_ple_up{l}")[0]
        g_dn = _wgrad(hid[l], dzb[None], f"g_ffn_down{l}")
        if l == 1:
            tok = emit({f"pg{l}": g_pg, f"pu{l}": g_pu})
            tok = tok + emit({f"dn{l}": g_dn})
        else:
            tok = emit({f"pg{l}": g_pg, f"pu{l}": g_pu, f"dn{l}": g_dn})
        dgu = _ffn_bwd_hidden(dzb, gu[l], getw(f"dn{l}"), f"ffn_bwd{l}", after=tok)
        g_gu = _wgrad(dgu.reshape(8, S, FFN_B), xbs[l][0][None], f"g_ffn_gate_up{l}")
        tok = emit({f"gu{l}": g_gu})
        dx1 = _ffn_bwd_input(dz2, dgu, getw(f"gu{l}"), f"ffn_bwd{l}", after=tok)
        return dx1, None

    dx1, tok = ffn_ple_bwd(1, dy)
    dz, dzb, do, d_ln_g[1][0], d_ln_b[1][0], gs["b_b_out"] = _mixout_bwd(dx1, z[1][0], ln_g(1, 0), w_bo[None], CDT,
                                                                        "b_out_bwd", after=tok)
    g_bo = _wgrad(o_b[None], dzb[None], "g_b_w_out")[0]
    dq, dkv4, dbq, dsinks = _attn_bwd(q4, kv4, probs, sink_probs, do[0])
    gs["b_b_q"] = dbq
    gs["b_sinks"] = dsinks
    g_q = _wgrad(x3b[None], dq[None], "g_b_w_q")[0]
    dx3, dkv, gs["kv_b"] = _qkv_bwd(dz, dq, dkv4, w_q, w_kv, "qkv_bwd", after=tok)
    g_kv = _wgrad(x3b[None], dkv[None], "g_kv_w")[0]
    tok = emit({"b_w_out": g_bo, "b_w_q": g_q, "kv_w": g_kv})
    dx1, tok = ffn_ple_bwd(0, dx3, tok)
    w_ao = getw("a_w_out")
    dz, dzb, dyr, d_ln_g[0][0], d_ln_b[0][0], _ = _mixout_bwd(dx1, z[0][0], ln_g(0, 0), w_ao[None], F32, "a_out_bwd",
                                                              after=tok)
    g_ao = _wgrad(y_a[None], dzb[None], "g_a_w_out")[0]
    tok = emit({"a_w_out": g_ao})
    dproj, gs["a_lower_bound"], gs["a_norm_gain"] = _hgrn_bwd(proj, sm["a_lower_bound"], sm["a_norm_gain"], o_a, states,
                                                              dyr[0], after=tok)
    tk = lambda t: (None, t, D)
    g_ain = _mm_tn(xb[None], dproj, N_DEV, lambda g, k: (0, k, 0), lambda g, k: (g // 2, k, g % 2),
                   tk, lambda t: (None, t, 512), (N_DEV, D, 512), (None, D, 512), lambda g, k: (g, 0, 0), name="g_a_w_in")
    gs["ple_b_gate"] = jnp.concatenate(g_bg, axis=0)
    gs["ln_gain"] = jnp.stack([jnp.concatenate(r, axis=0) for r in d_ln_g])
    gs["ln_bias"] = jnp.stack([jnp.concatenate(r, axis=0) for r in d_ln_b])
    gs["loss"] = loss
    tok = emit({"a_w_in": g_ain}, small=gs)
    grad_x = _inproj_bwd(dz, dproj, getw("a_w_in"), "a_in_bwd", after=tok)
    return loss, grad_x, gs


def _peer(k):
    x, y, c = lax.axis_index("x"), lax.axis_index("y"), lax.axis_index("c")
    px = 1 - x if k & 4 else x
    py = 1 - y if k & 2 else y
    pc = 1 - c if k & 1 else c
    return (px, py, pc), 4 * px + 2 * py + pc


def _my_index():
    return 4 * lax.axis_index("x") + 2 * lax.axis_index("y") + lax.axis_index("c")


def _piece_copy(mode, src, land, send_sems, recv_sems, t, k, sender, receiver, peer):
    return pltpu.make_async_remote_copy(
        src_ref=src if mode == "gather" else src.at[receiver], dst_ref=land.at[sender],
        send_sem=send_sems.at[t * 7 + k - 1], recv_sem=recv_sems.at[t * 7 + k - 1], device_id=peer, device_id_type=MESH)


def _sequencer_exchange(srcs, modes, name, collective_id, after=None):
    n = len(srcs)
    land_shapes = [((N_DEV,) + a.shape) if mode == "gather" else a.shape for a, mode in zip(srcs, modes)]
    extra = [] if after is None else [after]

    def body(*refs):
        src_refs, land_refs = refs[:n], refs[n + len(extra):2 * n + len(extra)]
        send_sems, recv_sems, local_sems = refs[2 * n + len(extra):]
        barrier = pltpu.get_barrier_semaphore()
        for k in range(1, N_DEV):
            pl.semaphore_signal(barrier, inc=1, device_id=_peer(k)[0], device_id_type=MESH)
        pl.semaphore_wait(barrier, N_DEV - 1)
        me = _my_index()
        local = []
        for i in range(n):
            cp = pltpu.make_async_copy(src_refs[i] if modes[i] == "gather" else src_refs[i].at[me], land_refs[i].at[me],
                                       local_sems.at[i])
            cp.start()
            local.append(cp)
        for k in range(1, N_DEV):
            peer, pid = _peer(k)
            for t in range(n):
                _piece_copy(modes[t], src_refs[t], land_refs[t], send_sems, recv_sems, t, k, me, pid, peer).start()
        for k in range(1, N_DEV):
            peer, pid = _peer(k)
            for t in range(n):
                _piece_copy(modes[t], src_refs[t], land_refs[t], send_sems, recv_sems, t, k, pid, me, peer).wait_recv()
        for k in range(1, N_DEV):
            peer, pid = _peer(k)
            for t in range(n):
                _piece_copy(modes[t], src_refs[t], land_refs[t], send_sems, recv_sems, t, k, me, pid, peer).wait_send()
        for cp in local:
            cp.wait()

    return pl.kernel(
        body, out_type=[_sds(s, a.dtype) for s, a in zip(land_shapes, srcs)],
        mesh=plsc.ScalarSubcoreMesh(axis_name="sequencer", num_cores=1),
        scratch_types=[pltpu.SemaphoreType.DMA((7 * n,)), pltpu.SemaphoreType.DMA((7 * n,)), pltpu.SemaphoreType.DMA((n,))],
        compiler_params=pltpu.CompilerParams(collective_id=collective_id), name=name)(*srcs, *extra)


def _sequencer_gather(srcs, name, collective_id, after=None):
    n = len(srcs)
    extra = [] if after is None else [after]

    def body(*refs):
        src_refs, land_refs = refs[:n], refs[n + len(extra):2 * n + len(extra)]
        send_sems, recv_sems, local_sems = refs[2 * n + len(extra):]
        x, y, c = lax.axis_index("x"), lax.axis_index("y"), lax.axis_index("c")
        sibling = (x, y, 1 - c)
        chips = [(1 - x, y), (x, 1 - y), (1 - x, 1 - y)]
        index = lambda px, py, pc: 4 * px + 2 * py + pc
        barrier = pltpu.get_barrier_semaphore()
        for peer in [sibling] + [(*chip, c) for chip in chips]:
            pl.semaphore_signal(barrier, inc=1, device_id=peer, device_id_type=MESH)
        pl.semaphore_wait(barrier, 4)

        def copy(t, k, slot, to, src=None):
            return pltpu.make_async_remote_copy(
                src_ref=land_refs[t].at[slot] if src is None else src, dst_ref=land_refs[t].at[slot],
                send_sem=send_sems.at[7 * t + k], recv_sem=recv_sems.at[7 * t + k], device_id=to, device_id_type=MESH)

        me = index(x, y, c)
        local = []
        for t in range(n):
            cp = pltpu.make_async_copy(src_refs[t], land_refs[t].at[me], local_sems.at[t])
            cp.start()
            local.append(cp)
        sends = []
        for t in range(n):
            sends.append(copy(t, 0, me, sibling, src=src_refs[t]))
            sends += [copy(t, 1 + j, me, (*chip, c), src=src_refs[t]) for j, chip in enumerate(chips)]
        for cp in sends:
            cp.start()
        for j, chip in enumerate(chips):
            for t in range(n):
                copy(t, 1 + j, index(*chip, c), sibling, src=src_refs[t]).wait_recv()
                passed = copy(t, 4 + j, index(*chip, c), sibling)
                passed.start()
                sends.append(passed)
        for t in range(n):
            copy(t, 0, index(x, y, 1 - c), sibling, src=src_refs[t]).wait_recv()
        for j, chip in enumerate(chips):
            for t in range(n):
                copy(t, 4 + j, index(*chip, 1 - c), sibling, src=src_refs[t]).wait_recv()
        for cp in sends:
            cp.wait_send()
        for cp in local:
            cp.wait()

    return pl.kernel(
        body, out_type=[_sds((N_DEV,) + a.shape, a.dtype) for a in srcs],
        mesh=plsc.ScalarSubcoreMesh(axis_name="sequencer", num_cores=1),
        scratch_types=[pltpu.SemaphoreType.DMA((7 * n,)), pltpu.SemaphoreType.DMA((7 * n,)), pltpu.SemaphoreType.DMA((n,))],
        compiler_params=pltpu.CompilerParams(collective_id=collective_id), name=name)(*srcs, *extra)


def _adamw(w, g, m, v):
    m = ADAM_B1 * m + (1.0 - ADAM_B1) * g
    v = ADAM_B2 * v + (1.0 - ADAM_B2) * (g * g)
    m_hat = m / (1.0 - ADAM_B1 ** ADAM_STEP)
    v_hat = v / (1.0 - ADAM_B2 ** ADAM_STEP)
    delta = -ADAM_LR * (m_hat / (jnp.sqrt(v_hat) + ADAM_EPS) + ADAM_WD * w)
    return delta, m, v


def _adam_big(w, parts, m, v, name, after=None):
    L, R, C = w.shape
    P = parts[0].shape[0]
    tr = _tile(R, (256, 128, 176, 64, 32, 16))
    nr = R // tr

    def body(w_ref, *refs):
        p_refs, (m_ref, v_ref, g_ref, d_ref, mo_ref, vo_ref) = refs[:L], refs[L:]
        for l in range(L):
            @pl.when(pl.program_id(0) == l)
            def _(p_ref=p_refs[l]):
                g = p_ref[0].astype(F32)
                for s in range(1, P):
                    g = g + p_ref[s].astype(F32)
                g_ref[...] = g
                d_ref[...], mo_ref[...], vo_ref[...] = _adamw(w_ref[...], g, m_ref[...], v_ref[...])

    row = pl.BlockSpec((None, tr, C), lambda l, i: (l, i, 0))
    park = lambda l_of: (lambda l, i: (0, jnp.where(l == l_of, i, 0 if l_of else nr - 1), 0))
    return _call(
        after, body, grid=(L, nr),
        in_specs=[row] + [pl.BlockSpec((P, tr, C), park(l)) for l in range(L)] + [row, row],
        out_specs=[row] * 4, out_shape=[_sds((L, R, C), F32)] * 4,
        compiler_params=_params(("arbitrary", "arbitrary")), name=name)(w, *parts, m, v)


SMALL = (("a_lower_bound", (2, 128), (2, D)), ("ln_gain", (3, 2, 128), (6, D)), ("ln_bias", (3, 2, 128), (6, D)),
         ("a_norm_gain", (1, 128), (1, 128)), ("kv_b", (1, 512), (1, 512)), ("b_b_q", (1, D), (1, D)),
         ("b_sinks", (1, ATT_QH), (1, 128)), ("b_b_out", (1, D), (1, D)), ("ple_b_gate", (2, D), (2, D)))


def _adam_small(parts, w, m, v, losses, after=None):
    k = len(SMALL)

    def body(*refs):
        p_refs, w_refs, m_refs, v_refs = refs[:k], refs[k:2 * k], refs[2 * k:3 * k], refs[3 * k:4 * k]
        loss_ref, outs, total_ref = refs[4 * k], refs[4 * k + 1:-1], refs[-1]
        total = loss_ref[0]
        for s in range(1, N_DEV):
            total = total + loss_ref[s]
        total_ref[...] = total
        me = _my_index()
        for i, (_, wshape, pshape) in enumerate(SMALL):
            cols = wshape[-1]
            lanes = slice(None) if cols == pshape[1] else (
                pl.ds(0, cols) if cols < 128 else pl.ds(pl.multiple_of(me * cols, cols), cols))
            at = [(slice(None), slice(None))] if len(wshape) == 2 else [
                (pl.ds(l * wshape[0] + kk, 1), (kk, pl.ds(l, 1), slice(None))) for kk in range(wshape[0]) for l in range(wshape[1])]
            for rows, own in at:
                g = p_refs[i][0, rows, lanes]
                for s in range(1, N_DEV):
                    g = g + p_refs[i][s, rows, lanes]
                g_ref, d_ref, mo_ref, vo_ref = outs[4 * i:4 * i + 4]
                g_ref[own] = g
                d_ref[own], mo_ref[own], vo_ref[own] = _adamw(w_refs[i][own], g, m_refs[i][own], v_refs[i][own])

    full = lambda shape: pl.BlockSpec(shape, lambda: (0,) * len(shape))
    names = [n for n, _, _ in SMALL]
    held = lambda a, ws: a.swapaxes(0, 1) if len(ws) == 3 else a.reshape(ws)
    back = lambda r, n: r.swapaxes(0, 1) if r.ndim == 3 else r.reshape(w[n].shape)
    res = _call(
        after, body,
        in_specs=[full((N_DEV,) + ps) for _, _, ps in SMALL] + [full(ws) for _, ws, _ in SMALL] * 3
        + [full((N_DEV, 1, 128))],
        out_specs=[full(ws) for _, ws, _ in SMALL for _ in range(4)] + [full((1, 128))],
        out_shape=[_sds(ws, F32) for _, ws, _ in SMALL for _ in range(4)] + [_sds((1, 128), F32)], name="adam_small")(
            *[parts[n] for n in names], *[held(a[n], ws) for a in (w, m, v) for n, ws, _ in SMALL], losses)
    return {n: [back(r, n) for r in res[4 * i:4 * i + 4]] for i, n in enumerate(names)}, res[-1][0, 0]


WEIGHTS = ("a_w_in", "a_lower_bound", "a_norm_gain", "a_w_out", "kv_w", "kv_b", "b_w_q", "b_b_q", "b_sinks", "b_w_out",
           "b_b_out", "ffn_w_gate_up", "ffn_w_down", "ple_w_up", "ple_w_gate", "ple_b_gate", "ln_gain", "ln_bias")


GATHER_GROUPS = (("a_w_in",), ("a_w_out", "gu0"), ("dn0", "pu0", "pg0"), ("kv_w", "b_w_q", "b_w_out"), ("gu1",),
                 ("dn1", "pu1", "pg1"))
KERNEL_LAYOUT = {
    "a_w_in": lambda a: a,
    "a_w_out": lambda a: a.reshape(D, D),
    "kv_w": lambda a: a.reshape(D, 2 * ATT_KVH * ATT_HD),
    "b_w_q": lambda a: a.reshape(D, D),
    "b_w_out": lambda a: a.reshape(D, D),
    "gu": lambda a: a.reshape(2, 4, FFN_B, D),
    "dn": lambda a: a.reshape(4, FFN_B, D),
    "pu": lambda a: a,
    "pg": lambda a: a.reshape(D, D),
}
_row_blocks = lambda a: a.reshape(N_DEV, -1, a.shape[-1])
OWNER_BLOCKS = {
    "a_w_in": lambda g: g,
    "a_w_out": _row_blocks,
    "kv_w": _row_blocks,
    "b_w_q": _row_blocks,
    "b_w_out": _row_blocks,
    "gu": lambda g: g,
    "dn": lambda g: _row_blocks(g.reshape(FFN_H, D)),
    "pu": lambda g: g.reshape(PLE_DIM, N_DEV, 128).transpose(1, 0, 2),
    "pg": _row_blocks,
}
ADAM_PARTS = (("kv_w", ("kv_w",)), ("b_w_q", ("b_w_q",)), ("b_w_out", ("b_w_out",)), ("ffn_w_gate_up", ("gu0", "gu1")),
              ("ffn_w_down", ("dn0", "dn1")), ("ple_w_up", ("pu0", "pu1")), ("ple_w_gate", ("pg0", "pg1")),
              ("a_w_out", ("a_w_out",)), ("a_w_in", ("a_w_in",)))


def kernel(x, p, a_w_in, a_lower_bound, a_norm_gain, a_w_out, kv_w, kv_b, b_w_q, b_b_q, b_sinks, b_w_out, b_b_out, ffn_w_gate_up, ffn_w_down, ple_w_up, ple_w_gate, ple_b_gate, ln_gain, ln_bias, loss_target, m_a_w_in, m_a_lower_bound, m_a_norm_gain, m_a_w_out, m_kv_w, m_kv_b, m_b_w_q, m_b_b_q, m_b_sinks, m_b_w_out, m_b_b_out, m_ffn_w_gate_up, m_ffn_w_down, m_ple_w_up, m_ple_w_gate, m_ple_b_gate, m_ln_gain, m_ln_bias, v_a_w_in, v_a_lower_bound, v_a_norm_gain, v_a_w_out, v_kv_w, v_kv_b, v_b_w_q, v_b_b_q, v_b_sinks, v_b_w_out, v_b_b_out, v_ffn_w_gate_up, v_ffn_w_down, v_ple_w_up, v_ple_w_gate, v_ple_b_gate, v_ln_gain, v_ln_bias):
    given = dict(locals())
    w = {n: given[n] for n in WEIGHTS}
    m = {n: given["m_" + n] for n in WEIGHTS}
    v = {n: given["v_" + n] for n in WEIGHTS}
    shards = {"a_w_in": a_w_in[0], "a_w_out": a_w_out[0], "kv_w": kv_w, "b_w_q": b_w_q[0], "b_w_out": b_w_out[0]}
    for l in range(2):
        shards.update({f"gu{l}": ffn_w_gate_up[l].T, f"dn{l}": ffn_w_down[l], f"pu{l}": ple_w_up[l], f"pg{l}": ple_w_gate[l]})
    sharded_small = [a_lower_bound, ln_gain.swapaxes(0, 1), ln_bias.swapaxes(0, 1)]
    gathered = {}
    for gi, g in enumerate(GATHER_GROUPS):
        lands = _sequencer_gather([shards[n].astype(CDT) for n in g] + (sharded_small if gi == 0 else []),
                                  f"gather{gi}", gi)
        for n, a in zip(g, lands):
            gathered[n] = KERNEL_LAYOUT[n.rstrip("01")](a)
        if gi == 0:
            alb = lands[len(g)].transpose(1, 0, 2).reshape(2, D)
            lng, lnb = [a.transpose(2, 1, 0, 3).reshape(2, 3, D) for a in lands[len(g) + 1:]]

    getw = gathered.__getitem__

    sm = {"a_lower_bound": alb, "ln_gain": lng, "ln_bias": lnb,
          "a_norm_gain": a_norm_gain, "kv_b": kv_b, "b_b_q": b_b_q[0], "b_sinks": b_sinks, "b_b_out": b_b_out,
          "ple_b_gate": ple_b_gate}

    scatters, small_parts = [], {}

    def emit(grads, small=None):
        names = list(grads)
        blocks = [OWNER_BLOCKS[n.rstrip("01")](grads[n]) for n in names]
        partials = [] if small is None else [small[n].reshape(ps) for n, _, ps in SMALL] + [small["loss"]]
        lands = _sequencer_exchange(blocks + partials, ["scatter"] * len(blocks) + ["gather"] * len(partials),
                                    f"scatter{len(scatters)}", len(GATHER_GROUPS) + len(scatters))
        scatters.append(dict(zip(names, lands)))
        small_parts.update(zip([n for n, _, _ in SMALL] + ["loss"], lands[len(blocks):]))
        return blocks + (list(scatters[-4].values()) if len(scatters) >= 4 else [])

    loss, grad_x, gs = _local_step(x[0], p[:, 0], loss_target[0], getw, sm, emit)

    out, parts, last = {}, {}, [grad_x]
    for landed in scatters:
        parts.update(landed)
        for n, keys in ADAM_PARTS:
            if n in out or not all(key in parts for key in keys):
                continue
            lrc = (1,) * (3 - w[n].ndim) + w[n].shape
            shard = (lambda a: a.reshape(lrc).swapaxes(1, 2)) if n == "ffn_w_gate_up" else (lambda a: a.reshape(lrc))
            res = _adam_big(shard(w[n]), [parts[key] for key in keys], shard(m[n]), shard(v[n]), "adam_" + n, after=last)
            out[n] = [(r.swapaxes(1, 2) if n == "ffn_w_gate_up" else r).reshape(w[n].shape) for r in res]
            last = [res[3]]
    small_out, loss = _adam_small(small_parts, w, m, v, small_parts["loss"], after=last)
    out.update(small_out)
    res = [loss, grad_x[None]]
    for i in range(4):
        res += [out[n][i] for n in WEIGHTS]
    return tuple(res)
```

```python
import jax
import jax.numpy as jnp
from jax import lax
from jax.experimental import pallas as pl
from jax.experimental.pallas import tpu as pltpu
from jax.experimental.pallas import tpu_sc as plsc

F32 = jnp.float32
CDT = jnp.bfloat16

N_DEV = 8
D = 1024
HG_H, HG_DK, HG_CH = 8, 128, 64
HG_HPB = 4
HG_CPB = 8
ATT_HD, ATT_QH, ATT_KVH, ATT_G, WINDOW = 64, 16, 4, 4, 128
ATT_BPB = 1
FFN_H = 2816
FFN_B = FFN_H // 4
PLE_DIM = 256
ALPHA = (2.0 * 2) ** 0.25
LN_EPS = 1e-5
RMS_EPS = 1e-6
ADAM_LR, ADAM_B1, ADAM_B2, ADAM_EPS, ADAM_WD, ADAM_STEP = 0.001, 0.9, 0.999, 1e-08, 0.01, 10
ROW_TILES = (512, 256, 128, 64)
VMEM_LIMIT = 48 * 1024 * 1024
NEG = -1e30

MESH = pl.DeviceIdType.MESH


def _tile(n, cands=ROW_TILES):
    for t in cands:
        if n % t == 0:
            return t
    return n


def _sds(shape, dtype):
    return jax.ShapeDtypeStruct(tuple(shape), dtype)


def _params(sem):
    return pltpu.CompilerParams(dimension_semantics=sem, vmem_limit_bytes=VMEM_LIMIT)


def _dot(a, b):
    return jnp.dot(a.astype(CDT), b.astype(CDT), preferred_element_type=F32)


def _dot_nt(a, b):
    return lax.dot_general(a.astype(CDT), b.astype(CDT), (((1,), (1,)), ((), ())), preferred_element_type=F32)


def _dot_tn(a, b):
    return lax.dot_general(a.astype(CDT), b.astype(CDT), (((0,), (0,)), ((), ())), preferred_element_type=F32)


def _sigmoid(x):
    return jax.nn.sigmoid(x)


def _ln_fwd(z, g, b):
    mu = jnp.mean(z, axis=-1, keepdims=True)
    zc = z - mu
    var = jnp.mean(zc * zc, axis=-1, keepdims=True)
    return zc * lax.rsqrt(var + LN_EPS) * g + b


def _ln_bwd(z, g, dy):
    mu = jnp.mean(z, axis=-1, keepdims=True)
    zc = z - mu
    var = jnp.mean(zc * zc, axis=-1, keepdims=True)
    rstd = lax.rsqrt(var + LN_EPS)
    xhat = zc * rstd
    dxh = dy * g
    dz = rstd * (dxh - jnp.mean(dxh, axis=-1, keepdims=True) - xhat * jnp.mean(dxh * xhat, axis=-1, keepdims=True))
    return dz, xhat


def _colsum(x):
    return jnp.sum(x, axis=0, keepdims=True)


def _acc(ref, val, first):
    @pl.when(first)
    def _():
        ref[...] = val

    @pl.when(jnp.logical_not(first))
    def _():
        ref[...] += val


def _zero_at(ref, first):
    @pl.when(first)
    def _():
        ref[...] = jnp.zeros_like(ref)


def _call(after, body, **kw):
    after = [] if after is None else list(after)
    specs = list(kw["in_specs"])
    kw["in_specs"] = [pl.BlockSpec(memory_space=pl.ANY)] * len(after) + specs

    def ordered_body(*refs):
        body(*refs[len(after):])

    call = pl.pallas_call(ordered_body, **kw)
    return lambda *args: call(*after, *args)


def _mm_tn(a3, b3, G, amap, bmap, ablock, bblock, out_shape, oblock, omap, name="mm_tn"):
    S = a3.shape[1]

    def body(a_ref, b_ref, o_ref):
        o_ref[...] = _dot_tn(a_ref[...], b_ref[...]).astype(o_ref.dtype)

    return _call(
        None, body, grid=(G, 1),
        in_specs=[pl.BlockSpec(ablock(S), amap), pl.BlockSpec(bblock(S), bmap)],
        out_specs=pl.BlockSpec(oblock, omap), out_shape=_sds(out_shape, CDT),
        compiler_params=_params(("arbitrary", "arbitrary")), name=name)(a3, b3)


def _wgrad(a3, b3, name):
    Ga, S, M = a3.shape
    Gb, _, N = b3.shape
    G = max(Ga, Gb)
    return _mm_tn(
        a3, b3, G,
        (lambda g, k: (g, k, 0)) if Ga > 1 else (lambda g, k: (0, k, 0)),
        (lambda g, k: (g, k, 0)) if Gb > 1 else (lambda g, k: (0, k, 0)),
        lambda tk: (None, tk, M), lambda tk: (None, tk, N),
        (G, M, N), (None, M, N), lambda g, k: (g, 0, 0), name=name)


def _mixout_ln(u3, w3, bias, xin, gain, beta, name):
    G, S, Kb = u3.shape
    tm = _tile(S)

    def body(u_ref, w_ref, b_ref, x_ref, g_ref, be_ref, z_ref, xo_ref, xob_ref):
        h = b_ref[...] + _dot(u_ref[0], w_ref[0])
        for g in range(1, G):
            h = h + _dot(u_ref[g], w_ref[g])
        z = ALPHA * x_ref[...] + h
        z_ref[...] = z
        y = _ln_fwd(z, g_ref[...], be_ref[...])
        xo_ref[...] = y
        xob_ref[...] = y.astype(CDT)

    row = pl.BlockSpec((tm, D), lambda i: (i, 0))
    vec = pl.BlockSpec((1, D), lambda i: (0, 0))
    return _call(
        None, body, grid=(S // tm,),
        in_specs=[pl.BlockSpec((G, tm, Kb), lambda i: (0, i, 0)), pl.BlockSpec((G, Kb, D), lambda i: (0, 0, 0)),
                  vec, row, vec, vec],
        out_specs=[row, row, row], out_shape=[_sds((S, D), F32), _sds((S, D), F32), _sds((S, D), CDT)],
        compiler_params=_params(("arbitrary",)), name=name)(u3, w3, bias, xin, gain, beta)


def _ffn_fwd(xin, xin_b, wgu, wdn, gain, beta, name):
    S = xin.shape[0]
    tm = _tile(S)

    def hidden(xb_ref, wgu_ref, gu_ref, hid_ref):
        xb = xb_ref[...]
        gate = _dot_nt(xb, wgu_ref[0])
        up = _dot_nt(xb, wgu_ref[1])
        gu_ref[0] = gate.astype(CDT)
        gu_ref[1] = up.astype(CDT)
        hid_ref[...] = (gate * _sigmoid(gate) * up).astype(CDT)

    gu, hid = _call(
        None, hidden, grid=(4, S // tm),
        in_specs=[pl.BlockSpec((tm, D), lambda j, i: (i, 0)), pl.BlockSpec((2, None, FFN_B, D), lambda j, i: (0, j, 0, 0))],
        out_specs=[pl.BlockSpec((2, None, tm, FFN_B), lambda j, i: (0, j, i, 0)),
                   pl.BlockSpec((None, tm, FFN_B), lambda j, i: (j, i, 0))],
        out_shape=[_sds((2, 4, S, FFN_B), CDT), _sds((4, S, FFN_B), CDT)],
        compiler_params=_params(("arbitrary", "arbitrary")), name=name + "_hidden")(xin_b, wgu)

    def down(x_ref, hid_ref, wdn_ref, g_ref, be_ref, z_ref, xo_ref, xob_ref):
        z = ALPHA * x_ref[...]
        for j in range(4):
            z = z + _dot(hid_ref[j], wdn_ref[j])
        z_ref[...] = z
        y = _ln_fwd(z, g_ref[...], be_ref[...])
        xo_ref[...] = y
        xob_ref[...] = y.astype(CDT)

    row = pl.BlockSpec((tm, D), lambda i: (i, 0))
    vec = pl.BlockSpec((1, D), lambda i: (0, 0))
    z, xo, xob = _call(
        None, down, grid=(S // tm,),
        in_specs=[row, pl.BlockSpec((4, tm, FFN_B), lambda i: (0, i, 0)), pl.BlockSpec((4, FFN_B, D), lambda i: (0, 0, 0)),
                  vec, vec],
        out_specs=[row, row, row], out_shape=[_sds((S, D), F32), _sds((S, D), F32), _sds((S, D), CDT)],
        compiler_params=_params(("arbitrary",)), name=name + "_down")(xin, hid, wdn, gain, beta)
    return gu, hid, z, xo, xob


def _ple_fwd(xin, xin_b, p_b, wpg, bgate, wpu, gain, beta, name, target=None):
    S = xin.shape[0]
    tm = _tile(S)

    def body(x_ref, xb_ref, p_ref, wpg_ref, bg_ref, wpu_ref, g_ref, be_ref, *rest):
        if target is None:
            sg_ref, up_ref, z_ref, xo_ref, xob_ref = rest
        else:
            t_ref, sg_ref, up_ref, z_ref, dy_ref, l_ref = rest
        sg =_sigmoid(_dot(xb_ref[...], wpg_ref[...]) + bg_ref[...])
        pb = p_ref[...]
        up = jnp.concatenate([_dot(pb, wpu_ref[j]) for j in range(N_DEV)], axis=-1)
        sg_ref[...] = sg.astype(CDT)
        up_ref[...] = (up * sg * (1.0 - sg)).astype(CDT)
        z = ALPHA * x_ref[...] + sg * up
        z_ref[...] = z
        y = _ln_fwd(z, g_ref[...], be_ref[...])
        if target is None:
            xo_ref[...] = y
            xob_ref[...] = y.astype(CDT)
        else:
            e = y - t_ref[...]
            dy_ref[...] = e * (1.0 / D)
            part = 0.5 * jnp.sum(jnp.sum(e * e, axis=-1, keepdims=True) * (1.0 / D), axis=0, keepdims=True)
            _acc(l_ref, jnp.broadcast_to(part, l_ref.shape), pl.program_id(0) == 0)

    row = pl.BlockSpec((tm, D), lambda i: (i, 0))
    vec = pl.BlockSpec((1, D), lambda i: (0, 0))
    last = target is not None
    return _call(
        None, body, grid=(S // tm,),
        in_specs=[row, row, pl.BlockSpec((tm, PLE_DIM), lambda i: (i, 0)), pl.BlockSpec((D, D), lambda i: (0, 0)), vec,
                  pl.BlockSpec((N_DEV, PLE_DIM, D // N_DEV), lambda i: (0, 0, 0)), vec, vec] + [row] * last,
        out_specs=[row] * 4 + [pl.BlockSpec((1, 128), lambda i: (0, 0)) if last else row],
        out_shape=[_sds((S, D), CDT)] * 2 + [_sds((S, D), F32)] * 2 + [_sds((1, 128), F32) if last else _sds((S, D), CDT)],
        compiler_params=_params(("arbitrary",)), name=name)(
            xin, xin_b, p_b, wpg, bgate, wpu, gain, beta, *([target] if last else []))


def _ple_bwd(dy, z, sg, up, gain, wpg, z_ffn, gain_ffn, name, after=None):
    S = dy.shape[0]
    tm = _tile(S)

    def body(dy_ref, z_ref, sg_ref, up_ref, g_ref, wpg_ref, zf_ref, gf_ref, dzf_ref, dzfb_ref, dgl_ref, dup_ref,
             dgain_ref, dbeta_ref, dbg_ref, dgainf_ref, dbetaf_ref):
        first = pl.program_id(0) == 0
        dy_ = dy_ref[...]
        dz, xhat = _ln_bwd(z_ref[...], g_ref[...], dy_)
        dgl = dz * up_ref[...].astype(F32)
        dgl_ref[...] = dgl.astype(CDT)
        dup_ref[...] = (dz * sg_ref[...].astype(F32)).astype(CDT)
        dx = ALPHA * dz + _dot_nt(dgl, wpg_ref[...])
        dzf, xhatf = _ln_bwd(zf_ref[...], gf_ref[...], dx)
        dzf_ref[...] = dzf
        dzfb_ref[...] = dzf.astype(CDT)
        _acc(dgain_ref, _colsum(dy_ * xhat), first)
        _acc(dbeta_ref, _colsum(dy_), first)
        _acc(dbg_ref, _colsum(dgl), first)
        _acc(dgainf_ref, _colsum(dx * xhatf), first)
        _acc(dbetaf_ref, _colsum(dx), first)

    row = pl.BlockSpec((tm, D), lambda i: (i, 0))
    vec = pl.BlockSpec((1, D), lambda i: (0, 0))
    return _call(
        after, body, grid=(S // tm,),
        in_specs=[row, row, row, row, vec, pl.BlockSpec((D, D), lambda i: (0, 0)), row, vec],
        out_specs=[row, row, row, row] + [vec] * 5,
        out_shape=[_sds((S, D), F32)] + [_sds((S, D), CDT)] * 3 + [_sds((1, D), F32)] * 5,
        compiler_params=_params(("arbitrary",)), name=name)(dy, z, sg, up, gain, wpg, z_ffn, gain_ffn)


def _ffn_bwd_hidden(dzb, gu, wdn, name, after=None):
    S = dzb.shape[0]
    tm = _tile(S)

    def hidden(dzb_ref, gu_ref, wdn_ref, dgu_ref):
        dhid = _dot_nt(dzb_ref[...], wdn_ref[...])
        gate, up = gu_ref[0].astype(F32), gu_ref[1].astype(F32)
        sg = _sigmoid(gate)
        dgu_ref[0] = (dhid * up * (sg * (1.0 + gate * (1.0 - sg)))).astype(CDT)
        dgu_ref[1] = (dhid * (gate * sg)).astype(CDT)

    blocks = pl.BlockSpec((2, None, tm, FFN_B), lambda j, i: (0, j, i, 0))
    return _call(
        after, hidden, grid=(4, S // tm),
        in_specs=[pl.BlockSpec((tm, D), lambda j, i: (i, 0)), blocks, pl.BlockSpec((None, FFN_B, D), lambda j, i: (j, 0, 0))],
        out_specs=blocks, out_shape=_sds((2, 4, S, FFN_B), CDT),
        compiler_params=_params(("arbitrary", "arbitrary")), name=name + "_hidden")(dzb, gu, wdn)


def _ffn_bwd_input(dz, dgu, wgu, name, after=None):
    S = dz.shape[0]
    tm = _tile(S)

    def to_input(dz_ref, dgu_ref, wgu_ref, dx_ref):
        acc = ALPHA * dz_ref[...]
        for g in range(2):
            for j in range(4):
                acc = acc + _dot(dgu_ref[g, j], wgu_ref[g, j])
        dx_ref[...] = acc

    rows = pl.BlockSpec((tm, D), lambda i: (i, 0))
    return _call(
        after, to_input, grid=(S // tm,),
        in_specs=[rows, pl.BlockSpec((2, 4, tm, FFN_B), lambda i: (0, 0, i, 0)),
                  pl.BlockSpec((2, 4, FFN_B, D), lambda i: (0, 0, 0, 0))],
        out_specs=rows, out_shape=_sds((S, D), F32),
        compiler_params=_params(("arbitrary",)), name=name + "_input")(dz, dgu, wgu)


def _mixout_bwd(dy, z, gain, w3, du_dtype, name, after=None):
    S = dy.shape[0]
    G, Kb, _ = w3.shape
    tm = _tile(S)

    def body(dy_ref, z_ref, g_ref, w_ref, dz_ref, dzb_ref, du_ref, dgain_ref, dbeta_ref, dbias_ref):
        first = pl.program_id(0) == 0
        dy_ = dy_ref[...]
        dz, xhat = _ln_bwd(z_ref[...], g_ref[...], dy_)
        dz_ref[...] = dz
        dzb = dz.astype(CDT)
        dzb_ref[...] = dzb
        for g in range(G):
            du_ref[g] = _dot_nt(dzb, w_ref[g]).astype(du_ref.dtype)
        _acc(dgain_ref, _colsum(dy_ * xhat), first)
        _acc(dbeta_ref, _colsum(dy_), first)
        _acc(dbias_ref, _colsum(dz), first)

    row = pl.BlockSpec((tm, D), lambda i: (i, 0))
    vec = pl.BlockSpec((1, D), lambda i: (0, 0))
    return _call(
        after, body, grid=(S // tm,), in_specs=[row, row, vec, pl.BlockSpec((G, Kb, D), lambda i: (0, 0, 0))],
        out_specs=[row, row, pl.BlockSpec((G, tm, Kb), lambda i: (0, i, 0)), vec, vec, vec],
        out_shape=[_sds((S, D), F32), _sds((S, D), CDT), _sds((G, S, Kb), du_dtype)] + [_sds((1, D), F32)] * 3,
        compiler_params=_params(("arbitrary",)), name=name)(dy, z, gain, w3)


def _half_select(low):
    r = lax.broadcasted_iota(jnp.int32, (2 * ATT_HD, ATT_HD), 0)
    c = lax.broadcasted_iota(jnp.int32, (2 * ATT_HD, ATT_HD), 1)
    return (r == c + (0 if low else ATT_HD)).astype(CDT)


def _half_place(low):
    r = lax.broadcasted_iota(jnp.int32, (ATT_HD, 2 * ATT_HD), 0)
    c = lax.broadcasted_iota(jnp.int32, (ATT_HD, 2 * ATT_HD), 1)
    return (c == r + (0 if low else ATT_HD)).astype(CDT)


def _pair_lanes(even, odd):
    return (jnp.dot(even, _half_place(True), preferred_element_type=F32)
            + jnp.dot(odd, _half_place(False), preferred_element_type=F32)).astype(CDT)


def _proj_heads(a, w, bias, heads, name):
    S, K = a.shape
    N = heads * ATT_HD
    tm = _tile(S)

    def body(a_ref, w_ref, b_ref, o_ref):
        acc = (_dot(a_ref[...], w_ref[...]) + b_ref[...]).astype(CDT)
        sel = (_half_select(True), _half_select(False))
        for h in range(heads):
            pair = acc[:, (h // 2) * 2 * ATT_HD:(h // 2 + 1) * 2 * ATT_HD]
            o_ref[h] = jnp.dot(pair, sel[h % 2], preferred_element_type=F32).astype(CDT)

    return _call(
        None, body, grid=(S // tm,),
        in_specs=[pl.BlockSpec((tm, K), lambda i: (i, 0)), pl.BlockSpec((K, N), lambda i: (0, 0)),
                  pl.BlockSpec((1, N), lambda i: (0, 0))],
        out_specs=pl.BlockSpec((heads, tm, ATT_HD), lambda i: (0, i, 0)), out_shape=_sds((heads, S, ATT_HD), CDT),
        compiler_params=_params(("arbitrary",)), name=name)(a, w, bias)


def _qkv_bwd(dz, dq, dkv4, wq, wkv, name, after=None):
    S = dz.shape[0]
    tm = _tile(S)
    HK = dkv4.shape[0]
    NK = HK * ATT_HD

    def body(dz_ref, dq_ref, dkv_ref, wq_ref, wkv_ref, dx_ref, dkvn_ref, dkvb_ref):
        first = pl.program_id(0) == 0
        dkvn = jnp.concatenate([_pair_lanes(dkv_ref[2 * i].astype(CDT), dkv_ref[2 * i + 1].astype(CDT))
                                for i in range(HK // 2)], axis=-1)
        dkvn_ref[...] = dkvn
        dx_ref[...] = ALPHA * dz_ref[...] + _dot_nt(dq_ref[...], wq_ref[...]) + _dot_nt(dkvn, wkv_ref[...])
        for h in range(HK):
            _acc(dkvb_ref.at[h], _colsum(dkv_ref[h]), first)

    row = pl.BlockSpec((tm, D), lambda i: (i, 0))
    return _call(
        after, body, grid=(S // tm,),
        in_specs=[row, row, pl.BlockSpec((HK, tm, ATT_HD), lambda i: (0, i, 0)),
                  pl.BlockSpec((D, D), lambda i: (0, 0)), pl.BlockSpec((D, NK), lambda i: (0, 0))],
        out_specs=[row, pl.BlockSpec((tm, NK), lambda i: (i, 0)), pl.BlockSpec((HK, 1, ATT_HD), lambda i: (0, 0, 0))],
        out_shape=[_sds((S, D), F32), _sds((S, NK), CDT), _sds((HK, 1, ATT_HD), F32)],
        compiler_params=_params(("arbitrary",)), name=name)(dz, dq, dkv4, wq, wkv)


def _inproj_fwd(xb, wain, name):
    S = xb.shape[0]
    tm = _tile(S) // 2
    nb = wain.shape[-1]

    def body(x_ref, w_ref, o_ref):
        x = x_ref[...]
        for j in range(N_DEV):
            o_ref[j // 2, :, pl.ds((j % 2) * nb, nb)] = _dot(x, w_ref[j])

    return _call(
        None, body, grid=(S // tm,),
        in_specs=[pl.BlockSpec((tm, D), lambda i: (i, 0)), pl.BlockSpec((N_DEV, D, nb), lambda i: (0, 0, 0))],
        out_specs=pl.BlockSpec((4, tm, D), lambda i: (0, i, 0)), out_shape=_sds((4, S, D), F32),
        compiler_params=_params(("arbitrary",)), name=name)(xb, wain)


def _inproj_bwd(dz, dproj, wain, name, after=None):
    S = dz.shape[0]
    tm = _tile(S)
    nb = wain.shape[-1]

    def body(dz_ref, dp_ref, w_ref, dx_ref):
        acc = ALPHA * dz_ref[...]
        for j in range(N_DEV):
            acc = acc + _dot_nt(dp_ref[j // 2, :, pl.ds((j % 2) * nb, nb)], w_ref[j])
        dx_ref[...] = acc

    row = pl.BlockSpec((tm, D), lambda i: (i, 0))
    return _call(
        after, body, grid=(S // tm,),
        in_specs=[row, pl.BlockSpec((4, tm, D), lambda i: (0, i, 0)), pl.BlockSpec((N_DEV, D, nb), lambda i: (0, 0, 0))],
        out_specs=row, out_shape=_sds((S, D), F32),
        compiler_params=_params(("arbitrary",)), name=name)(dz, dproj, wain)


def _running_sum(x, reverse=False):
    rows = x.shape[0]
    row = lax.broadcasted_iota(jnp.int32, x.shape, 0)
    step = 1
    while step < rows:
        if reverse:
            x = x + jnp.where(row < rows - step, pltpu.roll(x, rows - step, 0), 0.0)
        else:
            x = x + jnp.where(row >= step, pltpu.roll(x, step, 0), 0.0)
        step *= 2
    return x


def _hg_gates(q, f, alb_ref):
    a0, a1 = alb_ref[0:1, :], alb_ref[1:2, :]
    mx = jnp.maximum(a0, a1)
    e0, e1 = jnp.exp(a0 - mx), jnp.exp(a1 - mx)
    lb = e0 / (e0 + e1)
    sig = _sigmoid(f)
    forget = lb + (1.0 - lb) * sig
    k = (1.0 - lb) * _sigmoid(-f)
    qs = q * _sigmoid(q) * (HG_DK ** -0.5)
    return qs, k, jnp.log(forget), sig, lb, forget


def _hg_intra(qs, k, b, b_scr):
    b_scr[...] = b
    bm = b_scr[pl.ds(HG_CH // 2 - 1, 1), :]
    bl = b_scr[pl.ds(HG_CH - 1, 1), :]
    eb = jnp.exp(b)
    qb = qs * eb
    e_q = jnp.exp(b - bm)
    e_k = jnp.exp(bm - b)
    e_d = jnp.exp(bl - b)
    return qb, qs * e_q, k * e_k, k * e_d, jnp.exp(bl), eb, e_q, e_k, e_d


def _hgrn_fwd(proj, alb, ngain):
    S = proj.shape[1]
    nc = S // HG_CH
    nb = nc // HG_CPB
    rb, wb = HG_CPB * HG_CH, HG_HPB * HG_DK

    def body(pj_ref, alb_ref, ng_ref, o_ref, y_ref, st_ref, st_scr, b_scr):
        n = pl.program_id(1)

        @pl.when(n == 0)
        def _():
            st_scr[...] = jnp.zeros_like(st_scr)

        r = lax.broadcasted_iota(jnp.int32, (HG_CH, HG_CH), 0)
        c = lax.broadcasted_iota(jnp.int32, (HG_CH, HG_CH), 1)
        causal = r >= c
        for ci, j in [(ci, j) for ci in range(HG_CPB) for j in range(HG_HPB)]:
            rows, lanes = pl.ds(ci * HG_CH, HG_CH), pl.ds(j * HG_DK, HG_DK)
            q, f, v, g = pj_ref[0, rows, lanes], pj_ref[1, rows, lanes], pj_ref[2, rows, lanes], pj_ref[3, rows, lanes]
            qs, k, logf, _, _, _ = _hg_gates(q, f, alb_ref.at[:, lanes])
            b = _running_sum(logf)
            qb, qt, kt, kd, ebl, _, _, _, _ = _hg_intra(qs, k, b, b_scr.at[j, ci])
            st = st_scr[j]
            st_ref[j, ci] = st
            a = jnp.where(causal, _dot_nt(qt, kt), 0.0)
            o = _dot(a, v) + _dot_nt(qb, st)
            st_scr[j] = st * ebl + _dot_tn(v, kd)
            o_ref[rows, lanes] = o
            rinv = lax.rsqrt(jnp.mean(o * o, axis=-1, keepdims=True) + RMS_EPS)
            y_ref[rows, lanes] = (o * rinv * ng_ref[...] * (g * _sigmoid(g))).astype(CDT)

    blk = pl.BlockSpec((rb, wb), lambda h, n: (n, h))
    return _call(
        None, body, grid=(HG_H // HG_HPB, nb),
        in_specs=[pl.BlockSpec((4, rb, wb), lambda h, n: (0, n, h)), pl.BlockSpec((2, wb), lambda h, n: (0, h)),
                  pl.BlockSpec((1, HG_DK), lambda h, n: (0, 0))],
        out_specs=[blk, blk, pl.BlockSpec((HG_HPB, HG_CPB, HG_DK, HG_DK), lambda h, n: (h, n, 0, 0))],
        out_shape=[_sds((S, D), F32), _sds((S, D), CDT), _sds((HG_H, nc, HG_DK, HG_DK), F32)],
        scratch_shapes=[pltpu.VMEM((HG_HPB, HG_DK, HG_DK), F32), pltpu.VMEM((HG_HPB, HG_CPB, HG_CH, HG_DK), F32)],
        compiler_params=_params(("arbitrary", "arbitrary")), name="hgrn_fwd")(proj, alb, ngain)


def _hgrn_bwd(proj, alb, ngain, o, states, dy, after=None):
    S = proj.shape[1]
    nc = S // HG_CH
    nb = nc // HG_CPB
    rb, wb = HG_CPB * HG_CH, HG_HPB * HG_DK

    def body(pj_ref, alb_ref, ng_ref, o_ref, st_ref, dy_ref, dpj_ref, dalb_ref, dng_ref, dst_scr, b_scr):
        h, n = pl.program_id(0), pl.program_id(1)

        @pl.when(n == 0)
        def _():
            dst_scr[...] = jnp.zeros_like(dst_scr)
            dalb_ref[...] = jnp.zeros_like(dalb_ref)

        _zero_at(dng_ref, jnp.logical_and(h == 0, n == 0))
        ng = ng_ref[...]
        r = lax.broadcasted_iota(jnp.int32, (HG_CH, HG_CH), 0)
        c = lax.broadcasted_iota(jnp.int32, (HG_CH, HG_CH), 1)
        causal = r >= c
        dng = None
        for ci, j in [(ci, j) for ci in reversed(range(HG_CPB)) for j in range(HG_HPB)]:
            rows, lanes = pl.ds(ci * HG_CH, HG_CH), pl.ds(j * HG_DK, HG_DK)
            q, f, v, g = pj_ref[0, rows, lanes], pj_ref[1, rows, lanes], pj_ref[2, rows, lanes], pj_ref[3, rows, lanes]
            o_ = o_ref[rows, lanes]
            dy_ = dy_ref[rows, lanes]
            sg = _sigmoid(g)
            rinv = lax.rsqrt(jnp.mean(o_ * o_, axis=-1, keepdims=True) + RMS_EPS)
            nrm = o_ * rinv
            dr = dy_ * (g * sg)
            dg = dy_ * nrm * ng * (sg * (1.0 + g * (1.0 - sg)))
            dn = dr * ng
            do = rinv * (dn - nrm * jnp.mean(dn * nrm, axis=-1, keepdims=True))
            dng = _colsum(dr * nrm) if dng is None else dng + _colsum(dr * nrm)
            qs, k, logf, sig, lb, forget = _hg_gates(q, f, alb_ref.at[:, lanes])
            b = _running_sum(logf)
            qb, qt, kt, kd, ebl, eb, e_q, e_k, e_d = _hg_intra(qs, k, b, b_scr.at[j, ci])
            st = st_ref[j, ci]
            dstn = dst_scr[j]
            qt, kt, qb, kd = (t.astype(CDT).astype(F32) for t in (qt, kt, qb, kd))
            a = jnp.where(causal, _dot_nt(qt, kt), 0.0)
            da = jnp.where(causal, _dot_nt(do, v), 0.0)
            dv = _dot_tn(a, do) + _dot_nt(kd, dstn)
            dqb = _dot(do, st)
            dkd = _dot(v, dstn)
            dqt = _dot(da, kt)
            dkt = _dot_tn(da, qt)
            dbl = _colsum(dkd * kd) + ebl * _colsum(dstn * st)
            dst_scr[j] = dstn * ebl + _dot_tn(do, qb)
            dqs = dqt * e_q + dqb * eb
            dk = dkt * e_k + dkd * e_d
            db = dqt * qt + dqb * qb - dkt * kt - dkd * kd
            dlogf = _running_sum(db, reverse=True) + dbl
            dforget = dlogf / forget
            dsig = (1.0 - lb) * (dforget - dk)
            df = dsig * sig * (1.0 - sig)
            dlb = _colsum((dforget - dk) * (1.0 - sig))
            sq = _sigmoid(q)
            dq = dqs * (HG_DK ** -0.5) * (sq * (1.0 + q * (1.0 - sq)))
            dpj_ref[0, rows, lanes] = dq.astype(CDT)
            dpj_ref[1, rows, lanes] = df.astype(CDT)
            dpj_ref[2, rows, lanes] = dv.astype(CDT)
            dpj_ref[3, rows, lanes] = dg.astype(CDT)
            da0 = dlb * lb * (1.0 - lb)
            dalb_ref[pl.ds(0, 1), lanes] += da0
            dalb_ref[pl.ds(1, 1), lanes] -= da0
        dng_ref[...] += dng

    blk = pl.BlockSpec((rb, wb), lambda h, n: (nb - 1 - n, h))
    pj = pl.BlockSpec((4, rb, wb), lambda h, n: (0, nb - 1 - n, h))
    alb_blk = pl.BlockSpec((2, wb), lambda h, n: (0, h))
    ng_blk = pl.BlockSpec((1, HG_DK), lambda h, n: (0, 0))
    return _call(
        after, body, grid=(HG_H // HG_HPB, nb),
        in_specs=[pj, alb_blk, ng_blk, blk,
                  pl.BlockSpec((HG_HPB, HG_CPB, HG_DK, HG_DK), lambda h, n: (h, nb - 1 - n, 0, 0)), blk],
        out_specs=[pj, alb_blk, ng_blk],
        out_shape=[_sds((4, S, D), CDT), _sds((2, D), F32), _sds((1, HG_DK), F32)],
        scratch_shapes=[pltpu.VMEM((HG_HPB, HG_DK, HG_DK), F32), pltpu.VMEM((HG_HPB, HG_CPB, HG_CH, HG_DK), F32)],
        compiler_params=_params(("arbitrary", "arbitrary")), name="hgrn_bwd")(proj, alb, ngain, o, states, dy)


def _slope(h):
    return 2.0 ** (-8.0 * (h + 1) / ATT_QH)


def _attn_mask(n):
    qi = lax.broadcasted_iota(jnp.int32, (WINDOW, 2 * WINDOW), 0)
    si = lax.broadcasted_iota(jnp.int32, (WINDOW, 2 * WINDOW), 1)
    dist = qi - si + WINDOW
    valid = (dist >= 0) & (dist < WINDOW) & (n * WINDOW - WINDOW + si >= 0)
    return valid, dist.astype(F32)


def _attn_probs(qk, sink, slope, valid, distf):
    s = qk * (ATT_HD ** -0.5) - slope * distf
    s = jnp.where(valid, s, NEG)
    m = jnp.maximum(jnp.max(s, axis=-1, keepdims=True), sink)
    e = jnp.exp(s - m)
    es = jnp.exp(sink - m)
    inv = 1.0 / (jnp.sum(e, axis=-1, keepdims=True) + es)
    return e * inv, es * inv


def _attn_specs(S):
    steps = S // (ATT_BPB * WINDOW)
    cur = lambda H: pl.BlockSpec((H, ATT_BPB * WINDOW, ATT_HD), lambda n: (0, n, 0))
    prev = lambda H: pl.BlockSpec((H, WINDOW, ATT_HD), lambda n: (0, jnp.maximum(ATT_BPB * n - 1, 0), 0))
    return steps, cur, prev


def _attn_kv(kvc_ref, kvp_ref, head, bi):
    before = kvp_ref[head] if bi == 0 else kvc_ref[head, pl.ds((bi - 1) * WINDOW, WINDOW), :]
    return jnp.concatenate([before, kvc_ref[head, pl.ds(bi * WINDOW, WINDOW), :]], axis=0)


def _attn_fwd(q4, kv4, sinks):
    S = q4.shape[1]
    nb, cur, prev = _attn_specs(S)

    def body(sink_ref, q_ref, kvc_ref, kvp_ref, o_ref, p_ref, ps_ref):
        lane = lax.broadcasted_iota(jnp.int32, (1, 128), 1)
        place = (_half_place(True), _half_place(False))
        masks = [_attn_mask(pl.program_id(0) * ATT_BPB + bi) for bi in range(ATT_BPB)]
        sink_probs = [jnp.zeros((WINDOW, 128), F32) for _ in range(ATT_BPB)]
        heads = lambda kvh: range(kvh * ATT_G, (kvh + 1) * ATT_G)

        def scores(bi, kvh):
            rows = pl.ds(bi * WINDOW, WINDOW)
            kh = _attn_kv(kvc_ref, kvp_ref, kvh, bi)
            vh = _attn_kv(kvc_ref, kvp_ref, ATT_KVH + kvh, bi)
            v_pl = [jnp.dot(vh, m, preferred_element_type=F32).astype(CDT) for m in place]
            return v_pl, [_dot_nt(q_ref[h, rows, :], kh) for h in heads(kvh)]

        def softmax(bi, kvh, qks):
            rows = pl.ds(bi * WINDOW, WINDOW)
            probs = []
            for h, qk in zip(heads(kvh), qks):
                p, ps = _attn_probs(qk, sink_ref[0, h], _slope(h), *masks[bi])
                probs.append(p.astype(CDT))
                p_ref[h, rows, :] = probs[-1]
                sink_probs[bi] = sink_probs[bi] + jnp.where(lane == h, ps, 0.0)
            return probs

        def weighted_values(bi, kvh, v_pl, probs):
            rows = pl.ds(bi * WINDOW, WINDOW)
            for i in range(ATT_G // 2):
                lanes = pl.ds((kvh * ATT_G + 2 * i) * ATT_HD, 2 * ATT_HD)
                o_ref[rows, lanes] = (_dot(probs[2 * i], v_pl[0]) + _dot(probs[2 * i + 1], v_pl[1])).astype(CDT)

        groups = [(bi, kvh) for bi in range(ATT_BPB) for kvh in range(ATT_KVH)]
        ahead = scores(*groups[0])
        for gi, g in enumerate(groups):
            v_pl, qks = ahead
            probs = softmax(*g, qks)
            if gi + 1 < len(groups):
                ahead = scores(*groups[gi + 1])
            weighted_values(*g, v_pl, probs)
        for bi in range(ATT_BPB):
            ps_ref[pl.ds(bi * WINDOW, WINDOW), :] = sink_probs[bi]

    rows_spec = pl.BlockSpec((ATT_BPB * WINDOW, D), lambda n: (n, 0))
    return _call(
        None, body, grid=(nb,),
        in_specs=[pl.BlockSpec(memory_space=pltpu.SMEM), cur(ATT_QH), cur(2 * ATT_KVH), prev(2 * ATT_KVH)],
        out_specs=[rows_spec, pl.BlockSpec((ATT_QH, ATT_BPB * WINDOW, 2 * WINDOW), lambda n: (0, n, 0)),
                   pl.BlockSpec((ATT_BPB * WINDOW, 128), lambda n: (n, 0))],
        out_shape=[_sds((S, D), CDT), _sds((ATT_QH, S, 2 * WINDOW), CDT), _sds((S, 128), F32)],
        compiler_params=_params(("arbitrary",)), name="attn_fwd")(sinks, q4, kv4, kv4)


def _attn_bwd(q4, kv4, probs, sink_probs, do):
    S = q4.shape[1]
    nb, cur, prev = _attn_specs(S)

    def body(q_ref, kvc_ref, kvp_ref, p_ref, ps_ref, do_ref, dq_ref, dkv_ref, dbq_ref, dsink_ref):
        n = pl.program_id(0)
        first = n == 0

        @pl.when(first)
        def _():
            dkv_ref[...] = jnp.zeros_like(dkv_ref)
            dsink_ref[...] = jnp.zeros_like(dsink_ref)
            dbq_ref[...] = jnp.zeros_like(dbq_ref)

        lane = lax.broadcasted_iota(jnp.int32, (1, 128), 1)
        place = (_half_place(True), _half_place(False))
        row_dots = [jnp.zeros((WINDOW, 128), F32) for _ in range(ATT_BPB)]
        heads = lambda kvh: range(kvh * ATT_G, (kvh + 1) * ATT_G)
        pair_lanes = lambda h: pl.ds((h // 2) * 2 * ATT_HD, 2 * ATT_HD)

        def products_of_do(bi, kvh):
            rows = pl.ds(bi * WINDOW, WINDOW)
            kh = _attn_kv(kvc_ref, kvp_ref, kvh, bi)
            vh = _attn_kv(kvc_ref, kvp_ref, ATT_KVH + kvh, bi)
            k_pl = [jnp.dot(kh, m, preferred_element_type=F32).astype(CDT) for m in place]
            v_pl = [jnp.dot(vh, m, preferred_element_type=F32).astype(CDT) for m in place]
            dps = [_dot_nt(do_ref[rows, pair_lanes(h)], v_pl[h % 2]) for h in heads(kvh)]
            dv2 = [sum(_dot_tn(p_ref[h, rows, :], do_ref[rows, pair_lanes(h)]) for h in heads(kvh) if h % 2 == par)
                   for par in range(2)]
            return k_pl, dps, dv2

        def softmax_bwd(bi, kvh, dps):
            rows = pl.ds(bi * WINDOW, WINDOW)
            dss = []
            for h, dp in zip(heads(kvh), dps):
                p = p_ref[h, rows, :].astype(F32)
                dd = jnp.sum(p * dp, axis=-1, keepdims=True)
                dss.append((p * (dp - dd)).astype(CDT))
                row_dots[bi] = row_dots[bi] + jnp.where(lane == h, dd, 0.0)
            return dss

        def products_of_ds(bi, kvh, k_pl, dss, dv2):
            block = n * ATT_BPB + bi
            rows = pl.ds(bi * WINDOW, WINDOW)
            rows_cur = pl.ds(pl.multiple_of(block * WINDOW, WINDOW), WINDOW)
            rows_prev = pl.ds(pl.multiple_of(jnp.maximum(block - 1, 0) * WINDOW, WINDOW), WINDOW)
            for i in range(ATT_G // 2):
                h = kvh * ATT_G + 2 * i
                dq2 = (_dot(dss[2 * i], k_pl[0]) + _dot(dss[2 * i + 1], k_pl[1])) * (ATT_HD ** -0.5)
                dq_ref[rows, pair_lanes(h)] = dq2.astype(CDT)
                dbq_ref[:, pair_lanes(h)] += _colsum(dq2)
            dk = sum(_dot_tn(ds, q_ref[h, rows, :]) for h, ds in zip(heads(kvh), dss)) * (ATT_HD ** -0.5)
            dv = dv2[0][:, :ATT_HD] + pltpu.roll(dv2[1], ATT_HD, 1)[:, :ATT_HD]
            dkv_ref[kvh, rows_prev, :] += dk[:WINDOW]
            dkv_ref[kvh, rows_cur, :] += dk[WINDOW:]
            dkv_ref[ATT_KVH + kvh, rows_prev, :] += dv[:WINDOW]
            dkv_ref[ATT_KVH + kvh, rows_cur, :] += dv[WINDOW:]

        groups = [(bi, kvh) for bi in range(ATT_BPB) for kvh in range(ATT_KVH)]
        ahead = products_of_do(*groups[0])
        for gi, g in enumerate(groups):
            k_pl, dps, dv2 = ahead
            dss = softmax_bwd(*g, dps)
            if gi + 1 < len(groups):
                ahead = products_of_do(*groups[gi + 1])
            products_of_ds(*g, k_pl, dss, dv2)
        dsinks = jnp.zeros((1, 128), F32)
        for bi in range(ATT_BPB):
            dsinks = dsinks - _colsum(ps_ref[pl.ds(bi * WINDOW, WINDOW), :] * row_dots[bi])
        dsink_ref[...] += dsinks

    rows_spec = pl.BlockSpec((ATT_BPB * WINDOW, D), lambda n: (n, 0))
    return _call(
        None, body, grid=(nb,),
        in_specs=[cur(ATT_QH), cur(2 * ATT_KVH), prev(2 * ATT_KVH),
                  pl.BlockSpec((ATT_QH, ATT_BPB * WINDOW, 2 * WINDOW), lambda n: (0, n, 0)),
                  pl.BlockSpec((ATT_BPB * WINDOW, 128), lambda n: (n, 0)), rows_spec],
        out_specs=[rows_spec, pl.BlockSpec((2 * ATT_KVH, S, ATT_HD), lambda n: (0, 0, 0)),
                   pl.BlockSpec((1, D), lambda n: (0, 0)), pl.BlockSpec((1, 128), lambda n: (0, 0))],
        out_shape=[_sds((S, D), CDT), _sds((2 * ATT_KVH, S, ATT_HD), F32), _sds((1, D), F32),
                   _sds((1, 128), F32)],
        compiler_params=_params(("arbitrary",)), name="attn_bwd")(q4, kv4, kv4, probs, sink_probs, do)


def _local_step(x, p, target, getw, sm, emit):
    S = x.shape[0]
    vec = lambda a: a.reshape(1, -1)
    ln_g = lambda l, k: vec(sm["ln_gain"][l, k])
    ln_b = lambda l, k: vec(sm["ln_bias"][l, k])
    xb = x.astype(CDT)
    pb = p.astype(CDT)

    proj = _inproj_fwd(xb, getw("a_w_in"), "a_in")
    o_a, y_a, states = _hgrn_fwd(proj, sm["a_lower_bound"], sm["a_norm_gain"])
    zeros = jnp.zeros((1, D), F32)
    z = [[None] * 3 for _ in range(2)]
    xs = [[None] * 3 for _ in range(2)]
    xbs = [[None] * 3 for _ in range(2)]
    z[0][0], xs[0][0], xbs[0][0] = _mixout_ln(y_a[None], getw("a_w_out")[None], zeros, x, ln_g(0, 0), ln_b(0, 0),
                                              "a_out_ln")
    gu, hid, sgs, ups = [None, None], [None, None], [None, None], [None, None]

    def ffn_ple(l, target=None):
        wgu = getw(f"gu{l}")
        gu[l], hid[l], z[l][1], xs[l][1], xbs[l][1] = _ffn_fwd(
            xs[l][0], xbs[l][0], wgu, getw(f"dn{l}"), ln_g(l, 1), ln_b(l, 1), f"ffn_fwd{l}")
        sgs[l], ups[l], z[l][2], *out = _ple_fwd(
            xs[l][1], xbs[l][1], pb[l], getw(f"pg{l}"), vec(sm["ple_b_gate"][l]), getw(f"pu{l}"), ln_g(l, 2),
            ln_b(l, 2), f"ple_fwd{l}", target=target)
        if target is None:
            xs[l][2], xbs[l][2] = out
        return out

    ffn_ple(0)
    x3, x3b = xs[0][2], xbs[0][2]
    w_kv, w_q, w_bo = getw("kv_w"), getw("b_w_q"), getw("b_w_out")
    kv4 = _proj_heads(x3b, w_kv, vec(sm["kv_b"]), 2 * ATT_KVH, "kv_proj")
    q4 = _proj_heads(x3b, w_q, vec(sm["b_b_q"]), ATT_QH, "q_proj")
    o_b, probs, sink_probs = _attn_fwd(q4, kv4, sm["b_sinks"])
    z[1][0], xs[1][0], xbs[1][0] = _mixout_ln(o_b[None], w_bo[None], sm["b_b_out"], x3, ln_g(1, 0), ln_b(1, 0),
                                              "b_out_ln")
    dy, loss = ffn_ple(1, target)

    gs = {}
    d_ln_g = [[None] * 3 for _ in range(2)]
    d_ln_b = [[None] * 3 for _ in range(2)]
    g_bg = [None, None]

    def ffn_ple_bwd(l, dy, after=None):
        dz2, dzb, dgl, dup, d_ln_g[l][2], d_ln_b[l][2], g_bg[l], d_ln_g[l][1], d_ln_b[l][1] = _ple_bwd(
            dy, z[l][2], sgs[l], ups[l], ln_g(l, 2), getw(f"pg{l}"), z[l][1], ln_g(l, 1), f"ple_bwd{l}", after=after)
        g_pg = _wgrad(xbs[l][1][None], dgl[None], f"g_ple_gate{l}")[0]
        g_pu = _wgrad(pb[l][None], dup[None], f"g_ple_up{l}")[0]
        g_dn = _wgrad(hid[l], dzb[None], f"g_ffn_down{l}")
        if l == 1:
            tok = emit({f"pg{l}": g_pg, f"pu{l}": g_pu})
            tok = tok + emit({f"dn{l}": g_dn})
        else:
            tok = emit({f"pg{l}": g_pg, f"pu{l}": g_pu, f"dn{l}": g_dn})
        dgu = _ffn_bwd_hidden(dzb, gu[l], getw(f"dn{l}"), f"ffn_bwd{l}", after=tok)
        g_gu = _wgrad(dgu.reshape(8, S, FFN_B), xbs[l][0][None], f"g_ffn_gate_up{l}")
        tok = emit({f"gu{l}": g_gu})
        dx1 = _ffn_bwd_input(dz2, dgu, getw(f"gu{l}"), f"ffn_bwd{l}", after=tok)
        return dx1, None

    dx1, tok = ffn_ple_bwd(1, dy)
    dz, dzb, do, d_ln_g[1][0], d_ln_b[1][0], gs["b_b_out"] = _mixout_bwd(dx1, z[1][0], ln_g(1, 0), w_bo[None], CDT,
                                                                        "b_out_bwd", after=tok)
    g_bo = _wgrad(o_b[None], dzb[None], "g_b_w_out")[0]
    dq, dkv4, dbq, dsinks = _attn_bwd(q4, kv4, probs, sink_probs, do[0])
    gs["b_b_q"] = dbq
    gs["b_sinks"] = dsinks
    g_q = _wgrad(x3b[None], dq[None], "g_b_w_q")[0]
    dx3, dkv, gs["kv_b"] = _qkv_bwd(dz, dq, dkv4, w_q, w_kv, "qkv_bwd", after=tok)
    g_kv = _wgrad(x3b[None], dkv[None], "g_kv_w")[0]
    tok = emit({"b_w_out": g_bo, "b_w_q": g_q, "kv_w": g_kv})
    dx1, tok = ffn_ple_bwd(0, dx3, tok)
    w_ao = getw("a_w_out")
    dz, dzb, dyr, d_ln_g[0][0], d_ln_b[0][0], _ = _mixout_bwd(dx1, z[0][0], ln_g(0, 0), w_ao[None], F32, "a_out_bwd",
                                                              after=tok)
    g_ao = _wgrad(y_a[None], dzb[None], "g_a_w_out")[0]
    tok = emit({"a_w_out": g_ao})
    dproj, gs["a_lower_bound"], gs["a_norm_gain"] = _hgrn_bwd(proj, sm["a_lower_bound"], sm["a_norm_gain"], o_a, states,
                                                              dyr[0], after=tok)
    tk = lambda t: (None, t, D)
    g_ain = _mm_tn(xb[None], dproj, N_DEV, lambda g, k: (0, k, 0), lambda g, k: (g // 2, k, g % 2),
                   tk, lambda t: (None, t, 512), (N_DEV, D, 512), (None, D, 512), lambda g, k: (g, 0, 0), name="g_a_w_in")
    gs["ple_b_gate"] = jnp.concatenate(g_bg, axis=0)
    gs["ln_gain"] = jnp.stack([jnp.concatenate(r, axis=0) for r in d_ln_g])
    gs["ln_bias"] = jnp.stack([jnp.concatenate(r, axis=0) for r in d_ln_b])
    gs["loss"] = loss
    tok = emit({"a_w_in": g_ain}, small=gs)
    grad_x = _inproj_bwd(dz, dproj, getw("a_w_in"), "a_in_bwd", after=tok)
    return loss, grad_x, gs


def _peer(k):
    x, y, c = lax.axis_index("x"), lax.axis_index("y"), lax.axis_index("c")
    px = 1 - x if k & 4 else x
    py = 1 - y if k & 2 else y
    pc = 1 - c if k & 1 else c
    return (px, py, pc), 4 * px + 2 * py + pc


def _my_index():
    return 4 * lax.axis_index("x") + 2 * lax.axis_index("y") + lax.axis_index("c")


def _piece_copy(mode, src, land, send_sems, recv_sems, t, k, sender, receiver, peer):
    return pltpu.make_async_remote_copy(
        src_ref=src if mode == "gather" else src.at[receiver], dst_ref=land.at[sender],
        send_sem=send_sems.at[t * 7 + k - 1], recv_sem=recv_sems.at[t * 7 + k - 1], device_id=peer, device_id_type=MESH)


def _sequencer_exchange(srcs, modes, name, collective_id, after=None):
    n = len(srcs)
    land_shapes = [((N_DEV,) + a.shape) if mode == "gather" else a.shape for a, mode in zip(srcs, modes)]
    extra = [] if after is None else [after]

    def body(*refs):
        src_refs, land_refs = refs[:n], refs[n + len(extra):2 * n + len(extra)]
        send_sems, recv_sems, local_sems = refs[2 * n + len(extra):]
        barrier = pltpu.get_barrier_semaphore()
        for k in range(1, N_DEV):
            pl.semaphore_signal(barrier, inc=1, device_id=_peer(k)[0], device_id_type=MESH)
        pl.semaphore_wait(barrier, N_DEV - 1)
        me = _my_index()
        local = []
        for i in range(n):
            cp = pltpu.make_async_copy(src_refs[i] if modes[i] == "gather" else src_refs[i].at[me], land_refs[i].at[me],
                                       local_sems.at[i])
            cp.start()
            local.append(cp)
        for k in range(1, N_DEV):
            peer, pid = _peer(k)
            for t in range(n):
                _piece_copy(modes[t], src_refs[t], land_refs[t], send_sems, recv_sems, t, k, me, pid, peer).start()
        for k in range(1, N_DEV):
            peer, pid = _peer(k)
            for t in range(n):
                _piece_copy(modes[t], src_refs[t], land_refs[t], send_sems, recv_sems, t, k, pid, me, peer).wait_recv()
        for k in range(1, N_DEV):
            peer, pid = _peer(k)
            for t in range(n):
                _piece_copy(modes[t], src_refs[t], land_refs[t], send_sems, recv_sems, t, k, me, pid, peer).wait_send()
        for cp in local:
            cp.wait()

    return pl.kernel(
        body, out_type=[_sds(s, a.dtype) for s, a in zip(land_shapes, srcs)],
        mesh=plsc.ScalarSubcoreMesh(axis_name="sequencer", num_cores=1),
        scratch_types=[pltpu.SemaphoreType.DMA((7 * n,)), pltpu.SemaphoreType.DMA((7 * n,)), pltpu.SemaphoreType.DMA((n,))],
        compiler_params=pltpu.CompilerParams(collective_id=collective_id), name=name)(*srcs, *extra)


def _sequencer_gather(srcs, name, collective_id, after=None):
    n = len(srcs)
    extra = [] if after is None else [after]

    def body(*refs):
        src_refs, land_refs = refs[:n], refs[n + len(extra):2 * n + len(extra)]
        send_sems, recv_sems, local_sems = refs[2 * n + len(extra):]
        x, y, c = lax.axis_index("x"), lax.axis_index("y"), lax.axis_index("c")
        sibling = (x, y, 1 - c)
        chips = [(1 - x, y), (x, 1 - y), (1 - x, 1 - y)]
        index = lambda px, py, pc: 4 * px + 2 * py + pc
        barrier = pltpu.get_barrier_semaphore()
        for peer in [sibling] + [(*chip, c) for chip in chips]:
            pl.semaphore_signal(barrier, inc=1, device_id=peer, device_id_type=MESH)
        pl.semaphore_wait(barrier, 4)

        def copy(t, k, slot, to, src=None):
            return pltpu.make_async_remote_copy(
                src_ref=land_refs[t].at[slot] if src is None else src, dst_ref=land_refs[t].at[slot],
                send_sem=send_sems.at[7 * t + k], recv_sem=recv_sems.at[7 * t + k], device_id=to, device_id_type=MESH)

        me = index(x, y, c)
        local = []
        for t in range(n):
            cp = pltpu.make_async_copy(src_refs[t], land_refs[t].at[me], local_sems.at[t])
            cp.start()
            local.append(cp)
        sends = []
        for t in range(n):
            sends.append(copy(t, 0, me, sibling, src=src_refs[t]))
            sends += [copy(t, 1 + j, me, (*chip, c), src=src_refs[t]) for j, chip in enumerate(chips)]
        for cp in sends:
            cp.start()
        for j, chip in enumerate(chips):
            for t in range(n):
                copy(t, 1 + j, index(*chip, c), sibling, src=src_refs[t]).wait_recv()
                passed = copy(t, 4 + j, index(*chip, c), sibling)
                passed.start()
                sends.append(passed)
        for t in range(n):
            copy(t, 0, index(x, y, 1 - c), sibling, src=src_refs[t]).wait_recv()
        for j, chip in enumerate(chips):
            for t in range(n):
                copy(t, 4 + j, index(*chip, 1 - c), sibling, src=src_refs[t]).wait_recv()
        for cp in sends:
            cp.wait_send()
        for cp in local:
            cp.wait()

    return pl.kernel(
        body, out_type=[_sds((N_DEV,) + a.shape, a.dtype) for a in srcs],
        mesh=plsc.ScalarSubcoreMesh(axis_name="sequencer", num_cores=1),
        scratch_types=[pltpu.SemaphoreType.DMA((7 * n,)), pltpu.SemaphoreType.DMA((7 * n,)), pltpu.SemaphoreType.DMA((n,))],
        compiler_params=pltpu.CompilerParams(collective_id=collective_id), name=name)(*srcs, *extra)


def _adamw(w, g, m, v):
    m = ADAM_B1 * m + (1.0 - ADAM_B1) * g
    v = ADAM_B2 * v + (1.0 - ADAM_B2) * (g * g)
    m_hat = m / (1.0 - ADAM_B1 ** ADAM_STEP)
    v_hat = v / (1.0 - ADAM_B2 ** ADAM_STEP)
    delta = -ADAM_LR * (m_hat / (jnp.sqrt(v_hat) + ADAM_EPS) + ADAM_WD * w)
    return delta, m, v


def _adam_big(w, parts, m, v, name, after=None):
    L, R, C = w.shape
    P = parts[0].shape[0]
    tr = _tile(R, (256, 128, 176, 64, 32, 16))
    nr = R // tr

    def body(w_ref, *refs):
        p_refs, (m_ref, v_ref, g_ref, d_ref, mo_ref, vo_ref) = refs[:L], refs[L:]
        for l in range(L):
            @pl.when(pl.program_id(0) == l)
            def _(p_ref=p_refs[l]):
                g = p_ref[0].astype(F32)
                for s in range(1, P):
                    g = g + p_ref[s].astype(F32)
                g_ref[...] = g
                d_ref[...], mo_ref[...], vo_ref[...] = _adamw(w_ref[...], g, m_ref[...], v_ref[...])

    row = pl.BlockSpec((None, tr, C), lambda l, i: (l, i, 0))
    park = lambda l_of: (lambda l, i: (0, jnp.where(l == l_of, i, 0 if l_of else nr - 1), 0))
    return _call(
        after, body, grid=(L, nr),
        in_specs=[row] + [pl.BlockSpec((P, tr, C), park(l)) for l in range(L)] + [row, row],
        out_specs=[row] * 4, out_shape=[_sds((L, R, C), F32)] * 4,
        compiler_params=_params(("arbitrary", "arbitrary")), name=name)(w, *parts, m, v)


SMALL = (("a_lower_bound", (2, 128), (2, D)), ("ln_gain", (3, 2, 128), (6, D)), ("ln_bias", (3, 2, 128), (6, D)),
         ("a_norm_gain", (1, 128), (1, 128)), ("kv_b", (1, 512), (1, 512)), ("b_b_q", (1, D), (1, D)),
         ("b_sinks", (1, ATT_QH), (1, 128)), ("b_b_out", (1, D), (1, D)), ("ple_b_gate", (2, D), (2, D)))


def _adam_small(parts, w, m, v, losses, after=None):
    k = len(SMALL)

    def body(*refs):
        p_refs, w_refs, m_refs, v_refs = refs[:k], refs[k:2 * k], refs[2 * k:3 * k], refs[3 * k:4 * k]
        loss_ref, outs, total_ref = refs[4 * k], refs[4 * k + 1:-1], refs[-1]
        total = loss_ref[0]
        for s in range(1, N_DEV):
            total = total + loss_ref[s]
        total_ref[...] = total
        me = _my_index()
        for i, (_, wshape, pshape) in enumerate(SMALL):
            cols = wshape[-1]
            lanes = slice(None) if cols == pshape[1] else (
                pl.ds(0, cols) if cols < 128 else pl.ds(pl.multiple_of(me * cols, cols), cols))
            at = [(slice(None), slice(None))] if len(wshape) == 2 else [
                (pl.ds(l * wshape[0] + kk, 1), (kk, pl.ds(l, 1), slice(None))) for kk in range(wshape[0]) for l in range(wshape[1])]
            for rows, own in at:
                g = p_refs[i][0, rows, lanes]
                for s in range(1, N_DEV):
                    g = g + p_refs[i][s, rows, lanes]
                g_ref, d_ref, mo_ref, vo_ref = outs[4 * i:4 * i + 4]
                g_ref[own] = g
                d_ref[own], mo_ref[own], vo_ref[own] = _adamw(w_refs[i][own], g, m_refs[i][own], v_refs[i][own])

    full = lambda shape: pl.BlockSpec(shape, lambda: (0,) * len(shape))
    names = [n for n, _, _ in SMALL]
    held = lambda a, ws: a.swapaxes(0, 1) if len(ws) == 3 else a.reshape(ws)
    back = lambda r, n: r.swapaxes(0, 1) if r.ndim == 3 else r.reshape(w[n].shape)
    res = _call(
        after, body,
        in_specs=[full((N_DEV,) + ps) for _, _, ps in SMALL] + [full(ws) for _, ws, _ in SMALL] * 3
        + [full((N_DEV, 1, 128))],
        out_specs=[full(ws) for _, ws, _ in SMALL for _ in range(4)] + [full((1, 128))],
        out_shape=[_sds(ws, F32) for _, ws, _ in SMALL for _ in range(4)] + [_sds((1, 128), F32)], name="adam_small")(
            *[parts[n] for n in names], *[held(a[n], ws) for a in (w, m, v) for n, ws, _ in SMALL], losses)
    return {n: [back(r, n) for r in res[4 * i:4 * i + 4]] for i, n in enumerate(names)}, res[-1][0, 0]


WEIGHTS = ("a_w_in", "a_lower_bound", "a_norm_gain", "a_w_out", "kv_w", "kv_b", "b_w_q", "b_b_q", "b_sinks", "b_w_out",
           "b_b_out", "ffn_w_gate_up", "ffn_w_down", "ple_w_up", "ple_w_gate", "ple_b_gate", "ln_gain", "ln_bias")


GATHER_GROUPS = (("a_w_in", "a_w_out"), ("gu0",), ("dn0", "pu0", "pg0"), ("kv_w", "b_w_q", "b_w_out"), ("gu1",),
                 ("dn1", "pu1", "pg1"))
KERNEL_LAYOUT = {
    "a_w_in": lambda a: a,
    "a_w_out": lambda a: a.reshape(D, D),
    "kv_w": lambda a: a.reshape(D, 2 * ATT_KVH * ATT_HD),
    "b_w_q": lambda a: a.reshape(D, D),
    "b_w_out": lambda a: a.reshape(D, D),
    "gu": lambda a: a.reshape(2, 4, FFN_B, D),
    "dn": lambda a: a.reshape(4, FFN_B, D),
    "pu": lambda a: a,
    "pg": lambda a: a.reshape(D, D),
}
_row_blocks = lambda a: a.reshape(N_DEV, -1, a.shape[-1])
OWNER_BLOCKS = {
    "a_w_in": lambda g: g,
    "a_w_out": _row_blocks,
    "kv_w": _row_blocks,
    "b_w_q": _row_blocks,
    "b_w_out": _row_blocks,
    "gu": lambda g: g,
    "dn": lambda g: _row_blocks(g.reshape(FFN_H, D)),
    "pu": lambda g: g.reshape(PLE_DIM, N_DEV, 128).transpose(1, 0, 2),
    "pg": _row_blocks,
}
ADAM_PARTS = (("kv_w", ("kv_w",)), ("b_w_q", ("b_w_q",)), ("b_w_out", ("b_w_out",)), ("ffn_w_gate_up", ("gu0", "gu1")),
              ("ffn_w_down", ("dn0", "dn1")), ("ple_w_up", ("pu0", "pu1")), ("ple_w_gate", ("pg0", "pg1")),
              ("a_w_out", ("a_w_out",)), ("a_w_in", ("a_w_in",)))


def kernel(x, p, a_w_in, a_lower_bound, a_norm_gain, a_w_out, kv_w, kv_b, b_w_q, b_b_q, b_sinks, b_w_out, b_b_out, ffn_w_gate_up, ffn_w_down, ple_w_up, ple_w_gate, ple_b_gate, ln_gain, ln_bias, loss_target, m_a_w_in, m_a_lower_bound, m_a_norm_gain, m_a_w_out, m_kv_w, m_kv_b, m_b_w_q, m_b_b_q, m_b_sinks, m_b_w_out, m_b_b_out, m_ffn_w_gate_up, m_ffn_w_down, m_ple_w_up, m_ple_w_gate, m_ple_b_gate, m_ln_gain, m_ln_bias, v_a_w_in, v_a_lower_bound, v_a_norm_gain, v_a_w_out, v_kv_w, v_kv_b, v_b_w_q, v_b_b_q, v_b_sinks, v_b_w_out, v_b_b_out, v_ffn_w_gate_up, v_ffn_w_down, v_ple_w_up, v_ple_w_gate, v_ple_b_gate, v_ln_gain, v_ln_bias):
    given = dict(locals())
    w = {n: given[n] for n in WEIGHTS}
    m = {n: given["m_" + n] for n in WEIGHTS}
    v = {n: given["v_" + n] for n in WEIGHTS}
    shards = {"a_w_in": a_w_in[0], "a_w_out": a_w_out[0], "kv_w": kv_w, "b_w_q": b_w_q[0], "b_w_out": b_w_out[0]}
    for l in range(2):
        shards.update({f"gu{l}": ffn_w_gate_up[l].T, f"dn{l}": ffn_w_down[l], f"pu{l}": ple_w_up[l], f"pg{l}": ple_w_gate[l]})
    sharded_small = [a_lower_bound, ln_gain.swapaxes(0, 1), ln_bias.swapaxes(0, 1)]
    gathered = {}
    for gi, g in enumerate(GATHER_GROUPS):
        lands = _sequencer_gather([shards[n].astype(CDT) for n in g] + (sharded_small if gi == 0 else []),
                                  f"gather{gi}", gi)
        for n, a in zip(g, lands):
            gathered[n] = KERNEL_LAYOUT[n.rstrip("01")](a)
        if gi == 0:
            alb = lands[len(g)].transpose(1, 0, 2).reshape(2, D)
            lng, lnb = [a.transpose(2, 1, 0, 3).reshape(2, 3, D) for a in lands[len(g) + 1:]]

    getw = gathered.__getitem__

    sm = {"a_lower_bound": alb, "ln_gain": lng, "ln_bias": lnb,
          "a_norm_gain": a_norm_gain, "kv_b": kv_b, "b_b_q": b_b_q[0], "b_sinks": b_sinks, "b_b_out": b_b_out,
          "ple_b_gate": ple_b_gate}

    scatters, small_parts = [], {}

    def emit(grads, small=None):
        names = list(grads)
        blocks = [OWNER_BLOCKS[n.rstrip("01")](grads[n]) for n in names]
        partials = [] if small is None else [small[n].reshape(ps) for n, _, ps in SMALL] + [small["loss"]]
        lands = _sequencer_exchange(blocks + partials, ["scatter"] * len(blocks) + ["gather"] * len(partials),
                                    f"scatter{len(scatters)}", len(GATHER_GROUPS) + len(scatters))
        scatters.append(dict(zip(names, lands)))
        small_parts.update(zip([n for n, _, _ in SMALL] + ["loss"], lands[len(blocks):]))
        return blocks + (list(scatters[-4].values()) if len(scatters) >= 4 else [])

    loss, grad_x, gs = _local_step(x[0], p[:, 0], loss_target[0], getw, sm, emit)

    out, parts, last = {}, {}, [grad_x]
    for landed in scatters:
        parts.update(landed)
        for n, keys in ADAM_PARTS:
            if n in out or not all(key in parts for key in keys):
                continue
            lrc = (1,) * (3 - w[n].ndim) + w[n].shape
            shard = (lambda a: a.reshape(lrc).swapaxes(1, 2)) if n == "ffn_w_gate_up" else (lambda a: a.reshape(lrc))
            res = _adam_big(shard(w[n]), [parts[key] for key in keys], shard(m[n]), shard(v[n]), "adam_" + n, after=last)
            out[n] = [(r.swapaxes(1, 2) if n == "ffn_w_gate_up" else r).reshape(w[n].shape) for r in res]
            last = [res[3]]
    small_out, loss = _adam_small(small_parts, w, m, v, small_parts["loss"], after=last)
    out.update(small_out)
    res = [loss, grad_x[None]]
    for i in range(4):
        res += [out[n][i] for n in WEIGHTS]
    return tuple(res)
```

```python
import jax
import jax.numpy as jnp
from jax import lax
from jax.experimental import pallas as pl
from jax.experimental.pallas import tpu as pltpu
from jax.experimental.pallas import tpu_sc as plsc

F32 = jnp.float32
CDT = jnp.bfloat16

N_DEV = 8
D = 1024
HG_H, HG_DK, HG_CH = 8, 128, 64
HG_HPB = 4
HG_CPB = 8
ATT_HD, ATT_QH, ATT_KVH, ATT_G, WINDOW = 64, 16, 4, 4, 128
ATT_BPB = 1
FFN_H = 2816
FFN_B = FFN_H // 4
PLE_DIM = 256
ALPHA = (2.0 * 2) ** 0.25
LN_EPS = 1e-5
RMS_EPS = 1e-6
ADAM_LR, ADAM_B1, ADAM_B2, ADAM_EPS, ADAM_WD, ADAM_STEP = 0.001, 0.9, 0.999, 1e-08, 0.01, 10
ROW_TILES = (512, 256, 128, 64)
VMEM_LIMIT = 48 * 1024 * 1024
NEG = -1e30

MESH = pl.DeviceIdType.MESH


def _tile(n, cands=ROW_TILES):
    for t in cands:
        if n % t == 0:
            return t
    return n


def _sds(shape, dtype):
    return jax.ShapeDtypeStruct(tuple(shape), dtype)


def _params(sem):
    return pltpu.CompilerParams(dimension_semantics=sem, vmem_limit_bytes=VMEM_LIMIT)


def _dot(a, b):
    return jnp.dot(a.astype(CDT), b.astype(CDT), preferred_element_type=F32)


def _dot_nt(a, b):
    return lax.dot_general(a.astype(CDT), b.astype(CDT), (((1,), (1,)), ((), ())), preferred_element_type=F32)


def _dot_tn(a, b):
    return lax.dot_general(a.astype(CDT), b.astype(CDT), (((0,), (0,)), ((), ())), preferred_element_type=F32)


def _sigmoid(x):
    return jax.nn.sigmoid(x)


def _ln_fwd(z, g, b):
    mu = jnp.mean(z, axis=-1, keepdims=True)
    zc = z - mu
    var = jnp.mean(zc * zc, axis=-1, keepdims=True)
    return zc * lax.rsqrt(var + LN_EPS) * g + b


def _ln_bwd(z, g, dy):
    mu = jnp.mean(z, axis=-1, keepdims=True)
    zc = z - mu
    var = jnp.mean(zc * zc, axis=-1, keepdims=True)
    rstd = lax.rsqrt(var + LN_EPS)
    xhat = zc * rstd
    dxh = dy * g
    dz = rstd * (dxh - jnp.mean(dxh, axis=-1, keepdims=True) - xhat * jnp.mean(dxh * xhat, axis=-1, keepdims=True))
    return dz, xhat


def _colsum(x):
    return jnp.sum(x, axis=0, keepdims=True)


def _acc(ref, val, first):
    @pl.when(first)
    def _():
        ref[...] = val

    @pl.when(jnp.logical_not(first))
    def _():
        ref[...] += val


def _zero_at(ref, first):
    @pl.when(first)
    def _():
        ref[...] = jnp.zeros_like(ref)


def _call(after, body, **kw):
    after = [] if after is None else list(after)
    specs = list(kw["in_specs"])
    kw["in_specs"] = [pl.BlockSpec(memory_space=pl.ANY)] * len(after) + specs

    def ordered_body(*refs):
        body(*refs[len(after):])

    call = pl.pallas_call(ordered_body, **kw)
    return lambda *args: call(*after, *args)


def _mm_tn(a3, b3, G, amap, bmap, ablock, bblock, out_shape, oblock, omap, name="mm_tn"):
    S = a3.shape[1]

    def body(a_ref, b_ref, o_ref):
        o_ref[...] = _dot_tn(a_ref[...], b_ref[...]).astype(o_ref.dtype)

    return _call(
        None, body, grid=(G, 1),
        in_specs=[pl.BlockSpec(ablock(S), amap), pl.BlockSpec(bblock(S), bmap)],
        out_specs=pl.BlockSpec(oblock, omap), out_shape=_sds(out_shape, CDT),
        compiler_params=_params(("arbitrary", "arbitrary")), name=name)(a3, b3)


def _wgrad(a3, b3, name):
    Ga, S, M = a3.shape
    Gb, _, N = b3.shape
    G = max(Ga, Gb)
    return _mm_tn(
        a3, b3, G,
        (lambda g, k: (g, k, 0)) if Ga > 1 else (lambda g, k: (0, k, 0)),
        (lambda g, k: (g, k, 0)) if Gb > 1 else (lambda g, k: (0, k, 0)),
        lambda tk: (None, tk, M), lambda tk: (None, tk, N),
        (G, M, N), (None, M, N), lambda g, k: (g, 0, 0), name=name)


def _mixout_ln(u3, w3, bias, xin, gain, beta, name):
    G, S, Kb = u3.shape
    tm = _tile(S)

    def body(u_ref, w_ref, b_ref, x_ref, g_ref, be_ref, z_ref, xo_ref, xob_ref):
        h = b_ref[...] + _dot(u_ref[0], w_ref[0])
        for g in range(1, G):
            h = h + _dot(u_ref[g], w_ref[g])
        z = ALPHA * x_ref[...] + h
        z_ref[...] = z
        y = _ln_fwd(z, g_ref[...], be_ref[...])
        xo_ref[...] = y
        xob_ref[...] = y.astype(CDT)

    row = pl.BlockSpec((tm, D), lambda i: (i, 0))
    vec = pl.BlockSpec((1, D), lambda i: (0, 0))
    return _call(
        None, body, grid=(S // tm,),
        in_specs=[pl.BlockSpec((G, tm, Kb), lambda i: (0, i, 0)), pl.BlockSpec((G, Kb, D), lambda i: (0, 0, 0)),
                  vec, row, vec, vec],
        out_specs=[row, row, row], out_shape=[_sds((S, D), F32), _sds((S, D), F32), _sds((S, D), CDT)],
        compiler_params=_params(("arbitrary",)), name=name)(u3, w3, bias, xin, gain, beta)


def _ffn_fwd(xin, xin_b, wgu, wdn, gain, beta, name):
    S = xin.shape[0]
    tm = _tile(S)

    def hidden(xb_ref, wgu_ref, gu_ref, hid_ref):
        xb = xb_ref[...]
        gate = _dot_nt(xb, wgu_ref[0])
        up = _dot_nt(xb, wgu_ref[1])
        gu_ref[0] = gate.astype(CDT)
        gu_ref[1] = up.astype(CDT)
        hid_ref[...] = (gate * _sigmoid(gate) * up).astype(CDT)

    gu, hid = _call(
        None, hidden, grid=(4, S // tm),
        in_specs=[pl.BlockSpec((tm, D), lambda j, i: (i, 0)), pl.BlockSpec((2, None, FFN_B, D), lambda j, i: (0, j, 0, 0))],
        out_specs=[pl.BlockSpec((2, None, tm, FFN_B), lambda j, i: (0, j, i, 0)),
                   pl.BlockSpec((None, tm, FFN_B), lambda j, i: (j, i, 0))],
        out_shape=[_sds((2, 4, S, FFN_B), CDT), _sds((4, S, FFN_B), CDT)],
        compiler_params=_params(("arbitrary", "arbitrary")), name=name + "_hidden")(xin_b, wgu)

    def down(x_ref, hid_ref, wdn_ref, g_ref, be_ref, z_ref, xo_ref, xob_ref):
        z = ALPHA * x_ref[...]
        for j in range(4):
            z = z + _dot(hid_ref[j], wdn_ref[j])
        z_ref[...] = z
        y = _ln_fwd(z, g_ref[...], be_ref[...])
        xo_ref[...] = y
        xob_ref[...] = y.astype(CDT)

    row = pl.BlockSpec((tm, D), lambda i: (i, 0))
    vec = pl.BlockSpec((1, D), lambda i: (0, 0))
    z, xo, xob = _call(
        None, down, grid=(S // tm,),
        in_specs=[row, pl.BlockSpec((4, tm, FFN_B), lambda i: (0, i, 0)), pl.BlockSpec((4, FFN_B, D), lambda i: (0, 0, 0)),
                  vec, vec],
        out_specs=[row, row, row], out_shape=[_sds((S, D), F32), _sds((S, D), F32), _sds((S, D), CDT)],
        compiler_params=_params(("arbitrary",)), name=name + "_down")(xin, hid, wdn, gain, beta)
    return gu, hid, z, xo, xob


def _ple_fwd(xin, xin_b, p_b, wpg, bgate, wpu, gain, beta, name, target=None):
    S = xin.shape[0]
    tm = _tile(S)

    def body(x_ref, xb_ref, p_ref, wpg_ref, bg_ref, wpu_ref, g_ref, be_ref, *rest):
        if target is None:
            sg_ref, up_ref, z_ref, xo_ref, xob_ref = rest
        else:
            t_ref, sg_ref, up_ref, z_ref, dy_ref, l_ref = rest
        sg =_sigmoid(_dot(xb_ref[...], wpg_ref[...]) + bg_ref[...])
        pb = p_ref[...]
        up = jnp.concatenate([_dot(pb, wpu_ref[j]) for j in range(N_DEV)], axis=-1)
        sg_ref[...] = sg.astype(CDT)
        up_ref[...] = (up * sg * (1.0 - sg)).astype(CDT)
        z = ALPHA * x_ref[...] + sg * up
        z_ref[...] = z
        y = _ln_fwd(z, g_ref[...], be_ref[...])
        if target is None:
            xo_ref[...] = y
            xob_ref[...] = y.astype(CDT)
        else:
            e = y - t_ref[...]
            dy_ref[...] = e * (1.0 / D)
            part = 0.5 * jnp.sum(jnp.sum(e * e, axis=-1, keepdims=True) * (1.0 / D), axis=0, keepdims=True)
            _acc(l_ref, jnp.broadcast_to(part, l_ref.shape), pl.program_id(0) == 0)

    row = pl.BlockSpec((tm, D), lambda i: (i, 0))
    vec = pl.BlockSpec((1, D), lambda i: (0, 0))
    last = target is not None
    return _call(
        None, body, grid=(S // tm,),
        in_specs=[row, row, pl.BlockSpec((tm, PLE_DIM), lambda i: (i, 0)), pl.BlockSpec((D, D), lambda i: (0, 0)), vec,
                  pl.BlockSpec((N_DEV, PLE_DIM, D // N_DEV), lambda i: (0, 0, 0)), vec, vec] + [row] * last,
        out_specs=[row] * 4 + [pl.BlockSpec((1, 128), lambda i: (0, 0)) if last else row],
        out_shape=[_sds((S, D), CDT)] * 2 + [_sds((S, D), F32)] * 2 + [_sds((1, 128), F32) if last else _sds((S, D), CDT)],
        compiler_params=_params(("arbitrary",)), name=name)(
            xin, xin_b, p_b, wpg, bgate, wpu, gain, beta, *([target] if last else []))


def _ple_bwd(dy, z, sg, up, gain, wpg, z_ffn, gain_ffn, name, after=None):
    S = dy.shape[0]
    tm = _tile(S)

    def body(dy_ref, z_ref, sg_ref, up_ref, g_ref, wpg_ref, zf_ref, gf_ref, dzf_ref, dzfb_ref, dgl_ref, dup_ref,
             dgain_ref, dbeta_ref, dbg_ref, dgainf_ref, dbetaf_ref):
        first = pl.program_id(0) == 0
        dy_ = dy_ref[...]
        dz, xhat = _ln_bwd(z_ref[...], g_ref[...], dy_)
        dgl = dz * up_ref[...].astype(F32)
        dgl_ref[...] = dgl.astype(CDT)
        dup_ref[...] = (dz * sg_ref[...].astype(F32)).astype(CDT)
        dx = ALPHA * dz + _dot_nt(dgl, wpg_ref[...])
        dzf, xhatf = _ln_bwd(zf_ref[...], gf_ref[...], dx)
        dzf_ref[...] = dzf
        dzfb_ref[...] = dzf.astype(CDT)
        _acc(dgain_ref, _colsum(dy_ * xhat), first)
        _acc(dbeta_ref, _colsum(dy_), first)
        _acc(dbg_ref, _colsum(dgl), first)
        _acc(dgainf_ref, _colsum(dx * xhatf), first)
        _acc(dbetaf_ref, _colsum(dx), first)

    row = pl.BlockSpec((tm, D), lambda i: (i, 0))
    vec = pl.BlockSpec((1, D), lambda i: (0, 0))
    return _call(
        after, body, grid=(S // tm,),
        in_specs=[row, row, row, row, vec, pl.BlockSpec((D, D), lambda i: (0, 0)), row, vec],
        out_specs=[row, row, row, row] + [vec] * 5,
        out_shape=[_sds((S, D), F32)] + [_sds((S, D), CDT)] * 3 + [_sds((1, D), F32)] * 5,
        compiler_params=_params(("arbitrary",)), name=name)(dy, z, sg, up, gain, wpg, z_ffn, gain_ffn)


def _ffn_bwd_hidden(dzb, gu, wdn, name, after=None):
    S = dzb.shape[0]
    tm = _tile(S)

    def hidden(dzb_ref, gu_ref, wdn_ref, dgu_ref):
        dhid = _dot_nt(dzb_ref[...], wdn_ref[...])
        gate, up = gu_ref[0].astype(F32), gu_ref[1].astype(F32)
        sg = _sigmoid(gate)
        dgu_ref[0] = (dhid * up * (sg * (1.0 + gate * (1.0 - sg)))).astype(CDT)
        dgu_ref[1] = (dhid * (gate * sg)).astype(CDT)

    blocks = pl.BlockSpec((2, None, tm, FFN_B), lambda j, i: (0, j, i, 0))
    return _call(
        after, hidden, grid=(4, S // tm),
        in_specs=[pl.BlockSpec((tm, D), lambda j, i: (i, 0)), blocks, pl.BlockSpec((None, FFN_B, D), lambda j, i: (j, 0, 0))],
        out_specs=blocks, out_shape=_sds((2, 4, S, FFN_B), CDT),
        compiler_params=_params(("arbitrary", "arbitrary")), name=name + "_hidden")(dzb, gu, wdn)


def _ffn_bwd_input(dz, dgu, wgu, name, after=None):
    S = dz.shape[0]
    tm = _tile(S)

    def to_input(dz_ref, dgu_ref, wgu_ref, dx_ref):
        acc = ALPHA * dz_ref[...]
        for g in range(2):
            for j in range(4):
                acc = acc + _dot(dgu_ref[g, j], wgu_ref[g, j])
        dx_ref[...] = acc

    rows = pl.BlockSpec((tm, D), lambda i: (i, 0))
    return _call(
        after, to_input, grid=(S // tm,),
        in_specs=[rows, pl.BlockSpec((2, 4, tm, FFN_B), lambda i: (0, 0, i, 0)),
                  pl.BlockSpec((2, 4, FFN_B, D), lambda i: (0, 0, 0, 0))],
        out_specs=rows, out_shape=_sds((S, D), F32),
        compiler_params=_params(("arbitrary",)), name=name + "_input")(dz, dgu, wgu)


def _mixout_bwd(dy, z, gain, w3, du_dtype, name, after=None):
    S = dy.shape[0]
    G, Kb, _ = w3.shape
    tm = _tile(S)

    def body(dy_ref, z_ref, g_ref, w_ref, dz_ref, dzb_ref, du_ref, dgain_ref, dbeta_ref, dbias_ref):
        first = pl.program_id(0) == 0
        dy_ = dy_ref[...]
        dz, xhat = _ln_bwd(z_ref[...], g_ref[...], dy_)
        dz_ref[...] = dz
        dzb = dz.astype(CDT)
        dzb_ref[...] = dzb
        for g in range(G):
            du_ref[g] = _dot_nt(dzb, w_ref[g]).astype(du_ref.dtype)
        _acc(dgain_ref, _colsum(dy_ * xhat), first)
        _acc(dbeta_ref, _colsum(dy_), first)
        _acc(dbias_ref, _colsum(dz), first)

    row = pl.BlockSpec((tm, D), lambda i: (i, 0))
    vec = pl.BlockSpec((1, D), lambda i: (0, 0))
    return _call(
        after, body, grid=(S // tm,), in_specs=[row, row, vec, pl.BlockSpec((G, Kb, D), lambda i: (0, 0, 0))],
        out_specs=[row, row, pl.BlockSpec((G, tm, Kb), lambda i: (0, i, 0)), vec, vec, vec],
        out_shape=[_sds((S, D), F32), _sds((S, D), CDT), _sds((G, S, Kb), du_dtype)] + [_sds((1, D), F32)] * 3,
        compiler_params=_params(("arbitrary",)), name=name)(dy, z, gain, w3)


def _half_select(low):
    r = lax.broadcasted_iota(jnp.int32, (2 * ATT_HD, ATT_HD), 0)
    c = lax.broadcasted_iota(jnp.int32, (2 * ATT_HD, ATT_HD), 1)
    return (r == c + (0 if low else ATT_HD)).astype(CDT)


def _half_place(low):
    r = lax.broadcasted_iota(jnp.int32, (ATT_HD, 2 * ATT_HD), 0)
    c = lax.broadcasted_iota(jnp.int32, (ATT_HD, 2 * ATT_HD), 1)
    return (c == r + (0 if low else ATT_HD)).astype(CDT)


def _pair_lanes(even, odd):
    return (jnp.dot(even, _half_place(True), preferred_element_type=F32)
            + jnp.dot(odd, _half_place(False), preferred_element_type=F32)).astype(CDT)


def _proj_heads(a, w, bias, heads, name, per_head=True):
    S, K = a.shape
    N = heads * ATT_HD
    tm = _tile(S)

    def body(a_ref, w_ref, b_ref, o_ref):
        acc = (_dot(a_ref[...], w_ref[...]) + b_ref[...]).astype(CDT)
        if not per_head:
            o_ref[...] = acc
            return
        sel = (_half_select(True), _half_select(False))
        for h in range(heads):
            pair = acc[:, (h // 2) * 2 * ATT_HD:(h // 2 + 1) * 2 * ATT_HD]
            o_ref[h] = jnp.dot(pair, sel[h % 2], preferred_element_type=F32).astype(CDT)

    out = (pl.BlockSpec((heads, tm, ATT_HD), lambda i: (0, i, 0)), _sds((heads, S, ATT_HD), CDT)) if per_head else (
        pl.BlockSpec((tm, N), lambda i: (i, 0)), _sds((S, N), CDT))
    return _call(
        None, body, grid=(S // tm,),
        in_specs=[pl.BlockSpec((tm, K), lambda i: (i, 0)), pl.BlockSpec((K, N), lambda i: (0, 0)),
                  pl.BlockSpec((1, N), lambda i: (0, 0))],
        out_specs=out[0], out_shape=out[1],
        compiler_params=_params(("arbitrary",)), name=name)(a, w, bias)


def _qkv_bwd(dz, dq, dkv4, wq, wkv, name, after=None):
    S = dz.shape[0]
    tm = _tile(S)
    HK = dkv4.shape[0]
    NK = HK * ATT_HD

    def body(dz_ref, dq_ref, dkv_ref, wq_ref, wkv_ref, dx_ref, dkvn_ref, dkvb_ref):
        first = pl.program_id(0) == 0
        dkvn = jnp.concatenate([_pair_lanes(dkv_ref[2 * i].astype(CDT), dkv_ref[2 * i + 1].astype(CDT))
                                for i in range(HK // 2)], axis=-1)
        dkvn_ref[...] = dkvn
        dx_ref[...] = ALPHA * dz_ref[...] + _dot_nt(dq_ref[...], wq_ref[...]) + _dot_nt(dkvn, wkv_ref[...])
        for h in range(HK):
            _acc(dkvb_ref.at[h], _colsum(dkv_ref[h]), first)

    row = pl.BlockSpec((tm, D), lambda i: (i, 0))
    return _call(
        after, body, grid=(S // tm,),
        in_specs=[row, row, pl.BlockSpec((HK, tm, ATT_HD), lambda i: (0, i, 0)),
                  pl.BlockSpec((D, D), lambda i: (0, 0)), pl.BlockSpec((D, NK), lambda i: (0, 0))],
        out_specs=[row, pl.BlockSpec((tm, NK), lambda i: (i, 0)), pl.BlockSpec((HK, 1, ATT_HD), lambda i: (0, 0, 0))],
        out_shape=[_sds((S, D), F32), _sds((S, NK), CDT), _sds((HK, 1, ATT_HD), F32)],
        compiler_params=_params(("arbitrary",)), name=name)(dz, dq, dkv4, wq, wkv)


def _inproj_fwd(xb, wain, name):
    S = xb.shape[0]
    tm = _tile(S) // 2
    nb = wain.shape[-1]

    def body(x_ref, w_ref, o_ref):
        x = x_ref[...]
        for j in range(N_DEV):
            o_ref[j // 2, :, pl.ds((j % 2) * nb, nb)] = _dot(x, w_ref[j])

    return _call(
        None, body, grid=(S // tm,),
        in_specs=[pl.BlockSpec((tm, D), lambda i: (i, 0)), pl.BlockSpec((N_DEV, D, nb), lambda i: (0, 0, 0))],
        out_specs=pl.BlockSpec((4, tm, D), lambda i: (0, i, 0)), out_shape=_sds((4, S, D), F32),
        compiler_params=_params(("arbitrary",)), name=name)(xb, wain)


def _inproj_bwd(dz, dproj, wain, name, after=None):
    S = dz.shape[0]
    tm = _tile(S)
    nb = wain.shape[-1]

    def body(dz_ref, dp_ref, w_ref, dx_ref):
        acc = ALPHA * dz_ref[...]
        for j in range(N_DEV):
            acc = acc + _dot_nt(dp_ref[j // 2, :, pl.ds((j % 2) * nb, nb)], w_ref[j])
        dx_ref[...] = acc

    row = pl.BlockSpec((tm, D), lambda i: (i, 0))
    return _call(
        after, body, grid=(S // tm,),
        in_specs=[row, pl.BlockSpec((4, tm, D), lambda i: (0, i, 0)), pl.BlockSpec((N_DEV, D, nb), lambda i: (0, 0, 0))],
        out_specs=row, out_shape=_sds((S, D), F32),
        compiler_params=_params(("arbitrary",)), name=name)(dz, dproj, wain)


def _running_sum(x, reverse=False):
    rows = x.shape[0]
    row = lax.broadcasted_iota(jnp.int32, x.shape, 0)
    step = 1
    while step < rows:
        if reverse:
            x = x + jnp.where(row < rows - step, pltpu.roll(x, rows - step, 0), 0.0)
        else:
            x = x + jnp.where(row >= step, pltpu.roll(x, step, 0), 0.0)
        step *= 2
    return x


def _hg_gates(q, f, alb_ref):
    a0, a1 = alb_ref[0:1, :], alb_ref[1:2, :]
    mx = jnp.maximum(a0, a1)
    e0, e1 = jnp.exp(a0 - mx), jnp.exp(a1 - mx)
    lb = e0 / (e0 + e1)
    sig = _sigmoid(f)
    forget = lb + (1.0 - lb) * sig
    k = (1.0 - lb) * _sigmoid(-f)
    qs = q * _sigmoid(q) * (HG_DK ** -0.5)
    return qs, k, jnp.log(forget), sig, lb, forget


def _hg_intra(qs, k, b, b_scr):
    b_scr[...] = b
    bm = b_scr[pl.ds(HG_CH // 2 - 1, 1), :]
    bl = b_scr[pl.ds(HG_CH - 1, 1), :]
    eb = jnp.exp(b)
    qb = qs * eb
    e_q = jnp.exp(b - bm)
    e_k = jnp.exp(bm - b)
    e_d = jnp.exp(bl - b)
    return qb, qs * e_q, k * e_k, k * e_d, jnp.exp(bl), eb, e_q, e_k, e_d


def _hgrn_fwd(proj, alb, ngain):
    S = proj.shape[1]
    nc = S // HG_CH
    nb = nc // HG_CPB
    rb, wb = HG_CPB * HG_CH, HG_HPB * HG_DK

    def body(pj_ref, alb_ref, ng_ref, o_ref, y_ref, st_ref, st_scr, b_scr):
        n = pl.program_id(1)

        @pl.when(n == 0)
        def _():
            st_scr[...] = jnp.zeros_like(st_scr)

        r = lax.broadcasted_iota(jnp.int32, (HG_CH, HG_CH), 0)
        c = lax.broadcasted_iota(jnp.int32, (HG_CH, HG_CH), 1)
        causal = r >= c
        for ci, j in [(ci, j) for ci in range(HG_CPB) for j in range(HG_HPB)]:
            rows, lanes = pl.ds(ci * HG_CH, HG_CH), pl.ds(j * HG_DK, HG_DK)
            q, f, v, g = pj_ref[0, rows, lanes], pj_ref[1, rows, lanes], pj_ref[2, rows, lanes], pj_ref[3, rows, lanes]
            qs, k, logf, _, _, _ = _hg_gates(q, f, alb_ref.at[:, lanes])
            b = _running_sum(logf)
            qb, qt, kt, kd, ebl, _, _, _, _ = _hg_intra(qs, k, b, b_scr.at[j, ci])
            st = st_scr[j]
            st_ref[j, ci] = st
            a = jnp.where(causal, _dot_nt(qt, kt), 0.0)
            o = _dot(a, v) + _dot_nt(qb, st)
            st_scr[j] = st * ebl + _dot_tn(v, kd)
            o_ref[rows, lanes] = o
            rinv = lax.rsqrt(jnp.mean(o * o, axis=-1, keepdims=True) + RMS_EPS)
            y_ref[rows, lanes] = (o * rinv * ng_ref[...] * (g * _sigmoid(g))).astype(CDT)

    blk = pl.BlockSpec((rb, wb), lambda h, n: (n, h))
    return _call(
        None, body, grid=(HG_H // HG_HPB, nb),
        in_specs=[pl.BlockSpec((4, rb, wb), lambda h, n: (0, n, h)), pl.BlockSpec((2, wb), lambda h, n: (0, h)),
                  pl.BlockSpec((1, HG_DK), lambda h, n: (0, 0))],
        out_specs=[blk, blk, pl.BlockSpec((HG_HPB, HG_CPB, HG_DK, HG_DK), lambda h, n: (h, n, 0, 0))],
        out_shape=[_sds((S, D), F32), _sds((S, D), CDT), _sds((HG_H, nc, HG_DK, HG_DK), F32)],
        scratch_shapes=[pltpu.VMEM((HG_HPB, HG_DK, HG_DK), F32), pltpu.VMEM((HG_HPB, HG_CPB, HG_CH, HG_DK), F32)],
        compiler_params=_params(("arbitrary", "arbitrary")), name="hgrn_fwd")(proj, alb, ngain)


def _hgrn_bwd(proj, alb, ngain, o, states, dy, after=None):
    S = proj.shape[1]
    nc = S // HG_CH
    nb = nc // HG_CPB
    rb, wb = HG_CPB * HG_CH, HG_HPB * HG_DK

    def body(pj_ref, alb_ref, ng_ref, o_ref, st_ref, dy_ref, dpj_ref, dalb_ref, dng_ref, dst_scr, b_scr):
        h, n = pl.program_id(0), pl.program_id(1)

        @pl.when(n == 0)
        def _():
            dst_scr[...] = jnp.zeros_like(dst_scr)
            dalb_ref[...] = jnp.zeros_like(dalb_ref)

        _zero_at(dng_ref, jnp.logical_and(h == 0, n == 0))
        ng = ng_ref[...]
        r = lax.broadcasted_iota(jnp.int32, (HG_CH, HG_CH), 0)
        c = lax.broadcasted_iota(jnp.int32, (HG_CH, HG_CH), 1)
        causal = r >= c
        dng = None
        for ci, j in [(ci, j) for ci in reversed(range(HG_CPB)) for j in range(HG_HPB)]:
            rows, lanes = pl.ds(ci * HG_CH, HG_CH), pl.ds(j * HG_DK, HG_DK)
            q, f, v, g = pj_ref[0, rows, lanes], pj_ref[1, rows, lanes], pj_ref[2, rows, lanes], pj_ref[3, rows, lanes]
            o_ = o_ref[rows, lanes]
            dy_ = dy_ref[rows, lanes]
            sg = _sigmoid(g)
            rinv = lax.rsqrt(jnp.mean(o_ * o_, axis=-1, keepdims=True) + RMS_EPS)
            nrm = o_ * rinv
            dr = dy_ * (g * sg)
            dg = dy_ * nrm * ng * (sg * (1.0 + g * (1.0 - sg)))
            dn = dr * ng
            do = rinv * (dn - nrm * jnp.mean(dn * nrm, axis=-1, keepdims=True))
            dng = _colsum(dr * nrm) if dng is None else dng + _colsum(dr * nrm)
            qs, k, logf, sig, lb, forget = _hg_gates(q, f, alb_ref.at[:, lanes])
            b = _running_sum(logf)
            qb, qt, kt, kd, ebl, eb, e_q, e_k, e_d = _hg_intra(qs, k, b, b_scr.at[j, ci])
            st = st_ref[j, ci]
            dstn = dst_scr[j]
            qt, kt, qb, kd = (t.astype(CDT).astype(F32) for t in (qt, kt, qb, kd))
            a = jnp.where(causal, _dot_nt(qt, kt), 0.0)
            da = jnp.where(causal, _dot_nt(do, v), 0.0)
            dv = _dot_tn(a, do) + _dot_nt(kd, dstn)
            dqb = _dot(do, st)
            dkd = _dot(v, dstn)
            dqt = _dot(da, kt)
            dkt = _dot_tn(da, qt)
            dbl = _colsum(dkd * kd) + ebl * _colsum(dstn * st)
            dst_scr[j] = dstn * ebl + _dot_tn(do, qb)
            dqs = dqt * e_q + dqb * eb
            dk = dkt * e_k + dkd * e_d
            db = dqt * qt + dqb * qb - dkt * kt - dkd * kd
            dlogf = _running_sum(db, reverse=True) + dbl
            dforget = dlogf / forget
            dsig = (1.0 - lb) * (dforget - dk)
            df = dsig * sig * (1.0 - sig)
            dlb = _colsum((dforget - dk) * (1.0 - sig))
            sq = _sigmoid(q)
            dq = dqs * (HG_DK ** -0.5) * (sq * (1.0 + q * (1.0 - sq)))
            dpj_ref[0, rows, lanes] = dq.astype(CDT)
            dpj_ref[1, rows, lanes] = df.astype(CDT)
            dpj_ref[2, rows, lanes] = dv.astype(CDT)
            dpj_ref[3, rows, lanes] = dg.astype(CDT)
            da0 = dlb * lb * (1.0 - lb)
            dalb_ref[pl.ds(0, 1), lanes] += da0
            dalb_ref[pl.ds(1, 1), lanes] -= da0
        dng_ref[...] += dng

    blk = pl.BlockSpec((rb, wb), lambda h, n: (nb - 1 - n, h))
    pj = pl.BlockSpec((4, rb, wb), lambda h, n: (0, nb - 1 - n, h))
    alb_blk = pl.BlockSpec((2, wb), lambda h, n: (0, h))
    ng_blk = pl.BlockSpec((1, HG_DK), lambda h, n: (0, 0))
    return _call(
        after, body, grid=(HG_H // HG_HPB, nb),
        in_specs=[pj, alb_blk, ng_blk, blk,
                  pl.BlockSpec((HG_HPB, HG_CPB, HG_DK, HG_DK), lambda h, n: (h, nb - 1 - n, 0, 0)), blk],
        out_specs=[pj, alb_blk, ng_blk],
        out_shape=[_sds((4, S, D), CDT), _sds((2, D), F32), _sds((1, HG_DK), F32)],
        scratch_shapes=[pltpu.VMEM((HG_HPB, HG_DK, HG_DK), F32), pltpu.VMEM((HG_HPB, HG_CPB, HG_CH, HG_DK), F32)],
        compiler_params=_params(("arbitrary", "arbitrary")), name="hgrn_bwd")(proj, alb, ngain, o, states, dy)


def _slope(h):
    return 2.0 ** (-8.0 * (h + 1) / ATT_QH)


def _attn_mask(n):
    qi = lax.broadcasted_iota(jnp.int32, (WINDOW, 2 * WINDOW), 0)
    si = lax.broadcasted_iota(jnp.int32, (WINDOW, 2 * WINDOW), 1)
    dist = qi - si + WINDOW
    valid = (dist >= 0) & (dist < WINDOW) & (n * WINDOW - WINDOW + si >= 0)
    return valid, dist.astype(F32)


def _attn_probs(qk, sink, slope, valid, distf):
    s = qk * (ATT_HD ** -0.5) - slope * distf
    s = jnp.where(valid, s, NEG)
    m = jnp.maximum(jnp.max(s, axis=-1, keepdims=True), sink)
    e = jnp.exp(s - m)
    es = jnp.exp(sink - m)
    inv = 1.0 / (jnp.sum(e, axis=-1, keepdims=True) + es)
    return e * inv, es * inv


def _attn_specs(S):
    steps = S // (ATT_BPB * WINDOW)
    cur = lambda H: pl.BlockSpec((H, ATT_BPB * WINDOW, ATT_HD), lambda n: (0, n, 0))
    prev = lambda H: pl.BlockSpec((H, WINDOW, ATT_HD), lambda n: (0, jnp.maximum(ATT_BPB * n - 1, 0), 0))
    return steps, cur, prev


def _attn_kv(kvc_ref, kvp_ref, head, bi):
    before = kvp_ref[head] if bi == 0 else kvc_ref[head, pl.ds((bi - 1) * WINDOW, WINDOW), :]
    return jnp.concatenate([before, kvc_ref[head, pl.ds(bi * WINDOW, WINDOW), :]], axis=0)


def _attn_fwd(q, kv4, sinks):
    S = q.shape[0]
    nb, cur, prev = _attn_specs(S)

    def body(sink_ref, q_ref, kvc_ref, kvp_ref, o_ref, p_ref, ps_ref):
        lane = lax.broadcasted_iota(jnp.int32, (1, 128), 1)
        place = (_half_place(True), _half_place(False))
        masks = [_attn_mask(pl.program_id(0) * ATT_BPB + bi) for bi in range(ATT_BPB)]
        sink_probs = [jnp.zeros((WINDOW, 128), F32) for _ in range(ATT_BPB)]
        heads = lambda kvh: range(kvh * ATT_G, (kvh + 1) * ATT_G)

        def scores(bi, kvh):
            rows = pl.ds(bi * WINDOW, WINDOW)
            kh = _attn_kv(kvc_ref, kvp_ref, kvh, bi)
            vh = _attn_kv(kvc_ref, kvp_ref, ATT_KVH + kvh, bi)
            k_pl = [jnp.dot(kh, m, preferred_element_type=F32).astype(CDT) for m in place]
            v_pl = [jnp.dot(vh, m, preferred_element_type=F32).astype(CDT) for m in place]
            return v_pl, [_dot_nt(q_ref[rows, pl.ds((h // 2) * 2 * ATT_HD, 2 * ATT_HD)], k_pl[h % 2])
                          for h in heads(kvh)]

        def softmax(bi, kvh, qks):
            rows = pl.ds(bi * WINDOW, WINDOW)
            probs = []
            for h, qk in zip(heads(kvh), qks):
                p, ps = _attn_probs(qk, sink_ref[0, h], _slope(h), *masks[bi])
                probs.append(p.astype(CDT))
                p_ref[h, rows, :] = probs[-1]
                sink_probs[bi] = sink_probs[bi] + jnp.where(lane == h, ps, 0.0)
            return probs

        def weighted_values(bi, kvh, v_pl, probs):
            rows = pl.ds(bi * WINDOW, WINDOW)
            for i in range(ATT_G // 2):
                lanes = pl.ds((kvh * ATT_G + 2 * i) * ATT_HD, 2 * ATT_HD)
                o_ref[rows, lanes] = (_dot(probs[2 * i], v_pl[0]) + _dot(probs[2 * i + 1], v_pl[1])).astype(CDT)

        groups = [(bi, kvh) for bi in range(ATT_BPB) for kvh in range(ATT_KVH)]
        ahead = scores(*groups[0])
        for gi, g in enumerate(groups):
            v_pl, qks = ahead
            probs = softmax(*g, qks)
            if gi + 1 < len(groups):
                ahead = scores(*groups[gi + 1])
            weighted_values(*g, v_pl, probs)
        for bi in range(ATT_BPB):
            ps_ref[pl.ds(bi * WINDOW, WINDOW), :] = sink_probs[bi]

    rows_spec = pl.BlockSpec((ATT_BPB * WINDOW, D), lambda n: (n, 0))
    return _call(
        None, body, grid=(nb,),
        in_specs=[pl.BlockSpec(memory_space=pltpu.SMEM), rows_spec, cur(2 * ATT_KVH), prev(2 * ATT_KVH)],
        out_specs=[rows_spec, pl.BlockSpec((ATT_QH, ATT_BPB * WINDOW, 2 * WINDOW), lambda n: (0, n, 0)),
                   pl.BlockSpec((ATT_BPB * WINDOW, 128), lambda n: (n, 0))],
        out_shape=[_sds((S, D), CDT), _sds((ATT_QH, S, 2 * WINDOW), CDT), _sds((S, 128), F32)],
        compiler_params=_params(("arbitrary",)), name="attn_fwd")(sinks, q, kv4, kv4)


def _attn_bwd(q, kv4, probs, sink_probs, do):
    S = q.shape[0]
    nb, cur, prev = _attn_specs(S)

    def body(q_ref, kvc_ref, kvp_ref, p_ref, ps_ref, do_ref, dq_ref, dkv_ref, dbq_ref, dsink_ref):
        n = pl.program_id(0)
        first = n == 0

        @pl.when(first)
        def _():
            dkv_ref[...] = jnp.zeros_like(dkv_ref)
            dsink_ref[...] = jnp.zeros_like(dsink_ref)
            dbq_ref[...] = jnp.zeros_like(dbq_ref)

        lane = lax.broadcasted_iota(jnp.int32, (1, 128), 1)
        place = (_half_place(True), _half_place(False))
        row_dots = [jnp.zeros((WINDOW, 128), F32) for _ in range(ATT_BPB)]
        heads = lambda kvh: range(kvh * ATT_G, (kvh + 1) * ATT_G)
        pair_lanes = lambda h: pl.ds((h // 2) * 2 * ATT_HD, 2 * ATT_HD)

        def products_of_do(bi, kvh):
            rows = pl.ds(bi * WINDOW, WINDOW)
            kh = _attn_kv(kvc_ref, kvp_ref, kvh, bi)
            vh = _attn_kv(kvc_ref, kvp_ref, ATT_KVH + kvh, bi)
            k_pl = [jnp.dot(kh, m, preferred_element_type=F32).astype(CDT) for m in place]
            v_pl = [jnp.dot(vh, m, preferred_element_type=F32).astype(CDT) for m in place]
            dps = [_dot_nt(do_ref[rows, pair_lanes(h)], v_pl[h % 2]) for h in heads(kvh)]
            dv2 = [sum(_dot_tn(p_ref[h, rows, :], do_ref[rows, pair_lanes(h)]) for h in heads(kvh) if h % 2 == par)
                   for par in range(2)]
            return k_pl, dps, dv2

        def softmax_bwd(bi, kvh, dps):
            rows = pl.ds(bi * WINDOW, WINDOW)
            dss = []
            for h, dp in zip(heads(kvh), dps):
                p = p_ref[h, rows, :].astype(F32)
                dd = jnp.sum(p * dp, axis=-1, keepdims=True)
                dss.append((p * (dp - dd)).astype(CDT))
                row_dots[bi] = row_dots[bi] + jnp.where(lane == h, dd, 0.0)
            return dss

        def products_of_ds(bi, kvh, k_pl, dss, dv2):
            block = n * ATT_BPB + bi
            rows = pl.ds(bi * WINDOW, WINDOW)
            rows_cur = pl.ds(pl.multiple_of(block * WINDOW, WINDOW), WINDOW)
            rows_prev = pl.ds(pl.multiple_of(jnp.maximum(block - 1, 0) * WINDOW, WINDOW), WINDOW)
            for i in range(ATT_G // 2):
                h = kvh * ATT_G + 2 * i
                dq2 = (_dot(dss[2 * i], k_pl[0]) + _dot(dss[2 * i + 1], k_pl[1])) * (ATT_HD ** -0.5)
                dq_ref[rows, pair_lanes(h)] = dq2.astype(CDT)
                dbq_ref[:, pair_lanes(h)] += _colsum(dq2)
            dk2 = [sum(_dot_tn(ds, q_ref[rows, pair_lanes(h)]) for h, ds in zip(heads(kvh), dss) if h % 2 == par)
                   for par in range(2)]
            dk = (dk2[0][:, :ATT_HD] + pltpu.roll(dk2[1], ATT_HD, 1)[:, :ATT_HD]) * (ATT_HD ** -0.5)
            dv = dv2[0][:, :ATT_HD] + pltpu.roll(dv2[1], ATT_HD, 1)[:, :ATT_HD]
            dkv_ref[kvh, rows_prev, :] += dk[:WINDOW]
            dkv_ref[kvh, rows_cur, :] += dk[WINDOW:]
            dkv_ref[ATT_KVH + kvh, rows_prev, :] += dv[:WINDOW]
            dkv_ref[ATT_KVH + kvh, rows_cur, :] += dv[WINDOW:]

        groups = [(bi, kvh) for bi in range(ATT_BPB) for kvh in range(ATT_KVH)]
        ahead = products_of_do(*groups[0])
        for gi, g in enumerate(groups):
            k_pl, dps, dv2 = ahead
            dss = softmax_bwd(*g, dps)
            if gi + 1 < len(groups):
                ahead = products_of_do(*groups[gi + 1])
            products_of_ds(*g, k_pl, dss, dv2)
        dsinks = jnp.zeros((1, 128), F32)
        for bi in range(ATT_BPB):
            dsinks = dsinks - _colsum(ps_ref[pl.ds(bi * WINDOW, WINDOW), :] * row_dots[bi])
        dsink_ref[...] += dsinks

    rows_spec = pl.BlockSpec((ATT_BPB * WINDOW, D), lambda n: (n, 0))
    return _call(
        None, body, grid=(nb,),
        in_specs=[rows_spec, cur(2 * ATT_KVH), prev(2 * ATT_KVH),
                  pl.BlockSpec((ATT_QH, ATT_BPB * WINDOW, 2 * WINDOW), lambda n: (0, n, 0)),
                  pl.BlockSpec((ATT_BPB * WINDOW, 128), lambda n: (n, 0)), rows_spec],
        out_specs=[rows_spec, pl.BlockSpec((2 * ATT_KVH, S, ATT_HD), lambda n: (0, 0, 0)),
                   pl.BlockSpec((1, D), lambda n: (0, 0)), pl.BlockSpec((1, 128), lambda n: (0, 0))],
        out_shape=[_sds((S, D), CDT), _sds((2 * ATT_KVH, S, ATT_HD), F32), _sds((1, D), F32),
                   _sds((1, 128), F32)],
        compiler_params=_params(("arbitrary",)), name="attn_bwd")(q, kv4, kv4, probs, sink_probs, do)


def _local_step(x, p, target, getw, sm, emit):
    S = x.shape[0]
    vec = lambda a: a.reshape(1, -1)
    ln_g = lambda l, k: vec(sm["ln_gain"][l, k])
    ln_b = lambda l, k: vec(sm["ln_bias"][l, k])
    xb = x.astype(CDT)
    pb = p.astype(CDT)

    proj = _inproj_fwd(xb, getw("a_w_in"), "a_in")
    o_a, y_a, states = _hgrn_fwd(proj, sm["a_lower_bound"], sm["a_norm_gain"])
    zeros = jnp.zeros((1, D), F32)
    z = [[None] * 3 for _ in range(2)]
    xs = [[None] * 3 for _ in range(2)]
    xbs = [[None] * 3 for _ in range(2)]
    z[0][0], xs[0][0], xbs[0][0] = _mixout_ln(y_a[None], getw("a_w_out")[None], zeros, x, ln_g(0, 0), ln_b(0, 0),
                                              "a_out_ln")
    gu, hid, sgs, ups = [None, None], [None, None], [None, None], [None, None]

    def ffn_ple(l, target=None):
        wgu = getw(f"gu{l}")
        gu[l], hid[l], z[l][1], xs[l][1], xbs[l][1] = _ffn_fwd(
            xs[l][0], xbs[l][0], wgu, getw(f"dn{l}"), ln_g(l, 1), ln_b(l, 1), f"ffn_fwd{l}")
        sgs[l], ups[l], z[l][2], *out = _ple_fwd(
            xs[l][1], xbs[l][1], pb[l], getw(f"pg{l}"), vec(sm["ple_b_gate"][l]), getw(f"pu{l}"), ln_g(l, 2),
            ln_b(l, 2), f"ple_fwd{l}", target=target)
        if target is None:
            xs[l][2], xbs[l][2] = out
        return out

    ffn_ple(0)
    x3, x3b = xs[0][2], xbs[0][2]
    w_kv, w_q, w_bo = getw("kv_w"), getw("b_w_q"), getw("b_w_out")
    kv4 = _proj_heads(x3b, w_kv, vec(sm["kv_b"]), 2 * ATT_KVH, "kv_proj")
    q = _proj_heads(x3b, w_q, vec(sm["b_b_q"]), ATT_QH, "q_proj", per_head=False)
    o_b, probs, sink_probs = _attn_fwd(q, kv4, sm["b_sinks"])
    z[1][0], xs[1][0], xbs[1][0] = _mixout_ln(o_b[None], w_bo[None], sm["b_b_out"], x3, ln_g(1, 0), ln_b(1, 0),
                                              "b_out_ln")
    dy, loss = ffn_ple(1, target)

    gs = {}
    d_ln_g = [[None] * 3 for _ in range(2)]
    d_ln_b = [[None] * 3 for _ in range(2)]
    g_bg = [None, None]

    def ffn_ple_bwd(l, dy, after=None):
        dz2, dzb, dgl, dup, d_ln_g[l][2], d_ln_b[l][2], g_bg[l], d_ln_g[l][1], d_ln_b[l][1] = _ple_bwd(
            dy, z[l][2], sgs[l], ups[l], ln_g(l, 2), getw(f"pg{l}"), z[l][1], ln_g(l, 1), f"ple_bwd{l}", after=after)
        g_pg = _wgrad(xbs[l][1][None], dgl[None], f"g_ple_gate{l}")[0]
        g_pu = _wgrad(pb[l][None], dup[None], f"g_ple_up{l}")[0]
        g_dn = _wgrad(hid[l], dzb[None], f"g_ffn_down{l}")
        if l == 1:
            tok = emit({f"pg{l}": g_pg, f"pu{l}": g_pu})
            tok = tok + emit({f"dn{l}": g_dn})
        else:
            tok = emit({f"pg{l}": g_pg, f"pu{l}": g_pu, f"dn{l}": g_dn})
        dgu = _ffn_bwd_hidden(dzb, gu[l], getw(f"dn{l}"), f"ffn_bwd{l}", after=tok)
        g_gu = _wgrad(dgu.reshape(8, S, FFN_B), xbs[l][0][None], f"g_ffn_gate_up{l}")
        tok = emit({f"gu{l}": g_gu})
        dx1 = _ffn_bwd_input(dz2, dgu, getw(f"gu{l}"), f"ffn_bwd{l}", after=tok)
        return dx1, None

    dx1, tok = ffn_ple_bwd(1, dy)
    dz, dzb, do, d_ln_g[1][0], d_ln_b[1][0], gs["b_b_out"] = _mixout_bwd(dx1, z[1][0], ln_g(1, 0), w_bo[None], CDT,
                                                                        "b_out_bwd", after=tok)
    g_bo = _wgrad(o_b[None], dzb[None], "g_b_w_out")[0]
    dq, dkv4, dbq, dsinks = _attn_bwd(q, kv4, probs, sink_probs, do[0])
    gs["b_b_q"] = dbq
    gs["b_sinks"] = dsinks
    g_q = _wgrad(x3b[None], dq[None], "g_b_w_q")[0]
    dx3, dkv, gs["kv_b"] = _qkv_bwd(dz, dq, dkv4, w_q, w_kv, "qkv_bwd", after=tok)
    g_kv = _wgrad(x3b[None], dkv[None], "g_kv_w")[0]
    tok = emit({"b_w_out": g_bo, "b_w_q": g_q, "kv_w": g_kv})
    dx1, tok = ffn_ple_bwd(0, dx3, tok)
    w_ao = getw("a_w_out")
    dz, dzb, dyr, d_ln_g[0][0], d_ln_b[0][0], _ = _mixout_bwd(dx1, z[0][0], ln_g(0, 0), w_ao[None], F32, "a_out_bwd",
                                                              after=tok)
    g_ao = _wgrad(y_a[None], dzb[None], "g_a_w_out")[0]
    tok = emit({"a_w_out": g_ao})
    dproj, gs["a_lower_bound"], gs["a_norm_gain"] = _hgrn_bwd(proj, sm["a_lower_bound"], sm["a_norm_gain"], o_a, states,
                                                              dyr[0], after=tok)
    tk = lambda t: (None, t, D)
    g_ain = _mm_tn(xb[None], dproj, N_DEV, lambda g, k: (0, k, 0), lambda g, k: (g // 2, k, g % 2),
                   tk, lambda t: (None, t, 512), (N_DEV, D, 512), (None, D, 512), lambda g, k: (g, 0, 0), name="g_a_w_in")
    gs["ple_b_gate"] = jnp.concatenate(g_bg, axis=0)
    gs["ln_gain"] = jnp.stack([jnp.concatenate(r, axis=0) for r in d_ln_g])
    gs["ln_bias"] = jnp.stack([jnp.concatenate(r, axis=0) for r in d_ln_b])
    gs["loss"] = loss
    tok = emit({"a_w_in": g_ain}, small=gs)
    grad_x = _inproj_bwd(dz, dproj, getw("a_w_in"), "a_in_bwd", after=tok)
    return loss, grad_x, gs


def _peer(k):
    x, y, c = lax.axis_index("x"), lax.axis_index("y"), lax.axis_index("c")
    px = 1 - x if k & 4 else x
    py = 1 - y if k & 2 else y
    pc = 1 - c if k & 1 else c
    return (px, py, pc), 4 * px + 2 * py + pc


def _my_index():
    return 4 * lax.axis_index("x") + 2 * lax.axis_index("y") + lax.axis_index("c")


def _piece_copy(mode, src, land, send_sems, recv_sems, t, k, sender, receiver, peer):
    return pltpu.make_async_remote_copy(
        src_ref=src if mode == "gather" else src.at[receiver], dst_ref=land.at[sender],
        send_sem=send_sems.at[t * 7 + k - 1], recv_sem=recv_sems.at[t * 7 + k - 1], device_id=peer, device_id_type=MESH)


def _sequencer_exchange(srcs, modes, name, collective_id, after=None):
    n = len(srcs)
    land_shapes = [((N_DEV,) + a.shape) if mode == "gather" else a.shape for a, mode in zip(srcs, modes)]
    extra = [] if after is None else [after]

    def body(*refs):
        src_refs, land_refs = refs[:n], refs[n + len(extra):2 * n + len(extra)]
        send_sems, recv_sems, local_sems = refs[2 * n + len(extra):]
        barrier = pltpu.get_barrier_semaphore()
        for k in range(1, N_DEV):
            pl.semaphore_signal(barrier, inc=1, device_id=_peer(k)[0], device_id_type=MESH)
        pl.semaphore_wait(barrier, N_DEV - 1)
        me = _my_index()
        local = []
        for i in range(n):
            cp = pltpu.make_async_copy(src_refs[i] if modes[i] == "gather" else src_refs[i].at[me], land_refs[i].at[me],
                                       local_sems.at[i])
            cp.start()
            local.append(cp)
        for k in range(1, N_DEV):
            peer, pid = _peer(k)
            for t in range(n):
                _piece_copy(modes[t], src_refs[t], land_refs[t], send_sems, recv_sems, t, k, me, pid, peer).start()
        for k in range(1, N_DEV):
            peer, pid = _peer(k)
            for t in range(n):
                _piece_copy(modes[t], src_refs[t], land_refs[t], send_sems, recv_sems, t, k, pid, me, peer).wait_recv()
        for k in range(1, N_DEV):
            peer, pid = _peer(k)
            for t in range(n):
                _piece_copy(modes[t], src_refs[t], land_refs[t], send_sems, recv_sems, t, k, me, pid, peer).wait_send()
        for cp in local:
            cp.wait()

    return pl.kernel(
        body, out_type=[_sds(s, a.dtype) for s, a in zip(land_shapes, srcs)],
        mesh=plsc.ScalarSubcoreMesh(axis_name="sequencer", num_cores=1),
        scratch_types=[pltpu.SemaphoreType.DMA((7 * n,)), pltpu.SemaphoreType.DMA((7 * n,)), pltpu.SemaphoreType.DMA((n,))],
        compiler_params=pltpu.CompilerParams(collective_id=collective_id), name=name)(*srcs, *extra)


def _sequencer_gather(srcs, name, collective_id, after=None):
    n = len(srcs)
    extra = [] if after is None else [after]

    def body(*refs):
        src_refs, land_refs = refs[:n], refs[n + len(extra):2 * n + len(extra)]
        send_sems, recv_sems, local_sems = refs[2 * n + len(extra):]
        x, y, c = lax.axis_index("x"), lax.axis_index("y"), lax.axis_index("c")
        sibling = (x, y, 1 - c)
        chips = [(1 - x, y), (x, 1 - y), (1 - x, 1 - y)]
        index = lambda px, py, pc: 4 * px + 2 * py + pc
        barrier = pltpu.get_barrier_semaphore()
        for peer in [sibling] + [(*chip, c) for chip in chips]:
            pl.semaphore_signal(barrier, inc=1, device_id=peer, device_id_type=MESH)
        pl.semaphore_wait(barrier, 4)

        def copy(t, k, slot, to, src=None):
            return pltpu.make_async_remote_copy(
                src_ref=land_refs[t].at[slot] if src is None else src, dst_ref=land_refs[t].at[slot],
                send_sem=send_sems.at[7 * t + k], recv_sem=recv_sems.at[7 * t + k], device_id=to, device_id_type=MESH)

        me = index(x, y, c)
        local = []
        for t in range(n):
            cp = pltpu.make_async_copy(src_refs[t], land_refs[t].at[me], local_sems.at[t])
            cp.start()
            local.append(cp)
        sends = []
        for t in range(n):
            sends.append(copy(t, 0, me, sibling, src=src_refs[t]))
            sends += [copy(t, 1 + j, me, (*chip, c), src=src_refs[t]) for j, chip in enumerate(chips)]
        for cp in sends:
            cp.start()
        for j, chip in enumerate(chips):
            for t in range(n):
                copy(t, 1 + j, index(*chip, c), sibling, src=src_refs[t]).wait_recv()
                passed = copy(t, 4 + j, index(*chip, c), sibling)
                passed.start()
                sends.append(passed)
        for t in range(n):
            copy(t, 0, index(x, y, 1 - c), sibling, src=src_refs[t]).wait_recv()
        for j, chip in enumerate(chips):
            for t in range(n):
                copy(t, 4 + j, index(*chip, 1 - c), sibling, src=src_refs[t]).wait_recv()
        for cp in sends:
            cp.wait_send()
        for cp in local:
            cp.wait()

    return pl.kernel(
        body, out_type=[_sds((N_DEV,) + a.shape, a.dtype) for a in srcs],
        mesh=plsc.ScalarSubcoreMesh(axis_name="sequencer", num_cores=1),
        scratch_types=[pltpu.SemaphoreType.DMA((7 * n,)), pltpu.SemaphoreType.DMA((7 * n,)), pltpu.SemaphoreType.DMA((n,))],
        compiler_params=pltpu.CompilerParams(collective_id=collective_id), name=name)(*srcs, *extra)


def _adamw(w, g, m, v):
    m = ADAM_B1 * m + (1.0 - ADAM_B1) * g
    v = ADAM_B2 * v + (1.0 - ADAM_B2) * (g * g)
    m_hat = m / (1.0 - ADAM_B1 ** ADAM_STEP)
    v_hat = v / (1.0 - ADAM_B2 ** ADAM_STEP)
    delta = -ADAM_LR * (m_hat / (jnp.sqrt(v_hat) + ADAM_EPS) + ADAM_WD * w)
    return delta, m, v


def _adam_big(w, parts, m, v, name, after=None):
    L, R, C = w.shape
    P = parts[0].shape[0]
    tr = _tile(R, (256, 128, 176, 64, 32, 16))
    nr = R // tr

    def body(w_ref, *refs):
        p_refs, (m_ref, v_ref, g_ref, d_ref, mo_ref, vo_ref) = refs[:L], refs[L:]
        for l in range(L):
            @pl.when(pl.program_id(0) == l)
            def _(p_ref=p_refs[l]):
                g = p_ref[0].astype(F32)
                for s in range(1, P):
                    g = g + p_ref[s].astype(F32)
                g_ref[...] = g
                d_ref[...], mo_ref[...], vo_ref[...] = _adamw(w_ref[...], g, m_ref[...], v_ref[...])

    row = pl.BlockSpec((None, tr, C), lambda l, i: (l, i, 0))
    park = lambda l_of: (lambda l, i: (0, jnp.where(l == l_of, i, 0 if l_of else nr - 1), 0))
    return _call(
        after, body, grid=(L, nr),
        in_specs=[row] + [pl.BlockSpec((P, tr, C), park(l)) for l in range(L)] + [row, row],
        out_specs=[row] * 4, out_shape=[_sds((L, R, C), F32)] * 4,
        compiler_params=_params(("arbitrary", "arbitrary")), name=name)(w, *parts, m, v)


SMALL = (("a_lower_bound", (2, 128), (2, D)), ("ln_gain", (3, 2, 128), (6, D)), ("ln_bias", (3, 2, 128), (6, D)),
         ("a_norm_gain", (1, 128), (1, 128)), ("kv_b", (1, 512), (1, 512)), ("b_b_q", (1, D), (1, D)),
         ("b_sinks", (1, ATT_QH), (1, 128)), ("b_b_out", (1, D), (1, D)), ("ple_b_gate", (2, D), (2, D)))


def _adam_small(parts, w, m, v, losses, after=None):
    k = len(SMALL)

    def body(*refs):
        p_refs, w_refs, m_refs, v_refs = refs[:k], refs[k:2 * k], refs[2 * k:3 * k], refs[3 * k:4 * k]
        loss_ref, outs, total_ref = refs[4 * k], refs[4 * k + 1:-1], refs[-1]
        total = loss_ref[0]
        for s in range(1, N_DEV):
            total = total + loss_ref[s]
        total_ref[...] = total
        me = _my_index()
        for i, (_, wshape, pshape) in enumerate(SMALL):
            cols = wshape[-1]
            lanes = slice(None) if cols == pshape[1] else (
                pl.ds(0, cols) if cols < 128 else pl.ds(pl.multiple_of(me * cols, cols), cols))
            at = [(slice(None), slice(None))] if len(wshape) == 2 else [
                (pl.ds(l * wshape[0] + kk, 1), (kk, pl.ds(l, 1), slice(None))) for kk in range(wshape[0]) for l in range(wshape[1])]
            for rows, own in at:
                g = p_refs[i][0, rows, lanes]
                for s in range(1, N_DEV):
                    g = g + p_refs[i][s, rows, lanes]
                g_ref, d_ref, mo_ref, vo_ref = outs[4 * i:4 * i + 4]
                g_ref[own] = g
                d_ref[own], mo_ref[own], vo_ref[own] = _adamw(w_refs[i][own], g, m_refs[i][own], v_refs[i][own])

    full = lambda shape: pl.BlockSpec(shape, lambda: (0,) * len(shape))
    names = [n for n, _, _ in SMALL]
    held = lambda a, ws: a.swapaxes(0, 1) if len(ws) == 3 else a.reshape(ws)
    back = lambda r, n: r.swapaxes(0, 1) if r.ndim == 3 else r.reshape(w[n].shape)
    res = _call(
        after, body,
        in_specs=[full((N_DEV,) + ps) for _, _, ps in SMALL] + [full(ws) for _, ws, _ in SMALL] * 3
        + [full((N_DEV, 1, 128))],
        out_specs=[full(ws) for _, ws, _ in SMALL for _ in range(4)] + [full((1, 128))],
        out_shape=[_sds(ws, F32) for _, ws, _ in SMALL for _ in range(4)] + [_sds((1, 128), F32)], name="adam_small")(
            *[parts[n] for n in names], *[held(a[n], ws) for a in (w, m, v) for n, ws, _ in SMALL], losses)
    return {n: [back(r, n) for r in res[4 * i:4 * i + 4]] for i, n in enumerate(names)}, res[-1][0, 0]


WEIGHTS = ("a_w_in", "a_lower_bound", "a_norm_gain", "a_w_out", "kv_w", "kv_b", "b_w_q", "b_b_q", "b_sinks", "b_w_out",
           "b_b_out", "ffn_w_gate_up", "ffn_w_down", "ple_w_up", "ple_w_gate", "ple_b_gate", "ln_gain", "ln_bias")


GATHER_GROUPS = (("a_w_in",), ("a_w_out", "gu0"), ("dn0", "pu0", "pg0"), ("kv_w", "b_w_q", "b_w_out"), ("gu1",),
                 ("dn1", "pu1", "pg1"))
KERNEL_LAYOUT = {
    "a_w_in": lambda a: a,
    "a_w_out": lambda a: a.reshape(D, D),
    "kv_w": lambda a: a.reshape(D, 2 * ATT_KVH * ATT_HD),
    "b_w_q": lambda a: a.reshape(D, D),
    "b_w_out": lambda a: a.reshape(D, D),
    "gu": lambda a: a.reshape(2, 4, FFN_B, D),
    "dn": lambda a: a.reshape(4, FFN_B, D),
    "pu": lambda a: a,
    "pg": lambda a: a.reshape(D, D),
}
_row_blocks = lambda a: a.reshape(N_DEV, -1, a.shape[-1])
OWNER_BLOCKS = {
    "a_w_in": lambda g: g,
    "a_w_out": _row_blocks,
    "kv_w": _row_blocks,
    "b_w_q": _row_blocks,
    "b_w_out": _row_blocks,
    "gu": lambda g: g,
    "dn": lambda g: _row_blocks(g.reshape(FFN_H, D)),
    "pu": lambda g: g.reshape(PLE_DIM, N_DEV, 128).transpose(1, 0, 2),
    "pg": _row_blocks,
}
ADAM_PARTS = (("kv_w", ("kv_w",)), ("b_w_q", ("b_w_q",)), ("b_w_out", ("b_w_out",)), ("ffn_w_gate_up", ("gu0", "gu1")),
              ("ffn_w_down", ("dn0", "dn1")), ("ple_w_up", ("pu0", "pu1")), ("ple_w_gate", ("pg0", "pg1")),
              ("a_w_out", ("a_w_out",)), ("a_w_in", ("a_w_in",)))


def kernel(x, p, a_w_in, a_lower_bound, a_norm_gain, a_w_out, kv_w, kv_b, b_w_q, b_b_q, b_sinks, b_w_out, b_b_out, ffn_w_gate_up, ffn_w_down, ple_w_up, ple_w_gate, ple_b_gate, ln_gain, ln_bias, loss_target, m_a_w_in, m_a_lower_bound, m_a_norm_gain, m_a_w_out, m_kv_w, m_kv_b, m_b_w_q, m_b_b_q, m_b_sinks, m_b_w_out, m_b_b_out, m_ffn_w_gate_up, m_ffn_w_down, m_ple_w_up, m_ple_w_gate, m_ple_b_gate, m_ln_gain, m_ln_bias, v_a_w_in, v_a_lower_bound, v_a_norm_gain, v_a_w_out, v_kv_w, v_kv_b, v_b_w_q, v_b_b_q, v_b_sinks, v_b_w_out, v_b_b_out, v_ffn_w_gate_up, v_ffn_w_down, v_ple_w_up, v_ple_w_gate, v_ple_b_gate, v_ln_gain, v_ln_bias):
    given = dict(locals())
    w = {n: given[n] for n in WEIGHTS}
    m = {n: given["m_" + n] for n in WEIGHTS}
    v = {n: given["v_" + n] for n in WEIGHTS}
    shards = {"a_w_in": a_w_in[0], "a_w_out": a_w_out[0], "kv_w": kv_w, "b_w_q": b_w_q[0], "b_w_out": b_w_out[0]}
    for l in range(2):
        shards.update({f"gu{l}": ffn_w_gate_up[l].T, f"dn{l}": ffn_w_down[l], f"pu{l}": ple_w_up[l], f"pg{l}": ple_w_gate[l]})
    sharded_small = [a_lower_bound, ln_gain.swapaxes(0, 1), ln_bias.swapaxes(0, 1)]
    gathered = {}
    for gi, g in enumerate(GATHER_GROUPS):
        lands = _sequencer_gather([shards[n].astype(CDT) for n in g] + (sharded_small if gi == 0 else []),
                                  f"gather{gi}", gi)
        for n, a in zip(g, lands):
            gathered[n] = KERNEL_LAYOUT[n.rstrip("01")](a)
        if gi == 0:
            alb = lands[len(g)].transpose(1, 0, 2).reshape(2, D)
            lng, lnb = [a.transpose(2, 1, 0, 3).reshape(2, 3, D) for a in lands[len(g) + 1:]]

    getw = gathered.__getitem__

    sm = {"a_lower_bound": alb, "ln_gain": lng, "ln_bias": lnb,
          "a_norm_gain": a_norm_gain, "kv_b": kv_b, "b_b_q": b_b_q[0], "b_sinks": b_sinks, "b_b_out": b_b_out,
          "ple_b_gate": ple_b_gate}

    scatters, small_parts = [], {}

    def emit(grads, small=None):
        names = list(grads)
        blocks = [OWNER_BLOCKS[n.rstrip("01")](grads[n]) for n in names]
        partials = [] if small is None else [small[n].reshape(ps) for n, _, ps in SMALL] + [small["loss"]]
        lands = _sequencer_exchange(blocks + partials, ["scatter"] * len(blocks) + ["gather"] * len(partials),
                                    f"scatter{len(scatters)}", len(GATHER_GROUPS) + len(scatters))
        scatters.append(dict(zip(names, lands)))
        small_parts.update(zip([n for n, _, _ in SMALL] + ["loss"], lands[len(blocks):]))
        return blocks + (list(scatters[-4].values()) if len(scatters) >= 4 else [])

    loss, grad_x, gs = _local_step(x[0], p[:, 0], loss_target[0], getw, sm, emit)

    out, parts, last = {}, {}, [grad_x]
    for landed in scatters:
        parts.update(landed)
        for n, keys in ADAM_PARTS:
            if n in out or not all(key in parts for key in keys):
                continue
            lrc = (1,) * (3 - w[n].ndim) + w[n].shape
            shard = (lambda a: a.reshape(lrc).swapaxes(1, 2)) if n == "ffn_w_gate_up" else (lambda a: a.reshape(lrc))
            res = _adam_big(shard(w[n]), [parts[key] for key in keys], shard(m[n]), shard(v[n]), "adam_" + n, after=last)
            out[n] = [(r.swapaxes(1, 2) if n == "ffn_w_gate_up" else r).reshape(w[n].shape) for r in res]
            last = [res[3]]
    small_out, loss = _adam_small(small_parts, w, m, v, small_parts["loss"], after=last)
    out.update(small_out)
    res = [loss, grad_x[None]]
    for i in range(4):
        res += [out[n][i] for n in WEIGHTS]
    return tuple(res)
```

```python
import jax
import jax.numpy as jnp
from jax import lax
from jax.experimental import pallas as pl
from jax.experimental.pallas import tpu as pltpu
from jax.experimental.pallas import tpu_sc as plsc

F32 = jnp.float32
CDT = jnp.bfloat16

N_DEV = 8
D = 1024
HG_H, HG_DK, HG_CH = 8, 128, 64
HG_HPB = 4
HG_CPB = 8
ATT_HD, ATT_QH, ATT_KVH, ATT_G, WINDOW = 64, 16, 4, 4, 128
ATT_BPB = 1
FFN_H = 2816
FFN_B = FFN_H // 4
PLE_DIM = 256
ALPHA = (2.0 * 2) ** 0.25
LN_EPS = 1e-5
RMS_EPS = 1e-6
ADAM_LR, ADAM_B1, ADAM_B2, ADAM_EPS, ADAM_WD, ADAM_STEP = 0.001, 0.9, 0.999, 1e-08, 0.01, 10
ROW_TILES = (512, 256, 128, 64)
VMEM_LIMIT = 56 * 1024 * 1024
NEG = -1e30

MESH = pl.DeviceIdType.MESH


def _tile(n, cands=ROW_TILES):
    for t in cands:
        if n % t == 0:
            return t
    return n


def _sds(shape, dtype):
    return jax.ShapeDtypeStruct(tuple(shape), dtype)


def _params(sem):
    return pltpu.CompilerParams(dimension_semantics=sem, vmem_limit_bytes=VMEM_LIMIT)


def _dot(a, b):
    return jnp.dot(a.astype(CDT), b.astype(CDT), preferred_element_type=F32)


def _dot_nt(a, b):
    return lax.dot_general(a.astype(CDT), b.astype(CDT), (((1,), (1,)), ((), ())), preferred_element_type=F32)


def _dot_tn(a, b):
    return lax.dot_general(a.astype(CDT), b.astype(CDT), (((0,), (0,)), ((), ())), preferred_element_type=F32)


def _sigmoid(x):
    return jax.nn.sigmoid(x)


def _ln_fwd(z, g, b):
    mu = jnp.mean(z, axis=-1, keepdims=True)
    zc = z - mu
    var = jnp.mean(zc * zc, axis=-1, keepdims=True)
    return zc * lax.rsqrt(var + LN_EPS) * g + b


def _ln_bwd(z, g, dy):
    mu = jnp.mean(z, axis=-1, keepdims=True)
    zc = z - mu
    var = jnp.mean(zc * zc, axis=-1, keepdims=True)
    rstd = lax.rsqrt(var + LN_EPS)
    xhat = zc * rstd
    dxh = dy * g
    dz = rstd * (dxh - jnp.mean(dxh, axis=-1, keepdims=True) - xhat * jnp.mean(dxh * xhat, axis=-1, keepdims=True))
    return dz, xhat


def _colsum(x):
    return jnp.sum(x, axis=0, keepdims=True)


def _acc(ref, val, first):
    @pl.when(first)
    def _():
        ref[...] = val

    @pl.when(jnp.logical_not(first))
    def _():
        ref[...] += val


def _zero_at(ref, first):
    @pl.when(first)
    def _():
        ref[...] = jnp.zeros_like(ref)


def _call(after, body, **kw):
    after = [] if after is None else list(after)
    specs = list(kw["in_specs"])
    kw["in_specs"] = [pl.BlockSpec(memory_space=pl.ANY)] * len(after) + specs

    def ordered_body(*refs):
        body(*refs[len(after):])

    call = pl.pallas_call(ordered_body, **kw)
    return lambda *args: call(*after, *args)


def _mm_tn(a3, b3, G, amap, bmap, ablock, bblock, out_shape, oblock, omap, name="mm_tn"):
    S = a3.shape[1]

    def body(a_ref, b_ref, o_ref):
        o_ref[...] = _dot_tn(a_ref[...], b_ref[...]).astype(o_ref.dtype)

    return _call(
        None, body, grid=(G, 1),
        in_specs=[pl.BlockSpec(ablock(S), amap), pl.BlockSpec(bblock(S), bmap)],
        out_specs=pl.BlockSpec(oblock, omap), out_shape=_sds(out_shape, CDT),
        compiler_params=_params(("arbitrary", "arbitrary")), name=name)(a3, b3)


def _wgrad(a3, b3, name):
    Ga, S, M = a3.shape
    Gb, _, N = b3.shape
    G = max(Ga, Gb)
    return _mm_tn(
        a3, b3, G,
        (lambda g, k: (g, k, 0)) if Ga > 1 else (lambda g, k: (0, k, 0)),
        (lambda g, k: (g, k, 0)) if Gb > 1 else (lambda g, k: (0, k, 0)),
        lambda tk: (None, tk, M), lambda tk: (None, tk, N),
        (G, M, N), (None, M, N), lambda g, k: (g, 0, 0), name=name)


def _mixout_ln(u3, w3, bias, xin, gain, beta, name):
    G, S, Kb = u3.shape
    tm = _tile(S)

    def body(u_ref, w_ref, b_ref, x_ref, g_ref, be_ref, z_ref, xo_ref, xob_ref):
        h = b_ref[...] + _dot(u_ref[0], w_ref[0])
        for g in range(1, G):
            h = h + _dot(u_ref[g], w_ref[g])
        z = ALPHA * x_ref[...] + h
        z_ref[...] = z
        y = _ln_fwd(z, g_ref[...], be_ref[...])
        xo_ref[...] = y
        xob_ref[...] = y.astype(CDT)

    row = pl.BlockSpec((tm, D), lambda i: (i, 0))
    vec = pl.BlockSpec((1, D), lambda i: (0, 0))
    return _call(
        None, body, grid=(S // tm,),
        in_specs=[pl.BlockSpec((G, tm, Kb), lambda i: (0, i, 0)), pl.BlockSpec((G, Kb, D), lambda i: (0, 0, 0)),
                  vec, row, vec, vec],
        out_specs=[row, row, row], out_shape=[_sds((S, D), F32), _sds((S, D), F32), _sds((S, D), CDT)],
        compiler_params=_params(("arbitrary",)), name=name)(u3, w3, bias, xin, gain, beta)


def _ffn_fwd(xin, xin_b, wgu, wdn, gain, beta, name):
    S = xin.shape[0]
    tm = _tile(S)

    def hidden(xb_ref, wgu_ref, gu_ref, hid_ref):
        xb = xb_ref[...]
        gate = _dot_nt(xb, wgu_ref[0])
        up = _dot_nt(xb, wgu_ref[1])
        gu_ref[0] = gate.astype(CDT)
        gu_ref[1] = up.astype(CDT)
        hid_ref[...] = (gate * _sigmoid(gate) * up).astype(CDT)

    gu, hid = _call(
        None, hidden, grid=(4, S // tm),
        in_specs=[pl.BlockSpec((tm, D), lambda j, i: (i, 0)), pl.BlockSpec((2, None, FFN_B, D), lambda j, i: (0, j, 0, 0))],
        out_specs=[pl.BlockSpec((2, None, tm, FFN_B), lambda j, i: (0, j, i, 0)),
                   pl.BlockSpec((None, tm, FFN_B), lambda j, i: (j, i, 0))],
        out_shape=[_sds((2, 4, S, FFN_B), CDT), _sds((4, S, FFN_B), CDT)],
        compiler_params=_params(("arbitrary", "arbitrary")), name=name + "_hidden")(xin_b, wgu)

    def down(x_ref, hid_ref, wdn_ref, g_ref, be_ref, z_ref, xo_ref, xob_ref):
        z = ALPHA * x_ref[...]
        for j in range(4):
            z = z + _dot(hid_ref[j], wdn_ref[j])
        z_ref[...] = z
        y = _ln_fwd(z, g_ref[...], be_ref[...])
        xo_ref[...] = y
        xob_ref[...] = y.astype(CDT)

    row = pl.BlockSpec((tm, D), lambda i: (i, 0))
    vec = pl.BlockSpec((1, D), lambda i: (0, 0))
    z, xo, xob = _call(
        None, down, grid=(S // tm,),
        in_specs=[row, pl.BlockSpec((4, tm, FFN_B), lambda i: (0, i, 0)), pl.BlockSpec((4, FFN_B, D), lambda i: (0, 0, 0)),
                  vec, vec],
        out_specs=[row, row, row], out_shape=[_sds((S, D), F32), _sds((S, D), F32), _sds((S, D), CDT)],
        compiler_params=_params(("arbitrary",)), name=name + "_down")(xin, hid, wdn, gain, beta)
    return gu, hid, z, xo, xob


def _ple_fwd(xin, xin_b, p_b, wpg, bgate, wpu, gain, beta, name, target=None):
    S = xin.shape[0]
    tm = _tile(S)

    def body(x_ref, xb_ref, p_ref, wpg_ref, bg_ref, wpu_ref, g_ref, be_ref, *rest):
        if target is None:
            sg_ref, up_ref, z_ref, xo_ref, xob_ref = rest
        else:
            t_ref, sg_ref, up_ref, z_ref, dy_ref, l_ref = rest
        sg =_sigmoid(_dot(xb_ref[...], wpg_ref[...]) + bg_ref[...])
        pb = p_ref[...]
        up = jnp.concatenate([_dot(pb, wpu_ref[j]) for j in range(N_DEV)], axis=-1)
        sg_ref[...] = sg.astype(CDT)
        up_ref[...] = (up * sg * (1.0 - sg)).astype(CDT)
        z = ALPHA * x_ref[...] + sg * up
        z_ref[...] = z
        y = _ln_fwd(z, g_ref[...], be_ref[...])
        if target is None:
            xo_ref[...] = y
            xob_ref[...] = y.astype(CDT)
        else:
            e = y - t_ref[...]
            dy_ref[...] = e * (1.0 / D)
            part = 0.5 * jnp.sum(jnp.sum(e * e, axis=-1, keepdims=True) * (1.0 / D), axis=0, keepdims=True)
            _acc(l_ref, jnp.broadcast_to(part, l_ref.shape), pl.program_id(0) == 0)

    row = pl.BlockSpec((tm, D), lambda i: (i, 0))
    vec = pl.BlockSpec((1, D), lambda i: (0, 0))
    last = target is not None
    return _call(
        None, body, grid=(S // tm,),
        in_specs=[row, row, pl.BlockSpec((tm, PLE_DIM), lambda i: (i, 0)), pl.BlockSpec((D, D), lambda i: (0, 0)), vec,
                  pl.BlockSpec((N_DEV, PLE_DIM, D // N_DEV), lambda i: (0, 0, 0)), vec, vec] + [row] * last,
        out_specs=[row] * 4 + [pl.BlockSpec((1, 128), lambda i: (0, 0)) if last else row],
        out_shape=[_sds((S, D), CDT)] * 2 + [_sds((S, D), F32)] * 2 + [_sds((1, 128), F32) if last else _sds((S, D), CDT)],
        compiler_params=_params(("arbitrary",)), name=name)(
            xin, xin_b, p_b, wpg, bgate, wpu, gain, beta, *([target] if last else []))


def _ple_bwd(dy, z, sg, up, gain, wpg, z_ffn, gain_ffn, name, after=None):
    S = dy.shape[0]
    tm = _tile(S)

    def body(dy_ref, z_ref, sg_ref, up_ref, g_ref, wpg_ref, zf_ref, gf_ref, dzf_ref, dzfb_ref, dgl_ref, dup_ref,
             dgain_ref, dbeta_ref, dbg_ref, dgainf_ref, dbetaf_ref):
        first = pl.program_id(0) == 0
        dy_ = dy_ref[...]
        dz, xhat = _ln_bwd(z_ref[...], g_ref[...], dy_)
        dgl = dz * up_ref[...].astype(F32)
        dgl_ref[...] = dgl.astype(CDT)
        dup_ref[...] = (dz * sg_ref[...].astype(F32)).astype(CDT)
        dx = ALPHA * dz + _dot_nt(dgl, wpg_ref[...])
        dzf, xhatf = _ln_bwd(zf_ref[...], gf_ref[...], dx)
        dzf_ref[...] = dzf
        dzfb_ref[...] = dzf.astype(CDT)
        _acc(dgain_ref, _colsum(dy_ * xhat), first)
        _acc(dbeta_ref, _colsum(dy_), first)
        _acc(dbg_ref, _colsum(dgl), first)
        _acc(dgainf_ref, _colsum(dx * xhatf), first)
        _acc(dbetaf_ref, _colsum(dx), first)

    row = pl.BlockSpec((tm, D), lambda i: (i, 0))
    vec = pl.BlockSpec((1, D), lambda i: (0, 0))
    return _call(
        after, body, grid=(S // tm,),
        in_specs=[row, row, row, row, vec, pl.BlockSpec((D, D), lambda i: (0, 0)), row, vec],
        out_specs=[row, row, row, row] + [vec] * 5,
        out_shape=[_sds((S, D), F32)] + [_sds((S, D), CDT)] * 3 + [_sds((1, D), F32)] * 5,
        compiler_params=_params(("arbitrary",)), name=name)(dy, z, sg, up, gain, wpg, z_ffn, gain_ffn)


def _ffn_bwd_hidden(dzb, gu, wdn, name, after=None):
    S = dzb.shape[0]
    tm = _tile(S)

    def hidden(dzb_ref, gu_ref, wdn_ref, dgu_ref):
        dhid = _dot_nt(dzb_ref[...], wdn_ref[...])
        gate, up = gu_ref[0].astype(F32), gu_ref[1].astype(F32)
        sg = _sigmoid(gate)
        dgu_ref[0] = (dhid * up * (sg * (1.0 + gate * (1.0 - sg)))).astype(CDT)
        dgu_ref[1] = (dhid * (gate * sg)).astype(CDT)

    blocks = pl.BlockSpec((2, None, tm, FFN_B), lambda j, i: (0, j, i, 0))
    return _call(
        after, hidden, grid=(4, S // tm),
        in_specs=[pl.BlockSpec((tm, D), lambda j, i: (i, 0)), blocks, pl.BlockSpec((None, FFN_B, D), lambda j, i: (j, 0, 0))],
        out_specs=blocks, out_shape=_sds((2, 4, S, FFN_B), CDT),
        compiler_params=_params(("arbitrary", "arbitrary")), name=name + "_hidden")(dzb, gu, wdn)


def _ffn_bwd_input(dz, dgu, wgu, name, after=None):
    S = dz.shape[0]
    tm = _tile(S)

    def to_input(dz_ref, dgu_ref, wgu_ref, dx_ref):
        acc = ALPHA * dz_ref[...]
        for g in range(2):
            for j in range(4):
                acc = acc + _dot(dgu_ref[g, j], wgu_ref[g, j])
        dx_ref[...] = acc

    rows = pl.BlockSpec((tm, D), lambda i: (i, 0))
    return _call(
        after, to_input, grid=(S // tm,),
        in_specs=[rows, pl.BlockSpec((2, 4, tm, FFN_B), lambda i: (0, 0, i, 0)),
                  pl.BlockSpec((2, 4, FFN_B, D), lambda i: (0, 0, 0, 0))],
        out_specs=rows, out_shape=_sds((S, D), F32),
        compiler_params=_params(("arbitrary",)), name=name + "_input")(dz, dgu, wgu)


def _mixout_bwd(dy, z, gain, w3, du_dtype, name, after=None):
    S = dy.shape[0]
    G, Kb, _ = w3.shape
    tm = _tile(S)

    def body(dy_ref, z_ref, g_ref, w_ref, dz_ref, dzb_ref, du_ref, dgain_ref, dbeta_ref, dbias_ref):
        first = pl.program_id(0) == 0
        dy_ = dy_ref[...]
        dz, xhat = _ln_bwd(z_ref[...], g_ref[...], dy_)
        dz_ref[...] = dz
        dzb = dz.astype(CDT)
        dzb_ref[...] = dzb
        for g in range(G):
            du_ref[g] = _dot_nt(dzb, w_ref[g]).astype(du_ref.dtype)
        _acc(dgain_ref, _colsum(dy_ * xhat), first)
        _acc(dbeta_ref, _colsum(dy_), first)
        _acc(dbias_ref, _colsum(dz), first)

    row = pl.BlockSpec((tm, D), lambda i: (i, 0))
    vec = pl.BlockSpec((1, D), lambda i: (0, 0))
    return _call(
        after, body, grid=(S // tm,), in_specs=[row, row, vec, pl.BlockSpec((G, Kb, D), lambda i: (0, 0, 0))],
        out_specs=[row, row, pl.BlockSpec((G, tm, Kb), lambda i: (0, i, 0)), vec, vec, vec],
        out_shape=[_sds((S, D), F32), _sds((S, D), CDT), _sds((G, S, Kb), du_dtype)] + [_sds((1, D), F32)] * 3,
        compiler_params=_params(("arbitrary",)), name=name)(dy, z, gain, w3)


def _half_select(low):
    r = lax.broadcasted_iota(jnp.int32, (2 * ATT_HD, ATT_HD), 0)
    c = lax.broadcasted_iota(jnp.int32, (2 * ATT_HD, ATT_HD), 1)
    return (r == c + (0 if low else ATT_HD)).astype(CDT)


def _half_place(low):
    r = lax.broadcasted_iota(jnp.int32, (ATT_HD, 2 * ATT_HD), 0)
    c = lax.broadcasted_iota(jnp.int32, (ATT_HD, 2 * ATT_HD), 1)
    return (c == r + (0 if low else ATT_HD)).astype(CDT)


def _pair_lanes(even, odd):
    return (jnp.dot(even, _half_place(True), preferred_element_type=F32)
            + jnp.dot(odd, _half_place(False), preferred_element_type=F32)).astype(CDT)


def _proj_heads(a, w, bias, heads, name):
    S, K = a.shape
    N = heads * ATT_HD
    tm = _tile(S)

    def body(a_ref, w_ref, b_ref, o_ref):
        acc = (_dot(a_ref[...], w_ref[...]) + b_ref[...]).astype(CDT)
        sel = (_half_select(True), _half_select(False))
        for h in range(heads):
            pair = acc[:, (h // 2) * 2 * ATT_HD:(h // 2 + 1) * 2 * ATT_HD]
            o_ref[h] = jnp.dot(pair, sel[h % 2], preferred_element_type=F32).astype(CDT)

    return _call(
        None, body, grid=(S // tm,),
        in_specs=[pl.BlockSpec((tm, K), lambda i: (i, 0)), pl.BlockSpec((K, N), lambda i: (0, 0)),
                  pl.BlockSpec((1, N), lambda i: (0, 0))],
        out_specs=pl.BlockSpec((heads, tm, ATT_HD), lambda i: (0, i, 0)), out_shape=_sds((heads, S, ATT_HD), CDT),
        compiler_params=_params(("arbitrary",)), name=name)(a, w, bias)


def _qkv_bwd(dz, dq, dkv4, wq, wkv, name, after=None):
    S = dz.shape[0]
    tm = _tile(S)
    HK = dkv4.shape[0]
    NK = HK * ATT_HD

    def body(dz_ref, dq_ref, dkv_ref, wq_ref, wkv_ref, dx_ref, dkvn_ref, dkvb_ref):
        first = pl.program_id(0) == 0
        dkvn = jnp.concatenate([_pair_lanes(dkv_ref[2 * i].astype(CDT), dkv_ref[2 * i + 1].astype(CDT))
                                for i in range(HK // 2)], axis=-1)
        dkvn_ref[...] = dkvn
        dx_ref[...] = ALPHA * dz_ref[...] + _dot_nt(dq_ref[...], wq_ref[...]) + _dot_nt(dkvn, wkv_ref[...])
        for h in range(HK):
            _acc(dkvb_ref.at[h], _colsum(dkv_ref[h]), first)

    row = pl.BlockSpec((tm, D), lambda i: (i, 0))
    return _call(
        after, body, grid=(S // tm,),
        in_specs=[row, row, pl.BlockSpec((HK, tm, ATT_HD), lambda i: (0, i, 0)),
                  pl.BlockSpec((D, D), lambda i: (0, 0)), pl.BlockSpec((D, NK), lambda i: (0, 0))],
        out_specs=[row, pl.BlockSpec((tm, NK), lambda i: (i, 0)), pl.BlockSpec((HK, 1, ATT_HD), lambda i: (0, 0, 0))],
        out_shape=[_sds((S, D), F32), _sds((S, NK), CDT), _sds((HK, 1, ATT_HD), F32)],
        compiler_params=_params(("arbitrary",)), name=name)(dz, dq, dkv4, wq, wkv)


def _inproj_fwd(xb, wain, name):
    S = xb.shape[0]
    tm = _tile(S) // 2
    nb = wain.shape[-1]

    def body(x_ref, w_ref, o_ref):
        x = x_ref[...]
        for j in range(N_DEV):
            o_ref[j // 2, :, pl.ds((j % 2) * nb, nb)] = _dot(x, w_ref[j])

    return _call(
        None, body, grid=(S // tm,),
        in_specs=[pl.BlockSpec((tm, D), lambda i: (i, 0)), pl.BlockSpec((N_DEV, D, nb), lambda i: (0, 0, 0))],
        out_specs=pl.BlockSpec((4, tm, D), lambda i: (0, i, 0)), out_shape=_sds((4, S, D), F32),
        compiler_params=_params(("arbitrary",)), name=name)(xb, wain)


def _inproj_bwd(dz, dproj, wain, name, after=None):
    S = dz.shape[0]
    tm = _tile(S)
    nb = wain.shape[-1]

    def body(dz_ref, dp_ref, w_ref, dx_ref):
        acc = ALPHA * dz_ref[...]
        for j in range(N_DEV):
            acc = acc + _dot_nt(dp_ref[j // 2, :, pl.ds((j % 2) * nb, nb)], w_ref[j])
        dx_ref[...] = acc

    row = pl.BlockSpec((tm, D), lambda i: (i, 0))
    return _call(
        after, body, grid=(S // tm,),
        in_specs=[row, pl.BlockSpec((4, tm, D), lambda i: (0, i, 0)), pl.BlockSpec((N_DEV, D, nb), lambda i: (0, 0, 0))],
        out_specs=row, out_shape=_sds((S, D), F32),
        compiler_params=_params(("arbitrary",)), name=name)(dz, dproj, wain)


def _running_sum(x, reverse=False):
    rows = x.shape[0]
    row = lax.broadcasted_iota(jnp.int32, x.shape, 0)
    step = 1
    while step < rows:
        if reverse:
            x = x + jnp.where(row < rows - step, pltpu.roll(x, rows - step, 0), 0.0)
        else:
            x = x + jnp.where(row >= step, pltpu.roll(x, step, 0), 0.0)
        step *= 2
    return x


def _hg_gates(q, f, alb_ref):
    a0, a1 = alb_ref[0:1, :], alb_ref[1:2, :]
    mx = jnp.maximum(a0, a1)
    e0, e1 = jnp.exp(a0 - mx), jnp.exp(a1 - mx)
    lb = e0 / (e0 + e1)
    sig = _sigmoid(f)
    forget = lb + (1.0 - lb) * sig
    k = (1.0 - lb) * _sigmoid(-f)
    qs = q * _sigmoid(q) * (HG_DK ** -0.5)
    return qs, k, jnp.log(forget), sig, lb, forget


def _hg_intra(qs, k, b, b_scr):
    b_scr[...] = b
    bm = b_scr[pl.ds(HG_CH // 2 - 1, 1), :]
    bl = b_scr[pl.ds(HG_CH - 1, 1), :]
    eb = jnp.exp(b)
    qb = qs * eb
    e_q = jnp.exp(b - bm)
    e_k = jnp.exp(bm - b)
    e_d = jnp.exp(bl - b)
    return qb, qs * e_q, k * e_k, k * e_d, jnp.exp(bl), eb, e_q, e_k, e_d


def _hgrn_fwd(proj, alb, ngain):
    S = proj.shape[1]
    nc = S // HG_CH
    nb = nc // HG_CPB
    rb, wb = HG_CPB * HG_CH, HG_HPB * HG_DK

    def body(pj_ref, alb_ref, ng_ref, o_ref, y_ref, st_ref, st_scr, b_scr):
        n = pl.program_id(1)

        @pl.when(n == 0)
        def _():
            st_scr[...] = jnp.zeros_like(st_scr)

        r = lax.broadcasted_iota(jnp.int32, (HG_CH, HG_CH), 0)
        c = lax.broadcasted_iota(jnp.int32, (HG_CH, HG_CH), 1)
        causal = r >= c
        for ci, j in [(ci, j) for ci in range(HG_CPB) for j in range(HG_HPB)]:
            rows, lanes = pl.ds(ci * HG_CH, HG_CH), pl.ds(j * HG_DK, HG_DK)
            q, f, v, g = pj_ref[0, rows, lanes], pj_ref[1, rows, lanes], pj_ref[2, rows, lanes], pj_ref[3, rows, lanes]
            qs, k, logf, _, _, _ = _hg_gates(q, f, alb_ref.at[:, lanes])
            b = _running_sum(logf)
            qb, qt, kt, kd, ebl, _, _, _, _ = _hg_intra(qs, k, b, b_scr.at[j, ci])
            st = st_scr[j]
            st_ref[j, ci] = st
            a = jnp.where(causal, _dot_nt(qt, kt), 0.0)
            o = _dot(a, v) + _dot_nt(qb, st)
            st_scr[j] = st * ebl + _dot_tn(v, kd)
            o_ref[rows, lanes] = o
            rinv = lax.rsqrt(jnp.mean(o * o, axis=-1, keepdims=True) + RMS_EPS)
            y_ref[rows, lanes] = (o * rinv * ng_ref[...] * (g * _sigmoid(g))).astype(CDT)

    blk = pl.BlockSpec((rb, wb), lambda h, n: (n, h))
    return _call(
        None, body, grid=(HG_H // HG_HPB, nb),
        in_specs=[pl.BlockSpec((4, rb, wb), lambda h, n: (0, n, h)), pl.BlockSpec((2, wb), lambda h, n: (0, h)),
                  pl.BlockSpec((1, HG_DK), lambda h, n: (0, 0))],
        out_specs=[blk, blk, pl.BlockSpec((HG_HPB, HG_CPB, HG_DK, HG_DK), lambda h, n: (h, n, 0, 0))],
        out_shape=[_sds((S, D), F32), _sds((S, D), CDT), _sds((HG_H, nc, HG_DK, HG_DK), F32)],
        scratch_shapes=[pltpu.VMEM((HG_HPB, HG_DK, HG_DK), F32), pltpu.VMEM((HG_HPB, HG_CPB, HG_CH, HG_DK), F32)],
        compiler_params=_params(("arbitrary", "arbitrary")), name="hgrn_fwd")(proj, alb, ngain)


def _hgrn_bwd(proj, alb, ngain, o, states, dy, after=None):
    S = proj.shape[1]
    nc = S // HG_CH
    nb = nc // HG_CPB
    rb, wb = HG_CPB * HG_CH, HG_HPB * HG_DK

    def body(pj_ref, alb_ref, ng_ref, o_ref, st_ref, dy_ref, dpj_ref, dalb_ref, dng_ref, dst_scr, b_scr):
        h, n = pl.program_id(0), pl.program_id(1)

        @pl.when(n == 0)
        def _():
            dst_scr[...] = jnp.zeros_like(dst_scr)
            dalb_ref[...] = jnp.zeros_like(dalb_ref)

        _zero_at(dng_ref, jnp.logical_and(h == 0, n == 0))
        ng = ng_ref[...]
        r = lax.broadcasted_iota(jnp.int32, (HG_CH, HG_CH), 0)
        c = lax.broadcasted_iota(jnp.int32, (HG_CH, HG_CH), 1)
        causal = r >= c
        dng = None
        for ci, j in [(ci, j) for ci in reversed(range(HG_CPB)) for j in range(HG_HPB)]:
            rows, lanes = pl.ds(ci * HG_CH, HG_CH), pl.ds(j * HG_DK, HG_DK)
            q, f, v, g = pj_ref[0, rows, lanes], pj_ref[1, rows, lanes], pj_ref[2, rows, lanes], pj_ref[3, rows, lanes]
            o_ = o_ref[rows, lanes]
            dy_ = dy_ref[rows, lanes]
            sg = _sigmoid(g)
            rinv = lax.rsqrt(jnp.mean(o_ * o_, axis=-1, keepdims=True) + RMS_EPS)
            nrm = o_ * rinv
            dr = dy_ * (g * sg)
            dg = dy_ * nrm * ng * (sg * (1.0 + g * (1.0 - sg)))
            dn = dr * ng
            do = rinv * (dn - nrm * jnp.mean(dn * nrm, axis=-1, keepdims=True))
            dng = _colsum(dr * nrm) if dng is None else dng + _colsum(dr * nrm)
            qs, k, logf, sig, lb, forget = _hg_gates(q, f, alb_ref.at[:, lanes])
            b = _running_sum(logf)
            qb, qt, kt, kd, ebl, eb, e_q, e_k, e_d = _hg_intra(qs, k, b, b_scr.at[j, ci])
            st = st_ref[j, ci]
            dstn = dst_scr[j]
            qt, kt, qb, kd = (t.astype(CDT).astype(F32) for t in (qt, kt, qb, kd))
            a = jnp.where(causal, _dot_nt(qt, kt), 0.0)
            da = jnp.where(causal, _dot_nt(do, v), 0.0)
            dv = _dot_tn(a, do) + _dot_nt(kd, dstn)
            dqb = _dot(do, st)
            dkd = _dot(v, dstn)
            dqt = _dot(da, kt)
            dkt = _dot_tn(da, qt)
            dbl = _colsum(dkd * kd) + ebl * _colsum(dstn * st)
            dst_scr[j] = dstn * ebl + _dot_tn(do, qb)
            dqs = dqt * e_q + dqb * eb
            dk = dkt * e_k + dkd * e_d
            db = dqt * qt + dqb * qb - dkt * kt - dkd * kd
            dlogf = _running_sum(db, reverse=True) + dbl
            dforget = dlogf / forget
            dsig = (1.0 - lb) * (dforget - dk)
            df = dsig * sig * (1.0 - sig)
            dlb = _colsum((dforget - dk) * (1.0 - sig))
            sq = _sigmoid(q)
            dq = dqs * (HG_DK ** -0.5) * (sq * (1.0 + q * (1.0 - sq)))
            dpj_ref[0, rows, lanes] = dq.astype(CDT)
            dpj_ref[1, rows, lanes] = df.astype(CDT)
            dpj_ref[2, rows, lanes] = dv.astype(CDT)
            dpj_ref[3, rows, lanes] = dg.astype(CDT)
            da0 = dlb * lb * (1.0 - lb)
            dalb_ref[pl.ds(0, 1), lanes] += da0
            dalb_ref[pl.ds(1, 1), lanes] -= da0
        dng_ref[...] += dng

    blk = pl.BlockSpec((rb, wb), lambda h, n: (nb - 1 - n, h))
    pj = pl.BlockSpec((4, rb, wb), lambda h, n: (0, nb - 1 - n, h))
    alb_blk = pl.BlockSpec((2, wb), lambda h, n: (0, h))
    ng_blk = pl.BlockSpec((1, HG_DK), lambda h, n: (0, 0))
    return _call(
        after, body, grid=(HG_H // HG_HPB, nb),
        in_specs=[pj, alb_blk, ng_blk, blk,
                  pl.BlockSpec((HG_HPB, HG_CPB, HG_DK, HG_DK), lambda h, n: (h, nb - 1 - n, 0, 0)), blk],
        out_specs=[pj, alb_blk, ng_blk],
        out_shape=[_sds((4, S, D), CDT), _sds((2, D), F32), _sds((1, HG_DK), F32)],
        scratch_shapes=[pltpu.VMEM((HG_HPB, HG_DK, HG_DK), F32), pltpu.VMEM((HG_HPB, HG_CPB, HG_CH, HG_DK), F32)],
        compiler_params=_params(("arbitrary", "arbitrary")), name="hgrn_bwd")(proj, alb, ngain, o, states, dy)


def _slope(h):
    return 2.0 ** (-8.0 * (h + 1) / ATT_QH)


def _attn_mask(n):
    qi = lax.broadcasted_iota(jnp.int32, (WINDOW, 2 * WINDOW), 0)
    si = lax.broadcasted_iota(jnp.int32, (WINDOW, 2 * WINDOW), 1)
    dist = qi - si + WINDOW
    valid = (dist >= 0) & (dist < WINDOW) & (n * WINDOW - WINDOW + si >= 0)
    return valid, dist.astype(F32)


def _attn_probs(qk, sink, slope, valid, distf):
    s = qk * (ATT_HD ** -0.5) - slope * distf
    s = jnp.where(valid, s, NEG)
    m = jnp.maximum(jnp.max(s, axis=-1, keepdims=True), sink)
    e = jnp.exp(s - m)
    es = jnp.exp(sink - m)
    inv = 1.0 / (jnp.sum(e, axis=-1, keepdims=True) + es)
    return e * inv, es * inv


def _attn_specs(S):
    steps = S // (ATT_BPB * WINDOW)
    cur = lambda H: pl.BlockSpec((H, ATT_BPB * WINDOW, ATT_HD), lambda n: (0, n, 0))
    prev = lambda H: pl.BlockSpec((H, WINDOW, ATT_HD), lambda n: (0, jnp.maximum(ATT_BPB * n - 1, 0), 0))
    return steps, cur, prev


def _attn_kv(kvc_ref, kvp_ref, head, bi):
    before = kvp_ref[head] if bi == 0 else kvc_ref[head, pl.ds((bi - 1) * WINDOW, WINDOW), :]
    return jnp.concatenate([before, kvc_ref[head, pl.ds(bi * WINDOW, WINDOW), :]], axis=0)


def _attn_fwd(q4, kv4, sinks):
    S = q4.shape[1]
    nb, cur, prev = _attn_specs(S)

    def body(sink_ref, q_ref, kvc_ref, kvp_ref, o_ref, p_ref, ps_ref):
        lane = lax.broadcasted_iota(jnp.int32, (1, 128), 1)
        place = (_half_place(True), _half_place(False))
        masks = [_attn_mask(pl.program_id(0) * ATT_BPB + bi) for bi in range(ATT_BPB)]
        sink_probs = [jnp.zeros((WINDOW, 128), F32) for _ in range(ATT_BPB)]
        heads = lambda kvh: range(kvh * ATT_G, (kvh + 1) * ATT_G)

        def scores(bi, kvh):
            rows = pl.ds(bi * WINDOW, WINDOW)
            kh = _attn_kv(kvc_ref, kvp_ref, kvh, bi)
            vh = _attn_kv(kvc_ref, kvp_ref, ATT_KVH + kvh, bi)
            v_pl = [jnp.dot(vh, m, preferred_element_type=F32).astype(CDT) for m in place]
            return v_pl, [_dot_nt(q_ref[h, rows, :], kh) for h in heads(kvh)]

        def softmax(bi, kvh, qks):
            rows = pl.ds(bi * WINDOW, WINDOW)
            probs = []
            for h, qk in zip(heads(kvh), qks):
                p, ps = _attn_probs(qk, sink_ref[0, h], _slope(h), *masks[bi])
                probs.append(p.astype(CDT))
                p_ref[h, rows, :] = probs[-1]
                sink_probs[bi] = sink_probs[bi] + jnp.where(lane == h, ps, 0.0)
            return probs

        def weighted_values(bi, kvh, v_pl, probs):
            rows = pl.ds(bi * WINDOW, WINDOW)
            for i in range(ATT_G // 2):
                lanes = pl.ds((kvh * ATT_G + 2 * i) * ATT_HD, 2 * ATT_HD)
                o_ref[rows, lanes] = (_dot(probs[2 * i], v_pl[0]) + _dot(probs[2 * i + 1], v_pl[1])).astype(CDT)

        groups = [(bi, kvh) for bi in range(ATT_BPB) for kvh in range(ATT_KVH)]
        ahead = scores(*groups[0])
        for gi, g in enumerate(groups):
            v_pl, qks = ahead
            probs = softmax(*g, qks)
            if gi + 1 < len(groups):
                ahead = scores(*groups[gi + 1])
            weighted_values(*g, v_pl, probs)
        for bi in range(ATT_BPB):
            ps_ref[pl.ds(bi * WINDOW, WINDOW), :] = sink_probs[bi]

    rows_spec = pl.BlockSpec((ATT_BPB * WINDOW, D), lambda n: (n, 0))
    return _call(
        None, body, grid=(nb,),
        in_specs=[pl.BlockSpec(memory_space=pltpu.SMEM), cur(ATT_QH), cur(2 * ATT_KVH), prev(2 * ATT_KVH)],
        out_specs=[rows_spec, pl.BlockSpec((ATT_QH, ATT_BPB * WINDOW, 2 * WINDOW), lambda n: (0, n, 0)),
                   pl.BlockSpec((ATT_BPB * WINDOW, 128), lambda n: (n, 0))],
        out_shape=[_sds((S, D), CDT), _sds((ATT_QH, S, 2 * WINDOW), CDT), _sds((S, 128), F32)],
        compiler_params=_params(("arbitrary",)), name="attn_fwd")(sinks, q4, kv4, kv4)


def _attn_bwd(q4, kv4, probs, sink_probs, do):
    S = q4.shape[1]
    nb, cur, prev = _attn_specs(S)

    def body(q_ref, kvc_ref, kvp_ref, p_ref, ps_ref, do_ref, dq_ref, dkv_ref, dbq_ref, dsink_ref):
        n = pl.program_id(0)
        first = n == 0

        @pl.when(first)
        def _():
            dkv_ref[...] = jnp.zeros_like(dkv_ref)
            dsink_ref[...] = jnp.zeros_like(dsink_ref)
            dbq_ref[...] = jnp.zeros_like(dbq_ref)

        lane = lax.broadcasted_iota(jnp.int32, (1, 128), 1)
        place = (_half_place(True), _half_place(False))
        row_dots = [jnp.zeros((WINDOW, 128), F32) for _ in range(ATT_BPB)]
        heads = lambda kvh: range(kvh * ATT_G, (kvh + 1) * ATT_G)
        pair_lanes = lambda h: pl.ds((h // 2) * 2 * ATT_HD, 2 * ATT_HD)

        def products_of_do(bi, kvh):
            rows = pl.ds(bi * WINDOW, WINDOW)
            kh = _attn_kv(kvc_ref, kvp_ref, kvh, bi)
            vh = _attn_kv(kvc_ref, kvp_ref, ATT_KVH + kvh, bi)
            k_pl = [jnp.dot(kh, m, preferred_element_type=F32).astype(CDT) for m in place]
            v_pl = [jnp.dot(vh, m, preferred_element_type=F32).astype(CDT) for m in place]
            dps = [_dot_nt(do_ref[rows, pair_lanes(h)], v_pl[h % 2]) for h in heads(kvh)]
            dv2 = [sum(_dot_tn(p_ref[h, rows, :], do_ref[rows, pair_lanes(h)]) for h in heads(kvh) if h % 2 == par)
                   for par in range(2)]
            return k_pl, dps, dv2

        def softmax_bwd(bi, kvh, dps):
            rows = pl.ds(bi * WINDOW, WINDOW)
            dss = []
            for h, dp in zip(heads(kvh), dps):
                p = p_ref[h, rows, :].astype(F32)
                dd = jnp.sum(p * dp, axis=-1, keepdims=True)
                dss.append((p * (dp - dd)).astype(CDT))
                row_dots[bi] = row_dots[bi] + jnp.where(lane == h, dd, 0.0)
            return dss

        def products_of_ds(bi, kvh, k_pl, dss, dv2):
            block = n * ATT_BPB + bi
            rows = pl.ds(bi * WINDOW, WINDOW)
            rows_cur = pl.ds(pl.multiple_of(block * WINDOW, WINDOW), WINDOW)
            rows_prev = pl.ds(pl.multiple_of(jnp.maximum(block - 1, 0) * WINDOW, WINDOW), WINDOW)
            for i in range(ATT_G // 2):
                h = kvh * ATT_G + 2 * i
                dq2 = (_dot(dss[2 * i], k_pl[0]) + _dot(dss[2 * i + 1], k_pl[1])) * (ATT_HD ** -0.5)
                dq_ref[rows, pair_lanes(h)] = dq2.astype(CDT)
                dbq_ref[:, pair_lanes(h)] += _colsum(dq2)
            dk = sum(_dot_tn(ds, q_ref[h, rows, :]) for h, ds in zip(heads(kvh), dss)) * (ATT_HD ** -0.5)
            dv = dv2[0][:, :ATT_HD] + pltpu.roll(dv2[1], ATT_HD, 1)[:, :ATT_HD]
            dkv_ref[kvh, rows_prev, :] += dk[:WINDOW]
            dkv_ref[kvh, rows_cur, :] += dk[WINDOW:]
            dkv_ref[ATT_KVH + kvh, rows_prev, :] += dv[:WINDOW]
            dkv_ref[ATT_KVH + kvh, rows_cur, :] += dv[WINDOW:]

        groups = [(bi, kvh) for bi in range(ATT_BPB) for kvh in range(ATT_KVH)]
        ahead = products_of_do(*groups[0])
        for gi, g in enumerate(groups):
            k_pl, dps, dv2 = ahead
            dss = softmax_bwd(*g, dps)
            if gi + 1 < len(groups):
                ahead = products_of_do(*groups[gi + 1])
            products_of_ds(*g, k_pl, dss, dv2)
        dsinks = jnp.zeros((1, 128), F32)
        for bi in range(ATT_BPB):
            dsinks = dsinks - _colsum(ps_ref[pl.ds(bi * WINDOW, WINDOW), :] * row_dots[bi])
        dsink_ref[...] += dsinks

    rows_spec = pl.BlockSpec((ATT_BPB * WINDOW, D), lambda n: (n, 0))
    return _call(
        None, body, grid=(nb,),
        in_specs=[cur(ATT_QH), cur(2 * ATT_KVH), prev(2 * ATT_KVH),
                  pl.BlockSpec((ATT_QH, ATT_BPB * WINDOW, 2 * WINDOW), lambda n: (0, n, 0)),
                  pl.BlockSpec((ATT_BPB * WINDOW, 128), lambda n: (n, 0)), rows_spec],
        out_specs=[rows_spec, pl.BlockSpec((2 * ATT_KVH, S, ATT_HD), lambda n: (0, 0, 0)),
                   pl.BlockSpec((1, D), lambda n: (0, 0)), pl.BlockSpec((1, 128), lambda n: (0, 0))],
        out_shape=[_sds((S, D), CDT), _sds((2 * ATT_KVH, S, ATT_HD), F32), _sds((1, D), F32),
                   _sds((1, 128), F32)],
        compiler_params=_params(("arbitrary",)), name="attn_bwd")(q4, kv4, kv4, probs, sink_probs, do)


def _local_step(x, p, target, getw, sm, emit):
    S = x.shape[0]
    vec = lambda a: a.reshape(1, -1)
    ln_g = lambda l, k: vec(sm["ln_gain"][l, k])
    ln_b = lambda l, k: vec(sm["ln_bias"][l, k])
    xb = x.astype(CDT)
    pb = p.astype(CDT)

    proj = _inproj_fwd(xb, getw("a_w_in"), "a_in")
    o_a, y_a, states = _hgrn_fwd(proj, sm["a_lower_bound"], sm["a_norm_gain"])
    zeros = jnp.zeros((1, D), F32)
    z = [[None] * 3 for _ in range(2)]
    xs = [[None] * 3 for _ in range(2)]
    xbs = [[None] * 3 for _ in range(2)]
    z[0][0], xs[0][0], xbs[0][0] = _mixout_ln(y_a[None], getw("a_w_out")[None], zeros, x, ln_g(0, 0), ln_b(0, 0),
                                              "a_out_ln")
    gu, hid, sgs, ups = [None, None], [None, None], [None, None], [None, None]

    def ffn_ple(l, target=None):
        wgu = getw(f"gu{l}")
        gu[l], hid[l], z[l][1], xs[l][1], xbs[l][1] = _ffn_fwd(
            xs[l][0], xbs[l][0], wgu, getw(f"dn{l}"), ln_g(l, 1), ln_b(l, 1), f"ffn_fwd{l}")
        sgs[l], ups[l], z[l][2], *out = _ple_fwd(
            xs[l][1], xbs[l][1], pb[l], getw(f"pg{l}"), vec(sm["ple_b_gate"][l]), getw(f"pu{l}"), ln_g(l, 2),
            ln_b(l, 2), f"ple_fwd{l}", target=target)
        if target is None:
            xs[l][2], xbs[l][2] = out
        return out

    ffn_ple(0)
    x3, x3b = xs[0][2], xbs[0][2]
    w_kv, w_q, w_bo = getw("kv_w"), getw("b_w_q"), getw("b_w_out")
    kv4 = _proj_heads(x3b, w_kv, vec(sm["kv_b"]), 2 * ATT_KVH, "kv_proj")
    q4 = _proj_heads(x3b, w_q, vec(sm["b_b_q"]), ATT_QH, "q_proj")
    o_b, probs, sink_probs = _attn_fwd(q4, kv4, sm["b_sinks"])
    z[1][0], xs[1][0], xbs[1][0] = _mixout_ln(o_b[None], w_bo[None], sm["b_b_out"], x3, ln_g(1, 0), ln_b(1, 0),
                                              "b_out_ln")
    dy, loss = ffn_ple(1, target)

    gs = {}
    d_ln_g = [[None] * 3 for _ in range(2)]
    d_ln_b = [[None] * 3 for _ in range(2)]
    g_bg = [None, None]

    def ffn_ple_bwd(l, dy, after=None):
        dz2, dzb, dgl, dup, d_ln_g[l][2], d_ln_b[l][2], g_bg[l], d_ln_g[l][1], d_ln_b[l][1] = _ple_bwd(
            dy, z[l][2], sgs[l], ups[l], ln_g(l, 2), getw(f"pg{l}"), z[l][1], ln_g(l, 1), f"ple_bwd{l}", after=after)
        g_pg = _wgrad(xbs[l][1][None], dgl[None], f"g_ple_gate{l}")[0]
        g_pu = _wgrad(pb[l][None], dup[None], f"g_ple_up{l}")[0]
        g_dn = _wgrad(hid[l], dzb[None], f"g_ffn_down{l}")
        if l == 1:
            tok = emit({f"pg{l}": g_pg, f"pu{l}": g_pu})
            tok = tok + emit({f"dn{l}": g_dn})
        else:
            tok = emit({f"pg{l}": g_pg, f"pu{l}": g_pu, f"dn{l}": g_dn})
        dgu = _ffn_bwd_hidden(dzb, gu[l], getw(f"dn{l}"), f"ffn_bwd{l}", after=tok)
        g_gu = _wgrad(dgu.reshape(8, S, FFN_B), xbs[l][0][None], f"g_ffn_gate_up{l}")
        tok = emit({f"gu{l}": g_gu})
        dx1 = _ffn_bwd_input(dz2, dgu, getw(f"gu{l}"), f"ffn_bwd{l}", after=tok)
        return dx1, None

    dx1, tok = ffn_ple_bwd(1, dy)
    dz, dzb, do, d_ln_g[1][0], d_ln_b[1][0], gs["b_b_out"] = _mixout_bwd(dx1, z[1][0], ln_g(1, 0), w_bo[None], CDT,
                                                                        "b_out_bwd", after=tok)
    g_bo = _wgrad(o_b[None], dzb[None], "g_b_w_out")[0]
    dq, dkv4, dbq, dsinks = _attn_bwd(q4, kv4, probs, sink_probs, do[0])
    gs["b_b_q"] = dbq
    gs["b_sinks"] = dsinks
    g_q = _wgrad(x3b[None], dq[None], "g_b_w_q")[0]
    dx3, dkv, gs["kv_b"] = _qkv_bwd(dz, dq, dkv4, w_q, w_kv, "qkv_bwd", after=tok)
    g_kv = _wgrad(x3b[None], dkv[None], "g_kv_w")[0]
    tok = emit({"b_w_out": g_bo, "b_w_q": g_q, "kv_w": g_kv})
    dx1, tok = ffn_ple_bwd(0, dx3, tok)
    w_ao = getw("a_w_out")
    dz, dzb, dyr, d_ln_g[0][0], d_ln_b[0][0], _ = _mixout_bwd(dx1, z[0][0], ln_g(0, 0), w_ao[None], F32, "a_out_bwd",
                                                              after=tok)
    g_ao = _wgrad(y_a[None], dzb[None], "g_a_w_out")[0]
    tok = emit({"a_w_out": g_ao})
    dproj, gs["a_lower_bound"], gs["a_norm_gain"] = _hgrn_bwd(proj, sm["a_lower_bound"], sm["a_norm_gain"], o_a, states,
                                                              dyr[0], after=tok)
    tk = lambda t: (None, t, D)
    g_ain = _mm_tn(xb[None], dproj, N_DEV, lambda g, k: (0, k, 0), lambda g, k: (g // 2, k, g % 2),
                   tk, lambda t: (None, t, 512), (N_DEV, D, 512), (None, D, 512), lambda g, k: (g, 0, 0), name="g_a_w_in")
    gs["ple_b_gate"] = jnp.concatenate(g_bg, axis=0)
    gs["ln_gain"] = jnp.stack([jnp.concatenate(r, axis=0) for r in d_ln_g])
    gs["ln_bias"] = jnp.stack([jnp.concatenate(r, axis=0) for r in d_ln_b])
    gs["loss"] = loss
    tok = emit({"a_w_in": g_ain}, small=gs)
    grad_x = _inproj_bwd(dz, dproj, getw("a_w_in"), "a_in_bwd", after=tok)
    return loss, grad_x, gs


def _peer(k):
    x, y, c = lax.axis_index("x"), lax.axis_index("y"), lax.axis_index("c")
    px = 1 - x if k & 4 else x
    py = 1 - y if k & 2 else y
    pc = 1 - c if k & 1 else c
    return (px, py, pc), 4 * px + 2 * py + pc


def _my_index():
    return 4 * lax.axis_index("x") + 2 * lax.axis_index("y") + lax.axis_index("c")


def _piece_copy(mode, src, land, send_sems, recv_sems, t, k, sender, receiver, peer):
    return pltpu.make_async_remote_copy(
        src_ref=src if mode == "gather" else src.at[receiver], dst_ref=land.at[sender],
        send_sem=send_sems.at[t * 7 + k - 1], recv_sem=recv_sems.at[t * 7 + k - 1], device_id=peer, device_id_type=MESH)


def _sequencer_exchange(srcs, modes, name, collective_id, after=None):
    n = len(srcs)
    land_shapes = [((N_DEV,) + a.shape) if mode == "gather" else a.shape for a, mode in zip(srcs, modes)]
    extra = [] if after is None else [after]

    def body(*refs):
        src_refs, land_refs = refs[:n], refs[n + len(extra):2 * n + len(extra)]
        send_sems, recv_sems, local_sems = refs[2 * n + len(extra):]
        barrier = pltpu.get_barrier_semaphore()
        for k in range(1, N_DEV):
            pl.semaphore_signal(barrier, inc=1, device_id=_peer(k)[0], device_id_type=MESH)
        pl.semaphore_wait(barrier, N_DEV - 1)
        me = _my_index()
        local = []
        for i in range(n):
            cp = pltpu.make_async_copy(src_refs[i] if modes[i] == "gather" else src_refs[i].at[me], land_refs[i].at[me],
                                       local_sems.at[i])
            cp.start()
            local.append(cp)
        for k in range(1, N_DEV):
            peer, pid = _peer(k)
            for t in range(n):
                _piece_copy(modes[t], src_refs[t], land_refs[t], send_sems, recv_sems, t, k, me, pid, peer).start()
        for k in range(1, N_DEV):
            peer, pid = _peer(k)
            for t in range(n):
                _piece_copy(modes[t], src_refs[t], land_refs[t], send_sems, recv_sems, t, k, pid, me, peer).wait_recv()
        for k in range(1, N_DEV):
            peer, pid = _peer(k)
            for t in range(n):
                _piece_copy(modes[t], src_refs[t], land_refs[t], send_sems, recv_sems, t, k, me, pid, peer).wait_send()
        for cp in local:
            cp.wait()

    return pl.kernel(
        body, out_type=[_sds(s, a.dtype) for s, a in zip(land_shapes, srcs)],
        mesh=plsc.ScalarSubcoreMesh(axis_name="sequencer", num_cores=1),
        scratch_types=[pltpu.SemaphoreType.DMA((7 * n,)), pltpu.SemaphoreType.DMA((7 * n,)), pltpu.SemaphoreType.DMA((n,))],
        compiler_params=pltpu.CompilerParams(collective_id=collective_id), name=name)(*srcs, *extra)


def _sequencer_gather(srcs, name, collective_id, after=None):
    n = len(srcs)
    extra = [] if after is None else [after]

    def body(*refs):
        src_refs, land_refs = refs[:n], refs[n + len(extra):2 * n + len(extra)]
        send_sems, recv_sems, local_sems = refs[2 * n + len(extra):]
        x, y, c = lax.axis_index("x"), lax.axis_index("y"), lax.axis_index("c")
        sibling = (x, y, 1 - c)
        chips = [(1 - x, y), (x, 1 - y), (1 - x, 1 - y)]
        index = lambda px, py, pc: 4 * px + 2 * py + pc
        barrier = pltpu.get_barrier_semaphore()
        for peer in [sibling] + [(*chip, c) for chip in chips]:
            pl.semaphore_signal(barrier, inc=1, device_id=peer, device_id_type=MESH)
        pl.semaphore_wait(barrier, 4)

        def copy(t, k, slot, to, src=None):
            return pltpu.make_async_remote_copy(
                src_ref=land_refs[t].at[slot] if src is None else src, dst_ref=land_refs[t].at[slot],
                send_sem=send_sems.at[7 * t + k], recv_sem=recv_sems.at[7 * t + k], device_id=to, device_id_type=MESH)

        me = index(x, y, c)
        local = []
        for t in range(n):
            cp = pltpu.make_async_copy(src_refs[t], land_refs[t].at[me], local_sems.at[t])
            cp.start()
            local.append(cp)
        sends = []
        for t in range(n):
            sends.append(copy(t, 0, me, sibling, src=src_refs[t]))
            sends += [copy(t, 1 + j, me, (*chip, c), src=src_refs[t]) for j, chip in enumerate(chips)]
        for cp in sends:
            cp.start()
        for j, chip in enumerate(chips):
            for t in range(n):
                copy(t, 1 + j, index(*chip, c), sibling, src=src_refs[t]).wait_recv()
                passed = copy(t, 4 + j, index(*chip, c), sibling)
                passed.start()
                sends.append(passed)
        for t in range(n):
            copy(t, 0, index(x, y, 1 - c), sibling, src=src_refs[t]).wait_recv()
        for j, chip in enumerate(chips):
            for t in range(n):
                copy(t, 4 + j, index(*chip, 1 - c), sibling, src=src_refs[t]).wait_recv()
        for cp in sends:
            cp.wait_send()
        for cp in local:
            cp.wait()

    return pl.kernel(
        body, out_type=[_sds((N_DEV,) + a.shape, a.dtype) for a in srcs],
        mesh=plsc.ScalarSubcoreMesh(axis_name="sequencer", num_cores=1),
        scratch_types=[pltpu.SemaphoreType.DMA((7 * n,)), pltpu.SemaphoreType.DMA((7 * n,)), pltpu.SemaphoreType.DMA((n,))],
        compiler_params=pltpu.CompilerParams(collective_id=collective_id), name=name)(*srcs, *extra)


def _adamw(w, g, m, v):
    m = ADAM_B1 * m + (1.0 - ADAM_B1) * g
    v = ADAM_B2 * v + (1.0 - ADAM_B2) * (g * g)
    m_hat = m / (1.0 - ADAM_B1 ** ADAM_STEP)
    v_hat = v / (1.0 - ADAM_B2 ** ADAM_STEP)
    delta = -ADAM_LR * (m_hat / (jnp.sqrt(v_hat) + ADAM_EPS) + ADAM_WD * w)
    return delta, m, v


def _adam_big(w, parts, m, v, name, after=None):
    L, R, C = w.shape
    P = parts[0].shape[0]
    tr = _tile(R, (256, 128, 176, 64, 32, 16))
    nr = R // tr

    def body(w_ref, *refs):
        p_refs, (m_ref, v_ref, g_ref, d_ref, mo_ref, vo_ref) = refs[:L], refs[L:]
        for l in range(L):
            @pl.when(pl.program_id(0) == l)
            def _(p_ref=p_refs[l]):
                g = p_ref[0].astype(F32)
                for s in range(1, P):
                    g = g + p_ref[s].astype(F32)
                g_ref[...] = g
                d_ref[...], mo_ref[...], vo_ref[...] = _adamw(w_ref[...], g, m_ref[...], v_ref[...])

    row = pl.BlockSpec((None, tr, C), lambda l, i: (l, i, 0))
    park = lambda l_of: (lambda l, i: (0, jnp.where(l == l_of, i, 0 if l_of else nr - 1), 0))
    return _call(
        after, body, grid=(L, nr),
        in_specs=[row] + [pl.BlockSpec((P, tr, C), park(l)) for l in range(L)] + [row, row],
        out_specs=[row] * 4, out_shape=[_sds((L, R, C), F32)] * 4,
        compiler_params=_params(("arbitrary", "arbitrary")), name=name)(w, *parts, m, v)


SMALL = (("a_lower_bound", (2, 128), (2, D)), ("ln_gain", (3, 2, 128), (6, D)), ("ln_bias", (3, 2, 128), (6, D)),
         ("a_norm_gain", (1, 128), (1, 128)), ("kv_b", (1, 512), (1, 512)), ("b_b_q", (1, D), (1, D)),
         ("b_sinks", (1, ATT_QH), (1, 128)), ("b_b_out", (1, D), (1, D)), ("ple_b_gate", (2, D), (2, D)))


def _adam_small(parts, w, m, v, losses, after=None):
    k = len(SMALL)

    def body(*refs):
        p_refs, w_refs, m_refs, v_refs = refs[:k], refs[k:2 * k], refs[2 * k:3 * k], refs[3 * k:4 * k]
        loss_ref, outs, total_ref = refs[4 * k], refs[4 * k + 1:-1], refs[-1]
        total = loss_ref[0]
        for s in range(1, N_DEV):
            total = total + loss_ref[s]
        total_ref[...] = total
        me = _my_index()
        for i, (_, wshape, pshape) in enumerate(SMALL):
            cols = wshape[-1]
            lanes = slice(None) if cols == pshape[1] else (
                pl.ds(0, cols) if cols < 128 else pl.ds(pl.multiple_of(me * cols, cols), cols))
            at = [(slice(None), slice(None))] if len(wshape) == 2 else [
                (pl.ds(l * wshape[0] + kk, 1), (kk, pl.ds(l, 1), slice(None))) for kk in range(wshape[0]) for l in range(wshape[1])]
            for rows, own in at:
                g = p_refs[i][0, rows, lanes]
                for s in range(1, N_DEV):
                    g = g + p_refs[i][s, rows, lanes]
                g_ref, d_ref, mo_ref, vo_ref = outs[4 * i:4 * i + 4]
                g_ref[own] = g
                d_ref[own], mo_ref[own], vo_ref[own] = _adamw(w_refs[i][own], g, m_refs[i][own], v_refs[i][own])

    full = lambda shape: pl.BlockSpec(shape, lambda: (0,) * len(shape))
    names = [n for n, _, _ in SMALL]
    held = lambda a, ws: a.swapaxes(0, 1) if len(ws) == 3 else a.reshape(ws)
    back = lambda r, n: r.swapaxes(0, 1) if r.ndim == 3 else r.reshape(w[n].shape)
    res = _call(
        after, body,
        in_specs=[full((N_DEV,) + ps) for _, _, ps in SMALL] + [full(ws) for _, ws, _ in SMALL] * 3
        + [full((N_DEV, 1, 128))],
        out_specs=[full(ws) for _, ws, _ in SMALL for _ in range(4)] + [full((1, 128))],
        out_shape=[_sds(ws, F32) for _, ws, _ in SMALL for _ in range(4)] + [_sds((1, 128), F32)], name="adam_small")(
            *[parts[n] for n in names], *[held(a[n], ws) for a in (w, m, v) for n, ws, _ in SMALL], losses)
    return {n: [back(r, n) for r in res[4 * i:4 * i + 4]] for i, n in enumerate(names)}, res[-1][0, 0]


WEIGHTS = ("a_w_in", "a_lower_bound", "a_norm_gain", "a_w_out", "kv_w", "kv_b", "b_w_q", "b_b_q", "b_sinks", "b_w_out",
           "b_b_out", "ffn_w_gate_up", "ffn_w_down", "ple_w_up", "ple_w_gate", "ple_b_gate", "ln_gain", "ln_bias")


GATHER_GROUPS = (("a_w_in",), ("a_w_out", "gu0"), ("dn0", "pu0", "pg0"), ("kv_w", "b_w_q", "b_w_out"), ("gu1",),
                 ("dn1", "pu1", "pg1"))
KERNEL_LAYOUT = {
    "a_w_in": lambda a: a,
    "a_w_out": lambda a: a.reshape(D, D),
    "kv_w": lambda a: a.reshape(D, 2 * ATT_KVH * ATT_HD),
    "b_w_q": lambda a: a.reshape(D, D),
    "b_w_out": lambda a: a.reshape(D, D),
    "gu": lambda a: a.reshape(2, 4, FFN_B, D),
    "dn": lambda a: a.reshape(4, FFN_B, D),
    "pu": lambda a: a,
    "pg": lambda a: a.reshape(D, D),
}
_row_blocks = lambda a: a.reshape(N_DEV, -1, a.shape[-1])
OWNER_BLOCKS = {
    "a_w_in": lambda g: g,
    "a_w_out": _row_blocks,
    "kv_w": _row_blocks,
    "b_w_q": _row_blocks,
    "b_w_out": _row_blocks,
    "gu": lambda g: g,
    "dn": lambda g: _row_blocks(g.reshape(FFN_H, D)),
    "pu": lambda g: g.reshape(PLE_DIM, N_DEV, 128).transpose(1, 0, 2),
    "pg": _row_blocks,
}
ADAM_PARTS = (("kv_w", ("kv_w",)), ("b_w_q", ("b_w_q",)), ("b_w_out", ("b_w_out",)), ("ffn_w_gate_up", ("gu0", "gu1")),
              ("ffn_w_down", ("dn0", "dn1")), ("ple_w_up", ("pu0", "pu1")), ("ple_w_gate", ("pg0", "pg1")),
              ("a_w_out", ("a_w_out",)), ("a_w_in", ("a_w_in",)))


def kernel(x, p, a_w_in, a_lower_bound, a_norm_gain, a_w_out, kv_w, kv_b, b_w_q, b_b_q, b_sinks, b_w_out, b_b_out, ffn_w_gate_up, ffn_w_down, ple_w_up, ple_w_gate, ple_b_gate, ln_gain, ln_bias, loss_target, m_a_w_in, m_a_lower_bound, m_a_norm_gain, m_a_w_out, m_kv_w, m_kv_b, m_b_w_q, m_b_b_q, m_b_sinks, m_b_w_out, m_b_b_out, m_ffn_w_gate_up, m_ffn_w_down, m_ple_w_up, m_ple_w_gate, m_ple_b_gate, m_ln_gain, m_ln_bias, v_a_w_in, v_a_lower_bound, v_a_norm_gain, v_a_w_out, v_kv_w, v_kv_b, v_b_w_q, v_b_b_q, v_b_sinks, v_b_w_out, v_b_b_out, v_ffn_w_gate_up, v_ffn_w_down, v_ple_w_up, v_ple_w_gate, v_ple_b_gate, v_ln_gain, v_ln_bias):
    given = dict(locals())
    w = {n: given[n] for n in WEIGHTS}
    m = {n: given["m_" + n] for n in WEIGHTS}
    v = {n: given["v_" + n] for n in WEIGHTS}
    shards = {"a_w_in": a_w_in[0], "a_w_out": a_w_out[0], "kv_w": kv_w, "b_w_q": b_w_q[0], "b_w_out": b_w_out[0]}
    for l in range(2):
        shards.update({f"gu{l}": ffn_w_gate_up[l].T, f"dn{l}": ffn_w_down[l], f"pu{l}": ple_w_up[l], f"pg{l}": ple_w_gate[l]})
    sharded_small = [a_lower_bound, ln_gain.swapaxes(0, 1), ln_bias.swapaxes(0, 1)]
    gathered = {}
    for gi, g in enumerate(GATHER_GROUPS):
        lands = _sequencer_gather([shards[n].astype(CDT) for n in g] + (sharded_small if gi == 0 else []),
                                  f"gather{gi}", gi)
        for n, a in zip(g, lands):
            gathered[n] = KERNEL_LAYOUT[n.rstrip("01")](a)
        if gi == 0:
            alb = lands[len(g)].transpose(1, 0, 2).reshape(2, D)
            lng, lnb = [a.transpose(2, 1, 0, 3).reshape(2, 3, D) for a in lands[len(g) + 1:]]

    getw = gathered.__getitem__

    sm = {"a_lower_bound": alb, "ln_gain": lng, "ln_bias": lnb,
          "a_norm_gain": a_norm_gain, "kv_b": kv_b, "b_b_q": b_b_q[0], "b_sinks": b_sinks, "b_b_out": b_b_out,
          "ple_b_gate": ple_b_gate}

    scatters, small_parts = [], {}

    def emit(grads, small=None):
        names = list(grads)
        blocks = [OWNER_BLOCKS[n.rstrip("01")](grads[n]) for n in names]
        partials = [] if small is None else [small[n].reshape(ps) for n, _, ps in SMALL] + [small["loss"]]
        lands = _sequencer_exchange(blocks + partials, ["scatter"] * len(blocks) + ["gather"] * len(partials),
                                    f"scatter{len(scatters)}", len(GATHER_GROUPS) + len(scatters))
        scatters.append(dict(zip(names, lands)))
        small_parts.update(zip([n for n, _, _ in SMALL] + ["loss"], lands[len(blocks):]))
        return blocks + (list(scatters[-4].values()) if len(scatters) >= 4 else [])

    loss, grad_x, gs = _local_step(x[0], p[:, 0], loss_target[0], getw, sm, emit)

    out, parts, last = {}, {}, [grad_x]
    for landed in scatters:
        parts.update(landed)
        for n, keys in ADAM_PARTS:
            if n in out or not all(key in parts for key in keys):
                continue
            lrc = (1,) * (3 - w[n].ndim) + w[n].shape
            shard = (lambda a: a.reshape(lrc).swapaxes(1, 2)) if n == "ffn_w_gate_up" else (lambda a: a.reshape(lrc))
            res = _adam_big(shard(w[n]), [parts[key] for key in keys], shard(m[n]), shard(v[n]), "adam_" + n, after=last)
            out[n] = [(r.swapaxes(1, 2) if n == "ffn_w_gate_up" else r).reshape(w[n].shape) for r in res]
            last = [res[3]]
    small_out, loss = _adam_small(small_parts, w, m, v, small_parts["loss"], after=last)
    out.update(small_out)
    res = [loss, grad_x[None]]
    for i in range(4):
        res += [out[n][i] for n in WEIGHTS]
    return tuple(res)
```

```python
import jax
import jax.numpy as jnp
from jax import lax
from jax.experimental import pallas as pl
from jax.experimental.pallas import tpu as pltpu
from jax.experimental.pallas import tpu_sc as plsc

F32 = jnp.float32
CDT = jnp.bfloat16

N_DEV = 8
D = 1024
HG_H, HG_DK, HG_CH = 8, 128, 64
HG_HPB = 4
HG_CPB = 8
ATT_HD, ATT_QH, ATT_KVH, ATT_G, WINDOW = 64, 16, 4, 4, 128
ATT_BPB = 1
FFN_H = 2816
FFN_B = FFN_H // 4
PLE_DIM = 256
ALPHA = (2.0 * 2) ** 0.25
LN_EPS = 1e-5
RMS_EPS = 1e-6
ADAM_LR, ADAM_B1, ADAM_B2, ADAM_EPS, ADAM_WD, ADAM_STEP = 0.001, 0.9, 0.999, 1e-08, 0.01, 10
ROW_TILES = (512, 256, 128, 64)
VMEM_LIMIT = 48 * 1024 * 1024
NEG = -1e30

MESH = pl.DeviceIdType.MESH


def _tile(n, cands=ROW_TILES):
    for t in cands:
        if n % t == 0:
            return t
    return n


def _sds(shape, dtype):
    return jax.ShapeDtypeStruct(tuple(shape), dtype)


def _params(sem):
    return pltpu.CompilerParams(dimension_semantics=sem, vmem_limit_bytes=VMEM_LIMIT)


def _dot(a, b):
    return jnp.dot(a.astype(CDT), b.astype(CDT), preferred_element_type=F32)


def _dot_nt(a, b):
    return lax.dot_general(a.astype(CDT), b.astype(CDT), (((1,), (1,)), ((), ())), preferred_element_type=F32)


def _dot_tn(a, b):
    return lax.dot_general(a.astype(CDT), b.astype(CDT), (((0,), (0,)), ((), ())), preferred_element_type=F32)


def _sigmoid(x):
    return jax.nn.sigmoid(x)


def _ln_fwd(z, g, b):
    mu = jnp.mean(z, axis=-1, keepdims=True)
    zc = z - mu
    var = jnp.mean(zc * zc, axis=-1, keepdims=True)
    return zc * lax.rsqrt(var + LN_EPS) * g + b


def _ln_bwd(z, g, dy):
    mu = jnp.mean(z, axis=-1, keepdims=True)
    zc = z - mu
    var = jnp.mean(zc * zc, axis=-1, keepdims=True)
    rstd = lax.rsqrt(var + LN_EPS)
    xhat = zc * rstd
    dxh = dy * g
    dz = rstd * (dxh - jnp.mean(dxh, axis=-1, keepdims=True) - xhat * jnp.mean(dxh * xhat, axis=-1, keepdims=True))
    return dz, xhat


def _colsum(x):
    return jnp.sum(x, axis=0, keepdims=True)


def _acc(ref, val, first):
    @pl.when(first)
    def _():
        ref[...] = val

    @pl.when(jnp.logical_not(first))
    def _():
        ref[...] += val


def _zero_at(ref, first):
    @pl.when(first)
    def _():
        ref[...] = jnp.zeros_like(ref)


def _call(after, body, **kw):
    after = [] if after is None else list(after)
    specs = list(kw["in_specs"])
    kw["in_specs"] = [pl.BlockSpec(memory_space=pl.ANY)] * len(after) + specs

    def ordered_body(*refs):
        body(*refs[len(after):])

    call = pl.pallas_call(ordered_body, **kw)
    return lambda *args: call(*after, *args)


def _mm_tn(a3, b3, G, amap, bmap, ablock, bblock, out_shape, oblock, omap, name="mm_tn"):
    S = a3.shape[1]

    def body(a_ref, b_ref, o_ref):
        o_ref[...] = _dot_tn(a_ref[...], b_ref[...]).astype(o_ref.dtype)

    return _call(
        None, body, grid=(G, 1),
        in_specs=[pl.BlockSpec(ablock(S), amap), pl.BlockSpec(bblock(S), bmap)],
        out_specs=pl.BlockSpec(oblock, omap), out_shape=_sds(out_shape, CDT),
        compiler_params=_params(("arbitrary", "arbitrary")), name=name)(a3, b3)


def _wgrad(a3, b3, name):
    Ga, S, M = a3.shape
    Gb, _, N = b3.shape
    G = max(Ga, Gb)
    return _mm_tn(
        a3, b3, G,
        (lambda g, k: (g, k, 0)) if Ga > 1 else (lambda g, k: (0, k, 0)),
        (lambda g, k: (g, k, 0)) if Gb > 1 else (lambda g, k: (0, k, 0)),
        lambda tk: (None, tk, M), lambda tk: (None, tk, N),
        (G, M, N), (None, M, N), lambda g, k: (g, 0, 0), name=name)


def _mixout_ln(u3, w3, bias, xin, gain, beta, name):
    G, S, Kb = u3.shape
    tm = _tile(S)

    def body(u_ref, w_ref, b_ref, x_ref, g_ref, be_ref, z_ref, xo_ref, xob_ref):
        h = b_ref[...] + _dot(u_ref[0], w_ref[0])
        for g in range(1, G):
            h = h + _dot(u_ref[g], w_ref[g])
        z = ALPHA * x_ref[...] + h
        z_ref[...] = z
        y = _ln_fwd(z, g_ref[...], be_ref[...])
        xo_ref[...] = y
        xob_ref[...] = y.astype(CDT)

    row = pl.BlockSpec((tm, D), lambda i: (i, 0))
    vec = pl.BlockSpec((1, D), lambda i: (0, 0))
    return _call(
        None, body, grid=(S // tm,),
        in_specs=[pl.BlockSpec((G, tm, Kb), lambda i: (0, i, 0)), pl.BlockSpec((G, Kb, D), lambda i: (0, 0, 0)),
                  vec, row, vec, vec],
        out_specs=[row, row, row], out_shape=[_sds((S, D), F32), _sds((S, D), F32), _sds((S, D), CDT)],
        compiler_params=_params(("arbitrary",)), name=name)(u3, w3, bias, xin, gain, beta)


def _ffn_fwd(xin, xin_b, wgu, wdn, gain, beta, name):
    S = xin.shape[0]
    tm = _tile(S)

    def hidden(xb_ref, wgu_ref, gu_ref, hid_ref):
        xb = xb_ref[...]
        gate = _dot_nt(xb, wgu_ref[0])
        up = _dot_nt(xb, wgu_ref[1])
        gu_ref[0] = gate.astype(CDT)
        gu_ref[1] = up.astype(CDT)
        hid_ref[...] = (gate * _sigmoid(gate) * up).astype(CDT)

    gu, hid = _call(
        None, hidden, grid=(4, S // tm),
        in_specs=[pl.BlockSpec((tm, D), lambda j, i: (i, 0)), pl.BlockSpec((2, None, FFN_B, D), lambda j, i: (0, j, 0, 0))],
        out_specs=[pl.BlockSpec((2, None, tm, FFN_B), lambda j, i: (0, j, i, 0)),
                   pl.BlockSpec((None, tm, FFN_B), lambda j, i: (j, i, 0))],
        out_shape=[_sds((2, 4, S, FFN_B), CDT), _sds((4, S, FFN_B), CDT)],
        compiler_params=_params(("arbitrary", "arbitrary")), name=name + "_hidden")(xin_b, wgu)

    def down(x_ref, hid_ref, wdn_ref, g_ref, be_ref, z_ref, xo_ref, xob_ref):
        z = ALPHA * x_ref[...]
        for j in range(4):
            z = z + _dot(hid_ref[j], wdn_ref[j])
        z_ref[...] = z
        y = _ln_fwd(z, g_ref[...], be_ref[...])
        xo_ref[...] = y
        xob_ref[...] = y.astype(CDT)

    row = pl.BlockSpec((tm, D), lambda i: (i, 0))
    vec = pl.BlockSpec((1, D), lambda i: (0, 0))
    z, xo, xob = _call(
        None, down, grid=(S // tm,),
        in_specs=[row, pl.BlockSpec((4, tm, FFN_B), lambda i: (0, i, 0)), pl.BlockSpec((4, FFN_B, D), lambda i: (0, 0, 0)),
                  vec, vec],
        out_specs=[row, row, row], out_shape=[_sds((S, D), F32), _sds((S, D), F32), _sds((S, D), CDT)],
        compiler_params=_params(("arbitrary",)), name=name + "_down")(xin, hid, wdn, gain, beta)
    return gu, hid, z, xo, xob


def _ple_fwd(xin, xin_b, p_b, wpg, bgate, wpu, gain, beta, name, target=None):
    S = xin.shape[0]
    tm = _tile(S)

    def body(x_ref, xb_ref, p_ref, wpg_ref, bg_ref, wpu_ref, g_ref, be_ref, *rest):
        if target is None:
            sg_ref, up_ref, z_ref, xo_ref, xob_ref = rest
        else:
            t_ref, sg_ref, up_ref, z_ref, dy_ref, l_ref = rest
        sg =_sigmoid(_dot(xb_ref[...], wpg_ref[...]) + bg_ref[...])
        pb = p_ref[...]
        up = jnp.concatenate([_dot(pb, wpu_ref[j]) for j in range(N_DEV)], axis=-1)
        sg_ref[...] = sg.astype(CDT)
        up_ref[...] = (up * sg * (1.0 - sg)).astype(CDT)
        z = ALPHA * x_ref[...] + sg * up
        z_ref[...] = z
        y = _ln_fwd(z, g_ref[...], be_ref[...])
        if target is None:
            xo_ref[...] = y
            xob_ref[...] = y.astype(CDT)
        else:
            e = y - t_ref[...]
            dy_ref[...] = e * (1.0 / D)
            part = 0.5 * jnp.sum(jnp.sum(e * e, axis=-1, keepdims=True) * (1.0 / D), axis=0, keepdims=True)
            _acc(l_ref, jnp.broadcast_to(part, l_ref.shape), pl.program_id(0) == 0)

    row = pl.BlockSpec((tm, D), lambda i: (i, 0))
    vec = pl.BlockSpec((1, D), lambda i: (0, 0))
    last = target is not None
    return _call(
        None, body, grid=(S // tm,),
        in_specs=[row, row, pl.BlockSpec((tm, PLE_DIM), lambda i: (i, 0)), pl.BlockSpec((D, D), lambda i: (0, 0)), vec,
                  pl.BlockSpec((N_DEV, PLE_DIM, D // N_DEV), lambda i: (0, 0, 0)), vec, vec] + [row] * last,
        out_specs=[row] * 4 + [pl.BlockSpec((1, 128), lambda i: (0, 0)) if last else row],
        out_shape=[_sds((S, D), CDT)] * 2 + [_sds((S, D), F32)] * 2 + [_sds((1, 128), F32) if last else _sds((S, D), CDT)],
        compiler_params=_params(("arbitrary",)), name=name)(
            xin, xin_b, p_b, wpg, bgate, wpu, gain, beta, *([target] if last else []))


def _ple_bwd(dy, z, sg, up, gain, wpg, z_ffn, gain_ffn, name, after=None):
    S = dy.shape[0]
    tm = _tile(S)

    def body(dy_ref, z_ref, sg_ref, up_ref, g_ref, wpg_ref, zf_ref, gf_ref, dzf_ref, dzfb_ref, dgl_ref, dup_ref,
             dgain_ref, dbeta_ref, dbg_ref, dgainf_ref, dbetaf_ref):
        first = pl.program_id(0) == 0
        dy_ = dy_ref[...]
        dz, xhat = _ln_bwd(z_ref[...], g_ref[...], dy_)
        dgl = dz * up_ref[...].astype(F32)
        dgl_ref[...] = dgl.astype(CDT)
        dup_ref[...] = (dz * sg_ref[...].astype(F32)).astype(CDT)
        dx = ALPHA * dz + _dot_nt(dgl, wpg_ref[...])
        dzf, xhatf = _ln_bwd(zf_ref[...], gf_ref[...], dx)
        dzf_ref[...] = dzf
        dzfb_ref[...] = dzf.astype(CDT)
        _acc(dgain_ref, _colsum(dy_ * xhat), first)
        _acc(dbeta_ref, _colsum(dy_), first)
        _acc(dbg_ref, _colsum(dgl), first)
        _acc(dgainf_ref, _colsum(dx * xhatf), first)
        _acc(dbetaf_ref, _colsum(dx), first)

    row = pl.BlockSpec((tm, D), lambda i: (i, 0))
    vec = pl.BlockSpec((1, D), lambda i: (0, 0))
    return _call(
        after, body, grid=(S // tm,),
        in_specs=[row, row, row, row, vec, pl.BlockSpec((D, D), lambda i: (0, 0)), row, vec],
        out_specs=[row, row, row, row] + [vec] * 5,
        out_shape=[_sds((S, D), F32)] + [_sds((S, D), CDT)] * 3 + [_sds((1, D), F32)] * 5,
        compiler_params=_params(("arbitrary",)), name=name)(dy, z, sg, up, gain, wpg, z_ffn, gain_ffn)


def _ffn_bwd_hidden(dzb, gu, wdn, name, after=None):
    S = dzb.shape[0]
    tm = _tile(S)

    def hidden(dzb_ref, gu_ref, wdn_ref, dgu_ref):
        dhid = _dot_nt(dzb_ref[...], wdn_ref[...])
        gate, up = gu_ref[0].astype(F32), gu_ref[1].astype(F32)
        sg = _sigmoid(gate)
        dgu_ref[0] = (dhid * up * (sg * (1.0 + gate * (1.0 - sg)))).astype(CDT)
        dgu_ref[1] = (dhid * (gate * sg)).astype(CDT)

    blocks = pl.BlockSpec((2, None, tm, FFN_B), lambda j, i: (0, j, i, 0))
    return _call(
        after, hidden, grid=(4, S // tm),
        in_specs=[pl.BlockSpec((tm, D), lambda j, i: (i, 0)), blocks, pl.BlockSpec((None, FFN_B, D), lambda j, i: (j, 0, 0))],
        out_specs=blocks, out_shape=_sds((2, 4, S, FFN_B), CDT),
        compiler_params=_params(("arbitrary", "arbitrary")), name=name + "_hidden")(dzb, gu, wdn)


def _ffn_bwd_input(dz, dgu, wgu, name, after=None):
    S = dz.shape[0]
    tm = _tile(S)

    def to_input(dz_ref, dgu_ref, wgu_ref, dx_ref):
        acc = ALPHA * dz_ref[...]
        for g in range(2):
            for j in range(4):
                acc = acc + _dot(dgu_ref[g, j], wgu_ref[g, j])
        dx_ref[...] = acc

    rows = pl.BlockSpec((tm, D), lambda i: (i, 0))
    return _call(
        after, to_input, grid=(S // tm,),
        in_specs=[rows, pl.BlockSpec((2, 4, tm, FFN_B), lambda i: (0, 0, i, 0)),
                  pl.BlockSpec((2, 4, FFN_B, D), lambda i: (0, 0, 0, 0))],
        out_specs=rows, out_shape=_sds((S, D), F32),
        compiler_params=_params(("arbitrary",)), name=name + "_input")(dz, dgu, wgu)


def _mixout_bwd(dy, z, gain, w3, du_dtype, name, after=None):
    S = dy.shape[0]
    G, Kb, _ = w3.shape
    tm = _tile(S)

    def body(dy_ref, z_ref, g_ref, w_ref, dz_ref, dzb_ref, du_ref, dgain_ref, dbeta_ref, dbias_ref):
        first = pl.program_id(0) == 0
        dy_ = dy_ref[...]
        dz, xhat = _ln_bwd(z_ref[...], g_ref[...], dy_)
        dz_ref[...] = dz
        dzb = dz.astype(CDT)
        dzb_ref[...] = dzb
        for g in range(G):
            du_ref[g] = _dot_nt(dzb, w_ref[g]).astype(du_ref.dtype)
        _acc(dgain_ref, _colsum(dy_ * xhat), first)
        _acc(dbeta_ref, _colsum(dy_), first)
        _acc(dbias_ref, _colsum(dz), first)

    row = pl.BlockSpec((tm, D), lambda i: (i, 0))
    vec = pl.BlockSpec((1, D), lambda i: (0, 0))
    return _call(
        after, body, grid=(S // tm,), in_specs=[row, row, vec, pl.BlockSpec((G, Kb, D), lambda i: (0, 0, 0))],
        out_specs=[row, row, pl.BlockSpec((G, tm, Kb), lambda i: (0, i, 0)), vec, vec, vec],
        out_shape=[_sds((S, D), F32), _sds((S, D), CDT), _sds((G, S, Kb), du_dtype)] + [_sds((1, D), F32)] * 3,
        compiler_params=_params(("arbitrary",)), name=name)(dy, z, gain, w3)


def _half_select(low):
    r = lax.broadcasted_iota(jnp.int32, (2 * ATT_HD, ATT_HD), 0)
    c = lax.broadcasted_iota(jnp.int32, (2 * ATT_HD, ATT_HD), 1)
    return (r == c + (0 if low else ATT_HD)).astype(CDT)


def _half_place(low):
    r = lax.broadcasted_iota(jnp.int32, (ATT_HD, 2 * ATT_HD), 0)
    c = lax.broadcasted_iota(jnp.int32, (ATT_HD, 2 * ATT_HD), 1)
    return (c == r + (0 if low else ATT_HD)).astype(CDT)


def _pair_lanes(even, odd):
    return (jnp.dot(even, _half_place(True), preferred_element_type=F32)
            + jnp.dot(odd, _half_place(False), preferred_element_type=F32)).astype(CDT)


def _proj_heads(a, w, bias, heads, name):
    S, K = a.shape
    N = heads * ATT_HD
    tm = _tile(S)

    def body(a_ref, w_ref, b_ref, o_ref):
        acc = (_dot(a_ref[...], w_ref[...]) + b_ref[...]).astype(CDT)
        sel = (_half_select(True), _half_select(False))
        for h in range(heads):
            pair = acc[:, (h // 2) * 2 * ATT_HD:(h // 2 + 1) * 2 * ATT_HD]
            o_ref[h] = jnp.dot(pair, sel[h % 2], preferred_element_type=F32).astype(CDT)

    return _call(
        None, body, grid=(S // tm,),
        in_specs=[pl.BlockSpec((tm, K), lambda i: (i, 0)), pl.BlockSpec((K, N), lambda i: (0, 0)),
                  pl.BlockSpec((1, N), lambda i: (0, 0))],
        out_specs=pl.BlockSpec((heads, tm, ATT_HD), lambda i: (0, i, 0)), out_shape=_sds((heads, S, ATT_HD), CDT),
        compiler_params=_params(("arbitrary",)), name=name)(a, w, bias)


def _qkv_bwd(dz, dq, dkv4, wq, wkv, name, after=None):
    S = dz.shape[0]
    tm = _tile(S)
    HK = dkv4.shape[0]
    NK = HK * ATT_HD

    def body(dz_ref, dq_ref, dkv_ref, wq_ref, wkv_ref, dx_ref, dkvn_ref, dkvb_ref):
        first = pl.program_id(0) == 0
        dkvn = jnp.concatenate([_pair_lanes(dkv_ref[2 * i].astype(CDT), dkv_ref[2 * i + 1].astype(CDT))
                                for i in range(HK // 2)], axis=-1)
        dkvn_ref[...] = dkvn
        dx_ref[...] = ALPHA * dz_ref[...] + _dot_nt(dq_ref[...], wq_ref[...]) + _dot_nt(dkvn, wkv_ref[...])
        for h in range(HK):
            _acc(dkvb_ref.at[h], _colsum(dkv_ref[h]), first)

    row = pl.BlockSpec((tm, D), lambda i: (i, 0))
    return _call(
        after, body, grid=(S // tm,),
        in_specs=[row, row, pl.BlockSpec((HK, tm, ATT_HD), lambda i: (0, i, 0)),
                  pl.BlockSpec((D, D), lambda i: (0, 0)), pl.BlockSpec((D, NK), lambda i: (0, 0))],
        out_specs=[row, pl.BlockSpec((tm, NK), lambda i: (i, 0)), pl.BlockSpec((HK, 1, ATT_HD), lambda i: (0, 0, 0))],
        out_shape=[_sds((S, D), F32), _sds((S, NK), CDT), _sds((HK, 1, ATT_HD), F32)],
        compiler_params=_params(("arbitrary",)), name=name)(dz, dq, dkv4, wq, wkv)


def _inproj_fwd(xb, wain, name):
    S = xb.shape[0]
    tm = _tile(S) // 2
    nb = wain.shape[-1]

    def body(x_ref, w_ref, o_ref):
        x = x_ref[...]
        for j in range(N_DEV):
            o_ref[j // 2, :, pl.ds((j % 2) * nb, nb)] = _dot(x, w_ref[j])

    return _call(
        None, body, grid=(S // tm,),
        in_specs=[pl.BlockSpec((tm, D), lambda i: (i, 0)), pl.BlockSpec((N_DEV, D, nb), lambda i: (0, 0, 0))],
        out_specs=pl.BlockSpec((4, tm, D), lambda i: (0, i, 0)), out_shape=_sds((4, S, D), F32),
        compiler_params=_params(("arbitrary",)), name=name)(xb, wain)


def _inproj_bwd(dz, dproj, wain, name, after=None):
    S = dz.shape[0]
    tm = _tile(S)
    nb = wain.shape[-1]

    def body(dz_ref, dp_ref, w_ref, dx_ref):
        acc = ALPHA * dz_ref[...]
        for j in range(N_DEV):
            acc = acc + _dot_nt(dp_ref[j // 2, :, pl.ds((j % 2) * nb, nb)], w_ref[j])
        dx_ref[...] = acc

    row = pl.BlockSpec((tm, D), lambda i: (i, 0))
    return _call(
        after, body, grid=(S // tm,),
        in_specs=[row, pl.BlockSpec((4, tm, D), lambda i: (0, i, 0)), pl.BlockSpec((N_DEV, D, nb), lambda i: (0, 0, 0))],
        out_specs=row, out_shape=_sds((S, D), F32),
        compiler_params=_params(("arbitrary",)), name=name)(dz, dproj, wain)


def _running_sum(x, reverse=False):
    rows = x.shape[0]
    row = lax.broadcasted_iota(jnp.int32, x.shape, 0)
    step = 1
    while step < rows:
        if reverse:
            x = x + jnp.where(row < rows - step, pltpu.roll(x, rows - step, 0), 0.0)
        else:
            x = x + jnp.where(row >= step, pltpu.roll(x, step, 0), 0.0)
        step *= 2
    return x


def _hg_gates(q, f, alb_ref):
    a0, a1 = alb_ref[0:1, :], alb_ref[1:2, :]
    mx = jnp.maximum(a0, a1)
    e0, e1 = jnp.exp(a0 - mx), jnp.exp(a1 - mx)
    lb = e0 / (e0 + e1)
    sig = _sigmoid(f)
    forget = lb + (1.0 - lb) * sig
    k = (1.0 - lb) * _sigmoid(-f)
    qs = q * _sigmoid(q) * (HG_DK ** -0.5)
    return qs, k, jnp.log(forget), sig, lb, forget


def _hg_intra(qs, k, b, b_scr):
    b_scr[...] = b
    bm = b_scr[pl.ds(HG_CH // 2 - 1, 1), :]
    bl = b_scr[pl.ds(HG_CH - 1, 1), :]
    eb = jnp.exp(b)
    qb = qs * eb
    e_q = jnp.exp(b - bm)
    e_k = jnp.exp(bm - b)
    e_d = jnp.exp(bl - b)
    return qb, qs * e_q, k * e_k, k * e_d, jnp.exp(bl), eb, e_q, e_k, e_d


def _hgrn_fwd(proj, alb, ngain):
    S = proj.shape[1]
    nc = S // HG_CH
    nb = nc // HG_CPB
    rb, wb = HG_CPB * HG_CH, HG_HPB * HG_DK

    def body(pj_ref, alb_ref, ng_ref, o_ref, y_ref, st_ref, st_scr, b_scr):
        n = pl.program_id(1)

        @pl.when(n == 0)
        def _():
            st_scr[...] = jnp.zeros_like(st_scr)

        r = lax.broadcasted_iota(jnp.int32, (HG_CH, HG_CH), 0)
        c = lax.broadcasted_iota(jnp.int32, (HG_CH, HG_CH), 1)
        causal = r >= c
        for ci, j in [(ci, j) for ci in range(HG_CPB) for j in range(HG_HPB)]:
            rows, lanes = pl.ds(ci * HG_CH, HG_CH), pl.ds(j * HG_DK, HG_DK)
            q, f, v, g = pj_ref[0, rows, lanes], pj_ref[1, rows, lanes], pj_ref[2, rows, lanes], pj_ref[3, rows, lanes]
            qs, k, logf, _, _, _ = _hg_gates(q, f, alb_ref.at[:, lanes])
            b = _running_sum(logf)
            qb, qt, kt, kd, ebl, _, _, _, _ = _hg_intra(qs, k, b, b_scr.at[j, ci])
            st = st_scr[j]
            st_ref[j, ci] = st
            a = jnp.where(causal, _dot_nt(qt, kt), 0.0)
            o = _dot(a, v) + _dot_nt(qb, st)
            st_scr[j] = st * ebl + _dot_tn(v, kd)
            o_ref[rows, lanes] = o
            rinv = lax.rsqrt(jnp.mean(o * o, axis=-1, keepdims=True) + RMS_EPS)
            y_ref[rows, lanes] = (o * rinv * ng_ref[...] * (g * _sigmoid(g))).astype(CDT)

    blk = pl.BlockSpec((rb, wb), lambda h, n: (n, h))
    return _call(
        None, body, grid=(HG_H // HG_HPB, nb),
        in_specs=[pl.BlockSpec((4, rb, wb), lambda h, n: (0, n, h)), pl.BlockSpec((2, wb), lambda h, n: (0, h)),
                  pl.BlockSpec((1, HG_DK), lambda h, n: (0, 0))],
        out_specs=[blk, blk, pl.BlockSpec((HG_HPB, HG_CPB, HG_DK, HG_DK), lambda h, n: (h, n, 0, 0))],
        out_shape=[_sds((S, D), F32), _sds((S, D), CDT), _sds((HG_H, nc, HG_DK, HG_DK), F32)],
        scratch_shapes=[pltpu.VMEM((HG_HPB, HG_DK, HG_DK), F32), pltpu.VMEM((HG_HPB, HG_CPB, HG_CH, HG_DK), F32)],
        compiler_params=_params(("arbitrary", "arbitrary")), name="hgrn_fwd")(proj, alb, ngain)


def _hgrn_bwd(proj, alb, ngain, o, states, dy, after=None):
    S = proj.shape[1]
    nc = S // HG_CH
    nb = nc // HG_CPB
    rb, wb = HG_CPB * HG_CH, HG_HPB * HG_DK

    def body(pj_ref, alb_ref, ng_ref, o_ref, st_ref, dy_ref, dpj_ref, dalb_ref, dng_ref, dst_scr, b_scr):
        h, n = pl.program_id(0), pl.program_id(1)

        @pl.when(n == 0)
        def _():
            dst_scr[...] = jnp.zeros_like(dst_scr)
            dalb_ref[...] = jnp.zeros_like(dalb_ref)

        _zero_at(dng_ref, jnp.logical_and(h == 0, n == 0))
        ng = ng_ref[...]
        r = lax.broadcasted_iota(jnp.int32, (HG_CH, HG_CH), 0)
        c = lax.broadcasted_iota(jnp.int32, (HG_CH, HG_CH), 1)
        causal = r >= c
        dng = None
        for ci, j in [(ci, j) for ci in reversed(range(HG_CPB)) for j in range(HG_HPB)]:
            rows, lanes = pl.ds(ci * HG_CH, HG_CH), pl.ds(j * HG_DK, HG_DK)
            q, f, v, g = pj_ref[0, rows, lanes], pj_ref[1, rows, lanes], pj_ref[2, rows, lanes], pj_ref[3, rows, lanes]
            o_ = o_ref[rows, lanes]
            dy_ = dy_ref[rows, lanes]
            sg = _sigmoid(g)
            rinv = lax.rsqrt(jnp.mean(o_ * o_, axis=-1, keepdims=True) + RMS_EPS)
            nrm = o_ * rinv
            dr = dy_ * (g * sg)
            dg = dy_ * nrm * ng * (sg * (1.0 + g * (1.0 - sg)))
            dn = dr * ng
            do = rinv * (dn - nrm * jnp.mean(dn * nrm, axis=-1, keepdims=True))
            dng = _colsum(dr * nrm) if dng is None else dng + _colsum(dr * nrm)
            qs, k, logf, sig, lb, forget = _hg_gates(q, f, alb_ref.at[:, lanes])
            b = _running_sum(logf)
            qb, qt, kt, kd, ebl, eb, e_q, e_k, e_d = _hg_intra(qs, k, b, b_scr.at[j, ci])
            st = st_ref[j, ci]
            dstn = dst_scr[j]
            qt, kt, qb, kd = (t.astype(CDT).astype(F32) for t in (qt, kt, qb, kd))
            a = jnp.where(causal, _dot_nt(qt, kt), 0.0)
            da = jnp.where(causal, _dot_nt(do, v), 0.0)
            dv = _dot_tn(a, do) + _dot_nt(kd, dstn)
            dqb = _dot(do, st)
            dkd = _dot(v, dstn)
            dqt = _dot(da, kt)
            dkt = _dot_tn(da, qt)
            dbl = _colsum(dkd * kd) + ebl * _colsum(dstn * st)
            dst_scr[j] = dstn * ebl + _dot_tn(do, qb)
            dqs = dqt * e_q + dqb * eb
            dk = dkt * e_k + dkd * e_d
            db = dqt * qt + dqb * qb - dkt * kt - dkd * kd
            dlogf = _running_sum(db, reverse=True) + dbl
            dforget = dlogf / forget
            dsig = (1.0 - lb) * (dforget - dk)
            df = dsig * sig * (1.0 - sig)
            dlb = _colsum((dforget - dk) * (1.0 - sig))
            sq = _sigmoid(q)
            dq = dqs * (HG_DK ** -0.5) * (sq * (1.0 + q * (1.0 - sq)))
            dpj_ref[0, rows, lanes] = dq.astype(CDT)
            dpj_ref[1, rows, lanes] = df.astype(CDT)
            dpj_ref[2, rows, lanes] = dv.astype(CDT)
            dpj_ref[3, rows, lanes] = dg.astype(CDT)
            da0 = dlb * lb * (1.0 - lb)
            dalb_ref[pl.ds(0, 1), lanes] += da0
            dalb_ref[pl.ds(1, 1), lanes] -= da0
        dng_ref[...] += dng

    blk = pl.BlockSpec((rb, wb), lambda h, n: (nb - 1 - n, h))
    pj = pl.BlockSpec((4, rb, wb), lambda h, n: (0, nb - 1 - n, h))
    alb_blk = pl.BlockSpec((2, wb), lambda h, n: (0, h))
    ng_blk = pl.BlockSpec((1, HG_DK), lambda h, n: (0, 0))
    return _call(
        after, body, grid=(HG_H // HG_HPB, nb),
        in_specs=[pj, alb_blk, ng_blk, blk,
                  pl.BlockSpec((HG_HPB, HG_CPB, HG_DK, HG_DK), lambda h, n: (h, nb - 1 - n, 0, 0)), blk],
        out_specs=[pj, alb_blk, ng_blk],
        out_shape=[_sds((4, S, D), CDT), _sds((2, D), F32), _sds((1, HG_DK), F32)],
        scratch_shapes=[pltpu.VMEM((HG_HPB, HG_DK, HG_DK), F32), pltpu.VMEM((HG_HPB, HG_CPB, HG_CH, HG_DK), F32)],
        compiler_params=_params(("arbitrary", "arbitrary")), name="hgrn_bwd")(proj, alb, ngain, o, states, dy)


def _slope(h):
    return 2.0 ** (-8.0 * (h + 1) / ATT_QH)


def _attn_mask(n):
    qi = lax.broadcasted_iota(jnp.int32, (WINDOW, 2 * WINDOW), 0)
    si = lax.broadcasted_iota(jnp.int32, (WINDOW, 2 * WINDOW), 1)
    dist = qi - si + WINDOW
    valid = (dist >= 0) & (dist < WINDOW) & (n * WINDOW - WINDOW + si >= 0)
    return valid, dist.astype(F32)


def _attn_probs(qk, sink, slope, valid, distf):
    s = qk * (ATT_HD ** -0.5) - slope * distf
    s = jnp.where(valid, s, NEG)
    m = jnp.maximum(jnp.max(s, axis=-1, keepdims=True), sink)
    e = jnp.exp(s - m)
    es = jnp.exp(sink - m)
    inv = 1.0 / (jnp.sum(e, axis=-1, keepdims=True) + es)
    return e * inv, es * inv


def _attn_specs(S):
    steps = S // (ATT_BPB * WINDOW)
    cur = lambda H: pl.BlockSpec((H, ATT_BPB * WINDOW, ATT_HD), lambda n: (0, n, 0))
    prev = lambda H: pl.BlockSpec((H, WINDOW, ATT_HD), lambda n: (0, jnp.maximum(ATT_BPB * n - 1, 0), 0))
    return steps, cur, prev


def _attn_kv(kvc_ref, kvp_ref, head, bi):
    before = kvp_ref[head] if bi == 0 else kvc_ref[head, pl.ds((bi - 1) * WINDOW, WINDOW), :]
    return jnp.concatenate([before, kvc_ref[head, pl.ds(bi * WINDOW, WINDOW), :]], axis=0)


def _attn_fwd(q4, kv4, sinks):
    S = q4.shape[1]
    nb, cur, prev = _attn_specs(S)

    def body(sink_ref, q_ref, kvc_ref, kvp_ref, o_ref, p_ref, ps_ref):
        lane = lax.broadcasted_iota(jnp.int32, (1, 128), 1)
        place = (_half_place(True), _half_place(False))
        masks = [_attn_mask(pl.program_id(0) * ATT_BPB + bi) for bi in range(ATT_BPB)]
        sink_probs = [jnp.zeros((WINDOW, 128), F32) for _ in range(ATT_BPB)]
        heads = lambda kvh: range(kvh * ATT_G, (kvh + 1) * ATT_G)

        def scores(bi, kvh):
            rows = pl.ds(bi * WINDOW, WINDOW)
            kh = _attn_kv(kvc_ref, kvp_ref, kvh, bi)
            vh = _attn_kv(kvc_ref, kvp_ref, ATT_KVH + kvh, bi)
            v_pl = [jnp.dot(vh, m, preferred_element_type=F32).astype(CDT) for m in place]
            return v_pl, [_dot_nt(q_ref[h, rows, :], kh) for h in heads(kvh)]

        def softmax(bi, kvh, qks):
            rows = pl.ds(bi * WINDOW, WINDOW)
            probs = []
            for h, qk in zip(heads(kvh), qks):
                p, ps = _attn_probs(qk, sink_ref[0, h], _slope(h), *masks[bi])
                probs.append(p.astype(CDT))
                p_ref[h, rows, :] = probs[-1]
                sink_probs[bi] = sink_probs[bi] + jnp.where(lane == h, ps, 0.0)
            return probs

        def weighted_values(bi, kvh, v_pl, probs):
            rows = pl.ds(bi * WINDOW, WINDOW)
            for i in range(ATT_G // 2):
                lanes = pl.ds((kvh * ATT_G + 2 * i) * ATT_HD, 2 * ATT_HD)
                o_ref[rows, lanes] = (_dot(probs[2 * i], v_pl[0]) + _dot(probs[2 * i + 1], v_pl[1])).astype(CDT)

        groups = [(bi, kvh) for bi in range(ATT_BPB) for kvh in range(ATT_KVH)]
        ahead = scores(*groups[0])
        for gi, g in enumerate(groups):
            v_pl, qks = ahead
            probs = softmax(*g, qks)
            if gi + 1 < len(groups):
                ahead = scores(*groups[gi + 1])
            weighted_values(*g, v_pl, probs)
        for bi in range(ATT_BPB):
            ps_ref[pl.ds(bi * WINDOW, WINDOW), :] = sink_probs[bi]

    rows_spec = pl.BlockSpec((ATT_BPB * WINDOW, D), lambda n: (n, 0))
    return _call(
        None, body, grid=(nb,),
        in_specs=[pl.BlockSpec(memory_space=pltpu.SMEM), cur(ATT_QH), cur(2 * ATT_KVH), prev(2 * ATT_KVH)],
        out_specs=[rows_spec, pl.BlockSpec((ATT_QH, ATT_BPB * WINDOW, 2 * WINDOW), lambda n: (0, n, 0)),
                   pl.BlockSpec((ATT_BPB * WINDOW, 128), lambda n: (n, 0))],
        out_shape=[_sds((S, D), CDT), _sds((ATT_QH, S, 2 * WINDOW), CDT), _sds((S, 128), F32)],
        compiler_params=_params(("arbitrary",)), name="attn_fwd")(sinks, q4, kv4, kv4)


def _attn_bwd(q4, kv4, probs, sink_probs, do):
    S = q4.shape[1]
    nb, cur, prev = _attn_specs(S)

    def body(q_ref, kvc_ref, kvp_ref, p_ref, ps_ref, do_ref, dq_ref, dkv_ref, dbq_ref, dsink_ref):
        n = pl.program_id(0)
        first = n == 0

        @pl.when(first)
        def _():
            dkv_ref[...] = jnp.zeros_like(dkv_ref)
            dsink_ref[...] = jnp.zeros_like(dsink_ref)
            dbq_ref[...] = jnp.zeros_like(dbq_ref)

        lane = lax.broadcasted_iota(jnp.int32, (1, 128), 1)
        place = (_half_place(True), _half_place(False))
        row_dots = [jnp.zeros((WINDOW, 128), F32) for _ in range(ATT_BPB)]
        heads = lambda kvh: range(kvh * ATT_G, (kvh + 1) * ATT_G)
        pair_lanes = lambda h: pl.ds((h // 2) * 2 * ATT_HD, 2 * ATT_HD)

        def products_of_do(bi, kvh):
            rows = pl.ds(bi * WINDOW, WINDOW)
            kh = _attn_kv(kvc_ref, kvp_ref, kvh, bi)
            vh = _attn_kv(kvc_ref, kvp_ref, ATT_KVH + kvh, bi)
            k_pl = [jnp.dot(kh, m, preferred_element_type=F32).astype(CDT) for m in place]
            v_pl = [jnp.dot(vh, m, preferred_element_type=F32).astype(CDT) for m in place]
            dps = [_dot_nt(do_ref[rows, pair_lanes(h)], v_pl[h % 2]) for h in heads(kvh)]
            dv2 = [sum(_dot_tn(p_ref[h, rows, :], do_ref[rows, pair_lanes(h)]) for h in heads(kvh) if h % 2 == par)
                   for par in range(2)]
            return k_pl, dps, dv2

        def softmax_bwd(bi, kvh, dps):
            rows = pl.ds(bi * WINDOW, WINDOW)
            dss = []
            for h, dp in zip(heads(kvh), dps):
                p = p_ref[h, rows, :].astype(F32)
                dd = jnp.sum(p * dp, axis=-1, keepdims=True)
                dss.append((p * (dp - dd)).astype(CDT))
                row_dots[bi] = row_dots[bi] + jnp.where(lane == h, dd, 0.0)
            return dss

        def products_of_ds(bi, kvh, k_pl, dss, dv2):
            block = n * ATT_BPB + bi
            rows = pl.ds(bi * WINDOW, WINDOW)
            rows_cur = pl.ds(pl.multiple_of(block * WINDOW, WINDOW), WINDOW)
            rows_prev = pl.ds(pl.multiple_of(jnp.maximum(block - 1, 0) * WINDOW, WINDOW), WINDOW)
            for i in range(ATT_G // 2):
                h = kvh * ATT_G + 2 * i
                dq2 = (_dot(dss[2 * i], k_pl[0]) + _dot(dss[2 * i + 1], k_pl[1])) * (ATT_HD ** -0.5)
                dq_ref[rows, pair_lanes(h)] = dq2.astype(CDT)
                dbq_ref[:, pair_lanes(h)] += _colsum(dq2)
            dk = sum(_dot_tn(ds, q_ref[h, rows, :]) for h, ds in zip(heads(kvh), dss)) * (ATT_HD ** -0.5)
            dv = dv2[0][:, :ATT_HD] + pltpu.roll(dv2[1], ATT_HD, 1)[:, :ATT_HD]
            dkv_ref[kvh, rows_prev, :] += dk[:WINDOW]
            dkv_ref[kvh, rows_cur, :] += dk[WINDOW:]
            dkv_ref[ATT_KVH + kvh, rows_prev, :] += dv[:WINDOW]
            dkv_ref[ATT_KVH + kvh, rows_cur, :] += dv[WINDOW:]

        groups = [(bi, kvh) for bi in range(ATT_BPB) for kvh in range(ATT_KVH)]
        ahead = products_of_do(*groups[0])
        for gi, g in enumerate(groups):
            k_pl, dps, dv2 = ahead
            dss = softmax_bwd(*g, dps)
            if gi + 1 < len(groups):
                ahead = products_of_do(*groups[gi + 1])
            products_of_ds(*g, k_pl, dss, dv2)
        dsinks = jnp.zeros((1, 128), F32)
        for bi in range(ATT_BPB):
            dsinks = dsinks - _colsum(ps_ref[pl.ds(bi * WINDOW, WINDOW), :] * row_dots[bi])
        dsink_ref[...] += dsinks

    rows_spec = pl.BlockSpec((ATT_BPB * WINDOW, D), lambda n: (n, 0))
    return _call(
        None, body, grid=(nb,),
        in_specs=[cur(ATT_QH), cur(2 * ATT_KVH), prev(2 * ATT_KVH),
                  pl.BlockSpec((ATT_QH, ATT_BPB * WINDOW, 2 * WINDOW), lambda n: (0, n, 0)),
                  pl.BlockSpec((ATT_BPB * WINDOW, 128), lambda n: (n, 0)), rows_spec],
        out_specs=[rows_spec, pl.BlockSpec((2 * ATT_KVH, S, ATT_HD), lambda n: (0, 0, 0)),
                   pl.BlockSpec((1, D), lambda n: (0, 0)), pl.BlockSpec((1, 128), lambda n: (0, 0))],
        out_shape=[_sds((S, D), CDT), _sds((2 * ATT_KVH, S, ATT_HD), F32), _sds((1, D), F32),
                   _sds((1, 128), F32)],
        compiler_params=_params(("arbitrary",)), name="attn_bwd")(q4, kv4, kv4, probs, sink_probs, do)


def _local_step(x, p, target, getw, sm, emit):
    S = x.shape[0]
    vec = lambda a: a.reshape(1, -1)
    ln_g = lambda l, k: vec(sm["ln_gain"][l, k])
    ln_b = lambda l, k: vec(sm["ln_bias"][l, k])
    xb = x.astype(CDT)
    pb = p.astype(CDT)

    proj = _inproj_fwd(xb, getw("a_w_in"), "a_in")
    o_a, y_a, states = _hgrn_fwd(proj, sm["a_lower_bound"], sm["a_norm_gain"])
    zeros = jnp.zeros((1, D), F32)
    z = [[None] * 3 for _ in range(2)]
    xs = [[None] * 3 for _ in range(2)]
    xbs = [[None] * 3 for _ in range(2)]
    z[0][0], xs[0][0], xbs[0][0] = _mixout_ln(y_a[None], getw("a_w_out")[None], zeros, x, ln_g(0, 0), ln_b(0, 0),
                                              "a_out_ln")
    gu, hid, sgs, ups = [None, None], [None, None], [None, None], [None, None]

    def ffn_ple(l, target=None):
        wgu = getw(f"gu{l}")
        gu[l], hid[l], z[l][1], xs[l][1], xbs[l][1] = _ffn_fwd(
            xs[l][0], xbs[l][0], wgu, getw(f"dn{l}"), ln_g(l, 1), ln_b(l, 1), f"ffn_fwd{l}")
        sgs[l], ups[l], z[l][2], *out = _ple_fwd(
            xs[l][1], xbs[l][1], pb[l], getw(f"pg{l}"), vec(sm["ple_b_gate"][l]), getw(f"pu{l}"), ln_g(l, 2),
            ln_b(l, 2), f"ple_fwd{l}", target=target)
        if target is None:
            xs[l][2], xbs[l][2] = out
        return out

    ffn_ple(0)
    x3, x3b = xs[0][2], xbs[0][2]
    w_kv, w_q, w_bo = getw("kv_w"), getw("b_w_q"), getw("b_w_out")
    kv4 = _proj_heads(x3b, w_kv, vec(sm["kv_b"]), 2 * ATT_KVH, "kv_proj")
    q4 = _proj_heads(x3b, w_q, vec(sm["b_b_q"]), ATT_QH, "q_proj")
    o_b, probs, sink_probs = _attn_fwd(q4, kv4, sm["b_sinks"])
    z[1][0], xs[1][0], xbs[1][0] = _mixout_ln(o_b[None], w_bo[None], sm["b_b_out"], x3, ln_g(1, 0), ln_b(1, 0),
                                              "b_out_ln")
    dy, loss = ffn_ple(1, target)

    gs = {}
    d_ln_g = [[None] * 3 for _ in range(2)]
    d_ln_b = [[None] * 3 for _ in range(2)]
    g_bg = [None, None]

    def ffn_ple_bwd(l, dy, after=None):
        dz2, dzb, dgl, dup, d_ln_g[l][2], d_ln_b[l][2], g_bg[l], d_ln_g[l][1], d_ln_b[l][1] = _ple_bwd(
            dy, z[l][2], sgs[l], ups[l], ln_g(l, 2), getw(f"pg{l}"), z[l][1], ln_g(l, 1), f"ple_bwd{l}", after=after)
        g_pg = _wgrad(xbs[l][1][None], dgl[None], f"g_ple_gate{l}")[0]
        g_pu = _wgrad(pb[l][None], dup[None], f"g_ple_up{l}")[0]
        g_dn = _wgrad(hid[l], dzb[None], f"g_ffn_down{l}")
        if l == 1:
            tok = emit({f"pg{l}": g_pg, f"pu{l}": g_pu})
            tok = tok + emit({f"dn{l}": g_dn})
        else:
            tok = emit({f"pg{l}": g_pg, f"pu{l}": g_pu, f"dn{l}": g_dn})
        dgu = _ffn_bwd_hidden(dzb, gu[l], getw(f"dn{l}"), f"ffn_bwd{l}", after=tok)
        g_gu = _wgrad(dgu.reshape(8, S, FFN_B), xbs[l][0][None], f"g_ffn_gate_up{l}")
        tok = emit({f"gu{l}": g_gu})
        dx1 = _ffn_bwd_input(dz2, dgu, getw(f"gu{l}"), f"ffn_bwd{l}", after=tok)
        return dx1, None

    dx1, tok = ffn_ple_bwd(1, dy)
    dz, dzb, do, d_ln_g[1][0], d_ln_b[1][0], gs["b_b_out"] = _mixout_bwd(dx1, z[1][0], ln_g(1, 0), w_bo[None], CDT,
                                                                        "b_out_bwd", after=tok)
    g_bo = _wgrad(o_b[None], dzb[None], "g_b_w_out")[0]
    dq, dkv4, dbq, dsinks = _attn_bwd(q4, kv4, probs, sink_probs, do[0])
    gs["b_b_q"] = dbq
    gs["b_sinks"] = dsinks
    g_q = _wgrad(x3b[None], dq[None], "g_b_w_q")[0]
    dx3, dkv, gs["kv_b"] = _qkv_bwd(dz, dq, dkv4, w_q, w_kv, "qkv_bwd", after=tok)
    g_kv = _wgrad(x3b[None], dkv[None], "g_kv_w")[0]
    tok = emit({"b_w_out": g_bo, "b_w_q": g_q, "kv_w": g_kv})
    dx1, tok = ffn_ple_bwd(0, dx3, tok)
    w_ao = getw("a_w_out")
    dz, dzb, dyr, d_ln_g[0][0], d_ln_b[0][0], _ = _mixout_bwd(dx1, z[0][0], ln_g(0, 0), w_ao[None], F32, "a_out_bwd",
                                                              after=tok)
    g_ao = _wgrad(y_a[None], dzb[None], "g_a_w_out")[0]
    dproj, gs["a_lower_bound"], gs["a_norm_gain"] = _hgrn_bwd(proj, sm["a_lower_bound"], sm["a_norm_gain"], o_a, states,
                                                              dyr[0], after=[g_ao])
    tk = lambda t: (None, t, D)
    g_ain = _mm_tn(xb[None], dproj, N_DEV, lambda g, k: (0, k, 0), lambda g, k: (g // 2, k, g % 2),
                   tk, lambda t: (None, t, 512), (N_DEV, D, 512), (None, D, 512), lambda g, k: (g, 0, 0), name="g_a_w_in")
    gs["ple_b_gate"] = jnp.concatenate(g_bg, axis=0)
    gs["ln_gain"] = jnp.stack([jnp.concatenate(r, axis=0) for r in d_ln_g])
    gs["ln_bias"] = jnp.stack([jnp.concatenate(r, axis=0) for r in d_ln_b])
    gs["loss"] = loss
    tok = emit({"a_w_out": g_ao, "a_w_in": g_ain}, small=gs)
    grad_x = _inproj_bwd(dz, dproj, getw("a_w_in"), "a_in_bwd", after=tok)
    return loss, grad_x, gs


def _peer(k):
    x, y, c = lax.axis_index("x"), lax.axis_index("y"), lax.axis_index("c")
    px = 1 - x if k & 4 else x
    py = 1 - y if k & 2 else y
    pc = 1 - c if k & 1 else c
    return (px, py, pc), 4 * px + 2 * py + pc


def _my_index():
    return 4 * lax.axis_index("x") + 2 * lax.axis_index("y") + lax.axis_index("c")


def _piece_copy(mode, src, land, send_sems, recv_sems, t, k, sender, receiver, peer):
    return pltpu.make_async_remote_copy(
        src_ref=src if mode == "gather" else src.at[receiver], dst_ref=land.at[sender],
        send_sem=send_sems.at[t * 7 + k - 1], recv_sem=recv_sems.at[t * 7 + k - 1], device_id=peer, device_id_type=MESH)


def _sequencer_exchange(srcs, modes, name, collective_id, after=None):
    n = len(srcs)
    land_shapes = [((N_DEV,) + a.shape) if mode == "gather" else a.shape for a, mode in zip(srcs, modes)]
    extra = [] if after is None else [after]

    def body(*refs):
        src_refs, land_refs = refs[:n], refs[n + len(extra):2 * n + len(extra)]
        send_sems, recv_sems, local_sems = refs[2 * n + len(extra):]
        barrier = pltpu.get_barrier_semaphore()
        for k in range(1, N_DEV):
            pl.semaphore_signal(barrier, inc=1, device_id=_peer(k)[0], device_id_type=MESH)
        pl.semaphore_wait(barrier, N_DEV - 1)
        me = _my_index()
        local = []
        for i in range(n):
            cp = pltpu.make_async_copy(src_refs[i] if modes[i] == "gather" else src_refs[i].at[me], land_refs[i].at[me],
                                       local_sems.at[i])
            cp.start()
            local.append(cp)
        for k in range(1, N_DEV):
            peer, pid = _peer(k)
            for t in range(n):
                _piece_copy(modes[t], src_refs[t], land_refs[t], send_sems, recv_sems, t, k, me, pid, peer).start()
        for k in range(1, N_DEV):
            peer, pid = _peer(k)
            for t in range(n):
                _piece_copy(modes[t], src_refs[t], land_refs[t], send_sems, recv_sems, t, k, pid, me, peer).wait_recv()
        for k in range(1, N_DEV):
            peer, pid = _peer(k)
            for t in range(n):
                _piece_copy(modes[t], src_refs[t], land_refs[t], send_sems, recv_sems, t, k, me, pid, peer).wait_send()
        for cp in local:
            cp.wait()

    return pl.kernel(
        body, out_type=[_sds(s, a.dtype) for s, a in zip(land_shapes, srcs)],
        mesh=plsc.ScalarSubcoreMesh(axis_name="sequencer", num_cores=1),
        scratch_types=[pltpu.SemaphoreType.DMA((7 * n,)), pltpu.SemaphoreType.DMA((7 * n,)), pltpu.SemaphoreType.DMA((n,))],
        compiler_params=pltpu.CompilerParams(collective_id=collective_id), name=name)(*srcs, *extra)


def _sequencer_gather(srcs, name, collective_id, after=None):
    n = len(srcs)
    extra = [] if after is None else [after]

    def body(*refs):
        src_refs, land_refs = refs[:n], refs[n + len(extra):2 * n + len(extra)]
        send_sems, recv_sems, local_sems = refs[2 * n + len(extra):]
        x, y, c = lax.axis_index("x"), lax.axis_index("y"), lax.axis_index("c")
        sibling = (x, y, 1 - c)
        chips = [(1 - x, y), (x, 1 - y), (1 - x, 1 - y)]
        index = lambda px, py, pc: 4 * px + 2 * py + pc
        barrier = pltpu.get_barrier_semaphore()
        for peer in [sibling] + [(*chip, c) for chip in chips]:
            pl.semaphore_signal(barrier, inc=1, device_id=peer, device_id_type=MESH)
        pl.semaphore_wait(barrier, 4)

        def copy(t, k, slot, to, src=None):
            return pltpu.make_async_remote_copy(
                src_ref=land_refs[t].at[slot] if src is None else src, dst_ref=land_refs[t].at[slot],
                send_sem=send_sems.at[7 * t + k], recv_sem=recv_sems.at[7 * t + k], device_id=to, device_id_type=MESH)

        me = index(x, y, c)
        local = []
        for t in range(n):
            cp = pltpu.make_async_copy(src_refs[t], land_refs[t].at[me], local_sems.at[t])
            cp.start()
            local.append(cp)
        sends = []
        for t in range(n):
            sends.append(copy(t, 0, me, sibling, src=src_refs[t]))
            sends += [copy(t, 1 + j, me, (*chip, c), src=src_refs[t]) for j, chip in enumerate(chips)]
        for cp in sends:
            cp.start()
        for j, chip in enumerate(chips):
            for t in range(n):
                copy(t, 1 + j, index(*chip, c), sibling, src=src_refs[t]).wait_recv()
                passed = copy(t, 4 + j, index(*chip, c), sibling)
                passed.start()
                sends.append(passed)
        for t in range(n):
            copy(t, 0, index(x, y, 1 - c), sibling, src=src_refs[t]).wait_recv()
        for j, chip in enumerate(chips):
            for t in range(n):
                copy(t, 4 + j, index(*chip, 1 - c), sibling, src=src_refs[t]).wait_recv()
        for cp in sends:
            cp.wait_send()
        for cp in local:
            cp.wait()

    return pl.kernel(
        body, out_type=[_sds((N_DEV,) + a.shape, a.dtype) for a in srcs],
        mesh=plsc.ScalarSubcoreMesh(axis_name="sequencer", num_cores=1),
        scratch_types=[pltpu.SemaphoreType.DMA((7 * n,)), pltpu.SemaphoreType.DMA((7 * n,)), pltpu.SemaphoreType.DMA((n,))],
        compiler_params=pltpu.CompilerParams(collective_id=collective_id), name=name)(*srcs, *extra)


def _adamw(w, g, m, v):
    m = ADAM_B1 * m + (1.0 - ADAM_B1) * g
    v = ADAM_B2 * v + (1.0 - ADAM_B2) * (g * g)
    m_hat = m / (1.0 - ADAM_B1 ** ADAM_STEP)
    v_hat = v / (1.0 - ADAM_B2 ** ADAM_STEP)
    delta = -ADAM_LR * (m_hat / (jnp.sqrt(v_hat) + ADAM_EPS) + ADAM_WD * w)
    return delta, m, v


def _adam_big(w, parts, m, v, name, after=None):
    L, R, C = w.shape
    P = parts[0].shape[0]
    tr = _tile(R, (256, 128, 176, 64, 32, 16))
    nr = R // tr

    def body(w_ref, *refs):
        p_refs, (m_ref, v_ref, g_ref, d_ref, mo_ref, vo_ref) = refs[:L], refs[L:]
        for l in range(L):
            @pl.when(pl.program_id(0) == l)
            def _(p_ref=p_refs[l]):
                g = p_ref[0].astype(F32)
                for s in range(1, P):
                    g = g + p_ref[s].astype(F32)
                g_ref[...] = g
                d_ref[...], mo_ref[...], vo_ref[...] = _adamw(w_ref[...], g, m_ref[...], v_ref[...])

    row = pl.BlockSpec((None, tr, C), lambda l, i: (l, i, 0))
    park = lambda l_of: (lambda l, i: (0, jnp.where(l == l_of, i, 0 if l_of else nr - 1), 0))
    return _call(
        after, body, grid=(L, nr),
        in_specs=[row] + [pl.BlockSpec((P, tr, C), park(l)) for l in range(L)] + [row, row],
        out_specs=[row] * 4, out_shape=[_sds((L, R, C), F32)] * 4,
        compiler_params=_params(("arbitrary", "arbitrary")), name=name)(w, *parts, m, v)


SMALL = (("a_lower_bound", (2, 128), (2, D)), ("ln_gain", (3, 2, 128), (6, D)), ("ln_bias", (3, 2, 128), (6, D)),
         ("a_norm_gain", (1, 128), (1, 128)), ("kv_b", (1, 512), (1, 512)), ("b_b_q", (1, D), (1, D)),
         ("b_sinks", (1, ATT_QH), (1, 128)), ("b_b_out", (1, D), (1, D)), ("ple_b_gate", (2, D), (2, D)))


def _adam_small(parts, w, m, v, losses, after=None):
    k = len(SMALL)

    def body(*refs):
        p_refs, w_refs, m_refs, v_refs = refs[:k], refs[k:2 * k], refs[2 * k:3 * k], refs[3 * k:4 * k]
        loss_ref, outs, total_ref = refs[4 * k], refs[4 * k + 1:-1], refs[-1]
        total = loss_ref[0]
        for s in range(1, N_DEV):
            total = total + loss_ref[s]
        total_ref[...] = total
        me = _my_index()
        for i, (_, wshape, pshape) in enumerate(SMALL):
            cols = wshape[-1]
            lanes = slice(None) if cols == pshape[1] else (
                pl.ds(0, cols) if cols < 128 else pl.ds(pl.multiple_of(me * cols, cols), cols))
            at = [(slice(None), slice(None))] if len(wshape) == 2 else [
                (pl.ds(l * wshape[0] + kk, 1), (kk, pl.ds(l, 1), slice(None))) for kk in range(wshape[0]) for l in range(wshape[1])]
            for rows, own in at:
                g = p_refs[i][0, rows, lanes]
                for s in range(1, N_DEV):
                    g = g + p_refs[i][s, rows, lanes]
                g_ref, d_ref, mo_ref, vo_ref = outs[4 * i:4 * i + 4]
                g_ref[own] = g
                d_ref[own], mo_ref[own], vo_ref[own] = _adamw(w_refs[i][own], g, m_refs[i][own], v_refs[i][own])

    full = lambda shape: pl.BlockSpec(shape, lambda: (0,) * len(shape))
    names = [n for n, _, _ in SMALL]
    held = lambda a, ws: a.swapaxes(0, 1) if len(ws) == 3 else a.reshape(ws)
    back = lambda r, n: r.swapaxes(0, 1) if r.ndim == 3 else r.reshape(w[n].shape)
    res = _call(
        after, body,
        in_specs=[full((N_DEV,) + ps) for _, _, ps in SMALL] + [full(ws) for _, ws, _ in SMALL] * 3
        + [full((N_DEV, 1, 128))],
        out_specs=[full(ws) for _, ws, _ in SMALL for _ in range(4)] + [full((1, 128))],
        out_shape=[_sds(ws, F32) for _, ws, _ in SMALL for _ in range(4)] + [_sds((1, 128), F32)], name="adam_small")(
            *[parts[n] for n in names], *[held(a[n], ws) for a in (w, m, v) for n, ws, _ in SMALL], losses)
    return {n: [back(r, n) for r in res[4 * i:4 * i + 4]] for i, n in enumerate(names)}, res[-1][0, 0]


WEIGHTS = ("a_w_in", "a_lower_bound", "a_norm_gain", "a_w_out", "kv_w", "kv_b", "b_w_q", "b_b_q", "b_sinks", "b_w_out",
           "b_b_out", "ffn_w_gate_up", "ffn_w_down", "ple_w_up", "ple_w_gate", "ple_b_gate", "ln_gain", "ln_bias")


GATHER_GROUPS = (("a_w_in",), ("a_w_out", "gu0"), ("dn0", "pu0", "pg0"), ("kv_w", "b_w_q", "b_w_out"), ("gu1",),
                 ("dn1", "pu1", "pg1"))
KERNEL_LAYOUT = {
    "a_w_in": lambda a: a,
    "a_w_out": lambda a: a.reshape(D, D),
    "kv_w": lambda a: a.reshape(D, 2 * ATT_KVH * ATT_HD),
    "b_w_q": lambda a: a.reshape(D, D),
    "b_w_out": lambda a: a.reshape(D, D),
    "gu": lambda a: a.reshape(2, 4, FFN_B, D),
    "dn": lambda a: a.reshape(4, FFN_B, D),
    "pu": lambda a: a,
    "pg": lambda a: a.reshape(D, D),
}
_row_blocks = lambda a: a.reshape(N_DEV, -1, a.shape[-1])
OWNER_BLOCKS = {
    "a_w_in": lambda g: g,
    "a_w_out": _row_blocks,
    "kv_w": _row_blocks,
    "b_w_q": _row_blocks,
    "b_w_out": _row_blocks,
    "gu": lambda g: g,
    "dn": lambda g: _row_blocks(g.reshape(FFN_H, D)),
    "pu": lambda g: g.reshape(PLE_DIM, N_DEV, 128).transpose(1, 0, 2),
    "pg": _row_blocks,
}
ADAM_PARTS = (("kv_w", ("kv_w",)), ("b_w_q", ("b_w_q",)), ("b_w_out", ("b_w_out",)), ("ffn_w_gate_up", ("gu0", "gu1")),
              ("ffn_w_down", ("dn0", "dn1")), ("ple_w_up", ("pu0", "pu1")), ("ple_w_gate", ("pg0", "pg1")),
              ("a_w_out", ("a_w_out",)), ("a_w_in", ("a_w_in",)))


def kernel(x, p, a_w_in, a_lower_bound, a_norm_gain, a_w_out, kv_w, kv_b, b_w_q, b_b_q, b_sinks, b_w_out, b_b_out, ffn_w_gate_up, ffn_w_down, ple_w_up, ple_w_gate, ple_b_gate, ln_gain, ln_bias, loss_target, m_a_w_in, m_a_lower_bound, m_a_norm_gain, m_a_w_out, m_kv_w, m_kv_b, m_b_w_q, m_b_b_q, m_b_sinks, m_b_w_out, m_b_b_out, m_ffn_w_gate_up, m_ffn_w_down, m_ple_w_up, m_ple_w_gate, m_ple_b_gate, m_ln_gain, m_ln_bias, v_a_w_in, v_a_lower_bound, v_a_norm_gain, v_a_w_out, v_kv_w, v_kv_b, v_b_w_q, v_b_b_q, v_b_sinks, v_b_w_out, v_b_b_out, v_ffn_w_gate_up, v_ffn_w_down, v_ple_w_up, v_ple_w_gate, v_ple_b_gate, v_ln_gain, v_ln_bias):
    given = dict(locals())
    w = {n: given[n] for n in WEIGHTS}
    m = {n: given["m_" + n] for n in WEIGHTS}
    v = {n: given["v_" + n] for n in WEIGHTS}
    shards = {"a_w_in": a_w_in[0], "a_w_out": a_w_out[0], "kv_w": kv_w, "b_w_q": b_w_q[0], "b_w_out": b_w_out[0]}
    for l in range(2):
        shards.update({f"gu{l}": ffn_w_gate_up[l].T, f"dn{l}": ffn_w_down[l], f"pu{l}": ple_w_up[l], f"pg{l}": ple_w_gate[l]})
    sharded_small = [a_lower_bound, ln_gain.swapaxes(0, 1), ln_bias.swapaxes(0, 1)]
    gathered = {}
    for gi, g in enumerate(GATHER_GROUPS):
        lands = _sequencer_gather([shards[n].astype(CDT) for n in g] + (sharded_small if gi == 0 else []),
                                  f"gather{gi}", gi)
        for n, a in zip(g, lands):
            gathered[n] = KERNEL_LAYOUT[n.rstrip("01")](a)
        if gi == 0:
            alb = lands[len(g)].transpose(1, 0, 2).reshape(2, D)
            lng, lnb = [a.transpose(2, 1, 0, 3).reshape(2, 3, D) for a in lands[len(g) + 1:]]

    getw = gathered.__getitem__

    sm = {"a_lower_bound": alb, "ln_gain": lng, "ln_bias": lnb,
          "a_norm_gain": a_norm_gain, "kv_b": kv_b, "b_b_q": b_b_q[0], "b_sinks": b_sinks, "b_b_out": b_b_out,
          "ple_b_gate": ple_b_gate}

    scatters, small_parts = [], {}

    def emit(grads, small=None):
        names = list(grads)
        blocks = [OWNER_BLOCKS[n.rstrip("01")](grads[n]) for n in names]
        partials = [] if small is None else [small[n].reshape(ps) for n, _, ps in SMALL] + [small["loss"]]
        lands = _sequencer_exchange(blocks + partials, ["scatter"] * len(blocks) + ["gather"] * len(partials),
                                    f"scatter{len(scatters)}", len(GATHER_GROUPS) + len(scatters))
        scatters.append(dict(zip(names, lands)))
        small_parts.update(zip([n for n, _, _ in SMALL] + ["loss"], lands[len(blocks):]))
        return blocks + (list(scatters[-4].values()) if len(scatters) >= 4 else [])

    loss, grad_x, gs = _local_step(x[0], p[:, 0], loss_target[0], getw, sm, emit)

    out, parts, last = {}, {}, [grad_x]
    for landed in scatters:
        parts.update(landed)
        for n, keys in ADAM_PARTS:
            if n in out or not all(key in parts for key in keys):
                continue
            lrc = (1,) * (3 - w[n].ndim) + w[n].shape
            shard = (lambda a: a.reshape(lrc).swapaxes(1, 2)) if n == "ffn_w_gate_up" else (lambda a: a.reshape(lrc))
            res = _adam_big(shard(w[n]), [parts[key] for key in keys], shard(m[n]), shard(v[n]), "adam_" + n, after=last)
            out[n] = [(r.swapaxes(1, 2) if n == "ffn_w_gate_up" else r).reshape(w[n].shape) for r in res]
            last = [res[3]]
    small_out, loss = _adam_small(small_parts, w, m, v, small_parts["loss"], after=last)
    out.update(small_out)
    res = [loss, grad_x[None]]
    for i in range(4):
        res += [out[n][i] for n in WEIGHTS]
    return tuple(res)
```
